```python
import jax, jax.numpy as jnp
from jax import lax
import numpy as np

D_MODEL = 2048
BATCH = 8
SEQ = 2048
DEPTH = 1

HEAD_DIM = 128
N_HEADS = D_MODEL // HEAD_DIM
N_HEADS_B = N_HEADS // 4
N_HEADS_A = N_HEADS - N_HEADS_B
DILATED_PAIRS = ((128, 1), (512, 4), (2048, 16))
N_GROUPS_A = len(DILATED_PAIRS)
HEADS_PER_GROUP_A = N_HEADS_A // N_GROUPS_A
GRID_W = 64
WIN_R = 8
WIN_C = 16
QKV_W = N_HEADS * HEAD_DIM
D_IN = 3 * QKV_W + 2 * D_MODEL
D_A_OUT = HEADS_PER_GROUP_A * HEAD_DIM
D_B_OUT = N_HEADS_B * HEAD_DIM
D_FF = 4 * D_MODEL
ROPE_THETA = 10000.0
EPS = 1e-6
NEG_INF = -1e30

kernel_name = "hybrid_dilated_neighbourhood_gated_encoder"


def rms_norm(x, g):
    x32 = x.astype(jnp.float32)
    y = x32 * lax.rsqrt(jnp.mean(x32 * x32, axis=-1, keepdims=True) + EPS)
    return (y * g.astype(jnp.float32)).astype(x.dtype)


def rope(x, seq_len):
    pos = jnp.arange(seq_len, dtype=jnp.float32)
    inv = ROPE_THETA ** (-jnp.arange(0, HEAD_DIM, 2, dtype=jnp.float32) / HEAD_DIM)
    ang = pos[:, None] * inv[None, :]
    cos = jnp.cos(ang)[None, :, None, :]
    sin = jnp.sin(ang)[None, :, None, :]
    x32 = x.astype(jnp.float32)
    x1, x2 = jnp.split(x32, 2, axis=-1)
    return jnp.concatenate([x1 * cos - x2 * sin, x2 * cos + x1 * sin], axis=-1).astype(x.dtype)


def dilated_window_attention(q, k, v, window, dil):
    B, S, H, E = q.shape
    half = window // (2 * dil)
    blk = half
    M = S // dil
    nb = -(-M // blk)
    Mp = nb * blk

    def to_sub(t):
        return t.reshape(B, M, dil, H, E).transpose(0, 2, 3, 1, 4)

    qs = jnp.pad(to_sub(q), ((0, 0), (0, 0), (0, 0), (0, Mp - M), (0, 0)))
    qs = qs.reshape(B, dil, H, nb, blk, E)
    pad_kv = ((0, 0), (0, 0), (0, 0), (half, Mp - M + half), (0, 0))
    ks = jnp.pad(to_sub(k), pad_kv)
    vs = jnp.pad(to_sub(v), pad_kv)
    kb_len = blk + 2 * half
    idx = (jnp.arange(nb) * blk)[:, None] + jnp.arange(kb_len)[None, :]
    kb = ks[:, :, :, idx]
    vb = vs[:, :, :, idx]
    kpos = idx - half
    qpos = (jnp.arange(nb) * blk)[:, None] + jnp.arange(blk)[None, :]
    valid = (kpos >= 0) & (kpos < M)
    mask = (jnp.abs(kpos[:, None, :] - qpos[:, :, None]) <= half) & valid[:, None, :]
    s = jnp.einsum('bdhnqe,bdhnke->bdhnqk', qs, kb,
                   preferred_element_type=jnp.float32) * (E ** -0.5)
    s = jnp.where(mask, s, NEG_INF)
    lse = jax.nn.logsumexp(s, axis=-1)
    p = jnp.exp(s - lse[..., None]).astype(v.dtype)
    o = jnp.einsum('bdhnqk,bdhnke->bdhnqe', p, vb)
    o = o.reshape(B, dil, H, Mp, E)[:, :, :, :M].transpose(0, 3, 1, 2, 4).reshape(B, S, H, E)
    lse = lse.reshape(B, dil, H, Mp)[..., :M].transpose(0, 3, 1, 2).reshape(B, S, H)
    return o, lse


def neighbourhood_attention(q, k, v, rpb):
    B, S, H, E = q.shape
    rows = S // GRID_W
    kr = min(WIN_R, rows)
    kc = WIN_C

    def to_grid(t):
        return t.reshape(B, rows, GRID_W, H, E).transpose(0, 3, 1, 2, 4)

    qg, kg, vg = to_grid(q), to_grid(k), to_grid(v)
    r = jnp.arange(rows)
    row_idx = jnp.clip(r - kr // 2, 0, rows - kr)[:, None] + jnp.arange(kr)[None, :]
    kb = kg[:, :, row_idx]
    vb = vg[:, :, row_idx]
    c = jnp.arange(GRID_W)
    col_start = jnp.clip(c - kc // 2, 0, GRID_W - kc)
    col_mask = (c[None, :] >= col_start[:, None]) & (c[None, :] < col_start[:, None] + kc)
    dr = row_idx - r[:, None] + (WIN_R - 1)
    dc = jnp.clip(c[None, :] - c[:, None], -(kc - 1), kc - 1) + (WIN_C - 1)
    bias = rpb[:, dr[:, None, :, None], dc[None, :, None, :]]
    s = jnp.einsum('bhrqe,bhrjke->bhrqjk', qg, kb,
                   preferred_element_type=jnp.float32) * (E ** -0.5)
    s = s + bias.astype(jnp.float32)[None]
    s = jnp.where(col_mask[:, None, :], s, NEG_INF)
    p = jax.nn.softmax(s.reshape(B, H, rows, GRID_W, kr * GRID_W), axis=-1)
    p = p.reshape(s.shape).astype(v.dtype)
    o = jnp.einsum('bhrqjk,bhrjke->bhrqe', p, vb)
    return o.transpose(0, 2, 3, 1, 4).reshape(B, S, H, E)


def _fwd_setup_inputs(seed: int = 0) -> dict:
    key = jax.random.key(seed)
    ks = jax.random.split(key, 16)
    f32 = jnp.float32

    def nrm(k, shape, scale):
        return jax.random.normal(k, shape, f32) * scale

    return {
        "x": nrm(ks[0], (BATCH, SEQ, D_MODEL), 1.0),
        "norm_mix": 1.0 + nrm(ks[1], (DEPTH, D_MODEL), 0.05),
        "w_in": nrm(ks[2], (DEPTH, D_MODEL, D_IN), D_MODEL ** -0.5),
        "b_gate": nrm(ks[3], (DEPTH, 2 * D_MODEL), 0.1),
        "q_norm_a": 1.0 + nrm(ks[4], (DEPTH, HEAD_DIM), 0.05),
        "k_norm_a": 1.0 + nrm(ks[5], (DEPTH, HEAD_DIM), 0.05),
        "q_norm_b": 1.0 + nrm(ks[6], (DEPTH, HEAD_DIM), 0.05),
        "k_norm_b": 1.0 + nrm(ks[7], (DEPTH, HEAD_DIM), 0.05),
        "rpb_b": nrm(ks[8], (DEPTH, N_HEADS_B, 2 * WIN_R - 1, 2 * WIN_C - 1), 0.1),
        "w_proj_a": nrm(ks[9], (DEPTH, D_A_OUT, D_MODEL), D_A_OUT ** -0.5),
        "w_proj_b": nrm(ks[10], (DEPTH, D_B_OUT, D_MODEL), D_B_OUT ** -0.5),
        "w_out": nrm(ks[11], (DEPTH, D_MODEL, D_MODEL), D_MODEL ** -0.5),
        "norm_ffn": 1.0 + nrm(ks[12], (DEPTH, D_MODEL), 0.05),
        "w_up": nrm(ks[13], (DEPTH, D_MODEL, D_FF), D_MODEL ** -0.5),
        "w_down": nrm(ks[14], (DEPTH, D_FF, D_MODEL), D_FF ** -0.5),
    }


def _fwd_reference(x, norm_mix, w_in, b_gate, q_norm_a, k_norm_a, q_norm_b, k_norm_b, rpb_b,
              w_proj_a, w_proj_b, w_out, norm_ffn, w_up, w_down):
    B, S, _ = x.shape
    h = x
    for l in range(DEPTH):
        xn = rms_norm(h, norm_mix[l])
        proj = xn @ w_in[l]
        q, k, v, gate = jnp.split(proj, [QKV_W, 2 * QKV_W, 3 * QKV_W], axis=-1)
        q = q.reshape(B, S, N_HEADS, HEAD_DIM)
        k = k.reshape(B, S, N_HEADS, HEAD_DIM)
        v = v.reshape(B, S, N_HEADS, HEAD_DIM)

        qa = rope(rms_norm(q[:, :, :N_HEADS_A], q_norm_a[l]), S)
        ka = rope(rms_norm(k[:, :, :N_HEADS_A], k_norm_a[l]), S)
        va = v[:, :, :N_HEADS_A]
        outs, lses = [], []
        for g, (win, dil) in enumerate(DILATED_PAIRS):
            sl = slice(g * HEADS_PER_GROUP_A, (g + 1) * HEADS_PER_GROUP_A)
            o_g, lse_g = dilated_window_attention(qa[:, :, sl], ka[:, :, sl], va[:, :, sl], win, dil)
            outs.append(o_g)
            lses.append(lse_g)
        wts = jax.nn.softmax(jnp.stack(lses, axis=0), axis=0)
        oa = jnp.einsum('gbsh,gbshe->bshe', wts,
                        jnp.stack(outs, axis=0).astype(jnp.float32)).astype(x.dtype)

        qb = rms_norm(q[:, :, N_HEADS_A:], q_norm_b[l])
        kb = rms_norm(k[:, :, N_HEADS_A:], k_norm_b[l])
        ob = neighbourhood_attention(qb, kb, v[:, :, N_HEADS_A:], rpb_b[l])

        ya = oa.reshape(B, S, D_A_OUT) @ w_proj_a[l]
        yb = ob.reshape(B, S, D_B_OUT) @ w_proj_b[l]
        ga, gb = jnp.split(jax.nn.sigmoid((gate + b_gate[l]).astype(jnp.float32)), 2, axis=-1)
        mixed = (ga * ya + gb * yb).astype(x.dtype)
        h = h + mixed @ w_out[l]

        hn = rms_norm(h, norm_ffn[l])
        u = jax.nn.relu(hn @ w_up[l])
        h = h + (u * u) @ w_down[l]
    return h


import jax as _jax
import jax.numpy as _jnp

TWIN_FORMAT = 'train_step'
FWD_PARAMS = ['x', 'norm_mix', 'w_in', 'b_gate', 'q_norm_a', 'k_norm_a', 'q_norm_b', 'k_norm_b', 'rpb_b', 'w_proj_a', 'w_proj_b', 'w_out', 'norm_ffn', 'w_up', 'w_down']
TWIN_WEIGHTS = ['norm_mix', 'w_in', 'b_gate', 'q_norm_a', 'k_norm_a', 'q_norm_b', 'k_norm_b', 'rpb_b', 'w_proj_a', 'w_proj_b', 'w_out', 'norm_ffn', 'w_up', 'w_down']
TWIN_DIFF_INPUT = 'x'
TWIN_INPUTS = ['x', 'norm_mix', 'w_in', 'b_gate', 'q_norm_a', 'k_norm_a', 'q_norm_b', 'k_norm_b', 'rpb_b', 'w_proj_a', 'w_proj_b', 'w_out', 'norm_ffn', 'w_up', 'w_down', 'loss_target', 'm_norm_mix', 'm_w_in', 'm_b_gate', 'm_q_norm_a', 'm_k_norm_a', 'm_q_norm_b', 'm_k_norm_b', 'm_rpb_b', 'm_w_proj_a', 'm_w_proj_b', 'm_w_out', 'm_norm_ffn', 'm_w_up', 'm_w_down', 'v_norm_mix', 'v_w_in', 'v_b_gate', 'v_q_norm_a', 'v_k_norm_a', 'v_q_norm_b', 'v_k_norm_b', 'v_rpb_b', 'v_w_proj_a', 'v_w_proj_b', 'v_w_out', 'v_norm_ffn', 'v_w_up', 'v_w_down']
TWIN_OUTPUTS = ['loss', 'grad_x', 'grad_norm_mix', 'grad_w_in', 'grad_b_gate', 'grad_q_norm_a', 'grad_k_norm_a', 'grad_q_norm_b', 'grad_k_norm_b', 'grad_rpb_b', 'grad_w_proj_a', 'grad_w_proj_b', 'grad_w_out', 'grad_norm_ffn', 'grad_w_up', 'grad_w_down', 'delta_norm_mix', 'delta_w_in', 'delta_b_gate', 'delta_q_norm_a', 'delta_k_norm_a', 'delta_q_norm_b', 'delta_k_norm_b', 'delta_rpb_b', 'delta_w_proj_a', 'delta_w_proj_b', 'delta_w_out', 'delta_norm_ffn', 'delta_w_up', 'delta_w_down', 'new_m_norm_mix', 'new_m_w_in', 'new_m_b_gate', 'new_m_q_norm_a', 'new_m_k_norm_a', 'new_m_q_norm_b', 'new_m_k_norm_b', 'new_m_rpb_b', 'new_m_w_proj_a', 'new_m_w_proj_b', 'new_m_w_out', 'new_m_norm_ffn', 'new_m_w_up', 'new_m_w_down', 'new_v_norm_mix', 'new_v_w_in', 'new_v_b_gate', 'new_v_q_norm_a', 'new_v_k_norm_a', 'new_v_q_norm_b', 'new_v_k_norm_b', 'new_v_rpb_b', 'new_v_w_proj_a', 'new_v_w_proj_b', 'new_v_w_out', 'new_v_norm_ffn', 'new_v_w_up', 'new_v_w_down']
TWIN_LEAF_KINDS = {'loss': 'loss', 'grad_x': 'grad_x', 'grad_norm_mix': 'grad_w', 'grad_w_in': 'grad_w', 'grad_b_gate': 'grad_w', 'grad_q_norm_a': 'grad_w', 'grad_k_norm_a': 'grad_w', 'grad_q_norm_b': 'grad_w', 'grad_k_norm_b': 'grad_w', 'grad_rpb_b': 'grad_w', 'grad_w_proj_a': 'grad_w', 'grad_w_proj_b': 'grad_w', 'grad_w_out': 'grad_w', 'grad_norm_ffn': 'grad_w', 'grad_w_up': 'grad_w', 'grad_w_down': 'grad_w', 'delta_norm_mix': 'delta_w', 'delta_w_in': 'delta_w', 'delta_b_gate': 'delta_w', 'delta_q_norm_a': 'delta_w', 'delta_k_norm_a': 'delta_w', 'delta_q_norm_b': 'delta_w', 'delta_k_norm_b': 'delta_w', 'delta_rpb_b': 'delta_w', 'delta_w_proj_a': 'delta_w', 'delta_w_proj_b': 'delta_w', 'delta_w_out': 'delta_w', 'delta_norm_ffn': 'delta_w', 'delta_w_up': 'delta_w', 'delta_w_down': 'delta_w', 'new_m_norm_mix': 'new_m', 'new_m_w_in': 'new_m', 'new_m_b_gate': 'new_m', 'new_m_q_norm_a': 'new_m', 'new_m_k_norm_a': 'new_m', 'new_m_q_norm_b': 'new_m', 'new_m_k_norm_b': 'new_m', 'new_m_rpb_b': 'new_m', 'new_m_w_proj_a': 'new_m', 'new_m_w_proj_b': 'new_m', 'new_m_w_out': 'new_m', 'new_m_norm_ffn': 'new_m', 'new_m_w_up': 'new_m', 'new_m_w_down': 'new_m', 'new_v_norm_mix': 'new_v', 'new_v_w_in': 'new_v', 'new_v_b_gate': 'new_v', 'new_v_q_norm_a': 'new_v', 'new_v_k_norm_a': 'new_v', 'new_v_q_norm_b': 'new_v', 'new_v_k_norm_b': 'new_v', 'new_v_rpb_b': 'new_v', 'new_v_w_proj_a': 'new_v', 'new_v_w_proj_b': 'new_v', 'new_v_w_out': 'new_v', 'new_v_norm_ffn': 'new_v', 'new_v_w_up': 'new_v', 'new_v_w_down': 'new_v'}


def _forward(args):
    return _fwd_reference(*[args[k] for k in FWD_PARAMS])


def _output_shape():
    out = _jax.eval_shape(lambda: _forward(_fwd_setup_inputs(0)))
    return out.shape, out.dtype

N_MICROBATCH = 1
ADAM_LR = 0.001
ADAM_B1 = 0.9
ADAM_B2 = 0.999
ADAM_EPS = 1e-08
ADAM_WD = 0.01
ADAM_STEP = 10
PER_EXAMPLE_BATCH_AXIS = {'x': 0, 'loss_target': 0}
SHARED_INPUTS = []
_WEIGHT_DTYPES = {'norm_mix': _jnp.float32, 'w_in': _jnp.float32, 'b_gate': _jnp.float32, 'q_norm_a': _jnp.float32, 'k_norm_a': _jnp.float32, 'q_norm_b': _jnp.float32, 'k_norm_b': _jnp.float32, 'rpb_b': _jnp.float32, 'w_proj_a': _jnp.float32, 'w_proj_b': _jnp.float32, 'w_out': _jnp.float32, 'norm_ffn': _jnp.float32, 'w_up': _jnp.float32, 'w_down': _jnp.float32}
MOMENT_SCALE = {'norm_mix': 7.706288e-02, 'w_in': 3.114870e-02, 'b_gate': 1.596984e-02, 'q_norm_a': 2.364007e-01, 'k_norm_a': 2.371987e-01, 'q_norm_b': 3.911628e-01, 'k_norm_b': 3.934270e-01, 'rpb_b': 2.960772e-02, 'w_proj_a': 2.668050e-02, 'w_proj_b': 4.356080e-02, 'w_out': 5.176039e-02, 'norm_ffn': 2.418147e+01, 'w_up': 1.702760e-01, 'w_down': 1.995323e+00}


def _to_microbatches(a, axis):
    t = _jnp.moveaxis(a, axis, 0)
    t = t.reshape((N_MICROBATCH, t.shape[0] // N_MICROBATCH) + t.shape[1:])
    return _jnp.moveaxis(t, 1, axis + 1)


def setup_inputs(seed: int = 0) -> dict:
    inp = _fwd_setup_inputs(seed)
    key = _jax.random.fold_in(_jax.random.key(seed), 7919)
    shape, _ = _output_shape()
    out = dict(inp)
    out["loss_target"] = _jax.random.normal(_jax.random.fold_in(key, 0), shape, _jnp.float32)
    for i, name in enumerate(TWIN_WEIGHTS):
        w = inp[name].astype(_jnp.float32)
        if MOMENT_SCALE is None:
            s = _jnp.sqrt(_jnp.mean(_jnp.square(w)) + 1e-30)
        else:
            s = MOMENT_SCALE[name]
        km, kv = _jax.random.split(_jax.random.fold_in(key, i + 1))
        out[name] = w
        out["m_" + name] = s * _jax.random.normal(km, w.shape, _jnp.float32)
        out["v_" + name] = (s * s) * _jax.random.uniform(kv, w.shape, _jnp.float32, 0.5, 1.5)
    if N_MICROBATCH > 1:
        for name, axis in PER_EXAMPLE_BATCH_AXIS.items():
            out[name] = _to_microbatches(out[name], axis)
    return {'x': out['x'], 'norm_mix': out['norm_mix'], 'w_in': out['w_in'], 'b_gate': out['b_gate'], 'q_norm_a': out['q_norm_a'], 'k_norm_a': out['k_norm_a'], 'q_norm_b': out['q_norm_b'], 'k_norm_b': out['k_norm_b'], 'rpb_b': out['rpb_b'], 'w_proj_a': out['w_proj_a'], 'w_proj_b': out['w_proj_b'], 'w_out': out['w_out'], 'norm_ffn': out['norm_ffn'], 'w_up': out['w_up'], 'w_down': out['w_down'], 'loss_target': out['loss_target'], 'm_norm_mix': out['m_norm_mix'], 'm_w_in': out['m_w_in'], 'm_b_gate': out['m_b_gate'], 'm_q_norm_a': out['m_q_norm_a'], 'm_k_norm_a': out['m_k_norm_a'], 'm_q_norm_b': out['m_q_norm_b'], 'm_k_norm_b': out['m_k_norm_b'], 'm_rpb_b': out['m_rpb_b'], 'm_w_proj_a': out['m_w_proj_a'], 'm_w_proj_b': out['m_w_proj_b'], 'm_w_out': out['m_w_out'], 'm_norm_ffn': out['m_norm_ffn'], 'm_w_up': out['m_w_up'], 'm_w_down': out['m_w_down'], 'v_norm_mix': out['v_norm_mix'], 'v_w_in': out['v_w_in'], 'v_b_gate': out['v_b_gate'], 'v_q_norm_a': out['v_q_norm_a'], 'v_k_norm_a': out['v_k_norm_a'], 'v_q_norm_b': out['v_q_norm_b'], 'v_k_norm_b': out['v_k_norm_b'], 'v_rpb_b': out['v_rpb_b'], 'v_w_proj_a': out['v_w_proj_a'], 'v_w_proj_b': out['v_w_proj_b'], 'v_w_out': out['v_w_out'], 'v_norm_ffn': out['v_norm_ffn'], 'v_w_up': out['v_w_up'], 'v_w_down': out['v_w_down']}


def _loss(weights, diff, rest, loss_target):
    with _jax.named_scope("forward"):
        args = {**rest, TWIN_DIFF_INPUT: diff, **{k: w.astype(_WEIGHT_DTYPES[k]) for k, w in weights.items()}}
        y = _forward(args)
    with _jax.named_scope("loss_head"):
        err = _jnp.square(y.astype(_jnp.float32) - loss_target)
        return 0.5 * _jnp.sum(_jnp.mean(err, axis=-1)) if err.ndim else 0.5 * err


def _adamw(w, g, m, v):
    m = ADAM_B1 * m + (1.0 - ADAM_B1) * g
    v = ADAM_B2 * v + (1.0 - ADAM_B2) * _jnp.square(g)
    m_hat = m / (1.0 - ADAM_B1 ** ADAM_STEP)
    v_hat = v / (1.0 - ADAM_B2 ** ADAM_STEP)
    delta = -ADAM_LR * (m_hat / (_jnp.sqrt(v_hat) + ADAM_EPS) + ADAM_WD * w)
    return delta, m, v


def reference(x, norm_mix, w_in, b_gate, q_norm_a, k_norm_a, q_norm_b, k_norm_b, rpb_b, w_proj_a, w_proj_b, w_out, norm_ffn, w_up, w_down, loss_target, m_norm_mix, m_w_in, m_b_gate, m_q_norm_a, m_k_norm_a, m_q_norm_b, m_k_norm_b, m_rpb_b, m_w_proj_a, m_w_proj_b, m_w_out, m_norm_ffn, m_w_up, m_w_down, v_norm_mix, v_w_in, v_b_gate, v_q_norm_a, v_k_norm_a, v_q_norm_b, v_k_norm_b, v_rpb_b, v_w_proj_a, v_w_proj_b, v_w_out, v_norm_ffn, v_w_up, v_w_down):
    given = dict(x=x, norm_mix=norm_mix, w_in=w_in, b_gate=b_gate, q_norm_a=q_norm_a, k_norm_a=k_norm_a, q_norm_b=q_norm_b, k_norm_b=k_norm_b, rpb_b=rpb_b, w_proj_a=w_proj_a, w_proj_b=w_proj_b, w_out=w_out, norm_ffn=norm_ffn, w_up=w_up, w_down=w_down, loss_target=loss_target, m_norm_mix=m_norm_mix, m_w_in=m_w_in, m_b_gate=m_b_gate, m_q_norm_a=m_q_norm_a, m_k_norm_a=m_k_norm_a, m_q_norm_b=m_q_norm_b, m_k_norm_b=m_k_norm_b, m_rpb_b=m_rpb_b, m_w_proj_a=m_w_proj_a, m_w_proj_b=m_w_proj_b, m_w_out=m_w_out, m_norm_ffn=m_norm_ffn, m_w_up=m_w_up, m_w_down=m_w_down, v_norm_mix=v_norm_mix, v_w_in=v_w_in, v_b_gate=v_b_gate, v_q_norm_a=v_q_norm_a, v_k_norm_a=v_k_norm_a, v_q_norm_b=v_q_norm_b, v_k_norm_b=v_k_norm_b, v_rpb_b=v_rpb_b, v_w_proj_a=v_w_proj_a, v_w_proj_b=v_w_proj_b, v_w_out=v_w_out, v_norm_ffn=v_norm_ffn, v_w_up=v_w_up, v_w_down=v_w_down)
    weights = {n: given[n] for n in TWIN_WEIGHTS}
    shared = {n: given[n] for n in SHARED_INPUTS}
    per_example = {n: given[n] for n in ['x']}
    grad_fn = _jax.value_and_grad(_loss, argnums=(0, 1))

    def one_microbatch(ex, loss_target):
        ex = dict(ex)
        diff = ex.pop(TWIN_DIFF_INPUT)
        return grad_fn(weights, diff, {**shared, **ex}, loss_target)

    if N_MICROBATCH == 1:
        loss, (grad_w, grad_x) = one_microbatch(per_example, given["loss_target"])
    else:
        def body(carry, xs):
            loss_sum, grad_sum = carry
            l_k, (gw_k, gx_k) = one_microbatch(xs[0], xs[1])
            with _jax.named_scope("update"):
                return (loss_sum + l_k, _jax.tree.map(_jnp.add, grad_sum, gw_k)), gx_k

        init = (_jnp.zeros((), _jnp.float32), _jax.tree.map(_jnp.zeros_like, weights))
        (loss, grad_w), grad_x = _jax.lax.scan(body, init, (per_example, given["loss_target"]))
    with _jax.named_scope("update"):
        delta_w, new_m, new_v = {}, {}, {}
        for n in TWIN_WEIGHTS:
            delta_w[n], new_m[n], new_v[n] = _adamw(weights[n], grad_w[n], given["m_" + n], given["v_" + n])
    return (loss, grad_x, *[grad_w[n] for n in TWIN_WEIGHTS], *[delta_w[n] for n in TWIN_WEIGHTS],
            *[new_m[n] for n in TWIN_WEIGHTS], *[new_v[n] for n in TWIN_WEIGHTS])
```

```python
import functools

import jax
import jax.numpy as jnp
import numpy as np
from jax import lax
from jax.experimental import pallas as pl
from jax.experimental.pallas import tpu as pltpu

F32 = jnp.float32
BF16 = jnp.bfloat16

N_DEV = 8
S = 2048
D = 2048
HD = 128
NH = 16
NH_A = 12
QKV = NH * HD
D_IN = 3 * QKV + 2 * D
D_BR = 512
D_FF = 4 * D
GRID_W = 64
ROWS = S // GRID_W
WIN_R = 8
WIN_C = 16
EPS = 1e-6
NEG = -1e30
SCALE = HD ** -0.5
ROPE_THETA = 10000.0
GROUPS_A = ((64, 1, 512), (256, 4, 768), (1024, 16, 2048))
QB = 256

LR, B1, B2, AEPS, WD, STEP = 0.001, 0.9, 0.999, 1e-08, 0.01, 10
BC1 = 1.0 - B1 ** STEP
BC2 = 1.0 - B2 ** STEP

VMEM_LIMIT = 56 * 1024 * 1024
MESH = pl.DeviceIdType.MESH

NN = (((1,), (0,)), ((), ()))
NT = (((1,), (1,)), ((), ()))
TN = (((0,), (0,)), ((), ()))


def _params(sem):
    return pltpu.CompilerParams(dimension_semantics=sem, vmem_limit_bytes=VMEM_LIMIT)


def _matmul(a, b, *, dims, grid, a_spec, b_spec, nk, epi, out_shape, out_specs, name,
            extra=(), extra_specs=(), acc_shape=None):
    n_extra = len(extra)

    def body(a_ref, b_ref, *rest):
        ex = rest[:n_extra]
        if nk == 1:
            outs = rest[n_extra:]
            epi(lax.dot_general(a_ref[...], b_ref[...], dims, preferred_element_type=F32), ex, outs)
            return
        outs, acc = rest[n_extra:-1], rest[-1]
        k = pl.program_id(2)
        part = lax.dot_general(a_ref[...], b_ref[...], dims, preferred_element_type=F32)

        @pl.when(k == 0)
        def _():
            acc[...] = part

        @pl.when(k > 0)
        def _():
            acc[...] += part

        @pl.when(k == nk - 1)
        def _():
            epi(acc[...], ex, outs)

    scratch = [] if nk == 1 else [pltpu.VMEM(acc_shape, F32)]
    return pl.pallas_call(
        body, name=name, grid=grid,
        in_specs=[a_spec, b_spec, *extra_specs],
        out_specs=out_specs, out_shape=out_shape, scratch_shapes=scratch,
        compiler_params=_params(("parallel", "parallel", "arbitrary")),
    )(a, b, *extra)


def _epi_store(acc, ex, outs):
    outs[0][...] = acc.astype(outs[0].dtype)


def _epi_residual(acc, ex, outs):
    outs[0][...] = acc + ex[0][...]


def _mm_nn(a, b3, *, tm, tn, tk, name, out_dtype=F32, epi=_epi_store, extra=(), n_out=1,
           out_dtypes=None):
    m, kdim = a.shape
    g, _, ng = b3.shape
    n = g * ng
    npg = ng // tn
    nk = kdim // tk
    grid = (n // tn, m // tm, nk)
    tile = pl.BlockSpec((tm, tn), lambda j, i, k: (i, j))
    dts = out_dtypes or (out_dtype,) * n_out
    shapes = tuple(jax.ShapeDtypeStruct((m, n), dt) for dt in dts)
    return _matmul(
        a, b3, dims=NN, grid=grid, nk=nk, epi=epi, name=name,
        a_spec=pl.BlockSpec((tm, tk), lambda j, i, k: (i, k)),
        b_spec=pl.BlockSpec((None, tk, tn), lambda j, i, k: (j // npg, k, j % npg)),
        extra=extra, extra_specs=[tile] * len(extra),
        out_shape=shapes if len(dts) > 1 else shapes[0],
        out_specs=[tile] * len(dts) if len(dts) > 1 else tile,
        acc_shape=(tm, tn))


def _mm_nt(a, b3, *, tm, tn, tk, name, out_dtype=F32, epi=_epi_store, extra=()):
    m, kdim = a.shape
    g, n, kg = b3.shape
    kpg = kg // tk
    nk = kdim // tk
    grid = (n // tn, m // tm, nk)
    tile = pl.BlockSpec((tm, tn), lambda j, i, k: (i, j))
    return _matmul(
        a, b3, dims=NT, grid=grid, nk=nk, epi=epi, name=name,
        a_spec=pl.BlockSpec((tm, tk), lambda j, i, k: (i, k)),
        b_spec=pl.BlockSpec((None, tn, tk), lambda j, i, k: (k // kpg, j, k % kpg)),
        extra=extra, extra_specs=[tile] * len(extra),
        out_shape=jax.ShapeDtypeStruct((m, n), out_dtype), out_specs=tile,
        acc_shape=(tm, tn))


def _mm_tn(a, b, *, tm, tn, name, groups=1, out_dtype=BF16):
    t, m = a.shape
    _, n = b.shape
    ng = n // groups
    npg = ng // tn
    grid = (n // tn, m // tm, 1)
    return _matmul(
        a, b, dims=TN, grid=grid, nk=1, epi=_epi_store, name=name,
        a_spec=pl.BlockSpec((t, tm), lambda j, i, k: (0, i)),
        b_spec=pl.BlockSpec((t, tn), lambda j, i, k: (0, j)),
        out_shape=jax.ShapeDtypeStruct((groups, m, ng), out_dtype),
        out_specs=pl.BlockSpec((None, tm, tn), lambda j, i, k: (j // npg, i, j % npg)))


def _rms_fwd(x, g, *, name, tr=256):
    def body(x_ref, g_ref, y_ref, r_ref):
        xv = x_ref[...]
        r = lax.rsqrt(jnp.mean(xv * xv, axis=-1, keepdims=True) + EPS)
        y_ref[...] = (xv * r * g_ref[...]).astype(BF16)
        r_ref[...] = r

    row = pl.BlockSpec((tr, D), lambda i: (i, 0))
    return pl.pallas_call(
        body, name=name, grid=(S // tr,),
        in_specs=[row, pl.BlockSpec((1, D), lambda i: (0, 0))],
        out_specs=[row, pl.BlockSpec((tr, 1), lambda i: (i, 0))],
        out_shape=[jax.ShapeDtypeStruct((S, D), BF16), jax.ShapeDtypeStruct((S, 1), F32)],
        compiler_params=_params(("parallel",)),
    )(x, g)


def _rms_bwd(dy, x, rstd, g, resid, *, name, tr=256):
    def body(dy_ref, x_ref, r_ref, g_ref, res_ref, dx_ref, dxb_ref, dg_ref):
        r = r_ref[...]
        xh = x_ref[...] * r
        dyv = dy_ref[...]
        t = dyv * g_ref[...]
        dx = r * (t - xh * jnp.mean(t * xh, axis=-1, keepdims=True)) + res_ref[...]
        dx_ref[...] = dx
        dxb_ref[...] = dx.astype(BF16)
        part = jnp.sum(dyv * xh, axis=0, keepdims=True)

        @pl.when(pl.program_id(0) == 0)
        def _():
            dg_ref[...] = part

        @pl.when(pl.program_id(0) > 0)
        def _():
            dg_ref[...] += part

    row = pl.BlockSpec((tr, D), lambda i: (i, 0))
    vec = pl.BlockSpec((1, D), lambda i: (0, 0))
    return pl.pallas_call(
        body, name=name, grid=(S // tr,),
        in_specs=[row, row, pl.BlockSpec((tr, 1), lambda i: (i, 0)), vec, row],
        out_specs=[row, row, vec],
        out_shape=[jax.ShapeDtypeStruct((S, D), F32), jax.ShapeDtypeStruct((S, D), BF16),
                   jax.ShapeDtypeStruct((1, D), F32)],
        compiler_params=_params(("arbitrary",)),
    )(dy, x, rstd, g, resid)


def _loss_head(h2, target, *, tr=256):
    def body(h_ref, t_ref, dy_ref, dyb_ref, loss_ref):
        e = h_ref[...] - t_ref[...]
        dy = e * (1.0 / D)
        dy_ref[...] = dy
        dyb_ref[...] = dy.astype(BF16)
        part = (0.5 / D) * jnp.sum(jnp.sum(e * e, axis=-1, keepdims=True), axis=0, keepdims=True)

        @pl.when(pl.program_id(0) == 0)
        def _():
            loss_ref[...] = part

        @pl.when(pl.program_id(0) > 0)
        def _():
            loss_ref[...] += part

    row = pl.BlockSpec((tr, D), lambda i: (i, 0))
    return pl.pallas_call(
        body, name="loss_head", grid=(S // tr,),
        in_specs=[row, row],
        out_specs=[row, row, pl.BlockSpec((1, 1), lambda i: (0, 0))],
        out_shape=[jax.ShapeDtypeStruct((S, D), F32), jax.ShapeDtypeStruct((S, D), BF16),
                   jax.ShapeDtypeStruct((1, 1), F32)],
        compiler_params=_params(("arbitrary",)),
    )(h2, target)


def _rope_tables():
    pos = np.arange(S, dtype=np.float32)
    inv = (ROPE_THETA ** (-np.arange(0, HD, 2, dtype=np.float32) / HD)).astype(np.float32)
    ang = pos[:, None] * inv[None, :]
    cos, sin = np.cos(ang), np.sin(ang)
    return (jnp.asarray(np.concatenate([cos, cos], axis=-1), F32),
            jnp.asarray(np.concatenate([-sin, sin], axis=-1), F32))


def _swap_halves(t):
    return pltpu.roll(t, HD // 2, axis=1)


def _qk_prep(proj, gains, cos2, sin2, *, tr=256):
    def body(q_ref, k_ref, v_ref, g_ref, c_ref, s_ref, qn_ref, kn_ref, vb_ref):
        cos, sin = c_ref[...], s_ref[...]
        for src, dst, row_a, row_b in ((q_ref, qn_ref, 0, 2), (k_ref, kn_ref, 1, 3)):
            for h in range(NH):
                cols = slice(h * HD, (h + 1) * HD)
                t = src[:, cols]
                r = lax.rsqrt(jnp.mean(t * t, axis=-1, keepdims=True) + EPS)
                if h < NH_A:
                    y = t * r * g_ref[row_a:row_a + 1, :]
                    y = y * cos + _swap_halves(y) * sin
                else:
                    y = t * r * g_ref[row_b:row_b + 1, :]
                dst[:, cols] = y.astype(BF16)
        vb_ref[...] = v_ref[...].astype(BF16)

    def blk(c):
        return pl.BlockSpec((tr, QKV), lambda i: (i, c))
    tab = pl.BlockSpec((tr, HD), lambda i: (i, 0))
    out = pl.BlockSpec((tr, QKV), lambda i: (i, 0))
    return pl.pallas_call(
        body, name="qk_prep", grid=(S // tr,),
        in_specs=[blk(0), blk(1), blk(2), pl.BlockSpec((8, HD), lambda i: (0, 0)), tab, tab],
        out_specs=[out, out, out],
        out_shape=[jax.ShapeDtypeStruct((S, QKV), BF16)] * 3,
        compiler_params=_params(("parallel",)),
    )(proj, proj, proj, gains, cos2, sin2)


def _qk_prep_bwd(dproj, proj, gains, cos2, sin2, dq_parts, dk_parts, dv_parts, *, tr=256):
    def body(dp_in, q_ref, k_ref, g_ref, c_ref, s_ref, *rest):
        dqs, dks, dvs = rest[0:4], rest[4:8], rest[8:12]
        dp_out, dg_ref = rest[12:14]
        del dp_in
        cos, sin = c_ref[...], s_ref[...]
        dg_rows = []
        for src, grads, base, row_a, row_b in ((q_ref, dqs, 0, 0, 2), (k_ref, dks, QKV, 1, 3)):
            dg_a = jnp.zeros((1, HD), F32)
            dg_b = jnp.zeros((1, HD), F32)
            for h in range(NH):
                cols = slice(h * HD, (h + 1) * HD)
                t = src[:, cols]
                dy = grads[h // 4][:, (h % 4) * HD:(h % 4 + 1) * HD]
                r = lax.rsqrt(jnp.mean(t * t, axis=-1, keepdims=True) + EPS)
                xh = t * r
                if h < NH_A:
                    dy = dy * cos - _swap_halves(dy) * sin
                    gain = g_ref[row_a:row_a + 1, :]
                    dg_a = dg_a + jnp.sum(dy * xh, axis=0, keepdims=True)
                else:
                    gain = g_ref[row_b:row_b + 1, :]
                    dg_b = dg_b + jnp.sum(dy * xh, axis=0, keepdims=True)
                u = dy * gain
                dx = r * (u - xh * jnp.mean(u * xh, axis=-1, keepdims=True))
                dp_out[:, base + h * HD:base + (h + 1) * HD] = dx.astype(BF16)
            dg_rows += [(row_a, dg_a), (row_b, dg_b)]
        for g4 in range(4):
            dp_out[:, 2 * QKV + g4 * D_BR:2 * QKV + (g4 + 1) * D_BR] = dvs[g4][...].astype(BF16)

        first = pl.program_id(0) == 0

        @pl.when(first)
        def _():
            dg_ref[...] = jnp.zeros((8, HD), F32)

        for row, val in dg_rows:
            dg_ref[row:row + 1, :] += val

    def blk(c):
        return pl.BlockSpec((tr, QKV), lambda i: (i, c))
    tab = pl.BlockSpec((tr, HD), lambda i: (i, 0))
    part = pl.BlockSpec((tr, D_BR), lambda i: (i, 0))
    gain_spec = pl.BlockSpec((8, HD), lambda i: (0, 0))
    return pl.pallas_call(
        body, name="qk_prep_bwd", grid=(S // tr,),
        in_specs=[pl.BlockSpec(memory_space=pl.ANY), blk(0), blk(1), gain_spec, tab, tab] + [part] * 12,
        out_specs=[pl.BlockSpec((tr, 3 * QKV), lambda i: (i, 0)), gain_spec],
        out_shape=[jax.ShapeDtypeStruct((S, D_IN), BF16), jax.ShapeDtypeStruct((8, HD), F32)],
        input_output_aliases={0: 0},
        compiler_params=_params(("arbitrary",)),
    )(dproj, proj, proj, gains, cos2, sin2, *dq_parts, *dk_parts, *dv_parts)


def _gate_fwd(proj, b_gate, ya, yb, *, tr=256):
    def body(la_ref, lb_ref, ba_ref, bb_ref, ya_ref, yb_ref, o_ref):
        ga = jax.nn.sigmoid(la_ref[...] + ba_ref[...])
        gb = jax.nn.sigmoid(lb_ref[...] + bb_ref[...])
        o_ref[...] = (ga * ya_ref[...] + gb * yb_ref[...]).astype(BF16)

    row = pl.BlockSpec((tr, D), lambda i: (i, 0))
    return pl.pallas_call(
        body, name="gate_fwd", grid=(S // tr,),
        in_specs=[pl.BlockSpec((tr, D), lambda i: (i, 3)), pl.BlockSpec((tr, D), lambda i: (i, 4)),
                  pl.BlockSpec((1, D), lambda i: (0, 0)), pl.BlockSpec((1, D), lambda i: (0, 1)),
                  row, row],
        out_specs=row, out_shape=jax.ShapeDtypeStruct((S, D), BF16),
        compiler_params=_params(("parallel",)),
    )(proj, proj, b_gate, b_gate, ya, yb)


def _gate_bwd(branch, dmixed, proj, b_gate, y, dproj, *, tr=256):
    aliased = dproj is not None

    def body(dm_ref, l_ref, b_ref, y_ref, *rest):
        dy_ref, dp_ref, db_ref = rest[-3:]
        g = jax.nn.sigmoid(l_ref[...] + b_ref[...])
        dm = dm_ref[...]
        dy_ref[...] = (dm * g).astype(BF16)
        dl = dm * y_ref[...] * g * (1.0 - g)
        dp_ref[...] = dl.astype(BF16)
        part = jnp.sum(dl, axis=0, keepdims=True)

        @pl.when(pl.program_id(0) == 0)
        def _():
            db_ref[...] = part

        @pl.when(pl.program_id(0) > 0)
        def _():
            db_ref[...] += part

    row = pl.BlockSpec((tr, D), lambda i: (i, 0))
    col = pl.BlockSpec((tr, D), lambda i: (i, 3 + branch))
    vec = pl.BlockSpec((1, D), lambda i: (0, 0))
    return pl.pallas_call(
        body, name=f"gate_bwd_{branch}", grid=(S // tr,),
        in_specs=[row, col, pl.BlockSpec((1, D), lambda i: (0, branch)), row]
        + ([pl.BlockSpec(memory_space=pl.ANY)] if aliased else []),
        out_specs=[row, col, vec],
        out_shape=[jax.ShapeDtypeStruct((S, D), BF16), jax.ShapeDtypeStruct((S, D_IN), BF16),
                   jax.ShapeDtypeStruct((1, D), F32)],
        input_output_aliases={4: 1} if aliased else {},
        compiler_params=_params(("arbitrary",)),
    )(dmixed, proj, b_gate, y, *([dproj] if aliased else []))


def _window_start(t0, wk):
    if wk == S:
        return 0
    return pl.multiple_of(jnp.clip(t0 - (wk - QB) // 2, 0, S - wk), 128)


def _scores_a(q, kw, t0, start, hs, dil, wk):
    s = lax.dot_general(q, kw, NT, preferred_element_type=F32) * SCALE
    qpos = t0 + lax.broadcasted_iota(jnp.int32, (QB, 1), 0)
    kpos = start + lax.broadcasted_iota(jnp.int32, (1, wk), 1)
    diff = kpos - qpos
    keep = (jnp.abs(diff) <= hs) & ((diff & (dil - 1)) == 0)
    return jnp.where(keep, s, NEG)


def _attn_a_fwd(qn, kn, vb, gi):
    hs, dil, wk = GROUPS_A[gi]

    def body(q_ref, k_ref, v_ref, o_ref, lse_ref):
        t0 = pl.program_id(1) * QB
        start = _window_start(t0, wk)
        s = _scores_a(q_ref[...], k_ref[pl.ds(start, wk), :], t0, start, hs, dil, wk)
        m = jnp.max(s, axis=-1, keepdims=True)
        p = jnp.exp(s - m)
        l = jnp.sum(p, axis=-1, keepdims=True)
        o = lax.dot_general(p.astype(BF16), v_ref[pl.ds(start, wk), :], NN, preferred_element_type=F32)
        o_ref[...] = o / l
        lse_ref[...] = m + jnp.log(l)

    full = pl.BlockSpec((S, HD), lambda h, i: (0, 4 * gi + h))
    return pl.pallas_call(
        body, name=f"attn_a_fwd_{gi}", grid=(4, S // QB),
        in_specs=[pl.BlockSpec((QB, HD), lambda h, i: (i, 4 * gi + h)), full, full],
        out_specs=[pl.BlockSpec((QB, HD), lambda h, i: (i, h)),
                   pl.BlockSpec((None, QB, 1), lambda h, i: (h, i, 0))],
        out_shape=[jax.ShapeDtypeStruct((S, D_BR), F32), jax.ShapeDtypeStruct((4, S, 1), F32)],
        compiler_params=_params(("parallel", "parallel")),
    )(qn, kn, vb)


def _combine_a(os, lses, *, tr=256):
    def body(o0, o1, o2, l0, l1, l2, oa_ref, lse_ref):
        for h in range(4):
            cols = slice(h * HD, (h + 1) * HD)
            a, b, c = l0[h], l1[h], l2[h]
            m = jnp.maximum(jnp.maximum(a, b), c)
            wa, wb, wc = jnp.exp(a - m), jnp.exp(b - m), jnp.exp(c - m)
            tot = wa + wb + wc
            oa_ref[:, cols] = ((wa * o0[:, cols] + wb * o1[:, cols] + wc * o2[:, cols]) / tot).astype(BF16)
            lse_ref[h] = m + jnp.log(tot)

    row = pl.BlockSpec((tr, D_BR), lambda i: (i, 0))
    stat = pl.BlockSpec((4, tr, 1), lambda i: (0, i, 0))
    return pl.pallas_call(
        body, name="combine_a", grid=(S // tr,),
        in_specs=[row] * 3 + [stat] * 3, out_specs=[row, stat],
        out_shape=[jax.ShapeDtypeStruct((S, D_BR), BF16), jax.ShapeDtypeStruct((4, S, 1), F32)],
        compiler_params=_params(("parallel",)),
    )(*os, *lses)


def _attn_a_bwd(qn, kn, vb, oa, doa, lse, gi):
    hs, dil, wk = GROUPS_A[gi]

    def body(q_ref, k_ref, v_ref, o_ref, do_ref, lse_ref, dq_ref, dk_ref, dv_ref):
        @pl.when(pl.program_id(1) == 0)
        def _():
            dk_ref[...] = jnp.zeros((S, HD), F32)
            dv_ref[...] = jnp.zeros((S, HD), F32)

        t0 = pl.program_id(1) * QB
        start = _window_start(t0, wk)
        q = q_ref[...]
        kw = k_ref[pl.ds(start, wk), :]
        vw = v_ref[pl.ds(start, wk), :]
        p = jnp.exp(_scores_a(q, kw, t0, start, hs, dil, wk) - lse_ref[...])
        do = do_ref[...]
        dob = do.astype(BF16)
        dsum = jnp.sum(do * o_ref[...].astype(F32), axis=-1, keepdims=True)
        dp = lax.dot_general(dob, vw, NT, preferred_element_type=F32)
        ds = (p * (dp - dsum) * SCALE).astype(BF16)
        dq_ref[...] = lax.dot_general(ds, kw, NN, preferred_element_type=F32)
        dk_ref[pl.ds(start, wk), :] += lax.dot_general(ds, q, TN, preferred_element_type=F32)
        dv_ref[pl.ds(start, wk), :] += lax.dot_general(p.astype(BF16), dob, TN, preferred_element_type=F32)

    full = pl.BlockSpec((S, HD), lambda h, i: (0, 4 * gi + h))
    blk = pl.BlockSpec((QB, HD), lambda h, i: (i, h))
    acc = pl.BlockSpec((S, HD), lambda h, i: (0, h))
    shape = jax.ShapeDtypeStruct((S, D_BR), F32)
    return pl.pallas_call(
        body, name=f"attn_a_bwd_{gi}", grid=(4, S // QB),
        in_specs=[pl.BlockSpec((QB, HD), lambda h, i: (i, 4 * gi + h)), full, full, blk, blk,
                  pl.BlockSpec((None, QB, 1), lambda h, i: (h, i, 0))],
        out_specs=[blk, acc, acc], out_shape=[shape, shape, shape],
        compiler_params=_params(("parallel", "arbitrary")),
    )(qn, kn, vb, oa, doa, lse)


KEYS_B = WIN_R * GRID_W
N_OFF = WIN_R


def _bias_constants():
    q = np.arange(GRID_W)[:, None]
    kc = np.arange(GRID_W)[None, :]
    dc = np.clip(kc - q, -(WIN_C - 1), WIN_C - 1) + (WIN_C - 1)
    expand = np.zeros((HD, GRID_W * GRID_W), np.float32)
    expand[dc.reshape(-1), np.arange(GRID_W * GRID_W)] = 1.0
    cs = np.clip(q - WIN_C // 2, 0, GRID_W - WIN_C)
    keep = ((kc >= cs) & (kc < cs + WIN_C)).reshape(1, -1).astype(np.float32)
    sel = np.zeros((64, 4 * N_OFF * WIN_R), np.float32)
    for h in range(4):
        for off in range(N_OFF):
            for j in range(WIN_R):
                sel[h * (2 * WIN_R - 1) + off + j, (h * N_OFF + off) * WIN_R + j] = 1.0
    return jnp.asarray(expand), jnp.asarray(keep), jnp.asarray(sel)


def _bias_expand(rpb_pad, expand, keep, sel):
    def body(r_ref, e_ref, k_ref, s_ref, o_ref):
        t = lax.dot_general(r_ref[...], e_ref[...], NN, precision=lax.Precision.HIGHEST,
                            preferred_element_type=F32)
        rows = lax.dot_general(s_ref[...], t, TN, precision=lax.Precision.HIGHEST,
                               preferred_element_type=F32)
        o_ref[...] = jnp.where(k_ref[...] > 0.5, rows, NEG)

    return pl.pallas_call(
        body, name="bias_expand",
        out_shape=jax.ShapeDtypeStruct((4 * N_OFF * WIN_R, GRID_W * GRID_W), F32),
        compiler_params=pltpu.CompilerParams(vmem_limit_bytes=VMEM_LIMIT),
    )(rpb_pad, expand, keep, sel)


def _bias_reduce(dbias_rows, expand, sel):
    def body(x_ref, e_ref, s_ref, o_ref):
        z = lax.dot_general(x_ref[...], e_ref[...], NT, precision=lax.Precision.HIGHEST,
                            preferred_element_type=F32)
        o_ref[...] = lax.dot_general(s_ref[...], z, NN, precision=lax.Precision.HIGHEST,
                                     preferred_element_type=F32)

    return pl.pallas_call(
        body, name="bias_reduce", out_shape=jax.ShapeDtypeStruct((64, HD), F32),
        compiler_params=pltpu.CompilerParams(vmem_limit_bytes=VMEM_LIMIT),
    )(dbias_rows, expand, sel)


def _rows_to_tab(rows):
    t = rows.reshape(4, N_OFF, WIN_R, GRID_W, GRID_W)
    return t.transpose(0, 1, 3, 2, 4).reshape(4, N_OFF, GRID_W, KEYS_B)


def _tab_to_rows(tab):
    t = tab.reshape(4, N_OFF, GRID_W, WIN_R, GRID_W)
    return t.transpose(0, 1, 3, 2, 4).reshape(4 * N_OFF * WIN_R, GRID_W * GRID_W)


def _row_window(r):
    r0 = jnp.clip(r - WIN_R // 2, 0, ROWS - WIN_R)
    off = r0 + (WIN_R - 1) - r
    return pl.multiple_of(r * GRID_W, GRID_W), pl.multiple_of(r0 * GRID_W, GRID_W), off


def _attn_b_fwd(qn, kn, vb, bias_tab):
    def body(q_ref, k_ref, v_ref, b_ref, o_ref, lse_ref):
        def row(r, carry):
            qs, ks, off = _row_window(r)
            q = q_ref[pl.ds(qs, GRID_W), :]
            s = lax.dot_general(q, k_ref[pl.ds(ks, KEYS_B), :], NT, preferred_element_type=F32) * SCALE
            s = s + b_ref[off]
            m = jnp.max(s, axis=-1, keepdims=True)
            p = jnp.exp(s - m)
            l = jnp.sum(p, axis=-1, keepdims=True)
            o = lax.dot_general(p.astype(BF16), v_ref[pl.ds(ks, KEYS_B), :], NN, preferred_element_type=F32)
            o_ref[pl.ds(qs, GRID_W), :] = (o / l).astype(BF16)
            lse_ref[pl.ds(qs, GRID_W), :] = m + jnp.log(l)
            return carry

        lax.fori_loop(0, ROWS, row, 0)

    full = pl.BlockSpec((S, HD), lambda h: (0, NH_A + h))
    return pl.pallas_call(
        body, name="attn_b_fwd", grid=(4,),
        in_specs=[full, full, full, pl.BlockSpec((None, N_OFF, GRID_W, KEYS_B), lambda h: (h, 0, 0, 0))],
        out_specs=[pl.BlockSpec((S, HD), lambda h: (0, h)), pl.BlockSpec((None, S, 1), lambda h: (h, 0, 0))],
        out_shape=[jax.ShapeDtypeStruct((S, D_BR), BF16), jax.ShapeDtypeStruct((4, S, 1), F32)],
        compiler_params=_params(("parallel",)),
    )(qn, kn, vb, bias_tab)


def _attn_b_bwd(qn, kn, vb, bias_tab, ob, dob, lse):
    def body(q_ref, k_ref, v_ref, b_ref, o_ref, do_ref, lse_ref, dq_ref, dk_ref, dv_ref, db_ref):
        dk_ref[...] = jnp.zeros((S, HD), F32)
        dv_ref[...] = jnp.zeros((S, HD), F32)
        db_ref[...] = jnp.zeros((N_OFF, GRID_W, KEYS_B), F32)

        def row(r, carry):
            qs, ks, off = _row_window(r)
            rows = pl.ds(qs, GRID_W)
            keys = pl.ds(ks, KEYS_B)
            q = q_ref[rows, :]
            kw = k_ref[keys, :]
            s = lax.dot_general(q, kw, NT, preferred_element_type=F32) * SCALE + b_ref[off]
            p = jnp.exp(s - lse_ref[rows, :])
            do = do_ref[rows, :]
            dobf = do.astype(BF16)
            dsum = jnp.sum(do * o_ref[rows, :].astype(F32), axis=-1, keepdims=True)
            dp = lax.dot_general(dobf, v_ref[keys, :], NT, preferred_element_type=F32)
            ds = p * (dp - dsum)
            db_ref[off] += ds
            dsb = (ds * SCALE).astype(BF16)
            dq_ref[rows, :] = lax.dot_general(dsb, kw, NN, preferred_element_type=F32)
            dk_ref[keys, :] += lax.dot_general(dsb, q, TN, preferred_element_type=F32)
            dv_ref[keys, :] += lax.dot_general(p.astype(BF16), dobf, TN, preferred_element_type=F32)
            return carry

        lax.fori_loop(0, ROWS, row, 0)

    full = pl.BlockSpec((S, HD), lambda h: (0, NH_A + h))
    slot = pl.BlockSpec((S, HD), lambda h: (0, h))
    tab = pl.BlockSpec((None, N_OFF, GRID_W, KEYS_B), lambda h: (h, 0, 0, 0))
    shape = jax.ShapeDtypeStruct((S, D_BR), F32)
    return pl.pallas_call(
        body, name="attn_b_bwd", grid=(4,),
        in_specs=[full, full, full, tab, slot, slot, pl.BlockSpec((None, S, 1), lambda h: (h, 0, 0))],
        out_specs=[slot, slot, slot, tab],
        out_shape=[shape, shape, shape, jax.ShapeDtypeStruct((4, N_OFF, GRID_W, KEYS_B), F32)],
        compiler_params=_params(("parallel",)),
    )(qn, kn, vb, bias_tab, ob, dob, lse)


def _epi_relu_sq(acc, ex, outs):
    u = jnp.maximum(acc, 0.0)
    outs[0][...] = u.astype(BF16)
    outs[1][...] = (u * u).astype(BF16)


def _epi_relu_sq_bwd(acc, ex, outs):
    outs[0][...] = (acc * (2.0 * ex[0][...].astype(F32))).astype(BF16)


def _local_step(x, target, norm_mix, b_gate, gains, rpb_pad, norm_ffn,
                w_in, w_pa, w_pb, w_out, w_up, w_down):
    cos2, sin2 = _rope_tables()
    expand, keep, sel = _bias_constants()
    w_out3, w_down3 = w_out[None], w_down[None]

    xn, rstd1 = _rms_fwd(x, norm_mix, name="rms_mix")
    proj = _mm_nn(xn, w_in, tm=512, tn=1280, tk=D, name="proj")
    qn, kn, vb = _qk_prep(proj, gains, cos2, sin2)
    fwd_a = [_attn_a_fwd(qn, kn, vb, gi) for gi in range(3)]
    oa, lse_a = _combine_a([o for o, _ in fwd_a], [l for _, l in fwd_a])
    bias_tab = _rows_to_tab(_bias_expand(rpb_pad, expand, keep, sel))
    ob, lse_b = _attn_b_fwd(qn, kn, vb, bias_tab)
    ya = _mm_nn(oa, w_pa, tm=1024, tn=256, tk=D_BR, name="proj_a")
    yb = _mm_nn(ob, w_pb, tm=1024, tn=256, tk=D_BR, name="proj_b")
    mixed = _gate_fwd(proj, b_gate, ya, yb)
    h1 = _mm_nn(mixed, w_out3, tm=512, tn=512, tk=D, name="out_proj", epi=_epi_residual, extra=(x,))
    hn, rstd2 = _rms_fwd(h1, norm_ffn, name="rms_ffn")
    u, usq = _mm_nn(hn, w_up, tm=512, tn=1024, tk=D, name="ffn_up", epi=_epi_relu_sq,
                    out_dtypes=(BF16, BF16))
    h2 = _mm_nn(usq, w_down3, tm=512, tn=1024, tk=D, name="ffn_down", epi=_epi_residual, extra=(h1,))
    dy, dyb, loss = _loss_head(h2, target)

    g_down = _mm_tn(usq, dyb, tm=512, tn=1024, name="grad_w_down")
    du = _mm_nt(dyb, w_down3, tm=512, tn=1024, tk=D, name="ffn_down_bwd", out_dtype=BF16,
                epi=_epi_relu_sq_bwd, extra=(u,))
    g_up = _mm_tn(hn, du, tm=512, tn=1024, groups=N_DEV, name="grad_w_up")
    dhn = _mm_nt(du, w_up, tm=512, tn=1024, tk=1024, name="ffn_up_bwd")
    dh1, dh1b, g_norm_ffn = _rms_bwd(dhn, h1, rstd2, norm_ffn, dy, name="rms_ffn_bwd")

    g_out = _mm_tn(mixed, dh1b, tm=512, tn=1024, name="grad_w_out")
    dmixed = _mm_nt(dh1b, w_out3, tm=512, tn=512, tk=D, name="out_proj_bwd")
    dya, dproj, g_ba = _gate_bwd(0, dmixed, proj, b_gate, ya, None)
    dyb2, dproj, g_bb = _gate_bwd(1, dmixed, proj, b_gate, yb, dproj)
    g_pa = _mm_tn(oa, dya, tm=512, tn=256, groups=N_DEV, name="grad_w_proj_a")
    g_pb = _mm_tn(ob, dyb2, tm=512, tn=256, groups=N_DEV, name="grad_w_proj_b")
    doa = _mm_nt(dya, w_pa, tm=1024, tn=512, tk=256, name="proj_a_bwd")
    dob = _mm_nt(dyb2, w_pb, tm=1024, tn=512, tk=256, name="proj_b_bwd")
    bwd = [_attn_a_bwd(qn, kn, vb, oa, doa, lse_a, gi) for gi in range(3)]
    dqb, dkb, dvb, dbias = _attn_b_bwd(qn, kn, vb, bias_tab, ob, dob, lse_b)
    g_rpb = _bias_reduce(_tab_to_rows(dbias), expand, sel)
    dproj, g_gains = _qk_prep_bwd(dproj, proj, gains, cos2, sin2,
                                  [b[0] for b in bwd] + [dqb], [b[1] for b in bwd] + [dkb],
                                  [b[2] for b in bwd] + [dvb])
    g_in = _mm_tn(xn, dproj, tm=512, tn=1280, groups=N_DEV, name="grad_w_in")
    dxn = _mm_nt(dproj, w_in, tm=512, tn=1024, tk=1280, name="proj_bwd")
    grad_x, _, g_norm_mix = _rms_bwd(dxn, x, rstd1, norm_mix, dh1, name="rms_mix_bwd")

    big = (g_in, g_pa, g_pb, g_out.reshape(N_DEV, D // N_DEV, D), g_up, g_down.reshape(N_DEV, D_FF // N_DEV, D))
    small = (g_norm_mix, g_ba, g_bb, g_gains, g_rpb, g_norm_ffn)
    return loss, grad_x, big, small


def _cast_bf16(w, *, tr=256):
    rows, cols = w.shape
    tr = min(tr, rows)

    def body(w_ref, o_ref):
        o_ref[...] = w_ref[...].astype(BF16)

    spec = pl.BlockSpec((tr, cols), lambda i: (i, 0))
    return pl.pallas_call(
        body, name=f"cast_{rows}x{cols}", grid=(rows // tr,), in_specs=[spec], out_specs=spec,
        out_shape=jax.ShapeDtypeStruct((rows, cols), BF16), compiler_params=_params(("parallel",)),
    )(w)


def _me_and_peers():
    x, y, c = lax.axis_index("x"), lax.axis_index("y"), lax.axis_index("c")
    me = 4 * x + 2 * y + c
    peers = []
    for k in range(1, N_DEV):
        px = 1 - x if k & 4 else x
        py = 1 - y if k & 2 else y
        pc = 1 - c if k & 1 else c
        peers.append(((px, py, pc), 4 * px + 2 * py + pc))
    return me, peers


def _all_gather(shards):
    n = len(shards)

    def body(*refs):
        ins, outs = refs[:n], refs[n:2 * n]
        send, recv, lsem = refs[2 * n:]
        me, peers = _me_and_peers()
        local = [pltpu.make_async_copy(ins[w], outs[w].at[me], lsem.at[w]) for w in range(n)]
        for cp in local:
            cp.start()
        sends = []
        for k, (dev, _) in enumerate(peers):
            for w in range(n):
                cp = pltpu.make_async_remote_copy(ins[w], outs[w].at[me], send.at[w, k], recv.at[w, k],
                                                  device_id=dev, device_id_type=MESH)
                cp.start()
                sends.append(cp)
        for k, (dev, idx) in enumerate(peers):
            for w in range(n):
                pltpu.make_async_remote_copy(ins[w], outs[w].at[idx], send.at[w, k], recv.at[w, k],
                                             device_id=dev, device_id_type=MESH).wait_recv()
        for cp in sends:
            cp.wait_send()
        for cp in local:
            cp.wait()

    hbm = pl.BlockSpec(memory_space=pl.ANY)
    return pl.pallas_call(
        body, name="all_gather_weights",
        in_specs=[hbm] * n, out_specs=[hbm] * n,
        out_shape=[jax.ShapeDtypeStruct((N_DEV,) + s.shape, s.dtype) for s in shards],
        scratch_shapes=[pltpu.SemaphoreType.DMA((n, N_DEV - 1)), pltpu.SemaphoreType.DMA((n, N_DEV - 1)),
                        pltpu.SemaphoreType.DMA((n,))],
    )(*shards)


def _exchange_grads(grads):
    n = len(grads)

    def body(*refs):
        ins, outs = refs[:n], refs[n:2 * n]
        send, recv, lsem = refs[2 * n:]
        me, peers = _me_and_peers()
        local = [pltpu.make_async_copy(ins[w].at[me], outs[w].at[me], lsem.at[w]) for w in range(n)]
        for cp in local:
            cp.start()
        sends = []
        for k, (dev, idx) in enumerate(peers):
            for w in range(n):
                cp = pltpu.make_async_remote_copy(ins[w].at[idx], outs[w].at[me], send.at[w, k], recv.at[w, k],
                                                  device_id=dev, device_id_type=MESH)
                cp.start()
                sends.append(cp)
        for k, (dev, idx) in enumerate(peers):
            for w in range(n):
                pltpu.make_async_remote_copy(ins[w].at[me], outs[w].at[idx], send.at[w, k], recv.at[w, k],
                                             device_id=dev, device_id_type=MESH).wait_recv()
        for cp in sends:
            cp.wait_send()
        for cp in local:
            cp.wait()

    hbm = pl.BlockSpec(memory_space=pl.ANY)
    return pl.pallas_call(
        body, name="exchange_grads",
        in_specs=[hbm] * n, out_specs=[hbm] * n,
        out_shape=[jax.ShapeDtypeStruct(g.shape, g.dtype) for g in grads],
        scratch_shapes=[pltpu.SemaphoreType.DMA((n, N_DEV - 1)), pltpu.SemaphoreType.DMA((n, N_DEV - 1)),
                        pltpu.SemaphoreType.DMA((n,))],
    )(*grads)


def _adamw_math(g, w, m, v):
    m2 = B1 * m + (1.0 - B1) * g
    v2 = B2 * v + (1.0 - B2) * (g * g)
    delta = -LR * ((m2 / BC1) / (jnp.sqrt(v2 / BC2) + AEPS) + WD * w)
    return delta, m2, v2


def _adamw(parts, w, m, v, *, name, tr=128):
    rows, cols = w.shape

    def body(p_ref, w_ref, m_ref, v_ref, g_ref, d_ref, mo_ref, vo_ref):
        g = p_ref[0].astype(F32)
        for b in range(1, N_DEV):
            g = g + p_ref[b].astype(F32)
        delta, m2, v2 = _adamw_math(g, w_ref[...], m_ref[...], v_ref[...])
        g_ref[...] = g
        d_ref[...] = delta
        mo_ref[...] = m2
        vo_ref[...] = v2

    spec = pl.BlockSpec((tr, cols), lambda i: (i, 0))
    shape = jax.ShapeDtypeStruct((rows, cols), F32)
    return pl.pallas_call(
        body, name=name, grid=(rows // tr,),
        in_specs=[pl.BlockSpec((N_DEV, tr, cols), lambda i: (0, i, 0)), spec, spec, spec],
        out_specs=[spec] * 4, out_shape=[shape] * 4,
        compiler_params=_params(("parallel",)),
    )(parts, w, m, v)


def _small_update(part, w, m, v):
    rows = part.shape[0]

    def body(p_ref, w_ref, m_ref, v_ref, g_ref, d_ref, mo_ref, vo_ref, buf, send, recv):
        me, peers = _me_and_peers()
        buf[me] = p_ref[...]
        sends = []
        for k, (dev, _) in enumerate(peers):
            cp = pltpu.make_async_remote_copy(p_ref, buf.at[me], send.at[k], recv.at[k],
                                              device_id=dev, device_id_type=MESH)
            cp.start()
            sends.append(cp)
        for k, (dev, idx) in enumerate(peers):
            pltpu.make_async_remote_copy(p_ref, buf.at[idx], send.at[k], recv.at[k],
                                         device_id=dev, device_id_type=MESH).wait_recv()
        for cp in sends:
            cp.wait_send()
        g = buf[0]
        for b in range(1, N_DEV):
            g = g + buf[b]
        delta, m2, v2 = _adamw_math(g, w_ref[...], m_ref[...], v_ref[...])
        g_ref[...] = g
        d_ref[...] = delta
        mo_ref[...] = m2
        vo_ref[...] = v2

    vm = pl.BlockSpec(memory_space=pltpu.VMEM)
    shape = jax.ShapeDtypeStruct((rows, HD), F32)
    return pl.pallas_call(
        body, name="small_params_update",
        in_specs=[vm] * 4, out_specs=[vm] * 4, out_shape=[shape] * 4,
        scratch_shapes=[pltpu.VMEM((N_DEV, rows, HD), F32),
                        pltpu.SemaphoreType.DMA((N_DEV - 1,)), pltpu.SemaphoreType.DMA((N_DEV - 1,))],
    )(part, w, m, v)


_SMALL = (("norm_mix", 16), ("b_gate", 32), ("qk_gains", 8), ("rpb_b", 64), ("norm_ffn", 16))


def _pack_small(norm_mix, b_gate, qa, ka, qb, kb, rpb, norm_ffn):
    gains = jnp.concatenate([qa, ka, qb, kb, jnp.zeros((4, HD), F32)], axis=0)
    rpb_pad = jnp.pad(rpb.reshape(4 * (2 * WIN_R - 1), 2 * WIN_C - 1), ((0, 4), (0, HD - (2 * WIN_C - 1))))
    return jnp.concatenate([norm_mix.reshape(16, HD), b_gate.reshape(32, HD), gains, rpb_pad,
                            norm_ffn.reshape(16, HD)], axis=0)


def _unpack_small(p):
    norm_mix = p[0:16].reshape(1, D)
    b_gate = p[16:48].reshape(1, 2 * D)
    qa, ka, qb, kb = (p[48 + i:49 + i] for i in range(4))
    rpb = p[56:116, :2 * WIN_C - 1].reshape(1, 4, 2 * WIN_R - 1, 2 * WIN_C - 1)
    norm_ffn = p[120:136].reshape(1, D)
    return norm_mix, b_gate, qa, ka, qb, kb, rpb, norm_ffn


def kernel(x, norm_mix, w_in, b_gate, q_norm_a, k_norm_a, q_norm_b, k_norm_b, rpb_b, w_proj_a, w_proj_b, w_out, norm_ffn, w_up, w_down, loss_target, m_norm_mix, m_w_in, m_b_gate, m_q_norm_a, m_k_norm_a, m_q_norm_b, m_k_norm_b, m_rpb_b, m_w_proj_a, m_w_proj_b, m_w_out, m_norm_ffn, m_w_up, m_w_down, v_norm_mix, v_w_in, v_b_gate, v_q_norm_a, v_k_norm_a, v_q_norm_b, v_k_norm_b, v_rpb_b, v_w_proj_a, v_w_proj_b, v_w_out, v_norm_ffn, v_w_up, v_w_down):
    big_w = (w_in[0], w_proj_a[0], w_proj_b[0], w_out[0], w_up[0], w_down[0])
    big_m = (m_w_in[0], m_w_proj_a[0], m_w_proj_b[0], m_w_out[0], m_w_up[0], m_w_down[0])
    big_v = (v_w_in[0], v_w_proj_a[0], v_w_proj_b[0], v_w_out[0], v_w_up[0], v_w_down[0])
    names = ("w_in", "w_proj_a", "w_proj_b", "w_out", "w_up", "w_down")

    g_in, g_pa, g_pb, g_out, g_up, g_down = _all_gather([_cast_bf16(w) for w in big_w])
    small_w = _pack_small(norm_mix, b_gate, q_norm_a, k_norm_a, q_norm_b, k_norm_b, rpb_b, norm_ffn)
    small_m = _pack_small(m_norm_mix, m_b_gate, m_q_norm_a, m_k_norm_a, m_q_norm_b, m_k_norm_b, m_rpb_b, m_norm_ffn)
    small_v = _pack_small(v_norm_mix, v_b_gate, v_q_norm_a, v_k_norm_a, v_q_norm_b, v_k_norm_b, v_rpb_b, v_norm_ffn)

    loss, grad_x, big_g, small_g = _local_step(
        x[0], loss_target[0], norm_mix, b_gate, small_w[48:56], small_w[56:120], norm_ffn,
        g_in, g_pa, g_pb, g_out.reshape(D, D), g_up, g_down.reshape(D_FF, D))

    g_norm_mix, g_ba, g_bb, g_gains, g_rpb, g_norm_ffn = small_g
    small_part = jnp.concatenate([g_norm_mix.reshape(16, HD), g_ba.reshape(16, HD), g_bb.reshape(16, HD),
                                  g_gains, g_rpb, g_norm_ffn.reshape(16, HD)], axis=0)
    s_g, s_d, s_m, s_v = (_unpack_small(t) for t in _small_update(small_part, small_w, small_m, small_v))

    recv = _exchange_grads(list(big_g))
    upd = [_adamw(recv[i], big_w[i], big_m[i], big_v[i], name=f"adamw_{names[i]}") for i in range(6)]
    b_g, b_d, b_m, b_v = ([u[j][None] for u in upd] for j in range(4))

    def order(small, big):
        nm, bg, qa, ka, qb, kb, rpb, nf = small
        w_in_, pa_, pb_, out_, up_, down_ = big
        return (nm, w_in_, bg, qa, ka, qb, kb, rpb, pa_, pb_, out_, nf, up_, down_)

    total = lax.psum(loss[0, 0], ("x", "y", "c"))
    return (total, grad_x[None], *order(s_g, b_g), *order(s_d, b_d), *order(s_m, b_m), *order(s_v, b_v))
```

```python
import functools

import jax
import jax.numpy as jnp
import numpy as np
from jax import lax
from jax.experimental import pallas as pl
from jax.experimental.pallas import tpu as pltpu

F32 = jnp.float32
BF16 = jnp.bfloat16

N_DEV = 8
S = 2048
D = 2048
HD = 128
NH = 16
NH_A = 12
QKV = NH * HD
D_IN = 3 * QKV + 2 * D
D_BR = 512
D_FF = 4 * D
GRID_W = 64
ROWS = S // GRID_W
WIN_R = 8
WIN_C = 16
EPS = 1e-6
NEG = -1e30
SCALE = HD ** -0.5
ROPE_THETA = 10000.0
GROUPS_A = ((64, 1, 512), (256, 4, 768), (1024, 16, 2048))
QB = 256

LR, B1, B2, AEPS, WD, STEP = 0.001, 0.9, 0.999, 1e-08, 0.01, 10
BC1 = 1.0 - B1 ** STEP
BC2 = 1.0 - B2 ** STEP

VMEM_LIMIT = 56 * 1024 * 1024
MESH = pl.DeviceIdType.MESH

NN = (((1,), (0,)), ((), ()))
NT = (((1,), (1,)), ((), ()))
TN = (((0,), (0,)), ((), ()))


def _params(sem):
    return pltpu.CompilerParams(dimension_semantics=sem, vmem_limit_bytes=VMEM_LIMIT)


def _matmul(a, b, *, dims, grid, a_spec, b_spec, nk, epi, out_shape, out_specs, name,
            extra=(), extra_specs=(), acc_shape=None):
    n_extra = len(extra)

    def body(a_ref, b_ref, *rest):
        ex = rest[:n_extra]
        if nk == 1:
            outs = rest[n_extra:]
            epi(lax.dot_general(a_ref[...], b_ref[...], dims, preferred_element_type=F32), ex, outs)
            return
        outs, acc = rest[n_extra:-1], rest[-1]
        k = pl.program_id(2)
        part = lax.dot_general(a_ref[...], b_ref[...], dims, preferred_element_type=F32)

        @pl.when(k == 0)
        def _():
            acc[...] = part

        @pl.when(k > 0)
        def _():
            acc[...] += part

        @pl.when(k == nk - 1)
        def _():
            epi(acc[...], ex, outs)

    scratch = [] if nk == 1 else [pltpu.VMEM(acc_shape, F32)]
    return pl.pallas_call(
        body, name=name, grid=grid,
        in_specs=[a_spec, b_spec, *extra_specs],
        out_specs=out_specs, out_shape=out_shape, scratch_shapes=scratch,
        compiler_params=_params(("parallel", "parallel", "arbitrary")),
    )(a, b, *extra)


def _epi_store(acc, ex, outs):
    outs[0][...] = acc.astype(outs[0].dtype)


def _epi_residual(acc, ex, outs):
    outs[0][...] = acc + ex[0][...]


def _mm_nn(a, b3, *, tm, tn, tk, name, out_dtype=F32, epi=_epi_store, extra=(), n_out=1,
           out_dtypes=None):
    m, kdim = a.shape
    g, _, ng = b3.shape
    n = g * ng
    npg = ng // tn
    nk = kdim // tk
    grid = (n // tn, m // tm, nk)
    tile = pl.BlockSpec((tm, tn), lambda j, i, k: (i, j))
    dts = out_dtypes or (out_dtype,) * n_out
    shapes = tuple(jax.ShapeDtypeStruct((m, n), dt) for dt in dts)
    return _matmul(
        a, b3, dims=NN, grid=grid, nk=nk, epi=epi, name=name,
        a_spec=pl.BlockSpec((tm, tk), lambda j, i, k: (i, k)),
        b_spec=pl.BlockSpec((None, tk, tn), lambda j, i, k: (j // npg, k, j % npg)),
        extra=extra, extra_specs=[tile] * len(extra),
        out_shape=shapes if len(dts) > 1 else shapes[0],
        out_specs=[tile] * len(dts) if len(dts) > 1 else tile,
        acc_shape=(tm, tn))


def _mm_nt(a, b3, *, tm, tn, tk, name, out_dtype=F32, epi=_epi_store, extra=()):
    m, kdim = a.shape
    g, n, kg = b3.shape
    kpg = kg // tk
    nk = kdim // tk
    grid = (n // tn, m // tm, nk)
    tile = pl.BlockSpec((tm, tn), lambda j, i, k: (i, j))
    return _matmul(
        a, b3, dims=NT, grid=grid, nk=nk, epi=epi, name=name,
        a_spec=pl.BlockSpec((tm, tk), lambda j, i, k: (i, k)),
        b_spec=pl.BlockSpec((None, tn, tk), lambda j, i, k: (k // kpg, j, k % kpg)),
        extra=extra, extra_specs=[tile] * len(extra),
        out_shape=jax.ShapeDtypeStruct((m, n), out_dtype), out_specs=tile,
        acc_shape=(tm, tn))


def _mm_tn(a, b, *, tm, tn, name, groups=1, out_dtype=BF16):
    t, m = a.shape
    _, n = b.shape
    ng = n // groups
    npg = ng // tn
    grid = (n // tn, m // tm, 1)
    return _matmul(
        a, b, dims=TN, grid=grid, nk=1, epi=_epi_store, name=name,
        a_spec=pl.BlockSpec((t, tm), lambda j, i, k: (0, i)),
        b_spec=pl.BlockSpec((t, tn), lambda j, i, k: (0, j)),
        out_shape=jax.ShapeDtypeStruct((groups, m, ng), out_dtype),
        out_specs=pl.BlockSpec((None, tm, tn), lambda j, i, k: (j // npg, i, j % npg)))


def _rms_fwd(x, g, *, name, tr=256):
    def body(x_ref, g_ref, y_ref, r_ref):
        xv = x_ref[...]
        r = lax.rsqrt(jnp.mean(xv * xv, axis=-1, keepdims=True) + EPS)
        y_ref[...] = (xv * r * g_ref[...]).astype(BF16)
        r_ref[...] = r

    row = pl.BlockSpec((tr, D), lambda i: (i, 0))
    return pl.pallas_call(
        body, name=name, grid=(S // tr,),
        in_specs=[row, pl.BlockSpec((1, D), lambda i: (0, 0))],
        out_specs=[row, pl.BlockSpec((tr, 1), lambda i: (i, 0))],
        out_shape=[jax.ShapeDtypeStruct((S, D), BF16), jax.ShapeDtypeStruct((S, 1), F32)],
        compiler_params=_params(("parallel",)),
    )(x, g)


def _rms_bwd(dy, x, rstd, g, resid, *, name, tr=256):
    def body(dy_ref, x_ref, r_ref, g_ref, res_ref, dx_ref, dxb_ref, dg_ref):
        r = r_ref[...]
        xh = x_ref[...] * r
        dyv = dy_ref[...]
        t = dyv * g_ref[...]
        dx = r * (t - xh * jnp.mean(t * xh, axis=-1, keepdims=True)) + res_ref[...]
        dx_ref[...] = dx
        dxb_ref[...] = dx.astype(BF16)
        part = jnp.sum(dyv * xh, axis=0, keepdims=True)

        @pl.when(pl.program_id(0) == 0)
        def _():
            dg_ref[...] = part

        @pl.when(pl.program_id(0) > 0)
        def _():
            dg_ref[...] += part

    row = pl.BlockSpec((tr, D), lambda i: (i, 0))
    vec = pl.BlockSpec((1, D), lambda i: (0, 0))
    return pl.pallas_call(
        body, name=name, grid=(S // tr,),
        in_specs=[row, row, pl.BlockSpec((tr, 1), lambda i: (i, 0)), vec, row],
        out_specs=[row, row, vec],
        out_shape=[jax.ShapeDtypeStruct((S, D), F32), jax.ShapeDtypeStruct((S, D), BF16),
                   jax.ShapeDtypeStruct((1, D), F32)],
        compiler_params=_params(("arbitrary",)),
    )(dy, x, rstd, g, resid)


def _loss_head(h2, target, *, tr=256):
    def body(h_ref, t_ref, dy_ref, dyb_ref, loss_ref):
        e = h_ref[...] - t_ref[...]
        dy = e * (1.0 / D)
        dy_ref[...] = dy
        dyb_ref[...] = dy.astype(BF16)
        part = (0.5 / D) * jnp.sum(jnp.sum(e * e, axis=-1, keepdims=True), axis=0, keepdims=True)

        @pl.when(pl.program_id(0) == 0)
        def _():
            loss_ref[...] = part

        @pl.when(pl.program_id(0) > 0)
        def _():
            loss_ref[...] += part

    row = pl.BlockSpec((tr, D), lambda i: (i, 0))
    return pl.pallas_call(
        body, name="loss_head", grid=(S // tr,),
        in_specs=[row, row],
        out_specs=[row, row, pl.BlockSpec((1, 1), lambda i: (0, 0))],
        out_shape=[jax.ShapeDtypeStruct((S, D), F32), jax.ShapeDtypeStruct((S, D), BF16),
                   jax.ShapeDtypeStruct((1, 1), F32)],
        compiler_params=_params(("arbitrary",)),
    )(h2, target)


def _rope_tables():
    pos = np.arange(S, dtype=np.float32)
    inv = (ROPE_THETA ** (-np.arange(0, HD, 2, dtype=np.float32) / HD)).astype(np.float32)
    ang = pos[:, None] * inv[None, :]
    cos, sin = np.cos(ang), np.sin(ang)
    return (jnp.asarray(np.concatenate([cos, cos], axis=-1), F32),
            jnp.asarray(np.concatenate([-sin, sin], axis=-1), F32))


def _swap_halves(t):
    return pltpu.roll(t, HD // 2, axis=1)


def _qk_prep(proj, gains, cos2, sin2, *, tr=256):
    def body(q_ref, k_ref, v_ref, g_ref, c_ref, s_ref, qn_ref, kn_ref, vb_ref):
        cos, sin = c_ref[...], s_ref[...]
        for src, dst, row_a, row_b in ((q_ref, qn_ref, 0, 2), (k_ref, kn_ref, 1, 3)):
            for h in range(NH):
                cols = slice(h * HD, (h + 1) * HD)
                t = src[:, cols]
                r = lax.rsqrt(jnp.mean(t * t, axis=-1, keepdims=True) + EPS)
                if h < NH_A:
                    y = t * r * g_ref[row_a:row_a + 1, :]
                    y = y * cos + _swap_halves(y) * sin
                else:
                    y = t * r * g_ref[row_b:row_b + 1, :]
                dst[:, cols] = y.astype(BF16)
        vb_ref[...] = v_ref[...].astype(BF16)

    def blk(c):
        return pl.BlockSpec((tr, QKV), lambda i: (i, c))
    tab = pl.BlockSpec((tr, HD), lambda i: (i, 0))
    out = pl.BlockSpec((tr, QKV), lambda i: (i, 0))
    return pl.pallas_call(
        body, name="qk_prep", grid=(S // tr,),
        in_specs=[blk(0), blk(1), blk(2), pl.BlockSpec((8, HD), lambda i: (0, 0)), tab, tab],
        out_specs=[out, out, out],
        out_shape=[jax.ShapeDtypeStruct((S, QKV), BF16)] * 3,
        compiler_params=_params(("parallel",)),
    )(proj, proj, proj, gains, cos2, sin2)


def _qk_prep_bwd(dproj, proj, gains, cos2, sin2, dq_parts, dk_parts, dv_parts, *, tr=256):
    def body(dp_in, q_ref, k_ref, g_ref, c_ref, s_ref, *rest):
        dqs, dks, dvs = rest[0:4], rest[4:8], rest[8:12]
        dp_out, dg_ref = rest[12:14]
        del dp_in
        cos, sin = c_ref[...], s_ref[...]
        dg_rows = []
        for src, grads, base, row_a, row_b in ((q_ref, dqs, 0, 0, 2), (k_ref, dks, QKV, 1, 3)):
            dg_a = jnp.zeros((1, HD), F32)
            dg_b = jnp.zeros((1, HD), F32)
            for h in range(NH):
                cols = slice(h * HD, (h + 1) * HD)
                t = src[:, cols]
                dy = grads[h // 4][:, (h % 4) * HD:(h % 4 + 1) * HD]
                r = lax.rsqrt(jnp.mean(t * t, axis=-1, keepdims=True) + EPS)
                xh = t * r
                if h < NH_A:
                    dy = dy * cos - _swap_halves(dy) * sin
                    gain = g_ref[row_a:row_a + 1, :]
                    dg_a = dg_a + jnp.sum(dy * xh, axis=0, keepdims=True)
                else:
                    gain = g_ref[row_b:row_b + 1, :]
                    dg_b = dg_b + jnp.sum(dy * xh, axis=0, keepdims=True)
                u = dy * gain
                dx = r * (u - xh * jnp.mean(u * xh, axis=-1, keepdims=True))
                dp_out[:, base + h * HD:base + (h + 1) * HD] = dx.astype(BF16)
            dg_rows += [(row_a, dg_a), (row_b, dg_b)]
        for g4 in range(4):
            dp_out[:, 2 * QKV + g4 * D_BR:2 * QKV + (g4 + 1) * D_BR] = dvs[g4][...].astype(BF16)

        first = pl.program_id(0) == 0

        @pl.when(first)
        def _():
            dg_ref[...] = jnp.zeros((8, HD), F32)

        for row, val in dg_rows:
            dg_ref[row:row + 1, :] += val

    def blk(c):
        return pl.BlockSpec((tr, QKV), lambda i: (i, c))
    tab = pl.BlockSpec((tr, HD), lambda i: (i, 0))
    part = pl.BlockSpec((tr, D_BR), lambda i: (i, 0))
    gain_spec = pl.BlockSpec((8, HD), lambda i: (0, 0))
    return pl.pallas_call(
        body, name="qk_prep_bwd", grid=(S // tr,),
        in_specs=[pl.BlockSpec(memory_space=pl.ANY), blk(0), blk(1), gain_spec, tab, tab] + [part] * 12,
        out_specs=[pl.BlockSpec((tr, 3 * QKV), lambda i: (i, 0)), gain_spec],
        out_shape=[jax.ShapeDtypeStruct((S, D_IN), BF16), jax.ShapeDtypeStruct((8, HD), F32)],
        input_output_aliases={0: 0},
        compiler_params=_params(("arbitrary",)),
    )(dproj, proj, proj, gains, cos2, sin2, *dq_parts, *dk_parts, *dv_parts)


def _gate_fwd(proj, b_gate, ya, yb, *, tr=256):
    def body(la_ref, lb_ref, ba_ref, bb_ref, ya_ref, yb_ref, o_ref):
        ga = jax.nn.sigmoid(la_ref[...] + ba_ref[...])
        gb = jax.nn.sigmoid(lb_ref[...] + bb_ref[...])
        o_ref[...] = (ga * ya_ref[...] + gb * yb_ref[...]).astype(BF16)

    row = pl.BlockSpec((tr, D), lambda i: (i, 0))
    return pl.pallas_call(
        body, name="gate_fwd", grid=(S // tr,),
        in_specs=[pl.BlockSpec((tr, D), lambda i: (i, 3)), pl.BlockSpec((tr, D), lambda i: (i, 4)),
                  pl.BlockSpec((1, D), lambda i: (0, 0)), pl.BlockSpec((1, D), lambda i: (0, 1)),
                  row, row],
        out_specs=row, out_shape=jax.ShapeDtypeStruct((S, D), BF16),
        compiler_params=_params(("parallel",)),
    )(proj, proj, b_gate, b_gate, ya, yb)


def _gate_bwd(branch, dmixed, proj, b_gate, y, dproj, *, tr=256):
    aliased = dproj is not None

    def body(dm_ref, l_ref, b_ref, y_ref, *rest):
        dy_ref, dp_ref, db_ref = rest[-3:]
        g = jax.nn.sigmoid(l_ref[...] + b_ref[...])
        dm = dm_ref[...]
        dy_ref[...] = (dm * g).astype(BF16)
        dl = dm * y_ref[...] * g * (1.0 - g)
        dp_ref[...] = dl.astype(BF16)
        part = jnp.sum(dl, axis=0, keepdims=True)

        @pl.when(pl.program_id(0) == 0)
        def _():
            db_ref[...] = part

        @pl.when(pl.program_id(0) > 0)
        def _():
            db_ref[...] += part

    row = pl.BlockSpec((tr, D), lambda i: (i, 0))
    col = pl.BlockSpec((tr, D), lambda i: (i, 3 + branch))
    vec = pl.BlockSpec((1, D), lambda i: (0, 0))
    return pl.pallas_call(
        body, name=f"gate_bwd_{branch}", grid=(S // tr,),
        in_specs=[row, col, pl.BlockSpec((1, D), lambda i: (0, branch)), row]
        + ([pl.BlockSpec(memory_space=pl.ANY)] if aliased else []),
        out_specs=[row, col, vec],
        out_shape=[jax.ShapeDtypeStruct((S, D), BF16), jax.ShapeDtypeStruct((S, D_IN), BF16),
                   jax.ShapeDtypeStruct((1, D), F32)],
        input_output_aliases={4: 1} if aliased else {},
        compiler_params=_params(("arbitrary",)),
    )(dmixed, proj, b_gate, y, *([dproj] if aliased else []))


def _window_start(t0, wk):
    if wk == S:
        return 0
    return pl.multiple_of(jnp.clip(t0 - (wk - QB) // 2, 0, S - wk), 128)


def _scores_a(q, kw, t0, start, hs, dil, wk):
    s = lax.dot_general(q, kw, NT, preferred_element_type=F32) * SCALE
    qpos = t0 + lax.broadcasted_iota(jnp.int32, (QB, 1), 0)
    kpos = start + lax.broadcasted_iota(jnp.int32, (1, wk), 1)
    diff = kpos - qpos
    keep = (jnp.abs(diff) <= hs) & ((diff & (dil - 1)) == 0)
    return jnp.where(keep, s, NEG)


def _attn_a_fwd(qn, kn, vb, gi):
    hs, dil, wk = GROUPS_A[gi]

    def body(q_ref, k_ref, v_ref, o_ref, lse_ref):
        t0 = pl.program_id(1) * QB
        start = _window_start(t0, wk)
        s = _scores_a(q_ref[...], k_ref[pl.ds(start, wk), :], t0, start, hs, dil, wk)
        m = jnp.max(s, axis=-1, keepdims=True)
        p = jnp.exp(s - m)
        l = jnp.sum(p, axis=-1, keepdims=True)
        o = lax.dot_general(p.astype(BF16), v_ref[pl.ds(start, wk), :], NN, preferred_element_type=F32)
        o_ref[...] = o / l
        lse_ref[...] = m + jnp.log(l)

    full = pl.BlockSpec((S, HD), lambda h, i: (0, 4 * gi + h))
    return pl.pallas_call(
        body, name=f"attn_a_fwd_{gi}", grid=(4, S // QB),
        in_specs=[pl.BlockSpec((QB, HD), lambda h, i: (i, 4 * gi + h)), full, full],
        out_specs=[pl.BlockSpec((QB, HD), lambda h, i: (i, h)),
                   pl.BlockSpec((None, QB, 1), lambda h, i: (h, i, 0))],
        out_shape=[jax.ShapeDtypeStruct((S, D_BR), F32), jax.ShapeDtypeStruct((4, S, 1), F32)],
        compiler_params=_params(("parallel", "parallel")),
    )(qn, kn, vb)


def _combine_a(os, lses, *, tr=256):
    def body(o0, o1, o2, l0, l1, l2, oa_ref, lse_ref):
        for h in range(4):
            cols = slice(h * HD, (h + 1) * HD)
            a, b, c = l0[h], l1[h], l2[h]
            m = jnp.maximum(jnp.maximum(a, b), c)
            wa, wb, wc = jnp.exp(a - m), jnp.exp(b - m), jnp.exp(c - m)
            tot = wa + wb + wc
            oa_ref[:, cols] = ((wa * o0[:, cols] + wb * o1[:, cols] + wc * o2[:, cols]) / tot).astype(BF16)
            lse_ref[h] = m + jnp.log(tot)

    row = pl.BlockSpec((tr, D_BR), lambda i: (i, 0))
    stat = pl.BlockSpec((4, tr, 1), lambda i: (0, i, 0))
    return pl.pallas_call(
        body, name="combine_a", grid=(S // tr,),
        in_specs=[row] * 3 + [stat] * 3, out_specs=[row, stat],
        out_shape=[jax.ShapeDtypeStruct((S, D_BR), BF16), jax.ShapeDtypeStruct((4, S, 1), F32)],
        compiler_params=_params(("parallel",)),
    )(*os, *lses)


def _attn_a_bwd(qn, kn, vb, oa, doa, lse, gi):
    hs, dil, wk = GROUPS_A[gi]

    def body(q_ref, k_ref, v_ref, o_ref, do_ref, lse_ref, dq_ref, dk_ref, dv_ref):
        @pl.when(pl.program_id(1) == 0)
        def _():
            dk_ref[...] = jnp.zeros((S, HD), F32)
            dv_ref[...] = jnp.zeros((S, HD), F32)

        t0 = pl.program_id(1) * QB
        start = _window_start(t0, wk)
        q = q_ref[...]
        kw = k_ref[pl.ds(start, wk), :]
        vw = v_ref[pl.ds(start, wk), :]
        p = jnp.exp(_scores_a(q, kw, t0, start, hs, dil, wk) - lse_ref[...])
        do = do_ref[...]
        dob = do.astype(BF16)
        dsum = jnp.sum(do * o_ref[...].astype(F32), axis=-1, keepdims=True)
        dp = lax.dot_general(dob, vw, NT, preferred_element_type=F32)
        ds = (p * (dp - dsum) * SCALE).astype(BF16)
        dq_ref[...] = lax.dot_general(ds, kw, NN, preferred_element_type=F32)
        dk_ref[pl.ds(start, wk), :] += lax.dot_general(ds, q, TN, preferred_element_type=F32)
        dv_ref[pl.ds(start, wk), :] += lax.dot_general(p.astype(BF16), dob, TN, preferred_element_type=F32)

    full = pl.BlockSpec((S, HD), lambda h, i: (0, 4 * gi + h))
    blk = pl.BlockSpec((QB, HD), lambda h, i: (i, h))
    acc = pl.BlockSpec((S, HD), lambda h, i: (0, h))
    shape = jax.ShapeDtypeStruct((S, D_BR), F32)
    return pl.pallas_call(
        body, name=f"attn_a_bwd_{gi}", grid=(4, S // QB),
        in_specs=[pl.BlockSpec((QB, HD), lambda h, i: (i, 4 * gi + h)), full, full, blk, blk,
                  pl.BlockSpec((None, QB, 1), lambda h, i: (h, i, 0))],
        out_specs=[blk, acc, acc], out_shape=[shape, shape, shape],
        compiler_params=_params(("parallel", "arbitrary")),
    )(qn, kn, vb, oa, doa, lse)


KEYS_B = WIN_R * GRID_W
N_OFF = WIN_R


def _bias_constants():
    q = np.arange(GRID_W)[:, None]
    kc = np.arange(GRID_W)[None, :]
    dc = np.clip(kc - q, -(WIN_C - 1), WIN_C - 1) + (WIN_C - 1)
    expand = np.zeros((HD, GRID_W * GRID_W), np.float32)
    expand[dc.reshape(-1), np.arange(GRID_W * GRID_W)] = 1.0
    cs = np.clip(q - WIN_C // 2, 0, GRID_W - WIN_C)
    keep = ((kc >= cs) & (kc < cs + WIN_C)).reshape(1, -1).astype(np.float32)
    sel = np.zeros((64, 4 * N_OFF * WIN_R), np.float32)
    for h in range(4):
        for off in range(N_OFF):
            for j in range(WIN_R):
                sel[h * (2 * WIN_R - 1) + off + j, (h * N_OFF + off) * WIN_R + j] = 1.0
    return jnp.asarray(expand), jnp.asarray(keep), jnp.asarray(sel)


def _bias_expand(rpb_pad, expand, keep, sel):
    def body(r_ref, e_ref, k_ref, s_ref, o_ref):
        t = lax.dot_general(r_ref[...], e_ref[...], NN, precision=lax.Precision.HIGHEST,
                            preferred_element_type=F32)
        rows = lax.dot_general(s_ref[...], t, TN, precision=lax.Precision.HIGHEST,
                               preferred_element_type=F32)
        o_ref[...] = jnp.where(k_ref[...] > 0.5, rows, NEG)

    return pl.pallas_call(
        body, name="bias_expand",
        out_shape=jax.ShapeDtypeStruct((4 * N_OFF * WIN_R, GRID_W * GRID_W), F32),
        compiler_params=pltpu.CompilerParams(vmem_limit_bytes=VMEM_LIMIT),
    )(rpb_pad, expand, keep, sel)


def _bias_reduce(dbias_rows, expand, sel):
    def body(x_ref, e_ref, s_ref, o_ref):
        z = lax.dot_general(x_ref[...], e_ref[...], NT, precision=lax.Precision.HIGHEST,
                            preferred_element_type=F32)
        o_ref[...] = lax.dot_general(s_ref[...], z, NN, precision=lax.Precision.HIGHEST,
                                     preferred_element_type=F32)

    return pl.pallas_call(
        body, name="bias_reduce", out_shape=jax.ShapeDtypeStruct((64, HD), F32),
        compiler_params=pltpu.CompilerParams(vmem_limit_bytes=VMEM_LIMIT),
    )(dbias_rows, expand, sel)


def _rows_to_tab(rows):
    t = rows.reshape(4, N_OFF, WIN_R, GRID_W, GRID_W)
    return t.transpose(0, 1, 3, 2, 4).reshape(4, N_OFF, GRID_W, KEYS_B)


def _tab_to_rows(tab):
    t = tab.reshape(4, N_OFF, GRID_W, WIN_R, GRID_W)
    return t.transpose(0, 1, 3, 2, 4).reshape(4 * N_OFF * WIN_R, GRID_W * GRID_W)


def _row_window(r):
    r0 = jnp.clip(r - WIN_R // 2, 0, ROWS - WIN_R)
    off = r0 + (WIN_R - 1) - r
    return pl.multiple_of(r * GRID_W, GRID_W), pl.multiple_of(r0 * GRID_W, GRID_W), off


def _attn_b_fwd(qn, kn, vb, bias_tab):
    def body(q_ref, k_ref, v_ref, b_ref, o_ref, lse_ref):
        def row(r, carry):
            qs, ks, off = _row_window(r)
            q = q_ref[pl.ds(qs, GRID_W), :]
            s = lax.dot_general(q, k_ref[pl.ds(ks, KEYS_B), :], NT, preferred_element_type=F32) * SCALE
            s = s + b_ref[off]
            m = jnp.max(s, axis=-1, keepdims=True)
            p = jnp.exp(s - m)
            l = jnp.sum(p, axis=-1, keepdims=True)
            o = lax.dot_general(p.astype(BF16), v_ref[pl.ds(ks, KEYS_B), :], NN, preferred_element_type=F32)
            o_ref[pl.ds(qs, GRID_W), :] = (o / l).astype(BF16)
            lse_ref[pl.ds(qs, GRID_W), :] = m + jnp.log(l)
            return carry

        lax.fori_loop(0, ROWS, row, 0)

    full = pl.BlockSpec((S, HD), lambda h: (0, NH_A + h))
    return pl.pallas_call(
        body, name="attn_b_fwd", grid=(4,),
        in_specs=[full, full, full, pl.BlockSpec((None, N_OFF, GRID_W, KEYS_B), lambda h: (h, 0, 0, 0))],
        out_specs=[pl.BlockSpec((S, HD), lambda h: (0, h)), pl.BlockSpec((None, S, 1), lambda h: (h, 0, 0))],
        out_shape=[jax.ShapeDtypeStruct((S, D_BR), BF16), jax.ShapeDtypeStruct((4, S, 1), F32)],
        compiler_params=_params(("parallel",)),
    )(qn, kn, vb, bias_tab)


def _attn_b_bwd(qn, kn, vb, bias_tab, ob, dob, lse):
    def body(q_ref, k_ref, v_ref, b_ref, o_ref, do_ref, lse_ref, dq_ref, dk_ref, dv_ref, db_ref):
        dk_ref[...] = jnp.zeros((S, HD), F32)
        dv_ref[...] = jnp.zeros((S, HD), F32)
        db_ref[...] = jnp.zeros((N_OFF, GRID_W, KEYS_B), F32)

        def row(r, carry):
            qs, ks, off = _row_window(r)
            rows = pl.ds(qs, GRID_W)
            keys = pl.ds(ks, KEYS_B)
            q = q_ref[rows, :]
            kw = k_ref[keys, :]
            s = lax.dot_general(q, kw, NT, preferred_element_type=F32) * SCALE + b_ref[off]
            p = jnp.exp(s - lse_ref[rows, :])
            do = do_ref[rows, :]
            dobf = do.astype(BF16)
            dsum = jnp.sum(do * o_ref[rows, :].astype(F32), axis=-1, keepdims=True)
            dp = lax.dot_general(dobf, v_ref[keys, :], NT, preferred_element_type=F32)
            ds = p * (dp - dsum)
            db_ref[off] += ds
            dsb = (ds * SCALE).astype(BF16)
            dq_ref[rows, :] = lax.dot_general(dsb, kw, NN, preferred_element_type=F32)
            dk_ref[keys, :] += lax.dot_general(dsb, q, TN, preferred_element_type=F32)
            dv_ref[keys, :] += lax.dot_general(p.astype(BF16), dobf, TN, preferred_element_type=F32)
            return carry

        lax.fori_loop(0, ROWS, row, 0)

    full = pl.BlockSpec((S, HD), lambda h: (0, NH_A + h))
    slot = pl.BlockSpec((S, HD), lambda h: (0, h))
    tab = pl.BlockSpec((None, N_OFF, GRID_W, KEYS_B), lambda h: (h, 0, 0, 0))
    shape = jax.ShapeDtypeStruct((S, D_BR), F32)
    return pl.pallas_call(
        body, name="attn_b_bwd", grid=(4,),
        in_specs=[full, full, full, tab, slot, slot, pl.BlockSpec((None, S, 1), lambda h: (h, 0, 0))],
        out_specs=[slot, slot, slot, tab],
        out_shape=[shape, shape, shape, jax.ShapeDtypeStruct((4, N_OFF, GRID_W, KEYS_B), F32)],
        compiler_params=_params(("parallel",)),
    )(qn, kn, vb, bias_tab, ob, dob, lse)


def _epi_relu_sq(acc, ex, outs):
    u = jnp.maximum(acc, 0.0)
    outs[0][...] = u.astype(BF16)
    outs[1][...] = (u * u).astype(BF16)


def _epi_relu_sq_bwd(acc, ex, outs):
    outs[0][...] = (acc * (2.0 * ex[0][...].astype(F32))).astype(BF16)


def _local_step(x, target, norm_mix, b_gate, gains, rpb_pad, norm_ffn,
                w_in, w_pa, w_pb, w_out, w_up, w_down):
    cos2, sin2 = _rope_tables()
    expand, keep, sel = _bias_constants()
    w_out3, w_down3 = w_out[None], w_down[None]

    xn, rstd1 = _rms_fwd(x, norm_mix, name="rms_mix")
    proj = _mm_nn(xn, w_in, tm=512, tn=1280, tk=D, name="proj")
    qn, kn, vb = _qk_prep(proj, gains, cos2, sin2)
    fwd_a = [_attn_a_fwd(qn, kn, vb, gi) for gi in range(3)]
    oa, lse_a = _combine_a([o for o, _ in fwd_a], [l for _, l in fwd_a])
    bias_tab = _rows_to_tab(_bias_expand(rpb_pad, expand, keep, sel))
    ob, lse_b = _attn_b_fwd(qn, kn, vb, bias_tab)
    ya = _mm_nn(oa, w_pa, tm=1024, tn=256, tk=D_BR, name="proj_a")
    yb = _mm_nn(ob, w_pb, tm=1024, tn=256, tk=D_BR, name="proj_b")
    mixed = _gate_fwd(proj, b_gate, ya, yb)
    h1 = _mm_nn(mixed, w_out3, tm=512, tn=512, tk=D, name="out_proj", epi=_epi_residual, extra=(x,))
    hn, rstd2 = _rms_fwd(h1, norm_ffn, name="rms_ffn")
    u, usq = _mm_nn(hn, w_up, tm=512, tn=1024, tk=D, name="ffn_up", epi=_epi_relu_sq,
                    out_dtypes=(BF16, BF16))
    h2 = _mm_nn(usq, w_down3, tm=512, tn=1024, tk=D, name="ffn_down", epi=_epi_residual, extra=(h1,))
    dy, dyb, loss = _loss_head(h2, target)

    g_down = _mm_tn(usq, dyb, tm=512, tn=1024, name="grad_w_down")
    du = _mm_nt(dyb, w_down3, tm=512, tn=1024, tk=D, name="ffn_down_bwd", out_dtype=BF16,
                epi=_epi_relu_sq_bwd, extra=(u,))
    g_up = _mm_tn(hn, du, tm=512, tn=1024, groups=N_DEV, name="grad_w_up")
    dhn = _mm_nt(du, w_up, tm=512, tn=1024, tk=1024, name="ffn_up_bwd")
    dh1, dh1b, g_norm_ffn = _rms_bwd(dhn, h1, rstd2, norm_ffn, dy, name="rms_ffn_bwd")

    g_out = _mm_tn(mixed, dh1b, tm=512, tn=1024, name="grad_w_out")
    dmixed = _mm_nt(dh1b, w_out3, tm=512, tn=512, tk=D, name="out_proj_bwd")
    dya, dproj, g_ba = _gate_bwd(0, dmixed, proj, b_gate, ya, None)
    dyb2, dproj, g_bb = _gate_bwd(1, dmixed, proj, b_gate, yb, dproj)
    g_pa = _mm_tn(oa, dya, tm=512, tn=256, groups=N_DEV, name="grad_w_proj_a")
    g_pb = _mm_tn(ob, dyb2, tm=512, tn=256, groups=N_DEV, name="grad_w_proj_b")
    doa = _mm_nt(dya, w_pa, tm=1024, tn=512, tk=256, name="proj_a_bwd")
    dob = _mm_nt(dyb2, w_pb, tm=1024, tn=512, tk=256, name="proj_b_bwd")
    bwd = [_attn_a_bwd(qn, kn, vb, oa, doa, lse_a, gi) for gi in range(3)]
    dqb, dkb, dvb, dbias = _attn_b_bwd(qn, kn, vb, bias_tab, ob, dob, lse_b)
    g_rpb = _bias_reduce(_tab_to_rows(dbias), expand, sel)
    dproj, g_gains = _qk_prep_bwd(dproj, proj, gains, cos2, sin2,
                                  [b[0] for b in bwd] + [dqb], [b[1] for b in bwd] + [dkb],
                                  [b[2] for b in bwd] + [dvb])
    g_in = _mm_tn(xn, dproj, tm=512, tn=1280, groups=N_DEV, name="grad_w_in")
    dxn = _mm_nt(dproj, w_in, tm=512, tn=1024, tk=1280, name="proj_bwd")
    grad_x, _, g_norm_mix = _rms_bwd(dxn, x, rstd1, norm_mix, dh1, name="rms_mix_bwd")

    big = (g_in, g_pa, g_pb, g_out.reshape(N_DEV, D // N_DEV, D), g_up, g_down.reshape(N_DEV, D_FF // N_DEV, D))
    small = (g_norm_mix, g_ba, g_bb, g_gains, g_rpb, g_norm_ffn)
    return loss, grad_x, big, small


def _cast_bf16(w, *, tr=256):
    rows, cols = w.shape
    tr = min(tr, rows)

    def body(w_ref, o_ref):
        o_ref[...] = w_ref[...].astype(BF16)

    spec = pl.BlockSpec((tr, cols), lambda i: (i, 0))
    return pl.pallas_call(
        body, name=f"cast_{rows}x{cols}", grid=(rows // tr,), in_specs=[spec], out_specs=spec,
        out_shape=jax.ShapeDtypeStruct((rows, cols), BF16), compiler_params=_params(("parallel",)),
    )(w)


def _me_and_peers():
    x, y, c = lax.axis_index("x"), lax.axis_index("y"), lax.axis_index("c")
    me = 4 * x + 2 * y + c
    peers = []
    for k in range(1, N_DEV):
        px = 1 - x if k & 4 else x
        py = 1 - y if k & 2 else y
        pc = 1 - c if k & 1 else c
        peers.append(((px, py, pc), 4 * px + 2 * py + pc))
    return me, peers


def _all_gather(shards):
    n = len(shards)

    def body(*refs):
        ins, outs = refs[:n], refs[n:2 * n]
        send, recv, lsem = refs[2 * n:]
        x, y, c = lax.axis_index("x"), lax.axis_index("y"), lax.axis_index("c")
        me, sibling = (x, y, c), (x, y, 1 - c)
        chips = [(1 - x, y), (x, 1 - y), (1 - x, 1 - y)]

        def copy(w, k, block, to, src=None):
            px, py, pc = block
            dst = outs[w].at[4 * px + 2 * py + pc]
            return pltpu.make_async_remote_copy(dst if src is None else src, dst, send.at[w, k], recv.at[w, k],
                                                device_id=to, device_id_type=MESH)

        local = [pltpu.make_async_copy(ins[w], outs[w].at[4 * x + 2 * y + c], lsem.at[w]) for w in range(n)]
        for cp in local:
            cp.start()
        first = []
        for w in range(n):
            first += [copy(w, 1 + j, me, (*chip, c), src=ins[w]) for j, chip in enumerate(chips)]
            first.append(copy(w, 0, me, sibling, src=ins[w]))
        for cp in first:
            cp.start()
        passed = []
        for w in range(n):
            for j, chip in enumerate(chips):
                copy(w, 1 + j, (*chip, c), me).wait_recv()
                cp = copy(w, 4 + j, (*chip, c), sibling)
                cp.start()
                passed.append(cp)
        for w in range(n):
            copy(w, 0, sibling, me).wait_recv()
            for j, chip in enumerate(chips):
                copy(w, 4 + j, (*chip, 1 - c), me).wait_recv()
        for cp in first + passed:
            cp.wait_send()
        for cp in local:
            cp.wait()

    hbm = pl.BlockSpec(memory_space=pl.ANY)
    return pl.pallas_call(
        body, name="all_gather_weights",
        in_specs=[hbm] * n, out_specs=[hbm] * n,
        out_shape=[jax.ShapeDtypeStruct((N_DEV,) + s.shape, s.dtype) for s in shards],
        scratch_shapes=[pltpu.SemaphoreType.DMA((n, N_DEV - 1)), pltpu.SemaphoreType.DMA((n, N_DEV - 1)),
                        pltpu.SemaphoreType.DMA((n,))],
    )(*shards)


N_CHIP = 4
CHIPS = ((0, 0), (0, 1), (1, 0), (1, 1))


def _pair_exchange(grads):
    n = len(grads)

    def body(*refs):
        ins, outs = refs[:n], refs[n:2 * n]
        send, recv = refs[2 * n:]
        x, y, c = lax.axis_index("x"), lax.axis_index("y"), lax.axis_index("c")
        sends = []
        for w in range(n):
            for ch, (px, py) in enumerate(CHIPS):
                cp = pltpu.make_async_remote_copy(ins[w].at[4 * px + 2 * py + 1 - c], outs[w].at[ch],
                                                  send.at[w, ch], recv.at[w, ch],
                                                  device_id=(x, y, 1 - c), device_id_type=MESH)
                cp.start()
                sends.append(cp)
        for cp in sends:
            cp.wait_recv()
        for cp in sends:
            cp.wait_send()

    hbm = pl.BlockSpec(memory_space=pl.ANY)
    return pl.pallas_call(
        body, name="pair_exchange_grads",
        in_specs=[hbm] * n, out_specs=[hbm] * n,
        out_shape=[jax.ShapeDtypeStruct((N_CHIP,) + g.shape[1:], g.dtype) for g in grads],
        scratch_shapes=[pltpu.SemaphoreType.DMA((n, N_CHIP)), pltpu.SemaphoreType.DMA((n, N_CHIP))],
    )(*grads)


def _chip_sum(grad, got, *, name, tr=256):
    _, rows, cols = grad.shape
    tr = min(tr, rows)

    def body(core_ref, a_ref, b_ref, o_ref):
        del core_ref
        o_ref[...] = (a_ref[...].astype(F32) + b_ref[...].astype(F32)).astype(BF16)

    return pl.pallas_call(
        body, name=name,
        grid_spec=pltpu.PrefetchScalarGridSpec(
            num_scalar_prefetch=1, grid=(N_CHIP, rows // tr),
            in_specs=[pl.BlockSpec((None, None, tr, cols), lambda ch, i, core: (ch, core[0], i, 0)),
                      pl.BlockSpec((None, tr, cols), lambda ch, i, core: (ch, i, 0))],
            out_specs=pl.BlockSpec((None, tr, cols), lambda ch, i, core: (ch, i, 0))),
        out_shape=jax.ShapeDtypeStruct((N_CHIP, rows, cols), BF16),
        compiler_params=_params(("parallel", "parallel")),
    )(lax.axis_index("c").astype(jnp.int32).reshape(1), grad.reshape(N_CHIP, 2, rows, cols), got)


def _chip_exchange(parts):
    n = len(parts)

    def body(*refs):
        ins, outs = refs[:n], refs[n:2 * n]
        send, recv, lsem = refs[2 * n:]
        x, y, c = lax.axis_index("x"), lax.axis_index("y"), lax.axis_index("c")
        mine = 2 * x + y
        chips = [(1 - x, y), (x, 1 - y), (1 - x, 1 - y)]
        local = [pltpu.make_async_copy(ins[w].at[mine], outs[w].at[mine], lsem.at[w]) for w in range(n)]
        for cp in local:
            cp.start()
        sends = []
        for w in range(n):
            for j, (px, py) in enumerate(chips):
                cp = pltpu.make_async_remote_copy(ins[w].at[2 * px + py], outs[w].at[mine],
                                                  send.at[w, j], recv.at[w, j],
                                                  device_id=(px, py, c), device_id_type=MESH)
                cp.start()
                sends.append(cp)
        for w in range(n):
            for j, (px, py) in enumerate(chips):
                pltpu.make_async_remote_copy(ins[w].at[mine], outs[w].at[2 * px + py],
                                             send.at[w, j], recv.at[w, j],
                                             device_id=(px, py, c), device_id_type=MESH).wait_recv()
        for cp in sends:
            cp.wait_send()
        for cp in local:
            cp.wait()

    hbm = pl.BlockSpec(memory_space=pl.ANY)
    return pl.pallas_call(
        body, name="chip_exchange_grads",
        in_specs=[hbm] * n, out_specs=[hbm] * n,
        out_shape=[jax.ShapeDtypeStruct(p.shape, p.dtype) for p in parts],
        scratch_shapes=[pltpu.SemaphoreType.DMA((n, 3)), pltpu.SemaphoreType.DMA((n, 3)),
                        pltpu.SemaphoreType.DMA((n,))],
    )(*parts)


def _adamw_math(g, w, m, v):
    m2 = B1 * m + (1.0 - B1) * g
    v2 = B2 * v + (1.0 - B2) * (g * g)
    delta = -LR * ((m2 / BC1) / (jnp.sqrt(v2 / BC2) + AEPS) + WD * w)
    return delta, m2, v2


def _adamw(parts, w, m, v, *, name, tr=128):
    rows, cols = w.shape

    def body(p_ref, w_ref, m_ref, v_ref, g_ref, d_ref, mo_ref, vo_ref):
        g = p_ref[0].astype(F32)
        for b in range(1, N_CHIP):
            g = g + p_ref[b].astype(F32)
        delta, m2, v2 = _adamw_math(g, w_ref[...], m_ref[...], v_ref[...])
        g_ref[...] = g
        d_ref[...] = delta
        mo_ref[...] = m2
        vo_ref[...] = v2

    spec = pl.BlockSpec((tr, cols), lambda i: (i, 0))
    shape = jax.ShapeDtypeStruct((rows, cols), F32)
    return pl.pallas_call(
        body, name=name, grid=(rows // tr,),
        in_specs=[pl.BlockSpec((N_CHIP, tr, cols), lambda i: (0, i, 0)), spec, spec, spec],
        out_specs=[spec] * 4, out_shape=[shape] * 4,
        compiler_params=_params(("parallel",)),
    )(parts, w, m, v)


def _small_update(part, w, m, v):
    rows = part.shape[0]

    def body(p_ref, w_ref, m_ref, v_ref, g_ref, d_ref, mo_ref, vo_ref, buf, send, recv):
        me, peers = _me_and_peers()
        buf[me] = p_ref[...]
        sends = []
        for k, (dev, _) in enumerate(peers):
            cp = pltpu.make_async_remote_copy(p_ref, buf.at[me], send.at[k], recv.at[k],
                                              device_id=dev, device_id_type=MESH)
            cp.start()
            sends.append(cp)
        for k, (dev, idx) in enumerate(peers):
            pltpu.make_async_remote_copy(p_ref, buf.at[idx], send.at[k], recv.at[k],
                                         device_id=dev, device_id_type=MESH).wait_recv()
        for cp in sends:
            cp.wait_send()
        g = buf[0]
        for b in range(1, N_DEV):
            g = g + buf[b]
        delta, m2, v2 = _adamw_math(g, w_ref[...], m_ref[...], v_ref[...])
        g_ref[...] = g
        d_ref[...] = delta
        mo_ref[...] = m2
        vo_ref[...] = v2

    vm = pl.BlockSpec(memory_space=pltpu.VMEM)
    shape = jax.ShapeDtypeStruct((rows, HD), F32)
    return pl.pallas_call(
        body, name="small_params_update",
        in_specs=[vm] * 4, out_specs=[vm] * 4, out_shape=[shape] * 4,
        scratch_shapes=[pltpu.VMEM((N_DEV, rows, HD), F32),
                        pltpu.SemaphoreType.DMA((N_DEV - 1,)), pltpu.SemaphoreType.DMA((N_DEV - 1,))],
    )(part, w, m, v)


def _pack_small(norm_mix, b_gate, qa, ka, qb, kb, rpb, norm_ffn):
    gains = jnp.concatenate([qa, ka, qb, kb, jnp.zeros((4, HD), F32)], axis=0)
    rpb_pad = jnp.pad(rpb.reshape(4 * (2 * WIN_R - 1), 2 * WIN_C - 1), ((0, 4), (0, HD - (2 * WIN_C - 1))))
    return jnp.concatenate([norm_mix.reshape(16, HD), b_gate.reshape(32, HD), gains, rpb_pad,
                            norm_ffn.reshape(16, HD)], axis=0)


def _unpack_small(p):
    norm_mix = p[0:16].reshape(1, D)
    b_gate = p[16:48].reshape(1, 2 * D)
    qa, ka, qb, kb = (p[48 + i:49 + i] for i in range(4))
    rpb = p[56:116, :2 * WIN_C - 1].reshape(1, 4, 2 * WIN_R - 1, 2 * WIN_C - 1)
    norm_ffn = p[120:136].reshape(1, D)
    return norm_mix, b_gate, qa, ka, qb, kb, rpb, norm_ffn


def kernel(x, norm_mix, w_in, b_gate, q_norm_a, k_norm_a, q_norm_b, k_norm_b, rpb_b, w_proj_a, w_proj_b, w_out, norm_ffn, w_up, w_down, loss_target, m_norm_mix, m_w_in, m_b_gate, m_q_norm_a, m_k_norm_a, m_q_norm_b, m_k_norm_b, m_rpb_b, m_w_proj_a, m_w_proj_b, m_w_out, m_norm_ffn, m_w_up, m_w_down, v_norm_mix, v_w_in, v_b_gate, v_q_norm_a, v_k_norm_a, v_q_norm_b, v_k_norm_b, v_rpb_b, v_w_proj_a, v_w_proj_b, v_w_out, v_norm_ffn, v_w_up, v_w_down):
    big_w = (w_in[0], w_proj_a[0], w_proj_b[0], w_out[0], w_up[0], w_down[0])
    big_m = (m_w_in[0], m_w_proj_a[0], m_w_proj_b[0], m_w_out[0], m_w_up[0], m_w_down[0])
    big_v = (v_w_in[0], v_w_proj_a[0], v_w_proj_b[0], v_w_out[0], v_w_up[0], v_w_down[0])
    names = ("w_in", "w_proj_a", "w_proj_b", "w_out", "w_up", "w_down")

    g_in, g_pa, g_pb, g_out, g_up, g_down = _all_gather([_cast_bf16(w) for w in big_w])
    small_w = _pack_small(norm_mix, b_gate, q_norm_a, k_norm_a, q_norm_b, k_norm_b, rpb_b, norm_ffn)
    small_m = _pack_small(m_norm_mix, m_b_gate, m_q_norm_a, m_k_norm_a, m_q_norm_b, m_k_norm_b, m_rpb_b, m_norm_ffn)
    small_v = _pack_small(v_norm_mix, v_b_gate, v_q_norm_a, v_k_norm_a, v_q_norm_b, v_k_norm_b, v_rpb_b, v_norm_ffn)

    loss, grad_x, big_g, small_g = _local_step(
        x[0], loss_target[0], norm_mix, b_gate, small_w[48:56], small_w[56:120], norm_ffn,
        g_in, g_pa, g_pb, g_out.reshape(D, D), g_up, g_down.reshape(D_FF, D))

    g_norm_mix, g_ba, g_bb, g_gains, g_rpb, g_norm_ffn = small_g
    small_part = jnp.concatenate([g_norm_mix.reshape(16, HD), g_ba.reshape(16, HD), g_bb.reshape(16, HD),
                                  g_gains, g_rpb, g_norm_ffn.reshape(16, HD)], axis=0)
    s_g, s_d, s_m, s_v = (_unpack_small(t) for t in _small_update(small_part, small_w, small_m, small_v))

    got = _pair_exchange(list(big_g))
    sums = [_chip_sum(big_g[i], got[i], name=f"chip_sum_{names[i]}") for i in range(6)]
    recv = _chip_exchange(sums)
    upd =[_adamw(recv[i], big_w[i], big_m[i], big_v[i], name=f"adamw_{names[i]}") for i in range(6)]
    b_g, b_d, b_m, b_v = ([u[j][None] for u in upd] for j in range(4))

    def order(small, big):
        nm, bg, qa, ka, qb, kb, rpb, nf = small
        w_in_, pa_, pb_, out_, up_, down_ = big
        return (nm, w_in_, bg, qa, ka, qb, kb, rpb, pa_, pb_, out_, nf, up_, down_)

    total = lax.psum(loss[0, 0], ("x", "y", "c"))
    return (total, grad_x[None], *order(s_g, b_g), *order(s_d, b_d), *order(s_m, b_m), *order(s_v, b_v))
```

```python
import functools

import jax
import jax.numpy as jnp
import numpy as np
from jax import lax
from jax.experimental import pallas as pl
from jax.experimental.pallas import tpu as pltpu
from jax.experimental.pallas import tpu_sc as plsc

F32 = jnp.float32
BF16 = jnp.bfloat16

N_DEV = 8
S = 2048
D = 2048
HD = 128
NH = 16
NH_A = 12
QKV = NH * HD
D_IN = 3 * QKV + 2 * D
D_BR = 512
D_FF = 4 * D
GRID_W = 64
ROWS = S // GRID_W
WIN_R = 8
WIN_C = 16
EPS = 1e-6
NEG = -1e30
SCALE = HD ** -0.5
ROPE_THETA = 10000.0
GROUPS_A = ((64, 1, 512), (256, 4, 768), (1024, 16, 2048))
QB = 256

LR, B1, B2, AEPS, WD, STEP = 0.001, 0.9, 0.999, 1e-08, 0.01, 10
BC1 = 1.0 - B1 ** STEP
BC2 = 1.0 - B2 ** STEP

VMEM_LIMIT = 56 * 1024 * 1024
MESH = pl.DeviceIdType.MESH

NN = (((1,), (0,)), ((), ()))
NT = (((1,), (1,)), ((), ()))
TN = (((0,), (0,)), ((), ()))


def _params(sem):
    return pltpu.CompilerParams(dimension_semantics=sem, vmem_limit_bytes=VMEM_LIMIT)


def _matmul(a, b, *, dims, grid, a_spec, b_spec, nk, epi, out_shape, out_specs, name,
            extra=(), extra_specs=(), acc_shape=None):
    n_extra = len(extra)

    def body(a_ref, b_ref, *rest):
        ex = rest[:n_extra]
        if nk == 1:
            outs = rest[n_extra:]
            epi(lax.dot_general(a_ref[...], b_ref[...], dims, preferred_element_type=F32), ex, outs)
            return
        outs, acc = rest[n_extra:-1], rest[-1]
        k = pl.program_id(2)
        part = lax.dot_general(a_ref[...], b_ref[...], dims, preferred_element_type=F32)

        @pl.when(k == 0)
        def _():
            acc[...] = part

        @pl.when(k > 0)
        def _():
            acc[...] += part

        @pl.when(k == nk - 1)
        def _():
            epi(acc[...], ex, outs)

    scratch = [] if nk == 1 else [pltpu.VMEM(acc_shape, F32)]
    return pl.pallas_call(
        body, name=name, grid=grid,
        in_specs=[a_spec, b_spec, *extra_specs],
        out_specs=out_specs, out_shape=out_shape, scratch_shapes=scratch,
        compiler_params=_params(("parallel", "parallel", "arbitrary")),
    )(a, b, *extra)


def _epi_store(acc, ex, outs):
    outs[0][...] = acc.astype(outs[0].dtype)


def _epi_residual(acc, ex, outs):
    outs[0][...] = acc + ex[0][...]


def _mm_nn(a, b3, *, tm, tn, tk, name, out_dtype=F32, epi=_epi_store, extra=(), n_out=1,
           out_dtypes=None):
    m, kdim = a.shape
    g, _, ng = b3.shape
    n = g * ng
    npg = ng // tn
    nk = kdim // tk
    grid = (n // tn, m // tm, nk)
    tile = pl.BlockSpec((tm, tn), lambda j, i, k: (i, j))
    dts = out_dtypes or (out_dtype,) * n_out
    shapes = tuple(jax.ShapeDtypeStruct((m, n), dt) for dt in dts)
    return _matmul(
        a, b3, dims=NN, grid=grid, nk=nk, epi=epi, name=name,
        a_spec=pl.BlockSpec((tm, tk), lambda j, i, k: (i, k)),
        b_spec=pl.BlockSpec((None, tk, tn), lambda j, i, k: (j // npg, k, j % npg)),
        extra=extra, extra_specs=[tile] * len(extra),
        out_shape=shapes if len(dts) > 1 else shapes[0],
        out_specs=[tile] * len(dts) if len(dts) > 1 else tile,
        acc_shape=(tm, tn))


def _mm_nt(a, b3, *, tm, tn, tk, name, out_dtype=F32, epi=_epi_store, extra=()):
    m, kdim = a.shape
    g, n, kg = b3.shape
    kpg = kg // tk
    nk = kdim // tk
    grid = (n // tn, m // tm, nk)
    tile = pl.BlockSpec((tm, tn), lambda j, i, k: (i, j))
    return _matmul(
        a, b3, dims=NT, grid=grid, nk=nk, epi=epi, name=name,
        a_spec=pl.BlockSpec((tm, tk), lambda j, i, k: (i, k)),
        b_spec=pl.BlockSpec((None, tn, tk), lambda j, i, k: (k // kpg, j, k % kpg)),
        extra=extra, extra_specs=[tile] * len(extra),
        out_shape=jax.ShapeDtypeStruct((m, n), out_dtype), out_specs=tile,
        acc_shape=(tm, tn))


def _mm_tn(a, b, *, tm, tn, name, groups=1, out_dtype=BF16):
    t, m = a.shape
    _, n = b.shape
    ng = n // groups
    npg = ng // tn
    grid = (n // tn, m // tm, 1)
    return _matmul(
        a, b, dims=TN, grid=grid, nk=1, epi=_epi_store, name=name,
        a_spec=pl.BlockSpec((t, tm), lambda j, i, k: (0, i)),
        b_spec=pl.BlockSpec((t, tn), lambda j, i, k: (0, j)),
        out_shape=jax.ShapeDtypeStruct((groups, m, ng), out_dtype),
        out_specs=pl.BlockSpec((None, tm, tn), lambda j, i, k: (j // npg, i, j % npg)))


def _rms_fwd(x, g, *, name, tr=256):
    def body(x_ref, g_ref, y_ref, r_ref):
        xv = x_ref[...]
        r = lax.rsqrt(jnp.mean(xv * xv, axis=-1, keepdims=True) + EPS)
        y_ref[...] = (xv * r * g_ref[...]).astype(BF16)
        r_ref[...] = r

    row = pl.BlockSpec((tr, D), lambda i: (i, 0))
    return pl.pallas_call(
        body, name=name, grid=(S // tr,),
        in_specs=[row, pl.BlockSpec((1, D), lambda i: (0, 0))],
        out_specs=[row, pl.BlockSpec((tr, 1), lambda i: (i, 0))],
        out_shape=[jax.ShapeDtypeStruct((S, D), BF16), jax.ShapeDtypeStruct((S, 1), F32)],
        compiler_params=_params(("parallel",)),
    )(x, g)


def _rms_bwd(dy, x, rstd, g, resid, *, name, tr=256):
    def body(dy_ref, x_ref, r_ref, g_ref, res_ref, dx_ref, dxb_ref, dg_ref):
        r = r_ref[...]
        xh = x_ref[...] * r
        dyv = dy_ref[...]
        t = dyv * g_ref[...]
        dx = r * (t - xh * jnp.mean(t * xh, axis=-1, keepdims=True)) + res_ref[...]
        dx_ref[...] = dx
        dxb_ref[...] = dx.astype(BF16)
        part = jnp.sum(dyv * xh, axis=0, keepdims=True)

        @pl.when(pl.program_id(0) == 0)
        def _():
            dg_ref[...] = part

        @pl.when(pl.program_id(0) > 0)
        def _():
            dg_ref[...] += part

    row = pl.BlockSpec((tr, D), lambda i: (i, 0))
    vec = pl.BlockSpec((1, D), lambda i: (0, 0))
    return pl.pallas_call(
        body, name=name, grid=(S // tr,),
        in_specs=[row, row, pl.BlockSpec((tr, 1), lambda i: (i, 0)), vec, row],
        out_specs=[row, row, vec],
        out_shape=[jax.ShapeDtypeStruct((S, D), F32), jax.ShapeDtypeStruct((S, D), BF16),
                   jax.ShapeDtypeStruct((1, D), F32)],
        compiler_params=_params(("arbitrary",)),
    )(dy, x, rstd, g, resid)


def _loss_head(h2, target, *, tr=256):
    def body(h_ref, t_ref, dy_ref, dyb_ref, loss_ref):
        e = h_ref[...] - t_ref[...]
        dy = e * (1.0 / D)
        dy_ref[...] = dy
        dyb_ref[...] = dy.astype(BF16)
        part = (0.5 / D) * jnp.sum(jnp.sum(e * e, axis=-1, keepdims=True), axis=0, keepdims=True)

        @pl.when(pl.program_id(0) == 0)
        def _():
            loss_ref[...] = part

        @pl.when(pl.program_id(0) > 0)
        def _():
            loss_ref[...] += part

    row = pl.BlockSpec((tr, D), lambda i: (i, 0))
    return pl.pallas_call(
        body, name="loss_head", grid=(S // tr,),
        in_specs=[row, row],
        out_specs=[row, row, pl.BlockSpec((1, 1), lambda i: (0, 0))],
        out_shape=[jax.ShapeDtypeStruct((S, D), F32), jax.ShapeDtypeStruct((S, D), BF16),
                   jax.ShapeDtypeStruct((1, 1), F32)],
        compiler_params=_params(("arbitrary",)),
    )(h2, target)


def _rope_tables():
    pos = np.arange(S, dtype=np.float32)
    inv = (ROPE_THETA ** (-np.arange(0, HD, 2, dtype=np.float32) / HD)).astype(np.float32)
    ang = pos[:, None] * inv[None, :]
    cos, sin = np.cos(ang), np.sin(ang)
    return (jnp.asarray(np.concatenate([cos, cos], axis=-1), F32),
            jnp.asarray(np.concatenate([-sin, sin], axis=-1), F32))


def _swap_halves(t):
    return pltpu.roll(t, HD // 2, axis=1)


def _qk_prep(proj, gains, cos2, sin2, *, tr=256):
    def body(q_ref, k_ref, v_ref, g_ref, c_ref, s_ref, qn_ref, kn_ref, vb_ref):
        cos, sin = c_ref[...], s_ref[...]
        for src, dst, row_a, row_b in ((q_ref, qn_ref, 0, 2), (k_ref, kn_ref, 1, 3)):
            for h in range(NH):
                cols = slice(h * HD, (h + 1) * HD)
                t = src[:, cols]
                r = lax.rsqrt(jnp.mean(t * t, axis=-1, keepdims=True) + EPS)
                if h < NH_A:
                    y = t * r * g_ref[row_a:row_a + 1, :]
                    y = y * cos + _swap_halves(y) * sin
                else:
                    y = t * r * g_ref[row_b:row_b + 1, :]
                dst[:, cols] = y.astype(BF16)
        vb_ref[...] = v_ref[...].astype(BF16)

    def blk(c):
        return pl.BlockSpec((tr, QKV), lambda i: (i, c))
    tab = pl.BlockSpec((tr, HD), lambda i: (i, 0))
    out = pl.BlockSpec((tr, QKV), lambda i: (i, 0))
    return pl.pallas_call(
        body, name="qk_prep", grid=(S // tr,),
        in_specs=[blk(0), blk(1), blk(2), pl.BlockSpec((8, HD), lambda i: (0, 0)), tab, tab],
        out_specs=[out, out, out],
        out_shape=[jax.ShapeDtypeStruct((S, QKV), BF16)] * 3,
        compiler_params=_params(("parallel",)),
    )(proj, proj, proj, gains, cos2, sin2)


def _qk_prep_bwd(dproj, proj, gains, cos2, sin2, dq_parts, dk_parts, dv_parts, *, tr=256):
    def body(dp_in, q_ref, k_ref, g_ref, c_ref, s_ref, *rest):
        dqs, dks, dvs = rest[0:4], rest[4:8], rest[8:12]
        dp_out, dg_ref = rest[12:14]
        del dp_in
        cos, sin = c_ref[...], s_ref[...]
        dg_rows = []
        for src, grads, base, row_a, row_b in ((q_ref, dqs, 0, 0, 2), (k_ref, dks, QKV, 1, 3)):
            dg_a = jnp.zeros((1, HD), F32)
            dg_b = jnp.zeros((1, HD), F32)
            for h in range(NH):
                cols = slice(h * HD, (h + 1) * HD)
                t = src[:, cols]
                dy = grads[h // 4][:, (h % 4) * HD:(h % 4 + 1) * HD]
                r = lax.rsqrt(jnp.mean(t * t, axis=-1, keepdims=True) + EPS)
                xh = t * r
                if h < NH_A:
                    dy = dy * cos - _swap_halves(dy) * sin
                    gain = g_ref[row_a:row_a + 1, :]
                    dg_a = dg_a + jnp.sum(dy * xh, axis=0, keepdims=True)
                else:
                    gain = g_ref[row_b:row_b + 1, :]
                    dg_b = dg_b + jnp.sum(dy * xh, axis=0, keepdims=True)
                u = dy * gain
                dx = r * (u - xh * jnp.mean(u * xh, axis=-1, keepdims=True))
                dp_out[:, base + h * HD:base + (h + 1) * HD] = dx.astype(BF16)
            dg_rows += [(row_a, dg_a), (row_b, dg_b)]
        for g4 in range(4):
            dp_out[:, 2 * QKV + g4 * D_BR:2 * QKV + (g4 + 1) * D_BR] = dvs[g4][...].astype(BF16)

        first = pl.program_id(0) == 0

        @pl.when(first)
        def _():
            dg_ref[...] = jnp.zeros((8, HD), F32)

        for row, val in dg_rows:
            dg_ref[row:row + 1, :] += val

    def blk(c):
        return pl.BlockSpec((tr, QKV), lambda i: (i, c))
    tab = pl.BlockSpec((tr, HD), lambda i: (i, 0))
    part = pl.BlockSpec((tr, D_BR), lambda i: (i, 0))
    gain_spec = pl.BlockSpec((8, HD), lambda i: (0, 0))
    return pl.pallas_call(
        body, name="qk_prep_bwd", grid=(S // tr,),
        in_specs=[pl.BlockSpec(memory_space=pl.ANY), blk(0), blk(1), gain_spec, tab, tab] + [part] * 12,
        out_specs=[pl.BlockSpec((tr, 3 * QKV), lambda i: (i, 0)), gain_spec],
        out_shape=[jax.ShapeDtypeStruct((S, D_IN), BF16), jax.ShapeDtypeStruct((8, HD), F32)],
        input_output_aliases={0: 0},
        compiler_params=_params(("arbitrary",)),
    )(dproj, proj, proj, gains, cos2, sin2, *dq_parts, *dk_parts, *dv_parts)


def _gate_fwd(proj, b_gate, ya, yb, *, tr=256):
    def body(la_ref, lb_ref, ba_ref, bb_ref, ya_ref, yb_ref, o_ref):
        ga = jax.nn.sigmoid(la_ref[...] + ba_ref[...])
        gb = jax.nn.sigmoid(lb_ref[...] + bb_ref[...])
        o_ref[...] = (ga * ya_ref[...] + gb * yb_ref[...]).astype(BF16)

    row = pl.BlockSpec((tr, D), lambda i: (i, 0))
    return pl.pallas_call(
        body, name="gate_fwd", grid=(S // tr,),
        in_specs=[pl.BlockSpec((tr, D), lambda i: (i, 3)), pl.BlockSpec((tr, D), lambda i: (i, 4)),
                  pl.BlockSpec((1, D), lambda i: (0, 0)), pl.BlockSpec((1, D), lambda i: (0, 1)),
                  row, row],
        out_specs=row, out_shape=jax.ShapeDtypeStruct((S, D), BF16),
        compiler_params=_params(("parallel",)),
    )(proj, proj, b_gate, b_gate, ya, yb)


def _gate_bwd(branch, dmixed, proj, b_gate, y, dproj, *, tr=256):
    aliased = dproj is not None

    def body(dm_ref, l_ref, b_ref, y_ref, *rest):
        dy_ref, dp_ref, db_ref = rest[-3:]
        g = jax.nn.sigmoid(l_ref[...] + b_ref[...])
        dm = dm_ref[...]
        dy_ref[...] = (dm * g).astype(BF16)
        dl = dm * y_ref[...] * g * (1.0 - g)
        dp_ref[...] = dl.astype(BF16)
        part = jnp.sum(dl, axis=0, keepdims=True)

        @pl.when(pl.program_id(0) == 0)
        def _():
            db_ref[...] = part

        @pl.when(pl.program_id(0) > 0)
        def _():
            db_ref[...] += part

    row = pl.BlockSpec((tr, D), lambda i: (i, 0))
    col = pl.BlockSpec((tr, D), lambda i: (i, 3 + branch))
    vec = pl.BlockSpec((1, D), lambda i: (0, 0))
    return pl.pallas_call(
        body, name=f"gate_bwd_{branch}", grid=(S // tr,),
        in_specs=[row, col, pl.BlockSpec((1, D), lambda i: (0, branch)), row]
        + ([pl.BlockSpec(memory_space=pl.ANY)] if aliased else []),
        out_specs=[row, col, vec],
        out_shape=[jax.ShapeDtypeStruct((S, D), BF16), jax.ShapeDtypeStruct((S, D_IN), BF16),
                   jax.ShapeDtypeStruct((1, D), F32)],
        input_output_aliases={4: 1} if aliased else {},
        compiler_params=_params(("arbitrary",)),
    )(dmixed, proj, b_gate, y, *([dproj] if aliased else []))


def _window_start(t0, wk):
    if wk == S:
        return 0
    return pl.multiple_of(jnp.clip(t0 - (wk - QB) // 2, 0, S - wk), 128)


def _scores_a(q, kw, t0, start, hs, dil, wk):
    s = lax.dot_general(q, kw, NT, preferred_element_type=F32) * SCALE
    qpos = t0 + lax.broadcasted_iota(jnp.int32, (QB, 1), 0)
    kpos = start + lax.broadcasted_iota(jnp.int32, (1, wk), 1)
    diff = kpos - qpos
    keep = (jnp.abs(diff) <= hs) & ((diff & (dil - 1)) == 0)
    return jnp.where(keep, s, NEG)


def _attn_a_fwd(qn, kn, vb, gi):
    hs, dil, wk = GROUPS_A[gi]

    def body(q_ref, k_ref, v_ref, o_ref, lse_ref):
        t0 = pl.program_id(1) * QB
        start = _window_start(t0, wk)
        s = _scores_a(q_ref[...], k_ref[pl.ds(start, wk), :], t0, start, hs, dil, wk)
        m = jnp.max(s, axis=-1, keepdims=True)
        p = jnp.exp(s - m)
        l = jnp.sum(p, axis=-1, keepdims=True)
        o = lax.dot_general(p.astype(BF16), v_ref[pl.ds(start, wk), :], NN, preferred_element_type=F32)
        o_ref[...] = o / l
        lse_ref[...] = m + jnp.log(l)

    full = pl.BlockSpec((S, HD), lambda h, i: (0, 4 * gi + h))
    return pl.pallas_call(
        body, name=f"attn_a_fwd_{gi}", grid=(4, S // QB),
        in_specs=[pl.BlockSpec((QB, HD), lambda h, i: (i, 4 * gi + h)), full, full],
        out_specs=[pl.BlockSpec((QB, HD), lambda h, i: (i, h)),
                   pl.BlockSpec((None, QB, 1), lambda h, i: (h, i, 0))],
        out_shape=[jax.ShapeDtypeStruct((S, D_BR), F32), jax.ShapeDtypeStruct((4, S, 1), F32)],
        compiler_params=_params(("parallel", "parallel")),
    )(qn, kn, vb)


def _combine_a(os, lses, *, tr=256):
    def body(o0, o1, o2, l0, l1, l2, oa_ref, lse_ref):
        for h in range(4):
            cols = slice(h * HD, (h + 1) * HD)
            a, b, c = l0[h], l1[h], l2[h]
            m = jnp.maximum(jnp.maximum(a, b), c)
            wa, wb, wc = jnp.exp(a - m), jnp.exp(b - m), jnp.exp(c - m)
            tot = wa + wb + wc
            oa_ref[:, cols] = ((wa * o0[:, cols] + wb * o1[:, cols] + wc * o2[:, cols]) / tot).astype(BF16)
            lse_ref[h] = m + jnp.log(tot)

    row = pl.BlockSpec((tr, D_BR), lambda i: (i, 0))
    stat = pl.BlockSpec((4, tr, 1), lambda i: (0, i, 0))
    return pl.pallas_call(
        body, name="combine_a", grid=(S // tr,),
        in_specs=[row] * 3 + [stat] * 3, out_specs=[row, stat],
        out_shape=[jax.ShapeDtypeStruct((S, D_BR), BF16), jax.ShapeDtypeStruct((4, S, 1), F32)],
        compiler_params=_params(("parallel",)),
    )(*os, *lses)


def _attn_a_bwd(qn, kn, vb, oa, doa, lse, gi):
    hs, dil, wk = GROUPS_A[gi]

    def body(q_ref, k_ref, v_ref, o_ref, do_ref, lse_ref, dq_ref, dk_ref, dv_ref):
        @pl.when(pl.program_id(1) == 0)
        def _():
            dk_ref[...] = jnp.zeros((S, HD), F32)
            dv_ref[...] = jnp.zeros((S, HD), F32)

        t0 = pl.program_id(1) * QB
        start = _window_start(t0, wk)
        q = q_ref[...]
        kw = k_ref[pl.ds(start, wk), :]
        vw = v_ref[pl.ds(start, wk), :]
        p = jnp.exp(_scores_a(q, kw, t0, start, hs, dil, wk) - lse_ref[...])
        do = do_ref[...]
        dob = do.astype(BF16)
        dsum = jnp.sum(do * o_ref[...].astype(F32), axis=-1, keepdims=True)
        dp = lax.dot_general(dob, vw, NT, preferred_element_type=F32)
        ds = (p * (dp - dsum) * SCALE).astype(BF16)
        dq_ref[...] = lax.dot_general(ds, kw, NN, preferred_element_type=F32)
        dk_ref[pl.ds(start, wk), :] += lax.dot_general(ds, q, TN, preferred_element_type=F32)
        dv_ref[pl.ds(start, wk), :] += lax.dot_general(p.astype(BF16), dob, TN, preferred_element_type=F32)

    full = pl.BlockSpec((S, HD), lambda h, i: (0, 4 * gi + h))
    blk = pl.BlockSpec((QB, HD), lambda h, i: (i, h))
    acc = pl.BlockSpec((S, HD), lambda h, i: (0, h))
    shape = jax.ShapeDtypeStruct((S, D_BR), F32)
    return pl.pallas_call(
        body, name=f"attn_a_bwd_{gi}", grid=(4, S // QB),
        in_specs=[pl.BlockSpec((QB, HD), lambda h, i: (i, 4 * gi + h)), full, full, blk, blk,
                  pl.BlockSpec((None, QB, 1), lambda h, i: (h, i, 0))],
        out_specs=[blk, acc, acc], out_shape=[shape, shape, shape],
        compiler_params=_params(("parallel", "arbitrary")),
    )(qn, kn, vb, oa, doa, lse)


KEYS_B = WIN_R * GRID_W
N_OFF = WIN_R


def _bias_constants():
    q = np.arange(GRID_W)[:, None]
    kc = np.arange(GRID_W)[None, :]
    dc = np.clip(kc - q, -(WIN_C - 1), WIN_C - 1) + (WIN_C - 1)
    expand = np.zeros((HD, GRID_W * GRID_W), np.float32)
    expand[dc.reshape(-1), np.arange(GRID_W * GRID_W)] = 1.0
    cs = np.clip(q - WIN_C // 2, 0, GRID_W - WIN_C)
    keep = ((kc >= cs) & (kc < cs + WIN_C)).reshape(1, -1).astype(np.float32)
    sel = np.zeros((64, 4 * N_OFF * WIN_R), np.float32)
    for h in range(4):
        for off in range(N_OFF):
            for j in range(WIN_R):
                sel[h * (2 * WIN_R - 1) + off + j, (h * N_OFF + off) * WIN_R + j] = 1.0
    return jnp.asarray(expand), jnp.asarray(keep), jnp.asarray(sel)


def _bias_expand(rpb_pad, expand, keep, sel):
    def body(r_ref, e_ref, k_ref, s_ref, o_ref):
        t = lax.dot_general(r_ref[...], e_ref[...], NN, precision=lax.Precision.HIGHEST,
                            preferred_element_type=F32)
        rows = lax.dot_general(s_ref[...], t, TN, precision=lax.Precision.HIGHEST,
                               preferred_element_type=F32)
        o_ref[...] = jnp.where(k_ref[...] > 0.5, rows, NEG)

    return pl.pallas_call(
        body, name="bias_expand",
        out_shape=jax.ShapeDtypeStruct((4 * N_OFF * WIN_R, GRID_W * GRID_W), F32),
        compiler_params=pltpu.CompilerParams(vmem_limit_bytes=VMEM_LIMIT),
    )(rpb_pad, expand, keep, sel)


def _bias_reduce(dbias_rows, expand, sel):
    def body(x_ref, e_ref, s_ref, o_ref):
        z = lax.dot_general(x_ref[...], e_ref[...], NT, precision=lax.Precision.HIGHEST,
                            preferred_element_type=F32)
        o_ref[...] = lax.dot_general(s_ref[...], z, NN, precision=lax.Precision.HIGHEST,
                                     preferred_element_type=F32)

    return pl.pallas_call(
        body, name="bias_reduce", out_shape=jax.ShapeDtypeStruct((64, HD), F32),
        compiler_params=pltpu.CompilerParams(vmem_limit_bytes=VMEM_LIMIT),
    )(dbias_rows, expand, sel)


def _rows_to_tab(rows):
    t = rows.reshape(4, N_OFF, WIN_R, GRID_W, GRID_W)
    return t.transpose(0, 1, 3, 2, 4).reshape(4, N_OFF, GRID_W, KEYS_B)


def _tab_to_rows(tab):
    t = tab.reshape(4, N_OFF, GRID_W, WIN_R, GRID_W)
    return t.transpose(0, 1, 3, 2, 4).reshape(4 * N_OFF * WIN_R, GRID_W * GRID_W)


def _row_window(r):
    r0 = jnp.clip(r - WIN_R // 2, 0, ROWS - WIN_R)
    off = r0 + (WIN_R - 1) - r
    return pl.multiple_of(r * GRID_W, GRID_W), pl.multiple_of(r0 * GRID_W, GRID_W), off


def _attn_b_fwd(qn, kn, vb, bias_tab):
    def body(q_ref, k_ref, v_ref, b_ref, o_ref, lse_ref):
        def row(r, carry):
            qs, ks, off = _row_window(r)
            q = q_ref[pl.ds(qs, GRID_W), :]
            s = lax.dot_general(q, k_ref[pl.ds(ks, KEYS_B), :], NT, preferred_element_type=F32) * SCALE
            s = s + b_ref[off]
            m = jnp.max(s, axis=-1, keepdims=True)
            p = jnp.exp(s - m)
            l = jnp.sum(p, axis=-1, keepdims=True)
            o = lax.dot_general(p.astype(BF16), v_ref[pl.ds(ks, KEYS_B), :], NN, preferred_element_type=F32)
            o_ref[pl.ds(qs, GRID_W), :] = (o / l).astype(BF16)
            lse_ref[pl.ds(qs, GRID_W), :] = m + jnp.log(l)
            return carry

        lax.fori_loop(0, ROWS, row, 0)

    full = pl.BlockSpec((S, HD), lambda h: (0, NH_A + h))
    return pl.pallas_call(
        body, name="attn_b_fwd", grid=(4,),
        in_specs=[full, full, full, pl.BlockSpec((None, N_OFF, GRID_W, KEYS_B), lambda h: (h, 0, 0, 0))],
        out_specs=[pl.BlockSpec((S, HD), lambda h: (0, h)), pl.BlockSpec((None, S, 1), lambda h: (h, 0, 0))],
        out_shape=[jax.ShapeDtypeStruct((S, D_BR), BF16), jax.ShapeDtypeStruct((4, S, 1), F32)],
        compiler_params=_params(("parallel",)),
    )(qn, kn, vb, bias_tab)


def _attn_b_bwd(qn, kn, vb, bias_tab, ob, dob, lse):
    def body(q_ref, k_ref, v_ref, b_ref, o_ref, do_ref, lse_ref, dq_ref, dk_ref, dv_ref, db_ref):
        dk_ref[...] = jnp.zeros((S, HD), F32)
        dv_ref[...] = jnp.zeros((S, HD), F32)
        db_ref[...] = jnp.zeros((N_OFF, GRID_W, KEYS_B), F32)

        def row(r, carry):
            qs, ks, off = _row_window(r)
            rows = pl.ds(qs, GRID_W)
            keys = pl.ds(ks, KEYS_B)
            q = q_ref[rows, :]
            kw = k_ref[keys, :]
            s = lax.dot_general(q, kw, NT, preferred_element_type=F32) * SCALE + b_ref[off]
            p = jnp.exp(s - lse_ref[rows, :])
            do = do_ref[rows, :]
            dobf = do.astype(BF16)
            dsum = jnp.sum(do * o_ref[rows, :].astype(F32), axis=-1, keepdims=True)
            dp = lax.dot_general(dobf, v_ref[keys, :], NT, preferred_element_type=F32)
            ds = p * (dp - dsum)
            db_ref[off] += ds
            dsb = (ds * SCALE).astype(BF16)
            dq_ref[rows, :] = lax.dot_general(dsb, kw, NN, preferred_element_type=F32)
            dk_ref[keys, :] += lax.dot_general(dsb, q, TN, preferred_element_type=F32)
            dv_ref[keys, :] += lax.dot_general(p.astype(BF16), dobf, TN, preferred_element_type=F32)
            return carry

        lax.fori_loop(0, ROWS, row, 0)

    full = pl.BlockSpec((S, HD), lambda h: (0, NH_A + h))
    slot = pl.BlockSpec((S, HD), lambda h: (0, h))
    tab = pl.BlockSpec((None, N_OFF, GRID_W, KEYS_B), lambda h: (h, 0, 0, 0))
    shape = jax.ShapeDtypeStruct((S, D_BR), F32)
    return pl.pallas_call(
        body, name="attn_b_bwd", grid=(4,),
        in_specs=[full, full, full, tab, slot, slot, pl.BlockSpec((None, S, 1), lambda h: (h, 0, 0))],
        out_specs=[slot, slot, slot, tab],
        out_shape=[shape, shape, shape, jax.ShapeDtypeStruct((4, N_OFF, GRID_W, KEYS_B), F32)],
        compiler_params=_params(("parallel",)),
    )(qn, kn, vb, bias_tab, ob, dob, lse)


def _epi_relu_sq(acc, ex, outs):
    u = jnp.maximum(acc, 0.0)
    outs[0][...] = u.astype(BF16)
    outs[1][...] = (u * u).astype(BF16)


def _epi_relu_sq_bwd(acc, ex, outs):
    outs[0][...] = (acc * (2.0 * ex[0][...].astype(F32))).astype(BF16)


def _local_step(x, target, norm_mix, b_gate, gains, rpb_pad, norm_ffn,
                w_in, w_pa, w_pb, w_out, w_up, w_down):
    cos2, sin2 = _rope_tables()
    expand, keep, sel = _bias_constants()
    w_out3, w_down3 = w_out[None], w_down[None]

    xn, rstd1 = _rms_fwd(x, norm_mix, name="rms_mix")
    proj = _mm_nn(xn, w_in, tm=512, tn=1280, tk=D, name="proj")
    qn, kn, vb = _qk_prep(proj, gains, cos2, sin2)
    fwd_a = [_attn_a_fwd(qn, kn, vb, gi) for gi in range(3)]
    oa, lse_a = _combine_a([o for o, _ in fwd_a], [l for _, l in fwd_a])
    bias_tab = _rows_to_tab(_bias_expand(rpb_pad, expand, keep, sel))
    ob, lse_b = _attn_b_fwd(qn, kn, vb, bias_tab)
    ya = _mm_nn(oa, w_pa, tm=1024, tn=256, tk=D_BR, name="proj_a")
    yb = _mm_nn(ob, w_pb, tm=1024, tn=256, tk=D_BR, name="proj_b")
    mixed = _gate_fwd(proj, b_gate, ya, yb)
    h1 = _mm_nn(mixed, w_out3, tm=512, tn=512, tk=D, name="out_proj", epi=_epi_residual, extra=(x,))
    hn, rstd2 = _rms_fwd(h1, norm_ffn, name="rms_ffn")
    u, usq = _mm_nn(hn, w_up, tm=512, tn=1024, tk=D, name="ffn_up", epi=_epi_relu_sq,
                    out_dtypes=(BF16, BF16))
    h2 = _mm_nn(usq, w_down3, tm=512, tn=1024, tk=D, name="ffn_down", epi=_epi_residual, extra=(h1,))
    dy, dyb, loss = _loss_head(h2, target)

    g_down = _mm_tn(usq, dyb, tm=512, tn=1024, name="grad_w_down")
    du = _mm_nt(dyb, w_down3, tm=512, tn=1024, tk=D, name="ffn_down_bwd", out_dtype=BF16,
                epi=_epi_relu_sq_bwd, extra=(u,))
    g_up = _mm_tn(hn, du, tm=512, tn=1024, groups=N_DEV, name="grad_w_up")
    dhn = _mm_nt(du, w_up, tm=512, tn=1024, tk=1024, name="ffn_up_bwd")
    dh1, dh1b, g_norm_ffn = _rms_bwd(dhn, h1, rstd2, norm_ffn, dy, name="rms_ffn_bwd")

    g_out = _mm_tn(mixed, dh1b, tm=512, tn=1024, name="grad_w_out")
    dmixed = _mm_nt(dh1b, w_out3, tm=512, tn=512, tk=D, name="out_proj_bwd")
    dya, dproj, g_ba = _gate_bwd(0, dmixed, proj, b_gate, ya, None)
    dyb2, dproj, g_bb = _gate_bwd(1, dmixed, proj, b_gate, yb, dproj)
    g_pa = _mm_tn(oa, dya, tm=512, tn=256, groups=N_DEV, name="grad_w_proj_a")
    g_pb = _mm_tn(ob, dyb2, tm=512, tn=256, groups=N_DEV, name="grad_w_proj_b")
    doa = _mm_nt(dya, w_pa, tm=1024, tn=512, tk=256, name="proj_a_bwd")
    dob = _mm_nt(dyb2, w_pb, tm=1024, tn=512, tk=256, name="proj_b_bwd")
    bwd = [_attn_a_bwd(qn, kn, vb, oa, doa, lse_a, gi) for gi in range(3)]
    dqb, dkb, dvb, dbias = _attn_b_bwd(qn, kn, vb, bias_tab, ob, dob, lse_b)
    g_rpb = _bias_reduce(_tab_to_rows(dbias), expand, sel)
    dproj, g_gains = _qk_prep_bwd(dproj, proj, gains, cos2, sin2,
                                  [b[0] for b in bwd] + [dqb], [b[1] for b in bwd] + [dkb],
                                  [b[2] for b in bwd] + [dvb])
    g_in = _mm_tn(xn, dproj, tm=512, tn=1280, groups=N_DEV, name="grad_w_in")
    dxn = _mm_nt(dproj, w_in, tm=512, tn=1024, tk=1280, name="proj_bwd")
    grad_x, _, g_norm_mix = _rms_bwd(dxn, x, rstd1, norm_mix, dh1, name="rms_mix_bwd")

    big = (g_in, g_pa, g_pb, g_out.reshape(N_DEV, D // N_DEV, D), g_up, g_down.reshape(N_DEV, D_FF // N_DEV, D))
    small = (g_norm_mix, g_ba, g_bb, g_gains, g_rpb, g_norm_ffn)
    return loss, grad_x, big, small


def _cast_bf16(w, *, tr=256):
    rows, cols = w.shape
    tr = min(tr, rows)

    def body(w_ref, o_ref):
        o_ref[...] = w_ref[...].astype(BF16)

    spec = pl.BlockSpec((tr, cols), lambda i: (i, 0))
    return pl.pallas_call(
        body, name=f"cast_{rows}x{cols}", grid=(rows // tr,), in_specs=[spec], out_specs=spec,
        out_shape=jax.ShapeDtypeStruct((rows, cols), BF16), compiler_params=_params(("parallel",)),
    )(w)


def _me_and_peers():
    x, y, c = lax.axis_index("x"), lax.axis_index("y"), lax.axis_index("c")
    me = 4 * x + 2 * y + c
    peers = []
    for k in range(1, N_DEV):
        px = 1 - x if k & 4 else x
        py = 1 - y if k & 2 else y
        pc = 1 - c if k & 1 else c
        peers.append(((px, py, pc), 4 * px + 2 * py + pc))
    return me, peers


def _all_gather(shards):
    n = len(shards)

    def body(*refs):
        ins, outs = refs[:n], refs[n:2 * n]
        send, recv, lsem = refs[2 * n:]
        x, y, c = lax.axis_index("x"), lax.axis_index("y"), lax.axis_index("c")
        me, sibling = (x, y, c), (x, y, 1 - c)
        chips = [(1 - x, y), (x, 1 - y), (1 - x, 1 - y)]

        def copy(w, k, block, to, src=None):
            px, py, pc = block
            dst = outs[w].at[4 * px + 2 * py + pc]
            return pltpu.make_async_remote_copy(dst if src is None else src, dst, send.at[w, k], recv.at[w, k],
                                                device_id=to, device_id_type=MESH)

        local = [pltpu.make_async_copy(ins[w], outs[w].at[4 * x + 2 * y + c], lsem.at[w]) for w in range(n)]
        for cp in local:
            cp.start()
        first = []
        for w in range(n):
            first += [copy(w, 1 + j, me, (*chip, c), src=ins[w]) for j, chip in enumerate(chips)]
            first.append(copy(w, 0, me, sibling, src=ins[w]))
        for cp in first:
            cp.start()
        passed = []
        for w in range(n):
            for j, chip in enumerate(chips):
                copy(w, 1 + j, (*chip, c), me).wait_recv()
                cp = copy(w, 4 + j, (*chip, c), sibling)
                cp.start()
                passed.append(cp)
        for w in range(n):
            copy(w, 0, sibling, me).wait_recv()
            for j, chip in enumerate(chips):
                copy(w, 4 + j, (*chip, 1 - c), me).wait_recv()
        for cp in first + passed:
            cp.wait_send()
        for cp in local:
            cp.wait()

    hbm = pl.BlockSpec(memory_space=pl.ANY)
    return pl.pallas_call(
        body, name="all_gather_weights",
        in_specs=[hbm] * n, out_specs=[hbm] * n,
        out_shape=[jax.ShapeDtypeStruct((N_DEV,) + s.shape, s.dtype) for s in shards],
        scratch_shapes=[pltpu.SemaphoreType.DMA((n, N_DEV - 1)), pltpu.SemaphoreType.DMA((n, N_DEV - 1)),
                        pltpu.SemaphoreType.DMA((n,))],
    )(*shards)


def _gather_on_sequencer(shards, name):
    n = len(shards)
    hbm = pltpu.MemorySpace.HBM
    ins = [jax.new_ref(s, memory_space=hbm) for s in shards]
    outs = [jax.empty_ref(jax.ShapeDtypeStruct((N_DEV,) + s.shape, s.dtype), memory_space=hbm) for s in shards]

    @pl.kernel(mesh=plsc.ScalarSubcoreMesh(axis_name="seq", num_cores=1), name=name,
               scratch_types=(pltpu.SemaphoreType.DMA((n, N_DEV - 1)), pltpu.SemaphoreType.DMA((n, N_DEV - 1)),
                              pltpu.SemaphoreType.DMA((n,))),
               compiler_params=pltpu.CompilerParams(collective_id=0))
    def launch(send, recv, lsem):
        x, y, c = lax.axis_index("x"), lax.axis_index("y"), lax.axis_index("c")
        me, sibling = (x, y, c), (x, y, 1 - c)
        chips = [(1 - x, y), (x, 1 - y), (1 - x, 1 - y)]
        barrier = pltpu.get_barrier_semaphore()
        for peer in [sibling] + [(*chip, c) for chip in chips]:
            pl.semaphore_signal(barrier, inc=1, device_id=peer, device_id_type=MESH)
        pl.semaphore_wait(barrier, 4)

        def copy(w, k, block, to, src=None):
            px, py, pc = block
            dst = outs[w].at[4 * px + 2 * py + pc]
            return pltpu.make_async_remote_copy(dst if src is None else src, dst, send.at[w, k], recv.at[w, k],
                                                device_id=to, device_id_type=MESH)

        local = [pltpu.make_async_copy(ins[w], outs[w].at[4 * x + 2 * y + c], lsem.at[w]) for w in range(n)]
        for cp in local:
            cp.start()
        first = []
        for w in range(n):
            first += [copy(w, 1 + j, me, (*chip, c), src=ins[w]) for j, chip in enumerate(chips)]
            first.append(copy(w, 0, me, sibling, src=ins[w]))
        for cp in first:
            cp.start()
        passed = []
        for w in range(n):
            for j, chip in enumerate(chips):
                copy(w, 1 + j, (*chip, c), me).wait_recv()
                cp = copy(w, 4 + j, (*chip, c), sibling)
                cp.start()
                passed.append(cp)
        for w in range(n):
            copy(w, 0, sibling, me).wait_recv()
            for j, chip in enumerate(chips):
                copy(w, 4 + j, (*chip, 1 - c), me).wait_recv()
        for cp in first + passed:
            cp.wait_send()
        for cp in local:
            cp.wait()

    launch()
    return [o[...] for o in outs]


N_CHIP = 4
CHIPS = ((0, 0), (0, 1), (1, 0), (1, 1))


def _pair_exchange(grads):
    n = len(grads)

    def body(*refs):
        ins, outs = refs[:n], refs[n:2 * n]
        send, recv = refs[2 * n:]
        x, y, c = lax.axis_index("x"), lax.axis_index("y"), lax.axis_index("c")
        sends = []
        for w in range(n):
            for ch, (px, py) in enumerate(CHIPS):
                cp = pltpu.make_async_remote_copy(ins[w].at[4 * px + 2 * py + 1 - c], outs[w].at[ch],
                                                  send.at[w, ch], recv.at[w, ch],
                                                  device_id=(x, y, 1 - c), device_id_type=MESH)
                cp.start()
                sends.append(cp)
        for cp in sends:
            cp.wait_recv()
        for cp in sends:
            cp.wait_send()

    hbm = pl.BlockSpec(memory_space=pl.ANY)
    return pl.pallas_call(
        body, name="pair_exchange_grads",
        in_specs=[hbm] * n, out_specs=[hbm] * n,
        out_shape=[jax.ShapeDtypeStruct((N_CHIP,) + g.shape[1:], g.dtype) for g in grads],
        scratch_shapes=[pltpu.SemaphoreType.DMA((n, N_CHIP)), pltpu.SemaphoreType.DMA((n, N_CHIP))],
    )(*grads)


def _chip_sum(grad, got, *, name, tr=256):
    _, rows, cols = grad.shape
    tr = min(tr, rows)

    def body(core_ref, a_ref, b_ref, o_ref):
        del core_ref
        o_ref[...] = (a_ref[...].astype(F32) + b_ref[...].astype(F32)).astype(BF16)

    return pl.pallas_call(
        body, name=name,
        grid_spec=pltpu.PrefetchScalarGridSpec(
            num_scalar_prefetch=1, grid=(N_CHIP, rows // tr),
            in_specs=[pl.BlockSpec((None, None, tr, cols), lambda ch, i, core: (ch, core[0], i, 0)),
                      pl.BlockSpec((None, tr, cols), lambda ch, i, core: (ch, i, 0))],
            out_specs=pl.BlockSpec((None, tr, cols), lambda ch, i, core: (ch, i, 0))),
        out_shape=jax.ShapeDtypeStruct((N_CHIP, rows, cols), BF16),
        compiler_params=_params(("parallel", "parallel")),
    )(lax.axis_index("c").astype(jnp.int32).reshape(1), grad.reshape(N_CHIP, 2, rows, cols), got)


def _chip_exchange(parts):
    n = len(parts)

    def body(*refs):
        ins, outs = refs[:n], refs[n:2 * n]
        send, recv, lsem = refs[2 * n:]
        x, y, c = lax.axis_index("x"), lax.axis_index("y"), lax.axis_index("c")
        mine = 2 * x + y
        chips = [(1 - x, y), (x, 1 - y), (1 - x, 1 - y)]
        local = [pltpu.make_async_copy(ins[w].at[mine], outs[w].at[mine], lsem.at[w]) for w in range(n)]
        for cp in local:
            cp.start()
        sends = []
        for w in range(n):
            for j, (px, py) in enumerate(chips):
                cp = pltpu.make_async_remote_copy(ins[w].at[2 * px + py], outs[w].at[mine],
                                                  send.at[w, j], recv.at[w, j],
                                                  device_id=(px, py, c), device_id_type=MESH)
                cp.start()
                sends.append(cp)
        for w in range(n):
            for j, (px, py) in enumerate(chips):
                pltpu.make_async_remote_copy(ins[w].at[mine], outs[w].at[2 * px + py],
                                             send.at[w, j], recv.at[w, j],
                                             device_id=(px, py, c), device_id_type=MESH).wait_recv()
        for cp in sends:
            cp.wait_send()
        for cp in local:
            cp.wait()

    hbm = pl.BlockSpec(memory_space=pl.ANY)
    return pl.pallas_call(
        body, name="chip_exchange_grads",
        in_specs=[hbm] * n, out_specs=[hbm] * n,
        out_shape=[jax.ShapeDtypeStruct(p.shape, p.dtype) for p in parts],
        scratch_shapes=[pltpu.SemaphoreType.DMA((n, 3)), pltpu.SemaphoreType.DMA((n, 3)),
                        pltpu.SemaphoreType.DMA((n,))],
    )(*parts)


def _adamw_math(g, w, m, v):
    m2 = B1 * m + (1.0 - B1) * g
    v2 = B2 * v + (1.0 - B2) * (g * g)
    delta = -LR * ((m2 / BC1) / (jnp.sqrt(v2 / BC2) + AEPS) + WD * w)
    return delta, m2, v2


def _adamw(parts, w, m, v, *, name, tr=128):
    rows, cols = w.shape

    def body(p_ref, w_ref, m_ref, v_ref, g_ref, d_ref, mo_ref, vo_ref):
        g = p_ref[0].astype(F32)
        for b in range(1, N_CHIP):
            g = g + p_ref[b].astype(F32)
        delta, m2, v2 = _adamw_math(g, w_ref[...], m_ref[...], v_ref[...])
        g_ref[...] = g
        d_ref[...] = delta
        mo_ref[...] = m2
        vo_ref[...] = v2

    spec = pl.BlockSpec((tr, cols), lambda i: (i, 0))
    shape = jax.ShapeDtypeStruct((rows, cols), F32)
    return pl.pallas_call(
        body, name=name, grid=(rows // tr,),
        in_specs=[pl.BlockSpec((N_CHIP, tr, cols), lambda i: (0, i, 0)), spec, spec, spec],
        out_specs=[spec] * 4, out_shape=[shape] * 4,
        compiler_params=_params(("parallel",)),
    )(parts, w, m, v)


def _small_update(part, w, m, v):
    rows = part.shape[0]

    def body(p_ref, w_ref, m_ref, v_ref, g_ref, d_ref, mo_ref, vo_ref, buf, send, recv):
        me, peers = _me_and_peers()
        buf[me] = p_ref[...]
        sends = []
        for k, (dev, _) in enumerate(peers):
            cp = pltpu.make_async_remote_copy(p_ref, buf.at[me], send.at[k], recv.at[k],
                                              device_id=dev, device_id_type=MESH)
            cp.start()
            sends.append(cp)
        for k, (dev, idx) in enumerate(peers):
            pltpu.make_async_remote_copy(p_ref, buf.at[idx], send.at[k], recv.at[k],
                                         device_id=dev, device_id_type=MESH).wait_recv()
        for cp in sends:
            cp.wait_send()
        g = buf[0]
        for b in range(1, N_DEV):
            g = g + buf[b]
        delta, m2, v2 = _adamw_math(g, w_ref[...], m_ref[...], v_ref[...])
        g_ref[...] = g
        d_ref[...] = delta
        mo_ref[...] = m2
        vo_ref[...] = v2

    vm = pl.BlockSpec(memory_space=pltpu.VMEM)
    shape = jax.ShapeDtypeStruct((rows, HD), F32)
    return pl.pallas_call(
        body, name="small_params_update",
        in_specs=[vm] * 4, out_specs=[vm] * 4, out_shape=[shape] * 4,
        scratch_shapes=[pltpu.VMEM((N_DEV, rows, HD), F32),
                        pltpu.SemaphoreType.DMA((N_DEV - 1,)), pltpu.SemaphoreType.DMA((N_DEV - 1,))],
    )(part, w, m, v)


def _pack_small(norm_mix, b_gate, qa, ka, qb, kb, rpb, norm_ffn):
    gains = jnp.concatenate([qa, ka, qb, kb, jnp.zeros((4, HD), F32)], axis=0)
    rpb_pad = jnp.pad(rpb.reshape(4 * (2 * WIN_R - 1), 2 * WIN_C - 1), ((0, 4), (0, HD - (2 * WIN_C - 1))))
    return jnp.concatenate([norm_mix.reshape(16, HD), b_gate.reshape(32, HD), gains, rpb_pad,
                            norm_ffn.reshape(16, HD)], axis=0)


def _unpack_small(p):
    norm_mix = p[0:16].reshape(1, D)
    b_gate = p[16:48].reshape(1, 2 * D)
    qa, ka, qb, kb = (p[48 + i:49 + i] for i in range(4))
    rpb = p[56:116, :2 * WIN_C - 1].reshape(1, 4, 2 * WIN_R - 1, 2 * WIN_C - 1)
    norm_ffn = p[120:136].reshape(1, D)
    return norm_mix, b_gate, qa, ka, qb, kb, rpb, norm_ffn


def kernel(x, norm_mix, w_in, b_gate, q_norm_a, k_norm_a, q_norm_b, k_norm_b, rpb_b, w_proj_a, w_proj_b, w_out, norm_ffn, w_up, w_down, loss_target, m_norm_mix, m_w_in, m_b_gate, m_q_norm_a, m_k_norm_a, m_q_norm_b, m_k_norm_b, m_rpb_b, m_w_proj_a, m_w_proj_b, m_w_out, m_norm_ffn, m_w_up, m_w_down, v_norm_mix, v_w_in, v_b_gate, v_q_norm_a, v_k_norm_a, v_q_norm_b, v_k_norm_b, v_rpb_b, v_w_proj_a, v_w_proj_b, v_w_out, v_norm_ffn, v_w_up, v_w_down):
    big_w = (w_in[0], w_proj_a[0], w_proj_b[0], w_out[0], w_up[0], w_down[0])
    big_m = (m_w_in[0], m_w_proj_a[0], m_w_proj_b[0], m_w_out[0], m_w_up[0], m_w_down[0])
    big_v = (v_w_in[0], v_w_proj_a[0], v_w_proj_b[0], v_w_out[0], v_w_up[0], v_w_down[0])
    names = ("w_in", "w_proj_a", "w_proj_b", "w_out", "w_up", "w_down")

    shards = [_cast_bf16(w) for w in big_w]
    g_in, = _gather_on_sequencer(shards[0:1], "gather_w_in")
    g_pa, g_pb, g_out = _gather_on_sequencer(shards[1:4], "gather_w_mix")
    g_up, = _gather_on_sequencer(shards[4:5], "gather_w_up")
    g_down, = _gather_on_sequencer(shards[5:6], "gather_w_down")
    small_w = _pack_small(norm_mix, b_gate, q_norm_a, k_norm_a, q_norm_b, k_norm_b, rpb_b, norm_ffn)
    small_m = _pack_small(m_norm_mix, m_b_gate, m_q_norm_a, m_k_norm_a, m_q_norm_b, m_k_norm_b, m_rpb_b, m_norm_ffn)
    small_v = _pack_small(v_norm_mix, v_b_gate, v_q_norm_a, v_k_norm_a, v_q_norm_b, v_k_norm_b, v_rpb_b, v_norm_ffn)

    loss, grad_x, big_g, small_g = _local_step(
        x[0], loss_target[0], norm_mix, b_gate, small_w[48:56], small_w[56:120], norm_ffn,
        g_in, g_pa, g_pb, g_out.reshape(D, D), g_up, g_down.reshape(D_FF, D))

    g_norm_mix, g_ba, g_bb, g_gains, g_rpb, g_norm_ffn = small_g
    small_part = jnp.concatenate([g_norm_mix.reshape(16, HD), g_ba.reshape(16, HD), g_bb.reshape(16, HD),
                                  g_gains, g_rpb, g_norm_ffn.reshape(16, HD)], axis=0)
    s_g, s_d, s_m, s_v = (_unpack_small(t) for t in _small_update(small_part, small_w, small_m, small_v))

    got = _pair_exchange(list(big_g))
    sums = [_chip_sum(big_g[i], got[i], name=f"chip_sum_{names[i]}") for i in range(6)]
    recv = _chip_exchange(sums)
    upd =[_adamw(recv[i], big_w[i], big_m[i], big_v[i], name=f"adamw_{names[i]}") for i in range(6)]
    b_g, b_d, b_m, b_v = ([u[j][None] for u in upd] for j in range(4))

    def order(small, big):
        nm, bg, qa, ka, qb, kb, rpb, nf = small
        w_in_, pa_, pb_, out_, up_, down_ = big
        return (nm, w_in_, bg, qa, ka, qb, kb, rpb, pa_, pb_, out_, nf, up_, down_)

    total = lax.psum(loss[0, 0], ("x", "y", "c"))
    return (total, grad_x[None], *order(s_g, b_g), *order(s_d, b_d), *order(s_m, b_m), *order(s_v, b_v))
```

```python
import functools

import jax
import jax.numpy as jnp
import numpy as np
from jax import lax
from jax.experimental import pallas as pl
from jax.experimental.pallas import tpu as pltpu
from jax.experimental.pallas import tpu_sc as plsc

F32 = jnp.float32
BF16 = jnp.bfloat16

N_DEV = 8
S = 2048
D = 2048
HD = 128
NH = 16
NH_A = 12
QKV = NH * HD
D_IN = 3 * QKV + 2 * D
D_BR = 512
D_FF = 4 * D
GRID_W = 64
ROWS = S // GRID_W
WIN_R = 8
WIN_C = 16
EPS = 1e-6
NEG = -1e30
SCALE = HD ** -0.5
ROPE_THETA = 10000.0
GROUPS_A = ((64, 1, 512), (256, 4, 768), (1024, 16, 2048))
QB = 256

LR, B1, B2, AEPS, WD, STEP = 0.001, 0.9, 0.999, 1e-08, 0.01, 10
BC1 = 1.0 - B1 ** STEP
BC2 = 1.0 - B2 ** STEP

VMEM_LIMIT = 56 * 1024 * 1024
MESH = pl.DeviceIdType.MESH

NN = (((1,), (0,)), ((), ()))
NT = (((1,), (1,)), ((), ()))
TN = (((0,), (0,)), ((), ()))


def _params(sem):
    return pltpu.CompilerParams(dimension_semantics=sem, vmem_limit_bytes=VMEM_LIMIT)


def _matmul(a, b, *, dims, grid, a_spec, b_spec, nk, epi, out_shape, out_specs, name,
            extra=(), extra_specs=(), acc_shape=None):
    n_extra = len(extra)

    def body(a_ref, b_ref, *rest):
        ex = rest[:n_extra]
        if nk == 1:
            outs = rest[n_extra:]
            epi(lax.dot_general(a_ref[...], b_ref[...], dims, preferred_element_type=F32), ex, outs)
            return
        outs, acc = rest[n_extra:-1], rest[-1]
        k = pl.program_id(2)
        part = lax.dot_general(a_ref[...], b_ref[...], dims, preferred_element_type=F32)

        @pl.when(k == 0)
        def _():
            acc[...] = part

        @pl.when(k > 0)
        def _():
            acc[...] += part

        @pl.when(k == nk - 1)
        def _():
            epi(acc[...], ex, outs)

    scratch = [] if nk == 1 else [pltpu.VMEM(acc_shape, F32)]
    return pl.pallas_call(
        body, name=name, grid=grid,
        in_specs=[a_spec, b_spec, *extra_specs],
        out_specs=out_specs, out_shape=out_shape, scratch_shapes=scratch,
        compiler_params=_params(("parallel", "parallel", "arbitrary")),
    )(a, b, *extra)


def _epi_store(acc, ex, outs):
    outs[0][...] = acc.astype(outs[0].dtype)


def _epi_residual(acc, ex, outs):
    outs[0][...] = acc + ex[0][...]


def _mm_nn(a, b3, *, tm, tn, tk, name, out_dtype=F32, epi=_epi_store, extra=(), n_out=1,
           out_dtypes=None):
    m, kdim = a.shape
    g, _, ng = b3.shape
    n = g * ng
    npg = ng // tn
    nk = kdim // tk
    grid = (n // tn, m // tm, nk)
    tile = pl.BlockSpec((tm, tn), lambda j, i, k: (i, j))
    dts = out_dtypes or (out_dtype,) * n_out
    shapes = tuple(jax.ShapeDtypeStruct((m, n), dt) for dt in dts)
    return _matmul(
        a, b3, dims=NN, grid=grid, nk=nk, epi=epi, name=name,
        a_spec=pl.BlockSpec((tm, tk), lambda j, i, k: (i, k)),
        b_spec=pl.BlockSpec((None, tk, tn), lambda j, i, k: (j // npg, k, j % npg)),
        extra=extra, extra_specs=[tile] * len(extra),
        out_shape=shapes if len(dts) > 1 else shapes[0],
        out_specs=[tile] * len(dts) if len(dts) > 1 else tile,
        acc_shape=(tm, tn))


def _mm_nt(a, b3, *, tm, tn, tk, name, out_dtype=F32, epi=_epi_store, extra=()):
    m, kdim = a.shape
    g, n, kg = b3.shape
    kpg = kg // tk
    nk = kdim // tk
    grid = (n // tn, m // tm, nk)
    tile = pl.BlockSpec((tm, tn), lambda j, i, k: (i, j))
    return _matmul(
        a, b3, dims=NT, grid=grid, nk=nk, epi=epi, name=name,
        a_spec=pl.BlockSpec((tm, tk), lambda j, i, k: (i, k)),
        b_spec=pl.BlockSpec((None, tn, tk), lambda j, i, k: (k // kpg, j, k % kpg)),
        extra=extra, extra_specs=[tile] * len(extra),
        out_shape=jax.ShapeDtypeStruct((m, n), out_dtype), out_specs=tile,
        acc_shape=(tm, tn))


def _mm_tn(a, b, *, tm, tn, name, groups=1, out_dtype=BF16):
    t, m = a.shape
    _, n = b.shape
    ng = n // groups
    npg = ng // tn
    grid = (n // tn, m // tm, 1)
    return _matmul(
        a, b, dims=TN, grid=grid, nk=1, epi=_epi_store, name=name,
        a_spec=pl.BlockSpec((t, tm), lambda j, i, k: (0, i)),
        b_spec=pl.BlockSpec((t, tn), lambda j, i, k: (0, j)),
        out_shape=jax.ShapeDtypeStruct((groups, m, ng), out_dtype),
        out_specs=pl.BlockSpec((None, tm, tn), lambda j, i, k: (j // npg, i, j % npg)))


def _rms_fwd(x, g, *, name, tr=256):
    def body(x_ref, g_ref, y_ref, r_ref):
        xv = x_ref[...]
        r = lax.rsqrt(jnp.mean(xv * xv, axis=-1, keepdims=True) + EPS)
        y_ref[...] = (xv * r * g_ref[...]).astype(BF16)
        r_ref[...] = r

    row = pl.BlockSpec((tr, D), lambda i: (i, 0))
    return pl.pallas_call(
        body, name=name, grid=(S // tr,),
        in_specs=[row, pl.BlockSpec((1, D), lambda i: (0, 0))],
        out_specs=[row, pl.BlockSpec((tr, 1), lambda i: (i, 0))],
        out_shape=[jax.ShapeDtypeStruct((S, D), BF16), jax.ShapeDtypeStruct((S, 1), F32)],
        compiler_params=_params(("parallel",)),
    )(x, g)


def _rms_bwd(dy, x, rstd, g, resid, *, name, tr=256):
    def body(dy_ref, x_ref, r_ref, g_ref, res_ref, dx_ref, dxb_ref, dg_ref):
        r = r_ref[...]
        xh = x_ref[...] * r
        dyv = dy_ref[...]
        t = dyv * g_ref[...]
        dx = r * (t - xh * jnp.mean(t * xh, axis=-1, keepdims=True)) + res_ref[...]
        dx_ref[...] = dx
        dxb_ref[...] = dx.astype(BF16)
        part = jnp.sum(dyv * xh, axis=0, keepdims=True)

        @pl.when(pl.program_id(0) == 0)
        def _():
            dg_ref[...] = part

        @pl.when(pl.program_id(0) > 0)
        def _():
            dg_ref[...] += part

    row = pl.BlockSpec((tr, D), lambda i: (i, 0))
    vec = pl.BlockSpec((1, D), lambda i: (0, 0))
    return pl.pallas_call(
        body, name=name, grid=(S // tr,),
        in_specs=[row, row, pl.BlockSpec((tr, 1), lambda i: (i, 0)), vec, row],
        out_specs=[row, row, vec],
        out_shape=[jax.ShapeDtypeStruct((S, D), F32), jax.ShapeDtypeStruct((S, D), BF16),
                   jax.ShapeDtypeStruct((1, D), F32)],
        compiler_params=_params(("arbitrary",)),
    )(dy, x, rstd, g, resid)


def _loss_head(h2, target, *, tr=256):
    def body(h_ref, t_ref, dy_ref, dyb_ref, loss_ref):
        e = h_ref[...] - t_ref[...]
        dy = e * (1.0 / D)
        dy_ref[...] = dy
        dyb_ref[...] = dy.astype(BF16)
        part = (0.5 / D) * jnp.sum(jnp.sum(e * e, axis=-1, keepdims=True), axis=0, keepdims=True)

        @pl.when(pl.program_id(0) == 0)
        def _():
            loss_ref[...] = part

        @pl.when(pl.program_id(0) > 0)
        def _():
            loss_ref[...] += part

    row = pl.BlockSpec((tr, D), lambda i: (i, 0))
    return pl.pallas_call(
        body, name="loss_head", grid=(S // tr,),
        in_specs=[row, row],
        out_specs=[row, row, pl.BlockSpec((1, 1), lambda i: (0, 0))],
        out_shape=[jax.ShapeDtypeStruct((S, D), F32), jax.ShapeDtypeStruct((S, D), BF16),
                   jax.ShapeDtypeStruct((1, 1), F32)],
        compiler_params=_params(("arbitrary",)),
    )(h2, target)


def _rope_tables():
    pos = np.arange(S, dtype=np.float32)
    inv = (ROPE_THETA ** (-np.arange(0, HD, 2, dtype=np.float32) / HD)).astype(np.float32)
    ang = pos[:, None] * inv[None, :]
    cos, sin = np.cos(ang), np.sin(ang)
    return (jnp.asarray(np.concatenate([cos, cos], axis=-1), F32),
            jnp.asarray(np.concatenate([-sin, sin], axis=-1), F32))


def _swap_halves(t):
    return pltpu.roll(t, HD // 2, axis=1)


def _qk_prep(proj, gains, cos2, sin2, *, tr=256):
    def body(q_ref, k_ref, v_ref, g_ref, c_ref, s_ref, qn_ref, kn_ref, vb_ref):
        cos, sin = c_ref[...], s_ref[...]
        for src, dst, row_a, row_b in ((q_ref, qn_ref, 0, 2), (k_ref, kn_ref, 1, 3)):
            for h in range(NH):
                cols = slice(h * HD, (h + 1) * HD)
                t = src[:, cols]
                r = lax.rsqrt(jnp.mean(t * t, axis=-1, keepdims=True) + EPS)
                if h < NH_A:
                    y = t * r * g_ref[row_a:row_a + 1, :]
                    y = y * cos + _swap_halves(y) * sin
                else:
                    y = t * r * g_ref[row_b:row_b + 1, :]
                dst[:, cols] = y.astype(BF16)
        vb_ref[...] = v_ref[...].astype(BF16)

    def blk(c):
        return pl.BlockSpec((tr, QKV), lambda i: (i, c))
    tab = pl.BlockSpec((tr, HD), lambda i: (i, 0))
    out = pl.BlockSpec((tr, QKV), lambda i: (i, 0))
    return pl.pallas_call(
        body, name="qk_prep", grid=(S // tr,),
        in_specs=[blk(0), blk(1), blk(2), pl.BlockSpec((8, HD), lambda i: (0, 0)), tab, tab],
        out_specs=[out, out, out],
        out_shape=[jax.ShapeDtypeStruct((S, QKV), BF16)] * 3,
        compiler_params=_params(("parallel",)),
    )(proj, proj, proj, gains, cos2, sin2)


def _qk_prep_bwd(dproj, proj, gains, cos2, sin2, dq_parts, dk_parts, dv_parts, *, tr=256):
    def body(dp_in, q_ref, k_ref, g_ref, c_ref, s_ref, *rest):
        dqs, dks, dvs = rest[0:4], rest[4:8], rest[8:12]
        dp_out, dg_ref = rest[12:14]
        del dp_in
        cos, sin = c_ref[...], s_ref[...]
        dg_rows = []
        for src, grads, base, row_a, row_b in ((q_ref, dqs, 0, 0, 2), (k_ref, dks, QKV, 1, 3)):
            dg_a = jnp.zeros((1, HD), F32)
            dg_b = jnp.zeros((1, HD), F32)
            for h in range(NH):
                cols = slice(h * HD, (h + 1) * HD)
                t = src[:, cols]
                dy = grads[h // 4][:, (h % 4) * HD:(h % 4 + 1) * HD]
                r = lax.rsqrt(jnp.mean(t * t, axis=-1, keepdims=True) + EPS)
                xh = t * r
                if h < NH_A:
                    dy = dy * cos - _swap_halves(dy) * sin
                    gain = g_ref[row_a:row_a + 1, :]
                    dg_a = dg_a + jnp.sum(dy * xh, axis=0, keepdims=True)
                else:
                    gain = g_ref[row_b:row_b + 1, :]
                    dg_b = dg_b + jnp.sum(dy * xh, axis=0, keepdims=True)
                u = dy * gain
                dx = r * (u - xh * jnp.mean(u * xh, axis=-1, keepdims=True))
                dp_out[:, base + h * HD:base + (h + 1) * HD] = dx.astype(BF16)
            dg_rows += [(row_a, dg_a), (row_b, dg_b)]
        for g4 in range(4):
            dp_out[:, 2 * QKV + g4 * D_BR:2 * QKV + (g4 + 1) * D_BR] = dvs[g4][...].astype(BF16)

        first = pl.program_id(0) == 0

        @pl.when(first)
        def _():
            dg_ref[...] = jnp.zeros((8, HD), F32)

        for row, val in dg_rows:
            dg_ref[row:row + 1, :] += val

    def blk(c):
        return pl.BlockSpec((tr, QKV), lambda i: (i, c))
    tab = pl.BlockSpec((tr, HD), lambda i: (i, 0))
    part = pl.BlockSpec((tr, D_BR), lambda i: (i, 0))
    gain_spec = pl.BlockSpec((8, HD), lambda i: (0, 0))
    return pl.pallas_call(
        body, name="qk_prep_bwd", grid=(S // tr,),
        in_specs=[pl.BlockSpec(memory_space=pl.ANY), blk(0), blk(1), gain_spec, tab, tab] + [part] * 12,
        out_specs=[pl.BlockSpec((tr, 3 * QKV), lambda i: (i, 0)), gain_spec],
        out_shape=[jax.ShapeDtypeStruct((S, D_IN), BF16), jax.ShapeDtypeStruct((8, HD), F32)],
        input_output_aliases={0: 0},
        compiler_params=_params(("arbitrary",)),
    )(dproj, proj, proj, gains, cos2, sin2, *dq_parts, *dk_parts, *dv_parts)


def _gate_fwd(proj, b_gate, ya, yb, *, tr=256):
    def body(la_ref, lb_ref, ba_ref, bb_ref, ya_ref, yb_ref, o_ref):
        ga = jax.nn.sigmoid(la_ref[...] + ba_ref[...])
        gb = jax.nn.sigmoid(lb_ref[...] + bb_ref[...])
        o_ref[...] = (ga * ya_ref[...] + gb * yb_ref[...]).astype(BF16)

    row = pl.BlockSpec((tr, D), lambda i: (i, 0))
    return pl.pallas_call(
        body, name="gate_fwd", grid=(S // tr,),
        in_specs=[pl.BlockSpec((tr, D), lambda i: (i, 3)), pl.BlockSpec((tr, D), lambda i: (i, 4)),
                  pl.BlockSpec((1, D), lambda i: (0, 0)), pl.BlockSpec((1, D), lambda i: (0, 1)),
                  row, row],
        out_specs=row, out_shape=jax.ShapeDtypeStruct((S, D), BF16),
        compiler_params=_params(("parallel",)),
    )(proj, proj, b_gate, b_gate, ya, yb)


def _gate_bwd(branch, dmixed, proj, b_gate, y, dproj, *, tr=256):
    aliased = dproj is not None

    def body(dm_ref, l_ref, b_ref, y_ref, *rest):
        dy_ref, dp_ref, db_ref = rest[-3:]
        g = jax.nn.sigmoid(l_ref[...] + b_ref[...])
        dm = dm_ref[...]
        dy_ref[...] = (dm * g).astype(BF16)
        dl = dm * y_ref[...] * g * (1.0 - g)
        dp_ref[...] = dl.astype(BF16)
        part = jnp.sum(dl, axis=0, keepdims=True)

        @pl.when(pl.program_id(0) == 0)
        def _():
            db_ref[...] = part

        @pl.when(pl.program_id(0) > 0)
        def _():
            db_ref[...] += part

    row = pl.BlockSpec((tr, D), lambda i: (i, 0))
    col = pl.BlockSpec((tr, D), lambda i: (i, 3 + branch))
    vec = pl.BlockSpec((1, D), lambda i: (0, 0))
    return pl.pallas_call(
        body, name=f"gate_bwd_{branch}", grid=(S // tr,),
        in_specs=[row, col, pl.BlockSpec((1, D), lambda i: (0, branch)), row]
        + ([pl.BlockSpec(memory_space=pl.ANY)] if aliased else []),
        out_specs=[row, col, vec],
        out_shape=[jax.ShapeDtypeStruct((S, D), BF16), jax.ShapeDtypeStruct((S, D_IN), BF16),
                   jax.ShapeDtypeStruct((1, D), F32)],
        input_output_aliases={4: 1} if aliased else {},
        compiler_params=_params(("arbitrary",)),
    )(dmixed, proj, b_gate, y, *([dproj] if aliased else []))


def _window_start(t0, wk):
    if wk == S:
        return 0
    return pl.multiple_of(jnp.clip(t0 - (wk - QB) // 2, 0, S - wk), 128)


def _scores_a(q, kw, t0, start, hs, dil, wk):
    s = lax.dot_general(q, kw, NT, preferred_element_type=F32) * SCALE
    qpos = t0 + lax.broadcasted_iota(jnp.int32, (QB, 1), 0)
    kpos = start + lax.broadcasted_iota(jnp.int32, (1, wk), 1)
    diff = kpos - qpos
    keep = (jnp.abs(diff) <= hs) & ((diff & (dil - 1)) == 0)
    return jnp.where(keep, s, NEG)


def _attn_a_fwd(qn, kn, vb, gi):
    hs, dil, wk = GROUPS_A[gi]

    def body(q_ref, k_ref, v_ref, o_ref, lse_ref):
        t0 = pl.program_id(1) * QB
        start = _window_start(t0, wk)
        s = _scores_a(q_ref[...], k_ref[pl.ds(start, wk), :], t0, start, hs, dil, wk)
        m = jnp.max(s, axis=-1, keepdims=True)
        p = jnp.exp(s - m)
        l = jnp.sum(p, axis=-1, keepdims=True)
        o = lax.dot_general(p.astype(BF16), v_ref[pl.ds(start, wk), :], NN, preferred_element_type=F32)
        o_ref[...] = o / l
        lse_ref[...] = m + jnp.log(l)

    full = pl.BlockSpec((S, HD), lambda h, i: (0, 4 * gi + h))
    return pl.pallas_call(
        body, name=f"attn_a_fwd_{gi}", grid=(4, S // QB),
        in_specs=[pl.BlockSpec((QB, HD), lambda h, i: (i, 4 * gi + h)), full, full],
        out_specs=[pl.BlockSpec((QB, HD), lambda h, i: (i, h)),
                   pl.BlockSpec((None, QB, 1), lambda h, i: (h, i, 0))],
        out_shape=[jax.ShapeDtypeStruct((S, D_BR), F32), jax.ShapeDtypeStruct((4, S, 1), F32)],
        compiler_params=_params(("parallel", "parallel")),
    )(qn, kn, vb)


def _combine_a(os, lses, *, tr=256):
    def body(o0, o1, o2, l0, l1, l2, oa_ref, lse_ref):
        for h in range(4):
            cols = slice(h * HD, (h + 1) * HD)
            a, b, c = l0[h], l1[h], l2[h]
            m = jnp.maximum(jnp.maximum(a, b), c)
            wa, wb, wc = jnp.exp(a - m), jnp.exp(b - m), jnp.exp(c - m)
            tot = wa + wb + wc
            oa_ref[:, cols] = ((wa * o0[:, cols] + wb * o1[:, cols] + wc * o2[:, cols]) / tot).astype(BF16)
            lse_ref[h] = m + jnp.log(tot)

    row = pl.BlockSpec((tr, D_BR), lambda i: (i, 0))
    stat = pl.BlockSpec((4, tr, 1), lambda i: (0, i, 0))
    return pl.pallas_call(
        body, name="combine_a", grid=(S // tr,),
        in_specs=[row] * 3 + [stat] * 3, out_specs=[row, stat],
        out_shape=[jax.ShapeDtypeStruct((S, D_BR), BF16), jax.ShapeDtypeStruct((4, S, 1), F32)],
        compiler_params=_params(("parallel",)),
    )(*os, *lses)


def _attn_a_bwd(qn, kn, vb, oa, doa, lse, gi):
    hs, dil, wk = GROUPS_A[gi]

    def body(q_ref, k_ref, v_ref, o_ref, do_ref, lse_ref, dq_ref, dk_ref, dv_ref):
        @pl.when(pl.program_id(1) == 0)
        def _():
            dk_ref[...] = jnp.zeros((S, HD), F32)
            dv_ref[...] = jnp.zeros((S, HD), F32)

        t0 = pl.program_id(1) * QB
        start = _window_start(t0, wk)
        q = q_ref[...]
        kw = k_ref[pl.ds(start, wk), :]
        vw = v_ref[pl.ds(start, wk), :]
        p = jnp.exp(_scores_a(q, kw, t0, start, hs, dil, wk) - lse_ref[...])
        do = do_ref[...]
        dob = do.astype(BF16)
        dsum = jnp.sum(do * o_ref[...].astype(F32), axis=-1, keepdims=True)
        dp = lax.dot_general(dob, vw, NT, preferred_element_type=F32)
        ds = (p * (dp - dsum) * SCALE).astype(BF16)
        dq_ref[...] = lax.dot_general(ds, kw, NN, preferred_element_type=F32)
        dk_ref[pl.ds(start, wk), :] += lax.dot_general(ds, q, TN, preferred_element_type=F32)
        dv_ref[pl.ds(start, wk), :] += lax.dot_general(p.astype(BF16), dob, TN, preferred_element_type=F32)

    full = pl.BlockSpec((S, HD), lambda h, i: (0, 4 * gi + h))
    blk = pl.BlockSpec((QB, HD), lambda h, i: (i, h))
    acc = pl.BlockSpec((S, HD), lambda h, i: (0, h))
    shape = jax.ShapeDtypeStruct((S, D_BR), F32)
    return pl.pallas_call(
        body, name=f"attn_a_bwd_{gi}", grid=(4, S // QB),
        in_specs=[pl.BlockSpec((QB, HD), lambda h, i: (i, 4 * gi + h)), full, full, blk, blk,
                  pl.BlockSpec((None, QB, 1), lambda h, i: (h, i, 0))],
        out_specs=[blk, acc, acc], out_shape=[shape, shape, shape],
        compiler_params=_params(("parallel", "arbitrary")),
    )(qn, kn, vb, oa, doa, lse)


KEYS_B = WIN_R * GRID_W
N_OFF = WIN_R


def _bias_constants():
    q = np.arange(GRID_W)[:, None]
    kc = np.arange(GRID_W)[None, :]
    dc = np.clip(kc - q, -(WIN_C - 1), WIN_C - 1) + (WIN_C - 1)
    expand = np.zeros((HD, GRID_W * GRID_W), np.float32)
    expand[dc.reshape(-1), np.arange(GRID_W * GRID_W)] = 1.0
    cs = np.clip(q - WIN_C // 2, 0, GRID_W - WIN_C)
    keep = ((kc >= cs) & (kc < cs + WIN_C)).reshape(1, -1).astype(np.float32)
    sel = np.zeros((64, 4 * N_OFF * WIN_R), np.float32)
    for h in range(4):
        for off in range(N_OFF):
            for j in range(WIN_R):
                sel[h * (2 * WIN_R - 1) + off + j, (h * N_OFF + off) * WIN_R + j] = 1.0
    return jnp.asarray(expand), jnp.asarray(keep), jnp.asarray(sel)


def _bias_expand(rpb_pad, expand, keep, sel):
    def body(r_ref, e_ref, k_ref, s_ref, o_ref):
        t = lax.dot_general(r_ref[...], e_ref[...], NN, precision=lax.Precision.HIGHEST,
                            preferred_element_type=F32)
        rows = lax.dot_general(s_ref[...], t, TN, precision=lax.Precision.HIGHEST,
                               preferred_element_type=F32)
        o_ref[...] = jnp.where(k_ref[...] > 0.5, rows, NEG)

    return pl.pallas_call(
        body, name="bias_expand",
        out_shape=jax.ShapeDtypeStruct((4 * N_OFF * WIN_R, GRID_W * GRID_W), F32),
        compiler_params=pltpu.CompilerParams(vmem_limit_bytes=VMEM_LIMIT),
    )(rpb_pad, expand, keep, sel)


def _bias_reduce(dbias_rows, expand, sel):
    def body(x_ref, e_ref, s_ref, o_ref):
        z = lax.dot_general(x_ref[...], e_ref[...], NT, precision=lax.Precision.HIGHEST,
                            preferred_element_type=F32)
        o_ref[...] = lax.dot_general(s_ref[...], z, NN, precision=lax.Precision.HIGHEST,
                                     preferred_element_type=F32)

    return pl.pallas_call(
        body, name="bias_reduce", out_shape=jax.ShapeDtypeStruct((64, HD), F32),
        compiler_params=pltpu.CompilerParams(vmem_limit_bytes=VMEM_LIMIT),
    )(dbias_rows, expand, sel)


def _rows_to_tab(rows):
    t = rows.reshape(4, N_OFF, WIN_R, GRID_W, GRID_W)
    return t.transpose(0, 1, 3, 2, 4).reshape(4, N_OFF, GRID_W, KEYS_B)


def _tab_to_rows(tab):
    t = tab.reshape(4, N_OFF, GRID_W, WIN_R, GRID_W)
    return t.transpose(0, 1, 3, 2, 4).reshape(4 * N_OFF * WIN_R, GRID_W * GRID_W)


def _row_window(r):
    r0 = jnp.clip(r - WIN_R // 2, 0, ROWS - WIN_R)
    off = r0 + (WIN_R - 1) - r
    return pl.multiple_of(r * GRID_W, GRID_W), pl.multiple_of(r0 * GRID_W, GRID_W), off


def _attn_b_fwd(qn, kn, vb, bias_tab):
    def body(q_ref, k_ref, v_ref, b_ref, o_ref, lse_ref):
        def row(r, carry):
            qs, ks, off = _row_window(r)
            q = q_ref[pl.ds(qs, GRID_W), :]
            s = lax.dot_general(q, k_ref[pl.ds(ks, KEYS_B), :], NT, preferred_element_type=F32) * SCALE
            s = s + b_ref[off]
            m = jnp.max(s, axis=-1, keepdims=True)
            p = jnp.exp(s - m)
            l = jnp.sum(p, axis=-1, keepdims=True)
            o = lax.dot_general(p.astype(BF16), v_ref[pl.ds(ks, KEYS_B), :], NN, preferred_element_type=F32)
            o_ref[pl.ds(qs, GRID_W), :] = (o / l).astype(BF16)
            lse_ref[pl.ds(qs, GRID_W), :] = m + jnp.log(l)
            return carry

        lax.fori_loop(0, ROWS, row, 0)

    full = pl.BlockSpec((S, HD), lambda h: (0, NH_A + h))
    return pl.pallas_call(
        body, name="attn_b_fwd", grid=(4,),
        in_specs=[full, full, full, pl.BlockSpec((None, N_OFF, GRID_W, KEYS_B), lambda h: (h, 0, 0, 0))],
        out_specs=[pl.BlockSpec((S, HD), lambda h: (0, h)), pl.BlockSpec((None, S, 1), lambda h: (h, 0, 0))],
        out_shape=[jax.ShapeDtypeStruct((S, D_BR), BF16), jax.ShapeDtypeStruct((4, S, 1), F32)],
        compiler_params=_params(("parallel",)),
    )(qn, kn, vb, bias_tab)


def _attn_b_bwd(qn, kn, vb, bias_tab, ob, dob, lse):
    def body(q_ref, k_ref, v_ref, b_ref, o_ref, do_ref, lse_ref, dq_ref, dk_ref, dv_ref, db_ref):
        dk_ref[...] = jnp.zeros((S, HD), F32)
        dv_ref[...] = jnp.zeros((S, HD), F32)
        db_ref[...] = jnp.zeros((N_OFF, GRID_W, KEYS_B), F32)

        def row(r, carry):
            qs, ks, off = _row_window(r)
            rows = pl.ds(qs, GRID_W)
            keys = pl.ds(ks, KEYS_B)
            q = q_ref[rows, :]
            kw = k_ref[keys, :]
            s = lax.dot_general(q, kw, NT, preferred_element_type=F32) * SCALE + b_ref[off]
            p = jnp.exp(s - lse_ref[rows, :])
            do = do_ref[rows, :]
            dobf = do.astype(BF16)
            dsum = jnp.sum(do * o_ref[rows, :].astype(F32), axis=-1, keepdims=True)
            dp = lax.dot_general(dobf, v_ref[keys, :], NT, preferred_element_type=F32)
            ds = p * (dp - dsum)
            db_ref[off] += ds
            dsb = (ds * SCALE).astype(BF16)
            dq_ref[rows, :] = lax.dot_general(dsb, kw, NN, preferred_element_type=F32)
            dk_ref[keys, :] += lax.dot_general(dsb, q, TN, preferred_element_type=F32)
            dv_ref[keys, :] += lax.dot_general(p.astype(BF16), dobf, TN, preferred_element_type=F32)
            return carry

        lax.fori_loop(0, ROWS, row, 0)

    full = pl.BlockSpec((S, HD), lambda h: (0, NH_A + h))
    slot = pl.BlockSpec((S, HD), lambda h: (0, h))
    tab = pl.BlockSpec((None, N_OFF, GRID_W, KEYS_B), lambda h: (h, 0, 0, 0))
    shape = jax.ShapeDtypeStruct((S, D_BR), F32)
    return pl.pallas_call(
        body, name="attn_b_bwd", grid=(4,),
        in_specs=[full, full, full, tab, slot, slot, pl.BlockSpec((None, S, 1), lambda h: (h, 0, 0))],
        out_specs=[slot, slot, slot, tab],
        out_shape=[shape, shape, shape, jax.ShapeDtypeStruct((4, N_OFF, GRID_W, KEYS_B), F32)],
        compiler_params=_params(("parallel",)),
    )(qn, kn, vb, bias_tab, ob, dob, lse)


def _epi_relu_sq(acc, ex, outs):
    u = jnp.maximum(acc, 0.0)
    outs[0][...] = u.astype(BF16)
    outs[1][...] = (u * u).astype(BF16)


def _epi_relu_sq_bwd(acc, ex, outs):
    outs[0][...] = (acc * (2.0 * ex[0][...].astype(F32))).astype(BF16)


def _local_step(x, target, norm_mix, b_gate, gains, rpb_pad, norm_ffn,
                w_in, w_pa, w_pb, w_out, w_up, w_down):
    cos2, sin2 = _rope_tables()
    expand, keep, sel = _bias_constants()
    w_out3, w_down3 = w_out[None], w_down[None]

    xn, rstd1 = _rms_fwd(x, norm_mix, name="rms_mix")
    proj = _mm_nn(xn, w_in, tm=512, tn=1280, tk=D, name="proj")
    qn, kn, vb = _qk_prep(proj, gains, cos2, sin2)
    fwd_a = [_attn_a_fwd(qn, kn, vb, gi) for gi in range(3)]
    oa, lse_a = _combine_a([o for o, _ in fwd_a], [l for _, l in fwd_a])
    bias_tab = _rows_to_tab(_bias_expand(rpb_pad, expand, keep, sel))
    ob, lse_b = _attn_b_fwd(qn, kn, vb, bias_tab)
    ya = _mm_nn(oa, w_pa, tm=1024, tn=256, tk=D_BR, name="proj_a")
    yb = _mm_nn(ob, w_pb, tm=1024, tn=256, tk=D_BR, name="proj_b")
    mixed = _gate_fwd(proj, b_gate, ya, yb)
    h1 = _mm_nn(mixed, w_out3, tm=512, tn=512, tk=D, name="out_proj", epi=_epi_residual, extra=(x,))
    hn, rstd2 = _rms_fwd(h1, norm_ffn, name="rms_ffn")
    u, usq = _mm_nn(hn, w_up, tm=512, tn=1024, tk=D, name="ffn_up", epi=_epi_relu_sq,
                    out_dtypes=(BF16, BF16))
    h2 = _mm_nn(usq, w_down3, tm=512, tn=1024, tk=D, name="ffn_down", epi=_epi_residual, extra=(h1,))
    dy, dyb, loss = _loss_head(h2, target)

    g_down = _mm_tn(usq, dyb, tm=512, tn=1024, name="grad_w_down")
    du = _mm_nt(dyb, w_down3, tm=512, tn=1024, tk=D, name="ffn_down_bwd", out_dtype=BF16,
                epi=_epi_relu_sq_bwd, extra=(u,))
    g_up = _mm_tn(hn, du, tm=512, tn=1024, groups=N_DEV, name="grad_w_up")
    dhn = _mm_nt(du, w_up, tm=512, tn=1024, tk=1024, name="ffn_up_bwd")
    dh1, dh1b, g_norm_ffn = _rms_bwd(dhn, h1, rstd2, norm_ffn, dy, name="rms_ffn_bwd")

    g_out = _mm_tn(mixed, dh1b, tm=512, tn=1024, name="grad_w_out")
    dmixed = _mm_nt(dh1b, w_out3, tm=512, tn=512, tk=D, name="out_proj_bwd")
    dya, dproj, g_ba = _gate_bwd(0, dmixed, proj, b_gate, ya, None)
    dyb2, dproj, g_bb = _gate_bwd(1, dmixed, proj, b_gate, yb, dproj)
    g_pa = _mm_tn(oa, dya, tm=512, tn=256, groups=N_DEV, name="grad_w_proj_a")
    g_pb = _mm_tn(ob, dyb2, tm=512, tn=256, groups=N_DEV, name="grad_w_proj_b")
    doa = _mm_nt(dya, w_pa, tm=1024, tn=512, tk=256, name="proj_a_bwd")
    dob = _mm_nt(dyb2, w_pb, tm=1024, tn=512, tk=256, name="proj_b_bwd")
    bwd = [_attn_a_bwd(qn, kn, vb, oa, doa, lse_a, gi) for gi in range(3)]
    dqb, dkb, dvb, dbias = _attn_b_bwd(qn, kn, vb, bias_tab, ob, dob, lse_b)
    g_rpb = _bias_reduce(_tab_to_rows(dbias), expand, sel)
    dproj, g_gains = _qk_prep_bwd(dproj, proj, gains, cos2, sin2,
                                  [b[0] for b in bwd] + [dqb], [b[1] for b in bwd] + [dkb],
                                  [b[2] for b in bwd] + [dvb])
    g_in = _mm_tn(xn, dproj, tm=512, tn=1280, groups=N_DEV, name="grad_w_in")
    dxn = _mm_nt(dproj, w_in, tm=512, tn=1024, tk=1280, name="proj_bwd")
    grad_x, _, g_norm_mix = _rms_bwd(dxn, x, rstd1, norm_mix, dh1, name="rms_mix_bwd")

    big = (g_in, g_pa, g_pb, g_out.reshape(N_DEV, D // N_DEV, D), g_up, g_down.reshape(N_DEV, D_FF // N_DEV, D))
    small = (g_norm_mix, g_ba, g_bb, g_gains, g_rpb, g_norm_ffn)
    return loss, grad_x, big, small


def _cast_bf16(w, *, tr=256):
    rows, cols = w.shape
    tr = min(tr, rows)

    def body(w_ref, o_ref):
        o_ref[...] = w_ref[...].astype(BF16)

    spec = pl.BlockSpec((tr, cols), lambda i: (i, 0))
    return pl.pallas_call(
        body, name=f"cast_{rows}x{cols}", grid=(rows // tr,), in_specs=[spec], out_specs=spec,
        out_shape=jax.ShapeDtypeStruct((rows, cols), BF16), compiler_params=_params(("parallel",)),
    )(w)


def _me_and_peers():
    x, y, c = lax.axis_index("x"), lax.axis_index("y"), lax.axis_index("c")
    me = 4 * x + 2 * y + c
    peers = []
    for k in range(1, N_DEV):
        px = 1 - x if k & 4 else x
        py = 1 - y if k & 2 else y
        pc = 1 - c if k & 1 else c
        peers.append(((px, py, pc), 4 * px + 2 * py + pc))
    return me, peers


def _all_gather(shards):
    n = len(shards)

    def body(*refs):
        ins, outs = refs[:n], refs[n:2 * n]
        send, recv, lsem = refs[2 * n:]
        x, y, c = lax.axis_index("x"), lax.axis_index("y"), lax.axis_index("c")
        me, sibling = (x, y, c), (x, y, 1 - c)
        chips = [(1 - x, y), (x, 1 - y), (1 - x, 1 - y)]

        def copy(w, k, block, to, src=None):
            px, py, pc = block
            dst = outs[w].at[4 * px + 2 * py + pc]
            return pltpu.make_async_remote_copy(dst if src is None else src, dst, send.at[w, k], recv.at[w, k],
                                                device_id=to, device_id_type=MESH)

        local = [pltpu.make_async_copy(ins[w], outs[w].at[4 * x + 2 * y + c], lsem.at[w]) for w in range(n)]
        for cp in local:
            cp.start()
        first = []
        for w in range(n):
            first += [copy(w, 1 + j, me, (*chip, c), src=ins[w]) for j, chip in enumerate(chips)]
            first.append(copy(w, 0, me, sibling, src=ins[w]))
        for cp in first:
            cp.start()
        passed = []
        for w in range(n):
            for j, chip in enumerate(chips):
                copy(w, 1 + j, (*chip, c), me).wait_recv()
                cp = copy(w, 4 + j, (*chip, c), sibling)
                cp.start()
                passed.append(cp)
        for w in range(n):
            copy(w, 0, sibling, me).wait_recv()
            for j, chip in enumerate(chips):
                copy(w, 4 + j, (*chip, 1 - c), me).wait_recv()
        for cp in first + passed:
            cp.wait_send()
        for cp in local:
            cp.wait()

    hbm = pl.BlockSpec(memory_space=pl.ANY)
    return pl.pallas_call(
        body, name="all_gather_weights",
        in_specs=[hbm] * n, out_specs=[hbm] * n,
        out_shape=[jax.ShapeDtypeStruct((N_DEV,) + s.shape, s.dtype) for s in shards],
        scratch_shapes=[pltpu.SemaphoreType.DMA((n, N_DEV - 1)), pltpu.SemaphoreType.DMA((n, N_DEV - 1)),
                        pltpu.SemaphoreType.DMA((n,))],
    )(*shards)


def _gather_on_sequencer(shards, name):
    n = len(shards)
    hbm = pltpu.MemorySpace.HBM
    ins = [jax.new_ref(s, memory_space=hbm) for s in shards]
    outs = [jax.empty_ref(jax.ShapeDtypeStruct((N_DEV,) + s.shape, s.dtype), memory_space=hbm) for s in shards]

    @pl.kernel(mesh=plsc.ScalarSubcoreMesh(axis_name="seq", num_cores=1), name=name,
               scratch_types=(pltpu.SemaphoreType.DMA((n, N_DEV - 1)), pltpu.SemaphoreType.DMA((n, N_DEV - 1)),
                              pltpu.SemaphoreType.DMA((n,))),
               compiler_params=pltpu.CompilerParams(collective_id=0))
    def launch(send, recv, lsem):
        x, y, c = lax.axis_index("x"), lax.axis_index("y"), lax.axis_index("c")
        me, sibling = (x, y, c), (x, y, 1 - c)
        chips = [(1 - x, y), (x, 1 - y), (1 - x, 1 - y)]
        barrier = pltpu.get_barrier_semaphore()
        for peer in [sibling] + [(*chip, c) for chip in chips]:
            pl.semaphore_signal(barrier, inc=1, device_id=peer, device_id_type=MESH)
        pl.semaphore_wait(barrier, 4)

        def copy(w, k, block, to, src=None):
            px, py, pc = block
            dst = outs[w].at[4 * px + 2 * py + pc]
            return pltpu.make_async_remote_copy(dst if src is None else src, dst, send.at[w, k], recv.at[w, k],
                                                device_id=to, device_id_type=MESH)

        local = [pltpu.make_async_copy(ins[w], outs[w].at[4 * x + 2 * y + c], lsem.at[w]) for w in range(n)]
        for cp in local:
            cp.start()
        first = []
        for w in range(n):
            first += [copy(w, 1 + j, me, (*chip, c), src=ins[w]) for j, chip in enumerate(chips)]
            first.append(copy(w, 0, me, sibling, src=ins[w]))
        for cp in first:
            cp.start()
        passed = []
        for w in range(n):
            for j, chip in enumerate(chips):
                copy(w, 1 + j, (*chip, c), me).wait_recv()
                cp = copy(w, 4 + j, (*chip, c), sibling)
                cp.start()
                passed.append(cp)
        for w in range(n):
            copy(w, 0, sibling, me).wait_recv()
            for j, chip in enumerate(chips):
                copy(w, 4 + j, (*chip, 1 - c), me).wait_recv()
        for cp in first + passed:
            cp.wait_send()
        for cp in local:
            cp.wait()

    launch()
    return [o[...] for o in outs]


N_CHIP = 4
CHIPS = ((0, 0), (0, 1), (1, 0), (1, 1))


def _pair_exchange(grads):
    n = len(grads)

    def body(*refs):
        ins, outs = refs[:n], refs[n:2 * n]
        send, recv = refs[2 * n:]
        x, y, c = lax.axis_index("x"), lax.axis_index("y"), lax.axis_index("c")
        sends = []
        for w in range(n):
            for ch, (px, py) in enumerate(CHIPS):
                cp = pltpu.make_async_remote_copy(ins[w].at[4 * px + 2 * py + 1 - c], outs[w].at[ch],
                                                  send.at[w, ch], recv.at[w, ch],
                                                  device_id=(x, y, 1 - c), device_id_type=MESH)
                cp.start()
                sends.append(cp)
        for cp in sends:
            cp.wait_recv()
        for cp in sends:
            cp.wait_send()

    hbm = pl.BlockSpec(memory_space=pl.ANY)
    return pl.pallas_call(
        body, name="pair_exchange_grads",
        in_specs=[hbm] * n, out_specs=[hbm] * n,
        out_shape=[jax.ShapeDtypeStruct((N_CHIP,) + g.shape[1:], g.dtype) for g in grads],
        scratch_shapes=[pltpu.SemaphoreType.DMA((n, N_CHIP)), pltpu.SemaphoreType.DMA((n, N_CHIP))],
    )(*grads)


def _sequencer(name, n_sems, collective_id):
    return functools.partial(
        pl.kernel, mesh=plsc.ScalarSubcoreMesh(axis_name="seq", num_cores=1), name=name,
        scratch_types=tuple(pltpu.SemaphoreType.DMA(s) for s in n_sems),
        compiler_params=pltpu.CompilerParams(collective_id=collective_id))


def _handshake(peers):
    barrier = pltpu.get_barrier_semaphore()
    for peer in peers:
        pl.semaphore_signal(barrier, inc=1, device_id=peer, device_id_type=MESH)
    pl.semaphore_wait(barrier, len(peers))


def _pair_exchange_on_sequencer(grads, name):
    n = len(grads)
    hbm = pltpu.MemorySpace.HBM
    ins = [jax.new_ref(g, memory_space=hbm) for g in grads]
    outs = [jax.empty_ref(jax.ShapeDtypeStruct((N_CHIP,) + g.shape[1:], g.dtype), memory_space=hbm) for g in grads]

    @_sequencer(name, ((n, N_CHIP), (n, N_CHIP)), 1)
    def launch(send, recv):
        x, y, c = lax.axis_index("x"), lax.axis_index("y"), lax.axis_index("c")
        _handshake([(x, y, 1 - c)])
        sends = []
        for w in range(n):
            for ch, (px, py) in enumerate(CHIPS):
                cp = pltpu.make_async_remote_copy(ins[w].at[4 * px + 2 * py + 1 - c], outs[w].at[ch],
                                                  send.at[w, ch], recv.at[w, ch],
                                                  device_id=(x, y, 1 - c), device_id_type=MESH)
                cp.start()
                sends.append(cp)
        for cp in sends:
            cp.wait_recv()
        for cp in sends:
            cp.wait_send()

    launch()
    return [o[...] for o in outs]


def _chip_exchange_on_sequencer(parts, name):
    n = len(parts)
    hbm = pltpu.MemorySpace.HBM
    ins = [jax.new_ref(p, memory_space=hbm) for p in parts]
    outs = [jax.empty_ref(jax.ShapeDtypeStruct(p.shape, p.dtype), memory_space=hbm) for p in parts]

    @_sequencer(name, ((n, 3), (n, 3), (n,)), 2)
    def launch(send, recv, lsem):
        x, y, c = lax.axis_index("x"), lax.axis_index("y"), lax.axis_index("c")
        mine = 2 * x + y
        chips = [(1 - x, y), (x, 1 - y), (1 - x, 1 - y)]
        _handshake([(*chip, c) for chip in chips])
        local = [pltpu.make_async_copy(ins[w].at[mine], outs[w].at[mine], lsem.at[w]) for w in range(n)]
        for cp in local:
            cp.start()
        sends = []
        for w in range(n):
            for j, (px, py) in enumerate(chips):
                cp = pltpu.make_async_remote_copy(ins[w].at[2 * px + py], outs[w].at[mine],
                                                  send.at[w, j], recv.at[w, j],
                                                  device_id=(px, py, c), device_id_type=MESH)
                cp.start()
                sends.append(cp)
        for w in range(n):
            for j, (px, py) in enumerate(chips):
                pltpu.make_async_remote_copy(ins[w].at[mine], outs[w].at[2 * px + py],
                                             send.at[w, j], recv.at[w, j],
                                             device_id=(px, py, c), device_id_type=MESH).wait_recv()
        for cp in sends:
            cp.wait_send()
        for cp in local:
            cp.wait()

    launch()
    return [o[...] for o in outs]


def _chip_sum(grad, got, *, name, tr=256):
    _, rows, cols = grad.shape
    tr = min(tr, rows)

    def body(core_ref, a_ref, b_ref, o_ref):
        del core_ref
        o_ref[...] = (a_ref[...].astype(F32) + b_ref[...].astype(F32)).astype(BF16)

    return pl.pallas_call(
        body, name=name,
        grid_spec=pltpu.PrefetchScalarGridSpec(
            num_scalar_prefetch=1, grid=(N_CHIP, rows // tr),
            in_specs=[pl.BlockSpec((None, None, tr, cols), lambda ch, i, core: (ch, core[0], i, 0)),
                      pl.BlockSpec((None, tr, cols), lambda ch, i, core: (ch, i, 0))],
            out_specs=pl.BlockSpec((None, tr, cols), lambda ch, i, core: (ch, i, 0))),
        out_shape=jax.ShapeDtypeStruct((N_CHIP, rows, cols), BF16),
        compiler_params=_params(("parallel", "parallel")),
    )(lax.axis_index("c").astype(jnp.int32).reshape(1), grad.reshape(N_CHIP, 2, rows, cols), got)


def _chip_exchange(parts):
    n = len(parts)

    def body(*refs):
        ins, outs = refs[:n], refs[n:2 * n]
        send, recv, lsem = refs[2 * n:]
        x, y, c = lax.axis_index("x"), lax.axis_index("y"), lax.axis_index("c")
        mine = 2 * x + y
        chips = [(1 - x, y), (x, 1 - y), (1 - x, 1 - y)]
        local = [pltpu.make_async_copy(ins[w].at[mine], outs[w].at[mine], lsem.at[w]) for w in range(n)]
        for cp in local:
            cp.start()
        sends = []
        for w in range(n):
            for j, (px, py) in enumerate(chips):
                cp = pltpu.make_async_remote_copy(ins[w].at[2 * px + py], outs[w].at[mine],
                                                  send.at[w, j], recv.at[w, j],
                                                  device_id=(px, py, c), device_id_type=MESH)
                cp.start()
                sends.append(cp)
        for w in range(n):
            for j, (px, py) in enumerate(chips):
                pltpu.make_async_remote_copy(ins[w].at[mine], outs[w].at[2 * px + py],
                                             send.at[w, j], recv.at[w, j],
                                             device_id=(px, py, c), device_id_type=MESH).wait_recv()
        for cp in sends:
            cp.wait_send()
        for cp in local:
            cp.wait()

    hbm = pl.BlockSpec(memory_space=pl.ANY)
    return pl.pallas_call(
        body, name="chip_exchange_grads",
        in_specs=[hbm] * n, out_specs=[hbm] * n,
        out_shape=[jax.ShapeDtypeStruct(p.shape, p.dtype) for p in parts],
        scratch_shapes=[pltpu.SemaphoreType.DMA((n, 3)), pltpu.SemaphoreType.DMA((n, 3)),
                        pltpu.SemaphoreType.DMA((n,))],
    )(*parts)


def _adamw_math(g, w, m, v):
    m2 = B1 * m + (1.0 - B1) * g
    v2 = B2 * v + (1.0 - B2) * (g * g)
    delta = -LR * ((m2 / BC1) / (jnp.sqrt(v2 / BC2) + AEPS) + WD * w)
    return delta, m2, v2


def _adamw(parts, w, m, v, *, name, tr=128):
    rows, cols = w.shape

    def body(p_ref, w_ref, m_ref, v_ref, g_ref, d_ref, mo_ref, vo_ref):
        g = p_ref[0].astype(F32)
        for b in range(1, N_CHIP):
            g = g + p_ref[b].astype(F32)
        delta, m2, v2 = _adamw_math(g, w_ref[...], m_ref[...], v_ref[...])
        g_ref[...] = g
        d_ref[...] = delta
        mo_ref[...] = m2
        vo_ref[...] = v2

    spec = pl.BlockSpec((tr, cols), lambda i: (i, 0))
    shape = jax.ShapeDtypeStruct((rows, cols), F32)
    return pl.pallas_call(
        body, name=name, grid=(rows // tr,),
        in_specs=[pl.BlockSpec((N_CHIP, tr, cols), lambda i: (0, i, 0)), spec, spec, spec],
        out_specs=[spec] * 4, out_shape=[shape] * 4,
        compiler_params=_params(("parallel",)),
    )(parts, w, m, v)


def _small_update(part, w, m, v):
    rows = part.shape[0]

    def body(p_ref, w_ref, m_ref, v_ref, g_ref, d_ref, mo_ref, vo_ref, buf, send, recv):
        me, peers = _me_and_peers()
        buf[me] = p_ref[...]
        sends = []
        for k, (dev, _) in enumerate(peers):
            cp = pltpu.make_async_remote_copy(p_ref, buf.at[me], send.at[k], recv.at[k],
                                              device_id=dev, device_id_type=MESH)
            cp.start()
            sends.append(cp)
        for k, (dev, idx) in enumerate(peers):
            pltpu.make_async_remote_copy(p_ref, buf.at[idx], send.at[k], recv.at[k],
                                         device_id=dev, device_id_type=MESH).wait_recv()
        for cp in sends:
            cp.wait_send()
        g = buf[0]
        for b in range(1, N_DEV):
            g = g + buf[b]
        delta, m2, v2 = _adamw_math(g, w_ref[...], m_ref[...], v_ref[...])
        g_ref[...] = g
        d_ref[...] = delta
        mo_ref[...] = m2
        vo_ref[...] = v2

    vm = pl.BlockSpec(memory_space=pltpu.VMEM)
    shape = jax.ShapeDtypeStruct((rows, HD), F32)
    return pl.pallas_call(
        body, name="small_params_update",
        in_specs=[vm] * 4, out_specs=[vm] * 4, out_shape=[shape] * 4,
        scratch_shapes=[pltpu.VMEM((N_DEV, rows, HD), F32),
                        pltpu.SemaphoreType.DMA((N_DEV - 1,)), pltpu.SemaphoreType.DMA((N_DEV - 1,))],
    )(part, w, m, v)


def _pack_small(norm_mix, b_gate, qa, ka, qb, kb, rpb, norm_ffn):
    gains = jnp.concatenate([qa, ka, qb, kb, jnp.zeros((4, HD), F32)], axis=0)
    rpb_pad = jnp.pad(rpb.reshape(4 * (2 * WIN_R - 1), 2 * WIN_C - 1), ((0, 4), (0, HD - (2 * WIN_C - 1))))
    return jnp.concatenate([norm_mix.reshape(16, HD), b_gate.reshape(32, HD), gains, rpb_pad,
                            norm_ffn.reshape(16, HD)], axis=0)


def _unpack_small(p):
    norm_mix = p[0:16].reshape(1, D)
    b_gate = p[16:48].reshape(1, 2 * D)
    qa, ka, qb, kb = (p[48 + i:49 + i] for i in range(4))
    rpb = p[56:116, :2 * WIN_C - 1].reshape(1, 4, 2 * WIN_R - 1, 2 * WIN_C - 1)
    norm_ffn = p[120:136].reshape(1, D)
    return norm_mix, b_gate, qa, ka, qb, kb, rpb, norm_ffn


def kernel(x, norm_mix, w_in, b_gate, q_norm_a, k_norm_a, q_norm_b, k_norm_b, rpb_b, w_proj_a, w_proj_b, w_out, norm_ffn, w_up, w_down, loss_target, m_norm_mix, m_w_in, m_b_gate, m_q_norm_a, m_k_norm_a, m_q_norm_b, m_k_norm_b, m_rpb_b, m_w_proj_a, m_w_proj_b, m_w_out, m_norm_ffn, m_w_up, m_w_down, v_norm_mix, v_w_in, v_b_gate, v_q_norm_a, v_k_norm_a, v_q_norm_b, v_k_norm_b, v_rpb_b, v_w_proj_a, v_w_proj_b, v_w_out, v_norm_ffn, v_w_up, v_w_down):
    big_w = (w_in[0], w_proj_a[0], w_proj_b[0], w_out[0], w_up[0], w_down[0])
    big_m = (m_w_in[0], m_w_proj_a[0], m_w_proj_b[0], m_w_out[0], m_w_up[0], m_w_down[0])
    big_v = (v_w_in[0], v_w_proj_a[0], v_w_proj_b[0], v_w_out[0], v_w_up[0], v_w_down[0])
    names = ("w_in", "w_proj_a", "w_proj_b", "w_out", "w_up", "w_down")

    shards = [_cast_bf16(w) for w in big_w]
    g_in, = _gather_on_sequencer(shards[0:1], "gather_w_in")
    g_pa, g_pb, g_out = _gather_on_sequencer(shards[1:4], "gather_w_mix")
    g_up, = _gather_on_sequencer(shards[4:5], "gather_w_up")
    g_down, = _gather_on_sequencer(shards[5:6], "gather_w_down")
    small_w = _pack_small(norm_mix, b_gate, q_norm_a, k_norm_a, q_norm_b, k_norm_b, rpb_b, norm_ffn)
    small_m = _pack_small(m_norm_mix, m_b_gate, m_q_norm_a, m_k_norm_a, m_q_norm_b, m_k_norm_b, m_rpb_b, m_norm_ffn)
    small_v = _pack_small(v_norm_mix, v_b_gate, v_q_norm_a, v_k_norm_a, v_q_norm_b, v_k_norm_b, v_rpb_b, v_norm_ffn)

    loss, grad_x, big_g, small_g = _local_step(
        x[0], loss_target[0], norm_mix, b_gate, small_w[48:56], small_w[56:120], norm_ffn,
        g_in, g_pa, g_pb, g_out.reshape(D, D), g_up, g_down.reshape(D_FF, D))

    g_norm_mix, g_ba, g_bb, g_gains, g_rpb, g_norm_ffn = small_g
    small_part = jnp.concatenate([g_norm_mix.reshape(16, HD), g_ba.reshape(16, HD), g_bb.reshape(16, HD),
                                  g_gains, g_rpb, g_norm_ffn.reshape(16, HD)], axis=0)
    s_g, s_d, s_m, s_v = (_unpack_small(t) for t in _small_update(small_part, small_w, small_m, small_v))

    recv = [None] * 6
    for tag, members in (("w_down", (5,)), ("w_up", (4,)), ("w_mix", (3, 1, 2)), ("w_in", (0,))):
        got = _pair_exchange_on_sequencer([big_g[i] for i in members], f"pair_exchange_{tag}")
        sums = [_chip_sum(big_g[i], got[k], name=f"chip_sum_{names[i]}") for k, i in enumerate(members)]
        for i, r in zip(members, _chip_exchange_on_sequencer(sums, f"chip_exchange_{tag}")):
            recv[i] = r
    upd =[_adamw(recv[i], big_w[i], big_m[i], big_v[i], name=f"adamw_{names[i]}") for i in range(6)]
    b_g, b_d, b_m, b_v = ([u[j][None] for u in upd] for j in range(4))

    def order(small, big):
        nm, bg, qa, ka, qb, kb, rpb, nf = small
        w_in_, pa_, pb_, out_, up_, down_ = big
        return (nm, w_in_, bg, qa, ka, qb, kb, rpb, pa_, pb_, out_, nf, up_, down_)

    total = lax.psum(loss[0, 0], ("x", "y", "c"))
    return (total, grad_x[None], *order(s_g, b_g), *order(s_d, b_d), *order(s_m, b_m), *order(s_v, b_v))
```

```python
import functools

import jax
import jax.numpy as jnp
import numpy as np
from jax import lax
from jax.experimental import pallas as pl
from jax.experimental.pallas import tpu as pltpu
from jax.experimental.pallas import tpu_sc as plsc

F32 = jnp.float32
BF16 = jnp.bfloat16

N_DEV = 8
S = 2048
D = 2048
HD = 128
NH = 16
NH_A = 12
QKV = NH * HD
D_IN = 3 * QKV + 2 * D
D_BR = 512
D_FF = 4 * D
GRID_W = 64
ROWS = S // GRID_W
WIN_R = 8
WIN_C = 16
EPS = 1e-6
NEG = -1e30
SCALE = HD ** -0.5
ROPE_THETA = 10000.0
GROUPS_A = ((64, 1, 512), (256, 4, 768), (1024, 16, 2048))
QB = 256

LR, B1, B2, AEPS, WD, STEP = 0.001, 0.9, 0.999, 1e-08, 0.01, 10
BC1 = 1.0 - B1 ** STEP
BC2 = 1.0 - B2 ** STEP

VMEM_LIMIT = 56 * 1024 * 1024
MESH = pl.DeviceIdType.MESH

NN = (((1,), (0,)), ((), ()))
NT = (((1,), (1,)), ((), ()))
TN = (((0,), (0,)), ((), ()))


def _params(sem):
    return pltpu.CompilerParams(dimension_semantics=sem, vmem_limit_bytes=VMEM_LIMIT)


def _matmul(a, b, *, dims, grid, a_spec, b_spec, nk, epi, out_shape, out_specs, name,
            extra=(), extra_specs=(), acc_shape=None, after=()):
    n_extra = len(extra)

    def body(a_ref, b_ref, *rest):
        ex = rest[:n_extra]
        rest = rest[n_extra + len(after):]
        if nk == 1:
            outs = rest
            epi(lax.dot_general(a_ref[...], b_ref[...], dims, preferred_element_type=F32), ex, outs)
            return
        outs, acc = rest[:-1], rest[-1]
        k = pl.program_id(2)
        part = lax.dot_general(a_ref[...], b_ref[...], dims, preferred_element_type=F32)

        @pl.when(k == 0)
        def _():
            acc[...] = part

        @pl.when(k > 0)
        def _():
            acc[...] += part

        @pl.when(k == nk - 1)
        def _():
            epi(acc[...], ex, outs)

    scratch = [] if nk == 1 else [pltpu.VMEM(acc_shape, F32)]
    return pl.pallas_call(
        body, name=name, grid=grid,
        in_specs=[a_spec, b_spec, *extra_specs, *[pl.BlockSpec(memory_space=pl.ANY)] * len(after)],
        out_specs=out_specs, out_shape=out_shape, scratch_shapes=scratch,
        compiler_params=_params(("parallel", "parallel", "arbitrary")),
    )(a, b, *extra, *after)


def _epi_store(acc, ex, outs):
    outs[0][...] = acc.astype(outs[0].dtype)


def _epi_residual(acc, ex, outs):
    outs[0][...] = acc + ex[0][...]


def _mm_nn(a, b3, *, tm, tn, tk, name, out_dtype=F32, epi=_epi_store, extra=(), n_out=1,
           out_dtypes=None):
    m, kdim = a.shape
    g, _, ng = b3.shape
    n = g * ng
    npg = ng // tn
    nk = kdim // tk
    grid = (n // tn, m // tm, nk)
    tile = pl.BlockSpec((tm, tn), lambda j, i, k: (i, j))
    dts = out_dtypes or (out_dtype,) * n_out
    shapes = tuple(jax.ShapeDtypeStruct((m, n), dt) for dt in dts)
    return _matmul(
        a, b3, dims=NN, grid=grid, nk=nk, epi=epi, name=name,
        a_spec=pl.BlockSpec((tm, tk), lambda j, i, k: (i, k)),
        b_spec=pl.BlockSpec((None, tk, tn), lambda j, i, k: (j // npg, k, j % npg)),
        extra=extra, extra_specs=[tile] * len(extra),
        out_shape=shapes if len(dts) > 1 else shapes[0],
        out_specs=[tile] * len(dts) if len(dts) > 1 else tile,
        acc_shape=(tm, tn))


def _mm_nt(a, b3, *, tm, tn, tk, name, out_dtype=F32, epi=_epi_store, extra=(), after=()):
    m, kdim = a.shape
    g, n, kg = b3.shape
    kpg = kg // tk
    nk = kdim // tk
    grid = (n // tn, m // tm, nk)
    tile = pl.BlockSpec((tm, tn), lambda j, i, k: (i, j))
    return _matmul(
        a, b3, dims=NT, grid=grid, nk=nk, epi=epi, name=name,
        a_spec=pl.BlockSpec((tm, tk), lambda j, i, k: (i, k)),
        b_spec=pl.BlockSpec((None, tn, tk), lambda j, i, k: (k // kpg, j, k % kpg)),
        extra=extra, extra_specs=[tile] * len(extra), after=after,
        out_shape=jax.ShapeDtypeStruct((m, n), out_dtype), out_specs=tile,
        acc_shape=(tm, tn))


def _mm_tn(a, b, *, tm, tn, name, groups=1, out_dtype=BF16):
    t, m = a.shape
    _, n = b.shape
    ng = n // groups
    npg = ng // tn
    grid = (n // tn, m // tm, 1)
    return _matmul(
        a, b, dims=TN, grid=grid, nk=1, epi=_epi_store, name=name,
        a_spec=pl.BlockSpec((t, tm), lambda j, i, k: (0, i)),
        b_spec=pl.BlockSpec((t, tn), lambda j, i, k: (0, j)),
        out_shape=jax.ShapeDtypeStruct((groups, m, ng), out_dtype),
        out_specs=pl.BlockSpec((None, tm, tn), lambda j, i, k: (j // npg, i, j % npg)))


def _rms_fwd(x, g, *, name, tr=256):
    def body(x_ref, g_ref, y_ref, r_ref):
        xv = x_ref[...]
        r = lax.rsqrt(jnp.mean(xv * xv, axis=-1, keepdims=True) + EPS)
        y_ref[...] = (xv * r * g_ref[...]).astype(BF16)
        r_ref[...] = r

    row = pl.BlockSpec((tr, D), lambda i: (i, 0))
    return pl.pallas_call(
        body, name=name, grid=(S // tr,),
        in_specs=[row, pl.BlockSpec((1, D), lambda i: (0, 0))],
        out_specs=[row, pl.BlockSpec((tr, 1), lambda i: (i, 0))],
        out_shape=[jax.ShapeDtypeStruct((S, D), BF16), jax.ShapeDtypeStruct((S, 1), F32)],
        compiler_params=_params(("parallel",)),
    )(x, g)


def _rms_bwd(dy, x, rstd, g, resid, *, name, tr=256):
    def body(dy_ref, x_ref, r_ref, g_ref, res_ref, dx_ref, dxb_ref, dg_ref):
        r = r_ref[...]
        xh = x_ref[...] * r
        dyv = dy_ref[...]
        t = dyv * g_ref[...]
        dx = r * (t - xh * jnp.mean(t * xh, axis=-1, keepdims=True)) + res_ref[...]
        dx_ref[...] = dx
        dxb_ref[...] = dx.astype(BF16)
        part = jnp.sum(dyv * xh, axis=0, keepdims=True)

        @pl.when(pl.program_id(0) == 0)
        def _():
            dg_ref[...] = part

        @pl.when(pl.program_id(0) > 0)
        def _():
            dg_ref[...] += part

    row = pl.BlockSpec((tr, D), lambda i: (i, 0))
    vec = pl.BlockSpec((1, D), lambda i: (0, 0))
    return pl.pallas_call(
        body, name=name, grid=(S // tr,),
        in_specs=[row, row, pl.BlockSpec((tr, 1), lambda i: (i, 0)), vec, row],
        out_specs=[row, row, vec],
        out_shape=[jax.ShapeDtypeStruct((S, D), F32), jax.ShapeDtypeStruct((S, D), BF16),
                   jax.ShapeDtypeStruct((1, D), F32)],
        compiler_params=_params(("arbitrary",)),
    )(dy, x, rstd, g, resid)


def _loss_head(h2, target, *, tr=256):
    def body(h_ref, t_ref, dy_ref, dyb_ref, loss_ref):
        e = h_ref[...] - t_ref[...]
        dy = e * (1.0 / D)
        dy_ref[...] = dy
        dyb_ref[...] = dy.astype(BF16)
        part = (0.5 / D) * jnp.sum(jnp.sum(e * e, axis=-1, keepdims=True), axis=0, keepdims=True)

        @pl.when(pl.program_id(0) == 0)
        def _():
            loss_ref[...] = part

        @pl.when(pl.program_id(0) > 0)
        def _():
            loss_ref[...] += part

    row = pl.BlockSpec((tr, D), lambda i: (i, 0))
    return pl.pallas_call(
        body, name="loss_head", grid=(S // tr,),
        in_specs=[row, row],
        out_specs=[row, row, pl.BlockSpec((1, 1), lambda i: (0, 0))],
        out_shape=[jax.ShapeDtypeStruct((S, D), F32), jax.ShapeDtypeStruct((S, D), BF16),
                   jax.ShapeDtypeStruct((1, 1), F32)],
        compiler_params=_params(("arbitrary",)),
    )(h2, target)


def _rope_tables():
    pos = np.arange(S, dtype=np.float32)
    inv = (ROPE_THETA ** (-np.arange(0, HD, 2, dtype=np.float32) / HD)).astype(np.float32)
    ang = pos[:, None] * inv[None, :]
    cos, sin = np.cos(ang), np.sin(ang)
    return (jnp.asarray(np.concatenate([cos, cos], axis=-1), F32),
            jnp.asarray(np.concatenate([-sin, sin], axis=-1), F32))


def _swap_halves(t):
    return pltpu.roll(t, HD // 2, axis=1)


def _qk_prep(proj, gains, cos2, sin2, *, tr=256):
    def body(q_ref, k_ref, v_ref, g_ref, c_ref, s_ref, qn_ref, kn_ref, vb_ref):
        cos, sin = c_ref[...], s_ref[...]
        for src, dst, row_a, row_b in ((q_ref, qn_ref, 0, 2), (k_ref, kn_ref, 1, 3)):
            for h in range(NH):
                cols = slice(h * HD, (h + 1) * HD)
                t = src[:, cols]
                r = lax.rsqrt(jnp.mean(t * t, axis=-1, keepdims=True) + EPS)
                if h < NH_A:
                    y = t * r * g_ref[row_a:row_a + 1, :]
                    y = y * cos + _swap_halves(y) * sin
                else:
                    y = t * r * g_ref[row_b:row_b + 1, :]
                dst[:, cols] = y.astype(BF16)
        vb_ref[...] = v_ref[...].astype(BF16)

    def blk(c):
        return pl.BlockSpec((tr, QKV), lambda i: (i, c))
    tab = pl.BlockSpec((tr, HD), lambda i: (i, 0))
    out = pl.BlockSpec((tr, QKV), lambda i: (i, 0))
    return pl.pallas_call(
        body, name="qk_prep", grid=(S // tr,),
        in_specs=[blk(0), blk(1), blk(2), pl.BlockSpec((8, HD), lambda i: (0, 0)), tab, tab],
        out_specs=[out, out, out],
        out_shape=[jax.ShapeDtypeStruct((S, QKV), BF16)] * 3,
        compiler_params=_params(("parallel",)),
    )(proj, proj, proj, gains, cos2, sin2)


def _qk_prep_bwd(dproj, proj, gains, cos2, sin2, dq_parts, dk_parts, dv_parts, *, tr=256):
    def body(dp_in, q_ref, k_ref, g_ref, c_ref, s_ref, *rest):
        dqs, dks, dvs = rest[0:4], rest[4:8], rest[8:12]
        dp_out, dg_ref = rest[12:14]
        del dp_in
        cos, sin = c_ref[...], s_ref[...]
        dg_rows = []
        for src, grads, base, row_a, row_b in ((q_ref, dqs, 0, 0, 2), (k_ref, dks, QKV, 1, 3)):
            dg_a = jnp.zeros((1, HD), F32)
            dg_b = jnp.zeros((1, HD), F32)
            for h in range(NH):
                cols = slice(h * HD, (h + 1) * HD)
                t = src[:, cols]
                dy = grads[h // 4][:, (h % 4) * HD:(h % 4 + 1) * HD]
                r = lax.rsqrt(jnp.mean(t * t, axis=-1, keepdims=True) + EPS)
                xh = t * r
                if h < NH_A:
                    dy = dy * cos - _swap_halves(dy) * sin
                    gain = g_ref[row_a:row_a + 1, :]
                    dg_a = dg_a + jnp.sum(dy * xh, axis=0, keepdims=True)
                else:
                    gain = g_ref[row_b:row_b + 1, :]
                    dg_b = dg_b + jnp.sum(dy * xh, axis=0, keepdims=True)
                u = dy * gain
                dx = r * (u - xh * jnp.mean(u * xh, axis=-1, keepdims=True))
                dp_out[:, base + h * HD:base + (h + 1) * HD] = dx.astype(BF16)
            dg_rows += [(row_a, dg_a), (row_b, dg_b)]
        for g4 in range(4):
            dp_out[:, 2 * QKV + g4 * D_BR:2 * QKV + (g4 + 1) * D_BR] = dvs[g4][...].astype(BF16)

        first = pl.program_id(0) == 0

        @pl.when(first)
        def _():
            dg_ref[...] = jnp.zeros((8, HD), F32)

        for row, val in dg_rows:
            dg_ref[row:row + 1, :] += val

    def blk(c):
        return pl.BlockSpec((tr, QKV), lambda i: (i, c))
    tab = pl.BlockSpec((tr, HD), lambda i: (i, 0))
    part = pl.BlockSpec((tr, D_BR), lambda i: (i, 0))
    gain_spec = pl.BlockSpec((8, HD), lambda i: (0, 0))
    return pl.pallas_call(
        body, name="qk_prep_bwd", grid=(S // tr,),
        in_specs=[pl.BlockSpec(memory_space=pl.ANY), blk(0), blk(1), gain_spec, tab, tab] + [part] * 12,
        out_specs=[pl.BlockSpec((tr, 3 * QKV), lambda i: (i, 0)), gain_spec],
        out_shape=[jax.ShapeDtypeStruct((S, D_IN), BF16), jax.ShapeDtypeStruct((8, HD), F32)],
        input_output_aliases={0: 0},
        compiler_params=_params(("arbitrary",)),
    )(dproj, proj, proj, gains, cos2, sin2, *dq_parts, *dk_parts, *dv_parts)


def _gate_fwd(proj, b_gate, ya, yb, *, tr=256):
    def body(la_ref, lb_ref, ba_ref, bb_ref, ya_ref, yb_ref, o_ref):
        ga = jax.nn.sigmoid(la_ref[...] + ba_ref[...])
        gb = jax.nn.sigmoid(lb_ref[...] + bb_ref[...])
        o_ref[...] = (ga * ya_ref[...] + gb * yb_ref[...]).astype(BF16)

    row = pl.BlockSpec((tr, D), lambda i: (i, 0))
    return pl.pallas_call(
        body, name="gate_fwd", grid=(S // tr,),
        in_specs=[pl.BlockSpec((tr, D), lambda i: (i, 3)), pl.BlockSpec((tr, D), lambda i: (i, 4)),
                  pl.BlockSpec((1, D), lambda i: (0, 0)), pl.BlockSpec((1, D), lambda i: (0, 1)),
                  row, row],
        out_specs=row, out_shape=jax.ShapeDtypeStruct((S, D), BF16),
        compiler_params=_params(("parallel",)),
    )(proj, proj, b_gate, b_gate, ya, yb)


def _gate_bwd(branch, dmixed, proj, b_gate, y, dproj, *, tr=256):
    aliased = dproj is not None

    def body(dm_ref, l_ref, b_ref, y_ref, *rest):
        dy_ref, dp_ref, db_ref = rest[-3:]
        g = jax.nn.sigmoid(l_ref[...] + b_ref[...])
        dm = dm_ref[...]
        dy_ref[...] = (dm * g).astype(BF16)
        dl = dm * y_ref[...] * g * (1.0 - g)
        dp_ref[...] = dl.astype(BF16)
        part = jnp.sum(dl, axis=0, keepdims=True)

        @pl.when(pl.program_id(0) == 0)
        def _():
            db_ref[...] = part

        @pl.when(pl.program_id(0) > 0)
        def _():
            db_ref[...] += part

    row = pl.BlockSpec((tr, D), lambda i: (i, 0))
    col = pl.BlockSpec((tr, D), lambda i: (i, 3 + branch))
    vec = pl.BlockSpec((1, D), lambda i: (0, 0))
    return pl.pallas_call(
        body, name=f"gate_bwd_{branch}", grid=(S // tr,),
        in_specs=[row, col, pl.BlockSpec((1, D), lambda i: (0, branch)), row]
        + ([pl.BlockSpec(memory_space=pl.ANY)] if aliased else []),
        out_specs=[row, col, vec],
        out_shape=[jax.ShapeDtypeStruct((S, D), BF16), jax.ShapeDtypeStruct((S, D_IN), BF16),
                   jax.ShapeDtypeStruct((1, D), F32)],
        input_output_aliases={4: 1} if aliased else {},
        compiler_params=_params(("arbitrary",)),
    )(dmixed, proj, b_gate, y, *([dproj] if aliased else []))


def _window_start(t0, wk):
    if wk == S:
        return 0
    return pl.multiple_of(jnp.clip(t0 - (wk - QB) // 2, 0, S - wk), 128)


def _scores_a(q, kw, t0, start, hs, dil, wk):
    s = lax.dot_general(q, kw, NT, preferred_element_type=F32) * SCALE
    qpos = t0 + lax.broadcasted_iota(jnp.int32, (QB, 1), 0)
    kpos = start + lax.broadcasted_iota(jnp.int32, (1, wk), 1)
    diff = kpos - qpos
    keep = (jnp.abs(diff) <= hs) & ((diff & (dil - 1)) == 0)
    return jnp.where(keep, s, NEG)


def _attn_a_fwd(qn, kn, vb, gi):
    hs, dil, wk = GROUPS_A[gi]

    def body(q_ref, k_ref, v_ref, o_ref, lse_ref):
        t0 = pl.program_id(1) * QB
        start = _window_start(t0, wk)
        s = _scores_a(q_ref[...], k_ref[pl.ds(start, wk), :], t0, start, hs, dil, wk)
        m = jnp.max(s, axis=-1, keepdims=True)
        p = jnp.exp(s - m)
        l = jnp.sum(p, axis=-1, keepdims=True)
        o = lax.dot_general(p.astype(BF16), v_ref[pl.ds(start, wk), :], NN, preferred_element_type=F32)
        o_ref[...] = o / l
        lse_ref[...] = m + jnp.log(l)

    full = pl.BlockSpec((S, HD), lambda h, i: (0, 4 * gi + h))
    return pl.pallas_call(
        body, name=f"attn_a_fwd_{gi}", grid=(4, S // QB),
        in_specs=[pl.BlockSpec((QB, HD), lambda h, i: (i, 4 * gi + h)), full, full],
        out_specs=[pl.BlockSpec((QB, HD), lambda h, i: (i, h)),
                   pl.BlockSpec((None, QB, 1), lambda h, i: (h, i, 0))],
        out_shape=[jax.ShapeDtypeStruct((S, D_BR), F32), jax.ShapeDtypeStruct((4, S, 1), F32)],
        compiler_params=_params(("parallel", "parallel")),
    )(qn, kn, vb)


def _combine_a(os, lses, *, tr=256):
    def body(o0, o1, o2, l0, l1, l2, oa_ref, lse_ref):
        for h in range(4):
            cols = slice(h * HD, (h + 1) * HD)
            a, b, c = l0[h], l1[h], l2[h]
            m = jnp.maximum(jnp.maximum(a, b), c)
            wa, wb, wc = jnp.exp(a - m), jnp.exp(b - m), jnp.exp(c - m)
            tot = wa + wb + wc
            oa_ref[:, cols] = ((wa * o0[:, cols] + wb * o1[:, cols] + wc * o2[:, cols]) / tot).astype(BF16)
            lse_ref[h] = m + jnp.log(tot)

    row = pl.BlockSpec((tr, D_BR), lambda i: (i, 0))
    stat = pl.BlockSpec((4, tr, 1), lambda i: (0, i, 0))
    return pl.pallas_call(
        body, name="combine_a", grid=(S // tr,),
        in_specs=[row] * 3 + [stat] * 3, out_specs=[row, stat],
        out_shape=[jax.ShapeDtypeStruct((S, D_BR), BF16), jax.ShapeDtypeStruct((4, S, 1), F32)],
        compiler_params=_params(("parallel",)),
    )(*os, *lses)


def _attn_a_bwd(qn, kn, vb, oa, doa, lse, gi):
    hs, dil, wk = GROUPS_A[gi]

    def body(q_ref, k_ref, v_ref, o_ref, do_ref, lse_ref, dq_ref, dk_ref, dv_ref):
        @pl.when(pl.program_id(1) == 0)
        def _():
            dk_ref[...] = jnp.zeros((S, HD), F32)
            dv_ref[...] = jnp.zeros((S, HD), F32)

        t0 = pl.program_id(1) * QB
        start = _window_start(t0, wk)
        q = q_ref[...]
        kw = k_ref[pl.ds(start, wk), :]
        vw = v_ref[pl.ds(start, wk), :]
        p = jnp.exp(_scores_a(q, kw, t0, start, hs, dil, wk) - lse_ref[...])
        do = do_ref[...]
        dob = do.astype(BF16)
        dsum = jnp.sum(do * o_ref[...].astype(F32), axis=-1, keepdims=True)
        dp = lax.dot_general(dob, vw, NT, preferred_element_type=F32)
        ds = (p * (dp - dsum) * SCALE).astype(BF16)
        dq_ref[...] = lax.dot_general(ds, kw, NN, preferred_element_type=F32)
        dk_ref[pl.ds(start, wk), :] += lax.dot_general(ds, q, TN, preferred_element_type=F32)
        dv_ref[pl.ds(start, wk), :] += lax.dot_general(p.astype(BF16), dob, TN, preferred_element_type=F32)

    full = pl.BlockSpec((S, HD), lambda h, i: (0, 4 * gi + h))
    blk = pl.BlockSpec((QB, HD), lambda h, i: (i, h))
    acc = pl.BlockSpec((S, HD), lambda h, i: (0, h))
    shape = jax.ShapeDtypeStruct((S, D_BR), F32)
    return pl.pallas_call(
        body, name=f"attn_a_bwd_{gi}", grid=(4, S // QB),
        in_specs=[pl.BlockSpec((QB, HD), lambda h, i: (i, 4 * gi + h)), full, full, blk, blk,
                  pl.BlockSpec((None, QB, 1), lambda h, i: (h, i, 0))],
        out_specs=[blk, acc, acc], out_shape=[shape, shape, shape],
        compiler_params=_params(("parallel", "arbitrary")),
    )(qn, kn, vb, oa, doa, lse)


KEYS_B = WIN_R * GRID_W
N_OFF = WIN_R


def _bias_constants():
    q = np.arange(GRID_W)[:, None]
    kc = np.arange(GRID_W)[None, :]
    dc = np.clip(kc - q, -(WIN_C - 1), WIN_C - 1) + (WIN_C - 1)
    expand = np.zeros((HD, GRID_W * GRID_W), np.float32)
    expand[dc.reshape(-1), np.arange(GRID_W * GRID_W)] = 1.0
    cs = np.clip(q - WIN_C // 2, 0, GRID_W - WIN_C)
    keep = ((kc >= cs) & (kc < cs + WIN_C)).reshape(1, -1).astype(np.float32)
    sel = np.zeros((64, 4 * N_OFF * WIN_R), np.float32)
    for h in range(4):
        for off in range(N_OFF):
            for j in range(WIN_R):
                sel[h * (2 * WIN_R - 1) + off + j, (h * N_OFF + off) * WIN_R + j] = 1.0
    return jnp.asarray(expand), jnp.asarray(keep), jnp.asarray(sel)


def _bias_expand(rpb_pad, expand, keep, sel):
    def body(r_ref, e_ref, k_ref, s_ref, o_ref):
        t = lax.dot_general(r_ref[...], e_ref[...], NN, precision=lax.Precision.HIGHEST,
                            preferred_element_type=F32)
        rows = lax.dot_general(s_ref[...], t, TN, precision=lax.Precision.HIGHEST,
                               preferred_element_type=F32)
        o_ref[...] = jnp.where(k_ref[...] > 0.5, rows, NEG)

    return pl.pallas_call(
        body, name="bias_expand",
        out_shape=jax.ShapeDtypeStruct((4 * N_OFF * WIN_R, GRID_W * GRID_W), F32),
        compiler_params=pltpu.CompilerParams(vmem_limit_bytes=VMEM_LIMIT),
    )(rpb_pad, expand, keep, sel)


def _bias_reduce(dbias_rows, expand, sel):
    def body(x_ref, e_ref, s_ref, o_ref):
        z = lax.dot_general(x_ref[...], e_ref[...], NT, precision=lax.Precision.HIGHEST,
                            preferred_element_type=F32)
        o_ref[...] = lax.dot_general(s_ref[...], z, NN, precision=lax.Precision.HIGHEST,
                                     preferred_element_type=F32)

    return pl.pallas_call(
        body, name="bias_reduce", out_shape=jax.ShapeDtypeStruct((64, HD), F32),
        compiler_params=pltpu.CompilerParams(vmem_limit_bytes=VMEM_LIMIT),
    )(dbias_rows, expand, sel)


def _rows_to_tab(rows):
    t = rows.reshape(4, N_OFF, WIN_R, GRID_W, GRID_W)
    return t.transpose(0, 1, 3, 2, 4).reshape(4, N_OFF, GRID_W, KEYS_B)


def _tab_to_rows(tab):
    t = tab.reshape(4, N_OFF, GRID_W, WIN_R, GRID_W)
    return t.transpose(0, 1, 3, 2, 4).reshape(4 * N_OFF * WIN_R, GRID_W * GRID_W)


def _row_window(r):
    r0 = jnp.clip(r - WIN_R // 2, 0, ROWS - WIN_R)
    off = r0 + (WIN_R - 1) - r
    return pl.multiple_of(r * GRID_W, GRID_W), pl.multiple_of(r0 * GRID_W, GRID_W), off


def _attn_b_fwd(qn, kn, vb, bias_tab):
    def body(q_ref, k_ref, v_ref, b_ref, o_ref, lse_ref):
        def row(r, carry):
            qs, ks, off = _row_window(r)
            q = q_ref[pl.ds(qs, GRID_W), :]
            s = lax.dot_general(q, k_ref[pl.ds(ks, KEYS_B), :], NT, preferred_element_type=F32) * SCALE
            s = s + b_ref[off]
            m = jnp.max(s, axis=-1, keepdims=True)
            p = jnp.exp(s - m)
            l = jnp.sum(p, axis=-1, keepdims=True)
            o = lax.dot_general(p.astype(BF16), v_ref[pl.ds(ks, KEYS_B), :], NN, preferred_element_type=F32)
            o_ref[pl.ds(qs, GRID_W), :] = (o / l).astype(BF16)
            lse_ref[pl.ds(qs, GRID_W), :] = m + jnp.log(l)
            return carry

        lax.fori_loop(0, ROWS, row, 0)

    full = pl.BlockSpec((S, HD), lambda h: (0, NH_A + h))
    return pl.pallas_call(
        body, name="attn_b_fwd", grid=(4,),
        in_specs=[full, full, full, pl.BlockSpec((None, N_OFF, GRID_W, KEYS_B), lambda h: (h, 0, 0, 0))],
        out_specs=[pl.BlockSpec((S, HD), lambda h: (0, h)), pl.BlockSpec((None, S, 1), lambda h: (h, 0, 0))],
        out_shape=[jax.ShapeDtypeStruct((S, D_BR), BF16), jax.ShapeDtypeStruct((4, S, 1), F32)],
        compiler_params=_params(("parallel",)),
    )(qn, kn, vb, bias_tab)


def _attn_b_bwd(qn, kn, vb, bias_tab, ob, dob, lse):
    def body(q_ref, k_ref, v_ref, b_ref, o_ref, do_ref, lse_ref, dq_ref, dk_ref, dv_ref, db_ref):
        dk_ref[...] = jnp.zeros((S, HD), F32)
        dv_ref[...] = jnp.zeros((S, HD), F32)
        db_ref[...] = jnp.zeros((N_OFF, GRID_W, KEYS_B), F32)

        def row(r, carry):
            qs, ks, off = _row_window(r)
            rows = pl.ds(qs, GRID_W)
            keys = pl.ds(ks, KEYS_B)
            q = q_ref[rows, :]
            kw = k_ref[keys, :]
            s = lax.dot_general(q, kw, NT, preferred_element_type=F32) * SCALE + b_ref[off]
            p = jnp.exp(s - lse_ref[rows, :])
            do = do_ref[rows, :]
            dobf = do.astype(BF16)
            dsum = jnp.sum(do * o_ref[rows, :].astype(F32), axis=-1, keepdims=True)
            dp = lax.dot_general(dobf, v_ref[keys, :], NT, preferred_element_type=F32)
            ds = p * (dp - dsum)
            db_ref[off] += ds
            dsb = (ds * SCALE).astype(BF16)
            dq_ref[rows, :] = lax.dot_general(dsb, kw, NN, preferred_element_type=F32)
            dk_ref[keys, :] += lax.dot_general(dsb, q, TN, preferred_element_type=F32)
            dv_ref[keys, :] += lax.dot_general(p.astype(BF16), dobf, TN, preferred_element_type=F32)
            return carry

        lax.fori_loop(0, ROWS, row, 0)

    full = pl.BlockSpec((S, HD), lambda h: (0, NH_A + h))
    slot = pl.BlockSpec((S, HD), lambda h: (0, h))
    tab = pl.BlockSpec((None, N_OFF, GRID_W, KEYS_B), lambda h: (h, 0, 0, 0))
    shape = jax.ShapeDtypeStruct((S, D_BR), F32)
    return pl.pallas_call(
        body, name="attn_b_bwd", grid=(4,),
        in_specs=[full, full, full, tab, slot, slot, pl.BlockSpec((None, S, 1), lambda h: (h, 0, 0))],
        out_specs=[slot, slot, slot, tab],
        out_shape=[shape, shape, shape, jax.ShapeDtypeStruct((4, N_OFF, GRID_W, KEYS_B), F32)],
        compiler_params=_params(("parallel",)),
    )(qn, kn, vb, bias_tab, ob, dob, lse)


def _epi_relu_sq(acc, ex, outs):
    u = jnp.maximum(acc, 0.0)
    outs[0][...] = u.astype(BF16)
    outs[1][...] = (u * u).astype(BF16)


def _epi_relu_sq_bwd(acc, ex, outs):
    outs[0][...] = (acc * (2.0 * ex[0][...].astype(F32))).astype(BF16)


def _local_step(x, target, norm_mix, b_gate, gains, rpb_pad, norm_ffn,
                w_in, w_pa, w_pb, w_out, w_up, w_down, on_grads):
    cos2, sin2 = _rope_tables()
    expand, keep, sel = _bias_constants()
    w_out3, w_down3 = w_out[None], w_down[None]

    xn, rstd1 = _rms_fwd(x, norm_mix, name="rms_mix")
    proj = _mm_nn(xn, w_in, tm=512, tn=1280, tk=D, name="proj")
    qn, kn, vb = _qk_prep(proj, gains, cos2, sin2)
    fwd_a = [_attn_a_fwd(qn, kn, vb, gi) for gi in range(3)]
    oa, lse_a = _combine_a([o for o, _ in fwd_a], [l for _, l in fwd_a])
    bias_tab = _rows_to_tab(_bias_expand(rpb_pad, expand, keep, sel))
    ob, lse_b = _attn_b_fwd(qn, kn, vb, bias_tab)
    ya = _mm_nn(oa, w_pa, tm=1024, tn=256, tk=D_BR, name="proj_a")
    yb = _mm_nn(ob, w_pb, tm=1024, tn=256, tk=D_BR, name="proj_b")
    mixed = _gate_fwd(proj, b_gate, ya, yb)
    h1 = _mm_nn(mixed, w_out3, tm=512, tn=512, tk=D, name="out_proj", epi=_epi_residual, extra=(x,))
    hn, rstd2 = _rms_fwd(h1, norm_ffn, name="rms_ffn")
    u, usq = _mm_nn(hn, w_up, tm=512, tn=1024, tk=D, name="ffn_up", epi=_epi_relu_sq,
                    out_dtypes=(BF16, BF16))
    h2 = _mm_nn(usq, w_down3, tm=512, tn=1024, tk=D, name="ffn_down", epi=_epi_residual, extra=(h1,))
    dy, dyb, loss = _loss_head(h2, target)

    g_down = _mm_tn(usq, dyb, tm=512, tn=1024, name="grad_w_down")
    sent = on_grads("w_down", {5: g_down.reshape(N_DEV, D_FF // N_DEV, D)})
    du = _mm_nt(dyb, w_down3, tm=512, tn=1024, tk=D, name="ffn_down_bwd", out_dtype=BF16,
                epi=_epi_relu_sq_bwd, extra=(u,), after=sent)
    g_up = _mm_tn(hn, du, tm=512, tn=1024, groups=N_DEV, name="grad_w_up")
    sent = on_grads("w_up", {4: g_up})
    dhn = _mm_nt(du, w_up, tm=512, tn=1024, tk=1024, name="ffn_up_bwd", after=sent)
    dh1, dh1b, g_norm_ffn = _rms_bwd(dhn, h1, rstd2, norm_ffn, dy, name="rms_ffn_bwd")

    g_out = _mm_tn(mixed, dh1b, tm=512, tn=1024, name="grad_w_out")
    dmixed = _mm_nt(dh1b, w_out3, tm=512, tn=512, tk=D, name="out_proj_bwd")
    dya, dproj, g_ba = _gate_bwd(0, dmixed, proj, b_gate, ya, None)
    dyb2, dproj, g_bb = _gate_bwd(1, dmixed, proj, b_gate, yb, dproj)
    g_pa = _mm_tn(oa, dya, tm=512, tn=256, groups=N_DEV, name="grad_w_proj_a")
    g_pb = _mm_tn(ob, dyb2, tm=512, tn=256, groups=N_DEV, name="grad_w_proj_b")
    sent = on_grads("w_mix", {3: g_out.reshape(N_DEV, D // N_DEV, D), 1: g_pa, 2: g_pb})
    doa = _mm_nt(dya, w_pa, tm=1024, tn=512, tk=256, name="proj_a_bwd", after=sent)
    dob = _mm_nt(dyb2, w_pb, tm=1024, tn=512, tk=256, name="proj_b_bwd")
    bwd = [_attn_a_bwd(qn, kn, vb, oa, doa, lse_a, gi) for gi in range(3)]
    dqb, dkb, dvb, dbias = _attn_b_bwd(qn, kn, vb, bias_tab, ob, dob, lse_b)
    g_rpb = _bias_reduce(_tab_to_rows(dbias), expand, sel)
    dproj, g_gains = _qk_prep_bwd(dproj, proj, gains, cos2, sin2,
                                  [b[0] for b in bwd] + [dqb], [b[1] for b in bwd] + [dkb],
                                  [b[2] for b in bwd] + [dvb])
    g_in = _mm_tn(xn, dproj, tm=512, tn=1280, groups=N_DEV, name="grad_w_in")
    sent = on_grads("w_in", {0: g_in})
    dxn = _mm_nt(dproj, w_in, tm=512, tn=1024, tk=1280, name="proj_bwd", after=sent)
    grad_x, _, g_norm_mix = _rms_bwd(dxn, x, rstd1, norm_mix, dh1, name="rms_mix_bwd")

    small = (g_norm_mix, g_ba, g_bb, g_gains, g_rpb, g_norm_ffn)
    return loss, grad_x, small


def _cast_bf16(w, *, tr=256):
    rows, cols = w.shape
    tr = min(tr, rows)

    def body(w_ref, o_ref):
        o_ref[...] = w_ref[...].astype(BF16)

    spec = pl.BlockSpec((tr, cols), lambda i: (i, 0))
    return pl.pallas_call(
        body, name=f"cast_{rows}x{cols}", grid=(rows // tr,), in_specs=[spec], out_specs=spec,
        out_shape=jax.ShapeDtypeStruct((rows, cols), BF16), compiler_params=_params(("parallel",)),
    )(w)


def _me_and_peers():
    x, y, c = lax.axis_index("x"), lax.axis_index("y"), lax.axis_index("c")
    me = 4 * x + 2 * y + c
    peers = []
    for k in range(1, N_DEV):
        px = 1 - x if k & 4 else x
        py = 1 - y if k & 2 else y
        pc = 1 - c if k & 1 else c
        peers.append(((px, py, pc), 4 * px + 2 * py + pc))
    return me, peers


def _gather_on_sequencer(shards, name):
    n = len(shards)
    hbm = pltpu.MemorySpace.HBM
    ins = [jax.new_ref(s, memory_space=hbm) for s in shards]
    outs = [jax.empty_ref(jax.ShapeDtypeStruct((N_DEV,) + s.shape, s.dtype), memory_space=hbm) for s in shards]

    @pl.kernel(mesh=plsc.ScalarSubcoreMesh(axis_name="seq", num_cores=1), name=name,
               scratch_types=(pltpu.SemaphoreType.DMA((n, N_DEV - 1)), pltpu.SemaphoreType.DMA((n, N_DEV - 1)),
                              pltpu.SemaphoreType.DMA((n,))),
               compiler_params=pltpu.CompilerParams(collective_id=0))
    def launch(send, recv, lsem):
        x, y, c = lax.axis_index("x"), lax.axis_index("y"), lax.axis_index("c")
        me, sibling = (x, y, c), (x, y, 1 - c)
        chips = [(1 - x, y), (x, 1 - y), (1 - x, 1 - y)]
        barrier = pltpu.get_barrier_semaphore()
        for peer in [sibling] + [(*chip, c) for chip in chips]:
            pl.semaphore_signal(barrier, inc=1, device_id=peer, device_id_type=MESH)
        pl.semaphore_wait(barrier, 4)

        def copy(w, k, block, to, src=None):
            px, py, pc = block
            dst = outs[w].at[4 * px + 2 * py + pc]
            return pltpu.make_async_remote_copy(dst if src is None else src, dst, send.at[w, k], recv.at[w, k],
                                                device_id=to, device_id_type=MESH)

        local = [pltpu.make_async_copy(ins[w], outs[w].at[4 * x + 2 * y + c], lsem.at[w]) for w in range(n)]
        for cp in local:
            cp.start()
        first = []
        for w in range(n):
            first += [copy(w, 1 + j, me, (*chip, c), src=ins[w]) for j, chip in enumerate(chips)]
            first.append(copy(w, 0, me, sibling, src=ins[w]))
        for cp in first:
            cp.start()
        passed = []
        for w in range(n):
            for j, chip in enumerate(chips):
                copy(w, 1 + j, (*chip, c), me).wait_recv()
                cp = copy(w, 4 + j, (*chip, c), sibling)
                cp.start()
                passed.append(cp)
        for w in range(n):
            copy(w, 0, sibling, me).wait_recv()
            for j, chip in enumerate(chips):
                copy(w, 4 + j, (*chip, 1 - c), me).wait_recv()
        for cp in first + passed:
            cp.wait_send()
        for cp in local:
            cp.wait()

    launch()
    return [o[...] for o in outs]


N_CHIP = 4
CHIPS = ((0, 0), (0, 1), (1, 0), (1, 1))


def _sequencer(name, n_sems, collective_id):
    return functools.partial(
        pl.kernel, mesh=plsc.ScalarSubcoreMesh(axis_name="seq", num_cores=1), name=name,
        scratch_types=tuple(pltpu.SemaphoreType.DMA(s) for s in n_sems),
        compiler_params=pltpu.CompilerParams(collective_id=collective_id))


def _handshake(peers):
    barrier = pltpu.get_barrier_semaphore()
    for peer in peers:
        pl.semaphore_signal(barrier, inc=1, device_id=peer, device_id_type=MESH)
    pl.semaphore_wait(barrier, len(peers))


def _chip_exchange_on_sequencer(parts, name):
    n = len(parts)
    hbm = pltpu.MemorySpace.HBM
    ins = [jax.new_ref(p, memory_space=hbm) for p in parts]
    outs = [jax.empty_ref(jax.ShapeDtypeStruct(p.shape, p.dtype), memory_space=hbm) for p in parts]

    @_sequencer(name, ((n, 3), (n, 3), (n,)), 2)
    def launch(send, recv, lsem):
        x, y, c = lax.axis_index("x"), lax.axis_index("y"), lax.axis_index("c")
        mine = 2 * x + y
        chips = [(1 - x, y), (x, 1 - y), (1 - x, 1 - y)]
        _handshake([(*chip, c) for chip in chips])
        local = [pltpu.make_async_copy(ins[w].at[mine], outs[w].at[mine], lsem.at[w]) for w in range(n)]
        for cp in local:
            cp.start()
        sends = []
        for w in range(n):
            for j, (px, py) in enumerate(chips):
                cp = pltpu.make_async_remote_copy(ins[w].at[2 * px + py], outs[w].at[mine],
                                                  send.at[w, j], recv.at[w, j],
                                                  device_id=(px, py, c), device_id_type=MESH)
                cp.start()
                sends.append(cp)
        for w in range(n):
            for j, (px, py) in enumerate(chips):
                pltpu.make_async_remote_copy(ins[w].at[mine], outs[w].at[2 * px + py],
                                             send.at[w, j], recv.at[w, j],
                                             device_id=(px, py, c), device_id_type=MESH).wait_recv()
        for cp in sends:
            cp.wait_send()
        for cp in local:
            cp.wait()

    launch()
    return [o[...] for o in outs]


PAIR_CHUNK_BYTES = 3 * 1024 * 1024


def _pair_sum(grads, name):
    n = len(grads)
    shapes = [g.shape[1:] for g in grads]
    splits = []
    for rows, cols in shapes:
        ns = 1
        while rows * cols * 2 // ns > PAIR_CHUNK_BYTES and rows // (2 * ns) >= 16:
            ns *= 2
        splits.append(ns)

    def body(*refs):
        ins, outs = refs[:n], refs[n:2 * n]
        lands = refs[2 * n:3 * n]
        mines = refs[3 * n:4 * n]
        stages = refs[4 * n:5 * n]
        rsend, rrecv, lsem, osem = refs[5 * n:]
        x, y, c = lax.axis_index("x"), lax.axis_index("y"), lax.axis_index("c")
        remote = {}
        for w in range(n):
            for ch, (px, py) in enumerate(CHIPS):
                cp = pltpu.make_async_remote_copy(ins[w].at[4 * px + 2 * py + 1 - c], lands[w].at[ch],
                                                  rsend.at[w, ch], rrecv.at[w, ch],
                                                  device_id=(x, y, 1 - c), device_id_type=MESH)
                cp.start()
                remote[w, ch] = cp
        for w in range(n):
            rc = shapes[w][0] // splits[w]
            pieces = [(ch, p) for ch in range(N_CHIP) for p in range(splits[w])]

            def load(i, w=w, rc=rc, pieces=pieces):
                ch, p = pieces[i]
                px, py = CHIPS[ch]
                return pltpu.make_async_copy(ins[w].at[4 * px + 2 * py + c, pl.ds(p * rc, rc)],
                                             mines[w].at[i % 2], lsem.at[w, i % 2])

            def store(i, w=w, rc=rc, pieces=pieces):
                ch, p = pieces[i]
                return pltpu.make_async_copy(stages[w].at[i % 2], outs[w].at[ch, pl.ds(p * rc, rc)],
                                             osem.at[w, i % 2])

            load(0).start()
            for i, (ch, p) in enumerate(pieces):
                if i + 1 < len(pieces):
                    load(i + 1).start()
                load(i).wait()
                if p == 0:
                    remote[w, ch].wait_recv()
                if i >= 2:
                    store(i - 2).wait()
                theirs = lands[w][ch, p * rc:(p + 1) * rc, :]
                stages[w][i % 2] = (mines[w][i % 2].astype(F32) + theirs.astype(F32)).astype(BF16)
                store(i).start()
            for i in range(max(0, len(pieces) - 2), len(pieces)):
                store(i).wait()
        for cp in remote.values():
            cp.wait_send()

    hbm = pl.BlockSpec(memory_space=pl.ANY)
    scratch = [pltpu.VMEM((N_CHIP,) + sh, BF16) for sh in shapes]
    scratch += [pltpu.VMEM((2, sh[0] // ns, sh[1]), BF16) for sh, ns in zip(shapes, splits)] * 2
    scratch += [pltpu.SemaphoreType.DMA((n, N_CHIP)), pltpu.SemaphoreType.DMA((n, N_CHIP)),
                pltpu.SemaphoreType.DMA((n, 2)), pltpu.SemaphoreType.DMA((n, 2))]
    return pl.pallas_call(
        body, name=name, in_specs=[hbm] * n, out_specs=[hbm] * n,
        out_shape=[jax.ShapeDtypeStruct((N_CHIP,) + sh, BF16) for sh in shapes],
        scratch_shapes=scratch,
        compiler_params=pltpu.CompilerParams(vmem_limit_bytes=VMEM_LIMIT),
    )(*grads)


def _adamw_math(g, w, m, v):
    m2 = B1 * m + (1.0 - B1) * g
    v2 = B2 * v + (1.0 - B2) * (g * g)
    delta = -LR * ((m2 / BC1) / (jnp.sqrt(v2 / BC2) + AEPS) + WD * w)
    return delta, m2, v2


def _adamw(parts, w, m, v, *, name, tr=128):
    rows, cols = w.shape

    def body(p_ref, w_ref, m_ref, v_ref, g_ref, d_ref, mo_ref, vo_ref):
        g = p_ref[0].astype(F32)
        for b in range(1, N_CHIP):
            g = g + p_ref[b].astype(F32)
        delta, m2, v2 = _adamw_math(g, w_ref[...], m_ref[...], v_ref[...])
        g_ref[...] = g
        d_ref[...] = delta
        mo_ref[...] = m2
        vo_ref[...] = v2

    spec = pl.BlockSpec((tr, cols), lambda i: (i, 0))
    shape = jax.ShapeDtypeStruct((rows, cols), F32)
    return pl.pallas_call(
        body, name=name, grid=(rows // tr,),
        in_specs=[pl.BlockSpec((N_CHIP, tr, cols), lambda i: (0, i, 0)), spec, spec, spec],
        out_specs=[spec] * 4, out_shape=[shape] * 4,
        compiler_params=_params(("parallel",)),
    )(parts, w, m, v)


def _small_update(part, w, m, v):
    rows = part.shape[0]

    def body(p_ref, w_ref, m_ref, v_ref, g_ref, d_ref, mo_ref, vo_ref, buf, send, recv):
        me, peers = _me_and_peers()
        buf[me] = p_ref[...]
        sends = []
        for k, (dev, _) in enumerate(peers):
            cp = pltpu.make_async_remote_copy(p_ref, buf.at[me], send.at[k], recv.at[k],
                                              device_id=dev, device_id_type=MESH)
            cp.start()
            sends.append(cp)
        for k, (dev, idx) in enumerate(peers):
            pltpu.make_async_remote_copy(p_ref, buf.at[idx], send.at[k], recv.at[k],
                                         device_id=dev, device_id_type=MESH).wait_recv()
        for cp in sends:
            cp.wait_send()
        g = buf[0]
        for b in range(1, N_DEV):
            g = g + buf[b]
        delta, m2, v2 = _adamw_math(g, w_ref[...], m_ref[...], v_ref[...])
        g_ref[...] = g
        d_ref[...] = delta
        mo_ref[...] = m2
        vo_ref[...] = v2

    vm = pl.BlockSpec(memory_space=pltpu.VMEM)
    shape = jax.ShapeDtypeStruct((rows, HD), F32)
    return pl.pallas_call(
        body, name="small_params_update",
        in_specs=[vm] * 4, out_specs=[vm] * 4, out_shape=[shape] * 4,
        scratch_shapes=[pltpu.VMEM((N_DEV, rows, HD), F32),
                        pltpu.SemaphoreType.DMA((N_DEV - 1,)), pltpu.SemaphoreType.DMA((N_DEV - 1,))],
    )(part, w, m, v)


def _pack_small(norm_mix, b_gate, qa, ka, qb, kb, rpb, norm_ffn):
    gains = jnp.concatenate([qa, ka, qb, kb, jnp.zeros((4, HD), F32)], axis=0)
    rpb_pad = jnp.pad(rpb.reshape(4 * (2 * WIN_R - 1), 2 * WIN_C - 1), ((0, 4), (0, HD - (2 * WIN_C - 1))))
    return jnp.concatenate([norm_mix.reshape(16, HD), b_gate.reshape(32, HD), gains, rpb_pad,
                            norm_ffn.reshape(16, HD)], axis=0)


def _unpack_small(p):
    norm_mix = p[0:16].reshape(1, D)
    b_gate = p[16:48].reshape(1, 2 * D)
    qa, ka, qb, kb = (p[48 + i:49 + i] for i in range(4))
    rpb = p[56:116, :2 * WIN_C - 1].reshape(1, 4, 2 * WIN_R - 1, 2 * WIN_C - 1)
    norm_ffn = p[120:136].reshape(1, D)
    return norm_mix, b_gate, qa, ka, qb, kb, rpb, norm_ffn


def kernel(x, norm_mix, w_in, b_gate, q_norm_a, k_norm_a, q_norm_b, k_norm_b, rpb_b, w_proj_a, w_proj_b, w_out, norm_ffn, w_up, w_down, loss_target, m_norm_mix, m_w_in, m_b_gate, m_q_norm_a, m_k_norm_a, m_q_norm_b, m_k_norm_b, m_rpb_b, m_w_proj_a, m_w_proj_b, m_w_out, m_norm_ffn, m_w_up, m_w_down, v_norm_mix, v_w_in, v_b_gate, v_q_norm_a, v_k_norm_a, v_q_norm_b, v_k_norm_b, v_rpb_b, v_w_proj_a, v_w_proj_b, v_w_out, v_norm_ffn, v_w_up, v_w_down):
    big_w = (w_in[0], w_proj_a[0], w_proj_b[0], w_out[0], w_up[0], w_down[0])
    big_m = (m_w_in[0], m_w_proj_a[0], m_w_proj_b[0], m_w_out[0], m_w_up[0], m_w_down[0])
    big_v = (v_w_in[0], v_w_proj_a[0], v_w_proj_b[0], v_w_out[0], v_w_up[0], v_w_down[0])
    names = ("w_in", "w_proj_a", "w_proj_b", "w_out", "w_up", "w_down")

    shards = [_cast_bf16(w) for w in big_w]
    g_in, = _gather_on_sequencer(shards[0:1], "gather_w_in")
    g_pa, g_pb, g_out = _gather_on_sequencer(shards[1:4], "gather_w_mix")
    g_up, = _gather_on_sequencer(shards[4:5], "gather_w_up")
    g_down, = _gather_on_sequencer(shards[5:6], "gather_w_down")
    small_w = _pack_small(norm_mix, b_gate, q_norm_a, k_norm_a, q_norm_b, k_norm_b, rpb_b, norm_ffn)
    small_m = _pack_small(m_norm_mix, m_b_gate, m_q_norm_a, m_k_norm_a, m_q_norm_b, m_k_norm_b, m_rpb_b, m_norm_ffn)
    small_v = _pack_small(v_norm_mix, v_b_gate, v_q_norm_a, v_k_norm_a, v_q_norm_b, v_k_norm_b, v_rpb_b, v_norm_ffn)

    recv = [None] * 6

    def on_grads(tag, grads):
        sums = _pair_sum(list(grads.values()), f"pair_sum_{tag}")
        for i, r in zip(grads, _chip_exchange_on_sequencer(sums, f"chip_exchange_{tag}")):
            recv[i] = r
        return sums

    loss, grad_x, small_g = _local_step(
        x[0], loss_target[0], norm_mix, b_gate, small_w[48:56], small_w[56:120], norm_ffn,
        g_in, g_pa, g_pb, g_out.reshape(D, D), g_up, g_down.reshape(D_FF, D), on_grads)

    g_norm_mix, g_ba, g_bb, g_gains, g_rpb, g_norm_ffn = small_g
    small_part = jnp.concatenate([g_norm_mix.reshape(16, HD), g_ba.reshape(16, HD), g_bb.reshape(16, HD),
                                  g_gains, g_rpb, g_norm_ffn.reshape(16, HD)], axis=0)
    s_g, s_d, s_m, s_v = (_unpack_small(t) for t in _small_update(small_part, small_w, small_m, small_v))

    upd =[_adamw(recv[i], big_w[i], big_m[i], big_v[i], name=f"adamw_{names[i]}") for i in range(6)]
    b_g, b_d, b_m, b_v = ([u[j][None] for u in upd] for j in range(4))

    def order(small, big):
        nm, bg, qa, ka, qb, kb, rpb, nf = small
        w_in_, pa_, pb_, out_, up_, down_ = big
        return (nm, w_in_, bg, qa, ka, qb, kb, rpb, pa_, pb_, out_, nf, up_, down_)

    total = lax.psum(loss[0, 0], ("x", "y", "c"))
    return (total, grad_x[None], *order(s_g, b_g), *order(s_d, b_d), *order(s_m, b_m), *order(s_v, b_v))
```

```python
import functools

import jax
import jax.numpy as jnp
import numpy as np
from jax import lax
from jax.experimental import pallas as pl
from jax.experimental.pallas import tpu as pltpu
from jax.experimental.pallas import tpu_sc as plsc

F32 = jnp.float32
BF16 = jnp.bfloat16

N_DEV = 8
S = 2048
D = 2048
HD = 128
NH = 16
NH_A = 12
QKV = NH * HD
D_IN = 3 * QKV + 2 * D
D_BR = 512
D_FF = 4 * D
GRID_W = 64
ROWS = S // GRID_W
WIN_R = 8
WIN_C = 16
EPS = 1e-6
NEG = -1e30
SCALE = HD ** -0.5
ROPE_THETA = 10000.0
GROUPS_A = ((64, 1, 512), (256, 4, 768), (1024, 16, 2048))
QB = 256

LR, B1, B2, AEPS, WD, STEP = 0.001, 0.9, 0.999, 1e-08, 0.01, 10
BC1 = 1.0 - B1 ** STEP
BC2 = 1.0 - B2 ** STEP

VMEM_LIMIT = 56 * 1024 * 1024
MESH = pl.DeviceIdType.MESH

NN = (((1,), (0,)), ((), ()))
NT = (((1,), (1,)), ((), ()))
TN = (((0,), (0,)), ((), ()))


def _params(sem):
    return pltpu.CompilerParams(dimension_semantics=sem, vmem_limit_bytes=VMEM_LIMIT)


def _matmul(a, b, *, dims, grid, a_spec, b_spec, nk, epi, out_shape, out_specs, name,
            extra=(), extra_specs=(), acc_shape=None, after=()):
    n_extra = len(extra)

    def body(a_ref, b_ref, *rest):
        ex = rest[:n_extra]
        rest = rest[n_extra + len(after):]
        if nk == 1:
            outs = rest
            epi(lax.dot_general(a_ref[...], b_ref[...], dims, preferred_element_type=F32), ex, outs)
            return
        outs, acc = rest[:-1], rest[-1]
        k = pl.program_id(2)
        part = lax.dot_general(a_ref[...], b_ref[...], dims, preferred_element_type=F32)

        @pl.when(k == 0)
        def _():
            acc[...] = part

        @pl.when(k > 0)
        def _():
            acc[...] += part

        @pl.when(k == nk - 1)
        def _():
            epi(acc[...], ex, outs)

    scratch = [] if nk == 1 else [pltpu.VMEM(acc_shape, F32)]
    return pl.pallas_call(
        body, name=name, grid=grid,
        in_specs=[a_spec, b_spec, *extra_specs, *[pl.BlockSpec(memory_space=pl.ANY)] * len(after)],
        out_specs=out_specs, out_shape=out_shape, scratch_shapes=scratch,
        compiler_params=_params(("parallel", "parallel", "arbitrary")),
    )(a, b, *extra, *after)


def _epi_store(acc, ex, outs):
    outs[0][...] = acc.astype(outs[0].dtype)


def _epi_residual(acc, ex, outs):
    outs[0][...] = acc + ex[0][...]


def _mm_nn(a, b3, *, tm, tn, tk, name, out_dtype=F32, epi=_epi_store, extra=(), n_out=1,
           out_dtypes=None):
    m, kdim = a.shape
    g, _, ng = b3.shape
    n = g * ng
    npg = ng // tn
    nk = kdim // tk
    grid = (n // tn, m // tm, nk)
    tile = pl.BlockSpec((tm, tn), lambda j, i, k: (i, j))
    dts = out_dtypes or (out_dtype,) * n_out
    shapes = tuple(jax.ShapeDtypeStruct((m, n), dt) for dt in dts)
    return _matmul(
        a, b3, dims=NN, grid=grid, nk=nk, epi=epi, name=name,
        a_spec=pl.BlockSpec((tm, tk), lambda j, i, k: (i, k)),
        b_spec=pl.BlockSpec((None, tk, tn), lambda j, i, k: (j // npg, k, j % npg)),
        extra=extra, extra_specs=[tile] * len(extra),
        out_shape=shapes if len(dts) > 1 else shapes[0],
        out_specs=[tile] * len(dts) if len(dts) > 1 else tile,
        acc_shape=(tm, tn))


def _mm_nt(a, b3, *, tm, tn, tk, name, out_dtype=F32, epi=_epi_store, extra=(), after=()):
    m, kdim = a.shape
    g, n, kg = b3.shape
    kpg = kg // tk
    nk = kdim // tk
    grid = (n // tn, m // tm, nk)
    tile = pl.BlockSpec((tm, tn), lambda j, i, k: (i, j))
    return _matmul(
        a, b3, dims=NT, grid=grid, nk=nk, epi=epi, name=name,
        a_spec=pl.BlockSpec((tm, tk), lambda j, i, k: (i, k)),
        b_spec=pl.BlockSpec((None, tn, tk), lambda j, i, k: (k // kpg, j, k % kpg)),
        extra=extra, extra_specs=[tile] * len(extra), after=after,
        out_shape=jax.ShapeDtypeStruct((m, n), out_dtype), out_specs=tile,
        acc_shape=(tm, tn))


def _mm_tn(a, b, *, tm, tn, name, groups=1, out_dtype=BF16):
    t, m = a.shape
    _, n = b.shape
    ng = n // groups
    npg = ng // tn
    grid = (n // tn, m // tm, 1)
    return _matmul(
        a, b, dims=TN, grid=grid, nk=1, epi=_epi_store, name=name,
        a_spec=pl.BlockSpec((t, tm), lambda j, i, k: (0, i)),
        b_spec=pl.BlockSpec((t, tn), lambda j, i, k: (0, j)),
        out_shape=jax.ShapeDtypeStruct((groups, m, ng), out_dtype),
        out_specs=pl.BlockSpec((None, tm, tn), lambda j, i, k: (j // npg, i, j % npg)))


def _rms_fwd(x, g, *, name, tr=256):
    def body(x_ref, g_ref, y_ref, r_ref):
        xv = x_ref[...]
        r = lax.rsqrt(jnp.mean(xv * xv, axis=-1, keepdims=True) + EPS)
        y_ref[...] = (xv * r * g_ref[...]).astype(BF16)
        r_ref[...] = r

    row = pl.BlockSpec((tr, D), lambda i: (i, 0))
    return pl.pallas_call(
        body, name=name, grid=(S // tr,),
        in_specs=[row, pl.BlockSpec((1, D), lambda i: (0, 0))],
        out_specs=[row, pl.BlockSpec((tr, 1), lambda i: (i, 0))],
        out_shape=[jax.ShapeDtypeStruct((S, D), BF16), jax.ShapeDtypeStruct((S, 1), F32)],
        compiler_params=_params(("parallel",)),
    )(x, g)


def _rms_bwd(dy, x, rstd, g, resid, *, name, tr=256):
    def body(dy_ref, x_ref, r_ref, g_ref, res_ref, dx_ref, dxb_ref, dg_ref):
        r = r_ref[...]
        xh = x_ref[...] * r
        dyv = dy_ref[...]
        t = dyv * g_ref[...]
        dx = r * (t - xh * jnp.mean(t * xh, axis=-1, keepdims=True)) + res_ref[...]
        dx_ref[...] = dx
        dxb_ref[...] = dx.astype(BF16)
        part = jnp.sum(dyv * xh, axis=0, keepdims=True)

        @pl.when(pl.program_id(0) == 0)
        def _():
            dg_ref[...] = part

        @pl.when(pl.program_id(0) > 0)
        def _():
            dg_ref[...] += part

    row = pl.BlockSpec((tr, D), lambda i: (i, 0))
    vec = pl.BlockSpec((1, D), lambda i: (0, 0))
    return pl.pallas_call(
        body, name=name, grid=(S // tr,),
        in_specs=[row, row, pl.BlockSpec((tr, 1), lambda i: (i, 0)), vec, row],
        out_specs=[row, row, vec],
        out_shape=[jax.ShapeDtypeStruct((S, D), F32), jax.ShapeDtypeStruct((S, D), BF16),
                   jax.ShapeDtypeStruct((1, D), F32)],
        compiler_params=_params(("arbitrary",)),
    )(dy, x, rstd, g, resid)


def _loss_head(h2, target, *, tr=256):
    def body(h_ref, t_ref, dy_ref, dyb_ref, loss_ref):
        e = h_ref[...] - t_ref[...]
        dy = e * (1.0 / D)
        dy_ref[...] = dy
        dyb_ref[...] = dy.astype(BF16)
        part = (0.5 / D) * jnp.sum(jnp.sum(e * e, axis=-1, keepdims=True), axis=0, keepdims=True)

        @pl.when(pl.program_id(0) == 0)
        def _():
            loss_ref[...] = part

        @pl.when(pl.program_id(0) > 0)
        def _():
            loss_ref[...] += part

    row = pl.BlockSpec((tr, D), lambda i: (i, 0))
    return pl.pallas_call(
        body, name="loss_head", grid=(S // tr,),
        in_specs=[row, row],
        out_specs=[row, row, pl.BlockSpec((1, 1), lambda i: (0, 0))],
        out_shape=[jax.ShapeDtypeStruct((S, D), F32), jax.ShapeDtypeStruct((S, D), BF16),
                   jax.ShapeDtypeStruct((1, 1), F32)],
        compiler_params=_params(("arbitrary",)),
    )(h2, target)


def _rope_tables():
    pos = np.arange(S, dtype=np.float32)
    inv = (ROPE_THETA ** (-np.arange(0, HD, 2, dtype=np.float32) / HD)).astype(np.float32)
    ang = pos[:, None] * inv[None, :]
    cos, sin = np.cos(ang), np.sin(ang)
    return (jnp.asarray(np.concatenate([cos, cos], axis=-1), F32),
            jnp.asarray(np.concatenate([-sin, sin], axis=-1), F32))


def _swap_halves(t):
    return pltpu.roll(t, HD // 2, axis=1)


def _qk_prep(proj, gains, cos2, sin2, *, tr=256):
    def body(q_ref, k_ref, v_ref, g_ref, c_ref, s_ref, qn_ref, kn_ref, vb_ref):
        cos, sin = c_ref[...], s_ref[...]
        for src, dst, row_a, row_b in ((q_ref, qn_ref, 0, 2), (k_ref, kn_ref, 1, 3)):
            for h in range(NH):
                cols = slice(h * HD, (h + 1) * HD)
                t = src[:, cols]
                r = lax.rsqrt(jnp.mean(t * t, axis=-1, keepdims=True) + EPS)
                if h < NH_A:
                    y = t * r * g_ref[row_a:row_a + 1, :]
                    y = y * cos + _swap_halves(y) * sin
                else:
                    y = t * r * g_ref[row_b:row_b + 1, :]
                dst[:, cols] = y.astype(BF16)
        vb_ref[...] = v_ref[...].astype(BF16)

    def blk(c):
        return pl.BlockSpec((tr, QKV), lambda i: (i, c))
    tab = pl.BlockSpec((tr, HD), lambda i: (i, 0))
    out = pl.BlockSpec((tr, QKV), lambda i: (i, 0))
    return pl.pallas_call(
        body, name="qk_prep", grid=(S // tr,),
        in_specs=[blk(0), blk(1), blk(2), pl.BlockSpec((8, HD), lambda i: (0, 0)), tab, tab],
        out_specs=[out, out, out],
        out_shape=[jax.ShapeDtypeStruct((S, QKV), BF16)] * 3,
        compiler_params=_params(("parallel",)),
    )(proj, proj, proj, gains, cos2, sin2)


def _qk_prep_bwd(dproj, proj, gains, cos2, sin2, dq_parts, dk_parts, dv_parts, *, tr=256):
    def body(dp_in, q_ref, k_ref, g_ref, c_ref, s_ref, *rest):
        dqs, dks, dvs = rest[0:4], rest[4:8], rest[8:12]
        dp_out, dg_ref = rest[12:14]
        del dp_in
        cos, sin = c_ref[...], s_ref[...]
        dg_rows = []
        for src, grads, base, row_a, row_b in ((q_ref, dqs, 0, 0, 2), (k_ref, dks, QKV, 1, 3)):
            dg_a = jnp.zeros((1, HD), F32)
            dg_b = jnp.zeros((1, HD), F32)
            for h in range(NH):
                cols = slice(h * HD, (h + 1) * HD)
                t = src[:, cols]
                dy = grads[h // 4][:, (h % 4) * HD:(h % 4 + 1) * HD]
                r = lax.rsqrt(jnp.mean(t * t, axis=-1, keepdims=True) + EPS)
                xh = t * r
                if h < NH_A:
                    dy = dy * cos - _swap_halves(dy) * sin
                    gain = g_ref[row_a:row_a + 1, :]
                    dg_a = dg_a + jnp.sum(dy * xh, axis=0, keepdims=True)
                else:
                    gain = g_ref[row_b:row_b + 1, :]
                    dg_b = dg_b + jnp.sum(dy * xh, axis=0, keepdims=True)
                u = dy * gain
                dx = r * (u - xh * jnp.mean(u * xh, axis=-1, keepdims=True))
                dp_out[:, base + h * HD:base + (h + 1) * HD] = dx.astype(BF16)
            dg_rows += [(row_a, dg_a), (row_b, dg_b)]
        for g4 in range(4):
            dp_out[:, 2 * QKV + g4 * D_BR:2 * QKV + (g4 + 1) * D_BR] = dvs[g4][...].astype(BF16)

        first = pl.program_id(0) == 0

        @pl.when(first)
        def _():
            dg_ref[...] = jnp.zeros((8, HD), F32)

        for row, val in dg_rows:
            dg_ref[row:row + 1, :] += val

    def blk(c):
        return pl.BlockSpec((tr, QKV), lambda i: (i, c))
    tab = pl.BlockSpec((tr, HD), lambda i: (i, 0))
    part = pl.BlockSpec((tr, D_BR), lambda i: (i, 0))
    gain_spec = pl.BlockSpec((8, HD), lambda i: (0, 0))
    return pl.pallas_call(
        body, name="qk_prep_bwd", grid=(S // tr,),
        in_specs=[pl.BlockSpec(memory_space=pl.ANY), blk(0), blk(1), gain_spec, tab, tab] + [part] * 12,
        out_specs=[pl.BlockSpec((tr, 3 * QKV), lambda i: (i, 0)), gain_spec],
        out_shape=[jax.ShapeDtypeStruct((S, D_IN), BF16), jax.ShapeDtypeStruct((8, HD), F32)],
        input_output_aliases={0: 0},
        compiler_params=_params(("arbitrary",)),
    )(dproj, proj, proj, gains, cos2, sin2, *dq_parts, *dk_parts, *dv_parts)


def _gate_fwd(proj, b_gate, ya, yb, *, tr=256):
    def body(la_ref, lb_ref, ba_ref, bb_ref, ya_ref, yb_ref, o_ref):
        ga = jax.nn.sigmoid(la_ref[...] + ba_ref[...])
        gb = jax.nn.sigmoid(lb_ref[...] + bb_ref[...])
        o_ref[...] = (ga * ya_ref[...] + gb * yb_ref[...]).astype(BF16)

    row = pl.BlockSpec((tr, D), lambda i: (i, 0))
    return pl.pallas_call(
        body, name="gate_fwd", grid=(S // tr,),
        in_specs=[pl.BlockSpec((tr, D), lambda i: (i, 3)), pl.BlockSpec((tr, D), lambda i: (i, 4)),
                  pl.BlockSpec((1, D), lambda i: (0, 0)), pl.BlockSpec((1, D), lambda i: (0, 1)),
                  row, row],
        out_specs=row, out_shape=jax.ShapeDtypeStruct((S, D), BF16),
        compiler_params=_params(("parallel",)),
    )(proj, proj, b_gate, b_gate, ya, yb)


def _gate_bwd(branch, dmixed, proj, b_gate, y, dproj, *, tr=256):
    aliased = dproj is not None

    def body(dm_ref, l_ref, b_ref, y_ref, *rest):
        dy_ref, dp_ref, db_ref = rest[-3:]
        g = jax.nn.sigmoid(l_ref[...] + b_ref[...])
        dm = dm_ref[...]
        dy_ref[...] = (dm * g).astype(BF16)
        dl = dm * y_ref[...] * g * (1.0 - g)
        dp_ref[...] = dl.astype(BF16)
        part = jnp.sum(dl, axis=0, keepdims=True)

        @pl.when(pl.program_id(0) == 0)
        def _():
            db_ref[...] = part

        @pl.when(pl.program_id(0) > 0)
        def _():
            db_ref[...] += part

    row = pl.BlockSpec((tr, D), lambda i: (i, 0))
    col = pl.BlockSpec((tr, D), lambda i: (i, 3 + branch))
    vec = pl.BlockSpec((1, D), lambda i: (0, 0))
    return pl.pallas_call(
        body, name=f"gate_bwd_{branch}", grid=(S // tr,),
        in_specs=[row, col, pl.BlockSpec((1, D), lambda i: (0, branch)), row]
        + ([pl.BlockSpec(memory_space=pl.ANY)] if aliased else []),
        out_specs=[row, col, vec],
        out_shape=[jax.ShapeDtypeStruct((S, D), BF16), jax.ShapeDtypeStruct((S, D_IN), BF16),
                   jax.ShapeDtypeStruct((1, D), F32)],
        input_output_aliases={4: 1} if aliased else {},
        compiler_params=_params(("arbitrary",)),
    )(dmixed, proj, b_gate, y, *([dproj] if aliased else []))


def _window_start(t0, wk):
    if wk == S:
        return 0
    return pl.multiple_of(jnp.clip(t0 - (wk - QB) // 2, 0, S - wk), 128)


def _scores_a(q, kw, t0, start, hs, dil, wk):
    s = lax.dot_general(q, kw, NT, preferred_element_type=F32) * SCALE
    qpos = t0 + lax.broadcasted_iota(jnp.int32, (QB, 1), 0)
    kpos = start + lax.broadcasted_iota(jnp.int32, (1, wk), 1)
    diff = kpos - qpos
    keep = (jnp.abs(diff) <= hs) & ((diff & (dil - 1)) == 0)
    return jnp.where(keep, s, NEG)


def _attn_a_fwd(qn, kn, vb, gi):
    hs, dil, wk = GROUPS_A[gi]

    def body(q_ref, k_ref, v_ref, o_ref, lse_ref):
        t0 = pl.program_id(1) * QB
        start = _window_start(t0, wk)
        s = _scores_a(q_ref[...], k_ref[pl.ds(start, wk), :], t0, start, hs, dil, wk)
        m = jnp.max(s, axis=-1, keepdims=True)
        p = jnp.exp(s - m)
        l = jnp.sum(p, axis=-1, keepdims=True)
        o = lax.dot_general(p.astype(BF16), v_ref[pl.ds(start, wk), :], NN, preferred_element_type=F32)
        o_ref[...] = o / l
        lse_ref[...] = m + jnp.log(l)

    full = pl.BlockSpec((S, HD), lambda h, i: (0, 4 * gi + h))
    return pl.pallas_call(
        body, name=f"attn_a_fwd_{gi}", grid=(4, S // QB),
        in_specs=[pl.BlockSpec((QB, HD), lambda h, i: (i, 4 * gi + h)), full, full],
        out_specs=[pl.BlockSpec((QB, HD), lambda h, i: (i, h)),
                   pl.BlockSpec((None, QB, 1), lambda h, i: (h, i, 0))],
        out_shape=[jax.ShapeDtypeStruct((S, D_BR), F32), jax.ShapeDtypeStruct((4, S, 1), F32)],
        compiler_params=_params(("parallel", "parallel")),
    )(qn, kn, vb)


def _combine_a(os, lses, *, tr=256):
    def body(o0, o1, o2, l0, l1, l2, oa_ref, lse_ref):
        for h in range(4):
            cols = slice(h * HD, (h + 1) * HD)
            a, b, c = l0[h], l1[h], l2[h]
            m = jnp.maximum(jnp.maximum(a, b), c)
            wa, wb, wc = jnp.exp(a - m), jnp.exp(b - m), jnp.exp(c - m)
            tot = wa + wb + wc
            oa_ref[:, cols] = ((wa * o0[:, cols] + wb * o1[:, cols] + wc * o2[:, cols]) / tot).astype(BF16)
            lse_ref[h] = m + jnp.log(tot)

    row = pl.BlockSpec((tr, D_BR), lambda i: (i, 0))
    stat = pl.BlockSpec((4, tr, 1), lambda i: (0, i, 0))
    return pl.pallas_call(
        body, name="combine_a", grid=(S // tr,),
        in_specs=[row] * 3 + [stat] * 3, out_specs=[row, stat],
        out_shape=[jax.ShapeDtypeStruct((S, D_BR), BF16), jax.ShapeDtypeStruct((4, S, 1), F32)],
        compiler_params=_params(("parallel",)),
    )(*os, *lses)


def _attn_a_bwd(qn, kn, vb, oa, doa, lse, gi):
    hs, dil, wk = GROUPS_A[gi]

    def body(q_ref, k_ref, v_ref, o_ref, do_ref, lse_ref, dq_ref, dk_ref, dv_ref):
        @pl.when(pl.program_id(1) == 0)
        def _():
            dk_ref[...] = jnp.zeros((S, HD), F32)
            dv_ref[...] = jnp.zeros((S, HD), F32)

        t0 = pl.program_id(1) * QB
        start = _window_start(t0, wk)
        q = q_ref[...]
        kw = k_ref[pl.ds(start, wk), :]
        vw = v_ref[pl.ds(start, wk), :]
        p = jnp.exp(_scores_a(q, kw, t0, start, hs, dil, wk) - lse_ref[...])
        do = do_ref[...]
        dob = do.astype(BF16)
        dsum = jnp.sum(do * o_ref[...].astype(F32), axis=-1, keepdims=True)
        dp = lax.dot_general(dob, vw, NT, preferred_element_type=F32)
        ds = (p * (dp - dsum) * SCALE).astype(BF16)
        dq_ref[...] = lax.dot_general(ds, kw, NN, preferred_element_type=F32)
        dk_ref[pl.ds(start, wk), :] += lax.dot_general(ds, q, TN, preferred_element_type=F32)
        dv_ref[pl.ds(start, wk), :] += lax.dot_general(p.astype(BF16), dob, TN, preferred_element_type=F32)

    full = pl.BlockSpec((S, HD), lambda h, i: (0, 4 * gi + h))
    blk = pl.BlockSpec((QB, HD), lambda h, i: (i, h))
    acc = pl.BlockSpec((S, HD), lambda h, i: (0, h))
    shape = jax.ShapeDtypeStruct((S, D_BR), F32)
    return pl.pallas_call(
        body, name=f"attn_a_bwd_{gi}", grid=(4, S // QB),
        in_specs=[pl.BlockSpec((QB, HD), lambda h, i: (i, 4 * gi + h)), full, full, blk, blk,
                  pl.BlockSpec((None, QB, 1), lambda h, i: (h, i, 0))],
        out_specs=[blk, acc, acc], out_shape=[shape, shape, shape],
        compiler_params=_params(("parallel", "arbitrary")),
    )(qn, kn, vb, oa, doa, lse)


KEYS_B = WIN_R * GRID_W
N_OFF = WIN_R


def _bias_constants():
    q = np.arange(GRID_W)[:, None]
    kc = np.arange(GRID_W)[None, :]
    dc = np.clip(kc - q, -(WIN_C - 1), WIN_C - 1) + (WIN_C - 1)
    expand = np.zeros((HD, GRID_W * GRID_W), np.float32)
    expand[dc.reshape(-1), np.arange(GRID_W * GRID_W)] = 1.0
    cs = np.clip(q - WIN_C // 2, 0, GRID_W - WIN_C)
    keep = ((kc >= cs) & (kc < cs + WIN_C)).reshape(1, -1).astype(np.float32)
    sel = np.zeros((64, 4 * N_OFF * WIN_R), np.float32)
    for h in range(4):
        for off in range(N_OFF):
            for j in range(WIN_R):
                sel[h * (2 * WIN_R - 1) + off + j, (h * N_OFF + off) * WIN_R + j] = 1.0
    return jnp.asarray(expand), jnp.asarray(keep), jnp.asarray(sel)


def _bias_expand(rpb_pad, expand, keep, sel):
    def body(r_ref, e_ref, k_ref, s_ref, o_ref):
        t = lax.dot_general(r_ref[...], e_ref[...], NN, precision=lax.Precision.HIGHEST,
                            preferred_element_type=F32)
        rows = lax.dot_general(s_ref[...], t, TN, precision=lax.Precision.HIGHEST,
                               preferred_element_type=F32)
        o_ref[...] = jnp.where(k_ref[...] > 0.5, rows, NEG)

    return pl.pallas_call(
        body, name="bias_expand",
        out_shape=jax.ShapeDtypeStruct((4 * N_OFF * WIN_R, GRID_W * GRID_W), F32),
        compiler_params=pltpu.CompilerParams(vmem_limit_bytes=VMEM_LIMIT),
    )(rpb_pad, expand, keep, sel)


def _bias_reduce(dbias_rows, expand, sel):
    def body(x_ref, e_ref, s_ref, o_ref):
        z = lax.dot_general(x_ref[...], e_ref[...], NT, precision=lax.Precision.HIGHEST,
                            preferred_element_type=F32)
        o_ref[...] = lax.dot_general(s_ref[...], z, NN, precision=lax.Precision.HIGHEST,
                                     preferred_element_type=F32)

    return pl.pallas_call(
        body, name="bias_reduce", out_shape=jax.ShapeDtypeStruct((64, HD), F32),
        compiler_params=pltpu.CompilerParams(vmem_limit_bytes=VMEM_LIMIT),
    )(dbias_rows, expand, sel)


def _rows_to_tab(rows):
    t = rows.reshape(4, N_OFF, WIN_R, GRID_W, GRID_W)
    return t.transpose(0, 1, 3, 2, 4).reshape(4, N_OFF, GRID_W, KEYS_B)


def _tab_to_rows(tab):
    t = tab.reshape(4, N_OFF, GRID_W, WIN_R, GRID_W)
    return t.transpose(0, 1, 3, 2, 4).reshape(4 * N_OFF * WIN_R, GRID_W * GRID_W)


def _row_window(r):
    r0 = jnp.clip(r - WIN_R // 2, 0, ROWS - WIN_R)
    off = r0 + (WIN_R - 1) - r
    return pl.multiple_of(r * GRID_W, GRID_W), pl.multiple_of(r0 * GRID_W, GRID_W), off


def _attn_b_fwd(qn, kn, vb, bias_tab):
    def body(q_ref, k_ref, v_ref, b_ref, o_ref, lse_ref):
        def row(r, carry):
            qs, ks, off = _row_window(r)
            q = q_ref[pl.ds(qs, GRID_W), :]
            s = lax.dot_general(q, k_ref[pl.ds(ks, KEYS_B), :], NT, preferred_element_type=F32) * SCALE
            s = s + b_ref[off]
            m = jnp.max(s, axis=-1, keepdims=True)
            p = jnp.exp(s - m)
            l = jnp.sum(p, axis=-1, keepdims=True)
            o = lax.dot_general(p.astype(BF16), v_ref[pl.ds(ks, KEYS_B), :], NN, preferred_element_type=F32)
            o_ref[pl.ds(qs, GRID_W), :] = (o / l).astype(BF16)
            lse_ref[pl.ds(qs, GRID_W), :] = m + jnp.log(l)
            return carry

        lax.fori_loop(0, ROWS, row, 0)

    full = pl.BlockSpec((S, HD), lambda h: (0, NH_A + h))
    return pl.pallas_call(
        body, name="attn_b_fwd", grid=(4,),
        in_specs=[full, full, full, pl.BlockSpec((None, N_OFF, GRID_W, KEYS_B), lambda h: (h, 0, 0, 0))],
        out_specs=[pl.BlockSpec((S, HD), lambda h: (0, h)), pl.BlockSpec((None, S, 1), lambda h: (h, 0, 0))],
        out_shape=[jax.ShapeDtypeStruct((S, D_BR), BF16), jax.ShapeDtypeStruct((4, S, 1), F32)],
        compiler_params=_params(("parallel",)),
    )(qn, kn, vb, bias_tab)


def _attn_b_bwd(qn, kn, vb, bias_tab, ob, dob, lse):
    def body(q_ref, k_ref, v_ref, b_ref, o_ref, do_ref, lse_ref, dq_ref, dk_ref, dv_ref, db_ref):
        dk_ref[...] = jnp.zeros((S, HD), F32)
        dv_ref[...] = jnp.zeros((S, HD), F32)
        db_ref[...] = jnp.zeros((N_OFF, GRID_W, KEYS_B), F32)

        def row(r, carry):
            qs, ks, off = _row_window(r)
            rows = pl.ds(qs, GRID_W)
            keys = pl.ds(ks, KEYS_B)
            q = q_ref[rows, :]
            kw = k_ref[keys, :]
            s = lax.dot_general(q, kw, NT, preferred_element_type=F32) * SCALE + b_ref[off]
            p = jnp.exp(s - lse_ref[rows, :])
            do = do_ref[rows, :]
            dobf = do.astype(BF16)
            dsum = jnp.sum(do * o_ref[rows, :].astype(F32), axis=-1, keepdims=True)
            dp = lax.dot_general(dobf, v_ref[keys, :], NT, preferred_element_type=F32)
            ds = p * (dp - dsum)
            db_ref[off] += ds
            dsb = (ds * SCALE).astype(BF16)
            dq_ref[rows, :] = lax.dot_general(dsb, kw, NN, preferred_element_type=F32)
            dk_ref[keys, :] += lax.dot_general(dsb, q, TN, preferred_element_type=F32)
            dv_ref[keys, :] += lax.dot_general(p.astype(BF16), dobf, TN, preferred_element_type=F32)
            return carry

        lax.fori_loop(0, ROWS, row, 0)

    full = pl.BlockSpec((S, HD), lambda h: (0, NH_A + h))
    slot = pl.BlockSpec((S, HD), lambda h: (0, h))
    tab = pl.BlockSpec((None, N_OFF, GRID_W, KEYS_B), lambda h: (h, 0, 0, 0))
    shape = jax.ShapeDtypeStruct((S, D_BR), F32)
    return pl.pallas_call(
        body, name="attn_b_bwd", grid=(4,),
        in_specs=[full, full, full, tab, slot, slot, pl.BlockSpec((None, S, 1), lambda h: (h, 0, 0))],
        out_specs=[slot, slot, slot, tab],
        out_shape=[shape, shape, shape, jax.ShapeDtypeStruct((4, N_OFF, GRID_W, KEYS_B), F32)],
        compiler_params=_params(("parallel",)),
    )(qn, kn, vb, bias_tab, ob, dob, lse)


def _epi_relu_sq(acc, ex, outs):
    u = jnp.maximum(acc, 0.0)
    outs[0][...] = u.astype(BF16)
    outs[1][...] = (u * u).astype(BF16)


def _epi_relu_sq_bwd(acc, ex, outs):
    outs[0][...] = (acc * (2.0 * ex[0][...].astype(F32))).astype(BF16)


def _local_step(x, target, norm_mix, b_gate, gains, rpb_pad, norm_ffn,
                w_in, w_pa, w_pb, w_out, w_up, w_down, on_grads):
    cos2, sin2 = _rope_tables()
    expand, keep, sel = _bias_constants()
    w_out3, w_down3 = w_out[None], w_down[None]

    xn, rstd1 = _rms_fwd(x, norm_mix, name="rms_mix")
    proj = _mm_nn(xn, w_in, tm=512, tn=1280, tk=D, name="proj")
    qn, kn, vb = _qk_prep(proj, gains, cos2, sin2)
    fwd_a = [_attn_a_fwd(qn, kn, vb, gi) for gi in range(3)]
    oa, lse_a = _combine_a([o for o, _ in fwd_a], [l for _, l in fwd_a])
    bias_tab = _rows_to_tab(_bias_expand(rpb_pad, expand, keep, sel))
    ob, lse_b = _attn_b_fwd(qn, kn, vb, bias_tab)
    ya = _mm_nn(oa, w_pa, tm=1024, tn=256, tk=D_BR, name="proj_a")
    yb = _mm_nn(ob, w_pb, tm=1024, tn=256, tk=D_BR, name="proj_b")
    mixed = _gate_fwd(proj, b_gate, ya, yb)
    h1 = _mm_nn(mixed, w_out3, tm=512, tn=512, tk=D, name="out_proj", epi=_epi_residual, extra=(x,))
    hn, rstd2 = _rms_fwd(h1, norm_ffn, name="rms_ffn")
    u, usq = _mm_nn(hn, w_up, tm=512, tn=1024, tk=D, name="ffn_up", epi=_epi_relu_sq,
                    out_dtypes=(BF16, BF16))
    h2 = _mm_nn(usq, w_down3, tm=512, tn=1024, tk=D, name="ffn_down", epi=_epi_residual, extra=(h1,))
    dy, dyb, loss = _loss_head(h2, target)

    g_down = _mm_tn(usq, dyb, tm=512, tn=1024, name="grad_w_down")
    sent = on_grads("w_down", {5: g_down.reshape(N_DEV, D_FF // N_DEV, D)})
    du = _mm_nt(dyb, w_down3, tm=512, tn=1024, tk=D, name="ffn_down_bwd", out_dtype=BF16,
                epi=_epi_relu_sq_bwd, extra=(u,), after=sent)
    g_up = _mm_tn(hn, du, tm=512, tn=1024, groups=N_DEV, name="grad_w_up")
    sent = on_grads("w_up", {4: g_up})
    dhn = _mm_nt(du, w_up, tm=512, tn=1024, tk=1024, name="ffn_up_bwd", after=sent)
    dh1, dh1b, g_norm_ffn = _rms_bwd(dhn, h1, rstd2, norm_ffn, dy, name="rms_ffn_bwd")

    g_out = _mm_tn(mixed, dh1b, tm=512, tn=1024, name="grad_w_out")
    dmixed = _mm_nt(dh1b, w_out3, tm=512, tn=512, tk=D, name="out_proj_bwd")
    dya, dproj, g_ba = _gate_bwd(0, dmixed, proj, b_gate, ya, None)
    dyb2, dproj, g_bb = _gate_bwd(1, dmixed, proj, b_gate, yb, dproj)
    g_pa = _mm_tn(oa, dya, tm=512, tn=256, groups=N_DEV, name="grad_w_proj_a")
    g_pb = _mm_tn(ob, dyb2, tm=512, tn=256, groups=N_DEV, name="grad_w_proj_b")
    sent = on_grads("w_mix", {3: g_out.reshape(N_DEV, D // N_DEV, D), 1: g_pa, 2: g_pb})
    doa = _mm_nt(dya, w_pa, tm=1024, tn=512, tk=256, name="proj_a_bwd", after=sent)
    dob = _mm_nt(dyb2, w_pb, tm=1024, tn=512, tk=256, name="proj_b_bwd")
    bwd = [_attn_a_bwd(qn, kn, vb, oa, doa, lse_a, gi) for gi in range(3)]
    dqb, dkb, dvb, dbias = _attn_b_bwd(qn, kn, vb, bias_tab, ob, dob, lse_b)
    g_rpb = _bias_reduce(_tab_to_rows(dbias), expand, sel)
    dproj, g_gains = _qk_prep_bwd(dproj, proj, gains, cos2, sin2,
                                  [b[0] for b in bwd] + [dqb], [b[1] for b in bwd] + [dkb],
                                  [b[2] for b in bwd] + [dvb])
    g_in = _mm_tn(xn, dproj, tm=512, tn=1280, groups=N_DEV, name="grad_w_in")
    sent = on_grads("w_in", {0: g_in})
    dxn = _mm_nt(dproj, w_in, tm=512, tn=1024, tk=1280, name="proj_bwd", after=sent)
    grad_x, _, g_norm_mix = _rms_bwd(dxn, x, rstd1, norm_mix, dh1, name="rms_mix_bwd")

    small = (g_norm_mix, g_ba, g_bb, g_gains, g_rpb, g_norm_ffn)
    return loss, grad_x, small


def _cast_bf16(w, *, tr=256):
    rows, cols = w.shape
    tr = min(tr, rows)

    def body(w_ref, o_ref):
        o_ref[...] = w_ref[...].astype(BF16)

    spec = pl.BlockSpec((tr, cols), lambda i: (i, 0))
    return pl.pallas_call(
        body, name=f"cast_{rows}x{cols}", grid=(rows // tr,), in_specs=[spec], out_specs=spec,
        out_shape=jax.ShapeDtypeStruct((rows, cols), BF16), compiler_params=_params(("parallel",)),
    )(w)


def _me_and_peers():
    x, y, c = lax.axis_index("x"), lax.axis_index("y"), lax.axis_index("c")
    me = 4 * x + 2 * y + c
    peers = []
    for k in range(1, N_DEV):
        px = 1 - x if k & 4 else x
        py = 1 - y if k & 2 else y
        pc = 1 - c if k & 1 else c
        peers.append(((px, py, pc), 4 * px + 2 * py + pc))
    return me, peers


def _gather_on_sequencer(shards, name):
    n = len(shards)
    hbm = pltpu.MemorySpace.HBM
    ins = [jax.new_ref(s, memory_space=hbm) for s in shards]
    outs = [jax.empty_ref(jax.ShapeDtypeStruct((N_DEV,) + s.shape, s.dtype), memory_space=hbm) for s in shards]

    @pl.kernel(mesh=plsc.ScalarSubcoreMesh(axis_name="seq", num_cores=1), name=name,
               scratch_types=(pltpu.SemaphoreType.DMA((n, N_DEV - 1)), pltpu.SemaphoreType.DMA((n, N_DEV - 1)),
                              pltpu.SemaphoreType.DMA((n,))),
               compiler_params=pltpu.CompilerParams(collective_id=0))
    def launch(send, recv, lsem):
        x, y, c = lax.axis_index("x"), lax.axis_index("y"), lax.axis_index("c")
        me, sibling = (x, y, c), (x, y, 1 - c)
        chips = [(1 - x, y), (x, 1 - y), (1 - x, 1 - y)]
        barrier = pltpu.get_barrier_semaphore()
        for peer in [sibling] + [(*chip, c) for chip in chips]:
            pl.semaphore_signal(barrier, inc=1, device_id=peer, device_id_type=MESH)
        pl.semaphore_wait(barrier, 4)

        def copy(w, k, block, to, src=None):
            px, py, pc = block
            dst = outs[w].at[4 * px + 2 * py + pc]
            return pltpu.make_async_remote_copy(dst if src is None else src, dst, send.at[w, k], recv.at[w, k],
                                                device_id=to, device_id_type=MESH)

        local = [pltpu.make_async_copy(ins[w], outs[w].at[4 * x + 2 * y + c], lsem.at[w]) for w in range(n)]
        for cp in local:
            cp.start()
        first = []
        for w in range(n):
            first += [copy(w, 1 + j, me, (*chip, c), src=ins[w]) for j, chip in enumerate(chips)]
            first.append(copy(w, 0, me, sibling, src=ins[w]))
        for cp in first:
            cp.start()
        passed = []
        for w in range(n):
            for j, chip in enumerate(chips):
                copy(w, 1 + j, (*chip, c), me).wait_recv()
                cp = copy(w, 4 + j, (*chip, c), sibling)
                cp.start()
                passed.append(cp)
        for w in range(n):
            copy(w, 0, sibling, me).wait_recv()
            for j, chip in enumerate(chips):
                copy(w, 4 + j, (*chip, 1 - c), me).wait_recv()
        for cp in first + passed:
            cp.wait_send()
        for cp in local:
            cp.wait()

    launch()
    return [o[...] for o in outs]


N_CHIP = 4
CHIPS = ((0, 0), (0, 1), (1, 0), (1, 1))


def _sequencer(name, n_sems, collective_id):
    return functools.partial(
        pl.kernel, mesh=plsc.ScalarSubcoreMesh(axis_name="seq", num_cores=1), name=name,
        scratch_types=tuple(pltpu.SemaphoreType.DMA(s) for s in n_sems),
        compiler_params=pltpu.CompilerParams(collective_id=collective_id))


def _handshake(peers):
    barrier = pltpu.get_barrier_semaphore()
    for peer in peers:
        pl.semaphore_signal(barrier, inc=1, device_id=peer, device_id_type=MESH)
    pl.semaphore_wait(barrier, len(peers))


def _chip_exchange_on_sequencer(parts, name):
    n = len(parts)
    hbm = pltpu.MemorySpace.HBM
    ins = [jax.new_ref(p, memory_space=hbm) for p in parts]
    outs = [jax.empty_ref(jax.ShapeDtypeStruct(p.shape, p.dtype), memory_space=hbm) for p in parts]

    @_sequencer(name, ((n, 3), (n, 3), (n,)), 2)
    def launch(send, recv, lsem):
        x, y, c = lax.axis_index("x"), lax.axis_index("y"), lax.axis_index("c")
        mine = 2 * x + y
        chips = [(1 - x, y), (x, 1 - y), (1 - x, 1 - y)]
        _handshake([(*chip, c) for chip in chips])
        local = [pltpu.make_async_copy(ins[w].at[mine], outs[w].at[mine], lsem.at[w]) for w in range(n)]
        for cp in local:
            cp.start()
        sends = []
        for w in range(n):
            for j, (px, py) in enumerate(chips):
                cp = pltpu.make_async_remote_copy(ins[w].at[2 * px + py], outs[w].at[mine],
                                                  send.at[w, j], recv.at[w, j],
                                                  device_id=(px, py, c), device_id_type=MESH)
                cp.start()
                sends.append(cp)
        for w in range(n):
            for j, (px, py) in enumerate(chips):
                pltpu.make_async_remote_copy(ins[w].at[mine], outs[w].at[2 * px + py],
                                             send.at[w, j], recv.at[w, j],
                                             device_id=(px, py, c), device_id_type=MESH).wait_recv()
        for cp in sends:
            cp.wait_send()
        for cp in local:
            cp.wait()

    launch()
    return [o[...] for o in outs]


PAIR_CHUNK_BYTES = 3 * 1024 * 1024


def _pair_sum(grads, name, after=()):
    n = len(grads)
    n_after = len(after)
    shapes = [g.shape[1:] for g in grads]
    splits = []
    for rows, cols in shapes:
        ns = 1
        while rows * cols * 2 // ns > PAIR_CHUNK_BYTES and rows // (2 * ns) >= 16:
            ns *= 2
        splits.append(ns)

    def body(*refs):
        ins, refs = refs[:n], refs[n + n_after:]
        outs = refs[:n]
        lands = refs[n:2 * n]
        mines = refs[2 * n:3 * n]
        stages = refs[3 * n:4 * n]
        rsend, rrecv, lsem, osem = refs[4 * n:]
        x, y, c = lax.axis_index("x"), lax.axis_index("y"), lax.axis_index("c")
        remote = {}
        for w in range(n):
            for ch, (px, py) in enumerate(CHIPS):
                cp = pltpu.make_async_remote_copy(ins[w].at[4 * px + 2 * py + 1 - c], lands[w].at[ch],
                                                  rsend.at[w, ch], rrecv.at[w, ch],
                                                  device_id=(x, y, 1 - c), device_id_type=MESH)
                cp.start()
                remote[w, ch] = cp
        for w in range(n):
            rc = shapes[w][0] // splits[w]
            pieces = [(ch, p) for ch in range(N_CHIP) for p in range(splits[w])]

            def load(i, w=w, rc=rc, pieces=pieces):
                ch, p = pieces[i]
                px, py = CHIPS[ch]
                return pltpu.make_async_copy(ins[w].at[4 * px + 2 * py + c, pl.ds(p * rc, rc)],
                                             mines[w].at[i % 2], lsem.at[w, i % 2])

            def store(i, w=w, rc=rc, pieces=pieces):
                ch, p = pieces[i]
                return pltpu.make_async_copy(stages[w].at[i % 2], outs[w].at[ch, pl.ds(p * rc, rc)],
                                             osem.at[w, i % 2])

            load(0).start()
            for i, (ch, p) in enumerate(pieces):
                if i + 1 < len(pieces):
                    load(i + 1).start()
                load(i).wait()
                if p == 0:
                    remote[w, ch].wait_recv()
                if i >= 2:
                    store(i - 2).wait()
                theirs = lands[w][ch, p * rc:(p + 1) * rc, :]
                stages[w][i % 2] = (mines[w][i % 2].astype(F32) + theirs.astype(F32)).astype(BF16)
                store(i).start()
            for i in range(max(0, len(pieces) - 2), len(pieces)):
                store(i).wait()
        for cp in remote.values():
            cp.wait_send()

    hbm = pl.BlockSpec(memory_space=pl.ANY)
    scratch = [pltpu.VMEM((N_CHIP,) + sh, BF16) for sh in shapes]
    scratch += [pltpu.VMEM((2, sh[0] // ns, sh[1]), BF16) for sh, ns in zip(shapes, splits)] * 2
    scratch += [pltpu.SemaphoreType.DMA((n, N_CHIP)), pltpu.SemaphoreType.DMA((n, N_CHIP)),
                pltpu.SemaphoreType.DMA((n, 2)), pltpu.SemaphoreType.DMA((n, 2))]
    return pl.pallas_call(
        body, name=name, in_specs=[hbm] * (n + n_after), out_specs=[hbm] * n,
        out_shape=[jax.ShapeDtypeStruct((N_CHIP,) + sh, BF16) for sh in shapes],
        scratch_shapes=scratch,
        compiler_params=pltpu.CompilerParams(vmem_limit_bytes=VMEM_LIMIT),
    )(*grads, *after)


def _adamw_math(g, w, m, v):
    m2 = B1 * m + (1.0 - B1) * g
    v2 = B2 * v + (1.0 - B2) * (g * g)
    delta = -LR * ((m2 / BC1) / (jnp.sqrt(v2 / BC2) + AEPS) + WD * w)
    return delta, m2, v2


def _adamw(parts, w, m, v, *, name, after=(), tr=128):
    rows, cols = w.shape

    def body(p_ref, w_ref, m_ref, v_ref, *rest):
        g_ref, d_ref, mo_ref, vo_ref = rest[len(after):]
        g = p_ref[0].astype(F32)
        for b in range(1, N_CHIP):
            g = g + p_ref[b].astype(F32)
        delta, m2, v2 = _adamw_math(g, w_ref[...], m_ref[...], v_ref[...])
        g_ref[...] = g
        d_ref[...] = delta
        mo_ref[...] = m2
        vo_ref[...] = v2

    spec = pl.BlockSpec((tr, cols), lambda i: (i, 0))
    shape = jax.ShapeDtypeStruct((rows, cols), F32)
    return pl.pallas_call(
        body, name=name, grid=(rows // tr,),
        in_specs=[pl.BlockSpec((N_CHIP, tr, cols), lambda i: (0, i, 0)), spec, spec, spec]
        + [pl.BlockSpec(memory_space=pl.ANY)] * len(after),
        out_specs=[spec] * 4, out_shape=[shape] * 4,
        compiler_params=_params(("parallel",)),
    )(parts, w, m, v, *after)


def _small_update(part, w, m, v):
    rows = part.shape[0]

    def body(p_ref, w_ref, m_ref, v_ref, g_ref, d_ref, mo_ref, vo_ref, buf, send, recv):
        me, peers = _me_and_peers()
        buf[me] = p_ref[...]
        sends = []
        for k, (dev, _) in enumerate(peers):
            cp = pltpu.make_async_remote_copy(p_ref, buf.at[me], send.at[k], recv.at[k],
                                              device_id=dev, device_id_type=MESH)
            cp.start()
            sends.append(cp)
        for k, (dev, idx) in enumerate(peers):
            pltpu.make_async_remote_copy(p_ref, buf.at[idx], send.at[k], recv.at[k],
                                         device_id=dev, device_id_type=MESH).wait_recv()
        for cp in sends:
            cp.wait_send()
        g = buf[0]
        for b in range(1, N_DEV):
            g = g + buf[b]
        delta, m2, v2 = _adamw_math(g, w_ref[...], m_ref[...], v_ref[...])
        g_ref[...] = g
        d_ref[...] = delta
        mo_ref[...] = m2
        vo_ref[...] = v2

    vm = pl.BlockSpec(memory_space=pltpu.VMEM)
    shape = jax.ShapeDtypeStruct((rows, HD), F32)
    return pl.pallas_call(
        body, name="small_params_update",
        in_specs=[vm] * 4, out_specs=[vm] * 4, out_shape=[shape] * 4,
        scratch_shapes=[pltpu.VMEM((N_DEV, rows, HD), F32),
                        pltpu.SemaphoreType.DMA((N_DEV - 1,)), pltpu.SemaphoreType.DMA((N_DEV - 1,))],
    )(part, w, m, v)


def _pack_small(norm_mix, b_gate, qa, ka, qb, kb, rpb, norm_ffn):
    gains = jnp.concatenate([qa, ka, qb, kb, jnp.zeros((4, HD), F32)], axis=0)
    rpb_pad = jnp.pad(rpb.reshape(4 * (2 * WIN_R - 1), 2 * WIN_C - 1), ((0, 4), (0, HD - (2 * WIN_C - 1))))
    return jnp.concatenate([norm_mix.reshape(16, HD), b_gate.reshape(32, HD), gains, rpb_pad,
                            norm_ffn.reshape(16, HD)], axis=0)


def _unpack_small(p):
    norm_mix = p[0:16].reshape(1, D)
    b_gate = p[16:48].reshape(1, 2 * D)
    qa, ka, qb, kb = (p[48 + i:49 + i] for i in range(4))
    rpb = p[56:116, :2 * WIN_C - 1].reshape(1, 4, 2 * WIN_R - 1, 2 * WIN_C - 1)
    norm_ffn = p[120:136].reshape(1, D)
    return norm_mix, b_gate, qa, ka, qb, kb, rpb, norm_ffn


def kernel(x, norm_mix, w_in, b_gate, q_norm_a, k_norm_a, q_norm_b, k_norm_b, rpb_b, w_proj_a, w_proj_b, w_out, norm_ffn, w_up, w_down, loss_target, m_norm_mix, m_w_in, m_b_gate, m_q_norm_a, m_k_norm_a, m_q_norm_b, m_k_norm_b, m_rpb_b, m_w_proj_a, m_w_proj_b, m_w_out, m_norm_ffn, m_w_up, m_w_down, v_norm_mix, v_w_in, v_b_gate, v_q_norm_a, v_k_norm_a, v_q_norm_b, v_k_norm_b, v_rpb_b, v_w_proj_a, v_w_proj_b, v_w_out, v_norm_ffn, v_w_up, v_w_down):
    big_w = (w_in[0], w_proj_a[0], w_proj_b[0], w_out[0], w_up[0], w_down[0])
    big_m = (m_w_in[0], m_w_proj_a[0], m_w_proj_b[0], m_w_out[0], m_w_up[0], m_w_down[0])
    big_v = (v_w_in[0], v_w_proj_a[0], v_w_proj_b[0], v_w_out[0], v_w_up[0], v_w_down[0])
    names = ("w_in", "w_proj_a", "w_proj_b", "w_out", "w_up", "w_down")

    shards = [_cast_bf16(w) for w in big_w]
    g_in, = _gather_on_sequencer(shards[0:1], "gather_w_in")
    g_pa, g_pb, g_out = _gather_on_sequencer(shards[1:4], "gather_w_mix")
    g_up, = _gather_on_sequencer(shards[4:5], "gather_w_up")
    g_down, = _gather_on_sequencer(shards[5:6], "gather_w_down")
    small_w = _pack_small(norm_mix, b_gate, q_norm_a, k_norm_a, q_norm_b, k_norm_b, rpb_b, norm_ffn)
    small_m = _pack_small(m_norm_mix, m_b_gate, m_q_norm_a, m_k_norm_a, m_q_norm_b, m_k_norm_b, m_rpb_b, m_norm_ffn)
    small_v = _pack_small(v_norm_mix, v_b_gate, v_q_norm_a, v_k_norm_a, v_q_norm_b, v_k_norm_b, v_rpb_b, v_norm_ffn)

    upd = [None] * 6
    in_flight = {}

    def finish(after):
        done = []
        for i, r in in_flight.items():
            upd[i] = _adamw(r, big_w[i], big_m[i], big_v[i], name=f"adamw_{names[i]}", after=after)
            done.append(upd[i][0])
        in_flight.clear()
        return done

    def on_grads(tag, grads):
        new = list(grads.values())
        sums = _pair_sum(new, f"pair_sum_{tag}", after=finish(new))
        in_flight.update(zip(grads, _chip_exchange_on_sequencer(sums, f"chip_exchange_{tag}")))
        return sums

    loss, grad_x, small_g = _local_step(
        x[0], loss_target[0], norm_mix, b_gate, small_w[48:56], small_w[56:120], norm_ffn,
        g_in, g_pa, g_pb, g_out.reshape(D, D), g_up, g_down.reshape(D_FF, D), on_grads)

    g_norm_mix, g_ba, g_bb, g_gains, g_rpb, g_norm_ffn = small_g
    small_part = jnp.concatenate([g_norm_mix.reshape(16, HD), g_ba.reshape(16, HD), g_bb.reshape(16, HD),
                                  g_gains, g_rpb, g_norm_ffn.reshape(16, HD)], axis=0)
    s_g, s_d, s_m, s_v = (_unpack_small(t) for t in _small_update(small_part, small_w, small_m, small_v))

    finish([grad_x])
    b_g, b_d, b_m, b_v = ([u[j][None] for u in upd] for j in range(4))

    def order(small, big):
        nm, bg, qa, ka, qb, kb, rpb, nf = small
        w_in_, pa_, pb_, out_, up_, down_ = big
        return (nm, w_in_, bg, qa, ka, qb, kb, rpb, pa_, pb_, out_, nf, up_, down_)

    total = lax.psum(loss[0, 0], ("x", "y", "c"))
    return (total, grad_x[None], *order(s_g, b_g), *order(s_d, b_d), *order(s_m, b_m), *order(s_v, b_v))
```

```python
import functools

import jax
import jax.numpy as jnp
import numpy as np
from jax import lax
from jax.experimental import pallas as pl
from jax.experimental.pallas import tpu as pltpu
from jax.experimental.pallas import tpu_sc as plsc

F32 = jnp.float32
BF16 = jnp.bfloat16

N_DEV = 8
S = 2048
D = 2048
HD = 128
NH = 16
NH_A = 12
QKV = NH * HD
D_IN = 3 * QKV + 2 * D
D_BR = 512
D_FF = 4 * D
GRID_W = 64
ROWS = S // GRID_W
WIN_R = 8
WIN_C = 16
EPS = 1e-6
NEG = -1e30
SCALE = HD ** -0.5
ROPE_THETA = 10000.0
GROUPS_A = ((64, 1, 512), (256, 4, 768), (1024, 16, 2048))
QB = 256

LR, B1, B2, AEPS, WD, STEP = 0.001, 0.9, 0.999, 1e-08, 0.01, 10
BC1 = 1.0 - B1 ** STEP
BC2 = 1.0 - B2 ** STEP

VMEM_LIMIT = 56 * 1024 * 1024
MESH = pl.DeviceIdType.MESH

NN = (((1,), (0,)), ((), ()))
NT = (((1,), (1,)), ((), ()))
TN = (((0,), (0,)), ((), ()))


def _params(sem):
    return pltpu.CompilerParams(dimension_semantics=sem, vmem_limit_bytes=VMEM_LIMIT)


def _matmul(a, b, *, product, grid, a_spec, b_spec, epi, out_shape, out_specs, name,
            extra=(), extra_specs=(), after=()):
    n_extra = len(extra)

    def body(a_ref, b_ref, *rest):
        epi(product(a_ref, b_ref), rest[:n_extra], rest[n_extra + len(after):])

    return pl.pallas_call(
        body, name=name, grid=grid,
        in_specs=[a_spec, b_spec, *extra_specs, *[pl.BlockSpec(memory_space=pl.ANY)] * len(after)],
        out_specs=out_specs, out_shape=out_shape,
        compiler_params=_params(("parallel", "parallel")),
    )(a, b, *extra, *after)


def _dot(x, y, dims):
    return lax.dot_general(x, y, dims, preferred_element_type=F32)


def _epi_store(acc, ex, outs):
    outs[0][...] = acc.astype(outs[0].dtype)


def _epi_residual(acc, ex, outs):
    outs[0][...] = acc + ex[0][...]


def _mm_nn(a, b3, *, tm, tn, name, out_dtypes=(F32,), epi=_epi_store, extra=()):
    m, kdim = a.shape
    g, _, ng = b3.shape
    n = g * ng
    if tn <= ng:
        npg = ng // tn
        b_spec = pl.BlockSpec((None, kdim, tn), lambda j, i: (j // npg, 0, j % npg))

        def product(a_ref, b_ref):
            return _dot(a_ref[...], b_ref[...], NN)
    else:
        gb = tn // ng
        b_spec = pl.BlockSpec((gb, kdim, ng), lambda j, i: (j, 0, 0))

        def product(a_ref, b_ref):
            return jnp.concatenate([_dot(a_ref[...], b_ref[q], NN) for q in range(gb)], axis=1)

    tile = pl.BlockSpec((tm, tn), lambda j, i: (i, j))
    shapes = [jax.ShapeDtypeStruct((m, n), dt) for dt in out_dtypes]
    single = len(out_dtypes) == 1
    return _matmul(
        a, b3, product=product, grid=(n // tn, m // tm), epi=epi, name=name,
        a_spec=pl.BlockSpec((tm, kdim), lambda j, i: (i, 0)), b_spec=b_spec,
        extra=extra, extra_specs=[tile] * len(extra),
        out_shape=shapes[0] if single else shapes, out_specs=tile if single else [tile] * len(shapes))


def _mm_nt(a, b3, *, tm, tn, name, out_dtype=F32, epi=_epi_store, extra=(), after=()):
    m, kdim = a.shape
    g, n, kg = b3.shape

    def product(a_ref, b_ref):
        acc = _dot(a_ref[:, 0:kg], b_ref[0], NT)
        for q in range(1, g):
            acc = acc + _dot(a_ref[:, q * kg:(q + 1) * kg], b_ref[q], NT)
        return acc

    tile = pl.BlockSpec((tm, tn), lambda j, i: (i, j))
    return _matmul(
        a, b3, product=product, grid=(n // tn, m // tm), epi=epi, name=name,
        a_spec=pl.BlockSpec((tm, kdim), lambda j, i: (i, 0)),
        b_spec=pl.BlockSpec((g, tn, kg), lambda j, i: (0, j, 0)),
        extra=extra, extra_specs=[tile] * len(extra), after=after,
        out_shape=jax.ShapeDtypeStruct((m, n), out_dtype), out_specs=tile)


def _mm_tn(a, b, *, tm, tn, name, groups=1, out_dtype=BF16):
    t, m = a.shape
    _, n = b.shape
    ng = n // groups
    if tn <= ng:
        npg = ng // tn
        out_spec = pl.BlockSpec((None, tm, tn), lambda j, i: (j // npg, i, j % npg))
        epi = _epi_store

        def product(a_ref, b_ref):
            return _dot(a_ref[...], b_ref[...], TN)
    else:
        gb = tn // ng
        out_spec = pl.BlockSpec((gb, tm, ng), lambda j, i: (j, i, 0))

        def product(a_ref, b_ref):
            return [_dot(a_ref[...], b_ref[:, q * ng:(q + 1) * ng], TN) for q in range(gb)]

        def epi(parts, ex, outs):
            for q, part in enumerate(parts):
                outs[0][q] = part.astype(out_dtype)

    return _matmul(
        a, b, product=product, grid=(n // tn, m // tm), epi=epi, name=name,
        a_spec=pl.BlockSpec((t, tm), lambda j, i: (0, i)),
        b_spec=pl.BlockSpec((t, tn), lambda j, i: (0, j)),
        out_shape=jax.ShapeDtypeStruct((groups, m, ng), out_dtype), out_specs=out_spec)


def _rms_fwd(x, g, *, name, tr=256):
    def body(x_ref, g_ref, y_ref, r_ref):
        xv = x_ref[...]
        r = lax.rsqrt(jnp.mean(xv * xv, axis=-1, keepdims=True) + EPS)
        y_ref[...] = (xv * r * g_ref[...]).astype(BF16)
        r_ref[...] = r

    row = pl.BlockSpec((tr, D), lambda i: (i, 0))
    return pl.pallas_call(
        body, name=name, grid=(S // tr,),
        in_specs=[row, pl.BlockSpec((1, D), lambda i: (0, 0))],
        out_specs=[row, pl.BlockSpec((tr, 1), lambda i: (i, 0))],
        out_shape=[jax.ShapeDtypeStruct((S, D), BF16), jax.ShapeDtypeStruct((S, 1), F32)],
        compiler_params=_params(("parallel",)),
    )(x, g)


def _rms_bwd(dy, x, rstd, g, resid, *, name, tr=256):
    def body(dy_ref, x_ref, r_ref, g_ref, res_ref, dx_ref, dxb_ref, dg_ref):
        r = r_ref[...]
        xh = x_ref[...] * r
        dyv = dy_ref[...]
        t = dyv * g_ref[...]
        dx = r * (t - xh * jnp.mean(t * xh, axis=-1, keepdims=True)) + res_ref[...]
        dx_ref[...] = dx
        dxb_ref[...] = dx.astype(BF16)
        part = jnp.sum(dyv * xh, axis=0, keepdims=True)

        @pl.when(pl.program_id(0) == 0)
        def _():
            dg_ref[...] = part

        @pl.when(pl.program_id(0) > 0)
        def _():
            dg_ref[...] += part

    row = pl.BlockSpec((tr, D), lambda i: (i, 0))
    vec = pl.BlockSpec((1, D), lambda i: (0, 0))
    return pl.pallas_call(
        body, name=name, grid=(S // tr,),
        in_specs=[row, row, pl.BlockSpec((tr, 1), lambda i: (i, 0)), vec, row],
        out_specs=[row, row, vec],
        out_shape=[jax.ShapeDtypeStruct((S, D), F32), jax.ShapeDtypeStruct((S, D), BF16),
                   jax.ShapeDtypeStruct((1, D), F32)],
        compiler_params=_params(("arbitrary",)),
    )(dy, x, rstd, g, resid)


def _loss_head(h2, target, *, tr=256):
    def body(h_ref, t_ref, dy_ref, dyb_ref, loss_ref):
        e = h_ref[...] - t_ref[...]
        dy = e * (1.0 / D)
        dy_ref[...] = dy
        dyb_ref[...] = dy.astype(BF16)
        part = (0.5 / D) * jnp.sum(jnp.sum(e * e, axis=-1, keepdims=True), axis=0, keepdims=True)

        @pl.when(pl.program_id(0) == 0)
        def _():
            loss_ref[...] = part

        @pl.when(pl.program_id(0) > 0)
        def _():
            loss_ref[...] += part

    row = pl.BlockSpec((tr, D), lambda i: (i, 0))
    return pl.pallas_call(
        body, name="loss_head", grid=(S // tr,),
        in_specs=[row, row],
        out_specs=[row, row, pl.BlockSpec((1, 1), lambda i: (0, 0))],
        out_shape=[jax.ShapeDtypeStruct((S, D), F32), jax.ShapeDtypeStruct((S, D), BF16),
                   jax.ShapeDtypeStruct((1, 1), F32)],
        compiler_params=_params(("arbitrary",)),
    )(h2, target)


def _rope_tables():
    pos = np.arange(S, dtype=np.float32)
    inv = (ROPE_THETA ** (-np.arange(0, HD, 2, dtype=np.float32) / HD)).astype(np.float32)
    ang = pos[:, None] * inv[None, :]
    cos, sin = np.cos(ang), np.sin(ang)
    return (jnp.asarray(np.concatenate([cos, cos], axis=-1), F32),
            jnp.asarray(np.concatenate([-sin, sin], axis=-1), F32))


def _swap_halves(t):
    return pltpu.roll(t, HD // 2, axis=1)


def _qk_prep(proj, gains, cos2, sin2, *, tr=256):
    def body(q_ref, k_ref, v_ref, g_ref, c_ref, s_ref, qn_ref, kn_ref, vb_ref):
        cos, sin = c_ref[...], s_ref[...]
        for src, dst, row_a, row_b in ((q_ref, qn_ref, 0, 2), (k_ref, kn_ref, 1, 3)):
            for h in range(NH):
                cols = slice(h * HD, (h + 1) * HD)
                t = src[:, cols]
                r = lax.rsqrt(jnp.mean(t * t, axis=-1, keepdims=True) + EPS)
                if h < NH_A:
                    y = t * r * g_ref[row_a:row_a + 1, :]
                    y = y * cos + _swap_halves(y) * sin
                else:
                    y = t * r * g_ref[row_b:row_b + 1, :]
                dst[:, cols] = y.astype(BF16)
        vb_ref[...] = v_ref[...].astype(BF16)

    def blk(c):
        return pl.BlockSpec((tr, QKV), lambda i: (i, c))
    tab = pl.BlockSpec((tr, HD), lambda i: (i, 0))
    out = pl.BlockSpec((tr, QKV), lambda i: (i, 0))
    return pl.pallas_call(
        body, name="qk_prep", grid=(S // tr,),
        in_specs=[blk(0), blk(1), blk(2), pl.BlockSpec((8, HD), lambda i: (0, 0)), tab, tab],
        out_specs=[out, out, out],
        out_shape=[jax.ShapeDtypeStruct((S, QKV), BF16)] * 3,
        compiler_params=_params(("parallel",)),
    )(proj, proj, proj, gains, cos2, sin2)


def _qk_prep_bwd(dproj, proj, gains, cos2, sin2, dq_parts, dk_parts, dv_parts, *, tr=256):
    def body(dp_in, q_ref, k_ref, g_ref, c_ref, s_ref, *rest):
        dqs, dks, dvs = rest[0:4], rest[4:8], rest[8:12]
        dp_out, dg_ref = rest[12:14]
        del dp_in
        cos, sin = c_ref[...], s_ref[...]
        dg_rows = []
        for src, grads, base, row_a, row_b in ((q_ref, dqs, 0, 0, 2), (k_ref, dks, QKV, 1, 3)):
            dg_a = jnp.zeros((1, HD), F32)
            dg_b = jnp.zeros((1, HD), F32)
            for h in range(NH):
                cols = slice(h * HD, (h + 1) * HD)
                t = src[:, cols]
                dy = grads[h // 4][:, (h % 4) * HD:(h % 4 + 1) * HD]
                r = lax.rsqrt(jnp.mean(t * t, axis=-1, keepdims=True) + EPS)
                xh = t * r
                if h < NH_A:
                    dy = dy * cos - _swap_halves(dy) * sin
                    gain = g_ref[row_a:row_a + 1, :]
                    dg_a = dg_a + jnp.sum(dy * xh, axis=0, keepdims=True)
                else:
                    gain = g_ref[row_b:row_b + 1, :]
                    dg_b = dg_b + jnp.sum(dy * xh, axis=0, keepdims=True)
                u = dy * gain
                dx = r * (u - xh * jnp.mean(u * xh, axis=-1, keepdims=True))
                dp_out[:, base + h * HD:base + (h + 1) * HD] = dx.astype(BF16)
            dg_rows += [(row_a, dg_a), (row_b, dg_b)]
        for g4 in range(4):
            dp_out[:, 2 * QKV + g4 * D_BR:2 * QKV + (g4 + 1) * D_BR] = dvs[g4][...].astype(BF16)

        first = pl.program_id(0) == 0

        @pl.when(first)
        def _():
            dg_ref[...] = jnp.zeros((8, HD), F32)

        for row, val in dg_rows:
            dg_ref[row:row + 1, :] += val

    def blk(c):
        return pl.BlockSpec((tr, QKV), lambda i: (i, c))
    tab = pl.BlockSpec((tr, HD), lambda i: (i, 0))
    part = pl.BlockSpec((tr, D_BR), lambda i: (i, 0))
    gain_spec = pl.BlockSpec((8, HD), lambda i: (0, 0))
    return pl.pallas_call(
        body, name="qk_prep_bwd", grid=(S // tr,),
        in_specs=[pl.BlockSpec(memory_space=pl.ANY), blk(0), blk(1), gain_spec, tab, tab] + [part] * 12,
        out_specs=[pl.BlockSpec((tr, 3 * QKV), lambda i: (i, 0)), gain_spec],
        out_shape=[jax.ShapeDtypeStruct((S, D_IN), BF16), jax.ShapeDtypeStruct((8, HD), F32)],
        input_output_aliases={0: 0},
        compiler_params=_params(("arbitrary",)),
    )(dproj, proj, proj, gains, cos2, sin2, *dq_parts, *dk_parts, *dv_parts)


def _gate_fwd(proj, b_gate, ya, yb, *, tr=256):
    def body(la_ref, lb_ref, ba_ref, bb_ref, ya_ref, yb_ref, o_ref):
        ga = jax.nn.sigmoid(la_ref[...] + ba_ref[...])
        gb = jax.nn.sigmoid(lb_ref[...] + bb_ref[...])
        o_ref[...] = (ga * ya_ref[...] + gb * yb_ref[...]).astype(BF16)

    row = pl.BlockSpec((tr, D), lambda i: (i, 0))
    return pl.pallas_call(
        body, name="gate_fwd", grid=(S // tr,),
        in_specs=[pl.BlockSpec((tr, D), lambda i: (i, 3)), pl.BlockSpec((tr, D), lambda i: (i, 4)),
                  pl.BlockSpec((1, D), lambda i: (0, 0)), pl.BlockSpec((1, D), lambda i: (0, 1)),
                  row, row],
        out_specs=row, out_shape=jax.ShapeDtypeStruct((S, D), BF16),
        compiler_params=_params(("parallel",)),
    )(proj, proj, b_gate, b_gate, ya, yb)


def _gate_bwd(branch, dmixed, proj, b_gate, y, dproj, *, tr=256):
    aliased = dproj is not None

    def body(dm_ref, l_ref, b_ref, y_ref, *rest):
        dy_ref, dp_ref, db_ref = rest[-3:]
        g = jax.nn.sigmoid(l_ref[...] + b_ref[...])
        dm = dm_ref[...]
        dy_ref[...] = (dm * g).astype(BF16)
        dl = dm * y_ref[...] * g * (1.0 - g)
        dp_ref[...] = dl.astype(BF16)
        part = jnp.sum(dl, axis=0, keepdims=True)

        @pl.when(pl.program_id(0) == 0)
        def _():
            db_ref[...] = part

        @pl.when(pl.program_id(0) > 0)
        def _():
            db_ref[...] += part

    row = pl.BlockSpec((tr, D), lambda i: (i, 0))
    col = pl.BlockSpec((tr, D), lambda i: (i, 3 + branch))
    vec = pl.BlockSpec((1, D), lambda i: (0, 0))
    return pl.pallas_call(
        body, name=f"gate_bwd_{branch}", grid=(S // tr,),
        in_specs=[row, col, pl.BlockSpec((1, D), lambda i: (0, branch)), row]
        + ([pl.BlockSpec(memory_space=pl.ANY)] if aliased else []),
        out_specs=[row, col, vec],
        out_shape=[jax.ShapeDtypeStruct((S, D), BF16), jax.ShapeDtypeStruct((S, D_IN), BF16),
                   jax.ShapeDtypeStruct((1, D), F32)],
        input_output_aliases={4: 1} if aliased else {},
        compiler_params=_params(("arbitrary",)),
    )(dmixed, proj, b_gate, y, *([dproj] if aliased else []))


def _window_start(t0, wk):
    if wk == S:
        return 0
    return pl.multiple_of(jnp.clip(t0 - (wk - QB) // 2, 0, S - wk), 128)


def _scores_a(q, kw, t0, start, hs, dil, wk):
    s = lax.dot_general(q, kw, NT, preferred_element_type=F32) * SCALE
    qpos = t0 + lax.broadcasted_iota(jnp.int32, (QB, 1), 0)
    kpos = start + lax.broadcasted_iota(jnp.int32, (1, wk), 1)
    diff = kpos - qpos
    keep = (jnp.abs(diff) <= hs) & ((diff & (dil - 1)) == 0)
    return jnp.where(keep, s, NEG)


def _attn_a_fwd(qn, kn, vb, gi):
    hs, dil, wk = GROUPS_A[gi]

    def body(q_ref, k_ref, v_ref, o_ref, lse_ref):
        t0 = pl.program_id(1) * QB
        start = _window_start(t0, wk)
        s = _scores_a(q_ref[...], k_ref[pl.ds(start, wk), :], t0, start, hs, dil, wk)
        m = jnp.max(s, axis=-1, keepdims=True)
        p = jnp.exp(s - m)
        l = jnp.sum(p, axis=-1, keepdims=True)
        o = lax.dot_general(p.astype(BF16), v_ref[pl.ds(start, wk), :], NN, preferred_element_type=F32)
        o_ref[...] = o / l
        lse_ref[...] = m + jnp.log(l)

    full = pl.BlockSpec((S, HD), lambda h, i: (0, 4 * gi + h))
    return pl.pallas_call(
        body, name=f"attn_a_fwd_{gi}", grid=(4, S // QB),
        in_specs=[pl.BlockSpec((QB, HD), lambda h, i: (i, 4 * gi + h)), full, full],
        out_specs=[pl.BlockSpec((QB, HD), lambda h, i: (i, h)),
                   pl.BlockSpec((None, QB, 1), lambda h, i: (h, i, 0))],
        out_shape=[jax.ShapeDtypeStruct((S, D_BR), F32), jax.ShapeDtypeStruct((4, S, 1), F32)],
        compiler_params=_params(("parallel", "parallel")),
    )(qn, kn, vb)


def _combine_a(os, lses, *, tr=256):
    def body(o0, o1, o2, l0, l1, l2, oa_ref, lse_ref):
        for h in range(4):
            cols = slice(h * HD, (h + 1) * HD)
            a, b, c = l0[h], l1[h], l2[h]
            m = jnp.maximum(jnp.maximum(a, b), c)
            wa, wb, wc = jnp.exp(a - m), jnp.exp(b - m), jnp.exp(c - m)
            tot = wa + wb + wc
            oa_ref[:, cols] = ((wa * o0[:, cols] + wb * o1[:, cols] + wc * o2[:, cols]) / tot).astype(BF16)
            lse_ref[h] = m + jnp.log(tot)

    row = pl.BlockSpec((tr, D_BR), lambda i: (i, 0))
    stat = pl.BlockSpec((4, tr, 1), lambda i: (0, i, 0))
    return pl.pallas_call(
        body, name="combine_a", grid=(S // tr,),
        in_specs=[row] * 3 + [stat] * 3, out_specs=[row, stat],
        out_shape=[jax.ShapeDtypeStruct((S, D_BR), BF16), jax.ShapeDtypeStruct((4, S, 1), F32)],
        compiler_params=_params(("parallel",)),
    )(*os, *lses)


def _attn_a_bwd(qn, kn, vb, oa, doa, lse, gi):
    hs, dil, wk = GROUPS_A[gi]

    def body(q_ref, k_ref, v_ref, o_ref, do_ref, lse_ref, dq_ref, dk_ref, dv_ref):
        @pl.when(pl.program_id(1) == 0)
        def _():
            dk_ref[...] = jnp.zeros((S, HD), F32)
            dv_ref[...] = jnp.zeros((S, HD), F32)

        t0 = pl.program_id(1) * QB
        start = _window_start(t0, wk)
        q = q_ref[...]
        kw = k_ref[pl.ds(start, wk), :]
        vw = v_ref[pl.ds(start, wk), :]
        p = jnp.exp(_scores_a(q, kw, t0, start, hs, dil, wk) - lse_ref[...])
        do = do_ref[...]
        dob = do.astype(BF16)
        dsum = jnp.sum(do * o_ref[...].astype(F32), axis=-1, keepdims=True)
        dp = lax.dot_general(dob, vw, NT, preferred_element_type=F32)
        ds = (p * (dp - dsum) * SCALE).astype(BF16)
        dq_ref[...] = lax.dot_general(ds, kw, NN, preferred_element_type=F32)
        dk_ref[pl.ds(start, wk), :] += lax.dot_general(ds, q, TN, preferred_element_type=F32)
        dv_ref[pl.ds(start, wk), :] += lax.dot_general(p.astype(BF16), dob, TN, preferred_element_type=F32)

    full = pl.BlockSpec((S, HD), lambda h, i: (0, 4 * gi + h))
    blk = pl.BlockSpec((QB, HD), lambda h, i: (i, h))
    acc = pl.BlockSpec((S, HD), lambda h, i: (0, h))
    shape = jax.ShapeDtypeStruct((S, D_BR), F32)
    return pl.pallas_call(
        body, name=f"attn_a_bwd_{gi}", grid=(4, S // QB),
        in_specs=[pl.BlockSpec((QB, HD), lambda h, i: (i, 4 * gi + h)), full, full, blk, blk,
                  pl.BlockSpec((None, QB, 1), lambda h, i: (h, i, 0))],
        out_specs=[blk, acc, acc], out_shape=[shape, shape, shape],
        compiler_params=_params(("parallel", "arbitrary")),
    )(qn, kn, vb, oa, doa, lse)


KEYS_B = WIN_R * GRID_W
N_OFF = WIN_R


def _bias_constants():
    q = np.arange(GRID_W)[:, None]
    kc = np.arange(GRID_W)[None, :]
    dc = np.clip(kc - q, -(WIN_C - 1), WIN_C - 1) + (WIN_C - 1)
    expand = np.zeros((HD, GRID_W * GRID_W), np.float32)
    expand[dc.reshape(-1), np.arange(GRID_W * GRID_W)] = 1.0
    cs = np.clip(q - WIN_C // 2, 0, GRID_W - WIN_C)
    keep = ((kc >= cs) & (kc < cs + WIN_C)).reshape(1, -1).astype(np.float32)
    sel = np.zeros((64, 4 * N_OFF * WIN_R), np.float32)
    for h in range(4):
        for off in range(N_OFF):
            for j in range(WIN_R):
                sel[h * (2 * WIN_R - 1) + off + j, (h * N_OFF + off) * WIN_R + j] = 1.0
    return jnp.asarray(expand), jnp.asarray(keep), jnp.asarray(sel)


def _bias_expand(rpb_pad, expand, keep, sel):
    def body(r_ref, e_ref, k_ref, s_ref, o_ref):
        t = lax.dot_general(r_ref[...], e_ref[...], NN, precision=lax.Precision.HIGHEST,
                            preferred_element_type=F32)
        rows = lax.dot_general(s_ref[...], t, TN, precision=lax.Precision.HIGHEST,
                               preferred_element_type=F32)
        o_ref[...] = jnp.where(k_ref[...] > 0.5, rows, NEG)

    return pl.pallas_call(
        body, name="bias_expand",
        out_shape=jax.ShapeDtypeStruct((4 * N_OFF * WIN_R, GRID_W * GRID_W), F32),
        compiler_params=pltpu.CompilerParams(vmem_limit_bytes=VMEM_LIMIT),
    )(rpb_pad, expand, keep, sel)


def _bias_reduce(dbias_rows, expand, sel):
    def body(x_ref, e_ref, s_ref, o_ref):
        z = lax.dot_general(x_ref[...], e_ref[...], NT, precision=lax.Precision.HIGHEST,
                            preferred_element_type=F32)
        o_ref[...] = lax.dot_general(s_ref[...], z, NN, precision=lax.Precision.HIGHEST,
                                     preferred_element_type=F32)

    return pl.pallas_call(
        body, name="bias_reduce", out_shape=jax.ShapeDtypeStruct((64, HD), F32),
        compiler_params=pltpu.CompilerParams(vmem_limit_bytes=VMEM_LIMIT),
    )(dbias_rows, expand, sel)


def _rows_to_tab(rows):
    t = rows.reshape(4, N_OFF, WIN_R, GRID_W, GRID_W)
    return t.transpose(0, 1, 3, 2, 4).reshape(4, N_OFF, GRID_W, KEYS_B)


def _tab_to_rows(tab):
    t = tab.reshape(4, N_OFF, GRID_W, WIN_R, GRID_W)
    return t.transpose(0, 1, 3, 2, 4).reshape(4 * N_OFF * WIN_R, GRID_W * GRID_W)


def _row_window(r):
    r0 = jnp.clip(r - WIN_R // 2, 0, ROWS - WIN_R)
    off = r0 + (WIN_R - 1) - r
    return pl.multiple_of(r * GRID_W, GRID_W), pl.multiple_of(r0 * GRID_W, GRID_W), off


def _attn_b_fwd(qn, kn, vb, bias_tab):
    def body(q_ref, k_ref, v_ref, b_ref, o_ref, lse_ref):
        def row(r, carry):
            qs, ks, off = _row_window(r)
            q = q_ref[pl.ds(qs, GRID_W), :]
            s = lax.dot_general(q, k_ref[pl.ds(ks, KEYS_B), :], NT, preferred_element_type=F32) * SCALE
            s = s + b_ref[off]
            m = jnp.max(s, axis=-1, keepdims=True)
            p = jnp.exp(s - m)
            l = jnp.sum(p, axis=-1, keepdims=True)
            o = lax.dot_general(p.astype(BF16), v_ref[pl.ds(ks, KEYS_B), :], NN, preferred_element_type=F32)
            o_ref[pl.ds(qs, GRID_W), :] = (o / l).astype(BF16)
            lse_ref[pl.ds(qs, GRID_W), :] = m + jnp.log(l)
            return carry

        lax.fori_loop(0, ROWS, row, 0)

    full = pl.BlockSpec((S, HD), lambda h: (0, NH_A + h))
    return pl.pallas_call(
        body, name="attn_b_fwd", grid=(4,),
        in_specs=[full, full, full, pl.BlockSpec((None, N_OFF, GRID_W, KEYS_B), lambda h: (h, 0, 0, 0))],
        out_specs=[pl.BlockSpec((S, HD), lambda h: (0, h)), pl.BlockSpec((None, S, 1), lambda h: (h, 0, 0))],
        out_shape=[jax.ShapeDtypeStruct((S, D_BR), BF16), jax.ShapeDtypeStruct((4, S, 1), F32)],
        compiler_params=_params(("parallel",)),
    )(qn, kn, vb, bias_tab)


def _attn_b_bwd(qn, kn, vb, bias_tab, ob, dob, lse):
    def body(q_ref, k_ref, v_ref, b_ref, o_ref, do_ref, lse_ref, dq_ref, dk_ref, dv_ref, db_ref):
        dk_ref[...] = jnp.zeros((S, HD), F32)
        dv_ref[...] = jnp.zeros((S, HD), F32)
        db_ref[...] = jnp.zeros((N_OFF, GRID_W, KEYS_B), F32)

        def row(r, carry):
            qs, ks, off = _row_window(r)
            rows = pl.ds(qs, GRID_W)
            keys = pl.ds(ks, KEYS_B)
            q = q_ref[rows, :]
            kw = k_ref[keys, :]
            s = lax.dot_general(q, kw, NT, preferred_element_type=F32) * SCALE + b_ref[off]
            p = jnp.exp(s - lse_ref[rows, :])
            do = do_ref[rows, :]
            dobf = do.astype(BF16)
            dsum = jnp.sum(do * o_ref[rows, :].astype(F32), axis=-1, keepdims=True)
            dp = lax.dot_general(dobf, v_ref[keys, :], NT, preferred_element_type=F32)
            ds = p * (dp - dsum)
            db_ref[off] += ds
            dsb = (ds * SCALE).astype(BF16)
            dq_ref[rows, :] = lax.dot_general(dsb, kw, NN, preferred_element_type=F32)
            dk_ref[keys, :] += lax.dot_general(dsb, q, TN, preferred_element_type=F32)
            dv_ref[keys, :] += lax.dot_general(p.astype(BF16), dobf, TN, preferred_element_type=F32)
            return carry

        lax.fori_loop(0, ROWS, row, 0)

    full = pl.BlockSpec((S, HD), lambda h: (0, NH_A + h))
    slot = pl.BlockSpec((S, HD), lambda h: (0, h))
    tab = pl.BlockSpec((None, N_OFF, GRID_W, KEYS_B), lambda h: (h, 0, 0, 0))
    shape = jax.ShapeDtypeStruct((S, D_BR), F32)
    return pl.pallas_call(
        body, name="attn_b_bwd", grid=(4,),
        in_specs=[full, full, full, tab, slot, slot, pl.BlockSpec((None, S, 1), lambda h: (h, 0, 0))],
        out_specs=[slot, slot, slot, tab],
        out_shape=[shape, shape, shape, jax.ShapeDtypeStruct((4, N_OFF, GRID_W, KEYS_B), F32)],
        compiler_params=_params(("parallel",)),
    )(qn, kn, vb, bias_tab, ob, dob, lse)


def _epi_relu_sq(acc, ex, outs):
    u = jnp.maximum(acc, 0.0)
    outs[0][...] = u.astype(BF16)
    outs[1][...] = (u * u).astype(BF16)


def _epi_relu_sq_bwd(acc, ex, outs):
    outs[0][...] = (acc * (2.0 * ex[0][...].astype(F32))).astype(BF16)


def _local_step(x, target, norm_mix, b_gate, gains, rpb_pad, norm_ffn,
                w_in, w_pa, w_pb, w_out, w_up, w_down, on_grads):
    cos2, sin2 = _rope_tables()
    expand, keep, sel = _bias_constants()
    w_out3, w_down3 = w_out[None], w_down[None]

    xn, rstd1 = _rms_fwd(x, norm_mix, name="rms_mix")
    proj = _mm_nn(xn, w_in, tm=1024, tn=1280, name="proj")
    qn, kn, vb = _qk_prep(proj, gains, cos2, sin2)
    fwd_a = [_attn_a_fwd(qn, kn, vb, gi) for gi in range(3)]
    oa, lse_a = _combine_a([o for o, _ in fwd_a], [l for _, l in fwd_a])
    bias_tab = _rows_to_tab(_bias_expand(rpb_pad, expand, keep, sel))
    ob, lse_b = _attn_b_fwd(qn, kn, vb, bias_tab)
    ya = _mm_nn(oa, w_pa, tm=512, tn=D, name="proj_a")
    yb = _mm_nn(ob, w_pb, tm=512, tn=D, name="proj_b")
    mixed = _gate_fwd(proj, b_gate, ya, yb)
    h1 = _mm_nn(mixed, w_out3, tm=1024, tn=1024, name="out_proj", epi=_epi_residual, extra=(x,))
    hn, rstd2 = _rms_fwd(h1, norm_ffn, name="rms_ffn")
    u, usq = _mm_nn(hn, w_up, tm=1024, tn=1024, name="ffn_up", epi=_epi_relu_sq,
                    out_dtypes=(BF16, BF16))
    h2 = _mm_nn(usq, w_down3, tm=512, tn=512, name="ffn_down", epi=_epi_residual, extra=(h1,))
    dy, dyb, loss = _loss_head(h2, target)

    g_down = _mm_tn(usq, dyb, tm=1024, tn=1024, name="grad_w_down")
    sent = on_grads("w_down", {5: g_down.reshape(N_DEV, D_FF // N_DEV, D)})
    du = _mm_nt(dyb, w_down3, tm=1024, tn=1024, name="ffn_down_bwd", out_dtype=BF16,
                epi=_epi_relu_sq_bwd, extra=(u,), after=sent)
    g_up = _mm_tn(hn, du, tm=1024, tn=1024, groups=N_DEV, name="grad_w_up")
    sent = on_grads("w_up", {4: g_up})
    dhn = _mm_nt(du, w_up, tm=512, tn=512, name="ffn_up_bwd", after=sent)
    dh1, dh1b, g_norm_ffn = _rms_bwd(dhn, h1, rstd2, norm_ffn, dy, name="rms_ffn_bwd")

    g_out = _mm_tn(mixed, dh1b, tm=1024, tn=1024, name="grad_w_out")
    dmixed = _mm_nt(dh1b, w_out3, tm=1024, tn=1024, name="out_proj_bwd")
    dya, dproj, g_ba = _gate_bwd(0, dmixed, proj, b_gate, ya, None)
    dyb2, dproj, g_bb = _gate_bwd(1, dmixed, proj, b_gate, yb, dproj)
    g_pa = _mm_tn(oa, dya, tm=D_BR, tn=1024, groups=N_DEV, name="grad_w_proj_a")
    g_pb = _mm_tn(ob, dyb2, tm=D_BR, tn=1024, groups=N_DEV, name="grad_w_proj_b")
    sent = on_grads("w_mix", {3: g_out.reshape(N_DEV, D // N_DEV, D), 1: g_pa, 2: g_pb})
    doa = _mm_nt(dya, w_pa, tm=1024, tn=D_BR, name="proj_a_bwd", after=sent)
    dob = _mm_nt(dyb2, w_pb, tm=1024, tn=D_BR, name="proj_b_bwd")
    bwd = [_attn_a_bwd(qn, kn, vb, oa, doa, lse_a, gi) for gi in range(3)]
    dqb, dkb, dvb, dbias = _attn_b_bwd(qn, kn, vb, bias_tab, ob, dob, lse_b)
    g_rpb = _bias_reduce(_tab_to_rows(dbias), expand, sel)
    dproj, g_gains = _qk_prep_bwd(dproj, proj, gains, cos2, sin2,
                                  [b[0] for b in bwd] + [dqb], [b[1] for b in bwd] + [dkb],
                                  [b[2] for b in bwd] + [dvb])
    g_in = _mm_tn(xn, dproj, tm=1024, tn=1280, groups=N_DEV, name="grad_w_in")
    sent = on_grads("w_in", {0: g_in})
    dxn = _mm_nt(dproj, w_in, tm=256, tn=512, name="proj_bwd", after=sent)
    grad_x, _, g_norm_mix = _rms_bwd(dxn, x, rstd1, norm_mix, dh1, name="rms_mix_bwd")

    small = (g_norm_mix, g_ba, g_bb, g_gains, g_rpb, g_norm_ffn)
    return loss, grad_x, small


def _cast_bf16(w, *, tr=256):
    rows, cols = w.shape
    tr = min(tr, rows)

    def body(w_ref, o_ref):
        o_ref[...] = w_ref[...].astype(BF16)

    spec = pl.BlockSpec((tr, cols), lambda i: (i, 0))
    return pl.pallas_call(
        body, name=f"cast_{rows}x{cols}", grid=(rows // tr,), in_specs=[spec], out_specs=spec,
        out_shape=jax.ShapeDtypeStruct((rows, cols), BF16), compiler_params=_params(("parallel",)),
    )(w)


def _me_and_peers():
    x, y, c = lax.axis_index("x"), lax.axis_index("y"), lax.axis_index("c")
    me = 4 * x + 2 * y + c
    peers = []
    for k in range(1, N_DEV):
        px = 1 - x if k & 4 else x
        py = 1 - y if k & 2 else y
        pc = 1 - c if k & 1 else c
        peers.append(((px, py, pc), 4 * px + 2 * py + pc))
    return me, peers


def _gather_on_sequencer(shards, name):
    n = len(shards)
    hbm = pltpu.MemorySpace.HBM
    ins = [jax.new_ref(s, memory_space=hbm) for s in shards]
    outs = [jax.empty_ref(jax.ShapeDtypeStruct((N_DEV,) + s.shape, s.dtype), memory_space=hbm) for s in shards]

    @pl.kernel(mesh=plsc.ScalarSubcoreMesh(axis_name="seq", num_cores=1), name=name,
               scratch_types=(pltpu.SemaphoreType.DMA((n, N_DEV - 1)), pltpu.SemaphoreType.DMA((n, N_DEV - 1)),
                              pltpu.SemaphoreType.DMA((n,))),
               compiler_params=pltpu.CompilerParams(collective_id=0))
    def launch(send, recv, lsem):
        x, y, c = lax.axis_index("x"), lax.axis_index("y"), lax.axis_index("c")
        me, sibling = (x, y, c), (x, y, 1 - c)
        chips = [(1 - x, y), (x, 1 - y), (1 - x, 1 - y)]
        barrier = pltpu.get_barrier_semaphore()
        for peer in [sibling] + [(*chip, c) for chip in chips]:
            pl.semaphore_signal(barrier, inc=1, device_id=peer, device_id_type=MESH)
        pl.semaphore_wait(barrier, 4)

        def copy(w, k, block, to, src=None):
            px, py, pc = block
            dst = outs[w].at[4 * px + 2 * py + pc]
            return pltpu.make_async_remote_copy(dst if src is None else src, dst, send.at[w, k], recv.at[w, k],
                                                device_id=to, device_id_type=MESH)

        local = [pltpu.make_async_copy(ins[w], outs[w].at[4 * x + 2 * y + c], lsem.at[w]) for w in range(n)]
        for cp in local:
            cp.start()
        first = []
        for w in range(n):
            first += [copy(w, 1 + j, me, (*chip, c), src=ins[w]) for j, chip in enumerate(chips)]
            first.append(copy(w, 0, me, sibling, src=ins[w]))
        for cp in first:
            cp.start()
        passed = []
        for w in range(n):
            for j, chip in enumerate(chips):
                copy(w, 1 + j, (*chip, c), me).wait_recv()
                cp = copy(w, 4 + j, (*chip, c), sibling)
                cp.start()
                passed.append(cp)
        for w in range(n):
            copy(w, 0, sibling, me).wait_recv()
            for j, chip in enumerate(chips):
                copy(w, 4 + j, (*chip, 1 - c), me).wait_recv()
        for cp in first + passed:
            cp.wait_send()
        for cp in local:
            cp.wait()

    launch()
    return [o[...] for o in outs]


N_CHIP = 4
CHIPS = ((0, 0), (0, 1), (1, 0), (1, 1))


def _sequencer(name, n_sems, collective_id):
    return functools.partial(
        pl.kernel, mesh=plsc.ScalarSubcoreMesh(axis_name="seq", num_cores=1), name=name,
        scratch_types=tuple(pltpu.SemaphoreType.DMA(s) for s in n_sems),
        compiler_params=pltpu.CompilerParams(collective_id=collective_id))


def _handshake(peers):
    barrier = pltpu.get_barrier_semaphore()
    for peer in peers:
        pl.semaphore_signal(barrier, inc=1, device_id=peer, device_id_type=MESH)
    pl.semaphore_wait(barrier, len(peers))


def _chip_exchange_on_sequencer(parts, name):
    n = len(parts)
    hbm = pltpu.MemorySpace.HBM
    ins = [jax.new_ref(p, memory_space=hbm) for p in parts]
    outs = [jax.empty_ref(jax.ShapeDtypeStruct(p.shape, p.dtype), memory_space=hbm) for p in parts]

    @_sequencer(name, ((n, 3), (n, 3), (n,)), 2)
    def launch(send, recv, lsem):
        x, y, c = lax.axis_index("x"), lax.axis_index("y"), lax.axis_index("c")
        mine = 2 * x + y
        chips = [(1 - x, y), (x, 1 - y), (1 - x, 1 - y)]
        _handshake([(*chip, c) for chip in chips])
        local = [pltpu.make_async_copy(ins[w].at[mine], outs[w].at[mine], lsem.at[w]) for w in range(n)]
        for cp in local:
            cp.start()
        sends = []
        for w in range(n):
            for j, (px, py) in enumerate(chips):
                cp = pltpu.make_async_remote_copy(ins[w].at[2 * px + py], outs[w].at[mine],
                                                  send.at[w, j], recv.at[w, j],
                                                  device_id=(px, py, c), device_id_type=MESH)
                cp.start()
                sends.append(cp)
        for w in range(n):
            for j, (px, py) in enumerate(chips):
                pltpu.make_async_remote_copy(ins[w].at[mine], outs[w].at[2 * px + py],
                                             send.at[w, j], recv.at[w, j],
                                             device_id=(px, py, c), device_id_type=MESH).wait_recv()
        for cp in sends:
            cp.wait_send()
        for cp in local:
            cp.wait()

    launch()
    return [o[...] for o in outs]


PAIR_CHUNK_BYTES = 3 * 1024 * 1024


def _pair_sum(grads, name, after=()):
    n = len(grads)
    n_after = len(after)
    shapes = [g.shape[1:] for g in grads]
    splits = []
    for rows, cols in shapes:
        ns = 1
        while rows * cols * 2 // ns > PAIR_CHUNK_BYTES and rows // (2 * ns) >= 16:
            ns *= 2
        splits.append(ns)

    def body(*refs):
        ins, refs = refs[:n], refs[n + n_after:]
        outs = refs[:n]
        lands = refs[n:2 * n]
        mines = refs[2 * n:3 * n]
        stages = refs[3 * n:4 * n]
        rsend, rrecv, lsem, osem = refs[4 * n:]
        x, y, c = lax.axis_index("x"), lax.axis_index("y"), lax.axis_index("c")
        remote = {}
        for w in range(n):
            for ch, (px, py) in enumerate(CHIPS):
                cp = pltpu.make_async_remote_copy(ins[w].at[4 * px + 2 * py + 1 - c], lands[w].at[ch],
                                                  rsend.at[w, ch], rrecv.at[w, ch],
                                                  device_id=(x, y, 1 - c), device_id_type=MESH)
                cp.start()
                remote[w, ch] = cp
        for w in range(n):
            rc = shapes[w][0] // splits[w]
            pieces = [(ch, p) for ch in range(N_CHIP) for p in range(splits[w])]

            def load(i, w=w, rc=rc, pieces=pieces):
                ch, p = pieces[i]
                px, py = CHIPS[ch]
                return pltpu.make_async_copy(ins[w].at[4 * px + 2 * py + c, pl.ds(p * rc, rc)],
                                             mines[w].at[i % 2], lsem.at[w, i % 2])

            def store(i, w=w, rc=rc, pieces=pieces):
                ch, p = pieces[i]
                return pltpu.make_async_copy(stages[w].at[i % 2], outs[w].at[ch, pl.ds(p * rc, rc)],
                                             osem.at[w, i % 2])

            load(0).start()
            for i, (ch, p) in enumerate(pieces):
                if i + 1 < len(pieces):
                    load(i + 1).start()
                load(i).wait()
                if p == 0:
                    remote[w, ch].wait_recv()
                if i >= 2:
                    store(i - 2).wait()
                theirs = lands[w][ch, p * rc:(p + 1) * rc, :]
                stages[w][i % 2] = (mines[w][i % 2].astype(F32) + theirs.astype(F32)).astype(BF16)
                store(i).start()
            for i in range(max(0, len(pieces) - 2), len(pieces)):
                store(i).wait()
        for cp in remote.values():
            cp.wait_send()

    hbm = pl.BlockSpec(memory_space=pl.ANY)
    scratch = [pltpu.VMEM((N_CHIP,) + sh, BF16) for sh in shapes]
    scratch += [pltpu.VMEM((2, sh[0] // ns, sh[1]), BF16) for sh, ns in zip(shapes, splits)] * 2
    scratch += [pltpu.SemaphoreType.DMA((n, N_CHIP)), pltpu.SemaphoreType.DMA((n, N_CHIP)),
                pltpu.SemaphoreType.DMA((n, 2)), pltpu.SemaphoreType.DMA((n, 2))]
    return pl.pallas_call(
        body, name=name, in_specs=[hbm] * (n + n_after), out_specs=[hbm] * n,
        out_shape=[jax.ShapeDtypeStruct((N_CHIP,) + sh, BF16) for sh in shapes],
        scratch_shapes=scratch,
        compiler_params=pltpu.CompilerParams(vmem_limit_bytes=VMEM_LIMIT),
    )(*grads, *after)


def _adamw_math(g, w, m, v):
    m2 = B1 * m + (1.0 - B1) * g
    v2 = B2 * v + (1.0 - B2) * (g * g)
    delta = -LR * ((m2 / BC1) / (jnp.sqrt(v2 / BC2) + AEPS) + WD * w)
    return delta, m2, v2


def _adamw(parts, w, m, v, *, name, after=(), tr=128):
    rows, cols = w.shape

    def body(p_ref, w_ref, m_ref, v_ref, *rest):
        g_ref, d_ref, mo_ref, vo_ref = rest[len(after):]
        g = p_ref[0].astype(F32)
        for b in range(1, N_CHIP):
            g = g + p_ref[b].astype(F32)
        delta, m2, v2 = _adamw_math(g, w_ref[...], m_ref[...], v_ref[...])
        g_ref[...] = g
        d_ref[...] = delta
        mo_ref[...] = m2
        vo_ref[...] = v2

    spec = pl.BlockSpec((tr, cols), lambda i: (i, 0))
    shape = jax.ShapeDtypeStruct((rows, cols), F32)
    return pl.pallas_call(
        body, name=name, grid=(rows // tr,),
        in_specs=[pl.BlockSpec((N_CHIP, tr, cols), lambda i: (0, i, 0)), spec, spec, spec]
        + [pl.BlockSpec(memory_space=pl.ANY)] * len(after),
        out_specs=[spec] * 4, out_shape=[shape] * 4,
        compiler_params=_params(("parallel",)),
    )(parts, w, m, v, *after)


def _small_update(part, w, m, v):
    rows = part.shape[0]

    def body(p_ref, w_ref, m_ref, v_ref, g_ref, d_ref, mo_ref, vo_ref, buf, send, recv):
        me, peers = _me_and_peers()
        buf[me] = p_ref[...]
        sends = []
        for k, (dev, _) in enumerate(peers):
            cp = pltpu.make_async_remote_copy(p_ref, buf.at[me], send.at[k], recv.at[k],
                                              device_id=dev, device_id_type=MESH)
            cp.start()
            sends.append(cp)
        for k, (dev, idx) in enumerate(peers):
            pltpu.make_async_remote_copy(p_ref, buf.at[idx], send.at[k], recv.at[k],
                                         device_id=dev, device_id_type=MESH).wait_recv()
        for cp in sends:
            cp.wait_send()
        g = buf[0]
        for b in range(1, N_DEV):
            g = g + buf[b]
        delta, m2, v2 = _adamw_math(g, w_ref[...], m_ref[...], v_ref[...])
        g_ref[...] = g
        d_ref[...] = delta
        mo_ref[...] = m2
        vo_ref[...] = v2

    vm = pl.BlockSpec(memory_space=pltpu.VMEM)
    shape = jax.ShapeDtypeStruct((rows, HD), F32)
    return pl.pallas_call(
        body, name="small_params_update",
        in_specs=[vm] * 4, out_specs=[vm] * 4, out_shape=[shape] * 4,
        scratch_shapes=[pltpu.VMEM((N_DEV, rows, HD), F32),
                        pltpu.SemaphoreType.DMA((N_DEV - 1,)), pltpu.SemaphoreType.DMA((N_DEV - 1,))],
    )(part, w, m, v)


def _pack_small(norm_mix, b_gate, qa, ka, qb, kb, rpb, norm_ffn):
    gains = jnp.concatenate([qa, ka, qb, kb, jnp.zeros((4, HD), F32)], axis=0)
    rpb_pad = jnp.pad(rpb.reshape(4 * (2 * WIN_R - 1), 2 * WIN_C - 1), ((0, 4), (0, HD - (2 * WIN_C - 1))))
    return jnp.concatenate([norm_mix.reshape(16, HD), b_gate.reshape(32, HD), gains, rpb_pad,
                            norm_ffn.reshape(16, HD)], axis=0)


def _unpack_small(p):
    norm_mix = p[0:16].reshape(1, D)
    b_gate = p[16:48].reshape(1, 2 * D)
    qa, ka, qb, kb = (p[48 + i:49 + i] for i in range(4))
    rpb = p[56:116, :2 * WIN_C - 1].reshape(1, 4, 2 * WIN_R - 1, 2 * WIN_C - 1)
    norm_ffn = p[120:136].reshape(1, D)
    return norm_mix, b_gate, qa, ka, qb, kb, rpb, norm_ffn


def kernel(x, norm_mix, w_in, b_gate, q_norm_a, k_norm_a, q_norm_b, k_norm_b, rpb_b, w_proj_a, w_proj_b, w_out, norm_ffn, w_up, w_down, loss_target, m_norm_mix, m_w_in, m_b_gate, m_q_norm_a, m_k_norm_a, m_q_norm_b, m_k_norm_b, m_rpb_b, m_w_proj_a, m_w_proj_b, m_w_out, m_norm_ffn, m_w_up, m_w_down, v_norm_mix, v_w_in, v_b_gate, v_q_norm_a, v_k_norm_a, v_q_norm_b, v_k_norm_b, v_rpb_b, v_w_proj_a, v_w_proj_b, v_w_out, v_norm_ffn, v_w_up, v_w_down):
    big_w = (w_in[0], w_proj_a[0], w_proj_b[0], w_out[0], w_up[0], w_down[0])
    big_m = (m_w_in[0], m_w_proj_a[0], m_w_proj_b[0], m_w_out[0], m_w_up[0], m_w_down[0])
    big_v = (v_w_in[0], v_w_proj_a[0], v_w_proj_b[0], v_w_out[0], v_w_up[0], v_w_down[0])
    names = ("w_in", "w_proj_a", "w_proj_b", "w_out", "w_up", "w_down")

    shards = [_cast_bf16(w) for w in big_w]
    g_in, = _gather_on_sequencer(shards[0:1], "gather_w_in")
    g_pa, g_pb, g_out = _gather_on_sequencer(shards[1:4], "gather_w_mix")
    g_up, = _gather_on_sequencer(shards[4:5], "gather_w_up")
    g_down, = _gather_on_sequencer(shards[5:6], "gather_w_down")
    small_w = _pack_small(norm_mix, b_gate, q_norm_a, k_norm_a, q_norm_b, k_norm_b, rpb_b, norm_ffn)
    small_m = _pack_small(m_norm_mix, m_b_gate, m_q_norm_a, m_k_norm_a, m_q_norm_b, m_k_norm_b, m_rpb_b, m_norm_ffn)
    small_v = _pack_small(v_norm_mix, v_b_gate, v_q_norm_a, v_k_norm_a, v_q_norm_b, v_k_norm_b, v_rpb_b, v_norm_ffn)

    upd = [None] * 6
    in_flight = {}

    def finish(after):
        done = []
        for i, r in in_flight.items():
            upd[i] = _adamw(r, big_w[i], big_m[i], big_v[i], name=f"adamw_{names[i]}", after=after)
            done.append(upd[i][0])
        in_flight.clear()
        return done

    def on_grads(tag, grads):
        new = list(grads.values())
        sums = _pair_sum(new, f"pair_sum_{tag}", after=finish(new))
        in_flight.update(zip(grads, _chip_exchange_on_sequencer(sums, f"chip_exchange_{tag}")))
        return sums

    loss, grad_x, small_g = _local_step(
        x[0], loss_target[0], norm_mix, b_gate, small_w[48:56], small_w[56:120], norm_ffn,
        g_in, g_pa, g_pb, g_out.reshape(D, D), g_up, g_down.reshape(D_FF, D), on_grads)

    g_norm_mix, g_ba, g_bb, g_gains, g_rpb, g_norm_ffn = small_g
    small_part = jnp.concatenate([g_norm_mix.reshape(16, HD), g_ba.reshape(16, HD), g_bb.reshape(16, HD),
                                  g_gains, g_rpb, g_norm_ffn.reshape(16, HD)], axis=0)
    s_g, s_d, s_m, s_v = (_unpack_small(t) for t in _small_update(small_part, small_w, small_m, small_v))

    finish([grad_x])
    b_g, b_d, b_m, b_v = ([u[j][None] for u in upd] for j in range(4))

    def order(small, big):
        nm, bg, qa, ka, qb, kb, rpb, nf = small
        w_in_, pa_, pb_, out_, up_, down_ = big
        return (nm, w_in_, bg, qa, ka, qb, kb, rpb, pa_, pb_, out_, nf, up_, down_)

    total = lax.psum(loss[0, 0], ("x", "y", "c"))
    return (total, grad_x[None], *order(s_g, b_g), *order(s_d, b_d), *order(s_m, b_m), *order(s_v, b_v))
```

```python
import functools

import jax
import jax.numpy as jnp
import numpy as np
from jax import lax
from jax.experimental import pallas as pl
from jax.experimental.pallas import tpu as pltpu
from jax.experimental.pallas import tpu_sc as plsc

F32 = jnp.float32
BF16 = jnp.bfloat16

N_DEV = 8
S = 2048
D = 2048
HD = 128
NH = 16
NH_A = 12
QKV = NH * HD
D_IN = 3 * QKV + 2 * D
D_BR = 512
D_FF = 4 * D
GRID_W = 64
ROWS = S // GRID_W
WIN_R = 8
WIN_C = 16
EPS = 1e-6
NEG = -1e30
SCALE = HD ** -0.5
ROPE_THETA = 10000.0
GROUPS_A = ((64, 1, 512), (256, 4, 768), (1024, 16, 2048))
QB = 256

LR, B1, B2, AEPS, WD, STEP = 0.001, 0.9, 0.999, 1e-08, 0.01, 10
BC1 = 1.0 - B1 ** STEP
BC2 = 1.0 - B2 ** STEP

VMEM_LIMIT = 56 * 1024 * 1024
MESH = pl.DeviceIdType.MESH

NN = (((1,), (0,)), ((), ()))
NT = (((1,), (1,)), ((), ()))
TN = (((0,), (0,)), ((), ()))


def _params(sem):
    return pltpu.CompilerParams(dimension_semantics=sem, vmem_limit_bytes=VMEM_LIMIT)


def _matmul(a, b, *, product, grid, a_spec, b_spec, epi, out_shape, out_specs, name,
            extra=(), extra_specs=(), after=(), carried=False):
    n_extra = len(extra)

    def body(a_ref, b_ref, *rest):
        epi(product(a_ref, b_ref), rest[:n_extra], rest[n_extra + len(after):])

    return pl.pallas_call(
        body, name=name, grid=grid,
        in_specs=[a_spec, b_spec, *extra_specs, *[pl.BlockSpec(memory_space=pl.ANY)] * len(after)],
        out_specs=out_specs, out_shape=out_shape,
        compiler_params=_params(("arbitrary", "arbitrary") if carried else ("parallel", "parallel")),
    )(a, b, *extra, *after)


def _dot(x, y, dims):
    return lax.dot_general(x, y, dims, preferred_element_type=F32)


def _epi_store(acc, ex, outs):
    outs[0][...] = acc.astype(outs[0].dtype)


def _epi_residual(acc, ex, outs):
    outs[0][...] = acc + ex[0][...]


def _mm_nn(a, b3, *, tm, tn, name, out_dtypes=(F32,), epi=_epi_store, extra=(), total=False):
    m, kdim = a.shape
    g, _, ng = b3.shape
    n = g * ng
    if tn <= ng:
        npg = ng // tn
        b_spec = pl.BlockSpec((None, kdim, tn), lambda j, i: (j // npg, 0, j % npg))

        def product(a_ref, b_ref):
            return _dot(a_ref[...], b_ref[...], NN)
    else:
        gb = tn // ng
        b_spec = pl.BlockSpec((gb, kdim, ng), lambda j, i: (j, 0, 0))

        def product(a_ref, b_ref):
            return jnp.concatenate([_dot(a_ref[...], b_ref[q], NN) for q in range(gb)], axis=1)

    tile = pl.BlockSpec((tm, tn), lambda j, i: (i, j))
    shapes = [jax.ShapeDtypeStruct((m, n), dt) for dt in out_dtypes]
    specs = [tile] * len(shapes)
    if total:
        shapes.append(jax.ShapeDtypeStruct((1, 1), F32))
        specs.append(pl.BlockSpec((1, 1), lambda j, i: (0, 0)))
    single = len(shapes) == 1
    return _matmul(
        a, b3, product=product, grid=(n // tn, m // tm), epi=epi, name=name, carried=total,
        a_spec=pl.BlockSpec((tm, kdim), lambda j, i: (i, 0)), b_spec=b_spec,
        extra=extra, extra_specs=[tile] * len(extra),
        out_shape=shapes[0] if single else shapes, out_specs=specs[0] if single else specs)


def _mm_nt(a, b3, *, tm, tn, name, out_dtype=F32, epi=_epi_store, extra=(), after=()):
    m, kdim = a.shape
    g, n, kg = b3.shape

    def product(a_ref, b_ref):
        acc = _dot(a_ref[:, 0:kg], b_ref[0], NT)
        for q in range(1, g):
            acc = acc + _dot(a_ref[:, q * kg:(q + 1) * kg], b_ref[q], NT)
        return acc

    tile = pl.BlockSpec((tm, tn), lambda j, i: (i, j))
    return _matmul(
        a, b3, product=product, grid=(n // tn, m // tm), epi=epi, name=name,
        a_spec=pl.BlockSpec((tm, kdim), lambda j, i: (i, 0)),
        b_spec=pl.BlockSpec((g, tn, kg), lambda j, i: (0, j, 0)),
        extra=extra, extra_specs=[tile] * len(extra), after=after,
        out_shape=jax.ShapeDtypeStruct((m, n), out_dtype), out_specs=tile)


def _mm_tn(a, b, *, tm, tn, name, groups=1, out_dtype=BF16):
    t, m = a.shape
    _, n = b.shape
    ng = n // groups
    if tn <= ng:
        npg = ng // tn
        out_spec = pl.BlockSpec((None, tm, tn), lambda j, i: (j // npg, i, j % npg))
        epi = _epi_store

        def product(a_ref, b_ref):
            return _dot(a_ref[...], b_ref[...], TN)
    else:
        gb = tn // ng
        out_spec = pl.BlockSpec((gb, tm, ng), lambda j, i: (j, i, 0))

        def product(a_ref, b_ref):
            return [_dot(a_ref[...], b_ref[:, q * ng:(q + 1) * ng], TN) for q in range(gb)]

        def epi(parts, ex, outs):
            for q, part in enumerate(parts):
                outs[0][q] = part.astype(out_dtype)

    return _matmul(
        a, b, product=product, grid=(n // tn, m // tm), epi=epi, name=name,
        a_spec=pl.BlockSpec((t, tm), lambda j, i: (0, i)),
        b_spec=pl.BlockSpec((t, tn), lambda j, i: (0, j)),
        out_shape=jax.ShapeDtypeStruct((groups, m, ng), out_dtype), out_specs=out_spec)


def _rms_fwd(x, g, *, name, tr=256):
    def body(x_ref, g_ref, y_ref, r_ref):
        xv = x_ref[...]
        r = lax.rsqrt(jnp.mean(xv * xv, axis=-1, keepdims=True) + EPS)
        y_ref[...] = (xv * r * g_ref[...]).astype(BF16)
        r_ref[...] = r

    row = pl.BlockSpec((tr, D), lambda i: (i, 0))
    return pl.pallas_call(
        body, name=name, grid=(S // tr,),
        in_specs=[row, pl.BlockSpec((1, D), lambda i: (0, 0))],
        out_specs=[row, pl.BlockSpec((tr, 1), lambda i: (i, 0))],
        out_shape=[jax.ShapeDtypeStruct((S, D), BF16), jax.ShapeDtypeStruct((S, 1), F32)],
        compiler_params=_params(("parallel",)),
    )(x, g)


def _rms_bwd(dy, x, rstd, g, resid, *, name, tr=256):
    def body(dy_ref, x_ref, r_ref, g_ref, res_ref, dx_ref, dxb_ref, dg_ref):
        r = r_ref[...]
        xh = x_ref[...] * r
        dyv = dy_ref[...]
        t = dyv * g_ref[...]
        dx = r * (t - xh * jnp.mean(t * xh, axis=-1, keepdims=True)) + res_ref[...]
        dx_ref[...] = dx
        dxb_ref[...] = dx.astype(BF16)
        part = jnp.sum(dyv * xh, axis=0, keepdims=True)

        @pl.when(pl.program_id(0) == 0)
        def _():
            dg_ref[...] = part

        @pl.when(pl.program_id(0) > 0)
        def _():
            dg_ref[...] += part

    row = pl.BlockSpec((tr, D), lambda i: (i, 0))
    vec = pl.BlockSpec((1, D), lambda i: (0, 0))
    return pl.pallas_call(
        body, name=name, grid=(S // tr,),
        in_specs=[row, row, pl.BlockSpec((tr, 1), lambda i: (i, 0)), vec, row],
        out_specs=[row, row, vec],
        out_shape=[jax.ShapeDtypeStruct((S, D), F32), jax.ShapeDtypeStruct((S, D), BF16),
                   jax.ShapeDtypeStruct((1, D), F32)],
        compiler_params=_params(("arbitrary",)),
    )(dy, x, rstd, g, resid)


def _rope_tables():
    pos = np.arange(S, dtype=np.float32)
    inv = (ROPE_THETA ** (-np.arange(0, HD, 2, dtype=np.float32) / HD)).astype(np.float32)
    ang = pos[:, None] * inv[None, :]
    cos, sin = np.cos(ang), np.sin(ang)
    return (jnp.asarray(np.concatenate([cos, cos], axis=-1), F32),
            jnp.asarray(np.concatenate([-sin, sin], axis=-1), F32))


def _swap_halves(t):
    return pltpu.roll(t, HD // 2, axis=1)


def _qk_prep(proj, gains, cos2, sin2, *, tr=256):
    def body(q_ref, k_ref, v_ref, g_ref, c_ref, s_ref, qn_ref, kn_ref, vb_ref):
        cos, sin = c_ref[...], s_ref[...]
        for src, dst, row_a, row_b in ((q_ref, qn_ref, 0, 2), (k_ref, kn_ref, 1, 3)):
            for h in range(NH):
                cols = slice(h * HD, (h + 1) * HD)
                t = src[:, cols]
                r = lax.rsqrt(jnp.mean(t * t, axis=-1, keepdims=True) + EPS)
                if h < NH_A:
                    y = t * r * g_ref[row_a:row_a + 1, :]
                    y = y * cos + _swap_halves(y) * sin
                else:
                    y = t * r * g_ref[row_b:row_b + 1, :]
                dst[:, cols] = y.astype(BF16)
        vb_ref[...] = v_ref[...].astype(BF16)

    def blk(c):
        return pl.BlockSpec((tr, QKV), lambda i: (i, c))
    tab = pl.BlockSpec((tr, HD), lambda i: (i, 0))
    out = pl.BlockSpec((tr, QKV), lambda i: (i, 0))
    return pl.pallas_call(
        body, name="qk_prep", grid=(S // tr,),
        in_specs=[blk(0), blk(1), blk(2), pl.BlockSpec((8, HD), lambda i: (0, 0)), tab, tab],
        out_specs=[out, out, out],
        out_shape=[jax.ShapeDtypeStruct((S, QKV), BF16)] * 3,
        compiler_params=_params(("parallel",)),
    )(proj, proj, proj, gains, cos2, sin2)


def _qk_prep_bwd(dproj, proj, gains, cos2, sin2, dq_parts, dk_parts, dv_parts, *, tr=256):
    def body(dp_in, q_ref, k_ref, g_ref, c_ref, s_ref, *rest):
        dqs, dks, dvs = rest[0:4], rest[4:8], rest[8:12]
        dp_out, dg_ref = rest[12:14]
        del dp_in
        cos, sin = c_ref[...], s_ref[...]
        dg_rows = []
        for src, grads, base, row_a, row_b in ((q_ref, dqs, 0, 0, 2), (k_ref, dks, QKV, 1, 3)):
            dg_a = jnp.zeros((1, HD), F32)
            dg_b = jnp.zeros((1, HD), F32)
            for h in range(NH):
                cols = slice(h * HD, (h + 1) * HD)
                t = src[:, cols]
                dy = grads[h // 4][:, (h % 4) * HD:(h % 4 + 1) * HD]
                r = lax.rsqrt(jnp.mean(t * t, axis=-1, keepdims=True) + EPS)
                xh = t * r
                if h < NH_A:
                    dy = dy * cos - _swap_halves(dy) * sin
                    gain = g_ref[row_a:row_a + 1, :]
                    dg_a = dg_a + jnp.sum(dy * xh, axis=0, keepdims=True)
                else:
                    gain = g_ref[row_b:row_b + 1, :]
                    dg_b = dg_b + jnp.sum(dy * xh, axis=0, keepdims=True)
                u = dy * gain
                dx = r * (u - xh * jnp.mean(u * xh, axis=-1, keepdims=True))
                dp_out[:, base + h * HD:base + (h + 1) * HD] = dx.astype(BF16)
            dg_rows += [(row_a, dg_a), (row_b, dg_b)]
        for g4 in range(4):
            dp_out[:, 2 * QKV + g4 * D_BR:2 * QKV + (g4 + 1) * D_BR] = dvs[g4][...].astype(BF16)

        first = pl.program_id(0) == 0

        @pl.when(first)
        def _():
            dg_ref[...] = jnp.zeros((8, HD), F32)

        for row, val in dg_rows:
            dg_ref[row:row + 1, :] += val

    def blk(c):
        return pl.BlockSpec((tr, QKV), lambda i: (i, c))
    tab = pl.BlockSpec((tr, HD), lambda i: (i, 0))
    part = pl.BlockSpec((tr, D_BR), lambda i: (i, 0))
    gain_spec = pl.BlockSpec((8, HD), lambda i: (0, 0))
    return pl.pallas_call(
        body, name="qk_prep_bwd", grid=(S // tr,),
        in_specs=[pl.BlockSpec(memory_space=pl.ANY), blk(0), blk(1), gain_spec, tab, tab] + [part] * 12,
        out_specs=[pl.BlockSpec((tr, 3 * QKV), lambda i: (i, 0)), gain_spec],
        out_shape=[jax.ShapeDtypeStruct((S, D_IN), BF16), jax.ShapeDtypeStruct((8, HD), F32)],
        input_output_aliases={0: 0},
        compiler_params=_params(("arbitrary",)),
    )(dproj, proj, proj, gains, cos2, sin2, *dq_parts, *dk_parts, *dv_parts)


def _mix_fwd(oa, ob, w_pa, w_pb, proj, b_gate, *, tr=256):
    def body(oa_ref, ob_ref, pa_ref, pb_ref, la_ref, lb_ref, ba_ref, bb_ref, mix_ref, ya_ref, yb_ref):
        ya = jnp.concatenate([_dot(oa_ref[...], pa_ref[q], NN) for q in range(N_DEV)], axis=1)
        yb = jnp.concatenate([_dot(ob_ref[...], pb_ref[q], NN) for q in range(N_DEV)], axis=1)
        ga = jax.nn.sigmoid(la_ref[...] + ba_ref[...])
        gb = jax.nn.sigmoid(lb_ref[...] + bb_ref[...])
        mix_ref[...] = (ga * ya + gb * yb).astype(BF16)
        ya_ref[...] = ya.astype(BF16)
        yb_ref[...] = yb.astype(BF16)

    row = pl.BlockSpec((tr, D), lambda i: (i, 0))
    branch = pl.BlockSpec((tr, D_BR), lambda i: (i, 0))
    whole = pl.BlockSpec((N_DEV, D_BR, D // N_DEV), lambda i: (0, 0, 0))
    return pl.pallas_call(
        body, name="mix_fwd", grid=(S // tr,),
        in_specs=[branch, branch, whole, whole,
                  pl.BlockSpec((tr, D), lambda i: (i, 3)), pl.BlockSpec((tr, D), lambda i: (i, 4)),
                  pl.BlockSpec((1, D), lambda i: (0, 0)), pl.BlockSpec((1, D), lambda i: (0, 1))],
        out_specs=[row, row, row], out_shape=[jax.ShapeDtypeStruct((S, D), BF16)] * 3,
        compiler_params=_params(("parallel",)),
    )(oa, ob, w_pa, w_pb, proj, proj, b_gate, b_gate)


def _gate_bwd(branch, dmixed, proj, b_gate, y, dproj, *, tr=256):
    aliased = dproj is not None

    def body(dm_ref, l_ref, b_ref, y_ref, *rest):
        dy_ref, dp_ref, db_ref = rest[-3:]
        g = jax.nn.sigmoid(l_ref[...] + b_ref[...])
        dm = dm_ref[...]
        dy_ref[...] = (dm * g).astype(BF16)
        dl = dm * y_ref[...].astype(F32) * g * (1.0 - g)
        dp_ref[...] = dl.astype(BF16)
        part = jnp.sum(dl, axis=0, keepdims=True)

        @pl.when(pl.program_id(0) == 0)
        def _():
            db_ref[...] = part

        @pl.when(pl.program_id(0) > 0)
        def _():
            db_ref[...] += part

    row = pl.BlockSpec((tr, D), lambda i: (i, 0))
    col = pl.BlockSpec((tr, D), lambda i: (i, 3 + branch))
    vec = pl.BlockSpec((1, D), lambda i: (0, 0))
    return pl.pallas_call(
        body, name=f"gate_bwd_{branch}", grid=(S // tr,),
        in_specs=[row, col, pl.BlockSpec((1, D), lambda i: (0, branch)), row]
        + ([pl.BlockSpec(memory_space=pl.ANY)] if aliased else []),
        out_specs=[row, col, vec],
        out_shape=[jax.ShapeDtypeStruct((S, D), BF16), jax.ShapeDtypeStruct((S, D_IN), BF16),
                   jax.ShapeDtypeStruct((1, D), F32)],
        input_output_aliases={4: 1} if aliased else {},
        compiler_params=_params(("arbitrary",)),
    )(dmixed, proj, b_gate, y, *([dproj] if aliased else []))


def _window_start(t0, wk):
    if wk == S:
        return 0
    return pl.multiple_of(jnp.clip(t0 - (wk - QB) // 2, 0, S - wk), 128)


def _scores_a(q, kw, t0, start, hs, dil, wk):
    s = lax.dot_general(q, kw, NT, preferred_element_type=F32) * SCALE
    qpos = t0 + lax.broadcasted_iota(jnp.int32, (QB, 1), 0)
    kpos = start + lax.broadcasted_iota(jnp.int32, (1, wk), 1)
    diff = kpos - qpos
    keep = (jnp.abs(diff) <= hs) & ((diff & (dil - 1)) == 0)
    return jnp.where(keep, s, NEG)


def _attn_a_fwd(qn, kn, vb, gi):
    hs, dil, wk = GROUPS_A[gi]

    def body(q_ref, k_ref, v_ref, o_ref, lse_ref):
        t0 = pl.program_id(1) * QB
        start = _window_start(t0, wk)
        s = _scores_a(q_ref[...], k_ref[pl.ds(start, wk), :], t0, start, hs, dil, wk)
        m = jnp.max(s, axis=-1, keepdims=True)
        p = jnp.exp(s - m)
        l = jnp.sum(p, axis=-1, keepdims=True)
        o = lax.dot_general(p.astype(BF16), v_ref[pl.ds(start, wk), :], NN, preferred_element_type=F32)
        o_ref[...] = o / l
        lse_ref[...] = m + jnp.log(l)

    full = pl.BlockSpec((S, HD), lambda h, i: (0, 4 * gi + h))
    return pl.pallas_call(
        body, name=f"attn_a_fwd_{gi}", grid=(4, S // QB),
        in_specs=[pl.BlockSpec((QB, HD), lambda h, i: (i, 4 * gi + h)), full, full],
        out_specs=[pl.BlockSpec((QB, HD), lambda h, i: (i, h)),
                   pl.BlockSpec((None, QB, 1), lambda h, i: (h, i, 0))],
        out_shape=[jax.ShapeDtypeStruct((S, D_BR), F32), jax.ShapeDtypeStruct((4, S, 1), F32)],
        compiler_params=_params(("parallel", "parallel")),
    )(qn, kn, vb)


def _combine_a(os, lses, *, tr=256):
    def body(o0, o1, o2, l0, l1, l2, oa_ref, lse_ref):
        for h in range(4):
            cols = slice(h * HD, (h + 1) * HD)
            a, b, c = l0[h], l1[h], l2[h]
            m = jnp.maximum(jnp.maximum(a, b), c)
            wa, wb, wc = jnp.exp(a - m), jnp.exp(b - m), jnp.exp(c - m)
            tot = wa + wb + wc
            oa_ref[:, cols] = ((wa * o0[:, cols] + wb * o1[:, cols] + wc * o2[:, cols]) / tot).astype(BF16)
            lse_ref[h] = m + jnp.log(tot)

    row = pl.BlockSpec((tr, D_BR), lambda i: (i, 0))
    stat = pl.BlockSpec((4, tr, 1), lambda i: (0, i, 0))
    return pl.pallas_call(
        body, name="combine_a", grid=(S // tr,),
        in_specs=[row] * 3 + [stat] * 3, out_specs=[row, stat],
        out_shape=[jax.ShapeDtypeStruct((S, D_BR), BF16), jax.ShapeDtypeStruct((4, S, 1), F32)],
        compiler_params=_params(("parallel",)),
    )(*os, *lses)


def _attn_a_bwd(qn, kn, vb, oa, doa, lse, gi):
    hs, dil, wk = GROUPS_A[gi]

    def body(q_ref, k_ref, v_ref, o_ref, do_ref, lse_ref, dq_ref, dk_ref, dv_ref):
        @pl.when(pl.program_id(1) == 0)
        def _():
            dk_ref[...] = jnp.zeros((S, HD), F32)
            dv_ref[...] = jnp.zeros((S, HD), F32)

        t0 = pl.program_id(1) * QB
        start = _window_start(t0, wk)
        q = q_ref[...]
        kw = k_ref[pl.ds(start, wk), :]
        vw = v_ref[pl.ds(start, wk), :]
        p = jnp.exp(_scores_a(q, kw, t0, start, hs, dil, wk) - lse_ref[...])
        do = do_ref[...]
        dob = do.astype(BF16)
        dsum = jnp.sum(do * o_ref[...].astype(F32), axis=-1, keepdims=True)
        dp = lax.dot_general(dob, vw, NT, preferred_element_type=F32)
        ds = (p * (dp - dsum) * SCALE).astype(BF16)
        dq_ref[...] = lax.dot_general(ds, kw, NN, preferred_element_type=F32)
        dk_ref[pl.ds(start, wk), :] += lax.dot_general(ds, q, TN, preferred_element_type=F32)
        dv_ref[pl.ds(start, wk), :] += lax.dot_general(p.astype(BF16), dob, TN, preferred_element_type=F32)

    full = pl.BlockSpec((S, HD), lambda h, i: (0, 4 * gi + h))
    blk = pl.BlockSpec((QB, HD), lambda h, i: (i, h))
    acc = pl.BlockSpec((S, HD), lambda h, i: (0, h))
    shape = jax.ShapeDtypeStruct((S, D_BR), F32)
    return pl.pallas_call(
        body, name=f"attn_a_bwd_{gi}", grid=(4, S // QB),
        in_specs=[pl.BlockSpec((QB, HD), lambda h, i: (i, 4 * gi + h)), full, full, blk, blk,
                  pl.BlockSpec((None, QB, 1), lambda h, i: (h, i, 0))],
        out_specs=[blk, acc, acc], out_shape=[shape, shape, shape],
        compiler_params=_params(("parallel", "arbitrary")),
    )(qn, kn, vb, oa, doa, lse)


KEYS_B = WIN_R * GRID_W
N_OFF = WIN_R


def _bias_constants():
    q = np.arange(GRID_W)[:, None]
    kc = np.arange(GRID_W)[None, :]
    dc = np.clip(kc - q, -(WIN_C - 1), WIN_C - 1) + (WIN_C - 1)
    expand = np.zeros((HD, GRID_W * GRID_W), np.float32)
    expand[dc.reshape(-1), np.arange(GRID_W * GRID_W)] = 1.0
    cs = np.clip(q - WIN_C // 2, 0, GRID_W - WIN_C)
    keep = ((kc >= cs) & (kc < cs + WIN_C)).reshape(1, -1).astype(np.float32)
    sel = np.zeros((64, 4 * N_OFF * WIN_R), np.float32)
    for h in range(4):
        for off in range(N_OFF):
            for j in range(WIN_R):
                sel[h * (2 * WIN_R - 1) + off + j, (h * N_OFF + off) * WIN_R + j] = 1.0
    return jnp.asarray(expand), jnp.asarray(keep), jnp.asarray(sel)


def _bias_expand(rpb_pad, expand, keep, sel):
    def body(r_ref, e_ref, k_ref, s_ref, o_ref):
        t = lax.dot_general(r_ref[...], e_ref[...], NN, precision=lax.Precision.HIGHEST,
                            preferred_element_type=F32)
        rows = lax.dot_general(s_ref[...], t, TN, precision=lax.Precision.HIGHEST,
                               preferred_element_type=F32)
        o_ref[...] = jnp.where(k_ref[...] > 0.5, rows, NEG)

    return pl.pallas_call(
        body, name="bias_expand",
        out_shape=jax.ShapeDtypeStruct((4 * N_OFF * WIN_R, GRID_W * GRID_W), F32),
        compiler_params=pltpu.CompilerParams(vmem_limit_bytes=VMEM_LIMIT),
    )(rpb_pad, expand, keep, sel)


def _bias_reduce(dbias_rows, expand, sel):
    def body(x_ref, e_ref, s_ref, o_ref):
        z = lax.dot_general(x_ref[...], e_ref[...], NT, precision=lax.Precision.HIGHEST,
                            preferred_element_type=F32)
        o_ref[...] = lax.dot_general(s_ref[...], z, NN, precision=lax.Precision.HIGHEST,
                                     preferred_element_type=F32)

    return pl.pallas_call(
        body, name="bias_reduce", out_shape=jax.ShapeDtypeStruct((64, HD), F32),
        compiler_params=pltpu.CompilerParams(vmem_limit_bytes=VMEM_LIMIT),
    )(dbias_rows, expand, sel)


def _rows_to_tab(rows):
    t = rows.reshape(4, N_OFF, WIN_R, GRID_W, GRID_W)
    return t.transpose(0, 1, 3, 2, 4).reshape(4, N_OFF, GRID_W, KEYS_B)


def _tab_to_rows(tab):
    t = tab.reshape(4, N_OFF, GRID_W, WIN_R, GRID_W)
    return t.transpose(0, 1, 3, 2, 4).reshape(4 * N_OFF * WIN_R, GRID_W * GRID_W)


def _row_window(r):
    r0 = jnp.clip(r - WIN_R // 2, 0, ROWS - WIN_R)
    off = r0 + (WIN_R - 1) - r
    return pl.multiple_of(r * GRID_W, GRID_W), pl.multiple_of(r0 * GRID_W, GRID_W), off


def _attn_b_fwd(qn, kn, vb, bias_tab):
    def body(q_ref, k_ref, v_ref, b_ref, o_ref, lse_ref):
        def row(r, carry):
            qs, ks, off = _row_window(r)
            q = q_ref[pl.ds(qs, GRID_W), :]
            s = lax.dot_general(q, k_ref[pl.ds(ks, KEYS_B), :], NT, preferred_element_type=F32) * SCALE
            s = s + b_ref[off]
            m = jnp.max(s, axis=-1, keepdims=True)
            p = jnp.exp(s - m)
            l = jnp.sum(p, axis=-1, keepdims=True)
            o = lax.dot_general(p.astype(BF16), v_ref[pl.ds(ks, KEYS_B), :], NN, preferred_element_type=F32)
            o_ref[pl.ds(qs, GRID_W), :] = (o / l).astype(BF16)
            lse_ref[pl.ds(qs, GRID_W), :] = m + jnp.log(l)
            return carry

        lax.fori_loop(0, ROWS, row, 0)

    full = pl.BlockSpec((S, HD), lambda h: (0, NH_A + h))
    return pl.pallas_call(
        body, name="attn_b_fwd", grid=(4,),
        in_specs=[full, full, full, pl.BlockSpec((None, N_OFF, GRID_W, KEYS_B), lambda h: (h, 0, 0, 0))],
        out_specs=[pl.BlockSpec((S, HD), lambda h: (0, h)), pl.BlockSpec((None, S, 1), lambda h: (h, 0, 0))],
        out_shape=[jax.ShapeDtypeStruct((S, D_BR), BF16), jax.ShapeDtypeStruct((4, S, 1), F32)],
        compiler_params=_params(("parallel",)),
    )(qn, kn, vb, bias_tab)


def _attn_b_bwd(qn, kn, vb, bias_tab, ob, dob, lse):
    def body(q_ref, k_ref, v_ref, b_ref, o_ref, do_ref, lse_ref, dq_ref, dk_ref, dv_ref, db_ref):
        dk_ref[...] = jnp.zeros((S, HD), F32)
        dv_ref[...] = jnp.zeros((S, HD), F32)
        db_ref[...] = jnp.zeros((N_OFF, GRID_W, KEYS_B), F32)

        def row(r, carry):
            qs, ks, off = _row_window(r)
            rows = pl.ds(qs, GRID_W)
            keys = pl.ds(ks, KEYS_B)
            q = q_ref[rows, :]
            kw = k_ref[keys, :]
            s = lax.dot_general(q, kw, NT, preferred_element_type=F32) * SCALE + b_ref[off]
            p = jnp.exp(s - lse_ref[rows, :])
            do = do_ref[rows, :]
            dobf = do.astype(BF16)
            dsum = jnp.sum(do * o_ref[rows, :].astype(F32), axis=-1, keepdims=True)
            dp = lax.dot_general(dobf, v_ref[keys, :], NT, preferred_element_type=F32)
            ds = p * (dp - dsum)
            db_ref[off] += ds
            dsb = (ds * SCALE).astype(BF16)
            dq_ref[rows, :] = lax.dot_general(dsb, kw, NN, preferred_element_type=F32)
            dk_ref[keys, :] += lax.dot_general(dsb, q, TN, preferred_element_type=F32)
            dv_ref[keys, :] += lax.dot_general(p.astype(BF16), dobf, TN, preferred_element_type=F32)
            return carry

        lax.fori_loop(0, ROWS, row, 0)

    full = pl.BlockSpec((S, HD), lambda h: (0, NH_A + h))
    slot = pl.BlockSpec((S, HD), lambda h: (0, h))
    tab = pl.BlockSpec((None, N_OFF, GRID_W, KEYS_B), lambda h: (h, 0, 0, 0))
    shape = jax.ShapeDtypeStruct((S, D_BR), F32)
    return pl.pallas_call(
        body, name="attn_b_bwd", grid=(4,),
        in_specs=[full, full, full, tab, slot, slot, pl.BlockSpec((None, S, 1), lambda h: (h, 0, 0))],
        out_specs=[slot, slot, slot, tab],
        out_shape=[shape, shape, shape, jax.ShapeDtypeStruct((4, N_OFF, GRID_W, KEYS_B), F32)],
        compiler_params=_params(("parallel",)),
    )(qn, kn, vb, bias_tab, ob, dob, lse)


def _epi_relu_sq(acc, ex, outs):
    u = jnp.maximum(acc, 0.0)
    outs[0][...] = u.astype(BF16)
    outs[1][...] = (u * u).astype(BF16)


def _epi_relu_sq_bwd(acc, ex, outs):
    outs[0][...] = (acc * (2.0 * ex[0][...].astype(F32))).astype(BF16)


def _epi_loss_head(acc, ex, outs):
    e = acc + ex[0][...] - ex[1][...]
    dy = e * (1.0 / D)
    outs[0][...] = dy
    outs[1][...] = dy.astype(BF16)
    part = (0.5 / D) * jnp.sum(jnp.sum(e * e, axis=-1, keepdims=True), axis=0, keepdims=True)
    first = (pl.program_id(0) == 0) & (pl.program_id(1) == 0)

    @pl.when(first)
    def _():
        outs[2][...] = part

    @pl.when(jnp.logical_not(first))
    def _():
        outs[2][...] += part


def _local_step(x, target, norm_mix, b_gate, gains, rpb_pad, norm_ffn,
                w_in, w_pa, w_pb, w_out, w_up, w_down, on_grads):
    cos2, sin2 = _rope_tables()
    expand, keep, sel = _bias_constants()
    w_out3, w_down3 = w_out[None], w_down[None]

    xn, rstd1 = _rms_fwd(x, norm_mix, name="rms_mix")
    proj = _mm_nn(xn, w_in, tm=1024, tn=1280, name="proj")
    qn, kn, vb = _qk_prep(proj, gains, cos2, sin2)
    fwd_a = [_attn_a_fwd(qn, kn, vb, gi) for gi in range(3)]
    oa, lse_a = _combine_a([o for o, _ in fwd_a], [l for _, l in fwd_a])
    bias_tab = _rows_to_tab(_bias_expand(rpb_pad, expand, keep, sel))
    ob, lse_b = _attn_b_fwd(qn, kn, vb, bias_tab)
    mixed, ya, yb = _mix_fwd(oa, ob, w_pa, w_pb, proj, b_gate)
    h1 = _mm_nn(mixed, w_out3, tm=1024, tn=1024, name="out_proj", epi=_epi_residual, extra=(x,))
    hn, rstd2 = _rms_fwd(h1, norm_ffn, name="rms_ffn")
    u, usq = _mm_nn(hn, w_up, tm=1024, tn=1024, name="ffn_up", epi=_epi_relu_sq,
                    out_dtypes=(BF16, BF16))
    dy, dyb, loss = _mm_nn(usq, w_down3, tm=512, tn=512, name="ffn_down", epi=_epi_loss_head,
                           extra=(h1, target), out_dtypes=(F32, BF16), total=True)

    g_down = _mm_tn(usq, dyb, tm=1024, tn=1024, name="grad_w_down")
    sent = on_grads("w_down", {5: g_down.reshape(N_DEV, D_FF // N_DEV, D)})
    du = _mm_nt(dyb, w_down3, tm=1024, tn=1024, name="ffn_down_bwd", out_dtype=BF16,
                epi=_epi_relu_sq_bwd, extra=(u,), after=sent)
    g_up = _mm_tn(hn, du, tm=1024, tn=1024, groups=N_DEV, name="grad_w_up")
    sent = on_grads("w_up", {4: g_up})
    dhn = _mm_nt(du, w_up, tm=512, tn=512, name="ffn_up_bwd", after=sent)
    dh1, dh1b, g_norm_ffn = _rms_bwd(dhn, h1, rstd2, norm_ffn, dy, name="rms_ffn_bwd")

    g_out = _mm_tn(mixed, dh1b, tm=1024, tn=1024, name="grad_w_out")
    dmixed = _mm_nt(dh1b, w_out3, tm=1024, tn=1024, name="out_proj_bwd")
    dya, dproj, g_ba = _gate_bwd(0, dmixed, proj, b_gate, ya, None)
    dyb2, dproj, g_bb = _gate_bwd(1, dmixed, proj, b_gate, yb, dproj)
    g_pa = _mm_tn(oa, dya, tm=D_BR, tn=1024, groups=N_DEV, name="grad_w_proj_a")
    g_pb = _mm_tn(ob, dyb2, tm=D_BR, tn=1024, groups=N_DEV, name="grad_w_proj_b")
    sent = on_grads("w_mix", {3: g_out.reshape(N_DEV, D // N_DEV, D), 1: g_pa, 2: g_pb})
    doa = _mm_nt(dya, w_pa, tm=1024, tn=D_BR, name="proj_a_bwd", after=sent)
    dob = _mm_nt(dyb2, w_pb, tm=1024, tn=D_BR, name="proj_b_bwd")
    bwd = [_attn_a_bwd(qn, kn, vb, oa, doa, lse_a, gi) for gi in range(3)]
    dqb, dkb, dvb, dbias = _attn_b_bwd(qn, kn, vb, bias_tab, ob, dob, lse_b)
    g_rpb = _bias_reduce(_tab_to_rows(dbias), expand, sel)
    dproj, g_gains = _qk_prep_bwd(dproj, proj, gains, cos2, sin2,
                                  [b[0] for b in bwd] + [dqb], [b[1] for b in bwd] + [dkb],
                                  [b[2] for b in bwd] + [dvb])
    g_in = _mm_tn(xn, dproj, tm=1024, tn=1280, groups=N_DEV, name="grad_w_in")
    sent = on_grads("w_in", {0: g_in})
    dxn = _mm_nt(dproj, w_in, tm=256, tn=512, name="proj_bwd", after=sent)
    grad_x, _, g_norm_mix = _rms_bwd(dxn, x, rstd1, norm_mix, dh1, name="rms_mix_bwd")

    small = (g_norm_mix, g_ba, g_bb, g_gains, g_rpb, g_norm_ffn)
    return loss, grad_x, small


def _cast_bf16(w, *, tr=256):
    rows, cols = w.shape
    tr = min(tr, rows)

    def body(w_ref, o_ref):
        o_ref[...] = w_ref[...].astype(BF16)

    spec = pl.BlockSpec((tr, cols), lambda i: (i, 0))
    return pl.pallas_call(
        body, name=f"cast_{rows}x{cols}", grid=(rows // tr,), in_specs=[spec], out_specs=spec,
        out_shape=jax.ShapeDtypeStruct((rows, cols), BF16), compiler_params=_params(("parallel",)),
    )(w)


def _me_and_peers():
    x, y, c = lax.axis_index("x"), lax.axis_index("y"), lax.axis_index("c")
    me = 4 * x + 2 * y + c
    peers = []
    for k in range(1, N_DEV):
        px = 1 - x if k & 4 else x
        py = 1 - y if k & 2 else y
        pc = 1 - c if k & 1 else c
        peers.append(((px, py, pc), 4 * px + 2 * py + pc))
    return me, peers


def _gather_on_sequencer(shards, name):
    n = len(shards)
    hbm = pltpu.MemorySpace.HBM
    ins = [jax.new_ref(s, memory_space=hbm) for s in shards]
    outs = [jax.empty_ref(jax.ShapeDtypeStruct((N_DEV,) + s.shape, s.dtype), memory_space=hbm) for s in shards]

    @pl.kernel(mesh=plsc.ScalarSubcoreMesh(axis_name="seq", num_cores=1), name=name,
               scratch_types=(pltpu.SemaphoreType.DMA((n, N_DEV - 1)), pltpu.SemaphoreType.DMA((n, N_DEV - 1)),
                              pltpu.SemaphoreType.DMA((n,))),
               compiler_params=pltpu.CompilerParams(collective_id=0))
    def launch(send, recv, lsem):
        x, y, c = lax.axis_index("x"), lax.axis_index("y"), lax.axis_index("c")
        me, sibling = (x, y, c), (x, y, 1 - c)
        chips = [(1 - x, y), (x, 1 - y), (1 - x, 1 - y)]
        barrier = pltpu.get_barrier_semaphore()
        for peer in [sibling] + [(*chip, c) for chip in chips]:
            pl.semaphore_signal(barrier, inc=1, device_id=peer, device_id_type=MESH)
        pl.semaphore_wait(barrier, 4)

        def copy(w, k, block, to, src=None):
            px, py, pc = block
            dst = outs[w].at[4 * px + 2 * py + pc]
            return pltpu.make_async_remote_copy(dst if src is None else src, dst, send.at[w, k], recv.at[w, k],
                                                device_id=to, device_id_type=MESH)

        local = [pltpu.make_async_copy(ins[w], outs[w].at[4 * x + 2 * y + c], lsem.at[w]) for w in range(n)]
        for cp in local:
            cp.start()
        first = []
        for w in range(n):
            first += [copy(w, 1 + j, me, (*chip, c), src=ins[w]) for j, chip in enumerate(chips)]
            first.append(copy(w, 0, me, sibling, src=ins[w]))
        for cp in first:
            cp.start()
        passed = []
        for w in range(n):
            for j, chip in enumerate(chips):
                copy(w, 1 + j, (*chip, c), me).wait_recv()
                cp = copy(w, 4 + j, (*chip, c), sibling)
                cp.start()
                passed.append(cp)
        for w in range(n):
            copy(w, 0, sibling, me).wait_recv()
            for j, chip in enumerate(chips):
                copy(w, 4 + j, (*chip, 1 - c), me).wait_recv()
        for cp in first + passed:
            cp.wait_send()
        for cp in local:
            cp.wait()

    launch()
    return [o[...] for o in outs]


N_CHIP = 4
CHIPS = ((0, 0), (0, 1), (1, 0), (1, 1))


def _sequencer(name, n_sems, collective_id):
    return functools.partial(
        pl.kernel, mesh=plsc.ScalarSubcoreMesh(axis_name="seq", num_cores=1), name=name,
        scratch_types=tuple(pltpu.SemaphoreType.DMA(s) for s in n_sems),
        compiler_params=pltpu.CompilerParams(collective_id=collective_id))


def _handshake(peers):
    barrier = pltpu.get_barrier_semaphore()
    for peer in peers:
        pl.semaphore_signal(barrier, inc=1, device_id=peer, device_id_type=MESH)
    pl.semaphore_wait(barrier, len(peers))


def _chip_exchange_on_sequencer(parts, name):
    n = len(parts)
    hbm = pltpu.MemorySpace.HBM
    ins = [jax.new_ref(p, memory_space=hbm) for p in parts]
    outs = [jax.empty_ref(jax.ShapeDtypeStruct(p.shape, p.dtype), memory_space=hbm) for p in parts]

    @_sequencer(name, ((n, 3), (n, 3), (n,)), 2)
    def launch(send, recv, lsem):
        x, y, c = lax.axis_index("x"), lax.axis_index("y"), lax.axis_index("c")
        mine = 2 * x + y
        chips = [(1 - x, y), (x, 1 - y), (1 - x, 1 - y)]
        _handshake([(*chip, c) for chip in chips])
        local = [pltpu.make_async_copy(ins[w].at[mine], outs[w].at[mine], lsem.at[w]) for w in range(n)]
        for cp in local:
            cp.start()
        sends = []
        for w in range(n):
            for j, (px, py) in enumerate(chips):
                cp = pltpu.make_async_remote_copy(ins[w].at[2 * px + py], outs[w].at[mine],
                                                  send.at[w, j], recv.at[w, j],
                                                  device_id=(px, py, c), device_id_type=MESH)
                cp.start()
                sends.append(cp)
        for w in range(n):
            for j, (px, py) in enumerate(chips):
                pltpu.make_async_remote_copy(ins[w].at[mine], outs[w].at[2 * px + py],
                                             send.at[w, j], recv.at[w, j],
                                             device_id=(px, py, c), device_id_type=MESH).wait_recv()
        for cp in sends:
            cp.wait_send()
        for cp in local:
            cp.wait()

    launch()
    return [o[...] for o in outs]


PAIR_CHUNK_BYTES = 3 * 1024 * 1024


def _pair_sum(grads, name, after=()):
    n = len(grads)
    n_after = len(after)
    shapes = [g.shape[1:] for g in grads]
    splits = []
    for rows, cols in shapes:
        ns = 1
        while rows * cols * 2 // ns > PAIR_CHUNK_BYTES and rows // (2 * ns) >= 16:
            ns *= 2
        splits.append(ns)

    def body(*refs):
        ins, refs = refs[:n], refs[n + n_after:]
        outs = refs[:n]
        lands = refs[n:2 * n]
        mines = refs[2 * n:3 * n]
        stages = refs[3 * n:4 * n]
        rsend, rrecv, lsem, osem = refs[4 * n:]
        x, y, c = lax.axis_index("x"), lax.axis_index("y"), lax.axis_index("c")
        remote = {}
        for w in range(n):
            for ch, (px, py) in enumerate(CHIPS):
                cp = pltpu.make_async_remote_copy(ins[w].at[4 * px + 2 * py + 1 - c], lands[w].at[ch],
                                                  rsend.at[w, ch], rrecv.at[w, ch],
                                                  device_id=(x, y, 1 - c), device_id_type=MESH)
                cp.start()
                remote[w, ch] = cp
        for w in range(n):
            rc = shapes[w][0] // splits[w]
            pieces = [(ch, p) for ch in range(N_CHIP) for p in range(splits[w])]

            def load(i, w=w, rc=rc, pieces=pieces):
                ch, p = pieces[i]
                px, py = CHIPS[ch]
                return pltpu.make_async_copy(ins[w].at[4 * px + 2 * py + c, pl.ds(p * rc, rc)],
                                             mines[w].at[i % 2], lsem.at[w, i % 2])

            def store(i, w=w, rc=rc, pieces=pieces):
                ch, p = pieces[i]
                return pltpu.make_async_copy(stages[w].at[i % 2], outs[w].at[ch, pl.ds(p * rc, rc)],
                                             osem.at[w, i % 2])

            load(0).start()
            for i, (ch, p) in enumerate(pieces):
                if i + 1 < len(pieces):
                    load(i + 1).start()
                load(i).wait()
                if p == 0:
                    remote[w, ch].wait_recv()
                if i >= 2:
                    store(i - 2).wait()
                theirs = lands[w][ch, p * rc:(p + 1) * rc, :]
                stages[w][i % 2] = (mines[w][i % 2].astype(F32) + theirs.astype(F32)).astype(BF16)
                store(i).start()
            for i in range(max(0, len(pieces) - 2), len(pieces)):
                store(i).wait()
        for cp in remote.values():
            cp.wait_send()

    hbm = pl.BlockSpec(memory_space=pl.ANY)
    scratch = [pltpu.VMEM((N_CHIP,) + sh, BF16) for sh in shapes]
    scratch += [pltpu.VMEM((2, sh[0] // ns, sh[1]), BF16) for sh, ns in zip(shapes, splits)] * 2
    scratch += [pltpu.SemaphoreType.DMA((n, N_CHIP)), pltpu.SemaphoreType.DMA((n, N_CHIP)),
                pltpu.SemaphoreType.DMA((n, 2)), pltpu.SemaphoreType.DMA((n, 2))]
    return pl.pallas_call(
        body, name=name, in_specs=[hbm] * (n + n_after), out_specs=[hbm] * n,
        out_shape=[jax.ShapeDtypeStruct((N_CHIP,) + sh, BF16) for sh in shapes],
        scratch_shapes=scratch,
        compiler_params=pltpu.CompilerParams(vmem_limit_bytes=VMEM_LIMIT),
    )(*grads, *after)


def _adamw_math(g, w, m, v):
    m2 = B1 * m + (1.0 - B1) * g
    v2 = B2 * v + (1.0 - B2) * (g * g)
    delta = -LR * ((m2 / BC1) / (jnp.sqrt(v2 / BC2) + AEPS) + WD * w)
    return delta, m2, v2


def _adamw(parts, w, m, v, *, name, after=(), tr=256):
    rows, cols = w.shape

    def body(p_ref, w_ref, m_ref, v_ref, *rest):
        g_ref, d_ref, mo_ref, vo_ref = rest[len(after):]
        g = p_ref[0].astype(F32)
        for b in range(1, N_CHIP):
            g = g + p_ref[b].astype(F32)
        delta, m2, v2 = _adamw_math(g, w_ref[...], m_ref[...], v_ref[...])
        g_ref[...] = g
        d_ref[...] = delta
        mo_ref[...] = m2
        vo_ref[...] = v2

    spec = pl.BlockSpec((tr, cols), lambda i: (i, 0))
    shape = jax.ShapeDtypeStruct((rows, cols), F32)
    return pl.pallas_call(
        body, name=name, grid=(rows // tr,),
        in_specs=[pl.BlockSpec((N_CHIP, tr, cols), lambda i: (0, i, 0)), spec, spec, spec]
        + [pl.BlockSpec(memory_space=pl.ANY)] * len(after),
        out_specs=[spec] * 4, out_shape=[shape] * 4,
        compiler_params=_params(("parallel",)),
    )(parts, w, m, v, *after)


def _small_update(part, w, m, v):
    rows = part.shape[0]

    def body(p_ref, w_ref, m_ref, v_ref, g_ref, d_ref, mo_ref, vo_ref, buf, send, recv):
        me, peers = _me_and_peers()
        buf[me] = p_ref[...]
        sends = []
        for k, (dev, _) in enumerate(peers):
            cp = pltpu.make_async_remote_copy(p_ref, buf.at[me], send.at[k], recv.at[k],
                                              device_id=dev, device_id_type=MESH)
            cp.start()
            sends.append(cp)
        for k, (dev, idx) in enumerate(peers):
            pltpu.make_async_remote_copy(p_ref, buf.at[idx], send.at[k], recv.at[k],
                                         device_id=dev, device_id_type=MESH).wait_recv()
        for cp in sends:
            cp.wait_send()
        g = buf[0]
        for b in range(1, N_DEV):
            g = g + buf[b]
        delta, m2, v2 = _adamw_math(g, w_ref[...], m_ref[...], v_ref[...])
        g_ref[...] = g
        d_ref[...] = delta
        mo_ref[...] = m2
        vo_ref[...] = v2

    vm = pl.BlockSpec(memory_space=pltpu.VMEM)
    shape = jax.ShapeDtypeStruct((rows, HD), F32)
    return pl.pallas_call(
        body, name="small_params_update",
        in_specs=[vm] * 4, out_specs=[vm] * 4, out_shape=[shape] * 4,
        scratch_shapes=[pltpu.VMEM((N_DEV, rows, HD), F32),
                        pltpu.SemaphoreType.DMA((N_DEV - 1,)), pltpu.SemaphoreType.DMA((N_DEV - 1,))],
    )(part, w, m, v)


def _pack_small(norm_mix, b_gate, qa, ka, qb, kb, rpb, norm_ffn):
    gains = jnp.concatenate([qa, ka, qb, kb, jnp.zeros((4, HD), F32)], axis=0)
    rpb_pad = jnp.pad(rpb.reshape(4 * (2 * WIN_R - 1), 2 * WIN_C - 1), ((0, 4), (0, HD - (2 * WIN_C - 1))))
    return jnp.concatenate([norm_mix.reshape(16, HD), b_gate.reshape(32, HD), gains, rpb_pad,
                            norm_ffn.reshape(16, HD), jnp.zeros((8, HD), F32)], axis=0)


LOSS_ROW = 136


def _unpack_small(p):
    norm_mix = p[0:16].reshape(1, D)
    b_gate = p[16:48].reshape(1, 2 * D)
    qa, ka, qb, kb = (p[48 + i:49 + i] for i in range(4))
    rpb = p[56:116, :2 * WIN_C - 1].reshape(1, 4, 2 * WIN_R - 1, 2 * WIN_C - 1)
    norm_ffn = p[120:136].reshape(1, D)
    return norm_mix, b_gate, qa, ka, qb, kb, rpb, norm_ffn


def kernel(x, norm_mix, w_in, b_gate, q_norm_a, k_norm_a, q_norm_b, k_norm_b, rpb_b, w_proj_a, w_proj_b, w_out, norm_ffn, w_up, w_down, loss_target, m_norm_mix, m_w_in, m_b_gate, m_q_norm_a, m_k_norm_a, m_q_norm_b, m_k_norm_b, m_rpb_b, m_w_proj_a, m_w_proj_b, m_w_out, m_norm_ffn, m_w_up, m_w_down, v_norm_mix, v_w_in, v_b_gate, v_q_norm_a, v_k_norm_a, v_q_norm_b, v_k_norm_b, v_rpb_b, v_w_proj_a, v_w_proj_b, v_w_out, v_norm_ffn, v_w_up, v_w_down):
    big_w = (w_in[0], w_proj_a[0], w_proj_b[0], w_out[0], w_up[0], w_down[0])
    big_m = (m_w_in[0], m_w_proj_a[0], m_w_proj_b[0], m_w_out[0], m_w_up[0], m_w_down[0])
    big_v = (v_w_in[0], v_w_proj_a[0], v_w_proj_b[0], v_w_out[0], v_w_up[0], v_w_down[0])
    names = ("w_in", "w_proj_a", "w_proj_b", "w_out", "w_up", "w_down")

    shards = [_cast_bf16(w) for w in big_w]
    g_in, = _gather_on_sequencer(shards[0:1], "gather_w_in")
    g_pa, g_pb, g_out = _gather_on_sequencer(shards[1:4], "gather_w_mix")
    g_up, = _gather_on_sequencer(shards[4:5], "gather_w_up")
    g_down, = _gather_on_sequencer(shards[5:6], "gather_w_down")
    small_w = _pack_small(norm_mix, b_gate, q_norm_a, k_norm_a, q_norm_b, k_norm_b, rpb_b, norm_ffn)
    small_m = _pack_small(m_norm_mix, m_b_gate, m_q_norm_a, m_k_norm_a, m_q_norm_b, m_k_norm_b, m_rpb_b, m_norm_ffn)
    small_v = _pack_small(v_norm_mix, v_b_gate, v_q_norm_a, v_k_norm_a, v_q_norm_b, v_k_norm_b, v_rpb_b, v_norm_ffn)

    upd = [None] * 6
    in_flight = {}

    def finish(after):
        done = []
        for i, r in in_flight.items():
            upd[i] = _adamw(r, big_w[i], big_m[i], big_v[i], name=f"adamw_{names[i]}", after=after)
            done.append(upd[i][0])
        in_flight.clear()
        return done

    def on_grads(tag, grads):
        new = list(grads.values())
        sums = _pair_sum(new, f"pair_sum_{tag}", after=finish(new))
        in_flight.update(zip(grads, _chip_exchange_on_sequencer(sums, f"chip_exchange_{tag}")))
        return sums

    loss, grad_x, small_g = _local_step(
        x[0], loss_target[0], norm_mix, b_gate, small_w[48:56], small_w[56:120], norm_ffn,
        g_in, g_pa, g_pb, g_out.reshape(D, D), g_up, g_down.reshape(D_FF, D), on_grads)

    g_norm_mix, g_ba, g_bb, g_gains, g_rpb, g_norm_ffn = small_g
    small_part = jnp.concatenate([g_norm_mix.reshape(16, HD), g_ba.reshape(16, HD), g_bb.reshape(16, HD),
                                  g_gains, g_rpb, g_norm_ffn.reshape(16, HD),
                                  jnp.pad(loss, ((0, 7), (0, HD - 1)))], axis=0)
    slabs = _small_update(small_part, small_w, small_m, small_v)
    total = slabs[0][LOSS_ROW, 0]
    s_g, s_d, s_m, s_v = (_unpack_small(t) for t in slabs)

    finish([grad_x])
    b_g, b_d, b_m, b_v = ([u[j][None] for u in upd] for j in range(4))

    def order(small, big):
        nm, bg, qa, ka, qb, kb, rpb, nf = small
        w_in_, pa_, pb_, out_, up_, down_ = big
        return (nm, w_in_, bg, qa, ka, qb, kb, rpb, pa_, pb_, out_, nf, up_, down_)

    return (total, grad_x[None], *order(s_g, b_g), *order(s_d, b_d), *order(s_m, b_m), *order(s_v, b_v))
```

```python
import functools

import jax
import jax.numpy as jnp
import numpy as np
from jax import lax
from jax.experimental import pallas as pl
from jax.experimental.pallas import tpu as pltpu
from jax.experimental.pallas import tpu_sc as plsc

F32 = jnp.float32
BF16 = jnp.bfloat16

N_DEV = 8
S = 2048
D = 2048
HD = 128
NH = 16
NH_A = 12
QKV = NH * HD
D_IN = 3 * QKV + 2 * D
D_BR = 512
D_FF = 4 * D
GRID_W = 64
ROWS = S // GRID_W
WIN_R = 8
WIN_C = 16
EPS = 1e-6
NEG = -1e30
SCALE = HD ** -0.5
ROPE_THETA = 10000.0
DILATIONS = (1, 4, 16)
HALF_A = 64
QB = 128

LR, B1, B2, AEPS, WD, STEP = 0.001, 0.9, 0.999, 1e-08, 0.01, 10
BC1 = 1.0 - B1 ** STEP
BC2 = 1.0 - B2 ** STEP

VMEM_LIMIT = 56 * 1024 * 1024
MESH = pl.DeviceIdType.MESH

NN = (((1,), (0,)), ((), ()))
NT = (((1,), (1,)), ((), ()))
TN = (((0,), (0,)), ((), ()))


def _params(sem):
    return pltpu.CompilerParams(dimension_semantics=sem, vmem_limit_bytes=VMEM_LIMIT)


def _matmul(a, b, *, product, grid, a_spec, b_spec, epi, out_shape, out_specs, name,
            extra=(), extra_specs=(), after=(), carried=False):
    n_extra = len(extra)

    def body(a_ref, b_ref, *rest):
        epi(product(a_ref, b_ref), rest[:n_extra], rest[n_extra + len(after):])

    return pl.pallas_call(
        body, name=name, grid=grid,
        in_specs=[a_spec, b_spec, *extra_specs, *[pl.BlockSpec(memory_space=pl.ANY)] * len(after)],
        out_specs=out_specs, out_shape=out_shape,
        compiler_params=_params(("arbitrary", "arbitrary") if carried else ("parallel", "parallel")),
    )(a, b, *extra, *after)


def _dot(x, y, dims):
    return lax.dot_general(x, y, dims, preferred_element_type=F32)


def _epi_store(acc, ex, outs):
    outs[0][...] = acc.astype(outs[0].dtype)


def _epi_residual(acc, ex, outs):
    outs[0][...] = acc + ex[0][...]


def _mm_nn(a, b3, *, tm, tn, name, out_dtypes=(F32,), epi=_epi_store, extra=(), total=False):
    m, kdim = a.shape
    g, _, ng = b3.shape
    n = g * ng
    if tn <= ng:
        npg = ng // tn
        b_spec = pl.BlockSpec((None, kdim, tn), lambda j, i: (j // npg, 0, j % npg))

        def product(a_ref, b_ref):
            return _dot(a_ref[...], b_ref[...], NN)
    else:
        gb = tn // ng
        b_spec = pl.BlockSpec((gb, kdim, ng), lambda j, i: (j, 0, 0))

        def product(a_ref, b_ref):
            return jnp.concatenate([_dot(a_ref[...], b_ref[q], NN) for q in range(gb)], axis=1)

    tile = pl.BlockSpec((tm, tn), lambda j, i: (i, j))
    shapes = [jax.ShapeDtypeStruct((m, n), dt) for dt in out_dtypes]
    specs = [tile] * len(shapes)
    if total:
        shapes.append(jax.ShapeDtypeStruct((1, 1), F32))
        specs.append(pl.BlockSpec((1, 1), lambda j, i: (0, 0)))
    single = len(shapes) == 1
    return _matmul(
        a, b3, product=product, grid=(n // tn, m // tm), epi=epi, name=name, carried=total,
        a_spec=pl.BlockSpec((tm, kdim), lambda j, i: (i, 0)), b_spec=b_spec,
        extra=extra, extra_specs=[tile] * len(extra),
        out_shape=shapes[0] if single else shapes, out_specs=specs[0] if single else specs)


def _mm_nt(a, b3, *, tm, tn, name, out_dtype=F32, epi=_epi_store, extra=(), after=()):
    m, kdim = a.shape
    g, n, kg = b3.shape

    def product(a_ref, b_ref):
        acc = _dot(a_ref[:, 0:kg], b_ref[0], NT)
        for q in range(1, g):
            acc = acc + _dot(a_ref[:, q * kg:(q + 1) * kg], b_ref[q], NT)
        return acc

    tile = pl.BlockSpec((tm, tn), lambda j, i: (i, j))
    return _matmul(
        a, b3, product=product, grid=(n // tn, m // tm), epi=epi, name=name,
        a_spec=pl.BlockSpec((tm, kdim), lambda j, i: (i, 0)),
        b_spec=pl.BlockSpec((g, tn, kg), lambda j, i: (0, j, 0)),
        extra=extra, extra_specs=[tile] * len(extra), after=after,
        out_shape=jax.ShapeDtypeStruct((m, n), out_dtype), out_specs=tile)


def _mm_tn(a, b, *, tm, tn, name, groups=1, out_dtype=BF16):
    t, m = a.shape
    _, n = b.shape
    ng = n // groups
    if tn <= ng:
        npg = ng // tn
        out_spec = pl.BlockSpec((None, tm, tn), lambda j, i: (j // npg, i, j % npg))
        epi = _epi_store

        def product(a_ref, b_ref):
            return _dot(a_ref[...], b_ref[...], TN)
    else:
        gb = tn // ng
        out_spec = pl.BlockSpec((gb, tm, ng), lambda j, i: (j, i, 0))

        def product(a_ref, b_ref):
            return [_dot(a_ref[...], b_ref[:, q * ng:(q + 1) * ng], TN) for q in range(gb)]

        def epi(parts, ex, outs):
            for q, part in enumerate(parts):
                outs[0][q] = part.astype(out_dtype)

    return _matmul(
        a, b, product=product, grid=(n // tn, m // tm), epi=epi, name=name,
        a_spec=pl.BlockSpec((t, tm), lambda j, i: (0, i)),
        b_spec=pl.BlockSpec((t, tn), lambda j, i: (0, j)),
        out_shape=jax.ShapeDtypeStruct((groups, m, ng), out_dtype), out_specs=out_spec)


def _rms_fwd(x, g, *, name, tr=256):
    def body(x_ref, g_ref, y_ref, r_ref):
        xv = x_ref[...]
        r = lax.rsqrt(jnp.mean(xv * xv, axis=-1, keepdims=True) + EPS)
        y_ref[...] = (xv * r * g_ref[...]).astype(BF16)
        r_ref[...] = r

    row = pl.BlockSpec((tr, D), lambda i: (i, 0))
    return pl.pallas_call(
        body, name=name, grid=(S // tr,),
        in_specs=[row, pl.BlockSpec((1, D), lambda i: (0, 0))],
        out_specs=[row, pl.BlockSpec((tr, 1), lambda i: (i, 0))],
        out_shape=[jax.ShapeDtypeStruct((S, D), BF16), jax.ShapeDtypeStruct((S, 1), F32)],
        compiler_params=_params(("parallel",)),
    )(x, g)


def _rms_bwd(dy, x, rstd, g, resid, *, name, tr=256):
    def body(dy_ref, x_ref, r_ref, g_ref, res_ref, dx_ref, dxb_ref, dg_ref):
        r = r_ref[...]
        xh = x_ref[...] * r
        dyv = dy_ref[...]
        t = dyv * g_ref[...]
        dx = r * (t - xh * jnp.mean(t * xh, axis=-1, keepdims=True)) + res_ref[...]
        dx_ref[...] = dx
        dxb_ref[...] = dx.astype(BF16)
        part = jnp.sum(dyv * xh, axis=0, keepdims=True)

        @pl.when(pl.program_id(0) == 0)
        def _():
            dg_ref[...] = part

        @pl.when(pl.program_id(0) > 0)
        def _():
            dg_ref[...] += part

    row = pl.BlockSpec((tr, D), lambda i: (i, 0))
    vec = pl.BlockSpec((1, D), lambda i: (0, 0))
    return pl.pallas_call(
        body, name=name, grid=(S // tr,),
        in_specs=[row, row, pl.BlockSpec((tr, 1), lambda i: (i, 0)), vec, row],
        out_specs=[row, row, vec],
        out_shape=[jax.ShapeDtypeStruct((S, D), F32), jax.ShapeDtypeStruct((S, D), BF16),
                   jax.ShapeDtypeStruct((1, D), F32)],
        compiler_params=_params(("arbitrary",)),
    )(dy, x, rstd, g, resid)


def _rope_tables():
    pos = np.arange(S, dtype=np.float32)
    inv = (ROPE_THETA ** (-np.arange(0, HD, 2, dtype=np.float32) / HD)).astype(np.float32)
    ang = pos[:, None] * inv[None, :]
    cos, sin = np.cos(ang), np.sin(ang)
    return (jnp.asarray(np.concatenate([cos, cos], axis=-1), F32),
            jnp.asarray(np.concatenate([-sin, sin], axis=-1), F32))


def _swap_halves(t):
    return pltpu.roll(t, HD // 2, axis=1)


TOK = 256


def _lane_block_spec(d, last=HD):
    return pl.BlockSpec((4, TOK // d, d * last), lambda i: (0, i, 0))


def _to_lane_blocks(dst, head, val, d, scr, dtype):
    w = val.shape[1]
    if d == 1:
        dst[head] = val.astype(dtype)
        return
    scr[...] = val
    for r in range(d):
        dst[head, :, r * w:(r + 1) * w] = scr[pl.ds(r, TOK // d, stride=d), :].astype(dtype)


def _from_lane_blocks(src, head, d, w, scr):
    if d == 1:
        return src[head].astype(F32)
    for r in range(d):
        scr[pl.ds(r, TOK // d, stride=d), :] = src[head, :, r * w:(r + 1) * w].astype(F32)
    return scr[...]


def _qk_prep(proj, gains, cos2, sin2):
    def body(q_ref, k_ref, v_ref, g_ref, c_ref, s_ref, *rest):
        outs, scr = rest[:-1], rest[-1]
        cos, sin = c_ref[...], s_ref[...]
        for which, (src, row_a, row_b) in enumerate(((q_ref, 0, 2), (k_ref, 1, 3), (v_ref, None, None))):
            for h in range(NH):
                y = src[:, h * HD:(h + 1) * HD]
                if row_a is not None:
                    y = y * lax.rsqrt(jnp.mean(y * y, axis=-1, keepdims=True) + EPS)
                    if h < NH_A:
                        y = y * g_ref[row_a:row_a + 1, :]
                        y = y * cos + _swap_halves(y) * sin
                    else:
                        y = y * g_ref[row_b:row_b + 1, :]
                if h < NH_A:
                    gi = h // 4
                    _to_lane_blocks(outs[3 * gi + which], h % 4, y, DILATIONS[gi], scr, BF16)
                else:
                    hb = h - NH_A
                    outs[9 + which][:, hb * HD:(hb + 1) * HD] = y.astype(BF16)

    def blk(c):
        return pl.BlockSpec((TOK, QKV), lambda i: (i, c))
    tab = pl.BlockSpec((TOK, HD), lambda i: (i, 0))
    out_specs, out_shape = [], []
    for d in DILATIONS:
        out_specs += [_lane_block_spec(d)] * 3
        out_shape += [jax.ShapeDtypeStruct((4, S // d, d * HD), BF16)] * 3
    out_specs += [pl.BlockSpec((TOK, D_BR), lambda i: (i, 0))] * 3
    out_shape += [jax.ShapeDtypeStruct((S, D_BR), BF16)] * 3
    outs = pl.pallas_call(
        body, name="qk_prep", grid=(S // TOK,),
        in_specs=[blk(0), blk(1), blk(2), pl.BlockSpec((8, HD), lambda i: (0, 0)), tab, tab],
        out_specs=out_specs, out_shape=out_shape,
        scratch_shapes=[pltpu.VMEM((TOK, HD), F32)],
        compiler_params=_params(("parallel",)),
    )(proj, proj, proj, gains, cos2, sin2)
    return [tuple(outs[3 * gi:3 * gi + 3]) for gi in range(3)], tuple(outs[9:12])


def _qk_prep_bwd(dproj, proj, gains, cos2, sin2, grads_a, grads_b):
    def body(dp_in, q_ref, k_ref, g_ref, c_ref, s_ref, *rest):
        grads, (dp_out, dg_ref, scr) = rest[:12], rest[12:]
        del dp_in
        cos, sin = c_ref[...], s_ref[...]

        def grad_of(which, h):
            if h < NH_A:
                gi = h // 4
                return _from_lane_blocks(grads[3 * gi + which], h % 4, DILATIONS[gi], HD, scr)
            hb = h - NH_A
            return grads[9 + which][:, hb * HD:(hb + 1) * HD]

        dg_rows = []
        for which, (src, base, row_a, row_b) in enumerate(((q_ref, 0, 0, 2), (k_ref, QKV, 1, 3))):
            dg_a = jnp.zeros((1, HD), F32)
            dg_b = jnp.zeros((1, HD), F32)
            for h in range(NH):
                t = src[:, h * HD:(h + 1) * HD]
                dy = grad_of(which, h)
                r = lax.rsqrt(jnp.mean(t * t, axis=-1, keepdims=True) + EPS)
                xh = t * r
                if h < NH_A:
                    dy = dy * cos - _swap_halves(dy) * sin
                    gain = g_ref[row_a:row_a + 1, :]
                    dg_a = dg_a + jnp.sum(dy * xh, axis=0, keepdims=True)
                else:
                    gain = g_ref[row_b:row_b + 1, :]
                    dg_b = dg_b + jnp.sum(dy * xh, axis=0, keepdims=True)
                u = dy * gain
                dx = r * (u - xh * jnp.mean(u * xh, axis=-1, keepdims=True))
                dp_out[:, base + h * HD:base + (h + 1) * HD] = dx.astype(BF16)
            dg_rows += [(row_a, dg_a), (row_b, dg_b)]
        for h in range(NH):
            dp_out[:, 2 * QKV + h * HD:2 * QKV + (h + 1) * HD] = grad_of(2, h).astype(BF16)

        @pl.when(pl.program_id(0) == 0)
        def _():
            dg_ref[...] = jnp.zeros((8, HD), F32)

        for row, val in dg_rows:
            dg_ref[row:row + 1, :] += val

    def blk(c):
        return pl.BlockSpec((TOK, QKV), lambda i: (i, c))
    tab = pl.BlockSpec((TOK, HD), lambda i: (i, 0))
    gain_spec = pl.BlockSpec((8, HD), lambda i: (0, 0))
    grad_specs = [s for d in DILATIONS for s in [_lane_block_spec(d)] * 3]
    grad_specs += [pl.BlockSpec((TOK, D_BR), lambda i: (i, 0))] * 3
    return pl.pallas_call(
        body, name="qk_prep_bwd", grid=(S // TOK,),
        in_specs=[pl.BlockSpec(memory_space=pl.ANY), blk(0), blk(1), gain_spec, tab, tab] + grad_specs,
        out_specs=[pl.BlockSpec((TOK, 3 * QKV), lambda i: (i, 0)), gain_spec],
        out_shape=[jax.ShapeDtypeStruct((S, D_IN), BF16), jax.ShapeDtypeStruct((8, HD), F32)],
        input_output_aliases={0: 0},
        scratch_shapes=[pltpu.VMEM((TOK, HD), F32)],
        compiler_params=_params(("arbitrary",)),
    )(dproj, proj, proj, gains, cos2, sin2, *[g for grp in grads_a for g in grp], *grads_b)


def _mix_fwd(oa, ob, w_pa, w_pb, proj, b_gate, *, tr=256):
    def body(oa_ref, ob_ref, pa_ref, pb_ref, la_ref, lb_ref, ba_ref, bb_ref, mix_ref, ya_ref, yb_ref):
        ya = jnp.concatenate([_dot(oa_ref[...], pa_ref[q], NN) for q in range(N_DEV)], axis=1)
        yb = jnp.concatenate([_dot(ob_ref[...], pb_ref[q], NN) for q in range(N_DEV)], axis=1)
        ga = jax.nn.sigmoid(la_ref[...] + ba_ref[...])
        gb = jax.nn.sigmoid(lb_ref[...] + bb_ref[...])
        mix_ref[...] = (ga * ya + gb * yb).astype(BF16)
        ya_ref[...] = ya.astype(BF16)
        yb_ref[...] = yb.astype(BF16)

    row = pl.BlockSpec((tr, D), lambda i: (i, 0))
    branch = pl.BlockSpec((tr, D_BR), lambda i: (i, 0))
    whole = pl.BlockSpec((N_DEV, D_BR, D // N_DEV), lambda i: (0, 0, 0))
    return pl.pallas_call(
        body, name="mix_fwd", grid=(S // tr,),
        in_specs=[branch, branch, whole, whole,
                  pl.BlockSpec((tr, D), lambda i: (i, 3)), pl.BlockSpec((tr, D), lambda i: (i, 4)),
                  pl.BlockSpec((1, D), lambda i: (0, 0)), pl.BlockSpec((1, D), lambda i: (0, 1))],
        out_specs=[row, row, row], out_shape=[jax.ShapeDtypeStruct((S, D), BF16)] * 3,
        compiler_params=_params(("parallel",)),
    )(oa, ob, w_pa, w_pb, proj, proj, b_gate, b_gate)


def _gate_bwd(branch, dmixed, proj, b_gate, y, dproj, *, tr=256):
    aliased = dproj is not None

    def body(dm_ref, l_ref, b_ref, y_ref, *rest):
        dy_ref, dp_ref, db_ref = rest[-3:]
        g = jax.nn.sigmoid(l_ref[...] + b_ref[...])
        dm = dm_ref[...]
        dy_ref[...] = (dm * g).astype(BF16)
        dl = dm * y_ref[...].astype(F32) * g * (1.0 - g)
        dp_ref[...] = dl.astype(BF16)
        part = jnp.sum(dl, axis=0, keepdims=True)

        @pl.when(pl.program_id(0) == 0)
        def _():
            db_ref[...] = part

        @pl.when(pl.program_id(0) > 0)
        def _():
            db_ref[...] += part

    row = pl.BlockSpec((tr, D), lambda i: (i, 0))
    col = pl.BlockSpec((tr, D), lambda i: (i, 3 + branch))
    vec = pl.BlockSpec((1, D), lambda i: (0, 0))
    return pl.pallas_call(
        body, name=f"gate_bwd_{branch}", grid=(S // tr,),
        in_specs=[row, col, pl.BlockSpec((1, D), lambda i: (0, branch)), row]
        + ([pl.BlockSpec(memory_space=pl.ANY)] if aliased else []),
        out_specs=[row, col, vec],
        out_shape=[jax.ShapeDtypeStruct((S, D), BF16), jax.ShapeDtypeStruct((S, D_IN), BF16),
                   jax.ShapeDtypeStruct((1, D), F32)],
        input_output_aliases={4: 1} if aliased else {},
        compiler_params=_params(("arbitrary",)),
    )(dmixed, proj, b_gate, y, *([dproj] if aliased else []))


def _band_blocks(m_len):
    wk = min(m_len, QB + 2 * QB)
    return [(qb * QB, min(max(qb * QB - QB, 0), m_len - wk), wk) for qb in range(m_len // QB)]


def _band_scores(q, kw, q0, k0, wk):
    s = _dot(q, kw, NT) * SCALE
    qpos = q0 + lax.broadcasted_iota(jnp.int32, (QB, 1), 0)
    kpos = k0 + lax.broadcasted_iota(jnp.int32, (1, wk), 1)
    return jnp.where(jnp.abs(kpos - qpos) <= HALF_A, s, NEG)


def _attn_a_fwd(q, k, v, gi):
    d = DILATIONS[gi]
    m_len = S // d

    def body(q_ref, k_ref, v_ref, o_ref, lse_ref):
        for r in range(d):
            lanes = slice(r * HD, (r + 1) * HD)
            for q0, k0, wk in _band_blocks(m_len):
                s = _band_scores(q_ref[q0:q0 + QB, lanes], k_ref[k0:k0 + wk, lanes], q0, k0, wk)
                m = jnp.max(s, axis=-1, keepdims=True)
                p = jnp.exp(s - m)
                l = jnp.sum(p, axis=-1, keepdims=True)
                o_ref[q0:q0 + QB, lanes] = _dot(p.astype(BF16), v_ref[k0:k0 + wk, lanes], NN) / l
                lse_ref[q0:q0 + QB, r:r + 1] = m + jnp.log(l)

    head = pl.BlockSpec((None, m_len, d * HD), lambda h: (h, 0, 0))
    stat = pl.BlockSpec((None, m_len, d), lambda h: (h, 0, 0))
    return pl.pallas_call(
        body, name=f"attn_a_fwd_{gi}", grid=(4,),
        in_specs=[head, head, head], out_specs=[head, stat],
        out_shape=[jax.ShapeDtypeStruct((4, m_len, d * HD), F32), jax.ShapeDtypeStruct((4, m_len, d), F32)],
        compiler_params=_params(("parallel",)),
    )(q, k, v)


def _combine_a(os, lses):
    def body(o0, o1, o2, l0, l1, l2, oa_ref, lse_ref, scr, scr1):
        for h in range(4):
            o = [_from_lane_blocks(ref, h, d, HD, scr) for ref, d in zip((o0, o1, o2), DILATIONS)]
            a, b, c = (_from_lane_blocks(ref, h, d, 1, scr1) for ref, d in zip((l0, l1, l2), DILATIONS))
            m = jnp.maximum(jnp.maximum(a, b), c)
            wa, wb, wc = jnp.exp(a - m), jnp.exp(b - m), jnp.exp(c - m)
            tot = wa + wb + wc
            oa_ref[:, h * HD:(h + 1) * HD] = ((wa * o[0] + wb * o[1] + wc * o[2]) / tot).astype(BF16)
            lse_ref[h] = m + jnp.log(tot)

    return pl.pallas_call(
        body, name="combine_a", grid=(S // TOK,),
        in_specs=[_lane_block_spec(d) for d in DILATIONS] + [_lane_block_spec(d, 1) for d in DILATIONS],
        out_specs=[pl.BlockSpec((TOK, D_BR), lambda i: (i, 0)), pl.BlockSpec((4, TOK, 1), lambda i: (0, i, 0))],
        out_shape=[jax.ShapeDtypeStruct((S, D_BR), BF16), jax.ShapeDtypeStruct((4, S, 1), F32)],
        scratch_shapes=[pltpu.VMEM((TOK, HD), F32), pltpu.VMEM((TOK, 1), F32)],
        compiler_params=_params(("parallel",)),
    )(*os, *lses)


def _attn_a_bwd_prep(doa, oa, lse):
    def body(do_ref, o_ref, l_ref, *rest):
        outs, (scr, scr1) = rest[:9], rest[9:]
        for h in range(4):
            do = do_ref[:, h * HD:(h + 1) * HD]
            dsum = jnp.sum(do * o_ref[:, h * HD:(h + 1) * HD].astype(F32), axis=-1, keepdims=True)
            for gi, d in enumerate(DILATIONS):
                _to_lane_blocks(outs[3 * gi], h, do, d, scr, BF16)
                _to_lane_blocks(outs[3 * gi + 1], h, l_ref[h], d, scr1, F32)
                _to_lane_blocks(outs[3 * gi + 2], h, dsum, d, scr1, F32)

    row = pl.BlockSpec((TOK, D_BR), lambda i: (i, 0))
    out_specs, out_shape = [], []
    for d in DILATIONS:
        out_specs += [_lane_block_spec(d), _lane_block_spec(d, 1), _lane_block_spec(d, 1)]
        out_shape += [jax.ShapeDtypeStruct((4, S // d, d * HD), BF16)] + [jax.ShapeDtypeStruct((4, S // d, d), F32)] * 2
    outs = pl.pallas_call(
        body, name="attn_a_bwd_prep", grid=(S // TOK,),
        in_specs=[row, row, pl.BlockSpec((4, TOK, 1), lambda i: (0, i, 0))],
        out_specs=out_specs, out_shape=out_shape,
        scratch_shapes=[pltpu.VMEM((TOK, HD), F32), pltpu.VMEM((TOK, 1), F32)],
        compiler_params=_params(("parallel",)),
    )(doa, oa, lse)
    return [tuple(outs[3 * gi:3 * gi + 3]) for gi in range(3)]


def _attn_a_bwd(q, k, v, do, lse, dsum, gi):
    d = DILATIONS[gi]
    m_len = S // d

    def body(q_ref, k_ref, v_ref, do_ref, lse_ref, dsum_ref, dq_ref, dk_ref, dv_ref):
        dk_ref[...] = jnp.zeros((m_len, d * HD), F32)
        dv_ref[...] = jnp.zeros((m_len, d * HD), F32)
        for r in range(d):
            lanes = slice(r * HD, (r + 1) * HD)
            for q0, k0, wk in _band_blocks(m_len):
                rows, keys = slice(q0, q0 + QB), slice(k0, k0 + wk)
                qv, kw, vw, dov = q_ref[rows, lanes], k_ref[keys, lanes], v_ref[keys, lanes], do_ref[rows, lanes]
                p = jnp.exp(_band_scores(qv, kw, q0, k0, wk) - lse_ref[rows, r:r + 1])
                ds = (p * (_dot(dov, vw, NT) - dsum_ref[rows, r:r + 1]) * SCALE).astype(BF16)
                dq_ref[rows, lanes] = _dot(ds, kw, NN)
                dk_ref[keys, lanes] += _dot(ds, qv, TN)
                dv_ref[keys, lanes] += _dot(p.astype(BF16), dov, TN)

    head = pl.BlockSpec((None, m_len, d * HD), lambda h: (h, 0, 0))
    stat = pl.BlockSpec((None, m_len, d), lambda h: (h, 0, 0))
    shape = jax.ShapeDtypeStruct((4, m_len, d * HD), F32)
    return pl.pallas_call(
        body, name=f"attn_a_bwd_{gi}", grid=(4,),
        in_specs=[head, head, head, head, stat, stat], out_specs=[head, head, head],
        out_shape=[shape, shape, shape],
        compiler_params=_params(("parallel",)),
    )(q, k, v, do, lse, dsum)


KEYS_B = WIN_R * GRID_W
N_OFF = WIN_R


def _bias_constants():
    q = np.arange(GRID_W)[:, None]
    kc = np.arange(GRID_W)[None, :]
    dc = np.clip(kc - q, -(WIN_C - 1), WIN_C - 1) + (WIN_C - 1)
    expand = np.zeros((HD, GRID_W * GRID_W), np.float32)
    expand[dc.reshape(-1), np.arange(GRID_W * GRID_W)] = 1.0
    cs = np.clip(q - WIN_C // 2, 0, GRID_W - WIN_C)
    keep = ((kc >= cs) & (kc < cs + WIN_C)).reshape(1, -1).astype(np.float32)
    sel = np.zeros((64, 4 * N_OFF * WIN_R), np.float32)
    for h in range(4):
        for off in range(N_OFF):
            for j in range(WIN_R):
                sel[h * (2 * WIN_R - 1) + off + j, (h * N_OFF + off) * WIN_R + j] = 1.0
    return jnp.asarray(expand), jnp.asarray(keep), jnp.asarray(sel)


def _bias_expand(rpb_pad, expand, keep, sel):
    def body(r_ref, e_ref, k_ref, s_ref, o_ref):
        t = lax.dot_general(r_ref[...], e_ref[...], NN, precision=lax.Precision.HIGHEST,
                            preferred_element_type=F32)
        rows = lax.dot_general(s_ref[...], t, TN, precision=lax.Precision.HIGHEST,
                               preferred_element_type=F32)
        o_ref[...] = jnp.where(k_ref[...] > 0.5, rows, NEG)

    return pl.pallas_call(
        body, name="bias_expand",
        out_shape=jax.ShapeDtypeStruct((4 * N_OFF * WIN_R, GRID_W * GRID_W), F32),
        compiler_params=pltpu.CompilerParams(vmem_limit_bytes=VMEM_LIMIT),
    )(rpb_pad, expand, keep, sel)


def _bias_reduce(dbias_rows, expand, sel):
    def body(x_ref, e_ref, s_ref, o_ref):
        z = lax.dot_general(x_ref[...], e_ref[...], NT, precision=lax.Precision.HIGHEST,
                            preferred_element_type=F32)
        o_ref[...] = lax.dot_general(s_ref[...], z, NN, precision=lax.Precision.HIGHEST,
                                     preferred_element_type=F32)

    return pl.pallas_call(
        body, name="bias_reduce", out_shape=jax.ShapeDtypeStruct((64, HD), F32),
        compiler_params=pltpu.CompilerParams(vmem_limit_bytes=VMEM_LIMIT),
    )(dbias_rows, expand, sel)


def _rows_to_tab(rows):
    t = rows.reshape(4, N_OFF, WIN_R, GRID_W, GRID_W)
    return t.transpose(0, 1, 3, 2, 4).reshape(4, N_OFF, GRID_W, KEYS_B)


def _tab_to_rows(tab):
    t = tab.reshape(4, N_OFF, GRID_W, WIN_R, GRID_W)
    return t.transpose(0, 1, 3, 2, 4).reshape(4 * N_OFF * WIN_R, GRID_W * GRID_W)


def _row_window(r):
    r0 = jnp.clip(r - WIN_R // 2, 0, ROWS - WIN_R)
    off = r0 + (WIN_R - 1) - r
    return pl.multiple_of(r * GRID_W, GRID_W), pl.multiple_of(r0 * GRID_W, GRID_W), off


def _attn_b_fwd(qn, kn, vb, bias_tab):
    def body(q_ref, k_ref, v_ref, b_ref, o_ref, lse_ref):
        def row(r, carry):
            qs, ks, off = _row_window(r)
            q = q_ref[pl.ds(qs, GRID_W), :]
            s = lax.dot_general(q, k_ref[pl.ds(ks, KEYS_B), :], NT, preferred_element_type=F32) * SCALE
            s = s + b_ref[off]
            m = jnp.max(s, axis=-1, keepdims=True)
            p = jnp.exp(s - m)
            l = jnp.sum(p, axis=-1, keepdims=True)
            o = lax.dot_general(p.astype(BF16), v_ref[pl.ds(ks, KEYS_B), :], NN, preferred_element_type=F32)
            o_ref[pl.ds(qs, GRID_W), :] = (o / l).astype(BF16)
            lse_ref[pl.ds(qs, GRID_W), :] = m + jnp.log(l)
            return carry

        lax.fori_loop(0, ROWS, row, 0)

    full = pl.BlockSpec((S, HD), lambda h: (0, h))
    return pl.pallas_call(
        body, name="attn_b_fwd", grid=(4,),
        in_specs=[full, full, full, pl.BlockSpec((None, N_OFF, GRID_W, KEYS_B), lambda h: (h, 0, 0, 0))],
        out_specs=[pl.BlockSpec((S, HD), lambda h: (0, h)), pl.BlockSpec((None, S, 1), lambda h: (h, 0, 0))],
        out_shape=[jax.ShapeDtypeStruct((S, D_BR), BF16), jax.ShapeDtypeStruct((4, S, 1), F32)],
        compiler_params=_params(("parallel",)),
    )(qn, kn, vb, bias_tab)


def _attn_b_bwd(qn, kn, vb, bias_tab, ob, dob, lse):
    def body(q_ref, k_ref, v_ref, b_ref, o_ref, do_ref, lse_ref, dq_ref, dk_ref, dv_ref, db_ref):
        dk_ref[...] = jnp.zeros((S, HD), F32)
        dv_ref[...] = jnp.zeros((S, HD), F32)
        db_ref[...] = jnp.zeros((N_OFF, GRID_W, KEYS_B), F32)

        def row(r, carry):
            qs, ks, off = _row_window(r)
            rows = pl.ds(qs, GRID_W)
            keys = pl.ds(ks, KEYS_B)
            q = q_ref[rows, :]
            kw = k_ref[keys, :]
            s = lax.dot_general(q, kw, NT, preferred_element_type=F32) * SCALE + b_ref[off]
            p = jnp.exp(s - lse_ref[rows, :])
            do = do_ref[rows, :]
            dobf = do.astype(BF16)
            dsum = jnp.sum(do * o_ref[rows, :].astype(F32), axis=-1, keepdims=True)
            dp = lax.dot_general(dobf, v_ref[keys, :], NT, preferred_element_type=F32)
            ds = p * (dp - dsum)
            db_ref[off] += ds
            dsb = (ds * SCALE).astype(BF16)
            dq_ref[rows, :] = lax.dot_general(dsb, kw, NN, preferred_element_type=F32)
            dk_ref[keys, :] += lax.dot_general(dsb, q, TN, preferred_element_type=F32)
            dv_ref[keys, :] += lax.dot_general(p.astype(BF16), dobf, TN, preferred_element_type=F32)
            return carry

        lax.fori_loop(0, ROWS, row, 0)

    full = pl.BlockSpec((S, HD), lambda h: (0, h))
    slot = pl.BlockSpec((S, HD), lambda h: (0, h))
    tab = pl.BlockSpec((None, N_OFF, GRID_W, KEYS_B), lambda h: (h, 0, 0, 0))
    shape = jax.ShapeDtypeStruct((S, D_BR), F32)
    return pl.pallas_call(
        body, name="attn_b_bwd", grid=(4,),
        in_specs=[full, full, full, tab, slot, slot, pl.BlockSpec((None, S, 1), lambda h: (h, 0, 0))],
        out_specs=[slot, slot, slot, tab],
        out_shape=[shape, shape, shape, jax.ShapeDtypeStruct((4, N_OFF, GRID_W, KEYS_B), F32)],
        compiler_params=_params(("parallel",)),
    )(qn, kn, vb, bias_tab, ob, dob, lse)


def _epi_relu_sq(acc, ex, outs):
    u = jnp.maximum(acc, 0.0)
    outs[0][...] = u.astype(BF16)
    outs[1][...] = (u * u).astype(BF16)


def _epi_relu_sq_bwd(acc, ex, outs):
    outs[0][...] = (acc * (2.0 * ex[0][...].astype(F32))).astype(BF16)


def _epi_loss_head(acc, ex, outs):
    e = acc + ex[0][...] - ex[1][...]
    dy = e * (1.0 / D)
    outs[0][...] = dy
    outs[1][...] = dy.astype(BF16)
    part = (0.5 / D) * jnp.sum(jnp.sum(e * e, axis=-1, keepdims=True), axis=0, keepdims=True)
    first = (pl.program_id(0) == 0) & (pl.program_id(1) == 0)

    @pl.when(first)
    def _():
        outs[2][...] = part

    @pl.when(jnp.logical_not(first))
    def _():
        outs[2][...] += part


def _local_step(x, target, norm_mix, b_gate, gains, rpb_pad, norm_ffn,
                w_in, w_pa, w_pb, w_out, w_up, w_down, on_grads):
    cos2, sin2 = _rope_tables()
    expand, keep, sel = _bias_constants()
    w_out3, w_down3 = w_out[None], w_down[None]

    xn, rstd1 = _rms_fwd(x, norm_mix, name="rms_mix")
    proj = _mm_nn(xn, w_in, tm=1024, tn=1280, name="proj")
    qkv_a, qkv_b = _qk_prep(proj, gains, cos2, sin2)
    fwd_a = [_attn_a_fwd(*qkv_a[gi], gi) for gi in range(3)]
    oa, lse_a = _combine_a([o for o, _ in fwd_a], [l for _, l in fwd_a])
    bias_tab = _rows_to_tab(_bias_expand(rpb_pad, expand, keep, sel))
    ob, lse_b = _attn_b_fwd(*qkv_b, bias_tab)
    mixed, ya, yb = _mix_fwd(oa, ob, w_pa, w_pb, proj, b_gate)
    h1 = _mm_nn(mixed, w_out3, tm=1024, tn=1024, name="out_proj", epi=_epi_residual, extra=(x,))
    hn, rstd2 = _rms_fwd(h1, norm_ffn, name="rms_ffn")
    u, usq = _mm_nn(hn, w_up, tm=1024, tn=1024, name="ffn_up", epi=_epi_relu_sq,
                    out_dtypes=(BF16, BF16))
    dy, dyb, loss = _mm_nn(usq, w_down3, tm=512, tn=512, name="ffn_down", epi=_epi_loss_head,
                           extra=(h1, target), out_dtypes=(F32, BF16), total=True)

    g_down = _mm_tn(usq, dyb, tm=1024, tn=1024, name="grad_w_down")
    sent = on_grads("w_down", {5: g_down.reshape(N_DEV, D_FF // N_DEV, D)})
    du = _mm_nt(dyb, w_down3, tm=1024, tn=1024, name="ffn_down_bwd", out_dtype=BF16,
                epi=_epi_relu_sq_bwd, extra=(u,), after=sent)
    g_up = _mm_tn(hn, du, tm=1024, tn=1024, groups=N_DEV, name="grad_w_up")
    sent = on_grads("w_up", {4: g_up})
    dhn = _mm_nt(du, w_up, tm=512, tn=512, name="ffn_up_bwd", after=sent)
    dh1, dh1b, g_norm_ffn = _rms_bwd(dhn, h1, rstd2, norm_ffn, dy, name="rms_ffn_bwd")

    g_out = _mm_tn(mixed, dh1b, tm=1024, tn=1024, name="grad_w_out")
    dmixed = _mm_nt(dh1b, w_out3, tm=1024, tn=1024, name="out_proj_bwd")
    dya, dproj, g_ba = _gate_bwd(0, dmixed, proj, b_gate, ya, None)
    dyb2, dproj, g_bb = _gate_bwd(1, dmixed, proj, b_gate, yb, dproj)
    g_pa = _mm_tn(oa, dya, tm=D_BR, tn=1024, groups=N_DEV, name="grad_w_proj_a")
    g_pb = _mm_tn(ob, dyb2, tm=D_BR, tn=1024, groups=N_DEV, name="grad_w_proj_b")
    sent = on_grads("w_mix", {3: g_out.reshape(N_DEV, D // N_DEV, D), 1: g_pa, 2: g_pb})
    doa = _mm_nt(dya, w_pa, tm=1024, tn=D_BR, name="proj_a_bwd", after=sent)
    dob = _mm_nt(dyb2, w_pb, tm=1024, tn=D_BR, name="proj_b_bwd")
    prep = _attn_a_bwd_prep(doa, oa, lse_a)
    grads_a = [_attn_a_bwd(*qkv_a[gi], *prep[gi], gi) for gi in range(3)]
    dqb, dkb, dvb, dbias = _attn_b_bwd(*qkv_b, bias_tab, ob, dob, lse_b)
    g_rpb = _bias_reduce(_tab_to_rows(dbias), expand, sel)
    dproj, g_gains = _qk_prep_bwd(dproj, proj, gains, cos2, sin2, grads_a, (dqb, dkb, dvb))
    g_in = _mm_tn(xn, dproj, tm=1024, tn=1280, groups=N_DEV, name="grad_w_in")
    sent = on_grads("w_in", {0: g_in})
    dxn = _mm_nt(dproj, w_in, tm=256, tn=512, name="proj_bwd", after=sent)
    grad_x, _, g_norm_mix = _rms_bwd(dxn, x, rstd1, norm_mix, dh1, name="rms_mix_bwd")

    small = (g_norm_mix, g_ba, g_bb, g_gains, g_rpb, g_norm_ffn)
    return loss, grad_x, small


def _cast_bf16(w, *, tr=256):
    rows, cols = w.shape
    tr = min(tr, rows)

    def body(w_ref, o_ref):
        o_ref[...] = w_ref[...].astype(BF16)

    spec = pl.BlockSpec((tr, cols), lambda i: (i, 0))
    return pl.pallas_call(
        body, name=f"cast_{rows}x{cols}", grid=(rows // tr,), in_specs=[spec], out_specs=spec,
        out_shape=jax.ShapeDtypeStruct((rows, cols), BF16), compiler_params=_params(("parallel",)),
    )(w)


def _me_and_peers():
    x, y, c = lax.axis_index("x"), lax.axis_index("y"), lax.axis_index("c")
    me = 4 * x + 2 * y + c
    peers = []
    for k in range(1, N_DEV):
        px = 1 - x if k & 4 else x
        py = 1 - y if k & 2 else y
        pc = 1 - c if k & 1 else c
        peers.append(((px, py, pc), 4 * px + 2 * py + pc))
    return me, peers


def _gather_on_sequencer(shards, name):
    n = len(shards)
    hbm = pltpu.MemorySpace.HBM
    ins = [jax.new_ref(s, memory_space=hbm) for s in shards]
    outs = [jax.empty_ref(jax.ShapeDtypeStruct((N_DEV,) + s.shape, s.dtype), memory_space=hbm) for s in shards]

    @pl.kernel(mesh=plsc.ScalarSubcoreMesh(axis_name="seq", num_cores=1), name=name,
               scratch_types=(pltpu.SemaphoreType.DMA((n, N_DEV - 1)), pltpu.SemaphoreType.DMA((n, N_DEV - 1)),
                              pltpu.SemaphoreType.DMA((n,))),
               compiler_params=pltpu.CompilerParams(collective_id=0))
    def launch(send, recv, lsem):
        x, y, c = lax.axis_index("x"), lax.axis_index("y"), lax.axis_index("c")
        me, sibling = (x, y, c), (x, y, 1 - c)
        chips = [(1 - x, y), (x, 1 - y), (1 - x, 1 - y)]
        barrier = pltpu.get_barrier_semaphore()
        for peer in [sibling] + [(*chip, c) for chip in chips]:
            pl.semaphore_signal(barrier, inc=1, device_id=peer, device_id_type=MESH)
        pl.semaphore_wait(barrier, 4)

        def copy(w, k, block, to, src=None):
            px, py, pc = block
            dst = outs[w].at[4 * px + 2 * py + pc]
            return pltpu.make_async_remote_copy(dst if src is None else src, dst, send.at[w, k], recv.at[w, k],
                                                device_id=to, device_id_type=MESH)

        local = [pltpu.make_async_copy(ins[w], outs[w].at[4 * x + 2 * y + c], lsem.at[w]) for w in range(n)]
        for cp in local:
            cp.start()
        first = []
        for w in range(n):
            first += [copy(w, 1 + j, me, (*chip, c), src=ins[w]) for j, chip in enumerate(chips)]
            first.append(copy(w, 0, me, sibling, src=ins[w]))
        for cp in first:
            cp.start()
        passed = []
        for w in range(n):
            for j, chip in enumerate(chips):
                copy(w, 1 + j, (*chip, c), me).wait_recv()
                cp = copy(w, 4 + j, (*chip, c), sibling)
                cp.start()
                passed.append(cp)
        for w in range(n):
            copy(w, 0, sibling, me).wait_recv()
            for j, chip in enumerate(chips):
                copy(w, 4 + j, (*chip, 1 - c), me).wait_recv()
        for cp in first + passed:
            cp.wait_send()
        for cp in local:
            cp.wait()

    launch()
    return [o[...] for o in outs]


N_CHIP = 4
CHIPS = ((0, 0), (0, 1), (1, 0), (1, 1))


def _sequencer(name, n_sems, collective_id):
    return functools.partial(
        pl.kernel, mesh=plsc.ScalarSubcoreMesh(axis_name="seq", num_cores=1), name=name,
        scratch_types=tuple(pltpu.SemaphoreType.DMA(s) for s in n_sems),
        compiler_params=pltpu.CompilerParams(collective_id=collective_id))


def _handshake(peers):
    barrier = pltpu.get_barrier_semaphore()
    for peer in peers:
        pl.semaphore_signal(barrier, inc=1, device_id=peer, device_id_type=MESH)
    pl.semaphore_wait(barrier, len(peers))


def _chip_exchange_on_sequencer(parts, name):
    n = len(parts)
    hbm = pltpu.MemorySpace.HBM
    ins = [jax.new_ref(p, memory_space=hbm) for p in parts]
    outs = [jax.empty_ref(jax.ShapeDtypeStruct(p.shape, p.dtype), memory_space=hbm) for p in parts]

    @_sequencer(name, ((n, 3), (n, 3), (n,)), 2)
    def launch(send, recv, lsem):
        x, y, c = lax.axis_index("x"), lax.axis_index("y"), lax.axis_index("c")
        mine = 2 * x + y
        chips = [(1 - x, y), (x, 1 - y), (1 - x, 1 - y)]
        _handshake([(*chip, c) for chip in chips])
        local = [pltpu.make_async_copy(ins[w].at[mine], outs[w].at[mine], lsem.at[w]) for w in range(n)]
        for cp in local:
            cp.start()
        sends = []
        for w in range(n):
            for j, (px, py) in enumerate(chips):
                cp = pltpu.make_async_remote_copy(ins[w].at[2 * px + py], outs[w].at[mine],
                                                  send.at[w, j], recv.at[w, j],
                                                  device_id=(px, py, c), device_id_type=MESH)
                cp.start()
                sends.append(cp)
        for w in range(n):
            for j, (px, py) in enumerate(chips):
                pltpu.make_async_remote_copy(ins[w].at[mine], outs[w].at[2 * px + py],
                                             send.at[w, j], recv.at[w, j],
                                             device_id=(px, py, c), device_id_type=MESH).wait_recv()
        for cp in sends:
            cp.wait_send()
        for cp in local:
            cp.wait()

    launch()
    return [o[...] for o in outs]


PAIR_CHUNK_BYTES = 3 * 1024 * 1024


def _pair_sum(grads, name, after=()):
    n = len(grads)
    n_after = len(after)
    shapes = [g.shape[1:] for g in grads]
    splits = []
    for rows, cols in shapes:
        ns = 1
        while rows * cols * 2 // ns > PAIR_CHUNK_BYTES and rows // (2 * ns) >= 16:
            ns *= 2
        splits.append(ns)

    def body(*refs):
        ins, refs = refs[:n], refs[n + n_after:]
        outs = refs[:n]
        lands = refs[n:2 * n]
        mines = refs[2 * n:3 * n]
        stages = refs[3 * n:4 * n]
        rsend, rrecv, lsem, osem = refs[4 * n:]
        x, y, c = lax.axis_index("x"), lax.axis_index("y"), lax.axis_index("c")
        remote = {}
        for w in range(n):
            for ch, (px, py) in enumerate(CHIPS):
                cp = pltpu.make_async_remote_copy(ins[w].at[4 * px + 2 * py + 1 - c], lands[w].at[ch],
                                                  rsend.at[w, ch], rrecv.at[w, ch],
                                                  device_id=(x, y, 1 - c), device_id_type=MESH)
                cp.start()
                remote[w, ch] = cp
        for w in range(n):
            rc = shapes[w][0] // splits[w]
            pieces = [(ch, p) for ch in range(N_CHIP) for p in range(splits[w])]

            def load(i, w=w, rc=rc, pieces=pieces):
                ch, p = pieces[i]
                px, py = CHIPS[ch]
                return pltpu.make_async_copy(ins[w].at[4 * px + 2 * py + c, pl.ds(p * rc, rc)],
                                             mines[w].at[i % 2], lsem.at[w, i % 2])

            def store(i, w=w, rc=rc, pieces=pieces):
                ch, p = pieces[i]
                return pltpu.make_async_copy(stages[w].at[i % 2], outs[w].at[ch, pl.ds(p * rc, rc)],
                                             osem.at[w, i % 2])

            load(0).start()
            for i, (ch, p) in enumerate(pieces):
                if i + 1 < len(pieces):
                    load(i + 1).start()
                load(i).wait()
                if p == 0:
                    remote[w, ch].wait_recv()
                if i >= 2:
                    store(i - 2).wait()
                theirs = lands[w][ch, p * rc:(p + 1) * rc, :]
                stages[w][i % 2] = (mines[w][i % 2].astype(F32) + theirs.astype(F32)).astype(BF16)
                store(i).start()
            for i in range(max(0, len(pieces) - 2), len(pieces)):
                store(i).wait()
        for cp in remote.values():
            cp.wait_send()

    hbm = pl.BlockSpec(memory_space=pl.ANY)
    scratch = [pltpu.VMEM((N_CHIP,) + sh, BF16) for sh in shapes]
    scratch += [pltpu.VMEM((2, sh[0] // ns, sh[1]), BF16) for sh, ns in zip(shapes, splits)] * 2
    scratch += [pltpu.SemaphoreType.DMA((n, N_CHIP)), pltpu.SemaphoreType.DMA((n, N_CHIP)),
                pltpu.SemaphoreType.DMA((n, 2)), pltpu.SemaphoreType.DMA((n, 2))]
    return pl.pallas_call(
        body, name=name, in_specs=[hbm] * (n + n_after), out_specs=[hbm] * n,
        out_shape=[jax.ShapeDtypeStruct((N_CHIP,) + sh, BF16) for sh in shapes],
        scratch_shapes=scratch,
        compiler_params=pltpu.CompilerParams(vmem_limit_bytes=VMEM_LIMIT),
    )(*grads, *after)


def _adamw_math(g, w, m, v):
    m2 = B1 * m + (1.0 - B1) * g
    v2 = B2 * v + (1.0 - B2) * (g * g)
    delta = -LR * ((m2 / BC1) / (jnp.sqrt(v2 / BC2) + AEPS) + WD * w)
    return delta, m2, v2


def _adamw(parts, w, m, v, *, name, after=(), tr=256):
    rows, cols = w.shape

    def body(p_ref, w_ref, m_ref, v_ref, *rest):
        g_ref, d_ref, mo_ref, vo_ref = rest[len(after):]
        g = p_ref[0].astype(F32)
        for b in range(1, N_CHIP):
            g = g + p_ref[b].astype(F32)
        delta, m2, v2 = _adamw_math(g, w_ref[...], m_ref[...], v_ref[...])
        g_ref[...] = g
        d_ref[...] = delta
        mo_ref[...] = m2
        vo_ref[...] = v2

    spec = pl.BlockSpec((tr, cols), lambda i: (i, 0))
    shape = jax.ShapeDtypeStruct((rows, cols), F32)
    return pl.pallas_call(
        body, name=name, grid=(rows // tr,),
        in_specs=[pl.BlockSpec((N_CHIP, tr, cols), lambda i: (0, i, 0)), spec, spec, spec]
        + [pl.BlockSpec(memory_space=pl.ANY)] * len(after),
        out_specs=[spec] * 4, out_shape=[shape] * 4,
        compiler_params=_params(("parallel",)),
    )(parts, w, m, v, *after)


def _small_update(part, w, m, v):
    rows = part.shape[0]

    def body(p_ref, w_ref, m_ref, v_ref, g_ref, d_ref, mo_ref, vo_ref, buf, send, recv):
        me, peers = _me_and_peers()
        buf[me] = p_ref[...]
        sends = []
        for k, (dev, _) in enumerate(peers):
            cp = pltpu.make_async_remote_copy(p_ref, buf.at[me], send.at[k], recv.at[k],
                                              device_id=dev, device_id_type=MESH)
            cp.start()
            sends.append(cp)
        for k, (dev, idx) in enumerate(peers):
            pltpu.make_async_remote_copy(p_ref, buf.at[idx], send.at[k], recv.at[k],
                                         device_id=dev, device_id_type=MESH).wait_recv()
        for cp in sends:
            cp.wait_send()
        g = buf[0]
        for b in range(1, N_DEV):
            g = g + buf[b]
        delta, m2, v2 = _adamw_math(g, w_ref[...], m_ref[...], v_ref[...])
        g_ref[...] = g
        d_ref[...] = delta
        mo_ref[...] = m2
        vo_ref[...] = v2

    vm = pl.BlockSpec(memory_space=pltpu.VMEM)
    shape = jax.ShapeDtypeStruct((rows, HD), F32)
    return pl.pallas_call(
        body, name="small_params_update",
        in_specs=[vm] * 4, out_specs=[vm] * 4, out_shape=[shape] * 4,
        scratch_shapes=[pltpu.VMEM((N_DEV, rows, HD), F32),
                        pltpu.SemaphoreType.DMA((N_DEV - 1,)), pltpu.SemaphoreType.DMA((N_DEV - 1,))],
    )(part, w, m, v)


def _pack_small(norm_mix, b_gate, qa, ka, qb, kb, rpb, norm_ffn):
    gains = jnp.concatenate([qa, ka, qb, kb, jnp.zeros((4, HD), F32)], axis=0)
    rpb_pad = jnp.pad(rpb.reshape(4 * (2 * WIN_R - 1), 2 * WIN_C - 1), ((0, 4), (0, HD - (2 * WIN_C - 1))))
    return jnp.concatenate([norm_mix.reshape(16, HD), b_gate.reshape(32, HD), gains, rpb_pad,
                            norm_ffn.reshape(16, HD), jnp.zeros((8, HD), F32)], axis=0)


LOSS_ROW = 136


def _unpack_small(p):
    norm_mix = p[0:16].reshape(1, D)
    b_gate = p[16:48].reshape(1, 2 * D)
    qa, ka, qb, kb = (p[48 + i:49 + i] for i in range(4))
    rpb = p[56:116, :2 * WIN_C - 1].reshape(1, 4, 2 * WIN_R - 1, 2 * WIN_C - 1)
    norm_ffn = p[120:136].reshape(1, D)
    return norm_mix, b_gate, qa, ka, qb, kb, rpb, norm_ffn


def kernel(x, norm_mix, w_in, b_gate, q_norm_a, k_norm_a, q_norm_b, k_norm_b, rpb_b, w_proj_a, w_proj_b, w_out, norm_ffn, w_up, w_down, loss_target, m_norm_mix, m_w_in, m_b_gate, m_q_norm_a, m_k_norm_a, m_q_norm_b, m_k_norm_b, m_rpb_b, m_w_proj_a, m_w_proj_b, m_w_out, m_norm_ffn, m_w_up, m_w_down, v_norm_mix, v_w_in, v_b_gate, v_q_norm_a, v_k_norm_a, v_q_norm_b, v_k_norm_b, v_rpb_b, v_w_proj_a, v_w_proj_b, v_w_out, v_norm_ffn, v_w_up, v_w_down):
    big_w = (w_in[0], w_proj_a[0], w_proj_b[0], w_out[0], w_up[0], w_down[0])
    big_m = (m_w_in[0], m_w_proj_a[0], m_w_proj_b[0], m_w_out[0], m_w_up[0], m_w_down[0])
    big_v = (v_w_in[0], v_w_proj_a[0], v_w_proj_b[0], v_w_out[0], v_w_up[0], v_w_down[0])
    names = ("w_in", "w_proj_a", "w_proj_b", "w_out", "w_up", "w_down")

    shards = [_cast_bf16(w) for w in big_w]
    g_in, = _gather_on_sequencer(shards[0:1], "gather_w_in")
    g_pa, g_pb, g_out = _gather_on_sequencer(shards[1:4], "gather_w_mix")
    g_up, = _gather_on_sequencer(shards[4:5], "gather_w_up")
    g_down, = _gather_on_sequencer(shards[5:6], "gather_w_down")
    small_w = _pack_small(norm_mix, b_gate, q_norm_a, k_norm_a, q_norm_b, k_norm_b, rpb_b, norm_ffn)
    small_m = _pack_small(m_norm_mix, m_b_gate, m_q_norm_a, m_k_norm_a, m_q_norm_b, m_k_norm_b, m_rpb_b, m_norm_ffn)
    small_v = _pack_small(v_norm_mix, v_b_gate, v_q_norm_a, v_k_norm_a, v_q_norm_b, v_k_norm_b, v_rpb_b, v_norm_ffn)

    upd = [None] * 6
    in_flight = {}

    def finish(after):
        done = []
        for i, r in in_flight.items():
            upd[i] = _adamw(r, big_w[i], big_m[i], big_v[i], name=f"adamw_{names[i]}", after=after)
            done.append(upd[i][0])
        in_flight.clear()
        return done

    def on_grads(tag, grads):
        new = list(grads.values())
        sums = _pair_sum(new, f"pair_sum_{tag}", after=finish(new))
        in_flight.update(zip(grads, _chip_exchange_on_sequencer(sums, f"chip_exchange_{tag}")))
        return sums

    loss, grad_x, small_g = _local_step(
        x[0], loss_target[0], norm_mix, b_gate, small_w[48:56], small_w[56:120], norm_ffn,
        g_in, g_pa, g_pb, g_out.reshape(D, D), g_up, g_down.reshape(D_FF, D), on_grads)

    g_norm_mix, g_ba, g_bb, g_gains, g_rpb, g_norm_ffn = small_g
    small_part = jnp.concatenate([g_norm_mix.reshape(16, HD), g_ba.reshape(16, HD), g_bb.reshape(16, HD),
                                  g_gains, g_rpb, g_norm_ffn.reshape(16, HD),
                                  jnp.pad(loss, ((0, 7), (0, HD - 1)))], axis=0)
    slabs = _small_update(small_part, small_w, small_m, small_v)
    total = slabs[0][LOSS_ROW, 0]
    s_g, s_d, s_m, s_v = (_unpack_small(t) for t in slabs)

    finish([grad_x])
    b_g, b_d, b_m, b_v = ([u[j][None] for u in upd] for j in range(4))

    def order(small, big):
        nm, bg, qa, ka, qb, kb, rpb, nf = small
        w_in_, pa_, pb_, out_, up_, down_ = big
        return (nm, w_in_, bg, qa, ka, qb, kb, rpb, pa_, pb_, out_, nf, up_, down_)

    return (total, grad_x[None], *order(s_g, b_g), *order(s_d, b_d), *order(s_m, b_m), *order(s_v, b_v))
```

```python
import functools

import jax
import jax.numpy as jnp
import numpy as np
from jax import lax
from jax.experimental import pallas as pl
from jax.experimental.pallas import tpu as pltpu
from jax.experimental.pallas import tpu_sc as plsc

F32 = jnp.float32
BF16 = jnp.bfloat16

N_DEV = 8
S = 2048
D = 2048
HD = 128
NH = 16
NH_A = 12
QKV = NH * HD
D_IN = 3 * QKV + 2 * D
D_BR = 512
D_FF = 4 * D
GRID_W = 64
ROWS = S // GRID_W
WIN_R = 8
WIN_C = 16
EPS = 1e-6
NEG = -1e30
SCALE = HD ** -0.5
ROPE_THETA = 10000.0
DILATIONS = (1, 4, 16)
HALF_A = 64
QB = 128

LR, B1, B2, AEPS, WD, STEP = 0.001, 0.9, 0.999, 1e-08, 0.01, 10
BC1 = 1.0 - B1 ** STEP
BC2 = 1.0 - B2 ** STEP

VMEM_LIMIT = 56 * 1024 * 1024
MESH = pl.DeviceIdType.MESH

NN = (((1,), (0,)), ((), ()))
NT = (((1,), (1,)), ((), ()))
TN = (((0,), (0,)), ((), ()))


def _params(sem):
    return pltpu.CompilerParams(dimension_semantics=sem, vmem_limit_bytes=VMEM_LIMIT)


def _matmul(a, b, *, product, grid, a_spec, b_spec, epi, out_shape, out_specs, name,
            extra=(), extra_specs=(), after=(), carried=False):
    n_extra = len(extra)

    def body(a_ref, b_ref, *rest):
        epi(product(a_ref, b_ref), rest[:n_extra], rest[n_extra + len(after):])

    return pl.pallas_call(
        body, name=name, grid=grid,
        in_specs=[a_spec, b_spec, *extra_specs, *[pl.BlockSpec(memory_space=pl.ANY)] * len(after)],
        out_specs=out_specs, out_shape=out_shape,
        compiler_params=_params(("arbitrary", "arbitrary") if carried else ("parallel", "parallel")),
    )(a, b, *extra, *after)


def _dot(x, y, dims):
    return lax.dot_general(x, y, dims, preferred_element_type=F32)


def _epi_store(acc, ex, outs):
    outs[0][...] = acc.astype(outs[0].dtype)


def _epi_residual(acc, ex, outs):
    outs[0][...] = acc + ex[0][...]


def _mm_nn(a, b3, *, tm, tn, name, out_dtypes=(F32,), epi=_epi_store, extra=(), total=False):
    m, kdim = a.shape
    g, _, ng = b3.shape
    n = g * ng
    if tn <= ng:
        npg = ng // tn
        b_spec = pl.BlockSpec((None, kdim, tn), lambda j, i: (j // npg, 0, j % npg))

        def product(a_ref, b_ref):
            return _dot(a_ref[...], b_ref[...], NN)
    else:
        gb = tn // ng
        b_spec = pl.BlockSpec((gb, kdim, ng), lambda j, i: (j, 0, 0))

        def product(a_ref, b_ref):
            return jnp.concatenate([_dot(a_ref[...], b_ref[q], NN) for q in range(gb)], axis=1)

    tile = pl.BlockSpec((tm, tn), lambda j, i: (i, j))
    shapes = [jax.ShapeDtypeStruct((m, n), dt) for dt in out_dtypes]
    specs = [tile] * len(shapes)
    if total:
        shapes.append(jax.ShapeDtypeStruct((1, 1), F32))
        specs.append(pl.BlockSpec((1, 1), lambda j, i: (0, 0)))
    single = len(shapes) == 1
    return _matmul(
        a, b3, product=product, grid=(n // tn, m // tm), epi=epi, name=name, carried=total,
        a_spec=pl.BlockSpec((tm, kdim), lambda j, i: (i, 0)), b_spec=b_spec,
        extra=extra, extra_specs=[tile] * len(extra),
        out_shape=shapes[0] if single else shapes, out_specs=specs[0] if single else specs)


def _mm_nt(a, b3, *, tm, tn, name, out_dtype=F32, epi=_epi_store, extra=(), after=()):
    m, kdim = a.shape
    g, n, kg = b3.shape

    def product(a_ref, b_ref):
        acc = _dot(a_ref[:, 0:kg], b_ref[0], NT)
        for q in range(1, g):
            acc = acc + _dot(a_ref[:, q * kg:(q + 1) * kg], b_ref[q], NT)
        return acc

    tile = pl.BlockSpec((tm, tn), lambda j, i: (i, j))
    return _matmul(
        a, b3, product=product, grid=(n // tn, m // tm), epi=epi, name=name,
        a_spec=pl.BlockSpec((tm, kdim), lambda j, i: (i, 0)),
        b_spec=pl.BlockSpec((g, tn, kg), lambda j, i: (0, j, 0)),
        extra=extra, extra_specs=[tile] * len(extra), after=after,
        out_shape=jax.ShapeDtypeStruct((m, n), out_dtype), out_specs=tile)


def _mm_tn(a, b, *, tm, tn, name, groups=1, out_dtype=BF16):
    t, m = a.shape
    _, n = b.shape
    ng = n // groups
    if tn <= ng:
        npg = ng // tn
        out_spec = pl.BlockSpec((None, tm, tn), lambda j, i: (j // npg, i, j % npg))
        epi = _epi_store

        def product(a_ref, b_ref):
            return _dot(a_ref[...], b_ref[...], TN)
    else:
        gb = tn // ng
        out_spec = pl.BlockSpec((gb, tm, ng), lambda j, i: (j, i, 0))

        def product(a_ref, b_ref):
            return [_dot(a_ref[...], b_ref[:, q * ng:(q + 1) * ng], TN) for q in range(gb)]

        def epi(parts, ex, outs):
            for q, part in enumerate(parts):
                outs[0][q] = part.astype(out_dtype)

    return _matmul(
        a, b, product=product, grid=(n // tn, m // tm), epi=epi, name=name,
        a_spec=pl.BlockSpec((t, tm), lambda j, i: (0, i)),
        b_spec=pl.BlockSpec((t, tn), lambda j, i: (0, j)),
        out_shape=jax.ShapeDtypeStruct((groups, m, ng), out_dtype), out_specs=out_spec)


def _rms_fwd(x, g, *, name, tr=256):
    def body(x_ref, g_ref, y_ref, r_ref):
        xv = x_ref[...]
        r = lax.rsqrt(jnp.mean(xv * xv, axis=-1, keepdims=True) + EPS)
        y_ref[...] = (xv * r * g_ref[...]).astype(BF16)
        r_ref[...] = r

    row = pl.BlockSpec((tr, D), lambda i: (i, 0))
    return pl.pallas_call(
        body, name=name, grid=(S // tr,),
        in_specs=[row, pl.BlockSpec((1, D), lambda i: (0, 0))],
        out_specs=[row, pl.BlockSpec((tr, 1), lambda i: (i, 0))],
        out_shape=[jax.ShapeDtypeStruct((S, D), BF16), jax.ShapeDtypeStruct((S, 1), F32)],
        compiler_params=_params(("parallel",)),
    )(x, g)


def _rms_bwd(dy, x, rstd, g, resid, *, name, tr=256):
    def body(dy_ref, x_ref, r_ref, g_ref, res_ref, dx_ref, dxb_ref, dg_ref):
        r = r_ref[...]
        xh = x_ref[...] * r
        dyv = dy_ref[...]
        t = dyv * g_ref[...]
        dx = r * (t - xh * jnp.mean(t * xh, axis=-1, keepdims=True)) + res_ref[...]
        dx_ref[...] = dx
        dxb_ref[...] = dx.astype(BF16)
        part = jnp.sum(dyv * xh, axis=0, keepdims=True)

        @pl.when(pl.program_id(0) == 0)
        def _():
            dg_ref[...] = part

        @pl.when(pl.program_id(0) > 0)
        def _():
            dg_ref[...] += part

    row = pl.BlockSpec((tr, D), lambda i: (i, 0))
    vec = pl.BlockSpec((1, D), lambda i: (0, 0))
    return pl.pallas_call(
        body, name=name, grid=(S // tr,),
        in_specs=[row, row, pl.BlockSpec((tr, 1), lambda i: (i, 0)), vec, row],
        out_specs=[row, row, vec],
        out_shape=[jax.ShapeDtypeStruct((S, D), F32), jax.ShapeDtypeStruct((S, D), BF16),
                   jax.ShapeDtypeStruct((1, D), F32)],
        compiler_params=_params(("arbitrary",)),
    )(dy, x, rstd, g, resid)


def _rope_tables():
    pos = np.arange(S, dtype=np.float32)
    inv = (ROPE_THETA ** (-np.arange(0, HD, 2, dtype=np.float32) / HD)).astype(np.float32)
    ang = pos[:, None] * inv[None, :]
    cos, sin = np.cos(ang), np.sin(ang)
    return (jnp.asarray(np.concatenate([cos, cos], axis=-1), F32),
            jnp.asarray(np.concatenate([-sin, sin], axis=-1), F32))


def _swap_halves(t):
    return pltpu.roll(t, HD // 2, axis=1)


TOK = 256


def _lane_block_spec(d, last=HD):
    return pl.BlockSpec((4, TOK // d, d * last), lambda i: (0, i, 0))


def _to_lane_blocks(dst, head, val, d, scr, dtype):
    w = val.shape[1]
    if d == 1:
        dst[head] = val.astype(dtype)
        return
    scr[...] = val
    for r in range(d):
        dst[head, :, r * w:(r + 1) * w] = scr[pl.ds(r, TOK // d, stride=d), :].astype(dtype)


def _from_lane_blocks(src, head, d, w, scr):
    if d == 1:
        return src[head].astype(F32)
    for r in range(d):
        scr[pl.ds(r, TOK // d, stride=d), :] = src[head, :, r * w:(r + 1) * w].astype(F32)
    return scr[...]


def _qk_prep(proj, gains, cos2, sin2):
    def body(q_ref, k_ref, v_ref, g_ref, c_ref, s_ref, *rest):
        outs, scr = rest[:-1], rest[-1]
        cos, sin = c_ref[...], s_ref[...]
        for which, (src, row_a, row_b) in enumerate(((q_ref, 0, 2), (k_ref, 1, 3), (v_ref, None, None))):
            for h in range(NH):
                y = src[:, h * HD:(h + 1) * HD]
                if row_a is not None:
                    y = y * lax.rsqrt(jnp.mean(y * y, axis=-1, keepdims=True) + EPS)
                    if h < NH_A:
                        y = y * g_ref[row_a:row_a + 1, :]
                        y = y * cos + _swap_halves(y) * sin
                    else:
                        y = y * g_ref[row_b:row_b + 1, :]
                if h < NH_A:
                    gi = h // 4
                    _to_lane_blocks(outs[3 * gi + which], h % 4, y, DILATIONS[gi], scr, BF16)
                else:
                    hb = h - NH_A
                    outs[9 + which][:, hb * HD:(hb + 1) * HD] = y.astype(BF16)

    def blk(c):
        return pl.BlockSpec((TOK, QKV), lambda i: (i, c))
    tab = pl.BlockSpec((TOK, HD), lambda i: (i, 0))
    out_specs, out_shape = [], []
    for d in DILATIONS:
        out_specs += [_lane_block_spec(d)] * 3
        out_shape += [jax.ShapeDtypeStruct((4, S // d, d * HD), BF16)] * 3
    out_specs += [pl.BlockSpec((TOK, D_BR), lambda i: (i, 0))] * 3
    out_shape += [jax.ShapeDtypeStruct((S, D_BR), BF16)] * 3
    outs = pl.pallas_call(
        body, name="qk_prep", grid=(S // TOK,),
        in_specs=[blk(0), blk(1), blk(2), pl.BlockSpec((8, HD), lambda i: (0, 0)), tab, tab],
        out_specs=out_specs, out_shape=out_shape,
        scratch_shapes=[pltpu.VMEM((TOK, HD), F32)],
        compiler_params=_params(("parallel",)),
    )(proj, proj, proj, gains, cos2, sin2)
    return [tuple(outs[3 * gi:3 * gi + 3]) for gi in range(3)], tuple(outs[9:12])


def _qk_prep_bwd(dproj, proj, gains, cos2, sin2, grads_a, grads_b):
    def body(dp_in, q_ref, k_ref, g_ref, c_ref, s_ref, *rest):
        grads, (dp_out, dg_ref, scr) = rest[:12], rest[12:]
        del dp_in
        cos, sin = c_ref[...], s_ref[...]

        def grad_of(which, h):
            if h < NH_A:
                gi = h // 4
                return _from_lane_blocks(grads[3 * gi + which], h % 4, DILATIONS[gi], HD, scr)
            hb = h - NH_A
            return grads[9 + which][:, hb * HD:(hb + 1) * HD]

        dg_rows = []
        for which, (src, base, row_a, row_b) in enumerate(((q_ref, 0, 0, 2), (k_ref, QKV, 1, 3))):
            dg_a = jnp.zeros((1, HD), F32)
            dg_b = jnp.zeros((1, HD), F32)
            for h in range(NH):
                t = src[:, h * HD:(h + 1) * HD]
                dy = grad_of(which, h)
                r = lax.rsqrt(jnp.mean(t * t, axis=-1, keepdims=True) + EPS)
                xh = t * r
                if h < NH_A:
                    dy = dy * cos - _swap_halves(dy) * sin
                    gain = g_ref[row_a:row_a + 1, :]
                    dg_a = dg_a + jnp.sum(dy * xh, axis=0, keepdims=True)
                else:
                    gain = g_ref[row_b:row_b + 1, :]
                    dg_b = dg_b + jnp.sum(dy * xh, axis=0, keepdims=True)
                u = dy * gain
                dx = r * (u - xh * jnp.mean(u * xh, axis=-1, keepdims=True))
                dp_out[:, base + h * HD:base + (h + 1) * HD] = dx.astype(BF16)
            dg_rows += [(row_a, dg_a), (row_b, dg_b)]
        for h in range(NH):
            dp_out[:, 2 * QKV + h * HD:2 * QKV + (h + 1) * HD] = grad_of(2, h).astype(BF16)

        @pl.when(pl.program_id(0) == 0)
        def _():
            dg_ref[...] = jnp.zeros((8, HD), F32)

        for row, val in dg_rows:
            dg_ref[row:row + 1, :] += val

    def blk(c):
        return pl.BlockSpec((TOK, QKV), lambda i: (i, c))
    tab = pl.BlockSpec((TOK, HD), lambda i: (i, 0))
    gain_spec = pl.BlockSpec((8, HD), lambda i: (0, 0))
    grad_specs = [s for d in DILATIONS for s in [_lane_block_spec(d)] * 3]
    grad_specs += [pl.BlockSpec((TOK, D_BR), lambda i: (i, 0))] * 3
    return pl.pallas_call(
        body, name="qk_prep_bwd", grid=(S // TOK,),
        in_specs=[pl.BlockSpec(memory_space=pl.ANY), blk(0), blk(1), gain_spec, tab, tab] + grad_specs,
        out_specs=[pl.BlockSpec((TOK, 3 * QKV), lambda i: (i, 0)), gain_spec],
        out_shape=[jax.ShapeDtypeStruct((S, D_IN), BF16), jax.ShapeDtypeStruct((8, HD), F32)],
        input_output_aliases={0: 0},
        scratch_shapes=[pltpu.VMEM((TOK, HD), F32)],
        compiler_params=_params(("arbitrary",)),
    )(dproj, proj, proj, gains, cos2, sin2, *[g for grp in grads_a for g in grp], *grads_b)


def _mix_fwd(oa, ob, w_pa, w_pb, proj, b_gate, *, tr=256):
    def body(oa_ref, ob_ref, pa_ref, pb_ref, la_ref, lb_ref, ba_ref, bb_ref, mix_ref, ya_ref, yb_ref):
        ya = jnp.concatenate([_dot(oa_ref[...], pa_ref[q], NN) for q in range(N_DEV)], axis=1)
        yb = jnp.concatenate([_dot(ob_ref[...], pb_ref[q], NN) for q in range(N_DEV)], axis=1)
        ga = jax.nn.sigmoid(la_ref[...] + ba_ref[...])
        gb = jax.nn.sigmoid(lb_ref[...] + bb_ref[...])
        mix_ref[...] = (ga * ya + gb * yb).astype(BF16)
        ya_ref[...] = ya.astype(BF16)
        yb_ref[...] = yb.astype(BF16)

    row = pl.BlockSpec((tr, D), lambda i: (i, 0))
    branch = pl.BlockSpec((tr, D_BR), lambda i: (i, 0))
    whole = pl.BlockSpec((N_DEV, D_BR, D // N_DEV), lambda i: (0, 0, 0))
    return pl.pallas_call(
        body, name="mix_fwd", grid=(S // tr,),
        in_specs=[branch, branch, whole, whole,
                  pl.BlockSpec((tr, D), lambda i: (i, 3)), pl.BlockSpec((tr, D), lambda i: (i, 4)),
                  pl.BlockSpec((1, D), lambda i: (0, 0)), pl.BlockSpec((1, D), lambda i: (0, 1))],
        out_specs=[row, row, row], out_shape=[jax.ShapeDtypeStruct((S, D), BF16)] * 3,
        compiler_params=_params(("parallel",)),
    )(oa, ob, w_pa, w_pb, proj, proj, b_gate, b_gate)


def _gate_bwd(branch, dmixed, proj, b_gate, y, dproj, *, tr=256):
    aliased = dproj is not None

    def body(dm_ref, l_ref, b_ref, y_ref, *rest):
        dy_ref, dp_ref, db_ref = rest[-3:]
        g = jax.nn.sigmoid(l_ref[...] + b_ref[...])
        dm = dm_ref[...]
        dy_ref[...] = (dm * g).astype(BF16)
        dl = dm * y_ref[...].astype(F32) * g * (1.0 - g)
        dp_ref[...] = dl.astype(BF16)
        part = jnp.sum(dl, axis=0, keepdims=True)

        @pl.when(pl.program_id(0) == 0)
        def _():
            db_ref[...] = part

        @pl.when(pl.program_id(0) > 0)
        def _():
            db_ref[...] += part

    row = pl.BlockSpec((tr, D), lambda i: (i, 0))
    col = pl.BlockSpec((tr, D), lambda i: (i, 3 + branch))
    vec = pl.BlockSpec((1, D), lambda i: (0, 0))
    return pl.pallas_call(
        body, name=f"gate_bwd_{branch}", grid=(S // tr,),
        in_specs=[row, col, pl.BlockSpec((1, D), lambda i: (0, branch)), row]
        + ([pl.BlockSpec(memory_space=pl.ANY)] if aliased else []),
        out_specs=[row, col, vec],
        out_shape=[jax.ShapeDtypeStruct((S, D), BF16), jax.ShapeDtypeStruct((S, D_IN), BF16),
                   jax.ShapeDtypeStruct((1, D), F32)],
        input_output_aliases={4: 1} if aliased else {},
        compiler_params=_params(("arbitrary",)),
    )(dmixed, proj, b_gate, y, *([dproj] if aliased else []))


def _band_blocks(m_len):
    wk = min(m_len, QB + 2 * QB)
    return [(qb * QB, min(max(qb * QB - QB, 0), m_len - wk), wk) for qb in range(m_len // QB)]


def _band_scores(q, kw, q0, k0, wk):
    s = _dot(q, kw, NT) * SCALE
    qpos = q0 + lax.broadcasted_iota(jnp.int32, (QB, 1), 0)
    kpos = k0 + lax.broadcasted_iota(jnp.int32, (1, wk), 1)
    return jnp.where(jnp.abs(kpos - qpos) <= HALF_A, s, NEG)


def _attn_a_fwd(q, k, v, gi):
    d = DILATIONS[gi]
    m_len = S // d

    def body(q_ref, k_ref, v_ref, o_ref, lse_ref):
        for r in range(d):
            lanes = slice(r * HD, (r + 1) * HD)
            for q0, k0, wk in _band_blocks(m_len):
                s = _band_scores(q_ref[q0:q0 + QB, lanes], k_ref[k0:k0 + wk, lanes], q0, k0, wk)
                m = jnp.max(s, axis=-1, keepdims=True)
                p = jnp.exp(s - m)
                l = jnp.sum(p, axis=-1, keepdims=True)
                o_ref[q0:q0 + QB, lanes] = _dot(p.astype(BF16), v_ref[k0:k0 + wk, lanes], NN) / l
                lse_ref[q0:q0 + QB, r:r + 1] = m + jnp.log(l)

    head = pl.BlockSpec((None, m_len, d * HD), lambda h: (h, 0, 0))
    stat = pl.BlockSpec((None, m_len, d), lambda h: (h, 0, 0))
    return pl.pallas_call(
        body, name=f"attn_a_fwd_{gi}", grid=(4,),
        in_specs=[head, head, head], out_specs=[head, stat],
        out_shape=[jax.ShapeDtypeStruct((4, m_len, d * HD), F32), jax.ShapeDtypeStruct((4, m_len, d), F32)],
        compiler_params=_params(("parallel",)),
    )(q, k, v)


def _combine_a(os, lses):
    def body(o0, o1, o2, l0, l1, l2, oa_ref, lse_ref, scr, scr1):
        for h in range(4):
            o = [_from_lane_blocks(ref, h, d, HD, scr) for ref, d in zip((o0, o1, o2), DILATIONS)]
            a, b, c = (_from_lane_blocks(ref, h, d, 1, scr1) for ref, d in zip((l0, l1, l2), DILATIONS))
            m = jnp.maximum(jnp.maximum(a, b), c)
            wa, wb, wc = jnp.exp(a - m), jnp.exp(b - m), jnp.exp(c - m)
            tot = wa + wb + wc
            oa_ref[:, h * HD:(h + 1) * HD] = ((wa * o[0] + wb * o[1] + wc * o[2]) / tot).astype(BF16)
            lse_ref[h] = m + jnp.log(tot)

    return pl.pallas_call(
        body, name="combine_a", grid=(S // TOK,),
        in_specs=[_lane_block_spec(d) for d in DILATIONS] + [_lane_block_spec(d, 1) for d in DILATIONS],
        out_specs=[pl.BlockSpec((TOK, D_BR), lambda i: (i, 0)), pl.BlockSpec((4, TOK, 1), lambda i: (0, i, 0))],
        out_shape=[jax.ShapeDtypeStruct((S, D_BR), BF16), jax.ShapeDtypeStruct((4, S, 1), F32)],
        scratch_shapes=[pltpu.VMEM((TOK, HD), F32), pltpu.VMEM((TOK, 1), F32)],
        compiler_params=_params(("parallel",)),
    )(*os, *lses)


def _attn_a_bwd_prep(doa, oa, lse):
    def body(do_ref, o_ref, l_ref, *rest):
        outs, (scr, scr1) = rest[:9], rest[9:]
        for h in range(4):
            do = do_ref[:, h * HD:(h + 1) * HD]
            dsum = jnp.sum(do * o_ref[:, h * HD:(h + 1) * HD].astype(F32), axis=-1, keepdims=True)
            for gi, d in enumerate(DILATIONS):
                _to_lane_blocks(outs[3 * gi], h, do, d, scr, BF16)
                _to_lane_blocks(outs[3 * gi + 1], h, l_ref[h], d, scr1, F32)
                _to_lane_blocks(outs[3 * gi + 2], h, dsum, d, scr1, F32)

    row = pl.BlockSpec((TOK, D_BR), lambda i: (i, 0))
    out_specs, out_shape = [], []
    for d in DILATIONS:
        out_specs += [_lane_block_spec(d), _lane_block_spec(d, 1), _lane_block_spec(d, 1)]
        out_shape += [jax.ShapeDtypeStruct((4, S // d, d * HD), BF16)] + [jax.ShapeDtypeStruct((4, S // d, d), F32)] * 2
    outs = pl.pallas_call(
        body, name="attn_a_bwd_prep", grid=(S // TOK,),
        in_specs=[row, row, pl.BlockSpec((4, TOK, 1), lambda i: (0, i, 0))],
        out_specs=out_specs, out_shape=out_shape,
        scratch_shapes=[pltpu.VMEM((TOK, HD), F32), pltpu.VMEM((TOK, 1), F32)],
        compiler_params=_params(("parallel",)),
    )(doa, oa, lse)
    return [tuple(outs[3 * gi:3 * gi + 3]) for gi in range(3)]


def _attn_a_bwd(q, k, v, do, lse, dsum, gi):
    d = DILATIONS[gi]
    m_len = S // d

    def body(q_ref, k_ref, v_ref, do_ref, lse_ref, dsum_ref, dq_ref, dk_ref, dv_ref):
        dk_ref[...] = jnp.zeros((m_len, d * HD), F32)
        dv_ref[...] = jnp.zeros((m_len, d * HD), F32)
        for r in range(d):
            lanes = slice(r * HD, (r + 1) * HD)
            for q0, k0, wk in _band_blocks(m_len):
                rows, keys = slice(q0, q0 + QB), slice(k0, k0 + wk)
                qv, kw, vw, dov = q_ref[rows, lanes], k_ref[keys, lanes], v_ref[keys, lanes], do_ref[rows, lanes]
                p = jnp.exp(_band_scores(qv, kw, q0, k0, wk) - lse_ref[rows, r:r + 1])
                ds = (p * (_dot(dov, vw, NT) - dsum_ref[rows, r:r + 1]) * SCALE).astype(BF16)
                dq_ref[rows, lanes] = _dot(ds, kw, NN)
                dk_ref[keys, lanes] += _dot(ds, qv, TN)
                dv_ref[keys, lanes] += _dot(p.astype(BF16), dov, TN)

    head = pl.BlockSpec((None, m_len, d * HD), lambda h: (h, 0, 0))
    stat = pl.BlockSpec((None, m_len, d), lambda h: (h, 0, 0))
    shape = jax.ShapeDtypeStruct((4, m_len, d * HD), F32)
    return pl.pallas_call(
        body, name=f"attn_a_bwd_{gi}", grid=(4,),
        in_specs=[head, head, head, head, stat, stat], out_specs=[head, head, head],
        out_shape=[shape, shape, shape],
        compiler_params=_params(("parallel",)),
    )(q, k, v, do, lse, dsum)


KEYS_B = WIN_R * GRID_W
N_OFF = WIN_R


def _bias_constants():
    q = np.arange(GRID_W)[:, None]
    kc = np.arange(GRID_W)[None, :]
    dc = np.clip(kc - q, -(WIN_C - 1), WIN_C - 1) + (WIN_C - 1)
    expand = np.zeros((HD, GRID_W * GRID_W), np.float32)
    expand[dc.reshape(-1), np.arange(GRID_W * GRID_W)] = 1.0
    cs = np.clip(q - WIN_C // 2, 0, GRID_W - WIN_C)
    keep = ((kc >= cs) & (kc < cs + WIN_C)).reshape(1, -1).astype(np.float32)
    sel = np.zeros((64, 4 * N_OFF * WIN_R), np.float32)
    for h in range(4):
        for off in range(N_OFF):
            for j in range(WIN_R):
                sel[h * (2 * WIN_R - 1) + off + j, (h * N_OFF + off) * WIN_R + j] = 1.0
    return jnp.asarray(expand), jnp.asarray(keep), jnp.asarray(sel)


def _bias_expand(rpb_pad, expand, keep, sel):
    def body(r_ref, e_ref, k_ref, s_ref, o_ref):
        t = lax.dot_general(r_ref[...], e_ref[...], NN, precision=lax.Precision.HIGHEST,
                            preferred_element_type=F32)
        rows = lax.dot_general(s_ref[...], t, TN, precision=lax.Precision.HIGHEST,
                               preferred_element_type=F32)
        o_ref[...] = jnp.where(k_ref[...] > 0.5, rows, NEG)

    return pl.pallas_call(
        body, name="bias_expand",
        out_shape=jax.ShapeDtypeStruct((4 * N_OFF * WIN_R, GRID_W * GRID_W), F32),
        compiler_params=pltpu.CompilerParams(vmem_limit_bytes=VMEM_LIMIT),
    )(rpb_pad, expand, keep, sel)


def _bias_reduce(dbias_rows, expand, sel):
    def body(x_ref, e_ref, s_ref, o_ref):
        z = lax.dot_general(x_ref[...], e_ref[...], NT, precision=lax.Precision.HIGHEST,
                            preferred_element_type=F32)
        o_ref[...] = lax.dot_general(s_ref[...], z, NN, precision=lax.Precision.HIGHEST,
                                     preferred_element_type=F32)

    return pl.pallas_call(
        body, name="bias_reduce", out_shape=jax.ShapeDtypeStruct((64, HD), F32),
        compiler_params=pltpu.CompilerParams(vmem_limit_bytes=VMEM_LIMIT),
    )(dbias_rows, expand, sel)


def _rows_to_tab(rows):
    t = rows.reshape(4, N_OFF, WIN_R, GRID_W, GRID_W)
    return t.transpose(0, 1, 3, 2, 4).reshape(4, N_OFF, GRID_W, KEYS_B)


def _tab_to_rows(tab):
    t = tab.reshape(4, N_OFF, GRID_W, WIN_R, GRID_W)
    return t.transpose(0, 1, 3, 2, 4).reshape(4 * N_OFF * WIN_R, GRID_W * GRID_W)


def _row_window(r):
    r0 = jnp.clip(r - WIN_R // 2, 0, ROWS - WIN_R)
    off = r0 + (WIN_R - 1) - r
    return pl.multiple_of(r * GRID_W, GRID_W), pl.multiple_of(r0 * GRID_W, GRID_W), off


def _attn_b_fwd(qn, kn, vb, bias_tab):
    def body(q_ref, k_ref, v_ref, b_ref, o_ref, lse_ref):
        def row(r, carry):
            qs, ks, off = _row_window(r)
            q = q_ref[pl.ds(qs, GRID_W), :]
            s = lax.dot_general(q, k_ref[pl.ds(ks, KEYS_B), :], NT, preferred_element_type=F32) * SCALE
            s = s + b_ref[off]
            m = jnp.max(s, axis=-1, keepdims=True)
            p = jnp.exp(s - m)
            l = jnp.sum(p, axis=-1, keepdims=True)
            o = lax.dot_general(p.astype(BF16), v_ref[pl.ds(ks, KEYS_B), :], NN, preferred_element_type=F32)
            o_ref[pl.ds(qs, GRID_W), :] = (o / l).astype(BF16)
            lse_ref[pl.ds(qs, GRID_W), :] = m + jnp.log(l)
            return carry

        lax.fori_loop(0, ROWS, row, 0)

    full = pl.BlockSpec((S, HD), lambda h: (0, h))
    return pl.pallas_call(
        body, name="attn_b_fwd", grid=(4,),
        in_specs=[full, full, full, pl.BlockSpec((None, N_OFF, GRID_W, KEYS_B), lambda h: (h, 0, 0, 0))],
        out_specs=[pl.BlockSpec((S, HD), lambda h: (0, h)), pl.BlockSpec((None, S, 1), lambda h: (h, 0, 0))],
        out_shape=[jax.ShapeDtypeStruct((S, D_BR), BF16), jax.ShapeDtypeStruct((4, S, 1), F32)],
        compiler_params=_params(("parallel",)),
    )(qn, kn, vb, bias_tab)


def _attn_b_bwd(qn, kn, vb, bias_tab, ob, dob, lse):
    def body(q_ref, k_ref, v_ref, b_ref, o_ref, do_ref, lse_ref, dq_ref, dk_ref, dv_ref, db_ref):
        dk_ref[...] = jnp.zeros((S, HD), F32)
        dv_ref[...] = jnp.zeros((S, HD), F32)
        db_ref[...] = jnp.zeros((N_OFF, GRID_W, KEYS_B), F32)

        def row(r, carry):
            qs, ks, off = _row_window(r)
            rows = pl.ds(qs, GRID_W)
            keys = pl.ds(ks, KEYS_B)
            q = q_ref[rows, :]
            kw = k_ref[keys, :]
            s = lax.dot_general(q, kw, NT, preferred_element_type=F32) * SCALE + b_ref[off]
            p = jnp.exp(s - lse_ref[rows, :])
            do = do_ref[rows, :]
            dobf = do.astype(BF16)
            dsum = jnp.sum(do * o_ref[rows, :].astype(F32), axis=-1, keepdims=True)
            dp = lax.dot_general(dobf, v_ref[keys, :], NT, preferred_element_type=F32)
            ds = p * (dp - dsum)
            db_ref[off] += ds
            dsb = (ds * SCALE).astype(BF16)
            dq_ref[rows, :] = lax.dot_general(dsb, kw, NN, preferred_element_type=F32)
            dk_ref[keys, :] += lax.dot_general(dsb, q, TN, preferred_element_type=F32)
            dv_ref[keys, :] += lax.dot_general(p.astype(BF16), dobf, TN, preferred_element_type=F32)
            return carry

        lax.fori_loop(0, ROWS, row, 0)

    full = pl.BlockSpec((S, HD), lambda h: (0, h))
    slot = pl.BlockSpec((S, HD), lambda h: (0, h))
    tab = pl.BlockSpec((None, N_OFF, GRID_W, KEYS_B), lambda h: (h, 0, 0, 0))
    shape = jax.ShapeDtypeStruct((S, D_BR), F32)
    return pl.pallas_call(
        body, name="attn_b_bwd", grid=(4,),
        in_specs=[full, full, full, tab, slot, slot, pl.BlockSpec((None, S, 1), lambda h: (h, 0, 0))],
        out_specs=[slot, slot, slot, tab],
        out_shape=[shape, shape, shape, jax.ShapeDtypeStruct((4, N_OFF, GRID_W, KEYS_B), F32)],
        compiler_params=_params(("parallel",)),
    )(qn, kn, vb, bias_tab, ob, dob, lse)


def _epi_relu_sq(acc, ex, outs):
    u = jnp.maximum(acc, 0.0)
    outs[0][...] = u.astype(BF16)
    outs[1][...] = (u * u).astype(BF16)


def _epi_relu_sq_bwd(acc, ex, outs):
    outs[0][...] = (acc * (2.0 * ex[0][...].astype(F32))).astype(BF16)


def _epi_loss_head(acc, ex, outs):
    e = acc + ex[0][...] - ex[1][...]
    dy = e * (1.0 / D)
    outs[0][...] = dy
    outs[1][...] = dy.astype(BF16)
    part = (0.5 / D) * jnp.sum(jnp.sum(e * e, axis=-1, keepdims=True), axis=0, keepdims=True)
    first = (pl.program_id(0) == 0) & (pl.program_id(1) == 0)

    @pl.when(first)
    def _():
        outs[2][...] = part

    @pl.when(jnp.logical_not(first))
    def _():
        outs[2][...] += part


def _local_step(x, target, norm_mix, b_gate, gains, rpb_pad, norm_ffn,
                w_in, w_pa, w_pb, w_out, w_up, w_down, weight_grads):
    cos2, sin2 = _rope_tables()
    expand, keep, sel = _bias_constants()
    w_out3, w_down3 = w_out[None], w_down[None]

    xn, rstd1 = _rms_fwd(x, norm_mix, name="rms_mix")
    proj = _mm_nn(xn, w_in, tm=1024, tn=1280, name="proj")
    qkv_a, qkv_b = _qk_prep(proj, gains, cos2, sin2)
    fwd_a = [_attn_a_fwd(*qkv_a[gi], gi) for gi in range(3)]
    oa, lse_a = _combine_a([o for o, _ in fwd_a], [l for _, l in fwd_a])
    bias_tab = _rows_to_tab(_bias_expand(rpb_pad, expand, keep, sel))
    ob, lse_b = _attn_b_fwd(*qkv_b, bias_tab)
    mixed, ya, yb = _mix_fwd(oa, ob, w_pa, w_pb, proj, b_gate)
    h1 = _mm_nn(mixed, w_out3, tm=1024, tn=1024, name="out_proj", epi=_epi_residual, extra=(x,))
    hn, rstd2 = _rms_fwd(h1, norm_ffn, name="rms_ffn")
    u, usq = _mm_nn(hn, w_up, tm=1024, tn=1024, name="ffn_up", epi=_epi_relu_sq,
                    out_dtypes=(BF16, BF16))
    dy, dyb, loss = _mm_nn(usq, w_down3, tm=512, tn=512, name="ffn_down", epi=_epi_loss_head,
                           extra=(h1, target), out_dtypes=(F32, BF16), total=True)

    sent = weight_grads("w_down", {5: (usq, dyb)})
    du = _mm_nt(dyb, w_down3, tm=1024, tn=1024, name="ffn_down_bwd", out_dtype=BF16,
                epi=_epi_relu_sq_bwd, extra=(u,), after=sent)
    sent = weight_grads("w_up", {4: (hn, du)})
    dhn = _mm_nt(du, w_up, tm=512, tn=512, name="ffn_up_bwd", after=sent)
    dh1, dh1b, g_norm_ffn = _rms_bwd(dhn, h1, rstd2, norm_ffn, dy, name="rms_ffn_bwd")

    dmixed = _mm_nt(dh1b, w_out3, tm=1024, tn=1024, name="out_proj_bwd")
    dya, dproj, g_ba = _gate_bwd(0, dmixed, proj, b_gate, ya, None)
    dyb2, dproj, g_bb = _gate_bwd(1, dmixed, proj, b_gate, yb, dproj)
    sent = weight_grads("w_mix", {3: (mixed, dh1b), 1: (oa, dya), 2: (ob, dyb2)})
    doa = _mm_nt(dya, w_pa, tm=1024, tn=D_BR, name="proj_a_bwd", after=sent)
    dob = _mm_nt(dyb2, w_pb, tm=1024, tn=D_BR, name="proj_b_bwd")
    prep = _attn_a_bwd_prep(doa, oa, lse_a)
    grads_a = [_attn_a_bwd(*qkv_a[gi], *prep[gi], gi) for gi in range(3)]
    dqb, dkb, dvb, dbias = _attn_b_bwd(*qkv_b, bias_tab, ob, dob, lse_b)
    g_rpb = _bias_reduce(_tab_to_rows(dbias), expand, sel)
    dproj, g_gains = _qk_prep_bwd(dproj, proj, gains, cos2, sin2, grads_a, (dqb, dkb, dvb))
    sent = weight_grads("w_in", {0: (xn, dproj)})
    dxn = _mm_nt(dproj, w_in, tm=256, tn=512, name="proj_bwd", after=sent)
    grad_x, _, g_norm_mix = _rms_bwd(dxn, x, rstd1, norm_mix, dh1, name="rms_mix_bwd")

    small = (g_norm_mix, g_ba, g_bb, g_gains, g_rpb, g_norm_ffn)
    return loss, grad_x, small


def _cast_bf16(w, *, tr=256):
    rows, cols = w.shape
    tr = min(tr, rows)

    def body(w_ref, o_ref):
        o_ref[...] = w_ref[...].astype(BF16)

    spec = pl.BlockSpec((tr, cols), lambda i: (i, 0))
    return pl.pallas_call(
        body, name=f"cast_{rows}x{cols}", grid=(rows // tr,), in_specs=[spec], out_specs=spec,
        out_shape=jax.ShapeDtypeStruct((rows, cols), BF16), compiler_params=_params(("parallel",)),
    )(w)


def _me_and_peers():
    x, y, c = lax.axis_index("x"), lax.axis_index("y"), lax.axis_index("c")
    me = 4 * x + 2 * y + c
    peers = []
    for k in range(1, N_DEV):
        px = 1 - x if k & 4 else x
        py = 1 - y if k & 2 else y
        pc = 1 - c if k & 1 else c
        peers.append(((px, py, pc), 4 * px + 2 * py + pc))
    return me, peers


def _gather_on_sequencer(shards, name):
    n = len(shards)
    hbm = pltpu.MemorySpace.HBM
    ins = [jax.new_ref(s, memory_space=hbm) for s in shards]
    outs = [jax.empty_ref(jax.ShapeDtypeStruct((N_DEV,) + s.shape, s.dtype), memory_space=hbm) for s in shards]

    @pl.kernel(mesh=plsc.ScalarSubcoreMesh(axis_name="seq", num_cores=1), name=name,
               scratch_types=(pltpu.SemaphoreType.DMA((n, N_DEV - 1)), pltpu.SemaphoreType.DMA((n, N_DEV - 1)),
                              pltpu.SemaphoreType.DMA((n,))),
               compiler_params=pltpu.CompilerParams(collective_id=0))
    def launch(send, recv, lsem):
        x, y, c = lax.axis_index("x"), lax.axis_index("y"), lax.axis_index("c")
        me, sibling = (x, y, c), (x, y, 1 - c)
        chips = [(1 - x, y), (x, 1 - y), (1 - x, 1 - y)]
        barrier = pltpu.get_barrier_semaphore()
        for peer in [sibling] + [(*chip, c) for chip in chips]:
            pl.semaphore_signal(barrier, inc=1, device_id=peer, device_id_type=MESH)
        pl.semaphore_wait(barrier, 4)

        def copy(w, k, block, to, src=None):
            px, py, pc = block
            dst = outs[w].at[4 * px + 2 * py + pc]
            return pltpu.make_async_remote_copy(dst if src is None else src, dst, send.at[w, k], recv.at[w, k],
                                                device_id=to, device_id_type=MESH)

        local = [pltpu.make_async_copy(ins[w], outs[w].at[4 * x + 2 * y + c], lsem.at[w]) for w in range(n)]
        for cp in local:
            cp.start()
        first = []
        for w in range(n):
            first += [copy(w, 1 + j, me, (*chip, c), src=ins[w]) for j, chip in enumerate(chips)]
            first.append(copy(w, 0, me, sibling, src=ins[w]))
        for cp in first:
            cp.start()
        passed = []
        for w in range(n):
            for j, chip in enumerate(chips):
                copy(w, 1 + j, (*chip, c), me).wait_recv()
                cp = copy(w, 4 + j, (*chip, c), sibling)
                cp.start()
                passed.append(cp)
        for w in range(n):
            copy(w, 0, sibling, me).wait_recv()
            for j, chip in enumerate(chips):
                copy(w, 4 + j, (*chip, 1 - c), me).wait_recv()
        for cp in first + passed:
            cp.wait_send()
        for cp in local:
            cp.wait()

    launch()
    return [o[...] for o in outs]


N_CHIP = 4
CHIPS = ((0, 0), (0, 1), (1, 0), (1, 1))


def _sequencer(name, n_sems, collective_id):
    return functools.partial(
        pl.kernel, mesh=plsc.ScalarSubcoreMesh(axis_name="seq", num_cores=1), name=name,
        scratch_types=tuple(pltpu.SemaphoreType.DMA(s) for s in n_sems),
        compiler_params=pltpu.CompilerParams(collective_id=collective_id))


def _handshake(peers):
    barrier = pltpu.get_barrier_semaphore()
    for peer in peers:
        pl.semaphore_signal(barrier, inc=1, device_id=peer, device_id_type=MESH)
    pl.semaphore_wait(barrier, len(peers))


def _chip_exchange_on_sequencer(parts, name):
    n = len(parts)
    hbm = pltpu.MemorySpace.HBM
    ins = [jax.new_ref(p, memory_space=hbm) for p in parts]
    outs = [jax.empty_ref(jax.ShapeDtypeStruct(p.shape, p.dtype), memory_space=hbm) for p in parts]

    @_sequencer(name, ((n, 3), (n, 3), (n,)), 2)
    def launch(send, recv, lsem):
        x, y, c = lax.axis_index("x"), lax.axis_index("y"), lax.axis_index("c")
        mine = 2 * x + y
        chips = [(1 - x, y), (x, 1 - y), (1 - x, 1 - y)]
        _handshake([(*chip, c) for chip in chips])
        local = [pltpu.make_async_copy(ins[w].at[mine], outs[w].at[mine], lsem.at[w]) for w in range(n)]
        for cp in local:
            cp.start()
        sends = []
        for w in range(n):
            for j, (px, py) in enumerate(chips):
                cp = pltpu.make_async_remote_copy(ins[w].at[2 * px + py], outs[w].at[mine],
                                                  send.at[w, j], recv.at[w, j],
                                                  device_id=(px, py, c), device_id_type=MESH)
                cp.start()
                sends.append(cp)
        for w in range(n):
            for j, (px, py) in enumerate(chips):
                pltpu.make_async_remote_copy(ins[w].at[mine], outs[w].at[2 * px + py],
                                             send.at[w, j], recv.at[w, j],
                                             device_id=(px, py, c), device_id_type=MESH).wait_recv()
        for cp in sends:
            cp.wait_send()
        for cp in local:
            cp.wait()

    launch()
    return [o[...] for o in outs]


GRAD_TILES = (dict(blocks_on="cols", tm=512, tn=1280), dict(blocks_on="cols", tm=512, tn=256),
              dict(blocks_on="cols", tm=512, tn=256), dict(blocks_on="rows", tm=256, tn=1024),
              dict(blocks_on="cols", tm=1024, tn=1024), dict(blocks_on="rows", tm=1024, tn=1024))


def _mm_tn_pair(a, b, *, blocks_on, tm, tn, name):
    t_len, m = a.shape
    n = b.shape[1]
    if blocks_on == "rows":
        rows, cols, inner = m // N_DEV, n, n // tn
        assert tm == rows
        a_spec = pl.BlockSpec((t_len, tm), lambda p, t, blk: (0, blk[p]))
        b_spec = pl.BlockSpec((t_len, tn), lambda p, t, blk: (0, t))
        out_spec = pl.BlockSpec((None, tm, tn), lambda p, t, blk: (
            jnp.maximum(p - N_CHIP, 0), 0, jnp.where(p < N_CHIP, 0, t)))
    else:
        rows, cols, inner = m, n // N_DEV, m // tm
        assert tn == cols
        a_spec = pl.BlockSpec((t_len, tm), lambda p, t, blk: (0, t))
        b_spec = pl.BlockSpec((t_len, tn), lambda p, t, blk: (0, blk[p]))
        out_spec = pl.BlockSpec((None, tm, tn), lambda p, t, blk: (
            jnp.maximum(p - N_CHIP, 0), jnp.where(p < N_CHIP, 0, t), 0))

    def body(blk_ref, a_ref, b_ref, o_ref, land, stage, send_sem, recv_sem):
        del blk_ref
        p, t = pl.program_id(0), pl.program_id(1)
        step = p * inner + t
        x, y, c = lax.axis_index("x"), lax.axis_index("y"), lax.axis_index("c")
        tile = _dot(a_ref[...], b_ref[...], TN)

        def to_sibling(slot, chip, piece):
            return pltpu.make_async_remote_copy(stage.at[slot], land.at[chip, piece], send_sem.at[slot],
                                                recv_sem.at[chip, piece],
                                                device_id=(x, y, 1 - c), device_id_type=MESH)

        @pl.when(p < N_CHIP)
        def _():
            slot = step % 2

            @pl.when(step >= 2)
            def _():
                to_sibling(slot, 0, 0).wait_send()

            stage[slot] = tile.astype(BF16)
            to_sibling(slot, p, t).start()

        @pl.when(step == N_CHIP * inner)
        def _():
            for slot in range(min(2, N_CHIP * inner)):
                to_sibling(slot, 0, 0).wait_send()

        @pl.when(p >= N_CHIP)
        def _():
            chip = p - N_CHIP
            to_sibling(0, chip, t).wait_recv()
            o_ref[...] = (tile + land[chip, t].astype(F32)).astype(BF16)

    c = lax.axis_index("c")
    order = jnp.stack([2 * ch + 1 - c for ch in range(N_CHIP)] + [2 * ch + c for ch in range(N_CHIP)])
    return pl.pallas_call(
        body, name=name,
        grid_spec=pltpu.PrefetchScalarGridSpec(
            num_scalar_prefetch=1, grid=(N_DEV, inner), in_specs=[a_spec, b_spec], out_specs=out_spec,
            scratch_shapes=[pltpu.VMEM((N_CHIP, inner, tm, tn), BF16), pltpu.VMEM((2, tm, tn), BF16),
                            pltpu.SemaphoreType.DMA((2,)), pltpu.SemaphoreType.DMA((N_CHIP, inner))]),
        out_shape=jax.ShapeDtypeStruct((N_CHIP, rows, cols), BF16),
        compiler_params=_params(("arbitrary", "arbitrary")),
    )(order.astype(jnp.int32), a, b)


def _adamw_math(g, w, m, v):
    m2 = B1 * m + (1.0 - B1) * g
    v2 = B2 * v + (1.0 - B2) * (g * g)
    delta = -LR * ((m2 / BC1) / (jnp.sqrt(v2 / BC2) + AEPS) + WD * w)
    return delta, m2, v2


def _adamw(parts, w, m, v, *, name, after=(), tr=256):
    rows, cols = w.shape

    def body(p_ref, w_ref, m_ref, v_ref, *rest):
        g_ref, d_ref, mo_ref, vo_ref = rest[len(after):]
        g = p_ref[0].astype(F32)
        for b in range(1, N_CHIP):
            g = g + p_ref[b].astype(F32)
        delta, m2, v2 = _adamw_math(g, w_ref[...], m_ref[...], v_ref[...])
        g_ref[...] = g
        d_ref[...] = delta
        mo_ref[...] = m2
        vo_ref[...] = v2

    spec = pl.BlockSpec((tr, cols), lambda i: (i, 0))
    shape = jax.ShapeDtypeStruct((rows, cols), F32)
    return pl.pallas_call(
        body, name=name, grid=(rows // tr,),
        in_specs=[pl.BlockSpec((N_CHIP, tr, cols), lambda i: (0, i, 0)), spec, spec, spec]
        + [pl.BlockSpec(memory_space=pl.ANY)] * len(after),
        out_specs=[spec] * 4, out_shape=[shape] * 4,
        compiler_params=_params(("parallel",)),
    )(parts, w, m, v, *after)


def _small_update(part, w, m, v):
    rows = part.shape[0]

    def body(p_ref, w_ref, m_ref, v_ref, g_ref, d_ref, mo_ref, vo_ref, buf, send, recv):
        me, peers = _me_and_peers()
        buf[me] = p_ref[...]
        sends = []
        for k, (dev, _) in enumerate(peers):
            cp = pltpu.make_async_remote_copy(p_ref, buf.at[me], send.at[k], recv.at[k],
                                              device_id=dev, device_id_type=MESH)
            cp.start()
            sends.append(cp)
        for k, (dev, idx) in enumerate(peers):
            pltpu.make_async_remote_copy(p_ref, buf.at[idx], send.at[k], recv.at[k],
                                         device_id=dev, device_id_type=MESH).wait_recv()
        for cp in sends:
            cp.wait_send()
        g = buf[0]
        for b in range(1, N_DEV):
            g = g + buf[b]
        delta, m2, v2 = _adamw_math(g, w_ref[...], m_ref[...], v_ref[...])
        g_ref[...] = g
        d_ref[...] = delta
        mo_ref[...] = m2
        vo_ref[...] = v2

    vm = pl.BlockSpec(memory_space=pltpu.VMEM)
    shape = jax.ShapeDtypeStruct((rows, HD), F32)
    return pl.pallas_call(
        body, name="small_params_update",
        in_specs=[vm] * 4, out_specs=[vm] * 4, out_shape=[shape] * 4,
        scratch_shapes=[pltpu.VMEM((N_DEV, rows, HD), F32),
                        pltpu.SemaphoreType.DMA((N_DEV - 1,)), pltpu.SemaphoreType.DMA((N_DEV - 1,))],
    )(part, w, m, v)


def _pack_small(norm_mix, b_gate, qa, ka, qb, kb, rpb, norm_ffn):
    gains = jnp.concatenate([qa, ka, qb, kb, jnp.zeros((4, HD), F32)], axis=0)
    rpb_pad = jnp.pad(rpb.reshape(4 * (2 * WIN_R - 1), 2 * WIN_C - 1), ((0, 4), (0, HD - (2 * WIN_C - 1))))
    return jnp.concatenate([norm_mix.reshape(16, HD), b_gate.reshape(32, HD), gains, rpb_pad,
                            norm_ffn.reshape(16, HD), jnp.zeros((8, HD), F32)], axis=0)


LOSS_ROW = 136


def _unpack_small(p):
    norm_mix = p[0:16].reshape(1, D)
    b_gate = p[16:48].reshape(1, 2 * D)
    qa, ka, qb, kb = (p[48 + i:49 + i] for i in range(4))
    rpb = p[56:116, :2 * WIN_C - 1].reshape(1, 4, 2 * WIN_R - 1, 2 * WIN_C - 1)
    norm_ffn = p[120:136].reshape(1, D)
    return norm_mix, b_gate, qa, ka, qb, kb, rpb, norm_ffn


def kernel(x, norm_mix, w_in, b_gate, q_norm_a, k_norm_a, q_norm_b, k_norm_b, rpb_b, w_proj_a, w_proj_b, w_out, norm_ffn, w_up, w_down, loss_target, m_norm_mix, m_w_in, m_b_gate, m_q_norm_a, m_k_norm_a, m_q_norm_b, m_k_norm_b, m_rpb_b, m_w_proj_a, m_w_proj_b, m_w_out, m_norm_ffn, m_w_up, m_w_down, v_norm_mix, v_w_in, v_b_gate, v_q_norm_a, v_k_norm_a, v_q_norm_b, v_k_norm_b, v_rpb_b, v_w_proj_a, v_w_proj_b, v_w_out, v_norm_ffn, v_w_up, v_w_down):
    big_w = (w_in[0], w_proj_a[0], w_proj_b[0], w_out[0], w_up[0], w_down[0])
    big_m = (m_w_in[0], m_w_proj_a[0], m_w_proj_b[0], m_w_out[0], m_w_up[0], m_w_down[0])
    big_v = (v_w_in[0], v_w_proj_a[0], v_w_proj_b[0], v_w_out[0], v_w_up[0], v_w_down[0])
    names = ("w_in", "w_proj_a", "w_proj_b", "w_out", "w_up", "w_down")

    shards = [_cast_bf16(w) for w in big_w]
    g_in, = _gather_on_sequencer(shards[0:1], "gather_w_in")
    g_pa, g_pb, g_out = _gather_on_sequencer(shards[1:4], "gather_w_mix")
    g_up, = _gather_on_sequencer(shards[4:5], "gather_w_up")
    g_down, = _gather_on_sequencer(shards[5:6], "gather_w_down")
    small_w = _pack_small(norm_mix, b_gate, q_norm_a, k_norm_a, q_norm_b, k_norm_b, rpb_b, norm_ffn)
    small_m = _pack_small(m_norm_mix, m_b_gate, m_q_norm_a, m_k_norm_a, m_q_norm_b, m_k_norm_b, m_rpb_b, m_norm_ffn)
    small_v = _pack_small(v_norm_mix, v_b_gate, v_q_norm_a, v_k_norm_a, v_q_norm_b, v_k_norm_b, v_rpb_b, v_norm_ffn)

    upd = [None] * 6
    in_flight = {}

    def finish(after):
        done = []
        for i, r in in_flight.items():
            upd[i] = _adamw(r, big_w[i], big_m[i], big_v[i], name=f"adamw_{names[i]}", after=after)
            done.append(upd[i][0])
        in_flight.clear()
        return done

    def weight_grads(tag, operands):
        sums = {i: _mm_tn_pair(a, b, name=f"grad_{names[i]}", **GRAD_TILES[i]) for i, (a, b) in operands.items()}
        new = list(sums.values())
        done = finish(new)
        in_flight.update(zip(sums, _chip_exchange_on_sequencer(new, f"chip_exchange_{tag}")))
        return new + done

    loss, grad_x, small_g = _local_step(
        x[0], loss_target[0], norm_mix, b_gate, small_w[48:56], small_w[56:120], norm_ffn,
        g_in, g_pa, g_pb, g_out.reshape(D, D), g_up, g_down.reshape(D_FF, D), weight_grads)

    g_norm_mix, g_ba, g_bb, g_gains, g_rpb, g_norm_ffn = small_g
    small_part = jnp.concatenate([g_norm_mix.reshape(16, HD), g_ba.reshape(16, HD), g_bb.reshape(16, HD),
                                  g_gains, g_rpb, g_norm_ffn.reshape(16, HD),
                                  jnp.pad(loss, ((0, 7), (0, HD - 1)))], axis=0)
    slabs = _small_update(small_part, small_w, small_m, small_v)
    total = slabs[0][LOSS_ROW, 0]
    s_g, s_d, s_m, s_v = (_unpack_small(t) for t in slabs)

    finish([grad_x])
    b_g, b_d, b_m, b_v = ([u[j][None] for u in upd] for j in range(4))

    def order(small, big):
        nm, bg, qa, ka, qb, kb, rpb, nf = small
        w_in_, pa_, pb_, out_, up_, down_ = big
        return (nm, w_in_, bg, qa, ka, qb, kb, rpb, pa_, pb_, out_, nf, up_, down_)

    return (total, grad_x[None], *order(s_g, b_g), *order(s_d, b_d), *order(s_m, b_m), *order(s_v, b_v))
```

```python
import functools

import jax
import jax.numpy as jnp
import numpy as np
from jax import lax
from jax.experimental import pallas as pl
from jax.experimental.pallas import tpu as pltpu
from jax.experimental.pallas import tpu_sc as plsc

F32 = jnp.float32
BF16 = jnp.bfloat16

N_DEV = 8
S = 2048
D = 2048
HD = 128
NH = 16
NH_A = 12
QKV = NH * HD
D_IN = 3 * QKV + 2 * D
D_BR = 512
D_FF = 4 * D
GRID_W = 64
ROWS = S // GRID_W
WIN_R = 8
WIN_C = 16
EPS = 1e-6
NEG = -1e30
SCALE = HD ** -0.5
ROPE_THETA = 10000.0
DILATIONS = (1, 4, 16)
HALF_A = 64
QB = 128

LR, B1, B2, AEPS, WD, STEP = 0.001, 0.9, 0.999, 1e-08, 0.01, 10
BC1 = 1.0 - B1 ** STEP
BC2 = 1.0 - B2 ** STEP

VMEM_LIMIT = 56 * 1024 * 1024
MESH = pl.DeviceIdType.MESH

NN = (((1,), (0,)), ((), ()))
NT = (((1,), (1,)), ((), ()))
TN = (((0,), (0,)), ((), ()))


def _params(sem):
    return pltpu.CompilerParams(dimension_semantics=sem, vmem_limit_bytes=VMEM_LIMIT)


def _matmul(a, b, *, product, grid, a_spec, b_spec, epi, out_shape, out_specs, name,
            extra=(), extra_specs=(), after=(), carried=False):
    n_extra = len(extra)

    def body(a_ref, b_ref, *rest):
        epi(product(a_ref, b_ref), rest[:n_extra], rest[n_extra + len(after):])

    return pl.pallas_call(
        body, name=name, grid=grid,
        in_specs=[a_spec, b_spec, *extra_specs, *[pl.BlockSpec(memory_space=pl.ANY)] * len(after)],
        out_specs=out_specs, out_shape=out_shape,
        compiler_params=_params(("arbitrary", "arbitrary") if carried else ("parallel", "parallel")),
    )(a, b, *extra, *after)


def _dot(x, y, dims):
    return lax.dot_general(x, y, dims, preferred_element_type=F32)


def _epi_store(acc, ex, outs):
    outs[0][...] = acc.astype(outs[0].dtype)


def _epi_residual(acc, ex, outs):
    outs[0][...] = acc + ex[0][...]


def _mm_nn(a, b3, *, tm, tn, name, out_dtypes=(F32,), epi=_epi_store, extra=(), total=False):
    m, kdim = a.shape
    g, _, ng = b3.shape
    n = g * ng
    if tn <= ng:
        npg = ng // tn
        b_spec = pl.BlockSpec((None, kdim, tn), lambda j, i: (j // npg, 0, j % npg))

        def product(a_ref, b_ref):
            return _dot(a_ref[...], b_ref[...], NN)
    else:
        gb = tn // ng
        b_spec = pl.BlockSpec((gb, kdim, ng), lambda j, i: (j, 0, 0))

        def product(a_ref, b_ref):
            return jnp.concatenate([_dot(a_ref[...], b_ref[q], NN) for q in range(gb)], axis=1)

    tile = pl.BlockSpec((tm, tn), lambda j, i: (i, j))
    shapes = [jax.ShapeDtypeStruct((m, n), dt) for dt in out_dtypes]
    specs = [tile] * len(shapes)
    if total:
        shapes.append(jax.ShapeDtypeStruct((1, 1), F32))
        specs.append(pl.BlockSpec((1, 1), lambda j, i: (0, 0)))
    single = len(shapes) == 1
    return _matmul(
        a, b3, product=product, grid=(n // tn, m // tm), epi=epi, name=name, carried=total,
        a_spec=pl.BlockSpec((tm, kdim), lambda j, i: (i, 0)), b_spec=b_spec,
        extra=extra, extra_specs=[tile] * len(extra),
        out_shape=shapes[0] if single else shapes, out_specs=specs[0] if single else specs)


def _mm_nt(a, b3, *, tm, tn, name, out_dtype=F32, epi=_epi_store, extra=(), after=()):
    m, kdim = a.shape
    g, n, kg = b3.shape

    def product(a_ref, b_ref):
        acc = _dot(a_ref[:, 0:kg], b_ref[0], NT)
        for q in range(1, g):
            acc = acc + _dot(a_ref[:, q * kg:(q + 1) * kg], b_ref[q], NT)
        return acc

    tile = pl.BlockSpec((tm, tn), lambda j, i: (i, j))
    return _matmul(
        a, b3, product=product, grid=(n // tn, m // tm), epi=epi, name=name,
        a_spec=pl.BlockSpec((tm, kdim), lambda j, i: (i, 0)),
        b_spec=pl.BlockSpec((g, tn, kg), lambda j, i: (0, j, 0)),
        extra=extra, extra_specs=[tile] * len(extra), after=after,
        out_shape=jax.ShapeDtypeStruct((m, n), out_dtype), out_specs=tile)


def _mm_tn(a, b, *, tm, tn, name, groups=1, out_dtype=BF16):
    t, m = a.shape
    _, n = b.shape
    ng = n // groups
    if tn <= ng:
        npg = ng // tn
        out_spec = pl.BlockSpec((None, tm, tn), lambda j, i: (j // npg, i, j % npg))
        epi = _epi_store

        def product(a_ref, b_ref):
            return _dot(a_ref[...], b_ref[...], TN)
    else:
        gb = tn // ng
        out_spec = pl.BlockSpec((gb, tm, ng), lambda j, i: (j, i, 0))

        def product(a_ref, b_ref):
            return [_dot(a_ref[...], b_ref[:, q * ng:(q + 1) * ng], TN) for q in range(gb)]

        def epi(parts, ex, outs):
            for q, part in enumerate(parts):
                outs[0][q] = part.astype(out_dtype)

    return _matmul(
        a, b, product=product, grid=(n // tn, m // tm), epi=epi, name=name,
        a_spec=pl.BlockSpec((t, tm), lambda j, i: (0, i)),
        b_spec=pl.BlockSpec((t, tn), lambda j, i: (0, j)),
        out_shape=jax.ShapeDtypeStruct((groups, m, ng), out_dtype), out_specs=out_spec)


def _rms_fwd(x, g, *, name, tr=256):
    def body(x_ref, g_ref, y_ref, r_ref):
        xv = x_ref[...]
        r = lax.rsqrt(jnp.mean(xv * xv, axis=-1, keepdims=True) + EPS)
        y_ref[...] = (xv * r * g_ref[...]).astype(BF16)
        r_ref[...] = r

    row = pl.BlockSpec((tr, D), lambda i: (i, 0))
    return pl.pallas_call(
        body, name=name, grid=(S // tr,),
        in_specs=[row, pl.BlockSpec((1, D), lambda i: (0, 0))],
        out_specs=[row, pl.BlockSpec((tr, 1), lambda i: (i, 0))],
        out_shape=[jax.ShapeDtypeStruct((S, D), BF16), jax.ShapeDtypeStruct((S, 1), F32)],
        compiler_params=_params(("parallel",)),
    )(x, g)


def _rms_bwd(dy, x, rstd, g, resid, *, name, tr=256):
    def body(dy_ref, x_ref, r_ref, g_ref, res_ref, dx_ref, dxb_ref, dg_ref):
        r = r_ref[...]
        xh = x_ref[...] * r
        dyv = dy_ref[...]
        t = dyv * g_ref[...]
        dx = r * (t - xh * jnp.mean(t * xh, axis=-1, keepdims=True)) + res_ref[...]
        dx_ref[...] = dx
        dxb_ref[...] = dx.astype(BF16)
        part = jnp.sum(dyv * xh, axis=0, keepdims=True)

        @pl.when(pl.program_id(0) == 0)
        def _():
            dg_ref[...] = part

        @pl.when(pl.program_id(0) > 0)
        def _():
            dg_ref[...] += part

    row = pl.BlockSpec((tr, D), lambda i: (i, 0))
    vec = pl.BlockSpec((1, D), lambda i: (0, 0))
    return pl.pallas_call(
        body, name=name, grid=(S // tr,),
        in_specs=[row, row, pl.BlockSpec((tr, 1), lambda i: (i, 0)), vec, row],
        out_specs=[row, row, vec],
        out_shape=[jax.ShapeDtypeStruct((S, D), F32), jax.ShapeDtypeStruct((S, D), BF16),
                   jax.ShapeDtypeStruct((1, D), F32)],
        compiler_params=_params(("arbitrary",)),
    )(dy, x, rstd, g, resid)


def _rope_tables():
    pos = np.arange(S, dtype=np.float32)
    inv = (ROPE_THETA ** (-np.arange(0, HD, 2, dtype=np.float32) / HD)).astype(np.float32)
    ang = pos[:, None] * inv[None, :]
    cos, sin = np.cos(ang), np.sin(ang)
    return (jnp.asarray(np.concatenate([cos, cos], axis=-1), F32),
            jnp.asarray(np.concatenate([-sin, sin], axis=-1), F32))


def _swap_halves(t):
    return pltpu.roll(t, HD // 2, axis=1)


TOK = 256


def _lane_block_spec(d, last=HD):
    return pl.BlockSpec((4, TOK // d, d * last), lambda i: (0, i, 0))


def _to_lane_blocks(dst, head, val, d, scr, dtype):
    w = val.shape[1]
    if d == 1:
        dst[head] = val.astype(dtype)
        return
    scr[...] = val
    for r in range(d):
        dst[head, :, r * w:(r + 1) * w] = scr[pl.ds(r, TOK // d, stride=d), :].astype(dtype)


def _from_lane_blocks(src, head, d, w, scr):
    if d == 1:
        return src[head].astype(F32)
    for r in range(d):
        scr[pl.ds(r, TOK // d, stride=d), :] = src[head, :, r * w:(r + 1) * w].astype(F32)
    return scr[...]


def _qk_prep(proj, gains, cos2, sin2):
    def body(q_ref, k_ref, v_ref, g_ref, c_ref, s_ref, *rest):
        outs, scr = rest[:-1], rest[-1]
        cos, sin = c_ref[...], s_ref[...]
        for which, (src, row_a, row_b) in enumerate(((q_ref, 0, 2), (k_ref, 1, 3), (v_ref, None, None))):
            for h in range(NH):
                y = src[:, h * HD:(h + 1) * HD]
                if row_a is not None:
                    y = y * lax.rsqrt(jnp.mean(y * y, axis=-1, keepdims=True) + EPS)
                    if h < NH_A:
                        y = y * g_ref[row_a:row_a + 1, :]
                        y = y * cos + _swap_halves(y) * sin
                    else:
                        y = y * g_ref[row_b:row_b + 1, :]
                if h < NH_A:
                    gi = h // 4
                    _to_lane_blocks(outs[3 * gi + which], h % 4, y, DILATIONS[gi], scr, BF16)
                else:
                    hb = h - NH_A
                    outs[9 + which][:, hb * HD:(hb + 1) * HD] = y.astype(BF16)

    def blk(c):
        return pl.BlockSpec((TOK, QKV), lambda i: (i, c))
    tab = pl.BlockSpec((TOK, HD), lambda i: (i, 0))
    out_specs, out_shape = [], []
    for d in DILATIONS:
        out_specs += [_lane_block_spec(d)] * 3
        out_shape += [jax.ShapeDtypeStruct((4, S // d, d * HD), BF16)] * 3
    out_specs += [pl.BlockSpec((TOK, D_BR), lambda i: (i, 0))] * 3
    out_shape += [jax.ShapeDtypeStruct((S, D_BR), BF16)] * 3
    outs = pl.pallas_call(
        body, name="qk_prep", grid=(S // TOK,),
        in_specs=[blk(0), blk(1), blk(2), pl.BlockSpec((8, HD), lambda i: (0, 0)), tab, tab],
        out_specs=out_specs, out_shape=out_shape,
        scratch_shapes=[pltpu.VMEM((TOK, HD), F32)],
        compiler_params=_params(("parallel",)),
    )(proj, proj, proj, gains, cos2, sin2)
    return [tuple(outs[3 * gi:3 * gi + 3]) for gi in range(3)], tuple(outs[9:12])


def _qk_prep_bwd(dproj, proj, gains, cos2, sin2, grads_a, grads_b):
    def body(dp_in, q_ref, k_ref, g_ref, c_ref, s_ref, *rest):
        grads, (dp_out, dg_ref, scr) = rest[:12], rest[12:]
        del dp_in
        cos, sin = c_ref[...], s_ref[...]

        def grad_of(which, h):
            if h < NH_A:
                gi = h // 4
                return _from_lane_blocks(grads[3 * gi + which], h % 4, DILATIONS[gi], HD, scr)
            hb = h - NH_A
            return grads[9 + which][:, hb * HD:(hb + 1) * HD]

        dg_rows = []
        for which, (src, base, row_a, row_b) in enumerate(((q_ref, 0, 0, 2), (k_ref, QKV, 1, 3))):
            dg_a = jnp.zeros((1, HD), F32)
            dg_b = jnp.zeros((1, HD), F32)
            for h in range(NH):
                t = src[:, h * HD:(h + 1) * HD]
                dy = grad_of(which, h)
                r = lax.rsqrt(jnp.mean(t * t, axis=-1, keepdims=True) + EPS)
                xh = t * r
                if h < NH_A:
                    dy = dy * cos - _swap_halves(dy) * sin
                    gain = g_ref[row_a:row_a + 1, :]
                    dg_a = dg_a + jnp.sum(dy * xh, axis=0, keepdims=True)
                else:
                    gain = g_ref[row_b:row_b + 1, :]
                    dg_b = dg_b + jnp.sum(dy * xh, axis=0, keepdims=True)
                u = dy * gain
                dx = r * (u - xh * jnp.mean(u * xh, axis=-1, keepdims=True))
                dp_out[:, base + h * HD:base + (h + 1) * HD] = dx.astype(BF16)
            dg_rows += [(row_a, dg_a), (row_b, dg_b)]
        for h in range(NH):
            dp_out[:, 2 * QKV + h * HD:2 * QKV + (h + 1) * HD] = grad_of(2, h).astype(BF16)

        @pl.when(pl.program_id(0) == 0)
        def _():
            dg_ref[...] = jnp.zeros((8, HD), F32)

        for row, val in dg_rows:
            dg_ref[row:row + 1, :] += val

    def blk(c):
        return pl.BlockSpec((TOK, QKV), lambda i: (i, c))
    tab = pl.BlockSpec((TOK, HD), lambda i: (i, 0))
    gain_spec = pl.BlockSpec((8, HD), lambda i: (0, 0))
    grad_specs = [s for d in DILATIONS for s in [_lane_block_spec(d)] * 3]
    grad_specs += [pl.BlockSpec((TOK, D_BR), lambda i: (i, 0))] * 3
    return pl.pallas_call(
        body, name="qk_prep_bwd", grid=(S // TOK,),
        in_specs=[pl.BlockSpec(memory_space=pl.ANY), blk(0), blk(1), gain_spec, tab, tab] + grad_specs,
        out_specs=[pl.BlockSpec((TOK, 3 * QKV), lambda i: (i, 0)), gain_spec],
        out_shape=[jax.ShapeDtypeStruct((S, D_IN), BF16), jax.ShapeDtypeStruct((8, HD), F32)],
        input_output_aliases={0: 0},
        scratch_shapes=[pltpu.VMEM((TOK, HD), F32)],
        compiler_params=_params(("arbitrary",)),
    )(dproj, proj, proj, gains, cos2, sin2, *[g for grp in grads_a for g in grp], *grads_b)


def _mix_fwd(oa, ob, w_pa, w_pb, proj, b_gate, *, tr=256):
    def body(oa_ref, ob_ref, pa_ref, pb_ref, la_ref, lb_ref, ba_ref, bb_ref, mix_ref, ya_ref, yb_ref):
        ya = jnp.concatenate([_dot(oa_ref[...], pa_ref[q], NN) for q in range(N_DEV)], axis=1)
        yb = jnp.concatenate([_dot(ob_ref[...], pb_ref[q], NN) for q in range(N_DEV)], axis=1)
        ga = jax.nn.sigmoid(la_ref[...] + ba_ref[...])
        gb = jax.nn.sigmoid(lb_ref[...] + bb_ref[...])
        mix_ref[...] = (ga * ya + gb * yb).astype(BF16)
        ya_ref[...] = ya.astype(BF16)
        yb_ref[...] = yb.astype(BF16)

    row = pl.BlockSpec((tr, D), lambda i: (i, 0))
    branch = pl.BlockSpec((tr, D_BR), lambda i: (i, 0))
    whole = pl.BlockSpec((N_DEV, D_BR, D // N_DEV), lambda i: (0, 0, 0))
    return pl.pallas_call(
        body, name="mix_fwd", grid=(S // tr,),
        in_specs=[branch, branch, whole, whole,
                  pl.BlockSpec((tr, D), lambda i: (i, 3)), pl.BlockSpec((tr, D), lambda i: (i, 4)),
                  pl.BlockSpec((1, D), lambda i: (0, 0)), pl.BlockSpec((1, D), lambda i: (0, 1))],
        out_specs=[row, row, row], out_shape=[jax.ShapeDtypeStruct((S, D), BF16)] * 3,
        compiler_params=_params(("parallel",)),
    )(oa, ob, w_pa, w_pb, proj, proj, b_gate, b_gate)


def _gate_bwd(branch, dmixed, proj, b_gate, y, dproj, *, tr=256):
    aliased = dproj is not None

    def body(dm_ref, l_ref, b_ref, y_ref, *rest):
        dy_ref, dp_ref, db_ref = rest[-3:]
        g = jax.nn.sigmoid(l_ref[...] + b_ref[...])
        dm = dm_ref[...]
        dy_ref[...] = (dm * g).astype(BF16)
        dl = dm * y_ref[...].astype(F32) * g * (1.0 - g)
        dp_ref[...] = dl.astype(BF16)
        part = jnp.sum(dl, axis=0, keepdims=True)

        @pl.when(pl.program_id(0) == 0)
        def _():
            db_ref[...] = part

        @pl.when(pl.program_id(0) > 0)
        def _():
            db_ref[...] += part

    row = pl.BlockSpec((tr, D), lambda i: (i, 0))
    col = pl.BlockSpec((tr, D), lambda i: (i, 3 + branch))
    vec = pl.BlockSpec((1, D), lambda i: (0, 0))
    return pl.pallas_call(
        body, name=f"gate_bwd_{branch}", grid=(S // tr,),
        in_specs=[row, col, pl.BlockSpec((1, D), lambda i: (0, branch)), row]
        + ([pl.BlockSpec(memory_space=pl.ANY)] if aliased else []),
        out_specs=[row, col, vec],
        out_shape=[jax.ShapeDtypeStruct((S, D), BF16), jax.ShapeDtypeStruct((S, D_IN), BF16),
                   jax.ShapeDtypeStruct((1, D), F32)],
        input_output_aliases={4: 1} if aliased else {},
        compiler_params=_params(("arbitrary",)),
    )(dmixed, proj, b_gate, y, *([dproj] if aliased else []))


def _band_blocks(m_len):
    wk = min(m_len, QB + 2 * QB)
    return [(qb * QB, min(max(qb * QB - QB, 0), m_len - wk), wk) for qb in range(m_len // QB)]


def _band_scores(q, kw, q0, k0, wk):
    s = _dot(q, kw, NT) * SCALE
    qpos = q0 + lax.broadcasted_iota(jnp.int32, (QB, 1), 0)
    kpos = k0 + lax.broadcasted_iota(jnp.int32, (1, wk), 1)
    return jnp.where(jnp.abs(kpos - qpos) <= HALF_A, s, NEG)


def _attn_a_fwd(q, k, v, gi):
    d = DILATIONS[gi]
    m_len = S // d

    def body(q_ref, k_ref, v_ref, o_ref, lse_ref):
        for r in range(d):
            lanes = slice(r * HD, (r + 1) * HD)
            for q0, k0, wk in _band_blocks(m_len):
                s = _band_scores(q_ref[q0:q0 + QB, lanes], k_ref[k0:k0 + wk, lanes], q0, k0, wk)
                m = jnp.max(s, axis=-1, keepdims=True)
                p = jnp.exp(s - m)
                l = jnp.sum(p, axis=-1, keepdims=True)
                o_ref[q0:q0 + QB, lanes] = _dot(p.astype(BF16), v_ref[k0:k0 + wk, lanes], NN) / l
                lse_ref[q0:q0 + QB, r:r + 1] = m + jnp.log(l)

    head = pl.BlockSpec((None, m_len, d * HD), lambda h: (h, 0, 0))
    stat = pl.BlockSpec((None, m_len, d), lambda h: (h, 0, 0))
    return pl.pallas_call(
        body, name=f"attn_a_fwd_{gi}", grid=(4,),
        in_specs=[head, head, head], out_specs=[head, stat],
        out_shape=[jax.ShapeDtypeStruct((4, m_len, d * HD), F32), jax.ShapeDtypeStruct((4, m_len, d), F32)],
        compiler_params=_params(("parallel",)),
    )(q, k, v)


def _combine_a(os, lses):
    def body(o0, o1, o2, l0, l1, l2, oa_ref, lse_ref, scr, scr1):
        for h in range(4):
            o = [_from_lane_blocks(ref, h, d, HD, scr) for ref, d in zip((o0, o1, o2), DILATIONS)]
            a, b, c = (_from_lane_blocks(ref, h, d, 1, scr1) for ref, d in zip((l0, l1, l2), DILATIONS))
            m = jnp.maximum(jnp.maximum(a, b), c)
            wa, wb, wc = jnp.exp(a - m), jnp.exp(b - m), jnp.exp(c - m)
            tot = wa + wb + wc
            oa_ref[:, h * HD:(h + 1) * HD] = ((wa * o[0] + wb * o[1] + wc * o[2]) / tot).astype(BF16)
            lse_ref[h] = m + jnp.log(tot)

    return pl.pallas_call(
        body, name="combine_a", grid=(S // TOK,),
        in_specs=[_lane_block_spec(d) for d in DILATIONS] + [_lane_block_spec(d, 1) for d in DILATIONS],
        out_specs=[pl.BlockSpec((TOK, D_BR), lambda i: (i, 0)), pl.BlockSpec((4, TOK, 1), lambda i: (0, i, 0))],
        out_shape=[jax.ShapeDtypeStruct((S, D_BR), BF16), jax.ShapeDtypeStruct((4, S, 1), F32)],
        scratch_shapes=[pltpu.VMEM((TOK, HD), F32), pltpu.VMEM((TOK, 1), F32)],
        compiler_params=_params(("parallel",)),
    )(*os, *lses)


def _attn_a_bwd_prep(doa, oa, lse):
    def body(do_ref, o_ref, l_ref, *rest):
        outs, (scr, scr1) = rest[:9], rest[9:]
        for h in range(4):
            do = do_ref[:, h * HD:(h + 1) * HD]
            dsum = jnp.sum(do * o_ref[:, h * HD:(h + 1) * HD].astype(F32), axis=-1, keepdims=True)
            for gi, d in enumerate(DILATIONS):
                _to_lane_blocks(outs[3 * gi], h, do, d, scr, BF16)
                _to_lane_blocks(outs[3 * gi + 1], h, l_ref[h], d, scr1, F32)
                _to_lane_blocks(outs[3 * gi + 2], h, dsum, d, scr1, F32)

    row = pl.BlockSpec((TOK, D_BR), lambda i: (i, 0))
    out_specs, out_shape = [], []
    for d in DILATIONS:
        out_specs += [_lane_block_spec(d), _lane_block_spec(d, 1), _lane_block_spec(d, 1)]
        out_shape += [jax.ShapeDtypeStruct((4, S // d, d * HD), BF16)] + [jax.ShapeDtypeStruct((4, S // d, d), F32)] * 2
    outs = pl.pallas_call(
        body, name="attn_a_bwd_prep", grid=(S // TOK,),
        in_specs=[row, row, pl.BlockSpec((4, TOK, 1), lambda i: (0, i, 0))],
        out_specs=out_specs, out_shape=out_shape,
        scratch_shapes=[pltpu.VMEM((TOK, HD), F32), pltpu.VMEM((TOK, 1), F32)],
        compiler_params=_params(("parallel",)),
    )(doa, oa, lse)
    return [tuple(outs[3 * gi:3 * gi + 3]) for gi in range(3)]


def _attn_a_bwd(q, k, v, do, lse, dsum, gi):
    d = DILATIONS[gi]
    m_len = S // d

    def body(q_ref, k_ref, v_ref, do_ref, lse_ref, dsum_ref, dq_ref, dk_ref, dv_ref):
        dk_ref[...] = jnp.zeros((m_len, d * HD), F32)
        dv_ref[...] = jnp.zeros((m_len, d * HD), F32)
        for r in range(d):
            lanes = slice(r * HD, (r + 1) * HD)
            for q0, k0, wk in _band_blocks(m_len):
                rows, keys = slice(q0, q0 + QB), slice(k0, k0 + wk)
                qv, kw, vw, dov = q_ref[rows, lanes], k_ref[keys, lanes], v_ref[keys, lanes], do_ref[rows, lanes]
                p = jnp.exp(_band_scores(qv, kw, q0, k0, wk) - lse_ref[rows, r:r + 1])
                ds = (p * (_dot(dov, vw, NT) - dsum_ref[rows, r:r + 1]) * SCALE).astype(BF16)
                dq_ref[rows, lanes] = _dot(ds, kw, NN)
                dk_ref[keys, lanes] += _dot(ds, qv, TN)
                dv_ref[keys, lanes] += _dot(p.astype(BF16), dov, TN)

    head = pl.BlockSpec((None, m_len, d * HD), lambda h: (h, 0, 0))
    stat = pl.BlockSpec((None, m_len, d), lambda h: (h, 0, 0))
    shape = jax.ShapeDtypeStruct((4, m_len, d * HD), F32)
    return pl.pallas_call(
        body, name=f"attn_a_bwd_{gi}", grid=(4,),
        in_specs=[head, head, head, head, stat, stat], out_specs=[head, head, head],
        out_shape=[shape, shape, shape],
        compiler_params=_params(("parallel",)),
    )(q, k, v, do, lse, dsum)


KEYS_B = WIN_R * GRID_W
N_OFF = WIN_R


def _bias_constants():
    q = np.arange(GRID_W)[:, None]
    kc = np.arange(GRID_W)[None, :]
    dc = np.clip(kc - q, -(WIN_C - 1), WIN_C - 1) + (WIN_C - 1)
    expand = np.zeros((HD, GRID_W * GRID_W), np.float32)
    expand[dc.reshape(-1), np.arange(GRID_W * GRID_W)] = 1.0
    cs = np.clip(q - WIN_C // 2, 0, GRID_W - WIN_C)
    keep = ((kc >= cs) & (kc < cs + WIN_C)).reshape(1, -1).astype(np.float32)
    sel = np.zeros((64, 4 * N_OFF * WIN_R), np.float32)
    for h in range(4):
        for off in range(N_OFF):
            for j in range(WIN_R):
                sel[h * (2 * WIN_R - 1) + off + j, (h * N_OFF + off) * WIN_R + j] = 1.0
    return jnp.asarray(expand), jnp.asarray(keep), jnp.asarray(sel)


def _bias_expand(rpb_pad, expand, keep, sel):
    def body(r_ref, e_ref, k_ref, s_ref, o_ref):
        t = lax.dot_general(r_ref[...], e_ref[...], NN, precision=lax.Precision.HIGHEST,
                            preferred_element_type=F32)
        rows = lax.dot_general(s_ref[...], t, TN, precision=lax.Precision.HIGHEST,
                               preferred_element_type=F32)
        o_ref[...] = jnp.where(k_ref[...] > 0.5, rows, NEG)

    return pl.pallas_call(
        body, name="bias_expand",
        out_shape=jax.ShapeDtypeStruct((4 * N_OFF * WIN_R, GRID_W * GRID_W), F32),
        compiler_params=pltpu.CompilerParams(vmem_limit_bytes=VMEM_LIMIT),
    )(rpb_pad, expand, keep, sel)


def _bias_reduce(dbias_rows, expand, sel):
    def body(x_ref, e_ref, s_ref, o_ref):
        z = lax.dot_general(x_ref[...], e_ref[...], NT, precision=lax.Precision.HIGHEST,
                            preferred_element_type=F32)
        o_ref[...] = lax.dot_general(s_ref[...], z, NN, precision=lax.Precision.HIGHEST,
                                     preferred_element_type=F32)

    return pl.pallas_call(
        body, name="bias_reduce", out_shape=jax.ShapeDtypeStruct((64, HD), F32),
        compiler_params=pltpu.CompilerParams(vmem_limit_bytes=VMEM_LIMIT),
    )(dbias_rows, expand, sel)


def _rows_to_tab(rows):
    t = rows.reshape(4, N_OFF, WIN_R, GRID_W, GRID_W)
    return t.transpose(0, 1, 3, 2, 4).reshape(4, N_OFF, GRID_W, KEYS_B)


def _tab_to_rows(tab):
    t = tab.reshape(4, N_OFF, GRID_W, WIN_R, GRID_W)
    return t.transpose(0, 1, 3, 2, 4).reshape(4 * N_OFF * WIN_R, GRID_W * GRID_W)


def _row_window(r):
    r0 = jnp.clip(r - WIN_R // 2, 0, ROWS - WIN_R)
    off = r0 + (WIN_R - 1) - r
    return pl.multiple_of(r * GRID_W, GRID_W), pl.multiple_of(r0 * GRID_W, GRID_W), off


def _attn_b_fwd(qn, kn, vb, bias_tab):
    def body(q_ref, k_ref, v_ref, b_ref, o_ref, lse_ref):
        def row(r, carry):
            qs, ks, off = _row_window(r)
            q = q_ref[pl.ds(qs, GRID_W), :]
            s = lax.dot_general(q, k_ref[pl.ds(ks, KEYS_B), :], NT, preferred_element_type=F32) * SCALE
            s = s + b_ref[off]
            m = jnp.max(s, axis=-1, keepdims=True)
            p = jnp.exp(s - m)
            l = jnp.sum(p, axis=-1, keepdims=True)
            o = lax.dot_general(p.astype(BF16), v_ref[pl.ds(ks, KEYS_B), :], NN, preferred_element_type=F32)
            o_ref[pl.ds(qs, GRID_W), :] = (o / l).astype(BF16)
            lse_ref[pl.ds(qs, GRID_W), :] = m + jnp.log(l)
            return carry

        lax.fori_loop(0, ROWS, row, 0)

    full = pl.BlockSpec((S, HD), lambda h: (0, h))
    return pl.pallas_call(
        body, name="attn_b_fwd", grid=(4,),
        in_specs=[full, full, full, pl.BlockSpec((None, N_OFF, GRID_W, KEYS_B), lambda h: (h, 0, 0, 0))],
        out_specs=[pl.BlockSpec((S, HD), lambda h: (0, h)), pl.BlockSpec((None, S, 1), lambda h: (h, 0, 0))],
        out_shape=[jax.ShapeDtypeStruct((S, D_BR), BF16), jax.ShapeDtypeStruct((4, S, 1), F32)],
        compiler_params=_params(("parallel",)),
    )(qn, kn, vb, bias_tab)


def _attn_b_bwd(qn, kn, vb, bias_tab, ob, dob, lse):
    def body(q_ref, k_ref, v_ref, b_ref, o_ref, do_ref, lse_ref, dq_ref, dk_ref, dv_ref, db_ref):
        dk_ref[...] = jnp.zeros((S, HD), F32)
        dv_ref[...] = jnp.zeros((S, HD), F32)
        db_ref[...] = jnp.zeros((N_OFF, GRID_W, KEYS_B), F32)

        def row(r, carry):
            qs, ks, off = _row_window(r)
            rows = pl.ds(qs, GRID_W)
            keys = pl.ds(ks, KEYS_B)
            q = q_ref[rows, :]
            kw = k_ref[keys, :]
            s = lax.dot_general(q, kw, NT, preferred_element_type=F32) * SCALE + b_ref[off]
            p = jnp.exp(s - lse_ref[rows, :])
            do = do_ref[rows, :]
            dobf = do.astype(BF16)
            dsum = jnp.sum(do * o_ref[rows, :].astype(F32), axis=-1, keepdims=True)
            dp = lax.dot_general(dobf, v_ref[keys, :], NT, preferred_element_type=F32)
            ds = p * (dp - dsum)
            db_ref[off] += ds
            dsb = (ds * SCALE).astype(BF16)
            dq_ref[rows, :] = lax.dot_general(dsb, kw, NN, preferred_element_type=F32)
            dk_ref[keys, :] += lax.dot_general(dsb, q, TN, preferred_element_type=F32)
            dv_ref[keys, :] += lax.dot_general(p.astype(BF16), dobf, TN, preferred_element_type=F32)
            return carry

        lax.fori_loop(0, ROWS, row, 0)

    full = pl.BlockSpec((S, HD), lambda h: (0, h))
    slot = pl.BlockSpec((S, HD), lambda h: (0, h))
    tab = pl.BlockSpec((None, N_OFF, GRID_W, KEYS_B), lambda h: (h, 0, 0, 0))
    shape = jax.ShapeDtypeStruct((S, D_BR), F32)
    return pl.pallas_call(
        body, name="attn_b_bwd", grid=(4,),
        in_specs=[full, full, full, tab, slot, slot, pl.BlockSpec((None, S, 1), lambda h: (h, 0, 0))],
        out_specs=[slot, slot, slot, tab],
        out_shape=[shape, shape, shape, jax.ShapeDtypeStruct((4, N_OFF, GRID_W, KEYS_B), F32)],
        compiler_params=_params(("parallel",)),
    )(qn, kn, vb, bias_tab, ob, dob, lse)


def _epi_relu_sq(acc, ex, outs):
    u = jnp.maximum(acc, 0.0)
    outs[0][...] = u.astype(BF16)
    outs[1][...] = (u * u).astype(BF16)


def _epi_relu_sq_bwd(acc, ex, outs):
    outs[0][...] = (acc * (2.0 * ex[0][...].astype(F32))).astype(BF16)


def _epi_loss_head(acc, ex, outs):
    e = acc + ex[0][...] - ex[1][...]
    dy = e * (1.0 / D)
    outs[0][...] = dy
    outs[1][...] = dy.astype(BF16)
    part = (0.5 / D) * jnp.sum(jnp.sum(e * e, axis=-1, keepdims=True), axis=0, keepdims=True)
    first = (pl.program_id(0) == 0) & (pl.program_id(1) == 0)

    @pl.when(first)
    def _():
        outs[2][...] = part

    @pl.when(jnp.logical_not(first))
    def _():
        outs[2][...] += part


def _local_step(x, target, norm_mix, b_gate, gains, rpb_pad, norm_ffn,
                w_in, w_pa, w_pb, w_out, w_up, w_down, weight_grads):
    cos2, sin2 = _rope_tables()
    expand, keep, sel = _bias_constants()
    w_out3, w_down3 = w_out[None], w_down[None]

    xn, rstd1 = _rms_fwd(x, norm_mix, name="rms_mix")
    proj = _mm_nn(xn, w_in, tm=1024, tn=1280, name="proj")
    qkv_a, qkv_b = _qk_prep(proj, gains, cos2, sin2)
    fwd_a = [_attn_a_fwd(*qkv_a[gi], gi) for gi in range(3)]
    oa, lse_a = _combine_a([o for o, _ in fwd_a], [l for _, l in fwd_a])
    bias_tab = _rows_to_tab(_bias_expand(rpb_pad, expand, keep, sel))
    ob, lse_b = _attn_b_fwd(*qkv_b, bias_tab)
    mixed, ya, yb = _mix_fwd(oa, ob, w_pa, w_pb, proj, b_gate)
    h1 = _mm_nn(mixed, w_out3, tm=1024, tn=1024, name="out_proj", epi=_epi_residual, extra=(x,))
    hn, rstd2 = _rms_fwd(h1, norm_ffn, name="rms_ffn")
    u, usq = _mm_nn(hn, w_up, tm=1024, tn=1024, name="ffn_up", epi=_epi_relu_sq,
                    out_dtypes=(BF16, BF16))
    dy, dyb, loss = _mm_nn(usq, w_down3, tm=512, tn=512, name="ffn_down", epi=_epi_loss_head,
                           extra=(h1, target), out_dtypes=(F32, BF16), total=True)

    sent = weight_grads("w_down", {5: (usq, dyb)})
    du = _mm_nt(dyb, w_down3, tm=1024, tn=1024, name="ffn_down_bwd", out_dtype=BF16,
                epi=_epi_relu_sq_bwd, extra=(u,), after=sent)
    sent = weight_grads("w_up", {4: (hn, du)})
    dhn = _mm_nt(du, w_up, tm=512, tn=512, name="ffn_up_bwd", after=sent)
    dh1, dh1b, g_norm_ffn = _rms_bwd(dhn, h1, rstd2, norm_ffn, dy, name="rms_ffn_bwd")

    dmixed = _mm_nt(dh1b, w_out3, tm=1024, tn=1024, name="out_proj_bwd")
    dya, dproj, g_ba = _gate_bwd(0, dmixed, proj, b_gate, ya, None)
    dyb2, dproj, g_bb = _gate_bwd(1, dmixed, proj, b_gate, yb, dproj)
    sent = weight_grads("w_mix", {3: (mixed, dh1b), 1: (oa, dya), 2: (ob, dyb2)})
    doa = _mm_nt(dya, w_pa, tm=1024, tn=D_BR, name="proj_a_bwd", after=sent)
    dob = _mm_nt(dyb2, w_pb, tm=1024, tn=D_BR, name="proj_b_bwd")
    prep = _attn_a_bwd_prep(doa, oa, lse_a)
    grads_a = [_attn_a_bwd(*qkv_a[gi], *prep[gi], gi) for gi in range(3)]
    dqb, dkb, dvb, dbias = _attn_b_bwd(*qkv_b, bias_tab, ob, dob, lse_b)
    g_rpb = _bias_reduce(_tab_to_rows(dbias), expand, sel)
    dproj, g_gains = _qk_prep_bwd(dproj, proj, gains, cos2, sin2, grads_a, (dqb, dkb, dvb))
    sent = weight_grads("w_in", {0: (xn, dproj)})
    dxn = _mm_nt(dproj, w_in, tm=256, tn=512, name="proj_bwd", after=sent)
    grad_x, _, g_norm_mix = _rms_bwd(dxn, x, rstd1, norm_mix, dh1, name="rms_mix_bwd")

    small = (g_norm_mix, g_ba, g_bb, g_gains, g_rpb, g_norm_ffn)
    return loss, grad_x, small


def _cast_bf16(w, *, tr=256):
    rows, cols = w.shape
    tr = min(tr, rows)

    def body(w_ref, o_ref):
        o_ref[...] = w_ref[...].astype(BF16)

    spec = pl.BlockSpec((tr, cols), lambda i: (i, 0))
    return pl.pallas_call(
        body, name=f"cast_{rows}x{cols}", grid=(rows // tr,), in_specs=[spec], out_specs=spec,
        out_shape=jax.ShapeDtypeStruct((rows, cols), BF16), compiler_params=_params(("parallel",)),
    )(w)


def _me_and_peers():
    x, y, c = lax.axis_index("x"), lax.axis_index("y"), lax.axis_index("c")
    me = 4 * x + 2 * y + c
    peers = []
    for k in range(1, N_DEV):
        px = 1 - x if k & 4 else x
        py = 1 - y if k & 2 else y
        pc = 1 - c if k & 1 else c
        peers.append(((px, py, pc), 4 * px + 2 * py + pc))
    return me, peers


def _gather_on_sequencer(shards, name):
    n = len(shards)
    hbm = pltpu.MemorySpace.HBM
    ins = [jax.new_ref(s, memory_space=hbm) for s in shards]
    outs = [jax.empty_ref(jax.ShapeDtypeStruct((N_DEV,) + s.shape, s.dtype), memory_space=hbm) for s in shards]

    @pl.kernel(mesh=plsc.ScalarSubcoreMesh(axis_name="seq", num_cores=1), name=name,
               scratch_types=(pltpu.SemaphoreType.DMA((n, N_DEV - 1)), pltpu.SemaphoreType.DMA((n, N_DEV - 1)),
                              pltpu.SemaphoreType.DMA((n,))),
               compiler_params=pltpu.CompilerParams(collective_id=0))
    def launch(send, recv, lsem):
        x, y, c = lax.axis_index("x"), lax.axis_index("y"), lax.axis_index("c")
        me, sibling = (x, y, c), (x, y, 1 - c)
        chips = [(1 - x, y), (x, 1 - y), (1 - x, 1 - y)]
        barrier = pltpu.get_barrier_semaphore()
        for peer in [sibling] + [(*chip, c) for chip in chips]:
            pl.semaphore_signal(barrier, inc=1, device_id=peer, device_id_type=MESH)
        pl.semaphore_wait(barrier, 4)

        def copy(w, k, block, to, src=None):
            px, py, pc = block
            dst = outs[w].at[4 * px + 2 * py + pc]
            return pltpu.make_async_remote_copy(dst if src is None else src, dst, send.at[w, k], recv.at[w, k],
                                                device_id=to, device_id_type=MESH)

        local = [pltpu.make_async_copy(ins[w], outs[w].at[4 * x + 2 * y + c], lsem.at[w]) for w in range(n)]
        for cp in local:
            cp.start()
        first = []
        for w in range(n):
            first += [copy(w, 1 + j, me, (*chip, c), src=ins[w]) for j, chip in enumerate(chips)]
            first.append(copy(w, 0, me, sibling, src=ins[w]))
        for cp in first:
            cp.start()
        passed = []
        for w in range(n):
            for j, chip in enumerate(chips):
                copy(w, 1 + j, (*chip, c), me).wait_recv()
                cp = copy(w, 4 + j, (*chip, c), sibling)
                cp.start()
                passed.append(cp)
        for w in range(n):
            copy(w, 0, sibling, me).wait_recv()
            for j, chip in enumerate(chips):
                copy(w, 4 + j, (*chip, 1 - c), me).wait_recv()
        for cp in first + passed:
            cp.wait_send()
        for cp in local:
            cp.wait()

    launch()
    return [o[...] for o in outs]


N_CHIP = 4
CHIPS = ((0, 0), (0, 1), (1, 0), (1, 1))


def _sequencer(name, n_sems, collective_id):
    return functools.partial(
        pl.kernel, mesh=plsc.ScalarSubcoreMesh(axis_name="seq", num_cores=1), name=name,
        scratch_types=tuple(pltpu.SemaphoreType.DMA(s) for s in n_sems),
        compiler_params=pltpu.CompilerParams(collective_id=collective_id))


def _handshake(peers):
    barrier = pltpu.get_barrier_semaphore()
    for peer in peers:
        pl.semaphore_signal(barrier, inc=1, device_id=peer, device_id_type=MESH)
    pl.semaphore_wait(barrier, len(peers))


def _chip_exchange_on_sequencer(parts, name):
    n = len(parts)
    hbm = pltpu.MemorySpace.HBM
    ins = [jax.new_ref(p, memory_space=hbm) for p in parts]
    outs = [jax.empty_ref(jax.ShapeDtypeStruct(p.shape, p.dtype), memory_space=hbm) for p in parts]

    @_sequencer(name, ((n, 3), (n, 3), (n,)), 2)
    def launch(send, recv, lsem):
        x, y, c = lax.axis_index("x"), lax.axis_index("y"), lax.axis_index("c")
        mine = 2 * x + y
        chips = [(1 - x, y), (x, 1 - y), (1 - x, 1 - y)]
        _handshake([(*chip, c) for chip in chips])
        local = [pltpu.make_async_copy(ins[w].at[mine], outs[w].at[mine], lsem.at[w]) for w in range(n)]
        for cp in local:
            cp.start()
        sends = []
        for w in range(n):
            for j, (px, py) in enumerate(chips):
                cp = pltpu.make_async_remote_copy(ins[w].at[2 * px + py], outs[w].at[mine],
                                                  send.at[w, j], recv.at[w, j],
                                                  device_id=(px, py, c), device_id_type=MESH)
                cp.start()
                sends.append(cp)
        for w in range(n):
            for j, (px, py) in enumerate(chips):
                pltpu.make_async_remote_copy(ins[w].at[mine], outs[w].at[2 * px + py],
                                             send.at[w, j], recv.at[w, j],
                                             device_id=(px, py, c), device_id_type=MESH).wait_recv()
        for cp in sends:
            cp.wait_send()
        for cp in local:
            cp.wait()

    launch()
    return [o[...] for o in outs]


GRAD_TILES = (dict(blocks_on="cols", tm=512, tn=1280), dict(blocks_on="cols", tm=512, tn=256),
              dict(blocks_on="cols", tm=512, tn=256), dict(blocks_on="rows", tm=256, tn=1024),
              dict(blocks_on="cols", tm=1024, tn=1024), dict(blocks_on="rows", tm=1024, tn=1024))


def _mm_tn_pair(a, b, *, blocks_on, tm, tn, name):
    t_len, m = a.shape
    n = b.shape[1]
    if blocks_on == "rows":
        rows, cols, inner = m // N_DEV, n, n // tn
        assert tm == rows
        a_spec = pl.BlockSpec((t_len, tm), lambda p, t, blk: (0, blk[p]))
        b_spec = pl.BlockSpec((t_len, tn), lambda p, t, blk: (0, t))
        out_spec = pl.BlockSpec((None, tm, tn), lambda p, t, blk: (
            jnp.maximum(p - N_CHIP, 0), 0, jnp.where(p < N_CHIP, 0, t)))
    else:
        rows, cols, inner = m, n // N_DEV, m // tm
        assert tn == cols
        a_spec = pl.BlockSpec((t_len, tm), lambda p, t, blk: (0, t))
        b_spec = pl.BlockSpec((t_len, tn), lambda p, t, blk: (0, blk[p]))
        out_spec = pl.BlockSpec((None, tm, tn), lambda p, t, blk: (
            jnp.maximum(p - N_CHIP, 0), jnp.where(p < N_CHIP, 0, t), 0))

    def body(blk_ref, a_ref, b_ref, o_ref, land, stage, send_sem, recv_sem):
        del blk_ref
        p, t = pl.program_id(0), pl.program_id(1)
        step = p * inner + t
        x, y, c = lax.axis_index("x"), lax.axis_index("y"), lax.axis_index("c")
        tile = _dot(a_ref[...], b_ref[...], TN)

        def to_sibling(slot, chip, piece):
            return pltpu.make_async_remote_copy(stage.at[slot], land.at[chip, piece], send_sem.at[slot],
                                                recv_sem.at[chip, piece],
                                                device_id=(x, y, 1 - c), device_id_type=MESH)

        @pl.when(p < N_CHIP)
        def _():
            slot = step % 2

            @pl.when(step >= 2)
            def _():
                to_sibling(slot, 0, 0).wait_send()

            stage[slot] = tile.astype(BF16)
            to_sibling(slot, p, t).start()

        @pl.when(step == N_CHIP * inner)
        def _():
            for slot in range(min(2, N_CHIP * inner)):
                to_sibling(slot, 0, 0).wait_send()

        @pl.when(p >= N_CHIP)
        def _():
            chip = p - N_CHIP
            to_sibling(0, chip, t).wait_recv()
            o_ref[...] = (tile + land[chip, t].astype(F32)).astype(BF16)

    c = lax.axis_index("c")
    order = jnp.stack([2 * ch + 1 - c for ch in range(N_CHIP)] + [2 * ch + c for ch in range(N_CHIP)])
    return pl.pallas_call(
        body, name=name,
        grid_spec=pltpu.PrefetchScalarGridSpec(
            num_scalar_prefetch=1, grid=(N_DEV, inner), in_specs=[a_spec, b_spec], out_specs=out_spec,
            scratch_shapes=[pltpu.VMEM((N_CHIP, inner, tm, tn), BF16), pltpu.VMEM((2, tm, tn), BF16),
                            pltpu.SemaphoreType.DMA((2,)), pltpu.SemaphoreType.DMA((N_CHIP, inner))]),
        out_shape=jax.ShapeDtypeStruct((N_CHIP, rows, cols), BF16),
        compiler_params=_params(("arbitrary", "arbitrary")),
    )(order.astype(jnp.int32), a, b)


def _adamw_math(g, w, m, v):
    m2 = B1 * m + (1.0 - B1) * g
    v2 = B2 * v + (1.0 - B2) * (g * g)
    delta = -LR * ((m2 / BC1) / (jnp.sqrt(v2 / BC2) + AEPS) + WD * w)
    return delta, m2, v2


def _adamw(parts, w, m, v, *, name, after=(), tr=256):
    rows, cols = w.shape

    def body(p_ref, w_ref, m_ref, v_ref, *rest):
        g_ref, d_ref, mo_ref, vo_ref = rest[len(after):]
        g = p_ref[0].astype(F32)
        for b in range(1, N_CHIP):
            g = g + p_ref[b].astype(F32)
        delta, m2, v2 = _adamw_math(g, w_ref[...], m_ref[...], v_ref[...])
        g_ref[...] = g
        d_ref[...] = delta
        mo_ref[...] = m2
        vo_ref[...] = v2

    spec = pl.BlockSpec((tr, cols), lambda i: (i, 0))
    shape = jax.ShapeDtypeStruct((rows, cols), F32)
    return pl.pallas_call(
        body, name=name, grid=(rows // tr,),
        in_specs=[pl.BlockSpec((N_CHIP, tr, cols), lambda i: (0, i, 0)), spec, spec, spec]
        + [pl.BlockSpec(memory_space=pl.ANY)] * len(after),
        out_specs=[spec] * 4, out_shape=[shape] * 4,
        compiler_params=_params(("parallel",)),
    )(parts, w, m, v, *after)


def _small_update(part, w, m, v):
    rows = part.shape[0]

    def body(p_ref, w_ref, m_ref, v_ref, g_ref, d_ref, mo_ref, vo_ref, buf, send, recv):
        me, peers = _me_and_peers()
        buf[me] = p_ref[...]
        sends = []
        for k, (dev, _) in enumerate(peers):
            cp = pltpu.make_async_remote_copy(p_ref, buf.at[me], send.at[k], recv.at[k],
                                              device_id=dev, device_id_type=MESH)
            cp.start()
            sends.append(cp)
        for k, (dev, idx) in enumerate(peers):
            pltpu.make_async_remote_copy(p_ref, buf.at[idx], send.at[k], recv.at[k],
                                         device_id=dev, device_id_type=MESH).wait_recv()
        for cp in sends:
            cp.wait_send()
        g = buf[0]
        for b in range(1, N_DEV):
            g = g + buf[b]
        delta, m2, v2 = _adamw_math(g, w_ref[...], m_ref[...], v_ref[...])
        g_ref[...] = g
        d_ref[...] = delta
        mo_ref[...] = m2
        vo_ref[...] = v2

    vm = pl.BlockSpec(memory_space=pltpu.VMEM)
    shape = jax.ShapeDtypeStruct((rows, HD), F32)
    return pl.pallas_call(
        body, name="small_params_update",
        in_specs=[vm] * 4, out_specs=[vm] * 4, out_shape=[shape] * 4,
        scratch_shapes=[pltpu.VMEM((N_DEV, rows, HD), F32),
                        pltpu.SemaphoreType.DMA((N_DEV - 1,)), pltpu.SemaphoreType.DMA((N_DEV - 1,))],
    )(part, w, m, v)


def _pack_small(norm_mix, b_gate, qa, ka, qb, kb, rpb, norm_ffn):
    gains = jnp.concatenate([qa, ka, qb, kb, jnp.zeros((4, HD), F32)], axis=0)
    rpb_pad = jnp.pad(rpb.reshape(4 * (2 * WIN_R - 1), 2 * WIN_C - 1), ((0, 4), (0, HD - (2 * WIN_C - 1))))
    return jnp.concatenate([norm_mix.reshape(16, HD), b_gate.reshape(32, HD), gains, rpb_pad,
                            norm_ffn.reshape(16, HD), jnp.zeros((8, HD), F32)], axis=0)


LOSS_ROW = 136


def _unpack_small(p):
    norm_mix = p[0:16].reshape(1, D)
    b_gate = p[16:48].reshape(1, 2 * D)
    qa, ka, qb, kb = (p[48 + i:49 + i] for i in range(4))
    rpb = p[56:116, :2 * WIN_C - 1].reshape(1, 4, 2 * WIN_R - 1, 2 * WIN_C - 1)
    norm_ffn = p[120:136].reshape(1, D)
    return norm_mix, b_gate, qa, ka, qb, kb, rpb, norm_ffn


def kernel(x, norm_mix, w_in, b_gate, q_norm_a, k_norm_a, q_norm_b, k_norm_b, rpb_b, w_proj_a, w_proj_b, w_out, norm_ffn, w_up, w_down, loss_target, m_norm_mix, m_w_in, m_b_gate, m_q_norm_a, m_k_norm_a, m_q_norm_b, m_k_norm_b, m_rpb_b, m_w_proj_a, m_w_proj_b, m_w_out, m_norm_ffn, m_w_up, m_w_down, v_norm_mix, v_w_in, v_b_gate, v_q_norm_a, v_k_norm_a, v_q_norm_b, v_k_norm_b, v_rpb_b, v_w_proj_a, v_w_proj_b, v_w_out, v_norm_ffn, v_w_up, v_w_down):
    big_w = (w_in[0], w_proj_a[0], w_proj_b[0], w_out[0], w_up[0], w_down[0])
    big_m = (m_w_in[0], m_w_proj_a[0], m_w_proj_b[0], m_w_out[0], m_w_up[0], m_w_down[0])
    big_v = (v_w_in[0], v_w_proj_a[0], v_w_proj_b[0], v_w_out[0], v_w_up[0], v_w_down[0])
    names = ("w_in", "w_proj_a", "w_proj_b", "w_out", "w_up", "w_down")

    shards = [_cast_bf16(w) for w in big_w]
    g_in, = _gather_on_sequencer(shards[0:1], "gather_w_in")
    g_pa, g_pb, g_out, g_up = _gather_on_sequencer(shards[1:5], "gather_w_mix_up")
    g_down, = _gather_on_sequencer(shards[5:6], "gather_w_down")
    small_w = _pack_small(norm_mix, b_gate, q_norm_a, k_norm_a, q_norm_b, k_norm_b, rpb_b, norm_ffn)
    small_m = _pack_small(m_norm_mix, m_b_gate, m_q_norm_a, m_k_norm_a, m_q_norm_b, m_k_norm_b, m_rpb_b, m_norm_ffn)
    small_v = _pack_small(v_norm_mix, v_b_gate, v_q_norm_a, v_k_norm_a, v_q_norm_b, v_k_norm_b, v_rpb_b, v_norm_ffn)

    upd = [None] * 6
    in_flight = {}

    def weight_grads(tag, operands):
        sums = {i: _mm_tn_pair(a, b, name=f"grad_{names[i]}", **GRAD_TILES[i]) for i, (a, b) in operands.items()}
        new = list(sums.values())
        in_flight.update(zip(sums, _chip_exchange_on_sequencer(new, f"chip_exchange_{tag}")))
        return new

    loss, grad_x, small_g = _local_step(
        x[0], loss_target[0], norm_mix, b_gate, small_w[48:56], small_w[56:120], norm_ffn,
        g_in, g_pa, g_pb, g_out.reshape(D, D), g_up, g_down.reshape(D_FF, D), weight_grads)

    g_norm_mix, g_ba, g_bb, g_gains, g_rpb, g_norm_ffn = small_g
    small_part = jnp.concatenate([g_norm_mix.reshape(16, HD), g_ba.reshape(16, HD), g_bb.reshape(16, HD),
                                  g_gains, g_rpb, g_norm_ffn.reshape(16, HD),
                                  jnp.pad(loss, ((0, 7), (0, HD - 1)))], axis=0)
    slabs = _small_update(small_part, small_w, small_m, small_v)
    total = slabs[0][LOSS_ROW, 0]
    s_g, s_d, s_m, s_v = (_unpack_small(t) for t in slabs)

    for i, r in in_flight.items():
        upd[i] = _adamw(r, big_w[i], big_m[i], big_v[i], name=f"adamw_{names[i]}", after=[grad_x])
    b_g, b_d, b_m, b_v = ([u[j][None] for u in upd] for j in range(4))

    def order(small, big):
        nm, bg, qa, ka, qb, kb, rpb, nf = small
        w_in_, pa_, pb_, out_, up_, down_ = big
        return (nm, w_in_, bg, qa, ka, qb, kb, rpb, pa_, pb_, out_, nf, up_, down_)

    return (total, grad_x[None], *order(s_g, b_g), *order(s_d, b_d), *order(s_m, b_m), *order(s_v, b_v))
```

```python
import functools

import jax
import jax.numpy as jnp
import numpy as np
from jax import lax
from jax.experimental import pallas as pl
from jax.experimental.pallas import tpu as pltpu
from jax.experimental.pallas import tpu_sc as plsc

F32 = jnp.float32
BF16 = jnp.bfloat16

N_DEV = 8
S = 2048
D = 2048
HD = 128
NH = 16
NH_A = 12
QKV = NH * HD
D_IN = 3 * QKV + 2 * D
D_BR = 512
D_FF = 4 * D
GRID_W = 64
ROWS = S // GRID_W
WIN_R = 8
WIN_C = 16
EPS = 1e-6
NEG = -1e30
SCALE = HD ** -0.5
ROPE_THETA = 10000.0
DILATIONS = (1, 4, 16)
HALF_A = 64
QB = 128

LR, B1, B2, AEPS, WD, STEP = 0.001, 0.9, 0.999, 1e-08, 0.01, 10
BC1 = 1.0 - B1 ** STEP
BC2 = 1.0 - B2 ** STEP

VMEM_LIMIT = 56 * 1024 * 1024
MESH = pl.DeviceIdType.MESH

NN = (((1,), (0,)), ((), ()))
NT = (((1,), (1,)), ((), ()))
TN = (((0,), (0,)), ((), ()))


def _params(sem):
    return pltpu.CompilerParams(dimension_semantics=sem, vmem_limit_bytes=VMEM_LIMIT)


def _matmul(a, b, *, product, grid, a_spec, b_spec, epi, out_shape, out_specs, name,
            extra=(), extra_specs=(), after=(), carried=False):
    n_extra = len(extra)

    def body(a_ref, b_ref, *rest):
        epi(product(a_ref, b_ref), rest[:n_extra], rest[n_extra + len(after):])

    return pl.pallas_call(
        body, name=name, grid=grid,
        in_specs=[a_spec, b_spec, *extra_specs, *[pl.BlockSpec(memory_space=pl.ANY)] * len(after)],
        out_specs=out_specs, out_shape=out_shape,
        compiler_params=_params(("arbitrary", "arbitrary") if carried else ("parallel", "parallel")),
    )(a, b, *extra, *after)


def _dot(x, y, dims):
    return lax.dot_general(x, y, dims, preferred_element_type=F32)


def _epi_store(acc, ex, outs):
    outs[0][...] = acc.astype(outs[0].dtype)


def _epi_residual(acc, ex, outs):
    outs[0][...] = acc + ex[0][...]


def _mm_nn(a, b3, *, tm, tn, name, out_dtypes=(F32,), epi=_epi_store, extra=(), total=False):
    m, kdim = a.shape
    g, _, ng = b3.shape
    n = g * ng
    if tn <= ng:
        npg = ng // tn
        b_spec = pl.BlockSpec((None, kdim, tn), lambda j, i: (j // npg, 0, j % npg))

        def product(a_ref, b_ref):
            return _dot(a_ref[...], b_ref[...], NN)
    else:
        gb = tn // ng
        b_spec = pl.BlockSpec((gb, kdim, ng), lambda j, i: (j, 0, 0))

        def product(a_ref, b_ref):
            return jnp.concatenate([_dot(a_ref[...], b_ref[q], NN) for q in range(gb)], axis=1)

    tile = pl.BlockSpec((tm, tn), lambda j, i: (i, j))
    shapes = [jax.ShapeDtypeStruct((m, n), dt) for dt in out_dtypes]
    specs = [tile] * len(shapes)
    if total:
        shapes.append(jax.ShapeDtypeStruct((1, 1), F32))
        specs.append(pl.BlockSpec((1, 1), lambda j, i: (0, 0)))
    single = len(shapes) == 1
    return _matmul(
        a, b3, product=product, grid=(n // tn, m // tm), epi=epi, name=name, carried=total,
        a_spec=pl.BlockSpec((tm, kdim), lambda j, i: (i, 0)), b_spec=b_spec,
        extra=extra, extra_specs=[tile] * len(extra),
        out_shape=shapes[0] if single else shapes, out_specs=specs[0] if single else specs)


def _mm_nt(a, b3, *, tm, tn, name, out_dtype=F32, epi=_epi_store, extra=(), after=()):
    m, kdim = a.shape
    g, n, kg = b3.shape

    def product(a_ref, b_ref):
        acc = _dot(a_ref[:, 0:kg], b_ref[0], NT)
        for q in range(1, g):
            acc = acc + _dot(a_ref[:, q * kg:(q + 1) * kg], b_ref[q], NT)
        return acc

    tile = pl.BlockSpec((tm, tn), lambda j, i: (i, j))
    return _matmul(
        a, b3, product=product, grid=(n // tn, m // tm), epi=epi, name=name,
        a_spec=pl.BlockSpec((tm, kdim), lambda j, i: (i, 0)),
        b_spec=pl.BlockSpec((g, tn, kg), lambda j, i: (0, j, 0)),
        extra=extra, extra_specs=[tile] * len(extra), after=after,
        out_shape=jax.ShapeDtypeStruct((m, n), out_dtype), out_specs=tile)


def _mm_tn(a, b, *, tm, tn, name, groups=1, out_dtype=BF16):
    t, m = a.shape
    _, n = b.shape
    ng = n // groups
    if tn <= ng:
        npg = ng // tn
        out_spec = pl.BlockSpec((None, tm, tn), lambda j, i: (j // npg, i, j % npg))
        epi = _epi_store

        def product(a_ref, b_ref):
            return _dot(a_ref[...], b_ref[...], TN)
    else:
        gb = tn // ng
        out_spec = pl.BlockSpec((gb, tm, ng), lambda j, i: (j, i, 0))

        def product(a_ref, b_ref):
            return [_dot(a_ref[...], b_ref[:, q * ng:(q + 1) * ng], TN) for q in range(gb)]

        def epi(parts, ex, outs):
            for q, part in enumerate(parts):
                outs[0][q] = part.astype(out_dtype)

    return _matmul(
        a, b, product=product, grid=(n // tn, m // tm), epi=epi, name=name,
        a_spec=pl.BlockSpec((t, tm), lambda j, i: (0, i)),
        b_spec=pl.BlockSpec((t, tn), lambda j, i: (0, j)),
        out_shape=jax.ShapeDtypeStruct((groups, m, ng), out_dtype), out_specs=out_spec)


def _rms_fwd(x, g, *, name, tr=256):
    def body(x_ref, g_ref, y_ref, r_ref):
        xv = x_ref[...]
        r = lax.rsqrt(jnp.mean(xv * xv, axis=-1, keepdims=True) + EPS)
        y_ref[...] = (xv * r * g_ref[...]).astype(BF16)
        r_ref[...] = r

    row = pl.BlockSpec((tr, D), lambda i: (i, 0))
    return pl.pallas_call(
        body, name=name, grid=(S // tr,),
        in_specs=[row, pl.BlockSpec((1, D), lambda i: (0, 0))],
        out_specs=[row, pl.BlockSpec((tr, 1), lambda i: (i, 0))],
        out_shape=[jax.ShapeDtypeStruct((S, D), BF16), jax.ShapeDtypeStruct((S, 1), F32)],
        compiler_params=_params(("parallel",)),
    )(x, g)


def _rms_bwd(dy, x, rstd, g, resid, *, name, tr=256):
    def body(dy_ref, x_ref, r_ref, g_ref, res_ref, dx_ref, dxb_ref, dg_ref):
        r = r_ref[...]
        xh = x_ref[...] * r
        dyv = dy_ref[...]
        t = dyv * g_ref[...]
        dx = r * (t - xh * jnp.mean(t * xh, axis=-1, keepdims=True)) + res_ref[...]
        dx_ref[...] = dx
        dxb_ref[...] = dx.astype(BF16)
        part = jnp.sum(dyv * xh, axis=0, keepdims=True)

        @pl.when(pl.program_id(0) == 0)
        def _():
            dg_ref[...] = part

        @pl.when(pl.program_id(0) > 0)
        def _():
            dg_ref[...] += part

    row = pl.BlockSpec((tr, D), lambda i: (i, 0))
    vec = pl.BlockSpec((1, D), lambda i: (0, 0))
    return pl.pallas_call(
        body, name=name, grid=(S // tr,),
        in_specs=[row, row, pl.BlockSpec((tr, 1), lambda i: (i, 0)), vec, row],
        out_specs=[row, row, vec],
        out_shape=[jax.ShapeDtypeStruct((S, D), F32), jax.ShapeDtypeStruct((S, D), BF16),
                   jax.ShapeDtypeStruct((1, D), F32)],
        compiler_params=_params(("arbitrary",)),
    )(dy, x, rstd, g, resid)


def _rope_tables():
    pos = np.arange(S, dtype=np.float32)
    inv = (ROPE_THETA ** (-np.arange(0, HD, 2, dtype=np.float32) / HD)).astype(np.float32)
    ang = pos[:, None] * inv[None, :]
    cos, sin = np.cos(ang), np.sin(ang)
    return (jnp.asarray(np.concatenate([cos, cos], axis=-1), F32),
            jnp.asarray(np.concatenate([-sin, sin], axis=-1), F32))


def _swap_halves(t):
    return pltpu.roll(t, HD // 2, axis=1)


TOK = 256


def _lane_block_spec(d, last=HD):
    return pl.BlockSpec((4, TOK // d, d * last), lambda i: (0, i, 0))


def _to_lane_blocks(dst, head, val, d, scr, dtype):
    w = val.shape[1]
    if d == 1:
        dst[head] = val.astype(dtype)
        return
    scr[...] = val
    for r in range(d):
        dst[head, :, r * w:(r + 1) * w] = scr[pl.ds(r, TOK // d, stride=d), :].astype(dtype)


def _from_lane_blocks(src, head, d, w, scr):
    if d == 1:
        return src[head].astype(F32)
    for r in range(d):
        scr[pl.ds(r, TOK // d, stride=d), :] = src[head, :, r * w:(r + 1) * w].astype(F32)
    return scr[...]


def _qk_prep(proj, gains, cos2, sin2):
    def body(q_ref, k_ref, v_ref, g_ref, c_ref, s_ref, *rest):
        outs, scr = rest[:-1], rest[-1]
        cos, sin = c_ref[...], s_ref[...]
        for which, (src, row_a, row_b) in enumerate(((q_ref, 0, 2), (k_ref, 1, 3), (v_ref, None, None))):
            for h in range(NH):
                y = src[:, h * HD:(h + 1) * HD]
                if row_a is not None:
                    y = y * lax.rsqrt(jnp.mean(y * y, axis=-1, keepdims=True) + EPS)
                    if h < NH_A:
                        y = y * g_ref[row_a:row_a + 1, :]
                        y = y * cos + _swap_halves(y) * sin
                    else:
                        y = y * g_ref[row_b:row_b + 1, :]
                if h < NH_A:
                    gi = h // 4
                    _to_lane_blocks(outs[3 * gi + which], h % 4, y, DILATIONS[gi], scr, BF16)
                else:
                    hb = h - NH_A
                    outs[9 + which][:, hb * HD:(hb + 1) * HD] = y.astype(BF16)

    def blk(c):
        return pl.BlockSpec((TOK, QKV), lambda i: (i, c))
    tab = pl.BlockSpec((TOK, HD), lambda i: (i, 0))
    out_specs, out_shape = [], []
    for d in DILATIONS:
        out_specs += [_lane_block_spec(d)] * 3
        out_shape += [jax.ShapeDtypeStruct((4, S // d, d * HD), BF16)] * 3
    out_specs += [pl.BlockSpec((TOK, D_BR), lambda i: (i, 0))] * 3
    out_shape += [jax.ShapeDtypeStruct((S, D_BR), BF16)] * 3
    outs = pl.pallas_call(
        body, name="qk_prep", grid=(S // TOK,),
        in_specs=[blk(0), blk(1), blk(2), pl.BlockSpec((8, HD), lambda i: (0, 0)), tab, tab],
        out_specs=out_specs, out_shape=out_shape,
        scratch_shapes=[pltpu.VMEM((TOK, HD), F32)],
        compiler_params=_params(("parallel",)),
    )(proj, proj, proj, gains, cos2, sin2)
    return [tuple(outs[3 * gi:3 * gi + 3]) for gi in range(3)], tuple(outs[9:12])


def _qk_prep_bwd(dproj, proj, gains, cos2, sin2, grads_a, grads_b):
    def body(dp_in, q_ref, k_ref, g_ref, c_ref, s_ref, *rest):
        grads, (dp_out, dg_ref, scr) = rest[:12], rest[12:]
        del dp_in
        cos, sin = c_ref[...], s_ref[...]

        def grad_of(which, h):
            if h < NH_A:
                gi = h // 4
                return _from_lane_blocks(grads[3 * gi + which], h % 4, DILATIONS[gi], HD, scr)
            hb = h - NH_A
            return grads[9 + which][:, hb * HD:(hb + 1) * HD]

        dg_rows = []
        for which, (src, base, row_a, row_b) in enumerate(((q_ref, 0, 0, 2), (k_ref, QKV, 1, 3))):
            dg_a = jnp.zeros((1, HD), F32)
            dg_b = jnp.zeros((1, HD), F32)
            for h in range(NH):
                t = src[:, h * HD:(h + 1) * HD]
                dy = grad_of(which, h)
                r = lax.rsqrt(jnp.mean(t * t, axis=-1, keepdims=True) + EPS)
                xh = t * r
                if h < NH_A:
                    dy = dy * cos - _swap_halves(dy) * sin
                    gain = g_ref[row_a:row_a + 1, :]
                    dg_a = dg_a + jnp.sum(dy * xh, axis=0, keepdims=True)
                else:
                    gain = g_ref[row_b:row_b + 1, :]
                    dg_b = dg_b + jnp.sum(dy * xh, axis=0, keepdims=True)
                u = dy * gain
                dx = r * (u - xh * jnp.mean(u * xh, axis=-1, keepdims=True))
                dp_out[:, base + h * HD:base + (h + 1) * HD] = dx.astype(BF16)
            dg_rows += [(row_a, dg_a), (row_b, dg_b)]
        for h in range(NH):
            dp_out[:, 2 * QKV + h * HD:2 * QKV + (h + 1) * HD] = grad_of(2, h).astype(BF16)

        @pl.when(pl.program_id(0) == 0)
        def _():
            dg_ref[...] = jnp.zeros((8, HD), F32)

        for row, val in dg_rows:
            dg_ref[row:row + 1, :] += val

    def blk(c):
        return pl.BlockSpec((TOK, QKV), lambda i: (i, c))
    tab = pl.BlockSpec((TOK, HD), lambda i: (i, 0))
    gain_spec = pl.BlockSpec((8, HD), lambda i: (0, 0))
    grad_specs = [s for d in DILATIONS for s in [_lane_block_spec(d)] * 3]
    grad_specs += [pl.BlockSpec((TOK, D_BR), lambda i: (i, 0))] * 3
    return pl.pallas_call(
        body, name="qk_prep_bwd", grid=(S // TOK,),
        in_specs=[pl.BlockSpec(memory_space=pl.ANY), blk(0), blk(1), gain_spec, tab, tab] + grad_specs,
        out_specs=[pl.BlockSpec((TOK, 3 * QKV), lambda i: (i, 0)), gain_spec],
        out_shape=[jax.ShapeDtypeStruct((S, D_IN), BF16), jax.ShapeDtypeStruct((8, HD), F32)],
        input_output_aliases={0: 0},
        scratch_shapes=[pltpu.VMEM((TOK, HD), F32)],
        compiler_params=_params(("arbitrary",)),
    )(dproj, proj, proj, gains, cos2, sin2, *[g for grp in grads_a for g in grp], *grads_b)


def _mix_fwd(oa, ob, w_pa, w_pb, proj, b_gate, *, tr=256):
    def body(oa_ref, ob_ref, pa_ref, pb_ref, la_ref, lb_ref, ba_ref, bb_ref, mix_ref, ya_ref, yb_ref):
        ya = jnp.concatenate([_dot(oa_ref[...], pa_ref[q], NN) for q in range(N_DEV)], axis=1)
        yb = jnp.concatenate([_dot(ob_ref[...], pb_ref[q], NN) for q in range(N_DEV)], axis=1)
        ga = jax.nn.sigmoid(la_ref[...] + ba_ref[...])
        gb = jax.nn.sigmoid(lb_ref[...] + bb_ref[...])
        mix_ref[...] = (ga * ya + gb * yb).astype(BF16)
        ya_ref[...] = ya.astype(BF16)
        yb_ref[...] = yb.astype(BF16)

    row = pl.BlockSpec((tr, D), lambda i: (i, 0))
    branch = pl.BlockSpec((tr, D_BR), lambda i: (i, 0))
    whole = pl.BlockSpec((N_DEV, D_BR, D // N_DEV), lambda i: (0, 0, 0))
    return pl.pallas_call(
        body, name="mix_fwd", grid=(S // tr,),
        in_specs=[branch, branch, whole, whole,
                  pl.BlockSpec((tr, D), lambda i: (i, 3)), pl.BlockSpec((tr, D), lambda i: (i, 4)),
                  pl.BlockSpec((1, D), lambda i: (0, 0)), pl.BlockSpec((1, D), lambda i: (0, 1))],
        out_specs=[row, row, row], out_shape=[jax.ShapeDtypeStruct((S, D), BF16)] * 3,
        compiler_params=_params(("parallel",)),
    )(oa, ob, w_pa, w_pb, proj, proj, b_gate, b_gate)


def _gate_bwd(branch, dmixed, proj, b_gate, y, dproj, *, tr=256):
    aliased = dproj is not None

    def body(dm_ref, l_ref, b_ref, y_ref, *rest):
        dy_ref, dp_ref, db_ref = rest[-3:]
        g = jax.nn.sigmoid(l_ref[...] + b_ref[...])
        dm = dm_ref[...]
        dy_ref[...] = (dm * g).astype(BF16)
        dl = dm * y_ref[...].astype(F32) * g * (1.0 - g)
        dp_ref[...] = dl.astype(BF16)
        part = jnp.sum(dl, axis=0, keepdims=True)

        @pl.when(pl.program_id(0) == 0)
        def _():
            db_ref[...] = part

        @pl.when(pl.program_id(0) > 0)
        def _():
            db_ref[...] += part

    row = pl.BlockSpec((tr, D), lambda i: (i, 0))
    col = pl.BlockSpec((tr, D), lambda i: (i, 3 + branch))
    vec = pl.BlockSpec((1, D), lambda i: (0, 0))
    return pl.pallas_call(
        body, name=f"gate_bwd_{branch}", grid=(S // tr,),
        in_specs=[row, col, pl.BlockSpec((1, D), lambda i: (0, branch)), row]
        + ([pl.BlockSpec(memory_space=pl.ANY)] if aliased else []),
        out_specs=[row, col, vec],
        out_shape=[jax.ShapeDtypeStruct((S, D), BF16), jax.ShapeDtypeStruct((S, D_IN), BF16),
                   jax.ShapeDtypeStruct((1, D), F32)],
        input_output_aliases={4: 1} if aliased else {},
        compiler_params=_params(("arbitrary",)),
    )(dmixed, proj, b_gate, y, *([dproj] if aliased else []))


def _band_blocks(m_len):
    wk = min(m_len, QB + 2 * QB)
    return [(qb * QB, min(max(qb * QB - QB, 0), m_len - wk), wk) for qb in range(m_len // QB)]


def _band_scores(q, kw, q0, k0, wk):
    s = _dot(q, kw, NT) * SCALE
    qpos = q0 + lax.broadcasted_iota(jnp.int32, (QB, 1), 0)
    kpos = k0 + lax.broadcasted_iota(jnp.int32, (1, wk), 1)
    return jnp.where(jnp.abs(kpos - qpos) <= HALF_A, s, NEG)


def _attn_a_fwd(q, k, v, gi):
    d = DILATIONS[gi]
    m_len = S // d

    def body(q_ref, k_ref, v_ref, o_ref, lse_ref):
        for r in range(d):
            lanes = slice(r * HD, (r + 1) * HD)
            for q0, k0, wk in _band_blocks(m_len):
                s = _band_scores(q_ref[q0:q0 + QB, lanes], k_ref[k0:k0 + wk, lanes], q0, k0, wk)
                m = jnp.max(s, axis=-1, keepdims=True)
                p = jnp.exp(s - m)
                l = jnp.sum(p, axis=-1, keepdims=True)
                o_ref[q0:q0 + QB, lanes] = _dot(p.astype(BF16), v_ref[k0:k0 + wk, lanes], NN) / l
                lse_ref[q0:q0 + QB, r:r + 1] = m + jnp.log(l)

    head = pl.BlockSpec((None, m_len, d * HD), lambda h: (h, 0, 0))
    stat = pl.BlockSpec((None, m_len, d), lambda h: (h, 0, 0))
    return pl.pallas_call(
        body, name=f"attn_a_fwd_{gi}", grid=(4,),
        in_specs=[head, head, head], out_specs=[head, stat],
        out_shape=[jax.ShapeDtypeStruct((4, m_len, d * HD), F32), jax.ShapeDtypeStruct((4, m_len, d), F32)],
        compiler_params=_params(("parallel",)),
    )(q, k, v)


def _combine_a(os, lses):
    def body(o0, o1, o2, l0, l1, l2, oa_ref, lse_ref, scr, scr1):
        for h in range(4):
            o = [_from_lane_blocks(ref, h, d, HD, scr) for ref, d in zip((o0, o1, o2), DILATIONS)]
            a, b, c = (_from_lane_blocks(ref, h, d, 1, scr1) for ref, d in zip((l0, l1, l2), DILATIONS))
            m = jnp.maximum(jnp.maximum(a, b), c)
            wa, wb, wc = jnp.exp(a - m), jnp.exp(b - m), jnp.exp(c - m)
            tot = wa + wb + wc
            oa_ref[:, h * HD:(h + 1) * HD] = ((wa * o[0] + wb * o[1] + wc * o[2]) / tot).astype(BF16)
            lse_ref[h] = m + jnp.log(tot)

    return pl.pallas_call(
        body, name="combine_a", grid=(S // TOK,),
        in_specs=[_lane_block_spec(d) for d in DILATIONS] + [_lane_block_spec(d, 1) for d in DILATIONS],
        out_specs=[pl.BlockSpec((TOK, D_BR), lambda i: (i, 0)), pl.BlockSpec((4, TOK, 1), lambda i: (0, i, 0))],
        out_shape=[jax.ShapeDtypeStruct((S, D_BR), BF16), jax.ShapeDtypeStruct((4, S, 1), F32)],
        scratch_shapes=[pltpu.VMEM((TOK, HD), F32), pltpu.VMEM((TOK, 1), F32)],
        compiler_params=_params(("parallel",)),
    )(*os, *lses)


def _attn_a_bwd_prep(doa, oa, lse):
    def body(do_ref, o_ref, l_ref, *rest):
        outs, (scr, scr1) = rest[:9], rest[9:]
        for h in range(4):
            do = do_ref[:, h * HD:(h + 1) * HD]
            dsum = jnp.sum(do * o_ref[:, h * HD:(h + 1) * HD].astype(F32), axis=-1, keepdims=True)
            for gi, d in enumerate(DILATIONS):
                _to_lane_blocks(outs[3 * gi], h, do, d, scr, BF16)
                _to_lane_blocks(outs[3 * gi + 1], h, l_ref[h], d, scr1, F32)
                _to_lane_blocks(outs[3 * gi + 2], h, dsum, d, scr1, F32)

    row = pl.BlockSpec((TOK, D_BR), lambda i: (i, 0))
    out_specs, out_shape = [], []
    for d in DILATIONS:
        out_specs += [_lane_block_spec(d), _lane_block_spec(d, 1), _lane_block_spec(d, 1)]
        out_shape += [jax.ShapeDtypeStruct((4, S // d, d * HD), BF16)] + [jax.ShapeDtypeStruct((4, S // d, d), F32)] * 2
    outs = pl.pallas_call(
        body, name="attn_a_bwd_prep", grid=(S // TOK,),
        in_specs=[row, row, pl.BlockSpec((4, TOK, 1), lambda i: (0, i, 0))],
        out_specs=out_specs, out_shape=out_shape,
        scratch_shapes=[pltpu.VMEM((TOK, HD), F32), pltpu.VMEM((TOK, 1), F32)],
        compiler_params=_params(("parallel",)),
    )(doa, oa, lse)
    return [tuple(outs[3 * gi:3 * gi + 3]) for gi in range(3)]


def _attn_a_bwd(q, k, v, do, lse, dsum, gi):
    d = DILATIONS[gi]
    m_len = S // d

    def body(q_ref, k_ref, v_ref, do_ref, lse_ref, dsum_ref, dq_ref, dk_ref, dv_ref):
        dk_ref[...] = jnp.zeros((m_len, d * HD), F32)
        dv_ref[...] = jnp.zeros((m_len, d * HD), F32)
        for r in range(d):
            lanes = slice(r * HD, (r + 1) * HD)
            for q0, k0, wk in _band_blocks(m_len):
                rows, keys = slice(q0, q0 + QB), slice(k0, k0 + wk)
                qv, kw, vw, dov = q_ref[rows, lanes], k_ref[keys, lanes], v_ref[keys, lanes], do_ref[rows, lanes]
                p = jnp.exp(_band_scores(qv, kw, q0, k0, wk) - lse_ref[rows, r:r + 1])
                ds = (p * (_dot(dov, vw, NT) - dsum_ref[rows, r:r + 1]) * SCALE).astype(BF16)
                dq_ref[rows, lanes] = _dot(ds, kw, NN)
                dk_ref[keys, lanes] += _dot(ds, qv, TN)
                dv_ref[keys, lanes] += _dot(p.astype(BF16), dov, TN)

    head = pl.BlockSpec((None, m_len, d * HD), lambda h: (h, 0, 0))
    stat = pl.BlockSpec((None, m_len, d), lambda h: (h, 0, 0))
    shape = jax.ShapeDtypeStruct((4, m_len, d * HD), F32)
    return pl.pallas_call(
        body, name=f"attn_a_bwd_{gi}", grid=(4,),
        in_specs=[head, head, head, head, stat, stat], out_specs=[head, head, head],
        out_shape=[shape, shape, shape],
        compiler_params=_params(("parallel",)),
    )(q, k, v, do, lse, dsum)


KEYS_B = WIN_R * GRID_W
N_OFF = WIN_R


def _bias_constants():
    q = np.arange(GRID_W)[:, None]
    kc = np.arange(GRID_W)[None, :]
    dc = np.clip(kc - q, -(WIN_C - 1), WIN_C - 1) + (WIN_C - 1)
    expand = np.zeros((HD, GRID_W * GRID_W), np.float32)
    expand[dc.reshape(-1), np.arange(GRID_W * GRID_W)] = 1.0
    cs = np.clip(q - WIN_C // 2, 0, GRID_W - WIN_C)
    keep = ((kc >= cs) & (kc < cs + WIN_C)).reshape(1, -1).astype(np.float32)
    sel = np.zeros((64, 4 * N_OFF * WIN_R), np.float32)
    for h in range(4):
        for off in range(N_OFF):
            for j in range(WIN_R):
                sel[h * (2 * WIN_R - 1) + off + j, (h * N_OFF + off) * WIN_R + j] = 1.0
    return jnp.asarray(expand), jnp.asarray(keep), jnp.asarray(sel)


def _bias_expand(rpb_pad, expand, keep, sel):
    def body(r_ref, e_ref, k_ref, s_ref, o_ref):
        t = lax.dot_general(r_ref[...], e_ref[...], NN, precision=lax.Precision.HIGHEST,
                            preferred_element_type=F32)
        rows = lax.dot_general(s_ref[...], t, TN, precision=lax.Precision.HIGHEST,
                               preferred_element_type=F32)
        o_ref[...] = jnp.where(k_ref[...] > 0.5, rows, NEG)

    return pl.pallas_call(
        body, name="bias_expand",
        out_shape=jax.ShapeDtypeStruct((4 * N_OFF * WIN_R, GRID_W * GRID_W), F32),
        compiler_params=pltpu.CompilerParams(vmem_limit_bytes=VMEM_LIMIT),
    )(rpb_pad, expand, keep, sel)


def _bias_reduce(dbias_rows, expand, sel):
    def body(x_ref, e_ref, s_ref, o_ref):
        z = lax.dot_general(x_ref[...], e_ref[...], NT, precision=lax.Precision.HIGHEST,
                            preferred_element_type=F32)
        o_ref[...] = lax.dot_general(s_ref[...], z, NN, precision=lax.Precision.HIGHEST,
                                     preferred_element_type=F32)

    return pl.pallas_call(
        body, name="bias_reduce", out_shape=jax.ShapeDtypeStruct((64, HD), F32),
        compiler_params=pltpu.CompilerParams(vmem_limit_bytes=VMEM_LIMIT),
    )(dbias_rows, expand, sel)


def _rows_to_tab(rows):
    t = rows.reshape(4, N_OFF, WIN_R, GRID_W, GRID_W)
    return t.transpose(0, 1, 3, 2, 4).reshape(4, N_OFF, GRID_W, KEYS_B)


def _tab_to_rows(tab):
    t = tab.reshape(4, N_OFF, GRID_W, WIN_R, GRID_W)
    return t.transpose(0, 1, 3, 2, 4).reshape(4 * N_OFF * WIN_R, GRID_W * GRID_W)


def _row_window(r):
    r0 = jnp.clip(r - WIN_R // 2, 0, ROWS - WIN_R)
    off = r0 + (WIN_R - 1) - r
    return pl.multiple_of(r * GRID_W, GRID_W), pl.multiple_of(r0 * GRID_W, GRID_W), off


def _attn_b_fwd(qn, kn, vb, bias_tab):
    def body(q_ref, k_ref, v_ref, b_ref, o_ref, lse_ref):
        def row(r, carry):
            qs, ks, off = _row_window(r)
            q = q_ref[pl.ds(qs, GRID_W), :]
            s = lax.dot_general(q, k_ref[pl.ds(ks, KEYS_B), :], NT, preferred_element_type=F32) * SCALE
            s = s + b_ref[off]
            m = jnp.max(s, axis=-1, keepdims=True)
            p = jnp.exp(s - m)
            l = jnp.sum(p, axis=-1, keepdims=True)
            o = lax.dot_general(p.astype(BF16), v_ref[pl.ds(ks, KEYS_B), :], NN, preferred_element_type=F32)
            o_ref[pl.ds(qs, GRID_W), :] = (o / l).astype(BF16)
            lse_ref[pl.ds(qs, GRID_W), :] = m + jnp.log(l)
            return carry

        lax.fori_loop(0, ROWS, row, 0)

    full = pl.BlockSpec((S, HD), lambda h: (0, h))
    return pl.pallas_call(
        body, name="attn_b_fwd", grid=(4,),
        in_specs=[full, full, full, pl.BlockSpec((None, N_OFF, GRID_W, KEYS_B), lambda h: (h, 0, 0, 0))],
        out_specs=[pl.BlockSpec((S, HD), lambda h: (0, h)), pl.BlockSpec((None, S, 1), lambda h: (h, 0, 0))],
        out_shape=[jax.ShapeDtypeStruct((S, D_BR), BF16), jax.ShapeDtypeStruct((4, S, 1), F32)],
        compiler_params=_params(("parallel",)),
    )(qn, kn, vb, bias_tab)


def _attn_b_bwd(qn, kn, vb, bias_tab, ob, dob, lse):
    def body(q_ref, k_ref, v_ref, b_ref, o_ref, do_ref, lse_ref, dq_ref, dk_ref, dv_ref, db_ref):
        dk_ref[...] = jnp.zeros((S, HD), F32)
        dv_ref[...] = jnp.zeros((S, HD), F32)
        db_ref[...] = jnp.zeros((N_OFF, GRID_W, KEYS_B), F32)

        def row(r, carry):
            qs, ks, off = _row_window(r)
            rows = pl.ds(qs, GRID_W)
            keys = pl.ds(ks, KEYS_B)
            q = q_ref[rows, :]
            kw = k_ref[keys, :]
            s = lax.dot_general(q, kw, NT, preferred_element_type=F32) * SCALE + b_ref[off]
            p = jnp.exp(s - lse_ref[rows, :])
            do = do_ref[rows, :]
            dobf = do.astype(BF16)
            dsum = jnp.sum(do * o_ref[rows, :].astype(F32), axis=-1, keepdims=True)
            dp = lax.dot_general(dobf, v_ref[keys, :], NT, preferred_element_type=F32)
            ds = p * (dp - dsum)
            db_ref[off] += ds
            dsb = (ds * SCALE).astype(BF16)
            dq_ref[rows, :] = lax.dot_general(dsb, kw, NN, preferred_element_type=F32)
            dk_ref[keys, :] += lax.dot_general(dsb, q, TN, preferred_element_type=F32)
            dv_ref[keys, :] += lax.dot_general(p.astype(BF16), dobf, TN, preferred_element_type=F32)
            return carry

        lax.fori_loop(0, ROWS, row, 0)

    full = pl.BlockSpec((S, HD), lambda h: (0, h))
    slot = pl.BlockSpec((S, HD), lambda h: (0, h))
    tab = pl.BlockSpec((None, N_OFF, GRID_W, KEYS_B), lambda h: (h, 0, 0, 0))
    shape = jax.ShapeDtypeStruct((S, D_BR), F32)
    return pl.pallas_call(
        body, name="attn_b_bwd", grid=(4,),
        in_specs=[full, full, full, tab, slot, slot, pl.BlockSpec((None, S, 1), lambda h: (h, 0, 0))],
        out_specs=[slot, slot, slot, tab],
        out_shape=[shape, shape, shape, jax.ShapeDtypeStruct((4, N_OFF, GRID_W, KEYS_B), F32)],
        compiler_params=_params(("parallel",)),
    )(qn, kn, vb, bias_tab, ob, dob, lse)


def _epi_relu_sq(acc, ex, outs):
    u = jnp.maximum(acc, 0.0)
    outs[0][...] = u.astype(BF16)
    outs[1][...] = (u * u).astype(BF16)


def _epi_relu_sq_bwd(acc, ex, outs):
    outs[0][...] = (acc * (2.0 * ex[0][...].astype(F32))).astype(BF16)


def _epi_loss_head(acc, ex, outs):
    e = acc + ex[0][...] - ex[1][...]
    dy = e * (1.0 / D)
    outs[0][...] = dy
    outs[1][...] = dy.astype(BF16)
    part = (0.5 / D) * jnp.sum(jnp.sum(e * e, axis=-1, keepdims=True), axis=0, keepdims=True)
    first = (pl.program_id(0) == 0) & (pl.program_id(1) == 0)

    @pl.when(first)
    def _():
        outs[2][...] = part

    @pl.when(jnp.logical_not(first))
    def _():
        outs[2][...] += part


def _local_step(x, target, norm_mix, b_gate, gains, rpb_pad, norm_ffn,
                w_in, w_pa, w_pb, w_out, w_up, w_down, weight_grads):
    cos2, sin2 = _rope_tables()
    expand, keep, sel = _bias_constants()
    w_out3, w_down3 = w_out[None], w_down[None]

    xn, rstd1 = _rms_fwd(x, norm_mix, name="rms_mix")
    proj = _mm_nn(xn, w_in, tm=1024, tn=1280, name="proj")
    qkv_a, qkv_b = _qk_prep(proj, gains, cos2, sin2)
    fwd_a = [_attn_a_fwd(*qkv_a[gi], gi) for gi in range(3)]
    oa, lse_a = _combine_a([o for o, _ in fwd_a], [l for _, l in fwd_a])
    bias_tab = _rows_to_tab(_bias_expand(rpb_pad, expand, keep, sel))
    ob, lse_b = _attn_b_fwd(*qkv_b, bias_tab)
    mixed, ya, yb = _mix_fwd(oa, ob, w_pa, w_pb, proj, b_gate)
    h1 = _mm_nn(mixed, w_out3, tm=1024, tn=1024, name="out_proj", epi=_epi_residual, extra=(x,))
    hn, rstd2 = _rms_fwd(h1, norm_ffn, name="rms_ffn")
    u, usq = _mm_nn(hn, w_up, tm=1024, tn=1024, name="ffn_up", epi=_epi_relu_sq,
                    out_dtypes=(BF16, BF16))
    dy, dyb, loss = _mm_nn(usq, w_down3, tm=512, tn=512, name="ffn_down", epi=_epi_loss_head,
                           extra=(h1, target), out_dtypes=(F32, BF16), total=True)

    sent = weight_grads("w_down", {5: (usq, dyb)})
    du = _mm_nt(dyb, w_down3, tm=1024, tn=1024, name="ffn_down_bwd", out_dtype=BF16,
                epi=_epi_relu_sq_bwd, extra=(u,), after=sent)
    sent = weight_grads("w_up", {4: (hn, du)})
    dhn = _mm_nt(du, w_up, tm=512, tn=512, name="ffn_up_bwd", after=sent)
    dh1, dh1b, g_norm_ffn = _rms_bwd(dhn, h1, rstd2, norm_ffn, dy, name="rms_ffn_bwd")

    dmixed = _mm_nt(dh1b, w_out3, tm=1024, tn=1024, name="out_proj_bwd")
    dya, dproj, g_ba = _gate_bwd(0, dmixed, proj, b_gate, ya, None)
    dyb2, dproj, g_bb = _gate_bwd(1, dmixed, proj, b_gate, yb, dproj)
    sent = weight_grads("w_mix", {3: (mixed, dh1b), 1: (oa, dya), 2: (ob, dyb2)})
    doa = _mm_nt(dya, w_pa, tm=1024, tn=D_BR, name="proj_a_bwd", after=sent)
    dob = _mm_nt(dyb2, w_pb, tm=1024, tn=D_BR, name="proj_b_bwd")
    prep = _attn_a_bwd_prep(doa, oa, lse_a)
    grads_a = [_attn_a_bwd(*qkv_a[gi], *prep[gi], gi) for gi in range(3)]
    dqb, dkb, dvb, dbias = _attn_b_bwd(*qkv_b, bias_tab, ob, dob, lse_b)
    g_rpb = _bias_reduce(_tab_to_rows(dbias), expand, sel)
    dproj, g_gains = _qk_prep_bwd(dproj, proj, gains, cos2, sin2, grads_a, (dqb, dkb, dvb))
    sent = weight_grads("w_in", {0: (xn, dproj)})
    dxn = _mm_nt(dproj, w_in, tm=256, tn=512, name="proj_bwd", after=sent)
    grad_x, _, g_norm_mix = _rms_bwd(dxn, x, rstd1, norm_mix, dh1, name="rms_mix_bwd")

    small = (g_norm_mix, g_ba, g_bb, g_gains, g_rpb, g_norm_ffn)
    return loss, grad_x, small


def _cast_bf16(w, *, tr=256):
    rows, cols = w.shape
    tr = min(tr, rows)

    def body(w_ref, o_ref):
        o_ref[...] = w_ref[...].astype(BF16)

    spec = pl.BlockSpec((tr, cols), lambda i: (i, 0))
    return pl.pallas_call(
        body, name=f"cast_{rows}x{cols}", grid=(rows // tr,), in_specs=[spec], out_specs=spec,
        out_shape=jax.ShapeDtypeStruct((rows, cols), BF16), compiler_params=_params(("parallel",)),
    )(w)


def _me_and_peers():
    x, y, c = lax.axis_index("x"), lax.axis_index("y"), lax.axis_index("c")
    me = 4 * x + 2 * y + c
    peers = []
    for k in range(1, N_DEV):
        px = 1 - x if k & 4 else x
        py = 1 - y if k & 2 else y
        pc = 1 - c if k & 1 else c
        peers.append(((px, py, pc), 4 * px + 2 * py + pc))
    return me, peers


def _gather_on_sequencer(shards, name):
    n = len(shards)
    hbm = pltpu.MemorySpace.HBM
    ins = [jax.new_ref(s, memory_space=hbm) for s in shards]
    outs = [jax.empty_ref(jax.ShapeDtypeStruct((N_DEV,) + s.shape, s.dtype), memory_space=hbm) for s in shards]

    @pl.kernel(mesh=plsc.ScalarSubcoreMesh(axis_name="seq", num_cores=1), name=name,
               scratch_types=(pltpu.SemaphoreType.DMA((n, N_DEV - 1)), pltpu.SemaphoreType.DMA((n, N_DEV - 1)),
                              pltpu.SemaphoreType.DMA((n,))),
               compiler_params=pltpu.CompilerParams(collective_id=0))
    def launch(send, recv, lsem):
        x, y, c = lax.axis_index("x"), lax.axis_index("y"), lax.axis_index("c")
        me, sibling = (x, y, c), (x, y, 1 - c)
        chips = [(1 - x, y), (x, 1 - y), (1 - x, 1 - y)]
        barrier = pltpu.get_barrier_semaphore()
        for peer in [sibling] + [(*chip, c) for chip in chips]:
            pl.semaphore_signal(barrier, inc=1, device_id=peer, device_id_type=MESH)
        pl.semaphore_wait(barrier, 4)

        def copy(w, k, block, to, src=None):
            px, py, pc = block
            dst = outs[w].at[4 * px + 2 * py + pc]
            return pltpu.make_async_remote_copy(dst if src is None else src, dst, send.at[w, k], recv.at[w, k],
                                                device_id=to, device_id_type=MESH)

        local = [pltpu.make_async_copy(ins[w], outs[w].at[4 * x + 2 * y + c], lsem.at[w]) for w in range(n)]
        for cp in local:
            cp.start()
        first = []
        for w in range(n):
            first += [copy(w, 1 + j, me, (*chip, c), src=ins[w]) for j, chip in enumerate(chips)]
            first.append(copy(w, 0, me, sibling, src=ins[w]))
        for cp in first:
            cp.start()
        passed = []
        for w in range(n):
            for j, chip in enumerate(chips):
                copy(w, 1 + j, (*chip, c), me).wait_recv()
                cp = copy(w, 4 + j, (*chip, c), sibling)
                cp.start()
                passed.append(cp)
        for w in range(n):
            copy(w, 0, sibling, me).wait_recv()
            for j, chip in enumerate(chips):
                copy(w, 4 + j, (*chip, 1 - c), me).wait_recv()
        for cp in first + passed:
            cp.wait_send()
        for cp in local:
            cp.wait()

    launch()
    return [o[...] for o in outs]


N_CHIP = 4
CHIPS = ((0, 0), (0, 1), (1, 0), (1, 1))


def _sequencer(name, n_sems, collective_id):
    return functools.partial(
        pl.kernel, mesh=plsc.ScalarSubcoreMesh(axis_name="seq", num_cores=1), name=name,
        scratch_types=tuple(pltpu.SemaphoreType.DMA(s) for s in n_sems),
        compiler_params=pltpu.CompilerParams(collective_id=collective_id))


def _handshake(peers):
    barrier = pltpu.get_barrier_semaphore()
    for peer in peers:
        pl.semaphore_signal(barrier, inc=1, device_id=peer, device_id_type=MESH)
    pl.semaphore_wait(barrier, len(peers))


def _chip_exchange_on_sequencer(parts, name):
    n = len(parts)
    hbm = pltpu.MemorySpace.HBM
    ins = [jax.new_ref(p, memory_space=hbm) for p in parts]
    outs = [jax.empty_ref(jax.ShapeDtypeStruct(p.shape, p.dtype), memory_space=hbm) for p in parts]

    @_sequencer(name, ((n, 3), (n, 3), (n,)), 2)
    def launch(send, recv, lsem):
        x, y, c = lax.axis_index("x"), lax.axis_index("y"), lax.axis_index("c")
        mine = 2 * x + y
        chips = [(1 - x, y), (x, 1 - y), (1 - x, 1 - y)]
        _handshake([(*chip, c) for chip in chips])
        local = [pltpu.make_async_copy(ins[w].at[mine], outs[w].at[mine], lsem.at[w]) for w in range(n)]
        for cp in local:
            cp.start()
        sends = []
        for w in range(n):
            for j, (px, py) in enumerate(chips):
                cp = pltpu.make_async_remote_copy(ins[w].at[2 * px + py], outs[w].at[mine],
                                                  send.at[w, j], recv.at[w, j],
                                                  device_id=(px, py, c), device_id_type=MESH)
                cp.start()
                sends.append(cp)
        for w in range(n):
            for j, (px, py) in enumerate(chips):
                pltpu.make_async_remote_copy(ins[w].at[mine], outs[w].at[2 * px + py],
                                             send.at[w, j], recv.at[w, j],
                                             device_id=(px, py, c), device_id_type=MESH).wait_recv()
        for cp in sends:
            cp.wait_send()
        for cp in local:
            cp.wait()

    launch()
    return [o[...] for o in outs]


GRAD_TILES = (dict(blocks_on="cols", tm=512, tn=1280), dict(blocks_on="cols", tm=512, tn=256),
              dict(blocks_on="cols", tm=512, tn=256), dict(blocks_on="rows", tm=256, tn=1024),
              dict(blocks_on="cols", tm=1024, tn=1024), dict(blocks_on="rows", tm=1024, tn=1024))


def _mm_tn_pair(a, b, *, blocks_on, tm, tn, name):
    t_len, m = a.shape
    n = b.shape[1]
    if blocks_on == "rows":
        rows, cols, inner = m // N_DEV, n, n // tn
        assert tm == rows
        a_spec = pl.BlockSpec((t_len, tm), lambda p, t, blk: (0, blk[p]))
        b_spec = pl.BlockSpec((t_len, tn), lambda p, t, blk: (0, t))
        out_spec = pl.BlockSpec((None, tm, tn), lambda p, t, blk: (
            jnp.maximum(p - N_CHIP, 0), 0, jnp.where(p < N_CHIP, 0, t)))
    else:
        rows, cols, inner = m, n // N_DEV, m // tm
        assert tn == cols
        a_spec = pl.BlockSpec((t_len, tm), lambda p, t, blk: (0, t))
        b_spec = pl.BlockSpec((t_len, tn), lambda p, t, blk: (0, blk[p]))
        out_spec = pl.BlockSpec((None, tm, tn), lambda p, t, blk: (
            jnp.maximum(p - N_CHIP, 0), jnp.where(p < N_CHIP, 0, t), 0))

    def body(blk_ref, a_ref, b_ref, o_ref, land, stage, send_sem, recv_sem):
        del blk_ref
        p, t = pl.program_id(0), pl.program_id(1)
        step = p * inner + t
        x, y, c = lax.axis_index("x"), lax.axis_index("y"), lax.axis_index("c")
        tile = _dot(a_ref[...], b_ref[...], TN)

        def to_sibling(slot, chip, piece):
            return pltpu.make_async_remote_copy(stage.at[slot], land.at[chip, piece], send_sem.at[slot],
                                                recv_sem.at[chip, piece],
                                                device_id=(x, y, 1 - c), device_id_type=MESH)

        @pl.when(p < N_CHIP)
        def _():
            slot = step % 2

            @pl.when(step >= 2)
            def _():
                to_sibling(slot, 0, 0).wait_send()

            stage[slot] = tile.astype(BF16)
            to_sibling(slot, p, t).start()

        @pl.when(step == N_CHIP * inner)
        def _():
            for slot in range(min(2, N_CHIP * inner)):
                to_sibling(slot, 0, 0).wait_send()

        @pl.when(p >= N_CHIP)
        def _():
            chip = p - N_CHIP
            to_sibling(0, chip, t).wait_recv()
            o_ref[...] = (tile + land[chip, t].astype(F32)).astype(BF16)

    c = lax.axis_index("c")
    order = jnp.stack([2 * ch + 1 - c for ch in range(N_CHIP)] + [2 * ch + c for ch in range(N_CHIP)])
    return pl.pallas_call(
        body, name=name,
        grid_spec=pltpu.PrefetchScalarGridSpec(
            num_scalar_prefetch=1, grid=(N_DEV, inner), in_specs=[a_spec, b_spec], out_specs=out_spec,
            scratch_shapes=[pltpu.VMEM((N_CHIP, inner, tm, tn), BF16), pltpu.VMEM((2, tm, tn), BF16),
                            pltpu.SemaphoreType.DMA((2,)), pltpu.SemaphoreType.DMA((N_CHIP, inner))]),
        out_shape=jax.ShapeDtypeStruct((N_CHIP, rows, cols), BF16),
        compiler_params=_params(("arbitrary", "arbitrary")),
    )(order.astype(jnp.int32), a, b)


def _adamw_math(g, w, m, v):
    m2 = B1 * m + (1.0 - B1) * g
    v2 = B2 * v + (1.0 - B2) * (g * g)
    delta = -LR * ((m2 / BC1) / (jnp.sqrt(v2 / BC2) + AEPS) + WD * w)
    return delta, m2, v2


def _adamw(parts, w, m, v, *, name, after=(), tr=256):
    rows, cols = w.shape

    def body(p_ref, w_ref, m_ref, v_ref, *rest):
        g_ref, d_ref, mo_ref, vo_ref = rest[len(after):]
        g = p_ref[0].astype(F32)
        for b in range(1, N_CHIP):
            g = g + p_ref[b].astype(F32)
        delta, m2, v2 = _adamw_math(g, w_ref[...], m_ref[...], v_ref[...])
        g_ref[...] = g
        d_ref[...] = delta
        mo_ref[...] = m2
        vo_ref[...] = v2

    spec = pl.BlockSpec((tr, cols), lambda i: (i, 0))
    shape = jax.ShapeDtypeStruct((rows, cols), F32)
    return pl.pallas_call(
        body, name=name, grid=(rows // tr,),
        in_specs=[pl.BlockSpec((N_CHIP, tr, cols), lambda i: (0, i, 0)), spec, spec, spec]
        + [pl.BlockSpec(memory_space=pl.ANY)] * len(after),
        out_specs=[spec] * 4, out_shape=[shape] * 4,
        compiler_params=_params(("parallel",)),
    )(parts, w, m, v, *after)


def _small_update(part, w, m, v):
    rows = part.shape[0]

    def body(p_ref, w_ref, m_ref, v_ref, g_ref, d_ref, mo_ref, vo_ref, buf, send, recv):
        me, peers = _me_and_peers()
        buf[me] = p_ref[...]
        sends = []
        for k, (dev, _) in enumerate(peers):
            cp = pltpu.make_async_remote_copy(p_ref, buf.at[me], send.at[k], recv.at[k],
                                              device_id=dev, device_id_type=MESH)
            cp.start()
            sends.append(cp)
        for k, (dev, idx) in enumerate(peers):
            pltpu.make_async_remote_copy(p_ref, buf.at[idx], send.at[k], recv.at[k],
                                         device_id=dev, device_id_type=MESH).wait_recv()
        for cp in sends:
            cp.wait_send()
        g = buf[0]
        for b in range(1, N_DEV):
            g = g + buf[b]
        delta, m2, v2 = _adamw_math(g, w_ref[...], m_ref[...], v_ref[...])
        g_ref[...] = g
        d_ref[...] = delta
        mo_ref[...] = m2
        vo_ref[...] = v2

    vm = pl.BlockSpec(memory_space=pltpu.VMEM)
    shape = jax.ShapeDtypeStruct((rows, HD), F32)
    return pl.pallas_call(
        body, name="small_params_update",
        in_specs=[vm] * 4, out_specs=[vm] * 4, out_shape=[shape] * 4,
        scratch_shapes=[pltpu.VMEM((N_DEV, rows, HD), F32),
                        pltpu.SemaphoreType.DMA((N_DEV - 1,)), pltpu.SemaphoreType.DMA((N_DEV - 1,))],
    )(part, w, m, v)


def _pack_small(norm_mix, b_gate, qa, ka, qb, kb, rpb, norm_ffn):
    gains = jnp.concatenate([qa, ka, qb, kb, jnp.zeros((4, HD), F32)], axis=0)
    rpb_pad = jnp.pad(rpb.reshape(4 * (2 * WIN_R - 1), 2 * WIN_C - 1), ((0, 4), (0, HD - (2 * WIN_C - 1))))
    return jnp.concatenate([norm_mix.reshape(16, HD), b_gate.reshape(32, HD), gains, rpb_pad,
                            norm_ffn.reshape(16, HD), jnp.zeros((8, HD), F32)], axis=0)


LOSS_ROW = 136


def _unpack_small(p):
    norm_mix = p[0:16].reshape(1, D)
    b_gate = p[16:48].reshape(1, 2 * D)
    qa, ka, qb, kb = (p[48 + i:49 + i] for i in range(4))
    rpb = p[56:116, :2 * WIN_C - 1].reshape(1, 4, 2 * WIN_R - 1, 2 * WIN_C - 1)
    norm_ffn = p[120:136].reshape(1, D)
    return norm_mix, b_gate, qa, ka, qb, kb, rpb, norm_ffn


def kernel(x, norm_mix, w_in, b_gate, q_norm_a, k_norm_a, q_norm_b, k_norm_b, rpb_b, w_proj_a, w_proj_b, w_out, norm_ffn, w_up, w_down, loss_target, m_norm_mix, m_w_in, m_b_gate, m_q_norm_a, m_k_norm_a, m_q_norm_b, m_k_norm_b, m_rpb_b, m_w_proj_a, m_w_proj_b, m_w_out, m_norm_ffn, m_w_up, m_w_down, v_norm_mix, v_w_in, v_b_gate, v_q_norm_a, v_k_norm_a, v_q_norm_b, v_k_norm_b, v_rpb_b, v_w_proj_a, v_w_proj_b, v_w_out, v_norm_ffn, v_w_up, v_w_down):
    big_w = (w_in[0], w_proj_a[0], w_proj_b[0], w_out[0], w_up[0], w_down[0])
    big_m = (m_w_in[0], m_w_proj_a[0], m_w_proj_b[0], m_w_out[0], m_w_up[0], m_w_down[0])
    big_v = (v_w_in[0], v_w_proj_a[0], v_w_proj_b[0], v_w_out[0], v_w_up[0], v_w_down[0])
    names = ("w_in", "w_proj_a", "w_proj_b", "w_out", "w_up", "w_down")

    shards = [_cast_bf16(w) for w in big_w]
    g_in, = _gather_on_sequencer(shards[0:1], "gather_w_in")
    g_pa, g_pb, g_out, g_up = _gather_on_sequencer(shards[1:5], "gather_w_mix_up")
    g_down, = _gather_on_sequencer(shards[5:6], "gather_w_down")
    small_w = _pack_small(norm_mix, b_gate, q_norm_a, k_norm_a, q_norm_b, k_norm_b, rpb_b, norm_ffn)
    small_m = _pack_small(m_norm_mix, m_b_gate, m_q_norm_a, m_k_norm_a, m_q_norm_b, m_k_norm_b, m_rpb_b, m_norm_ffn)
    small_v = _pack_small(v_norm_mix, v_b_gate, v_q_norm_a, v_k_norm_a, v_q_norm_b, v_k_norm_b, v_rpb_b, v_norm_ffn)

    upd = [None] * 6
    in_flight = {}

    def weight_grads(tag, operands):
        sums = {i: _mm_tn_pair(a, b, name=f"grad_{names[i]}", **GRAD_TILES[i]) for i, (a, b) in operands.items()}
        new = list(sums.values())
        in_flight.update(zip(sums, _chip_exchange_on_sequencer(new, f"chip_exchange_{tag}")))
        return new

    loss, grad_x, small_g = _local_step(
        x[0], loss_target[0], norm_mix, b_gate, small_w[48:56], small_w[56:120], norm_ffn,
        g_in, g_pa, g_pb, g_out.reshape(D, D), g_up, g_down.reshape(D_FF, D), weight_grads)

    g_norm_mix, g_ba, g_bb, g_gains, g_rpb, g_norm_ffn = small_g
    small_part = jnp.concatenate([g_norm_mix.reshape(16, HD), g_ba.reshape(16, HD), g_bb.reshape(16, HD),
                                  g_gains, g_rpb, g_norm_ffn.reshape(16, HD),
                                  jnp.pad(loss, ((0, 7), (0, HD - 1)))], axis=0)
    slabs = _small_update(small_part, small_w, small_m, small_v)
    total = slabs[0][LOSS_ROW, 0]
    s_g, s_d, s_m, s_v = (_unpack_small(t) for t in slabs)

    last = grad_x
    for i, r in in_flight.items():
        upd[i] = _adamw(r, big_w[i], big_m[i], big_v[i], name=f"adamw_{names[i]}", after=[last])
        last = upd[i][0]
    b_g, b_d, b_m, b_v = ([u[j][None] for u in upd] for j in range(4))

    def order(small, big):
        nm, bg, qa, ka, qb, kb, rpb, nf = small
        w_in_, pa_, pb_, out_, up_, down_ = big
        return (nm, w_in_, bg, qa, ka, qb, kb, rpb, pa_, pb_, out_, nf, up_, down_)

    return (total, grad_x[None], *order(s_g, b_g), *order(s_d, b_d), *order(s_m, b_m), *order(s_v, b_v))
```

```python
import functools

import jax
import jax.numpy as jnp
import numpy as np
from jax import lax
from jax.experimental import pallas as pl
from jax.experimental.pallas import tpu as pltpu
from jax.experimental.pallas import tpu_sc as plsc

F32 = jnp.float32
BF16 = jnp.bfloat16

N_DEV = 8
S = 2048
D = 2048
HD = 128
NH = 16
NH_A = 12
QKV = NH * HD
D_IN = 3 * QKV + 2 * D
D_BR = 512
D_FF = 4 * D
GRID_W = 64
ROWS = S // GRID_W
WIN_R = 8
WIN_C = 16
EPS = 1e-6
NEG = -1e30
SCALE = HD ** -0.5
ROPE_THETA = 10000.0
DILATIONS = (1, 4, 16)
HALF_A = 64
QB = 128

LR, B1, B2, AEPS, WD, STEP = 0.001, 0.9, 0.999, 1e-08, 0.01, 10
BC1 = 1.0 - B1 ** STEP
BC2 = 1.0 - B2 ** STEP

VMEM_LIMIT = 56 * 1024 * 1024
MESH = pl.DeviceIdType.MESH

NN = (((1,), (0,)), ((), ()))
NT = (((1,), (1,)), ((), ()))
TN = (((0,), (0,)), ((), ()))


def _params(sem):
    return pltpu.CompilerParams(dimension_semantics=sem, vmem_limit_bytes=VMEM_LIMIT)


def _matmul(a, b, *, product, grid, a_spec, b_spec, epi, out_shape, out_specs, name,
            extra=(), extra_specs=(), after=(), carried=False):
    n_extra = len(extra)

    def body(a_ref, b_ref, *rest):
        epi(product(a_ref, b_ref), rest[:n_extra], rest[n_extra + len(after):])

    return pl.pallas_call(
        body, name=name, grid=grid,
        in_specs=[a_spec, b_spec, *extra_specs, *[pl.BlockSpec(memory_space=pl.ANY)] * len(after)],
        out_specs=out_specs, out_shape=out_shape,
        compiler_params=_params(("arbitrary", "arbitrary") if carried else ("parallel", "parallel")),
    )(a, b, *extra, *after)


def _dot(x, y, dims):
    return lax.dot_general(x, y, dims, preferred_element_type=F32)


def _epi_store(acc, ex, outs):
    outs[0][...] = acc.astype(outs[0].dtype)


def _epi_residual(acc, ex, outs):
    outs[0][...] = acc + ex[0][...]


def _mm_nn(a, b3, *, tm, tn, name, out_dtypes=(F32,), epi=_epi_store, extra=(), total=False):
    m, kdim = a.shape
    g, _, ng = b3.shape
    n = g * ng
    if tn <= ng:
        npg = ng // tn
        b_spec = pl.BlockSpec((None, kdim, tn), lambda j, i: (j // npg, 0, j % npg))

        def product(a_ref, b_ref):
            return _dot(a_ref[...], b_ref[...], NN)
    else:
        gb = tn // ng
        b_spec = pl.BlockSpec((gb, kdim, ng), lambda j, i: (j, 0, 0))

        def product(a_ref, b_ref):
            return jnp.concatenate([_dot(a_ref[...], b_ref[q], NN) for q in range(gb)], axis=1)

    tile = pl.BlockSpec((tm, tn), lambda j, i: (i, j))
    shapes = [jax.ShapeDtypeStruct((m, n), dt) for dt in out_dtypes]
    specs = [tile] * len(shapes)
    if total:
        shapes.append(jax.ShapeDtypeStruct((1, 1), F32))
        specs.append(pl.BlockSpec((1, 1), lambda j, i: (0, 0)))
    single = len(shapes) == 1
    return _matmul(
        a, b3, product=product, grid=(n // tn, m // tm), epi=epi, name=name, carried=total,
        a_spec=pl.BlockSpec((tm, kdim), lambda j, i: (i, 0)), b_spec=b_spec,
        extra=extra, extra_specs=[tile] * len(extra),
        out_shape=shapes[0] if single else shapes, out_specs=specs[0] if single else specs)


def _mm_nt(a, b3, *, tm, tn, name, out_dtype=F32, epi=_epi_store, extra=(), after=()):
    m, kdim = a.shape
    g, n, kg = b3.shape

    def product(a_ref, b_ref):
        acc = _dot(a_ref[:, 0:kg], b_ref[0], NT)
        for q in range(1, g):
            acc = acc + _dot(a_ref[:, q * kg:(q + 1) * kg], b_ref[q], NT)
        return acc

    tile = pl.BlockSpec((tm, tn), lambda j, i: (i, j))
    return _matmul(
        a, b3, product=product, grid=(n // tn, m // tm), epi=epi, name=name,
        a_spec=pl.BlockSpec((tm, kdim), lambda j, i: (i, 0)),
        b_spec=pl.BlockSpec((g, tn, kg), lambda j, i: (0, j, 0)),
        extra=extra, extra_specs=[tile] * len(extra), after=after,
        out_shape=jax.ShapeDtypeStruct((m, n), out_dtype), out_specs=tile)


def _mm_tn(a, b, *, tm, tn, name, groups=1, out_dtype=BF16):
    t, m = a.shape
    _, n = b.shape
    ng = n // groups
    if tn <= ng:
        npg = ng // tn
        out_spec = pl.BlockSpec((None, tm, tn), lambda j, i: (j // npg, i, j % npg))
        epi = _epi_store

        def product(a_ref, b_ref):
            return _dot(a_ref[...], b_ref[...], TN)
    else:
        gb = tn // ng
        out_spec = pl.BlockSpec((gb, tm, ng), lambda j, i: (j, i, 0))

        def product(a_ref, b_ref):
            return [_dot(a_ref[...], b_ref[:, q * ng:(q + 1) * ng], TN) for q in range(gb)]

        def epi(parts, ex, outs):
            for q, part in enumerate(parts):
                outs[0][q] = part.astype(out_dtype)

    return _matmul(
        a, b, product=product, grid=(n // tn, m // tm), epi=epi, name=name,
        a_spec=pl.BlockSpec((t, tm), lambda j, i: (0, i)),
        b_spec=pl.BlockSpec((t, tn), lambda j, i: (0, j)),
        out_shape=jax.ShapeDtypeStruct((groups, m, ng), out_dtype), out_specs=out_spec)


def _rms_fwd(x, g, *, name, tr=256):
    def body(x_ref, g_ref, y_ref, r_ref):
        xv = x_ref[...]
        r = lax.rsqrt(jnp.mean(xv * xv, axis=-1, keepdims=True) + EPS)
        y_ref[...] = (xv * r * g_ref[...]).astype(BF16)
        r_ref[...] = r

    row = pl.BlockSpec((tr, D), lambda i: (i, 0))
    return pl.pallas_call(
        body, name=name, grid=(S // tr,),
        in_specs=[row, pl.BlockSpec((1, D), lambda i: (0, 0))],
        out_specs=[row, pl.BlockSpec((tr, 1), lambda i: (i, 0))],
        out_shape=[jax.ShapeDtypeStruct((S, D), BF16), jax.ShapeDtypeStruct((S, 1), F32)],
        compiler_params=_params(("parallel",)),
    )(x, g)


def _rms_bwd(dy, x, rstd, g, resid, *, name, tr=256):
    def body(dy_ref, x_ref, r_ref, g_ref, res_ref, dx_ref, dxb_ref, dg_ref):
        r = r_ref[...]
        xh = x_ref[...] * r
        dyv = dy_ref[...]
        t = dyv * g_ref[...]
        dx = r * (t - xh * jnp.mean(t * xh, axis=-1, keepdims=True)) + res_ref[...]
        dx_ref[...] = dx
        dxb_ref[...] = dx.astype(BF16)
        part = jnp.sum(dyv * xh, axis=0, keepdims=True)

        @pl.when(pl.program_id(0) == 0)
        def _():
            dg_ref[...] = part

        @pl.when(pl.program_id(0) > 0)
        def _():
            dg_ref[...] += part

    row = pl.BlockSpec((tr, D), lambda i: (i, 0))
    vec = pl.BlockSpec((1, D), lambda i: (0, 0))
    return pl.pallas_call(
        body, name=name, grid=(S // tr,),
        in_specs=[row, row, pl.BlockSpec((tr, 1), lambda i: (i, 0)), vec, row],
        out_specs=[row, row, vec],
        out_shape=[jax.ShapeDtypeStruct((S, D), F32), jax.ShapeDtypeStruct((S, D), BF16),
                   jax.ShapeDtypeStruct((1, D), F32)],
        compiler_params=_params(("arbitrary",)),
    )(dy, x, rstd, g, resid)


def _rope_tables():
    pos = np.arange(S, dtype=np.float32)
    inv = (ROPE_THETA ** (-np.arange(0, HD, 2, dtype=np.float32) / HD)).astype(np.float32)
    ang = pos[:, None] * inv[None, :]
    cos, sin = np.cos(ang), np.sin(ang)
    return (jnp.asarray(np.concatenate([cos, cos], axis=-1), F32),
            jnp.asarray(np.concatenate([-sin, sin], axis=-1), F32))


def _swap_halves(t):
    return pltpu.roll(t, HD // 2, axis=1)


TOK = 256


def _lane_block_spec(d, last=HD):
    return pl.BlockSpec((4, TOK // d, d * last), lambda i: (0, i, 0))


def _to_lane_blocks(dst, head, val, d, scr, dtype):
    w = val.shape[1]
    if d == 1:
        dst[head] = val.astype(dtype)
        return
    scr[...] = val
    for r in range(d):
        dst[head, :, r * w:(r + 1) * w] = scr[pl.ds(r, TOK // d, stride=d), :].astype(dtype)


def _from_lane_blocks(src, head, d, w, scr):
    if d == 1:
        return src[head].astype(F32)
    for r in range(d):
        scr[pl.ds(r, TOK // d, stride=d), :] = src[head, :, r * w:(r + 1) * w].astype(F32)
    return scr[...]


def _qk_prep(proj, gains, cos2, sin2):
    def body(q_ref, k_ref, v_ref, g_ref, c_ref, s_ref, *rest):
        outs, scr = rest[:-1], rest[-1]
        cos, sin = c_ref[...], s_ref[...]
        for which, (src, row_a, row_b) in enumerate(((q_ref, 0, 2), (k_ref, 1, 3), (v_ref, None, None))):
            for h in range(NH):
                y = src[:, h * HD:(h + 1) * HD]
                if row_a is not None:
                    y = y * lax.rsqrt(jnp.mean(y * y, axis=-1, keepdims=True) + EPS)
                    if h < NH_A:
                        y = y * g_ref[row_a:row_a + 1, :]
                        y = y * cos + _swap_halves(y) * sin
                    else:
                        y = y * g_ref[row_b:row_b + 1, :]
                if h < NH_A:
                    gi = h // 4
                    _to_lane_blocks(outs[3 * gi + which], h % 4, y, DILATIONS[gi], scr, BF16)
                else:
                    hb = h - NH_A
                    outs[9 + which][:, hb * HD:(hb + 1) * HD] = y.astype(BF16)

    def blk(c):
        return pl.BlockSpec((TOK, QKV), lambda i: (i, c))
    tab = pl.BlockSpec((TOK, HD), lambda i: (i, 0))
    out_specs, out_shape = [], []
    for d in DILATIONS:
        out_specs += [_lane_block_spec(d)] * 3
        out_shape += [jax.ShapeDtypeStruct((4, S // d, d * HD), BF16)] * 3
    out_specs += [pl.BlockSpec((TOK, D_BR), lambda i: (i, 0))] * 3
    out_shape += [jax.ShapeDtypeStruct((S, D_BR), BF16)] * 3
    outs = pl.pallas_call(
        body, name="qk_prep", grid=(S // TOK,),
        in_specs=[blk(0), blk(1), blk(2), pl.BlockSpec((8, HD), lambda i: (0, 0)), tab, tab],
        out_specs=out_specs, out_shape=out_shape,
        scratch_shapes=[pltpu.VMEM((TOK, HD), F32)],
        compiler_params=_params(("parallel",)),
    )(proj, proj, proj, gains, cos2, sin2)
    return [tuple(outs[3 * gi:3 * gi + 3]) for gi in range(3)], tuple(outs[9:12])


def _qk_prep_bwd(dproj, proj, gains, cos2, sin2, grads_a, grads_b):
    def body(dp_in, q_ref, k_ref, g_ref, c_ref, s_ref, *rest):
        grads, (dp_out, dg_ref, scr) = rest[:12], rest[12:]
        del dp_in
        cos, sin = c_ref[...], s_ref[...]

        def grad_of(which, h):
            if h < NH_A:
                gi = h // 4
                return _from_lane_blocks(grads[3 * gi + which], h % 4, DILATIONS[gi], HD, scr)
            hb = h - NH_A
            return grads[9 + which][:, hb * HD:(hb + 1) * HD]

        dg_rows = []
        for which, (src, base, row_a, row_b) in enumerate(((q_ref, 0, 0, 2), (k_ref, QKV, 1, 3))):
            dg_a = jnp.zeros((1, HD), F32)
            dg_b = jnp.zeros((1, HD), F32)
            for h in range(NH):
                t = src[:, h * HD:(h + 1) * HD]
                dy = grad_of(which, h)
                r = lax.rsqrt(jnp.mean(t * t, axis=-1, keepdims=True) + EPS)
                xh = t * r
                if h < NH_A:
                    dy = dy * cos - _swap_halves(dy) * sin
                    gain = g_ref[row_a:row_a + 1, :]
                    dg_a = dg_a + jnp.sum(dy * xh, axis=0, keepdims=True)
                else:
                    gain = g_ref[row_b:row_b + 1, :]
                    dg_b = dg_b + jnp.sum(dy * xh, axis=0, keepdims=True)
                u = dy * gain
                dx = r * (u - xh * jnp.mean(u * xh, axis=-1, keepdims=True))
                dp_out[:, base + h * HD:base + (h + 1) * HD] = dx.astype(BF16)
            dg_rows += [(row_a, dg_a), (row_b, dg_b)]
        for h in range(NH):
            dp_out[:, 2 * QKV + h * HD:2 * QKV + (h + 1) * HD] = grad_of(2, h).astype(BF16)

        @pl.when(pl.program_id(0) == 0)
        def _():
            dg_ref[...] = jnp.zeros((8, HD), F32)

        for row, val in dg_rows:
            dg_ref[row:row + 1, :] += val

    def blk(c):
        return pl.BlockSpec((TOK, QKV), lambda i: (i, c))
    tab = pl.BlockSpec((TOK, HD), lambda i: (i, 0))
    gain_spec = pl.BlockSpec((8, HD), lambda i: (0, 0))
    grad_specs = [s for d in DILATIONS for s in [_lane_block_spec(d)] * 3]
    grad_specs += [pl.BlockSpec((TOK, D_BR), lambda i: (i, 0))] * 3
    return pl.pallas_call(
        body, name="qk_prep_bwd", grid=(S // TOK,),
        in_specs=[pl.BlockSpec(memory_space=pl.ANY), blk(0), blk(1), gain_spec, tab, tab] + grad_specs,
        out_specs=[pl.BlockSpec((TOK, 3 * QKV), lambda i: (i, 0)), gain_spec],
        out_shape=[jax.ShapeDtypeStruct((S, D_IN), BF16), jax.ShapeDtypeStruct((8, HD), F32)],
        input_output_aliases={0: 0},
        scratch_shapes=[pltpu.VMEM((TOK, HD), F32)],
        compiler_params=_params(("arbitrary",)),
    )(dproj, proj, proj, gains, cos2, sin2, *[g for grp in grads_a for g in grp], *grads_b)


def _mix_fwd(oa, ob, w_pa, w_pb, proj, b_gate, *, tr=256):
    def body(oa_ref, ob_ref, pa_ref, pb_ref, la_ref, lb_ref, ba_ref, bb_ref, mix_ref, ya_ref, yb_ref):
        ya = jnp.concatenate([_dot(oa_ref[...], pa_ref[q], NN) for q in range(N_DEV)], axis=1)
        yb = jnp.concatenate([_dot(ob_ref[...], pb_ref[q], NN) for q in range(N_DEV)], axis=1)
        ga = jax.nn.sigmoid(la_ref[...] + ba_ref[...])
        gb = jax.nn.sigmoid(lb_ref[...] + bb_ref[...])
        mix_ref[...] = (ga * ya + gb * yb).astype(BF16)
        ya_ref[...] = ya.astype(BF16)
        yb_ref[...] = yb.astype(BF16)

    row = pl.BlockSpec((tr, D), lambda i: (i, 0))
    branch = pl.BlockSpec((tr, D_BR), lambda i: (i, 0))
    whole = pl.BlockSpec((N_DEV, D_BR, D // N_DEV), lambda i: (0, 0, 0))
    return pl.pallas_call(
        body, name="mix_fwd", grid=(S // tr,),
        in_specs=[branch, branch, whole, whole,
                  pl.BlockSpec((tr, D), lambda i: (i, 3)), pl.BlockSpec((tr, D), lambda i: (i, 4)),
                  pl.BlockSpec((1, D), lambda i: (0, 0)), pl.BlockSpec((1, D), lambda i: (0, 1))],
        out_specs=[row, row, row], out_shape=[jax.ShapeDtypeStruct((S, D), BF16)] * 3,
        compiler_params=_params(("parallel",)),
    )(oa, ob, w_pa, w_pb, proj, proj, b_gate, b_gate)


def _mix_bwd(dh1b, w_out, proj, b_gate, ya, yb, *, tr=256):
    def body(dh_ref, w_ref, la_ref, lb_ref, b_ref, ya_ref, yb_ref, dya_ref, dyb_ref, dp_ref, db_ref):
        dm = _dot(dh_ref[...], w_ref[...], NT)
        parts = []
        for l_ref, y_ref, dy_ref, lo in ((la_ref, ya_ref, dya_ref, 0), (lb_ref, yb_ref, dyb_ref, D)):
            g = jax.nn.sigmoid(l_ref[...] + b_ref[:, lo:lo + D])
            dy_ref[...] = (dm * g).astype(BF16)
            dl = dm * y_ref[...].astype(F32) * g * (1.0 - g)
            dp_ref[:, lo:lo + D] = dl.astype(BF16)
            parts.append(jnp.sum(dl, axis=0, keepdims=True))
        part = jnp.concatenate(parts, axis=1)

        @pl.when(pl.program_id(0) == 0)
        def _():
            db_ref[...] = part

        @pl.when(pl.program_id(0) > 0)
        def _():
            db_ref[...] += part

    row = pl.BlockSpec((tr, D), lambda i: (i, 0))
    vec = pl.BlockSpec((1, 2 * D), lambda i: (0, 0))
    gate_cols = pl.BlockSpec((pl.Element(tr), pl.Element(2 * D)), lambda i: (i * tr, 3 * QKV))
    return pl.pallas_call(
        body, name="mix_bwd", grid=(S // tr,),
        in_specs=[row, pl.BlockSpec((D, D), lambda i: (0, 0)),
                  pl.BlockSpec((tr, D), lambda i: (i, 3)), pl.BlockSpec((tr, D), lambda i: (i, 4)), vec, row, row],
        out_specs=[row, row, gate_cols, vec],
        out_shape=[jax.ShapeDtypeStruct((S, D), BF16), jax.ShapeDtypeStruct((S, D), BF16),
                   jax.ShapeDtypeStruct((S, D_IN), BF16), jax.ShapeDtypeStruct((1, 2 * D), F32)],
        compiler_params=_params(("arbitrary",)),
    )(dh1b, w_out, proj, proj, b_gate, ya, yb)


def _band_blocks(m_len):
    wk = min(m_len, QB + 2 * QB)
    return [(qb * QB, min(max(qb * QB - QB, 0), m_len - wk), wk) for qb in range(m_len // QB)]


def _band_scores(q, kw, q0, k0, wk):
    s = _dot(q, kw, NT) * SCALE
    qpos = q0 + lax.broadcasted_iota(jnp.int32, (QB, 1), 0)
    kpos = k0 + lax.broadcasted_iota(jnp.int32, (1, wk), 1)
    return jnp.where(jnp.abs(kpos - qpos) <= HALF_A, s, NEG)


def _attn_a_fwd(q, k, v, gi):
    d = DILATIONS[gi]
    m_len = S // d

    def body(q_ref, k_ref, v_ref, o_ref, lse_ref):
        for r in range(d):
            lanes = slice(r * HD, (r + 1) * HD)
            for q0, k0, wk in _band_blocks(m_len):
                s = _band_scores(q_ref[q0:q0 + QB, lanes], k_ref[k0:k0 + wk, lanes], q0, k0, wk)
                m = jnp.max(s, axis=-1, keepdims=True)
                p = jnp.exp(s - m)
                l = jnp.sum(p, axis=-1, keepdims=True)
                o_ref[q0:q0 + QB, lanes] = _dot(p.astype(BF16), v_ref[k0:k0 + wk, lanes], NN) / l
                lse_ref[q0:q0 + QB, r:r + 1] = m + jnp.log(l)

    head = pl.BlockSpec((None, m_len, d * HD), lambda h: (h, 0, 0))
    stat = pl.BlockSpec((None, m_len, d), lambda h: (h, 0, 0))
    return pl.pallas_call(
        body, name=f"attn_a_fwd_{gi}", grid=(4,),
        in_specs=[head, head, head], out_specs=[head, stat],
        out_shape=[jax.ShapeDtypeStruct((4, m_len, d * HD), F32), jax.ShapeDtypeStruct((4, m_len, d), F32)],
        compiler_params=_params(("parallel",)),
    )(q, k, v)


def _combine_a(os, lses):
    def body(o0, o1, o2, l0, l1, l2, oa_ref, lse_ref, scr, scr1):
        for h in range(4):
            o = [_from_lane_blocks(ref, h, d, HD, scr) for ref, d in zip((o0, o1, o2), DILATIONS)]
            a, b, c = (_from_lane_blocks(ref, h, d, 1, scr1) for ref, d in zip((l0, l1, l2), DILATIONS))
            m = jnp.maximum(jnp.maximum(a, b), c)
            wa, wb, wc = jnp.exp(a - m), jnp.exp(b - m), jnp.exp(c - m)
            tot = wa + wb + wc
            oa_ref[:, h * HD:(h + 1) * HD] = ((wa * o[0] + wb * o[1] + wc * o[2]) / tot).astype(BF16)
            lse_ref[h] = m + jnp.log(tot)

    return pl.pallas_call(
        body, name="combine_a", grid=(S // TOK,),
        in_specs=[_lane_block_spec(d) for d in DILATIONS] + [_lane_block_spec(d, 1) for d in DILATIONS],
        out_specs=[pl.BlockSpec((TOK, D_BR), lambda i: (i, 0)), pl.BlockSpec((4, TOK, 1), lambda i: (0, i, 0))],
        out_shape=[jax.ShapeDtypeStruct((S, D_BR), BF16), jax.ShapeDtypeStruct((4, S, 1), F32)],
        scratch_shapes=[pltpu.VMEM((TOK, HD), F32), pltpu.VMEM((TOK, 1), F32)],
        compiler_params=_params(("parallel",)),
    )(*os, *lses)


def _proj_a_bwd(dya, w_pa, oa, lse):
    kg = D // N_DEV

    def body(dy_ref, w_ref, o_ref, l_ref, *rest):
        outs, (scr, scr1) = rest[:9], rest[9:]
        doa = _dot(dy_ref[:, 0:kg], w_ref[0], NT)
        for q in range(1, N_DEV):
            doa = doa + _dot(dy_ref[:, q * kg:(q + 1) * kg], w_ref[q], NT)
        for h in range(4):
            do = doa[:, h * HD:(h + 1) * HD]
            dsum = jnp.sum(do * o_ref[:, h * HD:(h + 1) * HD].astype(F32), axis=-1, keepdims=True)
            for gi, d in enumerate(DILATIONS):
                _to_lane_blocks(outs[3 * gi], h, do, d, scr, BF16)
                _to_lane_blocks(outs[3 * gi + 1], h, l_ref[h], d, scr1, F32)
                _to_lane_blocks(outs[3 * gi + 2], h, dsum, d, scr1, F32)

    row = pl.BlockSpec((TOK, D_BR), lambda i: (i, 0))
    out_specs, out_shape = [], []
    for d in DILATIONS:
        out_specs += [_lane_block_spec(d), _lane_block_spec(d, 1), _lane_block_spec(d, 1)]
        out_shape += [jax.ShapeDtypeStruct((4, S // d, d * HD), BF16)] + [jax.ShapeDtypeStruct((4, S // d, d), F32)] * 2
    outs = pl.pallas_call(
        body, name="proj_a_bwd", grid=(S // TOK,),
        in_specs=[pl.BlockSpec((TOK, D), lambda i: (i, 0)),
                  pl.BlockSpec((N_DEV, D_BR, kg), lambda i: (0, 0, 0)),
                  row, pl.BlockSpec((4, TOK, 1), lambda i: (0, i, 0))],
        out_specs=out_specs, out_shape=out_shape,
        scratch_shapes=[pltpu.VMEM((TOK, HD), F32), pltpu.VMEM((TOK, 1), F32)],
        compiler_params=_params(("parallel",)),
    )(dya, w_pa, oa, lse)
    return [tuple(outs[3 * gi:3 * gi + 3]) for gi in range(3)]


def _attn_a_bwd(q, k, v, do, lse, dsum, gi):
    d = DILATIONS[gi]
    m_len = S // d

    def body(q_ref, k_ref, v_ref, do_ref, lse_ref, dsum_ref, dq_ref, dk_ref, dv_ref):
        dk_ref[...] = jnp.zeros((m_len, d * HD), F32)
        dv_ref[...] = jnp.zeros((m_len, d * HD), F32)
        for r in range(d):
            lanes = slice(r * HD, (r + 1) * HD)
            for q0, k0, wk in _band_blocks(m_len):
                rows, keys = slice(q0, q0 + QB), slice(k0, k0 + wk)
                qv, kw, vw, dov = q_ref[rows, lanes], k_ref[keys, lanes], v_ref[keys, lanes], do_ref[rows, lanes]
                p = jnp.exp(_band_scores(qv, kw, q0, k0, wk) - lse_ref[rows, r:r + 1])
                ds = (p * (_dot(dov, vw, NT) - dsum_ref[rows, r:r + 1]) * SCALE).astype(BF16)
                dq_ref[rows, lanes] = _dot(ds, kw, NN)
                dk_ref[keys, lanes] += _dot(ds, qv, TN)
                dv_ref[keys, lanes] += _dot(p.astype(BF16), dov, TN)

    head = pl.BlockSpec((None, m_len, d * HD), lambda h: (h, 0, 0))
    stat = pl.BlockSpec((None, m_len, d), lambda h: (h, 0, 0))
    shape = jax.ShapeDtypeStruct((4, m_len, d * HD), F32)
    return pl.pallas_call(
        body, name=f"attn_a_bwd_{gi}", grid=(4,),
        in_specs=[head, head, head, head, stat, stat], out_specs=[head, head, head],
        out_shape=[shape, shape, shape],
        compiler_params=_params(("parallel",)),
    )(q, k, v, do, lse, dsum)


KEYS_B = WIN_R * GRID_W
N_OFF = WIN_R


def _bias_constants():
    q = np.arange(GRID_W)[:, None]
    kc = np.arange(GRID_W)[None, :]
    dc = np.clip(kc - q, -(WIN_C - 1), WIN_C - 1) + (WIN_C - 1)
    expand = np.zeros((HD, GRID_W * GRID_W), np.float32)
    expand[dc.reshape(-1), np.arange(GRID_W * GRID_W)] = 1.0
    cs = np.clip(q - WIN_C // 2, 0, GRID_W - WIN_C)
    keep = ((kc >= cs) & (kc < cs + WIN_C)).reshape(1, -1).astype(np.float32)
    sel = np.zeros((64, 4 * N_OFF * WIN_R), np.float32)
    for h in range(4):
        for off in range(N_OFF):
            for j in range(WIN_R):
                sel[h * (2 * WIN_R - 1) + off + j, (h * N_OFF + off) * WIN_R + j] = 1.0
    return jnp.asarray(expand), jnp.asarray(keep), jnp.asarray(sel)


def _bias_expand(rpb_pad, expand, keep, sel):
    def body(r_ref, e_ref, k_ref, s_ref, o_ref):
        t = lax.dot_general(r_ref[...], e_ref[...], NN, precision=lax.Precision.HIGHEST,
                            preferred_element_type=F32)
        rows = lax.dot_general(s_ref[...], t, TN, precision=lax.Precision.HIGHEST,
                               preferred_element_type=F32)
        o_ref[...] = jnp.where(k_ref[...] > 0.5, rows, NEG)

    return pl.pallas_call(
        body, name="bias_expand",
        out_shape=jax.ShapeDtypeStruct((4 * N_OFF * WIN_R, GRID_W * GRID_W), F32),
        compiler_params=pltpu.CompilerParams(vmem_limit_bytes=VMEM_LIMIT),
    )(rpb_pad, expand, keep, sel)


def _bias_reduce(dbias_rows, expand, sel):
    def body(x_ref, e_ref, s_ref, o_ref):
        z = lax.dot_general(x_ref[...], e_ref[...], NT, precision=lax.Precision.HIGHEST,
                            preferred_element_type=F32)
        o_ref[...] = lax.dot_general(s_ref[...], z, NN, precision=lax.Precision.HIGHEST,
                                     preferred_element_type=F32)

    return pl.pallas_call(
        body, name="bias_reduce", out_shape=jax.ShapeDtypeStruct((64, HD), F32),
        compiler_params=pltpu.CompilerParams(vmem_limit_bytes=VMEM_LIMIT),
    )(dbias_rows, expand, sel)


def _rows_to_tab(rows):
    t = rows.reshape(4, N_OFF, WIN_R, GRID_W, GRID_W)
    return t.transpose(0, 1, 3, 2, 4).reshape(4, N_OFF, GRID_W, KEYS_B)


def _tab_to_rows(tab):
    t = tab.reshape(4, N_OFF, GRID_W, WIN_R, GRID_W)
    return t.transpose(0, 1, 3, 2, 4).reshape(4 * N_OFF * WIN_R, GRID_W * GRID_W)


def _row_window(r):
    r0 = jnp.clip(r - WIN_R // 2, 0, ROWS - WIN_R)
    off = r0 + (WIN_R - 1) - r
    return pl.multiple_of(r * GRID_W, GRID_W), pl.multiple_of(r0 * GRID_W, GRID_W), off


def _attn_b_fwd(qn, kn, vb, bias_tab):
    def body(q_ref, k_ref, v_ref, b_ref, o_ref, lse_ref):
        def row(r, carry):
            qs, ks, off = _row_window(r)
            q = q_ref[pl.ds(qs, GRID_W), :]
            s = lax.dot_general(q, k_ref[pl.ds(ks, KEYS_B), :], NT, preferred_element_type=F32) * SCALE
            s = s + b_ref[off]
            m = jnp.max(s, axis=-1, keepdims=True)
            p = jnp.exp(s - m)
            l = jnp.sum(p, axis=-1, keepdims=True)
            o = lax.dot_general(p.astype(BF16), v_ref[pl.ds(ks, KEYS_B), :], NN, preferred_element_type=F32)
            o_ref[pl.ds(qs, GRID_W), :] = (o / l).astype(BF16)
            lse_ref[pl.ds(qs, GRID_W), :] = m + jnp.log(l)
            return carry

        lax.fori_loop(0, ROWS, row, 0)

    full = pl.BlockSpec((S, HD), lambda h: (0, h))
    return pl.pallas_call(
        body, name="attn_b_fwd", grid=(4,),
        in_specs=[full, full, full, pl.BlockSpec((None, N_OFF, GRID_W, KEYS_B), lambda h: (h, 0, 0, 0))],
        out_specs=[pl.BlockSpec((S, HD), lambda h: (0, h)), pl.BlockSpec((None, S, 1), lambda h: (h, 0, 0))],
        out_shape=[jax.ShapeDtypeStruct((S, D_BR), BF16), jax.ShapeDtypeStruct((4, S, 1), F32)],
        compiler_params=_params(("parallel",)),
    )(qn, kn, vb, bias_tab)


def _attn_b_bwd(qn, kn, vb, bias_tab, ob, dob, lse):
    def body(q_ref, k_ref, v_ref, b_ref, o_ref, do_ref, lse_ref, dq_ref, dk_ref, dv_ref, db_ref):
        dk_ref[...] = jnp.zeros((S, HD), F32)
        dv_ref[...] = jnp.zeros((S, HD), F32)
        db_ref[...] = jnp.zeros((N_OFF, GRID_W, KEYS_B), F32)

        def row(r, carry):
            qs, ks, off = _row_window(r)
            rows = pl.ds(qs, GRID_W)
            keys = pl.ds(ks, KEYS_B)
            q = q_ref[rows, :]
            kw = k_ref[keys, :]
            s = lax.dot_general(q, kw, NT, preferred_element_type=F32) * SCALE + b_ref[off]
            p = jnp.exp(s - lse_ref[rows, :])
            do = do_ref[rows, :]
            dobf = do.astype(BF16)
            dsum = jnp.sum(do * o_ref[rows, :].astype(F32), axis=-1, keepdims=True)
            dp = lax.dot_general(dobf, v_ref[keys, :], NT, preferred_element_type=F32)
            ds = p * (dp - dsum)
            db_ref[off] += ds
            dsb = (ds * SCALE).astype(BF16)
            dq_ref[rows, :] = lax.dot_general(dsb, kw, NN, preferred_element_type=F32)
            dk_ref[keys, :] += lax.dot_general(dsb, q, TN, preferred_element_type=F32)
            dv_ref[keys, :] += lax.dot_general(p.astype(BF16), dobf, TN, preferred_element_type=F32)
            return carry

        lax.fori_loop(0, ROWS, row, 0)

    full = pl.BlockSpec((S, HD), lambda h: (0, h))
    slot = pl.BlockSpec((S, HD), lambda h: (0, h))
    tab = pl.BlockSpec((None, N_OFF, GRID_W, KEYS_B), lambda h: (h, 0, 0, 0))
    shape = jax.ShapeDtypeStruct((S, D_BR), F32)
    return pl.pallas_call(
        body, name="attn_b_bwd", grid=(4,),
        in_specs=[full, full, full, tab, slot, slot, pl.BlockSpec((None, S, 1), lambda h: (h, 0, 0))],
        out_specs=[slot, slot, slot, tab],
        out_shape=[shape, shape, shape, jax.ShapeDtypeStruct((4, N_OFF, GRID_W, KEYS_B), F32)],
        compiler_params=_params(("parallel",)),
    )(qn, kn, vb, bias_tab, ob, dob, lse)


def _epi_relu_sq(acc, ex, outs):
    u = jnp.maximum(acc, 0.0)
    outs[0][...] = u.astype(BF16)
    outs[1][...] = (u * u).astype(BF16)


def _epi_relu_sq_bwd(acc, ex, outs):
    outs[0][...] = (acc * (2.0 * ex[0][...].astype(F32))).astype(BF16)


def _epi_loss_head(acc, ex, outs):
    e = acc + ex[0][...] - ex[1][...]
    dy = e * (1.0 / D)
    outs[0][...] = dy
    outs[1][...] = dy.astype(BF16)
    part = (0.5 / D) * jnp.sum(jnp.sum(e * e, axis=-1, keepdims=True), axis=0, keepdims=True)
    first = (pl.program_id(0) == 0) & (pl.program_id(1) == 0)

    @pl.when(first)
    def _():
        outs[2][...] = part

    @pl.when(jnp.logical_not(first))
    def _():
        outs[2][...] += part


def _local_step(x, target, norm_mix, b_gate, gains, rpb_pad, norm_ffn,
                w_in, w_pa, w_pb, w_out, w_up, w_down, weight_grads):
    cos2, sin2 = _rope_tables()
    expand, keep, sel = _bias_constants()
    w_out3, w_down3 = w_out[None], w_down[None]

    xn, rstd1 = _rms_fwd(x, norm_mix, name="rms_mix")
    proj = _mm_nn(xn, w_in, tm=1024, tn=1280, name="proj")
    qkv_a, qkv_b = _qk_prep(proj, gains, cos2, sin2)
    fwd_a = [_attn_a_fwd(*qkv_a[gi], gi) for gi in range(3)]
    oa, lse_a = _combine_a([o for o, _ in fwd_a], [l for _, l in fwd_a])
    bias_tab = _rows_to_tab(_bias_expand(rpb_pad, expand, keep, sel))
    ob, lse_b = _attn_b_fwd(*qkv_b, bias_tab)
    mixed, ya, yb = _mix_fwd(oa, ob, w_pa, w_pb, proj, b_gate)
    h1 = _mm_nn(mixed, w_out3, tm=1024, tn=1024, name="out_proj", epi=_epi_residual, extra=(x,))
    hn, rstd2 = _rms_fwd(h1, norm_ffn, name="rms_ffn")
    u, usq = _mm_nn(hn, w_up, tm=1024, tn=1024, name="ffn_up", epi=_epi_relu_sq,
                    out_dtypes=(BF16, BF16))
    dy, dyb, loss = _mm_nn(usq, w_down3, tm=512, tn=512, name="ffn_down", epi=_epi_loss_head,
                           extra=(h1, target), out_dtypes=(F32, BF16), total=True)

    sent = weight_grads("w_down", {5: (usq, dyb)})
    du = _mm_nt(dyb, w_down3, tm=1024, tn=1024, name="ffn_down_bwd", out_dtype=BF16,
                epi=_epi_relu_sq_bwd, extra=(u,), after=sent)
    sent = weight_grads("w_up", {4: (hn, du)})
    dhn = _mm_nt(du, w_up, tm=512, tn=512, name="ffn_up_bwd", after=sent)
    dh1, dh1b, g_norm_ffn = _rms_bwd(dhn, h1, rstd2, norm_ffn, dy, name="rms_ffn_bwd")

    dya, dyb2, dproj, g_b = _mix_bwd(dh1b, w_out, proj, b_gate, ya, yb)
    sent = weight_grads("w_mix", {3: (mixed, dh1b), 1: (oa, dya), 2: (ob, dyb2)})
    dob = _mm_nt(dyb2, w_pb, tm=1024, tn=D_BR, name="proj_b_bwd", after=sent)
    prep = _proj_a_bwd(dya, w_pa, oa, lse_a)
    grads_a = [_attn_a_bwd(*qkv_a[gi], *prep[gi], gi) for gi in range(3)]
    dqb, dkb, dvb, dbias = _attn_b_bwd(*qkv_b, bias_tab, ob, dob, lse_b)
    g_rpb = _bias_reduce(_tab_to_rows(dbias), expand, sel)
    dproj, g_gains = _qk_prep_bwd(dproj, proj, gains, cos2, sin2, grads_a, (dqb, dkb, dvb))
    sent = weight_grads("w_in", {0: (xn, dproj)})
    dxn = _mm_nt(dproj, w_in, tm=256, tn=512, name="proj_bwd", after=sent)
    grad_x, _, g_norm_mix = _rms_bwd(dxn, x, rstd1, norm_mix, dh1, name="rms_mix_bwd")

    small = (g_norm_mix, g_b, g_gains, g_rpb, g_norm_ffn)
    return loss, grad_x, small


def _cast_bf16(w, *, tr=256):
    rows, cols = w.shape
    tr = min(tr, rows)

    def body(w_ref, o_ref):
        o_ref[...] = w_ref[...].astype(BF16)

    spec = pl.BlockSpec((tr, cols), lambda i: (i, 0))
    return pl.pallas_call(
        body, name=f"cast_{rows}x{cols}", grid=(rows // tr,), in_specs=[spec], out_specs=spec,
        out_shape=jax.ShapeDtypeStruct((rows, cols), BF16), compiler_params=_params(("parallel",)),
    )(w)


def _me_and_peers():
    x, y, c = lax.axis_index("x"), lax.axis_index("y"), lax.axis_index("c")
    me = 4 * x + 2 * y + c
    peers = []
    for k in range(1, N_DEV):
        px = 1 - x if k & 4 else x
        py = 1 - y if k & 2 else y
        pc = 1 - c if k & 1 else c
        peers.append(((px, py, pc), 4 * px + 2 * py + pc))
    return me, peers


def _gather_on_sequencer(shards, name):
    n = len(shards)
    hbm = pltpu.MemorySpace.HBM
    ins = [jax.new_ref(s, memory_space=hbm) for s in shards]
    outs = [jax.empty_ref(jax.ShapeDtypeStruct((N_DEV,) + s.shape, s.dtype), memory_space=hbm) for s in shards]

    @pl.kernel(mesh=plsc.ScalarSubcoreMesh(axis_name="seq", num_cores=1), name=name,
               scratch_types=(pltpu.SemaphoreType.DMA((n, N_DEV - 1)), pltpu.SemaphoreType.DMA((n, N_DEV - 1)),
                              pltpu.SemaphoreType.DMA((n,))),
               compiler_params=pltpu.CompilerParams(collective_id=0))
    def launch(send, recv, lsem):
        x, y, c = lax.axis_index("x"), lax.axis_index("y"), lax.axis_index("c")
        me, sibling = (x, y, c), (x, y, 1 - c)
        chips = [(1 - x, y), (x, 1 - y), (1 - x, 1 - y)]
        barrier = pltpu.get_barrier_semaphore()
        for peer in [sibling] + [(*chip, c) for chip in chips]:
            pl.semaphore_signal(barrier, inc=1, device_id=peer, device_id_type=MESH)
        pl.semaphore_wait(barrier, 4)

        def copy(w, k, block, to, src=None):
            px, py, pc = block
            dst = outs[w].at[4 * px + 2 * py + pc]
            return pltpu.make_async_remote_copy(dst if src is None else src, dst, send.at[w, k], recv.at[w, k],
                                                device_id=to, device_id_type=MESH)

        local = [pltpu.make_async_copy(ins[w], outs[w].at[4 * x + 2 * y + c], lsem.at[w]) for w in range(n)]
        for cp in local:
            cp.start()
        first = []
        for w in range(n):
            first += [copy(w, 1 + j, me, (*chip, c), src=ins[w]) for j, chip in enumerate(chips)]
            first.append(copy(w, 0, me, sibling, src=ins[w]))
        for cp in first:
            cp.start()
        passed = []
        for w in range(n):
            for j, chip in enumerate(chips):
                copy(w, 1 + j, (*chip, c), me).wait_recv()
                cp = copy(w, 4 + j, (*chip, c), sibling)
                cp.start()
                passed.append(cp)
        for w in range(n):
            copy(w, 0, sibling, me).wait_recv()
            for j, chip in enumerate(chips):
                copy(w, 4 + j, (*chip, 1 - c), me).wait_recv()
        for cp in first + passed:
            cp.wait_send()
        for cp in local:
            cp.wait()

    launch()
    return [o[...] for o in outs]


N_CHIP = 4
CHIPS = ((0, 0), (0, 1), (1, 0), (1, 1))


def _sequencer(name, n_sems, collective_id):
    return functools.partial(
        pl.kernel, mesh=plsc.ScalarSubcoreMesh(axis_name="seq", num_cores=1), name=name,
        scratch_types=tuple(pltpu.SemaphoreType.DMA(s) for s in n_sems),
        compiler_params=pltpu.CompilerParams(collective_id=collective_id))


def _handshake(peers):
    barrier = pltpu.get_barrier_semaphore()
    for peer in peers:
        pl.semaphore_signal(barrier, inc=1, device_id=peer, device_id_type=MESH)
    pl.semaphore_wait(barrier, len(peers))


def _chip_exchange_on_sequencer(parts, name):
    n = len(parts)
    hbm = pltpu.MemorySpace.HBM
    ins = [jax.new_ref(p, memory_space=hbm) for p in parts]
    outs = [jax.empty_ref(jax.ShapeDtypeStruct(p.shape, p.dtype), memory_space=hbm) for p in parts]

    @_sequencer(name, ((n, 3), (n, 3), (n,)), 2)
    def launch(send, recv, lsem):
        x, y, c = lax.axis_index("x"), lax.axis_index("y"), lax.axis_index("c")
        mine = 2 * x + y
        chips = [(1 - x, y), (x, 1 - y), (1 - x, 1 - y)]
        _handshake([(*chip, c) for chip in chips])
        local = [pltpu.make_async_copy(ins[w].at[mine], outs[w].at[mine], lsem.at[w]) for w in range(n)]
        for cp in local:
            cp.start()
        sends = []
        for w in range(n):
            for j, (px, py) in enumerate(chips):
                cp = pltpu.make_async_remote_copy(ins[w].at[2 * px + py], outs[w].at[mine],
                                                  send.at[w, j], recv.at[w, j],
                                                  device_id=(px, py, c), device_id_type=MESH)
                cp.start()
                sends.append(cp)
        for w in range(n):
            for j, (px, py) in enumerate(chips):
                pltpu.make_async_remote_copy(ins[w].at[mine], outs[w].at[2 * px + py],
                                             send.at[w, j], recv.at[w, j],
                                             device_id=(px, py, c), device_id_type=MESH).wait_recv()
        for cp in sends:
            cp.wait_send()
        for cp in local:
            cp.wait()

    launch()
    return [o[...] for o in outs]


GRAD_TILES = (dict(blocks_on="cols", tm=512, tn=1280), dict(blocks_on="cols", tm=512, tn=256),
              dict(blocks_on="cols", tm=512, tn=256), dict(blocks_on="rows", tm=256, tn=2048),
              dict(blocks_on="cols", tm=1024, tn=1024), dict(blocks_on="rows", tm=1024, tn=1024))


def _mm_tn_pair(a, b, *, blocks_on, tm, tn, name):
    t_len, m = a.shape
    n = b.shape[1]
    if blocks_on == "rows":
        rows, cols, inner = m // N_DEV, n, n // tn
        assert tm == rows
        a_spec = pl.BlockSpec((t_len, tm), lambda p, t, blk: (0, blk[p]))
        b_spec = pl.BlockSpec((t_len, tn), lambda p, t, blk: (0, t))
        out_spec = pl.BlockSpec((None, tm, tn), lambda p, t, blk: (
            jnp.maximum(p - N_CHIP, 0), 0, jnp.where(p < N_CHIP, 0, t)))
    else:
        rows, cols, inner = m, n // N_DEV, m // tm
        assert tn == cols
        a_spec = pl.BlockSpec((t_len, tm), lambda p, t, blk: (0, t))
        b_spec = pl.BlockSpec((t_len, tn), lambda p, t, blk: (0, blk[p]))
        out_spec = pl.BlockSpec((None, tm, tn), lambda p, t, blk: (
            jnp.maximum(p - N_CHIP, 0), jnp.where(p < N_CHIP, 0, t), 0))

    def body(blk_ref, a_ref, b_ref, o_ref, land, stage, send_sem, recv_sem):
        del blk_ref
        p, t = pl.program_id(0), pl.program_id(1)
        step = p * inner + t
        x, y, c = lax.axis_index("x"), lax.axis_index("y"), lax.axis_index("c")
        tile = _dot(a_ref[...], b_ref[...], TN)

        def to_sibling(slot, chip, piece):
            return pltpu.make_async_remote_copy(stage.at[slot], land.at[chip, piece], send_sem.at[slot],
                                                recv_sem.at[chip, piece],
                                                device_id=(x, y, 1 - c), device_id_type=MESH)

        @pl.when(p < N_CHIP)
        def _():
            slot = step % 2

            @pl.when(step >= 2)
            def _():
                to_sibling(slot, 0, 0).wait_send()

            stage[slot] = tile.astype(BF16)
            to_sibling(slot, p, t).start()

        @pl.when(step == N_CHIP * inner)
        def _():
            for slot in range(min(2, N_CHIP * inner)):
                to_sibling(slot, 0, 0).wait_send()

        @pl.when(p >= N_CHIP)
        def _():
            chip = p - N_CHIP
            to_sibling(0, chip, t).wait_recv()
            o_ref[...] = (tile + land[chip, t].astype(F32)).astype(BF16)

    c = lax.axis_index("c")
    order = jnp.stack([2 * ch + 1 - c for ch in range(N_CHIP)] + [2 * ch + c for ch in range(N_CHIP)])
    return pl.pallas_call(
        body, name=name,
        grid_spec=pltpu.PrefetchScalarGridSpec(
            num_scalar_prefetch=1, grid=(N_DEV, inner), in_specs=[a_spec, b_spec], out_specs=out_spec,
            scratch_shapes=[pltpu.VMEM((N_CHIP, inner, tm, tn), BF16), pltpu.VMEM((2, tm, tn), BF16),
                            pltpu.SemaphoreType.DMA((2,)), pltpu.SemaphoreType.DMA((N_CHIP, inner))]),
        out_shape=jax.ShapeDtypeStruct((N_CHIP, rows, cols), BF16),
        compiler_params=_params(("arbitrary", "arbitrary")),
    )(order.astype(jnp.int32), a, b)


def _adamw_math(g, w, m, v):
    m2 = B1 * m + (1.0 - B1) * g
    v2 = B2 * v + (1.0 - B2) * (g * g)
    delta = -LR * ((m2 / BC1) / (jnp.sqrt(v2 / BC2) + AEPS) + WD * w)
    return delta, m2, v2


def _adamw(parts, w, m, v, *, name, after=(), tr=256):
    rows, cols = w.shape

    def body(p_ref, w_ref, m_ref, v_ref, *rest):
        g_ref, d_ref, mo_ref, vo_ref = rest[len(after):]
        g = p_ref[0].astype(F32)
        for b in range(1, N_CHIP):
            g = g + p_ref[b].astype(F32)
        delta, m2, v2 = _adamw_math(g, w_ref[...], m_ref[...], v_ref[...])
        g_ref[...] = g
        d_ref[...] = delta
        mo_ref[...] = m2
        vo_ref[...] = v2

    spec = pl.BlockSpec((tr, cols), lambda i: (i, 0))
    shape = jax.ShapeDtypeStruct((rows, cols), F32)
    return pl.pallas_call(
        body, name=name, grid=(rows // tr,),
        in_specs=[pl.BlockSpec((N_CHIP, tr, cols), lambda i: (0, i, 0)), spec, spec, spec]
        + [pl.BlockSpec(memory_space=pl.ANY)] * len(after),
        out_specs=[spec] * 4, out_shape=[shape] * 4,
        compiler_params=_params(("parallel",)),
    )(parts, w, m, v, *after)


def _small_update(part, w, m, v):
    rows = part.shape[0]

    def body(p_ref, w_ref, m_ref, v_ref, g_ref, d_ref, mo_ref, vo_ref, buf, send, recv):
        me, peers = _me_and_peers()
        buf[me] = p_ref[...]
        sends = []
        for k, (dev, _) in enumerate(peers):
            cp = pltpu.make_async_remote_copy(p_ref, buf.at[me], send.at[k], recv.at[k],
                                              device_id=dev, device_id_type=MESH)
            cp.start()
            sends.append(cp)
        for k, (dev, idx) in enumerate(peers):
            pltpu.make_async_remote_copy(p_ref, buf.at[idx], send.at[k], recv.at[k],
                                         device_id=dev, device_id_type=MESH).wait_recv()
        for cp in sends:
            cp.wait_send()
        g = buf[0]
        for b in range(1, N_DEV):
            g = g + buf[b]
        delta, m2, v2 = _adamw_math(g, w_ref[...], m_ref[...], v_ref[...])
        g_ref[...] = g
        d_ref[...] = delta
        mo_ref[...] = m2
        vo_ref[...] = v2

    vm = pl.BlockSpec(memory_space=pltpu.VMEM)
    shape = jax.ShapeDtypeStruct((rows, HD), F32)
    return pl.pallas_call(
        body, name="small_params_update",
        in_specs=[vm] * 4, out_specs=[vm] * 4, out_shape=[shape] * 4,
        scratch_shapes=[pltpu.VMEM((N_DEV, rows, HD), F32),
                        pltpu.SemaphoreType.DMA((N_DEV - 1,)), pltpu.SemaphoreType.DMA((N_DEV - 1,))],
    )(part, w, m, v)


def _pack_small(norm_mix, b_gate, qa, ka, qb, kb, rpb, norm_ffn):
    gains = jnp.concatenate([qa, ka, qb, kb, jnp.zeros((4, HD), F32)], axis=0)
    rpb_pad = jnp.pad(rpb.reshape(4 * (2 * WIN_R - 1), 2 * WIN_C - 1), ((0, 4), (0, HD - (2 * WIN_C - 1))))
    return jnp.concatenate([norm_mix.reshape(16, HD), b_gate.reshape(32, HD), gains, rpb_pad,
                            norm_ffn.reshape(16, HD), jnp.zeros((8, HD), F32)], axis=0)


LOSS_ROW = 136


def _unpack_small(p):
    norm_mix = p[0:16].reshape(1, D)
    b_gate = p[16:48].reshape(1, 2 * D)
    qa, ka, qb, kb = (p[48 + i:49 + i] for i in range(4))
    rpb = p[56:116, :2 * WIN_C - 1].reshape(1, 4, 2 * WIN_R - 1, 2 * WIN_C - 1)
    norm_ffn = p[120:136].reshape(1, D)
    return norm_mix, b_gate, qa, ka, qb, kb, rpb, norm_ffn


def kernel(x, norm_mix, w_in, b_gate, q_norm_a, k_norm_a, q_norm_b, k_norm_b, rpb_b, w_proj_a, w_proj_b, w_out, norm_ffn, w_up, w_down, loss_target, m_norm_mix, m_w_in, m_b_gate, m_q_norm_a, m_k_norm_a, m_q_norm_b, m_k_norm_b, m_rpb_b, m_w_proj_a, m_w_proj_b, m_w_out, m_norm_ffn, m_w_up, m_w_down, v_norm_mix, v_w_in, v_b_gate, v_q_norm_a, v_k_norm_a, v_q_norm_b, v_k_norm_b, v_rpb_b, v_w_proj_a, v_w_proj_b, v_w_out, v_norm_ffn, v_w_up, v_w_down):
    big_w = (w_in[0], w_proj_a[0], w_proj_b[0], w_out[0], w_up[0], w_down[0])
    big_m = (m_w_in[0], m_w_proj_a[0], m_w_proj_b[0], m_w_out[0], m_w_up[0], m_w_down[0])
    big_v = (v_w_in[0], v_w_proj_a[0], v_w_proj_b[0], v_w_out[0], v_w_up[0], v_w_down[0])
    names = ("w_in", "w_proj_a", "w_proj_b", "w_out", "w_up", "w_down")

    shards = [_cast_bf16(w) for w in big_w]
    g_in, = _gather_on_sequencer(shards[0:1], "gather_w_in")
    g_pa, g_pb, g_out, g_up = _gather_on_sequencer(shards[1:5], "gather_w_mix_up")
    g_down, = _gather_on_sequencer(shards[5:6], "gather_w_down")
    small_w = _pack_small(norm_mix, b_gate, q_norm_a, k_norm_a, q_norm_b, k_norm_b, rpb_b, norm_ffn)
    small_m = _pack_small(m_norm_mix, m_b_gate, m_q_norm_a, m_k_norm_a, m_q_norm_b, m_k_norm_b, m_rpb_b, m_norm_ffn)
    small_v = _pack_small(v_norm_mix, v_b_gate, v_q_norm_a, v_k_norm_a, v_q_norm_b, v_k_norm_b, v_rpb_b, v_norm_ffn)

    upd = [None] * 6
    in_flight = {}

    def weight_grads(tag, operands):
        sums = {i: _mm_tn_pair(a, b, name=f"grad_{names[i]}", **GRAD_TILES[i]) for i, (a, b) in operands.items()}
        new = list(sums.values())
        in_flight.update(zip(sums, _chip_exchange_on_sequencer(new, f"chip_exchange_{tag}")))
        return new

    loss, grad_x, small_g = _local_step(
        x[0], loss_target[0], norm_mix, b_gate, small_w[48:56], small_w[56:120], norm_ffn,
        g_in, g_pa, g_pb, g_out.reshape(D, D), g_up, g_down.reshape(D_FF, D), weight_grads)

    g_norm_mix, g_b, g_gains, g_rpb, g_norm_ffn = small_g
    small_part = jnp.concatenate([g_norm_mix.reshape(16, HD), g_b.reshape(32, HD),
                                  g_gains, g_rpb, g_norm_ffn.reshape(16, HD),
                                  jnp.pad(loss, ((0, 7), (0, HD - 1)))], axis=0)
    slabs = _small_update(small_part, small_w, small_m, small_v)
    total = slabs[0][LOSS_ROW, 0]
    s_g, s_d, s_m, s_v = (_unpack_small(t) for t in slabs)

    last = grad_x
    for i, r in in_flight.items():
        upd[i] = _adamw(r, big_w[i], big_m[i], big_v[i], name=f"adamw_{names[i]}", after=[last])
        last = upd[i][0]
    b_g, b_d, b_m, b_v = ([u[j][None] for u in upd] for j in range(4))

    def order(small, big):
        nm, bg, qa, ka, qb, kb, rpb, nf = small
        w_in_, pa_, pb_, out_, up_, down_ = big
        return (nm, w_in_, bg, qa, ka, qb, kb, rpb, pa_, pb_, out_, nf, up_, down_)

    return (total, grad_x[None], *order(s_g, b_g), *order(s_d, b_d), *order(s_m, b_m), *order(s_v, b_v))
```

```python
import functools
from typing import Callable, NamedTuple

import jax
import jax.numpy as jnp
import numpy as np
from jax import lax
from jax.experimental import pallas as pl
from jax.experimental.pallas import tpu as pltpu
from jax.experimental.pallas import tpu_sc as plsc

F32 = jnp.float32
BF16 = jnp.bfloat16

N_DEV = 8
S = 2048
D = 2048
HD = 128
NH = 16
NH_A = 12
QKV = NH * HD
D_IN = 3 * QKV + 2 * D
D_BR = 512
D_FF = 4 * D
GRID_W = 64
ROWS = S // GRID_W
WIN_R = 8
WIN_C = 16
EPS = 1e-6
NEG = -1e30
SCALE = HD ** -0.5
ROPE_THETA = 10000.0
DILATIONS = (1, 4, 16)
HALF_A = 64
QB = 128

LR, B1, B2, AEPS, WD, STEP = 0.001, 0.9, 0.999, 1e-08, 0.01, 10
BC1 = 1.0 - B1 ** STEP
BC2 = 1.0 - B2 ** STEP

VMEM_LIMIT = 56 * 1024 * 1024
MESH = pl.DeviceIdType.MESH

NN = (((1,), (0,)), ((), ()))
NT = (((1,), (1,)), ((), ()))
TN = (((0,), (0,)), ((), ()))


def _params(sem):
    return pltpu.CompilerParams(dimension_semantics=sem, vmem_limit_bytes=VMEM_LIMIT)


def _matmul(a, b, *, product, grid, a_spec, b_spec, epi, out_shape, out_specs, name,
            extra=(), extra_specs=(), after=(), carried=False, rider=None):
    n_extra = len(extra)
    single = not isinstance(out_shape, (list, tuple))
    out_shape = [out_shape] if single else list(out_shape)
    out_specs = [out_specs] if single else list(out_specs)
    ride = rider(grid[0] * grid[1], lambda j, i: j * grid[1] + i) if rider else None
    r_in = list(ride.inputs) if ride else []
    n_main = len(out_shape)

    def body(a_ref, b_ref, *rest):
        ins, outs = rest[:n_extra + len(after) + len(r_in)], rest[n_extra + len(after) + len(r_in):]
        epi(product(a_ref, b_ref), ins[:n_extra], outs[:n_main])
        if ride:
            ride.body(ins[n_extra + len(after):], outs[n_main:])

    res = pl.pallas_call(
        body, name=name, grid=grid,
        in_specs=[a_spec, b_spec, *extra_specs, *[pl.BlockSpec(memory_space=pl.ANY)] * len(after),
                  *(ride.in_specs if ride else [])],
        out_specs=out_specs + (ride.out_specs if ride else []),
        out_shape=out_shape + (ride.out_shape if ride else []),
        compiler_params=_params(("arbitrary", "arbitrary") if carried else ("parallel", "parallel")),
    )(a, b, *extra, *after, *r_in)
    main = res[0] if single else res[:n_main]
    return (main, res[n_main:]) if ride else main


def _dot(x, y, dims):
    return lax.dot_general(x, y, dims, preferred_element_type=F32)


def _epi_store(acc, ex, outs):
    outs[0][...] = acc.astype(outs[0].dtype)


def _epi_residual(acc, ex, outs):
    outs[0][...] = acc + ex[0][...]


def _mm_nn(a, b3, *, tm, tn, name, out_dtypes=(F32,), epi=_epi_store, extra=(), total=False):
    m, kdim = a.shape
    g, _, ng = b3.shape
    n = g * ng
    if tn <= ng:
        npg = ng // tn
        b_spec = pl.BlockSpec((None, kdim, tn), lambda j, i: (j // npg, 0, j % npg))

        def product(a_ref, b_ref):
            return _dot(a_ref[...], b_ref[...], NN)
    else:
        gb = tn // ng
        b_spec = pl.BlockSpec((gb, kdim, ng), lambda j, i: (j, 0, 0))

        def product(a_ref, b_ref):
            return jnp.concatenate([_dot(a_ref[...], b_ref[q], NN) for q in range(gb)], axis=1)

    tile = pl.BlockSpec((tm, tn), lambda j, i: (i, j))
    shapes = [jax.ShapeDtypeStruct((m, n), dt) for dt in out_dtypes]
    specs = [tile] * len(shapes)
    if total:
        shapes.append(jax.ShapeDtypeStruct((1, 1), F32))
        specs.append(pl.BlockSpec((1, 1), lambda j, i: (0, 0)))
    single = len(shapes) == 1
    return _matmul(
        a, b3, product=product, grid=(n // tn, m // tm), epi=epi, name=name, carried=total,
        a_spec=pl.BlockSpec((tm, kdim), lambda j, i: (i, 0)), b_spec=b_spec,
        extra=extra, extra_specs=[tile] * len(extra),
        out_shape=shapes[0] if single else shapes, out_specs=specs[0] if single else specs)


def _mm_nt(a, b3, *, tm, tn, name, out_dtype=F32, epi=_epi_store, extra=(), after=(), rider=None):
    m, kdim = a.shape
    g, n, kg = b3.shape

    def product(a_ref, b_ref):
        acc = _dot(a_ref[:, 0:kg], b_ref[0], NT)
        for q in range(1, g):
            acc = acc + _dot(a_ref[:, q * kg:(q + 1) * kg], b_ref[q], NT)
        return acc

    tile = pl.BlockSpec((tm, tn), lambda j, i: (i, j))
    return _matmul(
        a, b3, product=product, grid=(n // tn, m // tm), epi=epi, name=name,
        a_spec=pl.BlockSpec((tm, kdim), lambda j, i: (i, 0)),
        b_spec=pl.BlockSpec((g, tn, kg), lambda j, i: (0, j, 0)),
        extra=extra, extra_specs=[tile] * len(extra), after=after, rider=rider,
        out_shape=jax.ShapeDtypeStruct((m, n), out_dtype), out_specs=tile)


def _mm_tn(a, b, *, tm, tn, name, groups=1, out_dtype=BF16):
    t, m = a.shape
    _, n = b.shape
    ng = n // groups
    if tn <= ng:
        npg = ng // tn
        out_spec = pl.BlockSpec((None, tm, tn), lambda j, i: (j // npg, i, j % npg))
        epi = _epi_store

        def product(a_ref, b_ref):
            return _dot(a_ref[...], b_ref[...], TN)
    else:
        gb = tn // ng
        out_spec = pl.BlockSpec((gb, tm, ng), lambda j, i: (j, i, 0))

        def product(a_ref, b_ref):
            return [_dot(a_ref[...], b_ref[:, q * ng:(q + 1) * ng], TN) for q in range(gb)]

        def epi(parts, ex, outs):
            for q, part in enumerate(parts):
                outs[0][q] = part.astype(out_dtype)

    return _matmul(
        a, b, product=product, grid=(n // tn, m // tm), epi=epi, name=name,
        a_spec=pl.BlockSpec((t, tm), lambda j, i: (0, i)),
        b_spec=pl.BlockSpec((t, tn), lambda j, i: (0, j)),
        out_shape=jax.ShapeDtypeStruct((groups, m, ng), out_dtype), out_specs=out_spec)


def _rms_fwd(x, g, *, name, tr=256):
    def body(x_ref, g_ref, y_ref, r_ref):
        xv = x_ref[...]
        r = lax.rsqrt(jnp.mean(xv * xv, axis=-1, keepdims=True) + EPS)
        y_ref[...] = (xv * r * g_ref[...]).astype(BF16)
        r_ref[...] = r

    row = pl.BlockSpec((tr, D), lambda i: (i, 0))
    return pl.pallas_call(
        body, name=name, grid=(S // tr,),
        in_specs=[row, pl.BlockSpec((1, D), lambda i: (0, 0))],
        out_specs=[row, pl.BlockSpec((tr, 1), lambda i: (i, 0))],
        out_shape=[jax.ShapeDtypeStruct((S, D), BF16), jax.ShapeDtypeStruct((S, 1), F32)],
        compiler_params=_params(("parallel",)),
    )(x, g)


def _rms_bwd(dy, x, rstd, g, resid, *, name, tr=256):
    def body(dy_ref, x_ref, r_ref, g_ref, res_ref, dx_ref, dxb_ref, dg_ref):
        r = r_ref[...]
        xh = x_ref[...] * r
        dyv = dy_ref[...]
        t = dyv * g_ref[...]
        dx = r * (t - xh * jnp.mean(t * xh, axis=-1, keepdims=True)) + res_ref[...]
        dx_ref[...] = dx
        dxb_ref[...] = dx.astype(BF16)
        part = jnp.sum(dyv * xh, axis=0, keepdims=True)

        @pl.when(pl.program_id(0) == 0)
        def _():
            dg_ref[...] = part

        @pl.when(pl.program_id(0) > 0)
        def _():
            dg_ref[...] += part

    row = pl.BlockSpec((tr, D), lambda i: (i, 0))
    vec = pl.BlockSpec((1, D), lambda i: (0, 0))
    return pl.pallas_call(
        body, name=name, grid=(S // tr,),
        in_specs=[row, row, pl.BlockSpec((tr, 1), lambda i: (i, 0)), vec, row],
        out_specs=[row, row, vec],
        out_shape=[jax.ShapeDtypeStruct((S, D), F32), jax.ShapeDtypeStruct((S, D), BF16),
                   jax.ShapeDtypeStruct((1, D), F32)],
        compiler_params=_params(("arbitrary",)),
    )(dy, x, rstd, g, resid)


def _rope_tables():
    pos = np.arange(S, dtype=np.float32)
    inv = (ROPE_THETA ** (-np.arange(0, HD, 2, dtype=np.float32) / HD)).astype(np.float32)
    ang = pos[:, None] * inv[None, :]
    cos, sin = np.cos(ang), np.sin(ang)
    return (jnp.asarray(np.concatenate([cos, cos], axis=-1), F32),
            jnp.asarray(np.concatenate([-sin, sin], axis=-1), F32))


def _swap_halves(t):
    return pltpu.roll(t, HD // 2, axis=1)


TOK = 256


def _lane_block_spec(d, last=HD):
    return pl.BlockSpec((4, TOK // d, d * last), lambda i: (0, i, 0))


def _to_lane_blocks(dst, head, val, d, scr, dtype):
    w = val.shape[1]
    if d == 1:
        dst[head] = val.astype(dtype)
        return
    scr[...] = val
    for r in range(d):
        dst[head, :, r * w:(r + 1) * w] = scr[pl.ds(r, TOK // d, stride=d), :].astype(dtype)


def _from_lane_blocks(src, head, d, w, scr):
    if d == 1:
        return src[head].astype(F32)
    for r in range(d):
        scr[pl.ds(r, TOK // d, stride=d), :] = src[head, :, r * w:(r + 1) * w].astype(F32)
    return scr[...]


def _qk_prep(proj, gains, cos2, sin2):
    def body(q_ref, k_ref, v_ref, g_ref, c_ref, s_ref, *rest):
        outs, scr = rest[:-1], rest[-1]
        cos, sin = c_ref[...], s_ref[...]
        for which, (src, row_a, row_b) in enumerate(((q_ref, 0, 2), (k_ref, 1, 3), (v_ref, None, None))):
            for h in range(NH):
                y = src[:, h * HD:(h + 1) * HD]
                if row_a is not None:
                    y = y * lax.rsqrt(jnp.mean(y * y, axis=-1, keepdims=True) + EPS)
                    if h < NH_A:
                        y = y * g_ref[row_a:row_a + 1, :]
                        y = y * cos + _swap_halves(y) * sin
                    else:
                        y = y * g_ref[row_b:row_b + 1, :]
                if h < NH_A:
                    gi = h // 4
                    _to_lane_blocks(outs[3 * gi + which], h % 4, y, DILATIONS[gi], scr, BF16)
                else:
                    hb = h - NH_A
                    outs[9 + which][:, hb * HD:(hb + 1) * HD] = y.astype(BF16)

    def blk(c):
        return pl.BlockSpec((TOK, QKV), lambda i: (i, c))
    tab = pl.BlockSpec((TOK, HD), lambda i: (i, 0))
    out_specs, out_shape = [], []
    for d in DILATIONS:
        out_specs += [_lane_block_spec(d)] * 3
        out_shape += [jax.ShapeDtypeStruct((4, S // d, d * HD), BF16)] * 3
    out_specs += [pl.BlockSpec((TOK, D_BR), lambda i: (i, 0))] * 3
    out_shape += [jax.ShapeDtypeStruct((S, D_BR), BF16)] * 3
    outs = pl.pallas_call(
        body, name="qk_prep", grid=(S // TOK,),
        in_specs=[blk(0), blk(1), blk(2), pl.BlockSpec((8, HD), lambda i: (0, 0)), tab, tab],
        out_specs=out_specs, out_shape=out_shape,
        scratch_shapes=[pltpu.VMEM((TOK, HD), F32)],
        compiler_params=_params(("parallel",)),
    )(proj, proj, proj, gains, cos2, sin2)
    return [tuple(outs[3 * gi:3 * gi + 3]) for gi in range(3)], tuple(outs[9:12])


def _qk_prep_bwd(dproj, proj, gains, cos2, sin2, grads_a, grads_b):
    def body(dp_in, q_ref, k_ref, g_ref, c_ref, s_ref, *rest):
        grads, (dp_out, dg_ref, scr) = rest[:12], rest[12:]
        del dp_in
        cos, sin = c_ref[...], s_ref[...]

        def grad_of(which, h):
            if h < NH_A:
                gi = h // 4
                return _from_lane_blocks(grads[3 * gi + which], h % 4, DILATIONS[gi], HD, scr)
            hb = h - NH_A
            return grads[9 + which][:, hb * HD:(hb + 1) * HD]

        dg_rows = []
        for which, (src, base, row_a, row_b) in enumerate(((q_ref, 0, 0, 2), (k_ref, QKV, 1, 3))):
            dg_a = jnp.zeros((1, HD), F32)
            dg_b = jnp.zeros((1, HD), F32)
            for h in range(NH):
                t = src[:, h * HD:(h + 1) * HD]
                dy = grad_of(which, h)
                r = lax.rsqrt(jnp.mean(t * t, axis=-1, keepdims=True) + EPS)
                xh = t * r
                if h < NH_A:
                    dy = dy * cos - _swap_halves(dy) * sin
                    gain = g_ref[row_a:row_a + 1, :]
                    dg_a = dg_a + jnp.sum(dy * xh, axis=0, keepdims=True)
                else:
                    gain = g_ref[row_b:row_b + 1, :]
                    dg_b = dg_b + jnp.sum(dy * xh, axis=0, keepdims=True)
                u = dy * gain
                dx = r * (u - xh * jnp.mean(u * xh, axis=-1, keepdims=True))
                dp_out[:, base + h * HD:base + (h + 1) * HD] = dx.astype(BF16)
            dg_rows += [(row_a, dg_a), (row_b, dg_b)]
        for h in range(NH):
            dp_out[:, 2 * QKV + h * HD:2 * QKV + (h + 1) * HD] = grad_of(2, h).astype(BF16)

        @pl.when(pl.program_id(0) == 0)
        def _():
            dg_ref[...] = jnp.zeros((8, HD), F32)

        for row, val in dg_rows:
            dg_ref[row:row + 1, :] += val

    def blk(c):
        return pl.BlockSpec((TOK, QKV), lambda i: (i, c))
    tab = pl.BlockSpec((TOK, HD), lambda i: (i, 0))
    gain_spec = pl.BlockSpec((8, HD), lambda i: (0, 0))
    grad_specs = [s for d in DILATIONS for s in [_lane_block_spec(d)] * 3]
    grad_specs += [pl.BlockSpec((TOK, D_BR), lambda i: (i, 0))] * 3
    return pl.pallas_call(
        body, name="qk_prep_bwd", grid=(S // TOK,),
        in_specs=[pl.BlockSpec(memory_space=pl.ANY), blk(0), blk(1), gain_spec, tab, tab] + grad_specs,
        out_specs=[pl.BlockSpec((TOK, 3 * QKV), lambda i: (i, 0)), gain_spec],
        out_shape=[jax.ShapeDtypeStruct((S, D_IN), BF16), jax.ShapeDtypeStruct((8, HD), F32)],
        input_output_aliases={0: 0},
        scratch_shapes=[pltpu.VMEM((TOK, HD), F32)],
        compiler_params=_params(("arbitrary",)),
    )(dproj, proj, proj, gains, cos2, sin2, *[g for grp in grads_a for g in grp], *grads_b)


def _mix_fwd(oa, ob, w_pa, w_pb, proj, b_gate, *, tr=256):
    def body(oa_ref, ob_ref, pa_ref, pb_ref, la_ref, lb_ref, ba_ref, bb_ref, mix_ref, ya_ref, yb_ref):
        ya = jnp.concatenate([_dot(oa_ref[...], pa_ref[q], NN) for q in range(N_DEV)], axis=1)
        yb = jnp.concatenate([_dot(ob_ref[...], pb_ref[q], NN) for q in range(N_DEV)], axis=1)
        ga = jax.nn.sigmoid(la_ref[...] + ba_ref[...])
        gb = jax.nn.sigmoid(lb_ref[...] + bb_ref[...])
        mix_ref[...] = (ga * ya + gb * yb).astype(BF16)
        ya_ref[...] = ya.astype(BF16)
        yb_ref[...] = yb.astype(BF16)

    row = pl.BlockSpec((tr, D), lambda i: (i, 0))
    branch = pl.BlockSpec((tr, D_BR), lambda i: (i, 0))
    whole = pl.BlockSpec((N_DEV, D_BR, D // N_DEV), lambda i: (0, 0, 0))
    return pl.pallas_call(
        body, name="mix_fwd", grid=(S // tr,),
        in_specs=[branch, branch, whole, whole,
                  pl.BlockSpec((tr, D), lambda i: (i, 3)), pl.BlockSpec((tr, D), lambda i: (i, 4)),
                  pl.BlockSpec((1, D), lambda i: (0, 0)), pl.BlockSpec((1, D), lambda i: (0, 1))],
        out_specs=[row, row, row], out_shape=[jax.ShapeDtypeStruct((S, D), BF16)] * 3,
        compiler_params=_params(("parallel",)),
    )(oa, ob, w_pa, w_pb, proj, proj, b_gate, b_gate)


def _mix_bwd(dh1b, w_out, proj, b_gate, ya, yb, *, tr=256):
    def body(dh_ref, w_ref, la_ref, lb_ref, b_ref, ya_ref, yb_ref, dya_ref, dyb_ref, dp_ref, db_ref):
        dm = _dot(dh_ref[...], w_ref[...], NT)
        parts = []
        for l_ref, y_ref, dy_ref, lo in ((la_ref, ya_ref, dya_ref, 0), (lb_ref, yb_ref, dyb_ref, D)):
            g = jax.nn.sigmoid(l_ref[...] + b_ref[:, lo:lo + D])
            dy_ref[...] = (dm * g).astype(BF16)
            dl = dm * y_ref[...].astype(F32) * g * (1.0 - g)
            dp_ref[:, lo:lo + D] = dl.astype(BF16)
            parts.append(jnp.sum(dl, axis=0, keepdims=True))
        part = jnp.concatenate(parts, axis=1)

        @pl.when(pl.program_id(0) == 0)
        def _():
            db_ref[...] = part

        @pl.when(pl.program_id(0) > 0)
        def _():
            db_ref[...] += part

    row = pl.BlockSpec((tr, D), lambda i: (i, 0))
    vec = pl.BlockSpec((1, 2 * D), lambda i: (0, 0))
    gate_cols = pl.BlockSpec((pl.Element(tr), pl.Element(2 * D)), lambda i: (i * tr, 3 * QKV))
    return pl.pallas_call(
        body, name="mix_bwd", grid=(S // tr,),
        in_specs=[row, pl.BlockSpec((D, D), lambda i: (0, 0)),
                  pl.BlockSpec((tr, D), lambda i: (i, 3)), pl.BlockSpec((tr, D), lambda i: (i, 4)), vec, row, row],
        out_specs=[row, row, gate_cols, vec],
        out_shape=[jax.ShapeDtypeStruct((S, D), BF16), jax.ShapeDtypeStruct((S, D), BF16),
                   jax.ShapeDtypeStruct((S, D_IN), BF16), jax.ShapeDtypeStruct((1, 2 * D), F32)],
        compiler_params=_params(("arbitrary",)),
    )(dh1b, w_out, proj, proj, b_gate, ya, yb)


def _band_blocks(m_len):
    wk = min(m_len, QB + 2 * QB)
    return [(qb * QB, min(max(qb * QB - QB, 0), m_len - wk), wk) for qb in range(m_len // QB)]


def _band_scores(q, kw, q0, k0, wk):
    s = _dot(q, kw, NT) * SCALE
    qpos = q0 + lax.broadcasted_iota(jnp.int32, (QB, 1), 0)
    kpos = k0 + lax.broadcasted_iota(jnp.int32, (1, wk), 1)
    return jnp.where(jnp.abs(kpos - qpos) <= HALF_A, s, NEG)


def _attn_a_fwd(q, k, v, gi):
    d = DILATIONS[gi]
    m_len = S // d

    def body(q_ref, k_ref, v_ref, o_ref, lse_ref):
        for r in range(d):
            lanes = slice(r * HD, (r + 1) * HD)
            for q0, k0, wk in _band_blocks(m_len):
                s = _band_scores(q_ref[q0:q0 + QB, lanes], k_ref[k0:k0 + wk, lanes], q0, k0, wk)
                m = jnp.max(s, axis=-1, keepdims=True)
                p = jnp.exp(s - m)
                l = jnp.sum(p, axis=-1, keepdims=True)
                o_ref[q0:q0 + QB, lanes] = _dot(p.astype(BF16), v_ref[k0:k0 + wk, lanes], NN) / l
                lse_ref[q0:q0 + QB, r:r + 1] = m + jnp.log(l)

    head = pl.BlockSpec((None, m_len, d * HD), lambda h: (h, 0, 0))
    stat = pl.BlockSpec((None, m_len, d), lambda h: (h, 0, 0))
    return pl.pallas_call(
        body, name=f"attn_a_fwd_{gi}", grid=(4,),
        in_specs=[head, head, head], out_specs=[head, stat],
        out_shape=[jax.ShapeDtypeStruct((4, m_len, d * HD), F32), jax.ShapeDtypeStruct((4, m_len, d), F32)],
        compiler_params=_params(("parallel",)),
    )(q, k, v)


def _combine_a(os, lses):
    def body(o0, o1, o2, l0, l1, l2, oa_ref, lse_ref, scr, scr1):
        for h in range(4):
            o = [_from_lane_blocks(ref, h, d, HD, scr) for ref, d in zip((o0, o1, o2), DILATIONS)]
            a, b, c = (_from_lane_blocks(ref, h, d, 1, scr1) for ref, d in zip((l0, l1, l2), DILATIONS))
            m = jnp.maximum(jnp.maximum(a, b), c)
            wa, wb, wc = jnp.exp(a - m), jnp.exp(b - m), jnp.exp(c - m)
            tot = wa + wb + wc
            oa_ref[:, h * HD:(h + 1) * HD] = ((wa * o[0] + wb * o[1] + wc * o[2]) / tot).astype(BF16)
            lse_ref[h] = m + jnp.log(tot)

    return pl.pallas_call(
        body, name="combine_a", grid=(S // TOK,),
        in_specs=[_lane_block_spec(d) for d in DILATIONS] + [_lane_block_spec(d, 1) for d in DILATIONS],
        out_specs=[pl.BlockSpec((TOK, D_BR), lambda i: (i, 0)), pl.BlockSpec((4, TOK, 1), lambda i: (0, i, 0))],
        out_shape=[jax.ShapeDtypeStruct((S, D_BR), BF16), jax.ShapeDtypeStruct((4, S, 1), F32)],
        scratch_shapes=[pltpu.VMEM((TOK, HD), F32), pltpu.VMEM((TOK, 1), F32)],
        compiler_params=_params(("parallel",)),
    )(*os, *lses)


def _proj_a_bwd(dya, w_pa, oa, lse):
    kg = D // N_DEV

    def body(dy_ref, w_ref, o_ref, l_ref, *rest):
        outs, (scr, scr1) = rest[:9], rest[9:]
        doa = _dot(dy_ref[:, 0:kg], w_ref[0], NT)
        for q in range(1, N_DEV):
            doa = doa + _dot(dy_ref[:, q * kg:(q + 1) * kg], w_ref[q], NT)
        for h in range(4):
            do = doa[:, h * HD:(h + 1) * HD]
            dsum = jnp.sum(do * o_ref[:, h * HD:(h + 1) * HD].astype(F32), axis=-1, keepdims=True)
            for gi, d in enumerate(DILATIONS):
                _to_lane_blocks(outs[3 * gi], h, do, d, scr, BF16)
                _to_lane_blocks(outs[3 * gi + 1], h, l_ref[h], d, scr1, F32)
                _to_lane_blocks(outs[3 * gi + 2], h, dsum, d, scr1, F32)

    row = pl.BlockSpec((TOK, D_BR), lambda i: (i, 0))
    out_specs, out_shape = [], []
    for d in DILATIONS:
        out_specs += [_lane_block_spec(d), _lane_block_spec(d, 1), _lane_block_spec(d, 1)]
        out_shape += [jax.ShapeDtypeStruct((4, S // d, d * HD), BF16)] + [jax.ShapeDtypeStruct((4, S // d, d), F32)] * 2
    outs = pl.pallas_call(
        body, name="proj_a_bwd", grid=(S // TOK,),
        in_specs=[pl.BlockSpec((TOK, D), lambda i: (i, 0)),
                  pl.BlockSpec((N_DEV, D_BR, kg), lambda i: (0, 0, 0)),
                  row, pl.BlockSpec((4, TOK, 1), lambda i: (0, i, 0))],
        out_specs=out_specs, out_shape=out_shape,
        scratch_shapes=[pltpu.VMEM((TOK, HD), F32), pltpu.VMEM((TOK, 1), F32)],
        compiler_params=_params(("parallel",)),
    )(dya, w_pa, oa, lse)
    return [tuple(outs[3 * gi:3 * gi + 3]) for gi in range(3)]


def _attn_a_bwd(q, k, v, do, lse, dsum, gi):
    d = DILATIONS[gi]
    m_len = S // d

    def body(q_ref, k_ref, v_ref, do_ref, lse_ref, dsum_ref, dq_ref, dk_ref, dv_ref):
        dk_ref[...] = jnp.zeros((m_len, d * HD), F32)
        dv_ref[...] = jnp.zeros((m_len, d * HD), F32)
        for r in range(d):
            lanes = slice(r * HD, (r + 1) * HD)
            for q0, k0, wk in _band_blocks(m_len):
                rows, keys = slice(q0, q0 + QB), slice(k0, k0 + wk)
                qv, kw, vw, dov = q_ref[rows, lanes], k_ref[keys, lanes], v_ref[keys, lanes], do_ref[rows, lanes]
                p = jnp.exp(_band_scores(qv, kw, q0, k0, wk) - lse_ref[rows, r:r + 1])
                ds = (p * (_dot(dov, vw, NT) - dsum_ref[rows, r:r + 1]) * SCALE).astype(BF16)
                dq_ref[rows, lanes] = _dot(ds, kw, NN)
                dk_ref[keys, lanes] += _dot(ds, qv, TN)
                dv_ref[keys, lanes] += _dot(p.astype(BF16), dov, TN)

    head = pl.BlockSpec((None, m_len, d * HD), lambda h: (h, 0, 0))
    stat = pl.BlockSpec((None, m_len, d), lambda h: (h, 0, 0))
    shape = jax.ShapeDtypeStruct((4, m_len, d * HD), F32)
    return pl.pallas_call(
        body, name=f"attn_a_bwd_{gi}", grid=(4,),
        in_specs=[head, head, head, head, stat, stat], out_specs=[head, head, head],
        out_shape=[shape, shape, shape],
        compiler_params=_params(("parallel",)),
    )(q, k, v, do, lse, dsum)


KEYS_B = WIN_R * GRID_W
N_OFF = WIN_R


def _bias_constants():
    q = np.arange(GRID_W)[:, None]
    kc = np.arange(GRID_W)[None, :]
    dc = np.clip(kc - q, -(WIN_C - 1), WIN_C - 1) + (WIN_C - 1)
    expand = np.zeros((HD, GRID_W * GRID_W), np.float32)
    expand[dc.reshape(-1), np.arange(GRID_W * GRID_W)] = 1.0
    cs = np.clip(q - WIN_C // 2, 0, GRID_W - WIN_C)
    keep = ((kc >= cs) & (kc < cs + WIN_C)).reshape(1, -1).astype(np.float32)
    sel = np.zeros((64, 4 * N_OFF * WIN_R), np.float32)
    for h in range(4):
        for off in range(N_OFF):
            for j in range(WIN_R):
                sel[h * (2 * WIN_R - 1) + off + j, (h * N_OFF + off) * WIN_R + j] = 1.0
    return jnp.asarray(expand), jnp.asarray(keep), jnp.asarray(sel)


def _bias_expand(rpb_pad, expand, keep, sel):
    def body(r_ref, e_ref, k_ref, s_ref, o_ref):
        t = lax.dot_general(r_ref[...], e_ref[...], NN, precision=lax.Precision.HIGHEST,
                            preferred_element_type=F32)
        rows = lax.dot_general(s_ref[...], t, TN, precision=lax.Precision.HIGHEST,
                               preferred_element_type=F32)
        o_ref[...] = jnp.where(k_ref[...] > 0.5, rows, NEG)

    return pl.pallas_call(
        body, name="bias_expand",
        out_shape=jax.ShapeDtypeStruct((4 * N_OFF * WIN_R, GRID_W * GRID_W), F32),
        compiler_params=pltpu.CompilerParams(vmem_limit_bytes=VMEM_LIMIT),
    )(rpb_pad, expand, keep, sel)


def _bias_reduce(dbias_rows, expand, sel):
    def body(x_ref, e_ref, s_ref, o_ref):
        z = lax.dot_general(x_ref[...], e_ref[...], NT, precision=lax.Precision.HIGHEST,
                            preferred_element_type=F32)
        o_ref[...] = lax.dot_general(s_ref[...], z, NN, precision=lax.Precision.HIGHEST,
                                     preferred_element_type=F32)

    return pl.pallas_call(
        body, name="bias_reduce", out_shape=jax.ShapeDtypeStruct((64, HD), F32),
        compiler_params=pltpu.CompilerParams(vmem_limit_bytes=VMEM_LIMIT),
    )(dbias_rows, expand, sel)


def _rows_to_tab(rows):
    t = rows.reshape(4, N_OFF, WIN_R, GRID_W, GRID_W)
    return t.transpose(0, 1, 3, 2, 4).reshape(4, N_OFF, GRID_W, KEYS_B)


def _tab_to_rows(tab):
    t = tab.reshape(4, N_OFF, GRID_W, WIN_R, GRID_W)
    return t.transpose(0, 1, 3, 2, 4).reshape(4 * N_OFF * WIN_R, GRID_W * GRID_W)


def _row_window(r):
    r0 = jnp.clip(r - WIN_R // 2, 0, ROWS - WIN_R)
    off = r0 + (WIN_R - 1) - r
    return pl.multiple_of(r * GRID_W, GRID_W), pl.multiple_of(r0 * GRID_W, GRID_W), off


def _attn_b_fwd(qn, kn, vb, bias_tab):
    def body(q_ref, k_ref, v_ref, b_ref, o_ref, lse_ref):
        def row(r, carry):
            qs, ks, off = _row_window(r)
            q = q_ref[pl.ds(qs, GRID_W), :]
            s = lax.dot_general(q, k_ref[pl.ds(ks, KEYS_B), :], NT, preferred_element_type=F32) * SCALE
            s = s + b_ref[off]
            m = jnp.max(s, axis=-1, keepdims=True)
            p = jnp.exp(s - m)
            l = jnp.sum(p, axis=-1, keepdims=True)
            o = lax.dot_general(p.astype(BF16), v_ref[pl.ds(ks, KEYS_B), :], NN, preferred_element_type=F32)
            o_ref[pl.ds(qs, GRID_W), :] = (o / l).astype(BF16)
            lse_ref[pl.ds(qs, GRID_W), :] = m + jnp.log(l)
            return carry

        lax.fori_loop(0, ROWS, row, 0, unroll=2)

    full = pl.BlockSpec((S, HD), lambda h: (0, h))
    return pl.pallas_call(
        body, name="attn_b_fwd", grid=(4,),
        in_specs=[full, full, full, pl.BlockSpec((None, N_OFF, GRID_W, KEYS_B), lambda h: (h, 0, 0, 0))],
        out_specs=[pl.BlockSpec((S, HD), lambda h: (0, h)), pl.BlockSpec((None, S, 1), lambda h: (h, 0, 0))],
        out_shape=[jax.ShapeDtypeStruct((S, D_BR), BF16), jax.ShapeDtypeStruct((4, S, 1), F32)],
        compiler_params=_params(("parallel",)),
    )(qn, kn, vb, bias_tab)


def _attn_b_bwd(qn, kn, vb, bias_tab, ob, dob, lse):
    def body(q_ref, k_ref, v_ref, b_ref, o_ref, do_ref, lse_ref, dq_ref, dk_ref, dv_ref, db_ref):
        dk_ref[...] = jnp.zeros((S, HD), F32)
        dv_ref[...] = jnp.zeros((S, HD), F32)
        db_ref[...] = jnp.zeros((N_OFF, GRID_W, KEYS_B), F32)

        def row(r, carry):
            qs, ks, off = _row_window(r)
            rows = pl.ds(qs, GRID_W)
            keys = pl.ds(ks, KEYS_B)
            q = q_ref[rows, :]
            kw = k_ref[keys, :]
            s = lax.dot_general(q, kw, NT, preferred_element_type=F32) * SCALE + b_ref[off]
            p = jnp.exp(s - lse_ref[rows, :])
            do = do_ref[rows, :]
            dobf = do.astype(BF16)
            dsum = jnp.sum(do * o_ref[rows, :].astype(F32), axis=-1, keepdims=True)
            dp = lax.dot_general(dobf, v_ref[keys, :], NT, preferred_element_type=F32)
            ds = p * (dp - dsum)
            db_ref[off] += ds
            dsb = (ds * SCALE).astype(BF16)
            dq_ref[rows, :] = lax.dot_general(dsb, kw, NN, preferred_element_type=F32)
            dk_ref[keys, :] += lax.dot_general(dsb, q, TN, preferred_element_type=F32)
            dv_ref[keys, :] += lax.dot_general(p.astype(BF16), dobf, TN, preferred_element_type=F32)
            return carry

        lax.fori_loop(0, ROWS, row, 0, unroll=2)

    full = pl.BlockSpec((S, HD), lambda h: (0, h))
    slot = pl.BlockSpec((S, HD), lambda h: (0, h))
    tab = pl.BlockSpec((None, N_OFF, GRID_W, KEYS_B), lambda h: (h, 0, 0, 0))
    shape = jax.ShapeDtypeStruct((S, D_BR), F32)
    return pl.pallas_call(
        body, name="attn_b_bwd", grid=(4,),
        in_specs=[full, full, full, tab, slot, slot, pl.BlockSpec((None, S, 1), lambda h: (h, 0, 0))],
        out_specs=[slot, slot, slot, tab],
        out_shape=[shape, shape, shape, jax.ShapeDtypeStruct((4, N_OFF, GRID_W, KEYS_B), F32)],
        compiler_params=_params(("parallel",)),
    )(qn, kn, vb, bias_tab, ob, dob, lse)


def _epi_relu_sq(acc, ex, outs):
    u = jnp.maximum(acc, 0.0)
    outs[0][...] = u.astype(BF16)
    outs[1][...] = (u * u).astype(BF16)


def _epi_relu_sq_bwd(acc, ex, outs):
    outs[0][...] = (acc * (2.0 * ex[0][...].astype(F32))).astype(BF16)


def _epi_loss_head(acc, ex, outs):
    e = acc + ex[0][...] - ex[1][...]
    dy = e * (1.0 / D)
    outs[0][...] = dy
    outs[1][...] = dy.astype(BF16)
    part = (0.5 / D) * jnp.sum(jnp.sum(e * e, axis=-1, keepdims=True), axis=0, keepdims=True)
    first = (pl.program_id(0) == 0) & (pl.program_id(1) == 0)

    @pl.when(first)
    def _():
        outs[2][...] = part

    @pl.when(jnp.logical_not(first))
    def _():
        outs[2][...] += part


def _local_step(x, target, norm_mix, b_gate, gains, rpb_pad, norm_ffn,
                w_in, w_pa, w_pb, w_out, w_up, w_down, weight_grads, proj_bwd_rider=None):
    cos2, sin2 = _rope_tables()
    expand, keep, sel = _bias_constants()
    w_out3, w_down3 = w_out[None], w_down[None]

    xn, rstd1 = _rms_fwd(x, norm_mix, name="rms_mix")
    proj = _mm_nn(xn, w_in, tm=1024, tn=1280, name="proj")
    qkv_a, qkv_b = _qk_prep(proj, gains, cos2, sin2)
    fwd_a = [_attn_a_fwd(*qkv_a[gi], gi) for gi in range(3)]
    oa, lse_a = _combine_a([o for o, _ in fwd_a], [l for _, l in fwd_a])
    bias_tab = _rows_to_tab(_bias_expand(rpb_pad, expand, keep, sel))
    ob, lse_b = _attn_b_fwd(*qkv_b, bias_tab)
    mixed, ya, yb = _mix_fwd(oa, ob, w_pa, w_pb, proj, b_gate)
    h1 = _mm_nn(mixed, w_out3, tm=1024, tn=1024, name="out_proj", epi=_epi_residual, extra=(x,))
    hn, rstd2 = _rms_fwd(h1, norm_ffn, name="rms_ffn")
    u, usq = _mm_nn(hn, w_up, tm=1024, tn=1024, name="ffn_up", epi=_epi_relu_sq,
                    out_dtypes=(BF16, BF16))
    dy, dyb, loss = _mm_nn(usq, w_down3, tm=512, tn=512, name="ffn_down", epi=_epi_loss_head,
                           extra=(h1, target), out_dtypes=(F32, BF16), total=True)

    sent = weight_grads("w_down", {5: (usq, dyb)})
    du = _mm_nt(dyb, w_down3, tm=1024, tn=1024, name="ffn_down_bwd", out_dtype=BF16,
                epi=_epi_relu_sq_bwd, extra=(u,), after=sent)
    sent = weight_grads("w_up", {4: (hn, du)})
    dhn = _mm_nt(du, w_up, tm=512, tn=512, name="ffn_up_bwd", after=sent)
    dh1, dh1b, g_norm_ffn = _rms_bwd(dhn, h1, rstd2, norm_ffn, dy, name="rms_ffn_bwd")

    dya, dyb2, dproj, g_b = _mix_bwd(dh1b, w_out, proj, b_gate, ya, yb)
    sent = weight_grads("w_mix", {3: (mixed, dh1b), 1: (oa, dya), 2: (ob, dyb2)})
    dob = _mm_nt(dyb2, w_pb, tm=1024, tn=D_BR, name="proj_b_bwd", after=sent)
    prep = _proj_a_bwd(dya, w_pa, oa, lse_a)
    grads_a = [_attn_a_bwd(*qkv_a[gi], *prep[gi], gi) for gi in range(3)]
    dqb, dkb, dvb, dbias = _attn_b_bwd(*qkv_b, bias_tab, ob, dob, lse_b)
    g_rpb = _bias_reduce(_tab_to_rows(dbias), expand, sel)
    dproj, g_gains = _qk_prep_bwd(dproj, proj, gains, cos2, sin2, grads_a, (dqb, dkb, dvb))
    sent = weight_grads("w_in", {0: (xn, dproj)})
    if proj_bwd_rider is None:
        dxn = _mm_nt(dproj, w_in, tm=256, tn=512, name="proj_bwd", after=sent)
    else:
        rider, take = proj_bwd_rider()
        dxn, rode = _mm_nt(dproj, w_in, tm=256, tn=512, name="proj_bwd", after=sent, rider=rider)
        take(rode)
    grad_x, _, g_norm_mix = _rms_bwd(dxn, x, rstd1, norm_mix, dh1, name="rms_mix_bwd")

    small = (g_norm_mix, g_b, g_gains, g_rpb, g_norm_ffn)
    return loss, grad_x, small


def _cast_bf16(w, *, tr=256):
    rows, cols = w.shape
    tr = min(tr, rows)

    def body(w_ref, o_ref):
        o_ref[...] = w_ref[...].astype(BF16)

    spec = pl.BlockSpec((tr, cols), lambda i: (i, 0))
    return pl.pallas_call(
        body, name=f"cast_{rows}x{cols}", grid=(rows // tr,), in_specs=[spec], out_specs=spec,
        out_shape=jax.ShapeDtypeStruct((rows, cols), BF16), compiler_params=_params(("parallel",)),
    )(w)


def _me_and_peers():
    x, y, c = lax.axis_index("x"), lax.axis_index("y"), lax.axis_index("c")
    me = 4 * x + 2 * y + c
    peers = []
    for k in range(1, N_DEV):
        px = 1 - x if k & 4 else x
        py = 1 - y if k & 2 else y
        pc = 1 - c if k & 1 else c
        peers.append(((px, py, pc), 4 * px + 2 * py + pc))
    return me, peers


def _gather_on_sequencer(shards, name):
    n = len(shards)
    hbm = pltpu.MemorySpace.HBM
    ins = [jax.new_ref(s, memory_space=hbm) for s in shards]
    outs = [jax.empty_ref(jax.ShapeDtypeStruct((N_DEV,) + s.shape, s.dtype), memory_space=hbm) for s in shards]

    @pl.kernel(mesh=plsc.ScalarSubcoreMesh(axis_name="seq", num_cores=1), name=name,
               scratch_types=(pltpu.SemaphoreType.DMA((n, N_DEV - 1)), pltpu.SemaphoreType.DMA((n, N_DEV - 1)),
                              pltpu.SemaphoreType.DMA((n,))),
               compiler_params=pltpu.CompilerParams(collective_id=0))
    def launch(send, recv, lsem):
        x, y, c = lax.axis_index("x"), lax.axis_index("y"), lax.axis_index("c")
        me, sibling = (x, y, c), (x, y, 1 - c)
        chips = [(1 - x, y), (x, 1 - y), (1 - x, 1 - y)]
        barrier = pltpu.get_barrier_semaphore()
        for peer in [sibling] + [(*chip, c) for chip in chips]:
            pl.semaphore_signal(barrier, inc=1, device_id=peer, device_id_type=MESH)
        pl.semaphore_wait(barrier, 4)

        def copy(w, k, block, to, src=None):
            px, py, pc = block
            dst = outs[w].at[4 * px + 2 * py + pc]
            return pltpu.make_async_remote_copy(dst if src is None else src, dst, send.at[w, k], recv.at[w, k],
                                                device_id=to, device_id_type=MESH)

        local = [pltpu.make_async_copy(ins[w], outs[w].at[4 * x + 2 * y + c], lsem.at[w]) for w in range(n)]
        for cp in local:
            cp.start()
        first = []
        for w in range(n):
            first += [copy(w, 1 + j, me, (*chip, c), src=ins[w]) for j, chip in enumerate(chips)]
            first.append(copy(w, 0, me, sibling, src=ins[w]))
        for cp in first:
            cp.start()
        passed = []
        for w in range(n):
            for j, chip in enumerate(chips):
                copy(w, 1 + j, (*chip, c), me).wait_recv()
                cp = copy(w, 4 + j, (*chip, c), sibling)
                cp.start()
                passed.append(cp)
        for w in range(n):
            copy(w, 0, sibling, me).wait_recv()
            for j, chip in enumerate(chips):
                copy(w, 4 + j, (*chip, 1 - c), me).wait_recv()
        for cp in first + passed:
            cp.wait_send()
        for cp in local:
            cp.wait()

    launch()
    return [o[...] for o in outs]


N_CHIP = 4
CHIPS = ((0, 0), (0, 1), (1, 0), (1, 1))


def _sequencer(name, n_sems, collective_id):
    return functools.partial(
        pl.kernel, mesh=plsc.ScalarSubcoreMesh(axis_name="seq", num_cores=1), name=name,
        scratch_types=tuple(pltpu.SemaphoreType.DMA(s) for s in n_sems),
        compiler_params=pltpu.CompilerParams(collective_id=collective_id))


def _handshake(peers):
    barrier = pltpu.get_barrier_semaphore()
    for peer in peers:
        pl.semaphore_signal(barrier, inc=1, device_id=peer, device_id_type=MESH)
    pl.semaphore_wait(barrier, len(peers))


def _chip_exchange_on_sequencer(parts, name):
    n = len(parts)
    hbm = pltpu.MemorySpace.HBM
    ins = [jax.new_ref(p, memory_space=hbm) for p in parts]
    outs = [jax.empty_ref(jax.ShapeDtypeStruct(p.shape, p.dtype), memory_space=hbm) for p in parts]

    @_sequencer(name, ((n, 3), (n, 3), (n,)), 2)
    def launch(send, recv, lsem):
        x, y, c = lax.axis_index("x"), lax.axis_index("y"), lax.axis_index("c")
        mine = 2 * x + y
        chips = [(1 - x, y), (x, 1 - y), (1 - x, 1 - y)]
        _handshake([(*chip, c) for chip in chips])
        local = [pltpu.make_async_copy(ins[w].at[mine], outs[w].at[mine], lsem.at[w]) for w in range(n)]
        for cp in local:
            cp.start()
        sends = []
        for w in range(n):
            for j, (px, py) in enumerate(chips):
                cp = pltpu.make_async_remote_copy(ins[w].at[2 * px + py], outs[w].at[mine],
                                                  send.at[w, j], recv.at[w, j],
                                                  device_id=(px, py, c), device_id_type=MESH)
                cp.start()
                sends.append(cp)
        for w in range(n):
            for j, (px, py) in enumerate(chips):
                pltpu.make_async_remote_copy(ins[w].at[mine], outs[w].at[2 * px + py],
                                             send.at[w, j], recv.at[w, j],
                                             device_id=(px, py, c), device_id_type=MESH).wait_recv()
        for cp in sends:
            cp.wait_send()
        for cp in local:
            cp.wait()

    launch()
    return [o[...] for o in outs]


GRAD_TILES = (dict(blocks_on="cols", tm=512, tn=1280), dict(blocks_on="cols", tm=512, tn=256),
              dict(blocks_on="cols", tm=512, tn=256), dict(blocks_on="rows", tm=256, tn=2048),
              dict(blocks_on="cols", tm=1024, tn=1024), dict(blocks_on="rows", tm=1024, tn=1024))


def _mm_tn_pair(a, b, *, blocks_on, tm, tn, name, rider=None):
    t_len, m = a.shape
    n = b.shape[1]
    if blocks_on == "rows":
        rows, cols, inner = m // N_DEV, n, n // tn
        assert tm == rows
        a_spec = pl.BlockSpec((t_len, tm), lambda p, t, blk: (0, blk[p]))
        b_spec = pl.BlockSpec((t_len, tn), lambda p, t, blk: (0, t))
        out_spec = pl.BlockSpec((None, tm, tn), lambda p, t, blk: (
            jnp.maximum(p - N_CHIP, 0), 0, jnp.where(p < N_CHIP, 0, t)))
    else:
        rows, cols, inner = m, n // N_DEV, m // tm
        assert tn == cols
        a_spec = pl.BlockSpec((t_len, tm), lambda p, t, blk: (0, t))
        b_spec = pl.BlockSpec((t_len, tn), lambda p, t, blk: (0, blk[p]))
        out_spec = pl.BlockSpec((None, tm, tn), lambda p, t, blk: (
            jnp.maximum(p - N_CHIP, 0), jnp.where(p < N_CHIP, 0, t), 0))

    ride = rider(N_DEV * inner, lambda p, t: p * inner + t) if rider else None
    r_in = list(ride.inputs) if ride else []

    def body(blk_ref, a_ref, b_ref, *rest):
        del blk_ref
        o_ref = rest[len(r_in)]
        land, stage, send_sem, recv_sem = rest[-4:]
        if ride:
            ride.body(rest[:len(r_in)], rest[len(r_in) + 1:-4])
        p, t = pl.program_id(0), pl.program_id(1)
        step = p * inner + t
        x, y, c = lax.axis_index("x"), lax.axis_index("y"), lax.axis_index("c")
        tile = _dot(a_ref[...], b_ref[...], TN)

        def to_sibling(slot, chip, piece):
            return pltpu.make_async_remote_copy(stage.at[slot], land.at[chip, piece], send_sem.at[slot],
                                                recv_sem.at[chip, piece],
                                                device_id=(x, y, 1 - c), device_id_type=MESH)

        @pl.when(p < N_CHIP)
        def _():
            slot = step % 2

            @pl.when(step >= 2)
            def _():
                to_sibling(slot, 0, 0).wait_send()

            stage[slot] = tile.astype(BF16)
            to_sibling(slot, p, t).start()

        @pl.when(step == N_CHIP * inner)
        def _():
            for slot in range(min(2, N_CHIP * inner)):
                to_sibling(slot, 0, 0).wait_send()

        @pl.when(p >= N_CHIP)
        def _():
            chip = p - N_CHIP
            to_sibling(0, chip, t).wait_recv()
            o_ref[...] = (tile + land[chip, t].astype(F32)).astype(BF16)

    c = lax.axis_index("c")
    order = jnp.stack([2 * ch + 1 - c for ch in range(N_CHIP)] + [2 * ch + c for ch in range(N_CHIP)])
    res = pl.pallas_call(
        body, name=name,
        grid_spec=pltpu.PrefetchScalarGridSpec(
            num_scalar_prefetch=1, grid=(N_DEV, inner),
            in_specs=[a_spec, b_spec] + (ride.in_specs if ride else []),
            out_specs=[out_spec] + (ride.out_specs if ride else []),
            scratch_shapes=[pltpu.VMEM((N_CHIP, inner, tm, tn), BF16), pltpu.VMEM((2, tm, tn), BF16),
                            pltpu.SemaphoreType.DMA((2,)), pltpu.SemaphoreType.DMA((N_CHIP, inner))]),
        out_shape=[jax.ShapeDtypeStruct((N_CHIP, rows, cols), BF16)] + (ride.out_shape if ride else []),
        compiler_params=_params(("arbitrary", "arbitrary")),
    )(order.astype(jnp.int32), a, b, *r_in)
    return (res[0], res[1:]) if ride else res[0]


def _adamw_math(g, w, m, v):
    m2 = B1 * m + (1.0 - B1) * g
    v2 = B2 * v + (1.0 - B2) * (g * g)
    delta = -LR * ((m2 / BC1) / (jnp.sqrt(v2 / BC2) + AEPS) + WD * w)
    return delta, m2, v2


def _adamw_block(ins, outs):
    p_ref, w_ref, m_ref, v_ref = ins
    g = p_ref[0].astype(F32)
    for b in range(1, N_CHIP):
        g = g + p_ref[b].astype(F32)
    delta, m2, v2 = _adamw_math(g, w_ref[...], m_ref[...], v_ref[...])
    for ref, val in zip(outs, (g, delta, m2, v2)):
        ref[...] = val


class _Rider(NamedTuple):
    inputs: tuple
    in_specs: list
    out_shape: list
    out_specs: list
    body: Callable


def _adamw_rider(parts, w, m, v):
    rows, cols = w.shape

    def rider(steps, step_of):
        rr = rows // steps
        blk = pl.BlockSpec((rr, cols), lambda *ids: (step_of(*ids[:2]), 0))
        chips = pl.BlockSpec((N_CHIP, rr, cols), lambda *ids: (0, step_of(*ids[:2]), 0))
        shape = jax.ShapeDtypeStruct((rows, cols), F32)
        return _Rider((parts, w, m, v), [chips, blk, blk, blk], [shape] * 4, [blk] * 4, _adamw_block)

    return rider


def _adamw(parts, w, m, v, *, name, after=(), tr=256):
    rows, cols = w.shape

    def body(*refs):
        _adamw_block(refs[:4], refs[4 + len(after):])

    spec = pl.BlockSpec((tr, cols), lambda i: (i, 0))
    shape = jax.ShapeDtypeStruct((rows, cols), F32)
    return pl.pallas_call(
        body, name=name, grid=(rows // tr,),
        in_specs=[pl.BlockSpec((N_CHIP, tr, cols), lambda i: (0, i, 0)), spec, spec, spec]
        + [pl.BlockSpec(memory_space=pl.ANY)] * len(after),
        out_specs=[spec] * 4, out_shape=[shape] * 4,
        compiler_params=_params(("parallel",)),
    )(parts, w, m, v, *after)


def _small_update(part, w, m, v):
    rows = part.shape[0]

    def body(p_ref, w_ref, m_ref, v_ref, g_ref, d_ref, mo_ref, vo_ref, buf, send, recv):
        me, peers = _me_and_peers()
        buf[me] = p_ref[...]
        sends = []
        for k, (dev, _) in enumerate(peers):
            cp = pltpu.make_async_remote_copy(p_ref, buf.at[me], send.at[k], recv.at[k],
                                              device_id=dev, device_id_type=MESH)
            cp.start()
            sends.append(cp)
        for k, (dev, idx) in enumerate(peers):
            pltpu.make_async_remote_copy(p_ref, buf.at[idx], send.at[k], recv.at[k],
                                         device_id=dev, device_id_type=MESH).wait_recv()
        for cp in sends:
            cp.wait_send()
        g = buf[0]
        for b in range(1, N_DEV):
            g = g + buf[b]
        delta, m2, v2 = _adamw_math(g, w_ref[...], m_ref[...], v_ref[...])
        g_ref[...] = g
        d_ref[...] = delta
        mo_ref[...] = m2
        vo_ref[...] = v2

    vm = pl.BlockSpec(memory_space=pltpu.VMEM)
    shape = jax.ShapeDtypeStruct((rows, HD), F32)
    return pl.pallas_call(
        body, name="small_params_update",
        in_specs=[vm] * 4, out_specs=[vm] * 4, out_shape=[shape] * 4,
        scratch_shapes=[pltpu.VMEM((N_DEV, rows, HD), F32),
                        pltpu.SemaphoreType.DMA((N_DEV - 1,)), pltpu.SemaphoreType.DMA((N_DEV - 1,))],
    )(part, w, m, v)


def _pack_small(norm_mix, b_gate, qa, ka, qb, kb, rpb, norm_ffn):
    gains = jnp.concatenate([qa, ka, qb, kb, jnp.zeros((4, HD), F32)], axis=0)
    rpb_pad = jnp.pad(rpb.reshape(4 * (2 * WIN_R - 1), 2 * WIN_C - 1), ((0, 4), (0, HD - (2 * WIN_C - 1))))
    return jnp.concatenate([norm_mix.reshape(16, HD), b_gate.reshape(32, HD), gains, rpb_pad,
                            norm_ffn.reshape(16, HD), jnp.zeros((8, HD), F32)], axis=0)


LOSS_ROW = 136


def _unpack_small(p):
    norm_mix = p[0:16].reshape(1, D)
    b_gate = p[16:48].reshape(1, 2 * D)
    qa, ka, qb, kb = (p[48 + i:49 + i] for i in range(4))
    rpb = p[56:116, :2 * WIN_C - 1].reshape(1, 4, 2 * WIN_R - 1, 2 * WIN_C - 1)
    norm_ffn = p[120:136].reshape(1, D)
    return norm_mix, b_gate, qa, ka, qb, kb, rpb, norm_ffn


def kernel(x, norm_mix, w_in, b_gate, q_norm_a, k_norm_a, q_norm_b, k_norm_b, rpb_b, w_proj_a, w_proj_b, w_out, norm_ffn, w_up, w_down, loss_target, m_norm_mix, m_w_in, m_b_gate, m_q_norm_a, m_k_norm_a, m_q_norm_b, m_k_norm_b, m_rpb_b, m_w_proj_a, m_w_proj_b, m_w_out, m_norm_ffn, m_w_up, m_w_down, v_norm_mix, v_w_in, v_b_gate, v_q_norm_a, v_k_norm_a, v_q_norm_b, v_k_norm_b, v_rpb_b, v_w_proj_a, v_w_proj_b, v_w_out, v_norm_ffn, v_w_up, v_w_down):
    big_w = (w_in[0], w_proj_a[0], w_proj_b[0], w_out[0], w_up[0], w_down[0])
    big_m = (m_w_in[0], m_w_proj_a[0], m_w_proj_b[0], m_w_out[0], m_w_up[0], m_w_down[0])
    big_v = (v_w_in[0], v_w_proj_a[0], v_w_proj_b[0], v_w_out[0], v_w_up[0], v_w_down[0])
    names = ("w_in", "w_proj_a", "w_proj_b", "w_out", "w_up", "w_down")

    shards = [_cast_bf16(w) for w in big_w]
    g_in, = _gather_on_sequencer(shards[0:1], "gather_w_in")
    g_pa, g_pb, g_out, g_up = _gather_on_sequencer(shards[1:5], "gather_w_mix_up")
    g_down, = _gather_on_sequencer(shards[5:6], "gather_w_down")
    small_w = _pack_small(norm_mix, b_gate, q_norm_a, k_norm_a, q_norm_b, k_norm_b, rpb_b, norm_ffn)
    small_m = _pack_small(m_norm_mix, m_b_gate, m_q_norm_a, m_k_norm_a, m_q_norm_b, m_k_norm_b, m_rpb_b, m_norm_ffn)
    small_v = _pack_small(v_norm_mix, v_b_gate, v_q_norm_a, v_k_norm_a, v_q_norm_b, v_k_norm_b, v_rpb_b, v_norm_ffn)

    upd = [None] * 6
    in_flight = {}

    def weight_grads(tag, operands):
        sums = {}
        for i, (a, b) in operands.items():
            rider = adamw_rider(5) if tag == "w_in" else None
            sums[i] = _mm_tn_pair(a, b, name=f"grad_{names[i]}", rider=rider, **GRAD_TILES[i])
            if rider:
                sums[i], upd[5] = sums[i]
        new = list(sums.values())
        in_flight.update(zip(sums, _chip_exchange_on_sequencer(new, f"chip_exchange_{tag}")))
        return new

    def adamw_rider(i):
        return _adamw_rider(in_flight.pop(i), big_w[i], big_m[i], big_v[i])

    def proj_bwd_rider():
        return adamw_rider(4), functools.partial(upd.__setitem__, 4)

    loss, grad_x, small_g = _local_step(
        x[0], loss_target[0], norm_mix, b_gate, small_w[48:56], small_w[56:120], norm_ffn,
        g_in, g_pa, g_pb, g_out.reshape(D, D), g_up, g_down.reshape(D_FF, D), weight_grads, proj_bwd_rider)

    g_norm_mix, g_b, g_gains, g_rpb, g_norm_ffn = small_g
    small_part = jnp.concatenate([g_norm_mix.reshape(16, HD), g_b.reshape(32, HD),
                                  g_gains, g_rpb, g_norm_ffn.reshape(16, HD),
                                  jnp.pad(loss, ((0, 7), (0, HD - 1)))], axis=0)
    slabs = _small_update(small_part, small_w, small_m, small_v)
    total = slabs[0][LOSS_ROW, 0]
    s_g, s_d, s_m, s_v = (_unpack_small(t) for t in slabs)

    last = grad_x
    for i, r in in_flight.items():
        upd[i] = _adamw(r, big_w[i], big_m[i], big_v[i], name=f"adamw_{names[i]}", after=[last])
        last = upd[i][0]
    b_g, b_d, b_m, b_v = ([u[j][None] for u in upd] for j in range(4))

    def order(small, big):
        nm, bg, qa, ka, qb, kb, rpb, nf = small
        w_in_, pa_, pb_, out_, up_, down_ = big
        return (nm, w_in_, bg, qa, ka, qb, kb, rpb, pa_, pb_, out_, nf, up_, down_)

    return (total, grad_x[None], *order(s_g, b_g), *order(s_d, b_d), *order(s_m, b_m), *order(s_v, b_v))
```

```python
import functools
from typing import Callable, NamedTuple

import jax
import jax.numpy as jnp
import numpy as np
from jax import lax
from jax.experimental import pallas as pl
from jax.experimental.pallas import tpu as pltpu
from jax.experimental.pallas import tpu_sc as plsc

F32 = jnp.float32
BF16 = jnp.bfloat16

N_DEV = 8
S = 2048
D = 2048
HD = 128
NH = 16
NH_A = 12
QKV = NH * HD
D_IN = 3 * QKV + 2 * D
D_BR = 512
D_FF = 4 * D
GRID_W = 64
ROWS = S // GRID_W
WIN_R = 8
WIN_C = 16
EPS = 1e-6
NEG = -1e30
SCALE = HD ** -0.5
ROPE_THETA = 10000.0
DILATIONS = (1, 4, 16)
HALF_A = 64
QB = 128

LR, B1, B2, AEPS, WD, STEP = 0.001, 0.9, 0.999, 1e-08, 0.01, 10
BC1 = 1.0 - B1 ** STEP
BC2 = 1.0 - B2 ** STEP

VMEM_LIMIT = 56 * 1024 * 1024
MESH = pl.DeviceIdType.MESH

NN = (((1,), (0,)), ((), ()))
NT = (((1,), (1,)), ((), ()))
TN = (((0,), (0,)), ((), ()))


def _params(sem):
    return pltpu.CompilerParams(dimension_semantics=sem, vmem_limit_bytes=VMEM_LIMIT)


def _matmul(a, b, *, product, grid, a_spec, b_spec, epi, out_shape, out_specs, name,
            extra=(), extra_specs=(), after=(), carried=False, rider=None):
    n_extra = len(extra)
    single = not isinstance(out_shape, (list, tuple))
    out_shape = [out_shape] if single else list(out_shape)
    out_specs = [out_specs] if single else list(out_specs)
    ride = rider(grid[0] * grid[1], lambda j, i: j * grid[1] + i) if rider else None
    r_in = list(ride.inputs) if ride else []
    n_main = len(out_shape)

    def body(a_ref, b_ref, *rest):
        ins, outs = rest[:n_extra + len(after) + len(r_in)], rest[n_extra + len(after) + len(r_in):]
        epi(product(a_ref, b_ref), ins[:n_extra], outs[:n_main])
        if ride:
            ride.body(ins[n_extra + len(after):], outs[n_main:])

    res = pl.pallas_call(
        body, name=name, grid=grid,
        in_specs=[a_spec, b_spec, *extra_specs, *[pl.BlockSpec(memory_space=pl.ANY)] * len(after),
                  *(ride.in_specs if ride else [])],
        out_specs=out_specs + (ride.out_specs if ride else []),
        out_shape=out_shape + (ride.out_shape if ride else []),
        compiler_params=_params(("arbitrary", "arbitrary") if carried else ("parallel", "parallel")),
    )(a, b, *extra, *after, *r_in)
    main = res[0] if single else res[:n_main]
    return (main, res[n_main:]) if ride else main


def _dot(x, y, dims):
    return lax.dot_general(x, y, dims, preferred_element_type=F32)


def _epi_store(acc, ex, outs):
    outs[0][...] = acc.astype(outs[0].dtype)


def _epi_residual(acc, ex, outs):
    outs[0][...] = acc + ex[0][...]


def _mm_nn(a, b3, *, tm, tn, name, out_dtypes=(F32,), epi=_epi_store, extra=(), total=False):
    m, kdim = a.shape
    g, _, ng = b3.shape
    n = g * ng
    if tn <= ng:
        npg = ng // tn
        b_spec = pl.BlockSpec((None, kdim, tn), lambda j, i: (j // npg, 0, j % npg))

        def product(a_ref, b_ref):
            return _dot(a_ref[...], b_ref[...], NN)
    else:
        gb = tn // ng
        b_spec = pl.BlockSpec((gb, kdim, ng), lambda j, i: (j, 0, 0))

        def product(a_ref, b_ref):
            return jnp.concatenate([_dot(a_ref[...], b_ref[q], NN) for q in range(gb)], axis=1)

    tile = pl.BlockSpec((tm, tn), lambda j, i: (i, j))
    shapes = [jax.ShapeDtypeStruct((m, n), dt) for dt in out_dtypes]
    specs = [tile] * len(shapes)
    if total:
        shapes.append(jax.ShapeDtypeStruct((1, 1), F32))
        specs.append(pl.BlockSpec((1, 1), lambda j, i: (0, 0)))
    single = len(shapes) == 1
    return _matmul(
        a, b3, product=product, grid=(n // tn, m // tm), epi=epi, name=name, carried=total,
        a_spec=pl.BlockSpec((tm, kdim), lambda j, i: (i, 0)), b_spec=b_spec,
        extra=extra, extra_specs=[tile] * len(extra),
        out_shape=shapes[0] if single else shapes, out_specs=specs[0] if single else specs)


def _mm_nt(a, b3, *, tm, tn, name, out_dtype=F32, epi=_epi_store, extra=(), after=(), rider=None):
    m, kdim = a.shape
    g, n, kg = b3.shape

    def product(a_ref, b_ref):
        acc = _dot(a_ref[:, 0:kg], b_ref[0], NT)
        for q in range(1, g):
            acc = acc + _dot(a_ref[:, q * kg:(q + 1) * kg], b_ref[q], NT)
        return acc

    tile = pl.BlockSpec((tm, tn), lambda j, i: (i, j))
    return _matmul(
        a, b3, product=product, grid=(n // tn, m // tm), epi=epi, name=name,
        a_spec=pl.BlockSpec((tm, kdim), lambda j, i: (i, 0)),
        b_spec=pl.BlockSpec((g, tn, kg), lambda j, i: (0, j, 0)),
        extra=extra, extra_specs=[tile] * len(extra), after=after, rider=rider,
        out_shape=jax.ShapeDtypeStruct((m, n), out_dtype), out_specs=tile)


def _mm_tn(a, b, *, tm, tn, name, groups=1, out_dtype=BF16):
    t, m = a.shape
    _, n = b.shape
    ng = n // groups
    if tn <= ng:
        npg = ng // tn
        out_spec = pl.BlockSpec((None, tm, tn), lambda j, i: (j // npg, i, j % npg))
        epi = _epi_store

        def product(a_ref, b_ref):
            return _dot(a_ref[...], b_ref[...], TN)
    else:
        gb = tn // ng
        out_spec = pl.BlockSpec((gb, tm, ng), lambda j, i: (j, i, 0))

        def product(a_ref, b_ref):
            return [_dot(a_ref[...], b_ref[:, q * ng:(q + 1) * ng], TN) for q in range(gb)]

        def epi(parts, ex, outs):
            for q, part in enumerate(parts):
                outs[0][q] = part.astype(out_dtype)

    return _matmul(
        a, b, product=product, grid=(n // tn, m // tm), epi=epi, name=name,
        a_spec=pl.BlockSpec((t, tm), lambda j, i: (0, i)),
        b_spec=pl.BlockSpec((t, tn), lambda j, i: (0, j)),
        out_shape=jax.ShapeDtypeStruct((groups, m, ng), out_dtype), out_specs=out_spec)


def _rms_fwd(x, g, *, name, tr=256):
    def body(x_ref, g_ref, y_ref, r_ref):
        xv = x_ref[...]
        r = lax.rsqrt(jnp.mean(xv * xv, axis=-1, keepdims=True) + EPS)
        y_ref[...] = (xv * r * g_ref[...]).astype(BF16)
        r_ref[...] = r

    row = pl.BlockSpec((tr, D), lambda i: (i, 0))
    return pl.pallas_call(
        body, name=name, grid=(S // tr,),
        in_specs=[row, pl.BlockSpec((1, D), lambda i: (0, 0))],
        out_specs=[row, pl.BlockSpec((tr, 1), lambda i: (i, 0))],
        out_shape=[jax.ShapeDtypeStruct((S, D), BF16), jax.ShapeDtypeStruct((S, 1), F32)],
        compiler_params=_params(("parallel",)),
    )(x, g)


def _rms_bwd(dy, x, rstd, g, resid, *, name, tr=256):
    def body(dy_ref, x_ref, r_ref, g_ref, res_ref, dx_ref, dxb_ref, dg_ref):
        r = r_ref[...]
        xh = x_ref[...] * r
        dyv = dy_ref[...]
        t = dyv * g_ref[...]
        dx = r * (t - xh * jnp.mean(t * xh, axis=-1, keepdims=True)) + res_ref[...]
        dx_ref[...] = dx
        dxb_ref[...] = dx.astype(BF16)
        part = jnp.sum(dyv * xh, axis=0, keepdims=True)

        @pl.when(pl.program_id(0) == 0)
        def _():
            dg_ref[...] = part

        @pl.when(pl.program_id(0) > 0)
        def _():
            dg_ref[...] += part

    row = pl.BlockSpec((tr, D), lambda i: (i, 0))
    vec = pl.BlockSpec((1, D), lambda i: (0, 0))
    return pl.pallas_call(
        body, name=name, grid=(S // tr,),
        in_specs=[row, row, pl.BlockSpec((tr, 1), lambda i: (i, 0)), vec, row],
        out_specs=[row, row, vec],
        out_shape=[jax.ShapeDtypeStruct((S, D), F32), jax.ShapeDtypeStruct((S, D), BF16),
                   jax.ShapeDtypeStruct((1, D), F32)],
        compiler_params=_params(("arbitrary",)),
    )(dy, x, rstd, g, resid)


def _rope_tables():
    pos = np.arange(S, dtype=np.float32)
    inv = (ROPE_THETA ** (-np.arange(0, HD, 2, dtype=np.float32) / HD)).astype(np.float32)
    ang = pos[:, None] * inv[None, :]
    cos, sin = np.cos(ang), np.sin(ang)
    return (jnp.asarray(np.concatenate([cos, cos], axis=-1), F32),
            jnp.asarray(np.concatenate([-sin, sin], axis=-1), F32))


def _swap_halves(t):
    return pltpu.roll(t, HD // 2, axis=1)


TOK = 256


def _lane_block_spec(d, last=HD):
    return pl.BlockSpec((4, TOK // d, d * last), lambda i: (0, i, 0))


def _to_lane_blocks(dst, head, val, d, scr, dtype):
    w = val.shape[1]
    if d == 1:
        dst[head] = val.astype(dtype)
        return
    scr[...] = val
    for r in range(d):
        dst[head, :, r * w:(r + 1) * w] = scr[pl.ds(r, TOK // d, stride=d), :].astype(dtype)


def _from_lane_blocks(src, head, d, w, scr):
    if d == 1:
        return src[head].astype(F32)
    for r in range(d):
        scr[pl.ds(r, TOK // d, stride=d), :] = src[head, :, r * w:(r + 1) * w].astype(F32)
    return scr[...]


def _qk_prep(proj, gains, cos2, sin2):
    def body(q_ref, k_ref, v_ref, g_ref, c_ref, s_ref, *rest):
        outs, scr = rest[:-1], rest[-1]
        cos, sin = c_ref[...], s_ref[...]
        for which, (src, row_a, row_b) in enumerate(((q_ref, 0, 2), (k_ref, 1, 3), (v_ref, None, None))):
            for h in range(NH):
                y = src[:, h * HD:(h + 1) * HD]
                if row_a is not None:
                    y = y * lax.rsqrt(jnp.mean(y * y, axis=-1, keepdims=True) + EPS)
                    if h < NH_A:
                        y = y * g_ref[row_a:row_a + 1, :]
                        y = y * cos + _swap_halves(y) * sin
                    else:
                        y = y * g_ref[row_b:row_b + 1, :]
                if h < NH_A:
                    gi = h // 4
                    _to_lane_blocks(outs[3 * gi + which], h % 4, y, DILATIONS[gi], scr, BF16)
                else:
                    hb = h - NH_A
                    outs[9 + which][:, hb * HD:(hb + 1) * HD] = y.astype(BF16)

    def blk(c):
        return pl.BlockSpec((TOK, QKV), lambda i: (i, c))
    tab = pl.BlockSpec((TOK, HD), lambda i: (i, 0))
    out_specs, out_shape = [], []
    for d in DILATIONS:
        out_specs += [_lane_block_spec(d)] * 3
        out_shape += [jax.ShapeDtypeStruct((4, S // d, d * HD), BF16)] * 3
    out_specs += [pl.BlockSpec((TOK, D_BR), lambda i: (i, 0))] * 3
    out_shape += [jax.ShapeDtypeStruct((S, D_BR), BF16)] * 3
    outs = pl.pallas_call(
        body, name="qk_prep", grid=(S // TOK,),
        in_specs=[blk(0), blk(1), blk(2), pl.BlockSpec((8, HD), lambda i: (0, 0)), tab, tab],
        out_specs=out_specs, out_shape=out_shape,
        scratch_shapes=[pltpu.VMEM((TOK, HD), F32)],
        compiler_params=_params(("parallel",)),
    )(proj, proj, proj, gains, cos2, sin2)
    return [tuple(outs[3 * gi:3 * gi + 3]) for gi in range(3)], tuple(outs[9:12])


def _qk_prep_bwd(dproj, proj, gains, cos2, sin2, grads_a, grads_b):
    def body(dp_in, q_ref, k_ref, g_ref, c_ref, s_ref, *rest):
        grads, (dp_out, dg_ref, scr) = rest[:12], rest[12:]
        del dp_in
        cos, sin = c_ref[...], s_ref[...]

        def grad_of(which, h):
            if h < NH_A:
                gi = h // 4
                return _from_lane_blocks(grads[3 * gi + which], h % 4, DILATIONS[gi], HD, scr)
            hb = h - NH_A
            return grads[9 + which][:, hb * HD:(hb + 1) * HD]

        dg_rows = []
        for which, (src, base, row_a, row_b) in enumerate(((q_ref, 0, 0, 2), (k_ref, QKV, 1, 3))):
            dg_a = jnp.zeros((1, HD), F32)
            dg_b = jnp.zeros((1, HD), F32)
            for h in range(NH):
                t = src[:, h * HD:(h + 1) * HD]
                dy = grad_of(which, h)
                r = lax.rsqrt(jnp.mean(t * t, axis=-1, keepdims=True) + EPS)
                xh = t * r
                if h < NH_A:
                    dy = dy * cos - _swap_halves(dy) * sin
                    gain = g_ref[row_a:row_a + 1, :]
                    dg_a = dg_a + jnp.sum(dy * xh, axis=0, keepdims=True)
                else:
                    gain = g_ref[row_b:row_b + 1, :]
                    dg_b = dg_b + jnp.sum(dy * xh, axis=0, keepdims=True)
                u = dy * gain
                dx = r * (u - xh * jnp.mean(u * xh, axis=-1, keepdims=True))
                dp_out[:, base + h * HD:base + (h + 1) * HD] = dx.astype(BF16)
            dg_rows += [(row_a, dg_a), (row_b, dg_b)]
        for h in range(NH):
            dp_out[:, 2 * QKV + h * HD:2 * QKV + (h + 1) * HD] = grad_of(2, h).astype(BF16)

        @pl.when(pl.program_id(0) == 0)
        def _():
            dg_ref[...] = jnp.zeros((8, HD), F32)

        for row, val in dg_rows:
            dg_ref[row:row + 1, :] += val

    def blk(c):
        return pl.BlockSpec((TOK, QKV), lambda i: (i, c))
    tab = pl.BlockSpec((TOK, HD), lambda i: (i, 0))
    gain_spec = pl.BlockSpec((8, HD), lambda i: (0, 0))
    grad_specs = [s for d in DILATIONS for s in [_lane_block_spec(d)] * 3]
    grad_specs += [pl.BlockSpec((TOK, D_BR), lambda i: (i, 0))] * 3
    return pl.pallas_call(
        body, name="qk_prep_bwd", grid=(S // TOK,),
        in_specs=[pl.BlockSpec(memory_space=pl.ANY), blk(0), blk(1), gain_spec, tab, tab] + grad_specs,
        out_specs=[pl.BlockSpec((TOK, 3 * QKV), lambda i: (i, 0)), gain_spec],
        out_shape=[jax.ShapeDtypeStruct((S, D_IN), BF16), jax.ShapeDtypeStruct((8, HD), F32)],
        input_output_aliases={0: 0},
        scratch_shapes=[pltpu.VMEM((TOK, HD), F32)],
        compiler_params=_params(("arbitrary",)),
    )(dproj, proj, proj, gains, cos2, sin2, *[g for grp in grads_a for g in grp], *grads_b)


def _mix_fwd(oa, ob, w_pa, w_pb, proj, b_gate, *, tr=256):
    def body(oa_ref, ob_ref, pa_ref, pb_ref, la_ref, lb_ref, ba_ref, bb_ref, mix_ref, ya_ref, yb_ref):
        ya = jnp.concatenate([_dot(oa_ref[...], pa_ref[q], NN) for q in range(N_DEV)], axis=1)
        yb = jnp.concatenate([_dot(ob_ref[...], pb_ref[q], NN) for q in range(N_DEV)], axis=1)
        ga = jax.nn.sigmoid(la_ref[...] + ba_ref[...])
        gb = jax.nn.sigmoid(lb_ref[...] + bb_ref[...])
        mix_ref[...] = (ga * ya + gb * yb).astype(BF16)
        ya_ref[...] = ya.astype(BF16)
        yb_ref[...] = yb.astype(BF16)

    row = pl.BlockSpec((tr, D), lambda i: (i, 0))
    branch = pl.BlockSpec((tr, D_BR), lambda i: (i, 0))
    whole = pl.BlockSpec((N_DEV, D_BR, D // N_DEV), lambda i: (0, 0, 0))
    return pl.pallas_call(
        body, name="mix_fwd", grid=(S // tr,),
        in_specs=[branch, branch, whole, whole,
                  pl.BlockSpec((tr, D), lambda i: (i, 3)), pl.BlockSpec((tr, D), lambda i: (i, 4)),
                  pl.BlockSpec((1, D), lambda i: (0, 0)), pl.BlockSpec((1, D), lambda i: (0, 1))],
        out_specs=[row, row, row], out_shape=[jax.ShapeDtypeStruct((S, D), BF16)] * 3,
        compiler_params=_params(("parallel",)),
    )(oa, ob, w_pa, w_pb, proj, proj, b_gate, b_gate)


def _mix_bwd(dh1b, w_out, proj, b_gate, ya, yb, *, tr=256):
    def body(dh_ref, w_ref, la_ref, lb_ref, b_ref, ya_ref, yb_ref, dya_ref, dyb_ref, dp_ref, db_ref):
        dm = _dot(dh_ref[...], w_ref[...], NT)
        parts = []
        for l_ref, y_ref, dy_ref, lo in ((la_ref, ya_ref, dya_ref, 0), (lb_ref, yb_ref, dyb_ref, D)):
            g = jax.nn.sigmoid(l_ref[...] + b_ref[:, lo:lo + D])
            dy_ref[...] = (dm * g).astype(BF16)
            dl = dm * y_ref[...].astype(F32) * g * (1.0 - g)
            dp_ref[:, lo:lo + D] = dl.astype(BF16)
            parts.append(jnp.sum(dl, axis=0, keepdims=True))
        part = jnp.concatenate(parts, axis=1)

        @pl.when(pl.program_id(0) == 0)
        def _():
            db_ref[...] = part

        @pl.when(pl.program_id(0) > 0)
        def _():
            db_ref[...] += part

    row = pl.BlockSpec((tr, D), lambda i: (i, 0))
    vec = pl.BlockSpec((1, 2 * D), lambda i: (0, 0))
    gate_cols = pl.BlockSpec((pl.Element(tr), pl.Element(2 * D)), lambda i: (i * tr, 3 * QKV))
    return pl.pallas_call(
        body, name="mix_bwd", grid=(S // tr,),
        in_specs=[row, pl.BlockSpec((D, D), lambda i: (0, 0)),
                  pl.BlockSpec((tr, D), lambda i: (i, 3)), pl.BlockSpec((tr, D), lambda i: (i, 4)), vec, row, row],
        out_specs=[row, row, gate_cols, vec],
        out_shape=[jax.ShapeDtypeStruct((S, D), BF16), jax.ShapeDtypeStruct((S, D), BF16),
                   jax.ShapeDtypeStruct((S, D_IN), BF16), jax.ShapeDtypeStruct((1, 2 * D), F32)],
        compiler_params=_params(("arbitrary",)),
    )(dh1b, w_out, proj, proj, b_gate, ya, yb)


def _band_blocks(m_len):
    wk = min(m_len, QB + 2 * QB)
    return [(qb * QB, min(max(qb * QB - QB, 0), m_len - wk), wk) for qb in range(m_len // QB)]


def _band_scores(q, kw, q0, k0, wk):
    s = _dot(q, kw, NT) * SCALE
    qpos = q0 + lax.broadcasted_iota(jnp.int32, (QB, 1), 0)
    kpos = k0 + lax.broadcasted_iota(jnp.int32, (1, wk), 1)
    return jnp.where(jnp.abs(kpos - qpos) <= HALF_A, s, NEG)


def _attn_a_fwd(q, k, v, gi):
    d = DILATIONS[gi]
    m_len = S // d

    def body(q_ref, k_ref, v_ref, o_ref, lse_ref):
        for r in range(d):
            lanes = slice(r * HD, (r + 1) * HD)
            for q0, k0, wk in _band_blocks(m_len):
                s = _band_scores(q_ref[q0:q0 + QB, lanes], k_ref[k0:k0 + wk, lanes], q0, k0, wk)
                m = jnp.max(s, axis=-1, keepdims=True)
                p = jnp.exp(s - m)
                l = jnp.sum(p, axis=-1, keepdims=True)
                o_ref[q0:q0 + QB, lanes] = _dot(p.astype(BF16), v_ref[k0:k0 + wk, lanes], NN) / l
                lse_ref[q0:q0 + QB, r:r + 1] = m + jnp.log(l)

    head = pl.BlockSpec((None, m_len, d * HD), lambda h: (h, 0, 0))
    stat = pl.BlockSpec((None, m_len, d), lambda h: (h, 0, 0))
    return pl.pallas_call(
        body, name=f"attn_a_fwd_{gi}", grid=(4,),
        in_specs=[head, head, head], out_specs=[head, stat],
        out_shape=[jax.ShapeDtypeStruct((4, m_len, d * HD), F32), jax.ShapeDtypeStruct((4, m_len, d), F32)],
        compiler_params=_params(("parallel",)),
    )(q, k, v)


def _combine_a(os, lses):
    def body(o0, o1, o2, l0, l1, l2, oa_ref, lse_ref, scr, scr1):
        for h in range(4):
            o = [_from_lane_blocks(ref, h, d, HD, scr) for ref, d in zip((o0, o1, o2), DILATIONS)]
            a, b, c = (_from_lane_blocks(ref, h, d, 1, scr1) for ref, d in zip((l0, l1, l2), DILATIONS))
            m = jnp.maximum(jnp.maximum(a, b), c)
            wa, wb, wc = jnp.exp(a - m), jnp.exp(b - m), jnp.exp(c - m)
            tot = wa + wb + wc
            oa_ref[:, h * HD:(h + 1) * HD] = ((wa * o[0] + wb * o[1] + wc * o[2]) / tot).astype(BF16)
            lse_ref[h] = m + jnp.log(tot)

    return pl.pallas_call(
        body, name="combine_a", grid=(S // TOK,),
        in_specs=[_lane_block_spec(d) for d in DILATIONS] + [_lane_block_spec(d, 1) for d in DILATIONS],
        out_specs=[pl.BlockSpec((TOK, D_BR), lambda i: (i, 0)), pl.BlockSpec((4, TOK, 1), lambda i: (0, i, 0))],
        out_shape=[jax.ShapeDtypeStruct((S, D_BR), BF16), jax.ShapeDtypeStruct((4, S, 1), F32)],
        scratch_shapes=[pltpu.VMEM((TOK, HD), F32), pltpu.VMEM((TOK, 1), F32)],
        compiler_params=_params(("parallel",)),
    )(*os, *lses)


def _proj_a_bwd(dya, w_pa, oa, lse):
    kg = D // N_DEV

    def body(dy_ref, w_ref, o_ref, l_ref, *rest):
        outs, (scr, scr1) = rest[:9], rest[9:]
        doa = _dot(dy_ref[:, 0:kg], w_ref[0], NT)
        for q in range(1, N_DEV):
            doa = doa + _dot(dy_ref[:, q * kg:(q + 1) * kg], w_ref[q], NT)
        for h in range(4):
            do = doa[:, h * HD:(h + 1) * HD]
            dsum = jnp.sum(do * o_ref[:, h * HD:(h + 1) * HD].astype(F32), axis=-1, keepdims=True)
            for gi, d in enumerate(DILATIONS):
                _to_lane_blocks(outs[3 * gi], h, do, d, scr, BF16)
                _to_lane_blocks(outs[3 * gi + 1], h, l_ref[h], d, scr1, F32)
                _to_lane_blocks(outs[3 * gi + 2], h, dsum, d, scr1, F32)

    row = pl.BlockSpec((TOK, D_BR), lambda i: (i, 0))
    out_specs, out_shape = [], []
    for d in DILATIONS:
        out_specs += [_lane_block_spec(d), _lane_block_spec(d, 1), _lane_block_spec(d, 1)]
        out_shape += [jax.ShapeDtypeStruct((4, S // d, d * HD), BF16)] + [jax.ShapeDtypeStruct((4, S // d, d), F32)] * 2
    outs = pl.pallas_call(
        body, name="proj_a_bwd", grid=(S // TOK,),
        in_specs=[pl.BlockSpec((TOK, D), lambda i: (i, 0)),
                  pl.BlockSpec((N_DEV, D_BR, kg), lambda i: (0, 0, 0)),
                  row, pl.BlockSpec((4, TOK, 1), lambda i: (0, i, 0))],
        out_specs=out_specs, out_shape=out_shape,
        scratch_shapes=[pltpu.VMEM((TOK, HD), F32), pltpu.VMEM((TOK, 1), F32)],
        compiler_params=_params(("parallel",)),
    )(dya, w_pa, oa, lse)
    return [tuple(outs[3 * gi:3 * gi + 3]) for gi in range(3)]


def _attn_a_bwd(q, k, v, do, lse, dsum, gi):
    d = DILATIONS[gi]
    m_len = S // d

    def body(q_ref, k_ref, v_ref, do_ref, lse_ref, dsum_ref, dq_ref, dk_ref, dv_ref):
        dk_ref[...] = jnp.zeros((m_len, d * HD), F32)
        dv_ref[...] = jnp.zeros((m_len, d * HD), F32)
        for r in range(d):
            lanes = slice(r * HD, (r + 1) * HD)
            for q0, k0, wk in _band_blocks(m_len):
                rows, keys = slice(q0, q0 + QB), slice(k0, k0 + wk)
                qv, kw, vw, dov = q_ref[rows, lanes], k_ref[keys, lanes], v_ref[keys, lanes], do_ref[rows, lanes]
                p = jnp.exp(_band_scores(qv, kw, q0, k0, wk) - lse_ref[rows, r:r + 1])
                ds = (p * (_dot(dov, vw, NT) - dsum_ref[rows, r:r + 1]) * SCALE).astype(BF16)
                dq_ref[rows, lanes] = _dot(ds, kw, NN)
                dk_ref[keys, lanes] += _dot(ds, qv, TN)
                dv_ref[keys, lanes] += _dot(p.astype(BF16), dov, TN)

    head = pl.BlockSpec((None, m_len, d * HD), lambda h: (h, 0, 0))
    stat = pl.BlockSpec((None, m_len, d), lambda h: (h, 0, 0))
    shape = jax.ShapeDtypeStruct((4, m_len, d * HD), F32)
    return pl.pallas_call(
        body, name=f"attn_a_bwd_{gi}", grid=(4,),
        in_specs=[head, head, head, head, stat, stat], out_specs=[head, head, head],
        out_shape=[shape, shape, shape],
        compiler_params=_params(("parallel",)),
    )(q, k, v, do, lse, dsum)


KEYS_B = WIN_R * GRID_W
N_OFF = WIN_R


def _bias_constants():
    q = np.arange(GRID_W)[:, None]
    kc = np.arange(GRID_W)[None, :]
    dc = np.clip(kc - q, -(WIN_C - 1), WIN_C - 1) + (WIN_C - 1)
    expand = np.zeros((HD, GRID_W * GRID_W), np.float32)
    expand[dc.reshape(-1), np.arange(GRID_W * GRID_W)] = 1.0
    cs = np.clip(q - WIN_C // 2, 0, GRID_W - WIN_C)
    keep = ((kc >= cs) & (kc < cs + WIN_C)).reshape(1, -1).astype(np.float32)
    sel = np.zeros((64, 4 * N_OFF * WIN_R), np.float32)
    for h in range(4):
        for off in range(N_OFF):
            for j in range(WIN_R):
                sel[h * (2 * WIN_R - 1) + off + j, (h * N_OFF + off) * WIN_R + j] = 1.0
    return jnp.asarray(expand), jnp.asarray(keep), jnp.asarray(sel)


def _bias_expand(rpb_pad, expand, keep, sel):
    def body(r_ref, e_ref, k_ref, s_ref, o_ref):
        t = lax.dot_general(r_ref[...], e_ref[...], NN, precision=lax.Precision.HIGHEST,
                            preferred_element_type=F32)
        rows = lax.dot_general(s_ref[...], t, TN, precision=lax.Precision.HIGHEST,
                               preferred_element_type=F32)
        o_ref[...] = jnp.where(k_ref[...] > 0.5, rows, NEG)

    return pl.pallas_call(
        body, name="bias_expand",
        out_shape=jax.ShapeDtypeStruct((4 * N_OFF * WIN_R, GRID_W * GRID_W), F32),
        compiler_params=pltpu.CompilerParams(vmem_limit_bytes=VMEM_LIMIT),
    )(rpb_pad, expand, keep, sel)


def _bias_reduce(dbias_rows, expand, sel):
    def body(x_ref, e_ref, s_ref, o_ref):
        z = lax.dot_general(x_ref[...], e_ref[...], NT, precision=lax.Precision.HIGHEST,
                            preferred_element_type=F32)
        o_ref[...] = lax.dot_general(s_ref[...], z, NN, precision=lax.Precision.HIGHEST,
                                     preferred_element_type=F32)

    return pl.pallas_call(
        body, name="bias_reduce", out_shape=jax.ShapeDtypeStruct((64, HD), F32),
        compiler_params=pltpu.CompilerParams(vmem_limit_bytes=VMEM_LIMIT),
    )(dbias_rows, expand, sel)


def _rows_to_tab(rows):
    t = rows.reshape(4, N_OFF, WIN_R, GRID_W, GRID_W)
    return t.transpose(0, 1, 3, 2, 4).reshape(4, N_OFF, GRID_W, KEYS_B)


def _tab_to_rows(tab):
    t = tab.reshape(4, N_OFF, GRID_W, WIN_R, GRID_W)
    return t.transpose(0, 1, 3, 2, 4).reshape(4 * N_OFF * WIN_R, GRID_W * GRID_W)


def _row_window(r):
    r0 = jnp.clip(r - WIN_R // 2, 0, ROWS - WIN_R)
    off = r0 + (WIN_R - 1) - r
    return pl.multiple_of(r * GRID_W, GRID_W), pl.multiple_of(r0 * GRID_W, GRID_W), off


def _attn_b_fwd(qn, kn, vb, bias_tab):
    def body(q_ref, k_ref, v_ref, b_ref, o_ref, lse_ref):
        def row(r, carry):
            qs, ks, off = _row_window(r)
            q = q_ref[pl.ds(qs, GRID_W), :]
            s = lax.dot_general(q, k_ref[pl.ds(ks, KEYS_B), :], NT, preferred_element_type=F32) * SCALE
            s = s + b_ref[off]
            m = jnp.max(s, axis=-1, keepdims=True)
            p = jnp.exp(s - m)
            l = jnp.sum(p, axis=-1, keepdims=True)
            o = lax.dot_general(p.astype(BF16), v_ref[pl.ds(ks, KEYS_B), :], NN, preferred_element_type=F32)
            o_ref[pl.ds(qs, GRID_W), :] = (o / l).astype(BF16)
            lse_ref[pl.ds(qs, GRID_W), :] = m + jnp.log(l)
            return carry

        lax.fori_loop(0, ROWS, row, 0, unroll=2)

    full = pl.BlockSpec((S, HD), lambda h: (0, h))
    return pl.pallas_call(
        body, name="attn_b_fwd", grid=(4,),
        in_specs=[full, full, full, pl.BlockSpec((None, N_OFF, GRID_W, KEYS_B), lambda h: (h, 0, 0, 0))],
        out_specs=[pl.BlockSpec((S, HD), lambda h: (0, h)), pl.BlockSpec((None, S, 1), lambda h: (h, 0, 0))],
        out_shape=[jax.ShapeDtypeStruct((S, D_BR), BF16), jax.ShapeDtypeStruct((4, S, 1), F32)],
        compiler_params=_params(("parallel",)),
    )(qn, kn, vb, bias_tab)


def _attn_b_bwd(qn, kn, vb, bias_tab, ob, dob, lse):
    def body(q_ref, k_ref, v_ref, b_ref, o_ref, do_ref, lse_ref, dq_ref, dk_ref, dv_ref, db_ref):
        dk_ref[...] = jnp.zeros((S, HD), F32)
        dv_ref[...] = jnp.zeros((S, HD), F32)
        db_ref[...] = jnp.zeros((N_OFF, GRID_W, KEYS_B), F32)

        def row(r, carry):
            qs, ks, off = _row_window(r)
            rows = pl.ds(qs, GRID_W)
            keys = pl.ds(ks, KEYS_B)
            q = q_ref[rows, :]
            kw = k_ref[keys, :]
            s = lax.dot_general(q, kw, NT, preferred_element_type=F32) * SCALE + b_ref[off]
            p = jnp.exp(s - lse_ref[rows, :])
            do = do_ref[rows, :]
            dobf = do.astype(BF16)
            dsum = jnp.sum(do * o_ref[rows, :].astype(F32), axis=-1, keepdims=True)
            dp = lax.dot_general(dobf, v_ref[keys, :], NT, preferred_element_type=F32)
            ds = p * (dp - dsum)
            db_ref[off] += ds
            dsb = (ds * SCALE).astype(BF16)
            dq_ref[rows, :] = lax.dot_general(dsb, kw, NN, preferred_element_type=F32)
            dk_ref[keys, :] += lax.dot_general(dsb, q, TN, preferred_element_type=F32)
            dv_ref[keys, :] += lax.dot_general(p.astype(BF16), dobf, TN, preferred_element_type=F32)
            return carry

        lax.fori_loop(0, ROWS, row, 0, unroll=2)

    full = pl.BlockSpec((S, HD), lambda h: (0, h))
    slot = pl.BlockSpec((S, HD), lambda h: (0, h))
    tab = pl.BlockSpec((None, N_OFF, GRID_W, KEYS_B), lambda h: (h, 0, 0, 0))
    shape = jax.ShapeDtypeStruct((S, D_BR), F32)
    return pl.pallas_call(
        body, name="attn_b_bwd", grid=(4,),
        in_specs=[full, full, full, tab, slot, slot, pl.BlockSpec((None, S, 1), lambda h: (h, 0, 0))],
        out_specs=[slot, slot, slot, tab],
        out_shape=[shape, shape, shape, jax.ShapeDtypeStruct((4, N_OFF, GRID_W, KEYS_B), F32)],
        compiler_params=_params(("parallel",)),
    )(qn, kn, vb, bias_tab, ob, dob, lse)


def _epi_relu_sq(acc, ex, outs):
    u = jnp.maximum(acc, 0.0)
    outs[0][...] = u.astype(BF16)
    outs[1][...] = (u * u).astype(BF16)


def _epi_relu_sq_bwd(acc, ex, outs):
    outs[0][...] = (acc * (2.0 * ex[0][...].astype(F32))).astype(BF16)


def _epi_loss_head(acc, ex, outs):
    e = acc + ex[0][...] - ex[1][...]
    dy = e * (1.0 / D)
    outs[0][...] = dy
    outs[1][...] = dy.astype(BF16)
    part = (0.5 / D) * jnp.sum(jnp.sum(e * e, axis=-1, keepdims=True), axis=0, keepdims=True)
    first = (pl.program_id(0) == 0) & (pl.program_id(1) == 0)

    @pl.when(first)
    def _():
        outs[2][...] = part

    @pl.when(jnp.logical_not(first))
    def _():
        outs[2][...] += part


def _local_step(x, target, norm_mix, b_gate, gains, rpb_pad, norm_ffn,
                w_in, w_pa, w_pb, w_out, w_up, w_down, weight_grads, riders=lambda name: None):
    def ridden(name, *args, **kwargs):
        ride = riders(name)
        if ride is None:
            return _mm_nt(*args, name=name, **kwargs)
        out, rode = _mm_nt(*args, name=name, rider=ride[0], **kwargs)
        ride[1](rode)
        return out

    cos2, sin2 = _rope_tables()
    expand, keep, sel = _bias_constants()
    w_out3, w_down3 = w_out[None], w_down[None]

    xn, rstd1 = _rms_fwd(x, norm_mix, name="rms_mix")
    proj = _mm_nn(xn, w_in, tm=1024, tn=1280, name="proj")
    qkv_a, qkv_b = _qk_prep(proj, gains, cos2, sin2)
    fwd_a = [_attn_a_fwd(*qkv_a[gi], gi) for gi in range(3)]
    oa, lse_a = _combine_a([o for o, _ in fwd_a], [l for _, l in fwd_a])
    bias_tab = _rows_to_tab(_bias_expand(rpb_pad, expand, keep, sel))
    ob, lse_b = _attn_b_fwd(*qkv_b, bias_tab)
    mixed, ya, yb = _mix_fwd(oa, ob, w_pa, w_pb, proj, b_gate)
    h1 = _mm_nn(mixed, w_out3, tm=1024, tn=1024, name="out_proj", epi=_epi_residual, extra=(x,))
    hn, rstd2 = _rms_fwd(h1, norm_ffn, name="rms_ffn")
    u, usq = _mm_nn(hn, w_up, tm=1024, tn=1024, name="ffn_up", epi=_epi_relu_sq,
                    out_dtypes=(BF16, BF16))
    dy, dyb, loss = _mm_nn(usq, w_down3, tm=512, tn=512, name="ffn_down", epi=_epi_loss_head,
                           extra=(h1, target), out_dtypes=(F32, BF16), total=True)

    sent = weight_grads("w_down", {5: (usq, dyb)})
    du = _mm_nt(dyb, w_down3, tm=1024, tn=1024, name="ffn_down_bwd", out_dtype=BF16,
                epi=_epi_relu_sq_bwd, extra=(u,), after=sent)
    sent = weight_grads("w_up", {4: (hn, du)})
    dhn = ridden("ffn_up_bwd", du, w_up, tm=512, tn=512, after=sent)
    dh1, dh1b, g_norm_ffn = _rms_bwd(dhn, h1, rstd2, norm_ffn, dy, name="rms_ffn_bwd")

    dya, dyb2, dproj, g_b = _mix_bwd(dh1b, w_out, proj, b_gate, ya, yb)
    sent = weight_grads("w_mix", {3: (mixed, dh1b), 1: (oa, dya), 2: (ob, dyb2)})
    dob = _mm_nt(dyb2, w_pb, tm=1024, tn=D_BR, name="proj_b_bwd", after=sent)
    prep = _proj_a_bwd(dya, w_pa, oa, lse_a)
    grads_a = [_attn_a_bwd(*qkv_a[gi], *prep[gi], gi) for gi in range(3)]
    dqb, dkb, dvb, dbias = _attn_b_bwd(*qkv_b, bias_tab, ob, dob, lse_b)
    g_rpb = _bias_reduce(_tab_to_rows(dbias), expand, sel)
    dproj, g_gains = _qk_prep_bwd(dproj, proj, gains, cos2, sin2, grads_a, (dqb, dkb, dvb))
    sent = weight_grads("w_in", {0: (xn, dproj)})
    dxn = ridden("proj_bwd", dproj, w_in, tm=256, tn=512, after=sent)
    grad_x, _, g_norm_mix = _rms_bwd(dxn, x, rstd1, norm_mix, dh1, name="rms_mix_bwd")

    small = (g_norm_mix, g_b, g_gains, g_rpb, g_norm_ffn)
    return loss, grad_x, small


def _cast_bf16(w, *, tr=256):
    rows, cols = w.shape
    tr = min(tr, rows)

    def body(w_ref, o_ref):
        o_ref[...] = w_ref[...].astype(BF16)

    spec = pl.BlockSpec((tr, cols), lambda i: (i, 0))
    return pl.pallas_call(
        body, name=f"cast_{rows}x{cols}", grid=(rows // tr,), in_specs=[spec], out_specs=spec,
        out_shape=jax.ShapeDtypeStruct((rows, cols), BF16), compiler_params=_params(("parallel",)),
    )(w)


def _me_and_peers():
    x, y, c = lax.axis_index("x"), lax.axis_index("y"), lax.axis_index("c")
    me = 4 * x + 2 * y + c
    peers = []
    for k in range(1, N_DEV):
        px = 1 - x if k & 4 else x
        py = 1 - y if k & 2 else y
        pc = 1 - c if k & 1 else c
        peers.append(((px, py, pc), 4 * px + 2 * py + pc))
    return me, peers


def _gather_on_sequencer(shards, name):
    n = len(shards)
    hbm = pltpu.MemorySpace.HBM
    ins = [jax.new_ref(s, memory_space=hbm) for s in shards]
    outs = [jax.empty_ref(jax.ShapeDtypeStruct((N_DEV,) + s.shape, s.dtype), memory_space=hbm) for s in shards]

    @pl.kernel(mesh=plsc.ScalarSubcoreMesh(axis_name="seq", num_cores=1), name=name,
               scratch_types=(pltpu.SemaphoreType.DMA((n, N_DEV - 1)), pltpu.SemaphoreType.DMA((n, N_DEV - 1)),
                              pltpu.SemaphoreType.DMA((n,))),
               compiler_params=pltpu.CompilerParams(collective_id=0))
    def launch(send, recv, lsem):
        x, y, c = lax.axis_index("x"), lax.axis_index("y"), lax.axis_index("c")
        me, sibling = (x, y, c), (x, y, 1 - c)
        chips = [(1 - x, y), (x, 1 - y), (1 - x, 1 - y)]
        barrier = pltpu.get_barrier_semaphore()
        for peer in [sibling] + [(*chip, c) for chip in chips]:
            pl.semaphore_signal(barrier, inc=1, device_id=peer, device_id_type=MESH)
        pl.semaphore_wait(barrier, 4)

        def copy(w, k, block, to, src=None):
            px, py, pc = block
            dst = outs[w].at[4 * px + 2 * py + pc]
            return pltpu.make_async_remote_copy(dst if src is None else src, dst, send.at[w, k], recv.at[w, k],
                                                device_id=to, device_id_type=MESH)

        local = [pltpu.make_async_copy(ins[w], outs[w].at[4 * x + 2 * y + c], lsem.at[w]) for w in range(n)]
        for cp in local:
            cp.start()
        first = []
        for w in range(n):
            first += [copy(w, 1 + j, me, (*chip, c), src=ins[w]) for j, chip in enumerate(chips)]
            first.append(copy(w, 0, me, sibling, src=ins[w]))
        for cp in first:
            cp.start()
        passed = []
        for w in range(n):
            for j, chip in enumerate(chips):
                copy(w, 1 + j, (*chip, c), me).wait_recv()
                cp = copy(w, 4 + j, (*chip, c), sibling)
                cp.start()
                passed.append(cp)
        for w in range(n):
            copy(w, 0, sibling, me).wait_recv()
            for j, chip in enumerate(chips):
                copy(w, 4 + j, (*chip, 1 - c), me).wait_recv()
        for cp in first + passed:
            cp.wait_send()
        for cp in local:
            cp.wait()

    launch()
    return [o[...] for o in outs]


N_CHIP = 4
CHIPS = ((0, 0), (0, 1), (1, 0), (1, 1))


def _sequencer(name, n_sems, collective_id):
    return functools.partial(
        pl.kernel, mesh=plsc.ScalarSubcoreMesh(axis_name="seq", num_cores=1), name=name,
        scratch_types=tuple(pltpu.SemaphoreType.DMA(s) for s in n_sems),
        compiler_params=pltpu.CompilerParams(collective_id=collective_id))


def _handshake(peers):
    barrier = pltpu.get_barrier_semaphore()
    for peer in peers:
        pl.semaphore_signal(barrier, inc=1, device_id=peer, device_id_type=MESH)
    pl.semaphore_wait(barrier, len(peers))


def _chip_exchange_on_sequencer(parts, name):
    n = len(parts)
    hbm = pltpu.MemorySpace.HBM
    ins = [jax.new_ref(p, memory_space=hbm) for p in parts]
    outs = [jax.empty_ref(jax.ShapeDtypeStruct(p.shape, p.dtype), memory_space=hbm) for p in parts]

    @_sequencer(name, ((n, 3), (n, 3), (n,)), 2)
    def launch(send, recv, lsem):
        x, y, c = lax.axis_index("x"), lax.axis_index("y"), lax.axis_index("c")
        mine = 2 * x + y
        chips = [(1 - x, y), (x, 1 - y), (1 - x, 1 - y)]
        _handshake([(*chip, c) for chip in chips])
        local = [pltpu.make_async_copy(ins[w].at[mine], outs[w].at[mine], lsem.at[w]) for w in range(n)]
        for cp in local:
            cp.start()
        sends = []
        for w in range(n):
            for j, (px, py) in enumerate(chips):
                cp = pltpu.make_async_remote_copy(ins[w].at[2 * px + py], outs[w].at[mine],
                                                  send.at[w, j], recv.at[w, j],
                                                  device_id=(px, py, c), device_id_type=MESH)
                cp.start()
                sends.append(cp)
        for w in range(n):
            for j, (px, py) in enumerate(chips):
                pltpu.make_async_remote_copy(ins[w].at[mine], outs[w].at[2 * px + py],
                                             send.at[w, j], recv.at[w, j],
                                             device_id=(px, py, c), device_id_type=MESH).wait_recv()
        for cp in sends:
            cp.wait_send()
        for cp in local:
            cp.wait()

    launch()
    return [o[...] for o in outs]


GRAD_TILES = (dict(blocks_on="cols", tm=512, tn=1280), dict(blocks_on="cols", tm=512, tn=256),
              dict(blocks_on="cols", tm=512, tn=256), dict(blocks_on="rows", tm=256, tn=2048),
              dict(blocks_on="cols", tm=1024, tn=1024), dict(blocks_on="rows", tm=1024, tn=1024))


def _mm_tn_pair(a, b, *, blocks_on, tm, tn, name):
    t_len, m = a.shape
    n = b.shape[1]
    if blocks_on == "rows":
        rows, cols, inner = m // N_DEV, n, n // tn
        assert tm == rows
        a_spec = pl.BlockSpec((t_len, tm), lambda p, t, blk: (0, blk[p]))
        b_spec = pl.BlockSpec((t_len, tn), lambda p, t, blk: (0, t))
        out_spec = pl.BlockSpec((None, tm, tn), lambda p, t, blk: (
            jnp.maximum(p - N_CHIP, 0), 0, jnp.where(p < N_CHIP, 0, t)))
    else:
        rows, cols, inner = m, n // N_DEV, m // tm
        assert tn == cols
        a_spec = pl.BlockSpec((t_len, tm), lambda p, t, blk: (0, t))
        b_spec = pl.BlockSpec((t_len, tn), lambda p, t, blk: (0, blk[p]))
        out_spec = pl.BlockSpec((None, tm, tn), lambda p, t, blk: (
            jnp.maximum(p - N_CHIP, 0), jnp.where(p < N_CHIP, 0, t), 0))

    def body(blk_ref, a_ref, b_ref, o_ref, land, stage, send_sem, recv_sem):
        del blk_ref
        p, t = pl.program_id(0), pl.program_id(1)
        step = p * inner + t
        x, y, c = lax.axis_index("x"), lax.axis_index("y"), lax.axis_index("c")
        tile = _dot(a_ref[...], b_ref[...], TN)

        def to_sibling(slot, chip, piece):
            return pltpu.make_async_remote_copy(stage.at[slot], land.at[chip, piece], send_sem.at[slot],
                                                recv_sem.at[chip, piece],
                                                device_id=(x, y, 1 - c), device_id_type=MESH)

        @pl.when(p < N_CHIP)
        def _():
            slot = step % 2

            @pl.when(step >= 2)
            def _():
                to_sibling(slot, 0, 0).wait_send()

            stage[slot] = tile.astype(BF16)
            to_sibling(slot, p, t).start()

        @pl.when(step == N_CHIP * inner)
        def _():
            for slot in range(min(2, N_CHIP * inner)):
                to_sibling(slot, 0, 0).wait_send()

        @pl.when(p >= N_CHIP)
        def _():
            chip = p - N_CHIP
            to_sibling(0, chip, t).wait_recv()
            o_ref[...] = (tile + land[chip, t].astype(F32)).astype(BF16)

    c = lax.axis_index("c")
    order = jnp.stack([2 * ch + 1 - c for ch in range(N_CHIP)] + [2 * ch + c for ch in range(N_CHIP)])
    return pl.pallas_call(
        body, name=name,
        grid_spec=pltpu.PrefetchScalarGridSpec(
            num_scalar_prefetch=1, grid=(N_DEV, inner), in_specs=[a_spec, b_spec], out_specs=out_spec,
            scratch_shapes=[pltpu.VMEM((N_CHIP, inner, tm, tn), BF16), pltpu.VMEM((2, tm, tn), BF16),
                            pltpu.SemaphoreType.DMA((2,)), pltpu.SemaphoreType.DMA((N_CHIP, inner))]),
        out_shape=jax.ShapeDtypeStruct((N_CHIP, rows, cols), BF16),
        compiler_params=_params(("arbitrary", "arbitrary")),
    )(order.astype(jnp.int32), a, b)


def _adamw_math(g, w, m, v):
    m2 = B1 * m + (1.0 - B1) * g
    v2 = B2 * v + (1.0 - B2) * (g * g)
    delta = -LR * ((m2 / BC1) / (jnp.sqrt(v2 / BC2) + AEPS) + WD * w)
    return delta, m2, v2


def _adamw_block(ins, outs):
    p_ref, w_ref, m_ref, v_ref = ins
    g = p_ref[0].astype(F32)
    for b in range(1, N_CHIP):
        g = g + p_ref[b].astype(F32)
    delta, m2, v2 = _adamw_math(g, w_ref[...], m_ref[...], v_ref[...])
    for ref, val in zip(outs, (g, delta, m2, v2)):
        ref[...] = val


class _Rider(NamedTuple):
    inputs: tuple
    in_specs: list
    out_shape: list
    out_specs: list
    body: Callable


def _adamw_rider(parts, w, m, v):
    rows, cols = w.shape

    def rider(steps, step_of):
        rr = rows // steps
        blk = pl.BlockSpec((rr, cols), lambda *ids: (step_of(*ids[:2]), 0))
        chips = pl.BlockSpec((N_CHIP, rr, cols), lambda *ids: (0, step_of(*ids[:2]), 0))
        shape = jax.ShapeDtypeStruct((rows, cols), F32)
        return _Rider((parts, w, m, v), [chips, blk, blk, blk], [shape] * 4, [blk] * 4, _adamw_block)

    return rider


def _adamw(parts, w, m, v, *, name, after=(), tr=256):
    rows, cols = w.shape

    def body(*refs):
        _adamw_block(refs[:4], refs[4 + len(after):])

    spec = pl.BlockSpec((tr, cols), lambda i: (i, 0))
    shape = jax.ShapeDtypeStruct((rows, cols), F32)
    return pl.pallas_call(
        body, name=name, grid=(rows // tr,),
        in_specs=[pl.BlockSpec((N_CHIP, tr, cols), lambda i: (0, i, 0)), spec, spec, spec]
        + [pl.BlockSpec(memory_space=pl.ANY)] * len(after),
        out_specs=[spec] * 4, out_shape=[shape] * 4,
        compiler_params=_params(("parallel",)),
    )(parts, w, m, v, *after)


def _small_update(part, w, m, v):
    rows = part.shape[0]

    def body(p_ref, w_ref, m_ref, v_ref, g_ref, d_ref, mo_ref, vo_ref, buf, send, recv):
        me, peers = _me_and_peers()
        buf[me] = p_ref[...]
        sends = []
        for k, (dev, _) in enumerate(peers):
            cp = pltpu.make_async_remote_copy(p_ref, buf.at[me], send.at[k], recv.at[k],
                                              device_id=dev, device_id_type=MESH)
            cp.start()
            sends.append(cp)
        for k, (dev, idx) in enumerate(peers):
            pltpu.make_async_remote_copy(p_ref, buf.at[idx], send.at[k], recv.at[k],
                                         device_id=dev, device_id_type=MESH).wait_recv()
        for cp in sends:
            cp.wait_send()
        g = buf[0]
        for b in range(1, N_DEV):
            g = g + buf[b]
        delta, m2, v2 = _adamw_math(g, w_ref[...], m_ref[...], v_ref[...])
        g_ref[...] = g
        d_ref[...] = delta
        mo_ref[...] = m2
        vo_ref[...] = v2

    vm = pl.BlockSpec(memory_space=pltpu.VMEM)
    shape = jax.ShapeDtypeStruct((rows, HD), F32)
    return pl.pallas_call(
        body, name="small_params_update",
        in_specs=[vm] * 4, out_specs=[vm] * 4, out_shape=[shape] * 4,
        scratch_shapes=[pltpu.VMEM((N_DEV, rows, HD), F32),
                        pltpu.SemaphoreType.DMA((N_DEV - 1,)), pltpu.SemaphoreType.DMA((N_DEV - 1,))],
    )(part, w, m, v)


def _pack_small(norm_mix, b_gate, qa, ka, qb, kb, rpb, norm_ffn):
    gains = jnp.concatenate([qa, ka, qb, kb, jnp.zeros((4, HD), F32)], axis=0)
    rpb_pad = jnp.pad(rpb.reshape(4 * (2 * WIN_R - 1), 2 * WIN_C - 1), ((0, 4), (0, HD - (2 * WIN_C - 1))))
    return jnp.concatenate([norm_mix.reshape(16, HD), b_gate.reshape(32, HD), gains, rpb_pad,
                            norm_ffn.reshape(16, HD), jnp.zeros((8, HD), F32)], axis=0)


LOSS_ROW = 136


def _unpack_small(p):
    norm_mix = p[0:16].reshape(1, D)
    b_gate = p[16:48].reshape(1, 2 * D)
    qa, ka, qb, kb = (p[48 + i:49 + i] for i in range(4))
    rpb = p[56:116, :2 * WIN_C - 1].reshape(1, 4, 2 * WIN_R - 1, 2 * WIN_C - 1)
    norm_ffn = p[120:136].reshape(1, D)
    return norm_mix, b_gate, qa, ka, qb, kb, rpb, norm_ffn


def kernel(x, norm_mix, w_in, b_gate, q_norm_a, k_norm_a, q_norm_b, k_norm_b, rpb_b, w_proj_a, w_proj_b, w_out, norm_ffn, w_up, w_down, loss_target, m_norm_mix, m_w_in, m_b_gate, m_q_norm_a, m_k_norm_a, m_q_norm_b, m_k_norm_b, m_rpb_b, m_w_proj_a, m_w_proj_b, m_w_out, m_norm_ffn, m_w_up, m_w_down, v_norm_mix, v_w_in, v_b_gate, v_q_norm_a, v_k_norm_a, v_q_norm_b, v_k_norm_b, v_rpb_b, v_w_proj_a, v_w_proj_b, v_w_out, v_norm_ffn, v_w_up, v_w_down):
    big_w = (w_in[0], w_proj_a[0], w_proj_b[0], w_out[0], w_up[0], w_down[0])
    big_m = (m_w_in[0], m_w_proj_a[0], m_w_proj_b[0], m_w_out[0], m_w_up[0], m_w_down[0])
    big_v = (v_w_in[0], v_w_proj_a[0], v_w_proj_b[0], v_w_out[0], v_w_up[0], v_w_down[0])
    names = ("w_in", "w_proj_a", "w_proj_b", "w_out", "w_up", "w_down")

    shards = [_cast_bf16(w) for w in big_w]
    g_in, = _gather_on_sequencer(shards[0:1], "gather_w_in")
    g_pa, g_pb, g_out, g_up = _gather_on_sequencer(shards[1:5], "gather_w_mix_up")
    g_down, = _gather_on_sequencer(shards[5:6], "gather_w_down")
    small_w = _pack_small(norm_mix, b_gate, q_norm_a, k_norm_a, q_norm_b, k_norm_b, rpb_b, norm_ffn)
    small_m = _pack_small(m_norm_mix, m_b_gate, m_q_norm_a, m_k_norm_a, m_q_norm_b, m_k_norm_b, m_rpb_b, m_norm_ffn)
    small_v = _pack_small(v_norm_mix, v_b_gate, v_q_norm_a, v_k_norm_a, v_q_norm_b, v_k_norm_b, v_rpb_b, v_norm_ffn)

    upd = [None] * 6
    in_flight = {}

    def weight_grads(tag, operands):
        sums = {i: _mm_tn_pair(a, b, name=f"grad_{names[i]}", **GRAD_TILES[i]) for i, (a, b) in operands.items()}
        new = list(sums.values())
        in_flight.update(zip(sums, _chip_exchange_on_sequencer(new, f"chip_exchange_{tag}")))
        return new

    def riders(name):
        i = {"ffn_up_bwd": 5, "proj_bwd": 4}.get(name)
        if i is None:
            return None
        return (_adamw_rider(in_flight.pop(i), big_w[i], big_m[i], big_v[i]),
                functools.partial(upd.__setitem__, i))

    loss, grad_x, small_g = _local_step(
        x[0], loss_target[0], norm_mix, b_gate, small_w[48:56], small_w[56:120], norm_ffn,
        g_in, g_pa, g_pb, g_out.reshape(D, D), g_up, g_down.reshape(D_FF, D), weight_grads, riders)

    g_norm_mix, g_b, g_gains, g_rpb, g_norm_ffn = small_g
    small_part = jnp.concatenate([g_norm_mix.reshape(16, HD), g_b.reshape(32, HD),
                                  g_gains, g_rpb, g_norm_ffn.reshape(16, HD),
                                  jnp.pad(loss, ((0, 7), (0, HD - 1)))], axis=0)
    slabs = _small_update(small_part, small_w, small_m, small_v)
    total = slabs[0][LOSS_ROW, 0]
    s_g, s_d, s_m, s_v = (_unpack_small(t) for t in slabs)

    last = grad_x
    for i, r in in_flight.items():
        upd[i] = _adamw(r, big_w[i], big_m[i], big_v[i], name=f"adamw_{names[i]}", after=[last])
        last = upd[i][0]
    b_g, b_d, b_m, b_v = ([u[j][None] for u in upd] for j in range(4))

    def order(small, big):
        nm, bg, qa, ka, qb, kb, rpb, nf = small
        w_in_, pa_, pb_, out_, up_, down_ = big
        return (nm, w_in_, bg, qa, ka, qb, kb, rpb, pa_, pb_, out_, nf, up_, down_)

    return (total, grad_x[None], *order(s_g, b_g), *order(s_d, b_d), *order(s_m, b_m), *order(s_v, b_v))
```

```python
import functools
from typing import Callable, NamedTuple

import jax
import jax.numpy as jnp
import numpy as np
from jax import lax
from jax.experimental import pallas as pl
from jax.experimental.pallas import tpu as pltpu
from jax.experimental.pallas import tpu_sc as plsc

F32 = jnp.float32
BF16 = jnp.bfloat16

N_DEV = 8
S = 2048
D = 2048
HD = 128
NH = 16
NH_A = 12
QKV = NH * HD
D_IN = 3 * QKV + 2 * D
D_BR = 512
D_FF = 4 * D
GRID_W = 64
ROWS = S // GRID_W
WIN_R = 8
WIN_C = 16
EPS = 1e-6
NEG = -1e30
SCALE = HD ** -0.5
ROPE_THETA = 10000.0
DILATIONS = (1, 4, 16)
HALF_A = 64
QB = 128

LR, B1, B2, AEPS, WD, STEP = 0.001, 0.9, 0.999, 1e-08, 0.01, 10
BC1 = 1.0 - B1 ** STEP
BC2 = 1.0 - B2 ** STEP

VMEM_LIMIT = 56 * 1024 * 1024
MESH = pl.DeviceIdType.MESH

NN = (((1,), (0,)), ((), ()))
NT = (((1,), (1,)), ((), ()))
TN = (((0,), (0,)), ((), ()))


def _params(sem):
    return pltpu.CompilerParams(dimension_semantics=sem, vmem_limit_bytes=VMEM_LIMIT)


def _matmul(a, b, *, product, grid, a_spec, b_spec, epi, out_shape, out_specs, name,
            extra=(), extra_specs=(), after=(), carried=False, rider=None, into=()):
    n_extra = len(extra)
    single = not isinstance(out_shape, (list, tuple))
    out_shape = [out_shape] if single else list(out_shape)
    out_specs = [out_specs] if single else list(out_specs)
    ride = rider(grid[0] * grid[1], lambda j, i: j * grid[1] + i) if rider else None
    r_in = list(ride.inputs) if ride else []
    n_main = len(out_shape)

    def body(a_ref, b_ref, *rest):
        n_in = n_extra + len(after) + len(r_in)
        ins, outs = rest[:n_in], rest[n_in + len(into):]
        epi(product(a_ref, b_ref, ins[:n_extra]), ins[:n_extra], outs[:n_main])
        if ride:
            ride.body(ins[n_extra + len(after):], outs[n_main:])

    res = pl.pallas_call(
        body, name=name, grid=grid,
        in_specs=[a_spec, b_spec, *extra_specs, *[pl.BlockSpec(memory_space=pl.ANY)] * len(after),
                  *(ride.in_specs if ride else []), *[pl.BlockSpec(memory_space=pl.ANY)] * len(into)],
        out_specs=out_specs + (ride.out_specs if ride else []),
        out_shape=out_shape + (ride.out_shape if ride else []),
        input_output_aliases={2 + n_extra + len(after) + len(r_in) + k: k for k in range(len(into))},
        compiler_params=_params(("arbitrary", "arbitrary") if carried else ("parallel", "parallel")),
    )(a, b, *extra, *after, *r_in, *into)
    main = res[0] if single else res[:n_main]
    return (main, res[n_main:]) if ride else main


def _dot(x, y, dims):
    return lax.dot_general(x, y, dims, preferred_element_type=F32)


def _epi_store(acc, ex, outs):
    outs[0][...] = acc.astype(outs[0].dtype)


def _epi_residual(acc, ex, outs):
    outs[0][...] = acc + ex[0][...]


def _mm_nn(a, b3, *, tm, tn, name, out_dtypes=(F32,), epi=_epi_store, extra=(), total=False,
           col0=0, width=None, into=()):
    m, kdim = a.shape
    g, _, ng = b3.shape
    n = g * ng
    c0 = col0 // tn
    if tn <= ng:
        npg = ng // tn
        b_spec = pl.BlockSpec((None, kdim, tn), lambda j, i: (j // npg, 0, j % npg))

        def product(a_ref, b_ref, ex):
            return _dot(a_ref[...], b_ref[...], NN)
    else:
        gb = tn // ng
        b_spec = pl.BlockSpec((gb, kdim, ng), lambda j, i: (j, 0, 0))

        def product(a_ref, b_ref, ex):
            return jnp.concatenate([_dot(a_ref[...], b_ref[q], NN) for q in range(gb)], axis=1)

    tile = pl.BlockSpec((tm, tn), lambda j, i: (i, j + c0))
    shapes = [jax.ShapeDtypeStruct((m, width or n), dt) for dt in out_dtypes]
    specs = [tile] * len(shapes)
    if total:
        shapes.append(jax.ShapeDtypeStruct((1, 1), F32))
        specs.append(pl.BlockSpec((1, 1), lambda j, i: (0, 0)))
    single = len(shapes) == 1
    return _matmul(
        a, b3, product=product, grid=(n // tn, m // tm), epi=epi, name=name, carried=total, into=into,
        a_spec=pl.BlockSpec((tm, kdim), lambda j, i: (i, 0)), b_spec=b_spec,
        extra=extra, extra_specs=[tile] * len(extra),
        out_shape=shapes[0] if single else shapes, out_specs=specs[0] if single else specs)


def _mm_nt(a, b3, *, tm, tn, name, out_dtype=F32, epi=_epi_store, extra=(), after=(), rider=None, more_b=()):
    m, kdim = a.shape
    _, n, _ = b3.shape
    n_b = len(more_b)

    def product(a_ref, b_ref, ex):
        acc, k0 = None, 0
        for ref in (b_ref, *ex[:n_b]):
            for q in range(ref.shape[0]):
                part = _dot(a_ref[:, k0:k0 + ref.shape[2]], ref[q], NT)
                acc = part if acc is None else acc + part
                k0 += ref.shape[2]
        return acc

    def write(acc, ex, outs):
        epi(acc, ex[n_b:], outs)

    def w_spec(w):
        return pl.BlockSpec((w.shape[0], tn, w.shape[2]), lambda j, i: (0, j, 0))

    tile = pl.BlockSpec((tm, tn), lambda j, i: (i, j))
    return _matmul(
        a, b3, product=product, grid=(n // tn, m // tm), epi=write, name=name,
        a_spec=pl.BlockSpec((tm, kdim), lambda j, i: (i, 0)), b_spec=w_spec(b3),
        extra=(*more_b, *extra), extra_specs=[w_spec(w) for w in more_b] + [tile] * len(extra),
        after=after, rider=rider,
        out_shape=jax.ShapeDtypeStruct((m, n), out_dtype), out_specs=tile)


def _mm_tn(a, b, *, tm, tn, name, groups=1, out_dtype=BF16):
    t, m = a.shape
    _, n = b.shape
    ng = n // groups
    if tn <= ng:
        npg = ng // tn
        out_spec = pl.BlockSpec((None, tm, tn), lambda j, i: (j // npg, i, j % npg))
        epi = _epi_store

        def product(a_ref, b_ref, ex):
            return _dot(a_ref[...], b_ref[...], TN)
    else:
        gb = tn // ng
        out_spec = pl.BlockSpec((gb, tm, ng), lambda j, i: (j, i, 0))

        def product(a_ref, b_ref, ex):
            return [_dot(a_ref[...], b_ref[:, q * ng:(q + 1) * ng], TN) for q in range(gb)]

        def epi(parts, ex, outs):
            for q, part in enumerate(parts):
                outs[0][q] = part.astype(out_dtype)

    return _matmul(
        a, b, product=product, grid=(n // tn, m // tm), epi=epi, name=name,
        a_spec=pl.BlockSpec((t, tm), lambda j, i: (0, i)),
        b_spec=pl.BlockSpec((t, tn), lambda j, i: (0, j)),
        out_shape=jax.ShapeDtypeStruct((groups, m, ng), out_dtype), out_specs=out_spec)


def _rms_fwd(x, g, *, name, tr=256):
    def body(x_ref, g_ref, y_ref, r_ref):
        xv = x_ref[...]
        r = lax.rsqrt(jnp.mean(xv * xv, axis=-1, keepdims=True) + EPS)
        y_ref[...] = (xv * r * g_ref[...]).astype(BF16)
        r_ref[...] = r

    row = pl.BlockSpec((tr, D), lambda i: (i, 0))
    return pl.pallas_call(
        body, name=name, grid=(S // tr,),
        in_specs=[row, pl.BlockSpec((1, D), lambda i: (0, 0))],
        out_specs=[row, pl.BlockSpec((tr, 1), lambda i: (i, 0))],
        out_shape=[jax.ShapeDtypeStruct((S, D), BF16), jax.ShapeDtypeStruct((S, 1), F32)],
        compiler_params=_params(("parallel",)),
    )(x, g)


def _rms_bwd(dy, x, rstd, g, resid, *, name, tr=256):
    def body(dy_ref, x_ref, r_ref, g_ref, res_ref, dx_ref, dxb_ref, dg_ref):
        r = r_ref[...]
        xh = x_ref[...] * r
        dyv = dy_ref[...]
        t = dyv * g_ref[...]
        dx = r * (t - xh * jnp.mean(t * xh, axis=-1, keepdims=True)) + res_ref[...]
        dx_ref[...] = dx
        dxb_ref[...] = dx.astype(BF16)
        part = jnp.sum(dyv * xh, axis=0, keepdims=True)

        @pl.when(pl.program_id(0) == 0)
        def _():
            dg_ref[...] = part

        @pl.when(pl.program_id(0) > 0)
        def _():
            dg_ref[...] += part

    row = pl.BlockSpec((tr, D), lambda i: (i, 0))
    vec = pl.BlockSpec((1, D), lambda i: (0, 0))
    return pl.pallas_call(
        body, name=name, grid=(S // tr,),
        in_specs=[row, row, pl.BlockSpec((tr, 1), lambda i: (i, 0)), vec, row],
        out_specs=[row, row, vec],
        out_shape=[jax.ShapeDtypeStruct((S, D), F32), jax.ShapeDtypeStruct((S, D), BF16),
                   jax.ShapeDtypeStruct((1, D), F32)],
        compiler_params=_params(("arbitrary",)),
    )(dy, x, rstd, g, resid)


def _rope_tables():
    pos = np.arange(S, dtype=np.float32)
    inv = (ROPE_THETA ** (-np.arange(0, HD, 2, dtype=np.float32) / HD)).astype(np.float32)
    ang = pos[:, None] * inv[None, :]
    cos, sin = np.cos(ang), np.sin(ang)
    return (jnp.asarray(np.concatenate([cos, cos], axis=-1), F32),
            jnp.asarray(np.concatenate([-sin, sin], axis=-1), F32))


def _swap_halves(t):
    return pltpu.roll(t, HD // 2, axis=1)


TOK = 256


def _lane_block_spec(d, last=HD):
    return pl.BlockSpec((4, TOK // d, d * last), lambda i: (0, i, 0))


def _to_lane_blocks(dst, head, val, d, scr, dtype):
    w = val.shape[1]
    if d == 1:
        dst[head] = val.astype(dtype)
        return
    scr[...] = val
    for r in range(d):
        dst[head, :, r * w:(r + 1) * w] = scr[pl.ds(r, TOK // d, stride=d), :].astype(dtype)


def _from_lane_blocks(src, head, d, w, scr):
    if d == 1:
        return src[head].astype(F32)
    for r in range(d):
        scr[pl.ds(r, TOK // d, stride=d), :] = src[head, :, r * w:(r + 1) * w].astype(F32)
    return scr[...]


def _qk_prep(proj, gains, cos2, sin2):
    def body(q_ref, k_ref, v_ref, g_ref, c_ref, s_ref, *rest):
        outs, scr = rest[:-1], rest[-1]
        cos, sin = c_ref[...], s_ref[...]
        for which, (src, row_a, row_b) in enumerate(((q_ref, 0, 2), (k_ref, 1, 3), (v_ref, None, None))):
            for h in range(NH):
                y = src[:, h * HD:(h + 1) * HD]
                if row_a is not None:
                    y = y * lax.rsqrt(jnp.mean(y * y, axis=-1, keepdims=True) + EPS)
                    if h < NH_A:
                        y = y * g_ref[row_a:row_a + 1, :]
                        y = y * cos + _swap_halves(y) * sin
                    else:
                        y = y * g_ref[row_b:row_b + 1, :]
                if h < NH_A:
                    gi = h // 4
                    _to_lane_blocks(outs[3 * gi + which], h % 4, y, DILATIONS[gi], scr, BF16)
                else:
                    hb = h - NH_A
                    outs[9 + which][:, hb * HD:(hb + 1) * HD] = y.astype(BF16)

    def blk(c):
        return pl.BlockSpec((TOK, QKV), lambda i: (i, c))
    tab = pl.BlockSpec((TOK, HD), lambda i: (i, 0))
    out_specs, out_shape = [], []
    for d in DILATIONS:
        out_specs += [_lane_block_spec(d)] * 3
        out_shape += [jax.ShapeDtypeStruct((4, S // d, d * HD), BF16)] * 3
    out_specs += [pl.BlockSpec((TOK, D_BR), lambda i: (i, 0))] * 3
    out_shape += [jax.ShapeDtypeStruct((S, D_BR), BF16)] * 3
    outs = pl.pallas_call(
        body, name="qk_prep", grid=(S // TOK,),
        in_specs=[blk(0), blk(1), blk(2), pl.BlockSpec((8, HD), lambda i: (0, 0)), tab, tab],
        out_specs=out_specs, out_shape=out_shape,
        scratch_shapes=[pltpu.VMEM((TOK, HD), F32)],
        compiler_params=_params(("parallel",)),
    )(proj, proj, proj, gains, cos2, sin2)
    return [tuple(outs[3 * gi:3 * gi + 3]) for gi in range(3)], tuple(outs[9:12])


def _qk_prep_bwd(dproj, proj, gains, cos2, sin2, grads_a, grads_b):
    def body(dp_in, q_ref, k_ref, g_ref, c_ref, s_ref, *rest):
        grads, (dp_out, dg_ref, scr) = rest[:12], rest[12:]
        del dp_in
        cos, sin = c_ref[...], s_ref[...]

        def grad_of(which, h):
            if h < NH_A:
                gi = h // 4
                return _from_lane_blocks(grads[3 * gi + which], h % 4, DILATIONS[gi], HD, scr)
            hb = h - NH_A
            return grads[9 + which][:, hb * HD:(hb + 1) * HD]

        dg_rows = []
        for which, (src, base, row_a, row_b) in enumerate(((q_ref, 0, 0, 2), (k_ref, QKV, 1, 3))):
            dg_a = jnp.zeros((1, HD), F32)
            dg_b = jnp.zeros((1, HD), F32)
            for h in range(NH):
                t = src[:, h * HD:(h + 1) * HD]
                dy = grad_of(which, h)
                r = lax.rsqrt(jnp.mean(t * t, axis=-1, keepdims=True) + EPS)
                xh = t * r
                if h < NH_A:
                    dy = dy * cos - _swap_halves(dy) * sin
                    gain = g_ref[row_a:row_a + 1, :]
                    dg_a = dg_a + jnp.sum(dy * xh, axis=0, keepdims=True)
                else:
                    gain = g_ref[row_b:row_b + 1, :]
                    dg_b = dg_b + jnp.sum(dy * xh, axis=0, keepdims=True)
                u = dy * gain
                dx = r * (u - xh * jnp.mean(u * xh, axis=-1, keepdims=True))
                dp_out[:, base + h * HD:base + (h + 1) * HD] = dx.astype(BF16)
            dg_rows += [(row_a, dg_a), (row_b, dg_b)]
        for h in range(NH):
            dp_out[:, 2 * QKV + h * HD:2 * QKV + (h + 1) * HD] = grad_of(2, h).astype(BF16)

        @pl.when(pl.program_id(0) == 0)
        def _():
            dg_ref[...] = jnp.zeros((8, HD), F32)

        for row, val in dg_rows:
            dg_ref[row:row + 1, :] += val

    def blk(c):
        return pl.BlockSpec((TOK, QKV), lambda i: (i, c))
    tab = pl.BlockSpec((TOK, HD), lambda i: (i, 0))
    gain_spec = pl.BlockSpec((8, HD), lambda i: (0, 0))
    grad_specs = [s for d in DILATIONS for s in [_lane_block_spec(d)] * 3]
    grad_specs += [pl.BlockSpec((TOK, D_BR), lambda i: (i, 0))] * 3
    return pl.pallas_call(
        body, name="qk_prep_bwd", grid=(S // TOK,),
        in_specs=[pl.BlockSpec(memory_space=pl.ANY), blk(0), blk(1), gain_spec, tab, tab] + grad_specs,
        out_specs=[pl.BlockSpec((TOK, 3 * QKV), lambda i: (i, 0)), gain_spec],
        out_shape=[jax.ShapeDtypeStruct((S, D_IN), BF16), jax.ShapeDtypeStruct((8, HD), F32)],
        input_output_aliases={0: 0},
        scratch_shapes=[pltpu.VMEM((TOK, HD), F32)],
        compiler_params=_params(("arbitrary",)),
    )(dproj, proj, proj, gains, cos2, sin2, *[g for grp in grads_a for g in grp], *grads_b)


def _mix_fwd(oa, ob, w_pa, w_pb, proj, b_gate, *, tr=256):
    def body(oa_ref, ob_ref, pa_ref, pb_ref, la_ref, lb_ref, ba_ref, bb_ref, mix_ref, ya_ref, yb_ref):
        ya = jnp.concatenate([_dot(oa_ref[...], pa_ref[q], NN) for q in range(N_DEV)], axis=1)
        yb = jnp.concatenate([_dot(ob_ref[...], pb_ref[q], NN) for q in range(N_DEV)], axis=1)
        ga = jax.nn.sigmoid(la_ref[...] + ba_ref[...])
        gb = jax.nn.sigmoid(lb_ref[...] + bb_ref[...])
        mix_ref[...] = (ga * ya + gb * yb).astype(BF16)
        ya_ref[...] = ya.astype(BF16)
        yb_ref[...] = yb.astype(BF16)

    row = pl.BlockSpec((tr, D), lambda i: (i, 0))
    branch = pl.BlockSpec((tr, D_BR), lambda i: (i, 0))
    whole = pl.BlockSpec((N_DEV, D_BR, D // N_DEV), lambda i: (0, 0, 0))
    return pl.pallas_call(
        body, name="mix_fwd", grid=(S // tr,),
        in_specs=[branch, branch, whole, whole,
                  pl.BlockSpec((tr, D), lambda i: (i, 3)), pl.BlockSpec((tr, D), lambda i: (i, 4)),
                  pl.BlockSpec((1, D), lambda i: (0, 0)), pl.BlockSpec((1, D), lambda i: (0, 1))],
        out_specs=[row, row, row], out_shape=[jax.ShapeDtypeStruct((S, D), BF16)] * 3,
        compiler_params=_params(("parallel",)),
    )(oa, ob, w_pa, w_pb, proj, proj, b_gate, b_gate)


def _mix_bwd(dh1b, w_out, proj, b_gate, ya, yb, *, tr=256):
    def body(dh_ref, w_ref, la_ref, lb_ref, b_ref, ya_ref, yb_ref, dya_ref, dyb_ref, dp_ref, db_ref):
        dm = _dot(dh_ref[...], w_ref[...], NT)
        parts = []
        for l_ref, y_ref, dy_ref, lo in ((la_ref, ya_ref, dya_ref, 0), (lb_ref, yb_ref, dyb_ref, D)):
            g = jax.nn.sigmoid(l_ref[...] + b_ref[:, lo:lo + D])
            dy_ref[...] = (dm * g).astype(BF16)
            dl = dm * y_ref[...].astype(F32) * g * (1.0 - g)
            dp_ref[:, lo:lo + D] = dl.astype(BF16)
            parts.append(jnp.sum(dl, axis=0, keepdims=True))
        part = jnp.concatenate(parts, axis=1)

        @pl.when(pl.program_id(0) == 0)
        def _():
            db_ref[...] = part

        @pl.when(pl.program_id(0) > 0)
        def _():
            db_ref[...] += part

    row = pl.BlockSpec((tr, D), lambda i: (i, 0))
    vec = pl.BlockSpec((1, 2 * D), lambda i: (0, 0))
    gate_cols = pl.BlockSpec((pl.Element(tr), pl.Element(2 * D)), lambda i: (i * tr, 3 * QKV))
    return pl.pallas_call(
        body, name="mix_bwd", grid=(S // tr,),
        in_specs=[row, pl.BlockSpec((D, D), lambda i: (0, 0)),
                  pl.BlockSpec((tr, D), lambda i: (i, 3)), pl.BlockSpec((tr, D), lambda i: (i, 4)), vec, row, row],
        out_specs=[row, row, gate_cols, vec],
        out_shape=[jax.ShapeDtypeStruct((S, D), BF16), jax.ShapeDtypeStruct((S, D), BF16),
                   jax.ShapeDtypeStruct((S, D_IN), BF16), jax.ShapeDtypeStruct((1, 2 * D), F32)],
        compiler_params=_params(("arbitrary",)),
    )(dh1b, w_out, proj, proj, b_gate, ya, yb)


def _band_blocks(m_len):
    wk = min(m_len, QB + 2 * QB)
    return [(qb * QB, min(max(qb * QB - QB, 0), m_len - wk), wk) for qb in range(m_len // QB)]


def _band_scores(q, kw, q0, k0, wk):
    s = _dot(q, kw, NT) * SCALE
    qpos = q0 + lax.broadcasted_iota(jnp.int32, (QB, 1), 0)
    kpos = k0 + lax.broadcasted_iota(jnp.int32, (1, wk), 1)
    return jnp.where(jnp.abs(kpos - qpos) <= HALF_A, s, NEG)


def _attn_a_fwd(q, k, v, gi):
    d = DILATIONS[gi]
    m_len = S // d

    def body(q_ref, k_ref, v_ref, o_ref, lse_ref):
        for r in range(d):
            lanes = slice(r * HD, (r + 1) * HD)
            for q0, k0, wk in _band_blocks(m_len):
                s = _band_scores(q_ref[q0:q0 + QB, lanes], k_ref[k0:k0 + wk, lanes], q0, k0, wk)
                m = jnp.max(s, axis=-1, keepdims=True)
                p = jnp.exp(s - m)
                l = jnp.sum(p, axis=-1, keepdims=True)
                o_ref[q0:q0 + QB, lanes] = _dot(p.astype(BF16), v_ref[k0:k0 + wk, lanes], NN) / l
                lse_ref[q0:q0 + QB, r:r + 1] = m + jnp.log(l)

    head = pl.BlockSpec((None, m_len, d * HD), lambda h: (h, 0, 0))
    stat = pl.BlockSpec((None, m_len, d), lambda h: (h, 0, 0))
    return pl.pallas_call(
        body, name=f"attn_a_fwd_{gi}", grid=(4,),
        in_specs=[head, head, head], out_specs=[head, stat],
        out_shape=[jax.ShapeDtypeStruct((4, m_len, d * HD), F32), jax.ShapeDtypeStruct((4, m_len, d), F32)],
        compiler_params=_params(("parallel",)),
    )(q, k, v)


def _combine_a(os, lses):
    def body(o0, o1, o2, l0, l1, l2, oa_ref, lse_ref, scr, scr1):
        for h in range(4):
            o = [_from_lane_blocks(ref, h, d, HD, scr) for ref, d in zip((o0, o1, o2), DILATIONS)]
            a, b, c = (_from_lane_blocks(ref, h, d, 1, scr1) for ref, d in zip((l0, l1, l2), DILATIONS))
            m = jnp.maximum(jnp.maximum(a, b), c)
            wa, wb, wc = jnp.exp(a - m), jnp.exp(b - m), jnp.exp(c - m)
            tot = wa + wb + wc
            oa_ref[:, h * HD:(h + 1) * HD] = ((wa * o[0] + wb * o[1] + wc * o[2]) / tot).astype(BF16)
            lse_ref[h] = m + jnp.log(tot)

    return pl.pallas_call(
        body, name="combine_a", grid=(S // TOK,),
        in_specs=[_lane_block_spec(d) for d in DILATIONS] + [_lane_block_spec(d, 1) for d in DILATIONS],
        out_specs=[pl.BlockSpec((TOK, D_BR), lambda i: (i, 0)), pl.BlockSpec((4, TOK, 1), lambda i: (0, i, 0))],
        out_shape=[jax.ShapeDtypeStruct((S, D_BR), BF16), jax.ShapeDtypeStruct((4, S, 1), F32)],
        scratch_shapes=[pltpu.VMEM((TOK, HD), F32), pltpu.VMEM((TOK, 1), F32)],
        compiler_params=_params(("parallel",)),
    )(*os, *lses)


def _proj_a_bwd(dya, w_pa, oa, lse):
    kg = D // N_DEV

    def body(dy_ref, w_ref, o_ref, l_ref, *rest):
        outs, (scr, scr1) = rest[:9], rest[9:]
        doa = _dot(dy_ref[:, 0:kg], w_ref[0], NT)
        for q in range(1, N_DEV):
            doa = doa + _dot(dy_ref[:, q * kg:(q + 1) * kg], w_ref[q], NT)
        for h in range(4):
            do = doa[:, h * HD:(h + 1) * HD]
            dsum = jnp.sum(do * o_ref[:, h * HD:(h + 1) * HD].astype(F32), axis=-1, keepdims=True)
            for gi, d in enumerate(DILATIONS):
                _to_lane_blocks(outs[3 * gi], h, do, d, scr, BF16)
                _to_lane_blocks(outs[3 * gi + 1], h, l_ref[h], d, scr1, F32)
                _to_lane_blocks(outs[3 * gi + 2], h, dsum, d, scr1, F32)

    row = pl.BlockSpec((TOK, D_BR), lambda i: (i, 0))
    out_specs, out_shape = [], []
    for d in DILATIONS:
        out_specs += [_lane_block_spec(d), _lane_block_spec(d, 1), _lane_block_spec(d, 1)]
        out_shape += [jax.ShapeDtypeStruct((4, S // d, d * HD), BF16)] + [jax.ShapeDtypeStruct((4, S // d, d), F32)] * 2
    outs = pl.pallas_call(
        body, name="proj_a_bwd", grid=(S // TOK,),
        in_specs=[pl.BlockSpec((TOK, D), lambda i: (i, 0)),
                  pl.BlockSpec((N_DEV, D_BR, kg), lambda i: (0, 0, 0)),
                  row, pl.BlockSpec((4, TOK, 1), lambda i: (0, i, 0))],
        out_specs=out_specs, out_shape=out_shape,
        scratch_shapes=[pltpu.VMEM((TOK, HD), F32), pltpu.VMEM((TOK, 1), F32)],
        compiler_params=_params(("parallel",)),
    )(dya, w_pa, oa, lse)
    return [tuple(outs[3 * gi:3 * gi + 3]) for gi in range(3)]


def _attn_a_bwd(q, k, v, do, lse, dsum, gi):
    d = DILATIONS[gi]
    m_len = S // d

    def body(q_ref, k_ref, v_ref, do_ref, lse_ref, dsum_ref, dq_ref, dk_ref, dv_ref):
        dk_ref[...] = jnp.zeros((m_len, d * HD), F32)
        dv_ref[...] = jnp.zeros((m_len, d * HD), F32)
        for r in range(d):
            lanes = slice(r * HD, (r + 1) * HD)
            for q0, k0, wk in _band_blocks(m_len):
                rows, keys = slice(q0, q0 + QB), slice(k0, k0 + wk)
                qv, kw, vw, dov = q_ref[rows, lanes], k_ref[keys, lanes], v_ref[keys, lanes], do_ref[rows, lanes]
                p = jnp.exp(_band_scores(qv, kw, q0, k0, wk) - lse_ref[rows, r:r + 1])
                ds = (p * (_dot(dov, vw, NT) - dsum_ref[rows, r:r + 1]) * SCALE).astype(BF16)
                dq_ref[rows, lanes] = _dot(ds, kw, NN)
                dk_ref[keys, lanes] += _dot(ds, qv, TN)
                dv_ref[keys, lanes] += _dot(p.astype(BF16), dov, TN)

    head = pl.BlockSpec((None, m_len, d * HD), lambda h: (h, 0, 0))
    stat = pl.BlockSpec((None, m_len, d), lambda h: (h, 0, 0))
    shape = jax.ShapeDtypeStruct((4, m_len, d * HD), F32)
    return pl.pallas_call(
        body, name=f"attn_a_bwd_{gi}", grid=(4,),
        in_specs=[head, head, head, head, stat, stat], out_specs=[head, head, head],
        out_shape=[shape, shape, shape],
        compiler_params=_params(("parallel",)),
    )(q, k, v, do, lse, dsum)


KEYS_B = WIN_R * GRID_W
N_OFF = WIN_R


def _bias_constants():
    q = np.arange(GRID_W)[:, None]
    kc = np.arange(GRID_W)[None, :]
    dc = np.clip(kc - q, -(WIN_C - 1), WIN_C - 1) + (WIN_C - 1)
    expand = np.zeros((HD, GRID_W * GRID_W), np.float32)
    expand[dc.reshape(-1), np.arange(GRID_W * GRID_W)] = 1.0
    cs = np.clip(q - WIN_C // 2, 0, GRID_W - WIN_C)
    keep = ((kc >= cs) & (kc < cs + WIN_C)).reshape(1, -1).astype(np.float32)
    sel = np.zeros((64, 4 * N_OFF * WIN_R), np.float32)
    for h in range(4):
        for off in range(N_OFF):
            for j in range(WIN_R):
                sel[h * (2 * WIN_R - 1) + off + j, (h * N_OFF + off) * WIN_R + j] = 1.0
    return jnp.asarray(expand), jnp.asarray(keep), jnp.asarray(sel)


def _bias_expand(rpb_pad, expand, keep, sel):
    def body(r_ref, e_ref, k_ref, s_ref, o_ref):
        t = lax.dot_general(r_ref[...], e_ref[...], NN, precision=lax.Precision.HIGHEST,
                            preferred_element_type=F32)
        rows = lax.dot_general(s_ref[...], t, TN, precision=lax.Precision.HIGHEST,
                               preferred_element_type=F32)
        o_ref[...] = jnp.where(k_ref[...] > 0.5, rows, NEG)

    return pl.pallas_call(
        body, name="bias_expand",
        out_shape=jax.ShapeDtypeStruct((4 * N_OFF * WIN_R, GRID_W * GRID_W), F32),
        compiler_params=pltpu.CompilerParams(vmem_limit_bytes=VMEM_LIMIT),
    )(rpb_pad, expand, keep, sel)


def _bias_reduce(dbias_rows, expand, sel):
    def body(x_ref, e_ref, s_ref, o_ref):
        z = lax.dot_general(x_ref[...], e_ref[...], NT, precision=lax.Precision.HIGHEST,
                            preferred_element_type=F32)
        o_ref[...] = lax.dot_general(s_ref[...], z, NN, precision=lax.Precision.HIGHEST,
                                     preferred_element_type=F32)

    return pl.pallas_call(
        body, name="bias_reduce", out_shape=jax.ShapeDtypeStruct((64, HD), F32),
        compiler_params=pltpu.CompilerParams(vmem_limit_bytes=VMEM_LIMIT),
    )(dbias_rows, expand, sel)


def _rows_to_tab(rows):
    t = rows.reshape(4, N_OFF, WIN_R, GRID_W, GRID_W)
    return t.transpose(0, 1, 3, 2, 4).reshape(4, N_OFF, GRID_W, KEYS_B)


def _tab_to_rows(tab):
    t = tab.reshape(4, N_OFF, GRID_W, WIN_R, GRID_W)
    return t.transpose(0, 1, 3, 2, 4).reshape(4 * N_OFF * WIN_R, GRID_W * GRID_W)


def _row_window(r):
    r0 = jnp.clip(r - WIN_R // 2, 0, ROWS - WIN_R)
    off = r0 + (WIN_R - 1) - r
    return pl.multiple_of(r * GRID_W, GRID_W), pl.multiple_of(r0 * GRID_W, GRID_W), off


def _attn_b_fwd(qn, kn, vb, bias_tab):
    def body(q_ref, k_ref, v_ref, b_ref, o_ref, lse_ref):
        def row(r, carry):
            qs, ks, off = _row_window(r)
            q = q_ref[pl.ds(qs, GRID_W), :]
            s = lax.dot_general(q, k_ref[pl.ds(ks, KEYS_B), :], NT, preferred_element_type=F32) * SCALE
            s = s + b_ref[off]
            m = jnp.max(s, axis=-1, keepdims=True)
            p = jnp.exp(s - m)
            l = jnp.sum(p, axis=-1, keepdims=True)
            o = lax.dot_general(p.astype(BF16), v_ref[pl.ds(ks, KEYS_B), :], NN, preferred_element_type=F32)
            o_ref[pl.ds(qs, GRID_W), :] = (o / l).astype(BF16)
            lse_ref[pl.ds(qs, GRID_W), :] = m + jnp.log(l)
            return carry

        lax.fori_loop(0, ROWS, row, 0, unroll=2)

    full = pl.BlockSpec((S, HD), lambda h: (0, h))
    return pl.pallas_call(
        body, name="attn_b_fwd", grid=(4,),
        in_specs=[full, full, full, pl.BlockSpec((None, N_OFF, GRID_W, KEYS_B), lambda h: (h, 0, 0, 0))],
        out_specs=[pl.BlockSpec((S, HD), lambda h: (0, h)), pl.BlockSpec((None, S, 1), lambda h: (h, 0, 0))],
        out_shape=[jax.ShapeDtypeStruct((S, D_BR), BF16), jax.ShapeDtypeStruct((4, S, 1), F32)],
        compiler_params=_params(("parallel",)),
    )(qn, kn, vb, bias_tab)


def _attn_b_bwd(qn, kn, vb, bias_tab, ob, dob, lse):
    def body(q_ref, k_ref, v_ref, b_ref, o_ref, do_ref, lse_ref, dq_ref, dk_ref, dv_ref, db_ref):
        dk_ref[...] = jnp.zeros((S, HD), F32)
        dv_ref[...] = jnp.zeros((S, HD), F32)
        db_ref[...] = jnp.zeros((N_OFF, GRID_W, KEYS_B), F32)

        def row(r, carry):
            qs, ks, off = _row_window(r)
            rows = pl.ds(qs, GRID_W)
            keys = pl.ds(ks, KEYS_B)
            q = q_ref[rows, :]
            kw = k_ref[keys, :]
            s = lax.dot_general(q, kw, NT, preferred_element_type=F32) * SCALE + b_ref[off]
            p = jnp.exp(s - lse_ref[rows, :])
            do = do_ref[rows, :]
            dobf = do.astype(BF16)
            dsum = jnp.sum(do * o_ref[rows, :].astype(F32), axis=-1, keepdims=True)
            dp = lax.dot_general(dobf, v_ref[keys, :], NT, preferred_element_type=F32)
            ds = p * (dp - dsum)
            db_ref[off] += ds
            dsb = (ds * SCALE).astype(BF16)
            dq_ref[rows, :] = lax.dot_general(dsb, kw, NN, preferred_element_type=F32)
            dk_ref[keys, :] += lax.dot_general(dsb, q, TN, preferred_element_type=F32)
            dv_ref[keys, :] += lax.dot_general(p.astype(BF16), dobf, TN, preferred_element_type=F32)
            return carry

        lax.fori_loop(0, ROWS, row, 0, unroll=2)

    full = pl.BlockSpec((S, HD), lambda h: (0, h))
    slot = pl.BlockSpec((S, HD), lambda h: (0, h))
    tab = pl.BlockSpec((None, N_OFF, GRID_W, KEYS_B), lambda h: (h, 0, 0, 0))
    shape = jax.ShapeDtypeStruct((S, D_BR), F32)
    return pl.pallas_call(
        body, name="attn_b_bwd", grid=(4,),
        in_specs=[full, full, full, tab, slot, slot, pl.BlockSpec((None, S, 1), lambda h: (h, 0, 0))],
        out_specs=[slot, slot, slot, tab],
        out_shape=[shape, shape, shape, jax.ShapeDtypeStruct((4, N_OFF, GRID_W, KEYS_B), F32)],
        compiler_params=_params(("parallel",)),
    )(qn, kn, vb, bias_tab, ob, dob, lse)


def _epi_relu_sq(acc, ex, outs):
    u = jnp.maximum(acc, 0.0)
    outs[0][...] = u.astype(BF16)
    outs[1][...] = (u * u).astype(BF16)


def _epi_relu_sq_bwd(acc, ex, outs):
    outs[0][...] = (acc * (2.0 * ex[0][...].astype(F32))).astype(BF16)


def _epi_loss_head(acc, ex, outs):
    e = acc + ex[0][...] - ex[1][...]
    dy = e * (1.0 / D)
    outs[0][...] = dy
    outs[1][...] = dy.astype(BF16)
    part = (0.5 / D) * jnp.sum(jnp.sum(e * e, axis=-1, keepdims=True), axis=0, keepdims=True)
    first = (pl.program_id(0) == 0) & (pl.program_id(1) == 0)

    @pl.when(first)
    def _():
        outs[2][...] = part

    @pl.when(jnp.logical_not(first))
    def _():
        outs[2][...] += part


def _local_step(x, target, norm_mix, b_gate, gains, rpb_pad, norm_ffn,
                w_in, w_pa, w_pb, w_out, w_up, w_down, weight_grads, riders=lambda name: None):
    def ridden(name, *args, **kwargs):
        ride = riders(name)
        if ride is None:
            return _mm_nt(*args, name=name, **kwargs)
        out, rode = _mm_nt(*args, name=name, rider=ride[0], **kwargs)
        ride[1](rode)
        return out

    cos2, sin2 = _rope_tables()
    expand, keep, sel = _bias_constants()
    w_out3 = w_out[None]

    xn, rstd1 = _rms_fwd(x, norm_mix, name="rms_mix")
    proj = _mm_nn(xn, w_in, tm=1024, tn=1280, name="proj")
    qkv_a, qkv_b = _qk_prep(proj, gains, cos2, sin2)
    fwd_a = [_attn_a_fwd(*qkv_a[gi], gi) for gi in range(3)]
    oa, lse_a = _combine_a([o for o, _ in fwd_a], [l for _, l in fwd_a])
    bias_tab = _rows_to_tab(_bias_expand(rpb_pad, expand, keep, sel))
    ob, lse_b = _attn_b_fwd(*qkv_b, bias_tab)
    mixed, ya, yb = _mix_fwd(oa, ob, w_pa, w_pb, proj, b_gate)
    h1 = _mm_nn(mixed, w_out3, tm=1024, tn=1024, name="out_proj", epi=_epi_residual, extra=(x,))
    hn, rstd2 = _rms_fwd(h1, norm_ffn, name="rms_ffn")
    u, usq = _mm_nn(hn, w_up, tm=1024, tn=1024, name="ffn_up", epi=_epi_relu_sq,
                    out_dtypes=(BF16, BF16))
    dy, dyb, loss = _mm_nn(usq, w_down[0], tm=512, tn=512, name="ffn_down_0", epi=_epi_loss_head,
                           extra=(h1, target), out_dtypes=(F32, BF16), total=True, width=D)
    dy, dyb, loss_1 = _mm_nn(usq, w_down[1], tm=512, tn=512, name="ffn_down_1", epi=_epi_loss_head,
                             extra=(h1, target), out_dtypes=(F32, BF16), total=True, width=D,
                             col0=D // 2, into=(dy, dyb))
    loss = loss + loss_1

    sent = weight_grads("w_down", {5: (usq, dyb)})
    du = _mm_nt(dyb, w_down[0], more_b=(w_down[1],), tm=1024, tn=1024, name="ffn_down_bwd", out_dtype=BF16,
                epi=_epi_relu_sq_bwd, extra=(u,), after=sent)
    sent = weight_grads("w_up", {4: (hn, du)})
    dhn = ridden("ffn_up_bwd", du, w_up, tm=512, tn=512, after=sent)
    dh1, dh1b, g_norm_ffn = _rms_bwd(dhn, h1, rstd2, norm_ffn, dy, name="rms_ffn_bwd")

    dya, dyb2, dproj, g_b = _mix_bwd(dh1b, w_out, proj, b_gate, ya, yb)
    sent = weight_grads("w_mix", {3: (mixed, dh1b), 1: (oa, dya), 2: (ob, dyb2)})
    dob = _mm_nt(dyb2, w_pb, tm=1024, tn=D_BR, name="proj_b_bwd", after=sent)
    prep = _proj_a_bwd(dya, w_pa, oa, lse_a)
    grads_a = [_attn_a_bwd(*qkv_a[gi], *prep[gi], gi) for gi in range(3)]
    dqb, dkb, dvb, dbias = _attn_b_bwd(*qkv_b, bias_tab, ob, dob, lse_b)
    g_rpb = _bias_reduce(_tab_to_rows(dbias), expand, sel)
    dproj, g_gains = _qk_prep_bwd(dproj, proj, gains, cos2, sin2, grads_a, (dqb, dkb, dvb))
    sent = weight_grads("w_in", {0: (xn, dproj)})
    dxn = ridden("proj_bwd", dproj, w_in, tm=256, tn=512, after=sent)
    grad_x, _, g_norm_mix = _rms_bwd(dxn, x, rstd1, norm_mix, dh1, name="rms_mix_bwd")

    small = (g_norm_mix, g_b, g_gains, g_rpb, g_norm_ffn)
    return loss, grad_x, small


def _cast_bf16(w, *, part=0, parts=1, tr=256):
    rows, cols = w.shape[0], w.shape[1] // parts
    tr = min(tr, rows)

    def body(w_ref, o_ref):
        o_ref[...] = w_ref[...].astype(BF16)

    return pl.pallas_call(
        body, name=f"cast_{rows}x{cols}_{part}", grid=(rows // tr,),
        in_specs=[pl.BlockSpec((tr, cols), lambda i: (i, part))],
        out_specs=pl.BlockSpec((tr, cols), lambda i: (i, 0)),
        out_shape=jax.ShapeDtypeStruct((rows, cols), BF16), compiler_params=_params(("parallel",)),
    )(w)


def _me_and_peers():
    x, y, c = lax.axis_index("x"), lax.axis_index("y"), lax.axis_index("c")
    me = 4 * x + 2 * y + c
    peers = []
    for k in range(1, N_DEV):
        px = 1 - x if k & 4 else x
        py = 1 - y if k & 2 else y
        pc = 1 - c if k & 1 else c
        peers.append(((px, py, pc), 4 * px + 2 * py + pc))
    return me, peers


def _gather_on_sequencer(shards, name):
    n = len(shards)
    hbm = pltpu.MemorySpace.HBM
    ins = [jax.new_ref(s, memory_space=hbm) for s in shards]
    outs = [jax.empty_ref(jax.ShapeDtypeStruct((N_DEV,) + s.shape, s.dtype), memory_space=hbm) for s in shards]

    @pl.kernel(mesh=plsc.ScalarSubcoreMesh(axis_name="seq", num_cores=1), name=name,
               scratch_types=(pltpu.SemaphoreType.DMA((n, N_DEV - 1)), pltpu.SemaphoreType.DMA((n, N_DEV - 1)),
                              pltpu.SemaphoreType.DMA((n,))),
               compiler_params=pltpu.CompilerParams(collective_id=0))
    def launch(send, recv, lsem):
        x, y, c = lax.axis_index("x"), lax.axis_index("y"), lax.axis_index("c")
        me, sibling = (x, y, c), (x, y, 1 - c)
        chips = [(1 - x, y), (x, 1 - y), (1 - x, 1 - y)]
        barrier = pltpu.get_barrier_semaphore()
        for peer in [sibling] + [(*chip, c) for chip in chips]:
            pl.semaphore_signal(barrier, inc=1, device_id=peer, device_id_type=MESH)
        pl.semaphore_wait(barrier, 4)

        def copy(w, k, block, to, src=None):
            px, py, pc = block
            dst = outs[w].at[4 * px + 2 * py + pc]
            return pltpu.make_async_remote_copy(dst if src is None else src, dst, send.at[w, k], recv.at[w, k],
                                                device_id=to, device_id_type=MESH)

        local = [pltpu.make_async_copy(ins[w], outs[w].at[4 * x + 2 * y + c], lsem.at[w]) for w in range(n)]
        for cp in local:
            cp.start()
        first = []
        for w in range(n):
            first += [copy(w, 1 + j, me, (*chip, c), src=ins[w]) for j, chip in enumerate(chips)]
            first.append(copy(w, 0, me, sibling, src=ins[w]))
        for cp in first:
            cp.start()
        passed = []
        for w in range(n):
            for j, chip in enumerate(chips):
                copy(w, 1 + j, (*chip, c), me).wait_recv()
                cp = copy(w, 4 + j, (*chip, c), sibling)
                cp.start()
                passed.append(cp)
        for w in range(n):
            copy(w, 0, sibling, me).wait_recv()
            for j, chip in enumerate(chips):
                copy(w, 4 + j, (*chip, 1 - c), me).wait_recv()
        for cp in first + passed:
            cp.wait_send()
        for cp in local:
            cp.wait()

    launch()
    return [o[...] for o in outs]


N_CHIP = 4
CHIPS = ((0, 0), (0, 1), (1, 0), (1, 1))


def _sequencer(name, n_sems, collective_id):
    return functools.partial(
        pl.kernel, mesh=plsc.ScalarSubcoreMesh(axis_name="seq", num_cores=1), name=name,
        scratch_types=tuple(pltpu.SemaphoreType.DMA(s) for s in n_sems),
        compiler_params=pltpu.CompilerParams(collective_id=collective_id))


def _handshake(peers):
    barrier = pltpu.get_barrier_semaphore()
    for peer in peers:
        pl.semaphore_signal(barrier, inc=1, device_id=peer, device_id_type=MESH)
    pl.semaphore_wait(barrier, len(peers))


def _chip_exchange_on_sequencer(parts, name):
    n = len(parts)
    hbm = pltpu.MemorySpace.HBM
    ins = [jax.new_ref(p, memory_space=hbm) for p in parts]
    outs = [jax.empty_ref(jax.ShapeDtypeStruct(p.shape, p.dtype), memory_space=hbm) for p in parts]

    @_sequencer(name, ((n, 3), (n, 3), (n,)), 2)
    def launch(send, recv, lsem):
        x, y, c = lax.axis_index("x"), lax.axis_index("y"), lax.axis_index("c")
        mine = 2 * x + y
        chips = [(1 - x, y), (x, 1 - y), (1 - x, 1 - y)]
        _handshake([(*chip, c) for chip in chips])
        local = [pltpu.make_async_copy(ins[w].at[mine], outs[w].at[mine], lsem.at[w]) for w in range(n)]
        for cp in local:
            cp.start()
        sends = []
        for w in range(n):
            for j, (px, py) in enumerate(chips):
                cp = pltpu.make_async_remote_copy(ins[w].at[2 * px + py], outs[w].at[mine],
                                                  send.at[w, j], recv.at[w, j],
                                                  device_id=(px, py, c), device_id_type=MESH)
                cp.start()
                sends.append(cp)
        for w in range(n):
            for j, (px, py) in enumerate(chips):
                pltpu.make_async_remote_copy(ins[w].at[mine], outs[w].at[2 * px + py],
                                             send.at[w, j], recv.at[w, j],
                                             device_id=(px, py, c), device_id_type=MESH).wait_recv()
        for cp in sends:
            cp.wait_send()
        for cp in local:
            cp.wait()

    launch()
    return [o[...] for o in outs]


GRAD_TILES = (dict(blocks_on="cols", tm=512, tn=1280), dict(blocks_on="cols", tm=512, tn=256),
              dict(blocks_on="cols", tm=512, tn=256), dict(blocks_on="rows", tm=256, tn=2048),
              dict(blocks_on="cols", tm=1024, tn=1024), dict(blocks_on="rows", tm=1024, tn=1024))


def _mm_tn_pair(a, b, *, blocks_on, tm, tn, name):
    t_len, m = a.shape
    n = b.shape[1]
    if blocks_on == "rows":
        rows, cols, inner = m // N_DEV, n, n // tn
        assert tm == rows
        a_spec = pl.BlockSpec((t_len, tm), lambda p, t, blk: (0, blk[p]))
        b_spec = pl.BlockSpec((t_len, tn), lambda p, t, blk: (0, t))
        out_spec = pl.BlockSpec((None, tm, tn), lambda p, t, blk: (
            jnp.maximum(p - N_CHIP, 0), 0, jnp.where(p < N_CHIP, 0, t)))
    else:
        rows, cols, inner = m, n // N_DEV, m // tm
        assert tn == cols
        a_spec = pl.BlockSpec((t_len, tm), lambda p, t, blk: (0, t))
        b_spec = pl.BlockSpec((t_len, tn), lambda p, t, blk: (0, blk[p]))
        out_spec = pl.BlockSpec((None, tm, tn), lambda p, t, blk: (
            jnp.maximum(p - N_CHIP, 0), jnp.where(p < N_CHIP, 0, t), 0))

    def body(blk_ref, a_ref, b_ref, o_ref, land, stage, send_sem, recv_sem):
        del blk_ref
        p, t = pl.program_id(0), pl.program_id(1)
        step = p * inner + t
        x, y, c = lax.axis_index("x"), lax.axis_index("y"), lax.axis_index("c")
        tile = _dot(a_ref[...], b_ref[...], TN)

        def to_sibling(slot, chip, piece):
            return pltpu.make_async_remote_copy(stage.at[slot], land.at[chip, piece], send_sem.at[slot],
                                                recv_sem.at[chip, piece],
                                                device_id=(x, y, 1 - c), device_id_type=MESH)

        @pl.when(p < N_CHIP)
        def _():
            slot = step % 2

            @pl.when(step >= 2)
            def _():
                to_sibling(slot, 0, 0).wait_send()

            stage[slot] = tile.astype(BF16)
            to_sibling(slot, p, t).start()

        @pl.when(step == N_CHIP * inner)
        def _():
            for slot in range(min(2, N_CHIP * inner)):
                to_sibling(slot, 0, 0).wait_send()

        @pl.when(p >= N_CHIP)
        def _():
            chip = p - N_CHIP
            to_sibling(0, chip, t).wait_recv()
            o_ref[...] = (tile + land[chip, t].astype(F32)).astype(BF16)

    c = lax.axis_index("c")
    order = jnp.stack([2 * ch + 1 - c for ch in range(N_CHIP)] + [2 * ch + c for ch in range(N_CHIP)])
    return pl.pallas_call(
        body, name=name,
        grid_spec=pltpu.PrefetchScalarGridSpec(
            num_scalar_prefetch=1, grid=(N_DEV, inner), in_specs=[a_spec, b_spec], out_specs=out_spec,
            scratch_shapes=[pltpu.VMEM((N_CHIP, inner, tm, tn), BF16), pltpu.VMEM((2, tm, tn), BF16),
                            pltpu.SemaphoreType.DMA((2,)), pltpu.SemaphoreType.DMA((N_CHIP, inner))]),
        out_shape=jax.ShapeDtypeStruct((N_CHIP, rows, cols), BF16),
        compiler_params=_params(("arbitrary", "arbitrary")),
    )(order.astype(jnp.int32), a, b)


def _adamw_math(g, w, m, v):
    m2 = B1 * m + (1.0 - B1) * g
    v2 = B2 * v + (1.0 - B2) * (g * g)
    delta = -LR * ((m2 / BC1) / (jnp.sqrt(v2 / BC2) + AEPS) + WD * w)
    return delta, m2, v2


def _adamw_block(ins, outs):
    p_ref, w_ref, m_ref, v_ref = ins
    g = p_ref[0].astype(F32)
    for b in range(1, N_CHIP):
        g = g + p_ref[b].astype(F32)
    delta, m2, v2 = _adamw_math(g, w_ref[...], m_ref[...], v_ref[...])
    for ref, val in zip(outs, (g, delta, m2, v2)):
        ref[...] = val


class _Rider(NamedTuple):
    inputs: tuple
    in_specs: list
    out_shape: list
    out_specs: list
    body: Callable


def _adamw_rider(parts, w, m, v):
    rows, cols = w.shape

    def rider(steps, step_of):
        rr = rows // steps
        blk = pl.BlockSpec((rr, cols), lambda *ids: (step_of(*ids[:2]), 0))
        chips = pl.BlockSpec((N_CHIP, rr, cols), lambda *ids: (0, step_of(*ids[:2]), 0))
        shape = jax.ShapeDtypeStruct((rows, cols), F32)
        return _Rider((parts, w, m, v), [chips, blk, blk, blk], [shape] * 4, [blk] * 4, _adamw_block)

    return rider


def _adamw(parts, w, m, v, *, name, after=(), tr=256):
    rows, cols = w.shape

    def body(*refs):
        _adamw_block(refs[:4], refs[4 + len(after):])

    spec = pl.BlockSpec((tr, cols), lambda i: (i, 0))
    shape = jax.ShapeDtypeStruct((rows, cols), F32)
    return pl.pallas_call(
        body, name=name, grid=(rows // tr,),
        in_specs=[pl.BlockSpec((N_CHIP, tr, cols), lambda i: (0, i, 0)), spec, spec, spec]
        + [pl.BlockSpec(memory_space=pl.ANY)] * len(after),
        out_specs=[spec] * 4, out_shape=[shape] * 4,
        compiler_params=_params(("parallel",)),
    )(parts, w, m, v, *after)


def _small_update(part, w, m, v):
    rows = part.shape[0]

    def body(p_ref, w_ref, m_ref, v_ref, g_ref, d_ref, mo_ref, vo_ref, buf, send, recv):
        me, peers = _me_and_peers()
        buf[me] = p_ref[...]
        sends = []
        for k, (dev, _) in enumerate(peers):
            cp = pltpu.make_async_remote_copy(p_ref, buf.at[me], send.at[k], recv.at[k],
                                              device_id=dev, device_id_type=MESH)
            cp.start()
            sends.append(cp)
        for k, (dev, idx) in enumerate(peers):
            pltpu.make_async_remote_copy(p_ref, buf.at[idx], send.at[k], recv.at[k],
                                         device_id=dev, device_id_type=MESH).wait_recv()
        for cp in sends:
            cp.wait_send()
        g = buf[0]
        for b in range(1, N_DEV):
            g = g + buf[b]
        delta, m2, v2 = _adamw_math(g, w_ref[...], m_ref[...], v_ref[...])
        g_ref[...] = g
        d_ref[...] = delta
        mo_ref[...] = m2
        vo_ref[...] = v2

    vm = pl.BlockSpec(memory_space=pltpu.VMEM)
    shape = jax.ShapeDtypeStruct((rows, HD), F32)
    return pl.pallas_call(
        body, name="small_params_update",
        in_specs=[vm] * 4, out_specs=[vm] * 4, out_shape=[shape] * 4,
        scratch_shapes=[pltpu.VMEM((N_DEV, rows, HD), F32),
                        pltpu.SemaphoreType.DMA((N_DEV - 1,)), pltpu.SemaphoreType.DMA((N_DEV - 1,))],
    )(part, w, m, v)


def _pack_small(norm_mix, b_gate, qa, ka, qb, kb, rpb, norm_ffn):
    gains = jnp.concatenate([qa, ka, qb, kb, jnp.zeros((4, HD), F32)], axis=0)
    rpb_pad = jnp.pad(rpb.reshape(4 * (2 * WIN_R - 1), 2 * WIN_C - 1), ((0, 4), (0, HD - (2 * WIN_C - 1))))
    return jnp.concatenate([norm_mix.reshape(16, HD), b_gate.reshape(32, HD), gains, rpb_pad,
                            norm_ffn.reshape(16, HD), jnp.zeros((8, HD), F32)], axis=0)


LOSS_ROW = 136


def _unpack_small(p):
    norm_mix = p[0:16].reshape(1, D)
    b_gate = p[16:48].reshape(1, 2 * D)
    qa, ka, qb, kb = (p[48 + i:49 + i] for i in range(4))
    rpb = p[56:116, :2 * WIN_C - 1].reshape(1, 4, 2 * WIN_R - 1, 2 * WIN_C - 1)
    norm_ffn = p[120:136].reshape(1, D)
    return norm_mix, b_gate, qa, ka, qb, kb, rpb, norm_ffn


def kernel(x, norm_mix, w_in, b_gate, q_norm_a, k_norm_a, q_norm_b, k_norm_b, rpb_b, w_proj_a, w_proj_b, w_out, norm_ffn, w_up, w_down, loss_target, m_norm_mix, m_w_in, m_b_gate, m_q_norm_a, m_k_norm_a, m_q_norm_b, m_k_norm_b, m_rpb_b, m_w_proj_a, m_w_proj_b, m_w_out, m_norm_ffn, m_w_up, m_w_down, v_norm_mix, v_w_in, v_b_gate, v_q_norm_a, v_k_norm_a, v_q_norm_b, v_k_norm_b, v_rpb_b, v_w_proj_a, v_w_proj_b, v_w_out, v_norm_ffn, v_w_up, v_w_down):
    big_w = (w_in[0], w_proj_a[0], w_proj_b[0], w_out[0], w_up[0], w_down[0])
    big_m = (m_w_in[0], m_w_proj_a[0], m_w_proj_b[0], m_w_out[0], m_w_up[0], m_w_down[0])
    big_v = (v_w_in[0], v_w_proj_a[0], v_w_proj_b[0], v_w_out[0], v_w_up[0], v_w_down[0])
    names = ("w_in", "w_proj_a", "w_proj_b", "w_out", "w_up", "w_down")

    shards = [_cast_bf16(w) for w in big_w[:5]]
    g_in, = _gather_on_sequencer(shards[0:1], "gather_w_in")
    g_pa, g_pb, g_out, g_up = _gather_on_sequencer(shards[1:5], "gather_w_mix_up")
    g_down = [_gather_on_sequencer([_cast_bf16(big_w[5], part=h, parts=2)], f"gather_w_down_{h}")[0]
              .reshape(1, D_FF, D // 2) for h in range(2)]
    small_w = _pack_small(norm_mix, b_gate, q_norm_a, k_norm_a, q_norm_b, k_norm_b, rpb_b, norm_ffn)
    small_m = _pack_small(m_norm_mix, m_b_gate, m_q_norm_a, m_k_norm_a, m_q_norm_b, m_k_norm_b, m_rpb_b, m_norm_ffn)
    small_v = _pack_small(v_norm_mix, v_b_gate, v_q_norm_a, v_k_norm_a, v_q_norm_b, v_k_norm_b, v_rpb_b, v_norm_ffn)

    upd = [None] * 6
    in_flight = {}

    def weight_grads(tag, operands):
        sums = {i: _mm_tn_pair(a, b, name=f"grad_{names[i]}", **GRAD_TILES[i]) for i, (a, b) in operands.items()}
        new = list(sums.values())
        in_flight.update(zip(sums, _chip_exchange_on_sequencer(new, f"chip_exchange_{tag}")))
        return new

    def riders(name):
        i = {"ffn_up_bwd": 5, "proj_bwd": 4}.get(name)
        if i is None:
            return None
        return (_adamw_rider(in_flight.pop(i), big_w[i], big_m[i], big_v[i]),
                functools.partial(upd.__setitem__, i))

    loss, grad_x, small_g = _local_step(
        x[0], loss_target[0], norm_mix, b_gate, small_w[48:56], small_w[56:120], norm_ffn,
        g_in, g_pa, g_pb, g_out.reshape(D, D), g_up, g_down, weight_grads, riders)

    g_norm_mix, g_b, g_gains, g_rpb, g_norm_ffn = small_g
    small_part = jnp.concatenate([g_norm_mix.reshape(16, HD), g_b.reshape(32, HD),
                                  g_gains, g_rpb, g_norm_ffn.reshape(16, HD),
                                  jnp.pad(loss, ((0, 7), (0, HD - 1)))], axis=0)
    slabs = _small_update(small_part, small_w, small_m, small_v)
    total = slabs[0][LOSS_ROW, 0]
    s_g, s_d, s_m, s_v = (_unpack_small(t) for t in slabs)

    last = grad_x
    for i, r in in_flight.items():
        upd[i] = _adamw(r, big_w[i], big_m[i], big_v[i], name=f"adamw_{names[i]}", after=[last])
        last = upd[i][0]
    b_g, b_d, b_m, b_v = ([u[j][None] for u in upd] for j in range(4))

    def order(small, big):
        nm, bg, qa, ka, qb, kb, rpb, nf = small
        w_in_, pa_, pb_, out_, up_, down_ = big
        return (nm, w_in_, bg, qa, ka, qb, kb, rpb, pa_, pb_, out_, nf, up_, down_)

    return (total, grad_x[None], *order(s_g, b_g), *order(s_d, b_d), *order(s_m, b_m), *order(s_v, b_v))
```

```python
import functools
from typing import Callable, NamedTuple

import jax
import jax.numpy as jnp
import numpy as np
from jax import lax
from jax.experimental import pallas as pl
from jax.experimental.pallas import tpu as pltpu
from jax.experimental.pallas import tpu_sc as plsc

F32 = jnp.float32
BF16 = jnp.bfloat16

N_DEV = 8
S = 2048
D = 2048
HD = 128
NH = 16
NH_A = 12
QKV = NH * HD
D_IN = 3 * QKV + 2 * D
D_BR = 512
D_FF = 4 * D
GRID_W = 64
ROWS = S // GRID_W
WIN_R = 8
WIN_C = 16
EPS = 1e-6
NEG = -1e30
SCALE = HD ** -0.5
ROPE_THETA = 10000.0
DILATIONS = (1, 4, 16)
HALF_A = 64
QB = 128

LR, B1, B2, AEPS, WD, STEP = 0.001, 0.9, 0.999, 1e-08, 0.01, 10
BC1 = 1.0 - B1 ** STEP
BC2 = 1.0 - B2 ** STEP

VMEM_LIMIT = 56 * 1024 * 1024
MESH = pl.DeviceIdType.MESH

NN = (((1,), (0,)), ((), ()))
NT = (((1,), (1,)), ((), ()))
TN = (((0,), (0,)), ((), ()))


def _params(sem):
    return pltpu.CompilerParams(dimension_semantics=sem, vmem_limit_bytes=VMEM_LIMIT)


def _matmul(a, b, *, product, grid, a_spec, b_spec, epi, out_shape, out_specs, name,
            extra=(), extra_specs=(), after=(), carried=False, rider=None, into=()):
    n_extra = len(extra)
    single = not isinstance(out_shape, (list, tuple))
    out_shape = [out_shape] if single else list(out_shape)
    out_specs = [out_specs] if single else list(out_specs)
    ride = rider(grid[0] * grid[1], lambda j, i: j * grid[1] + i) if rider else None
    r_in = list(ride.inputs) if ride else []
    n_main = len(out_shape)

    def body(a_ref, b_ref, *rest):
        n_in = n_extra + len(after) + len(r_in)
        ins, outs = rest[:n_in], rest[n_in + len(into):]
        epi(product(a_ref, b_ref, ins[:n_extra]), ins[:n_extra], outs[:n_main])
        if ride:
            ride.body(ins[n_extra + len(after):], outs[n_main:])

    res = pl.pallas_call(
        body, name=name, grid=grid,
        in_specs=[a_spec, b_spec, *extra_specs, *[pl.BlockSpec(memory_space=pl.ANY)] * len(after),
                  *(ride.in_specs if ride else []), *[pl.BlockSpec(memory_space=pl.ANY)] * len(into)],
        out_specs=out_specs + (ride.out_specs if ride else []),
        out_shape=out_shape + (ride.out_shape if ride else []),
        input_output_aliases={2 + n_extra + len(after) + len(r_in) + k: k for k in range(len(into))},
        compiler_params=_params(("arbitrary", "arbitrary") if carried else ("parallel", "parallel")),
    )(a, b, *extra, *after, *r_in, *into)
    main = res[0] if single else res[:n_main]
    return (main, res[n_main:]) if ride else main


def _dot(x, y, dims):
    return lax.dot_general(x, y, dims, preferred_element_type=F32)


def _epi_store(acc, ex, outs):
    outs[0][...] = acc.astype(outs[0].dtype)


def _epi_residual(acc, ex, outs):
    outs[0][...] = acc + ex[0][...]


def _mm_nn(a, b3, *, tm, tn, name, out_dtypes=(F32,), epi=_epi_store, extra=(), total=False,
           col0=0, width=None, into=()):
    m, kdim = a.shape
    g, _, ng = b3.shape
    n = g * ng
    c0 = col0 // tn
    if tn <= ng:
        npg = ng // tn
        b_spec = pl.BlockSpec((None, kdim, tn), lambda j, i: (j // npg, 0, j % npg))

        def product(a_ref, b_ref, ex):
            return _dot(a_ref[...], b_ref[...], NN)
    else:
        gb = tn // ng
        b_spec = pl.BlockSpec((gb, kdim, ng), lambda j, i: (j, 0, 0))

        def product(a_ref, b_ref, ex):
            return jnp.concatenate([_dot(a_ref[...], b_ref[q], NN) for q in range(gb)], axis=1)

    tile = pl.BlockSpec((tm, tn), lambda j, i: (i, j + c0))
    shapes = [jax.ShapeDtypeStruct((m, width or n), dt) for dt in out_dtypes]
    specs = [tile] * len(shapes)
    if total:
        shapes.append(jax.ShapeDtypeStruct((1, 1), F32))
        specs.append(pl.BlockSpec((1, 1), lambda j, i: (0, 0)))
    single = len(shapes) == 1
    return _matmul(
        a, b3, product=product, grid=(n // tn, m // tm), epi=epi, name=name, carried=total, into=into,
        a_spec=pl.BlockSpec((tm, kdim), lambda j, i: (i, 0)), b_spec=b_spec,
        extra=extra, extra_specs=[tile] * len(extra),
        out_shape=shapes[0] if single else shapes, out_specs=specs[0] if single else specs)


def _mm_nt(a, b3, *, tm, tn, name, out_dtype=F32, epi=_epi_store, extra=(), after=(), rider=None, more_b=()):
    m, kdim = a.shape
    _, n, _ = b3.shape
    n_b = len(more_b)

    def product(a_ref, b_ref, ex):
        acc, k0 = None, 0
        for ref in (b_ref, *ex[:n_b]):
            for q in range(ref.shape[0]):
                part = _dot(a_ref[:, k0:k0 + ref.shape[2]], ref[q], NT)
                acc = part if acc is None else acc + part
                k0 += ref.shape[2]
        return acc

    def write(acc, ex, outs):
        epi(acc, ex[n_b:], outs)

    def w_spec(w):
        return pl.BlockSpec((w.shape[0], tn, w.shape[2]), lambda j, i: (0, j, 0))

    tile = pl.BlockSpec((tm, tn), lambda j, i: (i, j))
    return _matmul(
        a, b3, product=product, grid=(n // tn, m // tm), epi=write, name=name,
        a_spec=pl.BlockSpec((tm, kdim), lambda j, i: (i, 0)), b_spec=w_spec(b3),
        extra=(*more_b, *extra), extra_specs=[w_spec(w) for w in more_b] + [tile] * len(extra),
        after=after, rider=rider,
        out_shape=jax.ShapeDtypeStruct((m, n), out_dtype), out_specs=tile)


def _mm_tn(a, b, *, tm, tn, name, groups=1, out_dtype=BF16):
    t, m = a.shape
    _, n = b.shape
    ng = n // groups
    if tn <= ng:
        npg = ng // tn
        out_spec = pl.BlockSpec((None, tm, tn), lambda j, i: (j // npg, i, j % npg))
        epi = _epi_store

        def product(a_ref, b_ref, ex):
            return _dot(a_ref[...], b_ref[...], TN)
    else:
        gb = tn // ng
        out_spec = pl.BlockSpec((gb, tm, ng), lambda j, i: (j, i, 0))

        def product(a_ref, b_ref, ex):
            return [_dot(a_ref[...], b_ref[:, q * ng:(q + 1) * ng], TN) for q in range(gb)]

        def epi(parts, ex, outs):
            for q, part in enumerate(parts):
                outs[0][q] = part.astype(out_dtype)

    return _matmul(
        a, b, product=product, grid=(n // tn, m // tm), epi=epi, name=name,
        a_spec=pl.BlockSpec((t, tm), lambda j, i: (0, i)),
        b_spec=pl.BlockSpec((t, tn), lambda j, i: (0, j)),
        out_shape=jax.ShapeDtypeStruct((groups, m, ng), out_dtype), out_specs=out_spec)


def _rms_fwd(x, g, *, name, tr=256):
    def body(x_ref, g_ref, y_ref, r_ref):
        xv = x_ref[...]
        r = lax.rsqrt(jnp.mean(xv * xv, axis=-1, keepdims=True) + EPS)
        y_ref[...] = (xv * r * g_ref[...]).astype(BF16)
        r_ref[...] = r

    row = pl.BlockSpec((tr, D), lambda i: (i, 0))
    return pl.pallas_call(
        body, name=name, grid=(S // tr,),
        in_specs=[row, pl.BlockSpec((1, D), lambda i: (0, 0))],
        out_specs=[row, pl.BlockSpec((tr, 1), lambda i: (i, 0))],
        out_shape=[jax.ShapeDtypeStruct((S, D), BF16), jax.ShapeDtypeStruct((S, 1), F32)],
        compiler_params=_params(("parallel",)),
    )(x, g)


def _rms_bwd(dy, x, rstd, g, resid, *, name, bf16_copy, tr=256):
    def body(dy_ref, x_ref, r_ref, g_ref, res_ref, dx_ref, *rest):
        dg_ref = rest[-1]
        r = r_ref[...]
        xh = x_ref[...] * r
        dyv = dy_ref[...]
        t = dyv * g_ref[...]
        dx = r * (t - xh * jnp.mean(t * xh, axis=-1, keepdims=True)) + res_ref[...]
        dx_ref[...] = dx
        if bf16_copy:
            rest[0][...] = dx.astype(BF16)
        part = jnp.sum(dyv * xh, axis=0, keepdims=True)

        @pl.when(pl.program_id(0) == 0)
        def _():
            dg_ref[...] = part

        @pl.when(pl.program_id(0) > 0)
        def _():
            dg_ref[...] += part

    row = pl.BlockSpec((tr, D), lambda i: (i, 0))
    vec = pl.BlockSpec((1, D), lambda i: (0, 0))
    return pl.pallas_call(
        body, name=name, grid=(S // tr,),
        in_specs=[row, row, pl.BlockSpec((tr, 1), lambda i: (i, 0)), vec, row],
        out_specs=[row] + [row] * bf16_copy + [vec],
        out_shape=[jax.ShapeDtypeStruct((S, D), F32)] + [jax.ShapeDtypeStruct((S, D), BF16)] * bf16_copy
        + [jax.ShapeDtypeStruct((1, D), F32)],
        compiler_params=_params(("arbitrary",)),
    )(dy, x, rstd, g, resid)


def _rope_tables():
    pos = np.arange(S, dtype=np.float32)
    inv = (ROPE_THETA ** (-np.arange(0, HD, 2, dtype=np.float32) / HD)).astype(np.float32)
    ang = pos[:, None] * inv[None, :]
    cos, sin = np.cos(ang), np.sin(ang)
    return (jnp.asarray(np.concatenate([cos, cos], axis=-1), F32),
            jnp.asarray(np.concatenate([-sin, sin], axis=-1), F32))


def _swap_halves(t):
    return pltpu.roll(t, HD // 2, axis=1)


TOK = 256


def _lane_block_spec(d, last=HD):
    return pl.BlockSpec((4, TOK // d, d * last), lambda i: (0, i, 0))


def _to_lane_blocks(dst, head, val, d, scr, dtype):
    w = val.shape[1]
    if d == 1:
        dst[head] = val.astype(dtype)
        return
    scr[...] = val
    for r in range(d):
        dst[head, :, r * w:(r + 1) * w] = scr[pl.ds(r, TOK // d, stride=d), :].astype(dtype)


def _from_lane_blocks(src, head, d, w, scr):
    if d == 1:
        return src[head].astype(F32)
    for r in range(d):
        scr[pl.ds(r, TOK // d, stride=d), :] = src[head, :, r * w:(r + 1) * w].astype(F32)
    return scr[...]


def _qk_prep(proj, gains, cos2, sin2):
    def body(q_ref, k_ref, v_ref, g_ref, c_ref, s_ref, *rest):
        outs, scr = rest[:-1], rest[-1]
        cos, sin = c_ref[...], s_ref[...]
        for which, (src, row_a, row_b) in enumerate(((q_ref, 0, 2), (k_ref, 1, 3), (v_ref, None, None))):
            for h in range(NH):
                y = src[:, h * HD:(h + 1) * HD]
                if row_a is not None:
                    y = y * lax.rsqrt(jnp.mean(y * y, axis=-1, keepdims=True) + EPS)
                    if h < NH_A:
                        y = y * g_ref[row_a:row_a + 1, :]
                        y = y * cos + _swap_halves(y) * sin
                    else:
                        y = y * g_ref[row_b:row_b + 1, :]
                if h < NH_A:
                    gi = h // 4
                    _to_lane_blocks(outs[3 * gi + which], h % 4, y, DILATIONS[gi], scr, BF16)
                else:
                    hb = h - NH_A
                    outs[9 + which][:, hb * HD:(hb + 1) * HD] = y.astype(BF16)

    def blk(c):
        return pl.BlockSpec((TOK, QKV), lambda i: (i, c))
    tab = pl.BlockSpec((TOK, HD), lambda i: (i, 0))
    out_specs, out_shape = [], []
    for d in DILATIONS:
        out_specs += [_lane_block_spec(d)] * 3
        out_shape += [jax.ShapeDtypeStruct((4, S // d, d * HD), BF16)] * 3
    out_specs += [pl.BlockSpec((TOK, D_BR), lambda i: (i, 0))] * 3
    out_shape += [jax.ShapeDtypeStruct((S, D_BR), BF16)] * 3
    outs = pl.pallas_call(
        body, name="qk_prep", grid=(S // TOK,),
        in_specs=[blk(0), blk(1), blk(2), pl.BlockSpec((8, HD), lambda i: (0, 0)), tab, tab],
        out_specs=out_specs, out_shape=out_shape,
        scratch_shapes=[pltpu.VMEM((TOK, HD), F32)],
        compiler_params=_params(("parallel",)),
    )(proj, proj, proj, gains, cos2, sin2)
    return [tuple(outs[3 * gi:3 * gi + 3]) for gi in range(3)], tuple(outs[9:12])


def _qk_prep_bwd(dproj, proj, gains, cos2, sin2, grads_a, grads_b):
    def body(dp_in, q_ref, k_ref, g_ref, c_ref, s_ref, *rest):
        grads, (dp_out, dg_ref, scr) = rest[:12], rest[12:]
        del dp_in
        cos, sin = c_ref[...], s_ref[...]

        def grad_of(which, h):
            if h < NH_A:
                gi = h // 4
                return _from_lane_blocks(grads[3 * gi + which], h % 4, DILATIONS[gi], HD, scr)
            hb = h - NH_A
            return grads[9 + which][:, hb * HD:(hb + 1) * HD]

        dg_rows = []
        for which, (src, base, row_a, row_b) in enumerate(((q_ref, 0, 0, 2), (k_ref, QKV, 1, 3))):
            dg_a = jnp.zeros((1, HD), F32)
            dg_b = jnp.zeros((1, HD), F32)
            for h in range(NH):
                t = src[:, h * HD:(h + 1) * HD]
                dy = grad_of(which, h)
                r = lax.rsqrt(jnp.mean(t * t, axis=-1, keepdims=True) + EPS)
                xh = t * r
                if h < NH_A:
                    dy = dy * cos - _swap_halves(dy) * sin
                    gain = g_ref[row_a:row_a + 1, :]
                    dg_a = dg_a + jnp.sum(dy * xh, axis=0, keepdims=True)
                else:
                    gain = g_ref[row_b:row_b + 1, :]
                    dg_b = dg_b + jnp.sum(dy * xh, axis=0, keepdims=True)
                u = dy * gain
                dx = r * (u - xh * jnp.mean(u * xh, axis=-1, keepdims=True))
                dp_out[:, base + h * HD:base + (h + 1) * HD] = dx.astype(BF16)
            dg_rows += [(row_a, dg_a), (row_b, dg_b)]
        for h in range(NH):
            dp_out[:, 2 * QKV + h * HD:2 * QKV + (h + 1) * HD] = grad_of(2, h).astype(BF16)

        @pl.when(pl.program_id(0) == 0)
        def _():
            dg_ref[...] = jnp.zeros((8, HD), F32)

        for row, val in dg_rows:
            dg_ref[row:row + 1, :] += val

    def blk(c):
        return pl.BlockSpec((TOK, QKV), lambda i: (i, c))
    tab = pl.BlockSpec((TOK, HD), lambda i: (i, 0))
    gain_spec = pl.BlockSpec((8, HD), lambda i: (0, 0))
    grad_specs = [s for d in DILATIONS for s in [_lane_block_spec(d)] * 3]
    grad_specs += [pl.BlockSpec((TOK, D_BR), lambda i: (i, 0))] * 3
    return pl.pallas_call(
        body, name="qk_prep_bwd", grid=(S // TOK,),
        in_specs=[pl.BlockSpec(memory_space=pl.ANY), blk(0), blk(1), gain_spec, tab, tab] + grad_specs,
        out_specs=[pl.BlockSpec((TOK, 3 * QKV), lambda i: (i, 0)), gain_spec],
        out_shape=[jax.ShapeDtypeStruct((S, D_IN), BF16), jax.ShapeDtypeStruct((8, HD), F32)],
        input_output_aliases={0: 0},
        scratch_shapes=[pltpu.VMEM((TOK, HD), F32)],
        compiler_params=_params(("arbitrary",)),
    )(dproj, proj, proj, gains, cos2, sin2, *[g for grp in grads_a for g in grp], *grads_b)


def _mix_fwd(oa, ob, w_pa, w_pb, proj, b_gate, *, tr=256):
    def body(oa_ref, ob_ref, pa_ref, pb_ref, la_ref, lb_ref, ba_ref, bb_ref, mix_ref, ya_ref, yb_ref):
        ya = jnp.concatenate([_dot(oa_ref[...], pa_ref[q], NN) for q in range(N_DEV)], axis=1)
        yb = jnp.concatenate([_dot(ob_ref[...], pb_ref[q], NN) for q in range(N_DEV)], axis=1)
        ga = jax.nn.sigmoid(la_ref[...] + ba_ref[...])
        gb = jax.nn.sigmoid(lb_ref[...] + bb_ref[...])
        mix_ref[...] = (ga * ya + gb * yb).astype(BF16)
        ya_ref[...] = ya.astype(BF16)
        yb_ref[...] = yb.astype(BF16)

    row = pl.BlockSpec((tr, D), lambda i: (i, 0))
    branch = pl.BlockSpec((tr, D_BR), lambda i: (i, 0))
    whole = pl.BlockSpec((N_DEV, D_BR, D // N_DEV), lambda i: (0, 0, 0))
    return pl.pallas_call(
        body, name="mix_fwd", grid=(S // tr,),
        in_specs=[branch, branch, whole, whole,
                  pl.BlockSpec((tr, D), lambda i: (i, 3)), pl.BlockSpec((tr, D), lambda i: (i, 4)),
                  pl.BlockSpec((1, D), lambda i: (0, 0)), pl.BlockSpec((1, D), lambda i: (0, 1))],
        out_specs=[row, row, row], out_shape=[jax.ShapeDtypeStruct((S, D), BF16)] * 3,
        compiler_params=_params(("parallel",)),
    )(oa, ob, w_pa, w_pb, proj, proj, b_gate, b_gate)


def _mix_bwd(dh1b, w_out, proj, b_gate, ya, yb, *, tr=256):
    def body(dh_ref, w_ref, la_ref, lb_ref, b_ref, ya_ref, yb_ref, dya_ref, dyb_ref, dp_ref, db_ref):
        dm = _dot(dh_ref[...], w_ref[...], NT)
        parts = []
        for l_ref, y_ref, dy_ref, lo in ((la_ref, ya_ref, dya_ref, 0), (lb_ref, yb_ref, dyb_ref, D)):
            g = jax.nn.sigmoid(l_ref[...] + b_ref[:, lo:lo + D])
            dy_ref[...] = (dm * g).astype(BF16)
            dl = dm * y_ref[...].astype(F32) * g * (1.0 - g)
            dp_ref[:, lo:lo + D] = dl.astype(BF16)
            parts.append(jnp.sum(dl, axis=0, keepdims=True))
        part = jnp.concatenate(parts, axis=1)

        @pl.when(pl.program_id(0) == 0)
        def _():
            db_ref[...] = part

        @pl.when(pl.program_id(0) > 0)
        def _():
            db_ref[...] += part

    row = pl.BlockSpec((tr, D), lambda i: (i, 0))
    vec = pl.BlockSpec((1, 2 * D), lambda i: (0, 0))
    gate_cols = pl.BlockSpec((pl.Element(tr), pl.Element(2 * D)), lambda i: (i * tr, 3 * QKV))
    return pl.pallas_call(
        body, name="mix_bwd", grid=(S // tr,),
        in_specs=[row, pl.BlockSpec((D, D), lambda i: (0, 0)),
                  pl.BlockSpec((tr, D), lambda i: (i, 3)), pl.BlockSpec((tr, D), lambda i: (i, 4)), vec, row, row],
        out_specs=[row, row, gate_cols, vec],
        out_shape=[jax.ShapeDtypeStruct((S, D), BF16), jax.ShapeDtypeStruct((S, D), BF16),
                   jax.ShapeDtypeStruct((S, D_IN), BF16), jax.ShapeDtypeStruct((1, 2 * D), F32)],
        compiler_params=_params(("arbitrary",)),
    )(dh1b, w_out, proj, proj, b_gate, ya, yb)


def _band_blocks(m_len):
    wk = min(m_len, QB + 2 * QB)
    return [(qb * QB, min(max(qb * QB - QB, 0), m_len - wk), wk) for qb in range(m_len // QB)]


def _band_scores(q, kw, q0, k0, wk):
    s = _dot(q, kw, NT) * SCALE
    qpos = q0 + lax.broadcasted_iota(jnp.int32, (QB, 1), 0)
    kpos = k0 + lax.broadcasted_iota(jnp.int32, (1, wk), 1)
    return jnp.where(jnp.abs(kpos - qpos) <= HALF_A, s, NEG)


def _attn_a_fwd(q, k, v, gi):
    d = DILATIONS[gi]
    m_len = S // d

    def body(q_ref, k_ref, v_ref, o_ref, lse_ref):
        for r in range(d):
            lanes = slice(r * HD, (r + 1) * HD)
            for q0, k0, wk in _band_blocks(m_len):
                s = _band_scores(q_ref[q0:q0 + QB, lanes], k_ref[k0:k0 + wk, lanes], q0, k0, wk)
                m = jnp.max(s, axis=-1, keepdims=True)
                p = jnp.exp(s - m)
                l = jnp.sum(p, axis=-1, keepdims=True)
                o_ref[q0:q0 + QB, lanes] = _dot(p.astype(BF16), v_ref[k0:k0 + wk, lanes], NN) / l
                lse_ref[q0:q0 + QB, r:r + 1] = m + jnp.log(l)

    head = pl.BlockSpec((None, m_len, d * HD), lambda h: (h, 0, 0))
    stat = pl.BlockSpec((None, m_len, d), lambda h: (h, 0, 0))
    return pl.pallas_call(
        body, name=f"attn_a_fwd_{gi}", grid=(4,),
        in_specs=[head, head, head], out_specs=[head, stat],
        out_shape=[jax.ShapeDtypeStruct((4, m_len, d * HD), F32), jax.ShapeDtypeStruct((4, m_len, d), F32)],
        compiler_params=_params(("parallel",)),
    )(q, k, v)


def _combine_a(os, lses):
    def body(o0, o1, o2, l0, l1, l2, oa_ref, lse_ref, scr, scr1):
        for h in range(4):
            o = [_from_lane_blocks(ref, h, d, HD, scr) for ref, d in zip((o0, o1, o2), DILATIONS)]
            a, b, c = (_from_lane_blocks(ref, h, d, 1, scr1) for ref, d in zip((l0, l1, l2), DILATIONS))
            m = jnp.maximum(jnp.maximum(a, b), c)
            wa, wb, wc = jnp.exp(a - m), jnp.exp(b - m), jnp.exp(c - m)
            tot = wa + wb + wc
            oa_ref[:, h * HD:(h + 1) * HD] = ((wa * o[0] + wb * o[1] + wc * o[2]) / tot).astype(BF16)
            lse_ref[h] = m + jnp.log(tot)

    return pl.pallas_call(
        body, name="combine_a", grid=(S // TOK,),
        in_specs=[_lane_block_spec(d) for d in DILATIONS] + [_lane_block_spec(d, 1) for d in DILATIONS],
        out_specs=[pl.BlockSpec((TOK, D_BR), lambda i: (i, 0)), pl.BlockSpec((4, TOK, 1), lambda i: (0, i, 0))],
        out_shape=[jax.ShapeDtypeStruct((S, D_BR), BF16), jax.ShapeDtypeStruct((4, S, 1), F32)],
        scratch_shapes=[pltpu.VMEM((TOK, HD), F32), pltpu.VMEM((TOK, 1), F32)],
        compiler_params=_params(("parallel",)),
    )(*os, *lses)


def _proj_a_bwd(dya, w_pa, oa, lse):
    kg = D // N_DEV

    def body(dy_ref, w_ref, o_ref, l_ref, *rest):
        outs, (scr, scr1) = rest[:9], rest[9:]
        doa = _dot(dy_ref[:, 0:kg], w_ref[0], NT)
        for q in range(1, N_DEV):
            doa = doa + _dot(dy_ref[:, q * kg:(q + 1) * kg], w_ref[q], NT)
        for h in range(4):
            do = doa[:, h * HD:(h + 1) * HD]
            dsum = jnp.sum(do * o_ref[:, h * HD:(h + 1) * HD].astype(F32), axis=-1, keepdims=True)
            for gi, d in enumerate(DILATIONS):
                _to_lane_blocks(outs[3 * gi], h, do, d, scr, BF16)
                _to_lane_blocks(outs[3 * gi + 1], h, l_ref[h], d, scr1, F32)
                _to_lane_blocks(outs[3 * gi + 2], h, dsum, d, scr1, F32)

    row = pl.BlockSpec((TOK, D_BR), lambda i: (i, 0))
    out_specs, out_shape = [], []
    for d in DILATIONS:
        out_specs += [_lane_block_spec(d), _lane_block_spec(d, 1), _lane_block_spec(d, 1)]
        out_shape += [jax.ShapeDtypeStruct((4, S // d, d * HD), BF16)] + [jax.ShapeDtypeStruct((4, S // d, d), F32)] * 2
    outs = pl.pallas_call(
        body, name="proj_a_bwd", grid=(S // TOK,),
        in_specs=[pl.BlockSpec((TOK, D), lambda i: (i, 0)),
                  pl.BlockSpec((N_DEV, D_BR, kg), lambda i: (0, 0, 0)),
                  row, pl.BlockSpec((4, TOK, 1), lambda i: (0, i, 0))],
        out_specs=out_specs, out_shape=out_shape,
        scratch_shapes=[pltpu.VMEM((TOK, HD), F32), pltpu.VMEM((TOK, 1), F32)],
        compiler_params=_params(("parallel",)),
    )(dya, w_pa, oa, lse)
    return [tuple(outs[3 * gi:3 * gi + 3]) for gi in range(3)]


def _attn_a_bwd(q, k, v, do, lse, dsum, gi):
    d = DILATIONS[gi]
    m_len = S // d

    def body(q_ref, k_ref, v_ref, do_ref, lse_ref, dsum_ref, dq_ref, dk_ref, dv_ref):
        dk_ref[...] = jnp.zeros((m_len, d * HD), F32)
        dv_ref[...] = jnp.zeros((m_len, d * HD), F32)
        for r in range(d):
            lanes = slice(r * HD, (r + 1) * HD)
            for q0, k0, wk in _band_blocks(m_len):
                rows, keys = slice(q0, q0 + QB), slice(k0, k0 + wk)
                qv, kw, vw, dov = q_ref[rows, lanes], k_ref[keys, lanes], v_ref[keys, lanes], do_ref[rows, lanes]
                p = jnp.exp(_band_scores(qv, kw, q0, k0, wk) - lse_ref[rows, r:r + 1])
                ds = (p * (_dot(dov, vw, NT) - dsum_ref[rows, r:r + 1]) * SCALE).astype(BF16)
                dq_ref[rows, lanes] = _dot(ds, kw, NN)
                dk_ref[keys, lanes] += _dot(ds, qv, TN)
                dv_ref[keys, lanes] += _dot(p.astype(BF16), dov, TN)

    head = pl.BlockSpec((None, m_len, d * HD), lambda h: (h, 0, 0))
    stat = pl.BlockSpec((None, m_len, d), lambda h: (h, 0, 0))
    shape = jax.ShapeDtypeStruct((4, m_len, d * HD), F32)
    return pl.pallas_call(
        body, name=f"attn_a_bwd_{gi}", grid=(4,),
        in_specs=[head, head, head, head, stat, stat], out_specs=[head, head, head],
        out_shape=[shape, shape, shape],
        compiler_params=_params(("parallel",)),
    )(q, k, v, do, lse, dsum)


KEYS_B = WIN_R * GRID_W
N_OFF = WIN_R


def _bias_constants():
    q = np.arange(GRID_W)[:, None]
    kc = np.arange(GRID_W)[None, :]
    dc = np.clip(kc - q, -(WIN_C - 1), WIN_C - 1) + (WIN_C - 1)
    expand = np.zeros((HD, GRID_W * GRID_W), np.float32)
    expand[dc.reshape(-1), np.arange(GRID_W * GRID_W)] = 1.0
    cs = np.clip(q - WIN_C // 2, 0, GRID_W - WIN_C)
    keep = ((kc >= cs) & (kc < cs + WIN_C)).reshape(1, -1).astype(np.float32)
    sel = np.zeros((64, 4 * N_OFF * WIN_R), np.float32)
    for h in range(4):
        for off in range(N_OFF):
            for j in range(WIN_R):
                sel[h * (2 * WIN_R - 1) + off + j, (h * N_OFF + off) * WIN_R + j] = 1.0
    return jnp.asarray(expand), jnp.asarray(keep), jnp.asarray(sel)


def _bias_expand(rpb_pad, expand, keep, sel):
    def body(r_ref, e_ref, k_ref, s_ref, o_ref):
        t = lax.dot_general(r_ref[...], e_ref[...], NN, precision=lax.Precision.HIGHEST,
                            preferred_element_type=F32)
        rows = lax.dot_general(s_ref[...], t, TN, precision=lax.Precision.HIGHEST,
                               preferred_element_type=F32)
        o_ref[...] = jnp.where(k_ref[...] > 0.5, rows, NEG)

    return pl.pallas_call(
        body, name="bias_expand",
        out_shape=jax.ShapeDtypeStruct((4 * N_OFF * WIN_R, GRID_W * GRID_W), F32),
        compiler_params=pltpu.CompilerParams(vmem_limit_bytes=VMEM_LIMIT),
    )(rpb_pad, expand, keep, sel)


def _bias_reduce(dbias_tab):
    lane0 = GRID_W - WIN_C
    flip = np.zeros((GRID_W, GRID_W), np.float32)
    flip[np.arange(GRID_W), GRID_W - 1 - np.arange(GRID_W)] = 1.0
    place = np.zeros((WIN_R, 64, 4 * N_OFF), np.float32)
    for j in range(WIN_R):
        for h in range(4):
            for off in range(N_OFF):
                place[j, h * (2 * WIN_R - 1) + off + j, h * N_OFF + off] = 1.0

    def exact(x, y):
        return lax.dot_general(x, y, NN, precision=lax.Precision.HIGHEST, preferred_element_type=F32)

    def body(x_ref, flip_ref, place_ref, o_ref, z_ref):
        for h in range(4):
            for off in range(N_OFF):
                lined_up = pltpu.roll(exact(flip_ref[...], x_ref[h, off]), 0, axis=1, stride=1, stride_axis=0)
                z_ref[h * N_OFF + off:h * N_OFF + off + 1, :] = jnp.sum(lined_up, axis=0, keepdims=True)
        acc = jnp.zeros((64, HD), F32)
        for j in range(WIN_R):
            at_zero = pltpu.roll(z_ref[...], (KEYS_B - (j * GRID_W + lane0)) % KEYS_B, axis=1)[:, :HD]
            acc = acc + exact(place_ref[j], at_zero)
        lane = lax.broadcasted_iota(jnp.int32, (64, HD), 1)
        o_ref[...] = jnp.where(lane < 2 * WIN_C - 1, acc, 0.0)

    return pl.pallas_call(
        body, name="bias_reduce", out_shape=jax.ShapeDtypeStruct((64, HD), F32),
        scratch_shapes=[pltpu.VMEM((4 * N_OFF, KEYS_B), F32)],
        compiler_params=pltpu.CompilerParams(vmem_limit_bytes=VMEM_LIMIT),
    )(dbias_tab, jnp.asarray(flip), jnp.asarray(place))


def _rows_to_tab(rows):
    t = rows.reshape(4, N_OFF, WIN_R, GRID_W, GRID_W)
    return t.transpose(0, 1, 3, 2, 4).reshape(4, N_OFF, GRID_W, KEYS_B)


def _row_window(r):
    r0 = jnp.clip(r - WIN_R // 2, 0, ROWS - WIN_R)
    off = r0 + (WIN_R - 1) - r
    return pl.multiple_of(r * GRID_W, GRID_W), pl.multiple_of(r0 * GRID_W, GRID_W), off


def _attn_b_fwd(qn, kn, vb, bias_tab):
    def body(q_ref, k_ref, v_ref, b_ref, o_ref, lse_ref):
        def row(r, carry):
            qs, ks, off = _row_window(r)
            q = q_ref[pl.ds(qs, GRID_W), :]
            s = lax.dot_general(q, k_ref[pl.ds(ks, KEYS_B), :], NT, preferred_element_type=F32) * SCALE
            s = s + b_ref[off]
            m = jnp.max(s, axis=-1, keepdims=True)
            p = jnp.exp(s - m)
            l = jnp.sum(p, axis=-1, keepdims=True)
            o = lax.dot_general(p.astype(BF16), v_ref[pl.ds(ks, KEYS_B), :], NN, preferred_element_type=F32)
            o_ref[pl.ds(qs, GRID_W), :] = (o / l).astype(BF16)
            lse_ref[pl.ds(qs, GRID_W), :] = m + jnp.log(l)
            return carry

        lax.fori_loop(0, ROWS, row, 0, unroll=2)

    full = pl.BlockSpec((S, HD), lambda h: (0, h))
    return pl.pallas_call(
        body, name="attn_b_fwd", grid=(4,),
        in_specs=[full, full, full, pl.BlockSpec((None, N_OFF, GRID_W, KEYS_B), lambda h: (h, 0, 0, 0))],
        out_specs=[pl.BlockSpec((S, HD), lambda h: (0, h)), pl.BlockSpec((None, S, 1), lambda h: (h, 0, 0))],
        out_shape=[jax.ShapeDtypeStruct((S, D_BR), BF16), jax.ShapeDtypeStruct((4, S, 1), F32)],
        compiler_params=_params(("parallel",)),
    )(qn, kn, vb, bias_tab)


def _attn_b_bwd(qn, kn, vb, bias_tab, ob, dob, lse):
    def body(q_ref, k_ref, v_ref, b_ref, o_ref, do_ref, lse_ref, dq_ref, dk_ref, dv_ref, db_ref):
        dk_ref[...] = jnp.zeros((S, HD), F32)
        dv_ref[...] = jnp.zeros((S, HD), F32)
        db_ref[...] = jnp.zeros((N_OFF, GRID_W, KEYS_B), F32)

        def row(r, carry):
            qs, ks, off = _row_window(r)
            rows = pl.ds(qs, GRID_W)
            keys = pl.ds(ks, KEYS_B)
            q = q_ref[rows, :]
            kw = k_ref[keys, :]
            s = lax.dot_general(q, kw, NT, preferred_element_type=F32) * SCALE + b_ref[off]
            p = jnp.exp(s - lse_ref[rows, :])
            do = do_ref[rows, :]
            dobf = do.astype(BF16)
            dsum = jnp.sum(do * o_ref[rows, :].astype(F32), axis=-1, keepdims=True)
            dp = lax.dot_general(dobf, v_ref[keys, :], NT, preferred_element_type=F32)
            ds = p * (dp - dsum)
            db_ref[off] += ds
            dsb = (ds * SCALE).astype(BF16)
            dq_ref[rows, :] = lax.dot_general(dsb, kw, NN, preferred_element_type=F32)
            dk_ref[keys, :] += lax.dot_general(dsb, q, TN, preferred_element_type=F32)
            dv_ref[keys, :] += lax.dot_general(p.astype(BF16), dobf, TN, preferred_element_type=F32)
            return carry

        lax.fori_loop(0, ROWS, row, 0, unroll=2)

    full = pl.BlockSpec((S, HD), lambda h: (0, h))
    slot = pl.BlockSpec((S, HD), lambda h: (0, h))
    tab = pl.BlockSpec((None, N_OFF, GRID_W, KEYS_B), lambda h: (h, 0, 0, 0))
    shape = jax.ShapeDtypeStruct((S, D_BR), F32)
    return pl.pallas_call(
        body, name="attn_b_bwd", grid=(4,),
        in_specs=[full, full, full, tab, slot, slot, pl.BlockSpec((None, S, 1), lambda h: (h, 0, 0))],
        out_specs=[slot, slot, slot, tab],
        out_shape=[shape, shape, shape, jax.ShapeDtypeStruct((4, N_OFF, GRID_W, KEYS_B), F32)],
        compiler_params=_params(("parallel",)),
    )(qn, kn, vb, bias_tab, ob, dob, lse)


def _epi_relu_sq(acc, ex, outs):
    u = jnp.maximum(acc, 0.0)
    outs[0][...] = u.astype(BF16)
    outs[1][...] = (u * u).astype(BF16)


def _epi_relu_sq_bwd(acc, ex, outs):
    outs[0][...] = (acc * (2.0 * ex[0][...].astype(F32))).astype(BF16)


def _epi_loss_head(acc, ex, outs):
    e = acc + ex[0][...] - ex[1][...]
    dy = e * (1.0 / D)
    outs[0][...] = dy
    outs[1][...] = dy.astype(BF16)
    part = (0.5 / D) * jnp.sum(jnp.sum(e * e, axis=-1, keepdims=True), axis=0, keepdims=True)
    first = (pl.program_id(0) == 0) & (pl.program_id(1) == 0)

    @pl.when(first)
    def _():
        outs[2][...] = part

    @pl.when(jnp.logical_not(first))
    def _():
        outs[2][...] += part


def _local_step(x, target, norm_mix, b_gate, gains, rpb_pad, norm_ffn,
                w_in, w_pa, w_pb, w_out, w_up, w_down, weight_grads, riders=lambda name: None):
    def ridden(name, *args, **kwargs):
        ride = riders(name)
        if ride is None:
            return _mm_nt(*args, name=name, **kwargs)
        out, rode = _mm_nt(*args, name=name, rider=ride[0], **kwargs)
        ride[1](rode)
        return out

    cos2, sin2 = _rope_tables()
    expand, keep, sel = _bias_constants()
    w_out3 = w_out[None]

    xn, rstd1 = _rms_fwd(x, norm_mix, name="rms_mix")
    proj = _mm_nn(xn, w_in, tm=1024, tn=1280, name="proj")
    qkv_a, qkv_b = _qk_prep(proj, gains, cos2, sin2)
    fwd_a = [_attn_a_fwd(*qkv_a[gi], gi) for gi in range(3)]
    oa, lse_a = _combine_a([o for o, _ in fwd_a], [l for _, l in fwd_a])
    bias_tab = _rows_to_tab(_bias_expand(rpb_pad, expand, keep, sel))
    ob, lse_b = _attn_b_fwd(*qkv_b, bias_tab)
    mixed, ya, yb = _mix_fwd(oa, ob, w_pa, w_pb, proj, b_gate)
    h1 = _mm_nn(mixed, w_out3, tm=1024, tn=1024, name="out_proj", epi=_epi_residual, extra=(x,))
    hn, rstd2 = _rms_fwd(h1, norm_ffn, name="rms_ffn")
    u, usq = _mm_nn(hn, w_up, tm=1024, tn=1024, name="ffn_up", epi=_epi_relu_sq,
                    out_dtypes=(BF16, BF16))
    dy, dyb, loss = _mm_nn(usq, w_down[0], tm=512, tn=512, name="ffn_down_0", epi=_epi_loss_head,
                           extra=(h1, target), out_dtypes=(F32, BF16), total=True, width=D)
    dy, dyb, loss_1 = _mm_nn(usq, w_down[1], tm=512, tn=512, name="ffn_down_1", epi=_epi_loss_head,
                             extra=(h1, target), out_dtypes=(F32, BF16), total=True, width=D,
                             col0=D // 2, into=(dy, dyb))
    loss = loss + loss_1

    sent = weight_grads("w_down", {5: (usq, dyb)})
    du = _mm_nt(dyb, w_down[0], more_b=(w_down[1],), tm=1024, tn=1024, name="ffn_down_bwd", out_dtype=BF16,
                epi=_epi_relu_sq_bwd, extra=(u,), after=sent)
    sent = weight_grads("w_up", {4: (hn, du)})
    dhn = ridden("ffn_up_bwd", du, w_up, tm=512, tn=512, after=sent)
    dh1, dh1b, g_norm_ffn = _rms_bwd(dhn, h1, rstd2, norm_ffn, dy, name="rms_ffn_bwd", bf16_copy=True)

    dya, dyb2, dproj, g_b = _mix_bwd(dh1b, w_out, proj, b_gate, ya, yb)
    sent = weight_grads("w_mix", {3: (mixed, dh1b), 1: (oa, dya), 2: (ob, dyb2)})
    dob = _mm_nt(dyb2, w_pb, tm=1024, tn=D_BR, name="proj_b_bwd", after=sent)
    prep = _proj_a_bwd(dya, w_pa, oa, lse_a)
    grads_a = [_attn_a_bwd(*qkv_a[gi], *prep[gi], gi) for gi in range(3)]
    dqb, dkb, dvb, dbias = _attn_b_bwd(*qkv_b, bias_tab, ob, dob, lse_b)
    g_rpb = _bias_reduce(dbias)
    dproj, g_gains = _qk_prep_bwd(dproj, proj, gains, cos2, sin2, grads_a, (dqb, dkb, dvb))
    sent = weight_grads("w_in", {0: (xn, dproj)})
    dxn = ridden("proj_bwd", dproj, w_in, tm=256, tn=512, after=sent)
    grad_x, g_norm_mix = _rms_bwd(dxn, x, rstd1, norm_mix, dh1, name="rms_mix_bwd", bf16_copy=False)

    small = (g_norm_mix, g_b, g_gains, g_rpb, g_norm_ffn)
    return loss, grad_x, small


def _cast_bf16(w, *, part=0, parts=1, tr=256):
    rows, cols = w.shape[0], w.shape[1] // parts
    tr = min(tr, rows)

    def body(w_ref, o_ref):
        o_ref[...] = w_ref[...].astype(BF16)

    return pl.pallas_call(
        body, name=f"cast_{rows}x{cols}_{part}", grid=(rows // tr,),
        in_specs=[pl.BlockSpec((tr, cols), lambda i: (i, part))],
        out_specs=pl.BlockSpec((tr, cols), lambda i: (i, 0)),
        out_shape=jax.ShapeDtypeStruct((rows, cols), BF16), compiler_params=_params(("parallel",)),
    )(w)


def _me_and_peers():
    x, y, c = lax.axis_index("x"), lax.axis_index("y"), lax.axis_index("c")
    me = 4 * x + 2 * y + c
    peers = []
    for k in range(1, N_DEV):
        px = 1 - x if k & 4 else x
        py = 1 - y if k & 2 else y
        pc = 1 - c if k & 1 else c
        peers.append(((px, py, pc), 4 * px + 2 * py + pc))
    return me, peers


def _gather_on_sequencer(shards, name):
    n = len(shards)
    hbm = pltpu.MemorySpace.HBM
    ins = [jax.new_ref(s, memory_space=hbm) for s in shards]
    outs = [jax.empty_ref(jax.ShapeDtypeStruct((N_DEV,) + s.shape, s.dtype), memory_space=hbm) for s in shards]

    @_sequencer(name, ((n, N_DEV - 1), (n, N_DEV - 1), (n,)), 0)
    def launch(send, recv, lsem):
        x, y, c = lax.axis_index("x"), lax.axis_index("y"), lax.axis_index("c")
        me, sibling = (x, y, c), (x, y, 1 - c)
        chips = [(1 - x, y), (x, 1 - y), (1 - x, 1 - y)]
        _handshake([sibling] + [(*chip, c) for chip in chips])

        def copy(w, k, block, to, src=None):
            px, py, pc = block
            dst = outs[w].at[4 * px + 2 * py + pc]
            return pltpu.make_async_remote_copy(dst if src is None else src, dst, send.at[w, k], recv.at[w, k],
                                                device_id=to, device_id_type=MESH)

        local = [pltpu.make_async_copy(ins[w], outs[w].at[4 * x + 2 * y + c], lsem.at[w]) for w in range(n)]
        for cp in local:
            cp.start()
        first = []
        for w in range(n):
            first += [copy(w, 1 + j, me, (*chip, c), src=ins[w]) for j, chip in enumerate(chips)]
            first.append(copy(w, 0, me, sibling, src=ins[w]))
        for cp in first:
            cp.start()
        passed = []
        for w in range(n):
            for j, chip in enumerate(chips):
                copy(w, 1 + j, (*chip, c), me).wait_recv()
                cp = copy(w, 4 + j, (*chip, c), sibling)
                cp.start()
                passed.append(cp)
        for w in range(n):
            copy(w, 0, sibling, me).wait_recv()
            for j, chip in enumerate(chips):
                copy(w, 4 + j, (*chip, 1 - c), me).wait_recv()
        for cp in first + passed:
            cp.wait_send()
        for cp in local:
            cp.wait()

    launch()
    return [o[...] for o in outs]


N_CHIP = 4
CHIPS = ((0, 0), (0, 1), (1, 0), (1, 1))


def _sequencer(name, n_sems, collective_id):
    return functools.partial(
        pl.kernel, mesh=plsc.ScalarSubcoreMesh(axis_name="seq", num_cores=1), name=name,
        scratch_types=tuple(pltpu.SemaphoreType.DMA(s) for s in n_sems),
        compiler_params=pltpu.CompilerParams(collective_id=collective_id))


def _handshake(peers):
    barrier = pltpu.get_barrier_semaphore()
    for peer in peers:
        pl.semaphore_signal(barrier, inc=1, device_id=peer, device_id_type=MESH)
    pl.semaphore_wait(barrier, len(peers))


def _chip_exchange_on_sequencer(parts, name):
    n = len(parts)
    hbm = pltpu.MemorySpace.HBM
    ins = [jax.new_ref(p, memory_space=hbm) for p in parts]
    outs = [jax.empty_ref(jax.ShapeDtypeStruct(p.shape, p.dtype), memory_space=hbm) for p in parts]

    @_sequencer(name, ((n, 3), (n, 3), (n,)), 2)
    def launch(send, recv, lsem):
        x, y, c = lax.axis_index("x"), lax.axis_index("y"), lax.axis_index("c")
        mine = 2 * x + y
        chips = [(1 - x, y), (x, 1 - y), (1 - x, 1 - y)]
        _handshake([(*chip, c) for chip in chips])
        local = [pltpu.make_async_copy(ins[w].at[mine], outs[w].at[mine], lsem.at[w]) for w in range(n)]
        for cp in local:
            cp.start()
        sends = []
        for w in range(n):
            for j, (px, py) in enumerate(chips):
                cp = pltpu.make_async_remote_copy(ins[w].at[2 * px + py], outs[w].at[mine],
                                                  send.at[w, j], recv.at[w, j],
                                                  device_id=(px, py, c), device_id_type=MESH)
                cp.start()
                sends.append(cp)
        for w in range(n):
            for j, (px, py) in enumerate(chips):
                pltpu.make_async_remote_copy(ins[w].at[mine], outs[w].at[2 * px + py],
                                             send.at[w, j], recv.at[w, j],
                                             device_id=(px, py, c), device_id_type=MESH).wait_recv()
        for cp in sends:
            cp.wait_send()
        for cp in local:
            cp.wait()

    launch()
    return [o[...] for o in outs]


GRAD_TILES = (dict(blocks_on="cols", tm=512, tn=1280), dict(blocks_on="cols", tm=512, tn=256),
              dict(blocks_on="cols", tm=512, tn=256), dict(blocks_on="rows", tm=256, tn=2048),
              dict(blocks_on="cols", tm=1024, tn=1024), dict(blocks_on="rows", tm=1024, tn=1024))


def _mm_tn_pair(a, b, *, blocks_on, tm, tn, name):
    t_len, m = a.shape
    n = b.shape[1]
    if blocks_on == "rows":
        rows, cols, inner = m // N_DEV, n, n // tn
        assert tm == rows
        a_spec = pl.BlockSpec((t_len, tm), lambda p, t, blk: (0, blk[p]))
        b_spec = pl.BlockSpec((t_len, tn), lambda p, t, blk: (0, t))
        out_spec = pl.BlockSpec((None, tm, tn), lambda p, t, blk: (
            jnp.maximum(p - N_CHIP, 0), 0, jnp.where(p < N_CHIP, 0, t)))
    else:
        rows, cols, inner = m, n // N_DEV, m // tm
        assert tn == cols
        a_spec = pl.BlockSpec((t_len, tm), lambda p, t, blk: (0, t))
        b_spec = pl.BlockSpec((t_len, tn), lambda p, t, blk: (0, blk[p]))
        out_spec = pl.BlockSpec((None, tm, tn), lambda p, t, blk: (
            jnp.maximum(p - N_CHIP, 0), jnp.where(p < N_CHIP, 0, t), 0))

    def body(blk_ref, a_ref, b_ref, o_ref, land, stage, send_sem, recv_sem):
        del blk_ref
        p, t = pl.program_id(0), pl.program_id(1)
        step = p * inner + t
        x, y, c = lax.axis_index("x"), lax.axis_index("y"), lax.axis_index("c")
        tile = _dot(a_ref[...], b_ref[...], TN)

        def to_sibling(slot, chip, piece):
            return pltpu.make_async_remote_copy(stage.at[slot], land.at[chip, piece], send_sem.at[slot],
                                                recv_sem.at[chip, piece],
                                                device_id=(x, y, 1 - c), device_id_type=MESH)

        @pl.when(p < N_CHIP)
        def _():
            slot = step % 2

            @pl.when(step >= 2)
            def _():
                to_sibling(slot, 0, 0).wait_send()

            stage[slot] = tile.astype(BF16)
            to_sibling(slot, p, t).start()

        @pl.when(step == N_CHIP * inner)
        def _():
            for slot in range(min(2, N_CHIP * inner)):
                to_sibling(slot, 0, 0).wait_send()

        @pl.when(p >= N_CHIP)
        def _():
            chip = p - N_CHIP
            to_sibling(0, chip, t).wait_recv()
            o_ref[...] = (tile + land[chip, t].astype(F32)).astype(BF16)

    c = lax.axis_index("c")
    order = jnp.stack([2 * ch + 1 - c for ch in range(N_CHIP)] + [2 * ch + c for ch in range(N_CHIP)])
    return pl.pallas_call(
        body, name=name,
        grid_spec=pltpu.PrefetchScalarGridSpec(
            num_scalar_prefetch=1, grid=(N_DEV, inner), in_specs=[a_spec, b_spec], out_specs=out_spec,
            scratch_shapes=[pltpu.VMEM((N_CHIP, inner, tm, tn), BF16), pltpu.VMEM((2, tm, tn), BF16),
                            pltpu.SemaphoreType.DMA((2,)), pltpu.SemaphoreType.DMA((N_CHIP, inner))]),
        out_shape=jax.ShapeDtypeStruct((N_CHIP, rows, cols), BF16),
        compiler_params=_params(("arbitrary", "arbitrary")),
    )(order.astype(jnp.int32), a, b)


def _adamw_math(g, w, m, v):
    m2 = B1 * m + (1.0 - B1) * g
    v2 = B2 * v + (1.0 - B2) * (g * g)
    delta = -LR * ((m2 / BC1) / (jnp.sqrt(v2 / BC2) + AEPS) + WD * w)
    return delta, m2, v2


def _adamw_block(ins, outs):
    p_ref, w_ref, m_ref, v_ref = ins
    g = p_ref[0].astype(F32)
    for b in range(1, N_CHIP):
        g = g + p_ref[b].astype(F32)
    delta, m2, v2 = _adamw_math(g, w_ref[...], m_ref[...], v_ref[...])
    for ref, val in zip(outs, (g, delta, m2, v2)):
        ref[...] = val


class _Rider(NamedTuple):
    inputs: tuple
    in_specs: list
    out_shape: list
    out_specs: list
    body: Callable


def _adamw_rider(parts, w, m, v):
    rows, cols = w.shape

    def rider(steps, step_of):
        rr = rows // steps
        blk = pl.BlockSpec((rr, cols), lambda *ids: (step_of(*ids[:2]), 0))
        chips = pl.BlockSpec((N_CHIP, rr, cols), lambda *ids: (0, step_of(*ids[:2]), 0))
        shape = jax.ShapeDtypeStruct((rows, cols), F32)
        return _Rider((parts, w, m, v), [chips, blk, blk, blk], [shape] * 4, [blk] * 4, _adamw_block)

    return rider


def _adamw(parts, w, m, v, *, name, after=(), tr=256):
    rows, cols = w.shape

    def body(*refs):
        _adamw_block(refs[:4], refs[4 + len(after):])

    spec = pl.BlockSpec((tr, cols), lambda i: (i, 0))
    shape = jax.ShapeDtypeStruct((rows, cols), F32)
    return pl.pallas_call(
        body, name=name, grid=(rows // tr,),
        in_specs=[pl.BlockSpec((N_CHIP, tr, cols), lambda i: (0, i, 0)), spec, spec, spec]
        + [pl.BlockSpec(memory_space=pl.ANY)] * len(after),
        out_specs=[spec] * 4, out_shape=[shape] * 4,
        compiler_params=_params(("parallel",)),
    )(parts, w, m, v, *after)


def _small_update(part, w, m, v):
    rows = part.shape[0]

    def body(p_ref, w_ref, m_ref, v_ref, g_ref, d_ref, mo_ref, vo_ref, buf, send, recv):
        me, peers = _me_and_peers()
        buf[me] = p_ref[...]
        sends = []
        for k, (dev, _) in enumerate(peers):
            cp = pltpu.make_async_remote_copy(p_ref, buf.at[me], send.at[k], recv.at[k],
                                              device_id=dev, device_id_type=MESH)
            cp.start()
            sends.append(cp)
        for k, (dev, idx) in enumerate(peers):
            pltpu.make_async_remote_copy(p_ref, buf.at[idx], send.at[k], recv.at[k],
                                         device_id=dev, device_id_type=MESH).wait_recv()
        for cp in sends:
            cp.wait_send()
        g = buf[0]
        for b in range(1, N_DEV):
            g = g + buf[b]
        delta, m2, v2 = _adamw_math(g, w_ref[...], m_ref[...], v_ref[...])
        g_ref[...] = g
        d_ref[...] = delta
        mo_ref[...] = m2
        vo_ref[...] = v2

    vm = pl.BlockSpec(memory_space=pltpu.VMEM)
    shape = jax.ShapeDtypeStruct((rows, HD), F32)
    return pl.pallas_call(
        body, name="small_params_update",
        in_specs=[vm] * 4, out_specs=[vm] * 4, out_shape=[shape] * 4,
        scratch_shapes=[pltpu.VMEM((N_DEV, rows, HD), F32),
                        pltpu.SemaphoreType.DMA((N_DEV - 1,)), pltpu.SemaphoreType.DMA((N_DEV - 1,))],
    )(part, w, m, v)


def _pack_small(norm_mix, b_gate, qa, ka, qb, kb, rpb, norm_ffn):
    gains = jnp.concatenate([qa, ka, qb, kb, jnp.zeros((4, HD), F32)], axis=0)
    rpb_pad = jnp.pad(rpb.reshape(4 * (2 * WIN_R - 1), 2 * WIN_C - 1), ((0, 4), (0, HD - (2 * WIN_C - 1))))
    return jnp.concatenate([norm_mix.reshape(16, HD), b_gate.reshape(32, HD), gains, rpb_pad,
                            norm_ffn.reshape(16, HD), jnp.zeros((8, HD), F32)], axis=0)


LOSS_ROW = 136


def _unpack_small(p):
    norm_mix = p[0:16].reshape(1, D)
    b_gate = p[16:48].reshape(1, 2 * D)
    qa, ka, qb, kb = (p[48 + i:49 + i] for i in range(4))
    rpb = p[56:116, :2 * WIN_C - 1].reshape(1, 4, 2 * WIN_R - 1, 2 * WIN_C - 1)
    norm_ffn = p[120:136].reshape(1, D)
    return norm_mix, b_gate, qa, ka, qb, kb, rpb, norm_ffn


def kernel(x, norm_mix, w_in, b_gate, q_norm_a, k_norm_a, q_norm_b, k_norm_b, rpb_b, w_proj_a, w_proj_b, w_out, norm_ffn, w_up, w_down, loss_target, m_norm_mix, m_w_in, m_b_gate, m_q_norm_a, m_k_norm_a, m_q_norm_b, m_k_norm_b, m_rpb_b, m_w_proj_a, m_w_proj_b, m_w_out, m_norm_ffn, m_w_up, m_w_down, v_norm_mix, v_w_in, v_b_gate, v_q_norm_a, v_k_norm_a, v_q_norm_b, v_k_norm_b, v_rpb_b, v_w_proj_a, v_w_proj_b, v_w_out, v_norm_ffn, v_w_up, v_w_down):
    big_w = (w_in[0], w_proj_a[0], w_proj_b[0], w_out[0], w_up[0], w_down[0])
    big_m = (m_w_in[0], m_w_proj_a[0], m_w_proj_b[0], m_w_out[0], m_w_up[0], m_w_down[0])
    big_v = (v_w_in[0], v_w_proj_a[0], v_w_proj_b[0], v_w_out[0], v_w_up[0], v_w_down[0])
    names = ("w_in", "w_proj_a", "w_proj_b", "w_out", "w_up", "w_down")

    shards = [_cast_bf16(w) for w in big_w[:5]]
    g_in, = _gather_on_sequencer(shards[0:1], "gather_w_in")
    g_pa, g_pb, g_out, g_up = _gather_on_sequencer(shards[1:5], "gather_w_mix_up")
    g_down = [_gather_on_sequencer([_cast_bf16(big_w[5], part=h, parts=2)], f"gather_w_down_{h}")[0]
              .reshape(1, D_FF, D // 2) for h in range(2)]
    small_w = _pack_small(norm_mix, b_gate, q_norm_a, k_norm_a, q_norm_b, k_norm_b, rpb_b, norm_ffn)
    small_m = _pack_small(m_norm_mix, m_b_gate, m_q_norm_a, m_k_norm_a, m_q_norm_b, m_k_norm_b, m_rpb_b, m_norm_ffn)
    small_v = _pack_small(v_norm_mix, v_b_gate, v_q_norm_a, v_k_norm_a, v_q_norm_b, v_k_norm_b, v_rpb_b, v_norm_ffn)

    upd = [None] * 6
    in_flight = {}

    def weight_grads(tag, operands):
        sums = {i: _mm_tn_pair(a, b, name=f"grad_{names[i]}", **GRAD_TILES[i]) for i, (a, b) in operands.items()}
        new = list(sums.values())
        in_flight.update(zip(sums, _chip_exchange_on_sequencer(new, f"chip_exchange_{tag}")))
        return new

    def riders(name):
        i = {"ffn_up_bwd": 5, "proj_bwd": 4}.get(name)
        if i is None:
            return None
        return (_adamw_rider(in_flight.pop(i), big_w[i], big_m[i], big_v[i]),
                functools.partial(upd.__setitem__, i))

    loss, grad_x, small_g = _local_step(
        x[0], loss_target[0], norm_mix, b_gate, small_w[48:56], small_w[56:120], norm_ffn,
        g_in, g_pa, g_pb, g_out.reshape(D, D), g_up, g_down, weight_grads, riders)

    g_norm_mix, g_b, g_gains, g_rpb, g_norm_ffn = small_g
    small_part = jnp.concatenate([g_norm_mix.reshape(16, HD), g_b.reshape(32, HD),
                                  g_gains, g_rpb, g_norm_ffn.reshape(16, HD),
                                  jnp.pad(loss, ((0, 7), (0, HD - 1)))], axis=0)
    slabs = _small_update(small_part, small_w, small_m, small_v)
    total = slabs[0][LOSS_ROW, 0]
    s_g, s_d, s_m, s_v = (_unpack_small(t) for t in slabs)

    last = grad_x
    for i, r in in_flight.items():
        upd[i] = _adamw(r, big_w[i], big_m[i], big_v[i], name=f"adamw_{names[i]}", after=[last])
        last = upd[i][0]
    b_g, b_d, b_m, b_v = ([u[j][None] for u in upd] for j in range(4))

    def order(small, big):
        nm, bg, qa, ka, qb, kb, rpb, nf = small
        w_in_, pa_, pb_, out_, up_, down_ = big
        return (nm, w_in_, bg, qa, ka, qb, kb, rpb, pa_, pb_, out_, nf, up_, down_)

    return (total, grad_x[None], *order(s_g, b_g), *order(s_d, b_d), *order(s_m, b_m), *order(s_v, b_v))
```

```python
import functools
from typing import Callable, NamedTuple

import jax
import jax.numpy as jnp
import numpy as np
from jax import lax
from jax.experimental import pallas as pl
from jax.experimental.pallas import tpu as pltpu
from jax.experimental.pallas import tpu_sc as plsc

F32 = jnp.float32
BF16 = jnp.bfloat16

N_DEV = 8
S = 2048
D = 2048
HD = 128
NH = 16
NH_A = 12
QKV = NH * HD
D_IN = 3 * QKV + 2 * D
D_BR = 512
D_FF = 4 * D
GRID_W = 64
ROWS = S // GRID_W
WIN_R = 8
WIN_C = 16
EPS = 1e-6
NEG = -1e30
SCALE = HD ** -0.5
ROPE_THETA = 10000.0
DILATIONS = (1, 4, 16)
HALF_A = 64
QB = 128

LR, B1, B2, AEPS, WD, STEP = 0.001, 0.9, 0.999, 1e-08, 0.01, 10
BC1 = 1.0 - B1 ** STEP
BC2 = 1.0 - B2 ** STEP

VMEM_LIMIT = 56 * 1024 * 1024
MESH = pl.DeviceIdType.MESH

NN = (((1,), (0,)), ((), ()))
NT = (((1,), (1,)), ((), ()))
TN = (((0,), (0,)), ((), ()))


def _params(sem):
    return pltpu.CompilerParams(dimension_semantics=sem, vmem_limit_bytes=VMEM_LIMIT)


def _matmul(a, b, *, product, grid, a_spec, b_spec, epi, out_shape, out_specs, name,
            extra=(), extra_specs=(), after=(), carried=False, rider=None, into=()):
    n_extra = len(extra)
    single = not isinstance(out_shape, (list, tuple))
    out_shape = [out_shape] if single else list(out_shape)
    out_specs = [out_specs] if single else list(out_specs)
    ride = rider(grid[0] * grid[1], lambda j, i: j * grid[1] + i) if rider else None
    r_in = list(ride.inputs) if ride else []
    n_main = len(out_shape)

    def body(a_ref, b_ref, *rest):
        n_in = n_extra + len(after) + len(r_in)
        ins, outs = rest[:n_in], rest[n_in + len(into):]
        epi(product(a_ref, b_ref, ins[:n_extra]), ins[:n_extra], outs[:n_main])
        if ride:
            ride.body(ins[n_extra + len(after):], outs[n_main:])

    res = pl.pallas_call(
        body, name=name, grid=grid,
        in_specs=[a_spec, b_spec, *extra_specs, *[pl.BlockSpec(memory_space=pl.ANY)] * len(after),
                  *(ride.in_specs if ride else []), *[pl.BlockSpec(memory_space=pl.ANY)] * len(into)],
        out_specs=out_specs + (ride.out_specs if ride else []),
        out_shape=out_shape + (ride.out_shape if ride else []),
        input_output_aliases={2 + n_extra + len(after) + len(r_in) + k: k for k in range(len(into))},
        compiler_params=_params(("arbitrary", "arbitrary") if carried else ("parallel", "parallel")),
    )(a, b, *extra, *after, *r_in, *into)
    main = res[0] if single else res[:n_main]
    return (main, res[n_main:]) if ride else main


def _dot(x, y, dims):
    return lax.dot_general(x, y, dims, preferred_element_type=F32)


def _epi_store(acc, ex, outs):
    outs[0][...] = acc.astype(outs[0].dtype)


def _epi_residual(acc, ex, outs):
    outs[0][...] = acc + ex[0][...]


def _mm_nn(a, b3, *, tm, tn, name, out_dtypes=(F32,), epi=_epi_store, extra=(), total=False,
           col0=0, width=None, into=()):
    m, kdim = a.shape
    g, _, ng = b3.shape
    n = g * ng
    c0 = col0 // tn
    if tn <= ng:
        npg = ng // tn
        b_spec = pl.BlockSpec((None, kdim, tn), lambda j, i: (j // npg, 0, j % npg))

        def product(a_ref, b_ref, ex):
            return _dot(a_ref[...], b_ref[...], NN)
    else:
        gb = tn // ng
        b_spec = pl.BlockSpec((gb, kdim, ng), lambda j, i: (j, 0, 0))

        def product(a_ref, b_ref, ex):
            return jnp.concatenate([_dot(a_ref[...], b_ref[q], NN) for q in range(gb)], axis=1)

    tile = pl.BlockSpec((tm, tn), lambda j, i: (i, j + c0))
    shapes = [jax.ShapeDtypeStruct((m, width or n), dt) for dt in out_dtypes]
    specs = [tile] * len(shapes)
    if total:
        shapes.append(jax.ShapeDtypeStruct((1, 1), F32))
        specs.append(pl.BlockSpec((1, 1), lambda j, i: (0, 0)))
    single = len(shapes) == 1
    return _matmul(
        a, b3, product=product, grid=(n // tn, m // tm), epi=epi, name=name, carried=total, into=into,
        a_spec=pl.BlockSpec((tm, kdim), lambda j, i: (i, 0)), b_spec=b_spec,
        extra=extra, extra_specs=[tile] * len(extra),
        out_shape=shapes[0] if single else shapes, out_specs=specs[0] if single else specs)


def _mm_nt(a, b3, *, tm, tn, name, out_dtype=F32, epi=_epi_store, extra=(), after=(), rider=None, more_b=()):
    m, kdim = a.shape
    _, n, _ = b3.shape
    n_b = len(more_b)

    def product(a_ref, b_ref, ex):
        acc, k0 = None, 0
        for ref in (b_ref, *ex[:n_b]):
            for q in range(ref.shape[0]):
                part = _dot(a_ref[:, k0:k0 + ref.shape[2]], ref[q], NT)
                acc = part if acc is None else acc + part
                k0 += ref.shape[2]
        return acc

    def write(acc, ex, outs):
        epi(acc, ex[n_b:], outs)

    def w_spec(w):
        return pl.BlockSpec((w.shape[0], tn, w.shape[2]), lambda j, i: (0, j, 0))

    tile = pl.BlockSpec((tm, tn), lambda j, i: (i, j))
    return _matmul(
        a, b3, product=product, grid=(n // tn, m // tm), epi=write, name=name,
        a_spec=pl.BlockSpec((tm, kdim), lambda j, i: (i, 0)), b_spec=w_spec(b3),
        extra=(*more_b, *extra), extra_specs=[w_spec(w) for w in more_b] + [tile] * len(extra),
        after=after, rider=rider,
        out_shape=jax.ShapeDtypeStruct((m, n), out_dtype), out_specs=tile)


def _mm_tn(a, b, *, tm, tn, name, groups=1, out_dtype=BF16):
    t, m = a.shape
    _, n = b.shape
    ng = n // groups
    if tn <= ng:
        npg = ng // tn
        out_spec = pl.BlockSpec((None, tm, tn), lambda j, i: (j // npg, i, j % npg))
        epi = _epi_store

        def product(a_ref, b_ref, ex):
            return _dot(a_ref[...], b_ref[...], TN)
    else:
        gb = tn // ng
        out_spec = pl.BlockSpec((gb, tm, ng), lambda j, i: (j, i, 0))

        def product(a_ref, b_ref, ex):
            return [_dot(a_ref[...], b_ref[:, q * ng:(q + 1) * ng], TN) for q in range(gb)]

        def epi(parts, ex, outs):
            for q, part in enumerate(parts):
                outs[0][q] = part.astype(out_dtype)

    return _matmul(
        a, b, product=product, grid=(n // tn, m // tm), epi=epi, name=name,
        a_spec=pl.BlockSpec((t, tm), lambda j, i: (0, i)),
        b_spec=pl.BlockSpec((t, tn), lambda j, i: (0, j)),
        out_shape=jax.ShapeDtypeStruct((groups, m, ng), out_dtype), out_specs=out_spec)


def _rms_fwd(x, g, *, name, tr=256):
    def body(x_ref, g_ref, y_ref, r_ref):
        xv = x_ref[...]
        r = lax.rsqrt(jnp.mean(xv * xv, axis=-1, keepdims=True) + EPS)
        y_ref[...] = (xv * r * g_ref[...]).astype(BF16)
        r_ref[...] = r

    row = pl.BlockSpec((tr, D), lambda i: (i, 0))
    return pl.pallas_call(
        body, name=name, grid=(S // tr,),
        in_specs=[row, pl.BlockSpec((1, D), lambda i: (0, 0))],
        out_specs=[row, pl.BlockSpec((tr, 1), lambda i: (i, 0))],
        out_shape=[jax.ShapeDtypeStruct((S, D), BF16), jax.ShapeDtypeStruct((S, 1), F32)],
        compiler_params=_params(("parallel",)),
    )(x, g)


def _rms_bwd(dy, x, rstd, g, resid, *, name, bf16_copy, tr=256):
    def body(dy_ref, x_ref, r_ref, g_ref, res_ref, dx_ref, *rest):
        dg_ref = rest[-1]
        r = r_ref[...]
        xh = x_ref[...] * r
        dyv = dy_ref[...]
        t = dyv * g_ref[...]
        dx = r * (t - xh * jnp.mean(t * xh, axis=-1, keepdims=True)) + res_ref[...]
        dx_ref[...] = dx
        if bf16_copy:
            rest[0][...] = dx.astype(BF16)
        part = jnp.sum(dyv * xh, axis=0, keepdims=True)

        @pl.when(pl.program_id(0) == 0)
        def _():
            dg_ref[...] = part

        @pl.when(pl.program_id(0) > 0)
        def _():
            dg_ref[...] += part

    row = pl.BlockSpec((tr, D), lambda i: (i, 0))
    vec = pl.BlockSpec((1, D), lambda i: (0, 0))
    return pl.pallas_call(
        body, name=name, grid=(S // tr,),
        in_specs=[row, row, pl.BlockSpec((tr, 1), lambda i: (i, 0)), vec, row],
        out_specs=[row] + [row] * bf16_copy + [vec],
        out_shape=[jax.ShapeDtypeStruct((S, D), F32)] + [jax.ShapeDtypeStruct((S, D), BF16)] * bf16_copy
        + [jax.ShapeDtypeStruct((1, D), F32)],
        compiler_params=_params(("arbitrary",)),
    )(dy, x, rstd, g, resid)


def _rope_tables():
    pos = np.arange(S, dtype=np.float32)
    inv = (ROPE_THETA ** (-np.arange(0, HD, 2, dtype=np.float32) / HD)).astype(np.float32)
    ang = pos[:, None] * inv[None, :]
    cos, sin = np.cos(ang), np.sin(ang)
    return (jnp.asarray(np.concatenate([cos, cos], axis=-1), F32),
            jnp.asarray(np.concatenate([-sin, sin], axis=-1), F32))


def _swap_halves(t):
    return pltpu.roll(t, HD // 2, axis=1)


TOK = 256


def _lane_block_spec(d, last=HD):
    return pl.BlockSpec((4, TOK // d, d * last), lambda i: (0, i, 0))


def _to_lane_blocks(dst, head, val, d, scr, dtype):
    w = val.shape[1]
    if d == 1:
        dst[head] = val.astype(dtype)
        return
    scr[...] = val
    for r in range(d):
        dst[head, :, r * w:(r + 1) * w] = scr[pl.ds(r, TOK // d, stride=d), :].astype(dtype)


def _from_lane_blocks(src, head, d, w, scr):
    if d == 1:
        return src[head].astype(F32)
    for r in range(d):
        scr[pl.ds(r, TOK // d, stride=d), :] = src[head, :, r * w:(r + 1) * w].astype(F32)
    return scr[...]


def _qk_prep(proj, gains, cos2, sin2):
    def body(q_ref, k_ref, v_ref, g_ref, c_ref, s_ref, *rest):
        outs, scr = rest[:-1], rest[-1]
        cos, sin = c_ref[...], s_ref[...]
        for which, (src, row_a, row_b) in enumerate(((q_ref, 0, 2), (k_ref, 1, 3), (v_ref, None, None))):
            for h in range(NH):
                y = src[:, h * HD:(h + 1) * HD]
                if row_a is not None:
                    y = y * lax.rsqrt(jnp.mean(y * y, axis=-1, keepdims=True) + EPS)
                    if h < NH_A:
                        y = y * g_ref[row_a:row_a + 1, :]
                        y = y * cos + _swap_halves(y) * sin
                    else:
                        y = y * g_ref[row_b:row_b + 1, :]
                if h < NH_A:
                    gi = h // 4
                    _to_lane_blocks(outs[3 * gi + which], h % 4, y, DILATIONS[gi], scr, BF16)
                else:
                    hb = h - NH_A
                    outs[9 + which][:, hb * HD:(hb + 1) * HD] = y.astype(BF16)

    def blk(c):
        return pl.BlockSpec((TOK, QKV), lambda i: (i, c))
    tab = pl.BlockSpec((TOK, HD), lambda i: (i, 0))
    out_specs, out_shape = [], []
    for d in DILATIONS:
        out_specs += [_lane_block_spec(d)] * 3
        out_shape += [jax.ShapeDtypeStruct((4, S // d, d * HD), BF16)] * 3
    out_specs += [pl.BlockSpec((TOK, D_BR), lambda i: (i, 0))] * 3
    out_shape += [jax.ShapeDtypeStruct((S, D_BR), BF16)] * 3
    outs = pl.pallas_call(
        body, name="qk_prep", grid=(S // TOK,),
        in_specs=[blk(0), blk(1), blk(2), pl.BlockSpec((8, HD), lambda i: (0, 0)), tab, tab],
        out_specs=out_specs, out_shape=out_shape,
        scratch_shapes=[pltpu.VMEM((TOK, HD), F32)],
        compiler_params=_params(("parallel",)),
    )(proj, proj, proj, gains, cos2, sin2)
    return [tuple(outs[3 * gi:3 * gi + 3]) for gi in range(3)], tuple(outs[9:12])


def _qk_prep_bwd(dproj, proj, gains, cos2, sin2, grads_a, grads_b):
    def body(dp_in, q_ref, k_ref, g_ref, c_ref, s_ref, *rest):
        grads, (dp_out, dg_ref, scr) = rest[:12], rest[12:]
        del dp_in
        cos, sin = c_ref[...], s_ref[...]

        def grad_of(which, h):
            if h < NH_A:
                gi = h // 4
                return _from_lane_blocks(grads[3 * gi + which], h % 4, DILATIONS[gi], HD, scr)
            hb = h - NH_A
            return grads[9 + which][:, hb * HD:(hb + 1) * HD]

        dg_rows = []
        for which, (src, base, row_a, row_b) in enumerate(((q_ref, 0, 0, 2), (k_ref, QKV, 1, 3))):
            dg_a = jnp.zeros((1, HD), F32)
            dg_b = jnp.zeros((1, HD), F32)
            for h in range(NH):
                t = src[:, h * HD:(h + 1) * HD]
                dy = grad_of(which, h)
                r = lax.rsqrt(jnp.mean(t * t, axis=-1, keepdims=True) + EPS)
                xh = t * r
                if h < NH_A:
                    dy = dy * cos - _swap_halves(dy) * sin
                    gain = g_ref[row_a:row_a + 1, :]
                    dg_a = dg_a + jnp.sum(dy * xh, axis=0, keepdims=True)
                else:
                    gain = g_ref[row_b:row_b + 1, :]
                    dg_b = dg_b + jnp.sum(dy * xh, axis=0, keepdims=True)
                u = dy * gain
                dx = r * (u - xh * jnp.mean(u * xh, axis=-1, keepdims=True))
                dp_out[:, base + h * HD:base + (h + 1) * HD] = dx.astype(BF16)
            dg_rows += [(row_a, dg_a), (row_b, dg_b)]
        for h in range(NH):
            dp_out[:, 2 * QKV + h * HD:2 * QKV + (h + 1) * HD] = grad_of(2, h).astype(BF16)

        @pl.when(pl.program_id(0) == 0)
        def _():
            dg_ref[...] = jnp.zeros((8, HD), F32)

        for row, val in dg_rows:
            dg_ref[row:row + 1, :] += val

    def blk(c):
        return pl.BlockSpec((TOK, QKV), lambda i: (i, c))
    tab = pl.BlockSpec((TOK, HD), lambda i: (i, 0))
    gain_spec = pl.BlockSpec((8, HD), lambda i: (0, 0))
    grad_specs = [s for d in DILATIONS for s in [_lane_block_spec(d)] * 3]
    grad_specs += [pl.BlockSpec((TOK, D_BR), lambda i: (i, 0))] * 3
    return pl.pallas_call(
        body, name="qk_prep_bwd", grid=(S // TOK,),
        in_specs=[pl.BlockSpec(memory_space=pl.ANY), blk(0), blk(1), gain_spec, tab, tab] + grad_specs,
        out_specs=[pl.BlockSpec((TOK, 3 * QKV), lambda i: (i, 0)), gain_spec],
        out_shape=[jax.ShapeDtypeStruct((S, D_IN), BF16), jax.ShapeDtypeStruct((8, HD), F32)],
        input_output_aliases={0: 0},
        scratch_shapes=[pltpu.VMEM((TOK, HD), F32)],
        compiler_params=_params(("arbitrary",)),
    )(dproj, proj, proj, gains, cos2, sin2, *[g for grp in grads_a for g in grp], *grads_b)


def _mix_fwd(oa, ob, w_pa, w_pb, proj, b_gate, *, tr=256):
    def body(oa_ref, ob_ref, pa_ref, pb_ref, la_ref, lb_ref, ba_ref, bb_ref, mix_ref, ya_ref, yb_ref):
        ya = jnp.concatenate([_dot(oa_ref[...], pa_ref[q], NN) for q in range(N_DEV)], axis=1)
        yb = jnp.concatenate([_dot(ob_ref[...], pb_ref[q], NN) for q in range(N_DEV)], axis=1)
        ga = jax.nn.sigmoid(la_ref[...] + ba_ref[...])
        gb = jax.nn.sigmoid(lb_ref[...] + bb_ref[...])
        mix_ref[...] = (ga * ya + gb * yb).astype(BF16)
        ya_ref[...] = ya.astype(BF16)
        yb_ref[...] = yb.astype(BF16)

    row = pl.BlockSpec((tr, D), lambda i: (i, 0))
    branch = pl.BlockSpec((tr, D_BR), lambda i: (i, 0))
    whole = pl.BlockSpec((N_DEV, D_BR, D // N_DEV), lambda i: (0, 0, 0))
    return pl.pallas_call(
        body, name="mix_fwd", grid=(S // tr,),
        in_specs=[branch, branch, whole, whole,
                  pl.BlockSpec((tr, D), lambda i: (i, 3)), pl.BlockSpec((tr, D), lambda i: (i, 4)),
                  pl.BlockSpec((1, D), lambda i: (0, 0)), pl.BlockSpec((1, D), lambda i: (0, 1))],
        out_specs=[row, row, row], out_shape=[jax.ShapeDtypeStruct((S, D), BF16)] * 3,
        compiler_params=_params(("parallel",)),
    )(oa, ob, w_pa, w_pb, proj, proj, b_gate, b_gate)


def _mix_bwd(dh1b, w_out, proj, b_gate, ya, yb, *, tr=256):
    def body(dh_ref, w_ref, la_ref, lb_ref, b_ref, ya_ref, yb_ref, dya_ref, dyb_ref, dp_ref, db_ref):
        dm = _dot(dh_ref[...], w_ref[...], NT)
        parts = []
        for l_ref, y_ref, dy_ref, lo in ((la_ref, ya_ref, dya_ref, 0), (lb_ref, yb_ref, dyb_ref, D)):
            g = jax.nn.sigmoid(l_ref[...] + b_ref[:, lo:lo + D])
            dy_ref[...] = (dm * g).astype(BF16)
            dl = dm * y_ref[...].astype(F32) * g * (1.0 - g)
            dp_ref[:, lo:lo + D] = dl.astype(BF16)
            parts.append(jnp.sum(dl, axis=0, keepdims=True))
        part = jnp.concatenate(parts, axis=1)

        @pl.when(pl.program_id(0) == 0)
        def _():
            db_ref[...] = part

        @pl.when(pl.program_id(0) > 0)
        def _():
            db_ref[...] += part

    row = pl.BlockSpec((tr, D), lambda i: (i, 0))
    vec = pl.BlockSpec((1, 2 * D), lambda i: (0, 0))
    gate_cols = pl.BlockSpec((pl.Element(tr), pl.Element(2 * D)), lambda i: (i * tr, 3 * QKV))
    return pl.pallas_call(
        body, name="mix_bwd", grid=(S // tr,),
        in_specs=[row, pl.BlockSpec((D, D), lambda i: (0, 0)),
                  pl.BlockSpec((tr, D), lambda i: (i, 3)), pl.BlockSpec((tr, D), lambda i: (i, 4)), vec, row, row],
        out_specs=[row, row, gate_cols, vec],
        out_shape=[jax.ShapeDtypeStruct((S, D), BF16), jax.ShapeDtypeStruct((S, D), BF16),
                   jax.ShapeDtypeStruct((S, D_IN), BF16), jax.ShapeDtypeStruct((1, 2 * D), F32)],
        compiler_params=_params(("arbitrary",)),
    )(dh1b, w_out, proj, proj, b_gate, ya, yb)


def _band_blocks(m_len):
    wk = min(m_len, QB + 2 * QB)
    return [(qb * QB, min(max(qb * QB - QB, 0), m_len - wk), wk) for qb in range(m_len // QB)]


def _band_scores(q, kw, q0, k0, wk):
    s = _dot(q, kw, NT) * SCALE
    qpos = q0 + lax.broadcasted_iota(jnp.int32, (QB, 1), 0)
    kpos = k0 + lax.broadcasted_iota(jnp.int32, (1, wk), 1)
    return jnp.where(jnp.abs(kpos - qpos) <= HALF_A, s, NEG)


def _attn_a_fwd(q, k, v, gi):
    d = DILATIONS[gi]
    m_len = S // d

    def body(q_ref, k_ref, v_ref, o_ref, lse_ref):
        for r in range(d):
            lanes = slice(r * HD, (r + 1) * HD)
            for q0, k0, wk in _band_blocks(m_len):
                s = _band_scores(q_ref[q0:q0 + QB, lanes], k_ref[k0:k0 + wk, lanes], q0, k0, wk)
                m = jnp.max(s, axis=-1, keepdims=True)
                p = jnp.exp(s - m)
                l = jnp.sum(p, axis=-1, keepdims=True)
                o_ref[q0:q0 + QB, lanes] = _dot(p.astype(BF16), v_ref[k0:k0 + wk, lanes], NN) / l
                lse_ref[q0:q0 + QB, r:r + 1] = m + jnp.log(l)

    head = pl.BlockSpec((None, m_len, d * HD), lambda h: (h, 0, 0))
    stat = pl.BlockSpec((None, m_len, d), lambda h: (h, 0, 0))
    return pl.pallas_call(
        body, name=f"attn_a_fwd_{gi}", grid=(4,),
        in_specs=[head, head, head], out_specs=[head, stat],
        out_shape=[jax.ShapeDtypeStruct((4, m_len, d * HD), F32), jax.ShapeDtypeStruct((4, m_len, d), F32)],
        compiler_params=_params(("parallel",)),
    )(q, k, v)


def _combine_a(os, lses):
    def body(o0, o1, o2, l0, l1, l2, oa_ref, lse_ref, scr, scr1):
        for h in range(4):
            o = [_from_lane_blocks(ref, h, d, HD, scr) for ref, d in zip((o0, o1, o2), DILATIONS)]
            a, b, c = (_from_lane_blocks(ref, h, d, 1, scr1) for ref, d in zip((l0, l1, l2), DILATIONS))
            m = jnp.maximum(jnp.maximum(a, b), c)
            wa, wb, wc = jnp.exp(a - m), jnp.exp(b - m), jnp.exp(c - m)
            tot = wa + wb + wc
            oa_ref[:, h * HD:(h + 1) * HD] = ((wa * o[0] + wb * o[1] + wc * o[2]) / tot).astype(BF16)
            lse_ref[h] = m + jnp.log(tot)

    return pl.pallas_call(
        body, name="combine_a", grid=(S // TOK,),
        in_specs=[_lane_block_spec(d) for d in DILATIONS] + [_lane_block_spec(d, 1) for d in DILATIONS],
        out_specs=[pl.BlockSpec((TOK, D_BR), lambda i: (i, 0)), pl.BlockSpec((4, TOK, 1), lambda i: (0, i, 0))],
        out_shape=[jax.ShapeDtypeStruct((S, D_BR), BF16), jax.ShapeDtypeStruct((4, S, 1), F32)],
        scratch_shapes=[pltpu.VMEM((TOK, HD), F32), pltpu.VMEM((TOK, 1), F32)],
        compiler_params=_params(("parallel",)),
    )(*os, *lses)


def _proj_a_bwd(dya, w_pa, oa, lse):
    kg = D // N_DEV

    def body(dy_ref, w_ref, o_ref, l_ref, *rest):
        outs, (scr, scr1) = rest[:9], rest[9:]
        doa = _dot(dy_ref[:, 0:kg], w_ref[0], NT)
        for q in range(1, N_DEV):
            doa = doa + _dot(dy_ref[:, q * kg:(q + 1) * kg], w_ref[q], NT)
        for h in range(4):
            do = doa[:, h * HD:(h + 1) * HD]
            dsum = jnp.sum(do * o_ref[:, h * HD:(h + 1) * HD].astype(F32), axis=-1, keepdims=True)
            for gi, d in enumerate(DILATIONS):
                _to_lane_blocks(outs[3 * gi], h, do, d, scr, BF16)
                _to_lane_blocks(outs[3 * gi + 1], h, l_ref[h], d, scr1, F32)
                _to_lane_blocks(outs[3 * gi + 2], h, dsum, d, scr1, F32)

    row = pl.BlockSpec((TOK, D_BR), lambda i: (i, 0))
    out_specs, out_shape = [], []
    for d in DILATIONS:
        out_specs += [_lane_block_spec(d), _lane_block_spec(d, 1), _lane_block_spec(d, 1)]
        out_shape += [jax.ShapeDtypeStruct((4, S // d, d * HD), BF16)] + [jax.ShapeDtypeStruct((4, S // d, d), F32)] * 2
    outs = pl.pallas_call(
        body, name="proj_a_bwd", grid=(S // TOK,),
        in_specs=[pl.BlockSpec((TOK, D), lambda i: (i, 0)),
                  pl.BlockSpec((N_DEV, D_BR, kg), lambda i: (0, 0, 0)),
                  row, pl.BlockSpec((4, TOK, 1), lambda i: (0, i, 0))],
        out_specs=out_specs, out_shape=out_shape,
        scratch_shapes=[pltpu.VMEM((TOK, HD), F32), pltpu.VMEM((TOK, 1), F32)],
        compiler_params=_params(("parallel",)),
    )(dya, w_pa, oa, lse)
    return [tuple(outs[3 * gi:3 * gi + 3]) for gi in range(3)]


def _attn_a_bwd(q, k, v, do, lse, dsum, gi):
    d = DILATIONS[gi]
    m_len = S // d

    def body(q_ref, k_ref, v_ref, do_ref, lse_ref, dsum_ref, dq_ref, dk_ref, dv_ref):
        dk_ref[...] = jnp.zeros((m_len, d * HD), F32)
        dv_ref[...] = jnp.zeros((m_len, d * HD), F32)
        for r in range(d):
            lanes = slice(r * HD, (r + 1) * HD)
            for q0, k0, wk in _band_blocks(m_len):
                rows, keys = slice(q0, q0 + QB), slice(k0, k0 + wk)
                qv, kw, vw, dov = q_ref[rows, lanes], k_ref[keys, lanes], v_ref[keys, lanes], do_ref[rows, lanes]
                p = jnp.exp(_band_scores(qv, kw, q0, k0, wk) - lse_ref[rows, r:r + 1])
                ds = (p * (_dot(dov, vw, NT) - dsum_ref[rows, r:r + 1]) * SCALE).astype(BF16)
                dq_ref[rows, lanes] = _dot(ds, kw, NN)
                dk_ref[keys, lanes] += _dot(ds, qv, TN)
                dv_ref[keys, lanes] += _dot(p.astype(BF16), dov, TN)

    head = pl.BlockSpec((None, m_len, d * HD), lambda h: (h, 0, 0))
    stat = pl.BlockSpec((None, m_len, d), lambda h: (h, 0, 0))
    shape = jax.ShapeDtypeStruct((4, m_len, d * HD), F32)
    return pl.pallas_call(
        body, name=f"attn_a_bwd_{gi}", grid=(4,),
        in_specs=[head, head, head, head, stat, stat], out_specs=[head, head, head],
        out_shape=[shape, shape, shape],
        compiler_params=_params(("parallel",)),
    )(q, k, v, do, lse, dsum)


KEYS_B = WIN_R * GRID_W
N_OFF = WIN_R


def _bias_constants():
    q = np.arange(GRID_W)[:, None]
    kc = np.arange(GRID_W)[None, :]
    dc = np.clip(kc - q, -(WIN_C - 1), WIN_C - 1) + (WIN_C - 1)
    expand = np.zeros((HD, GRID_W * GRID_W), np.float32)
    expand[dc.reshape(-1), np.arange(GRID_W * GRID_W)] = 1.0
    cs = np.clip(q - WIN_C // 2, 0, GRID_W - WIN_C)
    keep = ((kc >= cs) & (kc < cs + WIN_C)).reshape(1, -1).astype(np.float32)
    sel = np.zeros((64, 4 * N_OFF * WIN_R), np.float32)
    for h in range(4):
        for off in range(N_OFF):
            for j in range(WIN_R):
                sel[h * (2 * WIN_R - 1) + off + j, (h * N_OFF + off) * WIN_R + j] = 1.0
    return jnp.asarray(expand), jnp.asarray(keep), jnp.asarray(sel)


def _bias_expand(rpb_pad, expand, keep, sel):
    def body(r_ref, e_ref, k_ref, s_ref, o_ref):
        t = lax.dot_general(r_ref[...], e_ref[...], NN, precision=lax.Precision.HIGHEST,
                            preferred_element_type=F32)
        rows = lax.dot_general(s_ref[...], t, TN, precision=lax.Precision.HIGHEST,
                               preferred_element_type=F32)
        o_ref[...] = jnp.where(k_ref[...] > 0.5, rows, NEG)

    return pl.pallas_call(
        body, name="bias_expand",
        out_shape=jax.ShapeDtypeStruct((4 * N_OFF * WIN_R, GRID_W * GRID_W), F32),
        compiler_params=pltpu.CompilerParams(vmem_limit_bytes=VMEM_LIMIT),
    )(rpb_pad, expand, keep, sel)


def _bias_reduce(dbias_tab):
    lane0 = GRID_W - WIN_C
    flip = np.zeros((GRID_W, GRID_W), np.float32)
    flip[np.arange(GRID_W), GRID_W - 1 - np.arange(GRID_W)] = 1.0
    place = np.zeros((WIN_R, 64, 4 * N_OFF), np.float32)
    for j in range(WIN_R):
        for h in range(4):
            for off in range(N_OFF):
                place[j, h * (2 * WIN_R - 1) + off + j, h * N_OFF + off] = 1.0

    def exact(x, y):
        return lax.dot_general(x, y, NN, precision=lax.Precision.HIGHEST, preferred_element_type=F32)

    def body(x_ref, flip_ref, place_ref, o_ref, z_ref):
        for h in range(4):
            for off in range(N_OFF):
                lined_up = pltpu.roll(exact(flip_ref[...], x_ref[h, off]), 0, axis=1, stride=1, stride_axis=0)
                z_ref[h * N_OFF + off:h * N_OFF + off + 1, :] = jnp.sum(lined_up, axis=0, keepdims=True)
        acc = jnp.zeros((64, HD), F32)
        for j in range(WIN_R):
            at_zero = pltpu.roll(z_ref[...], (KEYS_B - (j * GRID_W + lane0)) % KEYS_B, axis=1)[:, :HD]
            acc = acc + exact(place_ref[j], at_zero)
        lane = lax.broadcasted_iota(jnp.int32, (64, HD), 1)
        o_ref[...] = jnp.where(lane < 2 * WIN_C - 1, acc, 0.0)

    return pl.pallas_call(
        body, name="bias_reduce", out_shape=jax.ShapeDtypeStruct((64, HD), F32),
        scratch_shapes=[pltpu.VMEM((4 * N_OFF, KEYS_B), F32)],
        compiler_params=pltpu.CompilerParams(vmem_limit_bytes=VMEM_LIMIT),
    )(dbias_tab, jnp.asarray(flip), jnp.asarray(place))


def _rows_to_tab(rows):
    t = rows.reshape(4, N_OFF, WIN_R, GRID_W, GRID_W)
    return t.transpose(0, 1, 3, 2, 4).reshape(4, N_OFF, GRID_W, KEYS_B)


def _row_window(r):
    r0 = jnp.clip(r - WIN_R // 2, 0, ROWS - WIN_R)
    off = r0 + (WIN_R - 1) - r
    return pl.multiple_of(r * GRID_W, GRID_W), pl.multiple_of(r0 * GRID_W, GRID_W), off


def _attn_b_fwd(qn, kn, vb, bias_tab):
    def body(q_ref, k_ref, v_ref, b_ref, o_ref, lse_ref):
        def row(r, carry):
            qs, ks, off = _row_window(r)
            q = q_ref[pl.ds(qs, GRID_W), :]
            s = lax.dot_general(q, k_ref[pl.ds(ks, KEYS_B), :], NT, preferred_element_type=F32) * SCALE
            s = s + b_ref[off]
            m = jnp.max(s, axis=-1, keepdims=True)
            p = jnp.exp(s - m)
            l = jnp.sum(p, axis=-1, keepdims=True)
            o = lax.dot_general(p.astype(BF16), v_ref[pl.ds(ks, KEYS_B), :], NN, preferred_element_type=F32)
            o_ref[pl.ds(qs, GRID_W), :] = (o / l).astype(BF16)
            lse_ref[pl.ds(qs, GRID_W), :] = m + jnp.log(l)
            return carry

        lax.fori_loop(0, ROWS, row, 0, unroll=2)

    full = pl.BlockSpec((S, HD), lambda h: (0, h))
    return pl.pallas_call(
        body, name="attn_b_fwd", grid=(4,),
        in_specs=[full, full, full, pl.BlockSpec((None, N_OFF, GRID_W, KEYS_B), lambda h: (h, 0, 0, 0))],
        out_specs=[pl.BlockSpec((S, HD), lambda h: (0, h)), pl.BlockSpec((None, S, 1), lambda h: (h, 0, 0))],
        out_shape=[jax.ShapeDtypeStruct((S, D_BR), BF16), jax.ShapeDtypeStruct((4, S, 1), F32)],
        compiler_params=_params(("parallel",)),
    )(qn, kn, vb, bias_tab)


def _attn_b_bwd(qn, kn, vb, bias_tab, ob, dob, lse):
    def body(q_ref, k_ref, v_ref, b_ref, o_ref, do_ref, lse_ref, dq_ref, dk_ref, dv_ref, db_ref):
        dk_ref[...] = jnp.zeros((S, HD), F32)
        dv_ref[...] = jnp.zeros((S, HD), F32)
        db_ref[...] = jnp.zeros((N_OFF, GRID_W, KEYS_B), F32)

        def row(r, carry):
            qs, ks, off = _row_window(r)
            rows = pl.ds(qs, GRID_W)
            keys = pl.ds(ks, KEYS_B)
            q = q_ref[rows, :]
            kw = k_ref[keys, :]
            s = lax.dot_general(q, kw, NT, preferred_element_type=F32) * SCALE + b_ref[off]
            p = jnp.exp(s - lse_ref[rows, :])
            do = do_ref[rows, :]
            dobf = do.astype(BF16)
            dsum = jnp.sum(do * o_ref[rows, :].astype(F32), axis=-1, keepdims=True)
            dp = lax.dot_general(dobf, v_ref[keys, :], NT, preferred_element_type=F32)
            ds = p * (dp - dsum)
            db_ref[off] += ds
            dsb = (ds * SCALE).astype(BF16)
            dq_ref[rows, :] = lax.dot_general(dsb, kw, NN, preferred_element_type=F32)
            dk_ref[keys, :] += lax.dot_general(dsb, q, TN, preferred_element_type=F32)
            dv_ref[keys, :] += lax.dot_general(p.astype(BF16), dobf, TN, preferred_element_type=F32)
            return carry

        lax.fori_loop(0, ROWS, row, 0, unroll=2)

    full = pl.BlockSpec((S, HD), lambda h: (0, h))
    slot = pl.BlockSpec((S, HD), lambda h: (0, h))
    tab = pl.BlockSpec((None, N_OFF, GRID_W, KEYS_B), lambda h: (h, 0, 0, 0))
    shape = jax.ShapeDtypeStruct((S, D_BR), F32)
    return pl.pallas_call(
        body, name="attn_b_bwd", grid=(4,),
        in_specs=[full, full, full, tab, slot, slot, pl.BlockSpec((None, S, 1), lambda h: (h, 0, 0))],
        out_specs=[slot, slot, slot, tab],
        out_shape=[shape, shape, shape, jax.ShapeDtypeStruct((4, N_OFF, GRID_W, KEYS_B), F32)],
        compiler_params=_params(("parallel",)),
    )(qn, kn, vb, bias_tab, ob, dob, lse)


def _epi_relu_sq(acc, ex, outs):
    u = jnp.maximum(acc, 0.0)
    outs[0][...] = u.astype(BF16)
    outs[1][...] = (u * u).astype(BF16)


def _epi_relu_sq_bwd(acc, ex, outs):
    outs[0][...] = (acc * (2.0 * ex[0][...].astype(F32))).astype(BF16)


def _epi_loss_head(acc, ex, outs):
    e = acc + ex[0][...] - ex[1][...]
    dy = e * (1.0 / D)
    outs[0][...] = dy
    outs[1][...] = dy.astype(BF16)
    part = (0.5 / D) * jnp.sum(jnp.sum(e * e, axis=-1, keepdims=True), axis=0, keepdims=True)
    first = (pl.program_id(0) == 0) & (pl.program_id(1) == 0)

    @pl.when(first)
    def _():
        outs[2][...] = part

    @pl.when(jnp.logical_not(first))
    def _():
        outs[2][...] += part


def _local_step(x, target, norm_mix, b_gate, gains, rpb_pad, norm_ffn,
                w_in, w_pa, w_pb, w_out, w_up, w_down, weight_grads, riders=lambda name: None):
    def ridden(name, *args, **kwargs):
        ride = riders(name)
        if ride is None:
            return _mm_nt(*args, name=name, **kwargs)
        out, rode = _mm_nt(*args, name=name, rider=ride[0], **kwargs)
        ride[1](rode)
        return out

    cos2, sin2 = _rope_tables()
    expand, keep, sel = _bias_constants()
    w_out3 = w_out[None]

    xn, rstd1 = _rms_fwd(x, norm_mix, name="rms_mix")
    proj = _mm_nn(xn, w_in, tm=1024, tn=1280, name="proj")
    qkv_a, qkv_b = _qk_prep(proj, gains, cos2, sin2)
    fwd_a = [_attn_a_fwd(*qkv_a[gi], gi) for gi in range(3)]
    oa, lse_a = _combine_a([o for o, _ in fwd_a], [l for _, l in fwd_a])
    bias_tab = _rows_to_tab(_bias_expand(rpb_pad, expand, keep, sel))
    ob, lse_b = _attn_b_fwd(*qkv_b, bias_tab)
    mixed, ya, yb = _mix_fwd(oa, ob, w_pa, w_pb, proj, b_gate)
    h1 = _mm_nn(mixed, w_out3, tm=1024, tn=1024, name="out_proj", epi=_epi_residual, extra=(x,))
    hn, rstd2 = _rms_fwd(h1, norm_ffn, name="rms_ffn")
    u, usq = _mm_nn(hn, w_up, tm=1024, tn=1024, name="ffn_up", epi=_epi_relu_sq,
                    out_dtypes=(BF16, BF16))
    dy, dyb, loss = _mm_nn(usq, w_down[0], tm=512, tn=512, name="ffn_down_0", epi=_epi_loss_head,
                           extra=(h1, target), out_dtypes=(F32, BF16), total=True, width=D)
    dy, dyb, loss_1 = _mm_nn(usq, w_down[1], tm=512, tn=512, name="ffn_down_1", epi=_epi_loss_head,
                             extra=(h1, target), out_dtypes=(F32, BF16), total=True, width=D,
                             col0=D // 2, into=(dy, dyb))
    loss = loss + loss_1

    sent = weight_grads("w_down", {5: (usq, dyb)})
    du = _mm_nt(dyb, w_down[0], more_b=(w_down[1],), tm=1024, tn=1024, name="ffn_down_bwd", out_dtype=BF16,
                epi=_epi_relu_sq_bwd, extra=(u,), after=sent)
    sent = weight_grads("w_up", {4: (hn, du)})
    dhn = ridden("ffn_up_bwd", du, w_up, tm=512, tn=512, after=sent)
    dh1, dh1b, g_norm_ffn = _rms_bwd(dhn, h1, rstd2, norm_ffn, dy, name="rms_ffn_bwd", bf16_copy=True)

    dya, dyb2, dproj, g_b = _mix_bwd(dh1b, w_out, proj, b_gate, ya, yb)
    sent = weight_grads("w_mix", {3: (mixed, dh1b), 1: (oa, dya), 2: (ob, dyb2)})
    dob = _mm_nt(dyb2, w_pb, tm=1024, tn=D_BR, name="proj_b_bwd", after=sent)
    prep = _proj_a_bwd(dya, w_pa, oa, lse_a)
    grads_a = [_attn_a_bwd(*qkv_a[gi], *prep[gi], gi) for gi in range(3)]
    dqb, dkb, dvb, dbias = _attn_b_bwd(*qkv_b, bias_tab, ob, dob, lse_b)
    g_rpb = _bias_reduce(dbias)
    dproj, g_gains = _qk_prep_bwd(dproj, proj, gains, cos2, sin2, grads_a, (dqb, dkb, dvb))
    sent = weight_grads("w_in", {0: (xn, dproj)})
    dxn = ridden("proj_bwd", dproj, w_in, tm=256, tn=512, after=sent)
    grad_x, g_norm_mix = _rms_bwd(dxn, x, rstd1, norm_mix, dh1, name="rms_mix_bwd", bf16_copy=False)

    small = (g_norm_mix, g_b, g_gains, g_rpb, g_norm_ffn)
    return loss, grad_x, small


def _cast_bf16(w, *, part=0, parts=1, after=(), tr=256):
    rows, cols = w.shape[0], w.shape[1] // parts
    tr = min(tr, rows)

    def body(w_ref, *rest):
        rest[-1][...] = w_ref[...].astype(BF16)

    return pl.pallas_call(
        body, name=f"cast_{rows}x{cols}_{part}", grid=(rows // tr,),
        in_specs=[pl.BlockSpec((tr, cols), lambda i: (i, part))] + [pl.BlockSpec(memory_space=pl.ANY)] * len(after),
        out_specs=pl.BlockSpec((tr, cols), lambda i: (i, 0)),
        out_shape=jax.ShapeDtypeStruct((rows, cols), BF16), compiler_params=_params(("parallel",)),
    )(w, *after)


def _me_and_peers():
    x, y, c = lax.axis_index("x"), lax.axis_index("y"), lax.axis_index("c")
    me = 4 * x + 2 * y + c
    peers = []
    for k in range(1, N_DEV):
        px = 1 - x if k & 4 else x
        py = 1 - y if k & 2 else y
        pc = 1 - c if k & 1 else c
        peers.append(((px, py, pc), 4 * px + 2 * py + pc))
    return me, peers


def _gather_on_sequencer(shards, name):
    n = len(shards)
    hbm = pltpu.MemorySpace.HBM
    ins = [jax.new_ref(s, memory_space=hbm) for s in shards]
    outs = [jax.empty_ref(jax.ShapeDtypeStruct((N_DEV,) + s.shape, s.dtype), memory_space=hbm) for s in shards]

    @_sequencer(name, ((n, N_DEV - 1), (n, N_DEV - 1), (n,)), 0)
    def launch(send, recv, lsem):
        x, y, c = lax.axis_index("x"), lax.axis_index("y"), lax.axis_index("c")
        me, sibling = (x, y, c), (x, y, 1 - c)
        chips = [(1 - x, y), (x, 1 - y), (1 - x, 1 - y)]
        _handshake([sibling] + [(*chip, c) for chip in chips])

        def copy(w, k, block, to, src=None):
            px, py, pc = block
            dst = outs[w].at[4 * px + 2 * py + pc]
            return pltpu.make_async_remote_copy(dst if src is None else src, dst, send.at[w, k], recv.at[w, k],
                                                device_id=to, device_id_type=MESH)

        local = [pltpu.make_async_copy(ins[w], outs[w].at[4 * x + 2 * y + c], lsem.at[w]) for w in range(n)]
        for cp in local:
            cp.start()
        first = []
        for w in range(n):
            first += [copy(w, 1 + j, me, (*chip, c), src=ins[w]) for j, chip in enumerate(chips)]
            first.append(copy(w, 0, me, sibling, src=ins[w]))
        for cp in first:
            cp.start()
        passed = []
        for w in range(n):
            for j, chip in enumerate(chips):
                copy(w, 1 + j, (*chip, c), me).wait_recv()
                cp = copy(w, 4 + j, (*chip, c), sibling)
                cp.start()
                passed.append(cp)
        for w in range(n):
            copy(w, 0, sibling, me).wait_recv()
            for j, chip in enumerate(chips):
                copy(w, 4 + j, (*chip, 1 - c), me).wait_recv()
        for cp in first + passed:
            cp.wait_send()
        for cp in local:
            cp.wait()

    launch()
    return [o[...] for o in outs]


N_CHIP = 4
CHIPS = ((0, 0), (0, 1), (1, 0), (1, 1))


def _sequencer(name, n_sems, collective_id):
    return functools.partial(
        pl.kernel, mesh=plsc.ScalarSubcoreMesh(axis_name="seq", num_cores=1), name=name,
        scratch_types=tuple(pltpu.SemaphoreType.DMA(s) for s in n_sems),
        compiler_params=pltpu.CompilerParams(collective_id=collective_id))


def _handshake(peers):
    barrier = pltpu.get_barrier_semaphore()
    for peer in peers:
        pl.semaphore_signal(barrier, inc=1, device_id=peer, device_id_type=MESH)
    pl.semaphore_wait(barrier, len(peers))


def _chip_exchange_on_sequencer(parts, name):
    n = len(parts)
    hbm = pltpu.MemorySpace.HBM
    ins = [jax.new_ref(p, memory_space=hbm) for p in parts]
    outs = [jax.empty_ref(jax.ShapeDtypeStruct(p.shape, p.dtype), memory_space=hbm) for p in parts]

    @_sequencer(name, ((n, 3), (n, 3), (n,)), 2)
    def launch(send, recv, lsem):
        x, y, c = lax.axis_index("x"), lax.axis_index("y"), lax.axis_index("c")
        mine = 2 * x + y
        chips = [(1 - x, y), (x, 1 - y), (1 - x, 1 - y)]
        _handshake([(*chip, c) for chip in chips])
        local = [pltpu.make_async_copy(ins[w].at[mine], outs[w].at[mine], lsem.at[w]) for w in range(n)]
        for cp in local:
            cp.start()
        sends = []
        for w in range(n):
            for j, (px, py) in enumerate(chips):
                cp = pltpu.make_async_remote_copy(ins[w].at[2 * px + py], outs[w].at[mine],
                                                  send.at[w, j], recv.at[w, j],
                                                  device_id=(px, py, c), device_id_type=MESH)
                cp.start()
                sends.append(cp)
        for w in range(n):
            for j, (px, py) in enumerate(chips):
                pltpu.make_async_remote_copy(ins[w].at[mine], outs[w].at[2 * px + py],
                                             send.at[w, j], recv.at[w, j],
                                             device_id=(px, py, c), device_id_type=MESH).wait_recv()
        for cp in sends:
            cp.wait_send()
        for cp in local:
            cp.wait()

    launch()
    return [o[...] for o in outs]


GRAD_TILES = (dict(blocks_on="cols", tm=512, tn=1280), dict(blocks_on="cols", tm=512, tn=256),
              dict(blocks_on="cols", tm=512, tn=256), dict(blocks_on="rows", tm=256, tn=2048),
              dict(blocks_on="cols", tm=1024, tn=1024), dict(blocks_on="rows", tm=1024, tn=1024))


def _mm_tn_pair(a, b, *, blocks_on, tm, tn, name):
    t_len, m = a.shape
    n = b.shape[1]
    if blocks_on == "rows":
        rows, cols, inner = m // N_DEV, n, n // tn
        assert tm == rows
        a_spec = pl.BlockSpec((t_len, tm), lambda p, t, blk: (0, blk[p]))
        b_spec = pl.BlockSpec((t_len, tn), lambda p, t, blk: (0, t))
        out_spec = pl.BlockSpec((None, tm, tn), lambda p, t, blk: (
            jnp.maximum(p - N_CHIP, 0), 0, jnp.where(p < N_CHIP, 0, t)))
    else:
        rows, cols, inner = m, n // N_DEV, m // tm
        assert tn == cols
        a_spec = pl.BlockSpec((t_len, tm), lambda p, t, blk: (0, t))
        b_spec = pl.BlockSpec((t_len, tn), lambda p, t, blk: (0, blk[p]))
        out_spec = pl.BlockSpec((None, tm, tn), lambda p, t, blk: (
            jnp.maximum(p - N_CHIP, 0), jnp.where(p < N_CHIP, 0, t), 0))

    def body(blk_ref, a_ref, b_ref, o_ref, land, stage, send_sem, recv_sem):
        del blk_ref
        p, t = pl.program_id(0), pl.program_id(1)
        step = p * inner + t
        x, y, c = lax.axis_index("x"), lax.axis_index("y"), lax.axis_index("c")
        tile = _dot(a_ref[...], b_ref[...], TN)

        def to_sibling(slot, chip, piece):
            return pltpu.make_async_remote_copy(stage.at[slot], land.at[chip, piece], send_sem.at[slot],
                                                recv_sem.at[chip, piece],
                                                device_id=(x, y, 1 - c), device_id_type=MESH)

        @pl.when(p < N_CHIP)
        def _():
            slot = step % 2

            @pl.when(step >= 2)
            def _():
                to_sibling(slot, 0, 0).wait_send()

            stage[slot] = tile.astype(BF16)
            to_sibling(slot, p, t).start()

        @pl.when(step == N_CHIP * inner)
        def _():
            for slot in range(min(2, N_CHIP * inner)):
                to_sibling(slot, 0, 0).wait_send()

        @pl.when(p >= N_CHIP)
        def _():
            chip = p - N_CHIP
            to_sibling(0, chip, t).wait_recv()
            o_ref[...] = (tile + land[chip, t].astype(F32)).astype(BF16)

    c = lax.axis_index("c")
    order = jnp.stack([2 * ch + 1 - c for ch in range(N_CHIP)] + [2 * ch + c for ch in range(N_CHIP)])
    return pl.pallas_call(
        body, name=name,
        grid_spec=pltpu.PrefetchScalarGridSpec(
            num_scalar_prefetch=1, grid=(N_DEV, inner), in_specs=[a_spec, b_spec], out_specs=out_spec,
            scratch_shapes=[pltpu.VMEM((N_CHIP, inner, tm, tn), BF16), pltpu.VMEM((2, tm, tn), BF16),
                            pltpu.SemaphoreType.DMA((2,)), pltpu.SemaphoreType.DMA((N_CHIP, inner))]),
        out_shape=jax.ShapeDtypeStruct((N_CHIP, rows, cols), BF16),
        compiler_params=_params(("arbitrary", "arbitrary")),
    )(order.astype(jnp.int32), a, b)


def _adamw_math(g, w, m, v):
    m2 = B1 * m + (1.0 - B1) * g
    v2 = B2 * v + (1.0 - B2) * (g * g)
    delta = -LR * ((m2 / BC1) / (jnp.sqrt(v2 / BC2) + AEPS) + WD * w)
    return delta, m2, v2


def _adamw_block(ins, outs):
    p_ref, w_ref, m_ref, v_ref = ins
    g = p_ref[0].astype(F32)
    for b in range(1, N_CHIP):
        g = g + p_ref[b].astype(F32)
    delta, m2, v2 = _adamw_math(g, w_ref[...], m_ref[...], v_ref[...])
    for ref, val in zip(outs, (g, delta, m2, v2)):
        ref[...] = val


class _Rider(NamedTuple):
    inputs: tuple
    in_specs: list
    out_shape: list
    out_specs: list
    body: Callable


def _adamw_rider(parts, w, m, v):
    rows, cols = w.shape

    def rider(steps, step_of):
        rr = rows // steps
        blk = pl.BlockSpec((rr, cols), lambda *ids: (step_of(*ids[:2]), 0))
        chips = pl.BlockSpec((N_CHIP, rr, cols), lambda *ids: (0, step_of(*ids[:2]), 0))
        shape = jax.ShapeDtypeStruct((rows, cols), F32)
        return _Rider((parts, w, m, v), [chips, blk, blk, blk], [shape] * 4, [blk] * 4, _adamw_block)

    return rider


def _adamw(parts, w, m, v, *, name, after=(), tr=256):
    rows, cols = w.shape

    def body(*refs):
        _adamw_block(refs[:4], refs[4 + len(after):])

    spec = pl.BlockSpec((tr, cols), lambda i: (i, 0))
    shape = jax.ShapeDtypeStruct((rows, cols), F32)
    return pl.pallas_call(
        body, name=name, grid=(rows // tr,),
        in_specs=[pl.BlockSpec((N_CHIP, tr, cols), lambda i: (0, i, 0)), spec, spec, spec]
        + [pl.BlockSpec(memory_space=pl.ANY)] * len(after),
        out_specs=[spec] * 4, out_shape=[shape] * 4,
        compiler_params=_params(("parallel",)),
    )(parts, w, m, v, *after)


def _small_update(part, w, m, v):
    rows = part.shape[0]

    def body(p_ref, w_ref, m_ref, v_ref, g_ref, d_ref, mo_ref, vo_ref, buf, send, recv):
        me, peers = _me_and_peers()
        buf[me] = p_ref[...]
        sends = []
        for k, (dev, _) in enumerate(peers):
            cp = pltpu.make_async_remote_copy(p_ref, buf.at[me], send.at[k], recv.at[k],
                                              device_id=dev, device_id_type=MESH)
            cp.start()
            sends.append(cp)
        for k, (dev, idx) in enumerate(peers):
            pltpu.make_async_remote_copy(p_ref, buf.at[idx], send.at[k], recv.at[k],
                                         device_id=dev, device_id_type=MESH).wait_recv()
        for cp in sends:
            cp.wait_send()
        g = buf[0]
        for b in range(1, N_DEV):
            g = g + buf[b]
        delta, m2, v2 = _adamw_math(g, w_ref[...], m_ref[...], v_ref[...])
        g_ref[...] = g
        d_ref[...] = delta
        mo_ref[...] = m2
        vo_ref[...] = v2

    vm = pl.BlockSpec(memory_space=pltpu.VMEM)
    shape = jax.ShapeDtypeStruct((rows, HD), F32)
    return pl.pallas_call(
        body, name="small_params_update",
        in_specs=[vm] * 4, out_specs=[vm] * 4, out_shape=[shape] * 4,
        scratch_shapes=[pltpu.VMEM((N_DEV, rows, HD), F32),
                        pltpu.SemaphoreType.DMA((N_DEV - 1,)), pltpu.SemaphoreType.DMA((N_DEV - 1,))],
    )(part, w, m, v)


def _pack_small(norm_mix, b_gate, qa, ka, qb, kb, rpb, norm_ffn):
    gains = jnp.concatenate([qa, ka, qb, kb, jnp.zeros((4, HD), F32)], axis=0)
    rpb_pad = jnp.pad(rpb.reshape(4 * (2 * WIN_R - 1), 2 * WIN_C - 1), ((0, 4), (0, HD - (2 * WIN_C - 1))))
    return jnp.concatenate([norm_mix.reshape(16, HD), b_gate.reshape(32, HD), gains, rpb_pad,
                            norm_ffn.reshape(16, HD), jnp.zeros((8, HD), F32)], axis=0)


LOSS_ROW = 136


def _unpack_small(p):
    norm_mix = p[0:16].reshape(1, D)
    b_gate = p[16:48].reshape(1, 2 * D)
    qa, ka, qb, kb = (p[48 + i:49 + i] for i in range(4))
    rpb = p[56:116, :2 * WIN_C - 1].reshape(1, 4, 2 * WIN_R - 1, 2 * WIN_C - 1)
    norm_ffn = p[120:136].reshape(1, D)
    return norm_mix, b_gate, qa, ka, qb, kb, rpb, norm_ffn


def kernel(x, norm_mix, w_in, b_gate, q_norm_a, k_norm_a, q_norm_b, k_norm_b, rpb_b, w_proj_a, w_proj_b, w_out, norm_ffn, w_up, w_down, loss_target, m_norm_mix, m_w_in, m_b_gate, m_q_norm_a, m_k_norm_a, m_q_norm_b, m_k_norm_b, m_rpb_b, m_w_proj_a, m_w_proj_b, m_w_out, m_norm_ffn, m_w_up, m_w_down, v_norm_mix, v_w_in, v_b_gate, v_q_norm_a, v_k_norm_a, v_q_norm_b, v_k_norm_b, v_rpb_b, v_w_proj_a, v_w_proj_b, v_w_out, v_norm_ffn, v_w_up, v_w_down):
    big_w = (w_in[0], w_proj_a[0], w_proj_b[0], w_out[0], w_up[0], w_down[0])
    big_m = (m_w_in[0], m_w_proj_a[0], m_w_proj_b[0], m_w_out[0], m_w_up[0], m_w_down[0])
    big_v = (v_w_in[0], v_w_proj_a[0], v_w_proj_b[0], v_w_out[0], v_w_up[0], v_w_down[0])
    names = ("w_in", "w_proj_a", "w_proj_b", "w_out", "w_up", "w_down")

    shards = [_cast_bf16(w) for w in big_w[:5]]
    g_in, = _gather_on_sequencer(shards[0:1], "gather_w_in")
    g_pa, g_pb, g_out, g_up = _gather_on_sequencer(shards[1:5], "gather_w_mix_up")
    small_w = _pack_small(norm_mix, b_gate, q_norm_a, k_norm_a, q_norm_b, k_norm_b, rpb_b, norm_ffn)
    small_m = _pack_small(m_norm_mix, m_b_gate, m_q_norm_a, m_k_norm_a, m_q_norm_b, m_k_norm_b, m_rpb_b, m_norm_ffn)
    small_v = _pack_small(v_norm_mix, v_b_gate, v_q_norm_a, v_k_norm_a, v_q_norm_b, v_k_norm_b, v_rpb_b, v_norm_ffn)
    g_down = [_gather_on_sequencer([_cast_bf16(big_w[5], part=h, parts=2, after=(small_w, small_m, small_v) * h)],
                                   f"gather_w_down_{h}")[0].reshape(1, D_FF, D // 2) for h in range(2)]

    upd = [None] * 6
    in_flight = {}

    def weight_grads(tag, operands):
        sums = {i: _mm_tn_pair(a, b, name=f"grad_{names[i]}", **GRAD_TILES[i]) for i, (a, b) in operands.items()}
        new = list(sums.values())
        in_flight.update(zip(sums, _chip_exchange_on_sequencer(new, f"chip_exchange_{tag}")))
        return new

    def riders(name):
        i = {"proj_bwd": 5}.get(name)
        if i is None:
            return None
        return (_adamw_rider(in_flight.pop(i), big_w[i], big_m[i], big_v[i]),
                functools.partial(upd.__setitem__, i))

    loss, grad_x, small_g = _local_step(
        x[0], loss_target[0], norm_mix, b_gate, small_w[48:56], small_w[56:120], norm_ffn,
        g_in, g_pa, g_pb, g_out.reshape(D, D), g_up, g_down, weight_grads, riders)

    g_norm_mix, g_b, g_gains, g_rpb, g_norm_ffn = small_g
    small_part = jnp.concatenate([g_norm_mix.reshape(16, HD), g_b.reshape(32, HD),
                                  g_gains, g_rpb, g_norm_ffn.reshape(16, HD),
                                  jnp.pad(loss, ((0, 7), (0, HD - 1)))], axis=0)
    slabs = _small_update(small_part, small_w, small_m, small_v)
    total = slabs[0][LOSS_ROW, 0]
    s_g, s_d, s_m, s_v = (_unpack_small(t) for t in slabs)

    last = grad_x
    for i, r in in_flight.items():
        upd[i] = _adamw(r, big_w[i], big_m[i], big_v[i], name=f"adamw_{names[i]}", after=[last])
        last = upd[i][0]
    b_g, b_d, b_m, b_v = ([u[j][None] for u in upd] for j in range(4))

    def order(small, big):
        nm, bg, qa, ka, qb, kb, rpb, nf = small
        w_in_, pa_, pb_, out_, up_, down_ = big
        return (nm, w_in_, bg, qa, ka, qb, kb, rpb, pa_, pb_, out_, nf, up_, down_)

    return (total, grad_x[None], *order(s_g, b_g), *order(s_d, b_d), *order(s_m, b_m), *order(s_v, b_v))
```

```python
import functools
from typing import Callable, NamedTuple

import jax
import jax.numpy as jnp
import numpy as np
from jax import lax
from jax.experimental import pallas as pl
from jax.experimental.pallas import tpu as pltpu
from jax.experimental.pallas import tpu_sc as plsc

F32 = jnp.float32
BF16 = jnp.bfloat16

N_DEV = 8
S = 2048
D = 2048
HD = 128
NH = 16
NH_A = 12
QKV = NH * HD
D_IN = 3 * QKV + 2 * D
D_BR = 512
D_FF = 4 * D
GRID_W = 64
ROWS = S // GRID_W
WIN_R = 8
WIN_C = 16
EPS = 1e-6
NEG = -1e30
SCALE = HD ** -0.5
ROPE_THETA = 10000.0
DILATIONS = (1, 4, 16)
HALF_A = 64
QB = 128

LR, B1, B2, AEPS, WD, STEP = 0.001, 0.9, 0.999, 1e-08, 0.01, 10
BC1 = 1.0 - B1 ** STEP
BC2 = 1.0 - B2 ** STEP

VMEM_LIMIT = 56 * 1024 * 1024
MESH = pl.DeviceIdType.MESH

NN = (((1,), (0,)), ((), ()))
NT = (((1,), (1,)), ((), ()))
TN = (((0,), (0,)), ((), ()))


def _params(sem):
    return pltpu.CompilerParams(dimension_semantics=sem, vmem_limit_bytes=VMEM_LIMIT)


def _matmul(a, b, *, product, grid, a_spec, b_spec, epi, out_shape, out_specs, name,
            extra=(), extra_specs=(), after=(), carried=False, rider=None, into=()):
    n_extra = len(extra)
    single = not isinstance(out_shape, (list, tuple))
    out_shape = [out_shape] if single else list(out_shape)
    out_specs = [out_specs] if single else list(out_specs)
    ride = rider(grid[0] * grid[1], lambda j, i: j * grid[1] + i) if rider else None
    r_in = list(ride.inputs) if ride else []
    n_main = len(out_shape)

    def body(a_ref, b_ref, *rest):
        n_in = n_extra + len(after) + len(r_in)
        ins, outs = rest[:n_in], rest[n_in + len(into):]
        epi(product(a_ref, b_ref, ins[:n_extra]), ins[:n_extra], outs[:n_main])
        if ride:
            ride.body(ins[n_extra + len(after):], outs[n_main:])

    res = pl.pallas_call(
        body, name=name, grid=grid,
        in_specs=[a_spec, b_spec, *extra_specs, *[pl.BlockSpec(memory_space=pl.ANY)] * len(after),
                  *(ride.in_specs if ride else []), *[pl.BlockSpec(memory_space=pl.ANY)] * len(into)],
        out_specs=out_specs + (ride.out_specs if ride else []),
        out_shape=out_shape + (ride.out_shape if ride else []),
        input_output_aliases={2 + n_extra + len(after) + len(r_in) + k: k for k in range(len(into))},
        compiler_params=_params(("arbitrary", "arbitrary") if carried else ("parallel", "parallel")),
    )(a, b, *extra, *after, *r_in, *into)
    main = res[0] if single else res[:n_main]
    return (main, res[n_main:]) if ride else main


def _dot(x, y, dims):
    return lax.dot_general(x, y, dims, preferred_element_type=F32)


def _epi_store(acc, ex, outs):
    outs[0][...] = acc.astype(outs[0].dtype)


def _epi_residual(acc, ex, outs):
    outs[0][...] = acc + ex[0][...]


def _mm_nn(a, b3, *, tm, tn, name, out_dtypes=(F32,), epi=_epi_store, extra=(), total=False,
           col0=0, width=None, into=()):
    m, kdim = a.shape
    g, _, ng = b3.shape
    n = g * ng
    c0 = col0 // tn
    if tn <= ng:
        npg = ng // tn
        b_spec = pl.BlockSpec((None, kdim, tn), lambda j, i: (j // npg, 0, j % npg))

        def product(a_ref, b_ref, ex):
            return _dot(a_ref[...], b_ref[...], NN)
    else:
        gb = tn // ng
        b_spec = pl.BlockSpec((gb, kdim, ng), lambda j, i: (j, 0, 0))

        def product(a_ref, b_ref, ex):
            return jnp.concatenate([_dot(a_ref[...], b_ref[q], NN) for q in range(gb)], axis=1)

    tile = pl.BlockSpec((tm, tn), lambda j, i: (i, j + c0))
    shapes = [jax.ShapeDtypeStruct((m, width or n), dt) for dt in out_dtypes]
    specs = [tile] * len(shapes)
    if total:
        shapes.append(jax.ShapeDtypeStruct((1, 1), F32))
        specs.append(pl.BlockSpec((1, 1), lambda j, i: (0, 0)))
    single = len(shapes) == 1
    return _matmul(
        a, b3, product=product, grid=(n // tn, m // tm), epi=epi, name=name, carried=total, into=into,
        a_spec=pl.BlockSpec((tm, kdim), lambda j, i: (i, 0)), b_spec=b_spec,
        extra=extra, extra_specs=[tile] * len(extra),
        out_shape=shapes[0] if single else shapes, out_specs=specs[0] if single else specs)


def _mm_nt(a, b3, *, tm, tn, name, out_dtype=F32, epi=_epi_store, extra=(), after=(), rider=None, more_b=()):
    m, kdim = a.shape
    _, n, _ = b3.shape
    n_b = len(more_b)

    def product(a_ref, b_ref, ex):
        acc, k0 = None, 0
        for ref in (b_ref, *ex[:n_b]):
            for q in range(ref.shape[0]):
                part = _dot(a_ref[:, k0:k0 + ref.shape[2]], ref[q], NT)
                acc = part if acc is None else acc + part
                k0 += ref.shape[2]
        return acc

    def write(acc, ex, outs):
        epi(acc, ex[n_b:], outs)

    def w_spec(w):
        return pl.BlockSpec((w.shape[0], tn, w.shape[2]), lambda j, i: (0, j, 0))

    tile = pl.BlockSpec((tm, tn), lambda j, i: (i, j))
    return _matmul(
        a, b3, product=product, grid=(n // tn, m // tm), epi=write, name=name,
        a_spec=pl.BlockSpec((tm, kdim), lambda j, i: (i, 0)), b_spec=w_spec(b3),
        extra=(*more_b, *extra), extra_specs=[w_spec(w) for w in more_b] + [tile] * len(extra),
        after=after, rider=rider,
        out_shape=jax.ShapeDtypeStruct((m, n), out_dtype), out_specs=tile)


def _mm_tn(a, b, *, tm, tn, name, groups=1, out_dtype=BF16):
    t, m = a.shape
    _, n = b.shape
    ng = n // groups
    if tn <= ng:
        npg = ng // tn
        out_spec = pl.BlockSpec((None, tm, tn), lambda j, i: (j // npg, i, j % npg))
        epi = _epi_store

        def product(a_ref, b_ref, ex):
            return _dot(a_ref[...], b_ref[...], TN)
    else:
        gb = tn // ng
        out_spec = pl.BlockSpec((gb, tm, ng), lambda j, i: (j, i, 0))

        def product(a_ref, b_ref, ex):
            return [_dot(a_ref[...], b_ref[:, q * ng:(q + 1) * ng], TN) for q in range(gb)]

        def epi(parts, ex, outs):
            for q, part in enumerate(parts):
                outs[0][q] = part.astype(out_dtype)

    return _matmul(
        a, b, product=product, grid=(n // tn, m // tm), epi=epi, name=name,
        a_spec=pl.BlockSpec((t, tm), lambda j, i: (0, i)),
        b_spec=pl.BlockSpec((t, tn), lambda j, i: (0, j)),
        out_shape=jax.ShapeDtypeStruct((groups, m, ng), out_dtype), out_specs=out_spec)


def _rms_fwd(x, g, *, name, tr=256):
    def body(x_ref, g_ref, y_ref, r_ref):
        xv = x_ref[...]
        r = lax.rsqrt(jnp.mean(xv * xv, axis=-1, keepdims=True) + EPS)
        y_ref[...] = (xv * r * g_ref[...]).astype(BF16)
        r_ref[...] = r

    row = pl.BlockSpec((tr, D), lambda i: (i, 0))
    return pl.pallas_call(
        body, name=name, grid=(S // tr,),
        in_specs=[row, pl.BlockSpec((1, D), lambda i: (0, 0))],
        out_specs=[row, pl.BlockSpec((tr, 1), lambda i: (i, 0))],
        out_shape=[jax.ShapeDtypeStruct((S, D), BF16), jax.ShapeDtypeStruct((S, 1), F32)],
        compiler_params=_params(("parallel",)),
    )(x, g)


def _rms_bwd(dy, x, rstd, g, resid, *, name, bf16_copy, tr=256):
    def body(dy_ref, x_ref, r_ref, g_ref, res_ref, dx_ref, *rest):
        dg_ref = rest[-1]
        r = r_ref[...]
        xh = x_ref[...] * r
        dyv = dy_ref[...]
        t = dyv * g_ref[...]
        dx = r * (t - xh * jnp.mean(t * xh, axis=-1, keepdims=True)) + res_ref[...]
        dx_ref[...] = dx
        if bf16_copy:
            rest[0][...] = dx.astype(BF16)
        part = jnp.sum(dyv * xh, axis=0, keepdims=True)

        @pl.when(pl.program_id(0) == 0)
        def _():
            dg_ref[...] = part

        @pl.when(pl.program_id(0) > 0)
        def _():
            dg_ref[...] += part

    row = pl.BlockSpec((tr, D), lambda i: (i, 0))
    vec = pl.BlockSpec((1, D), lambda i: (0, 0))
    return pl.pallas_call(
        body, name=name, grid=(S // tr,),
        in_specs=[row, row, pl.BlockSpec((tr, 1), lambda i: (i, 0)), vec, row],
        out_specs=[row] + [row] * bf16_copy + [vec],
        out_shape=[jax.ShapeDtypeStruct((S, D), F32)] + [jax.ShapeDtypeStruct((S, D), BF16)] * bf16_copy
        + [jax.ShapeDtypeStruct((1, D), F32)],
        compiler_params=_params(("arbitrary",)),
    )(dy, x, rstd, g, resid)


def _rope_tables():
    pos = np.arange(S, dtype=np.float32)
    inv = (ROPE_THETA ** (-np.arange(0, HD, 2, dtype=np.float32) / HD)).astype(np.float32)
    ang = pos[:, None] * inv[None, :]
    cos, sin = np.cos(ang), np.sin(ang)
    return (jnp.asarray(np.concatenate([cos, cos], axis=-1), F32),
            jnp.asarray(np.concatenate([-sin, sin], axis=-1), F32))


def _swap_halves(t):
    return pltpu.roll(t, HD // 2, axis=1)


TOK = 256


def _lane_block_spec(d, last=HD):
    return pl.BlockSpec((4, TOK // d, d * last), lambda i: (0, i, 0))


def _to_lane_blocks(dst, head, val, d, scr, dtype):
    w = val.shape[1]
    if d == 1:
        dst[head] = val.astype(dtype)
        return
    scr[...] = val
    for r in range(d):
        dst[head, :, r * w:(r + 1) * w] = scr[pl.ds(r, TOK // d, stride=d), :].astype(dtype)


def _from_lane_blocks(src, head, d, w, scr):
    if d == 1:
        return src[head].astype(F32)
    for r in range(d):
        scr[pl.ds(r, TOK // d, stride=d), :] = src[head, :, r * w:(r + 1) * w].astype(F32)
    return scr[...]


def _qk_prep(proj, gains, cos2, sin2):
    def body(q_ref, k_ref, v_ref, g_ref, c_ref, s_ref, *rest):
        outs, scr = rest[:-1], rest[-1]
        cos, sin = c_ref[...], s_ref[...]
        for which, (src, row_a, row_b) in enumerate(((q_ref, 0, 2), (k_ref, 1, 3), (v_ref, None, None))):
            for h in range(NH):
                y = src[:, h * HD:(h + 1) * HD]
                if row_a is not None:
                    y = y * lax.rsqrt(jnp.mean(y * y, axis=-1, keepdims=True) + EPS)
                    if h < NH_A:
                        y = y * g_ref[row_a:row_a + 1, :]
                        y = y * cos + _swap_halves(y) * sin
                    else:
                        y = y * g_ref[row_b:row_b + 1, :]
                if h < NH_A:
                    gi = h // 4
                    _to_lane_blocks(outs[3 * gi + which], h % 4, y, DILATIONS[gi], scr, BF16)
                else:
                    hb = h - NH_A
                    outs[9 + which][:, hb * HD:(hb + 1) * HD] = y.astype(BF16)

    def blk(c):
        return pl.BlockSpec((TOK, QKV), lambda i: (i, c))
    tab = pl.BlockSpec((TOK, HD), lambda i: (i, 0))
    out_specs, out_shape = [], []
    for d in DILATIONS:
        out_specs += [_lane_block_spec(d)] * 3
        out_shape += [jax.ShapeDtypeStruct((4, S // d, d * HD), BF16)] * 3
    out_specs += [pl.BlockSpec((TOK, D_BR), lambda i: (i, 0))] * 3
    out_shape += [jax.ShapeDtypeStruct((S, D_BR), BF16)] * 3
    outs = pl.pallas_call(
        body, name="qk_prep", grid=(S // TOK,),
        in_specs=[blk(0), blk(1), blk(2), pl.BlockSpec((8, HD), lambda i: (0, 0)), tab, tab],
        out_specs=out_specs, out_shape=out_shape,
        scratch_shapes=[pltpu.VMEM((TOK, HD), F32)],
        compiler_params=_params(("parallel",)),
    )(proj, proj, proj, gains, cos2, sin2)
    return [tuple(outs[3 * gi:3 * gi + 3]) for gi in range(3)], tuple(outs[9:12])


def _qk_prep_bwd(dproj, proj, gains, cos2, sin2, grads_a, grads_b):
    def body(dp_in, q_ref, k_ref, g_ref, c_ref, s_ref, *rest):
        grads, (dp_out, dg_ref, scr) = rest[:12], rest[12:]
        del dp_in
        cos, sin = c_ref[...], s_ref[...]

        def grad_of(which, h):
            if h < NH_A:
                gi = h // 4
                return _from_lane_blocks(grads[3 * gi + which], h % 4, DILATIONS[gi], HD, scr)
            hb = h - NH_A
            return grads[9 + which][:, hb * HD:(hb + 1) * HD]

        dg_rows = []
        for which, (src, base, row_a, row_b) in enumerate(((q_ref, 0, 0, 2), (k_ref, QKV, 1, 3))):
            dg_a = jnp.zeros((1, HD), F32)
            dg_b = jnp.zeros((1, HD), F32)
            for h in range(NH):
                t = src[:, h * HD:(h + 1) * HD]
                dy = grad_of(which, h)
                r = lax.rsqrt(jnp.mean(t * t, axis=-1, keepdims=True) + EPS)
                xh = t * r
                if h < NH_A:
                    dy = dy * cos - _swap_halves(dy) * sin
                    gain = g_ref[row_a:row_a + 1, :]
                    dg_a = dg_a + jnp.sum(dy * xh, axis=0, keepdims=True)
                else:
                    gain = g_ref[row_b:row_b + 1, :]
                    dg_b = dg_b + jnp.sum(dy * xh, axis=0, keepdims=True)
                u = dy * gain
                dx = r * (u - xh * jnp.mean(u * xh, axis=-1, keepdims=True))
                dp_out[:, base + h * HD:base + (h + 1) * HD] = dx.astype(BF16)
            dg_rows += [(row_a, dg_a), (row_b, dg_b)]
        for h in range(NH):
            dp_out[:, 2 * QKV + h * HD:2 * QKV + (h + 1) * HD] = grad_of(2, h).astype(BF16)

        @pl.when(pl.program_id(0) == 0)
        def _():
            dg_ref[...] = jnp.zeros((8, HD), F32)

        for row, val in dg_rows:
            dg_ref[row:row + 1, :] += val

    def blk(c):
        return pl.BlockSpec((TOK, QKV), lambda i: (i, c))
    tab = pl.BlockSpec((TOK, HD), lambda i: (i, 0))
    gain_spec = pl.BlockSpec((8, HD), lambda i: (0, 0))
    grad_specs = [s for d in DILATIONS for s in [_lane_block_spec(d)] * 3]
    grad_specs += [pl.BlockSpec((TOK, D_BR), lambda i: (i, 0))] * 3
    return pl.pallas_call(
        body, name="qk_prep_bwd", grid=(S // TOK,),
        in_specs=[pl.BlockSpec(memory_space=pl.ANY), blk(0), blk(1), gain_spec, tab, tab] + grad_specs,
        out_specs=[pl.BlockSpec((TOK, 3 * QKV), lambda i: (i, 0)), gain_spec],
        out_shape=[jax.ShapeDtypeStruct((S, D_IN), BF16), jax.ShapeDtypeStruct((8, HD), F32)],
        input_output_aliases={0: 0},
        scratch_shapes=[pltpu.VMEM((TOK, HD), F32)],
        compiler_params=_params(("arbitrary",)),
    )(dproj, proj, proj, gains, cos2, sin2, *[g for grp in grads_a for g in grp], *grads_b)


def _mix_fwd(oa, ob, w_pa, w_pb, proj, b_gate, *, tr=256):
    def body(oa_ref, ob_ref, pa_ref, pb_ref, la_ref, lb_ref, ba_ref, bb_ref, mix_ref, ya_ref, yb_ref):
        ya = jnp.concatenate([_dot(oa_ref[...], pa_ref[q], NN) for q in range(N_DEV)], axis=1)
        yb = jnp.concatenate([_dot(ob_ref[...], pb_ref[q], NN) for q in range(N_DEV)], axis=1)
        ga = jax.nn.sigmoid(la_ref[...] + ba_ref[...])
        gb = jax.nn.sigmoid(lb_ref[...] + bb_ref[...])
        mix_ref[...] = (ga * ya + gb * yb).astype(BF16)
        ya_ref[...] = ya.astype(BF16)
        yb_ref[...] = yb.astype(BF16)

    row = pl.BlockSpec((tr, D), lambda i: (i, 0))
    branch = pl.BlockSpec((tr, D_BR), lambda i: (i, 0))
    whole = pl.BlockSpec((N_DEV, D_BR, D // N_DEV), lambda i: (0, 0, 0))
    return pl.pallas_call(
        body, name="mix_fwd", grid=(S // tr,),
        in_specs=[branch, branch, whole, whole,
                  pl.BlockSpec((tr, D), lambda i: (i, 3)), pl.BlockSpec((tr, D), lambda i: (i, 4)),
                  pl.BlockSpec((1, D), lambda i: (0, 0)), pl.BlockSpec((1, D), lambda i: (0, 1))],
        out_specs=[row, row, row], out_shape=[jax.ShapeDtypeStruct((S, D), BF16)] * 3,
        compiler_params=_params(("parallel",)),
    )(oa, ob, w_pa, w_pb, proj, proj, b_gate, b_gate)


def _mix_bwd(dh1b, w_out, proj, b_gate, ya, yb, *, tr=256):
    def body(dh_ref, w_ref, la_ref, lb_ref, b_ref, ya_ref, yb_ref, dya_ref, dyb_ref, dp_ref, db_ref):
        dm = _dot(dh_ref[...], w_ref[...], NT)
        parts = []
        for l_ref, y_ref, dy_ref, lo in ((la_ref, ya_ref, dya_ref, 0), (lb_ref, yb_ref, dyb_ref, D)):
            g = jax.nn.sigmoid(l_ref[...] + b_ref[:, lo:lo + D])
            dy_ref[...] = (dm * g).astype(BF16)
            dl = dm * y_ref[...].astype(F32) * g * (1.0 - g)
            dp_ref[:, lo:lo + D] = dl.astype(BF16)
            parts.append(jnp.sum(dl, axis=0, keepdims=True))
        part = jnp.concatenate(parts, axis=1)

        @pl.when(pl.program_id(0) == 0)
        def _():
            db_ref[...] = part

        @pl.when(pl.program_id(0) > 0)
        def _():
            db_ref[...] += part

    row = pl.BlockSpec((tr, D), lambda i: (i, 0))
    vec = pl.BlockSpec((1, 2 * D), lambda i: (0, 0))
    gate_cols = pl.BlockSpec((pl.Element(tr), pl.Element(2 * D)), lambda i: (i * tr, 3 * QKV))
    return pl.pallas_call(
        body, name="mix_bwd", grid=(S // tr,),
        in_specs=[row, pl.BlockSpec((D, D), lambda i: (0, 0)),
                  pl.BlockSpec((tr, D), lambda i: (i, 3)), pl.BlockSpec((tr, D), lambda i: (i, 4)), vec, row, row],
        out_specs=[row, row, gate_cols, vec],
        out_shape=[jax.ShapeDtypeStruct((S, D), BF16), jax.ShapeDtypeStruct((S, D), BF16),
                   jax.ShapeDtypeStruct((S, D_IN), BF16), jax.ShapeDtypeStruct((1, 2 * D), F32)],
        compiler_params=_params(("arbitrary",)),
    )(dh1b, w_out, proj, proj, b_gate, ya, yb)


def _band_blocks(m_len):
    wk = min(m_len, QB + 2 * QB)
    return [(qb * QB, min(max(qb * QB - QB, 0), m_len - wk), wk) for qb in range(m_len // QB)]


def _band_scores(q, kw, q0, k0, wk):
    s = _dot(q, kw, NT) * SCALE
    qpos = q0 + lax.broadcasted_iota(jnp.int32, (QB, 1), 0)
    kpos = k0 + lax.broadcasted_iota(jnp.int32, (1, wk), 1)
    return jnp.where(jnp.abs(kpos - qpos) <= HALF_A, s, NEG)


def _attn_a_fwd(q, k, v, gi):
    d = DILATIONS[gi]
    m_len = S // d

    def body(q_ref, k_ref, v_ref, o_ref, lse_ref):
        for r in range(d):
            lanes = slice(r * HD, (r + 1) * HD)
            for q0, k0, wk in _band_blocks(m_len):
                s = _band_scores(q_ref[q0:q0 + QB, lanes], k_ref[k0:k0 + wk, lanes], q0, k0, wk)
                m = jnp.max(s, axis=-1, keepdims=True)
                p = jnp.exp(s - m)
                l = jnp.sum(p, axis=-1, keepdims=True)
                o_ref[q0:q0 + QB, lanes] = _dot(p.astype(BF16), v_ref[k0:k0 + wk, lanes], NN) / l
                lse_ref[q0:q0 + QB, r:r + 1] = m + jnp.log(l)

    head = pl.BlockSpec((None, m_len, d * HD), lambda h: (h, 0, 0))
    stat = pl.BlockSpec((None, m_len, d), lambda h: (h, 0, 0))
    return pl.pallas_call(
        body, name=f"attn_a_fwd_{gi}", grid=(4,),
        in_specs=[head, head, head], out_specs=[head, stat],
        out_shape=[jax.ShapeDtypeStruct((4, m_len, d * HD), F32), jax.ShapeDtypeStruct((4, m_len, d), F32)],
        compiler_params=_params(("parallel",)),
    )(q, k, v)


def _combine_a(os, lses):
    def body(o0, o1, o2, l0, l1, l2, oa_ref, lse_ref, scr, scr1):
        for h in range(4):
            o = [_from_lane_blocks(ref, h, d, HD, scr) for ref, d in zip((o0, o1, o2), DILATIONS)]
            a, b, c = (_from_lane_blocks(ref, h, d, 1, scr1) for ref, d in zip((l0, l1, l2), DILATIONS))
            m = jnp.maximum(jnp.maximum(a, b), c)
            wa, wb, wc = jnp.exp(a - m), jnp.exp(b - m), jnp.exp(c - m)
            tot = wa + wb + wc
            oa_ref[:, h * HD:(h + 1) * HD] = ((wa * o[0] + wb * o[1] + wc * o[2]) / tot).astype(BF16)
            lse_ref[h] = m + jnp.log(tot)

    return pl.pallas_call(
        body, name="combine_a", grid=(S // TOK,),
        in_specs=[_lane_block_spec(d) for d in DILATIONS] + [_lane_block_spec(d, 1) for d in DILATIONS],
        out_specs=[pl.BlockSpec((TOK, D_BR), lambda i: (i, 0)), pl.BlockSpec((4, TOK, 1), lambda i: (0, i, 0))],
        out_shape=[jax.ShapeDtypeStruct((S, D_BR), BF16), jax.ShapeDtypeStruct((4, S, 1), F32)],
        scratch_shapes=[pltpu.VMEM((TOK, HD), F32), pltpu.VMEM((TOK, 1), F32)],
        compiler_params=_params(("parallel",)),
    )(*os, *lses)


def _proj_a_bwd(dya, w_pa, oa, lse):
    kg = D // N_DEV

    def body(dy_ref, w_ref, o_ref, l_ref, *rest):
        outs, (scr, scr1) = rest[:9], rest[9:]
        doa = _dot(dy_ref[:, 0:kg], w_ref[0], NT)
        for q in range(1, N_DEV):
            doa = doa + _dot(dy_ref[:, q * kg:(q + 1) * kg], w_ref[q], NT)
        for h in range(4):
            do = doa[:, h * HD:(h + 1) * HD]
            dsum = jnp.sum(do * o_ref[:, h * HD:(h + 1) * HD].astype(F32), axis=-1, keepdims=True)
            for gi, d in enumerate(DILATIONS):
                _to_lane_blocks(outs[3 * gi], h, do, d, scr, BF16)
                _to_lane_blocks(outs[3 * gi + 1], h, l_ref[h], d, scr1, F32)
                _to_lane_blocks(outs[3 * gi + 2], h, dsum, d, scr1, F32)

    row = pl.BlockSpec((TOK, D_BR), lambda i: (i, 0))
    out_specs, out_shape = [], []
    for d in DILATIONS:
        out_specs += [_lane_block_spec(d), _lane_block_spec(d, 1), _lane_block_spec(d, 1)]
        out_shape += [jax.ShapeDtypeStruct((4, S // d, d * HD), BF16)] + [jax.ShapeDtypeStruct((4, S // d, d), F32)] * 2
    outs = pl.pallas_call(
        body, name="proj_a_bwd", grid=(S // TOK,),
        in_specs=[pl.BlockSpec((TOK, D), lambda i: (i, 0)),
                  pl.BlockSpec((N_DEV, D_BR, kg), lambda i: (0, 0, 0)),
                  row, pl.BlockSpec((4, TOK, 1), lambda i: (0, i, 0))],
        out_specs=out_specs, out_shape=out_shape,
        scratch_shapes=[pltpu.VMEM((TOK, HD), F32), pltpu.VMEM((TOK, 1), F32)],
        compiler_params=_params(("parallel",)),
    )(dya, w_pa, oa, lse)
    return [tuple(outs[3 * gi:3 * gi + 3]) for gi in range(3)]


def _attn_a_bwd(q, k, v, do, lse, dsum, gi):
    d = DILATIONS[gi]
    m_len = S // d

    def body(q_ref, k_ref, v_ref, do_ref, lse_ref, dsum_ref, dq_ref, dk_ref, dv_ref):
        dk_ref[...] = jnp.zeros((m_len, d * HD), F32)
        dv_ref[...] = jnp.zeros((m_len, d * HD), F32)
        for r in range(d):
            lanes = slice(r * HD, (r + 1) * HD)
            for q0, k0, wk in _band_blocks(m_len):
                rows, keys = slice(q0, q0 + QB), slice(k0, k0 + wk)
                qv, kw, vw, dov = q_ref[rows, lanes], k_ref[keys, lanes], v_ref[keys, lanes], do_ref[rows, lanes]
                p = jnp.exp(_band_scores(qv, kw, q0, k0, wk) - lse_ref[rows, r:r + 1])
                ds = (p * (_dot(dov, vw, NT) - dsum_ref[rows, r:r + 1]) * SCALE).astype(BF16)
                dq_ref[rows, lanes] = _dot(ds, kw, NN)
                dk_ref[keys, lanes] += _dot(ds, qv, TN)
                dv_ref[keys, lanes] += _dot(p.astype(BF16), dov, TN)

    head = pl.BlockSpec((None, m_len, d * HD), lambda h: (h, 0, 0))
    stat = pl.BlockSpec((None, m_len, d), lambda h: (h, 0, 0))
    shape = jax.ShapeDtypeStruct((4, m_len, d * HD), F32)
    return pl.pallas_call(
        body, name=f"attn_a_bwd_{gi}", grid=(4,),
        in_specs=[head, head, head, head, stat, stat], out_specs=[head, head, head],
        out_shape=[shape, shape, shape],
        compiler_params=_params(("parallel",)),
    )(q, k, v, do, lse, dsum)


KEYS_B = WIN_R * GRID_W
N_OFF = WIN_R


def _bias_constants():
    q = np.arange(GRID_W)[:, None]
    kc = np.arange(GRID_W)[None, :]
    dc = np.clip(kc - q, -(WIN_C - 1), WIN_C - 1) + (WIN_C - 1)
    expand = np.zeros((HD, GRID_W * GRID_W), np.float32)
    expand[dc.reshape(-1), np.arange(GRID_W * GRID_W)] = 1.0
    cs = np.clip(q - WIN_C // 2, 0, GRID_W - WIN_C)
    keep = ((kc >= cs) & (kc < cs + WIN_C)).reshape(1, -1).astype(np.float32)
    sel = np.zeros((64, 4 * N_OFF * WIN_R), np.float32)
    for h in range(4):
        for off in range(N_OFF):
            for j in range(WIN_R):
                sel[h * (2 * WIN_R - 1) + off + j, (h * N_OFF + off) * WIN_R + j] = 1.0
    return jnp.asarray(expand), jnp.asarray(keep), jnp.asarray(sel)


def _bias_expand(rpb_pad, expand, keep, sel):
    def body(r_ref, e_ref, k_ref, s_ref, o_ref):
        t = lax.dot_general(r_ref[...], e_ref[...], NN, precision=lax.Precision.HIGHEST,
                            preferred_element_type=F32)
        rows = lax.dot_general(s_ref[...], t, TN, precision=lax.Precision.HIGHEST,
                               preferred_element_type=F32)
        o_ref[...] = jnp.where(k_ref[...] > 0.5, rows, NEG)

    return pl.pallas_call(
        body, name="bias_expand",
        out_shape=jax.ShapeDtypeStruct((4 * N_OFF * WIN_R, GRID_W * GRID_W), F32),
        compiler_params=pltpu.CompilerParams(vmem_limit_bytes=VMEM_LIMIT),
    )(rpb_pad, expand, keep, sel)


def _bias_reduce(dbias_tab):
    lane0 = GRID_W - WIN_C
    flip = np.zeros((GRID_W, GRID_W), np.float32)
    flip[np.arange(GRID_W), GRID_W - 1 - np.arange(GRID_W)] = 1.0
    place = np.zeros((WIN_R, 64, 4 * N_OFF), np.float32)
    for j in range(WIN_R):
        for h in range(4):
            for off in range(N_OFF):
                place[j, h * (2 * WIN_R - 1) + off + j, h * N_OFF + off] = 1.0

    def exact(x, y):
        return lax.dot_general(x, y, NN, precision=lax.Precision.HIGHEST, preferred_element_type=F32)

    def body(x_ref, flip_ref, place_ref, o_ref, z_ref):
        for h in range(4):
            for off in range(N_OFF):
                lined_up = pltpu.roll(exact(flip_ref[...], x_ref[h, off]), 0, axis=1, stride=1, stride_axis=0)
                z_ref[h * N_OFF + off:h * N_OFF + off + 1, :] = jnp.sum(lined_up, axis=0, keepdims=True)
        acc = jnp.zeros((64, HD), F32)
        for j in range(WIN_R):
            at_zero = pltpu.roll(z_ref[...], (KEYS_B - (j * GRID_W + lane0)) % KEYS_B, axis=1)[:, :HD]
            acc = acc + exact(place_ref[j], at_zero)
        lane = lax.broadcasted_iota(jnp.int32, (64, HD), 1)
        o_ref[...] = jnp.where(lane < 2 * WIN_C - 1, acc, 0.0)

    return pl.pallas_call(
        body, name="bias_reduce", out_shape=jax.ShapeDtypeStruct((64, HD), F32),
        scratch_shapes=[pltpu.VMEM((4 * N_OFF, KEYS_B), F32)],
        compiler_params=pltpu.CompilerParams(vmem_limit_bytes=VMEM_LIMIT),
    )(dbias_tab, jnp.asarray(flip), jnp.asarray(place))


def _rows_to_tab(rows):
    t = rows.reshape(4, N_OFF, WIN_R, GRID_W, GRID_W)
    return t.transpose(0, 1, 3, 2, 4).reshape(4, N_OFF, GRID_W, KEYS_B)


def _row_window(r):
    r0 = jnp.clip(r - WIN_R // 2, 0, ROWS - WIN_R)
    off = r0 + (WIN_R - 1) - r
    return pl.multiple_of(r * GRID_W, GRID_W), pl.multiple_of(r0 * GRID_W, GRID_W), off


def _attn_b_fwd(qn, kn, vb, bias_tab):
    def body(q_ref, k_ref, v_ref, b_ref, o_ref, lse_ref):
        def row(r, carry):
            qs, ks, off = _row_window(r)
            q = q_ref[pl.ds(qs, GRID_W), :]
            s = lax.dot_general(q, k_ref[pl.ds(ks, KEYS_B), :], NT, preferred_element_type=F32) * SCALE
            s = s + b_ref[off]
            m = jnp.max(s, axis=-1, keepdims=True)
            p = jnp.exp(s - m)
            l = jnp.sum(p, axis=-1, keepdims=True)
            o = lax.dot_general(p.astype(BF16), v_ref[pl.ds(ks, KEYS_B), :], NN, preferred_element_type=F32)
            o_ref[pl.ds(qs, GRID_W), :] = (o / l).astype(BF16)
            lse_ref[pl.ds(qs, GRID_W), :] = m + jnp.log(l)
            return carry

        lax.fori_loop(0, ROWS, row, 0, unroll=2)

    full = pl.BlockSpec((S, HD), lambda h: (0, h))
    return pl.pallas_call(
        body, name="attn_b_fwd", grid=(4,),
        in_specs=[full, full, full, pl.BlockSpec((None, N_OFF, GRID_W, KEYS_B), lambda h: (h, 0, 0, 0))],
        out_specs=[pl.BlockSpec((S, HD), lambda h: (0, h)), pl.BlockSpec((None, S, 1), lambda h: (h, 0, 0))],
        out_shape=[jax.ShapeDtypeStruct((S, D_BR), BF16), jax.ShapeDtypeStruct((4, S, 1), F32)],
        compiler_params=_params(("parallel",)),
    )(qn, kn, vb, bias_tab)


def _attn_b_bwd(qn, kn, vb, bias_tab, ob, dob, lse):
    def body(q_ref, k_ref, v_ref, b_ref, o_ref, do_ref, lse_ref, dq_ref, dk_ref, dv_ref, db_ref):
        dk_ref[...] = jnp.zeros((S, HD), F32)
        dv_ref[...] = jnp.zeros((S, HD), F32)
        db_ref[...] = jnp.zeros((N_OFF, GRID_W, KEYS_B), F32)

        def row(r, carry):
            qs, ks, off = _row_window(r)
            rows = pl.ds(qs, GRID_W)
            keys = pl.ds(ks, KEYS_B)
            q = q_ref[rows, :]
            kw = k_ref[keys, :]
            s = lax.dot_general(q, kw, NT, preferred_element_type=F32) * SCALE + b_ref[off]
            p = jnp.exp(s - lse_ref[rows, :])
            do = do_ref[rows, :]
            dobf = do.astype(BF16)
            dsum = jnp.sum(do * o_ref[rows, :].astype(F32), axis=-1, keepdims=True)
            dp = lax.dot_general(dobf, v_ref[keys, :], NT, preferred_element_type=F32)
            ds = p * (dp - dsum)
            db_ref[off] += ds
            dsb = (ds * SCALE).astype(BF16)
            dq_ref[rows, :] = lax.dot_general(dsb, kw, NN, preferred_element_type=F32)
            dk_ref[keys, :] += lax.dot_general(dsb, q, TN, preferred_element_type=F32)
            dv_ref[keys, :] += lax.dot_general(p.astype(BF16), dobf, TN, preferred_element_type=F32)
            return carry

        lax.fori_loop(0, ROWS, row, 0, unroll=2)

    full = pl.BlockSpec((S, HD), lambda h: (0, h))
    slot = pl.BlockSpec((S, HD), lambda h: (0, h))
    tab = pl.BlockSpec((None, N_OFF, GRID_W, KEYS_B), lambda h: (h, 0, 0, 0))
    shape = jax.ShapeDtypeStruct((S, D_BR), F32)
    return pl.pallas_call(
        body, name="attn_b_bwd", grid=(4,),
        in_specs=[full, full, full, tab, slot, slot, pl.BlockSpec((None, S, 1), lambda h: (h, 0, 0))],
        out_specs=[slot, slot, slot, tab],
        out_shape=[shape, shape, shape, jax.ShapeDtypeStruct((4, N_OFF, GRID_W, KEYS_B), F32)],
        compiler_params=_params(("parallel",)),
    )(qn, kn, vb, bias_tab, ob, dob, lse)


def _epi_relu_sq(acc, ex, outs):
    u = jnp.maximum(acc, 0.0)
    outs[0][...] = u.astype(BF16)
    outs[1][...] = (u * u).astype(BF16)


def _epi_relu_sq_bwd(acc, ex, outs):
    outs[0][...] = (acc * (2.0 * ex[0][...].astype(F32))).astype(BF16)


def _epi_loss_head(acc, ex, outs):
    e = acc + ex[0][...] - ex[1][...]
    dy = e * (1.0 / D)
    outs[0][...] = dy
    outs[1][...] = dy.astype(BF16)
    part = (0.5 / D) * jnp.sum(jnp.sum(e * e, axis=-1, keepdims=True), axis=0, keepdims=True)
    first = (pl.program_id(0) == 0) & (pl.program_id(1) == 0)

    @pl.when(first)
    def _():
        outs[2][...] = part

    @pl.when(jnp.logical_not(first))
    def _():
        outs[2][...] += part


def _local_step(x, target, norm_mix, b_gate, gains, rpb_pad, norm_ffn,
                w_in, w_pa, w_pb, w_out, w_up, w_down, weight_grads, riders=lambda name: None):
    def ridden(name, *args, **kwargs):
        ride = riders(name)
        if ride is None:
            return _mm_nt(*args, name=name, **kwargs)
        out, rode = _mm_nt(*args, name=name, rider=ride[0], **kwargs)
        ride[1](rode)
        return out

    cos2, sin2 = _rope_tables()
    expand, keep, sel = _bias_constants()
    w_out3 = w_out[None]

    xn, rstd1 = _rms_fwd(x, norm_mix, name="rms_mix")
    proj = _mm_nn(xn, w_in, tm=1024, tn=1280, name="proj")
    qkv_a, qkv_b = _qk_prep(proj, gains, cos2, sin2)
    fwd_a = [_attn_a_fwd(*qkv_a[gi], gi) for gi in range(3)]
    oa, lse_a = _combine_a([o for o, _ in fwd_a], [l for _, l in fwd_a])
    bias_tab = _rows_to_tab(_bias_expand(rpb_pad, expand, keep, sel))
    ob, lse_b = _attn_b_fwd(*qkv_b, bias_tab)
    mixed, ya, yb = _mix_fwd(oa, ob, w_pa, w_pb, proj, b_gate)
    h1 = _mm_nn(mixed, w_out3, tm=1024, tn=1024, name="out_proj", epi=_epi_residual, extra=(x,))
    hn, rstd2 = _rms_fwd(h1, norm_ffn, name="rms_ffn")
    u, usq = _mm_nn(hn, w_up, tm=1024, tn=1024, name="ffn_up", epi=_epi_relu_sq,
                    out_dtypes=(BF16, BF16))
    dy, dyb, loss = _mm_nn(usq, w_down[0], tm=512, tn=512, name="ffn_down_0", epi=_epi_loss_head,
                           extra=(h1, target), out_dtypes=(F32, BF16), total=True, width=D)
    dy, dyb, loss_1 = _mm_nn(usq, w_down[1], tm=512, tn=512, name="ffn_down_1", epi=_epi_loss_head,
                             extra=(h1, target), out_dtypes=(F32, BF16), total=True, width=D,
                             col0=D // 2, into=(dy, dyb))
    loss = loss + loss_1

    sent = weight_grads("w_down", {5: (usq, dyb)})
    du = _mm_nt(dyb, w_down[0], more_b=(w_down[1],), tm=1024, tn=1024, name="ffn_down_bwd", out_dtype=BF16,
                epi=_epi_relu_sq_bwd, extra=(u,), after=sent)
    sent = weight_grads("w_up", {4: (hn, du)})
    dhn = ridden("ffn_up_bwd", du, w_up, tm=512, tn=512, after=sent)
    dh1, dh1b, g_norm_ffn = _rms_bwd(dhn, h1, rstd2, norm_ffn, dy, name="rms_ffn_bwd", bf16_copy=True)

    dya, dyb2, dproj, g_b = _mix_bwd(dh1b, w_out, proj, b_gate, ya, yb)
    sent = weight_grads("w_mix", {3: (mixed, dh1b), 1: (oa, dya), 2: (ob, dyb2)})
    dob = _mm_nt(dyb2, w_pb, tm=1024, tn=D_BR, name="proj_b_bwd", after=sent)
    prep = _proj_a_bwd(dya, w_pa, oa, lse_a)
    grads_a = [_attn_a_bwd(*qkv_a[gi], *prep[gi], gi) for gi in range(3)]
    dqb, dkb, dvb, dbias = _attn_b_bwd(*qkv_b, bias_tab, ob, dob, lse_b)
    g_rpb = _bias_reduce(dbias)
    dproj, g_gains = _qk_prep_bwd(dproj, proj, gains, cos2, sin2, grads_a, (dqb, dkb, dvb))
    sent = weight_grads("w_in", {0: (xn, dproj)})
    dxn = ridden("proj_bwd", dproj, w_in, tm=256, tn=512, after=sent)
    grad_x, g_norm_mix = _rms_bwd(dxn, x, rstd1, norm_mix, dh1, name="rms_mix_bwd", bf16_copy=False)

    small = (g_norm_mix, g_b, g_gains, g_rpb, g_norm_ffn)
    return loss, grad_x, small


def _cast_bf16(w, *, part=0, parts=1, after=(), tr=256):
    rows, cols = w.shape[0], w.shape[1] // parts
    tr = min(tr, rows)

    def body(w_ref, *rest):
        rest[-1][...] = w_ref[...].astype(BF16)

    return pl.pallas_call(
        body, name=f"cast_{rows}x{cols}_{part}", grid=(rows // tr,),
        in_specs=[pl.BlockSpec((tr, cols), lambda i: (i, part))] + [pl.BlockSpec(memory_space=pl.ANY)] * len(after),
        out_specs=pl.BlockSpec((tr, cols), lambda i: (i, 0)),
        out_shape=jax.ShapeDtypeStruct((rows, cols), BF16), compiler_params=_params(("parallel",)),
    )(w, *after)


def _me_and_peers():
    x, y, c = lax.axis_index("x"), lax.axis_index("y"), lax.axis_index("c")
    me = 4 * x + 2 * y + c
    peers = []
    for k in range(1, N_DEV):
        px = 1 - x if k & 4 else x
        py = 1 - y if k & 2 else y
        pc = 1 - c if k & 1 else c
        peers.append(((px, py, pc), 4 * px + 2 * py + pc))
    return me, peers


def _gather_on_sequencer(shards, name):
    n = len(shards)
    hbm = pltpu.MemorySpace.HBM
    ins = [jax.new_ref(s, memory_space=hbm) for s in shards]
    outs = [jax.empty_ref(jax.ShapeDtypeStruct((N_DEV,) + s.shape, s.dtype), memory_space=hbm) for s in shards]

    @_sequencer(name, ((n, N_DEV - 1), (n, N_DEV - 1), (n,)), 0)
    def launch(send, recv, lsem):
        x, y, c = lax.axis_index("x"), lax.axis_index("y"), lax.axis_index("c")
        me, sibling = (x, y, c), (x, y, 1 - c)
        chips = [(1 - x, y), (x, 1 - y), (1 - x, 1 - y)]
        _handshake([sibling] + [(*chip, c) for chip in chips])

        def copy(w, k, block, to, src=None):
            px, py, pc = block
            dst = outs[w].at[4 * px + 2 * py + pc]
            return pltpu.make_async_remote_copy(dst if src is None else src, dst, send.at[w, k], recv.at[w, k],
                                                device_id=to, device_id_type=MESH)

        local = [pltpu.make_async_copy(ins[w], outs[w].at[4 * x + 2 * y + c], lsem.at[w]) for w in range(n)]
        for cp in local:
            cp.start()
        first = []
        for w in range(n):
            first += [copy(w, 1 + j, me, (*chip, c), src=ins[w]) for j, chip in enumerate(chips)]
            first.append(copy(w, 0, me, sibling, src=ins[w]))
        for cp in first:
            cp.start()
        passed = []
        for w in range(n):
            for j, chip in enumerate(chips):
                copy(w, 1 + j, (*chip, c), me).wait_recv()
                cp = copy(w, 4 + j, (*chip, c), sibling)
                cp.start()
                passed.append(cp)
        for w in range(n):
            copy(w, 0, sibling, me).wait_recv()
            for j, chip in enumerate(chips):
                copy(w, 4 + j, (*chip, 1 - c), me).wait_recv()
        for cp in first + passed:
            cp.wait_send()
        for cp in local:
            cp.wait()

    launch()
    return [o[...] for o in outs]


N_CHIP = 4
CHIPS = ((0, 0), (0, 1), (1, 0), (1, 1))


def _sequencer(name, n_sems, collective_id):
    return functools.partial(
        pl.kernel, mesh=plsc.ScalarSubcoreMesh(axis_name="seq", num_cores=1), name=name,
        scratch_types=tuple(pltpu.SemaphoreType.DMA(s) for s in n_sems),
        compiler_params=pltpu.CompilerParams(collective_id=collective_id))


def _handshake(peers):
    barrier = pltpu.get_barrier_semaphore()
    for peer in peers:
        pl.semaphore_signal(barrier, inc=1, device_id=peer, device_id_type=MESH)
    pl.semaphore_wait(barrier, len(peers))


def _chip_exchange_on_sequencer(parts, name):
    n = len(parts)
    hbm = pltpu.MemorySpace.HBM
    ins = [jax.new_ref(p, memory_space=hbm) for p in parts]
    outs = [jax.empty_ref(jax.ShapeDtypeStruct(p.shape, p.dtype), memory_space=hbm) for p in parts]

    @_sequencer(name, ((n, 3), (n, 3), (n,)), 2)
    def launch(send, recv, lsem):
        x, y, c = lax.axis_index("x"), lax.axis_index("y"), lax.axis_index("c")
        mine = 2 * x + y
        chips = [(1 - x, y), (x, 1 - y), (1 - x, 1 - y)]
        _handshake([(*chip, c) for chip in chips])
        local = [pltpu.make_async_copy(ins[w].at[mine], outs[w].at[mine], lsem.at[w]) for w in range(n)]
        for cp in local:
            cp.start()
        sends = []
        for w in range(n):
            for j, (px, py) in enumerate(chips):
                cp = pltpu.make_async_remote_copy(ins[w].at[2 * px + py], outs[w].at[mine],
                                                  send.at[w, j], recv.at[w, j],
                                                  device_id=(px, py, c), device_id_type=MESH)
                cp.start()
                sends.append(cp)
        for w in range(n):
            for j, (px, py) in enumerate(chips):
                pltpu.make_async_remote_copy(ins[w].at[mine], outs[w].at[2 * px + py],
                                             send.at[w, j], recv.at[w, j],
                                             device_id=(px, py, c), device_id_type=MESH).wait_recv()
        for cp in sends:
            cp.wait_send()
        for cp in local:
            cp.wait()

    launch()
    return [o[...] for o in outs]


GRAD_TILES = (dict(blocks_on="cols", tm=512, tn=1280), dict(blocks_on="cols", tm=512, tn=256),
              dict(blocks_on="cols", tm=512, tn=256), dict(blocks_on="rows", tm=256, tn=2048),
              dict(blocks_on="cols", tm=1024, tn=1024), dict(blocks_on="rows", tm=1024, tn=1024))


def _mm_tn_pair(a, b, *, blocks_on, tm, tn, name):
    t_len, m = a.shape
    n = b.shape[1]
    if blocks_on == "rows":
        rows, cols, inner = m // N_DEV, n, n // tn
        assert tm == rows
        a_spec = pl.BlockSpec((t_len, tm), lambda p, t, blk: (0, blk[p]))
        b_spec = pl.BlockSpec((t_len, tn), lambda p, t, blk: (0, t))
        out_spec = pl.BlockSpec((None, tm, tn), lambda p, t, blk: (
            jnp.maximum(p - N_CHIP, 0), 0, jnp.where(p < N_CHIP, 0, t)))
    else:
        rows, cols, inner = m, n // N_DEV, m // tm
        assert tn == cols
        a_spec = pl.BlockSpec((t_len, tm), lambda p, t, blk: (0, t))
        b_spec = pl.BlockSpec((t_len, tn), lambda p, t, blk: (0, blk[p]))
        out_spec = pl.BlockSpec((None, tm, tn), lambda p, t, blk: (
            jnp.maximum(p - N_CHIP, 0), jnp.where(p < N_CHIP, 0, t), 0))

    def body(blk_ref, a_ref, b_ref, o_ref, land, stage, send_sem, recv_sem):
        del blk_ref
        p, t = pl.program_id(0), pl.program_id(1)
        step = p * inner + t
        x, y, c = lax.axis_index("x"), lax.axis_index("y"), lax.axis_index("c")
        tile = _dot(a_ref[...], b_ref[...], TN)

        def to_sibling(slot, chip, piece):
            return pltpu.make_async_remote_copy(stage.at[slot], land.at[chip, piece], send_sem.at[slot],
                                                recv_sem.at[chip, piece],
                                                device_id=(x, y, 1 - c), device_id_type=MESH)

        @pl.when(p < N_CHIP)
        def _():
            slot = step % 2

            @pl.when(step >= 2)
            def _():
                to_sibling(slot, 0, 0).wait_send()

            stage[slot] = tile.astype(BF16)
            to_sibling(slot, p, t).start()

        @pl.when(step == N_CHIP * inner)
        def _():
            for slot in range(min(2, N_CHIP * inner)):
                to_sibling(slot, 0, 0).wait_send()

        @pl.when(p >= N_CHIP)
        def _():
            chip = p - N_CHIP
            to_sibling(0, chip, t).wait_recv()
            o_ref[...] = (tile + land[chip, t].astype(F32)).astype(BF16)

    c = lax.axis_index("c")
    order = jnp.stack([2 * ch + 1 - c for ch in range(N_CHIP)] + [2 * ch + c for ch in range(N_CHIP)])
    return pl.pallas_call(
        body, name=name,
        grid_spec=pltpu.PrefetchScalarGridSpec(
            num_scalar_prefetch=1, grid=(N_DEV, inner), in_specs=[a_spec, b_spec], out_specs=out_spec,
            scratch_shapes=[pltpu.VMEM((N_CHIP, inner, tm, tn), BF16), pltpu.VMEM((2, tm, tn), BF16),
                            pltpu.SemaphoreType.DMA((2,)), pltpu.SemaphoreType.DMA((N_CHIP, inner))]),
        out_shape=jax.ShapeDtypeStruct((N_CHIP, rows, cols), BF16),
        compiler_params=_params(("arbitrary", "arbitrary")),
    )(order.astype(jnp.int32), a, b)


def _adamw_math(g, w, m, v):
    m2 = B1 * m + (1.0 - B1) * g
    v2 = B2 * v + (1.0 - B2) * (g * g)
    delta = -LR * ((m2 / BC1) / (jnp.sqrt(v2 / BC2) + AEPS) + WD * w)
    return delta, m2, v2


def _adamw_block(ins, outs):
    p_ref, w_ref, m_ref, v_ref = ins
    g = p_ref[0].astype(F32)
    for b in range(1, N_CHIP):
        g = g + p_ref[b].astype(F32)
    delta, m2, v2 = _adamw_math(g, w_ref[...], m_ref[...], v_ref[...])
    for ref, val in zip(outs, (g, delta, m2, v2)):
        ref[...] = val


class _Rider(NamedTuple):
    inputs: tuple
    in_specs: list
    out_shape: list
    out_specs: list
    body: Callable


def _adamw_rider(parts, w, m, v):
    rows, cols = w.shape

    def rider(steps, step_of):
        rr = rows // steps
        blk = pl.BlockSpec((rr, cols), lambda *ids: (step_of(*ids[:2]), 0))
        chips = pl.BlockSpec((N_CHIP, rr, cols), lambda *ids: (0, step_of(*ids[:2]), 0))
        shape = jax.ShapeDtypeStruct((rows, cols), F32)
        return _Rider((parts, w, m, v), [chips, blk, blk, blk], [shape] * 4, [blk] * 4, _adamw_block)

    return rider


def _adamw(parts, w, m, v, *, name, after=(), tr=256):
    rows, cols = w.shape

    def body(*refs):
        _adamw_block(refs[:4], refs[4 + len(after):])

    spec = pl.BlockSpec((tr, cols), lambda i: (i, 0))
    shape = jax.ShapeDtypeStruct((rows, cols), F32)
    return pl.pallas_call(
        body, name=name, grid=(rows // tr,),
        in_specs=[pl.BlockSpec((N_CHIP, tr, cols), lambda i: (0, i, 0)), spec, spec, spec]
        + [pl.BlockSpec(memory_space=pl.ANY)] * len(after),
        out_specs=[spec] * 4, out_shape=[shape] * 4,
        compiler_params=_params(("parallel",)),
    )(parts, w, m, v, *after)


def _small_update(part, w, m, v, after=()):
    rows = part.shape[0]

    def body(p_ref, w_ref, m_ref, v_ref, *rest):
        g_ref, d_ref, mo_ref, vo_ref, buf, send, recv = rest[len(after):]
        me, peers = _me_and_peers()
        buf[me] = p_ref[...]
        sends = []
        for k, (dev, _) in enumerate(peers):
            cp = pltpu.make_async_remote_copy(p_ref, buf.at[me], send.at[k], recv.at[k],
                                              device_id=dev, device_id_type=MESH)
            cp.start()
            sends.append(cp)
        for k, (dev, idx) in enumerate(peers):
            pltpu.make_async_remote_copy(p_ref, buf.at[idx], send.at[k], recv.at[k],
                                         device_id=dev, device_id_type=MESH).wait_recv()
        for cp in sends:
            cp.wait_send()
        g = buf[0]
        for b in range(1, N_DEV):
            g = g + buf[b]
        delta, m2, v2 = _adamw_math(g, w_ref[...], m_ref[...], v_ref[...])
        g_ref[...] = g
        d_ref[...] = delta
        mo_ref[...] = m2
        vo_ref[...] = v2

    vm = pl.BlockSpec(memory_space=pltpu.VMEM)
    shape = jax.ShapeDtypeStruct((rows, HD), F32)
    return pl.pallas_call(
        body, name="small_params_update",
        in_specs=[vm] * 4 + [pl.BlockSpec(memory_space=pl.ANY)] * len(after),
        out_specs=[vm] * 4, out_shape=[shape] * 4,
        scratch_shapes=[pltpu.VMEM((N_DEV, rows, HD), F32),
                        pltpu.SemaphoreType.DMA((N_DEV - 1,)), pltpu.SemaphoreType.DMA((N_DEV - 1,))],
    )(part, w, m, v, *after)


def _pack_small(norm_mix, b_gate, qa, ka, qb, kb, rpb, norm_ffn):
    gains = jnp.concatenate([qa, ka, qb, kb, jnp.zeros((4, HD), F32)], axis=0)
    rpb_pad = jnp.pad(rpb.reshape(4 * (2 * WIN_R - 1), 2 * WIN_C - 1), ((0, 4), (0, HD - (2 * WIN_C - 1))))
    return jnp.concatenate([norm_mix.reshape(16, HD), b_gate.reshape(32, HD), gains, rpb_pad,
                            norm_ffn.reshape(16, HD), jnp.zeros((8, HD), F32)], axis=0)


LOSS_ROW = 136


def _unpack_small(p):
    norm_mix = p[0:16].reshape(1, D)
    b_gate = p[16:48].reshape(1, 2 * D)
    qa, ka, qb, kb = (p[48 + i:49 + i] for i in range(4))
    rpb = p[56:116, :2 * WIN_C - 1].reshape(1, 4, 2 * WIN_R - 1, 2 * WIN_C - 1)
    norm_ffn = p[120:136].reshape(1, D)
    return norm_mix, b_gate, qa, ka, qb, kb, rpb, norm_ffn


def kernel(x, norm_mix, w_in, b_gate, q_norm_a, k_norm_a, q_norm_b, k_norm_b, rpb_b, w_proj_a, w_proj_b, w_out, norm_ffn, w_up, w_down, loss_target, m_norm_mix, m_w_in, m_b_gate, m_q_norm_a, m_k_norm_a, m_q_norm_b, m_k_norm_b, m_rpb_b, m_w_proj_a, m_w_proj_b, m_w_out, m_norm_ffn, m_w_up, m_w_down, v_norm_mix, v_w_in, v_b_gate, v_q_norm_a, v_k_norm_a, v_q_norm_b, v_k_norm_b, v_rpb_b, v_w_proj_a, v_w_proj_b, v_w_out, v_norm_ffn, v_w_up, v_w_down):
    big_w = (w_in[0], w_proj_a[0], w_proj_b[0], w_out[0], w_up[0], w_down[0])
    big_m = (m_w_in[0], m_w_proj_a[0], m_w_proj_b[0], m_w_out[0], m_w_up[0], m_w_down[0])
    big_v = (v_w_in[0], v_w_proj_a[0], v_w_proj_b[0], v_w_out[0], v_w_up[0], v_w_down[0])
    names = ("w_in", "w_proj_a", "w_proj_b", "w_out", "w_up", "w_down")

    shards = [_cast_bf16(w) for w in big_w[:5]]
    g_in, = _gather_on_sequencer(shards[0:1], "gather_w_in")
    g_pa, g_pb, g_out, g_up = _gather_on_sequencer(shards[1:5], "gather_w_mix_up")
    small_w = _pack_small(norm_mix, b_gate, q_norm_a, k_norm_a, q_norm_b, k_norm_b, rpb_b, norm_ffn)
    small_m = _pack_small(m_norm_mix, m_b_gate, m_q_norm_a, m_k_norm_a, m_q_norm_b, m_k_norm_b, m_rpb_b, m_norm_ffn)
    small_v = _pack_small(v_norm_mix, v_b_gate, v_q_norm_a, v_k_norm_a, v_q_norm_b, v_k_norm_b, v_rpb_b, v_norm_ffn)
    g_down = [_gather_on_sequencer([_cast_bf16(big_w[5], part=h, parts=2, after=(small_w, small_m, small_v) * h)],
                                   f"gather_w_down_{h}")[0].reshape(1, D_FF, D // 2) for h in range(2)]

    upd = [None] * 6
    in_flight = {}

    def weight_grads(tag, operands):
        sums = {i: _mm_tn_pair(a, b, name=f"grad_{names[i]}", **GRAD_TILES[i]) for i, (a, b) in operands.items()}
        new = list(sums.values())
        in_flight.update(zip(sums, _chip_exchange_on_sequencer(new, f"chip_exchange_{tag}")))
        return new

    def riders(name):
        i = {"proj_bwd": 5}.get(name)
        if i is None:
            return None
        return (_adamw_rider(in_flight.pop(i), big_w[i], big_m[i], big_v[i]),
                functools.partial(upd.__setitem__, i))

    loss, grad_x, small_g = _local_step(
        x[0], loss_target[0], norm_mix, b_gate, small_w[48:56], small_w[56:120], norm_ffn,
        g_in, g_pa, g_pb, g_out.reshape(D, D), g_up, g_down, weight_grads, riders)

    g_norm_mix, g_b, g_gains, g_rpb, g_norm_ffn = small_g
    small_part = jnp.concatenate([g_norm_mix.reshape(16, HD), g_b.reshape(32, HD),
                                  g_gains, g_rpb, g_norm_ffn.reshape(16, HD),
                                  jnp.pad(loss, ((0, 7), (0, HD - 1)))], axis=0)
    last = grad_x
    for i, r in in_flight.items():
        if i == 0:
            slabs = _small_update(small_part, small_w, small_m, small_v, after=[last])
            last = slabs[0]
        upd[i] = _adamw(r, big_w[i], big_m[i], big_v[i], name=f"adamw_{names[i]}", after=[last])
        last = upd[i][0]
    total = slabs[0][LOSS_ROW, 0]
    s_g, s_d, s_m, s_v = (_unpack_small(t) for t in slabs)
    b_g, b_d, b_m, b_v = ([u[j][None] for u in upd] for j in range(4))

    def order(small, big):
        nm, bg, qa, ka, qb, kb, rpb, nf = small
        w_in_, pa_, pb_, out_, up_, down_ = big
        return (nm, w_in_, bg, qa, ka, qb, kb, rpb, pa_, pb_, out_, nf, up_, down_)

    return (total, grad_x[None], *order(s_g, b_g), *order(s_d, b_d), *order(s_m, b_m), *order(s_v, b_v))
```

```python
import functools
from typing import Callable, NamedTuple

import jax
import jax.numpy as jnp
import numpy as np
from jax import lax
from jax.experimental import pallas as pl
from jax.experimental.pallas import tpu as pltpu
from jax.experimental.pallas import tpu_sc as plsc

F32 = jnp.float32
BF16 = jnp.bfloat16

N_DEV = 8
S = 2048
D = 2048
HD = 128
NH = 16
NH_A = 12
QKV = NH * HD
D_IN = 3 * QKV + 2 * D
D_BR = 512
D_FF = 4 * D
GRID_W = 64
ROWS = S // GRID_W
WIN_R = 8
WIN_C = 16
EPS = 1e-6
NEG = -1e30
SCALE = HD ** -0.5
ROPE_THETA = 10000.0
DILATIONS = (1, 4, 16)
HALF_A = 64
QB = 128

LR, B1, B2, AEPS, WD, STEP = 0.001, 0.9, 0.999, 1e-08, 0.01, 10
BC1 = 1.0 - B1 ** STEP
BC2 = 1.0 - B2 ** STEP

VMEM_LIMIT = 56 * 1024 * 1024
MESH = pl.DeviceIdType.MESH

NN = (((1,), (0,)), ((), ()))
NT = (((1,), (1,)), ((), ()))
TN = (((0,), (0,)), ((), ()))


def _params(sem):
    return pltpu.CompilerParams(dimension_semantics=sem, vmem_limit_bytes=VMEM_LIMIT)


def _matmul(a, b, *, product, grid, a_spec, b_spec, epi, out_shape, out_specs, name,
            extra=(), extra_specs=(), after=(), carried=False, rider=None, into=()):
    n_extra = len(extra)
    single = not isinstance(out_shape, (list, tuple))
    out_shape = [out_shape] if single else list(out_shape)
    out_specs = [out_specs] if single else list(out_specs)
    ride = rider(grid[0] * grid[1], lambda j, i: j * grid[1] + i) if rider else None
    r_in = list(ride.inputs) if ride else []
    n_main = len(out_shape)

    def body(a_ref, b_ref, *rest):
        n_in = n_extra + len(after) + len(r_in)
        ins, outs = rest[:n_in], rest[n_in + len(into):]
        epi(product(a_ref, b_ref, ins[:n_extra]), ins[:n_extra], outs[:n_main])
        if ride:
            ride.body(ins[n_extra + len(after):], outs[n_main:])

    res = pl.pallas_call(
        body, name=name, grid=grid,
        in_specs=[a_spec, b_spec, *extra_specs, *[pl.BlockSpec(memory_space=pl.ANY)] * len(after),
                  *(ride.in_specs if ride else []), *[pl.BlockSpec(memory_space=pl.ANY)] * len(into)],
        out_specs=out_specs + (ride.out_specs if ride else []),
        out_shape=out_shape + (ride.out_shape if ride else []),
        input_output_aliases={2 + n_extra + len(after) + len(r_in) + k: k for k in range(len(into))},
        compiler_params=_params(("arbitrary", "arbitrary") if carried else ("parallel", "parallel")),
    )(a, b, *extra, *after, *r_in, *into)
    main = res[0] if single else res[:n_main]
    return (main, res[n_main:]) if ride else main


def _dot(x, y, dims):
    return lax.dot_general(x, y, dims, preferred_element_type=F32)


def _epi_store(acc, ex, outs):
    outs[0][...] = acc.astype(outs[0].dtype)


def _epi_residual(acc, ex, outs):
    outs[0][...] = acc + ex[0][...]


def _mm_nn(a, b3, *, tm, tn, name, out_dtypes=(F32,), epi=_epi_store, extra=(), total=False,
           col0=0, width=None, into=()):
    m, kdim = a.shape
    g, _, ng = b3.shape
    n = g * ng
    c0 = col0 // tn
    if tn <= ng:
        npg = ng // tn
        b_spec = pl.BlockSpec((None, kdim, tn), lambda j, i: (j // npg, 0, j % npg))

        def product(a_ref, b_ref, ex):
            return _dot(a_ref[...], b_ref[...], NN)
    else:
        gb = tn // ng
        b_spec = pl.BlockSpec((gb, kdim, ng), lambda j, i: (j, 0, 0))

        def product(a_ref, b_ref, ex):
            return jnp.concatenate([_dot(a_ref[...], b_ref[q], NN) for q in range(gb)], axis=1)

    tile = pl.BlockSpec((tm, tn), lambda j, i: (i, j + c0))
    shapes = [jax.ShapeDtypeStruct((m, width or n), dt) for dt in out_dtypes]
    specs = [tile] * len(shapes)
    if total:
        shapes.append(jax.ShapeDtypeStruct((1, 1), F32))
        specs.append(pl.BlockSpec((1, 1), lambda j, i: (0, 0)))
    single = len(shapes) == 1
    return _matmul(
        a, b3, product=product, grid=(n // tn, m // tm), epi=epi, name=name, carried=total, into=into,
        a_spec=pl.BlockSpec((tm, kdim), lambda j, i: (i, 0)), b_spec=b_spec,
        extra=extra, extra_specs=[tile] * len(extra),
        out_shape=shapes[0] if single else shapes, out_specs=specs[0] if single else specs)


def _mm_nt(a, b3, *, tm, tn, name, out_dtype=F32, epi=_epi_store, extra=(), after=(), rider=None, more_b=()):
    m, kdim = a.shape
    _, n, _ = b3.shape
    n_b = len(more_b)

    def product(a_ref, b_ref, ex):
        acc, k0 = None, 0
        for ref in (b_ref, *ex[:n_b]):
            for q in range(ref.shape[0]):
                part = _dot(a_ref[:, k0:k0 + ref.shape[2]], ref[q], NT)
                acc = part if acc is None else acc + part
                k0 += ref.shape[2]
        return acc

    def write(acc, ex, outs):
        epi(acc, ex[n_b:], outs)

    def w_spec(w):
        return pl.BlockSpec((w.shape[0], tn, w.shape[2]), lambda j, i: (0, j, 0))

    tile = pl.BlockSpec((tm, tn), lambda j, i: (i, j))
    return _matmul(
        a, b3, product=product, grid=(n // tn, m // tm), epi=write, name=name,
        a_spec=pl.BlockSpec((tm, kdim), lambda j, i: (i, 0)), b_spec=w_spec(b3),
        extra=(*more_b, *extra), extra_specs=[w_spec(w) for w in more_b] + [tile] * len(extra),
        after=after, rider=rider,
        out_shape=jax.ShapeDtypeStruct((m, n), out_dtype), out_specs=tile)


def _mm_tn(a, b, *, tm, tn, name, groups=1, out_dtype=BF16):
    t, m = a.shape
    _, n = b.shape
    ng = n // groups
    if tn <= ng:
        npg = ng // tn
        out_spec = pl.BlockSpec((None, tm, tn), lambda j, i: (j // npg, i, j % npg))
        epi = _epi_store

        def product(a_ref, b_ref, ex):
            return _dot(a_ref[...], b_ref[...], TN)
    else:
        gb = tn // ng
        out_spec = pl.BlockSpec((gb, tm, ng), lambda j, i: (j, i, 0))

        def product(a_ref, b_ref, ex):
            return [_dot(a_ref[...], b_ref[:, q * ng:(q + 1) * ng], TN) for q in range(gb)]

        def epi(parts, ex, outs):
            for q, part in enumerate(parts):
                outs[0][q] = part.astype(out_dtype)

    return _matmul(
        a, b, product=product, grid=(n // tn, m // tm), epi=epi, name=name,
        a_spec=pl.BlockSpec((t, tm), lambda j, i: (0, i)),
        b_spec=pl.BlockSpec((t, tn), lambda j, i: (0, j)),
        out_shape=jax.ShapeDtypeStruct((groups, m, ng), out_dtype), out_specs=out_spec)


def _rms_fwd(x, g, *, name, tr=256):
    def body(x_ref, g_ref, y_ref, r_ref):
        xv = x_ref[...]
        r = lax.rsqrt(jnp.mean(xv * xv, axis=-1, keepdims=True) + EPS)
        y_ref[...] = (xv * r * g_ref[...]).astype(BF16)
        r_ref[...] = r

    row = pl.BlockSpec((tr, D), lambda i: (i, 0))
    return pl.pallas_call(
        body, name=name, grid=(S // tr,),
        in_specs=[row, pl.BlockSpec((1, D), lambda i: (0, 0))],
        out_specs=[row, pl.BlockSpec((tr, 1), lambda i: (i, 0))],
        out_shape=[jax.ShapeDtypeStruct((S, D), BF16), jax.ShapeDtypeStruct((S, 1), F32)],
        compiler_params=_params(("parallel",)),
    )(x, g)


def _rms_bwd(dy, x, rstd, g, resid, *, name, bf16_copy, tr=256):
    def body(dy_ref, x_ref, r_ref, g_ref, res_ref, dx_ref, *rest):
        dg_ref = rest[-1]
        r = r_ref[...]
        xh = x_ref[...] * r
        dyv = dy_ref[...]
        t = dyv * g_ref[...]
        dx = r * (t - xh * jnp.mean(t * xh, axis=-1, keepdims=True)) + res_ref[...]
        dx_ref[...] = dx
        if bf16_copy:
            rest[0][...] = dx.astype(BF16)
        part = jnp.sum(dyv * xh, axis=0, keepdims=True)

        @pl.when(pl.program_id(0) == 0)
        def _():
            dg_ref[...] = part

        @pl.when(pl.program_id(0) > 0)
        def _():
            dg_ref[...] += part

    row = pl.BlockSpec((tr, D), lambda i: (i, 0))
    vec = pl.BlockSpec((1, D), lambda i: (0, 0))
    return pl.pallas_call(
        body, name=name, grid=(S // tr,),
        in_specs=[row, row, pl.BlockSpec((tr, 1), lambda i: (i, 0)), vec, row],
        out_specs=[row] + [row] * bf16_copy + [vec],
        out_shape=[jax.ShapeDtypeStruct((S, D), F32)] + [jax.ShapeDtypeStruct((S, D), BF16)] * bf16_copy
        + [jax.ShapeDtypeStruct((1, D), F32)],
        compiler_params=_params(("arbitrary",)),
    )(dy, x, rstd, g, resid)


def _rope_tables():
    pos = np.arange(S, dtype=np.float32)
    inv = (ROPE_THETA ** (-np.arange(0, HD, 2, dtype=np.float32) / HD)).astype(np.float32)
    ang = pos[:, None] * inv[None, :]
    cos, sin = np.cos(ang), np.sin(ang)
    return (jnp.asarray(np.concatenate([cos, cos], axis=-1), F32),
            jnp.asarray(np.concatenate([-sin, sin], axis=-1), F32))


def _swap_halves(t):
    return pltpu.roll(t, HD // 2, axis=1)


TOK = 256


def _lane_block_spec(d, last=HD):
    return pl.BlockSpec((4, TOK // d, d * last), lambda i: (0, i, 0))


def _to_lane_blocks(dst, head, val, d, scr, dtype):
    w = val.shape[1]
    if d == 1:
        dst[head] = val.astype(dtype)
        return
    scr[...] = val
    for r in range(d):
        dst[head, :, r * w:(r + 1) * w] = scr[pl.ds(r, TOK // d, stride=d), :].astype(dtype)


def _from_lane_blocks(src, head, d, w, scr):
    if d == 1:
        return src[head].astype(F32)
    for r in range(d):
        scr[pl.ds(r, TOK // d, stride=d), :] = src[head, :, r * w:(r + 1) * w].astype(F32)
    return scr[...]


def _qk_prep(proj, gains, cos2, sin2):
    def body(q_ref, k_ref, v_ref, g_ref, c_ref, s_ref, *rest):
        outs, scr = rest[:-1], rest[-1]
        cos, sin = c_ref[...], s_ref[...]
        for which, (src, row_a, row_b) in enumerate(((q_ref, 0, 2), (k_ref, 1, 3), (v_ref, None, None))):
            for h in range(NH):
                y = src[:, h * HD:(h + 1) * HD]
                if row_a is not None:
                    y = y * lax.rsqrt(jnp.mean(y * y, axis=-1, keepdims=True) + EPS)
                    if h < NH_A:
                        y = y * g_ref[row_a:row_a + 1, :]
                        y = y * cos + _swap_halves(y) * sin
                    else:
                        y = y * g_ref[row_b:row_b + 1, :]
                if h < NH_A:
                    gi = h // 4
                    _to_lane_blocks(outs[3 * gi + which], h % 4, y, DILATIONS[gi], scr, BF16)
                else:
                    hb = h - NH_A
                    outs[9 + which][:, hb * HD:(hb + 1) * HD] = y.astype(BF16)

    def blk(c):
        return pl.BlockSpec((TOK, QKV), lambda i: (i, c))
    tab = pl.BlockSpec((TOK, HD), lambda i: (i, 0))
    out_specs, out_shape = [], []
    for d in DILATIONS:
        out_specs += [_lane_block_spec(d)] * 3
        out_shape += [jax.ShapeDtypeStruct((4, S // d, d * HD), BF16)] * 3
    out_specs += [pl.BlockSpec((TOK, D_BR), lambda i: (i, 0))] * 3
    out_shape += [jax.ShapeDtypeStruct((S, D_BR), BF16)] * 3
    outs = pl.pallas_call(
        body, name="qk_prep", grid=(S // TOK,),
        in_specs=[blk(0), blk(1), blk(2), pl.BlockSpec((8, HD), lambda i: (0, 0)), tab, tab],
        out_specs=out_specs, out_shape=out_shape,
        scratch_shapes=[pltpu.VMEM((TOK, HD), F32)],
        compiler_params=_params(("parallel",)),
    )(proj, proj, proj, gains, cos2, sin2)
    return [tuple(outs[3 * gi:3 * gi + 3]) for gi in range(3)], tuple(outs[9:12])


def _qk_prep_bwd(dproj, proj, gains, cos2, sin2, grads_a, grads_b):
    def body(dp_in, q_ref, k_ref, g_ref, c_ref, s_ref, *rest):
        grads, (dp_out, dg_ref, scr) = rest[:12], rest[12:]
        del dp_in
        cos, sin = c_ref[...], s_ref[...]

        def grad_of(which, h):
            if h < NH_A:
                gi = h // 4
                return _from_lane_blocks(grads[3 * gi + which], h % 4, DILATIONS[gi], HD, scr)
            hb = h - NH_A
            return grads[9 + which][:, hb * HD:(hb + 1) * HD]

        dg_rows = []
        for which, (src, base, row_a, row_b) in enumerate(((q_ref, 0, 0, 2), (k_ref, QKV, 1, 3))):
            dg_a = jnp.zeros((1, HD), F32)
            dg_b = jnp.zeros((1, HD), F32)
            for h in range(NH):
                t = src[:, h * HD:(h + 1) * HD]
                dy = grad_of(which, h)
                r = lax.rsqrt(jnp.mean(t * t, axis=-1, keepdims=True) + EPS)
                xh = t * r
                if h < NH_A:
                    dy = dy * cos - _swap_halves(dy) * sin
                    gain = g_ref[row_a:row_a + 1, :]
                    dg_a = dg_a + jnp.sum(dy * xh, axis=0, keepdims=True)
                else:
                    gain = g_ref[row_b:row_b + 1, :]
                    dg_b = dg_b + jnp.sum(dy * xh, axis=0, keepdims=True)
                u = dy * gain
                dx = r * (u - xh * jnp.mean(u * xh, axis=-1, keepdims=True))
                dp_out[:, base + h * HD:base + (h + 1) * HD] = dx.astype(BF16)
            dg_rows += [(row_a, dg_a), (row_b, dg_b)]
        for h in range(NH):
            dp_out[:, 2 * QKV + h * HD:2 * QKV + (h + 1) * HD] = grad_of(2, h).astype(BF16)

        @pl.when(pl.program_id(0) == 0)
        def _():
            dg_ref[...] = jnp.zeros((8, HD), F32)

        for row, val in dg_rows:
            dg_ref[row:row + 1, :] += val

    def blk(c):
        return pl.BlockSpec((TOK, QKV), lambda i: (i, c))
    tab = pl.BlockSpec((TOK, HD), lambda i: (i, 0))
    gain_spec = pl.BlockSpec((8, HD), lambda i: (0, 0))
    grad_specs = [s for d in DILATIONS for s in [_lane_block_spec(d)] * 3]
    grad_specs += [pl.BlockSpec((TOK, D_BR), lambda i: (i, 0))] * 3
    return pl.pallas_call(
        body, name="qk_prep_bwd", grid=(S // TOK,),
        in_specs=[pl.BlockSpec(memory_space=pl.ANY), blk(0), blk(1), gain_spec, tab, tab] + grad_specs,
        out_specs=[pl.BlockSpec((TOK, 3 * QKV), lambda i: (i, 0)), gain_spec],
        out_shape=[jax.ShapeDtypeStruct((S, D_IN), BF16), jax.ShapeDtypeStruct((8, HD), F32)],
        input_output_aliases={0: 0},
        scratch_shapes=[pltpu.VMEM((TOK, HD), F32)],
        compiler_params=_params(("arbitrary",)),
    )(dproj, proj, proj, gains, cos2, sin2, *[g for grp in grads_a for g in grp], *grads_b)


def _mix_fwd(oa, ob, w_pa, w_pb, proj, b_gate, *, tr=256):
    def body(oa_ref, ob_ref, pa_ref, pb_ref, la_ref, lb_ref, ba_ref, bb_ref, mix_ref, ya_ref, yb_ref):
        ya = jnp.concatenate([_dot(oa_ref[...], pa_ref[q], NN) for q in range(N_DEV)], axis=1)
        yb = jnp.concatenate([_dot(ob_ref[...], pb_ref[q], NN) for q in range(N_DEV)], axis=1)
        ga = jax.nn.sigmoid(la_ref[...] + ba_ref[...])
        gb = jax.nn.sigmoid(lb_ref[...] + bb_ref[...])
        mix_ref[...] = (ga * ya + gb * yb).astype(BF16)
        ya_ref[...] = ya.astype(BF16)
        yb_ref[...] = yb.astype(BF16)

    row = pl.BlockSpec((tr, D), lambda i: (i, 0))
    branch = pl.BlockSpec((tr, D_BR), lambda i: (i, 0))
    whole = pl.BlockSpec((N_DEV, D_BR, D // N_DEV), lambda i: (0, 0, 0))
    return pl.pallas_call(
        body, name="mix_fwd", grid=(S // tr,),
        in_specs=[branch, branch, whole, whole,
                  pl.BlockSpec((tr, D), lambda i: (i, 3)), pl.BlockSpec((tr, D), lambda i: (i, 4)),
                  pl.BlockSpec((1, D), lambda i: (0, 0)), pl.BlockSpec((1, D), lambda i: (0, 1))],
        out_specs=[row, row, row], out_shape=[jax.ShapeDtypeStruct((S, D), BF16)] * 3,
        compiler_params=_params(("parallel",)),
    )(oa, ob, w_pa, w_pb, proj, proj, b_gate, b_gate)


def _mix_bwd(dh1b, w_out, proj, b_gate, ya, yb, *, tr=256):
    def body(dh_ref, w_ref, la_ref, lb_ref, b_ref, ya_ref, yb_ref, dya_ref, dyb_ref, dp_ref, db_ref):
        dm = _dot(dh_ref[...], w_ref[...], NT)
        parts = []
        for l_ref, y_ref, dy_ref, lo in ((la_ref, ya_ref, dya_ref, 0), (lb_ref, yb_ref, dyb_ref, D)):
            g = jax.nn.sigmoid(l_ref[...] + b_ref[:, lo:lo + D])
            dy_ref[...] = (dm * g).astype(BF16)
            dl = dm * y_ref[...].astype(F32) * g * (1.0 - g)
            dp_ref[:, lo:lo + D] = dl.astype(BF16)
            parts.append(jnp.sum(dl, axis=0, keepdims=True))
        part = jnp.concatenate(parts, axis=1)

        @pl.when(pl.program_id(0) == 0)
        def _():
            db_ref[...] = part

        @pl.when(pl.program_id(0) > 0)
        def _():
            db_ref[...] += part

    row = pl.BlockSpec((tr, D), lambda i: (i, 0))
    vec = pl.BlockSpec((1, 2 * D), lambda i: (0, 0))
    gate_cols = pl.BlockSpec((pl.Element(tr), pl.Element(2 * D)), lambda i: (i * tr, 3 * QKV))
    return pl.pallas_call(
        body, name="mix_bwd", grid=(S // tr,),
        in_specs=[row, pl.BlockSpec((D, D), lambda i: (0, 0)),
                  pl.BlockSpec((tr, D), lambda i: (i, 3)), pl.BlockSpec((tr, D), lambda i: (i, 4)), vec, row, row],
        out_specs=[row, row, gate_cols, vec],
        out_shape=[jax.ShapeDtypeStruct((S, D), BF16), jax.ShapeDtypeStruct((S, D), BF16),
                   jax.ShapeDtypeStruct((S, D_IN), BF16), jax.ShapeDtypeStruct((1, 2 * D), F32)],
        compiler_params=_params(("arbitrary",)),
    )(dh1b, w_out, proj, proj, b_gate, ya, yb)


def _band_blocks(m_len):
    wk = min(m_len, QB + 2 * QB)
    return [(qb * QB, min(max(qb * QB - QB, 0), m_len - wk), wk) for qb in range(m_len // QB)]


def _band_scores(q, kw, q0, k0, wk):
    s = _dot(q, kw, NT) * SCALE
    qpos = q0 + lax.broadcasted_iota(jnp.int32, (QB, 1), 0)
    kpos = k0 + lax.broadcasted_iota(jnp.int32, (1, wk), 1)
    return jnp.where(jnp.abs(kpos - qpos) <= HALF_A, s, NEG)


def _attn_a_fwd(q, k, v, gi):
    d = DILATIONS[gi]
    m_len = S // d

    def body(q_ref, k_ref, v_ref, o_ref, lse_ref):
        for r in range(d):
            lanes = slice(r * HD, (r + 1) * HD)
            for q0, k0, wk in _band_blocks(m_len):
                s = _band_scores(q_ref[q0:q0 + QB, lanes], k_ref[k0:k0 + wk, lanes], q0, k0, wk)
                m = jnp.max(s, axis=-1, keepdims=True)
                p = jnp.exp(s - m)
                l = jnp.sum(p, axis=-1, keepdims=True)
                o_ref[q0:q0 + QB, lanes] = _dot(p.astype(BF16), v_ref[k0:k0 + wk, lanes], NN) / l
                lse_ref[q0:q0 + QB, r:r + 1] = m + jnp.log(l)

    head = pl.BlockSpec((None, m_len, d * HD), lambda h: (h, 0, 0))
    stat = pl.BlockSpec((None, m_len, d), lambda h: (h, 0, 0))
    return pl.pallas_call(
        body, name=f"attn_a_fwd_{gi}", grid=(4,),
        in_specs=[head, head, head], out_specs=[head, stat],
        out_shape=[jax.ShapeDtypeStruct((4, m_len, d * HD), F32), jax.ShapeDtypeStruct((4, m_len, d), F32)],
        compiler_params=_params(("parallel",)),
    )(q, k, v)


def _combine_a(os, lses):
    def body(o0, o1, o2, l0, l1, l2, oa_ref, lse_ref, scr, scr1):
        for h in range(4):
            o = [_from_lane_blocks(ref, h, d, HD, scr) for ref, d in zip((o0, o1, o2), DILATIONS)]
            a, b, c = (_from_lane_blocks(ref, h, d, 1, scr1) for ref, d in zip((l0, l1, l2), DILATIONS))
            m = jnp.maximum(jnp.maximum(a, b), c)
            wa, wb, wc = jnp.exp(a - m), jnp.exp(b - m), jnp.exp(c - m)
            tot = wa + wb + wc
            oa_ref[:, h * HD:(h + 1) * HD] = ((wa * o[0] + wb * o[1] + wc * o[2]) / tot).astype(BF16)
            lse_ref[h] = m + jnp.log(tot)

    return pl.pallas_call(
        body, name="combine_a", grid=(S // TOK,),
        in_specs=[_lane_block_spec(d) for d in DILATIONS] + [_lane_block_spec(d, 1) for d in DILATIONS],
        out_specs=[pl.BlockSpec((TOK, D_BR), lambda i: (i, 0)), pl.BlockSpec((4, TOK, 1), lambda i: (0, i, 0))],
        out_shape=[jax.ShapeDtypeStruct((S, D_BR), BF16), jax.ShapeDtypeStruct((4, S, 1), F32)],
        scratch_shapes=[pltpu.VMEM((TOK, HD), F32), pltpu.VMEM((TOK, 1), F32)],
        compiler_params=_params(("parallel",)),
    )(*os, *lses)


def _proj_a_bwd(dya, w_pa, oa, lse):
    kg = D // N_DEV

    def body(dy_ref, w_ref, o_ref, l_ref, *rest):
        outs, (scr, scr1) = rest[:9], rest[9:]
        doa = _dot(dy_ref[:, 0:kg], w_ref[0], NT)
        for q in range(1, N_DEV):
            doa = doa + _dot(dy_ref[:, q * kg:(q + 1) * kg], w_ref[q], NT)
        for h in range(4):
            do = doa[:, h * HD:(h + 1) * HD]
            dsum = jnp.sum(do * o_ref[:, h * HD:(h + 1) * HD].astype(F32), axis=-1, keepdims=True)
            for gi, d in enumerate(DILATIONS):
                _to_lane_blocks(outs[3 * gi], h, do, d, scr, BF16)
                _to_lane_blocks(outs[3 * gi + 1], h, l_ref[h], d, scr1, F32)
                _to_lane_blocks(outs[3 * gi + 2], h, dsum, d, scr1, F32)

    row = pl.BlockSpec((TOK, D_BR), lambda i: (i, 0))
    out_specs, out_shape = [], []
    for d in DILATIONS:
        out_specs += [_lane_block_spec(d), _lane_block_spec(d, 1), _lane_block_spec(d, 1)]
        out_shape += [jax.ShapeDtypeStruct((4, S // d, d * HD), BF16)] + [jax.ShapeDtypeStruct((4, S // d, d), F32)] * 2
    outs = pl.pallas_call(
        body, name="proj_a_bwd", grid=(S // TOK,),
        in_specs=[pl.BlockSpec((TOK, D), lambda i: (i, 0)),
                  pl.BlockSpec((N_DEV, D_BR, kg), lambda i: (0, 0, 0)),
                  row, pl.BlockSpec((4, TOK, 1), lambda i: (0, i, 0))],
        out_specs=out_specs, out_shape=out_shape,
        scratch_shapes=[pltpu.VMEM((TOK, HD), F32), pltpu.VMEM((TOK, 1), F32)],
        compiler_params=_params(("parallel",)),
    )(dya, w_pa, oa, lse)
    return [tuple(outs[3 * gi:3 * gi + 3]) for gi in range(3)]


def _attn_a_bwd(q, k, v, do, lse, dsum, gi):
    d = DILATIONS[gi]
    m_len = S // d

    def body(q_ref, k_ref, v_ref, do_ref, lse_ref, dsum_ref, dq_ref, dk_ref, dv_ref):
        dk_ref[...] = jnp.zeros((m_len, d * HD), F32)
        dv_ref[...] = jnp.zeros((m_len, d * HD), F32)
        for r in range(d):
            lanes = slice(r * HD, (r + 1) * HD)
            for q0, k0, wk in _band_blocks(m_len):
                rows, keys = slice(q0, q0 + QB), slice(k0, k0 + wk)
                qv, kw, vw, dov = q_ref[rows, lanes], k_ref[keys, lanes], v_ref[keys, lanes], do_ref[rows, lanes]
                p = jnp.exp(_band_scores(qv, kw, q0, k0, wk) - lse_ref[rows, r:r + 1])
                ds = (p * (_dot(dov, vw, NT) - dsum_ref[rows, r:r + 1]) * SCALE).astype(BF16)
                dq_ref[rows, lanes] = _dot(ds, kw, NN)
                dk_ref[keys, lanes] += _dot(ds, qv, TN)
                dv_ref[keys, lanes] += _dot(p.astype(BF16), dov, TN)

    head = pl.BlockSpec((None, m_len, d * HD), lambda h: (h, 0, 0))
    stat = pl.BlockSpec((None, m_len, d), lambda h: (h, 0, 0))
    shape = jax.ShapeDtypeStruct((4, m_len, d * HD), F32)
    return pl.pallas_call(
        body, name=f"attn_a_bwd_{gi}", grid=(4,),
        in_specs=[head, head, head, head, stat, stat], out_specs=[head, head, head],
        out_shape=[shape, shape, shape],
        compiler_params=_params(("parallel",)),
    )(q, k, v, do, lse, dsum)


KEYS_B = WIN_R * GRID_W
N_OFF = WIN_R


def _bias_constants():
    q = np.arange(GRID_W)[:, None]
    kc = np.arange(GRID_W)[None, :]
    dc = np.clip(kc - q, -(WIN_C - 1), WIN_C - 1) + (WIN_C - 1)
    expand = np.zeros((HD, GRID_W * GRID_W), np.float32)
    expand[dc.reshape(-1), np.arange(GRID_W * GRID_W)] = 1.0
    cs = np.clip(q - WIN_C // 2, 0, GRID_W - WIN_C)
    keep = ((kc >= cs) & (kc < cs + WIN_C)).reshape(1, -1).astype(np.float32)
    sel = np.zeros((64, 4 * N_OFF * WIN_R), np.float32)
    for h in range(4):
        for off in range(N_OFF):
            for j in range(WIN_R):
                sel[h * (2 * WIN_R - 1) + off + j, (h * N_OFF + off) * WIN_R + j] = 1.0
    return jnp.asarray(expand), jnp.asarray(keep), jnp.asarray(sel)


def _bias_expand(rpb_pad, expand, keep, sel):
    def body(r_ref, e_ref, k_ref, s_ref, o_ref):
        t = lax.dot_general(r_ref[...], e_ref[...], NN, precision=lax.Precision.HIGHEST,
                            preferred_element_type=F32)
        rows = lax.dot_general(s_ref[...], t, TN, precision=lax.Precision.HIGHEST,
                               preferred_element_type=F32)
        o_ref[...] = jnp.where(k_ref[...] > 0.5, rows, NEG)

    return pl.pallas_call(
        body, name="bias_expand",
        out_shape=jax.ShapeDtypeStruct((4 * N_OFF * WIN_R, GRID_W * GRID_W), F32),
        compiler_params=pltpu.CompilerParams(vmem_limit_bytes=VMEM_LIMIT),
    )(rpb_pad, expand, keep, sel)


def _bias_reduce(dbias_tab):
    lane0 = GRID_W - WIN_C
    flip = np.zeros((GRID_W, GRID_W), np.float32)
    flip[np.arange(GRID_W), GRID_W - 1 - np.arange(GRID_W)] = 1.0
    place = np.zeros((WIN_R, 64, 4 * N_OFF), np.float32)
    for j in range(WIN_R):
        for h in range(4):
            for off in range(N_OFF):
                place[j, h * (2 * WIN_R - 1) + off + j, h * N_OFF + off] = 1.0

    def exact(x, y):
        return lax.dot_general(x, y, NN, precision=lax.Precision.HIGHEST, preferred_element_type=F32)

    def body(x_ref, flip_ref, place_ref, o_ref, z_ref):
        for h in range(4):
            for off in range(N_OFF):
                lined_up = pltpu.roll(exact(flip_ref[...], x_ref[h, off]), 0, axis=1, stride=1, stride_axis=0)
                z_ref[h * N_OFF + off:h * N_OFF + off + 1, :] = jnp.sum(lined_up, axis=0, keepdims=True)
        acc = jnp.zeros((64, HD), F32)
        for j in range(WIN_R):
            at_zero = pltpu.roll(z_ref[...], (KEYS_B - (j * GRID_W + lane0)) % KEYS_B, axis=1)[:, :HD]
            acc = acc + exact(place_ref[j], at_zero)
        lane = lax.broadcasted_iota(jnp.int32, (64, HD), 1)
        o_ref[...] = jnp.where(lane < 2 * WIN_C - 1, acc, 0.0)

    return pl.pallas_call(
        body, name="bias_reduce", out_shape=jax.ShapeDtypeStruct((64, HD), F32),
        scratch_shapes=[pltpu.VMEM((4 * N_OFF, KEYS_B), F32)],
        compiler_params=pltpu.CompilerParams(vmem_limit_bytes=VMEM_LIMIT),
    )(dbias_tab, jnp.asarray(flip), jnp.asarray(place))


def _rows_to_tab(rows):
    t = rows.reshape(4, N_OFF, WIN_R, GRID_W, GRID_W)
    return t.transpose(0, 1, 3, 2, 4).reshape(4, N_OFF, GRID_W, KEYS_B)


def _row_window(r):
    r0 = jnp.clip(r - WIN_R // 2, 0, ROWS - WIN_R)
    off = r0 + (WIN_R - 1) - r
    return pl.multiple_of(r * GRID_W, GRID_W), pl.multiple_of(r0 * GRID_W, GRID_W), off


def _attn_b_fwd(qn, kn, vb, bias_tab):
    def body(q_ref, k_ref, v_ref, b_ref, o_ref, lse_ref):
        def row(r, carry):
            qs, ks, off = _row_window(r)
            q = q_ref[pl.ds(qs, GRID_W), :]
            s = lax.dot_general(q, k_ref[pl.ds(ks, KEYS_B), :], NT, preferred_element_type=F32) * SCALE
            s = s + b_ref[off]
            m = jnp.max(s, axis=-1, keepdims=True)
            p = jnp.exp(s - m)
            l = jnp.sum(p, axis=-1, keepdims=True)
            o = lax.dot_general(p.astype(BF16), v_ref[pl.ds(ks, KEYS_B), :], NN, preferred_element_type=F32)
            o_ref[pl.ds(qs, GRID_W), :] = (o / l).astype(BF16)
            lse_ref[pl.ds(qs, GRID_W), :] = m + jnp.log(l)
            return carry

        lax.fori_loop(0, ROWS, row, 0, unroll=8)

    full = pl.BlockSpec((S, HD), lambda h: (0, h))
    return pl.pallas_call(
        body, name="attn_b_fwd", grid=(4,),
        in_specs=[full, full, full, pl.BlockSpec((None, N_OFF, GRID_W, KEYS_B), lambda h: (h, 0, 0, 0))],
        out_specs=[pl.BlockSpec((S, HD), lambda h: (0, h)), pl.BlockSpec((None, S, 1), lambda h: (h, 0, 0))],
        out_shape=[jax.ShapeDtypeStruct((S, D_BR), BF16), jax.ShapeDtypeStruct((4, S, 1), F32)],
        compiler_params=_params(("parallel",)),
    )(qn, kn, vb, bias_tab)


def _attn_b_bwd(qn, kn, vb, bias_tab, ob, dob, lse):
    def body(q_ref, k_ref, v_ref, b_ref, o_ref, do_ref, lse_ref, dq_ref, dk_ref, dv_ref, db_ref):
        dk_ref[...] = jnp.zeros((S, HD), F32)
        dv_ref[...] = jnp.zeros((S, HD), F32)
        db_ref[...] = jnp.zeros((N_OFF, GRID_W, KEYS_B), F32)

        def row(r, carry):
            qs, ks, off = _row_window(r)
            rows = pl.ds(qs, GRID_W)
            keys = pl.ds(ks, KEYS_B)
            q = q_ref[rows, :]
            kw = k_ref[keys, :]
            s = lax.dot_general(q, kw, NT, preferred_element_type=F32) * SCALE + b_ref[off]
            p = jnp.exp(s - lse_ref[rows, :])
            do = do_ref[rows, :]
            dobf = do.astype(BF16)
            dsum = jnp.sum(do * o_ref[rows, :].astype(F32), axis=-1, keepdims=True)
            dp = lax.dot_general(dobf, v_ref[keys, :], NT, preferred_element_type=F32)
            ds = p * (dp - dsum)
            db_ref[off] += ds
            dsb = (ds * SCALE).astype(BF16)
            dq_ref[rows, :] = lax.dot_general(dsb, kw, NN, preferred_element_type=F32)
            dk_ref[keys, :] += lax.dot_general(dsb, q, TN, preferred_element_type=F32)
            dv_ref[keys, :] += lax.dot_general(p.astype(BF16), dobf, TN, preferred_element_type=F32)
            return carry

        lax.fori_loop(0, ROWS, row, 0, unroll=8)

    full = pl.BlockSpec((S, HD), lambda h: (0, h))
    slot = pl.BlockSpec((S, HD), lambda h: (0, h))
    tab = pl.BlockSpec((None, N_OFF, GRID_W, KEYS_B), lambda h: (h, 0, 0, 0))
    shape = jax.ShapeDtypeStruct((S, D_BR), F32)
    return pl.pallas_call(
        body, name="attn_b_bwd", grid=(4,),
        in_specs=[full, full, full, tab, slot, slot, pl.BlockSpec((None, S, 1), lambda h: (h, 0, 0))],
        out_specs=[slot, slot, slot, tab],
        out_shape=[shape, shape, shape, jax.ShapeDtypeStruct((4, N_OFF, GRID_W, KEYS_B), F32)],
        compiler_params=_params(("parallel",)),
    )(qn, kn, vb, bias_tab, ob, dob, lse)


def _epi_relu_sq(acc, ex, outs):
    u = jnp.maximum(acc, 0.0)
    outs[0][...] = u.astype(BF16)
    outs[1][...] = (u * u).astype(BF16)


def _epi_relu_sq_bwd(acc, ex, outs):
    outs[0][...] = (acc * (2.0 * ex[0][...].astype(F32))).astype(BF16)


def _epi_loss_head(acc, ex, outs):
    e = acc + ex[0][...] - ex[1][...]
    dy = e * (1.0 / D)
    outs[0][...] = dy
    outs[1][...] = dy.astype(BF16)
    part = (0.5 / D) * jnp.sum(jnp.sum(e * e, axis=-1, keepdims=True), axis=0, keepdims=True)
    first = (pl.program_id(0) == 0) & (pl.program_id(1) == 0)

    @pl.when(first)
    def _():
        outs[2][...] = part

    @pl.when(jnp.logical_not(first))
    def _():
        outs[2][...] += part


def _local_step(x, target, norm_mix, b_gate, gains, rpb_pad, norm_ffn,
                w_in, w_pa, w_pb, w_out, w_up, w_down, weight_grads, riders=lambda name: None):
    def ridden(name, *args, **kwargs):
        ride = riders(name)
        if ride is None:
            return _mm_nt(*args, name=name, **kwargs)
        out, rode = _mm_nt(*args, name=name, rider=ride[0], **kwargs)
        ride[1](rode)
        return out

    cos2, sin2 = _rope_tables()
    expand, keep, sel = _bias_constants()
    w_out3 = w_out[None]

    xn, rstd1 = _rms_fwd(x, norm_mix, name="rms_mix")
    proj = _mm_nn(xn, w_in, tm=1024, tn=1280, name="proj")
    qkv_a, qkv_b = _qk_prep(proj, gains, cos2, sin2)
    fwd_a = [_attn_a_fwd(*qkv_a[gi], gi) for gi in range(3)]
    oa, lse_a = _combine_a([o for o, _ in fwd_a], [l for _, l in fwd_a])
    bias_tab = _rows_to_tab(_bias_expand(rpb_pad, expand, keep, sel))
    ob, lse_b = _attn_b_fwd(*qkv_b, bias_tab)
    mixed, ya, yb = _mix_fwd(oa, ob, w_pa, w_pb, proj, b_gate)
    h1 = _mm_nn(mixed, w_out3, tm=1024, tn=1024, name="out_proj", epi=_epi_residual, extra=(x,))
    hn, rstd2 = _rms_fwd(h1, norm_ffn, name="rms_ffn")
    u, usq = _mm_nn(hn, w_up, tm=1024, tn=1024, name="ffn_up", epi=_epi_relu_sq,
                    out_dtypes=(BF16, BF16))
    dy, dyb, loss = _mm_nn(usq, w_down[0], tm=512, tn=512, name="ffn_down_0", epi=_epi_loss_head,
                           extra=(h1, target), out_dtypes=(F32, BF16), total=True, width=D)
    dy, dyb, loss_1 = _mm_nn(usq, w_down[1], tm=512, tn=512, name="ffn_down_1", epi=_epi_loss_head,
                             extra=(h1, target), out_dtypes=(F32, BF16), total=True, width=D,
                             col0=D // 2, into=(dy, dyb))
    loss = loss + loss_1

    sent = weight_grads("w_down", {5: (usq, dyb)})
    du = _mm_nt(dyb, w_down[0], more_b=(w_down[1],), tm=1024, tn=1024, name="ffn_down_bwd", out_dtype=BF16,
                epi=_epi_relu_sq_bwd, extra=(u,), after=sent)
    sent = weight_grads("w_up", {4: (hn, du)})
    dhn = ridden("ffn_up_bwd", du, w_up, tm=512, tn=512, after=sent)
    dh1, dh1b, g_norm_ffn = _rms_bwd(dhn, h1, rstd2, norm_ffn, dy, name="rms_ffn_bwd", bf16_copy=True)

    dya, dyb2, dproj, g_b = _mix_bwd(dh1b, w_out, proj, b_gate, ya, yb)
    sent = weight_grads("w_mix", {3: (mixed, dh1b), 1: (oa, dya), 2: (ob, dyb2)})
    dob = _mm_nt(dyb2, w_pb, tm=1024, tn=D_BR, name="proj_b_bwd", after=sent)
    prep = _proj_a_bwd(dya, w_pa, oa, lse_a)
    grads_a = [_attn_a_bwd(*qkv_a[gi], *prep[gi], gi) for gi in range(3)]
    dqb, dkb, dvb, dbias = _attn_b_bwd(*qkv_b, bias_tab, ob, dob, lse_b)
    g_rpb = _bias_reduce(dbias)
    dproj, g_gains = _qk_prep_bwd(dproj, proj, gains, cos2, sin2, grads_a, (dqb, dkb, dvb))
    sent = weight_grads("w_in", {0: (xn, dproj)})
    dxn = ridden("proj_bwd", dproj, w_in, tm=256, tn=512, after=sent)
    grad_x, g_norm_mix = _rms_bwd(dxn, x, rstd1, norm_mix, dh1, name="rms_mix_bwd", bf16_copy=False)

    small = (g_norm_mix, g_b, g_gains, g_rpb, g_norm_ffn)
    return loss, grad_x, small


def _cast_bf16(w, *, part=0, parts=1, after=(), tr=256):
    rows, cols = w.shape[0], w.shape[1] // parts
    tr = min(tr, rows)

    def body(w_ref, *rest):
        rest[-1][...] = w_ref[...].astype(BF16)

    return pl.pallas_call(
        body, name=f"cast_{rows}x{cols}_{part}", grid=(rows // tr,),
        in_specs=[pl.BlockSpec((tr, cols), lambda i: (i, part))] + [pl.BlockSpec(memory_space=pl.ANY)] * len(after),
        out_specs=pl.BlockSpec((tr, cols), lambda i: (i, 0)),
        out_shape=jax.ShapeDtypeStruct((rows, cols), BF16), compiler_params=_params(("parallel",)),
    )(w, *after)


def _me_and_peers():
    x, y, c = lax.axis_index("x"), lax.axis_index("y"), lax.axis_index("c")
    me = 4 * x + 2 * y + c
    peers = []
    for k in range(1, N_DEV):
        px = 1 - x if k & 4 else x
        py = 1 - y if k & 2 else y
        pc = 1 - c if k & 1 else c
        peers.append(((px, py, pc), 4 * px + 2 * py + pc))
    return me, peers


def _gather_on_sequencer(shards, name):
    n = len(shards)
    hbm = pltpu.MemorySpace.HBM
    ins = [jax.new_ref(s, memory_space=hbm) for s in shards]
    outs = [jax.empty_ref(jax.ShapeDtypeStruct((N_DEV,) + s.shape, s.dtype), memory_space=hbm) for s in shards]

    @_sequencer(name, ((n, N_DEV - 1), (n, N_DEV - 1), (n,)), 0)
    def launch(send, recv, lsem):
        x, y, c = lax.axis_index("x"), lax.axis_index("y"), lax.axis_index("c")
        me, sibling = (x, y, c), (x, y, 1 - c)
        chips = [(1 - x, y), (x, 1 - y), (1 - x, 1 - y)]
        _handshake([sibling] + [(*chip, c) for chip in chips])

        def copy(w, k, block, to, src=None):
            px, py, pc = block
            dst = outs[w].at[4 * px + 2 * py + pc]
            return pltpu.make_async_remote_copy(dst if src is None else src, dst, send.at[w, k], recv.at[w, k],
                                                device_id=to, device_id_type=MESH)

        local = [pltpu.make_async_copy(ins[w], outs[w].at[4 * x + 2 * y + c], lsem.at[w]) for w in range(n)]
        for cp in local:
            cp.start()
        first = []
        for w in range(n):
            first += [copy(w, 1 + j, me, (*chip, c), src=ins[w]) for j, chip in enumerate(chips)]
            first.append(copy(w, 0, me, sibling, src=ins[w]))
        for cp in first:
            cp.start()
        passed = []
        for w in range(n):
            for j, chip in enumerate(chips):
                copy(w, 1 + j, (*chip, c), me).wait_recv()
                cp = copy(w, 4 + j, (*chip, c), sibling)
                cp.start()
                passed.append(cp)
        for w in range(n):
            copy(w, 0, sibling, me).wait_recv()
            for j, chip in enumerate(chips):
                copy(w, 4 + j, (*chip, 1 - c), me).wait_recv()
        for cp in first + passed:
            cp.wait_send()
        for cp in local:
            cp.wait()

    launch()
    return [o[...] for o in outs]


N_CHIP = 4
CHIPS = ((0, 0), (0, 1), (1, 0), (1, 1))


def _sequencer(name, n_sems, collective_id):
    return functools.partial(
        pl.kernel, mesh=plsc.ScalarSubcoreMesh(axis_name="seq", num_cores=1), name=name,
        scratch_types=tuple(pltpu.SemaphoreType.DMA(s) for s in n_sems),
        compiler_params=pltpu.CompilerParams(collective_id=collective_id))


def _handshake(peers):
    barrier = pltpu.get_barrier_semaphore()
    for peer in peers:
        pl.semaphore_signal(barrier, inc=1, device_id=peer, device_id_type=MESH)
    pl.semaphore_wait(barrier, len(peers))


def _chip_exchange_on_sequencer(parts, name):
    n = len(parts)
    hbm = pltpu.MemorySpace.HBM
    ins = [jax.new_ref(p, memory_space=hbm) for p in parts]
    outs = [jax.empty_ref(jax.ShapeDtypeStruct(p.shape, p.dtype), memory_space=hbm) for p in parts]

    @_sequencer(name, ((n, 3), (n, 3), (n,)), 2)
    def launch(send, recv, lsem):
        x, y, c = lax.axis_index("x"), lax.axis_index("y"), lax.axis_index("c")
        mine = 2 * x + y
        chips = [(1 - x, y), (x, 1 - y), (1 - x, 1 - y)]
        _handshake([(*chip, c) for chip in chips])
        local = [pltpu.make_async_copy(ins[w].at[mine], outs[w].at[mine], lsem.at[w]) for w in range(n)]
        for cp in local:
            cp.start()
        sends = []
        for w in range(n):
            for j, (px, py) in enumerate(chips):
                cp = pltpu.make_async_remote_copy(ins[w].at[2 * px + py], outs[w].at[mine],
                                                  send.at[w, j], recv.at[w, j],
                                                  device_id=(px, py, c), device_id_type=MESH)
                cp.start()
                sends.append(cp)
        for w in range(n):
            for j, (px, py) in enumerate(chips):
                pltpu.make_async_remote_copy(ins[w].at[mine], outs[w].at[2 * px + py],
                                             send.at[w, j], recv.at[w, j],
                                             device_id=(px, py, c), device_id_type=MESH).wait_recv()
        for cp in sends:
            cp.wait_send()
        for cp in local:
            cp.wait()

    launch()
    return [o[...] for o in outs]


GRAD_TILES = (dict(blocks_on="cols", tm=512, tn=1280), dict(blocks_on="cols", tm=512, tn=256),
              dict(blocks_on="cols", tm=512, tn=256), dict(blocks_on="rows", tm=256, tn=2048),
              dict(blocks_on="cols", tm=1024, tn=1024), dict(blocks_on="rows", tm=1024, tn=1024))


def _mm_tn_pair(a, b, *, blocks_on, tm, tn, name):
    t_len, m = a.shape
    n = b.shape[1]
    if blocks_on == "rows":
        rows, cols, inner = m // N_DEV, n, n // tn
        assert tm == rows
        a_spec = pl.BlockSpec((t_len, tm), lambda p, t, blk: (0, blk[p]))
        b_spec = pl.BlockSpec((t_len, tn), lambda p, t, blk: (0, t))
        out_spec = pl.BlockSpec((None, tm, tn), lambda p, t, blk: (
            jnp.maximum(p - N_CHIP, 0), 0, jnp.where(p < N_CHIP, 0, t)))
    else:
        rows, cols, inner = m, n // N_DEV, m // tm
        assert tn == cols
        a_spec = pl.BlockSpec((t_len, tm), lambda p, t, blk: (0, t))
        b_spec = pl.BlockSpec((t_len, tn), lambda p, t, blk: (0, blk[p]))
        out_spec = pl.BlockSpec((None, tm, tn), lambda p, t, blk: (
            jnp.maximum(p - N_CHIP, 0), jnp.where(p < N_CHIP, 0, t), 0))

    def body(blk_ref, a_ref, b_ref, o_ref, land, stage, send_sem, recv_sem):
        del blk_ref
        p, t = pl.program_id(0), pl.program_id(1)
        step = p * inner + t
        x, y, c = lax.axis_index("x"), lax.axis_index("y"), lax.axis_index("c")
        tile = _dot(a_ref[...], b_ref[...], TN)

        def to_sibling(slot, chip, piece):
            return pltpu.make_async_remote_copy(stage.at[slot], land.at[chip, piece], send_sem.at[slot],
                                                recv_sem.at[chip, piece],
                                                device_id=(x, y, 1 - c), device_id_type=MESH)

        @pl.when(p < N_CHIP)
        def _():
            slot = step % 2

            @pl.when(step >= 2)
            def _():
                to_sibling(slot, 0, 0).wait_send()

            stage[slot] = tile.astype(BF16)
            to_sibling(slot, p, t).start()

        @pl.when(step == N_CHIP * inner)
        def _():
            for slot in range(min(2, N_CHIP * inner)):
                to_sibling(slot, 0, 0).wait_send()

        @pl.when(p >= N_CHIP)
        def _():
            chip = p - N_CHIP
            to_sibling(0, chip, t).wait_recv()
            o_ref[...] = (tile + land[chip, t].astype(F32)).astype(BF16)

    c = lax.axis_index("c")
    order = jnp.stack([2 * ch + 1 - c for ch in range(N_CHIP)] + [2 * ch + c for ch in range(N_CHIP)])
    return pl.pallas_call(
        body, name=name,
        grid_spec=pltpu.PrefetchScalarGridSpec(
            num_scalar_prefetch=1, grid=(N_DEV, inner), in_specs=[a_spec, b_spec], out_specs=out_spec,
            scratch_shapes=[pltpu.VMEM((N_CHIP, inner, tm, tn), BF16), pltpu.VMEM((2, tm, tn), BF16),
                            pltpu.SemaphoreType.DMA((2,)), pltpu.SemaphoreType.DMA((N_CHIP, inner))]),
        out_shape=jax.ShapeDtypeStruct((N_CHIP, rows, cols), BF16),
        compiler_params=_params(("arbitrary", "arbitrary")),
    )(order.astype(jnp.int32), a, b)


def _adamw_math(g, w, m, v):
    m2 = B1 * m + (1.0 - B1) * g
    v2 = B2 * v + (1.0 - B2) * (g * g)
    delta = -LR * ((m2 / BC1) / (jnp.sqrt(v2 / BC2) + AEPS) + WD * w)
    return delta, m2, v2


def _adamw_block(ins, outs):
    p_ref, w_ref, m_ref, v_ref = ins
    g = p_ref[0].astype(F32)
    for b in range(1, N_CHIP):
        g = g + p_ref[b].astype(F32)
    delta, m2, v2 = _adamw_math(g, w_ref[...], m_ref[...], v_ref[...])
    for ref, val in zip(outs, (g, delta, m2, v2)):
        ref[...] = val


class _Rider(NamedTuple):
    inputs: tuple
    in_specs: list
    out_shape: list
    out_specs: list
    body: Callable


def _adamw_rider(parts, w, m, v):
    rows, cols = w.shape

    def rider(steps, step_of):
        rr = rows // steps
        blk = pl.BlockSpec((rr, cols), lambda *ids: (step_of(*ids[:2]), 0))
        chips = pl.BlockSpec((N_CHIP, rr, cols), lambda *ids: (0, step_of(*ids[:2]), 0))
        shape = jax.ShapeDtypeStruct((rows, cols), F32)
        return _Rider((parts, w, m, v), [chips, blk, blk, blk], [shape] * 4, [blk] * 4, _adamw_block)

    return rider


def _adamw(parts, w, m, v, *, name, after=(), tr=256):
    rows, cols = w.shape

    def body(*refs):
        _adamw_block(refs[:4], refs[4 + len(after):])

    spec = pl.BlockSpec((tr, cols), lambda i: (i, 0))
    shape = jax.ShapeDtypeStruct((rows, cols), F32)
    return pl.pallas_call(
        body, name=name, grid=(rows // tr,),
        in_specs=[pl.BlockSpec((N_CHIP, tr, cols), lambda i: (0, i, 0)), spec, spec, spec]
        + [pl.BlockSpec(memory_space=pl.ANY)] * len(after),
        out_specs=[spec] * 4, out_shape=[shape] * 4,
        compiler_params=_params(("parallel",)),
    )(parts, w, m, v, *after)


def _small_exchange(part, after=()):
    rows = part.shape[0]

    def body(p_ref, *rest):
        g_ref, buf, send, recv = rest[len(after):]
        me, peers = _me_and_peers()
        buf[me] = p_ref[...]
        sends = []
        for k, (dev, _) in enumerate(peers):
            cp = pltpu.make_async_remote_copy(p_ref, buf.at[me], send.at[k], recv.at[k],
                                              device_id=dev, device_id_type=MESH)
            cp.start()
            sends.append(cp)
        for k, (dev, idx) in enumerate(peers):
            pltpu.make_async_remote_copy(p_ref, buf.at[idx], send.at[k], recv.at[k],
                                         device_id=dev, device_id_type=MESH).wait_recv()
        for cp in sends:
            cp.wait_send()
        g = buf[0]
        for b in range(1, N_DEV):
            g = g + buf[b]
        g_ref[...] = g

    vm = pl.BlockSpec(memory_space=pltpu.VMEM)
    return pl.pallas_call(
        body, name="small_params_exchange",
        in_specs=[vm] + [pl.BlockSpec(memory_space=pl.ANY)] * len(after),
        out_specs=vm, out_shape=jax.ShapeDtypeStruct((rows, HD), F32),
        scratch_shapes=[pltpu.VMEM((N_DEV, rows, HD), F32),
                        pltpu.SemaphoreType.DMA((N_DEV - 1,)), pltpu.SemaphoreType.DMA((N_DEV - 1,))],
    )(part, *after)


def _small_adamw(g, w, m, v):
    def body(g_ref, w_ref, m_ref, v_ref, *outs):
        g = g_ref[...]
        delta, m2, v2 = _adamw_math(g, w_ref[...], m_ref[...], v_ref[...])
        for k, val in enumerate((g, delta, m2, v2)):
            norm_mix, b_gate, qa, ka, qb, kb, rpb, norm_ffn = outs[8 * k:8 * k + 8]
            for dst, row0, n_rows in ((norm_mix, 0, 16), (b_gate, 16, 32), (norm_ffn, 120, 16)):
                for r in range(n_rows):
                    dst[:, r * HD:(r + 1) * HD] = val[row0 + r:row0 + r + 1, :]
            for i, dst in enumerate((qa, ka, qb, kb)):
                dst[...] = val[48 + i:49 + i, :]
            rpb[...] = val[56:120, :]
        outs[32][...] = g[LOSS_ROW:LOSS_ROW + 1, 0:1]

    vm = pl.BlockSpec(memory_space=pltpu.VMEM)
    kinds = [jax.ShapeDtypeStruct(sh, F32) for sh in
             ((1, D), (1, 2 * D), (1, HD), (1, HD), (1, HD), (1, HD), (64, HD), (1, D))]
    outs = pl.pallas_call(
        body, name="small_params_adamw", in_specs=[vm] * 4, out_specs=[vm] * 33,
        out_shape=kinds * 4 + [jax.ShapeDtypeStruct((1, 1), F32)],
    )(g, w, m, v)
    return [outs[8 * k:8 * k + 8] for k in range(4)], outs[32]


def _pack_small(norm_mix, b_gate, qa, ka, qb, kb, rpb, norm_ffn):
    gains = jnp.concatenate([qa, ka, qb, kb, jnp.zeros((4, HD), F32)], axis=0)
    rpb_pad = jnp.pad(rpb.reshape(4 * (2 * WIN_R - 1), 2 * WIN_C - 1), ((0, 4), (0, HD - (2 * WIN_C - 1))))
    return jnp.concatenate([norm_mix.reshape(16, HD), b_gate.reshape(32, HD), gains, rpb_pad,
                            norm_ffn.reshape(16, HD), jnp.zeros((8, HD), F32)], axis=0)


LOSS_ROW = 136


def _rpb_from_rows(rows):
    return rows[:60, :2 * WIN_C - 1].reshape(1, 4, 2 * WIN_R - 1, 2 * WIN_C - 1)


def kernel(x, norm_mix, w_in, b_gate, q_norm_a, k_norm_a, q_norm_b, k_norm_b, rpb_b, w_proj_a, w_proj_b, w_out, norm_ffn, w_up, w_down, loss_target, m_norm_mix, m_w_in, m_b_gate, m_q_norm_a, m_k_norm_a, m_q_norm_b, m_k_norm_b, m_rpb_b, m_w_proj_a, m_w_proj_b, m_w_out, m_norm_ffn, m_w_up, m_w_down, v_norm_mix, v_w_in, v_b_gate, v_q_norm_a, v_k_norm_a, v_q_norm_b, v_k_norm_b, v_rpb_b, v_w_proj_a, v_w_proj_b, v_w_out, v_norm_ffn, v_w_up, v_w_down):
    big_w = (w_in[0], w_proj_a[0], w_proj_b[0], w_out[0], w_up[0], w_down[0])
    big_m = (m_w_in[0], m_w_proj_a[0], m_w_proj_b[0], m_w_out[0], m_w_up[0], m_w_down[0])
    big_v = (v_w_in[0], v_w_proj_a[0], v_w_proj_b[0], v_w_out[0], v_w_up[0], v_w_down[0])
    names = ("w_in", "w_proj_a", "w_proj_b", "w_out", "w_up", "w_down")

    shards = [_cast_bf16(w) for w in big_w[:5]]
    g_in, = _gather_on_sequencer(shards[0:1], "gather_w_in")
    g_pa, g_pb, g_out, g_up = _gather_on_sequencer(shards[1:5], "gather_w_mix_up")
    small_w = _pack_small(norm_mix, b_gate, q_norm_a, k_norm_a, q_norm_b, k_norm_b, rpb_b, norm_ffn)
    small_m = _pack_small(m_norm_mix, m_b_gate, m_q_norm_a, m_k_norm_a, m_q_norm_b, m_k_norm_b, m_rpb_b, m_norm_ffn)
    small_v = _pack_small(v_norm_mix, v_b_gate, v_q_norm_a, v_k_norm_a, v_q_norm_b, v_k_norm_b, v_rpb_b, v_norm_ffn)
    g_down = [_gather_on_sequencer([_cast_bf16(big_w[5], part=h, parts=2, after=(small_w, small_m, small_v) * h)],
                                   f"gather_w_down_{h}")[0].reshape(1, D_FF, D // 2) for h in range(2)]

    upd = [None] * 6
    in_flight = {}

    def weight_grads(tag, operands):
        sums = {i: _mm_tn_pair(a, b, name=f"grad_{names[i]}", **GRAD_TILES[i]) for i, (a, b) in operands.items()}
        new = list(sums.values())
        in_flight.update(zip(sums, _chip_exchange_on_sequencer(new, f"chip_exchange_{tag}")))
        return new

    def riders(name):
        i = {"proj_bwd": 5}.get(name)
        if i is None:
            return None
        return (_adamw_rider(in_flight.pop(i), big_w[i], big_m[i], big_v[i]),
                functools.partial(upd.__setitem__, i))

    loss, grad_x, small_g = _local_step(
        x[0], loss_target[0], norm_mix, b_gate, small_w[48:56], small_w[56:120], norm_ffn,
        g_in, g_pa, g_pb, g_out.reshape(D, D), g_up, g_down, weight_grads, riders)

    g_norm_mix, g_b, g_gains, g_rpb, g_norm_ffn = small_g
    small_part = jnp.concatenate([g_norm_mix.reshape(16, HD), g_b.reshape(32, HD),
                                  g_gains, g_rpb, g_norm_ffn.reshape(16, HD),
                                  jnp.pad(loss, ((0, 7), (0, HD - 1)))], axis=0)
    last = grad_x
    for i, r in in_flight.items():
        if i == 0:
            small_sum = _small_exchange(small_part, after=[last])
            small, total = _small_adamw(small_sum, small_w, small_m, small_v)
            last = total
        upd[i] = _adamw(r, big_w[i], big_m[i], big_v[i], name=f"adamw_{names[i]}", after=[last])
        last = upd[i][0]
    s_g, s_d, s_m, s_v = ((*k[:6], _rpb_from_rows(k[6]), k[7]) for k in small)
    b_g, b_d, b_m, b_v = ([u[j][None] for u in upd] for j in range(4))

    def order(small, big):
        nm, bg, qa, ka, qb, kb, rpb, nf = small
        w_in_, pa_, pb_, out_, up_, down_ = big
        return (nm, w_in_, bg, qa, ka, qb, kb, rpb, pa_, pb_, out_, nf, up_, down_)

    return (total[0, 0], grad_x[None], *order(s_g, b_g), *order(s_d, b_d), *order(s_m, b_m), *order(s_v, b_v))
```

```python
import functools
from typing import Callable, NamedTuple

import jax
import jax.numpy as jnp
import numpy as np
from jax import lax
from jax.experimental import pallas as pl
from jax.experimental.pallas import tpu as pltpu
from jax.experimental.pallas import tpu_sc as plsc

F32 = jnp.float32
BF16 = jnp.bfloat16

N_DEV = 8
S = 2048
D = 2048
HD = 128
NH = 16
NH_A = 12
QKV = NH * HD
D_IN = 3 * QKV + 2 * D
D_BR = 512
D_FF = 4 * D
GRID_W = 64
ROWS = S // GRID_W
WIN_R = 8
WIN_C = 16
EPS = 1e-6
NEG = -1e30
SCALE = HD ** -0.5
ROPE_THETA = 10000.0
DILATIONS = (1, 4, 16)
HALF_A = 64
QB = 128

LR, B1, B2, AEPS, WD, STEP = 0.001, 0.9, 0.999, 1e-08, 0.01, 10
BC1 = 1.0 - B1 ** STEP
BC2 = 1.0 - B2 ** STEP

VMEM_LIMIT = 56 * 1024 * 1024
MESH = pl.DeviceIdType.MESH

NN = (((1,), (0,)), ((), ()))
NT = (((1,), (1,)), ((), ()))
TN = (((0,), (0,)), ((), ()))


def _params(sem):
    return pltpu.CompilerParams(dimension_semantics=sem, vmem_limit_bytes=VMEM_LIMIT)


def _matmul(a, b, *, product, grid, a_spec, b_spec, epi, out_shape, out_specs, name,
            extra=(), extra_specs=(), after=(), carried=False, rider=None, into=()):
    n_extra = len(extra)
    single = not isinstance(out_shape, (list, tuple))
    out_shape = [out_shape] if single else list(out_shape)
    out_specs = [out_specs] if single else list(out_specs)
    ride = rider(grid[0] * grid[1], lambda j, i: j * grid[1] + i) if rider else None
    r_in = list(ride.inputs) if ride else []
    n_main = len(out_shape)

    def body(a_ref, b_ref, *rest):
        n_in = n_extra + len(after) + len(r_in)
        ins, outs = rest[:n_in], rest[n_in + len(into):]
        epi(product(a_ref, b_ref, ins[:n_extra]), ins[:n_extra], outs[:n_main])
        if ride:
            ride.body(ins[n_extra + len(after):], outs[n_main:])

    res = pl.pallas_call(
        body, name=name, grid=grid,
        in_specs=[a_spec, b_spec, *extra_specs, *[pl.BlockSpec(memory_space=pl.ANY)] * len(after),
                  *(ride.in_specs if ride else []), *[pl.BlockSpec(memory_space=pl.ANY)] * len(into)],
        out_specs=out_specs + (ride.out_specs if ride else []),
        out_shape=out_shape + (ride.out_shape if ride else []),
        input_output_aliases={2 + n_extra + len(after) + len(r_in) + k: k for k in range(len(into))},
        compiler_params=_params(("arbitrary", "arbitrary") if carried else ("parallel", "parallel")),
    )(a, b, *extra, *after, *r_in, *into)
    main = res[0] if single else res[:n_main]
    return (main, res[n_main:]) if ride else main


def _dot(x, y, dims):
    return lax.dot_general(x, y, dims, preferred_element_type=F32)


def _epi_store(acc, ex, outs):
    outs[0][...] = acc.astype(outs[0].dtype)


def _epi_residual(acc, ex, outs):
    outs[0][...] = acc + ex[0][...]


def _mm_nn(a, b3, *, tm, tn, name, out_dtypes=(F32,), epi=_epi_store, extra=(), total=False,
           col0=0, width=None, into=()):
    m, kdim = a.shape
    g, _, ng = b3.shape
    n = g * ng
    c0 = col0 // tn
    if tn <= ng:
        npg = ng // tn
        b_spec = pl.BlockSpec((None, kdim, tn), lambda j, i: (j // npg, 0, j % npg))

        def product(a_ref, b_ref, ex):
            return _dot(a_ref[...], b_ref[...], NN)
    else:
        gb = tn // ng
        b_spec = pl.BlockSpec((gb, kdim, ng), lambda j, i: (j, 0, 0))

        def product(a_ref, b_ref, ex):
            return jnp.concatenate([_dot(a_ref[...], b_ref[q], NN) for q in range(gb)], axis=1)

    tile = pl.BlockSpec((tm, tn), lambda j, i: (i, j + c0))
    shapes = [jax.ShapeDtypeStruct((m, width or n), dt) for dt in out_dtypes]
    specs = [tile] * len(shapes)
    if total:
        shapes.append(jax.ShapeDtypeStruct((1, 1), F32))
        specs.append(pl.BlockSpec((1, 1), lambda j, i: (0, 0)))
    single = len(shapes) == 1
    return _matmul(
        a, b3, product=product, grid=(n // tn, m // tm), epi=epi, name=name, carried=total, into=into,
        a_spec=pl.BlockSpec((tm, kdim), lambda j, i: (i, 0)), b_spec=b_spec,
        extra=extra, extra_specs=[tile] * len(extra),
        out_shape=shapes[0] if single else shapes, out_specs=specs[0] if single else specs)


def _mm_nt(a, b3, *, tm, tn, name, out_dtype=F32, epi=_epi_store, extra=(), after=(), rider=None, more_b=()):
    m, kdim = a.shape
    _, n, _ = b3.shape
    n_b = len(more_b)

    def product(a_ref, b_ref, ex):
        acc, k0 = None, 0
        for ref in (b_ref, *ex[:n_b]):
            for q in range(ref.shape[0]):
                part = _dot(a_ref[:, k0:k0 + ref.shape[2]], ref[q], NT)
                acc = part if acc is None else acc + part
                k0 += ref.shape[2]
        return acc

    def write(acc, ex, outs):
        epi(acc, ex[n_b:], outs)

    def w_spec(w):
        return pl.BlockSpec((w.shape[0], tn, w.shape[2]), lambda j, i: (0, j, 0))

    tile = pl.BlockSpec((tm, tn), lambda j, i: (i, j))
    return _matmul(
        a, b3, product=product, grid=(n // tn, m // tm), epi=write, name=name,
        a_spec=pl.BlockSpec((tm, kdim), lambda j, i: (i, 0)), b_spec=w_spec(b3),
        extra=(*more_b, *extra), extra_specs=[w_spec(w) for w in more_b] + [tile] * len(extra),
        after=after, rider=rider,
        out_shape=jax.ShapeDtypeStruct((m, n), out_dtype), out_specs=tile)


def _mm_tn(a, b, *, tm, tn, name, groups=1, out_dtype=BF16):
    t, m = a.shape
    _, n = b.shape
    ng = n // groups
    if tn <= ng:
        npg = ng // tn
        out_spec = pl.BlockSpec((None, tm, tn), lambda j, i: (j // npg, i, j % npg))
        epi = _epi_store

        def product(a_ref, b_ref, ex):
            return _dot(a_ref[...], b_ref[...], TN)
    else:
        gb = tn // ng
        out_spec = pl.BlockSpec((gb, tm, ng), lambda j, i: (j, i, 0))

        def product(a_ref, b_ref, ex):
            return [_dot(a_ref[...], b_ref[:, q * ng:(q + 1) * ng], TN) for q in range(gb)]

        def epi(parts, ex, outs):
            for q, part in enumerate(parts):
                outs[0][q] = part.astype(out_dtype)

    return _matmul(
        a, b, product=product, grid=(n // tn, m // tm), epi=epi, name=name,
        a_spec=pl.BlockSpec((t, tm), lambda j, i: (0, i)),
        b_spec=pl.BlockSpec((t, tn), lambda j, i: (0, j)),
        out_shape=jax.ShapeDtypeStruct((groups, m, ng), out_dtype), out_specs=out_spec)


def _rms_fwd(x, g, *, name, tr=256):
    def body(x_ref, g_ref, y_ref, r_ref):
        xv = x_ref[...]
        r = lax.rsqrt(jnp.mean(xv * xv, axis=-1, keepdims=True) + EPS)
        y_ref[...] = (xv * r * g_ref[...]).astype(BF16)
        r_ref[...] = r

    row = pl.BlockSpec((tr, D), lambda i: (i, 0))
    return pl.pallas_call(
        body, name=name, grid=(S // tr,),
        in_specs=[row, pl.BlockSpec((1, D), lambda i: (0, 0))],
        out_specs=[row, pl.BlockSpec((tr, 1), lambda i: (i, 0))],
        out_shape=[jax.ShapeDtypeStruct((S, D), BF16), jax.ShapeDtypeStruct((S, 1), F32)],
        compiler_params=_params(("parallel",)),
    )(x, g)


def _rms_bwd(dy, x, rstd, g, resid, *, name, bf16_copy, tr=256):
    def body(dy_ref, x_ref, r_ref, g_ref, res_ref, dx_ref, *rest):
        dg_ref = rest[-1]
        r = r_ref[...]
        xh = x_ref[...] * r
        dyv = dy_ref[...]
        t = dyv * g_ref[...]
        dx = r * (t - xh * jnp.mean(t * xh, axis=-1, keepdims=True)) + res_ref[...]
        dx_ref[...] = dx
        if bf16_copy:
            rest[0][...] = dx.astype(BF16)
        part = jnp.sum(dyv * xh, axis=0, keepdims=True)

        @pl.when(pl.program_id(0) == 0)
        def _():
            dg_ref[...] = part

        @pl.when(pl.program_id(0) > 0)
        def _():
            dg_ref[...] += part

    row = pl.BlockSpec((tr, D), lambda i: (i, 0))
    vec = pl.BlockSpec((1, D), lambda i: (0, 0))
    return pl.pallas_call(
        body, name=name, grid=(S // tr,),
        in_specs=[row, row, pl.BlockSpec((tr, 1), lambda i: (i, 0)), vec, row],
        out_specs=[row] + [row] * bf16_copy + [vec],
        out_shape=[jax.ShapeDtypeStruct((S, D), F32)] + [jax.ShapeDtypeStruct((S, D), BF16)] * bf16_copy
        + [jax.ShapeDtypeStruct((1, D), F32)],
        compiler_params=_params(("arbitrary",)),
    )(dy, x, rstd, g, resid)


def _rope_tables():
    pos = np.arange(S, dtype=np.float32)
    inv = (ROPE_THETA ** (-np.arange(0, HD, 2, dtype=np.float32) / HD)).astype(np.float32)
    ang = pos[:, None] * inv[None, :]
    cos, sin = np.cos(ang), np.sin(ang)
    return (jnp.asarray(np.concatenate([cos, cos], axis=-1), F32),
            jnp.asarray(np.concatenate([-sin, sin], axis=-1), F32))


def _swap_halves(t):
    return pltpu.roll(t, HD // 2, axis=1)


TOK = 256


def _lane_block_spec(d, last=HD):
    return pl.BlockSpec((4, TOK // d, d * last), lambda i: (0, i, 0))


def _to_lane_blocks(dst, head, val, d, scr, dtype):
    w = val.shape[1]
    if d == 1:
        dst[head] = val.astype(dtype)
        return
    scr[...] = val
    for r in range(d):
        dst[head, :, r * w:(r + 1) * w] = scr[pl.ds(r, TOK // d, stride=d), :].astype(dtype)


def _from_lane_blocks(src, head, d, w, scr):
    if d == 1:
        return src[head].astype(F32)
    for r in range(d):
        scr[pl.ds(r, TOK // d, stride=d), :] = src[head, :, r * w:(r + 1) * w].astype(F32)
    return scr[...]


def _qk_prep(proj, gains, cos2, sin2):
    def body(q_ref, k_ref, v_ref, g_ref, c_ref, s_ref, *rest):
        outs, scr = rest[:-1], rest[-1]
        cos, sin = c_ref[...], s_ref[...]
        for which, (src, row_a, row_b) in enumerate(((q_ref, 0, 2), (k_ref, 1, 3), (v_ref, None, None))):
            for h in range(NH):
                y = src[:, h * HD:(h + 1) * HD]
                if row_a is not None:
                    y = y * lax.rsqrt(jnp.mean(y * y, axis=-1, keepdims=True) + EPS)
                    if h < NH_A:
                        y = y * g_ref[row_a:row_a + 1, :]
                        y = y * cos + _swap_halves(y) * sin
                    else:
                        y = y * g_ref[row_b:row_b + 1, :]
                if h < NH_A:
                    gi = h // 4
                    _to_lane_blocks(outs[3 * gi + which], h % 4, y, DILATIONS[gi], scr, BF16)
                else:
                    hb = h - NH_A
                    outs[9 + which][:, hb * HD:(hb + 1) * HD] = y.astype(BF16)

    def blk(c):
        return pl.BlockSpec((TOK, QKV), lambda i: (i, c))
    tab = pl.BlockSpec((TOK, HD), lambda i: (i, 0))
    out_specs, out_shape = [], []
    for d in DILATIONS:
        out_specs += [_lane_block_spec(d)] * 3
        out_shape += [jax.ShapeDtypeStruct((4, S // d, d * HD), BF16)] * 3
    out_specs += [pl.BlockSpec((TOK, D_BR), lambda i: (i, 0))] * 3
    out_shape += [jax.ShapeDtypeStruct((S, D_BR), BF16)] * 3
    outs = pl.pallas_call(
        body, name="qk_prep", grid=(S // TOK,),
        in_specs=[blk(0), blk(1), blk(2), pl.BlockSpec((8, HD), lambda i: (0, 0)), tab, tab],
        out_specs=out_specs, out_shape=out_shape,
        scratch_shapes=[pltpu.VMEM((TOK, HD), F32)],
        compiler_params=_params(("parallel",)),
    )(proj, proj, proj, gains, cos2, sin2)
    return [tuple(outs[3 * gi:3 * gi + 3]) for gi in range(3)], tuple(outs[9:12])


def _qk_prep_bwd(dproj, proj, gains, cos2, sin2, grads_a, grads_b):
    def body(dp_in, q_ref, k_ref, g_ref, c_ref, s_ref, *rest):
        grads, (dp_out, dg_ref, scr) = rest[:12], rest[12:]
        del dp_in
        cos, sin = c_ref[...], s_ref[...]

        def grad_of(which, h):
            if h < NH_A:
                gi = h // 4
                return _from_lane_blocks(grads[3 * gi + which], h % 4, DILATIONS[gi], HD, scr)
            hb = h - NH_A
            return grads[9 + which][:, hb * HD:(hb + 1) * HD].astype(F32)

        dg_rows = []
        for which, (src, base, row_a, row_b) in enumerate(((q_ref, 0, 0, 2), (k_ref, QKV, 1, 3))):
            dg_a = jnp.zeros((1, HD), F32)
            dg_b = jnp.zeros((1, HD), F32)
            for h in range(NH):
                t = src[:, h * HD:(h + 1) * HD]
                dy = grad_of(which, h)
                r = lax.rsqrt(jnp.mean(t * t, axis=-1, keepdims=True) + EPS)
                xh = t * r
                if h < NH_A:
                    dy = dy * cos - _swap_halves(dy) * sin
                    gain = g_ref[row_a:row_a + 1, :]
                    dg_a = dg_a + jnp.sum(dy * xh, axis=0, keepdims=True)
                else:
                    gain = g_ref[row_b:row_b + 1, :]
                    dg_b = dg_b + jnp.sum(dy * xh, axis=0, keepdims=True)
                u = dy * gain
                dx = r * (u - xh * jnp.mean(u * xh, axis=-1, keepdims=True))
                dp_out[:, base + h * HD:base + (h + 1) * HD] = dx.astype(BF16)
            dg_rows += [(row_a, dg_a), (row_b, dg_b)]
        for h in range(NH):
            dp_out[:, 2 * QKV + h * HD:2 * QKV + (h + 1) * HD] = grad_of(2, h).astype(BF16)

        @pl.when(pl.program_id(0) == 0)
        def _():
            dg_ref[...] = jnp.zeros((8, HD), F32)

        for row, val in dg_rows:
            dg_ref[row:row + 1, :] += val

    def blk(c):
        return pl.BlockSpec((TOK, QKV), lambda i: (i, c))
    tab = pl.BlockSpec((TOK, HD), lambda i: (i, 0))
    gain_spec = pl.BlockSpec((8, HD), lambda i: (0, 0))
    grad_specs = [s for d in DILATIONS for s in [_lane_block_spec(d)] * 3]
    grad_specs += [pl.BlockSpec((TOK, D_BR), lambda i: (i, 0))] * 3
    return pl.pallas_call(
        body, name="qk_prep_bwd", grid=(S // TOK,),
        in_specs=[pl.BlockSpec(memory_space=pl.ANY), blk(0), blk(1), gain_spec, tab, tab] + grad_specs,
        out_specs=[pl.BlockSpec((TOK, 3 * QKV), lambda i: (i, 0)), gain_spec],
        out_shape=[jax.ShapeDtypeStruct((S, D_IN), BF16), jax.ShapeDtypeStruct((8, HD), F32)],
        input_output_aliases={0: 0},
        scratch_shapes=[pltpu.VMEM((TOK, HD), F32)],
        compiler_params=_params(("arbitrary",)),
    )(dproj, proj, proj, gains, cos2, sin2, *[g for grp in grads_a for g in grp], *grads_b)


def _mix_fwd(oa, ob, w_pa, w_pb, proj, b_gate, *, tr=256):
    def body(oa_ref, ob_ref, pa_ref, pb_ref, la_ref, lb_ref, ba_ref, bb_ref, mix_ref, ya_ref, yb_ref):
        ya = jnp.concatenate([_dot(oa_ref[...], pa_ref[q], NN) for q in range(N_DEV)], axis=1)
        yb = jnp.concatenate([_dot(ob_ref[...], pb_ref[q], NN) for q in range(N_DEV)], axis=1)
        ga = jax.nn.sigmoid(la_ref[...] + ba_ref[...])
        gb = jax.nn.sigmoid(lb_ref[...] + bb_ref[...])
        mix_ref[...] = (ga * ya + gb * yb).astype(BF16)
        ya_ref[...] = ya.astype(BF16)
        yb_ref[...] = yb.astype(BF16)

    row = pl.BlockSpec((tr, D), lambda i: (i, 0))
    branch = pl.BlockSpec((tr, D_BR), lambda i: (i, 0))
    whole = pl.BlockSpec((N_DEV, D_BR, D // N_DEV), lambda i: (0, 0, 0))
    return pl.pallas_call(
        body, name="mix_fwd", grid=(S // tr,),
        in_specs=[branch, branch, whole, whole,
                  pl.BlockSpec((tr, D), lambda i: (i, 3)), pl.BlockSpec((tr, D), lambda i: (i, 4)),
                  pl.BlockSpec((1, D), lambda i: (0, 0)), pl.BlockSpec((1, D), lambda i: (0, 1))],
        out_specs=[row, row, row], out_shape=[jax.ShapeDtypeStruct((S, D), BF16)] * 3,
        compiler_params=_params(("parallel",)),
    )(oa, ob, w_pa, w_pb, proj, proj, b_gate, b_gate)


def _mix_bwd(dh1b, w_out, proj, b_gate, ya, yb, *, tr=256):
    def body(dh_ref, w_ref, la_ref, lb_ref, b_ref, ya_ref, yb_ref, dya_ref, dyb_ref, dp_ref, db_ref):
        dm = _dot(dh_ref[...], w_ref[...], NT)
        parts = []
        for l_ref, y_ref, dy_ref, lo in ((la_ref, ya_ref, dya_ref, 0), (lb_ref, yb_ref, dyb_ref, D)):
            g = jax.nn.sigmoid(l_ref[...] + b_ref[:, lo:lo + D])
            dy_ref[...] = (dm * g).astype(BF16)
            dl = dm * y_ref[...].astype(F32) * g * (1.0 - g)
            dp_ref[:, lo:lo + D] = dl.astype(BF16)
            parts.append(jnp.sum(dl, axis=0, keepdims=True))
        part = jnp.concatenate(parts, axis=1)

        @pl.when(pl.program_id(0) == 0)
        def _():
            db_ref[...] = part

        @pl.when(pl.program_id(0) > 0)
        def _():
            db_ref[...] += part

    row = pl.BlockSpec((tr, D), lambda i: (i, 0))
    vec = pl.BlockSpec((1, 2 * D), lambda i: (0, 0))
    gate_cols = pl.BlockSpec((pl.Element(tr), pl.Element(2 * D)), lambda i: (i * tr, 3 * QKV))
    return pl.pallas_call(
        body, name="mix_bwd", grid=(S // tr,),
        in_specs=[row, pl.BlockSpec((D, D), lambda i: (0, 0)),
                  pl.BlockSpec((tr, D), lambda i: (i, 3)), pl.BlockSpec((tr, D), lambda i: (i, 4)), vec, row, row],
        out_specs=[row, row, gate_cols, vec],
        out_shape=[jax.ShapeDtypeStruct((S, D), BF16), jax.ShapeDtypeStruct((S, D), BF16),
                   jax.ShapeDtypeStruct((S, D_IN), BF16), jax.ShapeDtypeStruct((1, 2 * D), F32)],
        compiler_params=_params(("arbitrary",)),
    )(dh1b, w_out, proj, proj, b_gate, ya, yb)


def _band_blocks(m_len):
    wk = min(m_len, QB + 2 * QB)
    return [(qb * QB, min(max(qb * QB - QB, 0), m_len - wk), wk) for qb in range(m_len // QB)]


def _band_scores(q, kw, q0, k0, wk):
    s = _dot(q, kw, NT) * SCALE
    qpos = q0 + lax.broadcasted_iota(jnp.int32, (QB, 1), 0)
    kpos = k0 + lax.broadcasted_iota(jnp.int32, (1, wk), 1)
    return jnp.where(jnp.abs(kpos - qpos) <= HALF_A, s, NEG)


def _attn_a_fwd(q, k, v, gi):
    d = DILATIONS[gi]
    m_len = S // d

    def body(q_ref, k_ref, v_ref, o_ref, lse_ref):
        for r in range(d):
            lanes = slice(r * HD, (r + 1) * HD)
            for q0, k0, wk in _band_blocks(m_len):
                s = _band_scores(q_ref[q0:q0 + QB, lanes], k_ref[k0:k0 + wk, lanes], q0, k0, wk)
                m = jnp.max(s, axis=-1, keepdims=True)
                p = jnp.exp(s - m)
                l = jnp.sum(p, axis=-1, keepdims=True)
                o_ref[q0:q0 + QB, lanes] = _dot(p.astype(BF16), v_ref[k0:k0 + wk, lanes], NN) / l
                lse_ref[q0:q0 + QB, r:r + 1] = m + jnp.log(l)

    head = pl.BlockSpec((None, m_len, d * HD), lambda h: (h, 0, 0))
    stat = pl.BlockSpec((None, m_len, d), lambda h: (h, 0, 0))
    return pl.pallas_call(
        body, name=f"attn_a_fwd_{gi}", grid=(4,),
        in_specs=[head, head, head], out_specs=[head, stat],
        out_shape=[jax.ShapeDtypeStruct((4, m_len, d * HD), F32), jax.ShapeDtypeStruct((4, m_len, d), F32)],
        compiler_params=_params(("parallel",)),
    )(q, k, v)


def _combine_a(os, lses):
    def body(o0, o1, o2, l0, l1, l2, oa_ref, lse_ref, scr, scr1):
        for h in range(4):
            o = [_from_lane_blocks(ref, h, d, HD, scr) for ref, d in zip((o0, o1, o2), DILATIONS)]
            a, b, c = (_from_lane_blocks(ref, h, d, 1, scr1) for ref, d in zip((l0, l1, l2), DILATIONS))
            m = jnp.maximum(jnp.maximum(a, b), c)
            wa, wb, wc = jnp.exp(a - m), jnp.exp(b - m), jnp.exp(c - m)
            tot = wa + wb + wc
            oa_ref[:, h * HD:(h + 1) * HD] = ((wa * o[0] + wb * o[1] + wc * o[2]) / tot).astype(BF16)
            lse_ref[h] = m + jnp.log(tot)

    return pl.pallas_call(
        body, name="combine_a", grid=(S // TOK,),
        in_specs=[_lane_block_spec(d) for d in DILATIONS] + [_lane_block_spec(d, 1) for d in DILATIONS],
        out_specs=[pl.BlockSpec((TOK, D_BR), lambda i: (i, 0)), pl.BlockSpec((4, TOK, 1), lambda i: (0, i, 0))],
        out_shape=[jax.ShapeDtypeStruct((S, D_BR), BF16), jax.ShapeDtypeStruct((4, S, 1), F32)],
        scratch_shapes=[pltpu.VMEM((TOK, HD), F32), pltpu.VMEM((TOK, 1), F32)],
        compiler_params=_params(("parallel",)),
    )(*os, *lses)


def _proj_a_bwd(dya, w_pa, oa, lse):
    kg = D // N_DEV

    def body(dy_ref, w_ref, o_ref, l_ref, *rest):
        outs, (scr, scr1) = rest[:9], rest[9:]
        doa = _dot(dy_ref[:, 0:kg], w_ref[0], NT)
        for q in range(1, N_DEV):
            doa = doa + _dot(dy_ref[:, q * kg:(q + 1) * kg], w_ref[q], NT)
        for h in range(4):
            do = doa[:, h * HD:(h + 1) * HD]
            dsum = jnp.sum(do * o_ref[:, h * HD:(h + 1) * HD].astype(F32), axis=-1, keepdims=True)
            for gi, d in enumerate(DILATIONS):
                _to_lane_blocks(outs[3 * gi], h, do, d, scr, BF16)
                _to_lane_blocks(outs[3 * gi + 1], h, l_ref[h], d, scr1, F32)
                _to_lane_blocks(outs[3 * gi + 2], h, dsum, d, scr1, F32)

    row = pl.BlockSpec((TOK, D_BR), lambda i: (i, 0))
    out_specs, out_shape = [], []
    for d in DILATIONS:
        out_specs += [_lane_block_spec(d), _lane_block_spec(d, 1), _lane_block_spec(d, 1)]
        out_shape += [jax.ShapeDtypeStruct((4, S // d, d * HD), BF16)] + [jax.ShapeDtypeStruct((4, S // d, d), F32)] * 2
    outs = pl.pallas_call(
        body, name="proj_a_bwd", grid=(S // TOK,),
        in_specs=[pl.BlockSpec((TOK, D), lambda i: (i, 0)),
                  pl.BlockSpec((N_DEV, D_BR, kg), lambda i: (0, 0, 0)),
                  row, pl.BlockSpec((4, TOK, 1), lambda i: (0, i, 0))],
        out_specs=out_specs, out_shape=out_shape,
        scratch_shapes=[pltpu.VMEM((TOK, HD), F32), pltpu.VMEM((TOK, 1), F32)],
        compiler_params=_params(("parallel",)),
    )(dya, w_pa, oa, lse)
    return [tuple(outs[3 * gi:3 * gi + 3]) for gi in range(3)]


def _attn_a_bwd(q, k, v, do, lse, dsum, gi):
    d = DILATIONS[gi]
    m_len = S // d

    def body(q_ref, k_ref, v_ref, do_ref, lse_ref, dsum_ref, dq_ref, dk_out, dv_out, dk_ref, dv_ref):
        dk_ref[...] = jnp.zeros((m_len, d * HD), F32)
        dv_ref[...] = jnp.zeros((m_len, d * HD), F32)
        for r in range(d):
            lanes = slice(r * HD, (r + 1) * HD)
            for q0, k0, wk in _band_blocks(m_len):
                rows, keys = slice(q0, q0 + QB), slice(k0, k0 + wk)
                qv, kw, vw, dov = q_ref[rows, lanes], k_ref[keys, lanes], v_ref[keys, lanes], do_ref[rows, lanes]
                p = jnp.exp(_band_scores(qv, kw, q0, k0, wk) - lse_ref[rows, r:r + 1])
                ds = (p * (_dot(dov, vw, NT) - dsum_ref[rows, r:r + 1]) * SCALE).astype(BF16)
                dq_ref[rows, lanes] = _dot(ds, kw, NN).astype(BF16)
                dk_ref[keys, lanes] += _dot(ds, qv, TN)
                dv_ref[keys, lanes] += _dot(p.astype(BF16), dov, TN)
        dk_out[...] = dk_ref[...].astype(BF16)
        dv_out[...] = dv_ref[...].astype(BF16)

    head = pl.BlockSpec((None, m_len, d * HD), lambda h: (h, 0, 0))
    stat = pl.BlockSpec((None, m_len, d), lambda h: (h, 0, 0))
    shape = jax.ShapeDtypeStruct((4, m_len, d * HD), BF16)
    return pl.pallas_call(
        body, name=f"attn_a_bwd_{gi}", grid=(4,),
        in_specs=[head, head, head, head, stat, stat], out_specs=[head, head, head],
        out_shape=[shape, shape, shape],
        scratch_shapes=[pltpu.VMEM((m_len, d * HD), F32)] * 2,
        compiler_params=_params(("arbitrary",)),
    )(q, k, v, do, lse, dsum)


KEYS_B = WIN_R * GRID_W
N_OFF = WIN_R


def _bias_constants():
    q = np.arange(GRID_W)[:, None]
    kc = np.arange(GRID_W)[None, :]
    dc = np.clip(kc - q, -(WIN_C - 1), WIN_C - 1) + (WIN_C - 1)
    expand = np.zeros((HD, GRID_W * GRID_W), np.float32)
    expand[dc.reshape(-1), np.arange(GRID_W * GRID_W)] = 1.0
    cs = np.clip(q - WIN_C // 2, 0, GRID_W - WIN_C)
    keep = ((kc >= cs) & (kc < cs + WIN_C)).reshape(1, -1).astype(np.float32)
    sel = np.zeros((64, 4 * N_OFF * WIN_R), np.float32)
    for h in range(4):
        for off in range(N_OFF):
            for j in range(WIN_R):
                sel[h * (2 * WIN_R - 1) + off + j, (h * N_OFF + off) * WIN_R + j] = 1.0
    return jnp.asarray(expand), jnp.asarray(keep), jnp.asarray(sel)


def _bias_expand(rpb_pad, expand, keep, sel):
    def body(r_ref, e_ref, k_ref, s_ref, o_ref):
        t = lax.dot_general(r_ref[...], e_ref[...], NN, precision=lax.Precision.HIGHEST,
                            preferred_element_type=F32)
        rows = lax.dot_general(s_ref[...], t, TN, precision=lax.Precision.HIGHEST,
                               preferred_element_type=F32)
        o_ref[...] = jnp.where(k_ref[...] > 0.5, rows, NEG)

    return pl.pallas_call(
        body, name="bias_expand",
        out_shape=jax.ShapeDtypeStruct((4 * N_OFF * WIN_R, GRID_W * GRID_W), F32),
        compiler_params=pltpu.CompilerParams(vmem_limit_bytes=VMEM_LIMIT),
    )(rpb_pad, expand, keep, sel)


def _bias_reduce(dbias_tab):
    lane0 = GRID_W - WIN_C
    flip = np.zeros((GRID_W, GRID_W), np.float32)
    flip[np.arange(GRID_W), GRID_W - 1 - np.arange(GRID_W)] = 1.0
    place = np.zeros((WIN_R, 64, 4 * N_OFF), np.float32)
    for j in range(WIN_R):
        for h in range(4):
            for off in range(N_OFF):
                place[j, h * (2 * WIN_R - 1) + off + j, h * N_OFF + off] = 1.0

    def exact(x, y):
        return lax.dot_general(x, y, NN, precision=lax.Precision.HIGHEST, preferred_element_type=F32)

    def body(x_ref, flip_ref, place_ref, o_ref, z_ref):
        for h in range(4):
            for off in range(N_OFF):
                lined_up = pltpu.roll(exact(flip_ref[...], x_ref[h, off]), 0, axis=1, stride=1, stride_axis=0)
                z_ref[h * N_OFF + off:h * N_OFF + off + 1, :] = jnp.sum(lined_up, axis=0, keepdims=True)
        acc = jnp.zeros((64, HD), F32)
        for j in range(WIN_R):
            at_zero = pltpu.roll(z_ref[...], (KEYS_B - (j * GRID_W + lane0)) % KEYS_B, axis=1)[:, :HD]
            acc = acc + exact(place_ref[j], at_zero)
        lane = lax.broadcasted_iota(jnp.int32, (64, HD), 1)
        o_ref[...] = jnp.where(lane < 2 * WIN_C - 1, acc, 0.0)

    return pl.pallas_call(
        body, name="bias_reduce", out_shape=jax.ShapeDtypeStruct((64, HD), F32),
        scratch_shapes=[pltpu.VMEM((4 * N_OFF, KEYS_B), F32)],
        compiler_params=pltpu.CompilerParams(vmem_limit_bytes=VMEM_LIMIT),
    )(dbias_tab, jnp.asarray(flip), jnp.asarray(place))


def _rows_to_tab(rows):
    t = rows.reshape(4, N_OFF, WIN_R, GRID_W, GRID_W)
    return t.transpose(0, 1, 3, 2, 4).reshape(4, N_OFF, GRID_W, KEYS_B)


def _row_window(r):
    r0 = jnp.clip(r - WIN_R // 2, 0, ROWS - WIN_R)
    off = r0 + (WIN_R - 1) - r
    return pl.multiple_of(r * GRID_W, GRID_W), pl.multiple_of(r0 * GRID_W, GRID_W), off


def _attn_b_fwd(qn, kn, vb, bias_tab):
    def body(q_ref, k_ref, v_ref, b_ref, o_ref, lse_ref):
        def row(r, carry):
            qs, ks, off = _row_window(r)
            q = q_ref[pl.ds(qs, GRID_W), :]
            s = lax.dot_general(q, k_ref[pl.ds(ks, KEYS_B), :], NT, preferred_element_type=F32) * SCALE
            s = s + b_ref[off]
            m = jnp.max(s, axis=-1, keepdims=True)
            p = jnp.exp(s - m)
            l = jnp.sum(p, axis=-1, keepdims=True)
            o = lax.dot_general(p.astype(BF16), v_ref[pl.ds(ks, KEYS_B), :], NN, preferred_element_type=F32)
            o_ref[pl.ds(qs, GRID_W), :] = (o / l).astype(BF16)
            lse_ref[pl.ds(qs, GRID_W), :] = m + jnp.log(l)
            return carry

        lax.fori_loop(0, ROWS, row, 0, unroll=8)

    full = pl.BlockSpec((S, HD), lambda h: (0, h))
    return pl.pallas_call(
        body, name="attn_b_fwd", grid=(4,),
        in_specs=[full, full, full, pl.BlockSpec((None, N_OFF, GRID_W, KEYS_B), lambda h: (h, 0, 0, 0))],
        out_specs=[pl.BlockSpec((S, HD), lambda h: (0, h)), pl.BlockSpec((None, S, 1), lambda h: (h, 0, 0))],
        out_shape=[jax.ShapeDtypeStruct((S, D_BR), BF16), jax.ShapeDtypeStruct((4, S, 1), F32)],
        compiler_params=_params(("parallel",)),
    )(qn, kn, vb, bias_tab)


def _attn_b_bwd(qn, kn, vb, bias_tab, ob, dob, lse):
    def body(q_ref, k_ref, v_ref, b_ref, o_ref, do_ref, lse_ref, dq_ref, dk_out, dv_out, db_ref, dk_ref, dv_ref):
        dk_ref[...] = jnp.zeros((S, HD), F32)
        dv_ref[...] = jnp.zeros((S, HD), F32)
        db_ref[...] = jnp.zeros((N_OFF, GRID_W, KEYS_B), F32)

        def row(r, carry):
            qs, ks, off = _row_window(r)
            rows = pl.ds(qs, GRID_W)
            keys = pl.ds(ks, KEYS_B)
            q = q_ref[rows, :]
            kw = k_ref[keys, :]
            s = lax.dot_general(q, kw, NT, preferred_element_type=F32) * SCALE + b_ref[off]
            p = jnp.exp(s - lse_ref[rows, :])
            do = do_ref[rows, :]
            dobf = do.astype(BF16)
            dsum = jnp.sum(do * o_ref[rows, :].astype(F32), axis=-1, keepdims=True)
            dp = lax.dot_general(dobf, v_ref[keys, :], NT, preferred_element_type=F32)
            ds = p * (dp - dsum)
            db_ref[off] += ds
            dsb = (ds * SCALE).astype(BF16)
            dq_ref[rows, :] = lax.dot_general(dsb, kw, NN, preferred_element_type=F32).astype(BF16)
            dk_ref[keys, :] += lax.dot_general(dsb, q, TN, preferred_element_type=F32)
            dv_ref[keys, :] += lax.dot_general(p.astype(BF16), dobf, TN, preferred_element_type=F32)
            return carry

        lax.fori_loop(0, ROWS, row, 0, unroll=8)
        dk_out[...] = dk_ref[...].astype(BF16)
        dv_out[...] = dv_ref[...].astype(BF16)

    full = pl.BlockSpec((S, HD), lambda h: (0, h))
    slot = pl.BlockSpec((S, HD), lambda h: (0, h))
    tab = pl.BlockSpec((None, N_OFF, GRID_W, KEYS_B), lambda h: (h, 0, 0, 0))
    shape = jax.ShapeDtypeStruct((S, D_BR), BF16)
    return pl.pallas_call(
        body, name="attn_b_bwd", grid=(4,),
        in_specs=[full, full, full, tab, slot, slot, pl.BlockSpec((None, S, 1), lambda h: (h, 0, 0))],
        out_specs=[slot, slot, slot, tab],
        out_shape=[shape, shape, shape, jax.ShapeDtypeStruct((4, N_OFF, GRID_W, KEYS_B), F32)],
        scratch_shapes=[pltpu.VMEM((S, HD), F32)] * 2,
        compiler_params=_params(("arbitrary",)),
    )(qn, kn, vb, bias_tab, ob, dob, lse)


def _epi_relu_sq(acc, ex, outs):
    u = jnp.maximum(acc, 0.0)
    outs[0][...] = u.astype(BF16)
    outs[1][...] = (u * u).astype(BF16)


def _epi_relu_sq_bwd(acc, ex, outs):
    outs[0][...] = (acc * (2.0 * ex[0][...].astype(F32))).astype(BF16)


def _epi_loss_head(acc, ex, outs):
    e = acc + ex[0][...] - ex[1][...]
    dy = e * (1.0 / D)
    outs[0][...] = dy
    outs[1][...] = dy.astype(BF16)
    part = (0.5 / D) * jnp.sum(jnp.sum(e * e, axis=-1, keepdims=True), axis=0, keepdims=True)
    first = (pl.program_id(0) == 0) & (pl.program_id(1) == 0)

    @pl.when(first)
    def _():
        outs[2][...] = part

    @pl.when(jnp.logical_not(first))
    def _():
        outs[2][...] += part


def _local_step(x, target, norm_mix, b_gate, gains, rpb_pad, norm_ffn,
                w_in, w_pa, w_pb, w_out, w_up, w_down, weight_grads, riders=lambda name: None):
    def ridden(name, *args, **kwargs):
        ride = riders(name)
        if ride is None:
            return _mm_nt(*args, name=name, **kwargs)
        out, rode = _mm_nt(*args, name=name, rider=ride[0], **kwargs)
        ride[1](rode)
        return out

    cos2, sin2 = _rope_tables()
    expand, keep, sel = _bias_constants()
    w_out3 = w_out[None]

    xn, rstd1 = _rms_fwd(x, norm_mix, name="rms_mix")
    proj = _mm_nn(xn, w_in, tm=1024, tn=1280, name="proj")
    qkv_a, qkv_b = _qk_prep(proj, gains, cos2, sin2)
    fwd_a = [_attn_a_fwd(*qkv_a[gi], gi) for gi in range(3)]
    oa, lse_a = _combine_a([o for o, _ in fwd_a], [l for _, l in fwd_a])
    bias_tab = _rows_to_tab(_bias_expand(rpb_pad, expand, keep, sel))
    ob, lse_b = _attn_b_fwd(*qkv_b, bias_tab)
    mixed, ya, yb = _mix_fwd(oa, ob, w_pa, w_pb, proj, b_gate)
    h1 = _mm_nn(mixed, w_out3, tm=1024, tn=1024, name="out_proj", epi=_epi_residual, extra=(x,))
    hn, rstd2 = _rms_fwd(h1, norm_ffn, name="rms_ffn")
    u, usq = _mm_nn(hn, w_up, tm=1024, tn=1024, name="ffn_up", epi=_epi_relu_sq,
                    out_dtypes=(BF16, BF16))
    dy, dyb, loss = _mm_nn(usq, w_down[0], tm=512, tn=512, name="ffn_down_0", epi=_epi_loss_head,
                           extra=(h1, target), out_dtypes=(F32, BF16), total=True, width=D)
    dy, dyb, loss_1 = _mm_nn(usq, w_down[1], tm=512, tn=512, name="ffn_down_1", epi=_epi_loss_head,
                             extra=(h1, target), out_dtypes=(F32, BF16), total=True, width=D,
                             col0=D // 2, into=(dy, dyb))
    loss = loss + loss_1

    sent = weight_grads("w_down", {5: (usq, dyb)})
    du = _mm_nt(dyb, w_down[0], more_b=(w_down[1],), tm=1024, tn=1024, name="ffn_down_bwd", out_dtype=BF16,
                epi=_epi_relu_sq_bwd, extra=(u,), after=sent)
    sent = weight_grads("w_up", {4: (hn, du)})
    dhn = ridden("ffn_up_bwd", du, w_up, tm=512, tn=512, after=sent)
    dh1, dh1b, g_norm_ffn = _rms_bwd(dhn, h1, rstd2, norm_ffn, dy, name="rms_ffn_bwd", bf16_copy=True)

    dya, dyb2, dproj, g_b = _mix_bwd(dh1b, w_out, proj, b_gate, ya, yb)
    sent = weight_grads("w_mix", {3: (mixed, dh1b), 1: (oa, dya), 2: (ob, dyb2)})
    dob = _mm_nt(dyb2, w_pb, tm=1024, tn=D_BR, name="proj_b_bwd", after=sent)
    prep = _proj_a_bwd(dya, w_pa, oa, lse_a)
    grads_a = [_attn_a_bwd(*qkv_a[gi], *prep[gi], gi) for gi in range(3)]
    dqb, dkb, dvb, dbias = _attn_b_bwd(*qkv_b, bias_tab, ob, dob, lse_b)
    g_rpb = _bias_reduce(dbias)
    dproj, g_gains = _qk_prep_bwd(dproj, proj, gains, cos2, sin2, grads_a, (dqb, dkb, dvb))
    sent = weight_grads("w_in", {0: (xn, dproj)})
    dxn = ridden("proj_bwd", dproj, w_in, tm=256, tn=512, after=sent)
    grad_x, g_norm_mix = _rms_bwd(dxn, x, rstd1, norm_mix, dh1, name="rms_mix_bwd", bf16_copy=False)

    small = (g_norm_mix, g_b, g_gains, g_rpb, g_norm_ffn)
    return loss, grad_x, small


def _cast_bf16(w, *, part=0, parts=1, after=(), tr=256):
    rows, cols = w.shape[0], w.shape[1] // parts
    tr = min(tr, rows)

    def body(w_ref, *rest):
        rest[-1][...] = w_ref[...].astype(BF16)

    return pl.pallas_call(
        body, name=f"cast_{rows}x{cols}_{part}", grid=(rows // tr,),
        in_specs=[pl.BlockSpec((tr, cols), lambda i: (i, part))] + [pl.BlockSpec(memory_space=pl.ANY)] * len(after),
        out_specs=pl.BlockSpec((tr, cols), lambda i: (i, 0)),
        out_shape=jax.ShapeDtypeStruct((rows, cols), BF16), compiler_params=_params(("parallel",)),
    )(w, *after)


def _me_and_peers():
    x, y, c = lax.axis_index("x"), lax.axis_index("y"), lax.axis_index("c")
    me = 4 * x + 2 * y + c
    peers = []
    for k in range(1, N_DEV):
        px = 1 - x if k & 4 else x
        py = 1 - y if k & 2 else y
        pc = 1 - c if k & 1 else c
        peers.append(((px, py, pc), 4 * px + 2 * py + pc))
    return me, peers


def _gather_on_sequencer(shards, name):
    n = len(shards)
    hbm = pltpu.MemorySpace.HBM
    ins = [jax.new_ref(s, memory_space=hbm) for s in shards]
    outs = [jax.empty_ref(jax.ShapeDtypeStruct((N_DEV,) + s.shape, s.dtype), memory_space=hbm) for s in shards]

    @_sequencer(name, ((n, N_DEV - 1), (n, N_DEV - 1), (n,)), 0)
    def launch(send, recv, lsem):
        x, y, c = lax.axis_index("x"), lax.axis_index("y"), lax.axis_index("c")
        me, sibling = (x, y, c), (x, y, 1 - c)
        chips = [(1 - x, y), (x, 1 - y), (1 - x, 1 - y)]
        _handshake([sibling] + [(*chip, c) for chip in chips])

        def copy(w, k, block, to, src=None):
            px, py, pc = block
            dst = outs[w].at[4 * px + 2 * py + pc]
            return pltpu.make_async_remote_copy(dst if src is None else src, dst, send.at[w, k], recv.at[w, k],
                                                device_id=to, device_id_type=MESH)

        local = [pltpu.make_async_copy(ins[w], outs[w].at[4 * x + 2 * y + c], lsem.at[w]) for w in range(n)]
        for cp in local:
            cp.start()
        first = []
        for w in range(n):
            first += [copy(w, 1 + j, me, (*chip, c), src=ins[w]) for j, chip in enumerate(chips)]
            first.append(copy(w, 0, me, sibling, src=ins[w]))
        for cp in first:
            cp.start()
        passed = []
        for w in range(n):
            for j, chip in enumerate(chips):
                copy(w, 1 + j, (*chip, c), me).wait_recv()
                cp = copy(w, 4 + j, (*chip, c), sibling)
                cp.start()
                passed.append(cp)
        for w in range(n):
            copy(w, 0, sibling, me).wait_recv()
            for j, chip in enumerate(chips):
                copy(w, 4 + j, (*chip, 1 - c), me).wait_recv()
        for cp in first + passed:
            cp.wait_send()
        for cp in local:
            cp.wait()

    launch()
    return [o[...] for o in outs]


N_CHIP = 4
CHIPS = ((0, 0), (0, 1), (1, 0), (1, 1))


def _sequencer(name, n_sems, collective_id):
    return functools.partial(
        pl.kernel, mesh=plsc.ScalarSubcoreMesh(axis_name="seq", num_cores=1), name=name,
        scratch_types=tuple(pltpu.SemaphoreType.DMA(s) for s in n_sems),
        compiler_params=pltpu.CompilerParams(collective_id=collective_id))


def _handshake(peers):
    barrier = pltpu.get_barrier_semaphore()
    for peer in peers:
        pl.semaphore_signal(barrier, inc=1, device_id=peer, device_id_type=MESH)
    pl.semaphore_wait(barrier, len(peers))


def _chip_exchange_on_sequencer(parts, name):
    n = len(parts)
    hbm = pltpu.MemorySpace.HBM
    ins = [jax.new_ref(p, memory_space=hbm) for p in parts]
    outs = [jax.empty_ref(jax.ShapeDtypeStruct(p.shape, p.dtype), memory_space=hbm) for p in parts]

    @_sequencer(name, ((n, 3), (n, 3), (n,)), 2)
    def launch(send, recv, lsem):
        x, y, c = lax.axis_index("x"), lax.axis_index("y"), lax.axis_index("c")
        mine = 2 * x + y
        chips = [(1 - x, y), (x, 1 - y), (1 - x, 1 - y)]
        _handshake([(*chip, c) for chip in chips])
        local = [pltpu.make_async_copy(ins[w].at[mine], outs[w].at[mine], lsem.at[w]) for w in range(n)]
        for cp in local:
            cp.start()
        sends = []
        for w in range(n):
            for j, (px, py) in enumerate(chips):
                cp = pltpu.make_async_remote_copy(ins[w].at[2 * px + py], outs[w].at[mine],
                                                  send.at[w, j], recv.at[w, j],
                                                  device_id=(px, py, c), device_id_type=MESH)
                cp.start()
                sends.append(cp)
        for w in range(n):
            for j, (px, py) in enumerate(chips):
                pltpu.make_async_remote_copy(ins[w].at[mine], outs[w].at[2 * px + py],
                                             send.at[w, j], recv.at[w, j],
                                             device_id=(px, py, c), device_id_type=MESH).wait_recv()
        for cp in sends:
            cp.wait_send()
        for cp in local:
            cp.wait()

    launch()
    return [o[...] for o in outs]


GRAD_TILES = (dict(blocks_on="cols", tm=512, tn=1280), dict(blocks_on="cols", tm=512, tn=256),
              dict(blocks_on="cols", tm=512, tn=256), dict(blocks_on="rows", tm=256, tn=2048),
              dict(blocks_on="cols", tm=1024, tn=1024), dict(blocks_on="rows", tm=1024, tn=1024))


def _mm_tn_pair(a, b, *, blocks_on, tm, tn, name):
    t_len, m = a.shape
    n = b.shape[1]
    if blocks_on == "rows":
        rows, cols, inner = m // N_DEV, n, n // tn
        assert tm == rows
        a_spec = pl.BlockSpec((t_len, tm), lambda p, t, blk: (0, blk[p]))
        b_spec = pl.BlockSpec((t_len, tn), lambda p, t, blk: (0, t))
        out_spec = pl.BlockSpec((None, tm, tn), lambda p, t, blk: (
            jnp.maximum(p - N_CHIP, 0), 0, jnp.where(p < N_CHIP, 0, t)))
    else:
        rows, cols, inner = m, n // N_DEV, m // tm
        assert tn == cols
        a_spec = pl.BlockSpec((t_len, tm), lambda p, t, blk: (0, t))
        b_spec = pl.BlockSpec((t_len, tn), lambda p, t, blk: (0, blk[p]))
        out_spec = pl.BlockSpec((None, tm, tn), lambda p, t, blk: (
            jnp.maximum(p - N_CHIP, 0), jnp.where(p < N_CHIP, 0, t), 0))

    def body(blk_ref, a_ref, b_ref, o_ref, land, stage, send_sem, recv_sem):
        del blk_ref
        p, t = pl.program_id(0), pl.program_id(1)
        step = p * inner + t
        x, y, c = lax.axis_index("x"), lax.axis_index("y"), lax.axis_index("c")
        tile = _dot(a_ref[...], b_ref[...], TN)

        def to_sibling(slot, chip, piece):
            return pltpu.make_async_remote_copy(stage.at[slot], land.at[chip, piece], send_sem.at[slot],
                                                recv_sem.at[chip, piece],
                                                device_id=(x, y, 1 - c), device_id_type=MESH)

        @pl.when(p < N_CHIP)
        def _():
            slot = step % 2

            @pl.when(step >= 2)
            def _():
                to_sibling(slot, 0, 0).wait_send()

            stage[slot] = tile.astype(BF16)
            to_sibling(slot, p, t).start()

        @pl.when(step == N_CHIP * inner)
        def _():
            for slot in range(min(2, N_CHIP * inner)):
                to_sibling(slot, 0, 0).wait_send()

        @pl.when(p >= N_CHIP)
        def _():
            chip = p - N_CHIP
            to_sibling(0, chip, t).wait_recv()
            o_ref[...] = (tile + land[chip, t].astype(F32)).astype(BF16)

    c = lax.axis_index("c")
    order = jnp.stack([2 * ch + 1 - c for ch in range(N_CHIP)] + [2 * ch + c for ch in range(N_CHIP)])
    return pl.pallas_call(
        body, name=name,
        grid_spec=pltpu.PrefetchScalarGridSpec(
            num_scalar_prefetch=1, grid=(N_DEV, inner), in_specs=[a_spec, b_spec], out_specs=out_spec,
            scratch_shapes=[pltpu.VMEM((N_CHIP, inner, tm, tn), BF16), pltpu.VMEM((2, tm, tn), BF16),
                            pltpu.SemaphoreType.DMA((2,)), pltpu.SemaphoreType.DMA((N_CHIP, inner))]),
        out_shape=jax.ShapeDtypeStruct((N_CHIP, rows, cols), BF16),
        compiler_params=_params(("arbitrary", "arbitrary")),
    )(order.astype(jnp.int32), a, b)


def _adamw_math(g, w, m, v):
    m2 = B1 * m + (1.0 - B1) * g
    v2 = B2 * v + (1.0 - B2) * (g * g)
    delta = -LR * ((m2 / BC1) / (jnp.sqrt(v2 / BC2) + AEPS) + WD * w)
    return delta, m2, v2


def _adamw_block(ins, outs):
    p_ref, w_ref, m_ref, v_ref = ins
    g = p_ref[0].astype(F32)
    for b in range(1, N_CHIP):
        g = g + p_ref[b].astype(F32)
    delta, m2, v2 = _adamw_math(g, w_ref[...], m_ref[...], v_ref[...])
    for ref, val in zip(outs, (g, delta, m2, v2)):
        ref[...] = val


class _Rider(NamedTuple):
    inputs: tuple
    in_specs: list
    out_shape: list
    out_specs: list
    body: Callable


def _adamw_rider(parts, w, m, v):
    rows, cols = w.shape

    def rider(steps, step_of):
        rr = rows // steps
        blk = pl.BlockSpec((rr, cols), lambda *ids: (step_of(*ids[:2]), 0))
        chips = pl.BlockSpec((N_CHIP, rr, cols), lambda *ids: (0, step_of(*ids[:2]), 0))
        shape = jax.ShapeDtypeStruct((rows, cols), F32)
        return _Rider((parts, w, m, v), [chips, blk, blk, blk], [shape] * 4, [blk] * 4, _adamw_block)

    return rider


def _adamw(parts, w, m, v, *, name, after=(), tr=256):
    rows, cols = w.shape

    def body(*refs):
        _adamw_block(refs[:4], refs[4 + len(after):])

    spec = pl.BlockSpec((tr, cols), lambda i: (i, 0))
    shape = jax.ShapeDtypeStruct((rows, cols), F32)
    return pl.pallas_call(
        body, name=name, grid=(rows // tr,),
        in_specs=[pl.BlockSpec((N_CHIP, tr, cols), lambda i: (0, i, 0)), spec, spec, spec]
        + [pl.BlockSpec(memory_space=pl.ANY)] * len(after),
        out_specs=[spec] * 4, out_shape=[shape] * 4,
        compiler_params=_params(("parallel",)),
    )(parts, w, m, v, *after)


def _small_exchange(part, after=()):
    rows = part.shape[0]

    def body(p_ref, *rest):
        g_ref, buf, send, recv = rest[len(after):]
        me, peers = _me_and_peers()
        buf[me] = p_ref[...]
        sends = []
        for k, (dev, _) in enumerate(peers):
            cp = pltpu.make_async_remote_copy(p_ref, buf.at[me], send.at[k], recv.at[k],
                                              device_id=dev, device_id_type=MESH)
            cp.start()
            sends.append(cp)
        for k, (dev, idx) in enumerate(peers):
            pltpu.make_async_remote_copy(p_ref, buf.at[idx], send.at[k], recv.at[k],
                                         device_id=dev, device_id_type=MESH).wait_recv()
        for cp in sends:
            cp.wait_send()
        g = buf[0]
        for b in range(1, N_DEV):
            g = g + buf[b]
        g_ref[...] = g

    vm = pl.BlockSpec(memory_space=pltpu.VMEM)
    return pl.pallas_call(
        body, name="small_params_exchange",
        in_specs=[vm] + [pl.BlockSpec(memory_space=pl.ANY)] * len(after),
        out_specs=vm, out_shape=jax.ShapeDtypeStruct((rows, HD), F32),
        scratch_shapes=[pltpu.VMEM((N_DEV, rows, HD), F32),
                        pltpu.SemaphoreType.DMA((N_DEV - 1,)), pltpu.SemaphoreType.DMA((N_DEV - 1,))],
    )(part, *after)


def _small_adamw(g, w, m, v):
    def body(g_ref, w_ref, m_ref, v_ref, *outs):
        g = g_ref[...]
        delta, m2, v2 = _adamw_math(g, w_ref[...], m_ref[...], v_ref[...])
        for k, val in enumerate((g, delta, m2, v2)):
            norm_mix, b_gate, qa, ka, qb, kb, rpb, norm_ffn = outs[8 * k:8 * k + 8]
            for dst, row0, n_rows in ((norm_mix, 0, 16), (b_gate, 16, 32), (norm_ffn, 120, 16)):
                for r in range(n_rows):
                    dst[:, r * HD:(r + 1) * HD] = val[row0 + r:row0 + r + 1, :]
            for i, dst in enumerate((qa, ka, qb, kb)):
                dst[...] = val[48 + i:49 + i, :]
            rpb[...] = val[56:120, :]
        outs[32][...] = g[LOSS_ROW:LOSS_ROW + 1, 0:1]

    vm = pl.BlockSpec(memory_space=pltpu.VMEM)
    kinds = [jax.ShapeDtypeStruct(sh, F32) for sh in
             ((1, D), (1, 2 * D), (1, HD), (1, HD), (1, HD), (1, HD), (64, HD), (1, D))]
    outs = pl.pallas_call(
        body, name="small_params_adamw", in_specs=[vm] * 4, out_specs=[vm] * 33,
        out_shape=kinds * 4 + [jax.ShapeDtypeStruct((1, 1), F32)],
    )(g, w, m, v)
    return [outs[8 * k:8 * k + 8] for k in range(4)], outs[32]


def _pack_small(norm_mix, b_gate, qa, ka, qb, kb, rpb, norm_ffn):
    gains = jnp.concatenate([qa, ka, qb, kb, jnp.zeros((4, HD), F32)], axis=0)
    rpb_pad = jnp.pad(rpb.reshape(4 * (2 * WIN_R - 1), 2 * WIN_C - 1), ((0, 4), (0, HD - (2 * WIN_C - 1))))
    return jnp.concatenate([norm_mix.reshape(16, HD), b_gate.reshape(32, HD), gains, rpb_pad,
                            norm_ffn.reshape(16, HD), jnp.zeros((8, HD), F32)], axis=0)


LOSS_ROW = 136


def _rpb_from_rows(rows):
    return rows[:60, :2 * WIN_C - 1].reshape(1, 4, 2 * WIN_R - 1, 2 * WIN_C - 1)


def kernel(x, norm_mix, w_in, b_gate, q_norm_a, k_norm_a, q_norm_b, k_norm_b, rpb_b, w_proj_a, w_proj_b, w_out, norm_ffn, w_up, w_down, loss_target, m_norm_mix, m_w_in, m_b_gate, m_q_norm_a, m_k_norm_a, m_q_norm_b, m_k_norm_b, m_rpb_b, m_w_proj_a, m_w_proj_b, m_w_out, m_norm_ffn, m_w_up, m_w_down, v_norm_mix, v_w_in, v_b_gate, v_q_norm_a, v_k_norm_a, v_q_norm_b, v_k_norm_b, v_rpb_b, v_w_proj_a, v_w_proj_b, v_w_out, v_norm_ffn, v_w_up, v_w_down):
    big_w = (w_in[0], w_proj_a[0], w_proj_b[0], w_out[0], w_up[0], w_down[0])
    big_m = (m_w_in[0], m_w_proj_a[0], m_w_proj_b[0], m_w_out[0], m_w_up[0], m_w_down[0])
    big_v = (v_w_in[0], v_w_proj_a[0], v_w_proj_b[0], v_w_out[0], v_w_up[0], v_w_down[0])
    names = ("w_in", "w_proj_a", "w_proj_b", "w_out", "w_up", "w_down")

    shards = [_cast_bf16(w) for w in big_w[:5]]
    g_in, = _gather_on_sequencer(shards[0:1], "gather_w_in")
    g_pa, g_pb, g_out, g_up = _gather_on_sequencer(shards[1:5], "gather_w_mix_up")
    small_w = _pack_small(norm_mix, b_gate, q_norm_a, k_norm_a, q_norm_b, k_norm_b, rpb_b, norm_ffn)
    small_m = _pack_small(m_norm_mix, m_b_gate, m_q_norm_a, m_k_norm_a, m_q_norm_b, m_k_norm_b, m_rpb_b, m_norm_ffn)
    small_v = _pack_small(v_norm_mix, v_b_gate, v_q_norm_a, v_k_norm_a, v_q_norm_b, v_k_norm_b, v_rpb_b, v_norm_ffn)
    g_down = [_gather_on_sequencer([_cast_bf16(big_w[5], part=h, parts=2, after=(small_w, small_m, small_v) * h)],
                                   f"gather_w_down_{h}")[0].reshape(1, D_FF, D // 2) for h in range(2)]

    upd = [None] * 6
    in_flight = {}

    def weight_grads(tag, operands):
        sums = {i: _mm_tn_pair(a, b, name=f"grad_{names[i]}", **GRAD_TILES[i]) for i, (a, b) in operands.items()}
        new = list(sums.values())
        in_flight.update(zip(sums, _chip_exchange_on_sequencer(new, f"chip_exchange_{tag}")))
        return new

    def riders(name):
        i = {"proj_bwd": 5}.get(name)
        if i is None:
            return None
        return (_adamw_rider(in_flight.pop(i), big_w[i], big_m[i], big_v[i]),
                functools.partial(upd.__setitem__, i))

    loss, grad_x, small_g = _local_step(
        x[0], loss_target[0], norm_mix, b_gate, small_w[48:56], small_w[56:120], norm_ffn,
        g_in, g_pa, g_pb, g_out.reshape(D, D), g_up, g_down, weight_grads, riders)

    g_norm_mix, g_b, g_gains, g_rpb, g_norm_ffn = small_g
    small_part = jnp.concatenate([g_norm_mix.reshape(16, HD), g_b.reshape(32, HD),
                                  g_gains, g_rpb, g_norm_ffn.reshape(16, HD),
                                  jnp.pad(loss, ((0, 7), (0, HD - 1)))], axis=0)
    last = grad_x
    for i, r in in_flight.items():
        if i == 0:
            small_sum = _small_exchange(small_part, after=[last])
            small, total = _small_adamw(small_sum, small_w, small_m, small_v)
            last = total
        upd[i] = _adamw(r, big_w[i], big_m[i], big_v[i], name=f"adamw_{names[i]}", after=[last])
        last = upd[i][0]
    s_g, s_d, s_m, s_v = ((*k[:6], _rpb_from_rows(k[6]), k[7]) for k in small)
    b_g, b_d, b_m, b_v = ([u[j][None] for u in upd] for j in range(4))

    def order(small, big):
        nm, bg, qa, ka, qb, kb, rpb, nf = small
        w_in_, pa_, pb_, out_, up_, down_ = big
        return (nm, w_in_, bg, qa, ka, qb, kb, rpb, pa_, pb_, out_, nf, up_, down_)

    return (total[0, 0], grad_x[None], *order(s_g, b_g), *order(s_d, b_d), *order(s_m, b_m), *order(s_v, b_v))
```

```python
import functools
from typing import Callable, NamedTuple

import jax
import jax.numpy as jnp
import numpy as np
from jax import lax
from jax.experimental import pallas as pl
from jax.experimental.pallas import tpu as pltpu
from jax.experimental.pallas import tpu_sc as plsc

F32 = jnp.float32
BF16 = jnp.bfloat16

N_DEV = 8
S = 2048
D = 2048
HD = 128
NH = 16
NH_A = 12
QKV = NH * HD
D_IN = 3 * QKV + 2 * D
D_BR = 512
D_FF = 4 * D
GRID_W = 64
ROWS = S // GRID_W
WIN_R = 8
WIN_C = 16
EPS = 1e-6
NEG = -1e30
SCALE = HD ** -0.5
ROPE_THETA = 10000.0
DILATIONS = (1, 4, 16)
HALF_A = 64
QB = 128

LR, B1, B2, AEPS, WD, STEP = 0.001, 0.9, 0.999, 1e-08, 0.01, 10
BC1 = 1.0 - B1 ** STEP
BC2 = 1.0 - B2 ** STEP

VMEM_LIMIT = 56 * 1024 * 1024
MESH = pl.DeviceIdType.MESH

NN = (((1,), (0,)), ((), ()))
NT = (((1,), (1,)), ((), ()))
TN = (((0,), (0,)), ((), ()))


def _params(sem):
    return pltpu.CompilerParams(dimension_semantics=sem, vmem_limit_bytes=VMEM_LIMIT)


def _matmul(a, b, *, product, grid, a_spec, b_spec, epi, out_shape, out_specs, name,
            extra=(), extra_specs=(), after=(), carried=False, rider=None, into=()):
    n_extra = len(extra)
    single = not isinstance(out_shape, (list, tuple))
    out_shape = [out_shape] if single else list(out_shape)
    out_specs = [out_specs] if single else list(out_specs)
    ride = rider(grid[0] * grid[1], lambda j, i: j * grid[1] + i) if rider else None
    r_in = list(ride.inputs) if ride else []
    n_main = len(out_shape)

    def body(a_ref, b_ref, *rest):
        n_in = n_extra + len(after) + len(r_in)
        ins, outs = rest[:n_in], rest[n_in + len(into):]
        epi(product(a_ref, b_ref, ins[:n_extra]), ins[:n_extra], outs[:n_main])
        if ride:
            ride.body(ins[n_extra + len(after):], outs[n_main:])

    res = pl.pallas_call(
        body, name=name, grid=grid,
        in_specs=[a_spec, b_spec, *extra_specs, *[pl.BlockSpec(memory_space=pl.ANY)] * len(after),
                  *(ride.in_specs if ride else []), *[pl.BlockSpec(memory_space=pl.ANY)] * len(into)],
        out_specs=out_specs + (ride.out_specs if ride else []),
        out_shape=out_shape + (ride.out_shape if ride else []),
        input_output_aliases={2 + n_extra + len(after) + len(r_in) + k: k for k in range(len(into))},
        compiler_params=_params(("arbitrary", "arbitrary") if carried else ("parallel", "parallel")),
    )(a, b, *extra, *after, *r_in, *into)
    main = res[0] if single else res[:n_main]
    return (main, res[n_main:]) if ride else main


def _dot(x, y, dims):
    return lax.dot_general(x, y, dims, preferred_element_type=F32)


def _epi_store(acc, ex, outs):
    outs[0][...] = acc.astype(outs[0].dtype)


def _epi_residual(acc, ex, outs):
    outs[0][...] = acc + ex[0][...]


def _mm_nn(a, b3, *, tm, tn, name, out_dtypes=(F32,), epi=_epi_store, extra=(), total=False,
           col0=0, width=None, into=()):
    m, kdim = a.shape
    g, _, ng = b3.shape
    n = g * ng
    c0 = col0 // tn
    if tn <= ng:
        npg = ng // tn
        b_spec = pl.BlockSpec((None, kdim, tn), lambda j, i: (j // npg, 0, j % npg))

        def product(a_ref, b_ref, ex):
            return _dot(a_ref[...], b_ref[...], NN)
    else:
        gb = tn // ng
        b_spec = pl.BlockSpec((gb, kdim, ng), lambda j, i: (j, 0, 0))

        def product(a_ref, b_ref, ex):
            return jnp.concatenate([_dot(a_ref[...], b_ref[q], NN) for q in range(gb)], axis=1)

    tile = pl.BlockSpec((tm, tn), lambda j, i: (i, j + c0))
    shapes = [jax.ShapeDtypeStruct((m, width or n), dt) for dt in out_dtypes]
    specs = [tile] * len(shapes)
    if total:
        shapes.append(jax.ShapeDtypeStruct((1, 1), F32))
        specs.append(pl.BlockSpec((1, 1), lambda j, i: (0, 0)))
    single = len(shapes) == 1
    return _matmul(
        a, b3, product=product, grid=(n // tn, m // tm), epi=epi, name=name, carried=total, into=into,
        a_spec=pl.BlockSpec((tm, kdim), lambda j, i: (i, 0)), b_spec=b_spec,
        extra=extra, extra_specs=[tile] * len(extra),
        out_shape=shapes[0] if single else shapes, out_specs=specs[0] if single else specs)


def _mm_nt(a, b3, *, tm, tn, name, out_dtype=F32, epi=_epi_store, extra=(), after=(), rider=None, more_b=()):
    m, kdim = a.shape
    _, n, _ = b3.shape
    n_b = len(more_b)

    def product(a_ref, b_ref, ex):
        acc, k0 = None, 0
        for ref in (b_ref, *ex[:n_b]):
            for q in range(ref.shape[0]):
                part = _dot(a_ref[:, k0:k0 + ref.shape[2]], ref[q], NT)
                acc = part if acc is None else acc + part
                k0 += ref.shape[2]
        return acc

    def write(acc, ex, outs):
        epi(acc, ex[n_b:], outs)

    def w_spec(w):
        return pl.BlockSpec((w.shape[0], tn, w.shape[2]), lambda j, i: (0, j, 0))

    tile = pl.BlockSpec((tm, tn), lambda j, i: (i, j))
    return _matmul(
        a, b3, product=product, grid=(n // tn, m // tm), epi=write, name=name,
        a_spec=pl.BlockSpec((tm, kdim), lambda j, i: (i, 0)), b_spec=w_spec(b3),
        extra=(*more_b, *extra), extra_specs=[w_spec(w) for w in more_b] + [tile] * len(extra),
        after=after, rider=rider,
        out_shape=jax.ShapeDtypeStruct((m, n), out_dtype), out_specs=tile)


def _mm_tn(a, b, *, tm, tn, name, groups=1, out_dtype=BF16):
    t, m = a.shape
    _, n = b.shape
    ng = n // groups
    if tn <= ng:
        npg = ng // tn
        out_spec = pl.BlockSpec((None, tm, tn), lambda j, i: (j // npg, i, j % npg))
        epi = _epi_store

        def product(a_ref, b_ref, ex):
            return _dot(a_ref[...], b_ref[...], TN)
    else:
        gb = tn // ng
        out_spec = pl.BlockSpec((gb, tm, ng), lambda j, i: (j, i, 0))

        def product(a_ref, b_ref, ex):
            return [_dot(a_ref[...], b_ref[:, q * ng:(q + 1) * ng], TN) for q in range(gb)]

        def epi(parts, ex, outs):
            for q, part in enumerate(parts):
                outs[0][q] = part.astype(out_dtype)

    return _matmul(
        a, b, product=product, grid=(n // tn, m // tm), epi=epi, name=name,
        a_spec=pl.BlockSpec((t, tm), lambda j, i: (0, i)),
        b_spec=pl.BlockSpec((t, tn), lambda j, i: (0, j)),
        out_shape=jax.ShapeDtypeStruct((groups, m, ng), out_dtype), out_specs=out_spec)


def _rms_fwd(x, g, *, name, tr=256):
    def body(x_ref, g_ref, y_ref, r_ref):
        xv = x_ref[...]
        r = lax.rsqrt(jnp.mean(xv * xv, axis=-1, keepdims=True) + EPS)
        y_ref[...] = (xv * r * g_ref[...]).astype(BF16)
        r_ref[...] = r

    row = pl.BlockSpec((tr, D), lambda i: (i, 0))
    return pl.pallas_call(
        body, name=name, grid=(S // tr,),
        in_specs=[row, pl.BlockSpec((1, D), lambda i: (0, 0))],
        out_specs=[row, pl.BlockSpec((tr, 1), lambda i: (i, 0))],
        out_shape=[jax.ShapeDtypeStruct((S, D), BF16), jax.ShapeDtypeStruct((S, 1), F32)],
        compiler_params=_params(("parallel",)),
    )(x, g)


def _rms_bwd(dy, x, rstd, g, resid, *, name, bf16_copy, tr=256):
    def body(dy_ref, x_ref, r_ref, g_ref, res_ref, dx_ref, *rest):
        dg_ref = rest[-1]
        r = r_ref[...]
        xh = x_ref[...] * r
        dyv = dy_ref[...]
        t = dyv * g_ref[...]
        dx = r * (t - xh * jnp.mean(t * xh, axis=-1, keepdims=True)) + res_ref[...]
        dx_ref[...] = dx
        if bf16_copy:
            rest[0][...] = dx.astype(BF16)
        part = jnp.sum(dyv * xh, axis=0, keepdims=True)

        @pl.when(pl.program_id(0) == 0)
        def _():
            dg_ref[...] = part

        @pl.when(pl.program_id(0) > 0)
        def _():
            dg_ref[...] += part

    row = pl.BlockSpec((tr, D), lambda i: (i, 0))
    vec = pl.BlockSpec((1, D), lambda i: (0, 0))
    return pl.pallas_call(
        body, name=name, grid=(S // tr,),
        in_specs=[row, row, pl.BlockSpec((tr, 1), lambda i: (i, 0)), vec, row],
        out_specs=[row] + [row] * bf16_copy + [vec],
        out_shape=[jax.ShapeDtypeStruct((S, D), F32)] + [jax.ShapeDtypeStruct((S, D), BF16)] * bf16_copy
        + [jax.ShapeDtypeStruct((1, D), F32)],
        compiler_params=_params(("arbitrary",)),
    )(dy, x, rstd, g, resid)


def _rope_tables():
    pos = np.arange(S, dtype=np.float32)
    inv = (ROPE_THETA ** (-np.arange(0, HD, 2, dtype=np.float32) / HD)).astype(np.float32)
    ang = pos[:, None] * inv[None, :]
    cos, sin = np.cos(ang), np.sin(ang)
    return (jnp.asarray(np.concatenate([cos, cos], axis=-1), F32),
            jnp.asarray(np.concatenate([-sin, sin], axis=-1), F32))


def _swap_halves(t):
    return pltpu.roll(t, HD // 2, axis=1)


TOK = 256


def _lane_block_spec(d, last=HD):
    return pl.BlockSpec((4, TOK // d, d * last), lambda i: (0, i, 0))


def _to_lane_blocks(dst, head, val, d, scr, dtype):
    w = val.shape[1]
    if d == 1:
        dst[head] = val.astype(dtype)
        return
    scr[...] = val
    for r in range(d):
        dst[head, :, r * w:(r + 1) * w] = scr[pl.ds(r, TOK // d, stride=d), :].astype(dtype)


def _from_lane_blocks(src, head, d, w, scr):
    if d == 1:
        return src[head].astype(F32)
    for r in range(d):
        scr[pl.ds(r, TOK // d, stride=d), :] = src[head, :, r * w:(r + 1) * w].astype(F32)
    return scr[...]


def _qk_prep(proj, gains, cos2, sin2):
    def body(q_ref, k_ref, v_ref, g_ref, c_ref, s_ref, *rest):
        outs, scr = rest[:-1], rest[-1]
        cos, sin = c_ref[...], s_ref[...]
        for which, (src, row_a, row_b) in enumerate(((q_ref, 0, 2), (k_ref, 1, 3), (v_ref, None, None))):
            for h in range(NH):
                y = src[:, h * HD:(h + 1) * HD]
                if row_a is not None:
                    y = y * lax.rsqrt(jnp.mean(y * y, axis=-1, keepdims=True) + EPS)
                    if h < NH_A:
                        y = y * g_ref[row_a:row_a + 1, :]
                        y = y * cos + _swap_halves(y) * sin
                    else:
                        y = y * g_ref[row_b:row_b + 1, :]
                if h < NH_A:
                    gi = h // 4
                    _to_lane_blocks(outs[3 * gi + which], h % 4, y, DILATIONS[gi], scr, BF16)
                else:
                    hb = h - NH_A
                    outs[9 + which][:, hb * HD:(hb + 1) * HD] = y.astype(BF16)

    def blk(c):
        return pl.BlockSpec((TOK, QKV), lambda i: (i, c))
    tab = pl.BlockSpec((TOK, HD), lambda i: (i, 0))
    out_specs, out_shape = [], []
    for d in DILATIONS:
        out_specs += [_lane_block_spec(d)] * 3
        out_shape += [jax.ShapeDtypeStruct((4, S // d, d * HD), BF16)] * 3
    out_specs += [pl.BlockSpec((TOK, D_BR), lambda i: (i, 0))] * 3
    out_shape += [jax.ShapeDtypeStruct((S, D_BR), BF16)] * 3
    outs = pl.pallas_call(
        body, name="qk_prep", grid=(S // TOK,),
        in_specs=[blk(0), blk(1), blk(2), pl.BlockSpec((8, HD), lambda i: (0, 0)), tab, tab],
        out_specs=out_specs, out_shape=out_shape,
        scratch_shapes=[pltpu.VMEM((TOK, HD), F32)],
        compiler_params=_params(("parallel",)),
    )(proj, proj, proj, gains, cos2, sin2)
    return [tuple(outs[3 * gi:3 * gi + 3]) for gi in range(3)], tuple(outs[9:12])


def _qk_prep_bwd(dproj, proj, gains, cos2, sin2, grads_a, grads_b):
    def body(dp_in, q_ref, k_ref, g_ref, c_ref, s_ref, *rest):
        grads, (dp_out, dg_ref, scr) = rest[:12], rest[12:]
        del dp_in
        cos, sin = c_ref[...], s_ref[...]

        def grad_of(which, h):
            if h < NH_A:
                gi = h // 4
                return _from_lane_blocks(grads[3 * gi + which], h % 4, DILATIONS[gi], HD, scr)
            hb = h - NH_A
            return grads[9 + which][:, hb * HD:(hb + 1) * HD].astype(F32)

        dg_rows = []
        for which, (src, base, row_a, row_b) in enumerate(((q_ref, 0, 0, 2), (k_ref, QKV, 1, 3))):
            dg_a = jnp.zeros((1, HD), F32)
            dg_b = jnp.zeros((1, HD), F32)
            for h in range(NH):
                t = src[:, h * HD:(h + 1) * HD]
                dy = grad_of(which, h)
                r = lax.rsqrt(jnp.mean(t * t, axis=-1, keepdims=True) + EPS)
                xh = t * r
                if h < NH_A:
                    dy = dy * cos - _swap_halves(dy) * sin
                    gain = g_ref[row_a:row_a + 1, :]
                    dg_a = dg_a + jnp.sum(dy * xh, axis=0, keepdims=True)
                else:
                    gain = g_ref[row_b:row_b + 1, :]
                    dg_b = dg_b + jnp.sum(dy * xh, axis=0, keepdims=True)
                u = dy * gain
                dx = r * (u - xh * jnp.mean(u * xh, axis=-1, keepdims=True))
                dp_out[:, base + h * HD:base + (h + 1) * HD] = dx.astype(BF16)
            dg_rows += [(row_a, dg_a), (row_b, dg_b)]
        for h in range(NH):
            dp_out[:, 2 * QKV + h * HD:2 * QKV + (h + 1) * HD] = grad_of(2, h).astype(BF16)

        @pl.when(pl.program_id(0) == 0)
        def _():
            dg_ref[...] = jnp.zeros((8, HD), F32)

        for row, val in dg_rows:
            dg_ref[row:row + 1, :] += val

    def blk(c):
        return pl.BlockSpec((TOK, QKV), lambda i: (i, c))
    tab = pl.BlockSpec((TOK, HD), lambda i: (i, 0))
    gain_spec = pl.BlockSpec((8, HD), lambda i: (0, 0))
    grad_specs = [s for d in DILATIONS for s in [_lane_block_spec(d)] * 3]
    grad_specs += [pl.BlockSpec((TOK, D_BR), lambda i: (i, 0))] * 3
    return pl.pallas_call(
        body, name="qk_prep_bwd", grid=(S // TOK,),
        in_specs=[pl.BlockSpec(memory_space=pl.ANY), blk(0), blk(1), gain_spec, tab, tab] + grad_specs,
        out_specs=[pl.BlockSpec((TOK, 3 * QKV), lambda i: (i, 0)), gain_spec],
        out_shape=[jax.ShapeDtypeStruct((S, D_IN), BF16), jax.ShapeDtypeStruct((8, HD), F32)],
        input_output_aliases={0: 0},
        scratch_shapes=[pltpu.VMEM((TOK, HD), F32)],
        compiler_params=_params(("arbitrary",)),
    )(dproj, proj, proj, gains, cos2, sin2, *[g for grp in grads_a for g in grp], *grads_b)


def _mix_fwd(oa, ob, w_pa, w_pb, proj, b_gate, *, tr=256):
    def body(oa_ref, ob_ref, pa_ref, pb_ref, la_ref, lb_ref, ba_ref, bb_ref, mix_ref, ya_ref, yb_ref,
             ga_ref, gb_ref):
        ya = jnp.concatenate([_dot(oa_ref[...], pa_ref[q], NN) for q in range(N_DEV)], axis=1)
        yb = jnp.concatenate([_dot(ob_ref[...], pb_ref[q], NN) for q in range(N_DEV)], axis=1)
        ga = jax.nn.sigmoid(la_ref[...] + ba_ref[...])
        gb = jax.nn.sigmoid(lb_ref[...] + bb_ref[...])
        mix_ref[...] = (ga * ya + gb * yb).astype(BF16)
        ya_ref[...] = ya.astype(BF16)
        yb_ref[...] = yb.astype(BF16)
        ga_ref[...] = ga.astype(BF16)
        gb_ref[...] = gb.astype(BF16)

    row = pl.BlockSpec((tr, D), lambda i: (i, 0))
    branch = pl.BlockSpec((tr, D_BR), lambda i: (i, 0))
    whole = pl.BlockSpec((N_DEV, D_BR, D // N_DEV), lambda i: (0, 0, 0))
    return pl.pallas_call(
        body, name="mix_fwd", grid=(S // tr,),
        in_specs=[branch, branch, whole, whole,
                  pl.BlockSpec((tr, D), lambda i: (i, 3)), pl.BlockSpec((tr, D), lambda i: (i, 4)),
                  pl.BlockSpec((1, D), lambda i: (0, 0)), pl.BlockSpec((1, D), lambda i: (0, 1))],
        out_specs=[row] * 5, out_shape=[jax.ShapeDtypeStruct((S, D), BF16)] * 5,
        compiler_params=_params(("parallel",)),
    )(oa, ob, w_pa, w_pb, proj, proj, b_gate, b_gate)


def _mix_bwd(dh1b, w_out, ga, gb, ya, yb, *, tr=256):
    def body(dh_ref, w_ref, ga_ref, gb_ref, ya_ref, yb_ref, dya_ref, dyb_ref, dp_ref, db_ref):
        dm = _dot(dh_ref[...], w_ref[...], NT)
        parts = []
        for g_ref, y_ref, dy_ref, lo in ((ga_ref, ya_ref, dya_ref, 0), (gb_ref, yb_ref, dyb_ref, D)):
            g = g_ref[...].astype(F32)
            dy_ref[...] = (dm * g).astype(BF16)
            dl = dm * y_ref[...].astype(F32) * g * (1.0 - g)
            dp_ref[:, lo:lo + D] = dl.astype(BF16)
            parts.append(jnp.sum(dl, axis=0, keepdims=True))
        part = jnp.concatenate(parts, axis=1)

        @pl.when(pl.program_id(0) == 0)
        def _():
            db_ref[...] = part

        @pl.when(pl.program_id(0) > 0)
        def _():
            db_ref[...] += part

    row = pl.BlockSpec((tr, D), lambda i: (i, 0))
    vec = pl.BlockSpec((1, 2 * D), lambda i: (0, 0))
    gate_cols = pl.BlockSpec((pl.Element(tr), pl.Element(2 * D)), lambda i: (i * tr, 3 * QKV))
    return pl.pallas_call(
        body, name="mix_bwd", grid=(S // tr,),
        in_specs=[row, pl.BlockSpec((D, D), lambda i: (0, 0)), row, row, row, row],
        out_specs=[row, row, gate_cols, vec],
        out_shape=[jax.ShapeDtypeStruct((S, D), BF16), jax.ShapeDtypeStruct((S, D), BF16),
                   jax.ShapeDtypeStruct((S, D_IN), BF16), jax.ShapeDtypeStruct((1, 2 * D), F32)],
        compiler_params=_params(("arbitrary",)),
    )(dh1b, w_out, ga, gb, ya, yb)


def _band_blocks(m_len):
    wk = min(m_len, QB + 2 * QB)
    return [(qb * QB, min(max(qb * QB - QB, 0), m_len - wk), wk) for qb in range(m_len // QB)]


def _band_scores(q, kw, q0, k0, wk):
    s = _dot(q, kw, NT) * SCALE
    qpos = q0 + lax.broadcasted_iota(jnp.int32, (QB, 1), 0)
    kpos = k0 + lax.broadcasted_iota(jnp.int32, (1, wk), 1)
    return jnp.where(jnp.abs(kpos - qpos) <= HALF_A, s, NEG)


def _attn_a_fwd(q, k, v, gi):
    d = DILATIONS[gi]
    m_len = S // d

    def body(q_ref, k_ref, v_ref, o_ref, lse_ref):
        for r in range(d):
            lanes = slice(r * HD, (r + 1) * HD)
            for q0, k0, wk in _band_blocks(m_len):
                s = _band_scores(q_ref[q0:q0 + QB, lanes], k_ref[k0:k0 + wk, lanes], q0, k0, wk)
                m = jnp.max(s, axis=-1, keepdims=True)
                p = jnp.exp(s - m)
                l = jnp.sum(p, axis=-1, keepdims=True)
                o_ref[q0:q0 + QB, lanes] = _dot(p.astype(BF16), v_ref[k0:k0 + wk, lanes], NN) / l
                lse_ref[q0:q0 + QB, r:r + 1] = m + jnp.log(l)

    head = pl.BlockSpec((None, m_len, d * HD), lambda h: (h, 0, 0))
    stat = pl.BlockSpec((None, m_len, d), lambda h: (h, 0, 0))
    return pl.pallas_call(
        body, name=f"attn_a_fwd_{gi}", grid=(4,),
        in_specs=[head, head, head], out_specs=[head, stat],
        out_shape=[jax.ShapeDtypeStruct((4, m_len, d * HD), F32), jax.ShapeDtypeStruct((4, m_len, d), F32)],
        compiler_params=_params(("parallel",)),
    )(q, k, v)


def _combine_a(os, lses):
    def body(o0, o1, o2, l0, l1, l2, oa_ref, lse_ref, scr, scr1):
        for h in range(4):
            o = [_from_lane_blocks(ref, h, d, HD, scr) for ref, d in zip((o0, o1, o2), DILATIONS)]
            a, b, c = (_from_lane_blocks(ref, h, d, 1, scr1) for ref, d in zip((l0, l1, l2), DILATIONS))
            m = jnp.maximum(jnp.maximum(a, b), c)
            wa, wb, wc = jnp.exp(a - m), jnp.exp(b - m), jnp.exp(c - m)
            tot = wa + wb + wc
            oa_ref[:, h * HD:(h + 1) * HD] = ((wa * o[0] + wb * o[1] + wc * o[2]) / tot).astype(BF16)
            lse_ref[h] = m + jnp.log(tot)

    return pl.pallas_call(
        body, name="combine_a", grid=(S // TOK,),
        in_specs=[_lane_block_spec(d) for d in DILATIONS] + [_lane_block_spec(d, 1) for d in DILATIONS],
        out_specs=[pl.BlockSpec((TOK, D_BR), lambda i: (i, 0)), pl.BlockSpec((4, TOK, 1), lambda i: (0, i, 0))],
        out_shape=[jax.ShapeDtypeStruct((S, D_BR), BF16), jax.ShapeDtypeStruct((4, S, 1), F32)],
        scratch_shapes=[pltpu.VMEM((TOK, HD), F32), pltpu.VMEM((TOK, 1), F32)],
        compiler_params=_params(("parallel",)),
    )(*os, *lses)


def _proj_a_bwd(dya, w_pa, oa, lse):
    kg = D // N_DEV

    def body(dy_ref, w_ref, o_ref, l_ref, *rest):
        outs, (scr, scr1) = rest[:9], rest[9:]
        doa = _dot(dy_ref[:, 0:kg], w_ref[0], NT)
        for q in range(1, N_DEV):
            doa = doa + _dot(dy_ref[:, q * kg:(q + 1) * kg], w_ref[q], NT)
        for h in range(4):
            do = doa[:, h * HD:(h + 1) * HD]
            dsum = jnp.sum(do * o_ref[:, h * HD:(h + 1) * HD].astype(F32), axis=-1, keepdims=True)
            for gi, d in enumerate(DILATIONS):
                _to_lane_blocks(outs[3 * gi], h, do, d, scr, BF16)
                _to_lane_blocks(outs[3 * gi + 1], h, l_ref[h], d, scr1, F32)
                _to_lane_blocks(outs[3 * gi + 2], h, dsum, d, scr1, F32)

    row = pl.BlockSpec((TOK, D_BR), lambda i: (i, 0))
    out_specs, out_shape = [], []
    for d in DILATIONS:
        out_specs += [_lane_block_spec(d), _lane_block_spec(d, 1), _lane_block_spec(d, 1)]
        out_shape += [jax.ShapeDtypeStruct((4, S // d, d * HD), BF16)] + [jax.ShapeDtypeStruct((4, S // d, d), F32)] * 2
    outs = pl.pallas_call(
        body, name="proj_a_bwd", grid=(S // TOK,),
        in_specs=[pl.BlockSpec((TOK, D), lambda i: (i, 0)),
                  pl.BlockSpec((N_DEV, D_BR, kg), lambda i: (0, 0, 0)),
                  row, pl.BlockSpec((4, TOK, 1), lambda i: (0, i, 0))],
        out_specs=out_specs, out_shape=out_shape,
        scratch_shapes=[pltpu.VMEM((TOK, HD), F32), pltpu.VMEM((TOK, 1), F32)],
        compiler_params=_params(("parallel",)),
    )(dya, w_pa, oa, lse)
    return [tuple(outs[3 * gi:3 * gi + 3]) for gi in range(3)]


def _attn_a_bwd(q, k, v, do, lse, dsum, gi):
    d = DILATIONS[gi]
    m_len = S // d

    def body(q_ref, k_ref, v_ref, do_ref, lse_ref, dsum_ref, dq_ref, dk_out, dv_out, dk_ref, dv_ref):
        dk_ref[...] = jnp.zeros((m_len, d * HD), F32)
        dv_ref[...] = jnp.zeros((m_len, d * HD), F32)
        for r in range(d):
            lanes = slice(r * HD, (r + 1) * HD)
            for q0, k0, wk in _band_blocks(m_len):
                rows, keys = slice(q0, q0 + QB), slice(k0, k0 + wk)
                qv, kw, vw, dov = q_ref[rows, lanes], k_ref[keys, lanes], v_ref[keys, lanes], do_ref[rows, lanes]
                p = jnp.exp(_band_scores(qv, kw, q0, k0, wk) - lse_ref[rows, r:r + 1])
                ds = (p * (_dot(dov, vw, NT) - dsum_ref[rows, r:r + 1]) * SCALE).astype(BF16)
                dq_ref[rows, lanes] = _dot(ds, kw, NN).astype(BF16)
                dk_ref[keys, lanes] += _dot(ds, qv, TN)
                dv_ref[keys, lanes] += _dot(p.astype(BF16), dov, TN)
        dk_out[...] = dk_ref[...].astype(BF16)
        dv_out[...] = dv_ref[...].astype(BF16)

    head = pl.BlockSpec((None, m_len, d * HD), lambda h: (h, 0, 0))
    stat = pl.BlockSpec((None, m_len, d), lambda h: (h, 0, 0))
    shape = jax.ShapeDtypeStruct((4, m_len, d * HD), BF16)
    return pl.pallas_call(
        body, name=f"attn_a_bwd_{gi}", grid=(4,),
        in_specs=[head, head, head, head, stat, stat], out_specs=[head, head, head],
        out_shape=[shape, shape, shape],
        scratch_shapes=[pltpu.VMEM((m_len, d * HD), F32)] * 2,
        compiler_params=_params(("arbitrary",)),
    )(q, k, v, do, lse, dsum)


KEYS_B = WIN_R * GRID_W
N_OFF = WIN_R


def _bias_constants():
    q = np.arange(GRID_W)[:, None]
    kc = np.arange(GRID_W)[None, :]
    dc = np.clip(kc - q, -(WIN_C - 1), WIN_C - 1) + (WIN_C - 1)
    expand = np.zeros((HD, GRID_W * GRID_W), np.float32)
    expand[dc.reshape(-1), np.arange(GRID_W * GRID_W)] = 1.0
    cs = np.clip(q - WIN_C // 2, 0, GRID_W - WIN_C)
    keep = ((kc >= cs) & (kc < cs + WIN_C)).reshape(1, -1).astype(np.float32)
    sel = np.zeros((64, 4 * N_OFF * WIN_R), np.float32)
    for h in range(4):
        for off in range(N_OFF):
            for j in range(WIN_R):
                sel[h * (2 * WIN_R - 1) + off + j, (h * N_OFF + off) * WIN_R + j] = 1.0
    return jnp.asarray(expand), jnp.asarray(keep), jnp.asarray(sel)


def _bias_expand(rpb_pad, expand, keep, sel):
    def body(r_ref, e_ref, k_ref, s_ref, o_ref):
        t = lax.dot_general(r_ref[...], e_ref[...], NN, precision=lax.Precision.HIGHEST,
                            preferred_element_type=F32)
        rows = lax.dot_general(s_ref[...], t, TN, precision=lax.Precision.HIGHEST,
                               preferred_element_type=F32)
        o_ref[...] = jnp.where(k_ref[...] > 0.5, rows, NEG)

    return pl.pallas_call(
        body, name="bias_expand",
        out_shape=jax.ShapeDtypeStruct((4 * N_OFF * WIN_R, GRID_W * GRID_W), F32),
        compiler_params=pltpu.CompilerParams(vmem_limit_bytes=VMEM_LIMIT),
    )(rpb_pad, expand, keep, sel)


def _bias_reduce(dbias_tab):
    lane0 = GRID_W - WIN_C
    flip = np.zeros((GRID_W, GRID_W), np.float32)
    flip[np.arange(GRID_W), GRID_W - 1 - np.arange(GRID_W)] = 1.0
    place = np.zeros((WIN_R, 64, 4 * N_OFF), np.float32)
    for j in range(WIN_R):
        for h in range(4):
            for off in range(N_OFF):
                place[j, h * (2 * WIN_R - 1) + off + j, h * N_OFF + off] = 1.0

    def exact(x, y):
        return lax.dot_general(x, y, NN, precision=lax.Precision.HIGHEST, preferred_element_type=F32)

    def body(x_ref, flip_ref, place_ref, o_ref, z_ref):
        for h in range(4):
            for off in range(N_OFF):
                lined_up = pltpu.roll(exact(flip_ref[...], x_ref[h, off]), 0, axis=1, stride=1, stride_axis=0)
                z_ref[h * N_OFF + off:h * N_OFF + off + 1, :] = jnp.sum(lined_up, axis=0, keepdims=True)
        acc = jnp.zeros((64, HD), F32)
        for j in range(WIN_R):
            at_zero = pltpu.roll(z_ref[...], (KEYS_B - (j * GRID_W + lane0)) % KEYS_B, axis=1)[:, :HD]
            acc = acc + exact(place_ref[j], at_zero)
        lane = lax.broadcasted_iota(jnp.int32, (64, HD), 1)
        o_ref[...] = jnp.where(lane < 2 * WIN_C - 1, acc, 0.0)

    return pl.pallas_call(
        body, name="bias_reduce", out_shape=jax.ShapeDtypeStruct((64, HD), F32),
        scratch_shapes=[pltpu.VMEM((4 * N_OFF, KEYS_B), F32)],
        compiler_params=pltpu.CompilerParams(vmem_limit_bytes=VMEM_LIMIT),
    )(dbias_tab, jnp.asarray(flip), jnp.asarray(place))


def _rows_to_tab(rows):
    t = rows.reshape(4, N_OFF, WIN_R, GRID_W, GRID_W)
    return t.transpose(0, 1, 3, 2, 4).reshape(4, N_OFF, GRID_W, KEYS_B)


def _row_window(r):
    r0 = jnp.clip(r - WIN_R // 2, 0, ROWS - WIN_R)
    off = r0 + (WIN_R - 1) - r
    return pl.multiple_of(r * GRID_W, GRID_W), pl.multiple_of(r0 * GRID_W, GRID_W), off


def _attn_b_fwd(qn, kn, vb, bias_tab):
    def body(q_ref, k_ref, v_ref, b_ref, o_ref, lse_ref):
        def row(r, carry):
            qs, ks, off = _row_window(r)
            q = q_ref[pl.ds(qs, GRID_W), :]
            s = lax.dot_general(q, k_ref[pl.ds(ks, KEYS_B), :], NT, preferred_element_type=F32) * SCALE
            s = s + b_ref[off]
            m = jnp.max(s, axis=-1, keepdims=True)
            p = jnp.exp(s - m)
            l = jnp.sum(p, axis=-1, keepdims=True)
            o = lax.dot_general(p.astype(BF16), v_ref[pl.ds(ks, KEYS_B), :], NN, preferred_element_type=F32)
            o_ref[pl.ds(qs, GRID_W), :] = (o / l).astype(BF16)
            lse_ref[pl.ds(qs, GRID_W), :] = m + jnp.log(l)
            return carry

        lax.fori_loop(0, ROWS, row, 0, unroll=8)

    full = pl.BlockSpec((S, HD), lambda h: (0, h))
    return pl.pallas_call(
        body, name="attn_b_fwd", grid=(4,),
        in_specs=[full, full, full, pl.BlockSpec((None, N_OFF, GRID_W, KEYS_B), lambda h: (h, 0, 0, 0))],
        out_specs=[pl.BlockSpec((S, HD), lambda h: (0, h)), pl.BlockSpec((None, S, 1), lambda h: (h, 0, 0))],
        out_shape=[jax.ShapeDtypeStruct((S, D_BR), BF16), jax.ShapeDtypeStruct((4, S, 1), F32)],
        compiler_params=_params(("parallel",)),
    )(qn, kn, vb, bias_tab)


def _attn_b_bwd(qn, kn, vb, bias_tab, ob, dob, lse):
    def body(q_ref, k_ref, v_ref, b_ref, o_ref, do_ref, lse_ref, dq_ref, dk_out, dv_out, db_ref, dk_ref, dv_ref):
        dk_ref[...] = jnp.zeros((S, HD), F32)
        dv_ref[...] = jnp.zeros((S, HD), F32)
        db_ref[...] = jnp.zeros((N_OFF, GRID_W, KEYS_B), F32)

        def row(r, carry):
            qs, ks, off = _row_window(r)
            rows = pl.ds(qs, GRID_W)
            keys = pl.ds(ks, KEYS_B)
            q = q_ref[rows, :]
            kw = k_ref[keys, :]
            s = lax.dot_general(q, kw, NT, preferred_element_type=F32) * SCALE + b_ref[off]
            p = jnp.exp(s - lse_ref[rows, :])
            do = do_ref[rows, :]
            dobf = do.astype(BF16)
            dsum = jnp.sum(do * o_ref[rows, :].astype(F32), axis=-1, keepdims=True)
            dp = lax.dot_general(dobf, v_ref[keys, :], NT, preferred_element_type=F32)
            ds = p * (dp - dsum)
            db_ref[off] += ds
            dsb = (ds * SCALE).astype(BF16)
            dq_ref[rows, :] = lax.dot_general(dsb, kw, NN, preferred_element_type=F32).astype(BF16)
            dk_ref[keys, :] += lax.dot_general(dsb, q, TN, preferred_element_type=F32)
            dv_ref[keys, :] += lax.dot_general(p.astype(BF16), dobf, TN, preferred_element_type=F32)
            return carry

        lax.fori_loop(0, ROWS, row, 0, unroll=8)
        dk_out[...] = dk_ref[...].astype(BF16)
        dv_out[...] = dv_ref[...].astype(BF16)

    full = pl.BlockSpec((S, HD), lambda h: (0, h))
    slot = pl.BlockSpec((S, HD), lambda h: (0, h))
    tab = pl.BlockSpec((None, N_OFF, GRID_W, KEYS_B), lambda h: (h, 0, 0, 0))
    shape = jax.ShapeDtypeStruct((S, D_BR), BF16)
    return pl.pallas_call(
        body, name="attn_b_bwd", grid=(4,),
        in_specs=[full, full, full, tab, slot, slot, pl.BlockSpec((None, S, 1), lambda h: (h, 0, 0))],
        out_specs=[slot, slot, slot, tab],
        out_shape=[shape, shape, shape, jax.ShapeDtypeStruct((4, N_OFF, GRID_W, KEYS_B), F32)],
        scratch_shapes=[pltpu.VMEM((S, HD), F32)] * 2,
        compiler_params=_params(("arbitrary",)),
    )(qn, kn, vb, bias_tab, ob, dob, lse)


def _epi_relu_sq(acc, ex, outs):
    u = jnp.maximum(acc, 0.0)
    outs[0][...] = u.astype(BF16)
    outs[1][...] = (u * u).astype(BF16)


def _epi_relu_sq_bwd(acc, ex, outs):
    outs[0][...] = (acc * (2.0 * ex[0][...].astype(F32))).astype(BF16)


def _epi_loss_head(acc, ex, outs):
    e = acc + ex[0][...] - ex[1][...]
    dy = e * (1.0 / D)
    outs[0][...] = dy
    outs[1][...] = dy.astype(BF16)
    part = (0.5 / D) * jnp.sum(jnp.sum(e * e, axis=-1, keepdims=True), axis=0, keepdims=True)
    first = (pl.program_id(0) == 0) & (pl.program_id(1) == 0)

    @pl.when(first)
    def _():
        outs[2][...] = part

    @pl.when(jnp.logical_not(first))
    def _():
        outs[2][...] += part


def _local_step(x, target, norm_mix, b_gate, gains, rpb_pad, norm_ffn,
                w_in, w_pa, w_pb, w_out, w_up, w_down, weight_grads, riders=lambda name: None):
    def ridden(name, *args, **kwargs):
        ride = riders(name)
        if ride is None:
            return _mm_nt(*args, name=name, **kwargs)
        out, rode = _mm_nt(*args, name=name, rider=ride[0], **kwargs)
        ride[1](rode)
        return out

    cos2, sin2 = _rope_tables()
    expand, keep, sel = _bias_constants()
    w_out3 = w_out[None]

    xn, rstd1 = _rms_fwd(x, norm_mix, name="rms_mix")
    proj = _mm_nn(xn, w_in, tm=1024, tn=1280, name="proj")
    qkv_a, qkv_b = _qk_prep(proj, gains, cos2, sin2)
    fwd_a = [_attn_a_fwd(*qkv_a[gi], gi) for gi in range(3)]
    oa, lse_a = _combine_a([o for o, _ in fwd_a], [l for _, l in fwd_a])
    bias_tab = _rows_to_tab(_bias_expand(rpb_pad, expand, keep, sel))
    ob, lse_b = _attn_b_fwd(*qkv_b, bias_tab)
    mixed, ya, yb, ga, gb = _mix_fwd(oa, ob, w_pa, w_pb, proj, b_gate)
    h1 = _mm_nn(mixed, w_out3, tm=1024, tn=1024, name="out_proj", epi=_epi_residual, extra=(x,))
    hn, rstd2 = _rms_fwd(h1, norm_ffn, name="rms_ffn")
    u, usq = _mm_nn(hn, w_up, tm=1024, tn=1024, name="ffn_up", epi=_epi_relu_sq,
                    out_dtypes=(BF16, BF16))
    dy, dyb, loss = _mm_nn(usq, w_down[0], tm=512, tn=512, name="ffn_down_0", epi=_epi_loss_head,
                           extra=(h1, target), out_dtypes=(F32, BF16), total=True, width=D)
    dy, dyb, loss_1 = _mm_nn(usq, w_down[1], tm=512, tn=512, name="ffn_down_1", epi=_epi_loss_head,
                             extra=(h1, target), out_dtypes=(F32, BF16), total=True, width=D,
                             col0=D // 2, into=(dy, dyb))
    loss = loss + loss_1

    sent = weight_grads("w_down", {5: (usq, dyb)})
    du = _mm_nt(dyb, w_down[0], more_b=(w_down[1],), tm=1024, tn=1024, name="ffn_down_bwd", out_dtype=BF16,
                epi=_epi_relu_sq_bwd, extra=(u,), after=sent)
    sent = weight_grads("w_up", {4: (hn, du)})
    dhn = ridden("ffn_up_bwd", du, w_up, tm=512, tn=512, after=sent)
    dh1, dh1b, g_norm_ffn = _rms_bwd(dhn, h1, rstd2, norm_ffn, dy, name="rms_ffn_bwd", bf16_copy=True)

    dya, dyb2, dproj, g_b = _mix_bwd(dh1b, w_out, ga, gb, ya, yb)
    sent = weight_grads("w_mix", {3: (mixed, dh1b), 1: (oa, dya), 2: (ob, dyb2)})
    dob = _mm_nt(dyb2, w_pb, tm=1024, tn=D_BR, name="proj_b_bwd", after=sent)
    prep = _proj_a_bwd(dya, w_pa, oa, lse_a)
    grads_a = [_attn_a_bwd(*qkv_a[gi], *prep[gi], gi) for gi in range(3)]
    dqb, dkb, dvb, dbias = _attn_b_bwd(*qkv_b, bias_tab, ob, dob, lse_b)
    g_rpb = _bias_reduce(dbias)
    dproj, g_gains = _qk_prep_bwd(dproj, proj, gains, cos2, sin2, grads_a, (dqb, dkb, dvb))
    sent = weight_grads("w_in", {0: (xn, dproj)})
    dxn = ridden("proj_bwd", dproj, w_in, tm=256, tn=512, after=sent)
    grad_x, g_norm_mix = _rms_bwd(dxn, x, rstd1, norm_mix, dh1, name="rms_mix_bwd", bf16_copy=False)

    small = (g_norm_mix, g_b, g_gains, g_rpb, g_norm_ffn)
    return loss, grad_x, small


def _cast_bf16(w, *, part=0, parts=1, after=(), tr=256):
    rows, cols = w.shape[0], w.shape[1] // parts
    tr = min(tr, rows)

    def body(w_ref, *rest):
        rest[-1][...] = w_ref[...].astype(BF16)

    return pl.pallas_call(
        body, name=f"cast_{rows}x{cols}_{part}", grid=(rows // tr,),
        in_specs=[pl.BlockSpec((tr, cols), lambda i: (i, part))] + [pl.BlockSpec(memory_space=pl.ANY)] * len(after),
        out_specs=pl.BlockSpec((tr, cols), lambda i: (i, 0)),
        out_shape=jax.ShapeDtypeStruct((rows, cols), BF16), compiler_params=_params(("parallel",)),
    )(w, *after)


def _me_and_peers():
    x, y, c = lax.axis_index("x"), lax.axis_index("y"), lax.axis_index("c")
    me = 4 * x + 2 * y + c
    peers = []
    for k in range(1, N_DEV):
        px = 1 - x if k & 4 else x
        py = 1 - y if k & 2 else y
        pc = 1 - c if k & 1 else c
        peers.append(((px, py, pc), 4 * px + 2 * py + pc))
    return me, peers


def _gather_on_sequencer(shards, name):
    n = len(shards)
    hbm = pltpu.MemorySpace.HBM
    ins = [jax.new_ref(s, memory_space=hbm) for s in shards]
    outs = [jax.empty_ref(jax.ShapeDtypeStruct((N_DEV,) + s.shape, s.dtype), memory_space=hbm) for s in shards]

    @_sequencer(name, ((n, N_DEV - 1), (n, N_DEV - 1), (n,)), 0)
    def launch(send, recv, lsem):
        x, y, c = lax.axis_index("x"), lax.axis_index("y"), lax.axis_index("c")
        me, sibling = (x, y, c), (x, y, 1 - c)
        chips = [(1 - x, y), (x, 1 - y), (1 - x, 1 - y)]
        _handshake([sibling] + [(*chip, c) for chip in chips])

        def copy(w, k, block, to, src=None):
            px, py, pc = block
            dst = outs[w].at[4 * px + 2 * py + pc]
            return pltpu.make_async_remote_copy(dst if src is None else src, dst, send.at[w, k], recv.at[w, k],
                                                device_id=to, device_id_type=MESH)

        local = [pltpu.make_async_copy(ins[w], outs[w].at[4 * x + 2 * y + c], lsem.at[w]) for w in range(n)]
        for cp in local:
            cp.start()
        first = []
        for w in range(n):
            first += [copy(w, 1 + j, me, (*chip, c), src=ins[w]) for j, chip in enumerate(chips)]
            first.append(copy(w, 0, me, sibling, src=ins[w]))
        for cp in first:
            cp.start()
        passed = []
        for w in range(n):
            for j, chip in enumerate(chips):
                copy(w, 1 + j, (*chip, c), me).wait_recv()
                cp = copy(w, 4 + j, (*chip, c), sibling)
                cp.start()
                passed.append(cp)
        for w in range(n):
            copy(w, 0, sibling, me).wait_recv()
            for j, chip in enumerate(chips):
                copy(w, 4 + j, (*chip, 1 - c), me).wait_recv()
        for cp in first + passed:
            cp.wait_send()
        for cp in local:
            cp.wait()

    launch()
    return [o[...] for o in outs]


N_CHIP = 4


def _sequencer(name, n_sems, collective_id):
    return functools.partial(
        pl.kernel, mesh=plsc.ScalarSubcoreMesh(axis_name="seq", num_cores=1), name=name,
        scratch_types=tuple(pltpu.SemaphoreType.DMA(s) for s in n_sems),
        compiler_params=pltpu.CompilerParams(collective_id=collective_id))


def _handshake(peers):
    barrier = pltpu.get_barrier_semaphore()
    for peer in peers:
        pl.semaphore_signal(barrier, inc=1, device_id=peer, device_id_type=MESH)
    pl.semaphore_wait(barrier, len(peers))


def _chip_exchange_on_sequencer(parts, name):
    n = len(parts)
    hbm = pltpu.MemorySpace.HBM
    ins = [jax.new_ref(p, memory_space=hbm) for p in parts]
    outs = [jax.empty_ref(jax.ShapeDtypeStruct(p.shape, p.dtype), memory_space=hbm) for p in parts]

    @_sequencer(name, ((n, 3), (n, 3), (n,)), 2)
    def launch(send, recv, lsem):
        x, y, c = lax.axis_index("x"), lax.axis_index("y"), lax.axis_index("c")
        mine = 2 * x + y
        chips = [(1 - x, y), (x, 1 - y), (1 - x, 1 - y)]
        _handshake([(*chip, c) for chip in chips])
        local = [pltpu.make_async_copy(ins[w].at[mine], outs[w].at[mine], lsem.at[w]) for w in range(n)]
        for cp in local:
            cp.start()
        sends = []
        for w in range(n):
            for j, (px, py) in enumerate(chips):
                cp = pltpu.make_async_remote_copy(ins[w].at[2 * px + py], outs[w].at[mine],
                                                  send.at[w, j], recv.at[w, j],
                                                  device_id=(px, py, c), device_id_type=MESH)
                cp.start()
                sends.append(cp)
        for w in range(n):
            for j, (px, py) in enumerate(chips):
                pltpu.make_async_remote_copy(ins[w].at[mine], outs[w].at[2 * px + py],
                                             send.at[w, j], recv.at[w, j],
                                             device_id=(px, py, c), device_id_type=MESH).wait_recv()
        for cp in sends:
            cp.wait_send()
        for cp in local:
            cp.wait()

    launch()
    return [o[...] for o in outs]


GRAD_TILES = (dict(blocks_on="cols", tm=512, tn=1280), dict(blocks_on="cols", tm=512, tn=256),
              dict(blocks_on="cols", tm=512, tn=256), dict(blocks_on="rows", tm=256, tn=2048),
              dict(blocks_on="cols", tm=1024, tn=1024), dict(blocks_on="rows", tm=1024, tn=1024))


def _mm_tn_pair(a, b, *, blocks_on, tm, tn, name):
    t_len, m = a.shape
    n = b.shape[1]
    if blocks_on == "rows":
        rows, cols, inner = m // N_DEV, n, n // tn
        assert tm == rows
        a_spec = pl.BlockSpec((t_len, tm), lambda p, t, blk: (0, blk[p]))
        b_spec = pl.BlockSpec((t_len, tn), lambda p, t, blk: (0, t))
        out_spec = pl.BlockSpec((None, tm, tn), lambda p, t, blk: (
            jnp.maximum(p - N_CHIP, 0), 0, jnp.where(p < N_CHIP, 0, t)))
    else:
        rows, cols, inner = m, n // N_DEV, m // tm
        assert tn == cols
        a_spec = pl.BlockSpec((t_len, tm), lambda p, t, blk: (0, t))
        b_spec = pl.BlockSpec((t_len, tn), lambda p, t, blk: (0, blk[p]))
        out_spec = pl.BlockSpec((None, tm, tn), lambda p, t, blk: (
            jnp.maximum(p - N_CHIP, 0), jnp.where(p < N_CHIP, 0, t), 0))

    def body(blk_ref, a_ref, b_ref, o_ref, land, stage, send_sem, recv_sem):
        del blk_ref
        p, t = pl.program_id(0), pl.program_id(1)
        step = p * inner + t
        x, y, c = lax.axis_index("x"), lax.axis_index("y"), lax.axis_index("c")
        tile = _dot(a_ref[...], b_ref[...], TN)

        def to_sibling(slot, chip, piece):
            return pltpu.make_async_remote_copy(stage.at[slot], land.at[chip, piece], send_sem.at[slot],
                                                recv_sem.at[chip, piece],
                                                device_id=(x, y, 1 - c), device_id_type=MESH)

        @pl.when(p < N_CHIP)
        def _():
            slot = step % 2

            @pl.when(step >= 2)
            def _():
                to_sibling(slot, 0, 0).wait_send()

            stage[slot] = tile.astype(BF16)
            to_sibling(slot, p, t).start()

        @pl.when(step == N_CHIP * inner)
        def _():
            for slot in range(min(2, N_CHIP * inner)):
                to_sibling(slot, 0, 0).wait_send()

        @pl.when(p >= N_CHIP)
        def _():
            chip = p - N_CHIP
            to_sibling(0, chip, t).wait_recv()
            o_ref[...] = (tile + land[chip, t].astype(F32)).astype(BF16)

    c = lax.axis_index("c")
    order = jnp.stack([2 * ch + 1 - c for ch in range(N_CHIP)] + [2 * ch + c for ch in range(N_CHIP)])
    return pl.pallas_call(
        body, name=name,
        grid_spec=pltpu.PrefetchScalarGridSpec(
            num_scalar_prefetch=1, grid=(N_DEV, inner), in_specs=[a_spec, b_spec], out_specs=out_spec,
            scratch_shapes=[pltpu.VMEM((N_CHIP, inner, tm, tn), BF16), pltpu.VMEM((2, tm, tn), BF16),
                            pltpu.SemaphoreType.DMA((2,)), pltpu.SemaphoreType.DMA((N_CHIP, inner))]),
        out_shape=jax.ShapeDtypeStruct((N_CHIP, rows, cols), BF16),
        compiler_params=_params(("arbitrary", "arbitrary")),
    )(order.astype(jnp.int32), a, b)


def _adamw_math(g, w, m, v):
    m2 = B1 * m + (1.0 - B1) * g
    v2 = B2 * v + (1.0 - B2) * (g * g)
    delta = -LR * ((m2 / BC1) / (jnp.sqrt(v2 / BC2) + AEPS) + WD * w)
    return delta, m2, v2


def _adamw_block(ins, outs):
    p_ref, w_ref, m_ref, v_ref = ins
    g = p_ref[0].astype(F32)
    for b in range(1, N_CHIP):
        g = g + p_ref[b].astype(F32)
    delta, m2, v2 = _adamw_math(g, w_ref[...], m_ref[...], v_ref[...])
    for ref, val in zip(outs, (g, delta, m2, v2)):
        ref[...] = val


class _Rider(NamedTuple):
    inputs: tuple
    in_specs: list
    out_shape: list
    out_specs: list
    body: Callable


def _adamw_rider(parts, w, m, v):
    rows, cols = w.shape

    def rider(steps, step_of):
        rr = rows // steps
        blk = pl.BlockSpec((rr, cols), lambda *ids: (step_of(*ids[:2]), 0))
        chips = pl.BlockSpec((N_CHIP, rr, cols), lambda *ids: (0, step_of(*ids[:2]), 0))
        shape = jax.ShapeDtypeStruct((rows, cols), F32)
        return _Rider((parts, w, m, v), [chips, blk, blk, blk], [shape] * 4, [blk] * 4, _adamw_block)

    return rider


def _adamw(parts, w, m, v, *, name, after=(), tr=256):
    rows, cols = w.shape

    def body(*refs):
        _adamw_block(refs[:4], refs[4 + len(after):])

    spec = pl.BlockSpec((tr, cols), lambda i: (i, 0))
    shape = jax.ShapeDtypeStruct((rows, cols), F32)
    return pl.pallas_call(
        body, name=name, grid=(rows // tr,),
        in_specs=[pl.BlockSpec((N_CHIP, tr, cols), lambda i: (0, i, 0)), spec, spec, spec]
        + [pl.BlockSpec(memory_space=pl.ANY)] * len(after),
        out_specs=[spec] * 4, out_shape=[shape] * 4,
        compiler_params=_params(("parallel",)),
    )(parts, w, m, v, *after)


def _small_exchange(part, after=()):
    rows = part.shape[0]

    def body(p_ref, *rest):
        g_ref, buf, send, recv = rest[len(after):]
        me, peers = _me_and_peers()
        buf[me] = p_ref[...]
        sends = []
        for k, (dev, _) in enumerate(peers):
            cp = pltpu.make_async_remote_copy(p_ref, buf.at[me], send.at[k], recv.at[k],
                                              device_id=dev, device_id_type=MESH)
            cp.start()
            sends.append(cp)
        for k, (dev, idx) in enumerate(peers):
            pltpu.make_async_remote_copy(p_ref, buf.at[idx], send.at[k], recv.at[k],
                                         device_id=dev, device_id_type=MESH).wait_recv()
        for cp in sends:
            cp.wait_send()
        g = buf[0]
        for b in range(1, N_DEV):
            g = g + buf[b]
        g_ref[...] = g

    vm = pl.BlockSpec(memory_space=pltpu.VMEM)
    return pl.pallas_call(
        body, name="small_params_exchange",
        in_specs=[vm] + [pl.BlockSpec(memory_space=pl.ANY)] * len(after),
        out_specs=vm, out_shape=jax.ShapeDtypeStruct((rows, HD), F32),
        scratch_shapes=[pltpu.VMEM((N_DEV, rows, HD), F32),
                        pltpu.SemaphoreType.DMA((N_DEV - 1,)), pltpu.SemaphoreType.DMA((N_DEV - 1,))],
    )(part, *after)


def _small_adamw(g, w, m, v):
    def body(g_ref, w_ref, m_ref, v_ref, *outs):
        g = g_ref[...]
        delta, m2, v2 = _adamw_math(g, w_ref[...], m_ref[...], v_ref[...])
        for k, val in enumerate((g, delta, m2, v2)):
            norm_mix, b_gate, qa, ka, qb, kb, rpb, norm_ffn = outs[8 * k:8 * k + 8]
            for dst, row0, n_rows in ((norm_mix, 0, 16), (b_gate, 16, 32), (norm_ffn, 120, 16)):
                for r in range(n_rows):
                    dst[:, r * HD:(r + 1) * HD] = val[row0 + r:row0 + r + 1, :]
            for i, dst in enumerate((qa, ka, qb, kb)):
                dst[...] = val[48 + i:49 + i, :]
            rpb[...] = val[56:120, :]
        outs[32][...] = g[LOSS_ROW:LOSS_ROW + 1, 0:1]

    vm = pl.BlockSpec(memory_space=pltpu.VMEM)
    kinds = [jax.ShapeDtypeStruct(sh, F32) for sh in
             ((1, D), (1, 2 * D), (1, HD), (1, HD), (1, HD), (1, HD), (64, HD), (1, D))]
    outs = pl.pallas_call(
        body, name="small_params_adamw", in_specs=[vm] * 4, out_specs=[vm] * 33,
        out_shape=kinds * 4 + [jax.ShapeDtypeStruct((1, 1), F32)],
    )(g, w, m, v)
    return [outs[8 * k:8 * k + 8] for k in range(4)], outs[32]


def _pack_small(norm_mix, b_gate, qa, ka, qb, kb, rpb, norm_ffn):
    gains = jnp.concatenate([qa, ka, qb, kb, jnp.zeros((4, HD), F32)], axis=0)
    rpb_pad = jnp.pad(rpb.reshape(4 * (2 * WIN_R - 1), 2 * WIN_C - 1), ((0, 4), (0, HD - (2 * WIN_C - 1))))
    return jnp.concatenate([norm_mix.reshape(16, HD), b_gate.reshape(32, HD), gains, rpb_pad,
                            norm_ffn.reshape(16, HD), jnp.zeros((8, HD), F32)], axis=0)


LOSS_ROW = 136


def _rpb_from_rows(rows):
    return rows[:60, :2 * WIN_C - 1].reshape(1, 4, 2 * WIN_R - 1, 2 * WIN_C - 1)


def kernel(x, norm_mix, w_in, b_gate, q_norm_a, k_norm_a, q_norm_b, k_norm_b, rpb_b, w_proj_a, w_proj_b, w_out, norm_ffn, w_up, w_down, loss_target, m_norm_mix, m_w_in, m_b_gate, m_q_norm_a, m_k_norm_a, m_q_norm_b, m_k_norm_b, m_rpb_b, m_w_proj_a, m_w_proj_b, m_w_out, m_norm_ffn, m_w_up, m_w_down, v_norm_mix, v_w_in, v_b_gate, v_q_norm_a, v_k_norm_a, v_q_norm_b, v_k_norm_b, v_rpb_b, v_w_proj_a, v_w_proj_b, v_w_out, v_norm_ffn, v_w_up, v_w_down):
    big_w = (w_in[0], w_proj_a[0], w_proj_b[0], w_out[0], w_up[0], w_down[0])
    big_m = (m_w_in[0], m_w_proj_a[0], m_w_proj_b[0], m_w_out[0], m_w_up[0], m_w_down[0])
    big_v = (v_w_in[0], v_w_proj_a[0], v_w_proj_b[0], v_w_out[0], v_w_up[0], v_w_down[0])
    names = ("w_in", "w_proj_a", "w_proj_b", "w_out", "w_up", "w_down")

    shards = [_cast_bf16(w) for w in big_w[:5]]
    g_in, = _gather_on_sequencer(shards[0:1], "gather_w_in")
    g_pa, g_pb, g_out, g_up = _gather_on_sequencer(shards[1:5], "gather_w_mix_up")
    small_w = _pack_small(norm_mix, b_gate, q_norm_a, k_norm_a, q_norm_b, k_norm_b, rpb_b, norm_ffn)
    small_m = _pack_small(m_norm_mix, m_b_gate, m_q_norm_a, m_k_norm_a, m_q_norm_b, m_k_norm_b, m_rpb_b, m_norm_ffn)
    small_v = _pack_small(v_norm_mix, v_b_gate, v_q_norm_a, v_k_norm_a, v_q_norm_b, v_k_norm_b, v_rpb_b, v_norm_ffn)
    g_down = [_gather_on_sequencer([_cast_bf16(big_w[5], part=h, parts=2, after=(small_w, small_m, small_v) * h)],
                                   f"gather_w_down_{h}")[0].reshape(1, D_FF, D // 2) for h in range(2)]

    upd = [None] * 6
    in_flight = {}

    def weight_grads(tag, operands):
        sums = {i: _mm_tn_pair(a, b, name=f"grad_{names[i]}", **GRAD_TILES[i]) for i, (a, b) in operands.items()}
        new = list(sums.values())
        in_flight.update(zip(sums, _chip_exchange_on_sequencer(new, f"chip_exchange_{tag}")))
        return new

    def riders(name):
        i = {"proj_bwd": 5}.get(name)
        if i is None:
            return None
        return (_adamw_rider(in_flight.pop(i), big_w[i], big_m[i], big_v[i]),
                functools.partial(upd.__setitem__, i))

    loss, grad_x, small_g = _local_step(
        x[0], loss_target[0], norm_mix, b_gate, small_w[48:56], small_w[56:120], norm_ffn,
        g_in, g_pa, g_pb, g_out.reshape(D, D), g_up, g_down, weight_grads, riders)

    g_norm_mix, g_b, g_gains, g_rpb, g_norm_ffn = small_g
    small_part = jnp.concatenate([g_norm_mix.reshape(16, HD), g_b.reshape(32, HD),
                                  g_gains, g_rpb, g_norm_ffn.reshape(16, HD),
                                  jnp.pad(loss, ((0, 7), (0, HD - 1)))], axis=0)
    last = grad_x
    for i, r in in_flight.items():
        if i == 0:
            small_sum = _small_exchange(small_part, after=[last])
            small, total = _small_adamw(small_sum, small_w, small_m, small_v)
            last = total
        upd[i] = _adamw(r, big_w[i], big_m[i], big_v[i], name=f"adamw_{names[i]}", after=[last])
        last = upd[i][0]
    s_g, s_d, s_m, s_v = ((*k[:6], _rpb_from_rows(k[6]), k[7]) for k in small)
    b_g, b_d, b_m, b_v = ([u[j][None] for u in upd] for j in range(4))

    def order(small, big):
        nm, bg, qa, ka, qb, kb, rpb, nf = small
        w_in_, pa_, pb_, out_, up_, down_ = big
        return (nm, w_in_, bg, qa, ka, qb, kb, rpb, pa_, pb_, out_, nf, up_, down_)

    return (total[0, 0], grad_x[None], *order(s_g, b_g), *order(s_d, b_d), *order(s_m, b_m), *order(s_v, b_v))
```

```python
import functools
from typing import Callable, NamedTuple

import jax
import jax.numpy as jnp
import numpy as np
from jax import lax
from jax.experimental import pallas as pl
from jax.experimental.pallas import tpu as pltpu
from jax.experimental.pallas import tpu_sc as plsc

F32 = jnp.float32
BF16 = jnp.bfloat16

N_DEV = 8
S = 2048
D = 2048
HD = 128
NH = 16
NH_A = 12
QKV = NH * HD
D_IN = 3 * QKV + 2 * D
D_BR = 512
D_FF = 4 * D
GRID_W = 64
ROWS = S // GRID_W
WIN_R = 8
WIN_C = 16
EPS = 1e-6
NEG = -1e30
SCALE = HD ** -0.5
ROPE_THETA = 10000.0
DILATIONS = (1, 4, 16)
HALF_A = 64
QB = 128

LR, B1, B2, AEPS, WD, STEP = 0.001, 0.9, 0.999, 1e-08, 0.01, 10
BC1 = 1.0 - B1 ** STEP
BC2 = 1.0 - B2 ** STEP

VMEM_LIMIT = 56 * 1024 * 1024
MESH = pl.DeviceIdType.MESH

NN = (((1,), (0,)), ((), ()))
NT = (((1,), (1,)), ((), ()))
TN = (((0,), (0,)), ((), ()))


def _params(sem):
    return pltpu.CompilerParams(dimension_semantics=sem, vmem_limit_bytes=VMEM_LIMIT)


def _matmul(a, b, *, product, grid, a_spec, b_spec, epi, out_shape, out_specs, name,
            extra=(), extra_specs=(), after=(), carried=False, rider=None, into=()):
    n_extra = len(extra)
    single = not isinstance(out_shape, (list, tuple))
    out_shape = [out_shape] if single else list(out_shape)
    out_specs = [out_specs] if single else list(out_specs)
    ride = rider(grid[0] * grid[1], lambda j, i: j * grid[1] + i) if rider else None
    r_in = list(ride.inputs) if ride else []
    n_main = len(out_shape)

    def body(a_ref, b_ref, *rest):
        n_in = n_extra + len(after) + len(r_in)
        ins, outs = rest[:n_in], rest[n_in + len(into):]
        epi(product(a_ref, b_ref, ins[:n_extra]), ins[:n_extra], outs[:n_main])
        if ride:
            ride.body(ins[n_extra + len(after):], outs[n_main:])

    res = pl.pallas_call(
        body, name=name, grid=grid,
        in_specs=[a_spec, b_spec, *extra_specs, *[pl.BlockSpec(memory_space=pl.ANY)] * len(after),
                  *(ride.in_specs if ride else []), *[pl.BlockSpec(memory_space=pl.ANY)] * len(into)],
        out_specs=out_specs + (ride.out_specs if ride else []),
        out_shape=out_shape + (ride.out_shape if ride else []),
        input_output_aliases={2 + n_extra + len(after) + len(r_in) + k: k for k in range(len(into))},
        compiler_params=_params(("arbitrary", "arbitrary") if carried else ("parallel", "parallel")),
    )(a, b, *extra, *after, *r_in, *into)
    main = res[0] if single else res[:n_main]
    return (main, res[n_main:]) if ride else main


def _dot(x, y, dims):
    return lax.dot_general(x, y, dims, preferred_element_type=F32)


def _epi_store(acc, ex, outs):
    outs[0][...] = acc.astype(outs[0].dtype)


def _epi_residual(acc, ex, outs):
    outs[0][...] = acc + ex[0][...]


def _mm_nn(a, b3, *, tm, tn, name, out_dtypes=(F32,), epi=_epi_store, extra=(), total=False,
           col0=0, width=None, into=()):
    m, kdim = a.shape
    g, _, ng = b3.shape
    n = g * ng
    c0 = col0 // tn
    if tn <= ng:
        npg = ng // tn
        b_spec = pl.BlockSpec((None, kdim, tn), lambda j, i: (j // npg, 0, j % npg))

        def product(a_ref, b_ref, ex):
            return _dot(a_ref[...], b_ref[...], NN)
    else:
        gb = tn // ng
        b_spec = pl.BlockSpec((gb, kdim, ng), lambda j, i: (j, 0, 0))

        def product(a_ref, b_ref, ex):
            return jnp.concatenate([_dot(a_ref[...], b_ref[q], NN) for q in range(gb)], axis=1)

    tile = pl.BlockSpec((tm, tn), lambda j, i: (i, j + c0))
    shapes = [jax.ShapeDtypeStruct((m, width or n), dt) for dt in out_dtypes]
    specs = [tile] * len(shapes)
    if total:
        shapes.append(jax.ShapeDtypeStruct((1, 1), F32))
        specs.append(pl.BlockSpec((1, 1), lambda j, i: (0, 0)))
    single = len(shapes) == 1
    return _matmul(
        a, b3, product=product, grid=(n // tn, m // tm), epi=epi, name=name, carried=total, into=into,
        a_spec=pl.BlockSpec((tm, kdim), lambda j, i: (i, 0)), b_spec=b_spec,
        extra=extra, extra_specs=[tile] * len(extra),
        out_shape=shapes[0] if single else shapes, out_specs=specs[0] if single else specs)


def _mm_nt(a, b3, *, tm, tn, name, out_dtype=F32, epi=_epi_store, extra=(), after=(), rider=None, more_b=()):
    m, kdim = a.shape
    _, n, _ = b3.shape
    n_b = len(more_b)

    def product(a_ref, b_ref, ex):
        acc, k0 = None, 0
        for ref in (b_ref, *ex[:n_b]):
            for q in range(ref.shape[0]):
                part = _dot(a_ref[:, k0:k0 + ref.shape[2]], ref[q], NT)
                acc = part if acc is None else acc + part
                k0 += ref.shape[2]
        return acc

    def write(acc, ex, outs):
        epi(acc, ex[n_b:], outs)

    def w_spec(w):
        return pl.BlockSpec((w.shape[0], tn, w.shape[2]), lambda j, i: (0, j, 0))

    tile = pl.BlockSpec((tm, tn), lambda j, i: (i, j))
    return _matmul(
        a, b3, product=product, grid=(n // tn, m // tm), epi=write, name=name,
        a_spec=pl.BlockSpec((tm, kdim), lambda j, i: (i, 0)), b_spec=w_spec(b3),
        extra=(*more_b, *extra), extra_specs=[w_spec(w) for w in more_b] + [tile] * len(extra),
        after=after, rider=rider,
        out_shape=jax.ShapeDtypeStruct((m, n), out_dtype), out_specs=tile)


def _mm_tn(a, b, *, tm, tn, name, groups=1, out_dtype=BF16):
    t, m = a.shape
    _, n = b.shape
    ng = n // groups
    if tn <= ng:
        npg = ng // tn
        out_spec = pl.BlockSpec((None, tm, tn), lambda j, i: (j // npg, i, j % npg))
        epi = _epi_store

        def product(a_ref, b_ref, ex):
            return _dot(a_ref[...], b_ref[...], TN)
    else:
        gb = tn // ng
        out_spec = pl.BlockSpec((gb, tm, ng), lambda j, i: (j, i, 0))

        def product(a_ref, b_ref, ex):
            return [_dot(a_ref[...], b_ref[:, q * ng:(q + 1) * ng], TN) for q in range(gb)]

        def epi(parts, ex, outs):
            for q, part in enumerate(parts):
                outs[0][q] = part.astype(out_dtype)

    return _matmul(
        a, b, product=product, grid=(n // tn, m // tm), epi=epi, name=name,
        a_spec=pl.BlockSpec((t, tm), lambda j, i: (0, i)),
        b_spec=pl.BlockSpec((t, tn), lambda j, i: (0, j)),
        out_shape=jax.ShapeDtypeStruct((groups, m, ng), out_dtype), out_specs=out_spec)


def _rms_fwd(x, g, *, name, tr=256):
    def body(x_ref, g_ref, y_ref, r_ref):
        xv = x_ref[...]
        r = lax.rsqrt(jnp.mean(xv * xv, axis=-1, keepdims=True) + EPS)
        y_ref[...] = (xv * r * g_ref[...]).astype(BF16)
        r_ref[...] = r

    row = pl.BlockSpec((tr, D), lambda i: (i, 0))
    return pl.pallas_call(
        body, name=name, grid=(S // tr,),
        in_specs=[row, pl.BlockSpec((1, D), lambda i: (0, 0))],
        out_specs=[row, pl.BlockSpec((tr, 1), lambda i: (i, 0))],
        out_shape=[jax.ShapeDtypeStruct((S, D), BF16), jax.ShapeDtypeStruct((S, 1), F32)],
        compiler_params=_params(("parallel",)),
    )(x, g)


def _rms_bwd(dy, x, rstd, g, resid, *, name, bf16_copy, tr=256):
    def body(dy_ref, x_ref, r_ref, g_ref, res_ref, dx_ref, *rest):
        dg_ref = rest[-1]
        r = r_ref[...]
        xh = x_ref[...] * r
        dyv = dy_ref[...]
        t = dyv * g_ref[...]
        dx = r * (t - xh * jnp.mean(t * xh, axis=-1, keepdims=True)) + res_ref[...]
        dx_ref[...] = dx
        if bf16_copy:
            rest[0][...] = dx.astype(BF16)
        part = jnp.sum(dyv * xh, axis=0, keepdims=True)

        @pl.when(pl.program_id(0) == 0)
        def _():
            dg_ref[...] = part

        @pl.when(pl.program_id(0) > 0)
        def _():
            dg_ref[...] += part

    row = pl.BlockSpec((tr, D), lambda i: (i, 0))
    vec = pl.BlockSpec((1, D), lambda i: (0, 0))
    return pl.pallas_call(
        body, name=name, grid=(S // tr,),
        in_specs=[row, row, pl.BlockSpec((tr, 1), lambda i: (i, 0)), vec, row],
        out_specs=[row] + [row] * bf16_copy + [vec],
        out_shape=[jax.ShapeDtypeStruct((S, D), F32)] + [jax.ShapeDtypeStruct((S, D), BF16)] * bf16_copy
        + [jax.ShapeDtypeStruct((1, D), F32)],
        compiler_params=_params(("arbitrary",)),
    )(dy, x, rstd, g, resid)


def _rope_tables():
    pos = np.arange(S, dtype=np.float32)
    inv = (ROPE_THETA ** (-np.arange(0, HD, 2, dtype=np.float32) / HD)).astype(np.float32)
    ang = pos[:, None] * inv[None, :]
    cos, sin = np.cos(ang), np.sin(ang)
    return (jnp.asarray(np.concatenate([cos, cos], axis=-1), F32),
            jnp.asarray(np.concatenate([-sin, sin], axis=-1), F32))


def _swap_halves(t):
    return pltpu.roll(t, HD // 2, axis=1)


TOK = 256


def _lane_block_spec(d, last=HD):
    return pl.BlockSpec((4, TOK // d, d * last), lambda i: (0, i, 0))


def _to_lane_blocks(dst, head, val, d, scr, dtype):
    w = val.shape[1]
    if d == 1:
        dst[head] = val.astype(dtype)
        return
    scr[...] = val
    for r in range(d):
        dst[head, :, r * w:(r + 1) * w] = scr[pl.ds(r, TOK // d, stride=d), :].astype(dtype)


def _from_lane_blocks(src, head, d, w, scr):
    if d == 1:
        return src[head].astype(F32)
    for r in range(d):
        scr[pl.ds(r, TOK // d, stride=d), :] = src[head, :, r * w:(r + 1) * w].astype(F32)
    return scr[...]


def _qk_prep(proj, gains, cos2, sin2):
    def body(q_ref, k_ref, v_ref, g_ref, c_ref, s_ref, *rest):
        outs, scr = rest[:-1], rest[-1]
        cos, sin = c_ref[...], s_ref[...]
        for which, (src, row_a, row_b) in enumerate(((q_ref, 0, 2), (k_ref, 1, 3), (v_ref, None, None))):
            for h in range(NH):
                y = src[:, h * HD:(h + 1) * HD]
                if row_a is not None:
                    y = y * lax.rsqrt(jnp.mean(y * y, axis=-1, keepdims=True) + EPS)
                    if h < NH_A:
                        y = y * g_ref[row_a:row_a + 1, :]
                        y = y * cos + _swap_halves(y) * sin
                    else:
                        y = y * g_ref[row_b:row_b + 1, :]
                if h < NH_A:
                    gi = h // 4
                    _to_lane_blocks(outs[3 * gi + which], h % 4, y, DILATIONS[gi], scr, BF16)
                else:
                    hb = h - NH_A
                    outs[9 + which][:, hb * HD:(hb + 1) * HD] = y.astype(BF16)

    def blk(c):
        return pl.BlockSpec((TOK, QKV), lambda i: (i, c))
    tab = pl.BlockSpec((TOK, HD), lambda i: (i, 0))
    out_specs, out_shape = [], []
    for d in DILATIONS:
        out_specs += [_lane_block_spec(d)] * 3
        out_shape += [jax.ShapeDtypeStruct((4, S // d, d * HD), BF16)] * 3
    out_specs += [pl.BlockSpec((TOK, D_BR), lambda i: (i, 0))] * 3
    out_shape += [jax.ShapeDtypeStruct((S, D_BR), BF16)] * 3
    outs = pl.pallas_call(
        body, name="qk_prep", grid=(S // TOK,),
        in_specs=[blk(0), blk(1), blk(2), pl.BlockSpec((8, HD), lambda i: (0, 0)), tab, tab],
        out_specs=out_specs, out_shape=out_shape,
        scratch_shapes=[pltpu.VMEM((TOK, HD), F32)],
        compiler_params=_params(("parallel",)),
    )(proj, proj, proj, gains, cos2, sin2)
    return [tuple(outs[3 * gi:3 * gi + 3]) for gi in range(3)], tuple(outs[9:12])


def _qk_prep_bwd(dproj, proj, gains, cos2, sin2, grads_a, grads_b):
    def body(dp_in, q_ref, k_ref, g_ref, c_ref, s_ref, *rest):
        grads, (dp_out, dg_ref, scr) = rest[:12], rest[12:]
        del dp_in
        cos, sin = c_ref[...], s_ref[...]

        def grad_of(which, h):
            if h < NH_A:
                gi = h // 4
                return _from_lane_blocks(grads[3 * gi + which], h % 4, DILATIONS[gi], HD, scr)
            hb = h - NH_A
            return grads[9 + which][:, hb * HD:(hb + 1) * HD].astype(F32)

        dg_rows = []
        for which, (src, base, row_a, row_b) in enumerate(((q_ref, 0, 0, 2), (k_ref, QKV, 1, 3))):
            dg_a = jnp.zeros((1, HD), F32)
            dg_b = jnp.zeros((1, HD), F32)
            for h in range(NH):
                t = src[:, h * HD:(h + 1) * HD]
                dy = grad_of(which, h)
                r = lax.rsqrt(jnp.mean(t * t, axis=-1, keepdims=True) + EPS)
                xh = t * r
                if h < NH_A:
                    dy = dy * cos - _swap_halves(dy) * sin
                    gain = g_ref[row_a:row_a + 1, :]
                    dg_a = dg_a + jnp.sum(dy * xh, axis=0, keepdims=True)
                else:
                    gain = g_ref[row_b:row_b + 1, :]
                    dg_b = dg_b + jnp.sum(dy * xh, axis=0, keepdims=True)
                u = dy * gain
                dx = r * (u - xh * jnp.mean(u * xh, axis=-1, keepdims=True))
                dp_out[:, base + h * HD:base + (h + 1) * HD] = dx.astype(BF16)
            dg_rows += [(row_a, dg_a), (row_b, dg_b)]
        for h in range(NH):
            dp_out[:, 2 * QKV + h * HD:2 * QKV + (h + 1) * HD] = grad_of(2, h).astype(BF16)

        @pl.when(pl.program_id(0) == 0)
        def _():
            dg_ref[...] = jnp.zeros((8, HD), F32)

        for row, val in dg_rows:
            dg_ref[row:row + 1, :] += val

    def blk(c):
        return pl.BlockSpec((TOK, QKV), lambda i: (i, c))
    tab = pl.BlockSpec((TOK, HD), lambda i: (i, 0))
    gain_spec = pl.BlockSpec((8, HD), lambda i: (0, 0))
    grad_specs = [s for d in DILATIONS for s in [_lane_block_spec(d)] * 3]
    grad_specs += [pl.BlockSpec((TOK, D_BR), lambda i: (i, 0))] * 3
    return pl.pallas_call(
        body, name="qk_prep_bwd", grid=(S // TOK,),
        in_specs=[pl.BlockSpec(memory_space=pl.ANY), blk(0), blk(1), gain_spec, tab, tab] + grad_specs,
        out_specs=[pl.BlockSpec((TOK, 3 * QKV), lambda i: (i, 0)), gain_spec],
        out_shape=[jax.ShapeDtypeStruct((S, D_IN), BF16), jax.ShapeDtypeStruct((8, HD), F32)],
        input_output_aliases={0: 0},
        scratch_shapes=[pltpu.VMEM((TOK, HD), F32)],
        compiler_params=_params(("arbitrary",)),
    )(dproj, proj, proj, gains, cos2, sin2, *[g for grp in grads_a for g in grp], *grads_b)


def _mix_fwd(oa, ob, w_pa, w_pb, proj, b_gate, *, tr=256):
    def body(oa_ref, ob_ref, pa_ref, pb_ref, la_ref, lb_ref, ba_ref, bb_ref, mix_ref, ya_ref, yb_ref):
        ya = jnp.concatenate([_dot(oa_ref[...], pa_ref[q], NN) for q in range(N_DEV)], axis=1)
        yb = jnp.concatenate([_dot(ob_ref[...], pb_ref[q], NN) for q in range(N_DEV)], axis=1)
        ga = jax.nn.sigmoid(la_ref[...] + ba_ref[...])
        gb = jax.nn.sigmoid(lb_ref[...] + bb_ref[...])
        mix_ref[...] = (ga * ya + gb * yb).astype(BF16)
        ya_ref[...] = ya.astype(BF16)
        yb_ref[...] = yb.astype(BF16)

    row = pl.BlockSpec((tr, D), lambda i: (i, 0))
    branch = pl.BlockSpec((tr, D_BR), lambda i: (i, 0))
    whole = pl.BlockSpec((N_DEV, D_BR, D // N_DEV), lambda i: (0, 0, 0))
    return pl.pallas_call(
        body, name="mix_fwd", grid=(S // tr,),
        in_specs=[branch, branch, whole, whole,
                  pl.BlockSpec((tr, D), lambda i: (i, 3)), pl.BlockSpec((tr, D), lambda i: (i, 4)),
                  pl.BlockSpec((1, D), lambda i: (0, 0)), pl.BlockSpec((1, D), lambda i: (0, 1))],
        out_specs=[row, row, row], out_shape=[jax.ShapeDtypeStruct((S, D), BF16)] * 3,
        compiler_params=_params(("parallel",)),
    )(oa, ob, w_pa, w_pb, proj, proj, b_gate, b_gate)


def _mix_bwd(dh1b, w_out, proj, b_gate, ya, yb, *, tr=256):
    def body(dh_ref, w_ref, la_ref, lb_ref, b_ref, ya_ref, yb_ref, dya_ref, dyb_ref, dp_ref, db_ref):
        dm = _dot(dh_ref[...], w_ref[...], NT)
        parts = []
        for l_ref, y_ref, dy_ref, lo in ((la_ref, ya_ref, dya_ref, 0), (lb_ref, yb_ref, dyb_ref, D)):
            g = jax.nn.sigmoid(l_ref[...] + b_ref[:, lo:lo + D])
            dy_ref[...] = (dm * g).astype(BF16)
            dl = dm * y_ref[...].astype(F32) * g * (1.0 - g)
            dp_ref[:, lo:lo + D] = dl.astype(BF16)
            parts.append(jnp.sum(dl, axis=0, keepdims=True))
        part = jnp.concatenate(parts, axis=1)

        @pl.when(pl.program_id(0) == 0)
        def _():
            db_ref[...] = part

        @pl.when(pl.program_id(0) > 0)
        def _():
            db_ref[...] += part

    row = pl.BlockSpec((tr, D), lambda i: (i, 0))
    vec = pl.BlockSpec((1, 2 * D), lambda i: (0, 0))
    gate_cols = pl.BlockSpec((pl.Element(tr), pl.Element(2 * D)), lambda i: (i * tr, 3 * QKV))
    return pl.pallas_call(
        body, name="mix_bwd", grid=(S // tr,),
        in_specs=[row, pl.BlockSpec((D, D), lambda i: (0, 0)),
                  pl.BlockSpec((tr, D), lambda i: (i, 3)), pl.BlockSpec((tr, D), lambda i: (i, 4)), vec, row, row],
        out_specs=[row, row, gate_cols, vec],
        out_shape=[jax.ShapeDtypeStruct((S, D), BF16), jax.ShapeDtypeStruct((S, D), BF16),
                   jax.ShapeDtypeStruct((S, D_IN), BF16), jax.ShapeDtypeStruct((1, 2 * D), F32)],
        compiler_params=_params(("arbitrary",)),
    )(dh1b, w_out, proj, proj, b_gate, ya, yb)


def _band_blocks(m_len):
    wk = min(m_len, QB + 2 * QB)
    return [(qb * QB, min(max(qb * QB - QB, 0), m_len - wk), wk) for qb in range(m_len // QB)]


def _band_scores(q, kw, q0, k0, wk):
    s = _dot(q, kw, NT) * SCALE
    qpos = q0 + lax.broadcasted_iota(jnp.int32, (QB, 1), 0)
    kpos = k0 + lax.broadcasted_iota(jnp.int32, (1, wk), 1)
    return jnp.where(jnp.abs(kpos - qpos) <= HALF_A, s, NEG)


def _attn_a_fwd(q, k, v, gi):
    d = DILATIONS[gi]
    m_len = S // d

    def body(q_ref, k_ref, v_ref, o_ref, lse_ref):
        for r in range(d):
            lanes = slice(r * HD, (r + 1) * HD)
            for q0, k0, wk in _band_blocks(m_len):
                s = _band_scores(q_ref[q0:q0 + QB, lanes], k_ref[k0:k0 + wk, lanes], q0, k0, wk)
                m = jnp.max(s, axis=-1, keepdims=True)
                p = jnp.exp(s - m)
                l = jnp.sum(p, axis=-1, keepdims=True)
                o_ref[q0:q0 + QB, lanes] = _dot(p.astype(BF16), v_ref[k0:k0 + wk, lanes], NN) / l
                lse_ref[q0:q0 + QB, r:r + 1] = m + jnp.log(l)

    head = pl.BlockSpec((None, m_len, d * HD), lambda h: (h, 0, 0))
    stat = pl.BlockSpec((None, m_len, d), lambda h: (h, 0, 0))
    return pl.pallas_call(
        body, name=f"attn_a_fwd_{gi}", grid=(4,),
        in_specs=[head, head, head], out_specs=[head, stat],
        out_shape=[jax.ShapeDtypeStruct((4, m_len, d * HD), F32), jax.ShapeDtypeStruct((4, m_len, d), F32)],
        compiler_params=_params(("parallel",)),
    )(q, k, v)


def _combine_a(os, lses):
    def body(o0, o1, o2, l0, l1, l2, oa_ref, lse_ref, scr, scr1):
        for h in range(4):
            o = [_from_lane_blocks(ref, h, d, HD, scr) for ref, d in zip((o0, o1, o2), DILATIONS)]
            a, b, c = (_from_lane_blocks(ref, h, d, 1, scr1) for ref, d in zip((l0, l1, l2), DILATIONS))
            m = jnp.maximum(jnp.maximum(a, b), c)
            wa, wb, wc = jnp.exp(a - m), jnp.exp(b - m), jnp.exp(c - m)
            tot = wa + wb + wc
            oa_ref[:, h * HD:(h + 1) * HD] = ((wa * o[0] + wb * o[1] + wc * o[2]) / tot).astype(BF16)
            lse_ref[h] = m + jnp.log(tot)

    return pl.pallas_call(
        body, name="combine_a", grid=(S // TOK,),
        in_specs=[_lane_block_spec(d) for d in DILATIONS] + [_lane_block_spec(d, 1) for d in DILATIONS],
        out_specs=[pl.BlockSpec((TOK, D_BR), lambda i: (i, 0)), pl.BlockSpec((4, TOK, 1), lambda i: (0, i, 0))],
        out_shape=[jax.ShapeDtypeStruct((S, D_BR), BF16), jax.ShapeDtypeStruct((4, S, 1), F32)],
        scratch_shapes=[pltpu.VMEM((TOK, HD), F32), pltpu.VMEM((TOK, 1), F32)],
        compiler_params=_params(("parallel",)),
    )(*os, *lses)


def _proj_a_bwd(dya, w_pa, oa, lse):
    kg = D // N_DEV

    def body(dy_ref, w_ref, o_ref, l_ref, *rest):
        outs, (scr, scr1) = rest[:9], rest[9:]
        doa = _dot(dy_ref[:, 0:kg], w_ref[0], NT)
        for q in range(1, N_DEV):
            doa = doa + _dot(dy_ref[:, q * kg:(q + 1) * kg], w_ref[q], NT)
        for h in range(4):
            do = doa[:, h * HD:(h + 1) * HD]
            dsum = jnp.sum(do * o_ref[:, h * HD:(h + 1) * HD].astype(F32), axis=-1, keepdims=True)
            for gi, d in enumerate(DILATIONS):
                _to_lane_blocks(outs[3 * gi], h, do, d, scr, BF16)
                _to_lane_blocks(outs[3 * gi + 1], h, l_ref[h], d, scr1, F32)
                _to_lane_blocks(outs[3 * gi + 2], h, dsum, d, scr1, F32)

    row = pl.BlockSpec((TOK, D_BR), lambda i: (i, 0))
    out_specs, out_shape = [], []
    for d in DILATIONS:
        out_specs += [_lane_block_spec(d), _lane_block_spec(d, 1), _lane_block_spec(d, 1)]
        out_shape += [jax.ShapeDtypeStruct((4, S // d, d * HD), BF16)] + [jax.ShapeDtypeStruct((4, S // d, d), F32)] * 2
    outs = pl.pallas_call(
        body, name="proj_a_bwd", grid=(S // TOK,),
        in_specs=[pl.BlockSpec((TOK, D), lambda i: (i, 0)),
                  pl.BlockSpec((N_DEV, D_BR, kg), lambda i: (0, 0, 0)),
                  row, pl.BlockSpec((4, TOK, 1), lambda i: (0, i, 0))],
        out_specs=out_specs, out_shape=out_shape,
        scratch_shapes=[pltpu.VMEM((TOK, HD), F32), pltpu.VMEM((TOK, 1), F32)],
        compiler_params=_params(("parallel",)),
    )(dya, w_pa, oa, lse)
    return [tuple(outs[3 * gi:3 * gi + 3]) for gi in range(3)]


def _attn_a_bwd(q, k, v, do, lse, dsum, gi):
    d = DILATIONS[gi]
    m_len = S // d

    def body(q_ref, k_ref, v_ref, do_ref, lse_ref, dsum_ref, dq_ref, dk_out, dv_out, dk_ref, dv_ref):
        dk_ref[...] = jnp.zeros((m_len, d * HD), F32)
        dv_ref[...] = jnp.zeros((m_len, d * HD), F32)
        for r in range(d):
            lanes = slice(r * HD, (r + 1) * HD)
            for q0, k0, wk in _band_blocks(m_len):
                rows, keys = slice(q0, q0 + QB), slice(k0, k0 + wk)
                qv, kw, vw, dov = q_ref[rows, lanes], k_ref[keys, lanes], v_ref[keys, lanes], do_ref[rows, lanes]
                p = jnp.exp(_band_scores(qv, kw, q0, k0, wk) - lse_ref[rows, r:r + 1])
                ds = (p * (_dot(dov, vw, NT) - dsum_ref[rows, r:r + 1]) * SCALE).astype(BF16)
                dq_ref[rows, lanes] = _dot(ds, kw, NN).astype(BF16)
                dk_ref[keys, lanes] += _dot(ds, qv, TN)
                dv_ref[keys, lanes] += _dot(p.astype(BF16), dov, TN)
        dk_out[...] = dk_ref[...].astype(BF16)
        dv_out[...] = dv_ref[...].astype(BF16)

    head = pl.BlockSpec((None, m_len, d * HD), lambda h: (h, 0, 0))
    stat = pl.BlockSpec((None, m_len, d), lambda h: (h, 0, 0))
    shape = jax.ShapeDtypeStruct((4, m_len, d * HD), BF16)
    return pl.pallas_call(
        body, name=f"attn_a_bwd_{gi}", grid=(4,),
        in_specs=[head, head, head, head, stat, stat], out_specs=[head, head, head],
        out_shape=[shape, shape, shape],
        scratch_shapes=[pltpu.VMEM((m_len, d * HD), F32)] * 2,
        compiler_params=_params(("arbitrary",)),
    )(q, k, v, do, lse, dsum)


KEYS_B = WIN_R * GRID_W
N_OFF = WIN_R


def _bias_constants():
    q = np.arange(GRID_W)[:, None]
    kc = np.arange(GRID_W)[None, :]
    dc = np.clip(kc - q, -(WIN_C - 1), WIN_C - 1) + (WIN_C - 1)
    expand = np.zeros((HD, GRID_W * GRID_W), np.float32)
    expand[dc.reshape(-1), np.arange(GRID_W * GRID_W)] = 1.0
    cs = np.clip(q - WIN_C // 2, 0, GRID_W - WIN_C)
    keep = ((kc >= cs) & (kc < cs + WIN_C)).reshape(1, -1).astype(np.float32)
    sel = np.zeros((64, 4 * N_OFF * WIN_R), np.float32)
    for h in range(4):
        for off in range(N_OFF):
            for j in range(WIN_R):
                sel[h * (2 * WIN_R - 1) + off + j, (h * N_OFF + off) * WIN_R + j] = 1.0
    return jnp.asarray(expand), jnp.asarray(keep), jnp.asarray(sel)


def _bias_expand(rpb_pad, expand, keep, sel):
    def body(r_ref, e_ref, k_ref, s_ref, o_ref):
        t = lax.dot_general(r_ref[...], e_ref[...], NN, precision=lax.Precision.HIGHEST,
                            preferred_element_type=F32)
        rows = lax.dot_general(s_ref[...], t, TN, precision=lax.Precision.HIGHEST,
                               preferred_element_type=F32)
        o_ref[...] = jnp.where(k_ref[...] > 0.5, rows, NEG)

    return pl.pallas_call(
        body, name="bias_expand",
        out_shape=jax.ShapeDtypeStruct((4 * N_OFF * WIN_R, GRID_W * GRID_W), F32),
        compiler_params=pltpu.CompilerParams(vmem_limit_bytes=VMEM_LIMIT),
    )(rpb_pad, expand, keep, sel)


def _bias_reduce(dbias_tab):
    lane0 = GRID_W - WIN_C
    flip = np.zeros((GRID_W, GRID_W), np.float32)
    flip[np.arange(GRID_W), GRID_W - 1 - np.arange(GRID_W)] = 1.0
    place = np.zeros((WIN_R, 64, 4 * N_OFF), np.float32)
    for j in range(WIN_R):
        for h in range(4):
            for off in range(N_OFF):
                place[j, h * (2 * WIN_R - 1) + off + j, h * N_OFF + off] = 1.0

    def exact(x, y):
        return lax.dot_general(x, y, NN, precision=lax.Precision.HIGHEST, preferred_element_type=F32)

    def body(x_ref, flip_ref, place_ref, o_ref, z_ref):
        for h in range(4):
            for off in range(N_OFF):
                lined_up = pltpu.roll(exact(flip_ref[...], x_ref[h, off]), 0, axis=1, stride=1, stride_axis=0)
                z_ref[h * N_OFF + off:h * N_OFF + off + 1, :] = jnp.sum(lined_up, axis=0, keepdims=True)
        acc = jnp.zeros((64, HD), F32)
        for j in range(WIN_R):
            at_zero = pltpu.roll(z_ref[...], (KEYS_B - (j * GRID_W + lane0)) % KEYS_B, axis=1)[:, :HD]
            acc = acc + exact(place_ref[j], at_zero)
        lane = lax.broadcasted_iota(jnp.int32, (64, HD), 1)
        o_ref[...] = jnp.where(lane < 2 * WIN_C - 1, acc, 0.0)

    return pl.pallas_call(
        body, name="bias_reduce", out_shape=jax.ShapeDtypeStruct((64, HD), F32),
        scratch_shapes=[pltpu.VMEM((4 * N_OFF, KEYS_B), F32)],
        compiler_params=pltpu.CompilerParams(vmem_limit_bytes=VMEM_LIMIT),
    )(dbias_tab, jnp.asarray(flip), jnp.asarray(place))


def _rows_to_tab(rows):
    t = rows.reshape(4, N_OFF, WIN_R, GRID_W, GRID_W)
    return t.transpose(0, 1, 3, 2, 4).reshape(4, N_OFF, GRID_W, KEYS_B)


def _row_window(r):
    r0 = jnp.clip(r - WIN_R // 2, 0, ROWS - WIN_R)
    off = r0 + (WIN_R - 1) - r
    return pl.multiple_of(r * GRID_W, GRID_W), pl.multiple_of(r0 * GRID_W, GRID_W), off


def _attn_b_fwd(qn, kn, vb, bias_tab):
    def body(q_ref, k_ref, v_ref, b_ref, o_ref, lse_ref):
        def row(r, carry):
            qs, ks, off = _row_window(r)
            q = q_ref[pl.ds(qs, GRID_W), :]
            s = lax.dot_general(q, k_ref[pl.ds(ks, KEYS_B), :], NT, preferred_element_type=F32) * SCALE
            s = s + b_ref[off]
            m = jnp.max(s, axis=-1, keepdims=True)
            p = jnp.exp(s - m)
            l = jnp.sum(p, axis=-1, keepdims=True)
            o = lax.dot_general(p.astype(BF16), v_ref[pl.ds(ks, KEYS_B), :], NN, preferred_element_type=F32)
            o_ref[pl.ds(qs, GRID_W), :] = (o / l).astype(BF16)
            lse_ref[pl.ds(qs, GRID_W), :] = m + jnp.log(l)
            return carry

        lax.fori_loop(0, ROWS, row, 0, unroll=8)

    full = pl.BlockSpec((S, HD), lambda h: (0, h))
    return pl.pallas_call(
        body, name="attn_b_fwd", grid=(4,),
        in_specs=[full, full, full, pl.BlockSpec((None, N_OFF, GRID_W, KEYS_B), lambda h: (h, 0, 0, 0))],
        out_specs=[pl.BlockSpec((S, HD), lambda h: (0, h)), pl.BlockSpec((None, S, 1), lambda h: (h, 0, 0))],
        out_shape=[jax.ShapeDtypeStruct((S, D_BR), BF16), jax.ShapeDtypeStruct((4, S, 1), F32)],
        compiler_params=_params(("parallel",)),
    )(qn, kn, vb, bias_tab)


def _attn_b_bwd(qn, kn, vb, bias_tab, ob, dob, lse):
    def body(q_ref, k_ref, v_ref, b_ref, o_ref, do_ref, lse_ref, dq_ref, dk_out, dv_out, db_ref, dk_ref, dv_ref):
        dk_ref[...] = jnp.zeros((S, HD), F32)
        dv_ref[...] = jnp.zeros((S, HD), F32)
        db_ref[...] = jnp.zeros((N_OFF, GRID_W, KEYS_B), F32)

        def row(r, carry):
            qs, ks, off = _row_window(r)
            rows = pl.ds(qs, GRID_W)
            keys = pl.ds(ks, KEYS_B)
            q = q_ref[rows, :]
            kw = k_ref[keys, :]
            s = lax.dot_general(q, kw, NT, preferred_element_type=F32) * SCALE + b_ref[off]
            p = jnp.exp(s - lse_ref[rows, :])
            do = do_ref[rows, :]
            dobf = do.astype(BF16)
            dsum = jnp.sum(do * o_ref[rows, :].astype(F32), axis=-1, keepdims=True)
            dp = lax.dot_general(dobf, v_ref[keys, :], NT, preferred_element_type=F32)
            ds = p * (dp - dsum)
            db_ref[off] += ds
            dsb = (ds * SCALE).astype(BF16)
            dq_ref[rows, :] = lax.dot_general(dsb, kw, NN, preferred_element_type=F32).astype(BF16)
            dk_ref[keys, :] += lax.dot_general(dsb, q, TN, preferred_element_type=F32)
            dv_ref[keys, :] += lax.dot_general(p.astype(BF16), dobf, TN, preferred_element_type=F32)
            return carry

        lax.fori_loop(0, ROWS, row, 0, unroll=8)
        dk_out[...] = dk_ref[...].astype(BF16)
        dv_out[...] = dv_ref[...].astype(BF16)

    full = pl.BlockSpec((S, HD), lambda h: (0, h))
    slot = pl.BlockSpec((S, HD), lambda h: (0, h))
    tab = pl.BlockSpec((None, N_OFF, GRID_W, KEYS_B), lambda h: (h, 0, 0, 0))
    shape = jax.ShapeDtypeStruct((S, D_BR), BF16)
    return pl.pallas_call(
        body, name="attn_b_bwd", grid=(4,),
        in_specs=[full, full, full, tab, slot, slot, pl.BlockSpec((None, S, 1), lambda h: (h, 0, 0))],
        out_specs=[slot, slot, slot, tab],
        out_shape=[shape, shape, shape, jax.ShapeDtypeStruct((4, N_OFF, GRID_W, KEYS_B), F32)],
        scratch_shapes=[pltpu.VMEM((S, HD), F32)] * 2,
        compiler_params=_params(("arbitrary",)),
    )(qn, kn, vb, bias_tab, ob, dob, lse)


def _epi_relu_sq(acc, ex, outs):
    u = jnp.maximum(acc, 0.0)
    outs[0][...] = u.astype(BF16)
    outs[1][...] = (u * u).astype(BF16)


def _epi_relu_sq_bwd(acc, ex, outs):
    outs[0][...] = (acc * (2.0 * ex[0][...].astype(F32))).astype(BF16)


def _epi_loss_head(acc, ex, outs):
    e = acc + ex[0][...] - ex[1][...]
    dy = e * (1.0 / D)
    outs[0][...] = dy
    outs[1][...] = dy.astype(BF16)
    part = (0.5 / D) * jnp.sum(jnp.sum(e * e, axis=-1, keepdims=True), axis=0, keepdims=True)
    first = (pl.program_id(0) == 0) & (pl.program_id(1) == 0)

    @pl.when(first)
    def _():
        outs[2][...] = part

    @pl.when(jnp.logical_not(first))
    def _():
        outs[2][...] += part


def _local_step(x, target, norm_mix, b_gate, gains, rpb_pad, norm_ffn,
                w_in, w_pa, w_pb, w_out, w_up, w_down, weight_grads, riders=lambda name: None):
    def ridden(name, *args, **kwargs):
        ride = riders(name)
        if ride is None:
            return _mm_nt(*args, name=name, **kwargs)
        out, rode = _mm_nt(*args, name=name, rider=ride[0], **kwargs)
        ride[1](rode)
        return out

    cos2, sin2 = _rope_tables()
    expand, keep, sel = _bias_constants()
    w_out3 = w_out[None]

    xn, rstd1 = _rms_fwd(x, norm_mix, name="rms_mix")
    proj = _mm_nn(xn, w_in, tm=1024, tn=1280, name="proj")
    qkv_a, qkv_b = _qk_prep(proj, gains, cos2, sin2)
    fwd_a = [_attn_a_fwd(*qkv_a[gi], gi) for gi in range(3)]
    oa, lse_a = _combine_a([o for o, _ in fwd_a], [l for _, l in fwd_a])
    bias_tab = _rows_to_tab(_bias_expand(rpb_pad, expand, keep, sel))
    ob, lse_b = _attn_b_fwd(*qkv_b, bias_tab)
    mixed, ya, yb = _mix_fwd(oa, ob, w_pa, w_pb, proj, b_gate)
    h1 = _mm_nn(mixed, w_out3, tm=1024, tn=1024, name="out_proj", epi=_epi_residual, extra=(x,))
    hn, rstd2 = _rms_fwd(h1, norm_ffn, name="rms_ffn")
    u, usq = _mm_nn(hn, w_up, tm=1024, tn=1024, name="ffn_up", epi=_epi_relu_sq,
                    out_dtypes=(BF16, BF16))
    dy, dyb, loss = _mm_nn(usq, w_down[0], tm=512, tn=512, name="ffn_down_0", epi=_epi_loss_head,
                           extra=(h1, target), out_dtypes=(F32, BF16), total=True, width=D)
    dy, dyb, loss_1 = _mm_nn(usq, w_down[1], tm=512, tn=512, name="ffn_down_1", epi=_epi_loss_head,
                             extra=(h1, target), out_dtypes=(F32, BF16), total=True, width=D,
                             col0=D // 2, into=(dy, dyb))
    loss = loss + loss_1

    sent = weight_grads("w_down", {5: (usq, dyb)})
    du = _mm_nt(dyb, w_down[0], more_b=(w_down[1],), tm=1024, tn=1024, name="ffn_down_bwd", out_dtype=BF16,
                epi=_epi_relu_sq_bwd, extra=(u,), after=sent)
    sent = weight_grads("w_up", {4: (hn, du)})
    dhn = ridden("ffn_up_bwd", du, w_up, tm=512, tn=512, after=sent)
    dh1, dh1b, g_norm_ffn = _rms_bwd(dhn, h1, rstd2, norm_ffn, dy, name="rms_ffn_bwd", bf16_copy=True)

    dya, dyb2, dproj, g_b = _mix_bwd(dh1b, w_out, proj, b_gate, ya, yb)
    sent = weight_grads("w_mix", {3: (mixed, dh1b), 1: (oa, dya), 2: (ob, dyb2)})
    dob = _mm_nt(dyb2, w_pb, tm=1024, tn=D_BR, name="proj_b_bwd", after=sent)
    prep = _proj_a_bwd(dya, w_pa, oa, lse_a)
    grads_a = [_attn_a_bwd(*qkv_a[gi], *prep[gi], gi) for gi in range(3)]
    dqb, dkb, dvb, dbias = _attn_b_bwd(*qkv_b, bias_tab, ob, dob, lse_b)
    g_rpb = _bias_reduce(dbias)
    dproj, g_gains = _qk_prep_bwd(dproj, proj, gains, cos2, sin2, grads_a, (dqb, dkb, dvb))
    sent = weight_grads("w_in", {0: (xn, dproj)})
    dxn = ridden("proj_bwd", dproj, w_in, tm=256, tn=512, after=sent)
    grad_x, g_norm_mix = _rms_bwd(dxn, x, rstd1, norm_mix, dh1, name="rms_mix_bwd", bf16_copy=False)

    small = (g_norm_mix, g_b, g_gains, g_rpb, g_norm_ffn)
    return loss, grad_x, small


def _cast_bf16(w, *, part=0, parts=1, after=(), tr=256):
    rows, cols = w.shape[0], w.shape[1] // parts
    tr = min(tr, rows)

    def body(w_ref, *rest):
        rest[-1][...] = w_ref[...].astype(BF16)

    return pl.pallas_call(
        body, name=f"cast_{rows}x{cols}_{part}", grid=(rows // tr,),
        in_specs=[pl.BlockSpec((tr, cols), lambda i: (i, part))] + [pl.BlockSpec(memory_space=pl.ANY)] * len(after),
        out_specs=pl.BlockSpec((tr, cols), lambda i: (i, 0)),
        out_shape=jax.ShapeDtypeStruct((rows, cols), BF16), compiler_params=_params(("parallel",)),
    )(w, *after)


def _me_and_peers():
    x, y, c = lax.axis_index("x"), lax.axis_index("y"), lax.axis_index("c")
    me = 4 * x + 2 * y + c
    peers = []
    for k in range(1, N_DEV):
        px = 1 - x if k & 4 else x
        py = 1 - y if k & 2 else y
        pc = 1 - c if k & 1 else c
        peers.append(((px, py, pc), 4 * px + 2 * py + pc))
    return me, peers


def _gather_on_sequencer(shards, name):
    n = len(shards)
    hbm = pltpu.MemorySpace.HBM
    ins = [jax.new_ref(s, memory_space=hbm) for s in shards]
    outs = [jax.empty_ref(jax.ShapeDtypeStruct((N_DEV,) + s.shape, s.dtype), memory_space=hbm) for s in shards]

    @_sequencer(name, ((n, N_DEV - 1), (n, N_DEV - 1), (n,)), 0)
    def launch(send, recv, lsem):
        x, y, c = lax.axis_index("x"), lax.axis_index("y"), lax.axis_index("c")
        me, sibling = (x, y, c), (x, y, 1 - c)
        chips = [(1 - x, y), (x, 1 - y), (1 - x, 1 - y)]
        _handshake([sibling] + [(*chip, c) for chip in chips])

        def copy(w, k, block, to, src=None):
            px, py, pc = block
            dst = outs[w].at[4 * px + 2 * py + pc]
            return pltpu.make_async_remote_copy(dst if src is None else src, dst, send.at[w, k], recv.at[w, k],
                                                device_id=to, device_id_type=MESH)

        local = [pltpu.make_async_copy(ins[w], outs[w].at[4 * x + 2 * y + c], lsem.at[w]) for w in range(n)]
        for cp in local:
            cp.start()
        first = []
        for w in range(n):
            first += [copy(w, 1 + j, me, (*chip, c), src=ins[w]) for j, chip in enumerate(chips)]
            first.append(copy(w, 0, me, sibling, src=ins[w]))
        for cp in first:
            cp.start()
        passed = []
        for w in range(n):
            for j, chip in enumerate(chips):
                copy(w, 1 + j, (*chip, c), me).wait_recv()
                cp = copy(w, 4 + j, (*chip, c), sibling)
                cp.start()
                passed.append(cp)
        for w in range(n):
            copy(w, 0, sibling, me).wait_recv()
            for j, chip in enumerate(chips):
                copy(w, 4 + j, (*chip, 1 - c), me).wait_recv()
        for cp in first + passed:
            cp.wait_send()
        for cp in local:
            cp.wait()

    launch()
    return [o[...] for o in outs]


N_CHIP = 4


def _sequencer(name, n_sems, collective_id):
    return functools.partial(
        pl.kernel, mesh=plsc.ScalarSubcoreMesh(axis_name="seq", num_cores=1), name=name,
        scratch_types=tuple(pltpu.SemaphoreType.DMA(s) for s in n_sems),
        compiler_params=pltpu.CompilerParams(collective_id=collective_id))


def _handshake(peers):
    barrier = pltpu.get_barrier_semaphore()
    for peer in peers:
        pl.semaphore_signal(barrier, inc=1, device_id=peer, device_id_type=MESH)
    pl.semaphore_wait(barrier, len(peers))


def _chip_exchange_on_sequencer(parts, name):
    n = len(parts)
    hbm = pltpu.MemorySpace.HBM
    ins = [jax.new_ref(p, memory_space=hbm) for p in parts]
    outs = [jax.empty_ref(jax.ShapeDtypeStruct(p.shape, p.dtype), memory_space=hbm) for p in parts]

    @_sequencer(name, ((n, 3), (n, 3), (n,)), 2)
    def launch(send, recv, lsem):
        x, y, c = lax.axis_index("x"), lax.axis_index("y"), lax.axis_index("c")
        mine = 2 * x + y
        chips = [(1 - x, y), (x, 1 - y), (1 - x, 1 - y)]
        _handshake([(*chip, c) for chip in chips])
        local = [pltpu.make_async_copy(ins[w].at[mine], outs[w].at[mine], lsem.at[w]) for w in range(n)]
        for cp in local:
            cp.start()
        sends = []
        for w in range(n):
            for j, (px, py) in enumerate(chips):
                cp = pltpu.make_async_remote_copy(ins[w].at[2 * px + py], outs[w].at[mine],
                                                  send.at[w, j], recv.at[w, j],
                                                  device_id=(px, py, c), device_id_type=MESH)
                cp.start()
                sends.append(cp)
        for w in range(n):
            for j, (px, py) in enumerate(chips):
                pltpu.make_async_remote_copy(ins[w].at[mine], outs[w].at[2 * px + py],
                                             send.at[w, j], recv.at[w, j],
                                             device_id=(px, py, c), device_id_type=MESH).wait_recv()
        for cp in sends:
            cp.wait_send()
        for cp in local:
            cp.wait()

    launch()
    return [o[...] for o in outs]


GRAD_TILES = (dict(blocks_on="cols", tm=512, tn=1280), dict(blocks_on="cols", tm=512, tn=256),
              dict(blocks_on="cols", tm=512, tn=256), dict(blocks_on="rows", tm=256, tn=2048),
              dict(blocks_on="cols", tm=1024, tn=1024), dict(blocks_on="rows", tm=1024, tn=1024))


def _mm_tn_pair(a, b, *, blocks_on, tm, tn, name):
    t_len, m = a.shape
    n = b.shape[1]
    if blocks_on == "rows":
        rows, cols, inner = m // N_DEV, n, n // tn
        assert tm == rows
        a_spec = pl.BlockSpec((t_len, tm), lambda p, t, blk: (0, blk[p]))
        b_spec = pl.BlockSpec((t_len, tn), lambda p, t, blk: (0, t))
        out_spec = pl.BlockSpec((None, tm, tn), lambda p, t, blk: (
            jnp.maximum(p - N_CHIP, 0), 0, jnp.where(p < N_CHIP, 0, t)))
    else:
        rows, cols, inner = m, n // N_DEV, m // tm
        assert tn == cols
        a_spec = pl.BlockSpec((t_len, tm), lambda p, t, blk: (0, t))
        b_spec = pl.BlockSpec((t_len, tn), lambda p, t, blk: (0, blk[p]))
        out_spec = pl.BlockSpec((None, tm, tn), lambda p, t, blk: (
            jnp.maximum(p - N_CHIP, 0), jnp.where(p < N_CHIP, 0, t), 0))

    def body(blk_ref, a_ref, b_ref, o_ref, land, stage, send_sem, recv_sem):
        del blk_ref
        p, t = pl.program_id(0), pl.program_id(1)
        step = p * inner + t
        x, y, c = lax.axis_index("x"), lax.axis_index("y"), lax.axis_index("c")
        tile = _dot(a_ref[...], b_ref[...], TN)

        def to_sibling(slot, chip, piece):
            return pltpu.make_async_remote_copy(stage.at[slot], land.at[chip, piece], send_sem.at[slot],
                                                recv_sem.at[chip, piece],
                                                device_id=(x, y, 1 - c), device_id_type=MESH)

        @pl.when(p < N_CHIP)
        def _():
            slot = step % 2

            @pl.when(step >= 2)
            def _():
                to_sibling(slot, 0, 0).wait_send()

            stage[slot] = tile.astype(BF16)
            to_sibling(slot, p, t).start()

        @pl.when(step == N_CHIP * inner)
        def _():
            for slot in range(min(2, N_CHIP * inner)):
                to_sibling(slot, 0, 0).wait_send()

        @pl.when(p >= N_CHIP)
        def _():
            chip = p - N_CHIP
            to_sibling(0, chip, t).wait_recv()
            o_ref[...] = (tile + land[chip, t].astype(F32)).astype(BF16)

    c = lax.axis_index("c")
    order = jnp.stack([2 * ch + 1 - c for ch in range(N_CHIP)] + [2 * ch + c for ch in range(N_CHIP)])
    return pl.pallas_call(
        body, name=name,
        grid_spec=pltpu.PrefetchScalarGridSpec(
            num_scalar_prefetch=1, grid=(N_DEV, inner), in_specs=[a_spec, b_spec], out_specs=out_spec,
            scratch_shapes=[pltpu.VMEM((N_CHIP, inner, tm, tn), BF16), pltpu.VMEM((2, tm, tn), BF16),
                            pltpu.SemaphoreType.DMA((2,)), pltpu.SemaphoreType.DMA((N_CHIP, inner))]),
        out_shape=jax.ShapeDtypeStruct((N_CHIP, rows, cols), BF16),
        compiler_params=_params(("arbitrary", "arbitrary")),
    )(order.astype(jnp.int32), a, b)


def _adamw_math(g, w, m, v):
    m2 = B1 * m + (1.0 - B1) * g
    v2 = B2 * v + (1.0 - B2) * (g * g)
    delta = -LR * ((m2 / BC1) / (jnp.sqrt(v2 / BC2) + AEPS) + WD * w)
    return delta, m2, v2


def _adamw_block(ins, outs):
    p_ref, w_ref, m_ref, v_ref = ins
    g = p_ref[0].astype(F32)
    for b in range(1, N_CHIP):
        g = g + p_ref[b].astype(F32)
    delta, m2, v2 = _adamw_math(g, w_ref[...], m_ref[...], v_ref[...])
    for ref, val in zip(outs, (g, delta, m2, v2)):
        ref[...] = val


class _Rider(NamedTuple):
    inputs: tuple
    in_specs: list
    out_shape: list
    out_specs: list
    body: Callable


def _adamw_rider(parts, w, m, v):
    rows, cols = w.shape

    def rider(steps, step_of):
        rr = rows // steps
        blk = pl.BlockSpec((rr, cols), lambda *ids: (step_of(*ids[:2]), 0))
        chips = pl.BlockSpec((N_CHIP, rr, cols), lambda *ids: (0, step_of(*ids[:2]), 0))
        shape = jax.ShapeDtypeStruct((rows, cols), F32)
        return _Rider((parts, w, m, v), [chips, blk, blk, blk], [shape] * 4, [blk] * 4, _adamw_block)

    return rider


def _adamw(parts, w, m, v, *, name, after=(), tr=256):
    rows, cols = w.shape

    def body(*refs):
        _adamw_block(refs[:4], refs[4 + len(after):])

    spec = pl.BlockSpec((tr, cols), lambda i: (i, 0))
    shape = jax.ShapeDtypeStruct((rows, cols), F32)
    return pl.pallas_call(
        body, name=name, grid=(rows // tr,),
        in_specs=[pl.BlockSpec((N_CHIP, tr, cols), lambda i: (0, i, 0)), spec, spec, spec]
        + [pl.BlockSpec(memory_space=pl.ANY)] * len(after),
        out_specs=[spec] * 4, out_shape=[shape] * 4,
        compiler_params=_params(("parallel",)),
    )(parts, w, m, v, *after)


def _small_exchange(part, after=()):
    rows = part.shape[0]

    def body(p_ref, *rest):
        g_ref, buf, send, recv = rest[len(after):]
        me, peers = _me_and_peers()
        buf[me] = p_ref[...]
        sends = []
        for k, (dev, _) in enumerate(peers):
            cp = pltpu.make_async_remote_copy(p_ref, buf.at[me], send.at[k], recv.at[k],
                                              device_id=dev, device_id_type=MESH)
            cp.start()
            sends.append(cp)
        for k, (dev, idx) in enumerate(peers):
            pltpu.make_async_remote_copy(p_ref, buf.at[idx], send.at[k], recv.at[k],
                                         device_id=dev, device_id_type=MESH).wait_recv()
        for cp in sends:
            cp.wait_send()
        g = buf[0]
        for b in range(1, N_DEV):
            g = g + buf[b]
        g_ref[...] = g

    vm = pl.BlockSpec(memory_space=pltpu.VMEM)
    return pl.pallas_call(
        body, name="small_params_exchange",
        in_specs=[vm] + [pl.BlockSpec(memory_space=pl.ANY)] * len(after),
        out_specs=vm, out_shape=jax.ShapeDtypeStruct((rows, HD), F32),
        scratch_shapes=[pltpu.VMEM((N_DEV, rows, HD), F32),
                        pltpu.SemaphoreType.DMA((N_DEV - 1,)), pltpu.SemaphoreType.DMA((N_DEV - 1,))],
    )(part, *after)


def _small_adamw(g, w, m, v):
    def body(g_ref, w_ref, m_ref, v_ref, *outs):
        g = g_ref[...]
        delta, m2, v2 = _adamw_math(g, w_ref[...], m_ref[...], v_ref[...])
        for k, val in enumerate((g, delta, m2, v2)):
            norm_mix, b_gate, qa, ka, qb, kb, rpb, norm_ffn = outs[8 * k:8 * k + 8]
            for dst, row0, n_rows in ((norm_mix, 0, 16), (b_gate, 16, 32), (norm_ffn, 120, 16)):
                for r in range(n_rows):
                    dst[:, r * HD:(r + 1) * HD] = val[row0 + r:row0 + r + 1, :]
            for i, dst in enumerate((qa, ka, qb, kb)):
                dst[...] = val[48 + i:49 + i, :]
            for h in range(4):
                r0 = 56 + h * (2 * WIN_R - 1)
                rpb[0, h] = val[r0:r0 + 2 * WIN_R - 1, :2 * WIN_C - 1]
        outs[32][...] = g[LOSS_ROW:LOSS_ROW + 1, 0:1]

    vm = pl.BlockSpec(memory_space=pltpu.VMEM)
    kinds = [jax.ShapeDtypeStruct(sh, F32) for sh in
             ((1, D), (1, 2 * D), (1, HD), (1, HD), (1, HD), (1, HD), (1, 4, 2 * WIN_R - 1, 2 * WIN_C - 1), (1, D))]
    outs = pl.pallas_call(
        body, name="small_params_adamw", in_specs=[vm] * 4, out_specs=[vm] * 33,
        out_shape=kinds * 4 + [jax.ShapeDtypeStruct((1, 1), F32)],
    )(g, w, m, v)
    return [outs[8 * k:8 * k + 8] for k in range(4)], outs[32]


def _pack_small(norm_mix, b_gate, qa, ka, qb, kb, rpb, norm_ffn):
    gains = jnp.concatenate([qa, ka, qb, kb, jnp.zeros((4, HD), F32)], axis=0)
    rpb_pad = jnp.pad(rpb.reshape(4 * (2 * WIN_R - 1), 2 * WIN_C - 1), ((0, 4), (0, HD - (2 * WIN_C - 1))))
    return jnp.concatenate([norm_mix.reshape(16, HD), b_gate.reshape(32, HD), gains, rpb_pad,
                            norm_ffn.reshape(16, HD), jnp.zeros((8, HD), F32)], axis=0)


LOSS_ROW = 136


def kernel(x, norm_mix, w_in, b_gate, q_norm_a, k_norm_a, q_norm_b, k_norm_b, rpb_b, w_proj_a, w_proj_b, w_out, norm_ffn, w_up, w_down, loss_target, m_norm_mix, m_w_in, m_b_gate, m_q_norm_a, m_k_norm_a, m_q_norm_b, m_k_norm_b, m_rpb_b, m_w_proj_a, m_w_proj_b, m_w_out, m_norm_ffn, m_w_up, m_w_down, v_norm_mix, v_w_in, v_b_gate, v_q_norm_a, v_k_norm_a, v_q_norm_b, v_k_norm_b, v_rpb_b, v_w_proj_a, v_w_proj_b, v_w_out, v_norm_ffn, v_w_up, v_w_down):
    big_w = (w_in[0], w_proj_a[0], w_proj_b[0], w_out[0], w_up[0], w_down[0])
    big_m = (m_w_in[0], m_w_proj_a[0], m_w_proj_b[0], m_w_out[0], m_w_up[0], m_w_down[0])
    big_v = (v_w_in[0], v_w_proj_a[0], v_w_proj_b[0], v_w_out[0], v_w_up[0], v_w_down[0])
    names = ("w_in", "w_proj_a", "w_proj_b", "w_out", "w_up", "w_down")

    shards = [_cast_bf16(w) for w in big_w[:5]]
    g_in, = _gather_on_sequencer(shards[0:1], "gather_w_in")
    g_pa, g_pb, g_out, g_up = _gather_on_sequencer(shards[1:5], "gather_w_mix_up")
    small_w = _pack_small(norm_mix, b_gate, q_norm_a, k_norm_a, q_norm_b, k_norm_b, rpb_b, norm_ffn)
    small_m = _pack_small(m_norm_mix, m_b_gate, m_q_norm_a, m_k_norm_a, m_q_norm_b, m_k_norm_b, m_rpb_b, m_norm_ffn)
    small_v = _pack_small(v_norm_mix, v_b_gate, v_q_norm_a, v_k_norm_a, v_q_norm_b, v_k_norm_b, v_rpb_b, v_norm_ffn)
    g_down = [_gather_on_sequencer([_cast_bf16(big_w[5], part=h, parts=2, after=(small_w, small_m, small_v) * h)],
                                   f"gather_w_down_{h}")[0].reshape(1, D_FF, D // 2) for h in range(2)]

    upd = [None] * 6
    in_flight = {}

    def weight_grads(tag, operands):
        sums = {i: _mm_tn_pair(a, b, name=f"grad_{names[i]}", **GRAD_TILES[i]) for i, (a, b) in operands.items()}
        new = list(sums.values())
        in_flight.update(zip(sums, _chip_exchange_on_sequencer(new, f"chip_exchange_{tag}")))
        return new

    def riders(name):
        i = {"proj_bwd": 5}.get(name)
        if i is None:
            return None
        return (_adamw_rider(in_flight.pop(i), big_w[i], big_m[i], big_v[i]),
                functools.partial(upd.__setitem__, i))

    loss, grad_x, small_g = _local_step(
        x[0], loss_target[0], norm_mix, b_gate, small_w[48:56], small_w[56:120], norm_ffn,
        g_in, g_pa, g_pb, g_out.reshape(D, D), g_up, g_down, weight_grads, riders)

    g_norm_mix, g_b, g_gains, g_rpb, g_norm_ffn = small_g
    small_part = jnp.concatenate([g_norm_mix.reshape(16, HD), g_b.reshape(32, HD),
                                  g_gains, g_rpb, g_norm_ffn.reshape(16, HD),
                                  jnp.pad(loss, ((0, 7), (0, HD - 1)))], axis=0)
    last = grad_x
    for i, r in in_flight.items():
        if i == 0:
            small_sum = _small_exchange(small_part, after=[last])
            small, total = _small_adamw(small_sum, small_w, small_m, small_v)
            last = total
        upd[i] = _adamw(r, big_w[i], big_m[i], big_v[i], name=f"adamw_{names[i]}", after=[last])
        last = upd[i][0]
    s_g, s_d, s_m, s_v = small
    b_g, b_d, b_m, b_v = ([u[j][None] for u in upd] for j in range(4))

    def order(small, big):
        nm, bg, qa, ka, qb, kb, rpb, nf = small
        w_in_, pa_, pb_, out_, up_, down_ = big
        return (nm, w_in_, bg, qa, ka, qb, kb, rpb, pa_, pb_, out_, nf, up_, down_)

    return (total[0, 0], grad_x[None], *order(s_g, b_g), *order(s_d, b_d), *order(s_m, b_m), *order(s_v, b_v))
```

```python
import functools
from typing import Callable, NamedTuple

import jax
import jax.numpy as jnp
import numpy as np
from jax import lax
from jax.experimental import pallas as pl
from jax.experimental.pallas import tpu as pltpu
from jax.experimental.pallas import tpu_sc as plsc

F32 = jnp.float32
BF16 = jnp.bfloat16

N_DEV = 8
S = 2048
D = 2048
HD = 128
NH = 16
NH_A = 12
QKV = NH * HD
D_IN = 3 * QKV + 2 * D
D_BR = 512
D_FF = 4 * D
GRID_W = 64
ROWS = S // GRID_W
WIN_R = 8
WIN_C = 16
EPS = 1e-6
NEG = -1e30
SCALE = HD ** -0.5
ROPE_THETA = 10000.0
DILATIONS = (1, 4, 16)
HALF_A = 64
QB = 128

LR, B1, B2, AEPS, WD, STEP = 0.001, 0.9, 0.999, 1e-08, 0.01, 10
BC1 = 1.0 - B1 ** STEP
BC2 = 1.0 - B2 ** STEP

VMEM_LIMIT = 56 * 1024 * 1024
MESH = pl.DeviceIdType.MESH

NN = (((1,), (0,)), ((), ()))
NT = (((1,), (1,)), ((), ()))
TN = (((0,), (0,)), ((), ()))


def _params(sem):
    return pltpu.CompilerParams(dimension_semantics=sem, vmem_limit_bytes=VMEM_LIMIT)


def _matmul(a, b, *, product, grid, a_spec, b_spec, epi, out_shape, out_specs, name,
            extra=(), extra_specs=(), after=(), carried=False, rider=None, into=()):
    n_extra = len(extra)
    single = not isinstance(out_shape, (list, tuple))
    out_shape = [out_shape] if single else list(out_shape)
    out_specs = [out_specs] if single else list(out_specs)
    ride = rider(grid[0] * grid[1], lambda j, i: j * grid[1] + i) if rider else None
    r_in = list(ride.inputs) if ride else []
    n_main = len(out_shape)

    def body(a_ref, b_ref, *rest):
        n_in = n_extra + len(after) + len(r_in)
        ins, outs = rest[:n_in], rest[n_in + len(into):]
        epi(product(a_ref, b_ref, ins[:n_extra]), ins[:n_extra], outs[:n_main])
        if ride:
            ride.body(ins[n_extra + len(after):], outs[n_main:])

    res = pl.pallas_call(
        body, name=name, grid=grid,
        in_specs=[a_spec, b_spec, *extra_specs, *[pl.BlockSpec(memory_space=pl.ANY)] * len(after),
                  *(ride.in_specs if ride else []), *[pl.BlockSpec(memory_space=pl.ANY)] * len(into)],
        out_specs=out_specs + (ride.out_specs if ride else []),
        out_shape=out_shape + (ride.out_shape if ride else []),
        input_output_aliases={2 + n_extra + len(after) + len(r_in) + k: k for k in range(len(into))},
        compiler_params=_params(("arbitrary", "arbitrary") if carried else ("parallel", "parallel")),
    )(a, b, *extra, *after, *r_in, *into)
    main = res[0] if single else res[:n_main]
    return (main, res[n_main:]) if ride else main


def _dot(x, y, dims):
    return lax.dot_general(x, y, dims, preferred_element_type=F32)


def _epi_store(acc, ex, outs):
    outs[0][...] = acc.astype(outs[0].dtype)


def _epi_residual(acc, ex, outs):
    outs[0][...] = acc + ex[0][...]


def _mm_nn(a, b3, *, tm, tn, name, out_dtypes=(F32,), epi=_epi_store, extra=(), total=False,
           col0=0, width=None, into=()):
    m, kdim = a.shape
    g, _, ng = b3.shape
    n = g * ng
    c0 = col0 // tn
    if tn <= ng:
        npg = ng // tn
        b_spec = pl.BlockSpec((None, kdim, tn), lambda j, i: (j // npg, 0, j % npg))

        def product(a_ref, b_ref, ex):
            return _dot(a_ref[...], b_ref[...], NN)
    else:
        gb = tn // ng
        b_spec = pl.BlockSpec((gb, kdim, ng), lambda j, i: (j, 0, 0))

        def product(a_ref, b_ref, ex):
            return jnp.concatenate([_dot(a_ref[...], b_ref[q], NN) for q in range(gb)], axis=1)

    tile = pl.BlockSpec((tm, tn), lambda j, i: (i, j + c0))
    shapes = [jax.ShapeDtypeStruct((m, width or n), dt) for dt in out_dtypes]
    specs = [tile] * len(shapes)
    if total:
        shapes.append(jax.ShapeDtypeStruct((1, 1), F32))
        specs.append(pl.BlockSpec((1, 1), lambda j, i: (0, 0)))
    single = len(shapes) == 1
    return _matmul(
        a, b3, product=product, grid=(n // tn, m // tm), epi=epi, name=name, carried=total, into=into,
        a_spec=pl.BlockSpec((tm, kdim), lambda j, i: (i, 0)), b_spec=b_spec,
        extra=extra, extra_specs=[tile] * len(extra),
        out_shape=shapes[0] if single else shapes, out_specs=specs[0] if single else specs)


def _mm_nt(a, b3, *, tm, tn, name, out_dtype=F32, epi=_epi_store, extra=(), after=(), rider=None, more_b=(),
           single_w=False):
    m, kdim = a.shape
    _, n, _ = b3.shape
    n_b = len(more_b)

    def product(a_ref, b_ref, ex):
        acc, k0 = None, 0
        for ref in (b_ref, *ex[:n_b]):
            for q in range(ref.shape[0]):
                part = _dot(a_ref[:, k0:k0 + ref.shape[2]], ref[q], NT)
                acc = part if acc is None else acc + part
                k0 += ref.shape[2]
        return acc

    def write(acc, ex, outs):
        epi(acc, ex[n_b:], outs)

    def w_spec(w):
        return pl.BlockSpec((w.shape[0], tn, w.shape[2]), lambda j, i: (0, j, 0),
                            pipeline_mode=pl.Buffered(1) if single_w else None)

    tile = pl.BlockSpec((tm, tn), lambda j, i: (i, j))
    return _matmul(
        a, b3, product=product, grid=(n // tn, m // tm), epi=write, name=name,
        a_spec=pl.BlockSpec((tm, kdim), lambda j, i: (i, 0)), b_spec=w_spec(b3),
        extra=(*more_b, *extra), extra_specs=[w_spec(w) for w in more_b] + [tile] * len(extra),
        after=after, rider=rider,
        out_shape=jax.ShapeDtypeStruct((m, n), out_dtype), out_specs=tile)


def _mm_tn(a, b, *, tm, tn, name, groups=1, out_dtype=BF16):
    t, m = a.shape
    _, n = b.shape
    ng = n // groups
    if tn <= ng:
        npg = ng // tn
        out_spec = pl.BlockSpec((None, tm, tn), lambda j, i: (j // npg, i, j % npg))
        epi = _epi_store

        def product(a_ref, b_ref, ex):
            return _dot(a_ref[...], b_ref[...], TN)
    else:
        gb = tn // ng
        out_spec = pl.BlockSpec((gb, tm, ng), lambda j, i: (j, i, 0))

        def product(a_ref, b_ref, ex):
            return [_dot(a_ref[...], b_ref[:, q * ng:(q + 1) * ng], TN) for q in range(gb)]

        def epi(parts, ex, outs):
            for q, part in enumerate(parts):
                outs[0][q] = part.astype(out_dtype)

    return _matmul(
        a, b, product=product, grid=(n // tn, m // tm), epi=epi, name=name,
        a_spec=pl.BlockSpec((t, tm), lambda j, i: (0, i)),
        b_spec=pl.BlockSpec((t, tn), lambda j, i: (0, j)),
        out_shape=jax.ShapeDtypeStruct((groups, m, ng), out_dtype), out_specs=out_spec)


def _rms_fwd(x, g, *, name, tr=256):
    def body(x_ref, g_ref, y_ref, r_ref):
        xv = x_ref[...]
        r = lax.rsqrt(jnp.mean(xv * xv, axis=-1, keepdims=True) + EPS)
        y_ref[...] = (xv * r * g_ref[...]).astype(BF16)
        r_ref[...] = r

    row = pl.BlockSpec((tr, D), lambda i: (i, 0))
    return pl.pallas_call(
        body, name=name, grid=(S // tr,),
        in_specs=[row, pl.BlockSpec((1, D), lambda i: (0, 0))],
        out_specs=[row, pl.BlockSpec((tr, 1), lambda i: (i, 0))],
        out_shape=[jax.ShapeDtypeStruct((S, D), BF16), jax.ShapeDtypeStruct((S, 1), F32)],
        compiler_params=_params(("parallel",)),
    )(x, g)


def _rms_bwd(dy, x, rstd, g, resid, *, name, bf16_copy, tr=256):
    def body(dy_ref, x_ref, r_ref, g_ref, res_ref, dx_ref, *rest):
        dg_ref = rest[-1]
        r = r_ref[...]
        xh = x_ref[...] * r
        dyv = dy_ref[...]
        t = dyv * g_ref[...]
        dx = r * (t - xh * jnp.mean(t * xh, axis=-1, keepdims=True)) + res_ref[...]
        dx_ref[...] = dx
        if bf16_copy:
            rest[0][...] = dx.astype(BF16)
        part = jnp.sum(dyv * xh, axis=0, keepdims=True)

        @pl.when(pl.program_id(0) == 0)
        def _():
            dg_ref[...] = part

        @pl.when(pl.program_id(0) > 0)
        def _():
            dg_ref[...] += part

    row = pl.BlockSpec((tr, D), lambda i: (i, 0))
    vec = pl.BlockSpec((1, D), lambda i: (0, 0))
    return pl.pallas_call(
        body, name=name, grid=(S // tr,),
        in_specs=[row, row, pl.BlockSpec((tr, 1), lambda i: (i, 0)), vec, row],
        out_specs=[row] + [row] * bf16_copy + [vec],
        out_shape=[jax.ShapeDtypeStruct((S, D), F32)] + [jax.ShapeDtypeStruct((S, D), BF16)] * bf16_copy
        + [jax.ShapeDtypeStruct((1, D), F32)],
        compiler_params=_params(("arbitrary",)),
    )(dy, x, rstd, g, resid)


def _rope_tables():
    pos = np.arange(S, dtype=np.float32)
    inv = (ROPE_THETA ** (-np.arange(0, HD, 2, dtype=np.float32) / HD)).astype(np.float32)
    ang = pos[:, None] * inv[None, :]
    cos, sin = np.cos(ang), np.sin(ang)
    return (jnp.asarray(np.concatenate([cos, cos], axis=-1), F32),
            jnp.asarray(np.concatenate([-sin, sin], axis=-1), F32))


def _swap_halves(t):
    return pltpu.roll(t, HD // 2, axis=1)


TOK = 256


def _lane_block_spec(d, last=HD):
    return pl.BlockSpec((4, TOK // d, d * last), lambda i: (0, i, 0))


def _to_lane_blocks(dst, head, val, d, scr, dtype):
    w = val.shape[1]
    if d == 1:
        dst[head] = val.astype(dtype)
        return
    scr[...] = val
    for r in range(d):
        dst[head, :, r * w:(r + 1) * w] = scr[pl.ds(r, TOK // d, stride=d), :].astype(dtype)


def _from_lane_blocks(src, head, d, w, scr):
    if d == 1:
        return src[head].astype(F32)
    for r in range(d):
        scr[pl.ds(r, TOK // d, stride=d), :] = src[head, :, r * w:(r + 1) * w].astype(F32)
    return scr[...]


def _qk_prep(proj, gains, cos2, sin2):
    def body(q_ref, k_ref, v_ref, g_ref, c_ref, s_ref, *rest):
        outs, scr = rest[:-1], rest[-1]
        cos, sin = c_ref[...], s_ref[...]
        for which, (src, row_a, row_b) in enumerate(((q_ref, 0, 2), (k_ref, 1, 3), (v_ref, None, None))):
            for h in range(NH):
                y = src[:, h * HD:(h + 1) * HD]
                if row_a is not None:
                    y = y * lax.rsqrt(jnp.mean(y * y, axis=-1, keepdims=True) + EPS)
                    if h < NH_A:
                        y = y * g_ref[row_a:row_a + 1, :]
                        y = y * cos + _swap_halves(y) * sin
                    else:
                        y = y * g_ref[row_b:row_b + 1, :]
                if h < NH_A:
                    gi = h // 4
                    _to_lane_blocks(outs[3 * gi + which], h % 4, y, DILATIONS[gi], scr, BF16)
                else:
                    hb = h - NH_A
                    outs[9 + which][:, hb * HD:(hb + 1) * HD] = y.astype(BF16)

    def blk(c):
        return pl.BlockSpec((TOK, QKV), lambda i: (i, c))
    tab = pl.BlockSpec((TOK, HD), lambda i: (i, 0))
    out_specs, out_shape = [], []
    for d in DILATIONS:
        out_specs += [_lane_block_spec(d)] * 3
        out_shape += [jax.ShapeDtypeStruct((4, S // d, d * HD), BF16)] * 3
    out_specs += [pl.BlockSpec((TOK, D_BR), lambda i: (i, 0))] * 3
    out_shape += [jax.ShapeDtypeStruct((S, D_BR), BF16)] * 3
    outs = pl.pallas_call(
        body, name="qk_prep", grid=(S // TOK,),
        in_specs=[blk(0), blk(1), blk(2), pl.BlockSpec((8, HD), lambda i: (0, 0)), tab, tab],
        out_specs=out_specs, out_shape=out_shape,
        scratch_shapes=[pltpu.VMEM((TOK, HD), F32)],
        compiler_params=_params(("parallel",)),
    )(proj, proj, proj, gains, cos2, sin2)
    return [tuple(outs[3 * gi:3 * gi + 3]) for gi in range(3)], tuple(outs[9:12])


def _qk_prep_bwd(dproj, proj, gains, cos2, sin2, grads_a, grads_b):
    def body(dp_in, q_ref, k_ref, g_ref, c_ref, s_ref, *rest):
        grads, (dp_out, dg_ref, scr) = rest[:12], rest[12:]
        del dp_in
        cos, sin = c_ref[...], s_ref[...]

        def grad_of(which, h):
            if h < NH_A:
                gi = h // 4
                return _from_lane_blocks(grads[3 * gi + which], h % 4, DILATIONS[gi], HD, scr)
            hb = h - NH_A
            return grads[9 + which][:, hb * HD:(hb + 1) * HD].astype(F32)

        dg_rows = []
        for which, (src, base, row_a, row_b) in enumerate(((q_ref, 0, 0, 2), (k_ref, QKV, 1, 3))):
            dg_a = jnp.zeros((1, HD), F32)
            dg_b = jnp.zeros((1, HD), F32)
            for h in range(NH):
                t = src[:, h * HD:(h + 1) * HD]
                dy = grad_of(which, h)
                r = lax.rsqrt(jnp.mean(t * t, axis=-1, keepdims=True) + EPS)
                xh = t * r
                if h < NH_A:
                    dy = dy * cos - _swap_halves(dy) * sin
                    gain = g_ref[row_a:row_a + 1, :]
                    dg_a = dg_a + jnp.sum(dy * xh, axis=0, keepdims=True)
                else:
                    gain = g_ref[row_b:row_b + 1, :]
                    dg_b = dg_b + jnp.sum(dy * xh, axis=0, keepdims=True)
                u = dy * gain
                dx = r * (u - xh * jnp.mean(u * xh, axis=-1, keepdims=True))
                dp_out[:, base + h * HD:base + (h + 1) * HD] = dx.astype(BF16)
            dg_rows += [(row_a, dg_a), (row_b, dg_b)]
        for h in range(NH):
            dp_out[:, 2 * QKV + h * HD:2 * QKV + (h + 1) * HD] = grad_of(2, h).astype(BF16)

        @pl.when(pl.program_id(0) == 0)
        def _():
            dg_ref[...] = jnp.zeros((8, HD), F32)

        for row, val in dg_rows:
            dg_ref[row:row + 1, :] += val

    def blk(c):
        return pl.BlockSpec((TOK, QKV), lambda i: (i, c))
    tab = pl.BlockSpec((TOK, HD), lambda i: (i, 0))
    gain_spec = pl.BlockSpec((8, HD), lambda i: (0, 0))
    grad_specs = [s for d in DILATIONS for s in [_lane_block_spec(d)] * 3]
    grad_specs += [pl.BlockSpec((TOK, D_BR), lambda i: (i, 0))] * 3
    return pl.pallas_call(
        body, name="qk_prep_bwd", grid=(S // TOK,),
        in_specs=[pl.BlockSpec(memory_space=pl.ANY), blk(0), blk(1), gain_spec, tab, tab] + grad_specs,
        out_specs=[pl.BlockSpec((TOK, 3 * QKV), lambda i: (i, 0)), gain_spec],
        out_shape=[jax.ShapeDtypeStruct((S, D_IN), BF16), jax.ShapeDtypeStruct((8, HD), F32)],
        input_output_aliases={0: 0},
        scratch_shapes=[pltpu.VMEM((TOK, HD), F32)],
        compiler_params=_params(("arbitrary",)),
    )(dproj, proj, proj, gains, cos2, sin2, *[g for grp in grads_a for g in grp], *grads_b)


def _mix_fwd(oa, ob, w_pa, w_pb, proj, b_gate, *, tr=256):
    def body(oa_ref, ob_ref, pa_ref, pb_ref, la_ref, lb_ref, ba_ref, bb_ref, mix_ref, ya_ref, yb_ref):
        ya = jnp.concatenate([_dot(oa_ref[...], pa_ref[q], NN) for q in range(N_DEV)], axis=1)
        yb = jnp.concatenate([_dot(ob_ref[...], pb_ref[q], NN) for q in range(N_DEV)], axis=1)
        ga = jax.nn.sigmoid(la_ref[...] + ba_ref[...])
        gb = jax.nn.sigmoid(lb_ref[...] + bb_ref[...])
        mix_ref[...] = (ga * ya + gb * yb).astype(BF16)
        ya_ref[...] = ya.astype(BF16)
        yb_ref[...] = yb.astype(BF16)

    row = pl.BlockSpec((tr, D), lambda i: (i, 0))
    branch = pl.BlockSpec((tr, D_BR), lambda i: (i, 0))
    whole = pl.BlockSpec((N_DEV, D_BR, D // N_DEV), lambda i: (0, 0, 0))
    return pl.pallas_call(
        body, name="mix_fwd", grid=(S // tr,),
        in_specs=[branch, branch, whole, whole,
                  pl.BlockSpec((tr, D), lambda i: (i, 3)), pl.BlockSpec((tr, D), lambda i: (i, 4)),
                  pl.BlockSpec((1, D), lambda i: (0, 0)), pl.BlockSpec((1, D), lambda i: (0, 1))],
        out_specs=[row, row, row], out_shape=[jax.ShapeDtypeStruct((S, D), BF16)] * 3,
        compiler_params=_params(("parallel",)),
    )(oa, ob, w_pa, w_pb, proj, proj, b_gate, b_gate)


def _mix_bwd(dh1b, w_out, proj, b_gate, ya, yb, *, tr=256):
    def body(dh_ref, w_ref, la_ref, lb_ref, b_ref, ya_ref, yb_ref, dya_ref, dyb_ref, dp_ref, db_ref):
        dm = _dot(dh_ref[...], w_ref[...], NT)
        parts = []
        for l_ref, y_ref, dy_ref, lo in ((la_ref, ya_ref, dya_ref, 0), (lb_ref, yb_ref, dyb_ref, D)):
            g = jax.nn.sigmoid(l_ref[...] + b_ref[:, lo:lo + D])
            dy_ref[...] = (dm * g).astype(BF16)
            dl = dm * y_ref[...].astype(F32) * g * (1.0 - g)
            dp_ref[:, lo:lo + D] = dl.astype(BF16)
            parts.append(jnp.sum(dl, axis=0, keepdims=True))
        part = jnp.concatenate(parts, axis=1)

        @pl.when(pl.program_id(0) == 0)
        def _():
            db_ref[...] = part

        @pl.when(pl.program_id(0) > 0)
        def _():
            db_ref[...] += part

    row = pl.BlockSpec((tr, D), lambda i: (i, 0))
    vec = pl.BlockSpec((1, 2 * D), lambda i: (0, 0))
    gate_cols = pl.BlockSpec((pl.Element(tr), pl.Element(2 * D)), lambda i: (i * tr, 3 * QKV))
    return pl.pallas_call(
        body, name="mix_bwd", grid=(S // tr,),
        in_specs=[row, pl.BlockSpec((D, D), lambda i: (0, 0)),
                  pl.BlockSpec((tr, D), lambda i: (i, 3)), pl.BlockSpec((tr, D), lambda i: (i, 4)), vec, row, row],
        out_specs=[row, row, gate_cols, vec],
        out_shape=[jax.ShapeDtypeStruct((S, D), BF16), jax.ShapeDtypeStruct((S, D), BF16),
                   jax.ShapeDtypeStruct((S, D_IN), BF16), jax.ShapeDtypeStruct((1, 2 * D), F32)],
        compiler_params=_params(("arbitrary",)),
    )(dh1b, w_out, proj, proj, b_gate, ya, yb)


def _band_blocks(m_len):
    wk = min(m_len, QB + 2 * QB)
    return [(qb * QB, min(max(qb * QB - QB, 0), m_len - wk), wk) for qb in range(m_len // QB)]


def _band_scores(q, kw, q0, k0, wk):
    s = _dot(q, kw, NT) * SCALE
    qpos = q0 + lax.broadcasted_iota(jnp.int32, (QB, 1), 0)
    kpos = k0 + lax.broadcasted_iota(jnp.int32, (1, wk), 1)
    return jnp.where(jnp.abs(kpos - qpos) <= HALF_A, s, NEG)


def _attn_a_fwd(q, k, v, gi):
    d = DILATIONS[gi]
    m_len = S // d

    def body(q_ref, k_ref, v_ref, o_ref, lse_ref):
        for r in range(d):
            lanes = slice(r * HD, (r + 1) * HD)
            for q0, k0, wk in _band_blocks(m_len):
                s = _band_scores(q_ref[q0:q0 + QB, lanes], k_ref[k0:k0 + wk, lanes], q0, k0, wk)
                m = jnp.max(s, axis=-1, keepdims=True)
                p = jnp.exp(s - m)
                l = jnp.sum(p, axis=-1, keepdims=True)
                o_ref[q0:q0 + QB, lanes] = _dot(p.astype(BF16), v_ref[k0:k0 + wk, lanes], NN) / l
                lse_ref[q0:q0 + QB, r:r + 1] = m + jnp.log(l)

    head = pl.BlockSpec((None, m_len, d * HD), lambda h: (h, 0, 0))
    stat = pl.BlockSpec((None, m_len, d), lambda h: (h, 0, 0))
    return pl.pallas_call(
        body, name=f"attn_a_fwd_{gi}", grid=(4,),
        in_specs=[head, head, head], out_specs=[head, stat],
        out_shape=[jax.ShapeDtypeStruct((4, m_len, d * HD), F32), jax.ShapeDtypeStruct((4, m_len, d), F32)],
        compiler_params=_params(("parallel",)),
    )(q, k, v)


def _combine_a(os, lses):
    def body(o0, o1, o2, l0, l1, l2, oa_ref, lse_ref, scr, scr1):
        for h in range(4):
            o = [_from_lane_blocks(ref, h, d, HD, scr) for ref, d in zip((o0, o1, o2), DILATIONS)]
            a, b, c = (_from_lane_blocks(ref, h, d, 1, scr1) for ref, d in zip((l0, l1, l2), DILATIONS))
            m = jnp.maximum(jnp.maximum(a, b), c)
            wa, wb, wc = jnp.exp(a - m), jnp.exp(b - m), jnp.exp(c - m)
            tot = wa + wb + wc
            oa_ref[:, h * HD:(h + 1) * HD] = ((wa * o[0] + wb * o[1] + wc * o[2]) / tot).astype(BF16)
            lse_ref[h] = m + jnp.log(tot)

    return pl.pallas_call(
        body, name="combine_a", grid=(S // TOK,),
        in_specs=[_lane_block_spec(d) for d in DILATIONS] + [_lane_block_spec(d, 1) for d in DILATIONS],
        out_specs=[pl.BlockSpec((TOK, D_BR), lambda i: (i, 0)), pl.BlockSpec((4, TOK, 1), lambda i: (0, i, 0))],
        out_shape=[jax.ShapeDtypeStruct((S, D_BR), BF16), jax.ShapeDtypeStruct((4, S, 1), F32)],
        scratch_shapes=[pltpu.VMEM((TOK, HD), F32), pltpu.VMEM((TOK, 1), F32)],
        compiler_params=_params(("parallel",)),
    )(*os, *lses)


def _proj_a_bwd(dya, w_pa, oa, lse):
    kg = D // N_DEV

    def body(dy_ref, w_ref, o_ref, l_ref, *rest):
        outs, (scr, scr1) = rest[:9], rest[9:]
        doa = _dot(dy_ref[:, 0:kg], w_ref[0], NT)
        for q in range(1, N_DEV):
            doa = doa + _dot(dy_ref[:, q * kg:(q + 1) * kg], w_ref[q], NT)
        for h in range(4):
            do = doa[:, h * HD:(h + 1) * HD]
            dsum = jnp.sum(do * o_ref[:, h * HD:(h + 1) * HD].astype(F32), axis=-1, keepdims=True)
            for gi, d in enumerate(DILATIONS):
                _to_lane_blocks(outs[3 * gi], h, do, d, scr, BF16)
                _to_lane_blocks(outs[3 * gi + 1], h, l_ref[h], d, scr1, F32)
                _to_lane_blocks(outs[3 * gi + 2], h, dsum, d, scr1, F32)

    row = pl.BlockSpec((TOK, D_BR), lambda i: (i, 0))
    out_specs, out_shape = [], []
    for d in DILATIONS:
        out_specs += [_lane_block_spec(d), _lane_block_spec(d, 1), _lane_block_spec(d, 1)]
        out_shape += [jax.ShapeDtypeStruct((4, S // d, d * HD), BF16)] + [jax.ShapeDtypeStruct((4, S // d, d), F32)] * 2
    outs = pl.pallas_call(
        body, name="proj_a_bwd", grid=(S // TOK,),
        in_specs=[pl.BlockSpec((TOK, D), lambda i: (i, 0)),
                  pl.BlockSpec((N_DEV, D_BR, kg), lambda i: (0, 0, 0)),
                  row, pl.BlockSpec((4, TOK, 1), lambda i: (0, i, 0))],
        out_specs=out_specs, out_shape=out_shape,
        scratch_shapes=[pltpu.VMEM((TOK, HD), F32), pltpu.VMEM((TOK, 1), F32)],
        compiler_params=_params(("parallel",)),
    )(dya, w_pa, oa, lse)
    return [tuple(outs[3 * gi:3 * gi + 3]) for gi in range(3)]


def _attn_a_bwd(q, k, v, do, lse, dsum, gi):
    d = DILATIONS[gi]
    m_len = S // d

    def body(q_ref, k_ref, v_ref, do_ref, lse_ref, dsum_ref, dq_ref, dk_out, dv_out, dk_ref, dv_ref):
        dk_ref[...] = jnp.zeros((m_len, d * HD), F32)
        dv_ref[...] = jnp.zeros((m_len, d * HD), F32)
        for r in range(d):
            lanes = slice(r * HD, (r + 1) * HD)
            for q0, k0, wk in _band_blocks(m_len):
                rows, keys = slice(q0, q0 + QB), slice(k0, k0 + wk)
                qv, kw, vw, dov = q_ref[rows, lanes], k_ref[keys, lanes], v_ref[keys, lanes], do_ref[rows, lanes]
                p = jnp.exp(_band_scores(qv, kw, q0, k0, wk) - lse_ref[rows, r:r + 1])
                ds = (p * (_dot(dov, vw, NT) - dsum_ref[rows, r:r + 1]) * SCALE).astype(BF16)
                dq_ref[rows, lanes] = _dot(ds, kw, NN).astype(BF16)
                dk_ref[keys, lanes] += _dot(ds, qv, TN)
                dv_ref[keys, lanes] += _dot(p.astype(BF16), dov, TN)
        dk_out[...] = dk_ref[...].astype(BF16)
        dv_out[...] = dv_ref[...].astype(BF16)

    head = pl.BlockSpec((None, m_len, d * HD), lambda h: (h, 0, 0))
    stat = pl.BlockSpec((None, m_len, d), lambda h: (h, 0, 0))
    shape = jax.ShapeDtypeStruct((4, m_len, d * HD), BF16)
    return pl.pallas_call(
        body, name=f"attn_a_bwd_{gi}", grid=(4,),
        in_specs=[head, head, head, head, stat, stat], out_specs=[head, head, head],
        out_shape=[shape, shape, shape],
        scratch_shapes=[pltpu.VMEM((m_len, d * HD), F32)] * 2,
        compiler_params=_params(("arbitrary",)),
    )(q, k, v, do, lse, dsum)


KEYS_B = WIN_R * GRID_W
N_OFF = WIN_R


def _bias_constants():
    q = np.arange(GRID_W)[:, None]
    kc = np.arange(GRID_W)[None, :]
    dc = np.clip(kc - q, -(WIN_C - 1), WIN_C - 1) + (WIN_C - 1)
    expand = np.zeros((HD, GRID_W * GRID_W), np.float32)
    expand[dc.reshape(-1), np.arange(GRID_W * GRID_W)] = 1.0
    cs = np.clip(q - WIN_C // 2, 0, GRID_W - WIN_C)
    keep = ((kc >= cs) & (kc < cs + WIN_C)).reshape(1, -1).astype(np.float32)
    sel = np.zeros((64, 4 * N_OFF * WIN_R), np.float32)
    for h in range(4):
        for off in range(N_OFF):
            for j in range(WIN_R):
                sel[h * (2 * WIN_R - 1) + off + j, (h * N_OFF + off) * WIN_R + j] = 1.0
    return jnp.asarray(expand), jnp.asarray(keep), jnp.asarray(sel)


def _bias_expand(rpb_pad, expand, keep, sel):
    def body(r_ref, e_ref, k_ref, s_ref, o_ref):
        t = lax.dot_general(r_ref[...], e_ref[...], NN, precision=lax.Precision.HIGHEST,
                            preferred_element_type=F32)
        rows = lax.dot_general(s_ref[...], t, TN, precision=lax.Precision.HIGHEST,
                               preferred_element_type=F32)
        o_ref[...] = jnp.where(k_ref[...] > 0.5, rows, NEG)

    return pl.pallas_call(
        body, name="bias_expand",
        out_shape=jax.ShapeDtypeStruct((4 * N_OFF * WIN_R, GRID_W * GRID_W), F32),
        compiler_params=pltpu.CompilerParams(vmem_limit_bytes=VMEM_LIMIT),
    )(rpb_pad, expand, keep, sel)


def _bias_reduce(dbias_tab):
    lane0 = GRID_W - WIN_C
    flip = np.zeros((GRID_W, GRID_W), np.float32)
    flip[np.arange(GRID_W), GRID_W - 1 - np.arange(GRID_W)] = 1.0
    place = np.zeros((WIN_R, 64, 4 * N_OFF), np.float32)
    for j in range(WIN_R):
        for h in range(4):
            for off in range(N_OFF):
                place[j, h * (2 * WIN_R - 1) + off + j, h * N_OFF + off] = 1.0

    def exact(x, y):
        return lax.dot_general(x, y, NN, precision=lax.Precision.HIGHEST, preferred_element_type=F32)

    def body(x_ref, flip_ref, place_ref, o_ref, z_ref):
        for h in range(4):
            for off in range(N_OFF):
                lined_up = pltpu.roll(exact(flip_ref[...], x_ref[h, off]), 0, axis=1, stride=1, stride_axis=0)
                z_ref[h * N_OFF + off:h * N_OFF + off + 1, :] = jnp.sum(lined_up, axis=0, keepdims=True)
        acc = jnp.zeros((64, HD), F32)
        for j in range(WIN_R):
            at_zero = pltpu.roll(z_ref[...], (KEYS_B - (j * GRID_W + lane0)) % KEYS_B, axis=1)[:, :HD]
            acc = acc + exact(place_ref[j], at_zero)
        lane = lax.broadcasted_iota(jnp.int32, (64, HD), 1)
        o_ref[...] = jnp.where(lane < 2 * WIN_C - 1, acc, 0.0)

    return pl.pallas_call(
        body, name="bias_reduce", out_shape=jax.ShapeDtypeStruct((64, HD), F32),
        scratch_shapes=[pltpu.VMEM((4 * N_OFF, KEYS_B), F32)],
        compiler_params=pltpu.CompilerParams(vmem_limit_bytes=VMEM_LIMIT),
    )(dbias_tab, jnp.asarray(flip), jnp.asarray(place))


def _rows_to_tab(rows):
    t = rows.reshape(4, N_OFF, WIN_R, GRID_W, GRID_W)
    return t.transpose(0, 1, 3, 2, 4).reshape(4, N_OFF, GRID_W, KEYS_B)


def _row_window(r):
    r0 = jnp.clip(r - WIN_R // 2, 0, ROWS - WIN_R)
    off = r0 + (WIN_R - 1) - r
    return pl.multiple_of(r * GRID_W, GRID_W), pl.multiple_of(r0 * GRID_W, GRID_W), off


def _attn_b_fwd(qn, kn, vb, bias_tab):
    def body(q_ref, k_ref, v_ref, b_ref, o_ref, lse_ref):
        def row(r, carry):
            qs, ks, off = _row_window(r)
            q = q_ref[pl.ds(qs, GRID_W), :]
            s = lax.dot_general(q, k_ref[pl.ds(ks, KEYS_B), :], NT, preferred_element_type=F32) * SCALE
            s = s + b_ref[off]
            m = jnp.max(s, axis=-1, keepdims=True)
            p = jnp.exp(s - m)
            l = jnp.sum(p, axis=-1, keepdims=True)
            o = lax.dot_general(p.astype(BF16), v_ref[pl.ds(ks, KEYS_B), :], NN, preferred_element_type=F32)
            o_ref[pl.ds(qs, GRID_W), :] = (o / l).astype(BF16)
            lse_ref[pl.ds(qs, GRID_W), :] = m + jnp.log(l)
            return carry

        lax.fori_loop(0, ROWS, row, 0, unroll=8)

    full = pl.BlockSpec((S, HD), lambda h: (0, h))
    return pl.pallas_call(
        body, name="attn_b_fwd", grid=(4,),
        in_specs=[full, full, full, pl.BlockSpec((None, N_OFF, GRID_W, KEYS_B), lambda h: (h, 0, 0, 0))],
        out_specs=[pl.BlockSpec((S, HD), lambda h: (0, h)), pl.BlockSpec((None, S, 1), lambda h: (h, 0, 0))],
        out_shape=[jax.ShapeDtypeStruct((S, D_BR), BF16), jax.ShapeDtypeStruct((4, S, 1), F32)],
        compiler_params=_params(("parallel",)),
    )(qn, kn, vb, bias_tab)


def _attn_b_bwd(qn, kn, vb, bias_tab, ob, dob, lse):
    def body(q_ref, k_ref, v_ref, b_ref, o_ref, do_ref, lse_ref, dq_ref, dk_out, dv_out, db_ref, dk_ref, dv_ref):
        dk_ref[...] = jnp.zeros((S, HD), F32)
        dv_ref[...] = jnp.zeros((S, HD), F32)
        db_ref[...] = jnp.zeros((N_OFF, GRID_W, KEYS_B), F32)

        def row(r, carry):
            qs, ks, off = _row_window(r)
            rows = pl.ds(qs, GRID_W)
            keys = pl.ds(ks, KEYS_B)
            q = q_ref[rows, :]
            kw = k_ref[keys, :]
            s = lax.dot_general(q, kw, NT, preferred_element_type=F32) * SCALE + b_ref[off]
            p = jnp.exp(s - lse_ref[rows, :])
            do = do_ref[rows, :]
            dobf = do.astype(BF16)
            dsum = jnp.sum(do * o_ref[rows, :].astype(F32), axis=-1, keepdims=True)
            dp = lax.dot_general(dobf, v_ref[keys, :], NT, preferred_element_type=F32)
            ds = p * (dp - dsum)
            db_ref[off] += ds
            dsb = (ds * SCALE).astype(BF16)
            dq_ref[rows, :] = lax.dot_general(dsb, kw, NN, preferred_element_type=F32).astype(BF16)
            dk_ref[keys, :] += lax.dot_general(dsb, q, TN, preferred_element_type=F32)
            dv_ref[keys, :] += lax.dot_general(p.astype(BF16), dobf, TN, preferred_element_type=F32)
            return carry

        lax.fori_loop(0, ROWS, row, 0, unroll=8)
        dk_out[...] = dk_ref[...].astype(BF16)
        dv_out[...] = dv_ref[...].astype(BF16)

    full = pl.BlockSpec((S, HD), lambda h: (0, h))
    slot = pl.BlockSpec((S, HD), lambda h: (0, h))
    tab = pl.BlockSpec((None, N_OFF, GRID_W, KEYS_B), lambda h: (h, 0, 0, 0))
    shape = jax.ShapeDtypeStruct((S, D_BR), BF16)
    return pl.pallas_call(
        body, name="attn_b_bwd", grid=(4,),
        in_specs=[full, full, full, tab, slot, slot, pl.BlockSpec((None, S, 1), lambda h: (h, 0, 0))],
        out_specs=[slot, slot, slot, tab],
        out_shape=[shape, shape, shape, jax.ShapeDtypeStruct((4, N_OFF, GRID_W, KEYS_B), F32)],
        scratch_shapes=[pltpu.VMEM((S, HD), F32)] * 2,
        compiler_params=_params(("arbitrary",)),
    )(qn, kn, vb, bias_tab, ob, dob, lse)


def _epi_relu_sq(acc, ex, outs):
    u = jnp.maximum(acc, 0.0)
    outs[0][...] = u.astype(BF16)
    outs[1][...] = (u * u).astype(BF16)


def _epi_relu_sq_bwd(acc, ex, outs):
    outs[0][...] = (acc * (2.0 * ex[0][...].astype(F32))).astype(BF16)


def _epi_loss_head(acc, ex, outs):
    e = acc + ex[0][...] - ex[1][...]
    dy = e * (1.0 / D)
    outs[0][...] = dy
    outs[1][...] = dy.astype(BF16)
    part = (0.5 / D) * jnp.sum(jnp.sum(e * e, axis=-1, keepdims=True), axis=0, keepdims=True)
    first = (pl.program_id(0) == 0) & (pl.program_id(1) == 0)

    @pl.when(first)
    def _():
        outs[2][...] = part

    @pl.when(jnp.logical_not(first))
    def _():
        outs[2][...] += part


def _local_step(x, target, norm_mix, b_gate, gains, rpb_pad, norm_ffn,
                w_in, w_pa, w_pb, w_out, w_up, w_down, weight_grads, riders=lambda name: None):
    def ridden(name, *args, **kwargs):
        ride = riders(name)
        if ride is None:
            return _mm_nt(*args, name=name, **kwargs)
        out, rode = _mm_nt(*args, name=name, rider=ride[0], **kwargs)
        ride[1](rode)
        return out

    cos2, sin2 = _rope_tables()
    expand, keep, sel = _bias_constants()
    w_out3 = w_out[None]

    xn, rstd1 = _rms_fwd(x, norm_mix, name="rms_mix")
    proj = _mm_nn(xn, w_in, tm=1024, tn=1280, name="proj")
    qkv_a, qkv_b = _qk_prep(proj, gains, cos2, sin2)
    fwd_a = [_attn_a_fwd(*qkv_a[gi], gi) for gi in range(3)]
    oa, lse_a = _combine_a([o for o, _ in fwd_a], [l for _, l in fwd_a])
    bias_tab = _rows_to_tab(_bias_expand(rpb_pad, expand, keep, sel))
    ob, lse_b = _attn_b_fwd(*qkv_b, bias_tab)
    mixed, ya, yb = _mix_fwd(oa, ob, w_pa, w_pb, proj, b_gate)
    h1 = _mm_nn(mixed, w_out3, tm=1024, tn=1024, name="out_proj", epi=_epi_residual, extra=(x,))
    hn, rstd2 = _rms_fwd(h1, norm_ffn, name="rms_ffn")
    u, usq = _mm_nn(hn, w_up, tm=1024, tn=1024, name="ffn_up", epi=_epi_relu_sq,
                    out_dtypes=(BF16, BF16))
    dy, dyb, loss = _mm_nn(usq, w_down[0], tm=512, tn=512, name="ffn_down_0", epi=_epi_loss_head,
                           extra=(h1, target), out_dtypes=(F32, BF16), total=True, width=D)
    dy, dyb, loss_1 = _mm_nn(usq, w_down[1], tm=512, tn=512, name="ffn_down_1", epi=_epi_loss_head,
                             extra=(h1, target), out_dtypes=(F32, BF16), total=True, width=D,
                             col0=D // 2, into=(dy, dyb))
    loss = loss + loss_1

    sent = weight_grads("w_down", {5: (usq, dyb)})
    du = _mm_nt(dyb, w_down[0], more_b=(w_down[1],), tm=1024, tn=1024, name="ffn_down_bwd", out_dtype=BF16,
                epi=_epi_relu_sq_bwd, extra=(u,), after=sent)
    sent = weight_grads("w_up", {4: (hn, du)})
    dhn = ridden("ffn_up_bwd", du, w_up, tm=512, tn=1024, after=sent, single_w=True)
    dh1, dh1b, g_norm_ffn = _rms_bwd(dhn, h1, rstd2, norm_ffn, dy, name="rms_ffn_bwd", bf16_copy=True)

    dya, dyb2, dproj, g_b = _mix_bwd(dh1b, w_out, proj, b_gate, ya, yb)
    sent = weight_grads("w_mix", {3: (mixed, dh1b), 1: (oa, dya), 2: (ob, dyb2)})
    dob = _mm_nt(dyb2, w_pb, tm=1024, tn=D_BR, name="proj_b_bwd", after=sent)
    prep = _proj_a_bwd(dya, w_pa, oa, lse_a)
    grads_a = [_attn_a_bwd(*qkv_a[gi], *prep[gi], gi) for gi in range(3)]
    dqb, dkb, dvb, dbias = _attn_b_bwd(*qkv_b, bias_tab, ob, dob, lse_b)
    g_rpb = _bias_reduce(dbias)
    dproj, g_gains = _qk_prep_bwd(dproj, proj, gains, cos2, sin2, grads_a, (dqb, dkb, dvb))
    sent = weight_grads("w_in", {0: (xn, dproj)})
    dxn = ridden("proj_bwd", dproj, w_in, tm=256, tn=1024, after=sent, single_w=True)
    grad_x, g_norm_mix = _rms_bwd(dxn, x, rstd1, norm_mix, dh1, name="rms_mix_bwd", bf16_copy=False)

    small = (g_norm_mix, g_b, g_gains, g_rpb, g_norm_ffn)
    return loss, grad_x, small


def _cast_bf16(w, *, part=0, parts=1, after=(), tr=256):
    rows, cols = w.shape[0], w.shape[1] // parts
    tr = min(tr, rows)

    def body(w_ref, *rest):
        rest[-1][...] = w_ref[...].astype(BF16)

    return pl.pallas_call(
        body, name=f"cast_{rows}x{cols}_{part}", grid=(rows // tr,),
        in_specs=[pl.BlockSpec((tr, cols), lambda i: (i, part))] + [pl.BlockSpec(memory_space=pl.ANY)] * len(after),
        out_specs=pl.BlockSpec((tr, cols), lambda i: (i, 0)),
        out_shape=jax.ShapeDtypeStruct((rows, cols), BF16), compiler_params=_params(("parallel",)),
    )(w, *after)


def _me_and_peers():
    x, y, c = lax.axis_index("x"), lax.axis_index("y"), lax.axis_index("c")
    me = 4 * x + 2 * y + c
    peers = []
    for k in range(1, N_DEV):
        px = 1 - x if k & 4 else x
        py = 1 - y if k & 2 else y
        pc = 1 - c if k & 1 else c
        peers.append(((px, py, pc), 4 * px + 2 * py + pc))
    return me, peers


def _gather_on_sequencer(shards, name):
    n = len(shards)
    hbm = pltpu.MemorySpace.HBM
    ins = [jax.new_ref(s, memory_space=hbm) for s in shards]
    outs = [jax.empty_ref(jax.ShapeDtypeStruct((N_DEV,) + s.shape, s.dtype), memory_space=hbm) for s in shards]

    @_sequencer(name, ((n, N_DEV - 1), (n, N_DEV - 1), (n,)), 0)
    def launch(send, recv, lsem):
        x, y, c = lax.axis_index("x"), lax.axis_index("y"), lax.axis_index("c")
        me, sibling = (x, y, c), (x, y, 1 - c)
        chips = [(1 - x, y), (x, 1 - y), (1 - x, 1 - y)]
        _handshake([sibling] + [(*chip, c) for chip in chips])

        def copy(w, k, block, to, src=None):
            px, py, pc = block
            dst = outs[w].at[4 * px + 2 * py + pc]
            return pltpu.make_async_remote_copy(dst if src is None else src, dst, send.at[w, k], recv.at[w, k],
                                                device_id=to, device_id_type=MESH)

        local = [pltpu.make_async_copy(ins[w], outs[w].at[4 * x + 2 * y + c], lsem.at[w]) for w in range(n)]
        for cp in local:
            cp.start()
        first = []
        for w in range(n):
            first += [copy(w, 1 + j, me, (*chip, c), src=ins[w]) for j, chip in enumerate(chips)]
            first.append(copy(w, 0, me, sibling, src=ins[w]))
        for cp in first:
            cp.start()
        passed = []
        for w in range(n):
            for j, chip in enumerate(chips):
                copy(w, 1 + j, (*chip, c), me).wait_recv()
                cp = copy(w, 4 + j, (*chip, c), sibling)
                cp.start()
                passed.append(cp)
        for w in range(n):
            copy(w, 0, sibling, me).wait_recv()
            for j, chip in enumerate(chips):
                copy(w, 4 + j, (*chip, 1 - c), me).wait_recv()
        for cp in first + passed:
            cp.wait_send()
        for cp in local:
            cp.wait()

    launch()
    return [o[...] for o in outs]


N_CHIP = 4


def _sequencer(name, n_sems, collective_id):
    return functools.partial(
        pl.kernel, mesh=plsc.ScalarSubcoreMesh(axis_name="seq", num_cores=1), name=name,
        scratch_types=tuple(pltpu.SemaphoreType.DMA(s) for s in n_sems),
        compiler_params=pltpu.CompilerParams(collective_id=collective_id))


def _handshake(peers):
    barrier = pltpu.get_barrier_semaphore()
    for peer in peers:
        pl.semaphore_signal(barrier, inc=1, device_id=peer, device_id_type=MESH)
    pl.semaphore_wait(barrier, len(peers))


def _chip_exchange_on_sequencer(parts, name):
    n = len(parts)
    hbm = pltpu.MemorySpace.HBM
    ins = [jax.new_ref(p, memory_space=hbm) for p in parts]
    outs = [jax.empty_ref(jax.ShapeDtypeStruct(p.shape, p.dtype), memory_space=hbm) for p in parts]

    @_sequencer(name, ((n, 3), (n, 3), (n,)), 2)
    def launch(send, recv, lsem):
        x, y, c = lax.axis_index("x"), lax.axis_index("y"), lax.axis_index("c")
        mine = 2 * x + y
        chips = [(1 - x, y), (x, 1 - y), (1 - x, 1 - y)]
        _handshake([(*chip, c) for chip in chips])
        local = [pltpu.make_async_copy(ins[w].at[mine], outs[w].at[mine], lsem.at[w]) for w in range(n)]
        for cp in local:
            cp.start()
        sends = []
        for w in range(n):
            for j, (px, py) in enumerate(chips):
                cp = pltpu.make_async_remote_copy(ins[w].at[2 * px + py], outs[w].at[mine],
                                                  send.at[w, j], recv.at[w, j],
                                                  device_id=(px, py, c), device_id_type=MESH)
                cp.start()
                sends.append(cp)
        for w in range(n):
            for j, (px, py) in enumerate(chips):
                pltpu.make_async_remote_copy(ins[w].at[mine], outs[w].at[2 * px + py],
                                             send.at[w, j], recv.at[w, j],
                                             device_id=(px, py, c), device_id_type=MESH).wait_recv()
        for cp in sends:
            cp.wait_send()
        for cp in local:
            cp.wait()

    launch()
    return [o[...] for o in outs]


GRAD_TILES = (dict(blocks_on="cols", tm=512, tn=1280), dict(blocks_on="cols", tm=512, tn=256),
              dict(blocks_on="cols", tm=512, tn=256), dict(blocks_on="rows", tm=256, tn=2048),
              dict(blocks_on="cols", tm=1024, tn=1024), dict(blocks_on="rows", tm=1024, tn=1024))


def _mm_tn_pair(a, b, *, blocks_on, tm, tn, name):
    t_len, m = a.shape
    n = b.shape[1]
    if blocks_on == "rows":
        rows, cols, inner = m // N_DEV, n, n // tn
        assert tm == rows
        a_spec = pl.BlockSpec((t_len, tm), lambda p, t, blk: (0, blk[p]))
        b_spec = pl.BlockSpec((t_len, tn), lambda p, t, blk: (0, t))
        out_spec = pl.BlockSpec((None, tm, tn), lambda p, t, blk: (
            jnp.maximum(p - N_CHIP, 0), 0, jnp.where(p < N_CHIP, 0, t)))
    else:
        rows, cols, inner = m, n // N_DEV, m // tm
        assert tn == cols
        a_spec = pl.BlockSpec((t_len, tm), lambda p, t, blk: (0, t))
        b_spec = pl.BlockSpec((t_len, tn), lambda p, t, blk: (0, blk[p]))
        out_spec = pl.BlockSpec((None, tm, tn), lambda p, t, blk: (
            jnp.maximum(p - N_CHIP, 0), jnp.where(p < N_CHIP, 0, t), 0))

    def body(blk_ref, a_ref, b_ref, o_ref, land, stage, send_sem, recv_sem):
        del blk_ref
        p, t = pl.program_id(0), pl.program_id(1)
        step = p * inner + t
        x, y, c = lax.axis_index("x"), lax.axis_index("y"), lax.axis_index("c")
        tile = _dot(a_ref[...], b_ref[...], TN)

        def to_sibling(slot, chip, piece):
            return pltpu.make_async_remote_copy(stage.at[slot], land.at[chip, piece], send_sem.at[slot],
                                                recv_sem.at[chip, piece],
                                                device_id=(x, y, 1 - c), device_id_type=MESH)

        @pl.when(p < N_CHIP)
        def _():
            slot = step % 2

            @pl.when(step >= 2)
            def _():
                to_sibling(slot, 0, 0).wait_send()

            stage[slot] = tile.astype(BF16)
            to_sibling(slot, p, t).start()

        @pl.when(step == N_CHIP * inner)
        def _():
            for slot in range(min(2, N_CHIP * inner)):
                to_sibling(slot, 0, 0).wait_send()

        @pl.when(p >= N_CHIP)
        def _():
            chip = p - N_CHIP
            to_sibling(0, chip, t).wait_recv()
            o_ref[...] = (tile + land[chip, t].astype(F32)).astype(BF16)

    c = lax.axis_index("c")
    order = jnp.stack([2 * ch + 1 - c for ch in range(N_CHIP)] + [2 * ch + c for ch in range(N_CHIP)])
    return pl.pallas_call(
        body, name=name,
        grid_spec=pltpu.PrefetchScalarGridSpec(
            num_scalar_prefetch=1, grid=(N_DEV, inner), in_specs=[a_spec, b_spec], out_specs=out_spec,
            scratch_shapes=[pltpu.VMEM((N_CHIP, inner, tm, tn), BF16), pltpu.VMEM((2, tm, tn), BF16),
                            pltpu.SemaphoreType.DMA((2,)), pltpu.SemaphoreType.DMA((N_CHIP, inner))]),
        out_shape=jax.ShapeDtypeStruct((N_CHIP, rows, cols), BF16),
        compiler_params=_params(("arbitrary", "arbitrary")),
    )(order.astype(jnp.int32), a, b)


def _adamw_math(g, w, m, v):
    m2 = B1 * m + (1.0 - B1) * g
    v2 = B2 * v + (1.0 - B2) * (g * g)
    delta = -LR * ((m2 / BC1) / (jnp.sqrt(v2 / BC2) + AEPS) + WD * w)
    return delta, m2, v2


def _adamw_block(ins, outs):
    p_ref, w_ref, m_ref, v_ref = ins
    g = p_ref[0].astype(F32)
    for b in range(1, N_CHIP):
        g = g + p_ref[b].astype(F32)
    delta, m2, v2 = _adamw_math(g, w_ref[...], m_ref[...], v_ref[...])
    for ref, val in zip(outs, (g, delta, m2, v2)):
        ref[...] = val


class _Rider(NamedTuple):
    inputs: tuple
    in_specs: list
    out_shape: list
    out_specs: list
    body: Callable


def _adamw_rider(parts, w, m, v):
    rows, cols = w.shape

    def rider(steps, step_of):
        rr = rows // steps
        blk = pl.BlockSpec((rr, cols), lambda *ids: (step_of(*ids[:2]), 0))
        chips = pl.BlockSpec((N_CHIP, rr, cols), lambda *ids: (0, step_of(*ids[:2]), 0))
        shape = jax.ShapeDtypeStruct((rows, cols), F32)
        return _Rider((parts, w, m, v), [chips, blk, blk, blk], [shape] * 4, [blk] * 4, _adamw_block)

    return rider


def _adamw(parts, w, m, v, *, name, after=(), tr=256):
    rows, cols = w.shape

    def body(*refs):
        _adamw_block(refs[:4], refs[4 + len(after):])

    spec = pl.BlockSpec((tr, cols), lambda i: (i, 0))
    shape = jax.ShapeDtypeStruct((rows, cols), F32)
    return pl.pallas_call(
        body, name=name, grid=(rows // tr,),
        in_specs=[pl.BlockSpec((N_CHIP, tr, cols), lambda i: (0, i, 0)), spec, spec, spec]
        + [pl.BlockSpec(memory_space=pl.ANY)] * len(after),
        out_specs=[spec] * 4, out_shape=[shape] * 4,
        compiler_params=_params(("parallel",)),
    )(parts, w, m, v, *after)


def _small_exchange(part, after=()):
    rows = part.shape[0]

    def body(p_ref, *rest):
        g_ref, buf, send, recv = rest[len(after):]
        me, peers = _me_and_peers()
        buf[me] = p_ref[...]
        sends = []
        for k, (dev, _) in enumerate(peers):
            cp = pltpu.make_async_remote_copy(p_ref, buf.at[me], send.at[k], recv.at[k],
                                              device_id=dev, device_id_type=MESH)
            cp.start()
            sends.append(cp)
        for k, (dev, idx) in enumerate(peers):
            pltpu.make_async_remote_copy(p_ref, buf.at[idx], send.at[k], recv.at[k],
                                         device_id=dev, device_id_type=MESH).wait_recv()
        for cp in sends:
            cp.wait_send()
        g = buf[0]
        for b in range(1, N_DEV):
            g = g + buf[b]
        g_ref[...] = g

    vm = pl.BlockSpec(memory_space=pltpu.VMEM)
    return pl.pallas_call(
        body, name="small_params_exchange",
        in_specs=[vm] + [pl.BlockSpec(memory_space=pl.ANY)] * len(after),
        out_specs=vm, out_shape=jax.ShapeDtypeStruct((rows, HD), F32),
        scratch_shapes=[pltpu.VMEM((N_DEV, rows, HD), F32),
                        pltpu.SemaphoreType.DMA((N_DEV - 1,)), pltpu.SemaphoreType.DMA((N_DEV - 1,))],
    )(part, *after)


def _small_adamw(g, w, m, v):
    def body(g_ref, w_ref, m_ref, v_ref, *outs):
        g = g_ref[...]
        delta, m2, v2 = _adamw_math(g, w_ref[...], m_ref[...], v_ref[...])
        for k, val in enumerate((g, delta, m2, v2)):
            norm_mix, b_gate, qa, ka, qb, kb, rpb, norm_ffn = outs[8 * k:8 * k + 8]
            for dst, row0, n_rows in ((norm_mix, 0, 16), (b_gate, 16, 32), (norm_ffn, 120, 16)):
                for r in range(n_rows):
                    dst[:, r * HD:(r + 1) * HD] = val[row0 + r:row0 + r + 1, :]
            for i, dst in enumerate((qa, ka, qb, kb)):
                dst[...] = val[48 + i:49 + i, :]
            rpb[...] = val[56:120, :]
        outs[32][...] = g[LOSS_ROW:LOSS_ROW + 1, 0:1]

    vm = pl.BlockSpec(memory_space=pltpu.VMEM)
    kinds = [jax.ShapeDtypeStruct(sh, F32) for sh in
             ((1, D), (1, 2 * D), (1, HD), (1, HD), (1, HD), (1, HD), (64, HD), (1, D))]
    outs = pl.pallas_call(
        body, name="small_params_adamw", in_specs=[vm] * 4, out_specs=[vm] * 33,
        out_shape=kinds * 4 + [jax.ShapeDtypeStruct((1, 1), F32)],
    )(g, w, m, v)
    return [outs[8 * k:8 * k + 8] for k in range(4)], outs[32]


def _pack_small(norm_mix, b_gate, qa, ka, qb, kb, rpb, norm_ffn):
    gains = jnp.concatenate([qa, ka, qb, kb, jnp.zeros((4, HD), F32)], axis=0)
    rpb_pad = jnp.pad(rpb.reshape(4 * (2 * WIN_R - 1), 2 * WIN_C - 1), ((0, 4), (0, HD - (2 * WIN_C - 1))))
    return jnp.concatenate([norm_mix.reshape(16, HD), b_gate.reshape(32, HD), gains, rpb_pad,
                            norm_ffn.reshape(16, HD), jnp.zeros((8, HD), F32)], axis=0)


LOSS_ROW = 136


def _rpb_from_rows(rows):
    return rows[:60, :2 * WIN_C - 1].reshape(1, 4, 2 * WIN_R - 1, 2 * WIN_C - 1)


def kernel(x, norm_mix, w_in, b_gate, q_norm_a, k_norm_a, q_norm_b, k_norm_b, rpb_b, w_proj_a, w_proj_b, w_out, norm_ffn, w_up, w_down, loss_target, m_norm_mix, m_w_in, m_b_gate, m_q_norm_a, m_k_norm_a, m_q_norm_b, m_k_norm_b, m_rpb_b, m_w_proj_a, m_w_proj_b, m_w_out, m_norm_ffn, m_w_up, m_w_down, v_norm_mix, v_w_in, v_b_gate, v_q_norm_a, v_k_norm_a, v_q_norm_b, v_k_norm_b, v_rpb_b, v_w_proj_a, v_w_proj_b, v_w_out, v_norm_ffn, v_w_up, v_w_down):
    big_w = (w_in[0], w_proj_a[0], w_proj_b[0], w_out[0], w_up[0], w_down[0])
    big_m = (m_w_in[0], m_w_proj_a[0], m_w_proj_b[0], m_w_out[0], m_w_up[0], m_w_down[0])
    big_v = (v_w_in[0], v_w_proj_a[0], v_w_proj_b[0], v_w_out[0], v_w_up[0], v_w_down[0])
    names = ("w_in", "w_proj_a", "w_proj_b", "w_out", "w_up", "w_down")

    shards = [_cast_bf16(w) for w in big_w[:5]]
    g_in, = _gather_on_sequencer(shards[0:1], "gather_w_in")
    g_pa, g_pb, g_out, g_up = _gather_on_sequencer(shards[1:5], "gather_w_mix_up")
    small_w = _pack_small(norm_mix, b_gate, q_norm_a, k_norm_a, q_norm_b, k_norm_b, rpb_b, norm_ffn)
    small_m = _pack_small(m_norm_mix, m_b_gate, m_q_norm_a, m_k_norm_a, m_q_norm_b, m_k_norm_b, m_rpb_b, m_norm_ffn)
    small_v = _pack_small(v_norm_mix, v_b_gate, v_q_norm_a, v_k_norm_a, v_q_norm_b, v_k_norm_b, v_rpb_b, v_norm_ffn)
    g_down = [_gather_on_sequencer([_cast_bf16(big_w[5], part=h, parts=2, after=(small_w, small_m, small_v) * h)],
                                   f"gather_w_down_{h}")[0].reshape(1, D_FF, D // 2) for h in range(2)]

    upd = [None] * 6
    in_flight = {}

    def weight_grads(tag, operands):
        sums = {i: _mm_tn_pair(a, b, name=f"grad_{names[i]}", **GRAD_TILES[i]) for i, (a, b) in operands.items()}
        new = list(sums.values())
        in_flight.update(zip(sums, _chip_exchange_on_sequencer(new, f"chip_exchange_{tag}")))
        return new

    def riders(name):
        i = {"proj_bwd": 5}.get(name)
        if i is None:
            return None
        return (_adamw_rider(in_flight.pop(i), big_w[i], big_m[i], big_v[i]),
                functools.partial(upd.__setitem__, i))

    loss, grad_x, small_g = _local_step(
        x[0], loss_target[0], norm_mix, b_gate, small_w[48:56], small_w[56:120], norm_ffn,
        g_in, g_pa, g_pb, g_out.reshape(D, D), g_up, g_down, weight_grads, riders)

    g_norm_mix, g_b, g_gains, g_rpb, g_norm_ffn = small_g
    small_part = jnp.concatenate([g_norm_mix.reshape(16, HD), g_b.reshape(32, HD),
                                  g_gains, g_rpb, g_norm_ffn.reshape(16, HD),
                                  jnp.pad(loss, ((0, 7), (0, HD - 1)))], axis=0)
    last = grad_x
    for i, r in in_flight.items():
        if i == 0:
            small_sum = _small_exchange(small_part, after=[last])
            small, total = _small_adamw(small_sum, small_w, small_m, small_v)
            last = total
        upd[i] = _adamw(r, big_w[i], big_m[i], big_v[i], name=f"adamw_{names[i]}", after=[last])
        last = upd[i][0]
    s_g, s_d, s_m, s_v = ((*k[:6], _rpb_from_rows(k[6]), k[7]) for k in small)
    b_g, b_d, b_m, b_v = ([u[j][None] for u in upd] for j in range(4))

    def order(small, big):
        nm, bg, qa, ka, qb, kb, rpb, nf = small
        w_in_, pa_, pb_, out_, up_, down_ = big
        return (nm, w_in_, bg, qa, ka, qb, kb, rpb, pa_, pb_, out_, nf, up_, down_)

    return (total[0, 0], grad_x[None], *order(s_g, b_g), *order(s_d, b_d), *order(s_m, b_m), *order(s_v, b_v))
```

```python
import functools
from typing import Callable, NamedTuple

import jax
import jax.numpy as jnp
import numpy as np
from jax import lax
from jax.experimental import pallas as pl
from jax.experimental.pallas import tpu as pltpu
from jax.experimental.pallas import tpu_sc as plsc

F32 = jnp.float32
BF16 = jnp.bfloat16

N_DEV = 8
S = 2048
D = 2048
HD = 128
NH = 16
NH_A = 12
QKV = NH * HD
D_IN = 3 * QKV + 2 * D
D_BR = 512
D_FF = 4 * D
GRID_W = 64
ROWS = S // GRID_W
WIN_R = 8
WIN_C = 16
EPS = 1e-6
NEG = -1e30
SCALE = HD ** -0.5
ROPE_THETA = 10000.0
DILATIONS = (1, 4, 16)
HALF_A = 64
QB = 128

LR, B1, B2, AEPS, WD, STEP = 0.001, 0.9, 0.999, 1e-08, 0.01, 10
BC1 = 1.0 - B1 ** STEP
BC2 = 1.0 - B2 ** STEP

VMEM_LIMIT = 56 * 1024 * 1024
MESH = pl.DeviceIdType.MESH

NN = (((1,), (0,)), ((), ()))
NT = (((1,), (1,)), ((), ()))
TN = (((0,), (0,)), ((), ()))


def _params(sem):
    return pltpu.CompilerParams(dimension_semantics=sem, vmem_limit_bytes=VMEM_LIMIT)


def _matmul(a, b, *, product, grid, a_spec, b_spec, epi, out_shape, out_specs, name,
            extra=(), extra_specs=(), after=(), carried=False, rider=None, into=()):
    n_extra = len(extra)
    single = not isinstance(out_shape, (list, tuple))
    out_shape = [out_shape] if single else list(out_shape)
    out_specs = [out_specs] if single else list(out_specs)
    ride = rider(grid[0] * grid[1], lambda j, i: j * grid[1] + i) if rider else None
    r_in = list(ride.inputs) if ride else []
    n_main = len(out_shape)

    def body(a_ref, b_ref, *rest):
        n_in = n_extra + len(after) + len(r_in)
        ins, outs = rest[:n_in], rest[n_in + len(into):]
        epi(product(a_ref, b_ref, ins[:n_extra]), ins[:n_extra], outs[:n_main])
        if ride:
            ride.body(ins[n_extra + len(after):], outs[n_main:])

    res = pl.pallas_call(
        body, name=name, grid=grid,
        in_specs=[a_spec, b_spec, *extra_specs, *[pl.BlockSpec(memory_space=pl.ANY)] * len(after),
                  *(ride.in_specs if ride else []), *[pl.BlockSpec(memory_space=pl.ANY)] * len(into)],
        out_specs=out_specs + (ride.out_specs if ride else []),
        out_shape=out_shape + (ride.out_shape if ride else []),
        input_output_aliases={2 + n_extra + len(after) + len(r_in) + k: k for k in range(len(into))},
        compiler_params=_params(("arbitrary", "arbitrary") if carried else ("parallel", "parallel")),
    )(a, b, *extra, *after, *r_in, *into)
    main = res[0] if single else res[:n_main]
    return (main, res[n_main:]) if ride else main


def _dot(x, y, dims):
    return lax.dot_general(x, y, dims, preferred_element_type=F32)


def _epi_store(acc, ex, outs):
    outs[0][...] = acc.astype(outs[0].dtype)


def _epi_residual(acc, ex, outs):
    outs[0][...] = acc + ex[0][...]


def _mm_nn(a, b3, *, tm, tn, name, out_dtypes=(F32,), epi=_epi_store, extra=(), total=False,
           col0=0, width=None, into=()):
    m, kdim = a.shape
    g, _, ng = b3.shape
    n = g * ng
    c0 = col0 // tn
    if tn <= ng:
        npg = ng // tn
        b_spec = pl.BlockSpec((None, kdim, tn), lambda j, i: (j // npg, 0, j % npg))

        def product(a_ref, b_ref, ex):
            return _dot(a_ref[...], b_ref[...], NN)
    else:
        gb = tn // ng
        b_spec = pl.BlockSpec((gb, kdim, ng), lambda j, i: (j, 0, 0))

        def product(a_ref, b_ref, ex):
            return jnp.concatenate([_dot(a_ref[...], b_ref[q], NN) for q in range(gb)], axis=1)

    tile = pl.BlockSpec((tm, tn), lambda j, i: (i, j + c0))
    shapes = [jax.ShapeDtypeStruct((m, width or n), dt) for dt in out_dtypes]
    specs = [tile] * len(shapes)
    if total:
        shapes.append(jax.ShapeDtypeStruct((1, 1), F32))
        specs.append(pl.BlockSpec((1, 1), lambda j, i: (0, 0)))
    single = len(shapes) == 1
    return _matmul(
        a, b3, product=product, grid=(n // tn, m // tm), epi=epi, name=name, carried=total, into=into,
        a_spec=pl.BlockSpec((tm, kdim), lambda j, i: (i, 0)), b_spec=b_spec,
        extra=extra, extra_specs=[tile] * len(extra),
        out_shape=shapes[0] if single else shapes, out_specs=specs[0] if single else specs)


def _mm_nt(a, b3, *, tm, tn, name, out_dtype=F32, epi=_epi_store, extra=(), after=(), rider=None, more_b=()):
    m, kdim = a.shape
    _, n, _ = b3.shape
    n_b = len(more_b)

    def product(a_ref, b_ref, ex):
        acc, k0 = None, 0
        for ref in (b_ref, *ex[:n_b]):
            for q in range(ref.shape[0]):
                part = _dot(a_ref[:, k0:k0 + ref.shape[2]], ref[q], NT)
                acc = part if acc is None else acc + part
                k0 += ref.shape[2]
        return acc

    def write(acc, ex, outs):
        epi(acc, ex[n_b:], outs)

    def w_spec(w):
        return pl.BlockSpec((w.shape[0], tn, w.shape[2]), lambda j, i: (0, j, 0))

    tile = pl.BlockSpec((tm, tn), lambda j, i: (i, j))
    return _matmul(
        a, b3, product=product, grid=(n // tn, m // tm), epi=write, name=name,
        a_spec=pl.BlockSpec((tm, kdim), lambda j, i: (i, 0)), b_spec=w_spec(b3),
        extra=(*more_b, *extra), extra_specs=[w_spec(w) for w in more_b] + [tile] * len(extra),
        after=after, rider=rider,
        out_shape=jax.ShapeDtypeStruct((m, n), out_dtype), out_specs=tile)


def _mm_tn(a, b, *, tm, tn, name, groups=1, out_dtype=BF16):
    t, m = a.shape
    _, n = b.shape
    ng = n // groups
    if tn <= ng:
        npg = ng // tn
        out_spec = pl.BlockSpec((None, tm, tn), lambda j, i: (j // npg, i, j % npg))
        epi = _epi_store

        def product(a_ref, b_ref, ex):
            return _dot(a_ref[...], b_ref[...], TN)
    else:
        gb = tn // ng
        out_spec = pl.BlockSpec((gb, tm, ng), lambda j, i: (j, i, 0))

        def product(a_ref, b_ref, ex):
            return [_dot(a_ref[...], b_ref[:, q * ng:(q + 1) * ng], TN) for q in range(gb)]

        def epi(parts, ex, outs):
            for q, part in enumerate(parts):
                outs[0][q] = part.astype(out_dtype)

    return _matmul(
        a, b, product=product, grid=(n // tn, m // tm), epi=epi, name=name,
        a_spec=pl.BlockSpec((t, tm), lambda j, i: (0, i)),
        b_spec=pl.BlockSpec((t, tn), lambda j, i: (0, j)),
        out_shape=jax.ShapeDtypeStruct((groups, m, ng), out_dtype), out_specs=out_spec)


def _rms_fwd(x, g, *, name, tr=256):
    def body(x_ref, g_ref, y_ref, r_ref):
        xv = x_ref[...]
        r = lax.rsqrt(jnp.mean(xv * xv, axis=-1, keepdims=True) + EPS)
        y_ref[...] = (xv * r * g_ref[...]).astype(BF16)
        r_ref[...] = r

    row = pl.BlockSpec((tr, D), lambda i: (i, 0))
    return pl.pallas_call(
        body, name=name, grid=(S // tr,),
        in_specs=[row, pl.BlockSpec((1, D), lambda i: (0, 0))],
        out_specs=[row, pl.BlockSpec((tr, 1), lambda i: (i, 0))],
        out_shape=[jax.ShapeDtypeStruct((S, D), BF16), jax.ShapeDtypeStruct((S, 1), F32)],
        compiler_params=_params(("parallel",)),
    )(x, g)


def _rms_bwd(dy, x, rstd, g, resid, *, name, bf16_copy, tr=256):
    def body(dy_ref, x_ref, r_ref, g_ref, res_ref, dx_ref, *rest):
        dg_ref = rest[-1]
        r = r_ref[...]
        xh = x_ref[...] * r
        dyv = dy_ref[...]
        t = dyv * g_ref[...]
        dx = r * (t - xh * jnp.mean(t * xh, axis=-1, keepdims=True)) + res_ref[...]
        dx_ref[...] = dx
        if bf16_copy:
            rest[0][...] = dx.astype(BF16)
        part = jnp.sum(dyv * xh, axis=0, keepdims=True)

        @pl.when(pl.program_id(0) == 0)
        def _():
            dg_ref[...] = part

        @pl.when(pl.program_id(0) > 0)
        def _():
            dg_ref[...] += part

    row = pl.BlockSpec((tr, D), lambda i: (i, 0))
    vec = pl.BlockSpec((1, D), lambda i: (0, 0))
    return pl.pallas_call(
        body, name=name, grid=(S // tr,),
        in_specs=[row, row, pl.BlockSpec((tr, 1), lambda i: (i, 0)), vec, row],
        out_specs=[row] + [row] * bf16_copy + [vec],
        out_shape=[jax.ShapeDtypeStruct((S, D), F32)] + [jax.ShapeDtypeStruct((S, D), BF16)] * bf16_copy
        + [jax.ShapeDtypeStruct((1, D), F32)],
        compiler_params=_params(("arbitrary",)),
    )(dy, x, rstd, g, resid)


def _rope_tables():
    pos = np.arange(S, dtype=np.float32)
    inv = (ROPE_THETA ** (-np.arange(0, HD, 2, dtype=np.float32) / HD)).astype(np.float32)
    ang = pos[:, None] * inv[None, :]
    cos, sin = np.cos(ang), np.sin(ang)
    return (jnp.asarray(np.concatenate([cos, cos], axis=-1), F32),
            jnp.asarray(np.concatenate([-sin, sin], axis=-1), F32))


def _swap_halves(t):
    return pltpu.roll(t, HD // 2, axis=1)


TOK = 256


def _lane_block_spec(d, last=HD):
    return pl.BlockSpec((4, TOK // d, d * last), lambda i: (0, i, 0))


def _to_lane_blocks(dst, head, val, d, scr, dtype):
    w = val.shape[1]
    if d == 1:
        dst[head] = val.astype(dtype)
        return
    scr[...] = val
    for r in range(d):
        dst[head, :, r * w:(r + 1) * w] = scr[pl.ds(r, TOK // d, stride=d), :].astype(dtype)


def _from_lane_blocks(src, head, d, w, scr):
    if d == 1:
        return src[head].astype(F32)
    for r in range(d):
        scr[pl.ds(r, TOK // d, stride=d), :] = src[head, :, r * w:(r + 1) * w].astype(F32)
    return scr[...]


def _qk_prep(proj, gains, cos2, sin2):
    def body(q_ref, k_ref, v_ref, g_ref, c_ref, s_ref, *rest):
        outs, scr = rest[:-1], rest[-1]
        cos, sin = c_ref[...], s_ref[...]
        for which, (src, row_a, row_b) in enumerate(((q_ref, 0, 2), (k_ref, 1, 3), (v_ref, None, None))):
            for h in range(NH):
                y = src[:, h * HD:(h + 1) * HD]
                if row_a is not None:
                    y = y * lax.rsqrt(jnp.mean(y * y, axis=-1, keepdims=True) + EPS)
                    if h < NH_A:
                        y = y * g_ref[row_a:row_a + 1, :]
                        y = y * cos + _swap_halves(y) * sin
                    else:
                        y = y * g_ref[row_b:row_b + 1, :]
                if h < NH_A:
                    gi = h // 4
                    _to_lane_blocks(outs[3 * gi + which], h % 4, y, DILATIONS[gi], scr, BF16)
                else:
                    hb = h - NH_A
                    outs[9 + which][:, hb * HD:(hb + 1) * HD] = y.astype(BF16)

    def blk(c):
        return pl.BlockSpec((TOK, QKV), lambda i: (i, c))
    tab = pl.BlockSpec((TOK, HD), lambda i: (i, 0))
    out_specs, out_shape = [], []
    for d in DILATIONS:
        out_specs += [_lane_block_spec(d)] * 3
        out_shape += [jax.ShapeDtypeStruct((4, S // d, d * HD), BF16)] * 3
    out_specs += [pl.BlockSpec((TOK, D_BR), lambda i: (i, 0))] * 3
    out_shape += [jax.ShapeDtypeStruct((S, D_BR), BF16)] * 3
    outs = pl.pallas_call(
        body, name="qk_prep", grid=(S // TOK,),
        in_specs=[blk(0), blk(1), blk(2), pl.BlockSpec((8, HD), lambda i: (0, 0)), tab, tab],
        out_specs=out_specs, out_shape=out_shape,
        scratch_shapes=[pltpu.VMEM((TOK, HD), F32)],
        compiler_params=_params(("parallel",)),
    )(proj, proj, proj, gains, cos2, sin2)
    return [tuple(outs[3 * gi:3 * gi + 3]) for gi in range(3)], tuple(outs[9:12])


def _qk_prep_bwd(dproj, proj, gains, cos2, sin2, grads_a, grads_b):
    def body(dp_in, q_ref, k_ref, g_ref, c_ref, s_ref, *rest):
        grads, (dp_out, dg_ref, scr) = rest[:12], rest[12:]
        del dp_in
        cos, sin = c_ref[...], s_ref[...]

        def grad_of(which, h):
            if h < NH_A:
                gi = h // 4
                return _from_lane_blocks(grads[3 * gi + which], h % 4, DILATIONS[gi], HD, scr)
            hb = h - NH_A
            return grads[9 + which][:, hb * HD:(hb + 1) * HD].astype(F32)

        dg_rows = []
        for which, (src, base, row_a, row_b) in enumerate(((q_ref, 0, 0, 2), (k_ref, QKV, 1, 3))):
            dg_a = jnp.zeros((1, HD), F32)
            dg_b = jnp.zeros((1, HD), F32)
            for h in range(NH):
                t = src[:, h * HD:(h + 1) * HD]
                dy = grad_of(which, h)
                r = lax.rsqrt(jnp.mean(t * t, axis=-1, keepdims=True) + EPS)
                xh = t * r
                if h < NH_A:
                    dy = dy * cos - _swap_halves(dy) * sin
                    gain = g_ref[row_a:row_a + 1, :]
                    dg_a = dg_a + jnp.sum(dy * xh, axis=0, keepdims=True)
                else:
                    gain = g_ref[row_b:row_b + 1, :]
                    dg_b = dg_b + jnp.sum(dy * xh, axis=0, keepdims=True)
                u = dy * gain
                dx = r * (u - xh * jnp.mean(u * xh, axis=-1, keepdims=True))
                dp_out[:, base + h * HD:base + (h + 1) * HD] = dx.astype(BF16)
            dg_rows += [(row_a, dg_a), (row_b, dg_b)]
        for h in range(NH):
            dp_out[:, 2 * QKV + h * HD:2 * QKV + (h + 1) * HD] = grad_of(2, h).astype(BF16)

        @pl.when(pl.program_id(0) == 0)
        def _():
            dg_ref[...] = jnp.zeros((8, HD), F32)

        for row, val in dg_rows:
            dg_ref[row:row + 1, :] += val

    def blk(c):
        return pl.BlockSpec((TOK, QKV), lambda i: (i, c))
    tab = pl.BlockSpec((TOK, HD), lambda i: (i, 0))
    gain_spec = pl.BlockSpec((8, HD), lambda i: (0, 0))
    grad_specs = [s for d in DILATIONS for s in [_lane_block_spec(d)] * 3]
    grad_specs += [pl.BlockSpec((TOK, D_BR), lambda i: (i, 0))] * 3
    return pl.pallas_call(
        body, name="qk_prep_bwd", grid=(S // TOK,),
        in_specs=[pl.BlockSpec(memory_space=pl.ANY), blk(0), blk(1), gain_spec, tab, tab] + grad_specs,
        out_specs=[pl.BlockSpec((TOK, 3 * QKV), lambda i: (i, 0)), gain_spec],
        out_shape=[jax.ShapeDtypeStruct((S, D_IN), BF16), jax.ShapeDtypeStruct((8, HD), F32)],
        input_output_aliases={0: 0},
        scratch_shapes=[pltpu.VMEM((TOK, HD), F32)],
        compiler_params=_params(("arbitrary",)),
    )(dproj, proj, proj, gains, cos2, sin2, *[g for grp in grads_a for g in grp], *grads_b)


def _mix_fwd(oa, ob, w_pa, w_pb, proj, b_gate, *, tr=256):
    def body(oa_ref, ob_ref, pa_ref, pb_ref, la_ref, lb_ref, ba_ref, bb_ref, mix_ref, ya_ref, yb_ref):
        ya = jnp.concatenate([_dot(oa_ref[...], pa_ref[q], NN) for q in range(N_DEV)], axis=1)
        yb = jnp.concatenate([_dot(ob_ref[...], pb_ref[q], NN) for q in range(N_DEV)], axis=1)
        ga = jax.nn.sigmoid(la_ref[...] + ba_ref[...])
        gb = jax.nn.sigmoid(lb_ref[...] + bb_ref[...])
        mix_ref[...] = (ga * ya + gb * yb).astype(BF16)
        ya_ref[...] = ya.astype(BF16)
        yb_ref[...] = yb.astype(BF16)

    row = pl.BlockSpec((tr, D), lambda i: (i, 0))
    branch = pl.BlockSpec((tr, D_BR), lambda i: (i, 0))
    whole = pl.BlockSpec((N_DEV, D_BR, D // N_DEV), lambda i: (0, 0, 0))
    return pl.pallas_call(
        body, name="mix_fwd", grid=(S // tr,),
        in_specs=[branch, branch, whole, whole,
                  pl.BlockSpec((tr, D), lambda i: (i, 3)), pl.BlockSpec((tr, D), lambda i: (i, 4)),
                  pl.BlockSpec((1, D), lambda i: (0, 0)), pl.BlockSpec((1, D), lambda i: (0, 1))],
        out_specs=[row, row, row], out_shape=[jax.ShapeDtypeStruct((S, D), BF16)] * 3,
        compiler_params=_params(("parallel",)),
    )(oa, ob, w_pa, w_pb, proj, proj, b_gate, b_gate)


def _mix_bwd(dh1b, w_out, proj, b_gate, ya, yb, *, tr=256):
    def body(dh_ref, w_ref, la_ref, lb_ref, b_ref, ya_ref, yb_ref, dya_ref, dyb_ref, dp_ref, db_ref):
        dm = _dot(dh_ref[...], w_ref[...], NT)
        parts = []
        for l_ref, y_ref, dy_ref, lo in ((la_ref, ya_ref, dya_ref, 0), (lb_ref, yb_ref, dyb_ref, D)):
            g = jax.nn.sigmoid(l_ref[...] + b_ref[:, lo:lo + D])
            dy_ref[...] = (dm * g).astype(BF16)
            dl = dm * y_ref[...].astype(F32) * g * (1.0 - g)
            dp_ref[:, lo:lo + D] = dl.astype(BF16)
            parts.append(jnp.sum(dl, axis=0, keepdims=True))
        part = jnp.concatenate(parts, axis=1)

        @pl.when(pl.program_id(0) == 0)
        def _():
            db_ref[...] = part

        @pl.when(pl.program_id(0) > 0)
        def _():
            db_ref[...] += part

    row = pl.BlockSpec((tr, D), lambda i: (i, 0))
    vec = pl.BlockSpec((1, 2 * D), lambda i: (0, 0))
    gate_cols = pl.BlockSpec((pl.Element(tr), pl.Element(2 * D)), lambda i: (i * tr, 3 * QKV))
    return pl.pallas_call(
        body, name="mix_bwd", grid=(S // tr,),
        in_specs=[row, pl.BlockSpec((D, D), lambda i: (0, 0)),
                  pl.BlockSpec((tr, D), lambda i: (i, 3)), pl.BlockSpec((tr, D), lambda i: (i, 4)), vec, row, row],
        out_specs=[row, row, gate_cols, vec],
        out_shape=[jax.ShapeDtypeStruct((S, D), BF16), jax.ShapeDtypeStruct((S, D), BF16),
                   jax.ShapeDtypeStruct((S, D_IN), BF16), jax.ShapeDtypeStruct((1, 2 * D), F32)],
        compiler_params=_params(("arbitrary",)),
    )(dh1b, w_out, proj, proj, b_gate, ya, yb)


def _band_blocks(m_len):
    wk = min(m_len, QB + 2 * QB)
    return [(qb * QB, min(max(qb * QB - QB, 0), m_len - wk), wk) for qb in range(m_len // QB)]


def _band_scores(q, kw, q0, k0, wk):
    s = _dot(q, kw, NT) * SCALE
    qpos = q0 + lax.broadcasted_iota(jnp.int32, (QB, 1), 0)
    kpos = k0 + lax.broadcasted_iota(jnp.int32, (1, wk), 1)
    return jnp.where(jnp.abs(kpos - qpos) <= HALF_A, s, NEG)


def _attn_a_fwd(q, k, v, gi):
    d = DILATIONS[gi]
    m_len = S // d

    def body(q_ref, k_ref, v_ref, o_ref, lse_ref):
        for r in range(d):
            lanes = slice(r * HD, (r + 1) * HD)
            for q0, k0, wk in _band_blocks(m_len):
                s = _band_scores(q_ref[q0:q0 + QB, lanes], k_ref[k0:k0 + wk, lanes], q0, k0, wk)
                m = jnp.max(s, axis=-1, keepdims=True)
                p = jnp.exp(s - m)
                l = jnp.sum(p, axis=-1, keepdims=True)
                o_ref[q0:q0 + QB, lanes] = _dot(p.astype(BF16), v_ref[k0:k0 + wk, lanes], NN) / l
                lse_ref[q0:q0 + QB, r:r + 1] = m + jnp.log(l)

    head = pl.BlockSpec((None, m_len, d * HD), lambda h: (h, 0, 0))
    stat = pl.BlockSpec((None, m_len, d), lambda h: (h, 0, 0))
    return pl.pallas_call(
        body, name=f"attn_a_fwd_{gi}", grid=(4,),
        in_specs=[head, head, head], out_specs=[head, stat],
        out_shape=[jax.ShapeDtypeStruct((4, m_len, d * HD), F32), jax.ShapeDtypeStruct((4, m_len, d), F32)],
        compiler_params=_params(("parallel",)),
    )(q, k, v)


def _combine_a(os, lses):
    def body(o0, o1, o2, l0, l1, l2, oa_ref, lse_ref, scr, scr1):
        for h in range(4):
            o = [_from_lane_blocks(ref, h, d, HD, scr) for ref, d in zip((o0, o1, o2), DILATIONS)]
            a, b, c = (_from_lane_blocks(ref, h, d, 1, scr1) for ref, d in zip((l0, l1, l2), DILATIONS))
            m = jnp.maximum(jnp.maximum(a, b), c)
            wa, wb, wc = jnp.exp(a - m), jnp.exp(b - m), jnp.exp(c - m)
            tot = wa + wb + wc
            oa_ref[:, h * HD:(h + 1) * HD] = ((wa * o[0] + wb * o[1] + wc * o[2]) / tot).astype(BF16)
            lse_ref[h] = m + jnp.log(tot)

    return pl.pallas_call(
        body, name="combine_a", grid=(S // TOK,),
        in_specs=[_lane_block_spec(d) for d in DILATIONS] + [_lane_block_spec(d, 1) for d in DILATIONS],
        out_specs=[pl.BlockSpec((TOK, D_BR), lambda i: (i, 0)), pl.BlockSpec((4, TOK, 1), lambda i: (0, i, 0))],
        out_shape=[jax.ShapeDtypeStruct((S, D_BR), BF16), jax.ShapeDtypeStruct((4, S, 1), F32)],
        scratch_shapes=[pltpu.VMEM((TOK, HD), F32), pltpu.VMEM((TOK, 1), F32)],
        compiler_params=_params(("parallel",)),
    )(*os, *lses)


def _proj_a_bwd(dya, w_pa, oa, lse):
    kg = D // N_DEV

    def body(dy_ref, w_ref, o_ref, l_ref, *rest):
        outs, (scr, scr1) = rest[:9], rest[9:]
        doa = _dot(dy_ref[:, 0:kg], w_ref[0], NT)
        for q in range(1, N_DEV):
            doa = doa + _dot(dy_ref[:, q * kg:(q + 1) * kg], w_ref[q], NT)
        for h in range(4):
            do = doa[:, h * HD:(h + 1) * HD]
            dsum = jnp.sum(do * o_ref[:, h * HD:(h + 1) * HD].astype(F32), axis=-1, keepdims=True)
            for gi, d in enumerate(DILATIONS):
                _to_lane_blocks(outs[3 * gi], h, do, d, scr, BF16)
                _to_lane_blocks(outs[3 * gi + 1], h, l_ref[h], d, scr1, F32)
                _to_lane_blocks(outs[3 * gi + 2], h, dsum, d, scr1, F32)

    row = pl.BlockSpec((TOK, D_BR), lambda i: (i, 0))
    out_specs, out_shape = [], []
    for d in DILATIONS:
        out_specs += [_lane_block_spec(d), _lane_block_spec(d, 1), _lane_block_spec(d, 1)]
        out_shape += [jax.ShapeDtypeStruct((4, S // d, d * HD), BF16)] + [jax.ShapeDtypeStruct((4, S // d, d), F32)] * 2
    outs = pl.pallas_call(
        body, name="proj_a_bwd", grid=(S // TOK,),
        in_specs=[pl.BlockSpec((TOK, D), lambda i: (i, 0)),
                  pl.BlockSpec((N_DEV, D_BR, kg), lambda i: (0, 0, 0)),
                  row, pl.BlockSpec((4, TOK, 1), lambda i: (0, i, 0))],
        out_specs=out_specs, out_shape=out_shape,
        scratch_shapes=[pltpu.VMEM((TOK, HD), F32), pltpu.VMEM((TOK, 1), F32)],
        compiler_params=_params(("parallel",)),
    )(dya, w_pa, oa, lse)
    return [tuple(outs[3 * gi:3 * gi + 3]) for gi in range(3)]


def _attn_a_bwd(q, k, v, do, lse, dsum, gi):
    d = DILATIONS[gi]
    m_len = S // d

    def body(q_ref, k_ref, v_ref, do_ref, lse_ref, dsum_ref, dq_ref, dk_out, dv_out, dk_ref, dv_ref):
        dk_ref[...] = jnp.zeros((m_len, d * HD), F32)
        dv_ref[...] = jnp.zeros((m_len, d * HD), F32)
        for r in range(d):
            lanes = slice(r * HD, (r + 1) * HD)
            for q0, k0, wk in _band_blocks(m_len):
                rows, keys = slice(q0, q0 + QB), slice(k0, k0 + wk)
                qv, kw, vw, dov = q_ref[rows, lanes], k_ref[keys, lanes], v_ref[keys, lanes], do_ref[rows, lanes]
                p = jnp.exp(_band_scores(qv, kw, q0, k0, wk) - lse_ref[rows, r:r + 1])
                ds = (p * (_dot(dov, vw, NT) - dsum_ref[rows, r:r + 1]) * SCALE).astype(BF16)
                dq_ref[rows, lanes] = _dot(ds, kw, NN).astype(BF16)
                dk_ref[keys, lanes] += _dot(ds, qv, TN)
                dv_ref[keys, lanes] += _dot(p.astype(BF16), dov, TN)
        dk_out[...] = dk_ref[...].astype(BF16)
        dv_out[...] = dv_ref[...].astype(BF16)

    head = pl.BlockSpec((None, m_len, d * HD), lambda h: (h, 0, 0))
    stat = pl.BlockSpec((None, m_len, d), lambda h: (h, 0, 0))
    shape = jax.ShapeDtypeStruct((4, m_len, d * HD), BF16)
    return pl.pallas_call(
        body, name=f"attn_a_bwd_{gi}", grid=(4,),
        in_specs=[head, head, head, head, stat, stat], out_specs=[head, head, head],
        out_shape=[shape, shape, shape],
        scratch_shapes=[pltpu.VMEM((m_len, d * HD), F32)] * 2,
        compiler_params=_params(("arbitrary",)),
    )(q, k, v, do, lse, dsum)


KEYS_B = WIN_R * GRID_W
N_OFF = WIN_R


def _bias_constants():
    q = np.arange(GRID_W)[:, None]
    kc = np.arange(GRID_W)[None, :]
    dc = np.clip(kc - q, -(WIN_C - 1), WIN_C - 1) + (WIN_C - 1)
    expand = np.zeros((HD, GRID_W * GRID_W), np.float32)
    expand[dc.reshape(-1), np.arange(GRID_W * GRID_W)] = 1.0
    cs = np.clip(q - WIN_C // 2, 0, GRID_W - WIN_C)
    keep = ((kc >= cs) & (kc < cs + WIN_C)).reshape(1, -1).astype(np.float32)
    sel = np.zeros((64, 4 * N_OFF * WIN_R), np.float32)
    for h in range(4):
        for off in range(N_OFF):
            for j in range(WIN_R):
                sel[h * (2 * WIN_R - 1) + off + j, (h * N_OFF + off) * WIN_R + j] = 1.0
    return jnp.asarray(expand), jnp.asarray(keep), jnp.asarray(sel)


def _bias_expand(rpb_pad, expand, keep, sel):
    def body(r_ref, e_ref, k_ref, s_ref, o_ref):
        t = lax.dot_general(r_ref[...], e_ref[...], NN, precision=lax.Precision.HIGHEST,
                            preferred_element_type=F32)
        rows = lax.dot_general(s_ref[...], t, TN, precision=lax.Precision.HIGHEST,
                               preferred_element_type=F32)
        o_ref[...] = jnp.where(k_ref[...] > 0.5, rows, NEG)

    return pl.pallas_call(
        body, name="bias_expand",
        out_shape=jax.ShapeDtypeStruct((4 * N_OFF * WIN_R, GRID_W * GRID_W), F32),
        compiler_params=pltpu.CompilerParams(vmem_limit_bytes=VMEM_LIMIT),
    )(rpb_pad, expand, keep, sel)


def _bias_reduce(dbias_tab):
    lane0 = GRID_W - WIN_C
    flip = np.zeros((GRID_W, GRID_W), np.float32)
    flip[np.arange(GRID_W), GRID_W - 1 - np.arange(GRID_W)] = 1.0
    place = np.zeros((WIN_R, 64, 4 * N_OFF), np.float32)
    for j in range(WIN_R):
        for h in range(4):
            for off in range(N_OFF):
                place[j, h * (2 * WIN_R - 1) + off + j, h * N_OFF + off] = 1.0

    def exact(x, y):
        return lax.dot_general(x, y, NN, precision=lax.Precision.HIGHEST, preferred_element_type=F32)

    def body(x_ref, flip_ref, place_ref, o_ref, z_ref):
        for h in range(4):
            for off in range(N_OFF):
                lined_up = pltpu.roll(exact(flip_ref[...], x_ref[h, off]), 0, axis=1, stride=1, stride_axis=0)
                z_ref[h * N_OFF + off:h * N_OFF + off + 1, :] = jnp.sum(lined_up, axis=0, keepdims=True)
        acc = jnp.zeros((64, HD), F32)
        for j in range(WIN_R):
            at_zero = pltpu.roll(z_ref[...], (KEYS_B - (j * GRID_W + lane0)) % KEYS_B, axis=1)[:, :HD]
            acc = acc + exact(place_ref[j], at_zero)
        lane = lax.broadcasted_iota(jnp.int32, (64, HD), 1)
        o_ref[...] = jnp.where(lane < 2 * WIN_C - 1, acc, 0.0)

    return pl.pallas_call(
        body, name="bias_reduce", out_shape=jax.ShapeDtypeStruct((64, HD), F32),
        scratch_shapes=[pltpu.VMEM((4 * N_OFF, KEYS_B), F32)],
        compiler_params=pltpu.CompilerParams(vmem_limit_bytes=VMEM_LIMIT),
    )(dbias_tab, jnp.asarray(flip), jnp.asarray(place))


def _rows_to_tab(rows):
    t = rows.reshape(4, N_OFF, WIN_R, GRID_W, GRID_W)
    return t.transpose(0, 1, 3, 2, 4).reshape(4, N_OFF, GRID_W, KEYS_B)


def _row_window(r):
    r0 = jnp.clip(r - WIN_R // 2, 0, ROWS - WIN_R)
    off = r0 + (WIN_R - 1) - r
    return pl.multiple_of(r * GRID_W, GRID_W), pl.multiple_of(r0 * GRID_W, GRID_W), off


def _attn_b_fwd(qn, kn, vb, bias_tab):
    def body(q_ref, k_ref, v_ref, b_ref, o_ref, lse_ref):
        def row(r, carry):
            qs, ks, off = _row_window(r)
            q = q_ref[pl.ds(qs, GRID_W), :]
            s = lax.dot_general(q, k_ref[pl.ds(ks, KEYS_B), :], NT, preferred_element_type=F32) * SCALE
            s = s + b_ref[off]
            m = jnp.max(s, axis=-1, keepdims=True)
            p = jnp.exp(s - m)
            l = jnp.sum(p, axis=-1, keepdims=True)
            o = lax.dot_general(p.astype(BF16), v_ref[pl.ds(ks, KEYS_B), :], NN, preferred_element_type=F32)
            o_ref[pl.ds(qs, GRID_W), :] = (o / l).astype(BF16)
            lse_ref[pl.ds(qs, GRID_W), :] = m + jnp.log(l)
            return carry

        lax.fori_loop(0, ROWS, row, 0, unroll=8)

    full = pl.BlockSpec((S, HD), lambda h: (0, h))
    return pl.pallas_call(
        body, name="attn_b_fwd", grid=(4,),
        in_specs=[full, full, full, pl.BlockSpec((None, N_OFF, GRID_W, KEYS_B), lambda h: (h, 0, 0, 0))],
        out_specs=[pl.BlockSpec((S, HD), lambda h: (0, h)), pl.BlockSpec((None, S, 1), lambda h: (h, 0, 0))],
        out_shape=[jax.ShapeDtypeStruct((S, D_BR), BF16), jax.ShapeDtypeStruct((4, S, 1), F32)],
        compiler_params=_params(("parallel",)),
    )(qn, kn, vb, bias_tab)


def _attn_b_bwd(qn, kn, vb, bias_tab, ob, dob, lse):
    def body(q_ref, k_ref, v_ref, b_ref, o_ref, do_ref, lse_ref, dq_ref, dk_out, dv_out, db_ref, dk_ref, dv_ref):
        dk_ref[...] = jnp.zeros((S, HD), F32)
        dv_ref[...] = jnp.zeros((S, HD), F32)
        db_ref[...] = jnp.zeros((N_OFF, GRID_W, KEYS_B), F32)

        def row(r, carry):
            qs, ks, off = _row_window(r)
            rows = pl.ds(qs, GRID_W)
            keys = pl.ds(ks, KEYS_B)
            q = q_ref[rows, :]
            kw = k_ref[keys, :]
            s = lax.dot_general(q, kw, NT, preferred_element_type=F32) * SCALE + b_ref[off]
            p = jnp.exp(s - lse_ref[rows, :])
            do = do_ref[rows, :]
            dobf = do.astype(BF16)
            dsum = jnp.sum(do * o_ref[rows, :].astype(F32), axis=-1, keepdims=True)
            dp = lax.dot_general(dobf, v_ref[keys, :], NT, preferred_element_type=F32)
            ds = p * (dp - dsum)
            db_ref[off] += ds
            dsb = (ds * SCALE).astype(BF16)
            dq_ref[rows, :] = lax.dot_general(dsb, kw, NN, preferred_element_type=F32).astype(BF16)
            dk_ref[keys, :] += lax.dot_general(dsb, q, TN, preferred_element_type=F32)
            dv_ref[keys, :] += lax.dot_general(p.astype(BF16), dobf, TN, preferred_element_type=F32)
            return carry

        lax.fori_loop(0, ROWS, row, 0, unroll=8)
        dk_out[...] = dk_ref[...].astype(BF16)
        dv_out[...] = dv_ref[...].astype(BF16)

    full = pl.BlockSpec((S, HD), lambda h: (0, h))
    slot = pl.BlockSpec((S, HD), lambda h: (0, h))
    tab = pl.BlockSpec((None, N_OFF, GRID_W, KEYS_B), lambda h: (h, 0, 0, 0))
    shape = jax.ShapeDtypeStruct((S, D_BR), BF16)
    return pl.pallas_call(
        body, name="attn_b_bwd", grid=(4,),
        in_specs=[full, full, full, tab, slot, slot, pl.BlockSpec((None, S, 1), lambda h: (h, 0, 0))],
        out_specs=[slot, slot, slot, tab],
        out_shape=[shape, shape, shape, jax.ShapeDtypeStruct((4, N_OFF, GRID_W, KEYS_B), F32)],
        scratch_shapes=[pltpu.VMEM((S, HD), F32)] * 2,
        compiler_params=_params(("arbitrary",)),
    )(qn, kn, vb, bias_tab, ob, dob, lse)


def _epi_relu_sq(acc, ex, outs):
    u = jnp.maximum(acc, 0.0)
    outs[0][...] = u.astype(BF16)
    outs[1][...] = (u * u).astype(BF16)


def _epi_relu_sq_bwd(acc, ex, outs):
    outs[0][...] = (acc * (2.0 * ex[0][...].astype(F32))).astype(BF16)


def _epi_loss_head(acc, ex, outs):
    e = acc + ex[0][...] - ex[1][...]
    dy = e * (1.0 / D)
    outs[0][...] = dy
    outs[1][...] = dy.astype(BF16)
    part = (0.5 / D) * jnp.sum(jnp.sum(e * e, axis=-1, keepdims=True), axis=0, keepdims=True)
    first = (pl.program_id(0) == 0) & (pl.program_id(1) == 0)

    @pl.when(first)
    def _():
        outs[2][...] = part

    @pl.when(jnp.logical_not(first))
    def _():
        outs[2][...] += part


def _local_step(x, target, norm_mix, b_gate, gains, rpb_pad, norm_ffn,
                w_in, w_pa, w_pb, w_out, w_up, w_down, weight_grads, riders=lambda name: None):
    def ridden(name, *args, **kwargs):
        ride = riders(name)
        if ride is None:
            return _mm_nt(*args, name=name, **kwargs)
        out, rode = _mm_nt(*args, name=name, rider=ride[0], **kwargs)
        ride[1](rode)
        return out

    cos2, sin2 = _rope_tables()
    expand, keep, sel = _bias_constants()
    w_out3 = w_out[None]

    xn, rstd1 = _rms_fwd(x, norm_mix, name="rms_mix")
    proj = _mm_nn(xn, w_in, tm=1024, tn=1280, name="proj")
    qkv_a, qkv_b = _qk_prep(proj, gains, cos2, sin2)
    fwd_a = [_attn_a_fwd(*qkv_a[gi], gi) for gi in range(3)]
    oa, lse_a = _combine_a([o for o, _ in fwd_a], [l for _, l in fwd_a])
    bias_tab = _rows_to_tab(_bias_expand(rpb_pad, expand, keep, sel))
    ob, lse_b = _attn_b_fwd(*qkv_b, bias_tab)
    mixed, ya, yb = _mix_fwd(oa, ob, w_pa, w_pb, proj, b_gate)
    h1 = _mm_nn(mixed, w_out3, tm=1024, tn=1024, name="out_proj", epi=_epi_residual, extra=(x,))
    hn, rstd2 = _rms_fwd(h1, norm_ffn, name="rms_ffn")
    u, usq = _mm_nn(hn, w_up, tm=1024, tn=1024, name="ffn_up", epi=_epi_relu_sq,
                    out_dtypes=(BF16, BF16))
    dy, dyb, loss = _mm_nn(usq, w_down[0], tm=512, tn=512, name="ffn_down_0", epi=_epi_loss_head,
                           extra=(h1, target), out_dtypes=(F32, BF16), total=True, width=D)
    dy, dyb, loss_1 = _mm_nn(usq, w_down[1], tm=512, tn=512, name="ffn_down_1", epi=_epi_loss_head,
                             extra=(h1, target), out_dtypes=(F32, BF16), total=True, width=D,
                             col0=D // 2, into=(dy, dyb))
    loss = loss + loss_1

    sent = weight_grads("w_down", {5: (usq, dyb)})
    du = _mm_nt(dyb, w_down[0], more_b=(w_down[1],), tm=1024, tn=1024, name="ffn_down_bwd", out_dtype=BF16,
                epi=_epi_relu_sq_bwd, extra=(u,), after=sent)
    sent = weight_grads("w_up", {4: (hn, du)})
    dhn = ridden("ffn_up_bwd", du, w_up, tm=512, tn=512, after=sent)
    dh1, dh1b, g_norm_ffn = _rms_bwd(dhn, h1, rstd2, norm_ffn, dy, name="rms_ffn_bwd", bf16_copy=True)

    dya, dyb2, dproj, g_b = _mix_bwd(dh1b, w_out, proj, b_gate, ya, yb)
    sent = weight_grads("w_mix", {3: (mixed, dh1b), 1: (oa, dya), 2: (ob, dyb2)})
    dob = _mm_nt(dyb2, w_pb, tm=1024, tn=D_BR, name="proj_b_bwd", after=sent)
    prep = _proj_a_bwd(dya, w_pa, oa, lse_a)
    grads_a = [_attn_a_bwd(*qkv_a[gi], *prep[gi], gi) for gi in range(3)]
    dqb, dkb, dvb, dbias = _attn_b_bwd(*qkv_b, bias_tab, ob, dob, lse_b)
    g_rpb = _bias_reduce(dbias)
    dproj, g_gains = _qk_prep_bwd(dproj, proj, gains, cos2, sin2, grads_a, (dqb, dkb, dvb))
    sent = weight_grads("w_in", {0: (xn, dproj)})
    dxn = ridden("proj_bwd", dproj, w_in, tm=256, tn=512, after=sent)
    grad_x, g_norm_mix = _rms_bwd(dxn, x, rstd1, norm_mix, dh1, name="rms_mix_bwd", bf16_copy=False)

    small = (g_norm_mix, g_b, g_gains, g_rpb, g_norm_ffn)
    return loss, grad_x, small


def _cast_bf16(w, *, part=0, parts=1, after=(), tr=256):
    rows, cols = w.shape[0], w.shape[1] // parts
    tr = min(tr, rows)

    def body(w_ref, *rest):
        rest[-1][...] = w_ref[...].astype(BF16)

    return pl.pallas_call(
        body, name=f"cast_{rows}x{cols}_{part}", grid=(rows // tr,),
        in_specs=[pl.BlockSpec((tr, cols), lambda i: (i, part))] + [pl.BlockSpec(memory_space=pl.ANY)] * len(after),
        out_specs=pl.BlockSpec((tr, cols), lambda i: (i, 0)),
        out_shape=jax.ShapeDtypeStruct((rows, cols), BF16), compiler_params=_params(("parallel",)),
    )(w, *after)


def _me_and_peers():
    x, y, c = lax.axis_index("x"), lax.axis_index("y"), lax.axis_index("c")
    me = 4 * x + 2 * y + c
    peers = []
    for k in range(1, N_DEV):
        px = 1 - x if k & 4 else x
        py = 1 - y if k & 2 else y
        pc = 1 - c if k & 1 else c
        peers.append(((px, py, pc), 4 * px + 2 * py + pc))
    return me, peers


def _gather_on_sequencer(shards, name):
    n = len(shards)
    hbm = pltpu.MemorySpace.HBM
    ins = [jax.new_ref(s, memory_space=hbm) for s in shards]
    outs = [jax.empty_ref(jax.ShapeDtypeStruct((N_DEV,) + s.shape, s.dtype), memory_space=hbm) for s in shards]
    n_sem = 8

    @_sequencer(name, ((n, n_sem), (n, n_sem), (n,)), 0)
    def launch(send, recv, lsem):
        x, y, c = lax.axis_index("x"), lax.axis_index("y"), lax.axis_index("c")
        me, sibling = (x, y, c), (x, y, 1 - c)
        x_chip, y_chip, diagonal = (1 - x, y, c), (x, 1 - y, c), (1 - x, 1 - y, c)
        _handshake([sibling, x_chip, y_chip])

        def copy(w, k, block, to, src=None, half=None):
            px, py, pc = block
            dst = outs[w].at[4 * px + 2 * py + pc]
            if half is not None:
                rows = shards[w].shape[0] // 2
                dst = dst.at[pl.ds(half * rows, rows)]
            return pltpu.make_async_remote_copy(dst if src is None else src, dst, send.at[w, k], recv.at[w, k],
                                                device_id=to, device_id_type=MESH)

        local = [pltpu.make_async_copy(ins[w], outs[w].at[4 * x + 2 * y + c], lsem.at[w]) for w in range(n)]
        for cp in local:
            cp.start()
        sent = []
        for w in range(n):
            sent += [copy(w, 1, me, x_chip, src=ins[w]), copy(w, 2, me, y_chip, src=ins[w]),
                     copy(w, 0, me, sibling, src=ins[w])]
        for cp in sent:
            cp.start()
        for w in range(n):
            copy(w, 1, x_chip, me).wait_recv()
            sent += [copy(w, 3, x_chip, y_chip, half=0), copy(w, 5, x_chip, sibling)]
            sent[-2].start()
            sent[-1].start()
            copy(w, 2, y_chip, me).wait_recv()
            sent += [copy(w, 4, y_chip, x_chip, half=1), copy(w, 6, y_chip, sibling)]
            sent[-2].start()
            sent[-1].start()
        for w in range(n):
            copy(w, 3, diagonal, me, half=0).wait_recv()
            copy(w, 4, diagonal, me, half=1).wait_recv()
            sent.append(copy(w, 7, diagonal, sibling))
            sent[-1].start()
        for w in range(n):
            copy(w, 0, sibling, me).wait_recv()
            for k, chip in ((5, x_chip), (6, y_chip), (7, diagonal)):
                px, py, _ = chip
                copy(w, k, (px, py, 1 - c), me).wait_recv()
        for cp in sent:
            cp.wait_send()
        for cp in local:
            cp.wait()

    launch()
    return [o[...] for o in outs]


N_CHIP = 4


def _sequencer(name, n_sems, collective_id):
    return functools.partial(
        pl.kernel, mesh=plsc.ScalarSubcoreMesh(axis_name="seq", num_cores=1), name=name,
        scratch_types=tuple(pltpu.SemaphoreType.DMA(s) for s in n_sems),
        compiler_params=pltpu.CompilerParams(collective_id=collective_id))


def _handshake(peers):
    barrier = pltpu.get_barrier_semaphore()
    for peer in peers:
        pl.semaphore_signal(barrier, inc=1, device_id=peer, device_id_type=MESH)
    pl.semaphore_wait(barrier, len(peers))


def _chip_exchange_on_sequencer(parts, name):
    n = len(parts)
    hbm = pltpu.MemorySpace.HBM
    ins = [jax.new_ref(p, memory_space=hbm) for p in parts]
    outs = [jax.empty_ref(jax.ShapeDtypeStruct(p.shape, p.dtype), memory_space=hbm) for p in parts]

    @_sequencer(name, ((n, 3), (n, 3), (n,)), 2)
    def launch(send, recv, lsem):
        x, y, c = lax.axis_index("x"), lax.axis_index("y"), lax.axis_index("c")
        mine = 2 * x + y
        chips = [(1 - x, y), (x, 1 - y), (1 - x, 1 - y)]
        _handshake([(*chip, c) for chip in chips])
        local = [pltpu.make_async_copy(ins[w].at[mine], outs[w].at[mine], lsem.at[w]) for w in range(n)]
        for cp in local:
            cp.start()
        sends = []
        for w in range(n):
            for j, (px, py) in enumerate(chips):
                cp = pltpu.make_async_remote_copy(ins[w].at[2 * px + py], outs[w].at[mine],
                                                  send.at[w, j], recv.at[w, j],
                                                  device_id=(px, py, c), device_id_type=MESH)
                cp.start()
                sends.append(cp)
        for w in range(n):
            for j, (px, py) in enumerate(chips):
                pltpu.make_async_remote_copy(ins[w].at[mine], outs[w].at[2 * px + py],
                                             send.at[w, j], recv.at[w, j],
                                             device_id=(px, py, c), device_id_type=MESH).wait_recv()
        for cp in sends:
            cp.wait_send()
        for cp in local:
            cp.wait()

    launch()
    return [o[...] for o in outs]


GRAD_TILES = (dict(blocks_on="cols", tm=512, tn=1280), dict(blocks_on="cols", tm=512, tn=256),
              dict(blocks_on="cols", tm=512, tn=256), dict(blocks_on="rows", tm=256, tn=2048),
              dict(blocks_on="cols", tm=1024, tn=1024), dict(blocks_on="rows", tm=1024, tn=1024))


def _mm_tn_pair(a, b, *, blocks_on, tm, tn, name):
    t_len, m = a.shape
    n = b.shape[1]
    if blocks_on == "rows":
        rows, cols, inner = m // N_DEV, n, n // tn
        assert tm == rows
        a_spec = pl.BlockSpec((t_len, tm), lambda p, t, blk: (0, blk[p]))
        b_spec = pl.BlockSpec((t_len, tn), lambda p, t, blk: (0, t))
        out_spec = pl.BlockSpec((None, tm, tn), lambda p, t, blk: (
            jnp.maximum(p - N_CHIP, 0), 0, jnp.where(p < N_CHIP, 0, t)))
    else:
        rows, cols, inner = m, n // N_DEV, m // tm
        assert tn == cols
        a_spec = pl.BlockSpec((t_len, tm), lambda p, t, blk: (0, t))
        b_spec = pl.BlockSpec((t_len, tn), lambda p, t, blk: (0, blk[p]))
        out_spec = pl.BlockSpec((None, tm, tn), lambda p, t, blk: (
            jnp.maximum(p - N_CHIP, 0), jnp.where(p < N_CHIP, 0, t), 0))

    def body(blk_ref, a_ref, b_ref, o_ref, land, stage, send_sem, recv_sem):
        del blk_ref
        p, t = pl.program_id(0), pl.program_id(1)
        step = p * inner + t
        x, y, c = lax.axis_index("x"), lax.axis_index("y"), lax.axis_index("c")
        tile = _dot(a_ref[...], b_ref[...], TN)

        def to_sibling(slot, chip, piece):
            return pltpu.make_async_remote_copy(stage.at[slot], land.at[chip, piece], send_sem.at[slot],
                                                recv_sem.at[chip, piece],
                                                device_id=(x, y, 1 - c), device_id_type=MESH)

        @pl.when(p < N_CHIP)
        def _():
            slot = step % 2

            @pl.when(step >= 2)
            def _():
                to_sibling(slot, 0, 0).wait_send()

            stage[slot] = tile.astype(BF16)
            to_sibling(slot, p, t).start()

        @pl.when(step == N_CHIP * inner)
        def _():
            for slot in range(min(2, N_CHIP * inner)):
                to_sibling(slot, 0, 0).wait_send()

        @pl.when(p >= N_CHIP)
        def _():
            chip = p - N_CHIP
            to_sibling(0, chip, t).wait_recv()
            o_ref[...] = (tile + land[chip, t].astype(F32)).astype(BF16)

    c = lax.axis_index("c")
    order = jnp.stack([2 * ch + 1 - c for ch in range(N_CHIP)] + [2 * ch + c for ch in range(N_CHIP)])
    return pl.pallas_call(
        body, name=name,
        grid_spec=pltpu.PrefetchScalarGridSpec(
            num_scalar_prefetch=1, grid=(N_DEV, inner), in_specs=[a_spec, b_spec], out_specs=out_spec,
            scratch_shapes=[pltpu.VMEM((N_CHIP, inner, tm, tn), BF16), pltpu.VMEM((2, tm, tn), BF16),
                            pltpu.SemaphoreType.DMA((2,)), pltpu.SemaphoreType.DMA((N_CHIP, inner))]),
        out_shape=jax.ShapeDtypeStruct((N_CHIP, rows, cols), BF16),
        compiler_params=_params(("arbitrary", "arbitrary")),
    )(order.astype(jnp.int32), a, b)


def _adamw_math(g, w, m, v):
    m2 = B1 * m + (1.0 - B1) * g
    v2 = B2 * v + (1.0 - B2) * (g * g)
    delta = -LR * ((m2 / BC1) / (jnp.sqrt(v2 / BC2) + AEPS) + WD * w)
    return delta, m2, v2


def _adamw_block(ins, outs):
    p_ref, w_ref, m_ref, v_ref = ins
    g = p_ref[0].astype(F32)
    for b in range(1, N_CHIP):
        g = g + p_ref[b].astype(F32)
    delta, m2, v2 = _adamw_math(g, w_ref[...], m_ref[...], v_ref[...])
    for ref, val in zip(outs, (g, delta, m2, v2)):
        ref[...] = val


class _Rider(NamedTuple):
    inputs: tuple
    in_specs: list
    out_shape: list
    out_specs: list
    body: Callable


def _adamw_rider(parts, w, m, v):
    rows, cols = w.shape

    def rider(steps, step_of):
        rr = rows // steps
        blk = pl.BlockSpec((rr, cols), lambda *ids: (step_of(*ids[:2]), 0))
        chips = pl.BlockSpec((N_CHIP, rr, cols), lambda *ids: (0, step_of(*ids[:2]), 0))
        shape = jax.ShapeDtypeStruct((rows, cols), F32)
        return _Rider((parts, w, m, v), [chips, blk, blk, blk], [shape] * 4, [blk] * 4, _adamw_block)

    return rider


def _adamw(parts, w, m, v, *, name, after=(), tr=256):
    rows, cols = w.shape

    def body(*refs):
        _adamw_block(refs[:4], refs[4 + len(after):])

    spec = pl.BlockSpec((tr, cols), lambda i: (i, 0))
    shape = jax.ShapeDtypeStruct((rows, cols), F32)
    return pl.pallas_call(
        body, name=name, grid=(rows // tr,),
        in_specs=[pl.BlockSpec((N_CHIP, tr, cols), lambda i: (0, i, 0)), spec, spec, spec]
        + [pl.BlockSpec(memory_space=pl.ANY)] * len(after),
        out_specs=[spec] * 4, out_shape=[shape] * 4,
        compiler_params=_params(("parallel",)),
    )(parts, w, m, v, *after)


def _small_exchange(part, after=()):
    rows = part.shape[0]

    def body(p_ref, *rest):
        g_ref, buf, send, recv = rest[len(after):]
        me, peers = _me_and_peers()
        buf[me] = p_ref[...]
        sends = []
        for k, (dev, _) in enumerate(peers):
            cp = pltpu.make_async_remote_copy(p_ref, buf.at[me], send.at[k], recv.at[k],
                                              device_id=dev, device_id_type=MESH)
            cp.start()
            sends.append(cp)
        for k, (dev, idx) in enumerate(peers):
            pltpu.make_async_remote_copy(p_ref, buf.at[idx], send.at[k], recv.at[k],
                                         device_id=dev, device_id_type=MESH).wait_recv()
        for cp in sends:
            cp.wait_send()
        g = buf[0]
        for b in range(1, N_DEV):
            g = g + buf[b]
        g_ref[...] = g

    vm = pl.BlockSpec(memory_space=pltpu.VMEM)
    return pl.pallas_call(
        body, name="small_params_exchange",
        in_specs=[vm] + [pl.BlockSpec(memory_space=pl.ANY)] * len(after),
        out_specs=vm, out_shape=jax.ShapeDtypeStruct((rows, HD), F32),
        scratch_shapes=[pltpu.VMEM((N_DEV, rows, HD), F32),
                        pltpu.SemaphoreType.DMA((N_DEV - 1,)), pltpu.SemaphoreType.DMA((N_DEV - 1,))],
    )(part, *after)


def _small_adamw(g, w, m, v):
    def body(g_ref, w_ref, m_ref, v_ref, *outs):
        g = g_ref[...]
        delta, m2, v2 = _adamw_math(g, w_ref[...], m_ref[...], v_ref[...])
        for k, val in enumerate((g, delta, m2, v2)):
            norm_mix, b_gate, qa, ka, qb, kb, rpb, norm_ffn = outs[8 * k:8 * k + 8]
            for dst, row0, n_rows in ((norm_mix, 0, 16), (b_gate, 16, 32), (norm_ffn, 120, 16)):
                for r in range(n_rows):
                    dst[:, r * HD:(r + 1) * HD] = val[row0 + r:row0 + r + 1, :]
            for i, dst in enumerate((qa, ka, qb, kb)):
                dst[...] = val[48 + i:49 + i, :]
            rpb[...] = val[56:120, :]
        outs[32][...] = g[LOSS_ROW:LOSS_ROW + 1, 0:1]

    vm = pl.BlockSpec(memory_space=pltpu.VMEM)
    kinds = [jax.ShapeDtypeStruct(sh, F32) for sh in
             ((1, D), (1, 2 * D), (1, HD), (1, HD), (1, HD), (1, HD), (64, HD), (1, D))]
    outs = pl.pallas_call(
        body, name="small_params_adamw", in_specs=[vm] * 4, out_specs=[vm] * 33,
        out_shape=kinds * 4 + [jax.ShapeDtypeStruct((1, 1), F32)],
    )(g, w, m, v)
    return [outs[8 * k:8 * k + 8] for k in range(4)], outs[32]


def _pack_small(norm_mix, b_gate, qa, ka, qb, kb, rpb, norm_ffn):
    gains = jnp.concatenate([qa, ka, qb, kb, jnp.zeros((4, HD), F32)], axis=0)
    rpb_pad = jnp.pad(rpb.reshape(4 * (2 * WIN_R - 1), 2 * WIN_C - 1), ((0, 4), (0, HD - (2 * WIN_C - 1))))
    return jnp.concatenate([norm_mix.reshape(16, HD), b_gate.reshape(32, HD), gains, rpb_pad,
                            norm_ffn.reshape(16, HD), jnp.zeros((8, HD), F32)], axis=0)


LOSS_ROW = 136


def _rpb_from_rows(rows):
    return rows[:60, :2 * WIN_C - 1].reshape(1, 4, 2 * WIN_R - 1, 2 * WIN_C - 1)


def kernel(x, norm_mix, w_in, b_gate, q_norm_a, k_norm_a, q_norm_b, k_norm_b, rpb_b, w_proj_a, w_proj_b, w_out, norm_ffn, w_up, w_down, loss_target, m_norm_mix, m_w_in, m_b_gate, m_q_norm_a, m_k_norm_a, m_q_norm_b, m_k_norm_b, m_rpb_b, m_w_proj_a, m_w_proj_b, m_w_out, m_norm_ffn, m_w_up, m_w_down, v_norm_mix, v_w_in, v_b_gate, v_q_norm_a, v_k_norm_a, v_q_norm_b, v_k_norm_b, v_rpb_b, v_w_proj_a, v_w_proj_b, v_w_out, v_norm_ffn, v_w_up, v_w_down):
    big_w = (w_in[0], w_proj_a[0], w_proj_b[0], w_out[0], w_up[0], w_down[0])
    big_m = (m_w_in[0], m_w_proj_a[0], m_w_proj_b[0], m_w_out[0], m_w_up[0], m_w_down[0])
    big_v = (v_w_in[0], v_w_proj_a[0], v_w_proj_b[0], v_w_out[0], v_w_up[0], v_w_down[0])
    names = ("w_in", "w_proj_a", "w_proj_b", "w_out", "w_up", "w_down")

    shards = [_cast_bf16(w) for w in big_w[:5]]
    g_in, = _gather_on_sequencer(shards[0:1], "gather_w_in")
    g_pa, g_pb, g_out, g_up = _gather_on_sequencer(shards[1:5], "gather_w_mix_up")
    small_w = _pack_small(norm_mix, b_gate, q_norm_a, k_norm_a, q_norm_b, k_norm_b, rpb_b, norm_ffn)
    small_m = _pack_small(m_norm_mix, m_b_gate, m_q_norm_a, m_k_norm_a, m_q_norm_b, m_k_norm_b, m_rpb_b, m_norm_ffn)
    small_v = _pack_small(v_norm_mix, v_b_gate, v_q_norm_a, v_k_norm_a, v_q_norm_b, v_k_norm_b, v_rpb_b, v_norm_ffn)
    g_down = [_gather_on_sequencer([_cast_bf16(big_w[5], part=h, parts=2, after=(small_w, small_m, small_v) * h)],
                                   f"gather_w_down_{h}")[0].reshape(1, D_FF, D // 2) for h in range(2)]

    upd = [None] * 6
    in_flight = {}

    def weight_grads(tag, operands):
        sums = {i: _mm_tn_pair(a, b, name=f"grad_{names[i]}", **GRAD_TILES[i]) for i, (a, b) in operands.items()}
        new = list(sums.values())
        in_flight.update(zip(sums, _chip_exchange_on_sequencer(new, f"chip_exchange_{tag}")))
        return new

    def riders(name):
        i = {"proj_bwd": 5}.get(name)
        if i is None:
            return None
        return (_adamw_rider(in_flight.pop(i), big_w[i], big_m[i], big_v[i]),
                functools.partial(upd.__setitem__, i))

    loss, grad_x, small_g = _local_step(
        x[0], loss_target[0], norm_mix, b_gate, small_w[48:56], small_w[56:120], norm_ffn,
        g_in, g_pa, g_pb, g_out.reshape(D, D), g_up, g_down, weight_grads, riders)

    g_norm_mix, g_b, g_gains, g_rpb, g_norm_ffn = small_g
    small_part = jnp.concatenate([g_norm_mix.reshape(16, HD), g_b.reshape(32, HD),
                                  g_gains, g_rpb, g_norm_ffn.reshape(16, HD),
                                  jnp.pad(loss, ((0, 7), (0, HD - 1)))], axis=0)
    last = grad_x
    for i, r in in_flight.items():
        if i == 0:
            small_sum = _small_exchange(small_part, after=[last])
            small, total = _small_adamw(small_sum, small_w, small_m, small_v)
            last = total
        upd[i] = _adamw(r, big_w[i], big_m[i], big_v[i], name=f"adamw_{names[i]}", after=[last])
        last = upd[i][0]
    s_g, s_d, s_m, s_v = ((*k[:6], _rpb_from_rows(k[6]), k[7]) for k in small)
    b_g, b_d, b_m, b_v = ([u[j][None] for u in upd] for j in range(4))

    def order(small, big):
        nm, bg, qa, ka, qb, kb, rpb, nf = small
        w_in_, pa_, pb_, out_, up_, down_ = big
        return (nm, w_in_, bg, qa, ka, qb, kb, rpb, pa_, pb_, out_, nf, up_, down_)

    return (total[0, 0], grad_x[None], *order(s_g, b_g), *order(s_d, b_d), *order(s_m, b_m), *order(s_v, b_v))
```

```python
import functools
from typing import Callable, NamedTuple

import jax
import jax.numpy as jnp
import numpy as np
from jax import lax
from jax.experimental import pallas as pl
from jax.experimental.pallas import tpu as pltpu
from jax.experimental.pallas import tpu_sc as plsc

F32 = jnp.float32
BF16 = jnp.bfloat16

N_DEV = 8
S = 2048
D = 2048
HD = 128
NH = 16
NH_A = 12
QKV = NH * HD
D_IN = 3 * QKV + 2 * D
D_BR = 512
D_FF = 4 * D
GRID_W = 64
ROWS = S // GRID_W
WIN_R = 8
WIN_C = 16
EPS = 1e-6
NEG = -1e30
SCALE = HD ** -0.5
ROPE_THETA = 10000.0
DILATIONS = (1, 4, 16)
HALF_A = 64
QB = 128
W_IN_SPLIT = 768

LR, B1, B2, AEPS, WD, STEP = 0.001, 0.9, 0.999, 1e-08, 0.01, 10
BC1 = 1.0 - B1 ** STEP
BC2 = 1.0 - B2 ** STEP

VMEM_LIMIT = 56 * 1024 * 1024
MESH = pl.DeviceIdType.MESH

NN = (((1,), (0,)), ((), ()))
NT = (((1,), (1,)), ((), ()))
TN = (((0,), (0,)), ((), ()))


def _params(sem):
    return pltpu.CompilerParams(dimension_semantics=sem, vmem_limit_bytes=VMEM_LIMIT)


def _matmul(a, b, *, product, grid, a_spec, b_spec, epi, out_shape, out_specs, name,
            extra=(), extra_specs=(), after=(), carried=False, rider=None, into=()):
    n_extra = len(extra)
    single = not isinstance(out_shape, (list, tuple))
    out_shape = [out_shape] if single else list(out_shape)
    out_specs = [out_specs] if single else list(out_specs)
    ride = rider(grid[0] * grid[1], lambda j, i: j * grid[1] + i) if rider else None
    r_in = list(ride.inputs) if ride else []
    n_main = len(out_shape)

    def body(a_ref, b_ref, *rest):
        n_in = n_extra + len(after) + len(r_in)
        ins, outs = rest[:n_in], rest[n_in + len(into):]
        epi(product(a_ref, b_ref, ins[:n_extra]), ins[:n_extra], outs[:n_main])
        if ride:
            ride.body(ins[n_extra + len(after):], outs[n_main:])

    res = pl.pallas_call(
        body, name=name, grid=grid,
        in_specs=[a_spec, b_spec, *extra_specs, *[pl.BlockSpec(memory_space=pl.ANY)] * len(after),
                  *(ride.in_specs if ride else []), *[pl.BlockSpec(memory_space=pl.ANY)] * len(into)],
        out_specs=out_specs + (ride.out_specs if ride else []),
        out_shape=out_shape + (ride.out_shape if ride else []),
        input_output_aliases={2 + n_extra + len(after) + len(r_in) + k: k for k in range(len(into))},
        compiler_params=_params(("arbitrary", "arbitrary") if carried else ("parallel", "parallel")),
    )(a, b, *extra, *after, *r_in, *into)
    main = res[0] if single else res[:n_main]
    return (main, res[n_main:]) if ride else main


def _dot(x, y, dims):
    return lax.dot_general(x, y, dims, preferred_element_type=F32)


def _epi_store(acc, ex, outs):
    outs[0][...] = acc.astype(outs[0].dtype)


def _epi_residual(acc, ex, outs):
    outs[0][...] = acc + ex[0][...]


def _mm_nn(a, b3, *, tm, tn, name, out_dtypes=(F32,), epi=_epi_store, extra=(), total=False,
           col0=0, width=None, into=(), stride=None):
    m, kdim = a.shape
    g, _, ng = b3.shape
    n = g * ng
    c0 = col0 // tn
    if tn <= ng:
        npg = ng // tn
        b_spec = pl.BlockSpec((None, kdim, tn), lambda j, i: (j // npg, 0, j % npg))

        def product(a_ref, b_ref, ex):
            return _dot(a_ref[...], b_ref[...], NN)
    else:
        gb = tn // ng
        b_spec = pl.BlockSpec((gb, kdim, ng), lambda j, i: (j, 0, 0))

        def product(a_ref, b_ref, ex):
            return jnp.concatenate([_dot(a_ref[...], b_ref[q], NN) for q in range(gb)], axis=1)

    tile = pl.BlockSpec((tm, tn), lambda j, i: (i, j + c0))
    if stride is not None:
        assert tn == ng and not extra
        tile = pl.BlockSpec((pl.Element(tm), pl.Element(tn)),
                            lambda j, i: (i * tm, pl.multiple_of(j * stride + col0, 128)))
    shapes = [jax.ShapeDtypeStruct((m, width or n), dt) for dt in out_dtypes]
    specs = [tile] * len(shapes)
    if total:
        shapes.append(jax.ShapeDtypeStruct((1, 1), F32))
        specs.append(pl.BlockSpec((1, 1), lambda j, i: (0, 0)))
    single = len(shapes) == 1
    return _matmul(
        a, b3, product=product, grid=(n // tn, m // tm), epi=epi, name=name, carried=total, into=into,
        a_spec=pl.BlockSpec((tm, kdim), lambda j, i: (i, 0)), b_spec=b_spec,
        extra=extra, extra_specs=[tile] * len(extra),
        out_shape=shapes[0] if single else shapes, out_specs=specs[0] if single else specs)


def _mm_nt(a, b3, *, tm, tn, name, out_dtype=F32, epi=_epi_store, extra=(), after=(), rider=None, more_b=(),
           interleaved=False):
    m, kdim = a.shape
    _, n, _ = b3.shape
    n_b = len(more_b)

    def product(a_ref, b_ref, ex):
        refs = (b_ref, *ex[:n_b])
        pieces = ([(ref, q) for q in range(b_ref.shape[0]) for ref in refs] if interleaved
                  else [(ref, q) for ref in refs for q in range(ref.shape[0])])
        acc, k0 = None, 0
        for ref, q in pieces:
            part = _dot(a_ref[:, k0:k0 + ref.shape[2]], ref[q], NT)
            acc = part if acc is None else acc + part
            k0 += ref.shape[2]
        return acc

    def write(acc, ex, outs):
        epi(acc, ex[n_b:], outs)

    def w_spec(w):
        return pl.BlockSpec((w.shape[0], tn, w.shape[2]), lambda j, i: (0, j, 0))

    tile = pl.BlockSpec((tm, tn), lambda j, i: (i, j))
    return _matmul(
        a, b3, product=product, grid=(n // tn, m // tm), epi=write, name=name,
        a_spec=pl.BlockSpec((tm, kdim), lambda j, i: (i, 0)), b_spec=w_spec(b3),
        extra=(*more_b, *extra), extra_specs=[w_spec(w) for w in more_b] + [tile] * len(extra),
        after=after, rider=rider,
        out_shape=jax.ShapeDtypeStruct((m, n), out_dtype), out_specs=tile)


def _mm_tn(a, b, *, tm, tn, name, groups=1, out_dtype=BF16):
    t, m = a.shape
    _, n = b.shape
    ng = n // groups
    if tn <= ng:
        npg = ng // tn
        out_spec = pl.BlockSpec((None, tm, tn), lambda j, i: (j // npg, i, j % npg))
        epi = _epi_store

        def product(a_ref, b_ref, ex):
            return _dot(a_ref[...], b_ref[...], TN)
    else:
        gb = tn // ng
        out_spec = pl.BlockSpec((gb, tm, ng), lambda j, i: (j, i, 0))

        def product(a_ref, b_ref, ex):
            return [_dot(a_ref[...], b_ref[:, q * ng:(q + 1) * ng], TN) for q in range(gb)]

        def epi(parts, ex, outs):
            for q, part in enumerate(parts):
                outs[0][q] = part.astype(out_dtype)

    return _matmul(
        a, b, product=product, grid=(n // tn, m // tm), epi=epi, name=name,
        a_spec=pl.BlockSpec((t, tm), lambda j, i: (0, i)),
        b_spec=pl.BlockSpec((t, tn), lambda j, i: (0, j)),
        out_shape=jax.ShapeDtypeStruct((groups, m, ng), out_dtype), out_specs=out_spec)


def _rms_fwd(x, g, *, name, tr=256):
    def body(x_ref, g_ref, y_ref, r_ref):
        xv = x_ref[...]
        r = lax.rsqrt(jnp.mean(xv * xv, axis=-1, keepdims=True) + EPS)
        y_ref[...] = (xv * r * g_ref[...]).astype(BF16)
        r_ref[...] = r

    row = pl.BlockSpec((tr, D), lambda i: (i, 0))
    return pl.pallas_call(
        body, name=name, grid=(S // tr,),
        in_specs=[row, pl.BlockSpec((1, D), lambda i: (0, 0))],
        out_specs=[row, pl.BlockSpec((tr, 1), lambda i: (i, 0))],
        out_shape=[jax.ShapeDtypeStruct((S, D), BF16), jax.ShapeDtypeStruct((S, 1), F32)],
        compiler_params=_params(("parallel",)),
    )(x, g)


def _rms_bwd(dy, x, rstd, g, resid, *, name, bf16_copy, tr=256):
    def body(dy_ref, x_ref, r_ref, g_ref, res_ref, dx_ref, *rest):
        dg_ref = rest[-1]
        r = r_ref[...]
        xh = x_ref[...] * r
        dyv = dy_ref[...]
        t = dyv * g_ref[...]
        dx = r * (t - xh * jnp.mean(t * xh, axis=-1, keepdims=True)) + res_ref[...]
        dx_ref[...] = dx
        if bf16_copy:
            rest[0][...] = dx.astype(BF16)
        part = jnp.sum(dyv * xh, axis=0, keepdims=True)

        @pl.when(pl.program_id(0) == 0)
        def _():
            dg_ref[...] = part

        @pl.when(pl.program_id(0) > 0)
        def _():
            dg_ref[...] += part

    row = pl.BlockSpec((tr, D), lambda i: (i, 0))
    vec = pl.BlockSpec((1, D), lambda i: (0, 0))
    return pl.pallas_call(
        body, name=name, grid=(S // tr,),
        in_specs=[row, row, pl.BlockSpec((tr, 1), lambda i: (i, 0)), vec, row],
        out_specs=[row] + [row] * bf16_copy + [vec],
        out_shape=[jax.ShapeDtypeStruct((S, D), F32)] + [jax.ShapeDtypeStruct((S, D), BF16)] * bf16_copy
        + [jax.ShapeDtypeStruct((1, D), F32)],
        compiler_params=_params(("arbitrary",)),
    )(dy, x, rstd, g, resid)


def _rope_tables():
    pos = np.arange(S, dtype=np.float32)
    inv = (ROPE_THETA ** (-np.arange(0, HD, 2, dtype=np.float32) / HD)).astype(np.float32)
    ang = pos[:, None] * inv[None, :]
    cos, sin = np.cos(ang), np.sin(ang)
    return (jnp.asarray(np.concatenate([cos, cos], axis=-1), F32),
            jnp.asarray(np.concatenate([-sin, sin], axis=-1), F32))


def _swap_halves(t):
    return pltpu.roll(t, HD // 2, axis=1)


TOK = 256


def _lane_block_spec(d, last=HD):
    return pl.BlockSpec((4, TOK // d, d * last), lambda i: (0, i, 0))


def _to_lane_blocks(dst, head, val, d, scr, dtype):
    w = val.shape[1]
    if d == 1:
        dst[head] = val.astype(dtype)
        return
    scr[...] = val
    for r in range(d):
        dst[head, :, r * w:(r + 1) * w] = scr[pl.ds(r, TOK // d, stride=d), :].astype(dtype)


def _from_lane_blocks(src, head, d, w, scr):
    if d == 1:
        return src[head].astype(F32)
    for r in range(d):
        scr[pl.ds(r, TOK // d, stride=d), :] = src[head, :, r * w:(r + 1) * w].astype(F32)
    return scr[...]


def _qk_prep(proj, gains, cos2, sin2):
    def body(q_ref, k_ref, v_ref, g_ref, c_ref, s_ref, *rest):
        outs, scr = rest[:-1], rest[-1]
        cos, sin = c_ref[...], s_ref[...]
        for which, (src, row_a, row_b) in enumerate(((q_ref, 0, 2), (k_ref, 1, 3), (v_ref, None, None))):
            for h in range(NH):
                y = src[:, h * HD:(h + 1) * HD]
                if row_a is not None:
                    y = y * lax.rsqrt(jnp.mean(y * y, axis=-1, keepdims=True) + EPS)
                    if h < NH_A:
                        y = y * g_ref[row_a:row_a + 1, :]
                        y = y * cos + _swap_halves(y) * sin
                    else:
                        y = y * g_ref[row_b:row_b + 1, :]
                if h < NH_A:
                    gi = h // 4
                    _to_lane_blocks(outs[3 * gi + which], h % 4, y, DILATIONS[gi], scr, BF16)
                else:
                    hb = h - NH_A
                    outs[9 + which][:, hb * HD:(hb + 1) * HD] = y.astype(BF16)

    def blk(c):
        return pl.BlockSpec((TOK, QKV), lambda i: (i, c))
    tab = pl.BlockSpec((TOK, HD), lambda i: (i, 0))
    out_specs, out_shape = [], []
    for d in DILATIONS:
        out_specs += [_lane_block_spec(d)] * 3
        out_shape += [jax.ShapeDtypeStruct((4, S // d, d * HD), BF16)] * 3
    out_specs += [pl.BlockSpec((TOK, D_BR), lambda i: (i, 0))] * 3
    out_shape += [jax.ShapeDtypeStruct((S, D_BR), BF16)] * 3
    outs = pl.pallas_call(
        body, name="qk_prep", grid=(S // TOK,),
        in_specs=[blk(0), blk(1), blk(2), pl.BlockSpec((8, HD), lambda i: (0, 0)), tab, tab],
        out_specs=out_specs, out_shape=out_shape,
        scratch_shapes=[pltpu.VMEM((TOK, HD), F32)],
        compiler_params=_params(("parallel",)),
    )(proj, proj, proj, gains, cos2, sin2)
    return [tuple(outs[3 * gi:3 * gi + 3]) for gi in range(3)], tuple(outs[9:12])


def _qk_prep_bwd(dproj, proj, gains, cos2, sin2, grads_a, grads_b):
    def body(dp_in, q_ref, k_ref, g_ref, c_ref, s_ref, *rest):
        grads, (dp_out, dg_ref, scr) = rest[:12], rest[12:]
        del dp_in
        cos, sin = c_ref[...], s_ref[...]

        def grad_of(which, h):
            if h < NH_A:
                gi = h // 4
                return _from_lane_blocks(grads[3 * gi + which], h % 4, DILATIONS[gi], HD, scr)
            hb = h - NH_A
            return grads[9 + which][:, hb * HD:(hb + 1) * HD].astype(F32)

        dg_rows = []
        for which, (src, base, row_a, row_b) in enumerate(((q_ref, 0, 0, 2), (k_ref, QKV, 1, 3))):
            dg_a = jnp.zeros((1, HD), F32)
            dg_b = jnp.zeros((1, HD), F32)
            for h in range(NH):
                t = src[:, h * HD:(h + 1) * HD]
                dy = grad_of(which, h)
                r = lax.rsqrt(jnp.mean(t * t, axis=-1, keepdims=True) + EPS)
                xh = t * r
                if h < NH_A:
                    dy = dy * cos - _swap_halves(dy) * sin
                    gain = g_ref[row_a:row_a + 1, :]
                    dg_a = dg_a + jnp.sum(dy * xh, axis=0, keepdims=True)
                else:
                    gain = g_ref[row_b:row_b + 1, :]
                    dg_b = dg_b + jnp.sum(dy * xh, axis=0, keepdims=True)
                u = dy * gain
                dx = r * (u - xh * jnp.mean(u * xh, axis=-1, keepdims=True))
                dp_out[:, base + h * HD:base + (h + 1) * HD] = dx.astype(BF16)
            dg_rows += [(row_a, dg_a), (row_b, dg_b)]
        for h in range(NH):
            dp_out[:, 2 * QKV + h * HD:2 * QKV + (h + 1) * HD] = grad_of(2, h).astype(BF16)

        @pl.when(pl.program_id(0) == 0)
        def _():
            dg_ref[...] = jnp.zeros((8, HD), F32)

        for row, val in dg_rows:
            dg_ref[row:row + 1, :] += val

    def blk(c):
        return pl.BlockSpec((TOK, QKV), lambda i: (i, c))
    tab = pl.BlockSpec((TOK, HD), lambda i: (i, 0))
    gain_spec = pl.BlockSpec((8, HD), lambda i: (0, 0))
    grad_specs = [s for d in DILATIONS for s in [_lane_block_spec(d)] * 3]
    grad_specs += [pl.BlockSpec((TOK, D_BR), lambda i: (i, 0))] * 3
    return pl.pallas_call(
        body, name="qk_prep_bwd", grid=(S // TOK,),
        in_specs=[pl.BlockSpec(memory_space=pl.ANY), blk(0), blk(1), gain_spec, tab, tab] + grad_specs,
        out_specs=[pl.BlockSpec((TOK, 3 * QKV), lambda i: (i, 0)), gain_spec],
        out_shape=[jax.ShapeDtypeStruct((S, D_IN), BF16), jax.ShapeDtypeStruct((8, HD), F32)],
        input_output_aliases={0: 0},
        scratch_shapes=[pltpu.VMEM((TOK, HD), F32)],
        compiler_params=_params(("arbitrary",)),
    )(dproj, proj, proj, gains, cos2, sin2, *[g for grp in grads_a for g in grp], *grads_b)


def _mix_fwd(oa, ob, w_pa, w_pb, proj, b_gate, *, tr=256):
    def body(oa_ref, ob_ref, pa_ref, pb_ref, la_ref, lb_ref, ba_ref, bb_ref, mix_ref, ya_ref, yb_ref):
        ya = jnp.concatenate([_dot(oa_ref[...], pa_ref[q], NN) for q in range(N_DEV)], axis=1)
        yb = jnp.concatenate([_dot(ob_ref[...], pb_ref[q], NN) for q in range(N_DEV)], axis=1)
        ga = jax.nn.sigmoid(la_ref[...] + ba_ref[...])
        gb = jax.nn.sigmoid(lb_ref[...] + bb_ref[...])
        mix_ref[...] = (ga * ya + gb * yb).astype(BF16)
        ya_ref[...] = ya.astype(BF16)
        yb_ref[...] = yb.astype(BF16)

    row = pl.BlockSpec((tr, D), lambda i: (i, 0))
    branch = pl.BlockSpec((tr, D_BR), lambda i: (i, 0))
    whole = pl.BlockSpec((N_DEV, D_BR, D // N_DEV), lambda i: (0, 0, 0))
    return pl.pallas_call(
        body, name="mix_fwd", grid=(S // tr,),
        in_specs=[branch, branch, whole, whole,
                  pl.BlockSpec((tr, D), lambda i: (i, 3)), pl.BlockSpec((tr, D), lambda i: (i, 4)),
                  pl.BlockSpec((1, D), lambda i: (0, 0)), pl.BlockSpec((1, D), lambda i: (0, 1))],
        out_specs=[row, row, row], out_shape=[jax.ShapeDtypeStruct((S, D), BF16)] * 3,
        compiler_params=_params(("parallel",)),
    )(oa, ob, w_pa, w_pb, proj, proj, b_gate, b_gate)


def _mix_bwd(dh1b, w_out, proj, b_gate, ya, yb, *, tr=256):
    def body(dh_ref, w_ref, la_ref, lb_ref, b_ref, ya_ref, yb_ref, dya_ref, dyb_ref, dp_ref, db_ref):
        dm = _dot(dh_ref[...], w_ref[...], NT)
        parts = []
        for l_ref, y_ref, dy_ref, lo in ((la_ref, ya_ref, dya_ref, 0), (lb_ref, yb_ref, dyb_ref, D)):
            g = jax.nn.sigmoid(l_ref[...] + b_ref[:, lo:lo + D])
            dy_ref[...] = (dm * g).astype(BF16)
            dl = dm * y_ref[...].astype(F32) * g * (1.0 - g)
            dp_ref[:, lo:lo + D] = dl.astype(BF16)
            parts.append(jnp.sum(dl, axis=0, keepdims=True))
        part = jnp.concatenate(parts, axis=1)

        @pl.when(pl.program_id(0) == 0)
        def _():
            db_ref[...] = part

        @pl.when(pl.program_id(0) > 0)
        def _():
            db_ref[...] += part

    row = pl.BlockSpec((tr, D), lambda i: (i, 0))
    vec = pl.BlockSpec((1, 2 * D), lambda i: (0, 0))
    gate_cols = pl.BlockSpec((pl.Element(tr), pl.Element(2 * D)), lambda i: (i * tr, 3 * QKV))
    return pl.pallas_call(
        body, name="mix_bwd", grid=(S // tr,),
        in_specs=[row, pl.BlockSpec((D, D), lambda i: (0, 0)),
                  pl.BlockSpec((tr, D), lambda i: (i, 3)), pl.BlockSpec((tr, D), lambda i: (i, 4)), vec, row, row],
        out_specs=[row, row, gate_cols, vec],
        out_shape=[jax.ShapeDtypeStruct((S, D), BF16), jax.ShapeDtypeStruct((S, D), BF16),
                   jax.ShapeDtypeStruct((S, D_IN), BF16), jax.ShapeDtypeStruct((1, 2 * D), F32)],
        compiler_params=_params(("arbitrary",)),
    )(dh1b, w_out, proj, proj, b_gate, ya, yb)


def _band_blocks(m_len):
    wk = min(m_len, QB + 2 * QB)
    return [(qb * QB, min(max(qb * QB - QB, 0), m_len - wk), wk) for qb in range(m_len // QB)]


def _band_scores(q, kw, q0, k0, wk):
    s = _dot(q, kw, NT) * SCALE
    qpos = q0 + lax.broadcasted_iota(jnp.int32, (QB, 1), 0)
    kpos = k0 + lax.broadcasted_iota(jnp.int32, (1, wk), 1)
    return jnp.where(jnp.abs(kpos - qpos) <= HALF_A, s, NEG)


def _attn_a_fwd(q, k, v, gi):
    d = DILATIONS[gi]
    m_len = S // d

    def body(q_ref, k_ref, v_ref, o_ref, lse_ref):
        for r in range(d):
            lanes = slice(r * HD, (r + 1) * HD)
            for q0, k0, wk in _band_blocks(m_len):
                s = _band_scores(q_ref[q0:q0 + QB, lanes], k_ref[k0:k0 + wk, lanes], q0, k0, wk)
                m = jnp.max(s, axis=-1, keepdims=True)
                p = jnp.exp(s - m)
                l = jnp.sum(p, axis=-1, keepdims=True)
                o_ref[q0:q0 + QB, lanes] = _dot(p.astype(BF16), v_ref[k0:k0 + wk, lanes], NN) / l
                lse_ref[q0:q0 + QB, r:r + 1] = m + jnp.log(l)

    head = pl.BlockSpec((None, m_len, d * HD), lambda h: (h, 0, 0))
    stat = pl.BlockSpec((None, m_len, d), lambda h: (h, 0, 0))
    return pl.pallas_call(
        body, name=f"attn_a_fwd_{gi}", grid=(4,),
        in_specs=[head, head, head], out_specs=[head, stat],
        out_shape=[jax.ShapeDtypeStruct((4, m_len, d * HD), F32), jax.ShapeDtypeStruct((4, m_len, d), F32)],
        compiler_params=_params(("parallel",)),
    )(q, k, v)


def _combine_a(os, lses):
    def body(o0, o1, o2, l0, l1, l2, oa_ref, lse_ref, scr, scr1):
        for h in range(4):
            o = [_from_lane_blocks(ref, h, d, HD, scr) for ref, d in zip((o0, o1, o2), DILATIONS)]
            a, b, c = (_from_lane_blocks(ref, h, d, 1, scr1) for ref, d in zip((l0, l1, l2), DILATIONS))
            m = jnp.maximum(jnp.maximum(a, b), c)
            wa, wb, wc = jnp.exp(a - m), jnp.exp(b - m), jnp.exp(c - m)
            tot = wa + wb + wc
            oa_ref[:, h * HD:(h + 1) * HD] = ((wa * o[0] + wb * o[1] + wc * o[2]) / tot).astype(BF16)
            lse_ref[h] = m + jnp.log(tot)

    return pl.pallas_call(
        body, name="combine_a", grid=(S // TOK,),
        in_specs=[_lane_block_spec(d) for d in DILATIONS] + [_lane_block_spec(d, 1) for d in DILATIONS],
        out_specs=[pl.BlockSpec((TOK, D_BR), lambda i: (i, 0)), pl.BlockSpec((4, TOK, 1), lambda i: (0, i, 0))],
        out_shape=[jax.ShapeDtypeStruct((S, D_BR), BF16), jax.ShapeDtypeStruct((4, S, 1), F32)],
        scratch_shapes=[pltpu.VMEM((TOK, HD), F32), pltpu.VMEM((TOK, 1), F32)],
        compiler_params=_params(("parallel",)),
    )(*os, *lses)


def _proj_a_bwd(dya, w_pa, oa, lse):
    kg = D // N_DEV

    def body(dy_ref, w_ref, o_ref, l_ref, *rest):
        outs, (scr, scr1) = rest[:9], rest[9:]
        doa = _dot(dy_ref[:, 0:kg], w_ref[0], NT)
        for q in range(1, N_DEV):
            doa = doa + _dot(dy_ref[:, q * kg:(q + 1) * kg], w_ref[q], NT)
        for h in range(4):
            do = doa[:, h * HD:(h + 1) * HD]
            dsum = jnp.sum(do * o_ref[:, h * HD:(h + 1) * HD].astype(F32), axis=-1, keepdims=True)
            for gi, d in enumerate(DILATIONS):
                _to_lane_blocks(outs[3 * gi], h, do, d, scr, BF16)
                _to_lane_blocks(outs[3 * gi + 1], h, l_ref[h], d, scr1, F32)
                _to_lane_blocks(outs[3 * gi + 2], h, dsum, d, scr1, F32)

    row = pl.BlockSpec((TOK, D_BR), lambda i: (i, 0))
    out_specs, out_shape = [], []
    for d in DILATIONS:
        out_specs += [_lane_block_spec(d), _lane_block_spec(d, 1), _lane_block_spec(d, 1)]
        out_shape += [jax.ShapeDtypeStruct((4, S // d, d * HD), BF16)] + [jax.ShapeDtypeStruct((4, S // d, d), F32)] * 2
    outs = pl.pallas_call(
        body, name="proj_a_bwd", grid=(S // TOK,),
        in_specs=[pl.BlockSpec((TOK, D), lambda i: (i, 0)),
                  pl.BlockSpec((N_DEV, D_BR, kg), lambda i: (0, 0, 0)),
                  row, pl.BlockSpec((4, TOK, 1), lambda i: (0, i, 0))],
        out_specs=out_specs, out_shape=out_shape,
        scratch_shapes=[pltpu.VMEM((TOK, HD), F32), pltpu.VMEM((TOK, 1), F32)],
        compiler_params=_params(("parallel",)),
    )(dya, w_pa, oa, lse)
    return [tuple(outs[3 * gi:3 * gi + 3]) for gi in range(3)]


def _attn_a_bwd(q, k, v, do, lse, dsum, gi):
    d = DILATIONS[gi]
    m_len = S // d

    def body(q_ref, k_ref, v_ref, do_ref, lse_ref, dsum_ref, dq_ref, dk_out, dv_out, dk_ref, dv_ref):
        dk_ref[...] = jnp.zeros((m_len, d * HD), F32)
        dv_ref[...] = jnp.zeros((m_len, d * HD), F32)
        for r in range(d):
            lanes = slice(r * HD, (r + 1) * HD)
            for q0, k0, wk in _band_blocks(m_len):
                rows, keys = slice(q0, q0 + QB), slice(k0, k0 + wk)
                qv, kw, vw, dov = q_ref[rows, lanes], k_ref[keys, lanes], v_ref[keys, lanes], do_ref[rows, lanes]
                p = jnp.exp(_band_scores(qv, kw, q0, k0, wk) - lse_ref[rows, r:r + 1])
                ds = (p * (_dot(dov, vw, NT) - dsum_ref[rows, r:r + 1]) * SCALE).astype(BF16)
                dq_ref[rows, lanes] = _dot(ds, kw, NN).astype(BF16)
                dk_ref[keys, lanes] += _dot(ds, qv, TN)
                dv_ref[keys, lanes] += _dot(p.astype(BF16), dov, TN)
        dk_out[...] = dk_ref[...].astype(BF16)
        dv_out[...] = dv_ref[...].astype(BF16)

    head = pl.BlockSpec((None, m_len, d * HD), lambda h: (h, 0, 0))
    stat = pl.BlockSpec((None, m_len, d), lambda h: (h, 0, 0))
    shape = jax.ShapeDtypeStruct((4, m_len, d * HD), BF16)
    return pl.pallas_call(
        body, name=f"attn_a_bwd_{gi}", grid=(4,),
        in_specs=[head, head, head, head, stat, stat], out_specs=[head, head, head],
        out_shape=[shape, shape, shape],
        scratch_shapes=[pltpu.VMEM((m_len, d * HD), F32)] * 2,
        compiler_params=_params(("arbitrary",)),
    )(q, k, v, do, lse, dsum)


KEYS_B = WIN_R * GRID_W
N_OFF = WIN_R


def _bias_constants():
    q = np.arange(GRID_W)[:, None]
    kc = np.arange(GRID_W)[None, :]
    dc = np.clip(kc - q, -(WIN_C - 1), WIN_C - 1) + (WIN_C - 1)
    expand = np.zeros((HD, GRID_W * GRID_W), np.float32)
    expand[dc.reshape(-1), np.arange(GRID_W * GRID_W)] = 1.0
    cs = np.clip(q - WIN_C // 2, 0, GRID_W - WIN_C)
    keep = ((kc >= cs) & (kc < cs + WIN_C)).reshape(1, -1).astype(np.float32)
    sel = np.zeros((64, 4 * N_OFF * WIN_R), np.float32)
    for h in range(4):
        for off in range(N_OFF):
            for j in range(WIN_R):
                sel[h * (2 * WIN_R - 1) + off + j, (h * N_OFF + off) * WIN_R + j] = 1.0
    return jnp.asarray(expand), jnp.asarray(keep), jnp.asarray(sel)


def _bias_expand(rpb_pad, expand, keep, sel):
    def body(r_ref, e_ref, k_ref, s_ref, o_ref):
        t = lax.dot_general(r_ref[...], e_ref[...], NN, precision=lax.Precision.HIGHEST,
                            preferred_element_type=F32)
        rows = lax.dot_general(s_ref[...], t, TN, precision=lax.Precision.HIGHEST,
                               preferred_element_type=F32)
        o_ref[...] = jnp.where(k_ref[...] > 0.5, rows, NEG)

    return pl.pallas_call(
        body, name="bias_expand",
        out_shape=jax.ShapeDtypeStruct((4 * N_OFF * WIN_R, GRID_W * GRID_W), F32),
        compiler_params=pltpu.CompilerParams(vmem_limit_bytes=VMEM_LIMIT),
    )(rpb_pad, expand, keep, sel)


def _bias_reduce(dbias_tab):
    lane0 = GRID_W - WIN_C
    flip = np.zeros((GRID_W, GRID_W), np.float32)
    flip[np.arange(GRID_W), GRID_W - 1 - np.arange(GRID_W)] = 1.0
    place = np.zeros((WIN_R, 64, 4 * N_OFF), np.float32)
    for j in range(WIN_R):
        for h in range(4):
            for off in range(N_OFF):
                place[j, h * (2 * WIN_R - 1) + off + j, h * N_OFF + off] = 1.0

    def exact(x, y):
        return lax.dot_general(x, y, NN, precision=lax.Precision.HIGHEST, preferred_element_type=F32)

    def body(x_ref, flip_ref, place_ref, o_ref, z_ref):
        for h in range(4):
            for off in range(N_OFF):
                lined_up = pltpu.roll(exact(flip_ref[...], x_ref[h, off]), 0, axis=1, stride=1, stride_axis=0)
                z_ref[h * N_OFF + off:h * N_OFF + off + 1, :] = jnp.sum(lined_up, axis=0, keepdims=True)
        acc = jnp.zeros((64, HD), F32)
        for j in range(WIN_R):
            at_zero = pltpu.roll(z_ref[...], (KEYS_B - (j * GRID_W + lane0)) % KEYS_B, axis=1)[:, :HD]
            acc = acc + exact(place_ref[j], at_zero)
        lane = lax.broadcasted_iota(jnp.int32, (64, HD), 1)
        o_ref[...] = jnp.where(lane < 2 * WIN_C - 1, acc, 0.0)

    return pl.pallas_call(
        body, name="bias_reduce", out_shape=jax.ShapeDtypeStruct((64, HD), F32),
        scratch_shapes=[pltpu.VMEM((4 * N_OFF, KEYS_B), F32)],
        compiler_params=pltpu.CompilerParams(vmem_limit_bytes=VMEM_LIMIT),
    )(dbias_tab, jnp.asarray(flip), jnp.asarray(place))


def _rows_to_tab(rows):
    t = rows.reshape(4, N_OFF, WIN_R, GRID_W, GRID_W)
    return t.transpose(0, 1, 3, 2, 4).reshape(4, N_OFF, GRID_W, KEYS_B)


def _row_window(r):
    r0 = jnp.clip(r - WIN_R // 2, 0, ROWS - WIN_R)
    off = r0 + (WIN_R - 1) - r
    return pl.multiple_of(r * GRID_W, GRID_W), pl.multiple_of(r0 * GRID_W, GRID_W), off


def _attn_b_fwd(qn, kn, vb, bias_tab):
    def body(q_ref, k_ref, v_ref, b_ref, o_ref, lse_ref):
        def row(r, carry):
            qs, ks, off = _row_window(r)
            q = q_ref[pl.ds(qs, GRID_W), :]
            s = lax.dot_general(q, k_ref[pl.ds(ks, KEYS_B), :], NT, preferred_element_type=F32) * SCALE
            s = s + b_ref[off]
            m = jnp.max(s, axis=-1, keepdims=True)
            p = jnp.exp(s - m)
            l = jnp.sum(p, axis=-1, keepdims=True)
            o = lax.dot_general(p.astype(BF16), v_ref[pl.ds(ks, KEYS_B), :], NN, preferred_element_type=F32)
            o_ref[pl.ds(qs, GRID_W), :] = (o / l).astype(BF16)
            lse_ref[pl.ds(qs, GRID_W), :] = m + jnp.log(l)
            return carry

        lax.fori_loop(0, ROWS, row, 0, unroll=8)

    full = pl.BlockSpec((S, HD), lambda h: (0, h))
    return pl.pallas_call(
        body, name="attn_b_fwd", grid=(4,),
        in_specs=[full, full, full, pl.BlockSpec((None, N_OFF, GRID_W, KEYS_B), lambda h: (h, 0, 0, 0))],
        out_specs=[pl.BlockSpec((S, HD), lambda h: (0, h)), pl.BlockSpec((None, S, 1), lambda h: (h, 0, 0))],
        out_shape=[jax.ShapeDtypeStruct((S, D_BR), BF16), jax.ShapeDtypeStruct((4, S, 1), F32)],
        compiler_params=_params(("parallel",)),
    )(qn, kn, vb, bias_tab)


def _attn_b_bwd(qn, kn, vb, bias_tab, ob, dob, lse):
    def body(q_ref, k_ref, v_ref, b_ref, o_ref, do_ref, lse_ref, dq_ref, dk_out, dv_out, db_ref, dk_ref, dv_ref):
        dk_ref[...] = jnp.zeros((S, HD), F32)
        dv_ref[...] = jnp.zeros((S, HD), F32)
        db_ref[...] = jnp.zeros((N_OFF, GRID_W, KEYS_B), F32)

        def row(r, carry):
            qs, ks, off = _row_window(r)
            rows = pl.ds(qs, GRID_W)
            keys = pl.ds(ks, KEYS_B)
            q = q_ref[rows, :]
            kw = k_ref[keys, :]
            s = lax.dot_general(q, kw, NT, preferred_element_type=F32) * SCALE + b_ref[off]
            p = jnp.exp(s - lse_ref[rows, :])
            do = do_ref[rows, :]
            dobf = do.astype(BF16)
            dsum = jnp.sum(do * o_ref[rows, :].astype(F32), axis=-1, keepdims=True)
            dp = lax.dot_general(dobf, v_ref[keys, :], NT, preferred_element_type=F32)
            ds = p * (dp - dsum)
            db_ref[off] += ds
            dsb = (ds * SCALE).astype(BF16)
            dq_ref[rows, :] = lax.dot_general(dsb, kw, NN, preferred_element_type=F32).astype(BF16)
            dk_ref[keys, :] += lax.dot_general(dsb, q, TN, preferred_element_type=F32)
            dv_ref[keys, :] += lax.dot_general(p.astype(BF16), dobf, TN, preferred_element_type=F32)
            return carry

        lax.fori_loop(0, ROWS, row, 0, unroll=8)
        dk_out[...] = dk_ref[...].astype(BF16)
        dv_out[...] = dv_ref[...].astype(BF16)

    full = pl.BlockSpec((S, HD), lambda h: (0, h))
    slot = pl.BlockSpec((S, HD), lambda h: (0, h))
    tab = pl.BlockSpec((None, N_OFF, GRID_W, KEYS_B), lambda h: (h, 0, 0, 0))
    shape = jax.ShapeDtypeStruct((S, D_BR), BF16)
    return pl.pallas_call(
        body, name="attn_b_bwd", grid=(4,),
        in_specs=[full, full, full, tab, slot, slot, pl.BlockSpec((None, S, 1), lambda h: (h, 0, 0))],
        out_specs=[slot, slot, slot, tab],
        out_shape=[shape, shape, shape, jax.ShapeDtypeStruct((4, N_OFF, GRID_W, KEYS_B), F32)],
        scratch_shapes=[pltpu.VMEM((S, HD), F32)] * 2,
        compiler_params=_params(("arbitrary",)),
    )(qn, kn, vb, bias_tab, ob, dob, lse)


def _epi_relu_sq(acc, ex, outs):
    u = jnp.maximum(acc, 0.0)
    outs[0][...] = u.astype(BF16)
    outs[1][...] = (u * u).astype(BF16)


def _epi_relu_sq_bwd(acc, ex, outs):
    outs[0][...] = (acc * (2.0 * ex[0][...].astype(F32))).astype(BF16)


def _epi_loss_head(acc, ex, outs):
    e = acc + ex[0][...] - ex[1][...]
    dy = e * (1.0 / D)
    outs[0][...] = dy
    outs[1][...] = dy.astype(BF16)
    part = (0.5 / D) * jnp.sum(jnp.sum(e * e, axis=-1, keepdims=True), axis=0, keepdims=True)
    first = (pl.program_id(0) == 0) & (pl.program_id(1) == 0)

    @pl.when(first)
    def _():
        outs[2][...] = part

    @pl.when(jnp.logical_not(first))
    def _():
        outs[2][...] += part


def _local_step(x, target, norm_mix, b_gate, gains, rpb_pad, norm_ffn,
                w_in, w_pa, w_pb, w_out, w_up, w_down, weight_grads, riders=lambda name: None):
    def ridden(name, *args, **kwargs):
        ride = riders(name)
        if ride is None:
            return _mm_nt(*args, name=name, **kwargs)
        out, rode = _mm_nt(*args, name=name, rider=ride[0], **kwargs)
        ride[1](rode)
        return out

    cos2, sin2 = _rope_tables()
    expand, keep, sel = _bias_constants()
    w_out3 = w_out[None]

    xn, rstd1 = _rms_fwd(x, norm_mix, name="rms_mix")
    per_dev = D_IN // N_DEV
    proj = _mm_nn(xn, w_in[0], tm=1024, tn=W_IN_SPLIT, name="proj_0", stride=per_dev, width=D_IN)
    proj = _mm_nn(xn, w_in[1], tm=1024, tn=per_dev - W_IN_SPLIT, name="proj_1", stride=per_dev, width=D_IN,
                  col0=W_IN_SPLIT, into=(proj,))
    qkv_a, qkv_b = _qk_prep(proj, gains, cos2, sin2)
    fwd_a = [_attn_a_fwd(*qkv_a[gi], gi) for gi in range(3)]
    oa, lse_a = _combine_a([o for o, _ in fwd_a], [l for _, l in fwd_a])
    bias_tab = _rows_to_tab(_bias_expand(rpb_pad, expand, keep, sel))
    ob, lse_b = _attn_b_fwd(*qkv_b, bias_tab)
    mixed, ya, yb = _mix_fwd(oa, ob, w_pa, w_pb, proj, b_gate)
    h1 = _mm_nn(mixed, w_out3, tm=1024, tn=1024, name="out_proj", epi=_epi_residual, extra=(x,))
    hn, rstd2 = _rms_fwd(h1, norm_ffn, name="rms_ffn")
    u, usq = _mm_nn(hn, w_up, tm=1024, tn=1024, name="ffn_up", epi=_epi_relu_sq,
                    out_dtypes=(BF16, BF16))
    dy, dyb, loss = _mm_nn(usq, w_down[0], tm=512, tn=512, name="ffn_down_0", epi=_epi_loss_head,
                           extra=(h1, target), out_dtypes=(F32, BF16), total=True, width=D)
    dy, dyb, loss_1 = _mm_nn(usq, w_down[1], tm=512, tn=512, name="ffn_down_1", epi=_epi_loss_head,
                             extra=(h1, target), out_dtypes=(F32, BF16), total=True, width=D,
                             col0=D // 2, into=(dy, dyb))
    loss = loss + loss_1

    sent = weight_grads("w_down", {5: (usq, dyb)})
    du = _mm_nt(dyb, w_down[0], more_b=(w_down[1],), tm=1024, tn=1024, name="ffn_down_bwd", out_dtype=BF16,
                epi=_epi_relu_sq_bwd, extra=(u,), after=sent)
    sent = weight_grads("w_up", {4: (hn, du)})
    dhn = ridden("ffn_up_bwd", du, w_up, tm=512, tn=512, after=sent)
    dh1, dh1b, g_norm_ffn = _rms_bwd(dhn, h1, rstd2, norm_ffn, dy, name="rms_ffn_bwd", bf16_copy=True)

    dya, dyb2, dproj, g_b = _mix_bwd(dh1b, w_out, proj, b_gate, ya, yb)
    sent = weight_grads("w_mix", {3: (mixed, dh1b), 1: (oa, dya), 2: (ob, dyb2)})
    dob = _mm_nt(dyb2, w_pb, tm=1024, tn=D_BR, name="proj_b_bwd", after=sent)
    prep = _proj_a_bwd(dya, w_pa, oa, lse_a)
    grads_a = [_attn_a_bwd(*qkv_a[gi], *prep[gi], gi) for gi in range(3)]
    dqb, dkb, dvb, dbias = _attn_b_bwd(*qkv_b, bias_tab, ob, dob, lse_b)
    g_rpb = _bias_reduce(dbias)
    dproj, g_gains = _qk_prep_bwd(dproj, proj, gains, cos2, sin2, grads_a, (dqb, dkb, dvb))
    sent = weight_grads("w_in", {0: (xn, dproj)})
    dxn = ridden("proj_bwd", dproj, w_in[0], more_b=(w_in[1],), interleaved=True, tm=256, tn=512, after=sent)
    grad_x, g_norm_mix = _rms_bwd(dxn, x, rstd1, norm_mix, dh1, name="rms_mix_bwd", bf16_copy=False)

    small = (g_norm_mix, g_b, g_gains, g_rpb, g_norm_ffn)
    return loss, grad_x, small


def _cast_bf16(w, *, part=0, parts=1, window=None, after=(), tr=256):
    rows, cols = w.shape[0], w.shape[1] // parts
    tr = min(tr, rows)
    src = pl.BlockSpec((tr, cols), lambda i: (i, part))
    if window is not None:
        part, cols = window
        src = pl.BlockSpec((pl.Element(tr), pl.Element(cols)), lambda i: (i * tr, part))

    def body(w_ref, *rest):
        rest[-1][...] = w_ref[...].astype(BF16)

    return pl.pallas_call(
        body, name=f"cast_{rows}x{cols}_{part}", grid=(rows // tr,),
        in_specs=[src] + [pl.BlockSpec(memory_space=pl.ANY)] * len(after),
        out_specs=pl.BlockSpec((tr, cols), lambda i: (i, 0)),
        out_shape=jax.ShapeDtypeStruct((rows, cols), BF16), compiler_params=_params(("parallel",)),
    )(w, *after)


def _me_and_peers():
    x, y, c = lax.axis_index("x"), lax.axis_index("y"), lax.axis_index("c")
    me = 4 * x + 2 * y + c
    peers = []
    for k in range(1, N_DEV):
        px = 1 - x if k & 4 else x
        py = 1 - y if k & 2 else y
        pc = 1 - c if k & 1 else c
        peers.append(((px, py, pc), 4 * px + 2 * py + pc))
    return me, peers


def _gather_on_sequencer(shards, name):
    n = len(shards)
    hbm = pltpu.MemorySpace.HBM
    ins = [jax.new_ref(s, memory_space=hbm) for s in shards]
    outs = [jax.empty_ref(jax.ShapeDtypeStruct((N_DEV,) + s.shape, s.dtype), memory_space=hbm) for s in shards]
    n_sem = 8

    @_sequencer(name, ((n, n_sem), (n, n_sem), (n,)), 0)
    def launch(send, recv, lsem):
        x, y, c = lax.axis_index("x"), lax.axis_index("y"), lax.axis_index("c")
        me, sibling = (x, y, c), (x, y, 1 - c)
        x_chip, y_chip, diagonal = (1 - x, y, c), (x, 1 - y, c), (1 - x, 1 - y, c)
        _handshake([sibling, x_chip, y_chip])

        def copy(w, k, block, to, src=None, half=None):
            px, py, pc = block
            dst = outs[w].at[4 * px + 2 * py + pc]
            if half is not None:
                rows = shards[w].shape[0] // 2
                dst = dst.at[pl.ds(half * rows, rows)]
            return pltpu.make_async_remote_copy(dst if src is None else src, dst, send.at[w, k], recv.at[w, k],
                                                device_id=to, device_id_type=MESH)

        local = [pltpu.make_async_copy(ins[w], outs[w].at[4 * x + 2 * y + c], lsem.at[w]) for w in range(n)]
        for cp in local:
            cp.start()
        sent = []
        for w in range(n):
            sent += [copy(w, 1, me, x_chip, src=ins[w]), copy(w, 2, me, y_chip, src=ins[w]),
                     copy(w, 0, me, sibling, src=ins[w])]
        for cp in sent:
            cp.start()
        for w in range(n):
            copy(w, 1, x_chip, me).wait_recv()
            sent += [copy(w, 3, x_chip, y_chip, half=0), copy(w, 5, x_chip, sibling)]
            sent[-2].start()
            sent[-1].start()
            copy(w, 2, y_chip, me).wait_recv()
            sent += [copy(w, 4, y_chip, x_chip, half=1), copy(w, 6, y_chip, sibling)]
            sent[-2].start()
            sent[-1].start()
        for w in range(n):
            copy(w, 3, diagonal, me, half=0).wait_recv()
            copy(w, 4, diagonal, me, half=1).wait_recv()
            sent.append(copy(w, 7, diagonal, sibling))
            sent[-1].start()
        for w in range(n):
            copy(w, 0, sibling, me).wait_recv()
            for k, chip in ((5, x_chip), (6, y_chip), (7, diagonal)):
                px, py, _ = chip
                copy(w, k, (px, py, 1 - c), me).wait_recv()
        for cp in sent:
            cp.wait_send()
        for cp in local:
            cp.wait()

    launch()
    return [o[...] for o in outs]


N_CHIP = 4


def _sequencer(name, n_sems, collective_id):
    return functools.partial(
        pl.kernel, mesh=plsc.ScalarSubcoreMesh(axis_name="seq", num_cores=1), name=name,
        scratch_types=tuple(pltpu.SemaphoreType.DMA(s) for s in n_sems),
        compiler_params=pltpu.CompilerParams(collective_id=collective_id))


def _handshake(peers):
    barrier = pltpu.get_barrier_semaphore()
    for peer in peers:
        pl.semaphore_signal(barrier, inc=1, device_id=peer, device_id_type=MESH)
    pl.semaphore_wait(barrier, len(peers))


def _chip_exchange_on_sequencer(parts, name):
    n = len(parts)
    hbm = pltpu.MemorySpace.HBM
    ins = [jax.new_ref(p, memory_space=hbm) for p in parts]
    outs = [jax.empty_ref(jax.ShapeDtypeStruct(p.shape, p.dtype), memory_space=hbm) for p in parts]

    @_sequencer(name, ((n, 3), (n, 3), (n,)), 2)
    def launch(send, recv, lsem):
        x, y, c = lax.axis_index("x"), lax.axis_index("y"), lax.axis_index("c")
        mine = 2 * x + y
        chips = [(1 - x, y), (x, 1 - y), (1 - x, 1 - y)]
        _handshake([(*chip, c) for chip in chips])
        local = [pltpu.make_async_copy(ins[w].at[mine], outs[w].at[mine], lsem.at[w]) for w in range(n)]
        for cp in local:
            cp.start()
        sends = []
        for w in range(n):
            for j, (px, py) in enumerate(chips):
                cp = pltpu.make_async_remote_copy(ins[w].at[2 * px + py], outs[w].at[mine],
                                                  send.at[w, j], recv.at[w, j],
                                                  device_id=(px, py, c), device_id_type=MESH)
                cp.start()
                sends.append(cp)
        for w in range(n):
            for j, (px, py) in enumerate(chips):
                pltpu.make_async_remote_copy(ins[w].at[mine], outs[w].at[2 * px + py],
                                             send.at[w, j], recv.at[w, j],
                                             device_id=(px, py, c), device_id_type=MESH).wait_recv()
        for cp in sends:
            cp.wait_send()
        for cp in local:
            cp.wait()

    launch()
    return [o[...] for o in outs]


GRAD_TILES = (dict(blocks_on="cols", tm=512, tn=1280), dict(blocks_on="cols", tm=512, tn=256),
              dict(blocks_on="cols", tm=512, tn=256), dict(blocks_on="rows", tm=256, tn=2048),
              dict(blocks_on="cols", tm=1024, tn=1024), dict(blocks_on="rows", tm=1024, tn=1024))


def _mm_tn_pair(a, b, *, blocks_on, tm, tn, name):
    t_len, m = a.shape
    n = b.shape[1]
    if blocks_on == "rows":
        rows, cols, inner = m // N_DEV, n, n // tn
        assert tm == rows
        a_spec = pl.BlockSpec((t_len, tm), lambda p, t, blk: (0, blk[p]))
        b_spec = pl.BlockSpec((t_len, tn), lambda p, t, blk: (0, t))
        out_spec = pl.BlockSpec((None, tm, tn), lambda p, t, blk: (
            jnp.maximum(p - N_CHIP, 0), 0, jnp.where(p < N_CHIP, 0, t)))
    else:
        rows, cols, inner = m, n // N_DEV, m // tm
        assert tn == cols
        a_spec = pl.BlockSpec((t_len, tm), lambda p, t, blk: (0, t))
        b_spec = pl.BlockSpec((t_len, tn), lambda p, t, blk: (0, blk[p]))
        out_spec = pl.BlockSpec((None, tm, tn), lambda p, t, blk: (
            jnp.maximum(p - N_CHIP, 0), jnp.where(p < N_CHIP, 0, t), 0))

    def body(blk_ref, a_ref, b_ref, o_ref, land, stage, send_sem, recv_sem):
        del blk_ref
        p, t = pl.program_id(0), pl.program_id(1)
        step = p * inner + t
        x, y, c = lax.axis_index("x"), lax.axis_index("y"), lax.axis_index("c")
        tile = _dot(a_ref[...], b_ref[...], TN)

        def to_sibling(slot, chip, piece):
            return pltpu.make_async_remote_copy(stage.at[slot], land.at[chip, piece], send_sem.at[slot],
                                                recv_sem.at[chip, piece],
                                                device_id=(x, y, 1 - c), device_id_type=MESH)

        @pl.when(p < N_CHIP)
        def _():
            slot = step % 2

            @pl.when(step >= 2)
            def _():
                to_sibling(slot, 0, 0).wait_send()

            stage[slot] = tile.astype(BF16)
            to_sibling(slot, p, t).start()

        @pl.when(step == N_CHIP * inner)
        def _():
            for slot in range(min(2, N_CHIP * inner)):
                to_sibling(slot, 0, 0).wait_send()

        @pl.when(p >= N_CHIP)
        def _():
            chip = p - N_CHIP
            to_sibling(0, chip, t).wait_recv()
            o_ref[...] = (tile + land[chip, t].astype(F32)).astype(BF16)

    c = lax.axis_index("c")
    order = jnp.stack([2 * ch + 1 - c for ch in range(N_CHIP)] + [2 * ch + c for ch in range(N_CHIP)])
    return pl.pallas_call(
        body, name=name,
        grid_spec=pltpu.PrefetchScalarGridSpec(
            num_scalar_prefetch=1, grid=(N_DEV, inner), in_specs=[a_spec, b_spec], out_specs=out_spec,
            scratch_shapes=[pltpu.VMEM((N_CHIP, inner, tm, tn), BF16), pltpu.VMEM((2, tm, tn), BF16),
                            pltpu.SemaphoreType.DMA((2,)), pltpu.SemaphoreType.DMA((N_CHIP, inner))]),
        out_shape=jax.ShapeDtypeStruct((N_CHIP, rows, cols), BF16),
        compiler_params=_params(("arbitrary", "arbitrary")),
    )(order.astype(jnp.int32), a, b)


def _adamw_math(g, w, m, v):
    m2 = B1 * m + (1.0 - B1) * g
    v2 = B2 * v + (1.0 - B2) * (g * g)
    delta = -LR * ((m2 / BC1) / (jnp.sqrt(v2 / BC2) + AEPS) + WD * w)
    return delta, m2, v2


def _adamw_block(ins, outs):
    p_ref, w_ref, m_ref, v_ref = ins
    g = p_ref[0].astype(F32)
    for b in range(1, N_CHIP):
        g = g + p_ref[b].astype(F32)
    delta, m2, v2 = _adamw_math(g, w_ref[...], m_ref[...], v_ref[...])
    for ref, val in zip(outs, (g, delta, m2, v2)):
        ref[...] = val


class _Rider(NamedTuple):
    inputs: tuple
    in_specs: list
    out_shape: list
    out_specs: list
    body: Callable


def _adamw_rider(parts, w, m, v):
    rows, cols = w.shape

    def rider(steps, step_of):
        rr = rows // steps
        blk = pl.BlockSpec((rr, cols), lambda *ids: (step_of(*ids[:2]), 0))
        chips = pl.BlockSpec((N_CHIP, rr, cols), lambda *ids: (0, step_of(*ids[:2]), 0))
        shape = jax.ShapeDtypeStruct((rows, cols), F32)
        return _Rider((parts, w, m, v), [chips, blk, blk, blk], [shape] * 4, [blk] * 4, _adamw_block)

    return rider


def _adamw(parts, w, m, v, *, name, after=(), tr=256):
    rows, cols = w.shape

    def body(*refs):
        _adamw_block(refs[:4], refs[4 + len(after):])

    spec = pl.BlockSpec((tr, cols), lambda i: (i, 0))
    shape = jax.ShapeDtypeStruct((rows, cols), F32)
    return pl.pallas_call(
        body, name=name, grid=(rows // tr,),
        in_specs=[pl.BlockSpec((N_CHIP, tr, cols), lambda i: (0, i, 0)), spec, spec, spec]
        + [pl.BlockSpec(memory_space=pl.ANY)] * len(after),
        out_specs=[spec] * 4, out_shape=[shape] * 4,
        compiler_params=_params(("parallel",)),
    )(parts, w, m, v, *after)


def _small_exchange(part, after=()):
    rows = part.shape[0]

    def body(p_ref, *rest):
        g_ref, buf, send, recv = rest[len(after):]
        me, peers = _me_and_peers()
        buf[me] = p_ref[...]
        sends = []
        for k, (dev, _) in enumerate(peers):
            cp = pltpu.make_async_remote_copy(p_ref, buf.at[me], send.at[k], recv.at[k],
                                              device_id=dev, device_id_type=MESH)
            cp.start()
            sends.append(cp)
        for k, (dev, idx) in enumerate(peers):
            pltpu.make_async_remote_copy(p_ref, buf.at[idx], send.at[k], recv.at[k],
                                         device_id=dev, device_id_type=MESH).wait_recv()
        for cp in sends:
            cp.wait_send()
        g = buf[0]
        for b in range(1, N_DEV):
            g = g + buf[b]
        g_ref[...] = g

    vm = pl.BlockSpec(memory_space=pltpu.VMEM)
    return pl.pallas_call(
        body, name="small_params_exchange",
        in_specs=[vm] + [pl.BlockSpec(memory_space=pl.ANY)] * len(after),
        out_specs=vm, out_shape=jax.ShapeDtypeStruct((rows, HD), F32),
        scratch_shapes=[pltpu.VMEM((N_DEV, rows, HD), F32),
                        pltpu.SemaphoreType.DMA((N_DEV - 1,)), pltpu.SemaphoreType.DMA((N_DEV - 1,))],
    )(part, *after)


def _small_adamw(g, w, m, v):
    def body(g_ref, w_ref, m_ref, v_ref, *outs):
        g = g_ref[...]
        delta, m2, v2 = _adamw_math(g, w_ref[...], m_ref[...], v_ref[...])
        for k, val in enumerate((g, delta, m2, v2)):
            norm_mix, b_gate, qa, ka, qb, kb, rpb, norm_ffn = outs[8 * k:8 * k + 8]
            for dst, row0, n_rows in ((norm_mix, 0, 16), (b_gate, 16, 32), (norm_ffn, 120, 16)):
                for r in range(n_rows):
                    dst[:, r * HD:(r + 1) * HD] = val[row0 + r:row0 + r + 1, :]
            for i, dst in enumerate((qa, ka, qb, kb)):
                dst[...] = val[48 + i:49 + i, :]
            rpb[...] = val[56:120, :]
        outs[32][...] = g[LOSS_ROW:LOSS_ROW + 1, 0:1]

    vm = pl.BlockSpec(memory_space=pltpu.VMEM)
    kinds = [jax.ShapeDtypeStruct(sh, F32) for sh in
             ((1, D), (1, 2 * D), (1, HD), (1, HD), (1, HD), (1, HD), (64, HD), (1, D))]
    outs = pl.pallas_call(
        body, name="small_params_adamw", in_specs=[vm] * 4, out_specs=[vm] * 33,
        out_shape=kinds * 4 + [jax.ShapeDtypeStruct((1, 1), F32)],
    )(g, w, m, v)
    return [outs[8 * k:8 * k + 8] for k in range(4)], outs[32]


def _pack_small(norm_mix, b_gate, qa, ka, qb, kb, rpb, norm_ffn):
    gains = jnp.concatenate([qa, ka, qb, kb, jnp.zeros((4, HD), F32)], axis=0)
    rpb_pad = jnp.pad(rpb.reshape(4 * (2 * WIN_R - 1), 2 * WIN_C - 1), ((0, 4), (0, HD - (2 * WIN_C - 1))))
    return jnp.concatenate([norm_mix.reshape(16, HD), b_gate.reshape(32, HD), gains, rpb_pad,
                            norm_ffn.reshape(16, HD), jnp.zeros((8, HD), F32)], axis=0)


LOSS_ROW = 136


def _rpb_from_rows(rows):
    return rows[:60, :2 * WIN_C - 1].reshape(1, 4, 2 * WIN_R - 1, 2 * WIN_C - 1)


def kernel(x, norm_mix, w_in, b_gate, q_norm_a, k_norm_a, q_norm_b, k_norm_b, rpb_b, w_proj_a, w_proj_b, w_out, norm_ffn, w_up, w_down, loss_target, m_norm_mix, m_w_in, m_b_gate, m_q_norm_a, m_k_norm_a, m_q_norm_b, m_k_norm_b, m_rpb_b, m_w_proj_a, m_w_proj_b, m_w_out, m_norm_ffn, m_w_up, m_w_down, v_norm_mix, v_w_in, v_b_gate, v_q_norm_a, v_k_norm_a, v_q_norm_b, v_k_norm_b, v_rpb_b, v_w_proj_a, v_w_proj_b, v_w_out, v_norm_ffn, v_w_up, v_w_down):
    big_w = (w_in[0], w_proj_a[0], w_proj_b[0], w_out[0], w_up[0], w_down[0])
    big_m = (m_w_in[0], m_w_proj_a[0], m_w_proj_b[0], m_w_out[0], m_w_up[0], m_w_down[0])
    big_v = (v_w_in[0], v_w_proj_a[0], v_w_proj_b[0], v_w_out[0], v_w_up[0], v_w_down[0])
    names = ("w_in", "w_proj_a", "w_proj_b", "w_out", "w_up", "w_down")

    g_in = [_gather_on_sequencer([_cast_bf16(big_w[0], window=win)], f"gather_w_in_{k}")[0]
            for k, win in enumerate(((0, W_IN_SPLIT), (W_IN_SPLIT, D_IN // N_DEV - W_IN_SPLIT)))]
    shards = [None] + [_cast_bf16(w) for w in big_w[1:5]]
    g_pa, g_pb, g_out, g_up = _gather_on_sequencer(shards[1:5], "gather_w_mix_up")
    small_w = _pack_small(norm_mix, b_gate, q_norm_a, k_norm_a, q_norm_b, k_norm_b, rpb_b, norm_ffn)
    small_m = _pack_small(m_norm_mix, m_b_gate, m_q_norm_a, m_k_norm_a, m_q_norm_b, m_k_norm_b, m_rpb_b, m_norm_ffn)
    small_v = _pack_small(v_norm_mix, v_b_gate, v_q_norm_a, v_k_norm_a, v_q_norm_b, v_k_norm_b, v_rpb_b, v_norm_ffn)
    g_down = [_gather_on_sequencer([_cast_bf16(big_w[5], part=h, parts=2, after=(small_w, small_m, small_v) * h)],
                                   f"gather_w_down_{h}")[0].reshape(1, D_FF, D // 2) for h in range(2)]

    upd = [None] * 6
    in_flight = {}

    def weight_grads(tag, operands):
        sums = {i: _mm_tn_pair(a, b, name=f"grad_{names[i]}", **GRAD_TILES[i]) for i, (a, b) in operands.items()}
        new = list(sums.values())
        in_flight.update(zip(sums, _chip_exchange_on_sequencer(new, f"chip_exchange_{tag}")))
        return new

    def riders(name):
        i = {"proj_bwd": 5}.get(name)
        if i is None:
            return None
        return (_adamw_rider(in_flight.pop(i), big_w[i], big_m[i], big_v[i]),
                functools.partial(upd.__setitem__, i))

    loss, grad_x, small_g = _local_step(
        x[0], loss_target[0], norm_mix, b_gate, small_w[48:56], small_w[56:120], norm_ffn,
        g_in, g_pa, g_pb, g_out.reshape(D, D), g_up, g_down, weight_grads, riders)

    g_norm_mix, g_b, g_gains, g_rpb, g_norm_ffn = small_g
    small_part = jnp.concatenate([g_norm_mix.reshape(16, HD), g_b.reshape(32, HD),
                                  g_gains, g_rpb, g_norm_ffn.reshape(16, HD),
                                  jnp.pad(loss, ((0, 7), (0, HD - 1)))], axis=0)
    last = grad_x
    for i, r in in_flight.items():
        if i == 0:
            small_sum = _small_exchange(small_part, after=[last])
            small, total = _small_adamw(small_sum, small_w, small_m, small_v)
            last = total
        upd[i] = _adamw(r, big_w[i], big_m[i], big_v[i], name=f"adamw_{names[i]}", after=[last])
        last = upd[i][0]
    s_g, s_d, s_m, s_v = ((*k[:6], _rpb_from_rows(k[6]), k[7]) for k in small)
    b_g, b_d, b_m, b_v = ([u[j][None] for u in upd] for j in range(4))

    def order(small, big):
        nm, bg, qa, ka, qb, kb, rpb, nf = small
        w_in_, pa_, pb_, out_, up_, down_ = big
        return (nm, w_in_, bg, qa, ka, qb, kb, rpb, pa_, pb_, out_, nf, up_, down_)

    return (total[0, 0], grad_x[None], *order(s_g, b_g), *order(s_d, b_d), *order(s_m, b_m), *order(s_v, b_v))
```

```python
import functools
from typing import Callable, NamedTuple

import jax
import jax.numpy as jnp
import numpy as np
from jax import lax
from jax.experimental import pallas as pl
from jax.experimental.pallas import tpu as pltpu
from jax.experimental.pallas import tpu_sc as plsc

F32 = jnp.float32
BF16 = jnp.bfloat16

N_DEV = 8
S = 2048
D = 2048
HD = 128
NH = 16
NH_A = 12
QKV = NH * HD
D_IN = 3 * QKV + 2 * D
D_BR = 512
D_FF = 4 * D
GRID_W = 64
ROWS = S // GRID_W
WIN_R = 8
WIN_C = 16
EPS = 1e-6
NEG = -1e30
SCALE = HD ** -0.5
ROPE_THETA = 10000.0
DILATIONS = (1, 4, 16)
HALF_A = 64
QB = 128
W_IN_SPLIT = 768

LR, B1, B2, AEPS, WD, STEP = 0.001, 0.9, 0.999, 1e-08, 0.01, 10
BC1 = 1.0 - B1 ** STEP
BC2 = 1.0 - B2 ** STEP

VMEM_LIMIT = 56 * 1024 * 1024
MESH = pl.DeviceIdType.MESH

NN = (((1,), (0,)), ((), ()))
NT = (((1,), (1,)), ((), ()))
TN = (((0,), (0,)), ((), ()))


def _params(sem):
    return pltpu.CompilerParams(dimension_semantics=sem, vmem_limit_bytes=VMEM_LIMIT)


def _matmul(a, b, *, product, grid, a_spec, b_spec, epi, out_shape, out_specs, name,
            extra=(), extra_specs=(), after=(), carried=False, rider=None, into=()):
    n_extra = len(extra)
    single = not isinstance(out_shape, (list, tuple))
    out_shape = [out_shape] if single else list(out_shape)
    out_specs = [out_specs] if single else list(out_specs)
    ride = rider(grid[0] * grid[1], lambda j, i: j * grid[1] + i) if rider else None
    r_in = list(ride.inputs) if ride else []
    n_main = len(out_shape)

    def body(a_ref, b_ref, *rest):
        n_in = n_extra + len(after) + len(r_in)
        ins, outs = rest[:n_in], rest[n_in + len(into):]
        epi(product(a_ref, b_ref, ins[:n_extra]), ins[:n_extra], outs[:n_main])
        if ride:
            ride.body(ins[n_extra + len(after):], outs[n_main:])

    res = pl.pallas_call(
        body, name=name, grid=grid,
        in_specs=[a_spec, b_spec, *extra_specs, *[pl.BlockSpec(memory_space=pl.ANY)] * len(after),
                  *(ride.in_specs if ride else []), *[pl.BlockSpec(memory_space=pl.ANY)] * len(into)],
        out_specs=out_specs + (ride.out_specs if ride else []),
        out_shape=out_shape + (ride.out_shape if ride else []),
        input_output_aliases={2 + n_extra + len(after) + len(r_in) + k: k for k in range(len(into))},
        compiler_params=_params(("arbitrary", "arbitrary") if carried else ("parallel", "parallel")),
    )(a, b, *extra, *after, *r_in, *into)
    main = res[0] if single else res[:n_main]
    return (main, res[n_main:]) if ride else main


def _dot(x, y, dims):
    return lax.dot_general(x, y, dims, preferred_element_type=F32)


def _epi_store(acc, ex, outs):
    outs[0][...] = acc.astype(outs[0].dtype)


def _epi_residual(acc, ex, outs):
    outs[0][...] = acc + ex[0][...]


def _mm_nn(a, b3, *, tm, tn, name, out_dtypes=(F32,), epi=_epi_store, extra=(), total=False,
           col0=0, width=None, into=(), stride=None):
    m, kdim = a.shape
    g, _, ng = b3.shape
    n = g * ng
    c0 = col0 // tn
    if tn <= ng:
        npg = ng // tn
        b_spec = pl.BlockSpec((None, kdim, tn), lambda j, i: (j // npg, 0, j % npg))

        def product(a_ref, b_ref, ex):
            return _dot(a_ref[...], b_ref[...], NN)
    else:
        gb = tn // ng
        b_spec = pl.BlockSpec((gb, kdim, ng), lambda j, i: (j, 0, 0))

        def product(a_ref, b_ref, ex):
            return jnp.concatenate([_dot(a_ref[...], b_ref[q], NN) for q in range(gb)], axis=1)

    tile = pl.BlockSpec((tm, tn), lambda j, i: (i, j + c0))
    if stride is not None:
        assert tn == ng and not extra
        tile = pl.BlockSpec((pl.Element(tm), pl.Element(tn)),
                            lambda j, i: (i * tm, pl.multiple_of(j * stride + col0, 128)))
    shapes = [jax.ShapeDtypeStruct((m, width or n), dt) for dt in out_dtypes]
    specs = [tile] * len(shapes)
    if total:
        shapes.append(jax.ShapeDtypeStruct((1, 1), F32))
        specs.append(pl.BlockSpec((1, 1), lambda j, i: (0, 0)))
    single = len(shapes) == 1
    return _matmul(
        a, b3, product=product, grid=(n // tn, m // tm), epi=epi, name=name, carried=total, into=into,
        a_spec=pl.BlockSpec((tm, kdim), lambda j, i: (i, 0)), b_spec=b_spec,
        extra=extra, extra_specs=[tile] * len(extra),
        out_shape=shapes[0] if single else shapes, out_specs=specs[0] if single else specs)


def _mm_nt(a, b3, *, tm, tn, name, out_dtype=F32, epi=_epi_store, extra=(), after=(), rider=None, more_b=(),
           interleaved=False):
    m, kdim = a.shape
    _, n, _ = b3.shape
    n_b = len(more_b)

    def product(a_ref, b_ref, ex):
        refs = (b_ref, *ex[:n_b])
        pieces = ([(ref, q) for q in range(b_ref.shape[0]) for ref in refs] if interleaved
                  else [(ref, q) for ref in refs for q in range(ref.shape[0])])
        acc, k0 = None, 0
        for ref, q in pieces:
            part = _dot(a_ref[:, k0:k0 + ref.shape[2]], ref[q], NT)
            acc = part if acc is None else acc + part
            k0 += ref.shape[2]
        return acc

    def write(acc, ex, outs):
        epi(acc, ex[n_b:], outs)

    def w_spec(w):
        return pl.BlockSpec((w.shape[0], tn, w.shape[2]), lambda j, i: (0, j, 0))

    tile = pl.BlockSpec((tm, tn), lambda j, i: (i, j))
    return _matmul(
        a, b3, product=product, grid=(n // tn, m // tm), epi=write, name=name,
        a_spec=pl.BlockSpec((tm, kdim), lambda j, i: (i, 0)), b_spec=w_spec(b3),
        extra=(*more_b, *extra), extra_specs=[w_spec(w) for w in more_b] + [tile] * len(extra),
        after=after, rider=rider,
        out_shape=jax.ShapeDtypeStruct((m, n), out_dtype), out_specs=tile)


def _mm_tn(a, b, *, tm, tn, name, groups=1, out_dtype=BF16):
    t, m = a.shape
    _, n = b.shape
    ng = n // groups
    if tn <= ng:
        npg = ng // tn
        out_spec = pl.BlockSpec((None, tm, tn), lambda j, i: (j // npg, i, j % npg))
        epi = _epi_store

        def product(a_ref, b_ref, ex):
            return _dot(a_ref[...], b_ref[...], TN)
    else:
        gb = tn // ng
        out_spec = pl.BlockSpec((gb, tm, ng), lambda j, i: (j, i, 0))

        def product(a_ref, b_ref, ex):
            return [_dot(a_ref[...], b_ref[:, q * ng:(q + 1) * ng], TN) for q in range(gb)]

        def epi(parts, ex, outs):
            for q, part in enumerate(parts):
                outs[0][q] = part.astype(out_dtype)

    return _matmul(
        a, b, product=product, grid=(n // tn, m // tm), epi=epi, name=name,
        a_spec=pl.BlockSpec((t, tm), lambda j, i: (0, i)),
        b_spec=pl.BlockSpec((t, tn), lambda j, i: (0, j)),
        out_shape=jax.ShapeDtypeStruct((groups, m, ng), out_dtype), out_specs=out_spec)


def _rms_fwd(x, g, *, name, tr=256):
    def body(x_ref, g_ref, y_ref, r_ref):
        xv = x_ref[...]
        r = lax.rsqrt(jnp.mean(xv * xv, axis=-1, keepdims=True) + EPS)
        y_ref[...] = (xv * r * g_ref[...]).astype(BF16)
        r_ref[...] = r

    row = pl.BlockSpec((tr, D), lambda i: (i, 0))
    return pl.pallas_call(
        body, name=name, grid=(S // tr,),
        in_specs=[row, pl.BlockSpec((1, D), lambda i: (0, 0))],
        out_specs=[row, pl.BlockSpec((tr, 1), lambda i: (i, 0))],
        out_shape=[jax.ShapeDtypeStruct((S, D), BF16), jax.ShapeDtypeStruct((S, 1), F32)],
        compiler_params=_params(("parallel",)),
    )(x, g)


def _rms_bwd(dy, x, rstd, g, resid, *, name, bf16_copy, tr=256):
    def body(dy_ref, x_ref, r_ref, g_ref, res_ref, dx_ref, *rest):
        dg_ref = rest[-1]
        r = r_ref[...]
        xh = x_ref[...] * r
        dyv = dy_ref[...]
        t = dyv * g_ref[...]
        dx = r * (t - xh * jnp.mean(t * xh, axis=-1, keepdims=True)) + res_ref[...]
        dx_ref[...] = dx
        if bf16_copy:
            rest[0][...] = dx.astype(BF16)
        part = jnp.sum(dyv * xh, axis=0, keepdims=True)

        @pl.when(pl.program_id(0) == 0)
        def _():
            dg_ref[...] = part

        @pl.when(pl.program_id(0) > 0)
        def _():
            dg_ref[...] += part

    row = pl.BlockSpec((tr, D), lambda i: (i, 0))
    vec = pl.BlockSpec((1, D), lambda i: (0, 0))
    return pl.pallas_call(
        body, name=name, grid=(S // tr,),
        in_specs=[row, row, pl.BlockSpec((tr, 1), lambda i: (i, 0)), vec, row],
        out_specs=[row] + [row] * bf16_copy + [vec],
        out_shape=[jax.ShapeDtypeStruct((S, D), F32)] + [jax.ShapeDtypeStruct((S, D), BF16)] * bf16_copy
        + [jax.ShapeDtypeStruct((1, D), F32)],
        compiler_params=_params(("arbitrary",)),
    )(dy, x, rstd, g, resid)


def _rope_tables():
    pos = np.arange(S, dtype=np.float32)
    inv = (ROPE_THETA ** (-np.arange(0, HD, 2, dtype=np.float32) / HD)).astype(np.float32)
    ang = pos[:, None] * inv[None, :]
    cos, sin = np.cos(ang), np.sin(ang)
    return (jnp.asarray(np.concatenate([cos, cos], axis=-1), F32),
            jnp.asarray(np.concatenate([-sin, sin], axis=-1), F32))


def _swap_halves(t):
    return pltpu.roll(t, HD // 2, axis=1)


TOK = 256


def _lane_block_spec(d, last=HD):
    return pl.BlockSpec((4, TOK // d, d * last), lambda i: (0, i, 0))


def _to_lane_blocks(dst, head, val, d, scr, dtype):
    w = val.shape[1]
    if d == 1:
        dst[head] = val.astype(dtype)
        return
    scr[...] = val
    for r in range(d):
        dst[head, :, r * w:(r + 1) * w] = scr[pl.ds(r, TOK // d, stride=d), :].astype(dtype)


def _from_lane_blocks(src, head, d, w, scr):
    if d == 1:
        return src[head].astype(F32)
    for r in range(d):
        scr[pl.ds(r, TOK // d, stride=d), :] = src[head, :, r * w:(r + 1) * w].astype(F32)
    return scr[...]


def _qk_prep(proj, gains, cos2, sin2):
    def body(q_ref, k_ref, v_ref, g_ref, c_ref, s_ref, *rest):
        outs, scr = rest[:-1], rest[-1]
        cos, sin = c_ref[...], s_ref[...]
        for which, (src, row_a, row_b) in enumerate(((q_ref, 0, 2), (k_ref, 1, 3), (v_ref, None, None))):
            for h in range(NH):
                y = src[:, h * HD:(h + 1) * HD]
                if row_a is not None:
                    y = y * lax.rsqrt(jnp.mean(y * y, axis=-1, keepdims=True) + EPS)
                    if h < NH_A:
                        y = y * g_ref[row_a:row_a + 1, :]
                        y = y * cos + _swap_halves(y) * sin
                    else:
                        y = y * g_ref[row_b:row_b + 1, :]
                if h < NH_A:
                    gi = h // 4
                    _to_lane_blocks(outs[3 * gi + which], h % 4, y, DILATIONS[gi], scr, BF16)
                else:
                    hb = h - NH_A
                    outs[9 + which][:, hb * HD:(hb + 1) * HD] = y.astype(BF16)

    def blk(c):
        return pl.BlockSpec((TOK, QKV), lambda i: (i, c))
    tab = pl.BlockSpec((TOK, HD), lambda i: (i, 0))
    out_specs, out_shape = [], []
    for d in DILATIONS:
        out_specs += [_lane_block_spec(d)] * 3
        out_shape += [jax.ShapeDtypeStruct((4, S // d, d * HD), BF16)] * 3
    out_specs += [pl.BlockSpec((TOK, D_BR), lambda i: (i, 0))] * 3
    out_shape += [jax.ShapeDtypeStruct((S, D_BR), BF16)] * 3
    outs = pl.pallas_call(
        body, name="qk_prep", grid=(S // TOK,),
        in_specs=[blk(0), blk(1), blk(2), pl.BlockSpec((8, HD), lambda i: (0, 0)), tab, tab],
        out_specs=out_specs, out_shape=out_shape,
        scratch_shapes=[pltpu.VMEM((TOK, HD), F32)],
        compiler_params=_params(("parallel",)),
    )(proj, proj, proj, gains, cos2, sin2)
    return [tuple(outs[3 * gi:3 * gi + 3]) for gi in range(3)], tuple(outs[9:12])


def _qk_prep_bwd(dproj, proj, gains, cos2, sin2, grads_a, grads_b):
    def body(dp_in, q_ref, k_ref, g_ref, c_ref, s_ref, *rest):
        grads, (dp_out, dg_ref, scr) = rest[:12], rest[12:]
        del dp_in
        cos, sin = c_ref[...], s_ref[...]

        def grad_of(which, h):
            if h < NH_A:
                gi = h // 4
                return _from_lane_blocks(grads[3 * gi + which], h % 4, DILATIONS[gi], HD, scr)
            hb = h - NH_A
            return grads[9 + which][:, hb * HD:(hb + 1) * HD].astype(F32)

        dg_rows = []
        for which, (src, base, row_a, row_b) in enumerate(((q_ref, 0, 0, 2), (k_ref, QKV, 1, 3))):
            dg_a = jnp.zeros((1, HD), F32)
            dg_b = jnp.zeros((1, HD), F32)
            for h in range(NH):
                t = src[:, h * HD:(h + 1) * HD]
                dy = grad_of(which, h)
                r = lax.rsqrt(jnp.mean(t * t, axis=-1, keepdims=True) + EPS)
                xh = t * r
                if h < NH_A:
                    dy = dy * cos - _swap_halves(dy) * sin
                    gain = g_ref[row_a:row_a + 1, :]
                    dg_a = dg_a + jnp.sum(dy * xh, axis=0, keepdims=True)
                else:
                    gain = g_ref[row_b:row_b + 1, :]
                    dg_b = dg_b + jnp.sum(dy * xh, axis=0, keepdims=True)
                u = dy * gain
                dx = r * (u - xh * jnp.mean(u * xh, axis=-1, keepdims=True))
                dp_out[:, base + h * HD:base + (h + 1) * HD] = dx.astype(BF16)
            dg_rows += [(row_a, dg_a), (row_b, dg_b)]
        for h in range(NH):
            dp_out[:, 2 * QKV + h * HD:2 * QKV + (h + 1) * HD] = grad_of(2, h).astype(BF16)

        @pl.when(pl.program_id(0) == 0)
        def _():
            dg_ref[...] = jnp.zeros((8, HD), F32)

        for row, val in dg_rows:
            dg_ref[row:row + 1, :] += val

    def blk(c):
        return pl.BlockSpec((TOK, QKV), lambda i: (i, c))
    tab = pl.BlockSpec((TOK, HD), lambda i: (i, 0))
    gain_spec = pl.BlockSpec((8, HD), lambda i: (0, 0))
    grad_specs = [s for d in DILATIONS for s in [_lane_block_spec(d)] * 3]
    grad_specs += [pl.BlockSpec((TOK, D_BR), lambda i: (i, 0))] * 3
    return pl.pallas_call(
        body, name="qk_prep_bwd", grid=(S // TOK,),
        in_specs=[pl.BlockSpec(memory_space=pl.ANY), blk(0), blk(1), gain_spec, tab, tab] + grad_specs,
        out_specs=[pl.BlockSpec((TOK, 3 * QKV), lambda i: (i, 0)), gain_spec],
        out_shape=[jax.ShapeDtypeStruct((S, D_IN), BF16), jax.ShapeDtypeStruct((8, HD), F32)],
        input_output_aliases={0: 0},
        scratch_shapes=[pltpu.VMEM((TOK, HD), F32)],
        compiler_params=_params(("arbitrary",)),
    )(dproj, proj, proj, gains, cos2, sin2, *[g for grp in grads_a for g in grp], *grads_b)


def _mix_fwd(oa, ob, w_pa, w_pb, proj, b_gate, *, tr=256):
    def body(oa_ref, ob_ref, pa_ref, pb_ref, la_ref, lb_ref, ba_ref, bb_ref, mix_ref, ya_ref, yb_ref):
        ya = jnp.concatenate([_dot(oa_ref[...], pa_ref[q], NN) for q in range(N_DEV)], axis=1)
        yb = jnp.concatenate([_dot(ob_ref[...], pb_ref[q], NN) for q in range(N_DEV)], axis=1)
        ga = jax.nn.sigmoid(la_ref[...] + ba_ref[...])
        gb = jax.nn.sigmoid(lb_ref[...] + bb_ref[...])
        mix_ref[...] = (ga * ya + gb * yb).astype(BF16)
        ya_ref[...] = ya.astype(BF16)
        yb_ref[...] = yb.astype(BF16)

    row = pl.BlockSpec((tr, D), lambda i: (i, 0))
    branch = pl.BlockSpec((tr, D_BR), lambda i: (i, 0))
    whole = pl.BlockSpec((N_DEV, D_BR, D // N_DEV), lambda i: (0, 0, 0))
    return pl.pallas_call(
        body, name="mix_fwd", grid=(S // tr,),
        in_specs=[branch, branch, whole, whole,
                  pl.BlockSpec((tr, D), lambda i: (i, 3)), pl.BlockSpec((tr, D), lambda i: (i, 4)),
                  pl.BlockSpec((1, D), lambda i: (0, 0)), pl.BlockSpec((1, D), lambda i: (0, 1))],
        out_specs=[row, row, row], out_shape=[jax.ShapeDtypeStruct((S, D), BF16)] * 3,
        compiler_params=_params(("parallel",)),
    )(oa, ob, w_pa, w_pb, proj, proj, b_gate, b_gate)


def _mix_bwd(dh1b, w_out, proj, b_gate, ya, yb, *, tr=256):
    def body(dh_ref, w_ref, la_ref, lb_ref, b_ref, ya_ref, yb_ref, dya_ref, dyb_ref, dp_ref, db_ref):
        dm = _dot(dh_ref[...], w_ref[...], NT)
        parts = []
        for l_ref, y_ref, dy_ref, lo in ((la_ref, ya_ref, dya_ref, 0), (lb_ref, yb_ref, dyb_ref, D)):
            g = jax.nn.sigmoid(l_ref[...] + b_ref[:, lo:lo + D])
            dy_ref[...] = (dm * g).astype(BF16)
            dl = dm * y_ref[...].astype(F32) * g * (1.0 - g)
            dp_ref[:, lo:lo + D] = dl.astype(BF16)
            parts.append(jnp.sum(dl, axis=0, keepdims=True))
        part = jnp.concatenate(parts, axis=1)

        @pl.when(pl.program_id(0) == 0)
        def _():
            db_ref[...] = part

        @pl.when(pl.program_id(0) > 0)
        def _():
            db_ref[...] += part

    row = pl.BlockSpec((tr, D), lambda i: (i, 0))
    vec = pl.BlockSpec((1, 2 * D), lambda i: (0, 0))
    gate_cols = pl.BlockSpec((pl.Element(tr), pl.Element(2 * D)), lambda i: (i * tr, 3 * QKV))
    return pl.pallas_call(
        body, name="mix_bwd", grid=(S // tr,),
        in_specs=[row, pl.BlockSpec((D, D), lambda i: (0, 0)),
                  pl.BlockSpec((tr, D), lambda i: (i, 3)), pl.BlockSpec((tr, D), lambda i: (i, 4)), vec, row, row],
        out_specs=[row, row, gate_cols, vec],
        out_shape=[jax.ShapeDtypeStruct((S, D), BF16), jax.ShapeDtypeStruct((S, D), BF16),
                   jax.ShapeDtypeStruct((S, D_IN), BF16), jax.ShapeDtypeStruct((1, 2 * D), F32)],
        compiler_params=_params(("arbitrary",)),
    )(dh1b, w_out, proj, proj, b_gate, ya, yb)


def _band_blocks(m_len):
    wk = min(m_len, QB + 2 * QB)
    return [(qb * QB, min(max(qb * QB - QB, 0), m_len - wk), wk) for qb in range(m_len // QB)]


def _band_scores(q, kw, q0, k0, wk):
    s = _dot(q, kw, NT) * SCALE
    qpos = q0 + lax.broadcasted_iota(jnp.int32, (QB, 1), 0)
    kpos = k0 + lax.broadcasted_iota(jnp.int32, (1, wk), 1)
    return jnp.where(jnp.abs(kpos - qpos) <= HALF_A, s, NEG)


def _attn_a_fwd(q, k, v, gi):
    d = DILATIONS[gi]
    m_len = S // d

    def body(q_ref, k_ref, v_ref, o_ref, lse_ref):
        for r in range(d):
            lanes = slice(r * HD, (r + 1) * HD)
            for q0, k0, wk in _band_blocks(m_len):
                s = _band_scores(q_ref[q0:q0 + QB, lanes], k_ref[k0:k0 + wk, lanes], q0, k0, wk)
                m = jnp.max(s, axis=-1, keepdims=True)
                p = jnp.exp(s - m)
                l = jnp.sum(p, axis=-1, keepdims=True)
                o_ref[q0:q0 + QB, lanes] = _dot(p.astype(BF16), v_ref[k0:k0 + wk, lanes], NN) / l
                lse_ref[q0:q0 + QB, r:r + 1] = m + jnp.log(l)

    head = pl.BlockSpec((None, m_len, d * HD), lambda h: (h, 0, 0))
    stat = pl.BlockSpec((None, m_len, d), lambda h: (h, 0, 0))
    return pl.pallas_call(
        body, name=f"attn_a_fwd_{gi}", grid=(4,),
        in_specs=[head, head, head], out_specs=[head, stat],
        out_shape=[jax.ShapeDtypeStruct((4, m_len, d * HD), F32), jax.ShapeDtypeStruct((4, m_len, d), F32)],
        compiler_params=_params(("parallel",)),
    )(q, k, v)


def _combine_a(os, lses):
    def body(o0, o1, o2, l0, l1, l2, oa_ref, lse_ref, scr, scr1):
        for h in range(4):
            o = [_from_lane_blocks(ref, h, d, HD, scr) for ref, d in zip((o0, o1, o2), DILATIONS)]
            a, b, c = (_from_lane_blocks(ref, h, d, 1, scr1) for ref, d in zip((l0, l1, l2), DILATIONS))
            m = jnp.maximum(jnp.maximum(a, b), c)
            wa, wb, wc = jnp.exp(a - m), jnp.exp(b - m), jnp.exp(c - m)
            tot = wa + wb + wc
            oa_ref[:, h * HD:(h + 1) * HD] = ((wa * o[0] + wb * o[1] + wc * o[2]) / tot).astype(BF16)
            lse_ref[h] = m + jnp.log(tot)

    return pl.pallas_call(
        body, name="combine_a", grid=(S // TOK,),
        in_specs=[_lane_block_spec(d) for d in DILATIONS] + [_lane_block_spec(d, 1) for d in DILATIONS],
        out_specs=[pl.BlockSpec((TOK, D_BR), lambda i: (i, 0)), pl.BlockSpec((4, TOK, 1), lambda i: (0, i, 0))],
        out_shape=[jax.ShapeDtypeStruct((S, D_BR), BF16), jax.ShapeDtypeStruct((4, S, 1), F32)],
        scratch_shapes=[pltpu.VMEM((TOK, HD), F32), pltpu.VMEM((TOK, 1), F32)],
        compiler_params=_params(("parallel",)),
    )(*os, *lses)


def _proj_a_bwd(dya, w_pa, oa, lse):
    kg = D // N_DEV

    def body(dy_ref, w_ref, o_ref, l_ref, *rest):
        outs, (scr, scr1) = rest[:9], rest[9:]
        doa = _dot(dy_ref[:, 0:kg], w_ref[0], NT)
        for q in range(1, N_DEV):
            doa = doa + _dot(dy_ref[:, q * kg:(q + 1) * kg], w_ref[q], NT)
        for h in range(4):
            do = doa[:, h * HD:(h + 1) * HD]
            dsum = jnp.sum(do * o_ref[:, h * HD:(h + 1) * HD].astype(F32), axis=-1, keepdims=True)
            for gi, d in enumerate(DILATIONS):
                _to_lane_blocks(outs[3 * gi], h, do, d, scr, BF16)
                _to_lane_blocks(outs[3 * gi + 1], h, l_ref[h], d, scr1, F32)
                _to_lane_blocks(outs[3 * gi + 2], h, dsum, d, scr1, F32)

    row = pl.BlockSpec((TOK, D_BR), lambda i: (i, 0))
    out_specs, out_shape = [], []
    for d in DILATIONS:
        out_specs += [_lane_block_spec(d), _lane_block_spec(d, 1), _lane_block_spec(d, 1)]
        out_shape += [jax.ShapeDtypeStruct((4, S // d, d * HD), BF16)] + [jax.ShapeDtypeStruct((4, S // d, d), F32)] * 2
    outs = pl.pallas_call(
        body, name="proj_a_bwd", grid=(S // TOK,),
        in_specs=[pl.BlockSpec((TOK, D), lambda i: (i, 0)),
                  pl.BlockSpec((N_DEV, D_BR, kg), lambda i: (0, 0, 0)),
                  row, pl.BlockSpec((4, TOK, 1), lambda i: (0, i, 0))],
        out_specs=out_specs, out_shape=out_shape,
        scratch_shapes=[pltpu.VMEM((TOK, HD), F32), pltpu.VMEM((TOK, 1), F32)],
        compiler_params=_params(("parallel",)),
    )(dya, w_pa, oa, lse)
    return [tuple(outs[3 * gi:3 * gi + 3]) for gi in range(3)]


def _attn_a_bwd(q, k, v, do, lse, dsum, gi):
    d = DILATIONS[gi]
    m_len = S // d

    def body(q_ref, k_ref, v_ref, do_ref, lse_ref, dsum_ref, dq_ref, dk_out, dv_out, dk_ref, dv_ref):
        dk_ref[...] = jnp.zeros((m_len, d * HD), F32)
        dv_ref[...] = jnp.zeros((m_len, d * HD), F32)
        for r in range(d):
            lanes = slice(r * HD, (r + 1) * HD)
            for q0, k0, wk in _band_blocks(m_len):
                rows, keys = slice(q0, q0 + QB), slice(k0, k0 + wk)
                qv, kw, vw, dov = q_ref[rows, lanes], k_ref[keys, lanes], v_ref[keys, lanes], do_ref[rows, lanes]
                p = jnp.exp(_band_scores(qv, kw, q0, k0, wk) - lse_ref[rows, r:r + 1])
                ds = (p * (_dot(dov, vw, NT) - dsum_ref[rows, r:r + 1]) * SCALE).astype(BF16)
                dq_ref[rows, lanes] = _dot(ds, kw, NN).astype(BF16)
                dk_ref[keys, lanes] += _dot(ds, qv, TN)
                dv_ref[keys, lanes] += _dot(p.astype(BF16), dov, TN)
        dk_out[...] = dk_ref[...].astype(BF16)
        dv_out[...] = dv_ref[...].astype(BF16)

    head = pl.BlockSpec((None, m_len, d * HD), lambda h: (h, 0, 0))
    stat = pl.BlockSpec((None, m_len, d), lambda h: (h, 0, 0))
    shape = jax.ShapeDtypeStruct((4, m_len, d * HD), BF16)
    return pl.pallas_call(
        body, name=f"attn_a_bwd_{gi}", grid=(4,),
        in_specs=[head, head, head, head, stat, stat], out_specs=[head, head, head],
        out_shape=[shape, shape, shape],
        scratch_shapes=[pltpu.VMEM((m_len, d * HD), F32)] * 2,
        compiler_params=_params(("arbitrary",)),
    )(q, k, v, do, lse, dsum)


KEYS_B = WIN_R * GRID_W
N_OFF = WIN_R


def _bias_constants():
    q = np.arange(GRID_W)[:, None]
    kc = np.arange(GRID_W)[None, :]
    dc = np.clip(kc - q, -(WIN_C - 1), WIN_C - 1) + (WIN_C - 1)
    expand = np.zeros((HD, GRID_W * GRID_W), np.float32)
    expand[dc.reshape(-1), np.arange(GRID_W * GRID_W)] = 1.0
    cs = np.clip(q - WIN_C // 2, 0, GRID_W - WIN_C)
    keep = ((kc >= cs) & (kc < cs + WIN_C)).reshape(1, -1).astype(np.float32)
    sel = np.zeros((64, 4 * N_OFF * WIN_R), np.float32)
    for h in range(4):
        for off in range(N_OFF):
            for j in range(WIN_R):
                sel[h * (2 * WIN_R - 1) + off + j, (h * N_OFF + off) * WIN_R + j] = 1.0
    return jnp.asarray(expand), jnp.asarray(keep), jnp.asarray(sel)


def _bias_expand(rpb_pad, expand, keep, sel):
    def body(r_ref, e_ref, k_ref, s_ref, o_ref):
        t = lax.dot_general(r_ref[...], e_ref[...], NN, precision=lax.Precision.HIGHEST,
                            preferred_element_type=F32)
        rows = lax.dot_general(s_ref[...], t, TN, precision=lax.Precision.HIGHEST,
                               preferred_element_type=F32)
        o_ref[...] = jnp.where(k_ref[...] > 0.5, rows, NEG)

    return pl.pallas_call(
        body, name="bias_expand",
        out_shape=jax.ShapeDtypeStruct((4 * N_OFF * WIN_R, GRID_W * GRID_W), F32),
        compiler_params=pltpu.CompilerParams(vmem_limit_bytes=VMEM_LIMIT),
    )(rpb_pad, expand, keep, sel)


def _bias_reduce(dbias_tab):
    lane0 = GRID_W - WIN_C
    flip = np.zeros((GRID_W, GRID_W), np.float32)
    flip[np.arange(GRID_W), GRID_W - 1 - np.arange(GRID_W)] = 1.0
    place = np.zeros((WIN_R, 64, 4 * N_OFF), np.float32)
    for j in range(WIN_R):
        for h in range(4):
            for off in range(N_OFF):
                place[j, h * (2 * WIN_R - 1) + off + j, h * N_OFF + off] = 1.0

    def exact(x, y):
        return lax.dot_general(x, y, NN, precision=lax.Precision.HIGHEST, preferred_element_type=F32)

    def body(x_ref, flip_ref, place_ref, o_ref, z_ref):
        for h in range(4):
            for off in range(N_OFF):
                lined_up = pltpu.roll(exact(flip_ref[...], x_ref[h, off]), 0, axis=1, stride=1, stride_axis=0)
                z_ref[h * N_OFF + off:h * N_OFF + off + 1, :] = jnp.sum(lined_up, axis=0, keepdims=True)
        acc = jnp.zeros((64, HD), F32)
        for j in range(WIN_R):
            at_zero = pltpu.roll(z_ref[...], (KEYS_B - (j * GRID_W + lane0)) % KEYS_B, axis=1)[:, :HD]
            acc = acc + exact(place_ref[j], at_zero)
        lane = lax.broadcasted_iota(jnp.int32, (64, HD), 1)
        o_ref[...] = jnp.where(lane < 2 * WIN_C - 1, acc, 0.0)

    return pl.pallas_call(
        body, name="bias_reduce", out_shape=jax.ShapeDtypeStruct((64, HD), F32),
        scratch_shapes=[pltpu.VMEM((4 * N_OFF, KEYS_B), F32)],
        compiler_params=pltpu.CompilerParams(vmem_limit_bytes=VMEM_LIMIT),
    )(dbias_tab, jnp.asarray(flip), jnp.asarray(place))


def _rows_to_tab(rows):
    t = rows.reshape(4, N_OFF, WIN_R, GRID_W, GRID_W)
    return t.transpose(0, 1, 3, 2, 4).reshape(4, N_OFF, GRID_W, KEYS_B)


def _row_window(r):
    r0 = jnp.clip(r - WIN_R // 2, 0, ROWS - WIN_R)
    off = r0 + (WIN_R - 1) - r
    return pl.multiple_of(r * GRID_W, GRID_W), pl.multiple_of(r0 * GRID_W, GRID_W), off


def _attn_b_fwd(qn, kn, vb, bias_tab):
    def body(q_ref, k_ref, v_ref, b_ref, o_ref, lse_ref):
        def row(r, carry):
            qs, ks, off = _row_window(r)
            q = q_ref[pl.ds(qs, GRID_W), :]
            s = lax.dot_general(q, k_ref[pl.ds(ks, KEYS_B), :], NT, preferred_element_type=F32) * SCALE
            s = s + b_ref[off]
            m = jnp.max(s, axis=-1, keepdims=True)
            p = jnp.exp(s - m)
            l = jnp.sum(p, axis=-1, keepdims=True)
            o = lax.dot_general(p.astype(BF16), v_ref[pl.ds(ks, KEYS_B), :], NN, preferred_element_type=F32)
            o_ref[pl.ds(qs, GRID_W), :] = (o / l).astype(BF16)
            lse_ref[pl.ds(qs, GRID_W), :] = m + jnp.log(l)
            return carry

        lax.fori_loop(0, ROWS, row, 0, unroll=8)

    full = pl.BlockSpec((S, HD), lambda h: (0, h))
    return pl.pallas_call(
        body, name="attn_b_fwd", grid=(4,),
        in_specs=[full, full, full, pl.BlockSpec((None, N_OFF, GRID_W, KEYS_B), lambda h: (h, 0, 0, 0))],
        out_specs=[pl.BlockSpec((S, HD), lambda h: (0, h)), pl.BlockSpec((None, S, 1), lambda h: (h, 0, 0))],
        out_shape=[jax.ShapeDtypeStruct((S, D_BR), BF16), jax.ShapeDtypeStruct((4, S, 1), F32)],
        compiler_params=_params(("parallel",)),
    )(qn, kn, vb, bias_tab)


def _attn_b_bwd(qn, kn, vb, bias_tab, ob, dob, lse):
    def body(q_ref, k_ref, v_ref, b_ref, o_ref, do_ref, lse_ref, dq_ref, dk_out, dv_out, db_ref, dk_ref, dv_ref):
        dk_ref[...] = jnp.zeros((S, HD), F32)
        dv_ref[...] = jnp.zeros((S, HD), F32)
        db_ref[...] = jnp.zeros((N_OFF, GRID_W, KEYS_B), F32)

        def row(r, carry):
            qs, ks, off = _row_window(r)
            rows = pl.ds(qs, GRID_W)
            keys = pl.ds(ks, KEYS_B)
            q = q_ref[rows, :]
            kw = k_ref[keys, :]
            s = lax.dot_general(q, kw, NT, preferred_element_type=F32) * SCALE + b_ref[off]
            p = jnp.exp(s - lse_ref[rows, :])
            do = do_ref[rows, :]
            dobf = do.astype(BF16)
            dsum = jnp.sum(do * o_ref[rows, :].astype(F32), axis=-1, keepdims=True)
            dp = lax.dot_general(dobf, v_ref[keys, :], NT, preferred_element_type=F32)
            ds = p * (dp - dsum)
            db_ref[off] += ds
            dsb = (ds * SCALE).astype(BF16)
            dq_ref[rows, :] = lax.dot_general(dsb, kw, NN, preferred_element_type=F32).astype(BF16)
            dk_ref[keys, :] += lax.dot_general(dsb, q, TN, preferred_element_type=F32)
            dv_ref[keys, :] += lax.dot_general(p.astype(BF16), dobf, TN, preferred_element_type=F32)
            return carry

        lax.fori_loop(0, ROWS, row, 0, unroll=8)
        dk_out[...] = dk_ref[...].astype(BF16)
        dv_out[...] = dv_ref[...].astype(BF16)

    full = pl.BlockSpec((S, HD), lambda h: (0, h))
    slot = pl.BlockSpec((S, HD), lambda h: (0, h))
    tab = pl.BlockSpec((None, N_OFF, GRID_W, KEYS_B), lambda h: (h, 0, 0, 0))
    shape = jax.ShapeDtypeStruct((S, D_BR), BF16)
    return pl.pallas_call(
        body, name="attn_b_bwd", grid=(4,),
        in_specs=[full, full, full, tab, slot, slot, pl.BlockSpec((None, S, 1), lambda h: (h, 0, 0))],
        out_specs=[slot, slot, slot, tab],
        out_shape=[shape, shape, shape, jax.ShapeDtypeStruct((4, N_OFF, GRID_W, KEYS_B), F32)],
        scratch_shapes=[pltpu.VMEM((S, HD), F32)] * 2,
        compiler_params=_params(("arbitrary",)),
    )(qn, kn, vb, bias_tab, ob, dob, lse)


def _epi_relu_sq(acc, ex, outs):
    u = jnp.maximum(acc, 0.0)
    outs[0][...] = u.astype(BF16)
    outs[1][...] = (u * u).astype(BF16)


def _epi_relu_sq_bwd(acc, ex, outs):
    outs[0][...] = (acc * (2.0 * ex[0][...].astype(F32))).astype(BF16)


def _epi_loss_head(acc, ex, outs):
    e = acc + ex[0][...] - ex[1][...]
    dy = e * (1.0 / D)
    outs[0][...] = dy
    outs[1][...] = dy.astype(BF16)
    part = (0.5 / D) * jnp.sum(jnp.sum(e * e, axis=-1, keepdims=True), axis=0, keepdims=True)
    first = (pl.program_id(0) == 0) & (pl.program_id(1) == 0)

    @pl.when(first)
    def _():
        outs[2][...] = part

    @pl.when(jnp.logical_not(first))
    def _():
        outs[2][...] += part


def _local_step(x, target, norm_mix, b_gate, gains, rpb_pad, norm_ffn,
                w_in, w_pa, w_pb, w_out, w_up, w_down, weight_grads, riders=lambda name: None):
    def ridden(name, *args, **kwargs):
        ride = riders(name)
        if ride is None:
            return _mm_nt(*args, name=name, **kwargs)
        out, rode = _mm_nt(*args, name=name, rider=ride[0], **kwargs)
        ride[1](rode)
        return out

    cos2, sin2 = _rope_tables()
    expand, keep, sel = _bias_constants()
    w_out3 = w_out[None]

    xn, rstd1 = _rms_fwd(x, norm_mix, name="rms_mix")
    per_dev = D_IN // N_DEV
    proj = _mm_nn(xn, w_in[0], tm=S, tn=W_IN_SPLIT, name="proj_0", stride=per_dev, width=D_IN)
    proj = _mm_nn(xn, w_in[1], tm=S, tn=per_dev - W_IN_SPLIT, name="proj_1", stride=per_dev, width=D_IN,
                  col0=W_IN_SPLIT, into=(proj,))
    qkv_a, qkv_b = _qk_prep(proj, gains, cos2, sin2)
    fwd_a = [_attn_a_fwd(*qkv_a[gi], gi) for gi in range(3)]
    oa, lse_a = _combine_a([o for o, _ in fwd_a], [l for _, l in fwd_a])
    bias_tab = _rows_to_tab(_bias_expand(rpb_pad, expand, keep, sel))
    ob, lse_b = _attn_b_fwd(*qkv_b, bias_tab)
    mixed, ya, yb = _mix_fwd(oa, ob, w_pa, w_pb, proj, b_gate)
    h1 = _mm_nn(mixed, w_out3, tm=1024, tn=1024, name="out_proj", epi=_epi_residual, extra=(x,))
    hn, rstd2 = _rms_fwd(h1, norm_ffn, name="rms_ffn")
    u, usq = _mm_nn(hn, w_up, tm=1024, tn=1024, name="ffn_up", epi=_epi_relu_sq,
                    out_dtypes=(BF16, BF16))
    dy, dyb, loss = _mm_nn(usq, w_down[0], tm=512, tn=512, name="ffn_down_0", epi=_epi_loss_head,
                           extra=(h1, target), out_dtypes=(F32, BF16), total=True, width=D)
    dy, dyb, loss_1 = _mm_nn(usq, w_down[1], tm=512, tn=512, name="ffn_down_1", epi=_epi_loss_head,
                             extra=(h1, target), out_dtypes=(F32, BF16), total=True, width=D,
                             col0=D // 2, into=(dy, dyb))
    loss = loss + loss_1

    sent = weight_grads("w_down", {5: (usq, dyb)})
    du = _mm_nt(dyb, w_down[0], more_b=(w_down[1],), tm=1024, tn=1024, name="ffn_down_bwd", out_dtype=BF16,
                epi=_epi_relu_sq_bwd, extra=(u,), after=sent)
    sent = weight_grads("w_up", {4: (hn, du)})
    dhn = ridden("ffn_up_bwd", du, w_up, tm=512, tn=512, after=sent)
    dh1, dh1b, g_norm_ffn = _rms_bwd(dhn, h1, rstd2, norm_ffn, dy, name="rms_ffn_bwd", bf16_copy=True)

    dya, dyb2, dproj, g_b = _mix_bwd(dh1b, w_out, proj, b_gate, ya, yb)
    sent = weight_grads("w_mix", {3: (mixed, dh1b), 1: (oa, dya), 2: (ob, dyb2)})
    dob = _mm_nt(dyb2, w_pb, tm=1024, tn=D_BR, name="proj_b_bwd", after=sent)
    prep = _proj_a_bwd(dya, w_pa, oa, lse_a)
    grads_a = [_attn_a_bwd(*qkv_a[gi], *prep[gi], gi) for gi in range(3)]
    dqb, dkb, dvb, dbias = _attn_b_bwd(*qkv_b, bias_tab, ob, dob, lse_b)
    g_rpb = _bias_reduce(dbias)
    dproj, g_gains = _qk_prep_bwd(dproj, proj, gains, cos2, sin2, grads_a, (dqb, dkb, dvb))
    sent = weight_grads("w_in", {0: (xn, dproj)})
    dxn = ridden("proj_bwd", dproj, w_in[0], more_b=(w_in[1],), interleaved=True, tm=256, tn=512, after=sent)
    grad_x, g_norm_mix = _rms_bwd(dxn, x, rstd1, norm_mix, dh1, name="rms_mix_bwd", bf16_copy=False)

    small = (g_norm_mix, g_b, g_gains, g_rpb, g_norm_ffn)
    return loss, grad_x, small


def _cast_bf16(w, *, part=0, parts=1, window=None, after=(), tr=256):
    rows, cols = w.shape[0], w.shape[1] // parts
    tr = min(tr, rows)
    src = pl.BlockSpec((tr, cols), lambda i: (i, part))
    if window is not None:
        part, cols = window
        src = pl.BlockSpec((pl.Element(tr), pl.Element(cols)), lambda i: (i * tr, part))

    def body(w_ref, *rest):
        rest[-1][...] = w_ref[...].astype(BF16)

    return pl.pallas_call(
        body, name=f"cast_{rows}x{cols}_{part}", grid=(rows // tr,),
        in_specs=[src] + [pl.BlockSpec(memory_space=pl.ANY)] * len(after),
        out_specs=pl.BlockSpec((tr, cols), lambda i: (i, 0)),
        out_shape=jax.ShapeDtypeStruct((rows, cols), BF16), compiler_params=_params(("parallel",)),
    )(w, *after)


def _me_and_peers():
    x, y, c = lax.axis_index("x"), lax.axis_index("y"), lax.axis_index("c")
    me = 4 * x + 2 * y + c
    peers = []
    for k in range(1, N_DEV):
        px = 1 - x if k & 4 else x
        py = 1 - y if k & 2 else y
        pc = 1 - c if k & 1 else c
        peers.append(((px, py, pc), 4 * px + 2 * py + pc))
    return me, peers


def _gather_on_sequencer(shards, name):
    n = len(shards)
    hbm = pltpu.MemorySpace.HBM
    ins = [jax.new_ref(s, memory_space=hbm) for s in shards]
    outs = [jax.empty_ref(jax.ShapeDtypeStruct((N_DEV,) + s.shape, s.dtype), memory_space=hbm) for s in shards]
    n_sem = 8

    @_sequencer(name, ((n, n_sem), (n, n_sem), (n,)), 0)
    def launch(send, recv, lsem):
        x, y, c = lax.axis_index("x"), lax.axis_index("y"), lax.axis_index("c")
        me, sibling = (x, y, c), (x, y, 1 - c)
        x_chip, y_chip, diagonal = (1 - x, y, c), (x, 1 - y, c), (1 - x, 1 - y, c)
        _handshake([sibling, x_chip, y_chip])

        def copy(w, k, block, to, src=None, half=None):
            px, py, pc = block
            dst = outs[w].at[4 * px + 2 * py + pc]
            if half is not None:
                rows = shards[w].shape[0] // 2
                dst = dst.at[pl.ds(half * rows, rows)]
            return pltpu.make_async_remote_copy(dst if src is None else src, dst, send.at[w, k], recv.at[w, k],
                                                device_id=to, device_id_type=MESH)

        local = [pltpu.make_async_copy(ins[w], outs[w].at[4 * x + 2 * y + c], lsem.at[w]) for w in range(n)]
        for cp in local:
            cp.start()
        sent = []
        for w in range(n):
            sent += [copy(w, 1, me, x_chip, src=ins[w]), copy(w, 2, me, y_chip, src=ins[w]),
                     copy(w, 0, me, sibling, src=ins[w])]
        for cp in sent:
            cp.start()
        for w in range(n):
            copy(w, 1, x_chip, me).wait_recv()
            sent += [copy(w, 3, x_chip, y_chip, half=0), copy(w, 5, x_chip, sibling)]
            sent[-2].start()
            sent[-1].start()
            copy(w, 2, y_chip, me).wait_recv()
            sent += [copy(w, 4, y_chip, x_chip, half=1), copy(w, 6, y_chip, sibling)]
            sent[-2].start()
            sent[-1].start()
        for w in range(n):
            copy(w, 3, diagonal, me, half=0).wait_recv()
            copy(w, 4, diagonal, me, half=1).wait_recv()
            sent.append(copy(w, 7, diagonal, sibling))
            sent[-1].start()
        for w in range(n):
            copy(w, 0, sibling, me).wait_recv()
            for k, chip in ((5, x_chip), (6, y_chip), (7, diagonal)):
                px, py, _ = chip
                copy(w, k, (px, py, 1 - c), me).wait_recv()
        for cp in sent:
            cp.wait_send()
        for cp in local:
            cp.wait()

    launch()
    return [o[...] for o in outs]


N_CHIP = 4


def _sequencer(name, n_sems, collective_id):
    return functools.partial(
        pl.kernel, mesh=plsc.ScalarSubcoreMesh(axis_name="seq", num_cores=1), name=name,
        scratch_types=tuple(pltpu.SemaphoreType.DMA(s) for s in n_sems),
        compiler_params=pltpu.CompilerParams(collective_id=collective_id))


def _handshake(peers):
    barrier = pltpu.get_barrier_semaphore()
    for peer in peers:
        pl.semaphore_signal(barrier, inc=1, device_id=peer, device_id_type=MESH)
    pl.semaphore_wait(barrier, len(peers))


def _chip_exchange_on_sequencer(parts, name):
    n = len(parts)
    hbm = pltpu.MemorySpace.HBM
    ins = [jax.new_ref(p, memory_space=hbm) for p in parts]
    outs = [jax.empty_ref(jax.ShapeDtypeStruct(p.shape, p.dtype), memory_space=hbm) for p in parts]

    @_sequencer(name, ((n, 3), (n, 3), (n,)), 2)
    def launch(send, recv, lsem):
        x, y, c = lax.axis_index("x"), lax.axis_index("y"), lax.axis_index("c")
        mine = 2 * x + y
        chips = [(1 - x, y), (x, 1 - y), (1 - x, 1 - y)]
        _handshake([(*chip, c) for chip in chips])
        local = [pltpu.make_async_copy(ins[w].at[mine], outs[w].at[mine], lsem.at[w]) for w in range(n)]
        for cp in local:
            cp.start()
        sends = []
        for w in range(n):
            for j, (px, py) in enumerate(chips):
                cp = pltpu.make_async_remote_copy(ins[w].at[2 * px + py], outs[w].at[mine],
                                                  send.at[w, j], recv.at[w, j],
                                                  device_id=(px, py, c), device_id_type=MESH)
                cp.start()
                sends.append(cp)
        for w in range(n):
            for j, (px, py) in enumerate(chips):
                pltpu.make_async_remote_copy(ins[w].at[mine], outs[w].at[2 * px + py],
                                             send.at[w, j], recv.at[w, j],
                                             device_id=(px, py, c), device_id_type=MESH).wait_recv()
        for cp in sends:
            cp.wait_send()
        for cp in local:
            cp.wait()

    launch()
    return [o[...] for o in outs]


GRAD_TILES = (dict(blocks_on="cols", tm=512, tn=1280), dict(blocks_on="cols", tm=512, tn=256),
              dict(blocks_on="cols", tm=512, tn=256), dict(blocks_on="rows", tm=256, tn=2048),
              dict(blocks_on="cols", tm=1024, tn=1024), dict(blocks_on="rows", tm=1024, tn=1024))


def _mm_tn_pair(a, b, *, blocks_on, tm, tn, name):
    t_len, m = a.shape
    n = b.shape[1]
    if blocks_on == "rows":
        rows, cols, inner = m // N_DEV, n, n // tn
        assert tm == rows
        a_spec = pl.BlockSpec((t_len, tm), lambda p, t, blk: (0, blk[p]))
        b_spec = pl.BlockSpec((t_len, tn), lambda p, t, blk: (0, t))
        out_spec = pl.BlockSpec((None, tm, tn), lambda p, t, blk: (
            jnp.maximum(p - N_CHIP, 0), 0, jnp.where(p < N_CHIP, 0, t)))
    else:
        rows, cols, inner = m, n // N_DEV, m // tm
        assert tn == cols
        a_spec = pl.BlockSpec((t_len, tm), lambda p, t, blk: (0, t))
        b_spec = pl.BlockSpec((t_len, tn), lambda p, t, blk: (0, blk[p]))
        out_spec = pl.BlockSpec((None, tm, tn), lambda p, t, blk: (
            jnp.maximum(p - N_CHIP, 0), jnp.where(p < N_CHIP, 0, t), 0))

    def body(blk_ref, a_ref, b_ref, o_ref, land, stage, send_sem, recv_sem):
        del blk_ref
        p, t = pl.program_id(0), pl.program_id(1)
        step = p * inner + t
        x, y, c = lax.axis_index("x"), lax.axis_index("y"), lax.axis_index("c")
        tile = _dot(a_ref[...], b_ref[...], TN)

        def to_sibling(slot, chip, piece):
            return pltpu.make_async_remote_copy(stage.at[slot], land.at[chip, piece], send_sem.at[slot],
                                                recv_sem.at[chip, piece],
                                                device_id=(x, y, 1 - c), device_id_type=MESH)

        @pl.when(p < N_CHIP)
        def _():
            slot = step % 2

            @pl.when(step >= 2)
            def _():
                to_sibling(slot, 0, 0).wait_send()

            stage[slot] = tile.astype(BF16)
            to_sibling(slot, p, t).start()

        @pl.when(step == N_CHIP * inner)
        def _():
            for slot in range(min(2, N_CHIP * inner)):
                to_sibling(slot, 0, 0).wait_send()

        @pl.when(p >= N_CHIP)
        def _():
            chip = p - N_CHIP
            to_sibling(0, chip, t).wait_recv()
            o_ref[...] = (tile + land[chip, t].astype(F32)).astype(BF16)

    c = lax.axis_index("c")
    order = jnp.stack([2 * ch + 1 - c for ch in range(N_CHIP)] + [2 * ch + c for ch in range(N_CHIP)])
    return pl.pallas_call(
        body, name=name,
        grid_spec=pltpu.PrefetchScalarGridSpec(
            num_scalar_prefetch=1, grid=(N_DEV, inner), in_specs=[a_spec, b_spec], out_specs=out_spec,
            scratch_shapes=[pltpu.VMEM((N_CHIP, inner, tm, tn), BF16), pltpu.VMEM((2, tm, tn), BF16),
                            pltpu.SemaphoreType.DMA((2,)), pltpu.SemaphoreType.DMA((N_CHIP, inner))]),
        out_shape=jax.ShapeDtypeStruct((N_CHIP, rows, cols), BF16),
        compiler_params=_params(("arbitrary", "arbitrary")),
    )(order.astype(jnp.int32), a, b)


def _adamw_math(g, w, m, v):
    m2 = B1 * m + (1.0 - B1) * g
    v2 = B2 * v + (1.0 - B2) * (g * g)
    delta = -LR * ((m2 / BC1) / (jnp.sqrt(v2 / BC2) + AEPS) + WD * w)
    return delta, m2, v2


def _adamw_block(ins, outs):
    p_ref, w_ref, m_ref, v_ref = ins
    g = p_ref[0].astype(F32)
    for b in range(1, N_CHIP):
        g = g + p_ref[b].astype(F32)
    delta, m2, v2 = _adamw_math(g, w_ref[...], m_ref[...], v_ref[...])
    for ref, val in zip(outs, (g, delta, m2, v2)):
        ref[...] = val


class _Rider(NamedTuple):
    inputs: tuple
    in_specs: list
    out_shape: list
    out_specs: list
    body: Callable


def _adamw_rider(parts, w, m, v):
    rows, cols = w.shape

    def rider(steps, step_of):
        rr = rows // steps
        blk = pl.BlockSpec((rr, cols), lambda *ids: (step_of(*ids[:2]), 0))
        chips = pl.BlockSpec((N_CHIP, rr, cols), lambda *ids: (0, step_of(*ids[:2]), 0))
        shape = jax.ShapeDtypeStruct((rows, cols), F32)
        return _Rider((parts, w, m, v), [chips, blk, blk, blk], [shape] * 4, [blk] * 4, _adamw_block)

    return rider


def _adamw(parts, w, m, v, *, name, after=(), tr=256):
    rows, cols = w.shape

    def body(*refs):
        _adamw_block(refs[:4], refs[4 + len(after):])

    spec = pl.BlockSpec((tr, cols), lambda i: (i, 0))
    shape = jax.ShapeDtypeStruct((rows, cols), F32)
    return pl.pallas_call(
        body, name=name, grid=(rows // tr,),
        in_specs=[pl.BlockSpec((N_CHIP, tr, cols), lambda i: (0, i, 0)), spec, spec, spec]
        + [pl.BlockSpec(memory_space=pl.ANY)] * len(after),
        out_specs=[spec] * 4, out_shape=[shape] * 4,
        compiler_params=_params(("parallel",)),
    )(parts, w, m, v, *after)


def _small_exchange(part, after=()):
    rows = part.shape[0]

    def body(p_ref, *rest):
        g_ref, buf, send, recv = rest[len(after):]
        me, peers = _me_and_peers()
        buf[me] = p_ref[...]
        sends = []
        for k, (dev, _) in enumerate(peers):
            cp = pltpu.make_async_remote_copy(p_ref, buf.at[me], send.at[k], recv.at[k],
                                              device_id=dev, device_id_type=MESH)
            cp.start()
            sends.append(cp)
        for k, (dev, idx) in enumerate(peers):
            pltpu.make_async_remote_copy(p_ref, buf.at[idx], send.at[k], recv.at[k],
                                         device_id=dev, device_id_type=MESH).wait_recv()
        for cp in sends:
            cp.wait_send()
        g = buf[0]
        for b in range(1, N_DEV):
            g = g + buf[b]
        g_ref[...] = g

    vm = pl.BlockSpec(memory_space=pltpu.VMEM)
    return pl.pallas_call(
        body, name="small_params_exchange",
        in_specs=[vm] + [pl.BlockSpec(memory_space=pl.ANY)] * len(after),
        out_specs=vm, out_shape=jax.ShapeDtypeStruct((rows, HD), F32),
        scratch_shapes=[pltpu.VMEM((N_DEV, rows, HD), F32),
                        pltpu.SemaphoreType.DMA((N_DEV - 1,)), pltpu.SemaphoreType.DMA((N_DEV - 1,))],
    )(part, *after)


def _small_adamw(g, w, m, v):
    def body(g_ref, w_ref, m_ref, v_ref, *outs):
        g = g_ref[...]
        delta, m2, v2 = _adamw_math(g, w_ref[...], m_ref[...], v_ref[...])
        for k, val in enumerate((g, delta, m2, v2)):
            norm_mix, b_gate, qa, ka, qb, kb, rpb, norm_ffn = outs[8 * k:8 * k + 8]
            for dst, row0, n_rows in ((norm_mix, 0, 16), (b_gate, 16, 32), (norm_ffn, 120, 16)):
                for r in range(n_rows):
                    dst[:, r * HD:(r + 1) * HD] = val[row0 + r:row0 + r + 1, :]
            for i, dst in enumerate((qa, ka, qb, kb)):
                dst[...] = val[48 + i:49 + i, :]
            rpb[...] = val[56:120, :]
        outs[32][...] = g[LOSS_ROW:LOSS_ROW + 1, 0:1]

    vm = pl.BlockSpec(memory_space=pltpu.VMEM)
    kinds = [jax.ShapeDtypeStruct(sh, F32) for sh in
             ((1, D), (1, 2 * D), (1, HD), (1, HD), (1, HD), (1, HD), (64, HD), (1, D))]
    outs = pl.pallas_call(
        body, name="small_params_adamw", in_specs=[vm] * 4, out_specs=[vm] * 33,
        out_shape=kinds * 4 + [jax.ShapeDtypeStruct((1, 1), F32)],
    )(g, w, m, v)
    return [outs[8 * k:8 * k + 8] for k in range(4)], outs[32]


def _pack_small(norm_mix, b_gate, qa, ka, qb, kb, rpb, norm_ffn):
    gains = jnp.concatenate([qa, ka, qb, kb, jnp.zeros((4, HD), F32)], axis=0)
    rpb_pad = jnp.pad(rpb.reshape(4 * (2 * WIN_R - 1), 2 * WIN_C - 1), ((0, 4), (0, HD - (2 * WIN_C - 1))))
    return jnp.concatenate([norm_mix.reshape(16, HD), b_gate.reshape(32, HD), gains, rpb_pad,
                            norm_ffn.reshape(16, HD), jnp.zeros((8, HD), F32)], axis=0)


LOSS_ROW = 136


def _rpb_from_rows(rows):
    return rows[:60, :2 * WIN_C - 1].reshape(1, 4, 2 * WIN_R - 1, 2 * WIN_C - 1)


def kernel(x, norm_mix, w_in, b_gate, q_norm_a, k_norm_a, q_norm_b, k_norm_b, rpb_b, w_proj_a, w_proj_b, w_out, norm_ffn, w_up, w_down, loss_target, m_norm_mix, m_w_in, m_b_gate, m_q_norm_a, m_k_norm_a, m_q_norm_b, m_k_norm_b, m_rpb_b, m_w_proj_a, m_w_proj_b, m_w_out, m_norm_ffn, m_w_up, m_w_down, v_norm_mix, v_w_in, v_b_gate, v_q_norm_a, v_k_norm_a, v_q_norm_b, v_k_norm_b, v_rpb_b, v_w_proj_a, v_w_proj_b, v_w_out, v_norm_ffn, v_w_up, v_w_down):
    big_w = (w_in[0], w_proj_a[0], w_proj_b[0], w_out[0], w_up[0], w_down[0])
    big_m = (m_w_in[0], m_w_proj_a[0], m_w_proj_b[0], m_w_out[0], m_w_up[0], m_w_down[0])
    big_v = (v_w_in[0], v_w_proj_a[0], v_w_proj_b[0], v_w_out[0], v_w_up[0], v_w_down[0])
    names = ("w_in", "w_proj_a", "w_proj_b", "w_out", "w_up", "w_down")

    g_in = [_gather_on_sequencer([_cast_bf16(big_w[0], window=win)], f"gather_w_in_{k}")[0]
            for k, win in enumerate(((0, W_IN_SPLIT), (W_IN_SPLIT, D_IN // N_DEV - W_IN_SPLIT)))]
    shards = [None] + [_cast_bf16(w) for w in big_w[1:5]]
    g_pa, g_pb, g_out, g_up = _gather_on_sequencer(shards[1:5], "gather_w_mix_up")
    small_w = _pack_small(norm_mix, b_gate, q_norm_a, k_norm_a, q_norm_b, k_norm_b, rpb_b, norm_ffn)
    small_m = _pack_small(m_norm_mix, m_b_gate, m_q_norm_a, m_k_norm_a, m_q_norm_b, m_k_norm_b, m_rpb_b, m_norm_ffn)
    small_v = _pack_small(v_norm_mix, v_b_gate, v_q_norm_a, v_k_norm_a, v_q_norm_b, v_k_norm_b, v_rpb_b, v_norm_ffn)
    g_down = [_gather_on_sequencer([_cast_bf16(big_w[5], part=h, parts=2, after=(small_w, small_m, small_v) * h)],
                                   f"gather_w_down_{h}")[0].reshape(1, D_FF, D // 2) for h in range(2)]

    upd = [None] * 6
    in_flight = {}

    def weight_grads(tag, operands):
        sums = {i: _mm_tn_pair(a, b, name=f"grad_{names[i]}", **GRAD_TILES[i]) for i, (a, b) in operands.items()}
        new = list(sums.values())
        in_flight.update(zip(sums, _chip_exchange_on_sequencer(new, f"chip_exchange_{tag}")))
        return new

    def riders(name):
        i = {"proj_bwd": 5}.get(name)
        if i is None:
            return None
        return (_adamw_rider(in_flight.pop(i), big_w[i], big_m[i], big_v[i]),
                functools.partial(upd.__setitem__, i))

    loss, grad_x, small_g = _local_step(
        x[0], loss_target[0], norm_mix, b_gate, small_w[48:56], small_w[56:120], norm_ffn,
        g_in, g_pa, g_pb, g_out.reshape(D, D), g_up, g_down, weight_grads, riders)

    g_norm_mix, g_b, g_gains, g_rpb, g_norm_ffn = small_g
    small_part = jnp.concatenate([g_norm_mix.reshape(16, HD), g_b.reshape(32, HD),
                                  g_gains, g_rpb, g_norm_ffn.reshape(16, HD),
                                  jnp.pad(loss, ((0, 7), (0, HD - 1)))], axis=0)
    last = grad_x
    for i, r in in_flight.items():
        if i == 0:
            small_sum = _small_exchange(small_part, after=[last])
            small, total = _small_adamw(small_sum, small_w, small_m, small_v)
            last = total
        upd[i] = _adamw(r, big_w[i], big_m[i], big_v[i], name=f"adamw_{names[i]}", after=[last])
        last = upd[i][0]
    s_g, s_d, s_m, s_v = ((*k[:6], _rpb_from_rows(k[6]), k[7]) for k in small)
    b_g, b_d, b_m, b_v = ([u[j][None] for u in upd] for j in range(4))

    def order(small, big):
        nm, bg, qa, ka, qb, kb, rpb, nf = small
        w_in_, pa_, pb_, out_, up_, down_ = big
        return (nm, w_in_, bg, qa, ka, qb, kb, rpb, pa_, pb_, out_, nf, up_, down_)

    return (total[0, 0], grad_x[None], *order(s_g, b_g), *order(s_d, b_d), *order(s_m, b_m), *order(s_v, b_v))
```

```python
import functools
from typing import Callable, NamedTuple

import jax
import jax.numpy as jnp
import numpy as np
from jax import lax
from jax.experimental import pallas as pl
from jax.experimental.pallas import tpu as pltpu
from jax.experimental.pallas import tpu_sc as plsc

F32 = jnp.float32
BF16 = jnp.bfloat16

N_DEV = 8
S = 2048
D = 2048
HD = 128
NH = 16
NH_A = 12
QKV = NH * HD
D_IN = 3 * QKV + 2 * D
D_BR = 512
D_FF = 4 * D
GRID_W = 64
ROWS = S // GRID_W
WIN_R = 8
WIN_C = 16
EPS = 1e-6
NEG = -1e30
SCALE = HD ** -0.5
ROPE_THETA = 10000.0
DILATIONS = (1, 4, 16)
HALF_A = 64
QB = 128
W_IN_SPLIT = 768

LR, B1, B2, AEPS, WD, STEP = 0.001, 0.9, 0.999, 1e-08, 0.01, 10
BC1 = 1.0 - B1 ** STEP
BC2 = 1.0 - B2 ** STEP

VMEM_LIMIT = 56 * 1024 * 1024
MESH = pl.DeviceIdType.MESH

NN = (((1,), (0,)), ((), ()))
NT = (((1,), (1,)), ((), ()))
TN = (((0,), (0,)), ((), ()))


def _params(sem):
    return pltpu.CompilerParams(dimension_semantics=sem, vmem_limit_bytes=VMEM_LIMIT)


def _matmul(a, b, *, product, grid, a_spec, b_spec, epi, out_shape, out_specs, name,
            extra=(), extra_specs=(), after=(), carried=False, rider=None, into=()):
    n_extra = len(extra)
    single = not isinstance(out_shape, (list, tuple))
    out_shape = [out_shape] if single else list(out_shape)
    out_specs = [out_specs] if single else list(out_specs)
    ride = rider(grid[0] * grid[1], lambda j, i: j * grid[1] + i) if rider else None
    r_in = list(ride.inputs) if ride else []
    n_main = len(out_shape)

    def body(a_ref, b_ref, *rest):
        n_in = n_extra + len(after) + len(r_in)
        ins, outs = rest[:n_in], rest[n_in + len(into):]
        epi(product(a_ref, b_ref, ins[:n_extra]), ins[:n_extra], outs[:n_main])
        if ride:
            ride.body(ins[n_extra + len(after):], outs[n_main:])

    res = pl.pallas_call(
        body, name=name, grid=grid,
        in_specs=[a_spec, b_spec, *extra_specs, *[pl.BlockSpec(memory_space=pl.ANY)] * len(after),
                  *(ride.in_specs if ride else []), *[pl.BlockSpec(memory_space=pl.ANY)] * len(into)],
        out_specs=out_specs + (ride.out_specs if ride else []),
        out_shape=out_shape + (ride.out_shape if ride else []),
        input_output_aliases={2 + n_extra + len(after) + len(r_in) + k: k for k in range(len(into))},
        compiler_params=_params(("arbitrary", "arbitrary") if carried else ("parallel", "parallel")),
    )(a, b, *extra, *after, *r_in, *into)
    main = res[0] if single else res[:n_main]
    return (main, res[n_main:]) if ride else main


def _dot(x, y, dims):
    return lax.dot_general(x, y, dims, preferred_element_type=F32)


def _epi_store(acc, ex, outs):
    outs[0][...] = acc.astype(outs[0].dtype)


def _epi_residual(acc, ex, outs):
    outs[0][...] = acc + ex[0][...]


def _mm_nn(a, b3, *, tm, tn, name, out_dtypes=(F32,), epi=_epi_store, extra=(), total=False,
           col0=0, width=None, into=(), stride=None):
    m, kdim = a.shape
    g, _, ng = b3.shape
    n = g * ng
    c0 = col0 // tn
    if tn <= ng:
        npg = ng // tn
        b_spec = pl.BlockSpec((None, kdim, tn), lambda j, i: (j // npg, 0, j % npg))

        def product(a_ref, b_ref, ex):
            return _dot(a_ref[...], b_ref[...], NN)
    else:
        gb = tn // ng
        b_spec = pl.BlockSpec((gb, kdim, ng), lambda j, i: (j, 0, 0))

        def product(a_ref, b_ref, ex):
            return jnp.concatenate([_dot(a_ref[...], b_ref[q], NN) for q in range(gb)], axis=1)

    tile = pl.BlockSpec((tm, tn), lambda j, i: (i, j + c0))
    if stride is not None:
        assert tn == ng and not extra
        tile = pl.BlockSpec((pl.Element(tm), pl.Element(tn)),
                            lambda j, i: (i * tm, pl.multiple_of(j * stride + col0, 128)))
    shapes = [jax.ShapeDtypeStruct((m, width or n), dt) for dt in out_dtypes]
    specs = [tile] * len(shapes)
    if total:
        shapes.append(jax.ShapeDtypeStruct((1, 1), F32))
        specs.append(pl.BlockSpec((1, 1), lambda j, i: (0, 0)))
    single = len(shapes) == 1
    return _matmul(
        a, b3, product=product, grid=(n // tn, m // tm), epi=epi, name=name, carried=total, into=into,
        a_spec=pl.BlockSpec((tm, kdim), lambda j, i: (i, 0)), b_spec=b_spec,
        extra=extra, extra_specs=[tile] * len(extra),
        out_shape=shapes[0] if single else shapes, out_specs=specs[0] if single else specs)


def _mm_nt(a, b3, *, tm, tn, name, out_dtype=F32, epi=_epi_store, extra=(), after=(), rider=None, more_b=(),
           interleaved=False):
    m, kdim = a.shape
    _, n, _ = b3.shape
    n_b = len(more_b)

    def product(a_ref, b_ref, ex):
        refs = (b_ref, *ex[:n_b])
        pieces = ([(ref, q) for q in range(b_ref.shape[0]) for ref in refs] if interleaved
                  else [(ref, q) for ref in refs for q in range(ref.shape[0])])
        acc, k0 = None, 0
        for ref, q in pieces:
            part = _dot(a_ref[:, k0:k0 + ref.shape[2]], ref[q], NT)
            acc = part if acc is None else acc + part
            k0 += ref.shape[2]
        return acc

    def write(acc, ex, outs):
        epi(acc, ex[n_b:], outs)

    def w_spec(w):
        return pl.BlockSpec((w.shape[0], tn, w.shape[2]), lambda j, i: (0, j, 0))

    tile = pl.BlockSpec((tm, tn), lambda j, i: (i, j))
    return _matmul(
        a, b3, product=product, grid=(n // tn, m // tm), epi=write, name=name,
        a_spec=pl.BlockSpec((tm, kdim), lambda j, i: (i, 0)), b_spec=w_spec(b3),
        extra=(*more_b, *extra), extra_specs=[w_spec(w) for w in more_b] + [tile] * len(extra),
        after=after, rider=rider,
        out_shape=jax.ShapeDtypeStruct((m, n), out_dtype), out_specs=tile)


def _mm_tn(a, b, *, tm, tn, name, groups=1, out_dtype=BF16):
    t, m = a.shape
    _, n = b.shape
    ng = n // groups
    if tn <= ng:
        npg = ng // tn
        out_spec = pl.BlockSpec((None, tm, tn), lambda j, i: (j // npg, i, j % npg))
        epi = _epi_store

        def product(a_ref, b_ref, ex):
            return _dot(a_ref[...], b_ref[...], TN)
    else:
        gb = tn // ng
        out_spec = pl.BlockSpec((gb, tm, ng), lambda j, i: (j, i, 0))

        def product(a_ref, b_ref, ex):
            return [_dot(a_ref[...], b_ref[:, q * ng:(q + 1) * ng], TN) for q in range(gb)]

        def epi(parts, ex, outs):
            for q, part in enumerate(parts):
                outs[0][q] = part.astype(out_dtype)

    return _matmul(
        a, b, product=product, grid=(n // tn, m // tm), epi=epi, name=name,
        a_spec=pl.BlockSpec((t, tm), lambda j, i: (0, i)),
        b_spec=pl.BlockSpec((t, tn), lambda j, i: (0, j)),
        out_shape=jax.ShapeDtypeStruct((groups, m, ng), out_dtype), out_specs=out_spec)


def _rms_fwd(x, g, *, name, tr=256):
    def body(x_ref, g_ref, y_ref, r_ref):
        xv = x_ref[...]
        r = lax.rsqrt(jnp.mean(xv * xv, axis=-1, keepdims=True) + EPS)
        y_ref[...] = (xv * r * g_ref[...]).astype(BF16)
        r_ref[...] = r

    row = pl.BlockSpec((tr, D), lambda i: (i, 0))
    return pl.pallas_call(
        body, name=name, grid=(S // tr,),
        in_specs=[row, pl.BlockSpec((1, D), lambda i: (0, 0))],
        out_specs=[row, pl.BlockSpec((tr, 1), lambda i: (i, 0))],
        out_shape=[jax.ShapeDtypeStruct((S, D), BF16), jax.ShapeDtypeStruct((S, 1), F32)],
        compiler_params=_params(("parallel",)),
    )(x, g)


def _rms_bwd(dy, x, rstd, g, resid, *, name, bf16_copy, tr=256):
    def body(dy_ref, x_ref, r_ref, g_ref, res_ref, dx_ref, *rest):
        dg_ref = rest[-1]
        r = r_ref[...]
        xh = x_ref[...] * r
        dyv = dy_ref[...]
        t = dyv * g_ref[...]
        dx = r * (t - xh * jnp.mean(t * xh, axis=-1, keepdims=True)) + res_ref[...]
        dx_ref[...] = dx
        if bf16_copy:
            rest[0][...] = dx.astype(BF16)
        part = jnp.sum(dyv * xh, axis=0, keepdims=True)

        @pl.when(pl.program_id(0) == 0)
        def _():
            dg_ref[...] = part

        @pl.when(pl.program_id(0) > 0)
        def _():
            dg_ref[...] += part

    row = pl.BlockSpec((tr, D), lambda i: (i, 0))
    vec = pl.BlockSpec((1, D), lambda i: (0, 0))
    return pl.pallas_call(
        body, name=name, grid=(S // tr,),
        in_specs=[row, row, pl.BlockSpec((tr, 1), lambda i: (i, 0)), vec, row],
        out_specs=[row] + [row] * bf16_copy + [vec],
        out_shape=[jax.ShapeDtypeStruct((S, D), F32)] + [jax.ShapeDtypeStruct((S, D), BF16)] * bf16_copy
        + [jax.ShapeDtypeStruct((1, D), F32)],
        compiler_params=_params(("arbitrary",)),
    )(dy, x, rstd, g, resid)


def _rope_tables():
    pos = np.arange(S, dtype=np.float32)
    inv = (ROPE_THETA ** (-np.arange(0, HD, 2, dtype=np.float32) / HD)).astype(np.float32)
    ang = pos[:, None] * inv[None, :]
    cos, sin = np.cos(ang), np.sin(ang)
    return (jnp.asarray(np.concatenate([cos, cos], axis=-1), F32),
            jnp.asarray(np.concatenate([-sin, sin], axis=-1), F32))


def _swap_halves(t):
    return pltpu.roll(t, HD // 2, axis=1)


TOK = 256


def _lane_block_spec(d, last=HD):
    return pl.BlockSpec((4, TOK // d, d * last), lambda i: (0, i, 0))


def _to_lane_blocks(dst, head, val, d, scr, dtype):
    w = val.shape[1]
    if d == 1:
        dst[head] = val.astype(dtype)
        return
    scr[...] = val
    for r in range(d):
        dst[head, :, r * w:(r + 1) * w] = scr[pl.ds(r, TOK // d, stride=d), :].astype(dtype)


def _from_lane_blocks(src, head, d, w, scr):
    if d == 1:
        return src[head].astype(F32)
    for r in range(d):
        scr[pl.ds(r, TOK // d, stride=d), :] = src[head, :, r * w:(r + 1) * w].astype(F32)
    return scr[...]


def _qk_prep(proj, gains, cos2, sin2):
    def body(q_ref, k_ref, v_ref, g_ref, c_ref, s_ref, *rest):
        outs, scr = rest[:-1], rest[-1]
        cos, sin = c_ref[...], s_ref[...]
        for which, (src, row_a, row_b) in enumerate(((q_ref, 0, 2), (k_ref, 1, 3), (v_ref, None, None))):
            for h in range(NH):
                y = src[:, h * HD:(h + 1) * HD]
                if row_a is not None:
                    y = y * lax.rsqrt(jnp.mean(y * y, axis=-1, keepdims=True) + EPS)
                    if h < NH_A:
                        y = y * g_ref[row_a:row_a + 1, :]
                        y = y * cos + _swap_halves(y) * sin
                    else:
                        y = y * g_ref[row_b:row_b + 1, :]
                if h < NH_A:
                    gi = h // 4
                    _to_lane_blocks(outs[3 * gi + which], h % 4, y, DILATIONS[gi], scr, BF16)
                else:
                    hb = h - NH_A
                    outs[9 + which][:, hb * HD:(hb + 1) * HD] = y.astype(BF16)

    def blk(c):
        return pl.BlockSpec((TOK, QKV), lambda i: (i, c))
    tab = pl.BlockSpec((TOK, HD), lambda i: (i, 0))
    out_specs, out_shape = [], []
    for d in DILATIONS:
        out_specs += [_lane_block_spec(d)] * 3
        out_shape += [jax.ShapeDtypeStruct((4, S // d, d * HD), BF16)] * 3
    out_specs += [pl.BlockSpec((TOK, D_BR), lambda i: (i, 0))] * 3
    out_shape += [jax.ShapeDtypeStruct((S, D_BR), BF16)] * 3
    outs = pl.pallas_call(
        body, name="qk_prep", grid=(S // TOK,),
        in_specs=[blk(0), blk(1), blk(2), pl.BlockSpec((8, HD), lambda i: (0, 0)), tab, tab],
        out_specs=out_specs, out_shape=out_shape,
        scratch_shapes=[pltpu.VMEM((TOK, HD), F32)],
        compiler_params=_params(("parallel",)),
    )(proj, proj, proj, gains, cos2, sin2)
    return [tuple(outs[3 * gi:3 * gi + 3]) for gi in range(3)], tuple(outs[9:12])


def _qk_prep_bwd(dproj, proj, gains, cos2, sin2, grads_a, grads_b):
    def body(dp_in, q_ref, k_ref, g_ref, c_ref, s_ref, *rest):
        grads, (dp_out, dg_ref, scr) = rest[:12], rest[12:]
        del dp_in
        cos, sin = c_ref[...], s_ref[...]

        def grad_of(which, h):
            if h < NH_A:
                gi = h // 4
                return _from_lane_blocks(grads[3 * gi + which], h % 4, DILATIONS[gi], HD, scr)
            hb = h - NH_A
            return grads[9 + which][:, hb * HD:(hb + 1) * HD].astype(F32)

        dg_rows = []
        for which, (src, base, row_a, row_b) in enumerate(((q_ref, 0, 0, 2), (k_ref, QKV, 1, 3))):
            dg_a = jnp.zeros((1, HD), F32)
            dg_b = jnp.zeros((1, HD), F32)
            for h in range(NH):
                t = src[:, h * HD:(h + 1) * HD]
                dy = grad_of(which, h)
                r = lax.rsqrt(jnp.mean(t * t, axis=-1, keepdims=True) + EPS)
                xh = t * r
                if h < NH_A:
                    dy = dy * cos - _swap_halves(dy) * sin
                    gain = g_ref[row_a:row_a + 1, :]
                    dg_a = dg_a + jnp.sum(dy * xh, axis=0, keepdims=True)
                else:
                    gain = g_ref[row_b:row_b + 1, :]
                    dg_b = dg_b + jnp.sum(dy * xh, axis=0, keepdims=True)
                u = dy * gain
                dx = r * (u - xh * jnp.mean(u * xh, axis=-1, keepdims=True))
                dp_out[:, base + h * HD:base + (h + 1) * HD] = dx.astype(BF16)
            dg_rows += [(row_a, dg_a), (row_b, dg_b)]
        for h in range(NH):
            dp_out[:, 2 * QKV + h * HD:2 * QKV + (h + 1) * HD] = grad_of(2, h).astype(BF16)

        @pl.when(pl.program_id(0) == 0)
        def _():
            dg_ref[...] = jnp.zeros((8, HD), F32)

        for row, val in dg_rows:
            dg_ref[row:row + 1, :] += val

    def blk(c):
        return pl.BlockSpec((TOK, QKV), lambda i: (i, c))
    tab = pl.BlockSpec((TOK, HD), lambda i: (i, 0))
    gain_spec = pl.BlockSpec((8, HD), lambda i: (0, 0))
    grad_specs = [s for d in DILATIONS for s in [_lane_block_spec(d)] * 3]
    grad_specs += [pl.BlockSpec((TOK, D_BR), lambda i: (i, 0))] * 3
    return pl.pallas_call(
        body, name="qk_prep_bwd", grid=(S // TOK,),
        in_specs=[pl.BlockSpec(memory_space=pl.ANY), blk(0), blk(1), gain_spec, tab, tab] + grad_specs,
        out_specs=[pl.BlockSpec((TOK, 3 * QKV), lambda i: (i, 0)), gain_spec],
        out_shape=[jax.ShapeDtypeStruct((S, D_IN), BF16), jax.ShapeDtypeStruct((8, HD), F32)],
        input_output_aliases={0: 0},
        scratch_shapes=[pltpu.VMEM((TOK, HD), F32)],
        compiler_params=_params(("arbitrary",)),
    )(dproj, proj, proj, gains, cos2, sin2, *[g for grp in grads_a for g in grp], *grads_b)


def _mix_fwd(oa, ob, w_pa, w_pb, proj, b_gate, *, tr=256):
    def body(oa_ref, ob_ref, pa_ref, pb_ref, la_ref, lb_ref, ba_ref, bb_ref, mix_ref, ya_ref, yb_ref):
        ya = jnp.concatenate([_dot(oa_ref[...], pa_ref[q], NN) for q in range(N_DEV)], axis=1)
        yb = jnp.concatenate([_dot(ob_ref[...], pb_ref[q], NN) for q in range(N_DEV)], axis=1)
        ga = jax.nn.sigmoid(la_ref[...] + ba_ref[...])
        gb = jax.nn.sigmoid(lb_ref[...] + bb_ref[...])
        mix_ref[...] = (ga * ya + gb * yb).astype(BF16)
        ya_ref[...] = ya.astype(BF16)
        yb_ref[...] = yb.astype(BF16)

    row = pl.BlockSpec((tr, D), lambda i: (i, 0))
    branch = pl.BlockSpec((tr, D_BR), lambda i: (i, 0))
    whole = pl.BlockSpec((N_DEV, D_BR, D // N_DEV), lambda i: (0, 0, 0))
    return pl.pallas_call(
        body, name="mix_fwd", grid=(S // tr,),
        in_specs=[branch, branch, whole, whole,
                  pl.BlockSpec((tr, D), lambda i: (i, 3)), pl.BlockSpec((tr, D), lambda i: (i, 4)),
                  pl.BlockSpec((1, D), lambda i: (0, 0)), pl.BlockSpec((1, D), lambda i: (0, 1))],
        out_specs=[row, row, row], out_shape=[jax.ShapeDtypeStruct((S, D), BF16)] * 3,
        compiler_params=_params(("parallel",)),
    )(oa, ob, w_pa, w_pb, proj, proj, b_gate, b_gate)


def _mix_bwd(dh1b, w_out, proj, b_gate, ya, yb, *, tr=256):
    def body(dh_ref, w_ref, la_ref, lb_ref, b_ref, ya_ref, yb_ref, dya_ref, dyb_ref, dp_ref, db_ref):
        dm = _dot(dh_ref[...], w_ref[...], NT)
        parts = []
        for l_ref, y_ref, dy_ref, lo in ((la_ref, ya_ref, dya_ref, 0), (lb_ref, yb_ref, dyb_ref, D)):
            g = jax.nn.sigmoid(l_ref[...] + b_ref[:, lo:lo + D])
            dy_ref[...] = (dm * g).astype(BF16)
            dl = dm * y_ref[...].astype(F32) * g * (1.0 - g)
            dp_ref[:, lo:lo + D] = dl.astype(BF16)
            parts.append(jnp.sum(dl, axis=0, keepdims=True))
        part = jnp.concatenate(parts, axis=1)

        @pl.when(pl.program_id(0) == 0)
        def _():
            db_ref[...] = part

        @pl.when(pl.program_id(0) > 0)
        def _():
            db_ref[...] += part

    row = pl.BlockSpec((tr, D), lambda i: (i, 0))
    vec = pl.BlockSpec((1, 2 * D), lambda i: (0, 0))
    gate_cols = pl.BlockSpec((pl.Element(tr), pl.Element(2 * D)), lambda i: (i * tr, 3 * QKV))
    return pl.pallas_call(
        body, name="mix_bwd", grid=(S // tr,),
        in_specs=[row, pl.BlockSpec((D, D), lambda i: (0, 0)),
                  pl.BlockSpec((tr, D), lambda i: (i, 3)), pl.BlockSpec((tr, D), lambda i: (i, 4)), vec, row, row],
        out_specs=[row, row, gate_cols, vec],
        out_shape=[jax.ShapeDtypeStruct((S, D), BF16), jax.ShapeDtypeStruct((S, D), BF16),
                   jax.ShapeDtypeStruct((S, D_IN), BF16), jax.ShapeDtypeStruct((1, 2 * D), F32)],
        compiler_params=_params(("arbitrary",)),
    )(dh1b, w_out, proj, proj, b_gate, ya, yb)


def _band_blocks(m_len):
    wk = min(m_len, QB + 2 * QB)
    return [(qb * QB, min(max(qb * QB - QB, 0), m_len - wk), wk) for qb in range(m_len // QB)]


def _band_scores(q, kw, q0, k0, wk):
    s = _dot(q, kw, NT) * SCALE
    qpos = q0 + lax.broadcasted_iota(jnp.int32, (QB, 1), 0)
    kpos = k0 + lax.broadcasted_iota(jnp.int32, (1, wk), 1)
    return jnp.where(jnp.abs(kpos - qpos) <= HALF_A, s, NEG)


def _attn_a_fwd(q, k, v, gi):
    d = DILATIONS[gi]
    m_len = S // d

    def body(q_ref, k_ref, v_ref, o_ref, lse_ref):
        for r in range(d):
            lanes = slice(r * HD, (r + 1) * HD)
            for q0, k0, wk in _band_blocks(m_len):
                s = _band_scores(q_ref[q0:q0 + QB, lanes], k_ref[k0:k0 + wk, lanes], q0, k0, wk)
                m = jnp.max(s, axis=-1, keepdims=True)
                p = jnp.exp(s - m)
                l = jnp.sum(p, axis=-1, keepdims=True)
                o_ref[q0:q0 + QB, lanes] = _dot(p.astype(BF16), v_ref[k0:k0 + wk, lanes], NN) / l
                lse_ref[q0:q0 + QB, r:r + 1] = m + jnp.log(l)

    head = pl.BlockSpec((None, m_len, d * HD), lambda h: (h, 0, 0))
    stat = pl.BlockSpec((None, m_len, d), lambda h: (h, 0, 0))
    return pl.pallas_call(
        body, name=f"attn_a_fwd_{gi}", grid=(4,),
        in_specs=[head, head, head], out_specs=[head, stat],
        out_shape=[jax.ShapeDtypeStruct((4, m_len, d * HD), F32), jax.ShapeDtypeStruct((4, m_len, d), F32)],
        compiler_params=_params(("parallel",)),
    )(q, k, v)


def _combine_a(os, lses):
    def body(o0, o1, o2, l0, l1, l2, oa_ref, lse_ref, scr, scr1):
        for h in range(4):
            o = [_from_lane_blocks(ref, h, d, HD, scr) for ref, d in zip((o0, o1, o2), DILATIONS)]
            a, b, c = (_from_lane_blocks(ref, h, d, 1, scr1) for ref, d in zip((l0, l1, l2), DILATIONS))
            m = jnp.maximum(jnp.maximum(a, b), c)
            wa, wb, wc = jnp.exp(a - m), jnp.exp(b - m), jnp.exp(c - m)
            tot = wa + wb + wc
            oa_ref[:, h * HD:(h + 1) * HD] = ((wa * o[0] + wb * o[1] + wc * o[2]) / tot).astype(BF16)
            lse_ref[h] = m + jnp.log(tot)

    return pl.pallas_call(
        body, name="combine_a", grid=(S // TOK,),
        in_specs=[_lane_block_spec(d) for d in DILATIONS] + [_lane_block_spec(d, 1) for d in DILATIONS],
        out_specs=[pl.BlockSpec((TOK, D_BR), lambda i: (i, 0)), pl.BlockSpec((4, TOK, 1), lambda i: (0, i, 0))],
        out_shape=[jax.ShapeDtypeStruct((S, D_BR), BF16), jax.ShapeDtypeStruct((4, S, 1), F32)],
        scratch_shapes=[pltpu.VMEM((TOK, HD), F32), pltpu.VMEM((TOK, 1), F32)],
        compiler_params=_params(("parallel",)),
    )(*os, *lses)


def _proj_a_bwd(dya, w_pa, oa, lse):
    kg = D // N_DEV

    def body(dy_ref, w_ref, o_ref, l_ref, *rest):
        outs, (scr, scr1) = rest[:9], rest[9:]
        doa = _dot(dy_ref[:, 0:kg], w_ref[0], NT)
        for q in range(1, N_DEV):
            doa = doa + _dot(dy_ref[:, q * kg:(q + 1) * kg], w_ref[q], NT)
        for h in range(4):
            do = doa[:, h * HD:(h + 1) * HD]
            dsum = jnp.sum(do * o_ref[:, h * HD:(h + 1) * HD].astype(F32), axis=-1, keepdims=True)
            for gi, d in enumerate(DILATIONS):
                _to_lane_blocks(outs[3 * gi], h, do, d, scr, BF16)
                _to_lane_blocks(outs[3 * gi + 1], h, l_ref[h], d, scr1, F32)
                _to_lane_blocks(outs[3 * gi + 2], h, dsum, d, scr1, F32)

    row = pl.BlockSpec((TOK, D_BR), lambda i: (i, 0))
    out_specs, out_shape = [], []
    for d in DILATIONS:
        out_specs += [_lane_block_spec(d), _lane_block_spec(d, 1), _lane_block_spec(d, 1)]
        out_shape += [jax.ShapeDtypeStruct((4, S // d, d * HD), BF16)] + [jax.ShapeDtypeStruct((4, S // d, d), F32)] * 2
    outs = pl.pallas_call(
        body, name="proj_a_bwd", grid=(S // TOK,),
        in_specs=[pl.BlockSpec((TOK, D), lambda i: (i, 0)),
                  pl.BlockSpec((N_DEV, D_BR, kg), lambda i: (0, 0, 0)),
                  row, pl.BlockSpec((4, TOK, 1), lambda i: (0, i, 0))],
        out_specs=out_specs, out_shape=out_shape,
        scratch_shapes=[pltpu.VMEM((TOK, HD), F32), pltpu.VMEM((TOK, 1), F32)],
        compiler_params=_params(("parallel",)),
    )(dya, w_pa, oa, lse)
    return [tuple(outs[3 * gi:3 * gi + 3]) for gi in range(3)]


def _attn_a_bwd(q, k, v, do, lse, dsum, gi):
    d = DILATIONS[gi]
    m_len = S // d

    def body(q_ref, k_ref, v_ref, do_ref, lse_ref, dsum_ref, dq_ref, dk_out, dv_out, dk_ref, dv_ref):
        dk_ref[...] = jnp.zeros((m_len, d * HD), F32)
        dv_ref[...] = jnp.zeros((m_len, d * HD), F32)
        for r in range(d):
            lanes = slice(r * HD, (r + 1) * HD)
            for q0, k0, wk in _band_blocks(m_len):
                rows, keys = slice(q0, q0 + QB), slice(k0, k0 + wk)
                qv, kw, vw, dov = q_ref[rows, lanes], k_ref[keys, lanes], v_ref[keys, lanes], do_ref[rows, lanes]
                p = jnp.exp(_band_scores(qv, kw, q0, k0, wk) - lse_ref[rows, r:r + 1])
                ds = (p * (_dot(dov, vw, NT) - dsum_ref[rows, r:r + 1]) * SCALE).astype(BF16)
                dq_ref[rows, lanes] = _dot(ds, kw, NN).astype(BF16)
                dk_ref[keys, lanes] += _dot(ds, qv, TN)
                dv_ref[keys, lanes] += _dot(p.astype(BF16), dov, TN)
        dk_out[...] = dk_ref[...].astype(BF16)
        dv_out[...] = dv_ref[...].astype(BF16)

    head = pl.BlockSpec((None, m_len, d * HD), lambda h: (h, 0, 0))
    stat = pl.BlockSpec((None, m_len, d), lambda h: (h, 0, 0))
    shape = jax.ShapeDtypeStruct((4, m_len, d * HD), BF16)
    return pl.pallas_call(
        body, name=f"attn_a_bwd_{gi}", grid=(4,),
        in_specs=[head, head, head, head, stat, stat], out_specs=[head, head, head],
        out_shape=[shape, shape, shape],
        scratch_shapes=[pltpu.VMEM((m_len, d * HD), F32)] * 2,
        compiler_params=_params(("arbitrary",)),
    )(q, k, v, do, lse, dsum)


KEYS_B = WIN_R * GRID_W
N_OFF = WIN_R


def _bias_constants():
    q = np.arange(GRID_W)[:, None]
    kc = np.arange(GRID_W)[None, :]
    dc = np.clip(kc - q, -(WIN_C - 1), WIN_C - 1) + (WIN_C - 1)
    expand = np.zeros((HD, GRID_W * GRID_W), np.float32)
    expand[dc.reshape(-1), np.arange(GRID_W * GRID_W)] = 1.0
    cs = np.clip(q - WIN_C // 2, 0, GRID_W - WIN_C)
    keep = ((kc >= cs) & (kc < cs + WIN_C)).reshape(1, -1).astype(np.float32)
    sel = np.zeros((64, 4 * N_OFF * WIN_R), np.float32)
    for h in range(4):
        for off in range(N_OFF):
            for j in range(WIN_R):
                sel[h * (2 * WIN_R - 1) + off + j, (h * N_OFF + off) * WIN_R + j] = 1.0
    return jnp.asarray(expand), jnp.asarray(keep), jnp.asarray(sel)


def _bias_expand(rpb_pad, expand, keep, sel):
    def body(r_ref, e_ref, k_ref, s_ref, o_ref):
        t = lax.dot_general(r_ref[...], e_ref[...], NN, precision=lax.Precision.HIGHEST,
                            preferred_element_type=F32)
        rows = lax.dot_general(s_ref[...], t, TN, precision=lax.Precision.HIGHEST,
                               preferred_element_type=F32)
        o_ref[...] = jnp.where(k_ref[...] > 0.5, rows, NEG)

    return pl.pallas_call(
        body, name="bias_expand",
        out_shape=jax.ShapeDtypeStruct((4 * N_OFF * WIN_R, GRID_W * GRID_W), F32),
        compiler_params=pltpu.CompilerParams(vmem_limit_bytes=VMEM_LIMIT),
    )(rpb_pad, expand, keep, sel)


def _bias_reduce(dbias_tab):
    lane0 = GRID_W - WIN_C
    flip = np.zeros((GRID_W, GRID_W), np.float32)
    flip[np.arange(GRID_W), GRID_W - 1 - np.arange(GRID_W)] = 1.0
    place = np.zeros((WIN_R, 64, 4 * N_OFF), np.float32)
    for j in range(WIN_R):
        for h in range(4):
            for off in range(N_OFF):
                place[j, h * (2 * WIN_R - 1) + off + j, h * N_OFF + off] = 1.0

    def exact(x, y):
        return lax.dot_general(x, y, NN, precision=lax.Precision.HIGHEST, preferred_element_type=F32)

    def body(x_ref, flip_ref, place_ref, o_ref, z_ref):
        for h in range(4):
            for off in range(N_OFF):
                lined_up = pltpu.roll(exact(flip_ref[...], x_ref[h, off]), 0, axis=1, stride=1, stride_axis=0)
                z_ref[h * N_OFF + off:h * N_OFF + off + 1, :] = jnp.sum(lined_up, axis=0, keepdims=True)
        acc = jnp.zeros((64, HD), F32)
        for j in range(WIN_R):
            at_zero = pltpu.roll(z_ref[...], (KEYS_B - (j * GRID_W + lane0)) % KEYS_B, axis=1)[:, :HD]
            acc = acc + exact(place_ref[j], at_zero)
        lane = lax.broadcasted_iota(jnp.int32, (64, HD), 1)
        o_ref[...] = jnp.where(lane < 2 * WIN_C - 1, acc, 0.0)

    return pl.pallas_call(
        body, name="bias_reduce", out_shape=jax.ShapeDtypeStruct((64, HD), F32),
        scratch_shapes=[pltpu.VMEM((4 * N_OFF, KEYS_B), F32)],
        compiler_params=pltpu.CompilerParams(vmem_limit_bytes=VMEM_LIMIT),
    )(dbias_tab, jnp.asarray(flip), jnp.asarray(place))


def _rows_to_tab(rows):
    t = rows.reshape(4, N_OFF, WIN_R, GRID_W, GRID_W)
    return t.transpose(0, 1, 3, 2, 4).reshape(4, N_OFF, GRID_W, KEYS_B)


def _row_window(r):
    r0 = jnp.clip(r - WIN_R // 2, 0, ROWS - WIN_R)
    off = r0 + (WIN_R - 1) - r
    return pl.multiple_of(r * GRID_W, GRID_W), pl.multiple_of(r0 * GRID_W, GRID_W), off


def _attn_b_fwd(qn, kn, vb, bias_tab):
    def body(q_ref, k_ref, v_ref, b_ref, o_ref, lse_ref):
        def row(r, carry):
            qs, ks, off = _row_window(r)
            q = q_ref[pl.ds(qs, GRID_W), :]
            s = lax.dot_general(q, k_ref[pl.ds(ks, KEYS_B), :], NT, preferred_element_type=F32) * SCALE
            s = s + b_ref[off]
            m = jnp.max(s, axis=-1, keepdims=True)
            p = jnp.exp(s - m)
            l = jnp.sum(p, axis=-1, keepdims=True)
            o = lax.dot_general(p.astype(BF16), v_ref[pl.ds(ks, KEYS_B), :], NN, preferred_element_type=F32)
            o_ref[pl.ds(qs, GRID_W), :] = (o / l).astype(BF16)
            lse_ref[pl.ds(qs, GRID_W), :] = m + jnp.log(l)
            return carry

        lax.fori_loop(0, ROWS, row, 0, unroll=8)

    full = pl.BlockSpec((S, HD), lambda h: (0, h))
    return pl.pallas_call(
        body, name="attn_b_fwd", grid=(4,),
        in_specs=[full, full, full, pl.BlockSpec((None, N_OFF, GRID_W, KEYS_B), lambda h: (h, 0, 0, 0))],
        out_specs=[pl.BlockSpec((S, HD), lambda h: (0, h)), pl.BlockSpec((None, S, 1), lambda h: (h, 0, 0))],
        out_shape=[jax.ShapeDtypeStruct((S, D_BR), BF16), jax.ShapeDtypeStruct((4, S, 1), F32)],
        compiler_params=_params(("parallel",)),
    )(qn, kn, vb, bias_tab)


def _attn_b_bwd(qn, kn, vb, bias_tab, ob, dob, lse):
    def body(q_ref, k_ref, v_ref, b_ref, o_ref, do_ref, lse_ref, dq_ref, dk_out, dv_out, db_ref, dk_ref, dv_ref):
        dk_ref[...] = jnp.zeros((S, HD), F32)
        dv_ref[...] = jnp.zeros((S, HD), F32)
        db_ref[...] = jnp.zeros((N_OFF, GRID_W, KEYS_B), F32)

        def row(r, carry):
            qs, ks, off = _row_window(r)
            rows = pl.ds(qs, GRID_W)
            keys = pl.ds(ks, KEYS_B)
            q = q_ref[rows, :]
            kw = k_ref[keys, :]
            s = lax.dot_general(q, kw, NT, preferred_element_type=F32) * SCALE + b_ref[off]
            p = jnp.exp(s - lse_ref[rows, :])
            do = do_ref[rows, :]
            dobf = do.astype(BF16)
            dsum = jnp.sum(do * o_ref[rows, :].astype(F32), axis=-1, keepdims=True)
            dp = lax.dot_general(dobf, v_ref[keys, :], NT, preferred_element_type=F32)
            ds = p * (dp - dsum)
            db_ref[off] += ds
            dsb = (ds * SCALE).astype(BF16)
            dq_ref[rows, :] = lax.dot_general(dsb, kw, NN, preferred_element_type=F32).astype(BF16)
            dk_ref[keys, :] += lax.dot_general(dsb, q, TN, preferred_element_type=F32)
            dv_ref[keys, :] += lax.dot_general(p.astype(BF16), dobf, TN, preferred_element_type=F32)
            return carry

        lax.fori_loop(0, ROWS, row, 0, unroll=8)
        dk_out[...] = dk_ref[...].astype(BF16)
        dv_out[...] = dv_ref[...].astype(BF16)

    full = pl.BlockSpec((S, HD), lambda h: (0, h))
    slot = pl.BlockSpec((S, HD), lambda h: (0, h))
    tab = pl.BlockSpec((None, N_OFF, GRID_W, KEYS_B), lambda h: (h, 0, 0, 0))
    shape = jax.ShapeDtypeStruct((S, D_BR), BF16)
    return pl.pallas_call(
        body, name="attn_b_bwd", grid=(4,),
        in_specs=[full, full, full, tab, slot, slot, pl.BlockSpec((None, S, 1), lambda h: (h, 0, 0))],
        out_specs=[slot, slot, slot, tab],
        out_shape=[shape, shape, shape, jax.ShapeDtypeStruct((4, N_OFF, GRID_W, KEYS_B), F32)],
        scratch_shapes=[pltpu.VMEM((S, HD), F32)] * 2,
        compiler_params=_params(("arbitrary",)),
    )(qn, kn, vb, bias_tab, ob, dob, lse)


def _epi_relu_sq(acc, ex, outs):
    u = jnp.maximum(acc, 0.0)
    outs[0][...] = u.astype(BF16)
    outs[1][...] = (u * u).astype(BF16)


def _epi_relu_sq_bwd(acc, ex, outs):
    outs[0][...] = (acc * (2.0 * ex[0][...].astype(F32))).astype(BF16)


def _epi_loss_head(acc, ex, outs):
    e = acc + ex[0][...] - ex[1][...]
    dy = e * (1.0 / D)
    outs[0][...] = dy
    outs[1][...] = dy.astype(BF16)
    part = (0.5 / D) * jnp.sum(jnp.sum(e * e, axis=-1, keepdims=True), axis=0, keepdims=True)
    first = (pl.program_id(0) == 0) & (pl.program_id(1) == 0)

    @pl.when(first)
    def _():
        outs[2][...] = part

    @pl.when(jnp.logical_not(first))
    def _():
        outs[2][...] += part


def _local_step(x, target, norm_mix, b_gate, gains, rpb_pad, norm_ffn,
                w_in, w_pa, w_pb, w_out, w_up, w_down, weight_grads, riders=lambda name: None):
    def ridden(name, *args, **kwargs):
        ride = riders(name)
        if ride is None:
            return _mm_nt(*args, name=name, **kwargs)
        out, rode = _mm_nt(*args, name=name, rider=ride[0], **kwargs)
        ride[1](rode)
        return out

    cos2, sin2 = _rope_tables()
    expand, keep, sel = _bias_constants()
    w_out3 = w_out[None]

    xn, rstd1 = _rms_fwd(x, norm_mix, name="rms_mix")
    per_dev = D_IN // N_DEV
    proj = _mm_nn(xn, w_in[0], tm=S, tn=W_IN_SPLIT, name="proj_0", stride=per_dev, width=D_IN)
    proj = _mm_nn(xn, w_in[1], tm=S, tn=per_dev - W_IN_SPLIT, name="proj_1", stride=per_dev, width=D_IN,
                  col0=W_IN_SPLIT, into=(proj,))
    qkv_a, qkv_b = _qk_prep(proj, gains, cos2, sin2)
    fwd_a = [_attn_a_fwd(*qkv_a[gi], gi) for gi in range(3)]
    oa, lse_a = _combine_a([o for o, _ in fwd_a], [l for _, l in fwd_a])
    bias_tab = _rows_to_tab(_bias_expand(rpb_pad, expand, keep, sel))
    ob, lse_b = _attn_b_fwd(*qkv_b, bias_tab)
    mixed, ya, yb = _mix_fwd(oa, ob, w_pa, w_pb, proj, b_gate)
    h1 = _mm_nn(mixed, w_out3, tm=1024, tn=1024, name="out_proj", epi=_epi_residual, extra=(x,))
    hn, rstd2 = _rms_fwd(h1, norm_ffn, name="rms_ffn")
    u, usq = _mm_nn(hn, w_up, tm=S, tn=512, name="ffn_up", epi=_epi_relu_sq,
                    out_dtypes=(BF16, BF16))
    dy, dyb, loss = _mm_nn(usq, w_down[0], tm=512, tn=512, name="ffn_down_0", epi=_epi_loss_head,
                           extra=(h1, target), out_dtypes=(F32, BF16), total=True, width=D)
    dy, dyb, loss_1 = _mm_nn(usq, w_down[1], tm=512, tn=512, name="ffn_down_1", epi=_epi_loss_head,
                             extra=(h1, target), out_dtypes=(F32, BF16), total=True, width=D,
                             col0=D // 2, into=(dy, dyb))
    loss = loss + loss_1

    sent = weight_grads("w_down", {5: (usq, dyb)})
    du = _mm_nt(dyb, w_down[0], more_b=(w_down[1],), tm=1024, tn=1024, name="ffn_down_bwd", out_dtype=BF16,
                epi=_epi_relu_sq_bwd, extra=(u,), after=sent)
    sent = weight_grads("w_up", {4: (hn, du)})
    dhn = ridden("ffn_up_bwd", du, w_up, tm=512, tn=512, after=sent)
    dh1, dh1b, g_norm_ffn = _rms_bwd(dhn, h1, rstd2, norm_ffn, dy, name="rms_ffn_bwd", bf16_copy=True)

    dya, dyb2, dproj, g_b = _mix_bwd(dh1b, w_out, proj, b_gate, ya, yb)
    sent = weight_grads("w_mix", {3: (mixed, dh1b), 1: (oa, dya), 2: (ob, dyb2)})
    dob = _mm_nt(dyb2, w_pb, tm=1024, tn=D_BR, name="proj_b_bwd", after=sent)
    prep = _proj_a_bwd(dya, w_pa, oa, lse_a)
    grads_a = [_attn_a_bwd(*qkv_a[gi], *prep[gi], gi) for gi in range(3)]
    dqb, dkb, dvb, dbias = _attn_b_bwd(*qkv_b, bias_tab, ob, dob, lse_b)
    g_rpb = _bias_reduce(dbias)
    dproj, g_gains = _qk_prep_bwd(dproj, proj, gains, cos2, sin2, grads_a, (dqb, dkb, dvb))
    sent = weight_grads("w_in", {0: (xn, dproj)})
    dxn = ridden("proj_bwd", dproj, w_in[0], more_b=(w_in[1],), interleaved=True, tm=256, tn=512, after=sent)
    grad_x, g_norm_mix = _rms_bwd(dxn, x, rstd1, norm_mix, dh1, name="rms_mix_bwd", bf16_copy=False)

    small = (g_norm_mix, g_b, g_gains, g_rpb, g_norm_ffn)
    return loss, grad_x, small


def _cast_bf16(w, *, part=0, parts=1, window=None, after=(), tr=256):
    rows, cols = w.shape[0], w.shape[1] // parts
    tr = min(tr, rows)
    src = pl.BlockSpec((tr, cols), lambda i: (i, part))
    if window is not None:
        part, cols = window
        src = pl.BlockSpec((pl.Element(tr), pl.Element(cols)), lambda i: (i * tr, part))

    def body(w_ref, *rest):
        rest[-1][...] = w_ref[...].astype(BF16)

    return pl.pallas_call(
        body, name=f"cast_{rows}x{cols}_{part}", grid=(rows // tr,),
        in_specs=[src] + [pl.BlockSpec(memory_space=pl.ANY)] * len(after),
        out_specs=pl.BlockSpec((tr, cols), lambda i: (i, 0)),
        out_shape=jax.ShapeDtypeStruct((rows, cols), BF16), compiler_params=_params(("parallel",)),
    )(w, *after)


def _me_and_peers():
    x, y, c = lax.axis_index("x"), lax.axis_index("y"), lax.axis_index("c")
    me = 4 * x + 2 * y + c
    peers = []
    for k in range(1, N_DEV):
        px = 1 - x if k & 4 else x
        py = 1 - y if k & 2 else y
        pc = 1 - c if k & 1 else c
        peers.append(((px, py, pc), 4 * px + 2 * py + pc))
    return me, peers


def _gather_on_sequencer(shards, name):
    n = len(shards)
    hbm = pltpu.MemorySpace.HBM
    ins = [jax.new_ref(s, memory_space=hbm) for s in shards]
    outs = [jax.empty_ref(jax.ShapeDtypeStruct((N_DEV,) + s.shape, s.dtype), memory_space=hbm) for s in shards]
    n_sem = 8

    @_sequencer(name, ((n, n_sem), (n, n_sem), (n,)), 0)
    def launch(send, recv, lsem):
        x, y, c = lax.axis_index("x"), lax.axis_index("y"), lax.axis_index("c")
        me, sibling = (x, y, c), (x, y, 1 - c)
        x_chip, y_chip, diagonal = (1 - x, y, c), (x, 1 - y, c), (1 - x, 1 - y, c)
        _handshake([sibling, x_chip, y_chip])

        def copy(w, k, block, to, src=None, half=None):
            px, py, pc = block
            dst = outs[w].at[4 * px + 2 * py + pc]
            if half is not None:
                rows = shards[w].shape[0] // 2
                dst = dst.at[pl.ds(half * rows, rows)]
            return pltpu.make_async_remote_copy(dst if src is None else src, dst, send.at[w, k], recv.at[w, k],
                                                device_id=to, device_id_type=MESH)

        local = [pltpu.make_async_copy(ins[w], outs[w].at[4 * x + 2 * y + c], lsem.at[w]) for w in range(n)]
        for cp in local:
            cp.start()
        sent = []
        for w in range(n):
            sent += [copy(w, 1, me, x_chip, src=ins[w]), copy(w, 2, me, y_chip, src=ins[w]),
                     copy(w, 0, me, sibling, src=ins[w])]
        for cp in sent:
            cp.start()
        for w in range(n):
            copy(w, 1, x_chip, me).wait_recv()
            sent += [copy(w, 3, x_chip, y_chip, half=0), copy(w, 5, x_chip, sibling)]
            sent[-2].start()
            sent[-1].start()
            copy(w, 2, y_chip, me).wait_recv()
            sent += [copy(w, 4, y_chip, x_chip, half=1), copy(w, 6, y_chip, sibling)]
            sent[-2].start()
            sent[-1].start()
        for w in range(n):
            copy(w, 3, diagonal, me, half=0).wait_recv()
            copy(w, 4, diagonal, me, half=1).wait_recv()
            sent.append(copy(w, 7, diagonal, sibling))
            sent[-1].start()
        for w in range(n):
            copy(w, 0, sibling, me).wait_recv()
            for k, chip in ((5, x_chip), (6, y_chip), (7, diagonal)):
                px, py, _ = chip
                copy(w, k, (px, py, 1 - c), me).wait_recv()
        for cp in sent:
            cp.wait_send()
        for cp in local:
            cp.wait()

    launch()
    return [o[...] for o in outs]


N_CHIP = 4


def _sequencer(name, n_sems, collective_id):
    return functools.partial(
        pl.kernel, mesh=plsc.ScalarSubcoreMesh(axis_name="seq", num_cores=1), name=name,
        scratch_types=tuple(pltpu.SemaphoreType.DMA(s) for s in n_sems),
        compiler_params=pltpu.CompilerParams(collective_id=collective_id))


def _handshake(peers):
    barrier = pltpu.get_barrier_semaphore()
    for peer in peers:
        pl.semaphore_signal(barrier, inc=1, device_id=peer, device_id_type=MESH)
    pl.semaphore_wait(barrier, len(peers))


def _chip_exchange_on_sequencer(parts, name):
    n = len(parts)
    hbm = pltpu.MemorySpace.HBM
    ins = [jax.new_ref(p, memory_space=hbm) for p in parts]
    outs = [jax.empty_ref(jax.ShapeDtypeStruct(p.shape, p.dtype), memory_space=hbm) for p in parts]

    @_sequencer(name, ((n, 3), (n, 3), (n,)), 2)
    def launch(send, recv, lsem):
        x, y, c = lax.axis_index("x"), lax.axis_index("y"), lax.axis_index("c")
        mine = 2 * x + y
        chips = [(1 - x, y), (x, 1 - y), (1 - x, 1 - y)]
        _handshake([(*chip, c) for chip in chips])
        local = [pltpu.make_async_copy(ins[w].at[mine], outs[w].at[mine], lsem.at[w]) for w in range(n)]
        for cp in local:
            cp.start()
        sends = []
        for w in range(n):
            for j, (px, py) in enumerate(chips):
                cp = pltpu.make_async_remote_copy(ins[w].at[2 * px + py], outs[w].at[mine],
                                                  send.at[w, j], recv.at[w, j],
                                                  device_id=(px, py, c), device_id_type=MESH)
                cp.start()
                sends.append(cp)
        for w in range(n):
            for j, (px, py) in enumerate(chips):
                pltpu.make_async_remote_copy(ins[w].at[mine], outs[w].at[2 * px + py],
                                             send.at[w, j], recv.at[w, j],
                                             device_id=(px, py, c), device_id_type=MESH).wait_recv()
        for cp in sends:
            cp.wait_send()
        for cp in local:
            cp.wait()

    launch()
    return [o[...] for o in outs]


GRAD_TILES = (dict(blocks_on="cols", tm=512, tn=1280), dict(blocks_on="cols", tm=512, tn=256),
              dict(blocks_on="cols", tm=512, tn=256), dict(blocks_on="rows", tm=256, tn=2048),
              dict(blocks_on="cols", tm=1024, tn=1024), dict(blocks_on="rows", tm=1024, tn=1024))


def _mm_tn_pair(a, b, *, blocks_on, tm, tn, name):
    t_len, m = a.shape
    n = b.shape[1]
    if blocks_on == "rows":
        rows, cols, inner = m // N_DEV, n, n // tn
        assert tm == rows
        a_spec = pl.BlockSpec((t_len, tm), lambda p, t, blk: (0, blk[p]))
        b_spec = pl.BlockSpec((t_len, tn), lambda p, t, blk: (0, t))
        out_spec = pl.BlockSpec((None, tm, tn), lambda p, t, blk: (
            jnp.maximum(p - N_CHIP, 0), 0, jnp.where(p < N_CHIP, 0, t)))
    else:
        rows, cols, inner = m, n // N_DEV, m // tm
        assert tn == cols
        a_spec = pl.BlockSpec((t_len, tm), lambda p, t, blk: (0, t))
        b_spec = pl.BlockSpec((t_len, tn), lambda p, t, blk: (0, blk[p]))
        out_spec = pl.BlockSpec((None, tm, tn), lambda p, t, blk: (
            jnp.maximum(p - N_CHIP, 0), jnp.where(p < N_CHIP, 0, t), 0))

    def body(blk_ref, a_ref, b_ref, o_ref, land, stage, send_sem, recv_sem):
        del blk_ref
        p, t = pl.program_id(0), pl.program_id(1)
        step = p * inner + t
        x, y, c = lax.axis_index("x"), lax.axis_index("y"), lax.axis_index("c")
        tile = _dot(a_ref[...], b_ref[...], TN)

        def to_sibling(slot, chip, piece):
            return pltpu.make_async_remote_copy(stage.at[slot], land.at[chip, piece], send_sem.at[slot],
                                                recv_sem.at[chip, piece],
                                                device_id=(x, y, 1 - c), device_id_type=MESH)

        @pl.when(p < N_CHIP)
        def _():
            slot = step % 2

            @pl.when(step >= 2)
            def _():
                to_sibling(slot, 0, 0).wait_send()

            stage[slot] = tile.astype(BF16)
            to_sibling(slot, p, t).start()

        @pl.when(step == N_CHIP * inner)
        def _():
            for slot in range(min(2, N_CHIP * inner)):
                to_sibling(slot, 0, 0).wait_send()

        @pl.when(p >= N_CHIP)
        def _():
            chip = p - N_CHIP
            to_sibling(0, chip, t).wait_recv()
            o_ref[...] = (tile + land[chip, t].astype(F32)).astype(BF16)

    c = lax.axis_index("c")
    order = jnp.stack([2 * ch + 1 - c for ch in range(N_CHIP)] + [2 * ch + c for ch in range(N_CHIP)])
    return pl.pallas_call(
        body, name=name,
        grid_spec=pltpu.PrefetchScalarGridSpec(
            num_scalar_prefetch=1, grid=(N_DEV, inner), in_specs=[a_spec, b_spec], out_specs=out_spec,
            scratch_shapes=[pltpu.VMEM((N_CHIP, inner, tm, tn), BF16), pltpu.VMEM((2, tm, tn), BF16),
                            pltpu.SemaphoreType.DMA((2,)), pltpu.SemaphoreType.DMA((N_CHIP, inner))]),
        out_shape=jax.ShapeDtypeStruct((N_CHIP, rows, cols), BF16),
        compiler_params=_params(("arbitrary", "arbitrary")),
    )(order.astype(jnp.int32), a, b)


def _adamw_math(g, w, m, v):
    m2 = B1 * m + (1.0 - B1) * g
    v2 = B2 * v + (1.0 - B2) * (g * g)
    delta = -LR * ((m2 / BC1) / (jnp.sqrt(v2 / BC2) + AEPS) + WD * w)
    return delta, m2, v2


def _adamw_block(ins, outs):
    p_ref, w_ref, m_ref, v_ref = ins
    g = p_ref[0].astype(F32)
    for b in range(1, N_CHIP):
        g = g + p_ref[b].astype(F32)
    delta, m2, v2 = _adamw_math(g, w_ref[...], m_ref[...], v_ref[...])
    for ref, val in zip(outs, (g, delta, m2, v2)):
        ref[...] = val


class _Rider(NamedTuple):
    inputs: tuple
    in_specs: list
    out_shape: list
    out_specs: list
    body: Callable


def _adamw_rider(parts, w, m, v):
    rows, cols = w.shape

    def rider(steps, step_of):
        rr = rows // steps
        blk = pl.BlockSpec((rr, cols), lambda *ids: (step_of(*ids[:2]), 0))
        chips = pl.BlockSpec((N_CHIP, rr, cols), lambda *ids: (0, step_of(*ids[:2]), 0))
        shape = jax.ShapeDtypeStruct((rows, cols), F32)
        return _Rider((parts, w, m, v), [chips, blk, blk, blk], [shape] * 4, [blk] * 4, _adamw_block)

    return rider


def _adamw(parts, w, m, v, *, name, after=(), tr=256):
    rows, cols = w.shape

    def body(*refs):
        _adamw_block(refs[:4], refs[4 + len(after):])

    spec = pl.BlockSpec((tr, cols), lambda i: (i, 0))
    shape = jax.ShapeDtypeStruct((rows, cols), F32)
    return pl.pallas_call(
        body, name=name, grid=(rows // tr,),
        in_specs=[pl.BlockSpec((N_CHIP, tr, cols), lambda i: (0, i, 0)), spec, spec, spec]
        + [pl.BlockSpec(memory_space=pl.ANY)] * len(after),
        out_specs=[spec] * 4, out_shape=[shape] * 4,
        compiler_params=_params(("parallel",)),
    )(parts, w, m, v, *after)


def _small_exchange(part, after=()):
    rows = part.shape[0]

    def body(p_ref, *rest):
        g_ref, buf, send, recv = rest[len(after):]
        me, peers = _me_and_peers()
        buf[me] = p_ref[...]
        sends = []
        for k, (dev, _) in enumerate(peers):
            cp = pltpu.make_async_remote_copy(p_ref, buf.at[me], send.at[k], recv.at[k],
                                              device_id=dev, device_id_type=MESH)
            cp.start()
            sends.append(cp)
        for k, (dev, idx) in enumerate(peers):
            pltpu.make_async_remote_copy(p_ref, buf.at[idx], send.at[k], recv.at[k],
                                         device_id=dev, device_id_type=MESH).wait_recv()
        for cp in sends:
            cp.wait_send()
        g = buf[0]
        for b in range(1, N_DEV):
            g = g + buf[b]
        g_ref[...] = g

    vm = pl.BlockSpec(memory_space=pltpu.VMEM)
    return pl.pallas_call(
        body, name="small_params_exchange",
        in_specs=[vm] + [pl.BlockSpec(memory_space=pl.ANY)] * len(after),
        out_specs=vm, out_shape=jax.ShapeDtypeStruct((rows, HD), F32),
        scratch_shapes=[pltpu.VMEM((N_DEV, rows, HD), F32),
                        pltpu.SemaphoreType.DMA((N_DEV - 1,)), pltpu.SemaphoreType.DMA((N_DEV - 1,))],
    )(part, *after)


def _small_adamw(g, w, m, v):
    def body(g_ref, w_ref, m_ref, v_ref, *outs):
        g = g_ref[...]
        delta, m2, v2 = _adamw_math(g, w_ref[...], m_ref[...], v_ref[...])
        for k, val in enumerate((g, delta, m2, v2)):
            norm_mix, b_gate, qa, ka, qb, kb, rpb, norm_ffn = outs[8 * k:8 * k + 8]
            for dst, row0, n_rows in ((norm_mix, 0, 16), (b_gate, 16, 32), (norm_ffn, 120, 16)):
                for r in range(n_rows):
                    dst[:, r * HD:(r + 1) * HD] = val[row0 + r:row0 + r + 1, :]
            for i, dst in enumerate((qa, ka, qb, kb)):
                dst[...] = val[48 + i:49 + i, :]
            rpb[...] = val[56:120, :]
        outs[32][...] = g[LOSS_ROW:LOSS_ROW + 1, 0:1]

    vm = pl.BlockSpec(memory_space=pltpu.VMEM)
    kinds = [jax.ShapeDtypeStruct(sh, F32) for sh in
             ((1, D), (1, 2 * D), (1, HD), (1, HD), (1, HD), (1, HD), (64, HD), (1, D))]
    outs = pl.pallas_call(
        body, name="small_params_adamw", in_specs=[vm] * 4, out_specs=[vm] * 33,
        out_shape=kinds * 4 + [jax.ShapeDtypeStruct((1, 1), F32)],
    )(g, w, m, v)
    return [outs[8 * k:8 * k + 8] for k in range(4)], outs[32]


def _pack_small(norm_mix, b_gate, qa, ka, qb, kb, rpb, norm_ffn):
    gains = jnp.concatenate([qa, ka, qb, kb, jnp.zeros((4, HD), F32)], axis=0)
    rpb_pad = jnp.pad(rpb.reshape(4 * (2 * WIN_R - 1), 2 * WIN_C - 1), ((0, 4), (0, HD - (2 * WIN_C - 1))))
    return jnp.concatenate([norm_mix.reshape(16, HD), b_gate.reshape(32, HD), gains, rpb_pad,
                            norm_ffn.reshape(16, HD), jnp.zeros((8, HD), F32)], axis=0)


LOSS_ROW = 136


def _rpb_from_rows(rows):
    return rows[:60, :2 * WIN_C - 1].reshape(1, 4, 2 * WIN_R - 1, 2 * WIN_C - 1)


def kernel(x, norm_mix, w_in, b_gate, q_norm_a, k_norm_a, q_norm_b, k_norm_b, rpb_b, w_proj_a, w_proj_b, w_out, norm_ffn, w_up, w_down, loss_target, m_norm_mix, m_w_in, m_b_gate, m_q_norm_a, m_k_norm_a, m_q_norm_b, m_k_norm_b, m_rpb_b, m_w_proj_a, m_w_proj_b, m_w_out, m_norm_ffn, m_w_up, m_w_down, v_norm_mix, v_w_in, v_b_gate, v_q_norm_a, v_k_norm_a, v_q_norm_b, v_k_norm_b, v_rpb_b, v_w_proj_a, v_w_proj_b, v_w_out, v_norm_ffn, v_w_up, v_w_down):
    big_w = (w_in[0], w_proj_a[0], w_proj_b[0], w_out[0], w_up[0], w_down[0])
    big_m = (m_w_in[0], m_w_proj_a[0], m_w_proj_b[0], m_w_out[0], m_w_up[0], m_w_down[0])
    big_v = (v_w_in[0], v_w_proj_a[0], v_w_proj_b[0], v_w_out[0], v_w_up[0], v_w_down[0])
    names = ("w_in", "w_proj_a", "w_proj_b", "w_out", "w_up", "w_down")

    g_in = [_gather_on_sequencer([_cast_bf16(big_w[0], window=win)], f"gather_w_in_{k}")[0]
            for k, win in enumerate(((0, W_IN_SPLIT), (W_IN_SPLIT, D_IN // N_DEV - W_IN_SPLIT)))]
    shards = [None] + [_cast_bf16(w) for w in big_w[1:5]]
    g_pa, g_pb, g_out, g_up = _gather_on_sequencer(shards[1:5], "gather_w_mix_up")
    small_w = _pack_small(norm_mix, b_gate, q_norm_a, k_norm_a, q_norm_b, k_norm_b, rpb_b, norm_ffn)
    small_m = _pack_small(m_norm_mix, m_b_gate, m_q_norm_a, m_k_norm_a, m_q_norm_b, m_k_norm_b, m_rpb_b, m_norm_ffn)
    small_v = _pack_small(v_norm_mix, v_b_gate, v_q_norm_a, v_k_norm_a, v_q_norm_b, v_k_norm_b, v_rpb_b, v_norm_ffn)
    g_down = [_gather_on_sequencer([_cast_bf16(big_w[5], part=h, parts=2, after=(small_w, small_m, small_v) * h)],
                                   f"gather_w_down_{h}")[0].reshape(1, D_FF, D // 2) for h in range(2)]

    upd = [None] * 6
    in_flight = {}

    def weight_grads(tag, operands):
        sums = {i: _mm_tn_pair(a, b, name=f"grad_{names[i]}", **GRAD_TILES[i]) for i, (a, b) in operands.items()}
        new = list(sums.values())
        in_flight.update(zip(sums, _chip_exchange_on_sequencer(new, f"chip_exchange_{tag}")))
        return new

    def riders(name):
        i = {"proj_bwd": 5}.get(name)
        if i is None:
            return None
        return (_adamw_rider(in_flight.pop(i), big_w[i], big_m[i], big_v[i]),
                functools.partial(upd.__setitem__, i))

    loss, grad_x, small_g = _local_step(
        x[0], loss_target[0], norm_mix, b_gate, small_w[48:56], small_w[56:120], norm_ffn,
        g_in, g_pa, g_pb, g_out.reshape(D, D), g_up, g_down, weight_grads, riders)

    g_norm_mix, g_b, g_gains, g_rpb, g_norm_ffn = small_g
    small_part = jnp.concatenate([g_norm_mix.reshape(16, HD), g_b.reshape(32, HD),
                                  g_gains, g_rpb, g_norm_ffn.reshape(16, HD),
                                  jnp.pad(loss, ((0, 7), (0, HD - 1)))], axis=0)
    last = grad_x
    for i, r in in_flight.items():
        if i == 0:
            small_sum = _small_exchange(small_part, after=[last])
            small, total = _small_adamw(small_sum, small_w, small_m, small_v)
            last = total
        upd[i] = _adamw(r, big_w[i], big_m[i], big_v[i], name=f"adamw_{names[i]}", after=[last])
        last = upd[i][0]
    s_g, s_d, s_m, s_v = ((*k[:6], _rpb_from_rows(k[6]), k[7]) for k in small)
    b_g, b_d, b_m, b_v = ([u[j][None] for u in upd] for j in range(4))

    def order(small, big):
        nm, bg, qa, ka, qb, kb, rpb, nf = small
        w_in_, pa_, pb_, out_, up_, down_ = big
        return (nm, w_in_, bg, qa, ka, qb, kb, rpb, pa_, pb_, out_, nf, up_, down_)

    return (total[0, 0], grad_x[None], *order(s_g, b_g), *order(s_d, b_d), *order(s_m, b_m), *order(s_v, b_v))
```

```python
import functools
from typing import Callable, NamedTuple

import jax
import jax.numpy as jnp
import numpy as np
from jax import lax
from jax.experimental import pallas as pl
from jax.experimental.pallas import tpu as pltpu
from jax.experimental.pallas import tpu_sc as plsc

F32 = jnp.float32
BF16 = jnp.bfloat16

N_DEV = 8
S = 2048
D = 2048
HD = 128
NH = 16
NH_A = 12
QKV = NH * HD
D_IN = 3 * QKV + 2 * D
D_BR = 512
D_FF = 4 * D
GRID_W = 64
ROWS = S // GRID_W
WIN_R = 8
WIN_C = 16
EPS = 1e-6
NEG = -1e30
SCALE = HD ** -0.5
ROPE_THETA = 10000.0
DILATIONS = (1, 4, 16)
HALF_A = 64
QB = 128
W_IN_SPLIT = 768

LR, B1, B2, AEPS, WD, STEP = 0.001, 0.9, 0.999, 1e-08, 0.01, 10
BC1 = 1.0 - B1 ** STEP
BC2 = 1.0 - B2 ** STEP

VMEM_LIMIT = 56 * 1024 * 1024
MESH = pl.DeviceIdType.MESH

NN = (((1,), (0,)), ((), ()))
NT = (((1,), (1,)), ((), ()))
TN = (((0,), (0,)), ((), ()))


def _params(sem):
    return pltpu.CompilerParams(dimension_semantics=sem, vmem_limit_bytes=VMEM_LIMIT)


def _matmul(a, b, *, product, grid, a_spec, b_spec, epi, out_shape, out_specs, name,
            extra=(), extra_specs=(), after=(), carried=False, rider=None, into=()):
    n_extra = len(extra)
    single = not isinstance(out_shape, (list, tuple))
    out_shape = [out_shape] if single else list(out_shape)
    out_specs = [out_specs] if single else list(out_specs)
    ride = rider(grid[0] * grid[1], lambda j, i: j * grid[1] + i) if rider else None
    r_in = list(ride.inputs) if ride else []
    n_main = len(out_shape)

    def body(a_ref, b_ref, *rest):
        n_in = n_extra + len(after) + len(r_in)
        ins, outs = rest[:n_in], rest[n_in + len(into):]
        epi(product(a_ref, b_ref, ins[:n_extra]), ins[:n_extra], outs[:n_main])
        if ride:
            ride.body(ins[n_extra + len(after):], outs[n_main:])

    res = pl.pallas_call(
        body, name=name, grid=grid,
        in_specs=[a_spec, b_spec, *extra_specs, *[pl.BlockSpec(memory_space=pl.ANY)] * len(after),
                  *(ride.in_specs if ride else []), *[pl.BlockSpec(memory_space=pl.ANY)] * len(into)],
        out_specs=out_specs + (ride.out_specs if ride else []),
        out_shape=out_shape + (ride.out_shape if ride else []),
        input_output_aliases={2 + n_extra + len(after) + len(r_in) + k: k for k in range(len(into))},
        compiler_params=_params(("arbitrary", "arbitrary") if carried else ("parallel", "parallel")),
    )(a, b, *extra, *after, *r_in, *into)
    main = res[0] if single else res[:n_main]
    return (main, res[n_main:]) if ride else main


def _dot(x, y, dims):
    return lax.dot_general(x, y, dims, preferred_element_type=F32)


def _epi_store(acc, ex, outs):
    outs[0][...] = acc.astype(outs[0].dtype)


def _epi_residual(acc, ex, outs):
    outs[0][...] = acc + ex[0][...]


def _mm_nn(a, b3, *, tm, tn, name, out_dtypes=(F32,), epi=_epi_store, extra=(), total=False,
           col0=0, width=None, into=(), stride=None):
    m, kdim = a.shape
    g, _, ng = b3.shape
    n = g * ng
    c0 = col0 // tn
    if tn <= ng:
        npg = ng // tn
        b_spec = pl.BlockSpec((None, kdim, tn), lambda j, i: (j // npg, 0, j % npg))

        def product(a_ref, b_ref, ex):
            return _dot(a_ref[...], b_ref[...], NN)
    else:
        gb = tn // ng
        b_spec = pl.BlockSpec((gb, kdim, ng), lambda j, i: (j, 0, 0))

        def product(a_ref, b_ref, ex):
            return jnp.concatenate([_dot(a_ref[...], b_ref[q], NN) for q in range(gb)], axis=1)

    tile = pl.BlockSpec((tm, tn), lambda j, i: (i, j + c0))
    if stride is not None:
        assert tn == ng and not extra
        tile = pl.BlockSpec((pl.Element(tm), pl.Element(tn)),
                            lambda j, i: (i * tm, pl.multiple_of(j * stride + col0, 128)))
    shapes = [jax.ShapeDtypeStruct((m, width or n), dt) for dt in out_dtypes]
    specs = [tile] * len(shapes)
    if total:
        shapes.append(jax.ShapeDtypeStruct((1, 1), F32))
        specs.append(pl.BlockSpec((1, 1), lambda j, i: (0, 0)))
    single = len(shapes) == 1
    return _matmul(
        a, b3, product=product, grid=(n // tn, m // tm), epi=epi, name=name, carried=total, into=into,
        a_spec=pl.BlockSpec((tm, kdim), lambda j, i: (i, 0)), b_spec=b_spec,
        extra=extra, extra_specs=[tile] * len(extra),
        out_shape=shapes[0] if single else shapes, out_specs=specs[0] if single else specs)


def _mm_nt(a, b3, *, tm, tn, name, out_dtype=F32, epi=_epi_store, extra=(), after=(), rider=None, more_b=(),
           interleaved=False):
    m, kdim = a.shape
    _, n, _ = b3.shape
    n_b = len(more_b)

    def product(a_ref, b_ref, ex):
        refs = (b_ref, *ex[:n_b])
        pieces = ([(ref, q) for q in range(b_ref.shape[0]) for ref in refs] if interleaved
                  else [(ref, q) for ref in refs for q in range(ref.shape[0])])
        acc, k0 = None, 0
        for ref, q in pieces:
            part = _dot(a_ref[:, k0:k0 + ref.shape[2]], ref[q], NT)
            acc = part if acc is None else acc + part
            k0 += ref.shape[2]
        return acc

    def write(acc, ex, outs):
        epi(acc, ex[n_b:], outs)

    def w_spec(w):
        return pl.BlockSpec((w.shape[0], tn, w.shape[2]), lambda j, i: (0, j, 0))

    tile = pl.BlockSpec((tm, tn), lambda j, i: (i, j))
    return _matmul(
        a, b3, product=product, grid=(n // tn, m // tm), epi=write, name=name,
        a_spec=pl.BlockSpec((tm, kdim), lambda j, i: (i, 0)), b_spec=w_spec(b3),
        extra=(*more_b, *extra), extra_specs=[w_spec(w) for w in more_b] + [tile] * len(extra),
        after=after, rider=rider,
        out_shape=jax.ShapeDtypeStruct((m, n), out_dtype), out_specs=tile)


def _mm_tn(a, b, *, tm, tn, name, groups=1, out_dtype=BF16):
    t, m = a.shape
    _, n = b.shape
    ng = n // groups
    if tn <= ng:
        npg = ng // tn
        out_spec = pl.BlockSpec((None, tm, tn), lambda j, i: (j // npg, i, j % npg))
        epi = _epi_store

        def product(a_ref, b_ref, ex):
            return _dot(a_ref[...], b_ref[...], TN)
    else:
        gb = tn // ng
        out_spec = pl.BlockSpec((gb, tm, ng), lambda j, i: (j, i, 0))

        def product(a_ref, b_ref, ex):
            return [_dot(a_ref[...], b_ref[:, q * ng:(q + 1) * ng], TN) for q in range(gb)]

        def epi(parts, ex, outs):
            for q, part in enumerate(parts):
                outs[0][q] = part.astype(out_dtype)

    return _matmul(
        a, b, product=product, grid=(n // tn, m // tm), epi=epi, name=name,
        a_spec=pl.BlockSpec((t, tm), lambda j, i: (0, i)),
        b_spec=pl.BlockSpec((t, tn), lambda j, i: (0, j)),
        out_shape=jax.ShapeDtypeStruct((groups, m, ng), out_dtype), out_specs=out_spec)


def _rms_fwd(x, g, *, name, tr=256):
    def body(x_ref, g_ref, y_ref, r_ref):
        xv = x_ref[...]
        r = lax.rsqrt(jnp.mean(xv * xv, axis=-1, keepdims=True) + EPS)
        y_ref[...] = (xv * r * g_ref[...]).astype(BF16)
        r_ref[...] = r

    row = pl.BlockSpec((tr, D), lambda i: (i, 0))
    return pl.pallas_call(
        body, name=name, grid=(S // tr,),
        in_specs=[row, pl.BlockSpec((1, D), lambda i: (0, 0))],
        out_specs=[row, pl.BlockSpec((tr, 1), lambda i: (i, 0))],
        out_shape=[jax.ShapeDtypeStruct((S, D), BF16), jax.ShapeDtypeStruct((S, 1), F32)],
        compiler_params=_params(("parallel",)),
    )(x, g)


def _rms_bwd(dy, x, rstd, g, resid, *, name, bf16_copy, tr=256):
    def body(dy_ref, x_ref, r_ref, g_ref, res_ref, dx_ref, *rest):
        dg_ref = rest[-1]
        r = r_ref[...]
        xh = x_ref[...] * r
        dyv = dy_ref[...]
        t = dyv * g_ref[...]
        dx = r * (t - xh * jnp.mean(t * xh, axis=-1, keepdims=True)) + res_ref[...]
        dx_ref[...] = dx
        if bf16_copy:
            rest[0][...] = dx.astype(BF16)
        part = jnp.sum(dyv * xh, axis=0, keepdims=True)

        @pl.when(pl.program_id(0) == 0)
        def _():
            dg_ref[...] = part

        @pl.when(pl.program_id(0) > 0)
        def _():
            dg_ref[...] += part

    row = pl.BlockSpec((tr, D), lambda i: (i, 0))
    vec = pl.BlockSpec((1, D), lambda i: (0, 0))
    return pl.pallas_call(
        body, name=name, grid=(S // tr,),
        in_specs=[row, row, pl.BlockSpec((tr, 1), lambda i: (i, 0)), vec, row],
        out_specs=[row] + [row] * bf16_copy + [vec],
        out_shape=[jax.ShapeDtypeStruct((S, D), F32)] + [jax.ShapeDtypeStruct((S, D), BF16)] * bf16_copy
        + [jax.ShapeDtypeStruct((1, D), F32)],
        compiler_params=_params(("arbitrary",)),
    )(dy, x, rstd, g, resid)


def _rope_tables():
    pos = np.arange(S, dtype=np.float32)
    inv = (ROPE_THETA ** (-np.arange(0, HD, 2, dtype=np.float32) / HD)).astype(np.float32)
    ang = pos[:, None] * inv[None, :]
    cos, sin = np.cos(ang), np.sin(ang)
    return (jnp.asarray(np.concatenate([cos, cos], axis=-1), F32),
            jnp.asarray(np.concatenate([-sin, sin], axis=-1), F32))


def _swap_halves(t):
    return pltpu.roll(t, HD // 2, axis=1)


TOK = 256


def _lane_block_spec(d, last=HD):
    return pl.BlockSpec((4, TOK // d, d * last), lambda i: (0, i, 0))


def _to_lane_blocks(dst, head, val, d, scr, dtype):
    w = val.shape[1]
    if d == 1:
        dst[head] = val.astype(dtype)
        return
    scr[...] = val
    for r in range(d):
        dst[head, :, r * w:(r + 1) * w] = scr[pl.ds(r, TOK // d, stride=d), :].astype(dtype)


def _from_lane_blocks(src, head, d, w, scr):
    if d == 1:
        return src[head].astype(F32)
    for r in range(d):
        scr[pl.ds(r, TOK // d, stride=d), :] = src[head, :, r * w:(r + 1) * w].astype(F32)
    return scr[...]


def _qk_prep(proj, gains, cos2, sin2):
    def body(q_ref, k_ref, v_ref, g_ref, c_ref, s_ref, *rest):
        outs, scr = rest[:-1], rest[-1]
        cos, sin = c_ref[...], s_ref[...]
        for which, (src, row_a, row_b) in enumerate(((q_ref, 0, 2), (k_ref, 1, 3), (v_ref, None, None))):
            for h in range(NH):
                y = src[:, h * HD:(h + 1) * HD]
                if row_a is not None:
                    y = y * lax.rsqrt(jnp.mean(y * y, axis=-1, keepdims=True) + EPS)
                    if h < NH_A:
                        y = y * g_ref[row_a:row_a + 1, :]
                        y = y * cos + _swap_halves(y) * sin
                    else:
                        y = y * g_ref[row_b:row_b + 1, :]
                if h < NH_A:
                    gi = h // 4
                    _to_lane_blocks(outs[3 * gi + which], h % 4, y, DILATIONS[gi], scr, BF16)
                else:
                    hb = h - NH_A
                    outs[9 + which][:, hb * HD:(hb + 1) * HD] = y.astype(BF16)

    def blk(c):
        return pl.BlockSpec((TOK, QKV), lambda i: (i, c))
    tab = pl.BlockSpec((TOK, HD), lambda i: (i, 0))
    out_specs, out_shape = [], []
    for d in DILATIONS:
        out_specs += [_lane_block_spec(d)] * 3
        out_shape += [jax.ShapeDtypeStruct((4, S // d, d * HD), BF16)] * 3
    out_specs += [pl.BlockSpec((TOK, D_BR), lambda i: (i, 0))] * 3
    out_shape += [jax.ShapeDtypeStruct((S, D_BR), BF16)] * 3
    outs = pl.pallas_call(
        body, name="qk_prep", grid=(S // TOK,),
        in_specs=[blk(0), blk(1), blk(2), pl.BlockSpec((8, HD), lambda i: (0, 0)), tab, tab],
        out_specs=out_specs, out_shape=out_shape,
        scratch_shapes=[pltpu.VMEM((TOK, HD), F32)],
        compiler_params=_params(("parallel",)),
    )(proj, proj, proj, gains, cos2, sin2)
    return [tuple(outs[3 * gi:3 * gi + 3]) for gi in range(3)], tuple(outs[9:12])


def _qk_prep_bwd(dproj, proj, gains, cos2, sin2, grads_a, grads_b):
    def body(dp_in, q_ref, k_ref, g_ref, c_ref, s_ref, *rest):
        grads, (dp_out, dg_ref, scr) = rest[:12], rest[12:]
        del dp_in
        cos, sin = c_ref[...], s_ref[...]

        def grad_of(which, h):
            if h < NH_A:
                gi = h // 4
                return _from_lane_blocks(grads[3 * gi + which], h % 4, DILATIONS[gi], HD, scr)
            hb = h - NH_A
            return grads[9 + which][:, hb * HD:(hb + 1) * HD].astype(F32)

        dg_rows = []
        for which, (src, base, row_a, row_b) in enumerate(((q_ref, 0, 0, 2), (k_ref, QKV, 1, 3))):
            dg_a = jnp.zeros((1, HD), F32)
            dg_b = jnp.zeros((1, HD), F32)
            for h in range(NH):
                t = src[:, h * HD:(h + 1) * HD]
                dy = grad_of(which, h)
                r = lax.rsqrt(jnp.mean(t * t, axis=-1, keepdims=True) + EPS)
                xh = t * r
                if h < NH_A:
                    dy = dy * cos - _swap_halves(dy) * sin
                    gain = g_ref[row_a:row_a + 1, :]
                    dg_a = dg_a + jnp.sum(dy * xh, axis=0, keepdims=True)
                else:
                    gain = g_ref[row_b:row_b + 1, :]
                    dg_b = dg_b + jnp.sum(dy * xh, axis=0, keepdims=True)
                u = dy * gain
                dx = r * (u - xh * jnp.mean(u * xh, axis=-1, keepdims=True))
                dp_out[:, base + h * HD:base + (h + 1) * HD] = dx.astype(BF16)
            dg_rows += [(row_a, dg_a), (row_b, dg_b)]
        for h in range(NH):
            dp_out[:, 2 * QKV + h * HD:2 * QKV + (h + 1) * HD] = grad_of(2, h).astype(BF16)

        @pl.when(pl.program_id(0) == 0)
        def _():
            dg_ref[...] = jnp.zeros((8, HD), F32)

        for row, val in dg_rows:
            dg_ref[row:row + 1, :] += val

    def blk(c):
        return pl.BlockSpec((TOK, QKV), lambda i: (i, c))
    tab = pl.BlockSpec((TOK, HD), lambda i: (i, 0))
    gain_spec = pl.BlockSpec((8, HD), lambda i: (0, 0))
    grad_specs = [s for d in DILATIONS for s in [_lane_block_spec(d)] * 3]
    grad_specs += [pl.BlockSpec((TOK, D_BR), lambda i: (i, 0))] * 3
    return pl.pallas_call(
        body, name="qk_prep_bwd", grid=(S // TOK,),
        in_specs=[pl.BlockSpec(memory_space=pl.ANY), blk(0), blk(1), gain_spec, tab, tab] + grad_specs,
        out_specs=[pl.BlockSpec((TOK, 3 * QKV), lambda i: (i, 0)), gain_spec],
        out_shape=[jax.ShapeDtypeStruct((S, D_IN), BF16), jax.ShapeDtypeStruct((8, HD), F32)],
        input_output_aliases={0: 0},
        scratch_shapes=[pltpu.VMEM((TOK, HD), F32)],
        compiler_params=_params(("arbitrary",)),
    )(dproj, proj, proj, gains, cos2, sin2, *[g for grp in grads_a for g in grp], *grads_b)


def _mix_fwd(oa, ob, w_pa, w_pb, proj, b_gate, *, tr=256):
    def body(oa_ref, ob_ref, pa_ref, pb_ref, la_ref, lb_ref, ba_ref, bb_ref, mix_ref, ya_ref, yb_ref):
        ya = jnp.concatenate([_dot(oa_ref[...], pa_ref[q], NN) for q in range(N_DEV)], axis=1)
        yb = jnp.concatenate([_dot(ob_ref[...], pb_ref[q], NN) for q in range(N_DEV)], axis=1)
        ga = jax.nn.sigmoid(la_ref[...] + ba_ref[...])
        gb = jax.nn.sigmoid(lb_ref[...] + bb_ref[...])
        mix_ref[...] = (ga * ya + gb * yb).astype(BF16)
        ya_ref[...] = ya.astype(BF16)
        yb_ref[...] = yb.astype(BF16)

    row = pl.BlockSpec((tr, D), lambda i: (i, 0))
    branch = pl.BlockSpec((tr, D_BR), lambda i: (i, 0))
    whole = pl.BlockSpec((N_DEV, D_BR, D // N_DEV), lambda i: (0, 0, 0))
    return pl.pallas_call(
        body, name="mix_fwd", grid=(S // tr,),
        in_specs=[branch, branch, whole, whole,
                  pl.BlockSpec((tr, D), lambda i: (i, 3)), pl.BlockSpec((tr, D), lambda i: (i, 4)),
                  pl.BlockSpec((1, D), lambda i: (0, 0)), pl.BlockSpec((1, D), lambda i: (0, 1))],
        out_specs=[row, row, row], out_shape=[jax.ShapeDtypeStruct((S, D), BF16)] * 3,
        compiler_params=_params(("parallel",)),
    )(oa, ob, w_pa, w_pb, proj, proj, b_gate, b_gate)


def _mix_bwd(dh1b, w_out, proj, b_gate, ya, yb, *, tr=256):
    def body(dh_ref, w_ref, la_ref, lb_ref, b_ref, ya_ref, yb_ref, dya_ref, dyb_ref, dp_ref, db_ref):
        dm = _dot(dh_ref[...], w_ref[...], NT)
        parts = []
        for l_ref, y_ref, dy_ref, lo in ((la_ref, ya_ref, dya_ref, 0), (lb_ref, yb_ref, dyb_ref, D)):
            g = jax.nn.sigmoid(l_ref[...] + b_ref[:, lo:lo + D])
            dy_ref[...] = (dm * g).astype(BF16)
            dl = dm * y_ref[...].astype(F32) * g * (1.0 - g)
            dp_ref[:, lo:lo + D] = dl.astype(BF16)
            parts.append(jnp.sum(dl, axis=0, keepdims=True))
        part = jnp.concatenate(parts, axis=1)

        @pl.when(pl.program_id(0) == 0)
        def _():
            db_ref[...] = part

        @pl.when(pl.program_id(0) > 0)
        def _():
            db_ref[...] += part

    row = pl.BlockSpec((tr, D), lambda i: (i, 0))
    vec = pl.BlockSpec((1, 2 * D), lambda i: (0, 0))
    gate_cols = pl.BlockSpec((pl.Element(tr), pl.Element(2 * D)), lambda i: (i * tr, 3 * QKV))
    return pl.pallas_call(
        body, name="mix_bwd", grid=(S // tr,),
        in_specs=[row, pl.BlockSpec((D, D), lambda i: (0, 0)),
                  pl.BlockSpec((tr, D), lambda i: (i, 3)), pl.BlockSpec((tr, D), lambda i: (i, 4)), vec, row, row],
        out_specs=[row, row, gate_cols, vec],
        out_shape=[jax.ShapeDtypeStruct((S, D), BF16), jax.ShapeDtypeStruct((S, D), BF16),
                   jax.ShapeDtypeStruct((S, D_IN), BF16), jax.ShapeDtypeStruct((1, 2 * D), F32)],
        compiler_params=_params(("arbitrary",)),
    )(dh1b, w_out, proj, proj, b_gate, ya, yb)


def _band_blocks(m_len):
    wk = min(m_len, QB + 2 * QB)
    return [(qb * QB, min(max(qb * QB - QB, 0), m_len - wk), wk) for qb in range(m_len // QB)]


def _band_scores(q, kw, q0, k0, wk):
    s = _dot(q, kw, NT) * SCALE
    qpos = q0 + lax.broadcasted_iota(jnp.int32, (QB, 1), 0)
    kpos = k0 + lax.broadcasted_iota(jnp.int32, (1, wk), 1)
    return jnp.where(jnp.abs(kpos - qpos) <= HALF_A, s, NEG)


def _attn_a_fwd(q, k, v, gi):
    d = DILATIONS[gi]
    m_len = S // d

    def body(q_ref, k_ref, v_ref, o_ref, lse_ref):
        for r in range(d):
            lanes = slice(r * HD, (r + 1) * HD)
            for q0, k0, wk in _band_blocks(m_len):
                s = _band_scores(q_ref[q0:q0 + QB, lanes], k_ref[k0:k0 + wk, lanes], q0, k0, wk)
                m = jnp.max(s, axis=-1, keepdims=True)
                p = jnp.exp(s - m)
                l = jnp.sum(p, axis=-1, keepdims=True)
                o_ref[q0:q0 + QB, lanes] = _dot(p.astype(BF16), v_ref[k0:k0 + wk, lanes], NN) / l
                lse_ref[q0:q0 + QB, r:r + 1] = m + jnp.log(l)

    head = pl.BlockSpec((None, m_len, d * HD), lambda h: (h, 0, 0))
    stat = pl.BlockSpec((None, m_len, d), lambda h: (h, 0, 0))
    return pl.pallas_call(
        body, name=f"attn_a_fwd_{gi}", grid=(4,),
        in_specs=[head, head, head], out_specs=[head, stat],
        out_shape=[jax.ShapeDtypeStruct((4, m_len, d * HD), F32), jax.ShapeDtypeStruct((4, m_len, d), F32)],
        compiler_params=_params(("parallel",)),
    )(q, k, v)


def _combine_a(os, lses):
    def body(o0, o1, o2, l0, l1, l2, oa_ref, lse_ref, scr, scr1):
        for h in range(4):
            o = [_from_lane_blocks(ref, h, d, HD, scr) for ref, d in zip((o0, o1, o2), DILATIONS)]
            a, b, c = (_from_lane_blocks(ref, h, d, 1, scr1) for ref, d in zip((l0, l1, l2), DILATIONS))
            m = jnp.maximum(jnp.maximum(a, b), c)
            wa, wb, wc = jnp.exp(a - m), jnp.exp(b - m), jnp.exp(c - m)
            tot = wa + wb + wc
            oa_ref[:, h * HD:(h + 1) * HD] = ((wa * o[0] + wb * o[1] + wc * o[2]) / tot).astype(BF16)
            lse_ref[h] = m + jnp.log(tot)

    return pl.pallas_call(
        body, name="combine_a", grid=(S // TOK,),
        in_specs=[_lane_block_spec(d) for d in DILATIONS] + [_lane_block_spec(d, 1) for d in DILATIONS],
        out_specs=[pl.BlockSpec((TOK, D_BR), lambda i: (i, 0)), pl.BlockSpec((4, TOK, 1), lambda i: (0, i, 0))],
        out_shape=[jax.ShapeDtypeStruct((S, D_BR), BF16), jax.ShapeDtypeStruct((4, S, 1), F32)],
        scratch_shapes=[pltpu.VMEM((TOK, HD), F32), pltpu.VMEM((TOK, 1), F32)],
        compiler_params=_params(("parallel",)),
    )(*os, *lses)


def _proj_a_bwd(dya, w_pa, oa, lse):
    kg = D // N_DEV

    def body(dy_ref, w_ref, o_ref, l_ref, *rest):
        outs, (scr, scr1) = rest[:9], rest[9:]
        doa = _dot(dy_ref[:, 0:kg], w_ref[0], NT)
        for q in range(1, N_DEV):
            doa = doa + _dot(dy_ref[:, q * kg:(q + 1) * kg], w_ref[q], NT)
        for h in range(4):
            do = doa[:, h * HD:(h + 1) * HD]
            dsum = jnp.sum(do * o_ref[:, h * HD:(h + 1) * HD].astype(F32), axis=-1, keepdims=True)
            for gi, d in enumerate(DILATIONS):
                _to_lane_blocks(outs[3 * gi], h, do, d, scr, BF16)
                _to_lane_blocks(outs[3 * gi + 1], h, l_ref[h], d, scr1, F32)
                _to_lane_blocks(outs[3 * gi + 2], h, dsum, d, scr1, F32)

    row = pl.BlockSpec((TOK, D_BR), lambda i: (i, 0))
    out_specs, out_shape = [], []
    for d in DILATIONS:
        out_specs += [_lane_block_spec(d), _lane_block_spec(d, 1), _lane_block_spec(d, 1)]
        out_shape += [jax.ShapeDtypeStruct((4, S // d, d * HD), BF16)] + [jax.ShapeDtypeStruct((4, S // d, d), F32)] * 2
    outs = pl.pallas_call(
        body, name="proj_a_bwd", grid=(S // TOK,),
        in_specs=[pl.BlockSpec((TOK, D), lambda i: (i, 0)),
                  pl.BlockSpec((N_DEV, D_BR, kg), lambda i: (0, 0, 0)),
                  row, pl.BlockSpec((4, TOK, 1), lambda i: (0, i, 0))],
        out_specs=out_specs, out_shape=out_shape,
        scratch_shapes=[pltpu.VMEM((TOK, HD), F32), pltpu.VMEM((TOK, 1), F32)],
        compiler_params=_params(("parallel",)),
    )(dya, w_pa, oa, lse)
    return [tuple(outs[3 * gi:3 * gi + 3]) for gi in range(3)]


def _attn_a_bwd(q, k, v, do, lse, dsum, gi):
    d = DILATIONS[gi]
    m_len = S // d

    def body(q_ref, k_ref, v_ref, do_ref, lse_ref, dsum_ref, dq_ref, dk_out, dv_out, dk_ref, dv_ref):
        dk_ref[...] = jnp.zeros((m_len, d * HD), F32)
        dv_ref[...] = jnp.zeros((m_len, d * HD), F32)
        for r in range(d):
            lanes = slice(r * HD, (r + 1) * HD)
            for q0, k0, wk in _band_blocks(m_len):
                rows, keys = slice(q0, q0 + QB), slice(k0, k0 + wk)
                qv, kw, vw, dov = q_ref[rows, lanes], k_ref[keys, lanes], v_ref[keys, lanes], do_ref[rows, lanes]
                p = jnp.exp(_band_scores(qv, kw, q0, k0, wk) - lse_ref[rows, r:r + 1])
                ds = (p * (_dot(dov, vw, NT) - dsum_ref[rows, r:r + 1]) * SCALE).astype(BF16)
                dq_ref[rows, lanes] = _dot(ds, kw, NN).astype(BF16)
                dk_ref[keys, lanes] += _dot(ds, qv, TN)
                dv_ref[keys, lanes] += _dot(p.astype(BF16), dov, TN)
        dk_out[...] = dk_ref[...].astype(BF16)
        dv_out[...] = dv_ref[...].astype(BF16)

    head = pl.BlockSpec((None, m_len, d * HD), lambda h: (h, 0, 0))
    stat = pl.BlockSpec((None, m_len, d), lambda h: (h, 0, 0))
    shape = jax.ShapeDtypeStruct((4, m_len, d * HD), BF16)
    return pl.pallas_call(
        body, name=f"attn_a_bwd_{gi}", grid=(4,),
        in_specs=[head, head, head, head, stat, stat], out_specs=[head, head, head],
        out_shape=[shape, shape, shape],
        scratch_shapes=[pltpu.VMEM((m_len, d * HD), F32)] * 2,
        compiler_params=_params(("arbitrary",)),
    )(q, k, v, do, lse, dsum)


KEYS_B = WIN_R * GRID_W
N_OFF = WIN_R


def _bias_constants():
    q = np.arange(GRID_W)[:, None]
    kc = np.arange(GRID_W)[None, :]
    dc = np.clip(kc - q, -(WIN_C - 1), WIN_C - 1) + (WIN_C - 1)
    expand = np.zeros((HD, GRID_W * GRID_W), np.float32)
    expand[dc.reshape(-1), np.arange(GRID_W * GRID_W)] = 1.0
    cs = np.clip(q - WIN_C // 2, 0, GRID_W - WIN_C)
    keep = ((kc >= cs) & (kc < cs + WIN_C)).reshape(1, -1).astype(np.float32)
    sel = np.zeros((64, 4 * N_OFF * WIN_R), np.float32)
    for h in range(4):
        for off in range(N_OFF):
            for j in range(WIN_R):
                sel[h * (2 * WIN_R - 1) + off + j, (h * N_OFF + off) * WIN_R + j] = 1.0
    return jnp.asarray(expand), jnp.asarray(keep), jnp.asarray(sel)


def _bias_expand(rpb_pad, expand, keep, sel):
    def body(r_ref, e_ref, k_ref, s_ref, o_ref):
        t = lax.dot_general(r_ref[...], e_ref[...], NN, precision=lax.Precision.HIGHEST,
                            preferred_element_type=F32)
        rows = lax.dot_general(s_ref[...], t, TN, precision=lax.Precision.HIGHEST,
                               preferred_element_type=F32)
        o_ref[...] = jnp.where(k_ref[...] > 0.5, rows, NEG)

    return pl.pallas_call(
        body, name="bias_expand",
        out_shape=jax.ShapeDtypeStruct((4 * N_OFF * WIN_R, GRID_W * GRID_W), F32),
        compiler_params=pltpu.CompilerParams(vmem_limit_bytes=VMEM_LIMIT),
    )(rpb_pad, expand, keep, sel)


def _bias_reduce(dbias_tab):
    lane0 = GRID_W - WIN_C
    flip = np.zeros((GRID_W, GRID_W), np.float32)
    flip[np.arange(GRID_W), GRID_W - 1 - np.arange(GRID_W)] = 1.0
    place = np.zeros((WIN_R, 64, 4 * N_OFF), np.float32)
    for j in range(WIN_R):
        for h in range(4):
            for off in range(N_OFF):
                place[j, h * (2 * WIN_R - 1) + off + j, h * N_OFF + off] = 1.0

    def exact(x, y):
        return lax.dot_general(x, y, NN, precision=lax.Precision.HIGHEST, preferred_element_type=F32)

    def body(x_ref, flip_ref, place_ref, o_ref, z_ref):
        for h in range(4):
            for off in range(N_OFF):
                lined_up = pltpu.roll(exact(flip_ref[...], x_ref[h, off]), 0, axis=1, stride=1, stride_axis=0)
                z_ref[h * N_OFF + off:h * N_OFF + off + 1, :] = jnp.sum(lined_up, axis=0, keepdims=True)
        acc = jnp.zeros((64, HD), F32)
        for j in range(WIN_R):
            at_zero = pltpu.roll(z_ref[...], (KEYS_B - (j * GRID_W + lane0)) % KEYS_B, axis=1)[:, :HD]
            acc = acc + exact(place_ref[j], at_zero)
        lane = lax.broadcasted_iota(jnp.int32, (64, HD), 1)
        o_ref[...] = jnp.where(lane < 2 * WIN_C - 1, acc, 0.0)

    return pl.pallas_call(
        body, name="bias_reduce", out_shape=jax.ShapeDtypeStruct((64, HD), F32),
        scratch_shapes=[pltpu.VMEM((4 * N_OFF, KEYS_B), F32)],
        compiler_params=pltpu.CompilerParams(vmem_limit_bytes=VMEM_LIMIT),
    )(dbias_tab, jnp.asarray(flip), jnp.asarray(place))


def _rows_to_tab(rows):
    t = rows.reshape(4, N_OFF, WIN_R, GRID_W, GRID_W)
    return t.transpose(0, 1, 3, 2, 4).reshape(4, N_OFF, GRID_W, KEYS_B)


def _row_window(r):
    r0 = jnp.clip(r - WIN_R // 2, 0, ROWS - WIN_R)
    off = r0 + (WIN_R - 1) - r
    return pl.multiple_of(r * GRID_W, GRID_W), pl.multiple_of(r0 * GRID_W, GRID_W), off


def _attn_b_fwd(qn, kn, vb, bias_tab):
    def body(q_ref, k_ref, v_ref, b_ref, o_ref, lse_ref):
        def row(r, carry):
            qs, ks, off = _row_window(r)
            q = q_ref[pl.ds(qs, GRID_W), :]
            s = lax.dot_general(q, k_ref[pl.ds(ks, KEYS_B), :], NT, preferred_element_type=F32) * SCALE
            s = s + b_ref[off]
            m = jnp.max(s, axis=-1, keepdims=True)
            p = jnp.exp(s - m)
            l = jnp.sum(p, axis=-1, keepdims=True)
            o = lax.dot_general(p.astype(BF16), v_ref[pl.ds(ks, KEYS_B), :], NN, preferred_element_type=F32)
            o_ref[pl.ds(qs, GRID_W), :] = (o / l).astype(BF16)
            lse_ref[pl.ds(qs, GRID_W), :] = m + jnp.log(l)
            return carry

        lax.fori_loop(0, ROWS, row, 0, unroll=8)

    full = pl.BlockSpec((S, HD), lambda h: (0, h))
    return pl.pallas_call(
        body, name="attn_b_fwd", grid=(4,),
        in_specs=[full, full, full, pl.BlockSpec((None, N_OFF, GRID_W, KEYS_B), lambda h: (h, 0, 0, 0))],
        out_specs=[pl.BlockSpec((S, HD), lambda h: (0, h)), pl.BlockSpec((None, S, 1), lambda h: (h, 0, 0))],
        out_shape=[jax.ShapeDtypeStruct((S, D_BR), BF16), jax.ShapeDtypeStruct((4, S, 1), F32)],
        compiler_params=_params(("parallel",)),
    )(qn, kn, vb, bias_tab)


def _attn_b_bwd(qn, kn, vb, bias_tab, ob, dob, lse):
    def body(q_ref, k_ref, v_ref, b_ref, o_ref, do_ref, lse_ref, dq_ref, dk_out, dv_out, db_ref, dk_ref, dv_ref):
        dk_ref[...] = jnp.zeros((S, HD), F32)
        dv_ref[...] = jnp.zeros((S, HD), F32)
        db_ref[...] = jnp.zeros((N_OFF, GRID_W, KEYS_B), F32)

        def row(r, carry):
            qs, ks, off = _row_window(r)
            rows = pl.ds(qs, GRID_W)
            keys = pl.ds(ks, KEYS_B)
            q = q_ref[rows, :]
            kw = k_ref[keys, :]
            s = lax.dot_general(q, kw, NT, preferred_element_type=F32) * SCALE + b_ref[off]
            p = jnp.exp(s - lse_ref[rows, :])
            do = do_ref[rows, :]
            dobf = do.astype(BF16)
            dsum = jnp.sum(do * o_ref[rows, :].astype(F32), axis=-1, keepdims=True)
            dp = lax.dot_general(dobf, v_ref[keys, :], NT, preferred_element_type=F32)
            ds = p * (dp - dsum)
            db_ref[off] += ds
            dsb = (ds * SCALE).astype(BF16)
            dq_ref[rows, :] = lax.dot_general(dsb, kw, NN, preferred_element_type=F32).astype(BF16)
            dk_ref[keys, :] += lax.dot_general(dsb, q, TN, preferred_element_type=F32)
            dv_ref[keys, :] += lax.dot_general(p.astype(BF16), dobf, TN, preferred_element_type=F32)
            return carry

        lax.fori_loop(0, ROWS, row, 0, unroll=8)
        dk_out[...] = dk_ref[...].astype(BF16)
        dv_out[...] = dv_ref[...].astype(BF16)

    full = pl.BlockSpec((S, HD), lambda h: (0, h))
    slot = pl.BlockSpec((S, HD), lambda h: (0, h))
    tab = pl.BlockSpec((None, N_OFF, GRID_W, KEYS_B), lambda h: (h, 0, 0, 0))
    shape = jax.ShapeDtypeStruct((S, D_BR), BF16)
    return pl.pallas_call(
        body, name="attn_b_bwd", grid=(4,),
        in_specs=[full, full, full, tab, slot, slot, pl.BlockSpec((None, S, 1), lambda h: (h, 0, 0))],
        out_specs=[slot, slot, slot, tab],
        out_shape=[shape, shape, shape, jax.ShapeDtypeStruct((4, N_OFF, GRID_W, KEYS_B), F32)],
        scratch_shapes=[pltpu.VMEM((S, HD), F32)] * 2,
        compiler_params=_params(("arbitrary",)),
    )(qn, kn, vb, bias_tab, ob, dob, lse)


def _epi_relu_sq(acc, ex, outs):
    u = jnp.maximum(acc, 0.0)
    outs[0][...] = u.astype(BF16)
    outs[1][...] = (u * u).astype(BF16)


def _epi_relu_sq_bwd(acc, ex, outs):
    outs[0][...] = (acc * (2.0 * ex[0][...].astype(F32))).astype(BF16)


def _epi_loss_head(acc, ex, outs):
    e = acc + ex[0][...] - ex[1][...]
    dy = e * (1.0 / D)
    outs[0][...] = dy
    outs[1][...] = dy.astype(BF16)
    part = (0.5 / D) * jnp.sum(jnp.sum(e * e, axis=-1, keepdims=True), axis=0, keepdims=True)
    first = (pl.program_id(0) == 0) & (pl.program_id(1) == 0)

    @pl.when(first)
    def _():
        outs[2][...] = part

    @pl.when(jnp.logical_not(first))
    def _():
        outs[2][...] += part


def _local_step(x, target, norm_mix, b_gate, gains, rpb_pad, norm_ffn,
                w_in, w_pa, w_pb, w_out, w_up, w_down, weight_grads, riders=lambda name: None):
    def ridden(name, *args, **kwargs):
        ride = riders(name)
        if ride is None:
            return _mm_nt(*args, name=name, **kwargs)
        out, rode = _mm_nt(*args, name=name, rider=ride[0], **kwargs)
        ride[1](rode)
        return out

    cos2, sin2 = _rope_tables()
    expand, keep, sel = _bias_constants()
    w_out3 = w_out[None]

    xn, rstd1 = _rms_fwd(x, norm_mix, name="rms_mix")
    per_dev = D_IN // N_DEV
    proj = _mm_nn(xn, w_in[0], tm=S, tn=W_IN_SPLIT, name="proj_0", stride=per_dev, width=D_IN)
    proj = _mm_nn(xn, w_in[1], tm=S, tn=per_dev - W_IN_SPLIT, name="proj_1", stride=per_dev, width=D_IN,
                  col0=W_IN_SPLIT, into=(proj,))
    qkv_a, qkv_b = _qk_prep(proj, gains, cos2, sin2)
    fwd_a = [_attn_a_fwd(*qkv_a[gi], gi) for gi in range(3)]
    oa, lse_a = _combine_a([o for o, _ in fwd_a], [l for _, l in fwd_a])
    bias_tab = _rows_to_tab(_bias_expand(rpb_pad, expand, keep, sel))
    ob, lse_b = _attn_b_fwd(*qkv_b, bias_tab)
    mixed, ya, yb = _mix_fwd(oa, ob, w_pa, w_pb, proj, b_gate)
    h1 = _mm_nn(mixed, w_out3, tm=1024, tn=1024, name="out_proj", epi=_epi_residual, extra=(x,))
    hn, rstd2 = _rms_fwd(h1, norm_ffn, name="rms_ffn")
    u, usq = _mm_nn(hn, w_up, tm=S, tn=512, name="ffn_up", epi=_epi_relu_sq,
                    out_dtypes=(BF16, BF16))
    dy, dyb, loss = _mm_nn(usq, w_down, tm=512, tn=512, name="ffn_down", epi=_epi_loss_head,
                           extra=(h1, target), out_dtypes=(F32, BF16), total=True)

    sent = weight_grads("w_down", {5: (usq, dyb)})
    du = _mm_nt(dyb, w_down, tm=1024, tn=1024, name="ffn_down_bwd", out_dtype=BF16,
                epi=_epi_relu_sq_bwd, extra=(u,), after=sent)
    sent = weight_grads("w_up", {4: (hn, du)})
    dhn = ridden("ffn_up_bwd", du, w_up, tm=512, tn=512, after=sent)
    dh1, dh1b, g_norm_ffn = _rms_bwd(dhn, h1, rstd2, norm_ffn, dy, name="rms_ffn_bwd", bf16_copy=True)

    dya, dyb2, dproj, g_b = _mix_bwd(dh1b, w_out, proj, b_gate, ya, yb)
    sent = weight_grads("w_mix", {3: (mixed, dh1b), 1: (oa, dya), 2: (ob, dyb2)})
    dob = _mm_nt(dyb2, w_pb, tm=1024, tn=D_BR, name="proj_b_bwd", after=sent)
    prep = _proj_a_bwd(dya, w_pa, oa, lse_a)
    grads_a = [_attn_a_bwd(*qkv_a[gi], *prep[gi], gi) for gi in range(3)]
    dqb, dkb, dvb, dbias = _attn_b_bwd(*qkv_b, bias_tab, ob, dob, lse_b)
    g_rpb = _bias_reduce(dbias)
    dproj, g_gains = _qk_prep_bwd(dproj, proj, gains, cos2, sin2, grads_a, (dqb, dkb, dvb))
    sent = weight_grads("w_in", {0: (xn, dproj)})
    dxn = ridden("proj_bwd", dproj, w_in[0], more_b=(w_in[1],), interleaved=True, tm=256, tn=512, after=sent)
    grad_x, g_norm_mix = _rms_bwd(dxn, x, rstd1, norm_mix, dh1, name="rms_mix_bwd", bf16_copy=False)

    small = (g_norm_mix, g_b, g_gains, g_rpb, g_norm_ffn)
    return loss, grad_x, small


def _cast_bf16(w, *, part=0, parts=1, window=None, after=(), tr=256):
    rows, cols = w.shape[0], w.shape[1] // parts
    tr = min(tr, rows)
    src = pl.BlockSpec((tr, cols), lambda i: (i, part))
    if window is not None:
        part, cols = window
        src = pl.BlockSpec((pl.Element(tr), pl.Element(cols)), lambda i: (i * tr, part))

    def body(w_ref, *rest):
        rest[-1][...] = w_ref[...].astype(BF16)

    return pl.pallas_call(
        body, name=f"cast_{rows}x{cols}_{part}", grid=(rows // tr,),
        in_specs=[src] + [pl.BlockSpec(memory_space=pl.ANY)] * len(after),
        out_specs=pl.BlockSpec((tr, cols), lambda i: (i, 0)),
        out_shape=jax.ShapeDtypeStruct((rows, cols), BF16), compiler_params=_params(("parallel",)),
    )(w, *after)


def _me_and_peers():
    x, y, c = lax.axis_index("x"), lax.axis_index("y"), lax.axis_index("c")
    me = 4 * x + 2 * y + c
    peers = []
    for k in range(1, N_DEV):
        px = 1 - x if k & 4 else x
        py = 1 - y if k & 2 else y
        pc = 1 - c if k & 1 else c
        peers.append(((px, py, pc), 4 * px + 2 * py + pc))
    return me, peers


def _gather_on_sequencer(shards, name):
    n = len(shards)
    hbm = pltpu.MemorySpace.HBM
    ins = [jax.new_ref(s, memory_space=hbm) for s in shards]
    outs = [jax.empty_ref(jax.ShapeDtypeStruct((N_DEV,) + s.shape, s.dtype), memory_space=hbm) for s in shards]
    n_sem = 8

    @_sequencer(name, ((n, n_sem), (n, n_sem), (n,)), 0)
    def launch(send, recv, lsem):
        x, y, c = lax.axis_index("x"), lax.axis_index("y"), lax.axis_index("c")
        me, sibling = (x, y, c), (x, y, 1 - c)
        x_chip, y_chip, diagonal = (1 - x, y, c), (x, 1 - y, c), (1 - x, 1 - y, c)
        _handshake([sibling, x_chip, y_chip])

        def copy(w, k, block, to, src=None, half=None):
            px, py, pc = block
            dst = outs[w].at[4 * px + 2 * py + pc]
            if half is not None:
                rows = shards[w].shape[0] // 2
                dst = dst.at[pl.ds(half * rows, rows)]
            return pltpu.make_async_remote_copy(dst if src is None else src, dst, send.at[w, k], recv.at[w, k],
                                                device_id=to, device_id_type=MESH)

        local = [pltpu.make_async_copy(ins[w], outs[w].at[4 * x + 2 * y + c], lsem.at[w]) for w in range(n)]
        for cp in local:
            cp.start()
        sent = []
        for w in range(n):
            sent += [copy(w, 1, me, x_chip, src=ins[w]), copy(w, 2, me, y_chip, src=ins[w]),
                     copy(w, 0, me, sibling, src=ins[w])]
        for cp in sent:
            cp.start()
        for w in range(n):
            copy(w, 1, x_chip, me).wait_recv()
            sent += [copy(w, 3, x_chip, y_chip, half=0), copy(w, 5, x_chip, sibling)]
            sent[-2].start()
            sent[-1].start()
            copy(w, 2, y_chip, me).wait_recv()
            sent += [copy(w, 4, y_chip, x_chip, half=1), copy(w, 6, y_chip, sibling)]
            sent[-2].start()
            sent[-1].start()
        for w in range(n):
            copy(w, 3, diagonal, me, half=0).wait_recv()
            copy(w, 4, diagonal, me, half=1).wait_recv()
            sent.append(copy(w, 7, diagonal, sibling))
            sent[-1].start()
        for w in range(n):
            copy(w, 0, sibling, me).wait_recv()
            for k, chip in ((5, x_chip), (6, y_chip), (7, diagonal)):
                px, py, _ = chip
                copy(w, k, (px, py, 1 - c), me).wait_recv()
        for cp in sent:
            cp.wait_send()
        for cp in local:
            cp.wait()

    launch()
    return [o[...] for o in outs]


N_CHIP = 4


def _sequencer(name, n_sems, collective_id):
    return functools.partial(
        pl.kernel, mesh=plsc.ScalarSubcoreMesh(axis_name="seq", num_cores=1), name=name,
        scratch_types=tuple(pltpu.SemaphoreType.DMA(s) for s in n_sems),
        compiler_params=pltpu.CompilerParams(collective_id=collective_id))


def _handshake(peers):
    barrier = pltpu.get_barrier_semaphore()
    for peer in peers:
        pl.semaphore_signal(barrier, inc=1, device_id=peer, device_id_type=MESH)
    pl.semaphore_wait(barrier, len(peers))


def _chip_exchange_on_sequencer(parts, name):
    n = len(parts)
    hbm = pltpu.MemorySpace.HBM
    ins = [jax.new_ref(p, memory_space=hbm) for p in parts]
    outs = [jax.empty_ref(jax.ShapeDtypeStruct(p.shape, p.dtype), memory_space=hbm) for p in parts]

    @_sequencer(name, ((n, 3), (n, 3), (n,)), 2)
    def launch(send, recv, lsem):
        x, y, c = lax.axis_index("x"), lax.axis_index("y"), lax.axis_index("c")
        mine = 2 * x + y
        chips = [(1 - x, y), (x, 1 - y), (1 - x, 1 - y)]
        _handshake([(*chip, c) for chip in chips])
        local = [pltpu.make_async_copy(ins[w].at[mine], outs[w].at[mine], lsem.at[w]) for w in range(n)]
        for cp in local:
            cp.start()
        sends = []
        for w in range(n):
            for j, (px, py) in enumerate(chips):
                cp = pltpu.make_async_remote_copy(ins[w].at[2 * px + py], outs[w].at[mine],
                                                  send.at[w, j], recv.at[w, j],
                                                  device_id=(px, py, c), device_id_type=MESH)
                cp.start()
                sends.append(cp)
        for w in range(n):
            for j, (px, py) in enumerate(chips):
                pltpu.make_async_remote_copy(ins[w].at[mine], outs[w].at[2 * px + py],
                                             send.at[w, j], recv.at[w, j],
                                             device_id=(px, py, c), device_id_type=MESH).wait_recv()
        for cp in sends:
            cp.wait_send()
        for cp in local:
            cp.wait()

    launch()
    return [o[...] for o in outs]


GRAD_TILES = (dict(blocks_on="cols", tm=512, tn=1280), dict(blocks_on="cols", tm=512, tn=256),
              dict(blocks_on="cols", tm=512, tn=256), dict(blocks_on="rows", tm=256, tn=2048),
              dict(blocks_on="cols", tm=1024, tn=1024), dict(blocks_on="rows", tm=1024, tn=1024))


def _mm_tn_pair(a, b, *, blocks_on, tm, tn, name):
    t_len, m = a.shape
    n = b.shape[1]
    if blocks_on == "rows":
        rows, cols, inner = m // N_DEV, n, n // tn
        assert tm == rows
        a_spec = pl.BlockSpec((t_len, tm), lambda p, t, blk: (0, blk[p]))
        b_spec = pl.BlockSpec((t_len, tn), lambda p, t, blk: (0, t))
        out_spec = pl.BlockSpec((None, tm, tn), lambda p, t, blk: (
            jnp.maximum(p - N_CHIP, 0), 0, jnp.where(p < N_CHIP, 0, t)))
    else:
        rows, cols, inner = m, n // N_DEV, m // tm
        assert tn == cols
        a_spec = pl.BlockSpec((t_len, tm), lambda p, t, blk: (0, t))
        b_spec = pl.BlockSpec((t_len, tn), lambda p, t, blk: (0, blk[p]))
        out_spec = pl.BlockSpec((None, tm, tn), lambda p, t, blk: (
            jnp.maximum(p - N_CHIP, 0), jnp.where(p < N_CHIP, 0, t), 0))

    def body(blk_ref, a_ref, b_ref, o_ref, land, stage, send_sem, recv_sem):
        del blk_ref
        p, t = pl.program_id(0), pl.program_id(1)
        step = p * inner + t
        x, y, c = lax.axis_index("x"), lax.axis_index("y"), lax.axis_index("c")
        tile = _dot(a_ref[...], b_ref[...], TN)

        def to_sibling(slot, chip, piece):
            return pltpu.make_async_remote_copy(stage.at[slot], land.at[chip, piece], send_sem.at[slot],
                                                recv_sem.at[chip, piece],
                                                device_id=(x, y, 1 - c), device_id_type=MESH)

        @pl.when(p < N_CHIP)
        def _():
            slot = step % 2

            @pl.when(step >= 2)
            def _():
                to_sibling(slot, 0, 0).wait_send()

            stage[slot] = tile.astype(BF16)
            to_sibling(slot, p, t).start()

        @pl.when(step == N_CHIP * inner)
        def _():
            for slot in range(min(2, N_CHIP * inner)):
                to_sibling(slot, 0, 0).wait_send()

        @pl.when(p >= N_CHIP)
        def _():
            chip = p - N_CHIP
            to_sibling(0, chip, t).wait_recv()
            o_ref[...] = (tile + land[chip, t].astype(F32)).astype(BF16)

    c = lax.axis_index("c")
    order = jnp.stack([2 * ch + 1 - c for ch in range(N_CHIP)] + [2 * ch + c for ch in range(N_CHIP)])
    return pl.pallas_call(
        body, name=name,
        grid_spec=pltpu.PrefetchScalarGridSpec(
            num_scalar_prefetch=1, grid=(N_DEV, inner), in_specs=[a_spec, b_spec], out_specs=out_spec,
            scratch_shapes=[pltpu.VMEM((N_CHIP, inner, tm, tn), BF16), pltpu.VMEM((2, tm, tn), BF16),
                            pltpu.SemaphoreType.DMA((2,)), pltpu.SemaphoreType.DMA((N_CHIP, inner))]),
        out_shape=jax.ShapeDtypeStruct((N_CHIP, rows, cols), BF16),
        compiler_params=_params(("arbitrary", "arbitrary")),
    )(order.astype(jnp.int32), a, b)


def _adamw_math(g, w, m, v):
    m2 = B1 * m + (1.0 - B1) * g
    v2 = B2 * v + (1.0 - B2) * (g * g)
    delta = -LR * ((m2 / BC1) / (jnp.sqrt(v2 / BC2) + AEPS) + WD * w)
    return delta, m2, v2


def _adamw_block(ins, outs):
    p_ref, w_ref, m_ref, v_ref = ins
    g = p_ref[0].astype(F32)
    for b in range(1, N_CHIP):
        g = g + p_ref[b].astype(F32)
    delta, m2, v2 = _adamw_math(g, w_ref[...], m_ref[...], v_ref[...])
    for ref, val in zip(outs, (g, delta, m2, v2)):
        ref[...] = val


class _Rider(NamedTuple):
    inputs: tuple
    in_specs: list
    out_shape: list
    out_specs: list
    body: Callable


def _adamw_rider(parts, w, m, v):
    rows, cols = w.shape

    def rider(steps, step_of):
        rr = rows // steps
        blk = pl.BlockSpec((rr, cols), lambda *ids: (step_of(*ids[:2]), 0))
        chips = pl.BlockSpec((N_CHIP, rr, cols), lambda *ids: (0, step_of(*ids[:2]), 0))
        shape = jax.ShapeDtypeStruct((rows, cols), F32)
        return _Rider((parts, w, m, v), [chips, blk, blk, blk], [shape] * 4, [blk] * 4, _adamw_block)

    return rider


def _adamw(parts, w, m, v, *, name, after=(), tr=256):
    rows, cols = w.shape

    def body(*refs):
        _adamw_block(refs[:4], refs[4 + len(after):])

    spec = pl.BlockSpec((tr, cols), lambda i: (i, 0))
    shape = jax.ShapeDtypeStruct((rows, cols), F32)
    return pl.pallas_call(
        body, name=name, grid=(rows // tr,),
        in_specs=[pl.BlockSpec((N_CHIP, tr, cols), lambda i: (0, i, 0)), spec, spec, spec]
        + [pl.BlockSpec(memory_space=pl.ANY)] * len(after),
        out_specs=[spec] * 4, out_shape=[shape] * 4,
        compiler_params=_params(("parallel",)),
    )(parts, w, m, v, *after)


def _small_exchange(part, after=()):
    rows = part.shape[0]

    def body(p_ref, *rest):
        g_ref, buf, send, recv = rest[len(after):]
        me, peers = _me_and_peers()
        buf[me] = p_ref[...]
        sends = []
        for k, (dev, _) in enumerate(peers):
            cp = pltpu.make_async_remote_copy(p_ref, buf.at[me], send.at[k], recv.at[k],
                                              device_id=dev, device_id_type=MESH)
            cp.start()
            sends.append(cp)
        for k, (dev, idx) in enumerate(peers):
            pltpu.make_async_remote_copy(p_ref, buf.at[idx], send.at[k], recv.at[k],
                                         device_id=dev, device_id_type=MESH).wait_recv()
        for cp in sends:
            cp.wait_send()
        g = buf[0]
        for b in range(1, N_DEV):
            g = g + buf[b]
        g_ref[...] = g

    vm = pl.BlockSpec(memory_space=pltpu.VMEM)
    return pl.pallas_call(
        body, name="small_params_exchange",
        in_specs=[vm] + [pl.BlockSpec(memory_space=pl.ANY)] * len(after),
        out_specs=vm, out_shape=jax.ShapeDtypeStruct((rows, HD), F32),
        scratch_shapes=[pltpu.VMEM((N_DEV, rows, HD), F32),
                        pltpu.SemaphoreType.DMA((N_DEV - 1,)), pltpu.SemaphoreType.DMA((N_DEV - 1,))],
    )(part, *after)


def _small_adamw(g, w, m, v):
    def body(g_ref, w_ref, m_ref, v_ref, *outs):
        g = g_ref[...]
        delta, m2, v2 = _adamw_math(g, w_ref[...], m_ref[...], v_ref[...])
        for k, val in enumerate((g, delta, m2, v2)):
            norm_mix, b_gate, qa, ka, qb, kb, rpb, norm_ffn = outs[8 * k:8 * k + 8]
            for dst, row0, n_rows in ((norm_mix, 0, 16), (b_gate, 16, 32), (norm_ffn, 120, 16)):
                for r in range(n_rows):
                    dst[:, r * HD:(r + 1) * HD] = val[row0 + r:row0 + r + 1, :]
            for i, dst in enumerate((qa, ka, qb, kb)):
                dst[...] = val[48 + i:49 + i, :]
            rpb[...] = val[56:120, :]
        outs[32][...] = g[LOSS_ROW:LOSS_ROW + 1, 0:1]

    vm = pl.BlockSpec(memory_space=pltpu.VMEM)
    kinds = [jax.ShapeDtypeStruct(sh, F32) for sh in
             ((1, D), (1, 2 * D), (1, HD), (1, HD), (1, HD), (1, HD), (64, HD), (1, D))]
    outs = pl.pallas_call(
        body, name="small_params_adamw", in_specs=[vm] * 4, out_specs=[vm] * 33,
        out_shape=kinds * 4 + [jax.ShapeDtypeStruct((1, 1), F32)],
    )(g, w, m, v)
    return [outs[8 * k:8 * k + 8] for k in range(4)], outs[32]


def _pack_small(norm_mix, b_gate, qa, ka, qb, kb, rpb, norm_ffn):
    gains = jnp.concatenate([qa, ka, qb, kb, jnp.zeros((4, HD), F32)], axis=0)
    rpb_pad = jnp.pad(rpb.reshape(4 * (2 * WIN_R - 1), 2 * WIN_C - 1), ((0, 4), (0, HD - (2 * WIN_C - 1))))
    return jnp.concatenate([norm_mix.reshape(16, HD), b_gate.reshape(32, HD), gains, rpb_pad,
                            norm_ffn.reshape(16, HD), jnp.zeros((8, HD), F32)], axis=0)


LOSS_ROW = 136


def _rpb_from_rows(rows):
    return rows[:60, :2 * WIN_C - 1].reshape(1, 4, 2 * WIN_R - 1, 2 * WIN_C - 1)


def kernel(x, norm_mix, w_in, b_gate, q_norm_a, k_norm_a, q_norm_b, k_norm_b, rpb_b, w_proj_a, w_proj_b, w_out, norm_ffn, w_up, w_down, loss_target, m_norm_mix, m_w_in, m_b_gate, m_q_norm_a, m_k_norm_a, m_q_norm_b, m_k_norm_b, m_rpb_b, m_w_proj_a, m_w_proj_b, m_w_out, m_norm_ffn, m_w_up, m_w_down, v_norm_mix, v_w_in, v_b_gate, v_q_norm_a, v_k_norm_a, v_q_norm_b, v_k_norm_b, v_rpb_b, v_w_proj_a, v_w_proj_b, v_w_out, v_norm_ffn, v_w_up, v_w_down):
    big_w = (w_in[0], w_proj_a[0], w_proj_b[0], w_out[0], w_up[0], w_down[0])
    big_m = (m_w_in[0], m_w_proj_a[0], m_w_proj_b[0], m_w_out[0], m_w_up[0], m_w_down[0])
    big_v = (v_w_in[0], v_w_proj_a[0], v_w_proj_b[0], v_w_out[0], v_w_up[0], v_w_down[0])
    names = ("w_in", "w_proj_a", "w_proj_b", "w_out", "w_up", "w_down")

    g_in = [_gather_on_sequencer([_cast_bf16(big_w[0], window=win)], f"gather_w_in_{k}")[0]
            for k, win in enumerate(((0, W_IN_SPLIT), (W_IN_SPLIT, D_IN // N_DEV - W_IN_SPLIT)))]
    shards = [None] + [_cast_bf16(w) for w in big_w[1:5]]
    g_pa, g_pb, g_out, g_up = _gather_on_sequencer(shards[1:5], "gather_w_mix_up")
    small_w = _pack_small(norm_mix, b_gate, q_norm_a, k_norm_a, q_norm_b, k_norm_b, rpb_b, norm_ffn)
    small_m = _pack_small(m_norm_mix, m_b_gate, m_q_norm_a, m_k_norm_a, m_q_norm_b, m_k_norm_b, m_rpb_b, m_norm_ffn)
    small_v = _pack_small(v_norm_mix, v_b_gate, v_q_norm_a, v_k_norm_a, v_q_norm_b, v_k_norm_b, v_rpb_b, v_norm_ffn)
    g_down = _gather_on_sequencer([_cast_bf16(big_w[5], after=(small_w, small_m, small_v))],
                                  "gather_w_down")[0].reshape(1, D_FF, D)

    upd = [None] * 6
    in_flight = {}

    def weight_grads(tag, operands):
        sums = {i: _mm_tn_pair(a, b, name=f"grad_{names[i]}", **GRAD_TILES[i]) for i, (a, b) in operands.items()}
        new = list(sums.values())
        in_flight.update(zip(sums, _chip_exchange_on_sequencer(new, f"chip_exchange_{tag}")))
        return new

    def riders(name):
        i = {"proj_bwd": 5}.get(name)
        if i is None:
            return None
        return (_adamw_rider(in_flight.pop(i), big_w[i], big_m[i], big_v[i]),
                functools.partial(upd.__setitem__, i))

    loss, grad_x, small_g = _local_step(
        x[0], loss_target[0], norm_mix, b_gate, small_w[48:56], small_w[56:120], norm_ffn,
        g_in, g_pa, g_pb, g_out.reshape(D, D), g_up, g_down, weight_grads, riders)

    g_norm_mix, g_b, g_gains, g_rpb, g_norm_ffn = small_g
    small_part = jnp.concatenate([g_norm_mix.reshape(16, HD), g_b.reshape(32, HD),
                                  g_gains, g_rpb, g_norm_ffn.reshape(16, HD),
                                  jnp.pad(loss, ((0, 7), (0, HD - 1)))], axis=0)
    last = grad_x
    for i, r in in_flight.items():
        if i == 0:
            small_sum = _small_exchange(small_part, after=[last])
            small, total = _small_adamw(small_sum, small_w, small_m, small_v)
            last = total
        upd[i] = _adamw(r, big_w[i], big_m[i], big_v[i], name=f"adamw_{names[i]}", after=[last])
        last = upd[i][0]
    s_g, s_d, s_m, s_v = ((*k[:6], _rpb_from_rows(k[6]), k[7]) for k in small)
    b_g, b_d, b_m, b_v = ([u[j][None] for u in upd] for j in range(4))

    def order(small, big):
        nm, bg, qa, ka, qb, kb, rpb, nf = small
        w_in_, pa_, pb_, out_, up_, down_ = big
        return (nm, w_in_, bg, qa, ka, qb, kb, rpb, pa_, pb_, out_, nf, up_, down_)

    return (total[0, 0], grad_x[None], *order(s_g, b_g), *order(s_d, b_d), *order(s_m, b_m), *order(s_v, b_v))
```

```python
import functools
from typing import Callable, NamedTuple

import jax
import jax.numpy as jnp
import numpy as np
from jax import lax
from jax.experimental import pallas as pl
from jax.experimental.pallas import tpu as pltpu
from jax.experimental.pallas import tpu_sc as plsc

F32 = jnp.float32
BF16 = jnp.bfloat16

N_DEV = 8
S = 2048
D = 2048
HD = 128
NH = 16
NH_A = 12
QKV = NH * HD
D_IN = 3 * QKV + 2 * D
D_BR = 512
D_FF = 4 * D
GRID_W = 64
ROWS = S // GRID_W
WIN_R = 8
WIN_C = 16
EPS = 1e-6
NEG = -1e30
SCALE = HD ** -0.5
ROPE_THETA = 10000.0
DILATIONS = (1, 4, 16)
HALF_A = 64
QB = 128
W_IN_SPLIT = 768

LR, B1, B2, AEPS, WD, STEP = 0.001, 0.9, 0.999, 1e-08, 0.01, 10
BC1 = 1.0 - B1 ** STEP
BC2 = 1.0 - B2 ** STEP

VMEM_LIMIT = 56 * 1024 * 1024
MESH = pl.DeviceIdType.MESH

NN = (((1,), (0,)), ((), ()))
NT = (((1,), (1,)), ((), ()))
TN = (((0,), (0,)), ((), ()))


def _params(sem):
    return pltpu.CompilerParams(dimension_semantics=sem, vmem_limit_bytes=VMEM_LIMIT)


def _matmul(a, b, *, product, grid, a_spec, b_spec, epi, out_shape, out_specs, name,
            extra=(), extra_specs=(), after=(), carried=False, rider=None, into=()):
    n_extra = len(extra)
    single = not isinstance(out_shape, (list, tuple))
    out_shape = [out_shape] if single else list(out_shape)
    out_specs = [out_specs] if single else list(out_specs)
    ride = rider(grid[0] * grid[1], lambda j, i: j * grid[1] + i) if rider else None
    r_in = list(ride.inputs) if ride else []
    n_main = len(out_shape)

    def body(a_ref, b_ref, *rest):
        n_in = n_extra + len(after) + len(r_in)
        ins, outs = rest[:n_in], rest[n_in + len(into):]
        epi(product(a_ref, b_ref, ins[:n_extra]), ins[:n_extra], outs[:n_main])
        if ride:
            ride.body(ins[n_extra + len(after):], outs[n_main:])

    res = pl.pallas_call(
        body, name=name, grid=grid,
        in_specs=[a_spec, b_spec, *extra_specs, *[pl.BlockSpec(memory_space=pl.ANY)] * len(after),
                  *(ride.in_specs if ride else []), *[pl.BlockSpec(memory_space=pl.ANY)] * len(into)],
        out_specs=out_specs + (ride.out_specs if ride else []),
        out_shape=out_shape + (ride.out_shape if ride else []),
        input_output_aliases={2 + n_extra + len(after) + len(r_in) + k: k for k in range(len(into))},
        compiler_params=_params(("arbitrary", "arbitrary") if carried else ("parallel", "parallel")),
    )(a, b, *extra, *after, *r_in, *into)
    main = res[0] if single else res[:n_main]
    return (main, res[n_main:]) if ride else main


def _dot(x, y, dims):
    return lax.dot_general(x, y, dims, preferred_element_type=F32)


def _epi_store(acc, ex, outs):
    outs[0][...] = acc.astype(outs[0].dtype)


def _epi_residual(acc, ex, outs):
    outs[0][...] = acc + ex[0][...]


def _mm_nn(a, b3, *, tm, tn, name, out_dtypes=(F32,), epi=_epi_store, extra=(), total=False,
           col0=0, width=None, into=(), stride=None):
    m, kdim = a.shape
    g, _, ng = b3.shape
    n = g * ng
    c0 = col0 // tn
    if tn <= ng:
        npg = ng // tn
        b_spec = pl.BlockSpec((None, kdim, tn), lambda j, i: (j // npg, 0, j % npg))

        def product(a_ref, b_ref, ex):
            return _dot(a_ref[...], b_ref[...], NN)
    else:
        gb = tn // ng
        b_spec = pl.BlockSpec((gb, kdim, ng), lambda j, i: (j, 0, 0))

        def product(a_ref, b_ref, ex):
            return jnp.concatenate([_dot(a_ref[...], b_ref[q], NN) for q in range(gb)], axis=1)

    tile = pl.BlockSpec((tm, tn), lambda j, i: (i, j + c0))
    if stride is not None:
        assert tn == ng and not extra
        tile = pl.BlockSpec((pl.Element(tm), pl.Element(tn)),
                            lambda j, i: (i * tm, pl.multiple_of(j * stride + col0, 128)))
    shapes = [jax.ShapeDtypeStruct((m, width or n), dt) for dt in out_dtypes]
    specs = [tile] * len(shapes)
    if total:
        shapes.append(jax.ShapeDtypeStruct((1, 1), F32))
        specs.append(pl.BlockSpec((1, 1), lambda j, i: (0, 0)))
    single = len(shapes) == 1
    return _matmul(
        a, b3, product=product, grid=(n // tn, m // tm), epi=epi, name=name, carried=total, into=into,
        a_spec=pl.BlockSpec((tm, kdim), lambda j, i: (i, 0)), b_spec=b_spec,
        extra=extra, extra_specs=[tile] * len(extra),
        out_shape=shapes[0] if single else shapes, out_specs=specs[0] if single else specs)


def _mm_nt(a, b3, *, tm, tn, name, out_dtype=F32, epi=_epi_store, extra=(), after=(), rider=None, more_b=(),
           interleaved=False):
    m, kdim = a.shape
    _, n, _ = b3.shape
    n_b = len(more_b)

    def product(a_ref, b_ref, ex):
        refs = (b_ref, *ex[:n_b])
        pieces = ([(ref, q) for q in range(b_ref.shape[0]) for ref in refs] if interleaved
                  else [(ref, q) for ref in refs for q in range(ref.shape[0])])
        acc, k0 = None, 0
        for ref, q in pieces:
            part = _dot(a_ref[:, k0:k0 + ref.shape[2]], ref[q], NT)
            acc = part if acc is None else acc + part
            k0 += ref.shape[2]
        return acc

    def write(acc, ex, outs):
        epi(acc, ex[n_b:], outs)

    def w_spec(w):
        return pl.BlockSpec((w.shape[0], tn, w.shape[2]), lambda j, i: (0, j, 0))

    tile = pl.BlockSpec((tm, tn), lambda j, i: (i, j))
    return _matmul(
        a, b3, product=product, grid=(n // tn, m // tm), epi=write, name=name,
        a_spec=pl.BlockSpec((tm, kdim), lambda j, i: (i, 0)), b_spec=w_spec(b3),
        extra=(*more_b, *extra), extra_specs=[w_spec(w) for w in more_b] + [tile] * len(extra),
        after=after, rider=rider,
        out_shape=jax.ShapeDtypeStruct((m, n), out_dtype), out_specs=tile)


def _mm_tn(a, b, *, tm, tn, name, groups=1, out_dtype=BF16):
    t, m = a.shape
    _, n = b.shape
    ng = n // groups
    if tn <= ng:
        npg = ng // tn
        out_spec = pl.BlockSpec((None, tm, tn), lambda j, i: (j // npg, i, j % npg))
        epi = _epi_store

        def product(a_ref, b_ref, ex):
            return _dot(a_ref[...], b_ref[...], TN)
    else:
        gb = tn // ng
        out_spec = pl.BlockSpec((gb, tm, ng), lambda j, i: (j, i, 0))

        def product(a_ref, b_ref, ex):
            return [_dot(a_ref[...], b_ref[:, q * ng:(q + 1) * ng], TN) for q in range(gb)]

        def epi(parts, ex, outs):
            for q, part in enumerate(parts):
                outs[0][q] = part.astype(out_dtype)

    return _matmul(
        a, b, product=product, grid=(n // tn, m // tm), epi=epi, name=name,
        a_spec=pl.BlockSpec((t, tm), lambda j, i: (0, i)),
        b_spec=pl.BlockSpec((t, tn), lambda j, i: (0, j)),
        out_shape=jax.ShapeDtypeStruct((groups, m, ng), out_dtype), out_specs=out_spec)


def _rms_fwd(x, g, *, name, tr=256):
    def body(x_ref, g_ref, y_ref, r_ref):
        xv = x_ref[...]
        r = lax.rsqrt(jnp.mean(xv * xv, axis=-1, keepdims=True) + EPS)
        y_ref[...] = (xv * r * g_ref[...]).astype(BF16)
        r_ref[...] = r

    row = pl.BlockSpec((tr, D), lambda i: (i, 0))
    return pl.pallas_call(
        body, name=name, grid=(S // tr,),
        in_specs=[row, pl.BlockSpec((1, D), lambda i: (0, 0))],
        out_specs=[row, pl.BlockSpec((tr, 1), lambda i: (i, 0))],
        out_shape=[jax.ShapeDtypeStruct((S, D), BF16), jax.ShapeDtypeStruct((S, 1), F32)],
        compiler_params=_params(("parallel",)),
    )(x, g)


def _rms_bwd(dy, x, rstd, g, resid, *, name, bf16_copy, tr=256):
    def body(dy_ref, x_ref, r_ref, g_ref, res_ref, dx_ref, *rest):
        dg_ref = rest[-1]
        r = r_ref[...]
        xh = x_ref[...] * r
        dyv = dy_ref[...]
        t = dyv * g_ref[...]
        dx = r * (t - xh * jnp.mean(t * xh, axis=-1, keepdims=True)) + res_ref[...]
        dx_ref[...] = dx
        if bf16_copy:
            rest[0][...] = dx.astype(BF16)
        part = jnp.sum(dyv * xh, axis=0, keepdims=True)

        @pl.when(pl.program_id(0) == 0)
        def _():
            dg_ref[...] = part

        @pl.when(pl.program_id(0) > 0)
        def _():
            dg_ref[...] += part

    row = pl.BlockSpec((tr, D), lambda i: (i, 0))
    vec = pl.BlockSpec((1, D), lambda i: (0, 0))
    return pl.pallas_call(
        body, name=name, grid=(S // tr,),
        in_specs=[row, row, pl.BlockSpec((tr, 1), lambda i: (i, 0)), vec, row],
        out_specs=[row] + [row] * bf16_copy + [vec],
        out_shape=[jax.ShapeDtypeStruct((S, D), F32)] + [jax.ShapeDtypeStruct((S, D), BF16)] * bf16_copy
        + [jax.ShapeDtypeStruct((1, D), F32)],
        compiler_params=_params(("arbitrary",)),
    )(dy, x, rstd, g, resid)


def _rope_tables():
    pos = np.arange(S, dtype=np.float32)
    inv = (ROPE_THETA ** (-np.arange(0, HD, 2, dtype=np.float32) / HD)).astype(np.float32)
    ang = pos[:, None] * inv[None, :]
    cos, sin = np.cos(ang), np.sin(ang)
    return (jnp.asarray(np.concatenate([cos, cos], axis=-1), F32),
            jnp.asarray(np.concatenate([-sin, sin], axis=-1), F32))


def _swap_halves(t):
    return pltpu.roll(t, HD // 2, axis=1)


TOK = 256


def _lane_block_spec(d, last=HD):
    return pl.BlockSpec((4, TOK // d, d * last), lambda i: (0, i, 0))


def _to_lane_blocks(dst, head, val, d, scr, dtype):
    w = val.shape[1]
    if d == 1:
        dst[head] = val.astype(dtype)
        return
    scr[...] = val
    for r in range(d):
        dst[head, :, r * w:(r + 1) * w] = scr[pl.ds(r, TOK // d, stride=d), :].astype(dtype)


def _from_lane_blocks(src, head, d, w, scr):
    if d == 1:
        return src[head].astype(F32)
    for r in range(d):
        scr[pl.ds(r, TOK // d, stride=d), :] = src[head, :, r * w:(r + 1) * w].astype(F32)
    return scr[...]


def _qk_prep(proj, gains, cos2, sin2):
    def body(q_ref, k_ref, v_ref, g_ref, c_ref, s_ref, *rest):
        outs, scr = rest[:-1], rest[-1]
        cos, sin = c_ref[...], s_ref[...]
        for which, (src, row_a, row_b) in enumerate(((q_ref, 0, 2), (k_ref, 1, 3), (v_ref, None, None))):
            for h in range(NH):
                y = src[:, h * HD:(h + 1) * HD]
                if row_a is not None:
                    y = y * lax.rsqrt(jnp.mean(y * y, axis=-1, keepdims=True) + EPS)
                    if h < NH_A:
                        y = y * g_ref[row_a:row_a + 1, :]
                        y = y * cos + _swap_halves(y) * sin
                    else:
                        y = y * g_ref[row_b:row_b + 1, :]
                if h < NH_A:
                    gi = h // 4
                    _to_lane_blocks(outs[3 * gi + which], h % 4, y, DILATIONS[gi], scr, BF16)
                else:
                    hb = h - NH_A
                    outs[9 + which][:, hb * HD:(hb + 1) * HD] = y.astype(BF16)

    def blk(c):
        return pl.BlockSpec((TOK, QKV), lambda i: (i, c))
    tab = pl.BlockSpec((TOK, HD), lambda i: (i, 0))
    out_specs, out_shape = [], []
    for d in DILATIONS:
        out_specs += [_lane_block_spec(d)] * 3
        out_shape += [jax.ShapeDtypeStruct((4, S // d, d * HD), BF16)] * 3
    out_specs += [pl.BlockSpec((TOK, D_BR), lambda i: (i, 0))] * 3
    out_shape += [jax.ShapeDtypeStruct((S, D_BR), BF16)] * 3
    outs = pl.pallas_call(
        body, name="qk_prep", grid=(S // TOK,),
        in_specs=[blk(0), blk(1), blk(2), pl.BlockSpec((8, HD), lambda i: (0, 0)), tab, tab],
        out_specs=out_specs, out_shape=out_shape,
        scratch_shapes=[pltpu.VMEM((TOK, HD), F32)],
        compiler_params=_params(("parallel",)),
    )(proj, proj, proj, gains, cos2, sin2)
    return [tuple(outs[3 * gi:3 * gi + 3]) for gi in range(3)], tuple(outs[9:12])


def _qk_prep_bwd(dproj, proj, gains, cos2, sin2, grads_a, grads_b):
    def body(dp_in, q_ref, k_ref, g_ref, c_ref, s_ref, *rest):
        grads, (dp_out, dg_ref, scr) = rest[:12], rest[12:]
        del dp_in
        cos, sin = c_ref[...], s_ref[...]

        def grad_of(which, h):
            if h < NH_A:
                gi = h // 4
                return _from_lane_blocks(grads[3 * gi + which], h % 4, DILATIONS[gi], HD, scr)
            hb = h - NH_A
            return grads[9 + which][:, hb * HD:(hb + 1) * HD].astype(F32)

        dg_rows = []
        for which, (src, base, row_a, row_b) in enumerate(((q_ref, 0, 0, 2), (k_ref, QKV, 1, 3))):
            dg_a = jnp.zeros((1, HD), F32)
            dg_b = jnp.zeros((1, HD), F32)
            for h in range(NH):
                t = src[:, h * HD:(h + 1) * HD]
                dy = grad_of(which, h)
                r = lax.rsqrt(jnp.mean(t * t, axis=-1, keepdims=True) + EPS)
                xh = t * r
                if h < NH_A:
                    dy = dy * cos - _swap_halves(dy) * sin
                    gain = g_ref[row_a:row_a + 1, :]
                    dg_a = dg_a + jnp.sum(dy * xh, axis=0, keepdims=True)
                else:
                    gain = g_ref[row_b:row_b + 1, :]
                    dg_b = dg_b + jnp.sum(dy * xh, axis=0, keepdims=True)
                u = dy * gain
                dx = r * (u - xh * jnp.mean(u * xh, axis=-1, keepdims=True))
                dp_out[:, base + h * HD:base + (h + 1) * HD] = dx.astype(BF16)
            dg_rows += [(row_a, dg_a), (row_b, dg_b)]
        for h in range(NH):
            dp_out[:, 2 * QKV + h * HD:2 * QKV + (h + 1) * HD] = grad_of(2, h).astype(BF16)

        @pl.when(pl.program_id(0) == 0)
        def _():
            dg_ref[...] = jnp.zeros((8, HD), F32)

        for row, val in dg_rows:
            dg_ref[row:row + 1, :] += val

    def blk(c):
        return pl.BlockSpec((TOK, QKV), lambda i: (i, c))
    tab = pl.BlockSpec((TOK, HD), lambda i: (i, 0))
    gain_spec = pl.BlockSpec((8, HD), lambda i: (0, 0))
    grad_specs = [s for d in DILATIONS for s in [_lane_block_spec(d)] * 3]
    grad_specs += [pl.BlockSpec((TOK, D_BR), lambda i: (i, 0))] * 3
    return pl.pallas_call(
        body, name="qk_prep_bwd", grid=(S // TOK,),
        in_specs=[pl.BlockSpec(memory_space=pl.ANY), blk(0), blk(1), gain_spec, tab, tab] + grad_specs,
        out_specs=[pl.BlockSpec((TOK, 3 * QKV), lambda i: (i, 0)), gain_spec],
        out_shape=[jax.ShapeDtypeStruct((S, D_IN), BF16), jax.ShapeDtypeStruct((8, HD), F32)],
        input_output_aliases={0: 0},
        scratch_shapes=[pltpu.VMEM((TOK, HD), F32)],
        compiler_params=_params(("arbitrary",)),
    )(dproj, proj, proj, gains, cos2, sin2, *[g for grp in grads_a for g in grp], *grads_b)


def _mix_fwd(oa, ob, w_pa, w_pb, proj, b_gate, *, tr=256):
    def body(oa_ref, ob_ref, pa_ref, pb_ref, la_ref, lb_ref, ba_ref, bb_ref, mix_ref, ya_ref, yb_ref):
        ya = jnp.concatenate([_dot(oa_ref[...], pa_ref[q], NN) for q in range(N_DEV)], axis=1)
        yb = jnp.concatenate([_dot(ob_ref[...], pb_ref[q], NN) for q in range(N_DEV)], axis=1)
        ga = jax.nn.sigmoid(la_ref[...] + ba_ref[...])
        gb = jax.nn.sigmoid(lb_ref[...] + bb_ref[...])
        mix_ref[...] = (ga * ya + gb * yb).astype(BF16)
        ya_ref[...] = ya.astype(BF16)
        yb_ref[...] = yb.astype(BF16)

    row = pl.BlockSpec((tr, D), lambda i: (i, 0))
    branch = pl.BlockSpec((tr, D_BR), lambda i: (i, 0))
    whole = pl.BlockSpec((N_DEV, D_BR, D // N_DEV), lambda i: (0, 0, 0))
    return pl.pallas_call(
        body, name="mix_fwd", grid=(S // tr,),
        in_specs=[branch, branch, whole, whole,
                  pl.BlockSpec((tr, D), lambda i: (i, 3)), pl.BlockSpec((tr, D), lambda i: (i, 4)),
                  pl.BlockSpec((1, D), lambda i: (0, 0)), pl.BlockSpec((1, D), lambda i: (0, 1))],
        out_specs=[row, row, row], out_shape=[jax.ShapeDtypeStruct((S, D), BF16)] * 3,
        compiler_params=_params(("parallel",)),
    )(oa, ob, w_pa, w_pb, proj, proj, b_gate, b_gate)


def _mix_bwd(dh1b, w_out, proj, b_gate, ya, yb, *, tr=256):
    def body(dh_ref, w_ref, la_ref, lb_ref, b_ref, ya_ref, yb_ref, dya_ref, dyb_ref, dp_ref, db_ref):
        dm = _dot(dh_ref[...], w_ref[...], NT)
        parts = []
        for l_ref, y_ref, dy_ref, lo in ((la_ref, ya_ref, dya_ref, 0), (lb_ref, yb_ref, dyb_ref, D)):
            g = jax.nn.sigmoid(l_ref[...] + b_ref[:, lo:lo + D])
            dy_ref[...] = (dm * g).astype(BF16)
            dl = dm * y_ref[...].astype(F32) * g * (1.0 - g)
            dp_ref[:, lo:lo + D] = dl.astype(BF16)
            parts.append(jnp.sum(dl, axis=0, keepdims=True))
        part = jnp.concatenate(parts, axis=1)

        @pl.when(pl.program_id(0) == 0)
        def _():
            db_ref[...] = part

        @pl.when(pl.program_id(0) > 0)
        def _():
            db_ref[...] += part

    row = pl.BlockSpec((tr, D), lambda i: (i, 0))
    vec = pl.BlockSpec((1, 2 * D), lambda i: (0, 0))
    gate_cols = pl.BlockSpec((pl.Element(tr), pl.Element(2 * D)), lambda i: (i * tr, 3 * QKV))
    return pl.pallas_call(
        body, name="mix_bwd", grid=(S // tr,),
        in_specs=[row, pl.BlockSpec((D, D), lambda i: (0, 0)),
                  pl.BlockSpec((tr, D), lambda i: (i, 3)), pl.BlockSpec((tr, D), lambda i: (i, 4)), vec, row, row],
        out_specs=[row, row, gate_cols, vec],
        out_shape=[jax.ShapeDtypeStruct((S, D), BF16), jax.ShapeDtypeStruct((S, D), BF16),
                   jax.ShapeDtypeStruct((S, D_IN), BF16), jax.ShapeDtypeStruct((1, 2 * D), F32)],
        compiler_params=_params(("arbitrary",)),
    )(dh1b, w_out, proj, proj, b_gate, ya, yb)


def _band_blocks(m_len):
    wk = min(m_len, QB + 2 * QB)
    return [(qb * QB, min(max(qb * QB - QB, 0), m_len - wk), wk) for qb in range(m_len // QB)]


def _band_scores(q, kw, q0, k0, wk):
    s = _dot(q, kw, NT) * SCALE
    qpos = q0 + lax.broadcasted_iota(jnp.int32, (QB, 1), 0)
    kpos = k0 + lax.broadcasted_iota(jnp.int32, (1, wk), 1)
    return jnp.where(jnp.abs(kpos - qpos) <= HALF_A, s, NEG)


def _attn_a_fwd(q, k, v, gi):
    d = DILATIONS[gi]
    m_len = S // d

    def body(q_ref, k_ref, v_ref, o_ref, lse_ref):
        for r in range(d):
            lanes = slice(r * HD, (r + 1) * HD)
            for q0, k0, wk in _band_blocks(m_len):
                s = _band_scores(q_ref[q0:q0 + QB, lanes], k_ref[k0:k0 + wk, lanes], q0, k0, wk)
                m = jnp.max(s, axis=-1, keepdims=True)
                p = jnp.exp(s - m)
                l = jnp.sum(p, axis=-1, keepdims=True)
                o_ref[q0:q0 + QB, lanes] = _dot(p.astype(BF16), v_ref[k0:k0 + wk, lanes], NN) / l
                lse_ref[q0:q0 + QB, r:r + 1] = m + jnp.log(l)

    head = pl.BlockSpec((None, m_len, d * HD), lambda h: (h, 0, 0))
    stat = pl.BlockSpec((None, m_len, d), lambda h: (h, 0, 0))
    return pl.pallas_call(
        body, name=f"attn_a_fwd_{gi}", grid=(4,),
        in_specs=[head, head, head], out_specs=[head, stat],
        out_shape=[jax.ShapeDtypeStruct((4, m_len, d * HD), F32), jax.ShapeDtypeStruct((4, m_len, d), F32)],
        compiler_params=_params(("parallel",)),
    )(q, k, v)


def _combine_a(os, lses):
    def body(o0, o1, o2, l0, l1, l2, oa_ref, lse_ref, scr, scr1):
        for h in range(4):
            o = [_from_lane_blocks(ref, h, d, HD, scr) for ref, d in zip((o0, o1, o2), DILATIONS)]
            a, b, c = (_from_lane_blocks(ref, h, d, 1, scr1) for ref, d in zip((l0, l1, l2), DILATIONS))
            m = jnp.maximum(jnp.maximum(a, b), c)
            wa, wb, wc = jnp.exp(a - m), jnp.exp(b - m), jnp.exp(c - m)
            tot = wa + wb + wc
            oa_ref[:, h * HD:(h + 1) * HD] = ((wa * o[0] + wb * o[1] + wc * o[2]) / tot).astype(BF16)
            lse_ref[h] = m + jnp.log(tot)

    return pl.pallas_call(
        body, name="combine_a", grid=(S // TOK,),
        in_specs=[_lane_block_spec(d) for d in DILATIONS] + [_lane_block_spec(d, 1) for d in DILATIONS],
        out_specs=[pl.BlockSpec((TOK, D_BR), lambda i: (i, 0)), pl.BlockSpec((4, TOK, 1), lambda i: (0, i, 0))],
        out_shape=[jax.ShapeDtypeStruct((S, D_BR), BF16), jax.ShapeDtypeStruct((4, S, 1), F32)],
        scratch_shapes=[pltpu.VMEM((TOK, HD), F32), pltpu.VMEM((TOK, 1), F32)],
        compiler_params=_params(("parallel",)),
    )(*os, *lses)


def _proj_a_bwd(dya, w_pa, oa, lse):
    kg = D // N_DEV

    def body(dy_ref, w_ref, o_ref, l_ref, *rest):
        outs, (scr, scr1) = rest[:9], rest[9:]
        doa = _dot(dy_ref[:, 0:kg], w_ref[0], NT)
        for q in range(1, N_DEV):
            doa = doa + _dot(dy_ref[:, q * kg:(q + 1) * kg], w_ref[q], NT)
        for h in range(4):
            do = doa[:, h * HD:(h + 1) * HD]
            dsum = jnp.sum(do * o_ref[:, h * HD:(h + 1) * HD].astype(F32), axis=-1, keepdims=True)
            for gi, d in enumerate(DILATIONS):
                _to_lane_blocks(outs[3 * gi], h, do, d, scr, BF16)
                _to_lane_blocks(outs[3 * gi + 1], h, l_ref[h], d, scr1, F32)
                _to_lane_blocks(outs[3 * gi + 2], h, dsum, d, scr1, F32)

    row = pl.BlockSpec((TOK, D_BR), lambda i: (i, 0))
    out_specs, out_shape = [], []
    for d in DILATIONS:
        out_specs += [_lane_block_spec(d), _lane_block_spec(d, 1), _lane_block_spec(d, 1)]
        out_shape += [jax.ShapeDtypeStruct((4, S // d, d * HD), BF16)] + [jax.ShapeDtypeStruct((4, S // d, d), F32)] * 2
    outs = pl.pallas_call(
        body, name="proj_a_bwd", grid=(S // TOK,),
        in_specs=[pl.BlockSpec((TOK, D), lambda i: (i, 0)),
                  pl.BlockSpec((N_DEV, D_BR, kg), lambda i: (0, 0, 0)),
                  row, pl.BlockSpec((4, TOK, 1), lambda i: (0, i, 0))],
        out_specs=out_specs, out_shape=out_shape,
        scratch_shapes=[pltpu.VMEM((TOK, HD), F32), pltpu.VMEM((TOK, 1), F32)],
        compiler_params=_params(("parallel",)),
    )(dya, w_pa, oa, lse)
    return [tuple(outs[3 * gi:3 * gi + 3]) for gi in range(3)]


def _attn_a_bwd(q, k, v, do, lse, dsum, gi):
    d = DILATIONS[gi]
    m_len = S // d

    def body(q_ref, k_ref, v_ref, do_ref, lse_ref, dsum_ref, dq_ref, dk_out, dv_out, dk_ref, dv_ref):
        dk_ref[...] = jnp.zeros((m_len, d * HD), F32)
        dv_ref[...] = jnp.zeros((m_len, d * HD), F32)
        for r in range(d):
            lanes = slice(r * HD, (r + 1) * HD)
            for q0, k0, wk in _band_blocks(m_len):
                rows, keys = slice(q0, q0 + QB), slice(k0, k0 + wk)
                qv, kw, vw, dov = q_ref[rows, lanes], k_ref[keys, lanes], v_ref[keys, lanes], do_ref[rows, lanes]
                p = jnp.exp(_band_scores(qv, kw, q0, k0, wk) - lse_ref[rows, r:r + 1])
                ds = (p * (_dot(dov, vw, NT) - dsum_ref[rows, r:r + 1]) * SCALE).astype(BF16)
                dq_ref[rows, lanes] = _dot(ds, kw, NN).astype(BF16)
                dk_ref[keys, lanes] += _dot(ds, qv, TN)
                dv_ref[keys, lanes] += _dot(p.astype(BF16), dov, TN)
        dk_out[...] = dk_ref[...].astype(BF16)
        dv_out[...] = dv_ref[...].astype(BF16)

    head = pl.BlockSpec((None, m_len, d * HD), lambda h: (h, 0, 0))
    stat = pl.BlockSpec((None, m_len, d), lambda h: (h, 0, 0))
    shape = jax.ShapeDtypeStruct((4, m_len, d * HD), BF16)
    return pl.pallas_call(
        body, name=f"attn_a_bwd_{gi}", grid=(4,),
        in_specs=[head, head, head, head, stat, stat], out_specs=[head, head, head],
        out_shape=[shape, shape, shape],
        scratch_shapes=[pltpu.VMEM((m_len, d * HD), F32)] * 2,
        compiler_params=_params(("arbitrary",)),
    )(q, k, v, do, lse, dsum)


KEYS_B = WIN_R * GRID_W
N_OFF = WIN_R


def _bias_constants():
    q = np.arange(GRID_W)[:, None]
    kc = np.arange(GRID_W)[None, :]
    dc = np.clip(kc - q, -(WIN_C - 1), WIN_C - 1) + (WIN_C - 1)
    expand = np.zeros((HD, GRID_W * GRID_W), np.float32)
    expand[dc.reshape(-1), np.arange(GRID_W * GRID_W)] = 1.0
    cs = np.clip(q - WIN_C // 2, 0, GRID_W - WIN_C)
    keep = ((kc >= cs) & (kc < cs + WIN_C)).reshape(1, -1).astype(np.float32)
    sel = np.zeros((64, 4 * N_OFF * WIN_R), np.float32)
    for h in range(4):
        for off in range(N_OFF):
            for j in range(WIN_R):
                sel[h * (2 * WIN_R - 1) + off + j, (h * N_OFF + off) * WIN_R + j] = 1.0
    return jnp.asarray(expand), jnp.asarray(keep), jnp.asarray(sel)


def _bias_expand(rpb_pad, expand, keep, sel):
    def body(r_ref, e_ref, k_ref, s_ref, o_ref):
        t = lax.dot_general(r_ref[...], e_ref[...], NN, precision=lax.Precision.HIGHEST,
                            preferred_element_type=F32)
        rows = lax.dot_general(s_ref[...], t, TN, precision=lax.Precision.HIGHEST,
                               preferred_element_type=F32)
        o_ref[...] = jnp.where(k_ref[...] > 0.5, rows, NEG)

    return pl.pallas_call(
        body, name="bias_expand",
        out_shape=jax.ShapeDtypeStruct((4 * N_OFF * WIN_R, GRID_W * GRID_W), F32),
        compiler_params=pltpu.CompilerParams(vmem_limit_bytes=VMEM_LIMIT),
    )(rpb_pad, expand, keep, sel)


def _bias_reduce(dbias_tab):
    lane0 = GRID_W - WIN_C
    flip = np.zeros((GRID_W, GRID_W), np.float32)
    flip[np.arange(GRID_W), GRID_W - 1 - np.arange(GRID_W)] = 1.0
    place = np.zeros((WIN_R, 64, 4 * N_OFF), np.float32)
    for j in range(WIN_R):
        for h in range(4):
            for off in range(N_OFF):
                place[j, h * (2 * WIN_R - 1) + off + j, h * N_OFF + off] = 1.0

    def exact(x, y):
        return lax.dot_general(x, y, NN, precision=lax.Precision.HIGHEST, preferred_element_type=F32)

    def body(x_ref, flip_ref, place_ref, o_ref, z_ref):
        for h in range(4):
            for off in range(N_OFF):
                lined_up = pltpu.roll(exact(flip_ref[...], x_ref[h, off]), 0, axis=1, stride=1, stride_axis=0)
                z_ref[h * N_OFF + off:h * N_OFF + off + 1, :] = jnp.sum(lined_up, axis=0, keepdims=True)
        acc = jnp.zeros((64, HD), F32)
        for j in range(WIN_R):
            at_zero = pltpu.roll(z_ref[...], (KEYS_B - (j * GRID_W + lane0)) % KEYS_B, axis=1)[:, :HD]
            acc = acc + exact(place_ref[j], at_zero)
        lane = lax.broadcasted_iota(jnp.int32, (64, HD), 1)
        o_ref[...] = jnp.where(lane < 2 * WIN_C - 1, acc, 0.0)

    return pl.pallas_call(
        body, name="bias_reduce", out_shape=jax.ShapeDtypeStruct((64, HD), F32),
        scratch_shapes=[pltpu.VMEM((4 * N_OFF, KEYS_B), F32)],
        compiler_params=pltpu.CompilerParams(vmem_limit_bytes=VMEM_LIMIT),
    )(dbias_tab, jnp.asarray(flip), jnp.asarray(place))


def _rows_to_tab(rows):
    t = rows.reshape(4, N_OFF, WIN_R, GRID_W, GRID_W)
    return t.transpose(0, 1, 3, 2, 4).reshape(4, N_OFF, GRID_W, KEYS_B)


def _row_window(r):
    r0 = jnp.clip(r - WIN_R // 2, 0, ROWS - WIN_R)
    off = r0 + (WIN_R - 1) - r
    return pl.multiple_of(r * GRID_W, GRID_W), pl.multiple_of(r0 * GRID_W, GRID_W), off


def _attn_b_fwd(qn, kn, vb, bias_tab):
    def body(q_ref, k_ref, v_ref, b_ref, o_ref, lse_ref):
        def row(r, carry):
            qs, ks, off = _row_window(r)
            q = q_ref[pl.ds(qs, GRID_W), :]
            s = lax.dot_general(q, k_ref[pl.ds(ks, KEYS_B), :], NT, preferred_element_type=F32) * SCALE
            s = s + b_ref[off]
            m = jnp.max(s, axis=-1, keepdims=True)
            p = jnp.exp(s - m)
            l = jnp.sum(p, axis=-1, keepdims=True)
            o = lax.dot_general(p.astype(BF16), v_ref[pl.ds(ks, KEYS_B), :], NN, preferred_element_type=F32)
            o_ref[pl.ds(qs, GRID_W), :] = (o / l).astype(BF16)
            lse_ref[pl.ds(qs, GRID_W), :] = m + jnp.log(l)
            return carry

        lax.fori_loop(0, ROWS, row, 0, unroll=8)

    full = pl.BlockSpec((S, HD), lambda h: (0, h))
    return pl.pallas_call(
        body, name="attn_b_fwd", grid=(4,),
        in_specs=[full, full, full, pl.BlockSpec((None, N_OFF, GRID_W, KEYS_B), lambda h: (h, 0, 0, 0))],
        out_specs=[pl.BlockSpec((S, HD), lambda h: (0, h)), pl.BlockSpec((None, S, 1), lambda h: (h, 0, 0))],
        out_shape=[jax.ShapeDtypeStruct((S, D_BR), BF16), jax.ShapeDtypeStruct((4, S, 1), F32)],
        compiler_params=_params(("parallel",)),
    )(qn, kn, vb, bias_tab)


def _attn_b_bwd(qn, kn, vb, bias_tab, ob, dob, lse):
    def body(q_ref, k_ref, v_ref, b_ref, o_ref, do_ref, lse_ref, dq_ref, dk_out, dv_out, db_ref, dk_ref, dv_ref):
        dk_ref[...] = jnp.zeros((S, HD), F32)
        dv_ref[...] = jnp.zeros((S, HD), F32)
        db_ref[...] = jnp.zeros((N_OFF, GRID_W, KEYS_B), F32)

        def row(r, carry):
            qs, ks, off = _row_window(r)
            rows = pl.ds(qs, GRID_W)
            keys = pl.ds(ks, KEYS_B)
            q = q_ref[rows, :]
            kw = k_ref[keys, :]
            s = lax.dot_general(q, kw, NT, preferred_element_type=F32) * SCALE + b_ref[off]
            p = jnp.exp(s - lse_ref[rows, :])
            do = do_ref[rows, :]
            dobf = do.astype(BF16)
            dsum = jnp.sum(do * o_ref[rows, :].astype(F32), axis=-1, keepdims=True)
            dp = lax.dot_general(dobf, v_ref[keys, :], NT, preferred_element_type=F32)
            ds = p * (dp - dsum)
            db_ref[off] += ds
            dsb = (ds * SCALE).astype(BF16)
            dq_ref[rows, :] = lax.dot_general(dsb, kw, NN, preferred_element_type=F32).astype(BF16)
            dk_ref[keys, :] += lax.dot_general(dsb, q, TN, preferred_element_type=F32)
            dv_ref[keys, :] += lax.dot_general(p.astype(BF16), dobf, TN, preferred_element_type=F32)
            return carry

        lax.fori_loop(0, ROWS, row, 0, unroll=8)
        dk_out[...] = dk_ref[...].astype(BF16)
        dv_out[...] = dv_ref[...].astype(BF16)

    full = pl.BlockSpec((S, HD), lambda h: (0, h))
    slot = pl.BlockSpec((S, HD), lambda h: (0, h))
    tab = pl.BlockSpec((None, N_OFF, GRID_W, KEYS_B), lambda h: (h, 0, 0, 0))
    shape = jax.ShapeDtypeStruct((S, D_BR), BF16)
    return pl.pallas_call(
        body, name="attn_b_bwd", grid=(4,),
        in_specs=[full, full, full, tab, slot, slot, pl.BlockSpec((None, S, 1), lambda h: (h, 0, 0))],
        out_specs=[slot, slot, slot, tab],
        out_shape=[shape, shape, shape, jax.ShapeDtypeStruct((4, N_OFF, GRID_W, KEYS_B), F32)],
        scratch_shapes=[pltpu.VMEM((S, HD), F32)] * 2,
        compiler_params=_params(("arbitrary",)),
    )(qn, kn, vb, bias_tab, ob, dob, lse)


def _epi_relu_sq(acc, ex, outs):
    u = jnp.maximum(acc, 0.0)
    outs[0][...] = u.astype(BF16)
    outs[1][...] = (u * u).astype(BF16)


def _epi_relu_sq_bwd(acc, ex, outs):
    outs[0][...] = (acc * (2.0 * ex[0][...].astype(F32))).astype(BF16)


def _epi_loss_head(acc, ex, outs):
    e = acc + ex[0][...] - ex[1][...]
    dy = e * (1.0 / D)
    outs[0][...] = dy
    outs[1][...] = dy.astype(BF16)
    part = (0.5 / D) * jnp.sum(jnp.sum(e * e, axis=-1, keepdims=True), axis=0, keepdims=True)
    first = (pl.program_id(0) == 0) & (pl.program_id(1) == 0)

    @pl.when(first)
    def _():
        outs[2][...] = part

    @pl.when(jnp.logical_not(first))
    def _():
        outs[2][...] += part


def _local_step(x, target, norm_mix, b_gate, gains, rpb_pad, norm_ffn,
                w_in, w_pa, w_pb, w_out, w_up, w_down, weight_grads, riders=lambda name: None):
    def ridden(name, *args, **kwargs):
        ride = riders(name)
        if ride is None:
            return _mm_nt(*args, name=name, **kwargs)
        out, rode = _mm_nt(*args, name=name, rider=ride[0], **kwargs)
        ride[1](rode)
        return out

    cos2, sin2 = _rope_tables()
    expand, keep, sel = _bias_constants()
    w_out3 = w_out[None]

    xn, rstd1 = _rms_fwd(x, norm_mix, name="rms_mix")
    per_dev = D_IN // N_DEV
    proj = _mm_nn(xn, w_in[0], tm=S, tn=W_IN_SPLIT, name="proj_0", stride=per_dev, width=D_IN)
    proj = _mm_nn(xn, w_in[1], tm=S, tn=per_dev - W_IN_SPLIT, name="proj_1", stride=per_dev, width=D_IN,
                  col0=W_IN_SPLIT, into=(proj,))
    qkv_a, qkv_b = _qk_prep(proj, gains, cos2, sin2)
    fwd_a = [_attn_a_fwd(*qkv_a[gi], gi) for gi in range(3)]
    oa, lse_a = _combine_a([o for o, _ in fwd_a], [l for _, l in fwd_a])
    bias_tab = _rows_to_tab(_bias_expand(rpb_pad, expand, keep, sel))
    ob, lse_b = _attn_b_fwd(*qkv_b, bias_tab)
    mixed, ya, yb = _mix_fwd(oa, ob, w_pa, w_pb, proj, b_gate)
    h1 = _mm_nn(mixed, w_out3, tm=1024, tn=1024, name="out_proj", epi=_epi_residual, extra=(x,))
    hn, rstd2 = _rms_fwd(h1, norm_ffn, name="rms_ffn")
    u, usq = _mm_nn(hn, w_up, tm=S, tn=512, name="ffn_up", epi=_epi_relu_sq,
                    out_dtypes=(BF16, BF16))
    dy, dyb, loss = _mm_nn(usq, w_down, tm=512, tn=512, name="ffn_down", epi=_epi_loss_head,
                           extra=(h1, target), out_dtypes=(F32, BF16), total=True)

    sent = weight_grads("w_down", {5: (usq, dyb)})
    du = _mm_nt(dyb, w_down, tm=1024, tn=1024, name="ffn_down_bwd", out_dtype=BF16,
                epi=_epi_relu_sq_bwd, extra=(u,), after=sent)
    sent = weight_grads("w_up", {4: (hn, du)})
    dhn = ridden("ffn_up_bwd", du, w_up, tm=512, tn=512, after=sent)
    dh1, dh1b, g_norm_ffn = _rms_bwd(dhn, h1, rstd2, norm_ffn, dy, name="rms_ffn_bwd", bf16_copy=True)

    dya, dyb2, dproj, g_b = _mix_bwd(dh1b, w_out, proj, b_gate, ya, yb)
    sent = weight_grads("w_mix", {3: (mixed, dh1b)})
    dob = _mm_nt(dyb2, w_pb, tm=1024, tn=D_BR, name="proj_b_bwd", after=sent)
    prep = _proj_a_bwd(dya, w_pa, oa, lse_a)
    grads_a = [_attn_a_bwd(*qkv_a[gi], *prep[gi], gi) for gi in range(3)]
    dqb, dkb, dvb, dbias = _attn_b_bwd(*qkv_b, bias_tab, ob, dob, lse_b)
    g_rpb = _bias_reduce(dbias)
    dproj, g_gains = _qk_prep_bwd(dproj, proj, gains, cos2, sin2, grads_a, (dqb, dkb, dvb))
    sent = weight_grads("w_in", {0: (xn, dproj)})
    sent = sent + weight_grads("w_proj", {1: (oa, dya), 2: (ob, dyb2)}, after=sent)
    dxn = ridden("proj_bwd", dproj, w_in[0], more_b=(w_in[1],), interleaved=True, tm=256, tn=512, after=sent)
    grad_x, g_norm_mix = _rms_bwd(dxn, x, rstd1, norm_mix, dh1, name="rms_mix_bwd", bf16_copy=False)

    small = (g_norm_mix, g_b, g_gains, g_rpb, g_norm_ffn)
    return loss, grad_x, small


def _cast_bf16(w, *, part=0, parts=1, window=None, after=(), tr=256):
    rows, cols = w.shape[0], w.shape[1] // parts
    tr = min(tr, rows)
    src = pl.BlockSpec((tr, cols), lambda i: (i, part))
    if window is not None:
        part, cols = window
        src = pl.BlockSpec((pl.Element(tr), pl.Element(cols)), lambda i: (i * tr, part))

    def body(w_ref, *rest):
        rest[-1][...] = w_ref[...].astype(BF16)

    return pl.pallas_call(
        body, name=f"cast_{rows}x{cols}_{part}", grid=(rows // tr,),
        in_specs=[src] + [pl.BlockSpec(memory_space=pl.ANY)] * len(after),
        out_specs=pl.BlockSpec((tr, cols), lambda i: (i, 0)),
        out_shape=jax.ShapeDtypeStruct((rows, cols), BF16), compiler_params=_params(("parallel",)),
    )(w, *after)


def _me_and_peers():
    x, y, c = lax.axis_index("x"), lax.axis_index("y"), lax.axis_index("c")
    me = 4 * x + 2 * y + c
    peers = []
    for k in range(1, N_DEV):
        px = 1 - x if k & 4 else x
        py = 1 - y if k & 2 else y
        pc = 1 - c if k & 1 else c
        peers.append(((px, py, pc), 4 * px + 2 * py + pc))
    return me, peers


def _gather_on_sequencer(shards, name):
    n = len(shards)
    hbm = pltpu.MemorySpace.HBM
    ins = [jax.new_ref(s, memory_space=hbm) for s in shards]
    outs = [jax.empty_ref(jax.ShapeDtypeStruct((N_DEV,) + s.shape, s.dtype), memory_space=hbm) for s in shards]
    n_sem = 8

    @_sequencer(name, ((n, n_sem), (n, n_sem), (n,)), 0)
    def launch(send, recv, lsem):
        x, y, c = lax.axis_index("x"), lax.axis_index("y"), lax.axis_index("c")
        me, sibling = (x, y, c), (x, y, 1 - c)
        x_chip, y_chip, diagonal = (1 - x, y, c), (x, 1 - y, c), (1 - x, 1 - y, c)
        _handshake([sibling, x_chip, y_chip])

        def copy(w, k, block, to, src=None, half=None):
            px, py, pc = block
            dst = outs[w].at[4 * px + 2 * py + pc]
            if half is not None:
                rows = shards[w].shape[0] // 2
                dst = dst.at[pl.ds(half * rows, rows)]
            return pltpu.make_async_remote_copy(dst if src is None else src, dst, send.at[w, k], recv.at[w, k],
                                                device_id=to, device_id_type=MESH)

        local = [pltpu.make_async_copy(ins[w], outs[w].at[4 * x + 2 * y + c], lsem.at[w]) for w in range(n)]
        for cp in local:
            cp.start()
        sent = []
        for w in range(n):
            sent += [copy(w, 1, me, x_chip, src=ins[w]), copy(w, 2, me, y_chip, src=ins[w]),
                     copy(w, 0, me, sibling, src=ins[w])]
        for cp in sent:
            cp.start()
        for w in range(n):
            copy(w, 1, x_chip, me).wait_recv()
            sent += [copy(w, 3, x_chip, y_chip, half=0), copy(w, 5, x_chip, sibling)]
            sent[-2].start()
            sent[-1].start()
            copy(w, 2, y_chip, me).wait_recv()
            sent += [copy(w, 4, y_chip, x_chip, half=1), copy(w, 6, y_chip, sibling)]
            sent[-2].start()
            sent[-1].start()
        for w in range(n):
            copy(w, 3, diagonal, me, half=0).wait_recv()
            copy(w, 4, diagonal, me, half=1).wait_recv()
            sent.append(copy(w, 7, diagonal, sibling))
            sent[-1].start()
        for w in range(n):
            copy(w, 0, sibling, me).wait_recv()
            for k, chip in ((5, x_chip), (6, y_chip), (7, diagonal)):
                px, py, _ = chip
                copy(w, k, (px, py, 1 - c), me).wait_recv()
        for cp in sent:
            cp.wait_send()
        for cp in local:
            cp.wait()

    launch()
    return [o[...] for o in outs]


N_CHIP = 4


def _sequencer(name, n_sems, collective_id):
    return functools.partial(
        pl.kernel, mesh=plsc.ScalarSubcoreMesh(axis_name="seq", num_cores=1), name=name,
        scratch_types=tuple(pltpu.SemaphoreType.DMA(s) for s in n_sems),
        compiler_params=pltpu.CompilerParams(collective_id=collective_id))


def _handshake(peers):
    barrier = pltpu.get_barrier_semaphore()
    for peer in peers:
        pl.semaphore_signal(barrier, inc=1, device_id=peer, device_id_type=MESH)
    pl.semaphore_wait(barrier, len(peers))


def _chip_exchange_on_sequencer(parts, name):
    n = len(parts)
    hbm = pltpu.MemorySpace.HBM
    ins = [jax.new_ref(p, memory_space=hbm) for p in parts]
    outs = [jax.empty_ref(jax.ShapeDtypeStruct(p.shape, p.dtype), memory_space=hbm) for p in parts]

    @_sequencer(name, ((n, 3), (n, 3), (n,)), 2)
    def launch(send, recv, lsem):
        x, y, c = lax.axis_index("x"), lax.axis_index("y"), lax.axis_index("c")
        mine = 2 * x + y
        chips = [(1 - x, y), (x, 1 - y), (1 - x, 1 - y)]
        _handshake([(*chip, c) for chip in chips])
        local = [pltpu.make_async_copy(ins[w].at[mine], outs[w].at[mine], lsem.at[w]) for w in range(n)]
        for cp in local:
            cp.start()
        sends = []
        for w in range(n):
            for j, (px, py) in enumerate(chips):
                cp = pltpu.make_async_remote_copy(ins[w].at[2 * px + py], outs[w].at[mine],
                                                  send.at[w, j], recv.at[w, j],
                                                  device_id=(px, py, c), device_id_type=MESH)
                cp.start()
                sends.append(cp)
        for w in range(n):
            for j, (px, py) in enumerate(chips):
                pltpu.make_async_remote_copy(ins[w].at[mine], outs[w].at[2 * px + py],
                                             send.at[w, j], recv.at[w, j],
                                             device_id=(px, py, c), device_id_type=MESH).wait_recv()
        for cp in sends:
            cp.wait_send()
        for cp in local:
            cp.wait()

    launch()
    return [o[...] for o in outs]


GRAD_TILES = (dict(blocks_on="cols", tm=512, tn=1280), dict(blocks_on="cols", tm=512, tn=256),
              dict(blocks_on="cols", tm=512, tn=256), dict(blocks_on="rows", tm=256, tn=2048),
              dict(blocks_on="cols", tm=1024, tn=1024), dict(blocks_on="rows", tm=1024, tn=1024))


def _mm_tn_pair(a, b, *, blocks_on, tm, tn, name, after=()):
    t_len, m = a.shape
    n = b.shape[1]
    if blocks_on == "rows":
        rows, cols, inner = m // N_DEV, n, n // tn
        assert tm == rows
        a_spec = pl.BlockSpec((t_len, tm), lambda p, t, blk: (0, blk[p]))
        b_spec = pl.BlockSpec((t_len, tn), lambda p, t, blk: (0, t))
        out_spec = pl.BlockSpec((None, tm, tn), lambda p, t, blk: (
            jnp.maximum(p - N_CHIP, 0), 0, jnp.where(p < N_CHIP, 0, t)))
    else:
        rows, cols, inner = m, n // N_DEV, m // tm
        assert tn == cols
        a_spec = pl.BlockSpec((t_len, tm), lambda p, t, blk: (0, t))
        b_spec = pl.BlockSpec((t_len, tn), lambda p, t, blk: (0, blk[p]))
        out_spec = pl.BlockSpec((None, tm, tn), lambda p, t, blk: (
            jnp.maximum(p - N_CHIP, 0), jnp.where(p < N_CHIP, 0, t), 0))

    def body(blk_ref, a_ref, b_ref, *rest):
        del blk_ref
        o_ref, land, stage, send_sem, recv_sem = rest[len(after):]
        p, t = pl.program_id(0), pl.program_id(1)
        step = p * inner + t
        x, y, c = lax.axis_index("x"), lax.axis_index("y"), lax.axis_index("c")
        tile = _dot(a_ref[...], b_ref[...], TN)

        def to_sibling(slot, chip, piece):
            return pltpu.make_async_remote_copy(stage.at[slot], land.at[chip, piece], send_sem.at[slot],
                                                recv_sem.at[chip, piece],
                                                device_id=(x, y, 1 - c), device_id_type=MESH)

        @pl.when(p < N_CHIP)
        def _():
            slot = step % 2

            @pl.when(step >= 2)
            def _():
                to_sibling(slot, 0, 0).wait_send()

            stage[slot] = tile.astype(BF16)
            to_sibling(slot, p, t).start()

        @pl.when(step == N_CHIP * inner)
        def _():
            for slot in range(min(2, N_CHIP * inner)):
                to_sibling(slot, 0, 0).wait_send()

        @pl.when(p >= N_CHIP)
        def _():
            chip = p - N_CHIP
            to_sibling(0, chip, t).wait_recv()
            o_ref[...] = (tile + land[chip, t].astype(F32)).astype(BF16)

    c = lax.axis_index("c")
    order = jnp.stack([2 * ch + 1 - c for ch in range(N_CHIP)] + [2 * ch + c for ch in range(N_CHIP)])
    return pl.pallas_call(
        body, name=name,
        grid_spec=pltpu.PrefetchScalarGridSpec(
            num_scalar_prefetch=1, grid=(N_DEV, inner), out_specs=out_spec,
            in_specs=[a_spec, b_spec] + [pl.BlockSpec(memory_space=pl.ANY)] * len(after),
            scratch_shapes=[pltpu.VMEM((N_CHIP, inner, tm, tn), BF16), pltpu.VMEM((2, tm, tn), BF16),
                            pltpu.SemaphoreType.DMA((2,)), pltpu.SemaphoreType.DMA((N_CHIP, inner))]),
        out_shape=jax.ShapeDtypeStruct((N_CHIP, rows, cols), BF16),
        compiler_params=_params(("arbitrary", "arbitrary")),
    )(order.astype(jnp.int32), a, b, *after)


def _adamw_math(g, w, m, v):
    m2 = B1 * m + (1.0 - B1) * g
    v2 = B2 * v + (1.0 - B2) * (g * g)
    delta = -LR * ((m2 / BC1) / (jnp.sqrt(v2 / BC2) + AEPS) + WD * w)
    return delta, m2, v2


def _adamw_block(ins, outs):
    p_ref, w_ref, m_ref, v_ref = ins
    g = p_ref[0].astype(F32)
    for b in range(1, N_CHIP):
        g = g + p_ref[b].astype(F32)
    delta, m2, v2 = _adamw_math(g, w_ref[...], m_ref[...], v_ref[...])
    for ref, val in zip(outs, (g, delta, m2, v2)):
        ref[...] = val


class _Rider(NamedTuple):
    inputs: tuple
    in_specs: list
    out_shape: list
    out_specs: list
    body: Callable


def _adamw_rider(parts, w, m, v):
    rows, cols = w.shape

    def rider(steps, step_of):
        rr = rows // steps
        blk = pl.BlockSpec((rr, cols), lambda *ids: (step_of(*ids[:2]), 0))
        chips = pl.BlockSpec((N_CHIP, rr, cols), lambda *ids: (0, step_of(*ids[:2]), 0))
        shape = jax.ShapeDtypeStruct((rows, cols), F32)
        return _Rider((parts, w, m, v), [chips, blk, blk, blk], [shape] * 4, [blk] * 4, _adamw_block)

    return rider


def _adamw(parts, w, m, v, *, name, after=(), tr=256):
    rows, cols = w.shape

    def body(*refs):
        _adamw_block(refs[:4], refs[4 + len(after):])

    spec = pl.BlockSpec((tr, cols), lambda i: (i, 0))
    shape = jax.ShapeDtypeStruct((rows, cols), F32)
    return pl.pallas_call(
        body, name=name, grid=(rows // tr,),
        in_specs=[pl.BlockSpec((N_CHIP, tr, cols), lambda i: (0, i, 0)), spec, spec, spec]
        + [pl.BlockSpec(memory_space=pl.ANY)] * len(after),
        out_specs=[spec] * 4, out_shape=[shape] * 4,
        compiler_params=_params(("parallel",)),
    )(parts, w, m, v, *after)


def _small_exchange(part, after=()):
    rows = part.shape[0]

    def body(p_ref, *rest):
        g_ref, buf, send, recv = rest[len(after):]
        me, peers = _me_and_peers()
        buf[me] = p_ref[...]
        sends = []
        for k, (dev, _) in enumerate(peers):
            cp = pltpu.make_async_remote_copy(p_ref, buf.at[me], send.at[k], recv.at[k],
                                              device_id=dev, device_id_type=MESH)
            cp.start()
            sends.append(cp)
        for k, (dev, idx) in enumerate(peers):
            pltpu.make_async_remote_copy(p_ref, buf.at[idx], send.at[k], recv.at[k],
                                         device_id=dev, device_id_type=MESH).wait_recv()
        for cp in sends:
            cp.wait_send()
        g = buf[0]
        for b in range(1, N_DEV):
            g = g + buf[b]
        g_ref[...] = g

    vm = pl.BlockSpec(memory_space=pltpu.VMEM)
    return pl.pallas_call(
        body, name="small_params_exchange",
        in_specs=[vm] + [pl.BlockSpec(memory_space=pl.ANY)] * len(after),
        out_specs=vm, out_shape=jax.ShapeDtypeStruct((rows, HD), F32),
        scratch_shapes=[pltpu.VMEM((N_DEV, rows, HD), F32),
                        pltpu.SemaphoreType.DMA((N_DEV - 1,)), pltpu.SemaphoreType.DMA((N_DEV - 1,))],
    )(part, *after)


def _small_adamw(g, w, m, v):
    def body(g_ref, w_ref, m_ref, v_ref, *outs):
        g = g_ref[...]
        delta, m2, v2 = _adamw_math(g, w_ref[...], m_ref[...], v_ref[...])
        for k, val in enumerate((g, delta, m2, v2)):
            norm_mix, b_gate, qa, ka, qb, kb, rpb, norm_ffn = outs[8 * k:8 * k + 8]
            for dst, row0, n_rows in ((norm_mix, 0, 16), (b_gate, 16, 32), (norm_ffn, 120, 16)):
                for r in range(n_rows):
                    dst[:, r * HD:(r + 1) * HD] = val[row0 + r:row0 + r + 1, :]
            for i, dst in enumerate((qa, ka, qb, kb)):
                dst[...] = val[48 + i:49 + i, :]
            rpb[...] = val[56:120, :]
        outs[32][...] = g[LOSS_ROW:LOSS_ROW + 1, 0:1]

    vm = pl.BlockSpec(memory_space=pltpu.VMEM)
    kinds = [jax.ShapeDtypeStruct(sh, F32) for sh in
             ((1, D), (1, 2 * D), (1, HD), (1, HD), (1, HD), (1, HD), (64, HD), (1, D))]
    outs = pl.pallas_call(
        body, name="small_params_adamw", in_specs=[vm] * 4, out_specs=[vm] * 33,
        out_shape=kinds * 4 + [jax.ShapeDtypeStruct((1, 1), F32)],
    )(g, w, m, v)
    return [outs[8 * k:8 * k + 8] for k in range(4)], outs[32]


def _pack_small(norm_mix, b_gate, qa, ka, qb, kb, rpb, norm_ffn):
    gains = jnp.concatenate([qa, ka, qb, kb, jnp.zeros((4, HD), F32)], axis=0)
    rpb_pad = jnp.pad(rpb.reshape(4 * (2 * WIN_R - 1), 2 * WIN_C - 1), ((0, 4), (0, HD - (2 * WIN_C - 1))))
    return jnp.concatenate([norm_mix.reshape(16, HD), b_gate.reshape(32, HD), gains, rpb_pad,
                            norm_ffn.reshape(16, HD), jnp.zeros((8, HD), F32)], axis=0)


LOSS_ROW = 136


def _rpb_from_rows(rows):
    return rows[:60, :2 * WIN_C - 1].reshape(1, 4, 2 * WIN_R - 1, 2 * WIN_C - 1)


def kernel(x, norm_mix, w_in, b_gate, q_norm_a, k_norm_a, q_norm_b, k_norm_b, rpb_b, w_proj_a, w_proj_b, w_out, norm_ffn, w_up, w_down, loss_target, m_norm_mix, m_w_in, m_b_gate, m_q_norm_a, m_k_norm_a, m_q_norm_b, m_k_norm_b, m_rpb_b, m_w_proj_a, m_w_proj_b, m_w_out, m_norm_ffn, m_w_up, m_w_down, v_norm_mix, v_w_in, v_b_gate, v_q_norm_a, v_k_norm_a, v_q_norm_b, v_k_norm_b, v_rpb_b, v_w_proj_a, v_w_proj_b, v_w_out, v_norm_ffn, v_w_up, v_w_down):
    big_w = (w_in[0], w_proj_a[0], w_proj_b[0], w_out[0], w_up[0], w_down[0])
    big_m = (m_w_in[0], m_w_proj_a[0], m_w_proj_b[0], m_w_out[0], m_w_up[0], m_w_down[0])
    big_v = (v_w_in[0], v_w_proj_a[0], v_w_proj_b[0], v_w_out[0], v_w_up[0], v_w_down[0])
    names = ("w_in", "w_proj_a", "w_proj_b", "w_out", "w_up", "w_down")

    g_in = [_gather_on_sequencer([_cast_bf16(big_w[0], window=win)], f"gather_w_in_{k}")[0]
            for k, win in enumerate(((0, W_IN_SPLIT), (W_IN_SPLIT, D_IN // N_DEV - W_IN_SPLIT)))]
    shards = [None] + [_cast_bf16(w) for w in big_w[1:5]]
    g_pa, g_pb, g_out, g_up = _gather_on_sequencer(shards[1:5], "gather_w_mix_up")
    small_w = _pack_small(norm_mix, b_gate, q_norm_a, k_norm_a, q_norm_b, k_norm_b, rpb_b, norm_ffn)
    small_m = _pack_small(m_norm_mix, m_b_gate, m_q_norm_a, m_k_norm_a, m_q_norm_b, m_k_norm_b, m_rpb_b, m_norm_ffn)
    small_v = _pack_small(v_norm_mix, v_b_gate, v_q_norm_a, v_k_norm_a, v_q_norm_b, v_k_norm_b, v_rpb_b, v_norm_ffn)
    g_down = _gather_on_sequencer([_cast_bf16(big_w[5], after=(small_w, small_m, small_v))],
                                  "gather_w_down")[0].reshape(1, D_FF, D)

    upd = [None] * 6
    in_flight = {}

    def weight_grads(tag, operands, after=()):
        sums = {i: _mm_tn_pair(a, b, name=f"grad_{names[i]}", after=after, **GRAD_TILES[i])
                for i, (a, b) in operands.items()}
        new = list(sums.values())
        in_flight.update(zip(sums, _chip_exchange_on_sequencer(new, f"chip_exchange_{tag}")))
        return new

    def riders(name):
        i = {"proj_bwd": 5}.get(name)
        if i is None:
            return None
        return (_adamw_rider(in_flight.pop(i), big_w[i], big_m[i], big_v[i]),
                functools.partial(upd.__setitem__, i))

    loss, grad_x, small_g = _local_step(
        x[0], loss_target[0], norm_mix, b_gate, small_w[48:56], small_w[56:120], norm_ffn,
        g_in, g_pa, g_pb, g_out.reshape(D, D), g_up, g_down, weight_grads, riders)

    g_norm_mix, g_b, g_gains, g_rpb, g_norm_ffn = small_g
    small_part = jnp.concatenate([g_norm_mix.reshape(16, HD), g_b.reshape(32, HD),
                                  g_gains, g_rpb, g_norm_ffn.reshape(16, HD),
                                  jnp.pad(loss, ((0, 7), (0, HD - 1)))], axis=0)
    last = grad_x
    for i, r in in_flight.items():
        if i == 0:
            small_sum = _small_exchange(small_part, after=[last])
            small, total = _small_adamw(small_sum, small_w, small_m, small_v)
            last = total
        upd[i] = _adamw(r, big_w[i], big_m[i], big_v[i], name=f"adamw_{names[i]}", after=[last])
        last = upd[i][0]
    s_g, s_d, s_m, s_v = ((*k[:6], _rpb_from_rows(k[6]), k[7]) for k in small)
    b_g, b_d, b_m, b_v = ([u[j][None] for u in upd] for j in range(4))

    def order(small, big):
        nm, bg, qa, ka, qb, kb, rpb, nf = small
        w_in_, pa_, pb_, out_, up_, down_ = big
        return (nm, w_in_, bg, qa, ka, qb, kb, rpb, pa_, pb_, out_, nf, up_, down_)

    return (total[0, 0], grad_x[None], *order(s_g, b_g), *order(s_d, b_d), *order(s_m, b_m), *order(s_v, b_v))
```

```python
import functools
from typing import Callable, NamedTuple

import jax
import jax.numpy as jnp
import numpy as np
from jax import lax
from jax.experimental import pallas as pl
from jax.experimental.pallas import tpu as pltpu
from jax.experimental.pallas import tpu_sc as plsc

F32 = jnp.float32
BF16 = jnp.bfloat16

N_DEV = 8
S = 2048
D = 2048
HD = 128
NH = 16
NH_A = 12
QKV = NH * HD
D_IN = 3 * QKV + 2 * D
D_BR = 512
D_FF = 4 * D
GRID_W = 64
ROWS = S // GRID_W
WIN_R = 8
WIN_C = 16
EPS = 1e-6
NEG = -1e30
SCALE = HD ** -0.5
ROPE_THETA = 10000.0
DILATIONS = (1, 4, 16)
HALF_A = 64
QB = 128
W_IN_SPLIT = 768

LR, B1, B2, AEPS, WD, STEP = 0.001, 0.9, 0.999, 1e-08, 0.01, 10
BC1 = 1.0 - B1 ** STEP
BC2 = 1.0 - B2 ** STEP

VMEM_LIMIT = 56 * 1024 * 1024
MESH = pl.DeviceIdType.MESH

NN = (((1,), (0,)), ((), ()))
NT = (((1,), (1,)), ((), ()))
TN = (((0,), (0,)), ((), ()))


def _params(sem):
    return pltpu.CompilerParams(dimension_semantics=sem, vmem_limit_bytes=VMEM_LIMIT)


def _matmul(a, b, *, product, grid, a_spec, b_spec, epi, out_shape, out_specs, name,
            extra=(), extra_specs=(), after=(), carried=False, rider=None, into=()):
    n_extra = len(extra)
    single = not isinstance(out_shape, (list, tuple))
    out_shape = [out_shape] if single else list(out_shape)
    out_specs = [out_specs] if single else list(out_specs)
    ride = rider(grid[0] * grid[1], lambda j, i: j * grid[1] + i) if rider else None
    r_in = list(ride.inputs) if ride else []
    n_main = len(out_shape)

    def body(a_ref, b_ref, *rest):
        n_in = n_extra + len(after) + len(r_in)
        ins, outs = rest[:n_in], rest[n_in + len(into):]
        epi(product(a_ref, b_ref, ins[:n_extra]), ins[:n_extra], outs[:n_main])
        if ride:
            ride.body(ins[n_extra + len(after):], outs[n_main:])

    res = pl.pallas_call(
        body, name=name, grid=grid,
        in_specs=[a_spec, b_spec, *extra_specs, *[pl.BlockSpec(memory_space=pl.ANY)] * len(after),
                  *(ride.in_specs if ride else []), *[pl.BlockSpec(memory_space=pl.ANY)] * len(into)],
        out_specs=out_specs + (ride.out_specs if ride else []),
        out_shape=out_shape + (ride.out_shape if ride else []),
        input_output_aliases={2 + n_extra + len(after) + len(r_in) + k: k for k in range(len(into))},
        compiler_params=_params(("arbitrary", "arbitrary") if carried else ("parallel", "parallel")),
    )(a, b, *extra, *after, *r_in, *into)
    main = res[0] if single else res[:n_main]
    return (main, res[n_main:]) if ride else main


def _dot(x, y, dims):
    return lax.dot_general(x, y, dims, preferred_element_type=F32)


def _epi_store(acc, ex, outs):
    outs[0][...] = acc.astype(outs[0].dtype)


def _epi_residual(acc, ex, outs):
    outs[0][...] = acc + ex[0][...]


def _mm_nn(a, b3, *, tm, tn, name, out_dtypes=(F32,), epi=_epi_store, extra=(), total=False,
           col0=0, width=None, into=(), stride=None):
    m, kdim = a.shape
    g, _, ng = b3.shape
    n = g * ng
    c0 = col0 // tn
    if tn <= ng:
        npg = ng // tn
        b_spec = pl.BlockSpec((None, kdim, tn), lambda j, i: (j // npg, 0, j % npg))

        def product(a_ref, b_ref, ex):
            return _dot(a_ref[...], b_ref[...], NN)
    else:
        gb = tn // ng
        b_spec = pl.BlockSpec((gb, kdim, ng), lambda j, i: (j, 0, 0))

        def product(a_ref, b_ref, ex):
            return jnp.concatenate([_dot(a_ref[...], b_ref[q], NN) for q in range(gb)], axis=1)

    tile = pl.BlockSpec((tm, tn), lambda j, i: (i, j + c0))
    if stride is not None:
        assert tn == ng and not extra
        tile = pl.BlockSpec((pl.Element(tm), pl.Element(tn)),
                            lambda j, i: (i * tm, pl.multiple_of(j * stride + col0, 128)))
    shapes = [jax.ShapeDtypeStruct((m, width or n), dt) for dt in out_dtypes]
    specs = [tile] * len(shapes)
    if total:
        shapes.append(jax.ShapeDtypeStruct((1, 1), F32))
        specs.append(pl.BlockSpec((1, 1), lambda j, i: (0, 0)))
    single = len(shapes) == 1
    return _matmul(
        a, b3, product=product, grid=(n // tn, m // tm), epi=epi, name=name, carried=total, into=into,
        a_spec=pl.BlockSpec((tm, kdim), lambda j, i: (i, 0)), b_spec=b_spec,
        extra=extra, extra_specs=[tile] * len(extra),
        out_shape=shapes[0] if single else shapes, out_specs=specs[0] if single else specs)


def _mm_nt(a, b3, *, tm, tn, name, out_dtype=F32, epi=_epi_store, extra=(), after=(), rider=None, more_b=(),
           interleaved=False):
    m, kdim = a.shape
    _, n, _ = b3.shape
    n_b = len(more_b)

    def product(a_ref, b_ref, ex):
        refs = (b_ref, *ex[:n_b])
        pieces = ([(ref, q) for q in range(b_ref.shape[0]) for ref in refs] if interleaved
                  else [(ref, q) for ref in refs for q in range(ref.shape[0])])
        acc, k0 = None, 0
        for ref, q in pieces:
            part = _dot(a_ref[:, k0:k0 + ref.shape[2]], ref[q], NT)
            acc = part if acc is None else acc + part
            k0 += ref.shape[2]
        return acc

    def write(acc, ex, outs):
        epi(acc, ex[n_b:], outs)

    def w_spec(w):
        return pl.BlockSpec((w.shape[0], tn, w.shape[2]), lambda j, i: (0, j, 0))

    tile = pl.BlockSpec((tm, tn), lambda j, i: (i, j))
    return _matmul(
        a, b3, product=product, grid=(n // tn, m // tm), epi=write, name=name,
        a_spec=pl.BlockSpec((tm, kdim), lambda j, i: (i, 0)), b_spec=w_spec(b3),
        extra=(*more_b, *extra), extra_specs=[w_spec(w) for w in more_b] + [tile] * len(extra),
        after=after, rider=rider,
        out_shape=jax.ShapeDtypeStruct((m, n), out_dtype), out_specs=tile)


def _mm_tn(a, b, *, tm, tn, name, groups=1, out_dtype=BF16):
    t, m = a.shape
    _, n = b.shape
    ng = n // groups
    if tn <= ng:
        npg = ng // tn
        out_spec = pl.BlockSpec((None, tm, tn), lambda j, i: (j // npg, i, j % npg))
        epi = _epi_store

        def product(a_ref, b_ref, ex):
            return _dot(a_ref[...], b_ref[...], TN)
    else:
        gb = tn // ng
        out_spec = pl.BlockSpec((gb, tm, ng), lambda j, i: (j, i, 0))

        def product(a_ref, b_ref, ex):
            return [_dot(a_ref[...], b_ref[:, q * ng:(q + 1) * ng], TN) for q in range(gb)]

        def epi(parts, ex, outs):
            for q, part in enumerate(parts):
                outs[0][q] = part.astype(out_dtype)

    return _matmul(
        a, b, product=product, grid=(n // tn, m // tm), epi=epi, name=name,
        a_spec=pl.BlockSpec((t, tm), lambda j, i: (0, i)),
        b_spec=pl.BlockSpec((t, tn), lambda j, i: (0, j)),
        out_shape=jax.ShapeDtypeStruct((groups, m, ng), out_dtype), out_specs=out_spec)


def _rms_fwd(x, g, *, name, tr=256):
    def body(x_ref, g_ref, y_ref, r_ref):
        xv = x_ref[...]
        r = lax.rsqrt(jnp.mean(xv * xv, axis=-1, keepdims=True) + EPS)
        y_ref[...] = (xv * r * g_ref[...]).astype(BF16)
        r_ref[...] = r

    row = pl.BlockSpec((tr, D), lambda i: (i, 0))
    return pl.pallas_call(
        body, name=name, grid=(S // tr,),
        in_specs=[row, pl.BlockSpec((1, D), lambda i: (0, 0))],
        out_specs=[row, pl.BlockSpec((tr, 1), lambda i: (i, 0))],
        out_shape=[jax.ShapeDtypeStruct((S, D), BF16), jax.ShapeDtypeStruct((S, 1), F32)],
        compiler_params=_params(("parallel",)),
    )(x, g)


def _rms_bwd(dy, x, rstd, g, resid, *, name, bf16_copy, tr=256):
    def body(dy_ref, x_ref, r_ref, g_ref, res_ref, dx_ref, *rest):
        dg_ref = rest[-1]
        r = r_ref[...]
        xh = x_ref[...] * r
        dyv = dy_ref[...]
        t = dyv * g_ref[...]
        dx = r * (t - xh * jnp.mean(t * xh, axis=-1, keepdims=True)) + res_ref[...]
        dx_ref[...] = dx
        if bf16_copy:
            rest[0][...] = dx.astype(BF16)
        part = jnp.sum(dyv * xh, axis=0, keepdims=True)

        @pl.when(pl.program_id(0) == 0)
        def _():
            dg_ref[...] = part

        @pl.when(pl.program_id(0) > 0)
        def _():
            dg_ref[...] += part

    row = pl.BlockSpec((tr, D), lambda i: (i, 0))
    vec = pl.BlockSpec((1, D), lambda i: (0, 0))
    return pl.pallas_call(
        body, name=name, grid=(S // tr,),
        in_specs=[row, row, pl.BlockSpec((tr, 1), lambda i: (i, 0)), vec, row],
        out_specs=[row] + [row] * bf16_copy + [vec],
        out_shape=[jax.ShapeDtypeStruct((S, D), F32)] + [jax.ShapeDtypeStruct((S, D), BF16)] * bf16_copy
        + [jax.ShapeDtypeStruct((1, D), F32)],
        compiler_params=_params(("arbitrary",)),
    )(dy, x, rstd, g, resid)


def _rope_tables():
    pos = np.arange(S, dtype=np.float32)
    inv = (ROPE_THETA ** (-np.arange(0, HD, 2, dtype=np.float32) / HD)).astype(np.float32)
    ang = pos[:, None] * inv[None, :]
    cos, sin = np.cos(ang), np.sin(ang)
    return (jnp.asarray(np.concatenate([cos, cos], axis=-1), F32),
            jnp.asarray(np.concatenate([-sin, sin], axis=-1), F32))


def _swap_halves(t):
    return pltpu.roll(t, HD // 2, axis=1)


TOK = 256


def _lane_block_spec(d, last=HD):
    return pl.BlockSpec((4, TOK // d, d * last), lambda i: (0, i, 0))


def _to_lane_blocks(dst, head, val, d, scr, dtype):
    w = val.shape[1]
    if d == 1:
        dst[head] = val.astype(dtype)
        return
    scr[...] = val
    for r in range(d):
        dst[head, :, r * w:(r + 1) * w] = scr[pl.ds(r, TOK // d, stride=d), :].astype(dtype)


def _from_lane_blocks(src, head, d, w, scr):
    if d == 1:
        return src[head].astype(F32)
    for r in range(d):
        scr[pl.ds(r, TOK // d, stride=d), :] = src[head, :, r * w:(r + 1) * w].astype(F32)
    return scr[...]


def _qk_prep(proj, gains, cos2, sin2):
    def body(q_ref, k_ref, v_ref, g_ref, c_ref, s_ref, *rest):
        outs, scr = rest[:-1], rest[-1]
        cos, sin = c_ref[...], s_ref[...]
        for which, (src, row_a, row_b) in enumerate(((q_ref, 0, 2), (k_ref, 1, 3), (v_ref, None, None))):
            for h in range(NH):
                y = src[:, h * HD:(h + 1) * HD]
                if row_a is not None:
                    y = y * lax.rsqrt(jnp.mean(y * y, axis=-1, keepdims=True) + EPS)
                    if h < NH_A:
                        y = y * g_ref[row_a:row_a + 1, :]
                        y = y * cos + _swap_halves(y) * sin
                    else:
                        y = y * g_ref[row_b:row_b + 1, :]
                if h < NH_A:
                    gi = h // 4
                    _to_lane_blocks(outs[3 * gi + which], h % 4, y, DILATIONS[gi], scr, BF16)
                else:
                    hb = h - NH_A
                    outs[9 + which][:, hb * HD:(hb + 1) * HD] = y.astype(BF16)

    def blk(c):
        return pl.BlockSpec((TOK, QKV), lambda i: (i, c))
    tab = pl.BlockSpec((TOK, HD), lambda i: (i, 0))
    out_specs, out_shape = [], []
    for d in DILATIONS:
        out_specs += [_lane_block_spec(d)] * 3
        out_shape += [jax.ShapeDtypeStruct((4, S // d, d * HD), BF16)] * 3
    out_specs += [pl.BlockSpec((TOK, D_BR), lambda i: (i, 0))] * 3
    out_shape += [jax.ShapeDtypeStruct((S, D_BR), BF16)] * 3
    outs = pl.pallas_call(
        body, name="qk_prep", grid=(S // TOK,),
        in_specs=[blk(0), blk(1), blk(2), pl.BlockSpec((8, HD), lambda i: (0, 0)), tab, tab],
        out_specs=out_specs, out_shape=out_shape,
        scratch_shapes=[pltpu.VMEM((TOK, HD), F32)],
        compiler_params=_params(("parallel",)),
    )(proj, proj, proj, gains, cos2, sin2)
    return [tuple(outs[3 * gi:3 * gi + 3]) for gi in range(3)], tuple(outs[9:12])


def _qk_prep_bwd(dproj, proj, gains, cos2, sin2, grads_a, grads_b):
    def body(dp_in, q_ref, k_ref, g_ref, c_ref, s_ref, *rest):
        grads, (dp_out, dg_ref, scr) = rest[:12], rest[12:]
        del dp_in
        cos, sin = c_ref[...], s_ref[...]

        def grad_of(which, h):
            if h < NH_A:
                gi = h // 4
                return _from_lane_blocks(grads[3 * gi + which], h % 4, DILATIONS[gi], HD, scr)
            hb = h - NH_A
            return grads[9 + which][:, hb * HD:(hb + 1) * HD].astype(F32)

        dg_rows = []
        for which, (src, base, row_a, row_b) in enumerate(((q_ref, 0, 0, 2), (k_ref, QKV, 1, 3))):
            dg_a = jnp.zeros((1, HD), F32)
            dg_b = jnp.zeros((1, HD), F32)
            for h in range(NH):
                t = src[:, h * HD:(h + 1) * HD]
                dy = grad_of(which, h)
                r = lax.rsqrt(jnp.mean(t * t, axis=-1, keepdims=True) + EPS)
                xh = t * r
                if h < NH_A:
                    dy = dy * cos - _swap_halves(dy) * sin
                    gain = g_ref[row_a:row_a + 1, :]
                    dg_a = dg_a + jnp.sum(dy * xh, axis=0, keepdims=True)
                else:
                    gain = g_ref[row_b:row_b + 1, :]
                    dg_b = dg_b + jnp.sum(dy * xh, axis=0, keepdims=True)
                u = dy * gain
                dx = r * (u - xh * jnp.mean(u * xh, axis=-1, keepdims=True))
                dp_out[:, base + h * HD:base + (h + 1) * HD] = dx.astype(BF16)
            dg_rows += [(row_a, dg_a), (row_b, dg_b)]
        for h in range(NH):
            dp_out[:, 2 * QKV + h * HD:2 * QKV + (h + 1) * HD] = grad_of(2, h).astype(BF16)

        @pl.when(pl.program_id(0) == 0)
        def _():
            dg_ref[...] = jnp.zeros((8, HD), F32)

        for row, val in dg_rows:
            dg_ref[row:row + 1, :] += val

    def blk(c):
        return pl.BlockSpec((TOK, QKV), lambda i: (i, c))
    tab = pl.BlockSpec((TOK, HD), lambda i: (i, 0))
    gain_spec = pl.BlockSpec((8, HD), lambda i: (0, 0))
    grad_specs = [s for d in DILATIONS for s in [_lane_block_spec(d)] * 3]
    grad_specs += [pl.BlockSpec((TOK, D_BR), lambda i: (i, 0))] * 3
    return pl.pallas_call(
        body, name="qk_prep_bwd", grid=(S // TOK,),
        in_specs=[pl.BlockSpec(memory_space=pl.ANY), blk(0), blk(1), gain_spec, tab, tab] + grad_specs,
        out_specs=[pl.BlockSpec((TOK, 3 * QKV), lambda i: (i, 0)), gain_spec],
        out_shape=[jax.ShapeDtypeStruct((S, D_IN), BF16), jax.ShapeDtypeStruct((8, HD), F32)],
        input_output_aliases={0: 0},
        scratch_shapes=[pltpu.VMEM((TOK, HD), F32)],
        compiler_params=_params(("arbitrary",)),
    )(dproj, proj, proj, gains, cos2, sin2, *[g for grp in grads_a for g in grp], *grads_b)


def _mix_fwd(oa, ob, w_pa, w_pb, proj, b_gate, *, tr=256):
    def body(oa_ref, ob_ref, pa_ref, pb_ref, la_ref, lb_ref, ba_ref, bb_ref, mix_ref, ya_ref, yb_ref):
        ya = jnp.concatenate([_dot(oa_ref[...], pa_ref[q], NN) for q in range(N_DEV)], axis=1)
        yb = jnp.concatenate([_dot(ob_ref[...], pb_ref[q], NN) for q in range(N_DEV)], axis=1)
        ga = jax.nn.sigmoid(la_ref[...] + ba_ref[...])
        gb = jax.nn.sigmoid(lb_ref[...] + bb_ref[...])
        mix_ref[...] = (ga * ya + gb * yb).astype(BF16)
        ya_ref[...] = ya.astype(BF16)
        yb_ref[...] = yb.astype(BF16)

    row = pl.BlockSpec((tr, D), lambda i: (i, 0))
    branch = pl.BlockSpec((tr, D_BR), lambda i: (i, 0))
    whole = pl.BlockSpec((N_DEV, D_BR, D // N_DEV), lambda i: (0, 0, 0))
    return pl.pallas_call(
        body, name="mix_fwd", grid=(S // tr,),
        in_specs=[branch, branch, whole, whole,
                  pl.BlockSpec((tr, D), lambda i: (i, 3)), pl.BlockSpec((tr, D), lambda i: (i, 4)),
                  pl.BlockSpec((1, D), lambda i: (0, 0)), pl.BlockSpec((1, D), lambda i: (0, 1))],
        out_specs=[row, row, row], out_shape=[jax.ShapeDtypeStruct((S, D), BF16)] * 3,
        compiler_params=_params(("parallel",)),
    )(oa, ob, w_pa, w_pb, proj, proj, b_gate, b_gate)


def _mix_bwd(dh1b, w_out, proj, b_gate, ya, yb, *, tr=256):
    def body(dh_ref, w_ref, la_ref, lb_ref, b_ref, ya_ref, yb_ref, dya_ref, dyb_ref, dp_ref, db_ref):
        dm = _dot(dh_ref[...], w_ref[...], NT)
        parts = []
        for l_ref, y_ref, dy_ref, lo in ((la_ref, ya_ref, dya_ref, 0), (lb_ref, yb_ref, dyb_ref, D)):
            g = jax.nn.sigmoid(l_ref[...] + b_ref[:, lo:lo + D])
            dy_ref[...] = (dm * g).astype(BF16)
            dl = dm * y_ref[...].astype(F32) * g * (1.0 - g)
            dp_ref[:, lo:lo + D] = dl.astype(BF16)
            parts.append(jnp.sum(dl, axis=0, keepdims=True))
        part = jnp.concatenate(parts, axis=1)

        @pl.when(pl.program_id(0) == 0)
        def _():
            db_ref[...] = part

        @pl.when(pl.program_id(0) > 0)
        def _():
            db_ref[...] += part

    row = pl.BlockSpec((tr, D), lambda i: (i, 0))
    vec = pl.BlockSpec((1, 2 * D), lambda i: (0, 0))
    gate_cols = pl.BlockSpec((pl.Element(tr), pl.Element(2 * D)), lambda i: (i * tr, 3 * QKV))
    return pl.pallas_call(
        body, name="mix_bwd", grid=(S // tr,),
        in_specs=[row, pl.BlockSpec((D, D), lambda i: (0, 0)),
                  pl.BlockSpec((tr, D), lambda i: (i, 3)), pl.BlockSpec((tr, D), lambda i: (i, 4)), vec, row, row],
        out_specs=[row, row, gate_cols, vec],
        out_shape=[jax.ShapeDtypeStruct((S, D), BF16), jax.ShapeDtypeStruct((S, D), BF16),
                   jax.ShapeDtypeStruct((S, D_IN), BF16), jax.ShapeDtypeStruct((1, 2 * D), F32)],
        compiler_params=_params(("arbitrary",)),
    )(dh1b, w_out, proj, proj, b_gate, ya, yb)


def _band_blocks(m_len):
    wk = min(m_len, QB + 2 * QB)
    return [(qb * QB, min(max(qb * QB - QB, 0), m_len - wk), wk) for qb in range(m_len // QB)]


def _band_scores(q, kw, q0, k0, wk):
    s = _dot(q, kw, NT) * SCALE
    qpos = q0 + lax.broadcasted_iota(jnp.int32, (QB, 1), 0)
    kpos = k0 + lax.broadcasted_iota(jnp.int32, (1, wk), 1)
    return jnp.where(jnp.abs(kpos - qpos) <= HALF_A, s, NEG)


def _attn_a_fwd(q, k, v, gi):
    d = DILATIONS[gi]
    m_len = S // d

    def body(q_ref, k_ref, v_ref, o_ref, lse_ref):
        for r in range(d):
            lanes = slice(r * HD, (r + 1) * HD)
            for q0, k0, wk in _band_blocks(m_len):
                s = _band_scores(q_ref[q0:q0 + QB, lanes], k_ref[k0:k0 + wk, lanes], q0, k0, wk)
                m = jnp.max(s, axis=-1, keepdims=True)
                p = jnp.exp(s - m)
                l = jnp.sum(p, axis=-1, keepdims=True)
                o_ref[q0:q0 + QB, lanes] = _dot(p.astype(BF16), v_ref[k0:k0 + wk, lanes], NN) / l
                lse_ref[q0:q0 + QB, r:r + 1] = m + jnp.log(l)

    head = pl.BlockSpec((None, m_len, d * HD), lambda h: (h, 0, 0))
    stat = pl.BlockSpec((None, m_len, d), lambda h: (h, 0, 0))
    return pl.pallas_call(
        body, name=f"attn_a_fwd_{gi}", grid=(4,),
        in_specs=[head, head, head], out_specs=[head, stat],
        out_shape=[jax.ShapeDtypeStruct((4, m_len, d * HD), F32), jax.ShapeDtypeStruct((4, m_len, d), F32)],
        compiler_params=_params(("parallel",)),
    )(q, k, v)


def _combine_a(os, lses):
    def body(o0, o1, o2, l0, l1, l2, oa_ref, lse_ref, scr, scr1):
        for h in range(4):
            o = [_from_lane_blocks(ref, h, d, HD, scr) for ref, d in zip((o0, o1, o2), DILATIONS)]
            a, b, c = (_from_lane_blocks(ref, h, d, 1, scr1) for ref, d in zip((l0, l1, l2), DILATIONS))
            m = jnp.maximum(jnp.maximum(a, b), c)
            wa, wb, wc = jnp.exp(a - m), jnp.exp(b - m), jnp.exp(c - m)
            tot = wa + wb + wc
            oa_ref[:, h * HD:(h + 1) * HD] = ((wa * o[0] + wb * o[1] + wc * o[2]) / tot).astype(BF16)
            lse_ref[h] = m + jnp.log(tot)

    return pl.pallas_call(
        body, name="combine_a", grid=(S // TOK,),
        in_specs=[_lane_block_spec(d) for d in DILATIONS] + [_lane_block_spec(d, 1) for d in DILATIONS],
        out_specs=[pl.BlockSpec((TOK, D_BR), lambda i: (i, 0)), pl.BlockSpec((4, TOK, 1), lambda i: (0, i, 0))],
        out_shape=[jax.ShapeDtypeStruct((S, D_BR), BF16), jax.ShapeDtypeStruct((4, S, 1), F32)],
        scratch_shapes=[pltpu.VMEM((TOK, HD), F32), pltpu.VMEM((TOK, 1), F32)],
        compiler_params=_params(("parallel",)),
    )(*os, *lses)


def _proj_a_bwd(dya, w_pa, oa, lse):
    kg = D // N_DEV

    def body(dy_ref, w_ref, o_ref, l_ref, *rest):
        outs, (scr, scr1) = rest[:9], rest[9:]
        doa = _dot(dy_ref[:, 0:kg], w_ref[0], NT)
        for q in range(1, N_DEV):
            doa = doa + _dot(dy_ref[:, q * kg:(q + 1) * kg], w_ref[q], NT)
        for h in range(4):
            do = doa[:, h * HD:(h + 1) * HD]
            dsum = jnp.sum(do * o_ref[:, h * HD:(h + 1) * HD].astype(F32), axis=-1, keepdims=True)
            for gi, d in enumerate(DILATIONS):
                _to_lane_blocks(outs[3 * gi], h, do, d, scr, BF16)
                _to_lane_blocks(outs[3 * gi + 1], h, l_ref[h], d, scr1, F32)
                _to_lane_blocks(outs[3 * gi + 2], h, dsum, d, scr1, F32)

    row = pl.BlockSpec((TOK, D_BR), lambda i: (i, 0))
    out_specs, out_shape = [], []
    for d in DILATIONS:
        out_specs += [_lane_block_spec(d), _lane_block_spec(d, 1), _lane_block_spec(d, 1)]
        out_shape += [jax.ShapeDtypeStruct((4, S // d, d * HD), BF16)] + [jax.ShapeDtypeStruct((4, S // d, d), F32)] * 2
    outs = pl.pallas_call(
        body, name="proj_a_bwd", grid=(S // TOK,),
        in_specs=[pl.BlockSpec((TOK, D), lambda i: (i, 0)),
                  pl.BlockSpec((N_DEV, D_BR, kg), lambda i: (0, 0, 0)),
                  row, pl.BlockSpec((4, TOK, 1), lambda i: (0, i, 0))],
        out_specs=out_specs, out_shape=out_shape,
        scratch_shapes=[pltpu.VMEM((TOK, HD), F32), pltpu.VMEM((TOK, 1), F32)],
        compiler_params=_params(("parallel",)),
    )(dya, w_pa, oa, lse)
    return [tuple(outs[3 * gi:3 * gi + 3]) for gi in range(3)]


def _attn_a_bwd(q, k, v, do, lse, dsum, gi):
    d = DILATIONS[gi]
    m_len = S // d

    def body(q_ref, k_ref, v_ref, do_ref, lse_ref, dsum_ref, dq_ref, dk_out, dv_out, dk_ref, dv_ref):
        dk_ref[...] = jnp.zeros((m_len, d * HD), F32)
        dv_ref[...] = jnp.zeros((m_len, d * HD), F32)
        for r in range(d):
            lanes = slice(r * HD, (r + 1) * HD)
            for q0, k0, wk in _band_blocks(m_len):
                rows, keys = slice(q0, q0 + QB), slice(k0, k0 + wk)
                qv, kw, vw, dov = q_ref[rows, lanes], k_ref[keys, lanes], v_ref[keys, lanes], do_ref[rows, lanes]
                p = jnp.exp(_band_scores(qv, kw, q0, k0, wk) - lse_ref[rows, r:r + 1])
                ds = (p * (_dot(dov, vw, NT) - dsum_ref[rows, r:r + 1]) * SCALE).astype(BF16)
                dq_ref[rows, lanes] = _dot(ds, kw, NN).astype(BF16)
                dk_ref[keys, lanes] += _dot(ds, qv, TN)
                dv_ref[keys, lanes] += _dot(p.astype(BF16), dov, TN)
        dk_out[...] = dk_ref[...].astype(BF16)
        dv_out[...] = dv_ref[...].astype(BF16)

    head = pl.BlockSpec((None, m_len, d * HD), lambda h: (h, 0, 0))
    stat = pl.BlockSpec((None, m_len, d), lambda h: (h, 0, 0))
    shape = jax.ShapeDtypeStruct((4, m_len, d * HD), BF16)
    return pl.pallas_call(
        body, name=f"attn_a_bwd_{gi}", grid=(4,),
        in_specs=[head, head, head, head, stat, stat], out_specs=[head, head, head],
        out_shape=[shape, shape, shape],
        scratch_shapes=[pltpu.VMEM((m_len, d * HD), F32)] * 2,
        compiler_params=_params(("arbitrary",)),
    )(q, k, v, do, lse, dsum)


KEYS_B = WIN_R * GRID_W
N_OFF = WIN_R


def _bias_constants():
    q = np.arange(GRID_W)[:, None]
    kc = np.arange(GRID_W)[None, :]
    dc = np.clip(kc - q, -(WIN_C - 1), WIN_C - 1) + (WIN_C - 1)
    expand = np.zeros((HD, GRID_W * GRID_W), np.float32)
    expand[dc.reshape(-1), np.arange(GRID_W * GRID_W)] = 1.0
    cs = np.clip(q - WIN_C // 2, 0, GRID_W - WIN_C)
    keep = ((kc >= cs) & (kc < cs + WIN_C)).reshape(1, -1).astype(np.float32)
    sel = np.zeros((64, 4 * N_OFF * WIN_R), np.float32)
    for h in range(4):
        for off in range(N_OFF):
            for j in range(WIN_R):
                sel[h * (2 * WIN_R - 1) + off + j, (h * N_OFF + off) * WIN_R + j] = 1.0
    return jnp.asarray(expand), jnp.asarray(keep), jnp.asarray(sel)


def _bias_expand(rpb_pad, expand, keep, sel):
    def body(r_ref, e_ref, k_ref, s_ref, o_ref):
        t = lax.dot_general(r_ref[...], e_ref[...], NN, precision=lax.Precision.HIGHEST,
                            preferred_element_type=F32)
        rows = lax.dot_general(s_ref[...], t, TN, precision=lax.Precision.HIGHEST,
                               preferred_element_type=F32)
        o_ref[...] = jnp.where(k_ref[...] > 0.5, rows, NEG)

    return pl.pallas_call(
        body, name="bias_expand",
        out_shape=jax.ShapeDtypeStruct((4 * N_OFF * WIN_R, GRID_W * GRID_W), F32),
        compiler_params=pltpu.CompilerParams(vmem_limit_bytes=VMEM_LIMIT),
    )(rpb_pad, expand, keep, sel)


def _bias_reduce(dbias_tab):
    lane0 = GRID_W - WIN_C
    flip = np.zeros((GRID_W, GRID_W), np.float32)
    flip[np.arange(GRID_W), GRID_W - 1 - np.arange(GRID_W)] = 1.0
    place = np.zeros((WIN_R, 64, 4 * N_OFF), np.float32)
    for j in range(WIN_R):
        for h in range(4):
            for off in range(N_OFF):
                place[j, h * (2 * WIN_R - 1) + off + j, h * N_OFF + off] = 1.0

    def exact(x, y):
        return lax.dot_general(x, y, NN, precision=lax.Precision.HIGHEST, preferred_element_type=F32)

    def body(x_ref, flip_ref, place_ref, o_ref, z_ref):
        for h in range(4):
            for off in range(N_OFF):
                lined_up = pltpu.roll(exact(flip_ref[...], x_ref[h, off]), 0, axis=1, stride=1, stride_axis=0)
                z_ref[h * N_OFF + off:h * N_OFF + off + 1, :] = jnp.sum(lined_up, axis=0, keepdims=True)
        acc = jnp.zeros((64, HD), F32)
        for j in range(WIN_R):
            at_zero = pltpu.roll(z_ref[...], (KEYS_B - (j * GRID_W + lane0)) % KEYS_B, axis=1)[:, :HD]
            acc = acc + exact(place_ref[j], at_zero)
        lane = lax.broadcasted_iota(jnp.int32, (64, HD), 1)
        o_ref[...] = jnp.where(lane < 2 * WIN_C - 1, acc, 0.0)

    return pl.pallas_call(
        body, name="bias_reduce", out_shape=jax.ShapeDtypeStruct((64, HD), F32),
        scratch_shapes=[pltpu.VMEM((4 * N_OFF, KEYS_B), F32)],
        compiler_params=pltpu.CompilerParams(vmem_limit_bytes=VMEM_LIMIT),
    )(dbias_tab, jnp.asarray(flip), jnp.asarray(place))


def _rows_to_tab(rows):
    t = rows.reshape(4, N_OFF, WIN_R, GRID_W, GRID_W)
    return t.transpose(0, 1, 3, 2, 4).reshape(4, N_OFF, GRID_W, KEYS_B)


def _row_window(r):
    r0 = jnp.clip(r - WIN_R // 2, 0, ROWS - WIN_R)
    off = r0 + (WIN_R - 1) - r
    return pl.multiple_of(r * GRID_W, GRID_W), pl.multiple_of(r0 * GRID_W, GRID_W), off


def _attn_b_fwd(qn, kn, vb, bias_tab):
    def body(q_ref, k_ref, v_ref, b_ref, o_ref, lse_ref):
        def row(r, carry):
            qs, ks, off = _row_window(r)
            q = q_ref[pl.ds(qs, GRID_W), :]
            s = lax.dot_general(q, k_ref[pl.ds(ks, KEYS_B), :], NT, preferred_element_type=F32) * SCALE
            s = s + b_ref[off]
            m = jnp.max(s, axis=-1, keepdims=True)
            p = jnp.exp(s - m)
            l = jnp.sum(p, axis=-1, keepdims=True)
            o = lax.dot_general(p.astype(BF16), v_ref[pl.ds(ks, KEYS_B), :], NN, preferred_element_type=F32)
            o_ref[pl.ds(qs, GRID_W), :] = (o / l).astype(BF16)
            lse_ref[pl.ds(qs, GRID_W), :] = m + jnp.log(l)
            return carry

        lax.fori_loop(0, ROWS, row, 0, unroll=8)

    full = pl.BlockSpec((S, HD), lambda h: (0, h))
    return pl.pallas_call(
        body, name="attn_b_fwd", grid=(4,),
        in_specs=[full, full, full, pl.BlockSpec((None, N_OFF, GRID_W, KEYS_B), lambda h: (h, 0, 0, 0))],
        out_specs=[pl.BlockSpec((S, HD), lambda h: (0, h)), pl.BlockSpec((None, S, 1), lambda h: (h, 0, 0))],
        out_shape=[jax.ShapeDtypeStruct((S, D_BR), BF16), jax.ShapeDtypeStruct((4, S, 1), F32)],
        compiler_params=_params(("parallel",)),
    )(qn, kn, vb, bias_tab)


def _attn_b_bwd(qn, kn, vb, bias_tab, ob, dob, lse):
    def body(q_ref, k_ref, v_ref, b_ref, o_ref, do_ref, lse_ref, dq_ref, dk_out, dv_out, db_ref, dk_ref, dv_ref):
        dk_ref[...] = jnp.zeros((S, HD), F32)
        dv_ref[...] = jnp.zeros((S, HD), F32)
        db_ref[...] = jnp.zeros((N_OFF, GRID_W, KEYS_B), F32)

        def row(r, carry):
            qs, ks, off = _row_window(r)
            rows = pl.ds(qs, GRID_W)
            keys = pl.ds(ks, KEYS_B)
            q = q_ref[rows, :]
            kw = k_ref[keys, :]
            s = lax.dot_general(q, kw, NT, preferred_element_type=F32) * SCALE + b_ref[off]
            p = jnp.exp(s - lse_ref[rows, :])
            do = do_ref[rows, :]
            dobf = do.astype(BF16)
            dsum = jnp.sum(do * o_ref[rows, :].astype(F32), axis=-1, keepdims=True)
            dp = lax.dot_general(dobf, v_ref[keys, :], NT, preferred_element_type=F32)
            ds = p * (dp - dsum)
            db_ref[off] += ds
            dsb = (ds * SCALE).astype(BF16)
            dq_ref[rows, :] = lax.dot_general(dsb, kw, NN, preferred_element_type=F32).astype(BF16)
            dk_ref[keys, :] += lax.dot_general(dsb, q, TN, preferred_element_type=F32)
            dv_ref[keys, :] += lax.dot_general(p.astype(BF16), dobf, TN, preferred_element_type=F32)
            return carry

        lax.fori_loop(0, ROWS, row, 0, unroll=8)
        dk_out[...] = dk_ref[...].astype(BF16)
        dv_out[...] = dv_ref[...].astype(BF16)

    full = pl.BlockSpec((S, HD), lambda h: (0, h))
    slot = pl.BlockSpec((S, HD), lambda h: (0, h))
    tab = pl.BlockSpec((None, N_OFF, GRID_W, KEYS_B), lambda h: (h, 0, 0, 0))
    shape = jax.ShapeDtypeStruct((S, D_BR), BF16)
    return pl.pallas_call(
        body, name="attn_b_bwd", grid=(4,),
        in_specs=[full, full, full, tab, slot, slot, pl.BlockSpec((None, S, 1), lambda h: (h, 0, 0))],
        out_specs=[slot, slot, slot, tab],
        out_shape=[shape, shape, shape, jax.ShapeDtypeStruct((4, N_OFF, GRID_W, KEYS_B), F32)],
        scratch_shapes=[pltpu.VMEM((S, HD), F32)] * 2,
        compiler_params=_params(("arbitrary",)),
    )(qn, kn, vb, bias_tab, ob, dob, lse)


def _epi_relu_sq(acc, ex, outs):
    u = jnp.maximum(acc, 0.0)
    outs[0][...] = u.astype(BF16)
    outs[1][...] = (u * u).astype(BF16)


def _epi_relu_sq_bwd(acc, ex, outs):
    outs[0][...] = (acc * (2.0 * ex[0][...].astype(F32))).astype(BF16)


def _epi_loss_head(acc, ex, outs):
    e = acc + ex[0][...] - ex[1][...]
    dy = e * (1.0 / D)
    outs[0][...] = dy
    outs[1][...] = dy.astype(BF16)
    part = (0.5 / D) * jnp.sum(jnp.sum(e * e, axis=-1, keepdims=True), axis=0, keepdims=True)
    first = (pl.program_id(0) == 0) & (pl.program_id(1) == 0)

    @pl.when(first)
    def _():
        outs[2][...] = part

    @pl.when(jnp.logical_not(first))
    def _():
        outs[2][...] += part


def _local_step(x, target, norm_mix, b_gate, gains, rpb_pad, norm_ffn,
                w_in, w_pa, w_pb, w_out, w_up, w_down, weight_grads, riders=lambda name: None):
    def ridden(name, *args, **kwargs):
        ride = riders(name)
        if ride is None:
            return _mm_nt(*args, name=name, **kwargs)
        out, rode = _mm_nt(*args, name=name, rider=ride[0], **kwargs)
        ride[1](rode)
        return out

    cos2, sin2 = _rope_tables()
    expand, keep, sel = _bias_constants()
    w_out3 = w_out[None]

    xn, rstd1 = _rms_fwd(x, norm_mix, name="rms_mix")
    per_dev = D_IN // N_DEV
    proj = _mm_nn(xn, w_in[0], tm=S, tn=W_IN_SPLIT, name="proj_0", stride=per_dev, width=D_IN)
    proj = _mm_nn(xn, w_in[1], tm=S, tn=per_dev - W_IN_SPLIT, name="proj_1", stride=per_dev, width=D_IN,
                  col0=W_IN_SPLIT, into=(proj,))
    qkv_a, qkv_b = _qk_prep(proj, gains, cos2, sin2)
    fwd_a = [_attn_a_fwd(*qkv_a[gi], gi) for gi in range(3)]
    oa, lse_a = _combine_a([o for o, _ in fwd_a], [l for _, l in fwd_a])
    bias_tab = _rows_to_tab(_bias_expand(rpb_pad, expand, keep, sel))
    ob, lse_b = _attn_b_fwd(*qkv_b, bias_tab)
    mixed, ya, yb = _mix_fwd(oa, ob, w_pa, w_pb, proj, b_gate)
    h1 = _mm_nn(mixed, w_out3, tm=1024, tn=1024, name="out_proj", epi=_epi_residual, extra=(x,))
    hn, rstd2 = _rms_fwd(h1, norm_ffn, name="rms_ffn")
    u, usq = _mm_nn(hn, w_up, tm=S, tn=512, name="ffn_up", epi=_epi_relu_sq,
                    out_dtypes=(BF16, BF16))
    dy, dyb, loss = _mm_nn(usq, w_down, tm=512, tn=512, name="ffn_down", epi=_epi_loss_head,
                           extra=(h1, target), out_dtypes=(F32, BF16), total=True)

    sent = weight_grads("w_down", {5: (usq, dyb)})
    du = _mm_nt(dyb, w_down, tm=1024, tn=1024, name="ffn_down_bwd", out_dtype=BF16,
                epi=_epi_relu_sq_bwd, extra=(u,), after=sent)
    sent = weight_grads("w_up", {4: (hn, du)})
    dhn = ridden("ffn_up_bwd", du, w_up, tm=512, tn=512, after=sent)
    dh1, dh1b, g_norm_ffn = _rms_bwd(dhn, h1, rstd2, norm_ffn, dy, name="rms_ffn_bwd", bf16_copy=True)

    dya, dyb2, dproj, g_b = _mix_bwd(dh1b, w_out, proj, b_gate, ya, yb)
    sent = weight_grads("w_mix", {3: (mixed, dh1b)})
    dob = _mm_nt(dyb2, w_pb, tm=1024, tn=D_BR, name="proj_b_bwd", after=sent)
    prep = _proj_a_bwd(dya, w_pa, oa, lse_a)
    grads_a = [_attn_a_bwd(*qkv_a[gi], *prep[gi], gi) for gi in range(3)]
    dqb, dkb, dvb, dbias = _attn_b_bwd(*qkv_b, bias_tab, ob, dob, lse_b)
    g_rpb = _bias_reduce(dbias)
    dproj, g_gains = _qk_prep_bwd(dproj, proj, gains, cos2, sin2, grads_a, (dqb, dkb, dvb))
    sent = weight_grads("w_in", {0: (xn, dproj)})
    sent = sent + weight_grads("w_proj", {1: (oa, dya), 2: (ob, dyb2)}, after=sent)
    dxn = ridden("proj_bwd", dproj, w_in[0], more_b=(w_in[1],), interleaved=True, tm=256, tn=512, after=sent)
    grad_x, g_norm_mix = _rms_bwd(dxn, x, rstd1, norm_mix, dh1, name="rms_mix_bwd", bf16_copy=False)

    small = (g_norm_mix, g_b, g_gains, g_rpb, g_norm_ffn)
    return loss, grad_x, small


def _cast_bf16(w, *, part=0, parts=1, window=None, after=(), tr=256):
    rows, cols = w.shape[0], w.shape[1] // parts
    tr = min(tr, rows)
    src = pl.BlockSpec((tr, cols), lambda i: (i, part))
    if window is not None:
        part, cols = window
        src = pl.BlockSpec((pl.Element(tr), pl.Element(cols)), lambda i: (i * tr, part))

    def body(w_ref, *rest):
        rest[-1][...] = w_ref[...].astype(BF16)

    return pl.pallas_call(
        body, name=f"cast_{rows}x{cols}_{part}", grid=(rows // tr,),
        in_specs=[src] + [pl.BlockSpec(memory_space=pl.ANY)] * len(after),
        out_specs=pl.BlockSpec((tr, cols), lambda i: (i, 0)),
        out_shape=jax.ShapeDtypeStruct((rows, cols), BF16), compiler_params=_params(("parallel",)),
    )(w, *after)


def _me_and_peers():
    x, y, c = lax.axis_index("x"), lax.axis_index("y"), lax.axis_index("c")
    me = 4 * x + 2 * y + c
    peers = []
    for k in range(1, N_DEV):
        px = 1 - x if k & 4 else x
        py = 1 - y if k & 2 else y
        pc = 1 - c if k & 1 else c
        peers.append(((px, py, pc), 4 * px + 2 * py + pc))
    return me, peers


def _gather_on_sequencer(shards, name):
    n = len(shards)
    hbm = pltpu.MemorySpace.HBM
    ins = [jax.new_ref(s, memory_space=hbm) for s in shards]
    outs = [jax.empty_ref(jax.ShapeDtypeStruct((N_DEV,) + s.shape, s.dtype), memory_space=hbm) for s in shards]
    n_sem = 8

    @_sequencer(name, ((n, n_sem), (n, n_sem), (n,)), 0)
    def launch(send, recv, lsem):
        x, y, c = lax.axis_index("x"), lax.axis_index("y"), lax.axis_index("c")
        me, sibling = (x, y, c), (x, y, 1 - c)
        x_chip, y_chip, diagonal = (1 - x, y, c), (x, 1 - y, c), (1 - x, 1 - y, c)
        _handshake([sibling, x_chip, y_chip])

        def copy(w, k, block, to, src=None, half=None):
            px, py, pc = block
            dst = outs[w].at[4 * px + 2 * py + pc]
            if half is not None:
                rows = shards[w].shape[0] // 2
                dst = dst.at[pl.ds(half * rows, rows)]
            return pltpu.make_async_remote_copy(dst if src is None else src, dst, send.at[w, k], recv.at[w, k],
                                                device_id=to, device_id_type=MESH)

        local = [pltpu.make_async_copy(ins[w], outs[w].at[4 * x + 2 * y + c], lsem.at[w]) for w in range(n)]
        for cp in local:
            cp.start()
        sent = []
        for w in range(n):
            sent += [copy(w, 1, me, x_chip, src=ins[w]), copy(w, 2, me, y_chip, src=ins[w]),
                     copy(w, 0, me, sibling, src=ins[w])]
        for cp in sent:
            cp.start()
        for w in range(n):
            copy(w, 1, x_chip, me).wait_recv()
            sent += [copy(w, 3, x_chip, y_chip, half=0), copy(w, 5, x_chip, sibling)]
            sent[-2].start()
            sent[-1].start()
            copy(w, 2, y_chip, me).wait_recv()
            sent += [copy(w, 4, y_chip, x_chip, half=1), copy(w, 6, y_chip, sibling)]
            sent[-2].start()
            sent[-1].start()
        for w in range(n):
            copy(w, 3, diagonal, me, half=0).wait_recv()
            copy(w, 4, diagonal, me, half=1).wait_recv()
            sent.append(copy(w, 7, diagonal, sibling))
            sent[-1].start()
        for w in range(n):
            copy(w, 0, sibling, me).wait_recv()
            for k, chip in ((5, x_chip), (6, y_chip), (7, diagonal)):
                px, py, _ = chip
                copy(w, k, (px, py, 1 - c), me).wait_recv()
        for cp in sent:
            cp.wait_send()
        for cp in local:
            cp.wait()

    launch()
    return [o[...] for o in outs]


N_CHIP = 4


def _sequencer(name, n_sems, collective_id):
    return functools.partial(
        pl.kernel, mesh=plsc.ScalarSubcoreMesh(axis_name="seq", num_cores=1), name=name,
        scratch_types=tuple(pltpu.SemaphoreType.DMA(s) for s in n_sems),
        compiler_params=pltpu.CompilerParams(collective_id=collective_id))


def _handshake(peers):
    barrier = pltpu.get_barrier_semaphore()
    for peer in peers:
        pl.semaphore_signal(barrier, inc=1, device_id=peer, device_id_type=MESH)
    pl.semaphore_wait(barrier, len(peers))


def _chip_exchange_on_sequencer(parts, name):
    n = len(parts)
    hbm = pltpu.MemorySpace.HBM
    ins = [jax.new_ref(p, memory_space=hbm) for p in parts]
    outs = [jax.empty_ref(jax.ShapeDtypeStruct(p.shape, p.dtype), memory_space=hbm) for p in parts]

    @_sequencer(name, ((n, 3), (n, 3), (n,)), 2)
    def launch(send, recv, lsem):
        x, y, c = lax.axis_index("x"), lax.axis_index("y"), lax.axis_index("c")
        mine = 2 * x + y
        chips = [(1 - x, y), (x, 1 - y), (1 - x, 1 - y)]
        _handshake([(*chip, c) for chip in chips])
        local = [pltpu.make_async_copy(ins[w].at[mine], outs[w].at[mine], lsem.at[w]) for w in range(n)]
        for cp in local:
            cp.start()
        sends = []
        for w in range(n):
            for j, (px, py) in enumerate(chips):
                cp = pltpu.make_async_remote_copy(ins[w].at[2 * px + py], outs[w].at[mine],
                                                  send.at[w, j], recv.at[w, j],
                                                  device_id=(px, py, c), device_id_type=MESH)
                cp.start()
                sends.append(cp)
        for w in range(n):
            for j, (px, py) in enumerate(chips):
                pltpu.make_async_remote_copy(ins[w].at[mine], outs[w].at[2 * px + py],
                                             send.at[w, j], recv.at[w, j],
                                             device_id=(px, py, c), device_id_type=MESH).wait_recv()
        for cp in sends:
            cp.wait_send()
        for cp in local:
            cp.wait()

    launch()
    return [o[...] for o in outs]


GRAD_TILES = (dict(blocks_on="cols", tm=512, tn=1280), dict(blocks_on="cols", tm=512, tn=256),
              dict(blocks_on="cols", tm=512, tn=256), dict(blocks_on="rows", tm=256, tn=2048),
              dict(blocks_on="cols", tm=1024, tn=1024), dict(blocks_on="rows", tm=1024, tn=1024))


def _mm_tn_pair(a, b, *, blocks_on, tm, tn, name, after=()):
    t_len, m = a.shape
    n = b.shape[1]
    if blocks_on == "rows":
        rows, cols, inner = m // N_DEV, n, n // tn
        assert tm == rows
        a_spec = pl.BlockSpec((t_len, tm), lambda p, t, blk: (0, blk[p]))
        b_spec = pl.BlockSpec((t_len, tn), lambda p, t, blk: (0, t))
        out_spec = pl.BlockSpec((None, tm, tn), lambda p, t, blk: (
            jnp.maximum(p - N_CHIP, 0), 0, jnp.where(p < N_CHIP, 0, t)))
    else:
        rows, cols, inner = m, n // N_DEV, m // tm
        assert tn == cols
        a_spec = pl.BlockSpec((t_len, tm), lambda p, t, blk: (0, t))
        b_spec = pl.BlockSpec((t_len, tn), lambda p, t, blk: (0, blk[p]))
        out_spec = pl.BlockSpec((None, tm, tn), lambda p, t, blk: (
            jnp.maximum(p - N_CHIP, 0), jnp.where(p < N_CHIP, 0, t), 0))

    def body(blk_ref, a_ref, b_ref, *rest):
        del blk_ref
        o_ref, land, stage, send_sem, recv_sem = rest[len(after):]
        p, t = pl.program_id(0), pl.program_id(1)
        step = p * inner + t
        x, y, c = lax.axis_index("x"), lax.axis_index("y"), lax.axis_index("c")
        tile = _dot(a_ref[...], b_ref[...], TN)

        def to_sibling(slot, chip, piece):
            return pltpu.make_async_remote_copy(stage.at[slot], land.at[chip, piece], send_sem.at[slot],
                                                recv_sem.at[chip, piece],
                                                device_id=(x, y, 1 - c), device_id_type=MESH)

        @pl.when(p < N_CHIP)
        def _():
            slot = step % 2

            @pl.when(step >= 2)
            def _():
                to_sibling(slot, 0, 0).wait_send()

            stage[slot] = tile.astype(BF16)
            to_sibling(slot, p, t).start()

        @pl.when(step == N_CHIP * inner)
        def _():
            for slot in range(min(2, N_CHIP * inner)):
                to_sibling(slot, 0, 0).wait_send()

        @pl.when(p >= N_CHIP)
        def _():
            chip = p - N_CHIP
            to_sibling(0, chip, t).wait_recv()
            o_ref[...] = (tile + land[chip, t].astype(F32)).astype(BF16)

    c = lax.axis_index("c")
    order = jnp.stack([2 * ch + 1 - c for ch in range(N_CHIP)] + [2 * ch + c for ch in range(N_CHIP)])
    return pl.pallas_call(
        body, name=name,
        grid_spec=pltpu.PrefetchScalarGridSpec(
            num_scalar_prefetch=1, grid=(N_DEV, inner), out_specs=out_spec,
            in_specs=[a_spec, b_spec] + [pl.BlockSpec(memory_space=pl.ANY)] * len(after),
            scratch_shapes=[pltpu.VMEM((N_CHIP, inner, tm, tn), BF16), pltpu.VMEM((2, tm, tn), BF16),
                            pltpu.SemaphoreType.DMA((2,)), pltpu.SemaphoreType.DMA((N_CHIP, inner))]),
        out_shape=jax.ShapeDtypeStruct((N_CHIP, rows, cols), BF16),
        compiler_params=_params(("arbitrary", "arbitrary")),
    )(order.astype(jnp.int32), a, b, *after)


def _adamw_math(g, w, m, v):
    m2 = B1 * m + (1.0 - B1) * g
    v2 = B2 * v + (1.0 - B2) * (g * g)
    delta = -LR * ((m2 / BC1) / (jnp.sqrt(v2 / BC2) + AEPS) + WD * w)
    return delta, m2, v2


def _adamw_block(ins, outs):
    p_ref, w_ref, m_ref, v_ref = ins
    g = p_ref[0].astype(F32)
    for b in range(1, N_CHIP):
        g = g + p_ref[b].astype(F32)
    delta, m2, v2 = _adamw_math(g, w_ref[...], m_ref[...], v_ref[...])
    for ref, val in zip(outs, (g, delta, m2, v2)):
        ref[...] = val


class _Rider(NamedTuple):
    inputs: tuple
    in_specs: list
    out_shape: list
    out_specs: list
    body: Callable


def _adamw_rider(parts, w, m, v):
    rows, cols = w.shape

    def rider(steps, step_of):
        rr = rows // steps
        blk = pl.BlockSpec((rr, cols), lambda *ids: (step_of(*ids[:2]), 0))
        chips = pl.BlockSpec((N_CHIP, rr, cols), lambda *ids: (0, step_of(*ids[:2]), 0))
        shape = jax.ShapeDtypeStruct((rows, cols), F32)
        return _Rider((parts, w, m, v), [chips, blk, blk, blk], [shape] * 4, [blk] * 4, _adamw_block)

    return rider


def _adamw(parts, w, m, v, *, name, after=(), tr=256):
    rows, cols = w.shape

    def body(*refs):
        _adamw_block(refs[:4], refs[4 + len(after):])

    spec = pl.BlockSpec((tr, cols), lambda i: (i, 0))
    shape = jax.ShapeDtypeStruct((rows, cols), F32)
    return pl.pallas_call(
        body, name=name, grid=(rows // tr,),
        in_specs=[pl.BlockSpec((N_CHIP, tr, cols), lambda i: (0, i, 0)), spec, spec, spec]
        + [pl.BlockSpec(memory_space=pl.ANY)] * len(after),
        out_specs=[spec] * 4, out_shape=[shape] * 4,
        compiler_params=_params(("parallel",)),
    )(parts, w, m, v, *after)


def _small_exchange(part, after=()):
    rows = part.shape[0]

    def body(p_ref, *rest):
        g_ref, buf, send, recv = rest[len(after):]
        me, peers = _me_and_peers()
        buf[me] = p_ref[...]
        sends = []
        for k, (dev, _) in enumerate(peers):
            cp = pltpu.make_async_remote_copy(p_ref, buf.at[me], send.at[k], recv.at[k],
                                              device_id=dev, device_id_type=MESH)
            cp.start()
            sends.append(cp)
        for k, (dev, idx) in enumerate(peers):
            pltpu.make_async_remote_copy(p_ref, buf.at[idx], send.at[k], recv.at[k],
                                         device_id=dev, device_id_type=MESH).wait_recv()
        for cp in sends:
            cp.wait_send()
        g = buf[0]
        for b in range(1, N_DEV):
            g = g + buf[b]
        g_ref[...] = g

    vm = pl.BlockSpec(memory_space=pltpu.VMEM)
    return pl.pallas_call(
        body, name="small_params_exchange",
        in_specs=[vm] + [pl.BlockSpec(memory_space=pl.ANY)] * len(after),
        out_specs=vm, out_shape=jax.ShapeDtypeStruct((rows, HD), F32),
        scratch_shapes=[pltpu.VMEM((N_DEV, rows, HD), F32),
                        pltpu.SemaphoreType.DMA((N_DEV - 1,)), pltpu.SemaphoreType.DMA((N_DEV - 1,))],
    )(part, *after)


def _small_adamw(g, w, m, v):
    def body(g_ref, w_ref, m_ref, v_ref, *outs):
        g = g_ref[...]
        delta, m2, v2 = _adamw_math(g, w_ref[...], m_ref[...], v_ref[...])
        for k, val in enumerate((g, delta, m2, v2)):
            norm_mix, b_gate, qa, ka, qb, kb, rpb, norm_ffn = outs[8 * k:8 * k + 8]
            for dst, row0, n_rows in ((norm_mix, 0, 16), (b_gate, 16, 32), (norm_ffn, 120, 16)):
                for r in range(n_rows):
                    dst[:, r * HD:(r + 1) * HD] = val[row0 + r:row0 + r + 1, :]
            for i, dst in enumerate((qa, ka, qb, kb)):
                dst[...] = val[48 + i:49 + i, :]
            rpb[...] = val[56:120, :]
        outs[32][...] = g[LOSS_ROW:LOSS_ROW + 1, 0:1]

    vm = pl.BlockSpec(memory_space=pltpu.VMEM)
    kinds = [jax.ShapeDtypeStruct(sh, F32) for sh in
             ((1, D), (1, 2 * D), (1, HD), (1, HD), (1, HD), (1, HD), (64, HD), (1, D))]
    outs = pl.pallas_call(
        body, name="small_params_adamw", in_specs=[vm] * 4, out_specs=[vm] * 33,
        out_shape=kinds * 4 + [jax.ShapeDtypeStruct((1, 1), F32)],
    )(g, w, m, v)
    return [outs[8 * k:8 * k + 8] for k in range(4)], outs[32]


def _pack_small(norm_mix, b_gate, qa, ka, qb, kb, rpb, norm_ffn):
    gains = jnp.concatenate([qa, ka, qb, kb, jnp.zeros((4, HD), F32)], axis=0)
    rpb_pad = jnp.pad(rpb.reshape(4 * (2 * WIN_R - 1), 2 * WIN_C - 1), ((0, 4), (0, HD - (2 * WIN_C - 1))))
    return jnp.concatenate([norm_mix.reshape(16, HD), b_gate.reshape(32, HD), gains, rpb_pad,
                            norm_ffn.reshape(16, HD), jnp.zeros((8, HD), F32)], axis=0)


LOSS_ROW = 136


def _rpb_from_rows(rows):
    return rows[:60, :2 * WIN_C - 1].reshape(1, 4, 2 * WIN_R - 1, 2 * WIN_C - 1)


def kernel(x, norm_mix, w_in, b_gate, q_norm_a, k_norm_a, q_norm_b, k_norm_b, rpb_b, w_proj_a, w_proj_b, w_out, norm_ffn, w_up, w_down, loss_target, m_norm_mix, m_w_in, m_b_gate, m_q_norm_a, m_k_norm_a, m_q_norm_b, m_k_norm_b, m_rpb_b, m_w_proj_a, m_w_proj_b, m_w_out, m_norm_ffn, m_w_up, m_w_down, v_norm_mix, v_w_in, v_b_gate, v_q_norm_a, v_k_norm_a, v_q_norm_b, v_k_norm_b, v_rpb_b, v_w_proj_a, v_w_proj_b, v_w_out, v_norm_ffn, v_w_up, v_w_down):
    big_w = (w_in[0], w_proj_a[0], w_proj_b[0], w_out[0], w_up[0], w_down[0])
    big_m = (m_w_in[0], m_w_proj_a[0], m_w_proj_b[0], m_w_out[0], m_w_up[0], m_w_down[0])
    big_v = (v_w_in[0], v_w_proj_a[0], v_w_proj_b[0], v_w_out[0], v_w_up[0], v_w_down[0])
    names = ("w_in", "w_proj_a", "w_proj_b", "w_out", "w_up", "w_down")

    g_in = [_gather_on_sequencer([_cast_bf16(big_w[0], window=win)], f"gather_w_in_{k}")[0]
            for k, win in enumerate(((0, W_IN_SPLIT), (W_IN_SPLIT, D_IN // N_DEV - W_IN_SPLIT)))]
    shards = [None] + [_cast_bf16(w) for w in big_w[1:5]]
    g_pa, g_pb, g_out, g_up = _gather_on_sequencer(shards[1:5], "gather_w_mix_up")
    small_w = _pack_small(norm_mix, b_gate, q_norm_a, k_norm_a, q_norm_b, k_norm_b, rpb_b, norm_ffn)
    small_m = _pack_small(m_norm_mix, m_b_gate, m_q_norm_a, m_k_norm_a, m_q_norm_b, m_k_norm_b, m_rpb_b, m_norm_ffn)
    small_v = _pack_small(v_norm_mix, v_b_gate, v_q_norm_a, v_k_norm_a, v_q_norm_b, v_k_norm_b, v_rpb_b, v_norm_ffn)
    g_down = _gather_on_sequencer([_cast_bf16(big_w[5], after=(small_w, small_m, small_v))],
                                  "gather_w_down")[0].reshape(1, D_FF, D)

    upd = [None] * 6
    in_flight = {}

    def weight_grads(tag, operands, after=()):
        sums = {i: _mm_tn_pair(a, b, name=f"grad_{names[i]}", after=after, **GRAD_TILES[i])
                for i, (a, b) in operands.items()}
        new = list(sums.values())
        in_flight.update(zip(sums, _chip_exchange_on_sequencer(new, f"chip_exchange_{tag}")))
        return new

    def riders(name):
        i = {"proj_bwd": 5}.get(name)
        if i is None:
            return None
        return (_adamw_rider(in_flight.pop(i), big_w[i], big_m[i], big_v[i]),
                functools.partial(upd.__setitem__, i))

    loss, grad_x, small_g = _local_step(
        x[0], loss_target[0], norm_mix, b_gate, small_w[48:56], small_w[56:120], norm_ffn,
        g_in, g_pa, g_pb, g_out.reshape(D, D), g_up, g_down, weight_grads, riders)

    g_norm_mix, g_b, g_gains, g_rpb, g_norm_ffn = small_g
    small_part = jnp.concatenate([g_norm_mix.reshape(16, HD), g_b.reshape(32, HD),
                                  g_gains, g_rpb, g_norm_ffn.reshape(16, HD),
                                  jnp.pad(loss, ((0, 7), (0, HD - 1)))], axis=0)
    last = grad_x
    for i, r in in_flight.items():
        upd[i] = _adamw(r, big_w[i], big_m[i], big_v[i], name=f"adamw_{names[i]}", after=[last])
        last = upd[i][0]
        if i == 0:
            small_sum = _small_exchange(small_part, after=[last])
            small, total = _small_adamw(small_sum, small_w, small_m, small_v)
            last = total
    s_g, s_d, s_m, s_v = ((*k[:6], _rpb_from_rows(k[6]), k[7]) for k in small)
    b_g, b_d, b_m, b_v = ([u[j][None] for u in upd] for j in range(4))

    def order(small, big):
        nm, bg, qa, ka, qb, kb, rpb, nf = small
        w_in_, pa_, pb_, out_, up_, down_ = big
        return (nm, w_in_, bg, qa, ka, qb, kb, rpb, pa_, pb_, out_, nf, up_, down_)

    return (total[0, 0], grad_x[None], *order(s_g, b_g), *order(s_d, b_d), *order(s_m, b_m), *order(s_v, b_v))
```
